```python
import math
import jax, jax.numpy as jnp
from jax import lax
import numpy as np

D_MODEL = 1024
BATCH = 8
SEQ = 4096
DEPTH = 2

HEAD_DIM = 64
N_HEADS_A = D_MODEL // (2 * HEAD_DIM)
N_HEADS_B = D_MODEL // (2 * HEAD_DIM)
N_KV_B = max(1, N_HEADS_B // 4)
GROUP_B = N_HEADS_B // N_KV_B
MIX_WIDTH = (N_HEADS_A + N_HEADS_B) * HEAD_DIM
DILATED_CONFIGS = ((128, 1), (512, 4), (2048, 16))
SWA_RADIUS = 128
N_BUCKETS = 32
MAX_DISTANCE = 1024
D_FF = ((8 * D_MODEL // 3 + 127) // 128) * 128
PLE_DIM = 256
EPS = 1e-6
QBLOCK = 128
NEG = -1e30

QKV_WIDTHS = (N_HEADS_A * HEAD_DIM, N_HEADS_A * HEAD_DIM, N_HEADS_A * HEAD_DIM,
              N_HEADS_B * HEAD_DIM, N_KV_B * HEAD_DIM, N_KV_B * HEAD_DIM)
QKV_WIDTH = sum(QKV_WIDTHS)

kernel_name = "hybrid_dilated_swa_macaron_encoder"


def rms_norm(x, g):
    xf = x.astype(jnp.float32)
    y = xf * lax.rsqrt(jnp.mean(xf * xf, axis=-1, keepdims=True) + EPS) * g.astype(jnp.float32)
    return y.astype(x.dtype)


def swiglu(h, w_in, w_out):
    gate, up = jnp.split(h @ w_in, 2, axis=-1)
    return (jax.nn.silu(gate) * up) @ w_out


def t5_bucket(rel):
    half = N_BUCKETS // 2
    max_exact = half // 2
    ret = jnp.where(rel > 0, half, 0)
    n = jnp.abs(rel)
    nf = jnp.maximum(n, 1).astype(jnp.float32)
    large = max_exact + (jnp.log(nf / max_exact) / math.log(MAX_DISTANCE / max_exact)
                         * (half - max_exact)).astype(jnp.int32)
    large = jnp.minimum(large, half - 1)
    return ret + jnp.where(n < max_exact, n, large)


def banded_attention(q, k, v, radius, dilation, bias_heads, sink=None):
    N, Hk, G, L, E = q.shape
    bq = math.gcd(L, QBLOCK)
    nb = L // bq
    W = bq + 2 * radius
    pad = ((0, 0), (0, 0), (radius, radius), (0, 0))
    idx = (jnp.arange(nb) * bq)[:, None] + jnp.arange(W)[None, :]
    kb = jnp.pad(k, pad)[:, :, idx]
    vb = jnp.pad(v, pad)[:, :, idx]
    qb = q.reshape(N, Hk, G, nb, bq, E)
    logits = jnp.einsum('nhgbqe,nhbke->nhgbqk', qb, kb,
                        preferred_element_type=jnp.float32) * (E ** -0.5)
    rel = jnp.arange(W)[None, :] - radius - jnp.arange(bq)[:, None]
    bias = bias_heads[t5_bucket(rel * dilation)].astype(jnp.float32)
    bias = jnp.transpose(bias.reshape(bq, W, Hk, G), (2, 3, 0, 1))[:, :, None]
    in_band = jnp.abs(rel) <= radius
    in_seq = (idx >= radius) & (idx < radius + L)
    valid = in_band[None] & in_seq[:, None, :]
    logits = jnp.where(valid, logits + bias, NEG)
    m = jnp.max(logits, axis=-1)
    if sink is not None:
        sink_b = sink.astype(jnp.float32)[None, :, :, None, None]
        m = jnp.maximum(m, sink_b)
    pr = jnp.exp(logits - m[..., None])
    denom = jnp.sum(pr, axis=-1)
    if sink is not None:
        denom = denom + jnp.exp(sink_b - m)
    out = jnp.einsum('nhgbqk,nhbke->nhgbqe', pr.astype(v.dtype), vb,
                     preferred_element_type=jnp.float32) / denom[..., None]
    lse = m + jnp.log(denom)
    return out.reshape(N, Hk, G, L, E).astype(q.dtype), lse.reshape(N, Hk, G, L)


def dilated_attention(q, k, v, bias_heads):
    B, H, S, E = q.shape
    outs, lses = [], []
    for window, d in DILATED_CONFIGS:
        L = S // d
        def split(t):
            return t.reshape(B, H, L, d, E).transpose(0, 3, 1, 2, 4).reshape(B * d, H, L, E)
        o, lse = banded_attention(split(q)[:, :, None], split(k), split(v),
                                  window // (2 * d), d, bias_heads)
        outs.append(o[:, :, 0].reshape(B, d, H, L, E).transpose(0, 2, 3, 1, 4).reshape(B, H, S, E))
        lses.append(lse[:, :, 0].reshape(B, d, H, L).transpose(0, 2, 3, 1).reshape(B, H, S))
    wts = jax.nn.softmax(jnp.stack(lses), axis=0)
    return jnp.einsum('cbhs,cbhse->bhse', wts, jnp.stack(outs).astype(jnp.float32)).astype(q.dtype)


def _fwd_setup_inputs(seed: int = 0) -> dict:
    key = jax.random.key(seed)
    ks = jax.random.split(key, 20)
    f32 = jnp.float32

    def nrm(k, shape):
        return jax.random.normal(k, shape, f32)

    def w(k, shape, fan_in):
        return nrm(k, shape) * fan_in ** -0.5

    def gain(k, shape):
        return 1.0 + 0.05 * nrm(k, shape)

    return {
        "x": nrm(ks[0], (BATCH, SEQ, D_MODEL)),
        "p": nrm(ks[1], (DEPTH, BATCH, SEQ, PLE_DIM)),
        "rel_bias": 0.5 * nrm(ks[2], (N_BUCKETS, N_HEADS_A + N_HEADS_B)),
        "norm_ffn1": gain(ks[3], (DEPTH, D_MODEL)),
        "ffn1_w_in": w(ks[4], (DEPTH, D_MODEL, 2 * D_FF), D_MODEL),
        "ffn1_w_out": w(ks[5], (DEPTH, D_FF, D_MODEL), D_FF),
        "norm_mix": gain(ks[6], (DEPTH, D_MODEL)),
        "w_qkv": w(ks[7], (DEPTH, D_MODEL, QKV_WIDTH), D_MODEL),
        "q_norm_a": gain(ks[8], (DEPTH, HEAD_DIM)),
        "k_norm_a": gain(ks[9], (DEPTH, HEAD_DIM)),
        "q_norm_b": gain(ks[10], (DEPTH, HEAD_DIM)),
        "k_norm_b": gain(ks[11], (DEPTH, HEAD_DIM)),
        "sink_b": 0.5 * nrm(ks[12], (DEPTH, N_HEADS_B)),
        "w_o": w(ks[13], (DEPTH, MIX_WIDTH, D_MODEL), MIX_WIDTH),
        "norm_ffn2": gain(ks[14], (DEPTH, D_MODEL)),
        "ffn2_w_in": w(ks[15], (DEPTH, D_MODEL, 2 * D_FF), D_MODEL),
        "ffn2_w_out": w(ks[16], (DEPTH, D_FF, D_MODEL), D_FF),
        "norm_ple": gain(ks[17], (DEPTH, D_MODEL)),
        "w_ple_gate": w(ks[18], (DEPTH, D_MODEL, D_MODEL), D_MODEL),
        "w_ple_proj": w(ks[19], (DEPTH, PLE_DIM, D_MODEL), PLE_DIM),
    }


def _fwd_reference(x, p, rel_bias, norm_ffn1, ffn1_w_in, ffn1_w_out, norm_mix, w_qkv,
              q_norm_a, k_norm_a, q_norm_b, k_norm_b, sink_b, w_o, norm_ffn2,
              ffn2_w_in, ffn2_w_out, norm_ple, w_ple_gate, w_ple_proj):
    B, S, _ = x.shape
    split_at = [int(c) for c in np.cumsum(QKV_WIDTHS)[:-1]]
    bias_a = rel_bias[:, :N_HEADS_A]
    bias_b = rel_bias[:, N_HEADS_A:]

    def heads(t, n):
        return t.reshape(B, S, n, HEAD_DIM).transpose(0, 2, 1, 3)

    for i in range(DEPTH):
        x = x + 0.5 * swiglu(rms_norm(x, norm_ffn1[i]), ffn1_w_in[i], ffn1_w_out[i])

        h = rms_norm(x, norm_mix[i])
        qa, ka, va, qb, kb, vb = jnp.split(h @ w_qkv[i], split_at, axis=-1)

        qa = rms_norm(heads(qa, N_HEADS_A), q_norm_a[i])
        ka = rms_norm(heads(ka, N_HEADS_A), k_norm_a[i])
        oa = dilated_attention(qa, ka, heads(va, N_HEADS_A), bias_a)

        qb = rms_norm(heads(qb, N_HEADS_B), q_norm_b[i]).reshape(B, N_KV_B, GROUP_B, S, HEAD_DIM)
        kb = rms_norm(heads(kb, N_KV_B), k_norm_b[i])
        ob, _ = banded_attention(qb, kb, heads(vb, N_KV_B), SWA_RADIUS, 1, bias_b,
                                 sink_b[i].reshape(N_KV_B, GROUP_B))
        ob = ob.reshape(B, N_HEADS_B, S, HEAD_DIM)

        o = jnp.concatenate([oa, ob], axis=1).transpose(0, 2, 1, 3).reshape(B, S, MIX_WIDTH)
        x = x + o @ w_o[i]

        x = x + 0.5 * swiglu(rms_norm(x, norm_ffn2[i]), ffn2_w_in[i], ffn2_w_out[i])

        gate = jax.nn.sigmoid(rms_norm(x, norm_ple[i]) @ w_ple_gate[i])
        x = x + gate * (p[i] @ w_ple_proj[i])
    return x


import jax as _jax
import jax.numpy as _jnp

TWIN_FORMAT = 'train_step'
FWD_PARAMS = ['x', 'p', 'rel_bias', 'norm_ffn1', 'ffn1_w_in', 'ffn1_w_out', 'norm_mix', 'w_qkv', 'q_norm_a', 'k_norm_a', 'q_norm_b', 'k_norm_b', 'sink_b', 'w_o', 'norm_ffn2', 'ffn2_w_in', 'ffn2_w_out', 'norm_ple', 'w_ple_gate', 'w_ple_proj']
TWIN_WEIGHTS = ['rel_bias', 'norm_ffn1', 'ffn1_w_in', 'ffn1_w_out', 'norm_mix', 'w_qkv', 'q_norm_a', 'k_norm_a', 'q_norm_b', 'k_norm_b', 'sink_b', 'w_o', 'norm_ffn2', 'ffn2_w_in', 'ffn2_w_out', 'norm_ple', 'w_ple_gate', 'w_ple_proj']
TWIN_DIFF_INPUT = 'x'
TWIN_INPUTS = ['x', 'p', 'rel_bias', 'norm_ffn1', 'ffn1_w_in', 'ffn1_w_out', 'norm_mix', 'w_qkv', 'q_norm_a', 'k_norm_a', 'q_norm_b', 'k_norm_b', 'sink_b', 'w_o', 'norm_ffn2', 'ffn2_w_in', 'ffn2_w_out', 'norm_ple', 'w_ple_gate', 'w_ple_proj', 'loss_target', 'm_rel_bias', 'm_norm_ffn1', 'm_ffn1_w_in', 'm_ffn1_w_out', 'm_norm_mix', 'm_w_qkv', 'm_q_norm_a', 'm_k_norm_a', 'm_q_norm_b', 'm_k_norm_b', 'm_sink_b', 'm_w_o', 'm_norm_ffn2', 'm_ffn2_w_in', 'm_ffn2_w_out', 'm_norm_ple', 'm_w_ple_gate', 'm_w_ple_proj', 'v_rel_bias', 'v_norm_ffn1', 'v_ffn1_w_in', 'v_ffn1_w_out', 'v_norm_mix', 'v_w_qkv', 'v_q_norm_a', 'v_k_norm_a', 'v_q_norm_b', 'v_k_norm_b', 'v_sink_b', 'v_w_o', 'v_norm_ffn2', 'v_ffn2_w_in', 'v_ffn2_w_out', 'v_norm_ple', 'v_w_ple_gate', 'v_w_ple_proj']
TWIN_OUTPUTS = ['loss', 'grad_x', 'grad_rel_bias', 'grad_norm_ffn1', 'grad_ffn1_w_in', 'grad_ffn1_w_out', 'grad_norm_mix', 'grad_w_qkv', 'grad_q_norm_a', 'grad_k_norm_a', 'grad_q_norm_b', 'grad_k_norm_b', 'grad_sink_b', 'grad_w_o', 'grad_norm_ffn2', 'grad_ffn2_w_in', 'grad_ffn2_w_out', 'grad_norm_ple', 'grad_w_ple_gate', 'grad_w_ple_proj', 'delta_rel_bias', 'delta_norm_ffn1', 'delta_ffn1_w_in', 'delta_ffn1_w_out', 'delta_norm_mix', 'delta_w_qkv', 'delta_q_norm_a', 'delta_k_norm_a', 'delta_q_norm_b', 'delta_k_norm_b', 'delta_sink_b', 'delta_w_o', 'delta_norm_ffn2', 'delta_ffn2_w_in', 'delta_ffn2_w_out', 'delta_norm_ple', 'delta_w_ple_gate', 'delta_w_ple_proj', 'new_m_rel_bias', 'new_m_norm_ffn1', 'new_m_ffn1_w_in', 'new_m_ffn1_w_out', 'new_m_norm_mix', 'new_m_w_qkv', 'new_m_q_norm_a', 'new_m_k_norm_a', 'new_m_q_norm_b', 'new_m_k_norm_b', 'new_m_sink_b', 'new_m_w_o', 'new_m_norm_ffn2', 'new_m_ffn2_w_in', 'new_m_ffn2_w_out', 'new_m_norm_ple', 'new_m_w_ple_gate', 'new_m_w_ple_proj', 'new_v_rel_bias', 'new_v_norm_ffn1', 'new_v_ffn1_w_in', 'new_v_ffn1_w_out', 'new_v_norm_mix', 'new_v_w_qkv', 'new_v_q_norm_a', 'new_v_k_norm_a', 'new_v_q_norm_b', 'new_v_k_norm_b', 'new_v_sink_b', 'new_v_w_o', 'new_v_norm_ffn2', 'new_v_ffn2_w_in', 'new_v_ffn2_w_out', 'new_v_norm_ple', 'new_v_w_ple_gate', 'new_v_w_ple_proj']
TWIN_LEAF_KINDS = {'loss': 'loss', 'grad_x': 'grad_x', 'grad_rel_bias': 'grad_w', 'grad_norm_ffn1': 'grad_w', 'grad_ffn1_w_in': 'grad_w', 'grad_ffn1_w_out': 'grad_w', 'grad_norm_mix': 'grad_w', 'grad_w_qkv': 'grad_w', 'grad_q_norm_a': 'grad_w', 'grad_k_norm_a': 'grad_w', 'grad_q_norm_b': 'grad_w', 'grad_k_norm_b': 'grad_w', 'grad_sink_b': 'grad_w', 'grad_w_o': 'grad_w', 'grad_norm_ffn2': 'grad_w', 'grad_ffn2_w_in': 'grad_w', 'grad_ffn2_w_out': 'grad_w', 'grad_norm_ple': 'grad_w', 'grad_w_ple_gate': 'grad_w', 'grad_w_ple_proj': 'grad_w', 'delta_rel_bias': 'delta_w', 'delta_norm_ffn1': 'delta_w', 'delta_ffn1_w_in': 'delta_w', 'delta_ffn1_w_out': 'delta_w', 'delta_norm_mix': 'delta_w', 'delta_w_qkv': 'delta_w', 'delta_q_norm_a': 'delta_w', 'delta_k_norm_a': 'delta_w', 'delta_q_norm_b': 'delta_w', 'delta_k_norm_b': 'delta_w', 'delta_sink_b': 'delta_w', 'delta_w_o': 'delta_w', 'delta_norm_ffn2': 'delta_w', 'delta_ffn2_w_in': 'delta_w', 'delta_ffn2_w_out': 'delta_w', 'delta_norm_ple': 'delta_w', 'delta_w_ple_gate': 'delta_w', 'delta_w_ple_proj': 'delta_w', 'new_m_rel_bias': 'new_m', 'new_m_norm_ffn1': 'new_m', 'new_m_ffn1_w_in': 'new_m', 'new_m_ffn1_w_out': 'new_m', 'new_m_norm_mix': 'new_m', 'new_m_w_qkv': 'new_m', 'new_m_q_norm_a': 'new_m', 'new_m_k_norm_a': 'new_m', 'new_m_q_norm_b': 'new_m', 'new_m_k_norm_b': 'new_m', 'new_m_sink_b': 'new_m', 'new_m_w_o': 'new_m', 'new_m_norm_ffn2': 'new_m', 'new_m_ffn2_w_in': 'new_m', 'new_m_ffn2_w_out': 'new_m', 'new_m_norm_ple': 'new_m', 'new_m_w_ple_gate': 'new_m', 'new_m_w_ple_proj': 'new_m', 'new_v_rel_bias': 'new_v', 'new_v_norm_ffn1': 'new_v', 'new_v_ffn1_w_in': 'new_v', 'new_v_ffn1_w_out': 'new_v', 'new_v_norm_mix': 'new_v', 'new_v_w_qkv': 'new_v', 'new_v_q_norm_a': 'new_v', 'new_v_k_norm_a': 'new_v', 'new_v_q_norm_b': 'new_v', 'new_v_k_norm_b': 'new_v', 'new_v_sink_b': 'new_v', 'new_v_w_o': 'new_v', 'new_v_norm_ffn2': 'new_v', 'new_v_ffn2_w_in': 'new_v', 'new_v_ffn2_w_out': 'new_v', 'new_v_norm_ple': 'new_v', 'new_v_w_ple_gate': 'new_v', 'new_v_w_ple_proj': 'new_v'}


def _forward(args):
    return _fwd_reference(*[args[k] for k in FWD_PARAMS])


def _output_shape():
    def fwd():
        inp = _fwd_setup_inputs(0)
        return _fwd_reference(*[inp[k] for k in FWD_PARAMS])
    out = _jax.eval_shape(fwd)
    return out.shape, out.dtype

N_MICROBATCH = 1
ADAM_LR = 0.001
ADAM_B1 = 0.9
ADAM_B2 = 0.999
ADAM_EPS = 1e-08
ADAM_WD = 0.01
ADAM_STEP = 10
PER_EXAMPLE_BATCH_AXIS = {'x': 0, 'p': 1, 'loss_target': 0}
SHARED_INPUTS = []
_WEIGHT_DTYPES = {'rel_bias': _jnp.float32, 'norm_ffn1': _jnp.float32, 'ffn1_w_in': _jnp.float32, 'ffn1_w_out': _jnp.float32, 'norm_mix': _jnp.float32, 'w_qkv': _jnp.float32, 'q_norm_a': _jnp.float32, 'k_norm_a': _jnp.float32, 'q_norm_b': _jnp.float32, 'k_norm_b': _jnp.float32, 'sink_b': _jnp.float32, 'w_o': _jnp.float32, 'norm_ffn2': _jnp.float32, 'ffn2_w_in': _jnp.float32, 'ffn2_w_out': _jnp.float32, 'norm_ple': _jnp.float32, 'w_ple_gate': _jnp.float32, 'w_ple_proj': _jnp.float32}
MOMENT_SCALE = {'rel_bias': 1.250477e+00, 'norm_ffn1': 6.177859e+00, 'ffn1_w_in': 8.371150e-02, 'ffn1_w_out': 1.470273e-01, 'norm_mix': 3.682550e-01, 'w_qkv': 7.231630e-02, 'q_norm_a': 2.329568e+00, 'k_norm_a': 2.329281e+00, 'q_norm_b': 2.489800e+00, 'k_norm_b': 2.474215e+00, 'sink_b': 5.815535e-02, 'w_o': 7.235080e-02, 'norm_ffn2': 6.161272e+00, 'ffn2_w_in': 7.941979e-02, 'ffn2_w_out': 1.417045e-01, 'norm_ple': 9.870003e-01, 'w_ple_gate': 8.771179e-02, 'w_ple_proj': 5.121498e-01}


def _to_microbatches(a, axis):
    t = _jnp.moveaxis(a, axis, 0)
    t = t.reshape((N_MICROBATCH, t.shape[0] // N_MICROBATCH) + t.shape[1:])
    return _jnp.moveaxis(t, 1, axis + 1)


def setup_inputs(seed: int = 0) -> dict:
    inp = _fwd_setup_inputs(seed)
    key = _jax.random.fold_in(_jax.random.key(seed), 7919)
    shape, _ = _output_shape()
    out = dict(inp)
    out["loss_target"] = _jax.random.normal(_jax.random.fold_in(key, 0), shape, _jnp.float32)
    for i, name in enumerate(TWIN_WEIGHTS):
        w = inp[name].astype(_jnp.float32)
        if MOMENT_SCALE is None:
            s = _jnp.sqrt(_jnp.mean(_jnp.square(w)) + 1e-30)
        else:
            s = MOMENT_SCALE[name]
        km, kv = _jax.random.split(_jax.random.fold_in(key, i + 1))
        out[name] = w
        out["m_" + name] = s * _jax.random.normal(km, w.shape, _jnp.float32)
        out["v_" + name] = (s * s) * _jax.random.uniform(kv, w.shape, _jnp.float32, 0.5, 1.5)
    if N_MICROBATCH > 1:
        for name, axis in PER_EXAMPLE_BATCH_AXIS.items():
            out[name] = _to_microbatches(out[name], axis)
    return {'x': out['x'], 'p': out['p'], 'rel_bias': out['rel_bias'], 'norm_ffn1': out['norm_ffn1'], 'ffn1_w_in': out['ffn1_w_in'], 'ffn1_w_out': out['ffn1_w_out'], 'norm_mix': out['norm_mix'], 'w_qkv': out['w_qkv'], 'q_norm_a': out['q_norm_a'], 'k_norm_a': out['k_norm_a'], 'q_norm_b': out['q_norm_b'], 'k_norm_b': out['k_norm_b'], 'sink_b': out['sink_b'], 'w_o': out['w_o'], 'norm_ffn2': out['norm_ffn2'], 'ffn2_w_in': out['ffn2_w_in'], 'ffn2_w_out': out['ffn2_w_out'], 'norm_ple': out['norm_ple'], 'w_ple_gate': out['w_ple_gate'], 'w_ple_proj': out['w_ple_proj'], 'loss_target': out['loss_target'], 'm_rel_bias': out['m_rel_bias'], 'm_norm_ffn1': out['m_norm_ffn1'], 'm_ffn1_w_in': out['m_ffn1_w_in'], 'm_ffn1_w_out': out['m_ffn1_w_out'], 'm_norm_mix': out['m_norm_mix'], 'm_w_qkv': out['m_w_qkv'], 'm_q_norm_a': out['m_q_norm_a'], 'm_k_norm_a': out['m_k_norm_a'], 'm_q_norm_b': out['m_q_norm_b'], 'm_k_norm_b': out['m_k_norm_b'], 'm_sink_b': out['m_sink_b'], 'm_w_o': out['m_w_o'], 'm_norm_ffn2': out['m_norm_ffn2'], 'm_ffn2_w_in': out['m_ffn2_w_in'], 'm_ffn2_w_out': out['m_ffn2_w_out'], 'm_norm_ple': out['m_norm_ple'], 'm_w_ple_gate': out['m_w_ple_gate'], 'm_w_ple_proj': out['m_w_ple_proj'], 'v_rel_bias': out['v_rel_bias'], 'v_norm_ffn1': out['v_norm_ffn1'], 'v_ffn1_w_in': out['v_ffn1_w_in'], 'v_ffn1_w_out': out['v_ffn1_w_out'], 'v_norm_mix': out['v_norm_mix'], 'v_w_qkv': out['v_w_qkv'], 'v_q_norm_a': out['v_q_norm_a'], 'v_k_norm_a': out['v_k_norm_a'], 'v_q_norm_b': out['v_q_norm_b'], 'v_k_norm_b': out['v_k_norm_b'], 'v_sink_b': out['v_sink_b'], 'v_w_o': out['v_w_o'], 'v_norm_ffn2': out['v_norm_ffn2'], 'v_ffn2_w_in': out['v_ffn2_w_in'], 'v_ffn2_w_out': out['v_ffn2_w_out'], 'v_norm_ple': out['v_norm_ple'], 'v_w_ple_gate': out['v_w_ple_gate'], 'v_w_ple_proj': out['v_w_ple_proj']}


def _loss(weights, diff, rest, loss_target):
    with _jax.named_scope("forward"):
        args = {**rest, TWIN_DIFF_INPUT: diff, **{k: w.astype(_WEIGHT_DTYPES[k]) for k, w in weights.items()}}
        y = _forward(args)
    with _jax.named_scope("loss_head"):
        err = _jnp.square(y.astype(_jnp.float32) - loss_target)
        return 0.5 * _jnp.sum(_jnp.mean(err, axis=-1)) if err.ndim else 0.5 * err


def _adamw(w, g, m, v):
    m = ADAM_B1 * m + (1.0 - ADAM_B1) * g
    v = ADAM_B2 * v + (1.0 - ADAM_B2) * _jnp.square(g)
    m_hat = m / (1.0 - ADAM_B1 ** ADAM_STEP)
    v_hat = v / (1.0 - ADAM_B2 ** ADAM_STEP)
    delta = -ADAM_LR * (m_hat / (_jnp.sqrt(v_hat) + ADAM_EPS) + ADAM_WD * w)
    return delta, m, v


def reference(x, p, rel_bias, norm_ffn1, ffn1_w_in, ffn1_w_out, norm_mix, w_qkv, q_norm_a, k_norm_a, q_norm_b, k_norm_b, sink_b, w_o, norm_ffn2, ffn2_w_in, ffn2_w_out, norm_ple, w_ple_gate, w_ple_proj, loss_target, m_rel_bias, m_norm_ffn1, m_ffn1_w_in, m_ffn1_w_out, m_norm_mix, m_w_qkv, m_q_norm_a, m_k_norm_a, m_q_norm_b, m_k_norm_b, m_sink_b, m_w_o, m_norm_ffn2, m_ffn2_w_in, m_ffn2_w_out, m_norm_ple, m_w_ple_gate, m_w_ple_proj, v_rel_bias, v_norm_ffn1, v_ffn1_w_in, v_ffn1_w_out, v_norm_mix, v_w_qkv, v_q_norm_a, v_k_norm_a, v_q_norm_b, v_k_norm_b, v_sink_b, v_w_o, v_norm_ffn2, v_ffn2_w_in, v_ffn2_w_out, v_norm_ple, v_w_ple_gate, v_w_ple_proj):
    given = dict(x=x, p=p, rel_bias=rel_bias, norm_ffn1=norm_ffn1, ffn1_w_in=ffn1_w_in, ffn1_w_out=ffn1_w_out, norm_mix=norm_mix, w_qkv=w_qkv, q_norm_a=q_norm_a, k_norm_a=k_norm_a, q_norm_b=q_norm_b, k_norm_b=k_norm_b, sink_b=sink_b, w_o=w_o, norm_ffn2=norm_ffn2, ffn2_w_in=ffn2_w_in, ffn2_w_out=ffn2_w_out, norm_ple=norm_ple, w_ple_gate=w_ple_gate, w_ple_proj=w_ple_proj, loss_target=loss_target, m_rel_bias=m_rel_bias, m_norm_ffn1=m_norm_ffn1, m_ffn1_w_in=m_ffn1_w_in, m_ffn1_w_out=m_ffn1_w_out, m_norm_mix=m_norm_mix, m_w_qkv=m_w_qkv, m_q_norm_a=m_q_norm_a, m_k_norm_a=m_k_norm_a, m_q_norm_b=m_q_norm_b, m_k_norm_b=m_k_norm_b, m_sink_b=m_sink_b, m_w_o=m_w_o, m_norm_ffn2=m_norm_ffn2, m_ffn2_w_in=m_ffn2_w_in, m_ffn2_w_out=m_ffn2_w_out, m_norm_ple=m_norm_ple, m_w_ple_gate=m_w_ple_gate, m_w_ple_proj=m_w_ple_proj, v_rel_bias=v_rel_bias, v_norm_ffn1=v_norm_ffn1, v_ffn1_w_in=v_ffn1_w_in, v_ffn1_w_out=v_ffn1_w_out, v_norm_mix=v_norm_mix, v_w_qkv=v_w_qkv, v_q_norm_a=v_q_norm_a, v_k_norm_a=v_k_norm_a, v_q_norm_b=v_q_norm_b, v_k_norm_b=v_k_norm_b, v_sink_b=v_sink_b, v_w_o=v_w_o, v_norm_ffn2=v_norm_ffn2, v_ffn2_w_in=v_ffn2_w_in, v_ffn2_w_out=v_ffn2_w_out, v_norm_ple=v_norm_ple, v_w_ple_gate=v_w_ple_gate, v_w_ple_proj=v_w_ple_proj)
    weights = {n: given[n] for n in TWIN_WEIGHTS}
    shared = {n: given[n] for n in SHARED_INPUTS}
    per_example = {n: given[n] for n in ['x', 'p']}
    grad_fn = _jax.value_and_grad(_loss, argnums=(0, 1))

    def one_microbatch(ex, loss_target):
        ex = dict(ex)
        diff = ex.pop(TWIN_DIFF_INPUT)
        return grad_fn(weights, diff, {**shared, **ex}, loss_target)

    if N_MICROBATCH == 1:
        loss, (grad_w, grad_x) = one_microbatch(per_example, given["loss_target"])
    else:
        def body(carry, xs):
            loss_sum, grad_sum = carry
            l_k, (gw_k, gx_k) = one_microbatch(xs[0], xs[1])
            with _jax.named_scope("update"):
                return (loss_sum + l_k, _jax.tree.map(_jnp.add, grad_sum, gw_k)), gx_k

        init = (_jnp.zeros((), _jnp.float32), _jax.tree.map(_jnp.zeros_like, weights))
        (loss, grad_w), grad_x = _jax.lax.scan(body, init, (per_example, given["loss_target"]))
    with _jax.named_scope("update"):
        delta_w, new_m, new_v = {}, {}, {}
        for n in TWIN_WEIGHTS:
            delta_w[n], new_m[n], new_v[n] = _adamw(weights[n], grad_w[n], given["m_" + n], given["v_" + n])
    return (loss, grad_x, *[grad_w[n] for n in TWIN_WEIGHTS], *[delta_w[n] for n in TWIN_WEIGHTS],
            *[new_m[n] for n in TWIN_WEIGHTS], *[new_v[n] for n in TWIN_WEIGHTS])
```

```python
import functools
import math

import jax
import jax.numpy as jnp
from jax import lax
from jax.experimental import pallas as pl
from jax.experimental.pallas import tpu as pltpu

F32 = jnp.float32
BF16 = jnp.bfloat16

N_DEV = 8
HEAD_DIM = 64
PAIR = 2 * HEAD_DIM
BQ = 128
N_BUCKETS = 32
MAX_DISTANCE = 1024
DILATED = ((64, 1), (64, 4), (64, 16))
SWA_RADIUS = 128
EPS = 1e-6
NEG = -1e30
ADAM_LR, ADAM_B1, ADAM_B2, ADAM_EPS, ADAM_WD, ADAM_STEP = 0.001, 0.9, 0.999, 1e-08, 0.01, 10
VMEM_LIMIT = 56 * 1024 * 1024
AXES = ("x", "y", "c")
MESH = pl.DeviceIdType.MESH

BIG = ("ffn1_w_in", "ffn1_w_out", "w_qkv", "w_o", "ffn2_w_in", "ffn2_w_out", "w_ple_gate", "w_ple_proj")
SMALL = ("rel_bias", "norm_ffn1", "norm_mix", "q_norm_a", "k_norm_a", "q_norm_b", "k_norm_b", "sink_b",
         "norm_ffn2", "norm_ple")
WEIGHTS = ("rel_bias", "norm_ffn1", "ffn1_w_in", "ffn1_w_out", "norm_mix", "w_qkv", "q_norm_a", "k_norm_a",
           "q_norm_b", "k_norm_b", "sink_b", "w_o", "norm_ffn2", "ffn2_w_in", "ffn2_w_out", "norm_ple",
           "w_ple_gate", "w_ple_proj")
SMALL_ROWS = 96


def _params(*sem):
    return pltpu.CompilerParams(dimension_semantics=sem, vmem_limit_bytes=VMEM_LIMIT)


def _dot(a, b):
    return jnp.dot(a, b, preferred_element_type=F32)


def _dot_nt(a, b):
    return lax.dot_general(a, b, (((1,), (1,)), ((), ())), preferred_element_type=F32)


def _dot_tn(a, b):
    return lax.dot_general(a, b, (((0,), (0,)), ((), ())), preferred_element_type=F32)


def _sigmoid(x):
    return 1.0 / (1.0 + jnp.exp(-x))


def _rstd(xv):
    return lax.rsqrt(jnp.mean(xv * xv, axis=-1, keepdims=True) + EPS)


def _norm_bwd(dh, xv, gv):
    r = _rstd(xv)
    xn = xv * r
    dg = jnp.sum(dh * xn, axis=0, keepdims=True)
    dxn = dh * gv
    dx = r * (dxn - xn * jnp.mean(dxn * xn, axis=-1, keepdims=True))
    return dx, dg


def _lo_mask(shape):
    return lax.broadcasted_iota(jnp.int32, shape, len(shape) - 1) < HEAD_DIM


def _half_sum(t, lo):
    s0 = jnp.sum(jnp.where(lo, t, 0.0), axis=1, keepdims=True)
    s1 = jnp.sum(jnp.where(lo, 0.0, t), axis=1, keepdims=True)
    return jnp.where(lo, s0, s1)


def _ffn_fwd(x, g, w_in8, w_out4, tm):
    T, D = x.shape
    nj, C = w_out4.shape[0], w_out4.shape[1]

    def body(x_ref, g_ref, wg_ref, wu_ref, wo_ref, xo_ref, h_ref, zg_ref, zu_ref, s_ref, h_scr, acc):
        j = pl.program_id(1)

        @pl.when(j == 0)
        def _():
            xv = x_ref[...]
            hb = (xv * _rstd(xv) * g_ref[...]).astype(BF16)
            h_scr[...] = hb
            h_ref[...] = hb
            acc[...] = jnp.zeros_like(acc)

        hb = h_scr[...]
        gt = _dot(hb, wg_ref[...])
        up = _dot(hb, wu_ref[...])
        s = (gt * _sigmoid(gt) * up).astype(BF16)
        zg_ref[...] = gt.astype(BF16)
        zu_ref[...] = up.astype(BF16)
        s_ref[...] = s
        acc[...] += _dot(s, wo_ref[...])

        @pl.when(j == nj - 1)
        def _():
            xo_ref[...] = x_ref[...] + 0.5 * acc[...]

    tok = pl.BlockSpec((tm, D), lambda i, j: (i, 0))
    chunk = pl.BlockSpec((None, tm, C), lambda i, j: (j, i, 0))
    return pl.pallas_call(
        body, name="ffn_fwd", grid=(T // tm, nj),
        in_specs=[tok, pl.BlockSpec((1, D), lambda i, j: (0, 0)),
                  pl.BlockSpec((None, D, C), lambda i, j: (j, 0, 0)),
                  pl.BlockSpec((None, D, C), lambda i, j: (j + nj, 0, 0)),
                  pl.BlockSpec((None, C, D), lambda i, j: (j, 0, 0))],
        out_specs=[tok, tok, chunk, chunk, chunk],
        out_shape=[jax.ShapeDtypeStruct((T, D), F32), jax.ShapeDtypeStruct((T, D), BF16),
                   jax.ShapeDtypeStruct((nj, T, C), BF16), jax.ShapeDtypeStruct((nj, T, C), BF16),
                   jax.ShapeDtypeStruct((nj, T, C), BF16)],
        scratch_shapes=[pltpu.VMEM((tm, D), BF16), pltpu.VMEM((tm, D), F32)],
        compiler_params=_params("parallel", "arbitrary"),
    )(x, g, w_in8, w_in8, w_out4)


def _ffn_bwd(dxo, x, g, zg, zu, w_in8, w_out4, tm):
    T, D = x.shape
    nj, C = w_out4.shape[0], w_out4.shape[1]

    def body(dxo_ref, x_ref, g_ref, zg_ref, zu_ref, wg_ref, wu_ref, wo_ref,
             dx_ref, dy_ref, dzg_ref, dzu_ref, dgn_ref, dy_scr, acc):
        i, j = pl.program_id(0), pl.program_id(1)

        @pl.when(j == 0)
        def _():
            dyb = (0.5 * dxo_ref[...]).astype(BF16)
            dy_scr[...] = dyb
            dy_ref[...] = dyb
            acc[...] = jnp.zeros_like(acc)

        ds = _dot_nt(dy_scr[...], wo_ref[...])
        gt = zg_ref[...].astype(F32)
        up = zu_ref[...].astype(F32)
        sg = _sigmoid(gt)
        dgt = (ds * up * (sg * (1.0 + gt * (1.0 - sg)))).astype(BF16)
        dup = (ds * (gt * sg)).astype(BF16)
        dzg_ref[...] = dgt
        dzu_ref[...] = dup
        acc[...] += _dot_nt(dgt, wg_ref[...]) + _dot_nt(dup, wu_ref[...])

        @pl.when(j == nj - 1)
        def _():
            dx, dg = _norm_bwd(acc[...], x_ref[...], g_ref[...])
            dx_ref[...] = dxo_ref[...] + dx

            @pl.when(i == 0)
            def _():
                dgn_ref[...] = dg

            @pl.when(i > 0)
            def _():
                dgn_ref[...] += dg

    tok = pl.BlockSpec((tm, D), lambda i, j: (i, 0))
    chunk = pl.BlockSpec((None, tm, C), lambda i, j: (j, i, 0))
    row = pl.BlockSpec((1, D), lambda i, j: (0, 0))
    return pl.pallas_call(
        body, name="ffn_bwd", grid=(T // tm, nj),
        in_specs=[tok, tok, row, chunk, chunk,
                  pl.BlockSpec((None, D, C), lambda i, j: (j, 0, 0)),
                  pl.BlockSpec((None, D, C), lambda i, j: (j + nj, 0, 0)),
                  pl.BlockSpec((None, C, D), lambda i, j: (j, 0, 0))],
        out_specs=[tok, tok, chunk, chunk, row],
        out_shape=[jax.ShapeDtypeStruct((T, D), F32), jax.ShapeDtypeStruct((T, D), BF16),
                   jax.ShapeDtypeStruct((nj, T, C), BF16), jax.ShapeDtypeStruct((nj, T, C), BF16),
                   jax.ShapeDtypeStruct((1, D), F32)],
        scratch_shapes=[pltpu.VMEM((tm, D), BF16), pltpu.VMEM((tm, D), F32)],
        compiler_params=_params("arbitrary", "arbitrary"),
    )(dxo, x, g, zg, zu, w_in8, w_in8, w_out4)


def _ffn_dw(h, dzg, dzu, s, dy, tk):
    T, D = h.shape
    nj, C = s.shape[0], s.shape[2]

    def body(h_ref, dzg_ref, dzu_ref, s_ref, dy_ref, dwg_ref, dwu_ref, dwo_ref):
        k = pl.program_id(1)

        @pl.when(k == 0)
        def _():
            dwg_ref[...] = jnp.zeros_like(dwg_ref)
            dwu_ref[...] = jnp.zeros_like(dwu_ref)
            dwo_ref[...] = jnp.zeros_like(dwo_ref)

        hb = h_ref[...]
        dwg_ref[...] += _dot_tn(hb, dzg_ref[...])
        dwu_ref[...] += _dot_tn(hb, dzu_ref[...])
        dwo_ref[...] += _dot_tn(s_ref[...], dy_ref[...])

    tok = pl.BlockSpec((tk, D), lambda j, k: (k, 0))
    chunk = pl.BlockSpec((None, tk, C), lambda j, k: (j, k, 0))
    return pl.pallas_call(
        body, name="ffn_dw", grid=(nj, T // tk),
        in_specs=[tok, chunk, chunk, chunk, tok],
        out_specs=[pl.BlockSpec((None, D, C), lambda j, k: (j, 0, 0)),
                   pl.BlockSpec((None, D, C), lambda j, k: (j, 0, 0)),
                   pl.BlockSpec((None, C, D), lambda j, k: (j, 0, 0))],
        out_shape=[jax.ShapeDtypeStruct((nj, D, C), F32), jax.ShapeDtypeStruct((nj, D, C), F32),
                   jax.ShapeDtypeStruct((nj, C, D), F32)],
        compiler_params=_params("parallel", "arbitrary"),
    )(h, dzg, dzu, s, dy)


def _matmul_tn(a, b, tn, tk):
    T, Ka = a.shape
    N = b.shape[1]

    def body(a_ref, b_ref, o_ref):
        @pl.when(pl.program_id(1) == 0)
        def _():
            o_ref[...] = jnp.zeros_like(o_ref)

        o_ref[...] += _dot_tn(a_ref[...], b_ref[...])

    return pl.pallas_call(
        body, name="matmul_tn", grid=(N // tn, T // tk),
        in_specs=[pl.BlockSpec((tk, Ka), lambda n, k: (k, 0)), pl.BlockSpec((tk, tn), lambda n, k: (k, n))],
        out_specs=pl.BlockSpec((Ka, tn), lambda n, k: (0, n)),
        out_shape=jax.ShapeDtypeStruct((Ka, N), F32),
        compiler_params=_params("parallel", "arbitrary"),
    )(a, b)


def _qkv_fwd(x, g, w, tm):
    T, D = x.shape
    N = w.shape[1]

    def body(x_ref, g_ref, w_ref, o_ref, h_ref):
        xv = x_ref[...]
        hb = (xv * _rstd(xv) * g_ref[...]).astype(BF16)
        h_ref[...] = hb
        o_ref[...] = _dot(hb, w_ref[...])

    return pl.pallas_call(
        body, name="qkv_fwd", grid=(T // tm,),
        in_specs=[pl.BlockSpec((tm, D), lambda i: (i, 0)), pl.BlockSpec((1, D), lambda i: (0, 0)),
                  pl.BlockSpec((D, N), lambda i: (0, 0))],
        out_specs=[pl.BlockSpec((tm, N), lambda i: (i, 0)), pl.BlockSpec((tm, D), lambda i: (i, 0))],
        out_shape=[jax.ShapeDtypeStruct((T, N), F32), jax.ShapeDtypeStruct((T, D), BF16)],
        compiler_params=_params("parallel"),
    )(x, g, w)


def _attn_prep(qkv, gains2, tm):
    T = qkv.shape[0]
    scale = HEAD_DIM ** -0.5

    def body(qkv_ref, g_ref, qa_ref, ka_ref, va_ref, qb_ref, kb_ref, vb_ref):
        lo = _lo_mask((tm, PAIR))

        def normed(c, gi, mult):
            xv = qkv_ref[:, c * PAIR:(c + 1) * PAIR]
            r = lax.rsqrt(_half_sum(xv * xv, lo) * (1.0 / HEAD_DIM) + EPS)
            y = xv * r * g_ref[gi:gi + 1, :]
            return y * mult if mult != 1.0 else y

        def both_halves(v):
            sw = pltpu.roll(v, HEAD_DIM, 1)
            return jnp.where(lo, v, sw), jnp.where(lo, sw, v)

        for c in range(4):
            qa_ref[c] = normed(c, 0, scale).astype(BF16)
            ka_ref[c] = normed(4 + c, 1, 1.0).astype(BF16)
            va_ref[c] = qkv_ref[:, (8 + c) * PAIR:(9 + c) * PAIR].astype(BF16)
            qb_ref[c] = normed(12 + c, 2, scale).astype(BF16)
        k0, k1 = both_halves(normed(16, 3, 1.0))
        kb_ref[0] = k0.astype(BF16)
        kb_ref[1] = k1.astype(BF16)
        v0, v1 = both_halves(qkv_ref[:, 17 * PAIR:18 * PAIR])
        vb_ref[0] = v0.astype(BF16)
        vb_ref[1] = v1.astype(BF16)

    four = pl.BlockSpec((4, tm, PAIR), lambda i: (0, i, 0))
    two = pl.BlockSpec((2, tm, PAIR), lambda i: (0, i, 0))
    s4 = jax.ShapeDtypeStruct((4, T, PAIR), BF16)
    s2 = jax.ShapeDtypeStruct((2, T, PAIR), BF16)
    return pl.pallas_call(
        body, name="attn_prep", grid=(T // tm,),
        in_specs=[pl.BlockSpec((tm, qkv.shape[1]), lambda i: (i, 0)), pl.BlockSpec((4, PAIR), lambda i: (0, 0))],
        out_specs=[four, four, four, four, two, two],
        out_shape=[s4, s4, s4, s4, s2, s2],
        compiler_params=_params("parallel"),
    )(qkv, gains2)


def _attn_masks(b, L, R, W):
    col = lax.broadcasted_iota(jnp.int32, (BQ, W), 1) + (b * BQ - R)
    return (col >= 0) & (col < L)


def _attn_fwd(q, kp, vp, bias, sink, R, pairs_per_kv, pairs_per_bias):
    N, L, _ = q.shape
    W = BQ + 2 * R
    nb = L // BQ

    def body(sink_ref, q_ref, k_ref, v_ref, bias_ref, o_ref, lse_ref):
        n = pl.program_id(0)
        lo_q = _lo_mask((BQ, PAIR))
        sk = (sink_ref[2 * n], sink_ref[2 * n + 1])

        def blk(b, carry):
            q0 = pl.multiple_of(b * BQ, BQ)
            qv = q_ref[pl.ds(q0, BQ), :]
            kw = k_ref[pl.ds(q0, W), :]
            vw = v_ref[pl.ds(q0, W), :]
            valid = _attn_masks(b, L, R, W)
            outs, lses = [], []
            for h in range(2):
                qh = jnp.where(lo_q, qv, jnp.zeros_like(qv)) if h == 0 else jnp.where(lo_q, jnp.zeros_like(qv), qv)
                s = jnp.where(valid, _dot_nt(qh, kw) + bias_ref[h], NEG)
                m = jnp.maximum(jnp.max(s, axis=1, keepdims=True), sk[h])
                p = jnp.exp(s - m)
                l = jnp.sum(p, axis=1, keepdims=True) + jnp.exp(sk[h] - m)
                outs.append(_dot(p.astype(BF16), vw) / l)
                lses.append(m + jnp.log(l))
            o_ref[pl.ds(q0, BQ), :] = jnp.where(lo_q, outs[0], outs[1])
            lse_ref[pl.ds(q0, BQ), :] = jnp.where(lo_q, lses[0], lses[1])
            return carry

        lax.fori_loop(0, nb, blk, 0)

    qspec = pl.BlockSpec((None, L, PAIR), lambda n: (n, 0, 0))
    kspec = pl.BlockSpec((None, L + 2 * R, PAIR), lambda n: (n // pairs_per_kv, 0, 0))
    return pl.pallas_call(
        body, name="attn_fwd", grid=(N,),
        in_specs=[pl.BlockSpec(memory_space=pltpu.SMEM), qspec, kspec, kspec,
                  pl.BlockSpec((2, BQ, W), lambda n: (n // pairs_per_bias, 0, 0))],
        out_specs=[qspec, qspec],
        out_shape=[jax.ShapeDtypeStruct((N, L, PAIR), F32), jax.ShapeDtypeStruct((N, L, PAIR), F32)],
        compiler_params=_params("parallel"),
    )(sink, q, kp, vp, bias)


def _attn_bwd(q, kp, vp, bias, sink, o, lse, do, R, pairs_per_kv, pairs_per_bias):
    N, L, _ = q.shape
    Nk = kp.shape[0]
    Hb = bias.shape[0]
    W = BQ + 2 * R
    nb = L // BQ

    def body(sink_ref, q_ref, k_ref, v_ref, bias_ref, o_ref, lse_ref, do_ref,
             dq_ref, dk_ref, dv_ref, dbias_ref, dsink_ref):
        n = pl.program_id(0)
        lo_q = _lo_mask((BQ, PAIR))
        lo_w = _lo_mask((W, PAIR))
        sk = (sink_ref[2 * n], sink_ref[2 * n + 1])

        @pl.when(n % pairs_per_kv == 0)
        def _():
            dk_ref[...] = jnp.zeros_like(dk_ref)
            dv_ref[...] = jnp.zeros_like(dv_ref)

        @pl.when(n % pairs_per_bias == 0)
        def _():
            dbias_ref[...] = jnp.zeros_like(dbias_ref)

        def blk(b, dsk):
            q0 = pl.multiple_of(b * BQ, BQ)
            qv = q_ref[pl.ds(q0, BQ), :]
            kw = k_ref[pl.ds(q0, W), :]
            vw = v_ref[pl.ds(q0, W), :]
            dov = do_ref[pl.ds(q0, BQ), :]
            lsev = lse_ref[pl.ds(q0, BQ), :]
            delta2 = _half_sum(dov * o_ref[pl.ds(q0, BQ), :], lo_q)
            dob = dov.astype(BF16)
            valid = _attn_masks(b, L, R, W)
            zq = jnp.zeros_like(qv)
            zd = jnp.zeros_like(dob)
            dq_h, dk_h, dv_h, dsk_new = [], [], [], []
            for h in range(2):
                sel = lo_q if h == 0 else jnp.logical_not(lo_q)
                qh = jnp.where(sel, qv, zq)
                doh = jnp.where(sel, dob, zd)
                lse_h = jnp.max(jnp.where(sel, lsev, NEG), axis=1, keepdims=True)
                delta_h = jnp.max(jnp.where(sel, delta2, NEG), axis=1, keepdims=True)
                s = jnp.where(valid, _dot_nt(qh, kw) + bias_ref[h], NEG)
                p = jnp.exp(s - lse_h)
                dp = _dot_nt(doh, vw)
                ds = p * (dp - delta_h)
                dsb = ds.astype(BF16)
                dbias_ref[h] += ds
                dq_h.append(_dot(dsb, kw))
                dk_h.append(_dot_tn(dsb, qh))
                dv_h.append(_dot_tn(p.astype(BF16), doh))
                dsk_new.append(dsk[h] - jnp.exp(sk[h] - lse_h) * delta_h)
            dq_ref[pl.ds(q0, BQ), :] = jnp.where(lo_q, dq_h[0], dq_h[1])
            dk_ref[pl.ds(q0, W), :] += jnp.where(lo_w, dk_h[0], dk_h[1])
            dv_ref[pl.ds(q0, W), :] += jnp.where(lo_w, dv_h[0], dv_h[1])
            return tuple(dsk_new)

        zero = jnp.zeros((BQ, 1), F32)
        d0, d1 = lax.fori_loop(0, nb, blk, (zero, zero))
        lane = lax.broadcasted_iota(jnp.int32, (8, PAIR), 1)
        dsink_ref[...] = jnp.where(lane < HEAD_DIM, jnp.sum(d0, axis=0, keepdims=True),
                                   jnp.sum(d1, axis=0, keepdims=True))

    qspec = pl.BlockSpec((None, L, PAIR), lambda n: (n, 0, 0))
    kspec = pl.BlockSpec((None, L + 2 * R, PAIR), lambda n: (n // pairs_per_kv, 0, 0))
    bspec = pl.BlockSpec((2, BQ, W), lambda n: (n // pairs_per_bias, 0, 0))
    return pl.pallas_call(
        body, name="attn_bwd", grid=(N,),
        in_specs=[pl.BlockSpec(memory_space=pltpu.SMEM), qspec, kspec, kspec, bspec, qspec, qspec, qspec],
        out_specs=[qspec, kspec, kspec, bspec, pl.BlockSpec((None, 8, PAIR), lambda n: (n, 0, 0))],
        out_shape=[jax.ShapeDtypeStruct((N, L, PAIR), F32),
                   jax.ShapeDtypeStruct((Nk, L + 2 * R, PAIR), F32),
                   jax.ShapeDtypeStruct((Nk, L + 2 * R, PAIR), F32),
                   jax.ShapeDtypeStruct((Hb, BQ, W), F32),
                   jax.ShapeDtypeStruct((N, 8, PAIR), F32)],
        compiler_params=_params("arbitrary"),
    )(sink, q, kp, vp, bias, o, lse, do)


def _attn_merge(o1, l1, o4, l4, o16, l16, ob, tm):
    T = o1.shape[1]

    def body(o1_ref, l1_ref, o4_ref, l4_ref, o16_ref, l16_ref, ob_ref, oa_ref, la_ref, cat_ref):
        for c in range(4):
            a, b, d = l1_ref[c], l4_ref[c], l16_ref[c]
            m = jnp.maximum(jnp.maximum(a, b), d)
            wa, wb, wd = jnp.exp(a - m), jnp.exp(b - m), jnp.exp(d - m)
            z = wa + wb + wd
            o = (wa * o1_ref[c] + wb * o4_ref[c] + wd * o16_ref[c]) / z
            oa_ref[c] = o
            la_ref[c] = m + jnp.log(z)
            cat_ref[:, c * PAIR:(c + 1) * PAIR] = o.astype(BF16)
            cat_ref[:, (4 + c) * PAIR:(5 + c) * PAIR] = ob_ref[c].astype(BF16)

    four = pl.BlockSpec((4, tm, PAIR), lambda i: (0, i, 0))
    s4 = jax.ShapeDtypeStruct((4, T, PAIR), F32)
    return pl.pallas_call(
        body, name="attn_merge", grid=(T // tm,),
        in_specs=[four] * 7,
        out_specs=[four, four, pl.BlockSpec((tm, 8 * PAIR), lambda i: (i, 0))],
        out_shape=[s4, s4, jax.ShapeDtypeStruct((T, 8 * PAIR), BF16)],
        compiler_params=_params("parallel"),
    )(o1, l1, o4, l4, o16, l16, ob)


def _oproj_fwd(x, o_cat, w, tm):
    T, D = x.shape

    def body(x_ref, o_ref, w_ref, out_ref):
        out_ref[...] = x_ref[...] + _dot(o_ref[...], w_ref[...])

    tok = pl.BlockSpec((tm, D), lambda i: (i, 0))
    return pl.pallas_call(
        body, name="oproj_fwd", grid=(T // tm,),
        in_specs=[tok, pl.BlockSpec((tm, o_cat.shape[1]), lambda i: (i, 0)),
                  pl.BlockSpec(w.shape, lambda i: (0, 0))],
        out_specs=tok, out_shape=jax.ShapeDtypeStruct((T, D), F32),
        compiler_params=_params("parallel"),
    )(x, o_cat, w)


def _oproj_bwd(dx, w, tm):
    T, D = dx.shape

    def body(dx_ref, w_ref, dxb_ref, do_ref):
        db = dx_ref[...].astype(BF16)
        dxb_ref[...] = db
        do = _dot_nt(db, w_ref[...])
        for c in range(8):
            do_ref[c] = do[:, c * PAIR:(c + 1) * PAIR]

    tok = pl.BlockSpec((tm, D), lambda i: (i, 0))
    return pl.pallas_call(
        body, name="oproj_bwd", grid=(T // tm,),
        in_specs=[tok, pl.BlockSpec(w.shape, lambda i: (0, 0))],
        out_specs=[tok, pl.BlockSpec((8, tm, PAIR), lambda i: (0, i, 0))],
        out_shape=[jax.ShapeDtypeStruct((T, D), BF16), jax.ShapeDtypeStruct((8, T, PAIR), F32)],
        compiler_params=_params("parallel"),
    )(dx, w)


def _attn_post(qkv, gains2, dqa, dka, dva, dqb, dkb, dvb, tm):
    T, NQ = qkv.shape
    scale = HEAD_DIM ** -0.5

    def body(qkv_ref, g_ref, qa1, qa4, qa16, ka1, ka4, ka16, va1, va4, va16, qb_ref, kb_ref, vb_ref,
             out_ref, dg_ref):
        lo = _lo_mask((tm, PAIR))

        @pl.when(pl.program_id(0) == 0)
        def _():
            dg_ref[...] = jnp.zeros_like(dg_ref)

        def norm_bwd(c, gi, dy):
            xv = qkv_ref[:, c * PAIR:(c + 1) * PAIR]
            r = lax.rsqrt(_half_sum(xv * xv, lo) * (1.0 / HEAD_DIM) + EPS)
            xn = xv * r
            dg_ref[gi:gi + 1, :] += jnp.sum(dy * xn, axis=0, keepdims=True)
            dxn = dy * g_ref[gi:gi + 1, :]
            dx = r * (dxn - xn * (_half_sum(dxn * xn, lo) * (1.0 / HEAD_DIM)))
            out_ref[:, c * PAIR:(c + 1) * PAIR] = dx.astype(BF16)

        def fold(v):
            return v + pltpu.roll(v, HEAD_DIM, 1)

        for c in range(4):
            norm_bwd(c, 0, (qa1[c] + qa4[c] + qa16[c]) * scale)
            norm_bwd(4 + c, 1, ka1[c] + ka4[c] + ka16[c])
            out_ref[:, (8 + c) * PAIR:(9 + c) * PAIR] = (va1[c] + va4[c] + va16[c]).astype(BF16)
            norm_bwd(12 + c, 2, qb_ref[c] * scale)
        norm_bwd(16, 3, jnp.where(lo, fold(kb_ref[0]), fold(kb_ref[1])))
        out_ref[:, 17 * PAIR:18 * PAIR] = jnp.where(lo, fold(vb_ref[0]), fold(vb_ref[1])).astype(BF16)

    four = pl.BlockSpec((4, tm, PAIR), lambda i: (0, i, 0))
    two = pl.BlockSpec((2, tm, PAIR), lambda i: (0, i, 0))
    return pl.pallas_call(
        body, name="attn_post", grid=(T // tm,),
        in_specs=[pl.BlockSpec((tm, NQ), lambda i: (i, 0)), pl.BlockSpec((4, PAIR), lambda i: (0, 0))]
        + [four] * 10 + [two, two],
        out_specs=[pl.BlockSpec((tm, NQ), lambda i: (i, 0)), pl.BlockSpec((4, PAIR), lambda i: (0, 0))],
        out_shape=[jax.ShapeDtypeStruct((T, NQ), BF16), jax.ShapeDtypeStruct((4, PAIR), F32)],
        compiler_params=_params("arbitrary"),
    )(qkv, gains2, *dqa, *dka, *dva, dqb, dkb, dvb)


def _dense_norm_bwd(dres, dz, w, x, g, tm):
    T, D = x.shape
    N = dz.shape[1]

    def body(dres_ref, dz_ref, w_ref, x_ref, g_ref, dx_ref, dgn_ref):
        i = pl.program_id(0)
        dx, dg = _norm_bwd(_dot_nt(dz_ref[...], w_ref[...]), x_ref[...], g_ref[...])
        dx_ref[...] = dres_ref[...] + dx

        @pl.when(i == 0)
        def _():
            dgn_ref[...] = dg

        @pl.when(i > 0)
        def _():
            dgn_ref[...] += dg

    tok = pl.BlockSpec((tm, D), lambda i: (i, 0))
    row = pl.BlockSpec((1, D), lambda i: (0, 0))
    return pl.pallas_call(
        body, name="dense_norm_bwd", grid=(T // tm,),
        in_specs=[tok, pl.BlockSpec((tm, N), lambda i: (i, 0)), pl.BlockSpec((D, N), lambda i: (0, 0)), tok, row],
        out_specs=[tok, row],
        out_shape=[jax.ShapeDtypeStruct((T, D), F32), jax.ShapeDtypeStruct((1, D), F32)],
        compiler_params=_params("arbitrary"),
    )(dres, dz, w, x, g)


def _bias_reduce(onehot, dbm):
    Hb, K = dbm.shape

    def body(oh_ref, d_ref, out_ref):
        oh = oh_ref[...]
        d = d_ref[...]
        hi = d.astype(BF16)
        r1 = d - hi.astype(F32)
        mid = r1.astype(BF16)
        low = (r1 - mid.astype(F32)).astype(BF16)
        out_ref[...] = _dot_nt(hi, oh) + _dot_nt(mid, oh) + _dot_nt(low, oh)

    vm = pl.BlockSpec(memory_space=pltpu.VMEM)
    return pl.pallas_call(
        body, name="bias_reduce", in_specs=[vm, vm], out_specs=vm,
        out_shape=jax.ShapeDtypeStruct((Hb, 128), F32),
        compiler_params=pltpu.CompilerParams(vmem_limit_bytes=VMEM_LIMIT),
    )(onehot, dbm)


def _ple_fwd(x, g, wg, p, wp, target, tm):
    T, D = x.shape
    P = p.shape[1]
    with_loss = target is not None

    def body(*refs):
        if with_loss:
            x_ref, g_ref, wg_ref, p_ref, wp_ref, t_ref, y_ref, hn_ref, gate_ref, pp_ref, pb_ref, loss_ref = refs
        else:
            x_ref, g_ref, wg_ref, p_ref, wp_ref, y_ref, hn_ref, gate_ref, pp_ref, pb_ref = refs
        i = pl.program_id(0)
        xv = x_ref[...]
        hb = (xv * _rstd(xv) * g_ref[...]).astype(BF16)
        hn_ref[...] = hb
        gate = _sigmoid(_dot(hb, wg_ref[...]))
        pb = p_ref[...].astype(BF16)
        pb_ref[...] = pb
        pp = _dot(pb, wp_ref[...])
        gate_ref[...] = gate
        pp_ref[...] = pp
        y = xv + gate * pp
        if with_loss:
            err = y - t_ref[...]
            y_ref[...] = err * (1.0 / D)
            part = jnp.broadcast_to(0.5 * jnp.sum(jnp.sum(err * err, axis=1, keepdims=True) * (1.0 / D),
                                                  axis=0, keepdims=True), (1, 128))

            @pl.when(i == 0)
            def _():
                loss_ref[...] = part

            @pl.when(i > 0)
            def _():
                loss_ref[...] += part
        else:
            y_ref[...] = y

    tok = pl.BlockSpec((tm, D), lambda i: (i, 0))
    ptok = pl.BlockSpec((tm, P), lambda i: (i, 0))
    in_specs = [tok, pl.BlockSpec((1, D), lambda i: (0, 0)), pl.BlockSpec((D, D), lambda i: (0, 0)), ptok,
                pl.BlockSpec((P, D), lambda i: (0, 0))]
    out_specs = [tok, tok, tok, tok, ptok]
    out_shape = [jax.ShapeDtypeStruct((T, D), F32), jax.ShapeDtypeStruct((T, D), BF16),
                 jax.ShapeDtypeStruct((T, D), F32), jax.ShapeDtypeStruct((T, D), F32),
                 jax.ShapeDtypeStruct((T, P), BF16)]
    args = [x, g, wg, p, wp]
    if with_loss:
        in_specs.append(tok)
        out_specs.append(pl.BlockSpec((1, 128), lambda i: (0, 0)))
        out_shape.append(jax.ShapeDtypeStruct((1, 128), F32))
        args.append(target)
    return pl.pallas_call(
        body, name="ple_fwd_loss" if with_loss else "ple_fwd", grid=(T // tm,),
        in_specs=in_specs, out_specs=out_specs, out_shape=out_shape,
        compiler_params=_params("arbitrary" if with_loss else "parallel"),
    )(*args)


def _ple_bwd(dy, gate, pp, tm):
    T, D = dy.shape

    def body(dy_ref, gate_ref, pp_ref, dgl_ref, dpp_ref):
        d = dy_ref[...]
        gt = gate_ref[...]
        dgl_ref[...] = (d * pp_ref[...] * gt * (1.0 - gt)).astype(BF16)
        dpp_ref[...] = (d * gt).astype(BF16)

    tok = pl.BlockSpec((tm, D), lambda i: (i, 0))
    return pl.pallas_call(
        body, name="ple_bwd", grid=(T // tm,), in_specs=[tok, tok, tok], out_specs=[tok, tok],
        out_shape=[jax.ShapeDtypeStruct((T, D), BF16), jax.ShapeDtypeStruct((T, D), BF16)],
        compiler_params=_params("parallel"),
    )(dy, gate, pp)


def _adamw(w, g, m, v):
    shape = w.shape
    C = shape[-1]
    w2, g2, m2, v2 = (a.reshape(-1, C) for a in (w, g, m, v))
    Rn = w2.shape[0]
    tr = Rn
    for cand in (512, 352, 256):
        if Rn % cand == 0:
            tr = cand
            break
    c1 = 1.0 - ADAM_B1 ** ADAM_STEP
    c2 = 1.0 - ADAM_B2 ** ADAM_STEP

    def body(w_ref, g_ref, m_ref, v_ref, d_ref, nm_ref, nv_ref):
        gv = g_ref[...]
        mn = ADAM_B1 * m_ref[...] + (1.0 - ADAM_B1) * gv
        vn = ADAM_B2 * v_ref[...] + (1.0 - ADAM_B2) * (gv * gv)
        d_ref[...] = -ADAM_LR * ((mn / c1) / (jnp.sqrt(vn / c2) + ADAM_EPS) + ADAM_WD * w_ref[...])
        nm_ref[...] = mn
        nv_ref[...] = vn

    spec = pl.BlockSpec((tr, C), lambda i: (i, 0))
    sh = jax.ShapeDtypeStruct((Rn, C), F32)
    d, nm, nv = pl.pallas_call(
        body, name="adamw", grid=(Rn // tr,), in_specs=[spec] * 4, out_specs=[spec] * 3, out_shape=[sh] * 3,
        compiler_params=_params("parallel"),
    )(w2, g2, m2, v2)
    return d.reshape(shape), nm.reshape(shape), nv.reshape(shape)


def _my_place():
    x, y, c = lax.axis_index("x"), lax.axis_index("y"), lax.axis_index("c")
    chips = [(1 - x, y), (x, 1 - y), (1 - x, 1 - y)]
    return x, y, c, chips


def _all_gather(flat):
    R, Wd = flat.shape

    def body(x_ref, out_ref, send_sems, recv_sems, local_sem):
        x, y, c, chips = _my_place()
        me, sibling = (x, y, c), (x, y, 1 - c)

        def rows(px, py, pc):
            return out_ref.at[4 * px + 2 * py + pc]

        def copy(k, block, to, src=None):
            return pltpu.make_async_remote_copy(
                src_ref=rows(*block) if src is None else src, dst_ref=rows(*block),
                send_sem=send_sems.at[k], recv_sem=recv_sems.at[k], device_id=to, device_id_type=MESH)

        mine = pltpu.make_async_copy(x_ref, rows(*me), local_sem)
        mine.start()
        first = [copy(0, me, sibling, src=x_ref)]
        first += [copy(1 + j, me, (*chip, c), src=x_ref) for j, chip in enumerate(chips)]
        for cp in first:
            cp.start()
        passed = [copy(4 + j, (*chip, c), sibling) for j, chip in enumerate(chips)]
        for j, chip in enumerate(chips):
            copy(1 + j, (*chip, c), me).wait_recv()
            passed[j].start()
        copy(0, sibling, me).wait_recv()
        for j, chip in enumerate(chips):
            copy(4 + j, (*chip, 1 - c), me).wait_recv()
        for cp in first + passed:
            cp.wait_send()
        mine.wait()

    return pl.pallas_call(
        body, name="all_gather",
        in_specs=[pl.BlockSpec(memory_space=pl.ANY)], out_specs=pl.BlockSpec(memory_space=pl.ANY),
        out_shape=jax.ShapeDtypeStruct((N_DEV, R, Wd), flat.dtype),
        scratch_shapes=[pltpu.SemaphoreType.DMA((7,)), pltpu.SemaphoreType.DMA((7,)), pltpu.SemaphoreType.DMA],
    )(flat)


def _reduce_scatter(gparts, tr):
    _, R, Wd = gparts.shape
    nt = R // tr

    def body(g_ref, out_ref, a_ref, p_ref, b_ref, vb, vo_b, vo_f, d2d_send, d2d_recv, ici_send, ici_recv):
        x, y, c, chips = _my_place()
        sibling = (x, y, 1 - c)
        allchips = [(x, y)] + chips

        def dev(chip, pc):
            return 4 * chip[0] + 2 * chip[1] + pc

        d2d = [pltpu.make_async_remote_copy(
            src_ref=g_ref.at[dev(q, 1 - c)], dst_ref=a_ref.at[a], send_sem=d2d_send.at[a], recv_sem=d2d_recv.at[a],
            device_id=sibling, device_id_type=MESH) for a, q in enumerate(allchips)]
        for cp in d2d:
            cp.start()

        def add_tiles(srcs, dst, vo):
            def step(t, carry):
                r = pl.ds(pl.multiple_of(t * tr, tr), tr)
                acc = None
                for s_i, src in enumerate(srcs):
                    pltpu.sync_copy(src.at[r], vb.at[s_i])
                for s_i in range(len(srcs)):
                    term = vb[s_i].astype(F32)
                    acc = term if acc is None else acc + term
                vo[...] = acc.astype(vo.dtype)
                pltpu.sync_copy(vo, dst.at[r])
                return carry

            lax.fori_loop(0, nt, step, 0)

        ici = []
        for j, q in enumerate(chips):
            d2d[j + 1].wait_recv()
            add_tiles([g_ref.at[dev(q, c)], a_ref.at[j + 1]], p_ref.at[j], vo_b)
            cp = pltpu.make_async_remote_copy(
                src_ref=p_ref.at[j], dst_ref=b_ref.at[j], send_sem=ici_send.at[j], recv_sem=ici_recv.at[j],
                device_id=(*q, c), device_id_type=MESH)
            cp.start()
            ici.append(cp)
        d2d[0].wait_recv()
        for cp in ici:
            cp.wait_recv()
        add_tiles([g_ref.at[dev((x, y), c)], a_ref.at[0], b_ref.at[0], b_ref.at[1], b_ref.at[2]], out_ref, vo_f)
        for cp in d2d + ici:
            cp.wait_send()

    hbm = pl.BlockSpec(memory_space=pl.ANY)
    out, _, _, _ = pl.pallas_call(
        body, name="reduce_scatter",
        in_specs=[hbm], out_specs=[hbm, hbm, hbm, hbm],
        out_shape=[jax.ShapeDtypeStruct((R, Wd), F32), jax.ShapeDtypeStruct((4, R, Wd), BF16),
                   jax.ShapeDtypeStruct((3, R, Wd), BF16), jax.ShapeDtypeStruct((3, R, Wd), BF16)],
        scratch_shapes=[pltpu.VMEM((5, tr, Wd), BF16), pltpu.VMEM((tr, Wd), BF16), pltpu.VMEM((tr, Wd), F32),
                        pltpu.SemaphoreType.DMA((4,)), pltpu.SemaphoreType.DMA((4,)),
                        pltpu.SemaphoreType.DMA((3,)), pltpu.SemaphoreType.DMA((3,))],
        compiler_params=pltpu.CompilerParams(vmem_limit_bytes=VMEM_LIMIT),
    )(gparts)
    return out


def _all_reduce_small(v):
    Rn, Wd = v.shape

    def body(v_ref, out_ref, gat_ref, send_sems, recv_sems):
        x, y, c, _ = _my_place()
        me = 4 * x + 2 * y + c
        gat_ref[me] = v_ref[...]
        copies = []
        for k in range(1, N_DEV):
            fx, fy, fc = (k >> 2) & 1, (k >> 1) & 1, k & 1
            peer = (x ^ fx, y ^ fy, c ^ fc)
            cp = pltpu.make_async_remote_copy(
                src_ref=v_ref, dst_ref=gat_ref.at[me], send_sem=send_sems.at[k - 1], recv_sem=recv_sems.at[k - 1],
                device_id=peer, device_id_type=MESH)
            cp.start()
            copies.append(cp)
        for cp in copies:
            cp.wait_recv()
        for cp in copies:
            cp.wait_send()
        acc = gat_ref[0]
        for k in range(1, N_DEV):
            acc = acc + gat_ref[k]
        out_ref[...] = acc

    vm = pl.BlockSpec(memory_space=pltpu.VMEM)
    return pl.pallas_call(
        body, name="all_reduce_small", in_specs=[vm], out_specs=vm,
        out_shape=jax.ShapeDtypeStruct((Rn, Wd), F32),
        scratch_shapes=[pltpu.VMEM((N_DEV, Rn, Wd), F32), pltpu.SemaphoreType.DMA((7,)),
                        pltpu.SemaphoreType.DMA((7,))],
    )(v)


def _t5_bucket(rel):
    half = N_BUCKETS // 2
    max_exact = half // 2
    ret = jnp.where(rel > 0, half, 0)
    n = jnp.abs(rel)
    nf = jnp.maximum(n, 1).astype(F32)
    large = max_exact + (jnp.log(nf / max_exact) / math.log(MAX_DISTANCE / max_exact)
                         * (half - max_exact)).astype(jnp.int32)
    large = jnp.minimum(large, half - 1)
    return ret + jnp.where(n < max_exact, n, large)


def _band(R, d):
    W = BQ + 2 * R
    rel = jnp.arange(W)[None, :] - R - jnp.arange(BQ)[:, None]
    return _t5_bucket(rel * d), jnp.abs(rel) <= R


def _bias_matrix(table, R, d):
    bkt, in_band = _band(R, d)
    return jnp.where(in_band[None], jnp.transpose(table[bkt], (2, 0, 1)), NEG)


def _bias_grad(dbm, R, d):
    bkt, in_band = _band(R, d)
    onehot = ((bkt.reshape(1, -1) == jnp.arange(128)[:, None]) & in_band.reshape(1, -1)).astype(BF16)
    return _bias_reduce(onehot, dbm.reshape(dbm.shape[0], -1))[:, :N_BUCKETS].T


def _deint(a, d):
    if d == 1:
        return a
    H, T, X = a.shape
    return a.reshape(H, T // d, d, X).transpose(0, 2, 1, 3).reshape(H * d, T // d, X)


def _reint(a, d):
    if d == 1:
        return a
    Hd, L, X = a.shape
    return a.reshape(Hd // d, d, L, X).transpose(0, 2, 1, 3).reshape(Hd // d, L * d, X)


def _pad_rows(a, R):
    return jnp.pad(a, ((0, 0), (R, R), (0, 0)))


def _tile2(gain):
    return jnp.concatenate([gain, gain])


def _layer_fwd(x, p, w, sm, i, target, tm):
    saved = {}
    saved["x0"] = x
    x1, saved["h1"], saved["zg1"], saved["zu1"], saved["s1"] = _ffn_fwd(
        x, sm["norm_ffn1"][i][None], w["ffn1_w_in"], w["ffn1_w_out"], tm)
    saved["x1"] = x1
    qkv, saved["hm"] = _qkv_fwd(x1, sm["norm_mix"][i][None], w["w_qkv"], tm)
    saved["qkv"] = qkv
    gains2 = jnp.stack([_tile2(sm[k][i]) for k in ("q_norm_a", "k_norm_a", "q_norm_b", "k_norm_b")])
    saved["gains2"] = gains2
    qa, ka, va, qb, kb, vb = _attn_prep(qkv, gains2, tm)
    no_sink = jnp.full((8,), NEG, F32)
    branches = []
    outs = []
    for R, d in DILATED:
        bias = _bias_matrix(sm["rel_bias"][:, :8], R, d)
        qd, kd, vd = _deint(qa, d), _pad_rows(_deint(ka, d), R), _pad_rows(_deint(va, d), R)
        sink = jnp.tile(no_sink, d)
        o, lse = _attn_fwd(qd, kd, vd, bias, sink, R, 1, d)
        branches.append((qd, kd, vd, bias, sink, R, d))
        outs += [_reint(o, d), _reint(lse, d)]
    bias_b = _bias_matrix(sm["rel_bias"][:, 8:], SWA_RADIUS, 1)
    kbp, vbp = _pad_rows(kb, SWA_RADIUS), _pad_rows(vb, SWA_RADIUS)
    sink_b = sm["sink_b"][i]
    ob, lb = _attn_fwd(qb, kbp, vbp, bias_b, sink_b, SWA_RADIUS, 2, 1)
    oa, la, o_cat = _attn_merge(*outs, ob, tm)
    saved.update(branches=branches, b=(qb, kbp, vbp, bias_b, sink_b), oa=oa, la=la, ob=ob, lb=lb, o_cat=o_cat)
    x2 = _oproj_fwd(x1, o_cat, w["w_o"], tm)
    saved["x2"] = x2
    x3, saved["h2"], saved["zg2"], saved["zu2"], saved["s2"] = _ffn_fwd(
        x2, sm["norm_ffn2"][i][None], w["ffn2_w_in"], w["ffn2_w_out"], tm)
    saved["x3"] = x3
    res = _ple_fwd(x3, sm["norm_ple"][i][None], w["w_ple_gate"], p, w["w_ple_proj"], target, tm)
    y, saved["hp"], saved["gate"], saved["pp"], saved["pb"] = res[:5]
    loss = res[5] if target is not None else None
    return y, loss, saved


def _layer_bwd(dy, w, sm, i, sv, tm):
    gb, gs = {}, {}
    D = dy.shape[1]
    dgl, dpp = _ple_bwd(dy, sv["gate"], sv["pp"], tm)
    gb["w_ple_gate"] = _matmul_tn(sv["hp"], dgl, D, tm)
    gb["w_ple_proj"] = _matmul_tn(sv["pb"], dpp, D, tm)
    dx3, gs["norm_ple"] = _dense_norm_bwd(dy, dgl, w["w_ple_gate"], sv["x3"], sm["norm_ple"][i][None], tm)
    dx2, dyb, dzg, dzu, gs["norm_ffn2"] = _ffn_bwd(dx3, sv["x2"], sm["norm_ffn2"][i][None], sv["zg2"], sv["zu2"],
                                                   w["ffn2_w_in"], w["ffn2_w_out"], tm)
    dwg, dwu, dwo = _ffn_dw(sv["h2"], dzg, dzu, sv["s2"], dyb, tm)
    gb["ffn2_w_in"], gb["ffn2_w_out"] = jnp.concatenate([dwg, dwu], axis=0), dwo
    dx2b, do = _oproj_bwd(dx2, w["w_o"], tm)
    gb["w_o"] = _matmul_tn(sv["o_cat"], dx2b, D, tm)
    do_a, do_b = do[:4], do[4:]
    dqa, dka, dva = [], [], []
    drel_a = 0.0
    for qd, kd, vd, bias, sink, R, d in sv["branches"]:
        dq, dk, dv, dbm, _ = _attn_bwd(qd, kd, vd, bias, sink, _deint(sv["oa"], d), _deint(sv["la"], d),
                                        _deint(do_a, d), R, 1, d)
        L = qd.shape[1]
        dqa.append(_reint(dq, d))
        dka.append(_reint(dk[:, R:R + L], d))
        dva.append(_reint(dv[:, R:R + L], d))
        drel_a = drel_a + _bias_grad(dbm, R, d)
    qb, kbp, vbp, bias_b, sink_b = sv["b"]
    dqb, dkb, dvb, dbm_b, dsink = _attn_bwd(qb, kbp, vbp, bias_b, sink_b, sv["ob"], sv["lb"], do_b,
                                            SWA_RADIUS, 2, 1)
    T = qb.shape[1]
    drel_b = _bias_grad(dbm_b, SWA_RADIUS, 1)
    gs["rel_bias"] = jnp.concatenate([drel_a, drel_b], axis=1)
    gs["sink_b"] = jnp.stack([dsink[:, 0, 0], dsink[:, 0, HEAD_DIM]], axis=1).reshape(-1)
    dqkv, dgains2 = _attn_post(sv["qkv"], sv["gains2"], dqa, dka, dva, dqb,
                               dkb[:, SWA_RADIUS:SWA_RADIUS + T], dvb[:, SWA_RADIUS:SWA_RADIUS + T], tm // 2)
    dgains = dgains2[:, :HEAD_DIM] + dgains2[:, HEAD_DIM:]
    for k, name in enumerate(("q_norm_a", "k_norm_a", "q_norm_b", "k_norm_b")):
        gs[name] = dgains[k]
    gb["w_qkv"] = _matmul_tn(sv["hm"], dqkv, dqkv.shape[1] // 2, tm)
    dx1, gs["norm_mix"] = _dense_norm_bwd(dx2, dqkv, w["w_qkv"], sv["x1"], sm["norm_mix"][i][None], tm)
    dx0, dyb, dzg, dzu, gs["norm_ffn1"] = _ffn_bwd(dx1, sv["x0"], sm["norm_ffn1"][i][None], sv["zg1"], sv["zu1"],
                                                   w["ffn1_w_in"], w["ffn1_w_out"], tm)
    dwg, dwu, dwo = _ffn_dw(sv["h1"], dzg, dzu, sv["s1"], dyb, tm)
    gb["ffn1_w_in"], gb["ffn1_w_out"] = jnp.concatenate([dwg, dwu], axis=0), dwo
    return dx0, gb, gs


def _local_step(x, p, target, wl, sm, tm=512):
    depth = len(wl)
    saves = []
    loss = None
    for i in range(depth):
        x, loss, sv = _layer_fwd(x, p[i], wl[i], sm, i, target if i == depth - 1 else None, tm)
        saves.append(sv)
    dy = x
    gbs = [None] * depth
    gsm = {}
    for i in reversed(range(depth)):
        dy, gbs[i], gs = _layer_bwd(dy, wl[i], sm, i, saves[i], tm)
        for k, v in gs.items():
            gsm.setdefault(k, {})[i] = v
    small = {}
    for k, per_layer in gsm.items():
        if k == "rel_bias":
            small[k] = sum(per_layer.values())
        else:
            small[k] = jnp.stack([per_layer[i].reshape(-1) for i in range(depth)])
    return loss, dy, gbs, small


def _to_gathered(name, full):
    if name in ("ffn1_w_in", "ffn2_w_in"):
        return full.reshape(N_DEV, 1024, -1)
    if name in ("ffn1_w_out", "ffn2_w_out"):
        return full.reshape(N_DEV // 2, -1, 1024)
    if name == "w_qkv":
        return full.reshape(N_DEV, 1024, -1).transpose(1, 0, 2).reshape(1024, -1)
    if name == "w_ple_proj":
        return full.reshape(N_DEV, 256, -1).transpose(1, 0, 2).reshape(256, -1)
    return full.reshape(-1, 1024)


def _from_gathered(name, g):
    if name in ("ffn1_w_in", "ffn2_w_in", "ffn1_w_out", "ffn2_w_out", "w_o", "w_ple_gate"):
        return g.reshape(N_DEV, -1, 1024)
    rows = g.shape[0]
    return g.reshape(rows, N_DEV, -1).transpose(1, 0, 2).reshape(N_DEV, -1, 1024)


def _pack_small(d, extra=None):
    parts = [d[k].reshape(-1) for k in SMALL]
    if extra is not None:
        parts.append(extra.reshape(-1))
    flat = jnp.concatenate(parts)
    return jnp.pad(flat, (0, SMALL_ROWS * 128 - flat.shape[0])).reshape(SMALL_ROWS, 128)


def _unpack_small(buf, like):
    flat = buf.reshape(-1)
    out, off = {}, 0
    for k in SMALL:
        n = like[k].size
        out[k] = flat[off:off + n].reshape(like[k].shape)
        off += n
    return out, flat[off]


def kernel(x, p, rel_bias, norm_ffn1, ffn1_w_in, ffn1_w_out, norm_mix, w_qkv, q_norm_a, k_norm_a, q_norm_b, k_norm_b, sink_b, w_o, norm_ffn2, ffn2_w_in, ffn2_w_out, norm_ple, w_ple_gate, w_ple_proj, loss_target, m_rel_bias, m_norm_ffn1, m_ffn1_w_in, m_ffn1_w_out, m_norm_mix, m_w_qkv, m_q_norm_a, m_k_norm_a, m_q_norm_b, m_k_norm_b, m_sink_b, m_w_o, m_norm_ffn2, m_ffn2_w_in, m_ffn2_w_out, m_norm_ple, m_w_ple_gate, m_w_ple_proj, v_rel_bias, v_norm_ffn1, v_ffn1_w_in, v_ffn1_w_out, v_norm_mix, v_w_qkv, v_q_norm_a, v_k_norm_a, v_q_norm_b, v_k_norm_b, v_sink_b, v_w_o, v_norm_ffn2, v_ffn2_w_in, v_ffn2_w_out, v_norm_ple, v_w_ple_gate, v_w_ple_proj):
    wts = dict(rel_bias=rel_bias, norm_ffn1=norm_ffn1, ffn1_w_in=ffn1_w_in, ffn1_w_out=ffn1_w_out,
               norm_mix=norm_mix, w_qkv=w_qkv, q_norm_a=q_norm_a, k_norm_a=k_norm_a, q_norm_b=q_norm_b,
               k_norm_b=k_norm_b, sink_b=sink_b, w_o=w_o, norm_ffn2=norm_ffn2, ffn2_w_in=ffn2_w_in,
               ffn2_w_out=ffn2_w_out, norm_ple=norm_ple, w_ple_gate=w_ple_gate, w_ple_proj=w_ple_proj)
    mom = dict(rel_bias=m_rel_bias, norm_ffn1=m_norm_ffn1, ffn1_w_in=m_ffn1_w_in, ffn1_w_out=m_ffn1_w_out,
               norm_mix=m_norm_mix, w_qkv=m_w_qkv, q_norm_a=m_q_norm_a, k_norm_a=m_k_norm_a, q_norm_b=m_q_norm_b,
               k_norm_b=m_k_norm_b, sink_b=m_sink_b, w_o=m_w_o, norm_ffn2=m_norm_ffn2, ffn2_w_in=m_ffn2_w_in,
               ffn2_w_out=m_ffn2_w_out, norm_ple=m_norm_ple, w_ple_gate=m_w_ple_gate, w_ple_proj=m_w_ple_proj)
    var = dict(rel_bias=v_rel_bias, norm_ffn1=v_norm_ffn1, ffn1_w_in=v_ffn1_w_in, ffn1_w_out=v_ffn1_w_out,
               norm_mix=v_norm_mix, w_qkv=v_w_qkv, q_norm_a=v_q_norm_a, k_norm_a=v_k_norm_a, q_norm_b=v_q_norm_b,
               k_norm_b=v_k_norm_b, sink_b=v_sink_b, w_o=v_w_o, norm_ffn2=v_norm_ffn2, ffn2_w_in=v_ffn2_w_in,
               ffn2_w_out=v_ffn2_w_out, norm_ple=v_norm_ple, w_ple_gate=v_w_ple_gate, w_ple_proj=v_w_ple_proj)
    depth = ffn1_w_in.shape[0]

    rows = {k: wts[k][0].size // 1024 for k in BIG}
    flat = jnp.concatenate([wts[k][i].reshape(-1, 1024) for i in range(depth) for k in BIG]).astype(BF16)
    gathered = _all_gather(flat)
    wl, off = [], 0
    for i in range(depth):
        layer = {}
        for k in BIG:
            layer[k] = _to_gathered(k, gathered[:, off:off + rows[k]])
            off += rows[k]
        wl.append(layer)

    sm = {k: wts[k] for k in SMALL}
    loss, dx, gbs, gsmall = _local_step(x[0], p[:, 0], loss_target[0], wl, sm)

    gparts = jnp.concatenate([_from_gathered(k, gbs[i][k]).astype(BF16) for i in range(depth) for k in BIG],
                             axis=1)
    gsum = _reduce_scatter(gparts, 768)
    small_sum, loss_sum = _unpack_small(_all_reduce_small(_pack_small(gsmall, loss[0, :1])), sm)

    grads, off = dict(small_sum), 0
    per_layer = {k: [] for k in BIG}
    for i in range(depth):
        for k in BIG:
            per_layer[k].append(gsum[off:off + rows[k]].reshape(wts[k].shape[1:]))
            off += rows[k]
    for k in BIG:
        grads[k] = jnp.stack(per_layer[k])

    delta, new_m, new_v = {}, {}, {}
    for k in BIG:
        delta[k], new_m[k], new_v[k] = _adamw(wts[k], grads[k], mom[k], var[k])
    zeros = {k: jnp.zeros_like(wts[k]) for k in SMALL}
    ds, ms, vs = _adamw(_pack_small(wts), _pack_small(small_sum), _pack_small(mom), _pack_small(var))
    for packed, dst in ((ds, delta), (ms, new_m), (vs, new_v)):
        dst.update(_unpack_small(packed, zeros)[0])

    return (loss_sum, dx[None], *[grads[k] for k in WEIGHTS], *[delta[k] for k in WEIGHTS],
            *[new_m[k] for k in WEIGHTS], *[new_v[k] for k in WEIGHTS])
```

```python
import functools
import math

import jax
import jax.numpy as jnp
from jax import lax
from jax.experimental import pallas as pl
from jax.experimental.pallas import tpu as pltpu

F32 = jnp.float32
BF16 = jnp.bfloat16

N_DEV = 8
HEAD_DIM = 64
PAIR = 2 * HEAD_DIM
BQ = 128
N_BUCKETS = 32
MAX_DISTANCE = 1024
DILATED = ((64, 1), (64, 4), (64, 16))
SWA_RADIUS = 128
EPS = 1e-6
NEG = -1e30
ADAM_LR, ADAM_B1, ADAM_B2, ADAM_EPS, ADAM_WD, ADAM_STEP = 0.001, 0.9, 0.999, 1e-08, 0.01, 10
VMEM_LIMIT = 56 * 1024 * 1024
AXES = ("x", "y", "c")
MESH = pl.DeviceIdType.MESH

BIG = ("ffn1_w_in", "ffn1_w_out", "w_qkv", "w_o", "ffn2_w_in", "ffn2_w_out", "w_ple_gate", "w_ple_proj")
SMALL = ("rel_bias", "norm_ffn1", "norm_mix", "q_norm_a", "k_norm_a", "q_norm_b", "k_norm_b", "sink_b",
         "norm_ffn2", "norm_ple")
WEIGHTS = ("rel_bias", "norm_ffn1", "ffn1_w_in", "ffn1_w_out", "norm_mix", "w_qkv", "q_norm_a", "k_norm_a",
           "q_norm_b", "k_norm_b", "sink_b", "w_o", "norm_ffn2", "ffn2_w_in", "ffn2_w_out", "norm_ple",
           "w_ple_gate", "w_ple_proj")
SMALL_ROWS = 96


def _params(*sem):
    return pltpu.CompilerParams(dimension_semantics=sem, vmem_limit_bytes=VMEM_LIMIT)


def _dot(a, b):
    return jnp.dot(a, b, preferred_element_type=F32)


def _dot_nt(a, b):
    return lax.dot_general(a, b, (((1,), (1,)), ((), ())), preferred_element_type=F32)


def _dot_tn(a, b):
    return lax.dot_general(a, b, (((0,), (0,)), ((), ())), preferred_element_type=F32)


def _sigmoid(x):
    return 1.0 / (1.0 + jnp.exp(-x))


def _rstd(xv):
    return lax.rsqrt(jnp.mean(xv * xv, axis=-1, keepdims=True) + EPS)


def _norm_bwd(dh, xv, gv):
    r = _rstd(xv)
    xn = xv * r
    dg = jnp.sum(dh * xn, axis=0, keepdims=True)
    dxn = dh * gv
    dx = r * (dxn - xn * jnp.mean(dxn * xn, axis=-1, keepdims=True))
    return dx, dg


def _lo_mask(shape):
    return lax.broadcasted_iota(jnp.int32, shape, len(shape) - 1) < HEAD_DIM


def _half_sum(t, lo):
    s0 = jnp.sum(jnp.where(lo, t, 0.0), axis=1, keepdims=True)
    s1 = jnp.sum(jnp.where(lo, 0.0, t), axis=1, keepdims=True)
    return jnp.where(lo, s0, s1)


def _ffn_fwd(x, g, w_in8, w_out4, tm):
    T, D = x.shape
    nj, C = w_out4.shape[0], w_out4.shape[1]

    def body(x_ref, g_ref, wg_ref, wu_ref, wo_ref, xo_ref, h_ref, zg_ref, zu_ref, s_ref, h_scr, acc):
        j = pl.program_id(1)

        @pl.when(j == 0)
        def _():
            xv = x_ref[...]
            hb = (xv * _rstd(xv) * g_ref[...]).astype(BF16)
            h_scr[...] = hb
            h_ref[...] = hb
            acc[...] = jnp.zeros_like(acc)

        hb = h_scr[...]
        gt = _dot(hb, wg_ref[...])
        up = _dot(hb, wu_ref[...])
        s = (gt * _sigmoid(gt) * up).astype(BF16)
        zg_ref[...] = gt.astype(BF16)
        zu_ref[...] = up.astype(BF16)
        s_ref[...] = s
        acc[...] += _dot(s, wo_ref[...])

        @pl.when(j == nj - 1)
        def _():
            xo_ref[...] = x_ref[...] + 0.5 * acc[...]

    tok = pl.BlockSpec((tm, D), lambda i, j: (i, 0))
    chunk = pl.BlockSpec((None, tm, C), lambda i, j: (j, i, 0))
    return pl.pallas_call(
        body, name="ffn_fwd", grid=(T // tm, nj),
        in_specs=[tok, pl.BlockSpec((1, D), lambda i, j: (0, 0)),
                  pl.BlockSpec((None, D, C), lambda i, j: (j, 0, 0)),
                  pl.BlockSpec((None, D, C), lambda i, j: (j + nj, 0, 0)),
                  pl.BlockSpec((None, C, D), lambda i, j: (j, 0, 0))],
        out_specs=[tok, tok, chunk, chunk, chunk],
        out_shape=[jax.ShapeDtypeStruct((T, D), F32), jax.ShapeDtypeStruct((T, D), BF16),
                   jax.ShapeDtypeStruct((nj, T, C), BF16), jax.ShapeDtypeStruct((nj, T, C), BF16),
                   jax.ShapeDtypeStruct((nj, T, C), BF16)],
        scratch_shapes=[pltpu.VMEM((tm, D), BF16), pltpu.VMEM((tm, D), F32)],
        compiler_params=_params("parallel", "arbitrary"),
    )(x, g, w_in8, w_in8, w_out4)


def _ffn_bwd(dxo, x, g, zg, zu, w_in8, w_out4, tm):
    T, D = x.shape
    nj, C = w_out4.shape[0], w_out4.shape[1]

    def body(dxo_ref, x_ref, g_ref, zg_ref, zu_ref, wg_ref, wu_ref, wo_ref,
             dx_ref, dy_ref, dzg_ref, dzu_ref, dgn_ref, dy_scr, acc):
        i, j = pl.program_id(0), pl.program_id(1)

        @pl.when(j == 0)
        def _():
            dyb = (0.5 * dxo_ref[...]).astype(BF16)
            dy_scr[...] = dyb
            dy_ref[...] = dyb
            acc[...] = jnp.zeros_like(acc)

        ds = _dot_nt(dy_scr[...], wo_ref[...])
        gt = zg_ref[...].astype(F32)
        up = zu_ref[...].astype(F32)
        sg = _sigmoid(gt)
        dgt = (ds * up * (sg * (1.0 + gt * (1.0 - sg)))).astype(BF16)
        dup = (ds * (gt * sg)).astype(BF16)
        dzg_ref[...] = dgt
        dzu_ref[...] = dup
        acc[...] += _dot_nt(dgt, wg_ref[...]) + _dot_nt(dup, wu_ref[...])

        @pl.when(j == nj - 1)
        def _():
            dx, dg = _norm_bwd(acc[...], x_ref[...], g_ref[...])
            dx_ref[...] = dxo_ref[...] + dx

            @pl.when(i == 0)
            def _():
                dgn_ref[...] = dg

            @pl.when(i > 0)
            def _():
                dgn_ref[...] += dg

    tok = pl.BlockSpec((tm, D), lambda i, j: (i, 0))
    chunk = pl.BlockSpec((None, tm, C), lambda i, j: (j, i, 0))
    row = pl.BlockSpec((1, D), lambda i, j: (0, 0))
    return pl.pallas_call(
        body, name="ffn_bwd", grid=(T // tm, nj),
        in_specs=[tok, tok, row, chunk, chunk,
                  pl.BlockSpec((None, D, C), lambda i, j: (j, 0, 0)),
                  pl.BlockSpec((None, D, C), lambda i, j: (j + nj, 0, 0)),
                  pl.BlockSpec((None, C, D), lambda i, j: (j, 0, 0))],
        out_specs=[tok, tok, chunk, chunk, row],
        out_shape=[jax.ShapeDtypeStruct((T, D), F32), jax.ShapeDtypeStruct((T, D), BF16),
                   jax.ShapeDtypeStruct((nj, T, C), BF16), jax.ShapeDtypeStruct((nj, T, C), BF16),
                   jax.ShapeDtypeStruct((1, D), F32)],
        scratch_shapes=[pltpu.VMEM((tm, D), BF16), pltpu.VMEM((tm, D), F32)],
        compiler_params=_params("arbitrary", "arbitrary"),
    )(dxo, x, g, zg, zu, w_in8, w_in8, w_out4)


def _ffn_dw(h, dzg, dzu, s, dy, tk):
    T, D = h.shape
    nj, C = s.shape[0], s.shape[2]
    nk = T // tk

    def body(h_ref, dzg_ref, dzu_ref, s_ref, dy_ref, dwg_ref, dwu_ref, dwo_ref, ag, au, ao):
        k = pl.program_id(1)

        @pl.when(k == 0)
        def _():
            ag[...] = jnp.zeros_like(ag)
            au[...] = jnp.zeros_like(au)
            ao[...] = jnp.zeros_like(ao)

        hb = h_ref[...]
        ag[...] += _dot_tn(hb, dzg_ref[...])
        au[...] += _dot_tn(hb, dzu_ref[...])
        ao[...] += _dot_tn(s_ref[...], dy_ref[...])

        @pl.when(k == nk - 1)
        def _():
            dwg_ref[...] = ag[...].astype(BF16)
            dwu_ref[...] = au[...].astype(BF16)
            dwo_ref[...] = ao[...].astype(BF16)

    tok = pl.BlockSpec((tk, D), lambda j, k: (k, 0))
    chunk = pl.BlockSpec((None, tk, C), lambda j, k: (j, k, 0))
    return pl.pallas_call(
        body, name="ffn_dw", grid=(nj, nk),
        in_specs=[tok, chunk, chunk, chunk, tok],
        out_specs=[pl.BlockSpec((None, D, C), lambda j, k: (j, 0, 0)),
                   pl.BlockSpec((None, D, C), lambda j, k: (j, 0, 0)),
                   pl.BlockSpec((None, C, D), lambda j, k: (j, 0, 0))],
        out_shape=[jax.ShapeDtypeStruct((nj, D, C), BF16), jax.ShapeDtypeStruct((nj, D, C), BF16),
                   jax.ShapeDtypeStruct((nj, C, D), BF16)],
        scratch_shapes=[pltpu.VMEM((D, C), F32), pltpu.VMEM((D, C), F32), pltpu.VMEM((C, D), F32)],
        compiler_params=_params("parallel", "arbitrary"),
    )(h, dzg, dzu, s, dy)


def _matmul_tn(a, b, tn, tk):
    T, Ka = a.shape
    N = b.shape[1]
    nk = T // tk

    def body(a_ref, b_ref, o_ref, acc):
        k = pl.program_id(1)

        @pl.when(k == 0)
        def _():
            acc[...] = jnp.zeros_like(acc)

        acc[...] += _dot_tn(a_ref[...], b_ref[...])

        @pl.when(k == nk - 1)
        def _():
            o_ref[...] = acc[...].astype(BF16)

    return pl.pallas_call(
        body, name="matmul_tn", grid=(N // tn, nk),
        in_specs=[pl.BlockSpec((tk, Ka), lambda n, k: (k, 0)), pl.BlockSpec((tk, tn), lambda n, k: (k, n))],
        out_specs=pl.BlockSpec((Ka, tn), lambda n, k: (0, n)),
        out_shape=jax.ShapeDtypeStruct((Ka, N), BF16),
        scratch_shapes=[pltpu.VMEM((Ka, tn), F32)],
        compiler_params=_params("parallel", "arbitrary"),
    )(a, b)


def _qkv_fwd(x, g, w, tm):
    T, D = x.shape
    N = w.shape[1]

    def body(x_ref, g_ref, w_ref, o_ref, h_ref):
        xv = x_ref[...]
        hb = (xv * _rstd(xv) * g_ref[...]).astype(BF16)
        h_ref[...] = hb
        o_ref[...] = _dot(hb, w_ref[...])

    return pl.pallas_call(
        body, name="qkv_fwd", grid=(T // tm,),
        in_specs=[pl.BlockSpec((tm, D), lambda i: (i, 0)), pl.BlockSpec((1, D), lambda i: (0, 0)),
                  pl.BlockSpec((D, N), lambda i: (0, 0))],
        out_specs=[pl.BlockSpec((tm, N), lambda i: (i, 0)), pl.BlockSpec((tm, D), lambda i: (i, 0))],
        out_shape=[jax.ShapeDtypeStruct((T, N), F32), jax.ShapeDtypeStruct((T, D), BF16)],
        compiler_params=_params("parallel"),
    )(x, g, w)


def _attn_prep(qkv, gains2, tm):
    T = qkv.shape[0]
    scale = HEAD_DIM ** -0.5

    def body(qkv_ref, g_ref, qa_ref, ka_ref, va_ref, qb_ref, kb_ref, vb_ref):
        lo = _lo_mask((tm, PAIR))

        def normed(c, gi, mult):
            xv = qkv_ref[:, c * PAIR:(c + 1) * PAIR]
            r = lax.rsqrt(_half_sum(xv * xv, lo) * (1.0 / HEAD_DIM) + EPS)
            y = xv * r * g_ref[gi:gi + 1, :]
            return y * mult if mult != 1.0 else y

        def both_halves(v):
            sw = pltpu.roll(v, HEAD_DIM, 1)
            return jnp.where(lo, v, sw), jnp.where(lo, sw, v)

        for c in range(4):
            qa_ref[c] = normed(c, 0, scale).astype(BF16)
            ka_ref[c] = normed(4 + c, 1, 1.0).astype(BF16)
            va_ref[c] = qkv_ref[:, (8 + c) * PAIR:(9 + c) * PAIR].astype(BF16)
            qb_ref[c] = normed(12 + c, 2, scale).astype(BF16)
        k0, k1 = both_halves(normed(16, 3, 1.0))
        kb_ref[0] = k0.astype(BF16)
        kb_ref[1] = k1.astype(BF16)
        v0, v1 = both_halves(qkv_ref[:, 17 * PAIR:18 * PAIR])
        vb_ref[0] = v0.astype(BF16)
        vb_ref[1] = v1.astype(BF16)

    four = pl.BlockSpec((4, tm, PAIR), lambda i: (0, i, 0))
    two = pl.BlockSpec((2, tm, PAIR), lambda i: (0, i, 0))
    s4 = jax.ShapeDtypeStruct((4, T, PAIR), BF16)
    s2 = jax.ShapeDtypeStruct((2, T, PAIR), BF16)
    return pl.pallas_call(
        body, name="attn_prep", grid=(T // tm,),
        in_specs=[pl.BlockSpec((tm, qkv.shape[1]), lambda i: (i, 0)), pl.BlockSpec((4, PAIR), lambda i: (0, 0))],
        out_specs=[four, four, four, four, two, two],
        out_shape=[s4, s4, s4, s4, s2, s2],
        compiler_params=_params("parallel"),
    )(qkv, gains2)


BLOCKS_PER_ITER = 2


def _loop_blocks(nb, body, init):
    u = BLOCKS_PER_ITER if nb % BLOCKS_PER_ITER == 0 else 1

    def outer(i, carry):
        for k in range(u):
            carry = body(i * u + k, carry)
        return carry

    return lax.fori_loop(0, nb // u, outer, init)


def _attn_masks(b, L, R, W):
    col = lax.broadcasted_iota(jnp.int32, (BQ, W), 1) + (b * BQ - R)
    return (col >= 0) & (col < L)


def _attn_fwd(q, kp, vp, bias, sink, R, pairs_per_kv, pairs_per_bias):
    N, L, _ = q.shape
    W = BQ + 2 * R
    nb = L // BQ

    def body(sink_ref, q_ref, k_ref, v_ref, bias_ref, o_ref, lse_ref):
        n = pl.program_id(0)
        lo_q = _lo_mask((BQ, PAIR))
        sk = (sink_ref[2 * n], sink_ref[2 * n + 1])

        def blk(b, carry):
            q0 = pl.multiple_of(b * BQ, BQ)
            qv = q_ref[pl.ds(q0, BQ), :]
            kw = k_ref[pl.ds(q0, W), :]
            vw = v_ref[pl.ds(q0, W), :]
            valid = _attn_masks(b, L, R, W)
            outs, lses = [], []
            for h in range(2):
                qh = jnp.where(lo_q, qv, jnp.zeros_like(qv)) if h == 0 else jnp.where(lo_q, jnp.zeros_like(qv), qv)
                s = jnp.where(valid, _dot_nt(qh, kw) + bias_ref[h], NEG)
                m = jnp.maximum(jnp.max(s, axis=1, keepdims=True), sk[h])
                p = jnp.exp(s - m)
                l = jnp.sum(p, axis=1, keepdims=True) + jnp.exp(sk[h] - m)
                outs.append(_dot(p.astype(BF16), vw) / l)
                lses.append(m + jnp.log(l))
            o_ref[pl.ds(q0, BQ), :] = jnp.where(lo_q, outs[0], outs[1])
            lse_ref[pl.ds(q0, BQ), :] = jnp.where(lo_q, lses[0], lses[1])
            return carry

        _loop_blocks(nb, blk, 0)

    qspec = pl.BlockSpec((None, L, PAIR), lambda n: (n, 0, 0))
    kspec = pl.BlockSpec((None, L + 2 * R, PAIR), lambda n: (n // pairs_per_kv, 0, 0))
    return pl.pallas_call(
        body, name="attn_fwd", grid=(N,),
        in_specs=[pl.BlockSpec(memory_space=pltpu.SMEM), qspec, kspec, kspec,
                  pl.BlockSpec((2, BQ, W), lambda n: (n // pairs_per_bias, 0, 0))],
        out_specs=[qspec, qspec],
        out_shape=[jax.ShapeDtypeStruct((N, L, PAIR), F32), jax.ShapeDtypeStruct((N, L, PAIR), F32)],
        compiler_params=_params("parallel"),
    )(sink, q, kp, vp, bias)


def _attn_bwd(q, kp, vp, bias, sink, o, lse, do, R, pairs_per_kv, pairs_per_bias):
    N, L, _ = q.shape
    Nk = kp.shape[0]
    Hb = bias.shape[0]
    W = BQ + 2 * R
    nb = L // BQ

    def body(sink_ref, q_ref, k_ref, v_ref, bias_ref, o_ref, lse_ref, do_ref,
             dq_ref, dk_ref, dv_ref, dbias_ref, dsink_ref):
        n = pl.program_id(0)
        lo_q = _lo_mask((BQ, PAIR))
        lo_w = _lo_mask((W, PAIR))
        sk = (sink_ref[2 * n], sink_ref[2 * n + 1])

        @pl.when(n % pairs_per_kv == 0)
        def _():
            dk_ref[...] = jnp.zeros_like(dk_ref)
            dv_ref[...] = jnp.zeros_like(dv_ref)

        @pl.when(n % pairs_per_bias == 0)
        def _():
            dbias_ref[...] = jnp.zeros_like(dbias_ref)

        def blk(b, dsk):
            q0 = pl.multiple_of(b * BQ, BQ)
            qv = q_ref[pl.ds(q0, BQ), :]
            kw = k_ref[pl.ds(q0, W), :]
            vw = v_ref[pl.ds(q0, W), :]
            dov = do_ref[pl.ds(q0, BQ), :]
            lsev = lse_ref[pl.ds(q0, BQ), :]
            delta2 = _half_sum(dov * o_ref[pl.ds(q0, BQ), :], lo_q)
            dob = dov.astype(BF16)
            valid = _attn_masks(b, L, R, W)
            zq = jnp.zeros_like(qv)
            zd = jnp.zeros_like(dob)
            dq_h, dk_h, dv_h, dsk_new = [], [], [], []
            for h in range(2):
                sel = lo_q if h == 0 else jnp.logical_not(lo_q)
                qh = jnp.where(sel, qv, zq)
                doh = jnp.where(sel, dob, zd)
                lse_h = jnp.max(jnp.where(sel, lsev, NEG), axis=1, keepdims=True)
                delta_h = jnp.max(jnp.where(sel, delta2, NEG), axis=1, keepdims=True)
                s = jnp.where(valid, _dot_nt(qh, kw) + bias_ref[h], NEG)
                p = jnp.exp(s - lse_h)
                dp = _dot_nt(doh, vw)
                ds = p * (dp - delta_h)
                dsb = ds.astype(BF16)
                dbias_ref[h] += ds
                dq_h.append(_dot(dsb, kw))
                dk_h.append(_dot_tn(dsb, qh))
                dv_h.append(_dot_tn(p.astype(BF16), doh))
                dsk_new.append(dsk[h] - jnp.exp(sk[h] - lse_h) * delta_h)
            dq_ref[pl.ds(q0, BQ), :] = jnp.where(lo_q, dq_h[0], dq_h[1])
            dk_ref[pl.ds(q0, W), :] += jnp.where(lo_w, dk_h[0], dk_h[1])
            dv_ref[pl.ds(q0, W), :] += jnp.where(lo_w, dv_h[0], dv_h[1])
            return tuple(dsk_new)

        zero = jnp.zeros((BQ, 1), F32)
        d0, d1 = _loop_blocks(nb, blk, (zero, zero))
        lane = lax.broadcasted_iota(jnp.int32, (8, PAIR), 1)
        dsink_ref[...] = jnp.where(lane < HEAD_DIM, jnp.sum(d0, axis=0, keepdims=True),
                                   jnp.sum(d1, axis=0, keepdims=True))

    qspec = pl.BlockSpec((None, L, PAIR), lambda n: (n, 0, 0))
    kspec = pl.BlockSpec((None, L + 2 * R, PAIR), lambda n: (n // pairs_per_kv, 0, 0))
    bspec = pl.BlockSpec((2, BQ, W), lambda n: (n // pairs_per_bias, 0, 0))
    return pl.pallas_call(
        body, name="attn_bwd", grid=(N,),
        in_specs=[pl.BlockSpec(memory_space=pltpu.SMEM), qspec, kspec, kspec, bspec, qspec, qspec, qspec],
        out_specs=[qspec, kspec, kspec, bspec, pl.BlockSpec((None, 8, PAIR), lambda n: (n, 0, 0))],
        out_shape=[jax.ShapeDtypeStruct((N, L, PAIR), F32),
                   jax.ShapeDtypeStruct((Nk, L + 2 * R, PAIR), F32),
                   jax.ShapeDtypeStruct((Nk, L + 2 * R, PAIR), F32),
                   jax.ShapeDtypeStruct((Hb, BQ, W), F32),
                   jax.ShapeDtypeStruct((N, 8, PAIR), F32)],
        compiler_params=_params("arbitrary"),
    )(sink, q, kp, vp, bias, o, lse, do)


def _attn_merge(o1, l1, o4, l4, o16, l16, ob, tm):
    T = o1.shape[1]

    def body(o1_ref, l1_ref, o4_ref, l4_ref, o16_ref, l16_ref, ob_ref, oa_ref, la_ref, cat_ref):
        for c in range(4):
            a, b, d = l1_ref[c], l4_ref[c], l16_ref[c]
            m = jnp.maximum(jnp.maximum(a, b), d)
            wa, wb, wd = jnp.exp(a - m), jnp.exp(b - m), jnp.exp(d - m)
            z = wa + wb + wd
            o = (wa * o1_ref[c] + wb * o4_ref[c] + wd * o16_ref[c]) / z
            oa_ref[c] = o
            la_ref[c] = m + jnp.log(z)
            cat_ref[:, c * PAIR:(c + 1) * PAIR] = o.astype(BF16)
            cat_ref[:, (4 + c) * PAIR:(5 + c) * PAIR] = ob_ref[c].astype(BF16)

    four = pl.BlockSpec((4, tm, PAIR), lambda i: (0, i, 0))
    s4 = jax.ShapeDtypeStruct((4, T, PAIR), F32)
    return pl.pallas_call(
        body, name="attn_merge", grid=(T // tm,),
        in_specs=[four] * 7,
        out_specs=[four, four, pl.BlockSpec((tm, 8 * PAIR), lambda i: (i, 0))],
        out_shape=[s4, s4, jax.ShapeDtypeStruct((T, 8 * PAIR), BF16)],
        compiler_params=_params("parallel"),
    )(o1, l1, o4, l4, o16, l16, ob)


def _oproj_fwd(x, o_cat, w, tm):
    T, D = x.shape

    def body(x_ref, o_ref, w_ref, out_ref):
        out_ref[...] = x_ref[...] + _dot(o_ref[...], w_ref[...])

    tok = pl.BlockSpec((tm, D), lambda i: (i, 0))
    return pl.pallas_call(
        body, name="oproj_fwd", grid=(T // tm,),
        in_specs=[tok, pl.BlockSpec((tm, o_cat.shape[1]), lambda i: (i, 0)),
                  pl.BlockSpec(w.shape, lambda i: (0, 0))],
        out_specs=tok, out_shape=jax.ShapeDtypeStruct((T, D), F32),
        compiler_params=_params("parallel"),
    )(x, o_cat, w)


def _oproj_bwd(dx, w, tm):
    T, D = dx.shape

    def body(dx_ref, w_ref, dxb_ref, do_ref):
        db = dx_ref[...].astype(BF16)
        dxb_ref[...] = db
        do = _dot_nt(db, w_ref[...])
        for c in range(8):
            do_ref[c] = do[:, c * PAIR:(c + 1) * PAIR]

    tok = pl.BlockSpec((tm, D), lambda i: (i, 0))
    return pl.pallas_call(
        body, name="oproj_bwd", grid=(T // tm,),
        in_specs=[tok, pl.BlockSpec(w.shape, lambda i: (0, 0))],
        out_specs=[tok, pl.BlockSpec((8, tm, PAIR), lambda i: (0, i, 0))],
        out_shape=[jax.ShapeDtypeStruct((T, D), BF16), jax.ShapeDtypeStruct((8, T, PAIR), F32)],
        compiler_params=_params("parallel"),
    )(dx, w)


def _attn_post(qkv, gains2, dqa, dka, dva, dqb, dkb, dvb, tm):
    T, NQ = qkv.shape
    scale = HEAD_DIM ** -0.5

    def body(qkv_ref, g_ref, qa1, qa4, qa16, ka1, ka4, ka16, va1, va4, va16, qb_ref, kb_ref, vb_ref,
             out_ref, dg_ref):
        lo = _lo_mask((tm, PAIR))

        @pl.when(pl.program_id(0) == 0)
        def _():
            dg_ref[...] = jnp.zeros_like(dg_ref)

        def norm_bwd(c, gi, dy):
            xv = qkv_ref[:, c * PAIR:(c + 1) * PAIR]
            r = lax.rsqrt(_half_sum(xv * xv, lo) * (1.0 / HEAD_DIM) + EPS)
            xn = xv * r
            dg_ref[gi:gi + 1, :] += jnp.sum(dy * xn, axis=0, keepdims=True)
            dxn = dy * g_ref[gi:gi + 1, :]
            dx = r * (dxn - xn * (_half_sum(dxn * xn, lo) * (1.0 / HEAD_DIM)))
            out_ref[:, c * PAIR:(c + 1) * PAIR] = dx.astype(BF16)

        def fold(v):
            return v + pltpu.roll(v, HEAD_DIM, 1)

        for c in range(4):
            norm_bwd(c, 0, (qa1[c] + qa4[c] + qa16[c]) * scale)
            norm_bwd(4 + c, 1, ka1[c] + ka4[c] + ka16[c])
            out_ref[:, (8 + c) * PAIR:(9 + c) * PAIR] = (va1[c] + va4[c] + va16[c]).astype(BF16)
            norm_bwd(12 + c, 2, qb_ref[c] * scale)
        norm_bwd(16, 3, jnp.where(lo, fold(kb_ref[0]), fold(kb_ref[1])))
        out_ref[:, 17 * PAIR:18 * PAIR] = jnp.where(lo, fold(vb_ref[0]), fold(vb_ref[1])).astype(BF16)

    four = pl.BlockSpec((4, tm, PAIR), lambda i: (0, i, 0))
    two = pl.BlockSpec((2, tm, PAIR), lambda i: (0, i, 0))
    return pl.pallas_call(
        body, name="attn_post", grid=(T // tm,),
        in_specs=[pl.BlockSpec((tm, NQ), lambda i: (i, 0)), pl.BlockSpec((4, PAIR), lambda i: (0, 0))]
        + [four] * 10 + [two, two],
        out_specs=[pl.BlockSpec((tm, NQ), lambda i: (i, 0)), pl.BlockSpec((4, PAIR), lambda i: (0, 0))],
        out_shape=[jax.ShapeDtypeStruct((T, NQ), BF16), jax.ShapeDtypeStruct((4, PAIR), F32)],
        compiler_params=_params("arbitrary"),
    )(qkv, gains2, *dqa, *dka, *dva, dqb, dkb, dvb)


def _dense_norm_bwd(dres, dz, w, x, g, tm):
    T, D = x.shape
    N = dz.shape[1]

    def body(dres_ref, dz_ref, w_ref, x_ref, g_ref, dx_ref, dgn_ref):
        i = pl.program_id(0)
        dx, dg = _norm_bwd(_dot_nt(dz_ref[...], w_ref[...]), x_ref[...], g_ref[...])
        dx_ref[...] = dres_ref[...] + dx

        @pl.when(i == 0)
        def _():
            dgn_ref[...] = dg

        @pl.when(i > 0)
        def _():
            dgn_ref[...] += dg

    tok = pl.BlockSpec((tm, D), lambda i: (i, 0))
    row = pl.BlockSpec((1, D), lambda i: (0, 0))
    return pl.pallas_call(
        body, name="dense_norm_bwd", grid=(T // tm,),
        in_specs=[tok, pl.BlockSpec((tm, N), lambda i: (i, 0)), pl.BlockSpec((D, N), lambda i: (0, 0)), tok, row],
        out_specs=[tok, row],
        out_shape=[jax.ShapeDtypeStruct((T, D), F32), jax.ShapeDtypeStruct((1, D), F32)],
        compiler_params=_params("arbitrary"),
    )(dres, dz, w, x, g)


def _bias_reduce(onehot, dbm):
    Hb, K = dbm.shape

    def body(oh_ref, d_ref, out_ref):
        oh = oh_ref[...]
        d = d_ref[...]
        hi = d.astype(BF16)
        r1 = d - hi.astype(F32)
        mid = r1.astype(BF16)
        low = (r1 - mid.astype(F32)).astype(BF16)
        out_ref[...] = _dot_nt(hi, oh) + _dot_nt(mid, oh) + _dot_nt(low, oh)

    vm = pl.BlockSpec(memory_space=pltpu.VMEM)
    return pl.pallas_call(
        body, name="bias_reduce", in_specs=[vm, vm], out_specs=vm,
        out_shape=jax.ShapeDtypeStruct((Hb, 128), F32),
        compiler_params=pltpu.CompilerParams(vmem_limit_bytes=VMEM_LIMIT),
    )(onehot, dbm)


def _ple_fwd(x, g, wg, p, wp, target, tm):
    T, D = x.shape
    P = p.shape[1]
    with_loss = target is not None

    def body(*refs):
        if with_loss:
            x_ref, g_ref, wg_ref, p_ref, wp_ref, t_ref, y_ref, hn_ref, gate_ref, pp_ref, pb_ref, loss_ref = refs
        else:
            x_ref, g_ref, wg_ref, p_ref, wp_ref, y_ref, hn_ref, gate_ref, pp_ref, pb_ref = refs
        i = pl.program_id(0)
        xv = x_ref[...]
        hb = (xv * _rstd(xv) * g_ref[...]).astype(BF16)
        hn_ref[...] = hb
        gate = _sigmoid(_dot(hb, wg_ref[...]))
        pb = p_ref[...].astype(BF16)
        pb_ref[...] = pb
        pp = _dot(pb, wp_ref[...])
        gate_ref[...] = gate
        pp_ref[...] = pp
        y = xv + gate * pp
        if with_loss:
            err = y - t_ref[...]
            y_ref[...] = err * (1.0 / D)
            part = jnp.broadcast_to(0.5 * jnp.sum(jnp.sum(err * err, axis=1, keepdims=True) * (1.0 / D),
                                                  axis=0, keepdims=True), (1, 128))

            @pl.when(i == 0)
            def _():
                loss_ref[...] = part

            @pl.when(i > 0)
            def _():
                loss_ref[...] += part
        else:
            y_ref[...] = y

    tok = pl.BlockSpec((tm, D), lambda i: (i, 0))
    ptok = pl.BlockSpec((tm, P), lambda i: (i, 0))
    in_specs = [tok, pl.BlockSpec((1, D), lambda i: (0, 0)), pl.BlockSpec((D, D), lambda i: (0, 0)), ptok,
                pl.BlockSpec((P, D), lambda i: (0, 0))]
    out_specs = [tok, tok, tok, tok, ptok]
    out_shape = [jax.ShapeDtypeStruct((T, D), F32), jax.ShapeDtypeStruct((T, D), BF16),
                 jax.ShapeDtypeStruct((T, D), F32), jax.ShapeDtypeStruct((T, D), F32),
                 jax.ShapeDtypeStruct((T, P), BF16)]
    args = [x, g, wg, p, wp]
    if with_loss:
        in_specs.append(tok)
        out_specs.append(pl.BlockSpec((1, 128), lambda i: (0, 0)))
        out_shape.append(jax.ShapeDtypeStruct((1, 128), F32))
        args.append(target)
    return pl.pallas_call(
        body, name="ple_fwd_loss" if with_loss else "ple_fwd", grid=(T // tm,),
        in_specs=in_specs, out_specs=out_specs, out_shape=out_shape,
        compiler_params=_params("arbitrary" if with_loss else "parallel"),
    )(*args)


def _ple_bwd(dy, gate, pp, tm):
    T, D = dy.shape

    def body(dy_ref, gate_ref, pp_ref, dgl_ref, dpp_ref):
        d = dy_ref[...]
        gt = gate_ref[...]
        dgl_ref[...] = (d * pp_ref[...] * gt * (1.0 - gt)).astype(BF16)
        dpp_ref[...] = (d * gt).astype(BF16)

    tok = pl.BlockSpec((tm, D), lambda i: (i, 0))
    return pl.pallas_call(
        body, name="ple_bwd", grid=(T // tm,), in_specs=[tok, tok, tok], out_specs=[tok, tok],
        out_shape=[jax.ShapeDtypeStruct((T, D), BF16), jax.ShapeDtypeStruct((T, D), BF16)],
        compiler_params=_params("parallel"),
    )(dy, gate, pp)


def _adamw(w, g, m, v):
    shape = w.shape
    C = shape[-1]
    w2, g2, m2, v2 = (a.reshape(-1, C) for a in (w, g, m, v))
    Rn = w2.shape[0]
    tr = Rn
    for cand in (512, 352, 256):
        if Rn % cand == 0:
            tr = cand
            break
    c1 = 1.0 - ADAM_B1 ** ADAM_STEP
    c2 = 1.0 - ADAM_B2 ** ADAM_STEP

    def body(w_ref, g_ref, m_ref, v_ref, d_ref, nm_ref, nv_ref):
        gv = g_ref[...]
        mn = ADAM_B1 * m_ref[...] + (1.0 - ADAM_B1) * gv
        vn = ADAM_B2 * v_ref[...] + (1.0 - ADAM_B2) * (gv * gv)
        d_ref[...] = -ADAM_LR * ((mn / c1) / (jnp.sqrt(vn / c2) + ADAM_EPS) + ADAM_WD * w_ref[...])
        nm_ref[...] = mn
        nv_ref[...] = vn

    spec = pl.BlockSpec((tr, C), lambda i: (i, 0))
    sh = jax.ShapeDtypeStruct((Rn, C), F32)
    d, nm, nv = pl.pallas_call(
        body, name="adamw", grid=(Rn // tr,), in_specs=[spec] * 4, out_specs=[spec] * 3, out_shape=[sh] * 3,
        compiler_params=_params("parallel"),
    )(w2, g2, m2, v2)
    return d.reshape(shape), nm.reshape(shape), nv.reshape(shape)


def _my_place():
    x, y, c = lax.axis_index("x"), lax.axis_index("y"), lax.axis_index("c")
    chips = [(1 - x, y), (x, 1 - y), (1 - x, 1 - y)]
    return x, y, c, chips


def _all_gather(flat):
    R, Wd = flat.shape

    def body(x_ref, out_ref, send_sems, recv_sems, local_sem):
        x, y, c, chips = _my_place()
        me, sibling = (x, y, c), (x, y, 1 - c)

        def rows(px, py, pc):
            return out_ref.at[4 * px + 2 * py + pc]

        def copy(k, block, to, src=None):
            return pltpu.make_async_remote_copy(
                src_ref=rows(*block) if src is None else src, dst_ref=rows(*block),
                send_sem=send_sems.at[k], recv_sem=recv_sems.at[k], device_id=to, device_id_type=MESH)

        mine = pltpu.make_async_copy(x_ref, rows(*me), local_sem)
        mine.start()
        first = [copy(0, me, sibling, src=x_ref)]
        first += [copy(1 + j, me, (*chip, c), src=x_ref) for j, chip in enumerate(chips)]
        for cp in first:
            cp.start()
        passed = [copy(4 + j, (*chip, c), sibling) for j, chip in enumerate(chips)]
        for j, chip in enumerate(chips):
            copy(1 + j, (*chip, c), me).wait_recv()
            passed[j].start()
        copy(0, sibling, me).wait_recv()
        for j, chip in enumerate(chips):
            copy(4 + j, (*chip, 1 - c), me).wait_recv()
        for cp in first + passed:
            cp.wait_send()
        mine.wait()

    return pl.pallas_call(
        body, name="all_gather",
        in_specs=[pl.BlockSpec(memory_space=pl.ANY)], out_specs=pl.BlockSpec(memory_space=pl.ANY),
        out_shape=jax.ShapeDtypeStruct((N_DEV, R, Wd), flat.dtype),
        scratch_shapes=[pltpu.SemaphoreType.DMA((7,)), pltpu.SemaphoreType.DMA((7,)), pltpu.SemaphoreType.DMA],
    )(flat)


def _reduce_scatter(gparts, tr):
    _, R, Wd = gparts.shape
    nt = R // tr

    def body(g_ref, out_ref, a_ref, p_ref, b_ref, vb, vo_b, vo_f, d2d_send, d2d_recv, ici_send, ici_recv):
        x, y, c, chips = _my_place()
        sibling = (x, y, 1 - c)
        allchips = [(x, y)] + chips

        def dev(chip, pc):
            return 4 * chip[0] + 2 * chip[1] + pc

        d2d = [pltpu.make_async_remote_copy(
            src_ref=g_ref.at[dev(q, 1 - c)], dst_ref=a_ref.at[a], send_sem=d2d_send.at[a], recv_sem=d2d_recv.at[a],
            device_id=sibling, device_id_type=MESH) for a, q in enumerate(allchips)]
        for cp in d2d:
            cp.start()

        def add_tiles(srcs, dst, vo):
            def step(t, carry):
                r = pl.ds(pl.multiple_of(t * tr, tr), tr)
                acc = None
                for s_i, src in enumerate(srcs):
                    pltpu.sync_copy(src.at[r], vb.at[s_i])
                for s_i in range(len(srcs)):
                    term = vb[s_i].astype(F32)
                    acc = term if acc is None else acc + term
                vo[...] = acc.astype(vo.dtype)
                pltpu.sync_copy(vo, dst.at[r])
                return carry

            lax.fori_loop(0, nt, step, 0)

        ici = []
        for j, q in enumerate(chips):
            d2d[j + 1].wait_recv()
            add_tiles([g_ref.at[dev(q, c)], a_ref.at[j + 1]], p_ref.at[j], vo_b)
            cp = pltpu.make_async_remote_copy(
                src_ref=p_ref.at[j], dst_ref=b_ref.at[j], send_sem=ici_send.at[j], recv_sem=ici_recv.at[j],
                device_id=(*q, c), device_id_type=MESH)
            cp.start()
            ici.append(cp)
        d2d[0].wait_recv()
        for cp in ici:
            cp.wait_recv()
        add_tiles([g_ref.at[dev((x, y), c)], a_ref.at[0], b_ref.at[0], b_ref.at[1], b_ref.at[2]], out_ref, vo_f)
        for cp in d2d + ici:
            cp.wait_send()

    hbm = pl.BlockSpec(memory_space=pl.ANY)
    out, _, _, _ = pl.pallas_call(
        body, name="reduce_scatter",
        in_specs=[hbm], out_specs=[hbm, hbm, hbm, hbm],
        out_shape=[jax.ShapeDtypeStruct((R, Wd), F32), jax.ShapeDtypeStruct((4, R, Wd), BF16),
                   jax.ShapeDtypeStruct((3, R, Wd), BF16), jax.ShapeDtypeStruct((3, R, Wd), BF16)],
        scratch_shapes=[pltpu.VMEM((5, tr, Wd), BF16), pltpu.VMEM((tr, Wd), BF16), pltpu.VMEM((tr, Wd), F32),
                        pltpu.SemaphoreType.DMA((4,)), pltpu.SemaphoreType.DMA((4,)),
                        pltpu.SemaphoreType.DMA((3,)), pltpu.SemaphoreType.DMA((3,))],
        compiler_params=pltpu.CompilerParams(vmem_limit_bytes=VMEM_LIMIT),
    )(gparts)
    return out


def _all_reduce_small(v):
    Rn, Wd = v.shape

    def body(v_ref, out_ref, gat_ref, send_sems, recv_sems):
        x, y, c, _ = _my_place()
        me = 4 * x + 2 * y + c
        gat_ref[me] = v_ref[...]
        copies = []
        for k in range(1, N_DEV):
            fx, fy, fc = (k >> 2) & 1, (k >> 1) & 1, k & 1
            peer = (x ^ fx, y ^ fy, c ^ fc)
            cp = pltpu.make_async_remote_copy(
                src_ref=v_ref, dst_ref=gat_ref.at[me], send_sem=send_sems.at[k - 1], recv_sem=recv_sems.at[k - 1],
                device_id=peer, device_id_type=MESH)
            cp.start()
            copies.append(cp)
        for cp in copies:
            cp.wait_recv()
        for cp in copies:
            cp.wait_send()
        acc = gat_ref[0]
        for k in range(1, N_DEV):
            acc = acc + gat_ref[k]
        out_ref[...] = acc

    vm = pl.BlockSpec(memory_space=pltpu.VMEM)
    return pl.pallas_call(
        body, name="all_reduce_small", in_specs=[vm], out_specs=vm,
        out_shape=jax.ShapeDtypeStruct((Rn, Wd), F32),
        scratch_shapes=[pltpu.VMEM((N_DEV, Rn, Wd), F32), pltpu.SemaphoreType.DMA((7,)),
                        pltpu.SemaphoreType.DMA((7,))],
    )(v)


def _t5_bucket(rel):
    half = N_BUCKETS // 2
    max_exact = half // 2
    ret = jnp.where(rel > 0, half, 0)
    n = jnp.abs(rel)
    nf = jnp.maximum(n, 1).astype(F32)
    large = max_exact + (jnp.log(nf / max_exact) / math.log(MAX_DISTANCE / max_exact)
                         * (half - max_exact)).astype(jnp.int32)
    large = jnp.minimum(large, half - 1)
    return ret + jnp.where(n < max_exact, n, large)


def _band(R, d):
    W = BQ + 2 * R
    rel = jnp.arange(W)[None, :] - R - jnp.arange(BQ)[:, None]
    return _t5_bucket(rel * d), jnp.abs(rel) <= R


def _onehot(R, d):
    bkt, in_band = _band(R, d)
    return ((bkt.reshape(1, -1) == jnp.arange(128)[:, None]) & in_band.reshape(1, -1)).astype(BF16)


def _bias_expand(table_t, onehot):
    H = table_t.shape[0]
    K = onehot.shape[1]

    def body(t_ref, oh_ref, out_ref):
        oh = oh_ref[...]
        t = t_ref[...]
        hi = t.astype(BF16)
        r1 = t - hi.astype(F32)
        mid = r1.astype(BF16)
        low = (r1 - mid.astype(F32)).astype(BF16)
        marked = _dot(jnp.ones(t.shape, BF16), oh) > 0.5
        out_ref[...] = jnp.where(marked, _dot(hi, oh) + _dot(mid, oh) + _dot(low, oh), NEG)

    vm = pl.BlockSpec(memory_space=pltpu.VMEM)
    return pl.pallas_call(
        body, name="bias_expand", in_specs=[vm, vm], out_specs=vm,
        out_shape=jax.ShapeDtypeStruct((H, K), F32),
        compiler_params=pltpu.CompilerParams(vmem_limit_bytes=VMEM_LIMIT),
    )(table_t, onehot)


def _bias_matrix(table, R, d):
    table_t = jnp.pad(table.T, ((0, 0), (0, 128 - N_BUCKETS)))
    return _bias_expand(table_t, _onehot(R, d)).reshape(table.shape[1], BQ, BQ + 2 * R)


def _bias_grad(dbm, R, d):
    return _bias_reduce(_onehot(R, d), dbm.reshape(dbm.shape[0], -1))[:, :N_BUCKETS].T


def _deint(a, d):
    if d == 1:
        return a
    H, T, X = a.shape
    return a.reshape(H, T // d, d, X).transpose(0, 2, 1, 3).reshape(H * d, T // d, X)


def _reint(a, d):
    if d == 1:
        return a
    Hd, L, X = a.shape
    return a.reshape(Hd // d, d, L, X).transpose(0, 2, 1, 3).reshape(Hd // d, L * d, X)


def _pad_rows(a, R):
    return jnp.pad(a, ((0, 0), (R, R), (0, 0)))


def _tile2(gain):
    return jnp.concatenate([gain, gain])


def _layer_fwd(x, p, w, sm, i, target, tm, biases):
    saved = {}
    saved["x0"] = x
    x1, saved["h1"], saved["zg1"], saved["zu1"], saved["s1"] = _ffn_fwd(
        x, sm["norm_ffn1"][i][None], w["ffn1_w_in"], w["ffn1_w_out"], tm)
    saved["x1"] = x1
    qkv, saved["hm"] = _qkv_fwd(x1, sm["norm_mix"][i][None], w["w_qkv"], tm)
    saved["qkv"] = qkv
    gains2 = jnp.stack([_tile2(sm[k][i]) for k in ("q_norm_a", "k_norm_a", "q_norm_b", "k_norm_b")])
    saved["gains2"] = gains2
    qa, ka, va, qb, kb, vb = _attn_prep(qkv, gains2, tm)
    no_sink = jnp.full((8,), NEG, F32)
    branches = []
    outs = []
    for (R, d), bias in zip(DILATED, biases[:3]):
        qd, kd, vd = _deint(qa, d), _pad_rows(_deint(ka, d), R), _pad_rows(_deint(va, d), R)
        sink = jnp.tile(no_sink, d)
        o, lse = _attn_fwd(qd, kd, vd, bias, sink, R, 1, d)
        branches.append((qd, kd, vd, bias, sink, R, d))
        outs += [_reint(o, d), _reint(lse, d)]
    bias_b = biases[3]
    kbp, vbp = _pad_rows(kb, SWA_RADIUS), _pad_rows(vb, SWA_RADIUS)
    sink_b = sm["sink_b"][i]
    ob, lb = _attn_fwd(qb, kbp, vbp, bias_b, sink_b, SWA_RADIUS, 2, 1)
    oa, la, o_cat = _attn_merge(*outs, ob, tm)
    saved.update(branches=branches, b=(qb, kbp, vbp, bias_b, sink_b), oa=oa, la=la, ob=ob, lb=lb, o_cat=o_cat)
    x2 = _oproj_fwd(x1, o_cat, w["w_o"], tm)
    saved["x2"] = x2
    x3, saved["h2"], saved["zg2"], saved["zu2"], saved["s2"] = _ffn_fwd(
        x2, sm["norm_ffn2"][i][None], w["ffn2_w_in"], w["ffn2_w_out"], tm)
    saved["x3"] = x3
    res = _ple_fwd(x3, sm["norm_ple"][i][None], w["w_ple_gate"], p, w["w_ple_proj"], target, tm)
    y, saved["hp"], saved["gate"], saved["pp"], saved["pb"] = res[:5]
    loss = res[5] if target is not None else None
    return y, loss, saved


def _layer_bwd(dy, w, sm, i, sv, tm):
    gb, gs = {}, {}
    D = dy.shape[1]
    dgl, dpp = _ple_bwd(dy, sv["gate"], sv["pp"], tm)
    gb["w_ple_gate"] = _matmul_tn(sv["hp"], dgl, D, tm)
    gb["w_ple_proj"] = _matmul_tn(sv["pb"], dpp, D, tm)
    dx3, gs["norm_ple"] = _dense_norm_bwd(dy, dgl, w["w_ple_gate"], sv["x3"], sm["norm_ple"][i][None], tm)
    dx2, dyb, dzg, dzu, gs["norm_ffn2"] = _ffn_bwd(dx3, sv["x2"], sm["norm_ffn2"][i][None], sv["zg2"], sv["zu2"],
                                                   w["ffn2_w_in"], w["ffn2_w_out"], tm)
    dwg, dwu, dwo = _ffn_dw(sv["h2"], dzg, dzu, sv["s2"], dyb, tm)
    gb["ffn2_w_in"], gb["ffn2_w_out"] = jnp.concatenate([dwg, dwu], axis=0), dwo
    dx2b, do = _oproj_bwd(dx2, w["w_o"], tm)
    gb["w_o"] = _matmul_tn(sv["o_cat"], dx2b, D, tm)
    do_a, do_b = do[:4], do[4:]
    dqa, dka, dva = [], [], []
    drel_a = 0.0
    for qd, kd, vd, bias, sink, R, d in sv["branches"]:
        dq, dk, dv, dbm, _ = _attn_bwd(qd, kd, vd, bias, sink, _deint(sv["oa"], d), _deint(sv["la"], d),
                                        _deint(do_a, d), R, 1, d)
        L = qd.shape[1]
        dqa.append(_reint(dq, d))
        dka.append(_reint(dk[:, R:R + L], d))
        dva.append(_reint(dv[:, R:R + L], d))
        drel_a = drel_a + _bias_grad(dbm, R, d)
    qb, kbp, vbp, bias_b, sink_b = sv["b"]
    dqb, dkb, dvb, dbm_b, dsink = _attn_bwd(qb, kbp, vbp, bias_b, sink_b, sv["ob"], sv["lb"], do_b,
                                            SWA_RADIUS, 2, 1)
    T = qb.shape[1]
    drel_b = _bias_grad(dbm_b, SWA_RADIUS, 1)
    gs["rel_bias"] = jnp.concatenate([drel_a, drel_b], axis=1)
    gs["sink_b"] = jnp.stack([dsink[:, 0, 0], dsink[:, 0, HEAD_DIM]], axis=1).reshape(-1)
    dqkv, dgains2 = _attn_post(sv["qkv"], sv["gains2"], dqa, dka, dva, dqb,
                               dkb[:, SWA_RADIUS:SWA_RADIUS + T], dvb[:, SWA_RADIUS:SWA_RADIUS + T], tm // 2)
    dgains = dgains2[:, :HEAD_DIM] + dgains2[:, HEAD_DIM:]
    for k, name in enumerate(("q_norm_a", "k_norm_a", "q_norm_b", "k_norm_b")):
        gs[name] = dgains[k]
    gb["w_qkv"] = _matmul_tn(sv["hm"], dqkv, dqkv.shape[1] // 2, tm)
    dx1, gs["norm_mix"] = _dense_norm_bwd(dx2, dqkv, w["w_qkv"], sv["x1"], sm["norm_mix"][i][None], tm)
    dx0, dyb, dzg, dzu, gs["norm_ffn1"] = _ffn_bwd(dx1, sv["x0"], sm["norm_ffn1"][i][None], sv["zg1"], sv["zu1"],
                                                   w["ffn1_w_in"], w["ffn1_w_out"], tm)
    dwg, dwu, dwo = _ffn_dw(sv["h1"], dzg, dzu, sv["s1"], dyb, tm)
    gb["ffn1_w_in"], gb["ffn1_w_out"] = jnp.concatenate([dwg, dwu], axis=0), dwo
    return dx0, gb, gs


def _local_step(x, p, target, wl, sm, tm=512):
    depth = len(wl)
    saves = []
    loss = None
    biases = [_bias_matrix(sm["rel_bias"][:, :8], R, d) for R, d in DILATED]
    biases.append(_bias_matrix(sm["rel_bias"][:, 8:], SWA_RADIUS, 1))
    for i in range(depth):
        x, loss, sv = _layer_fwd(x, p[i], wl[i], sm, i, target if i == depth - 1 else None, tm, biases)
        saves.append(sv)
    dy = x
    gbs = [None] * depth
    gsm = {}
    for i in reversed(range(depth)):
        dy, gbs[i], gs = _layer_bwd(dy, wl[i], sm, i, saves[i], tm)
        for k, v in gs.items():
            gsm.setdefault(k, {})[i] = v
    small = {}
    for k, per_layer in gsm.items():
        if k == "rel_bias":
            small[k] = sum(per_layer.values())
        else:
            small[k] = jnp.stack([per_layer[i].reshape(-1) for i in range(depth)])
    return loss, dy, gbs, small


def _to_gathered(name, full):
    if name in ("ffn1_w_in", "ffn2_w_in"):
        return full.reshape(N_DEV, 1024, -1)
    if name in ("ffn1_w_out", "ffn2_w_out"):
        return full.reshape(N_DEV // 2, -1, 1024)
    if name == "w_qkv":
        return full.reshape(N_DEV, 1024, -1).transpose(1, 0, 2).reshape(1024, -1)
    if name == "w_ple_proj":
        return full.reshape(N_DEV, 256, -1).transpose(1, 0, 2).reshape(256, -1)
    return full.reshape(-1, 1024)


def _from_gathered(name, g):
    if name in ("ffn1_w_in", "ffn2_w_in", "ffn1_w_out", "ffn2_w_out", "w_o", "w_ple_gate"):
        return g.reshape(N_DEV, -1, 1024)
    rows = g.shape[0]
    return g.reshape(rows, N_DEV, -1).transpose(1, 0, 2).reshape(N_DEV, -1, 1024)


def _pack_small(d, extra=None):
    parts = [d[k].reshape(-1) for k in SMALL]
    if extra is not None:
        parts.append(extra.reshape(-1))
    flat = jnp.concatenate(parts)
    return jnp.pad(flat, (0, SMALL_ROWS * 128 - flat.shape[0])).reshape(SMALL_ROWS, 128)


def _unpack_small(buf, like):
    flat = buf.reshape(-1)
    out, off = {}, 0
    for k in SMALL:
        n = like[k].size
        out[k] = flat[off:off + n].reshape(like[k].shape)
        off += n
    return out, flat[off]


def kernel(x, p, rel_bias, norm_ffn1, ffn1_w_in, ffn1_w_out, norm_mix, w_qkv, q_norm_a, k_norm_a, q_norm_b, k_norm_b, sink_b, w_o, norm_ffn2, ffn2_w_in, ffn2_w_out, norm_ple, w_ple_gate, w_ple_proj, loss_target, m_rel_bias, m_norm_ffn1, m_ffn1_w_in, m_ffn1_w_out, m_norm_mix, m_w_qkv, m_q_norm_a, m_k_norm_a, m_q_norm_b, m_k_norm_b, m_sink_b, m_w_o, m_norm_ffn2, m_ffn2_w_in, m_ffn2_w_out, m_norm_ple, m_w_ple_gate, m_w_ple_proj, v_rel_bias, v_norm_ffn1, v_ffn1_w_in, v_ffn1_w_out, v_norm_mix, v_w_qkv, v_q_norm_a, v_k_norm_a, v_q_norm_b, v_k_norm_b, v_sink_b, v_w_o, v_norm_ffn2, v_ffn2_w_in, v_ffn2_w_out, v_norm_ple, v_w_ple_gate, v_w_ple_proj):
    wts = dict(rel_bias=rel_bias, norm_ffn1=norm_ffn1, ffn1_w_in=ffn1_w_in, ffn1_w_out=ffn1_w_out,
               norm_mix=norm_mix, w_qkv=w_qkv, q_norm_a=q_norm_a, k_norm_a=k_norm_a, q_norm_b=q_norm_b,
               k_norm_b=k_norm_b, sink_b=sink_b, w_o=w_o, norm_ffn2=norm_ffn2, ffn2_w_in=ffn2_w_in,
               ffn2_w_out=ffn2_w_out, norm_ple=norm_ple, w_ple_gate=w_ple_gate, w_ple_proj=w_ple_proj)
    mom = dict(rel_bias=m_rel_bias, norm_ffn1=m_norm_ffn1, ffn1_w_in=m_ffn1_w_in, ffn1_w_out=m_ffn1_w_out,
               norm_mix=m_norm_mix, w_qkv=m_w_qkv, q_norm_a=m_q_norm_a, k_norm_a=m_k_norm_a, q_norm_b=m_q_norm_b,
               k_norm_b=m_k_norm_b, sink_b=m_sink_b, w_o=m_w_o, norm_ffn2=m_norm_ffn2, ffn2_w_in=m_ffn2_w_in,
               ffn2_w_out=m_ffn2_w_out, norm_ple=m_norm_ple, w_ple_gate=m_w_ple_gate, w_ple_proj=m_w_ple_proj)
    var = dict(rel_bias=v_rel_bias, norm_ffn1=v_norm_ffn1, ffn1_w_in=v_ffn1_w_in, ffn1_w_out=v_ffn1_w_out,
               norm_mix=v_norm_mix, w_qkv=v_w_qkv, q_norm_a=v_q_norm_a, k_norm_a=v_k_norm_a, q_norm_b=v_q_norm_b,
               k_norm_b=v_k_norm_b, sink_b=v_sink_b, w_o=v_w_o, norm_ffn2=v_norm_ffn2, ffn2_w_in=v_ffn2_w_in,
               ffn2_w_out=v_ffn2_w_out, norm_ple=v_norm_ple, w_ple_gate=v_w_ple_gate, w_ple_proj=v_w_ple_proj)
    depth = ffn1_w_in.shape[0]

    rows = {k: wts[k][0].size // 1024 for k in BIG}
    flat = jnp.concatenate([wts[k][i].reshape(-1, 1024) for i in range(depth) for k in BIG]).astype(BF16)
    gathered = _all_gather(flat)
    wl, off = [], 0
    for i in range(depth):
        layer = {}
        for k in BIG:
            layer[k] = _to_gathered(k, gathered[:, off:off + rows[k]])
            off += rows[k]
        wl.append(layer)

    sm = {k: wts[k] for k in SMALL}
    loss, dx, gbs, gsmall = _local_step(x[0], p[:, 0], loss_target[0], wl, sm)

    gparts = jnp.concatenate([_from_gathered(k, gbs[i][k]) for i in range(depth) for k in BIG],
                             axis=1)
    gsum = _reduce_scatter(gparts, 768)
    small_sum, loss_sum = _unpack_small(_all_reduce_small(_pack_small(gsmall, loss[0, :1])), sm)

    grads, off = dict(small_sum), 0
    per_layer = {k: [] for k in BIG}
    for i in range(depth):
        for k in BIG:
            per_layer[k].append(gsum[off:off + rows[k]].reshape(wts[k].shape[1:]))
            off += rows[k]
    for k in BIG:
        grads[k] = jnp.stack(per_layer[k])

    delta, new_m, new_v = {}, {}, {}
    for k in BIG:
        delta[k], new_m[k], new_v[k] = _adamw(wts[k], grads[k], mom[k], var[k])
    zeros = {k: jnp.zeros_like(wts[k]) for k in SMALL}
    ds, ms, vs = _adamw(_pack_small(wts), _pack_small(small_sum), _pack_small(mom), _pack_small(var))
    for packed, dst in ((ds, delta), (ms, new_m), (vs, new_v)):
        dst.update(_unpack_small(packed, zeros)[0])

    return (loss_sum, dx[None], *[grads[k] for k in WEIGHTS], *[delta[k] for k in WEIGHTS],
            *[new_m[k] for k in WEIGHTS], *[new_v[k] for k in WEIGHTS])
```

```python
import functools
import math

import jax
import jax.numpy as jnp
from jax import lax
from jax.experimental import pallas as pl
from jax.experimental.pallas import tpu as pltpu

F32 = jnp.float32
BF16 = jnp.bfloat16

N_DEV = 8
HEAD_DIM = 64
PAIR = 2 * HEAD_DIM
BQ = 128
N_BUCKETS = 32
MAX_DISTANCE = 1024
DILATED = ((64, 1), (64, 4), (64, 16))
SWA_RADIUS = 128
EPS = 1e-6
NEG = -1e30
ADAM_LR, ADAM_B1, ADAM_B2, ADAM_EPS, ADAM_WD, ADAM_STEP = 0.001, 0.9, 0.999, 1e-08, 0.01, 10
VMEM_LIMIT = 56 * 1024 * 1024
AXES = ("x", "y", "c")
MESH = pl.DeviceIdType.MESH

BIG = ("ffn1_w_in", "ffn1_w_out", "w_qkv", "w_o", "ffn2_w_in", "ffn2_w_out", "w_ple_gate", "w_ple_proj")
SMALL = ("rel_bias", "norm_ffn1", "norm_mix", "q_norm_a", "k_norm_a", "q_norm_b", "k_norm_b", "sink_b",
         "norm_ffn2", "norm_ple")
WEIGHTS = ("rel_bias", "norm_ffn1", "ffn1_w_in", "ffn1_w_out", "norm_mix", "w_qkv", "q_norm_a", "k_norm_a",
           "q_norm_b", "k_norm_b", "sink_b", "w_o", "norm_ffn2", "ffn2_w_in", "ffn2_w_out", "norm_ple",
           "w_ple_gate", "w_ple_proj")
SMALL_ROWS = 96


def _params(*sem):
    return pltpu.CompilerParams(dimension_semantics=sem, vmem_limit_bytes=VMEM_LIMIT)


def _dot(a, b):
    return jnp.dot(a, b, preferred_element_type=F32)


def _dot_nt(a, b):
    return lax.dot_general(a, b, (((1,), (1,)), ((), ())), preferred_element_type=F32)


def _dot_tn(a, b):
    return lax.dot_general(a, b, (((0,), (0,)), ((), ())), preferred_element_type=F32)


def _sigmoid(x):
    return 1.0 / (1.0 + jnp.exp(-x))


def _rstd(xv):
    return lax.rsqrt(jnp.mean(xv * xv, axis=-1, keepdims=True) + EPS)


def _norm_bwd(dh, xv, gv):
    r = _rstd(xv)
    xn = xv * r
    dg = jnp.sum(dh * xn, axis=0, keepdims=True)
    dxn = dh * gv
    dx = r * (dxn - xn * jnp.mean(dxn * xn, axis=-1, keepdims=True))
    return dx, dg


def _lo_mask(shape):
    return lax.broadcasted_iota(jnp.int32, shape, len(shape) - 1) < HEAD_DIM


def _half_sum(t, lo):
    s0 = jnp.sum(jnp.where(lo, t, 0.0), axis=1, keepdims=True)
    s1 = jnp.sum(jnp.where(lo, 0.0, t), axis=1, keepdims=True)
    return jnp.where(lo, s0, s1)


def _ffn_weight_specs(f, nj, D, C):
    return [pl.BlockSpec((None, None, D, C), lambda i, j: (j, f, 0, 0)),
            pl.BlockSpec((None, None, D, C), lambda i, j: (j + nj, f, 0, 0)),
            pl.BlockSpec((2, C // 2, D), lambda i, j: (j, f, 0))]


def _with_dep(body, dep, in_specs, args):
    if dep is None:
        return body, in_specs, args

    def body_after(dep_ref, *refs):
        body(*refs)

    return body_after, [pl.BlockSpec(memory_space=pl.ANY)] + in_specs, [dep] + args


def _ffn_fwd(x, g, ga, gb, f, tm, dep=None):
    T, D = x.shape
    nj, C = ga.shape[0] // 2, ga.shape[3]

    def body(x_ref, g_ref, wg_ref, wu_ref, wo_ref, xo_ref, h_ref, zg_ref, zu_ref, s_ref, h_scr, acc):
        j = pl.program_id(1)

        @pl.when(j == 0)
        def _():
            xv = x_ref[...]
            hb = (xv * _rstd(xv) * g_ref[...]).astype(BF16)
            h_scr[...] = hb
            h_ref[...] = hb
            acc[...] = jnp.zeros_like(acc)

        hb = h_scr[...]
        gt = _dot(hb, wg_ref[...])
        up = _dot(hb, wu_ref[...])
        s = (gt * _sigmoid(gt) * up).astype(BF16)
        zg_ref[...] = gt.astype(BF16)
        zu_ref[...] = up.astype(BF16)
        s_ref[...] = s
        acc[...] += _dot(s, wo_ref[...].reshape(C, D))

        @pl.when(j == nj - 1)
        def _():
            xo_ref[...] = x_ref[...] + 0.5 * acc[...]

    tok = pl.BlockSpec((tm, D), lambda i, j: (i, 0))
    chunk = pl.BlockSpec((None, tm, C), lambda i, j: (j, i, 0))
    in_specs = [tok, pl.BlockSpec((1, D), lambda i, j: (0, 0))] + _ffn_weight_specs(f, nj, D, C)
    body, in_specs, args = _with_dep(body, dep, in_specs, [x, g, ga, ga, gb])
    return pl.pallas_call(
        body, name="ffn_fwd", grid=(T // tm, nj),
        in_specs=in_specs,
        out_specs=[tok, tok, chunk, chunk, chunk],
        out_shape=[jax.ShapeDtypeStruct((T, D), F32), jax.ShapeDtypeStruct((T, D), BF16),
                   jax.ShapeDtypeStruct((nj, T, C), BF16), jax.ShapeDtypeStruct((nj, T, C), BF16),
                   jax.ShapeDtypeStruct((nj, T, C), BF16)],
        scratch_shapes=[pltpu.VMEM((tm, D), BF16), pltpu.VMEM((tm, D), F32)],
        compiler_params=_params("parallel", "arbitrary"),
    )(*args)


def _ffn_bwd(dxo, x, g, zg, zu, ga, gb, f, tm):
    T, D = x.shape
    nj, C = ga.shape[0] // 2, ga.shape[3]

    def body(dxo_ref, x_ref, g_ref, zg_ref, zu_ref, wg_ref, wu_ref, wo_ref,
             dx_ref, dy_ref, dzg_ref, dzu_ref, dgn_ref, dy_scr, acc):
        i, j = pl.program_id(0), pl.program_id(1)

        @pl.when(j == 0)
        def _():
            dyb = (0.5 * dxo_ref[...]).astype(BF16)
            dy_scr[...] = dyb
            dy_ref[...] = dyb
            acc[...] = jnp.zeros_like(acc)

        ds = _dot_nt(dy_scr[...], wo_ref[...].reshape(C, D))
        gt = zg_ref[...].astype(F32)
        up = zu_ref[...].astype(F32)
        sg = _sigmoid(gt)
        dgt = (ds * up * (sg * (1.0 + gt * (1.0 - sg)))).astype(BF16)
        dup = (ds * (gt * sg)).astype(BF16)
        dzg_ref[...] = dgt
        dzu_ref[...] = dup
        acc[...] += _dot_nt(dgt, wg_ref[...]) + _dot_nt(dup, wu_ref[...])

        @pl.when(j == nj - 1)
        def _():
            dx, dg = _norm_bwd(acc[...], x_ref[...], g_ref[...])
            dx_ref[...] = dxo_ref[...] + dx

            @pl.when(i == 0)
            def _():
                dgn_ref[...] = dg

            @pl.when(i > 0)
            def _():
                dgn_ref[...] += dg

    tok = pl.BlockSpec((tm, D), lambda i, j: (i, 0))
    chunk = pl.BlockSpec((None, tm, C), lambda i, j: (j, i, 0))
    row = pl.BlockSpec((1, D), lambda i, j: (0, 0))
    return pl.pallas_call(
        body, name="ffn_bwd", grid=(T // tm, nj),
        in_specs=[tok, tok, row, chunk, chunk] + _ffn_weight_specs(f, nj, D, C),
        out_specs=[tok, tok, chunk, chunk, row],
        out_shape=[jax.ShapeDtypeStruct((T, D), F32), jax.ShapeDtypeStruct((T, D), BF16),
                   jax.ShapeDtypeStruct((nj, T, C), BF16), jax.ShapeDtypeStruct((nj, T, C), BF16),
                   jax.ShapeDtypeStruct((1, D), F32)],
        scratch_shapes=[pltpu.VMEM((tm, D), BF16), pltpu.VMEM((tm, D), F32)],
        compiler_params=_params("arbitrary", "arbitrary"),
    )(dxo, x, g, zg, zu, ga, ga, gb)


def _ffn_dw(h, dzg, dzu, s, dy, tk):
    T, D = h.shape
    nj, C = s.shape[0], s.shape[2]
    nk = T // tk

    def body(h_ref, dzg_ref, dzu_ref, s_ref, dy_ref, dwg_ref, dwu_ref, dwo_ref, ag, au, ao):
        k = pl.program_id(1)

        @pl.when(k == 0)
        def _():
            ag[...] = jnp.zeros_like(ag)
            au[...] = jnp.zeros_like(au)
            ao[...] = jnp.zeros_like(ao)

        hb = h_ref[...]
        ag[...] += _dot_tn(hb, dzg_ref[...])
        au[...] += _dot_tn(hb, dzu_ref[...])
        ao[...] += _dot_tn(s_ref[...], dy_ref[...])

        @pl.when(k == nk - 1)
        def _():
            dwg_ref[...] = ag[...].astype(BF16)
            dwu_ref[...] = au[...].astype(BF16)
            dwo_ref[...] = ao[...].astype(BF16)

    tok = pl.BlockSpec((tk, D), lambda j, k: (k, 0))
    chunk = pl.BlockSpec((None, tk, C), lambda j, k: (j, k, 0))
    return pl.pallas_call(
        body, name="ffn_dw", grid=(nj, nk),
        in_specs=[tok, chunk, chunk, chunk, tok],
        out_specs=[pl.BlockSpec((None, D, C), lambda j, k: (j, 0, 0)),
                   pl.BlockSpec((None, D, C), lambda j, k: (j, 0, 0)),
                   pl.BlockSpec((None, C, D), lambda j, k: (j, 0, 0))],
        out_shape=[jax.ShapeDtypeStruct((nj, D, C), BF16), jax.ShapeDtypeStruct((nj, D, C), BF16),
                   jax.ShapeDtypeStruct((nj, C, D), BF16)],
        scratch_shapes=[pltpu.VMEM((D, C), F32), pltpu.VMEM((D, C), F32), pltpu.VMEM((C, D), F32)],
        compiler_params=_params("parallel", "arbitrary"),
    )(h, dzg, dzu, s, dy)


def _matmul_tn(a, b, tn, tk):
    T, Ka = a.shape
    N = b.shape[1]
    nk = T // tk

    def body(a_ref, b_ref, o_ref, acc):
        k = pl.program_id(1)

        @pl.when(k == 0)
        def _():
            acc[...] = jnp.zeros_like(acc)

        acc[...] += _dot_tn(a_ref[...], b_ref[...])

        @pl.when(k == nk - 1)
        def _():
            o_ref[...] = acc[...].astype(BF16)

    return pl.pallas_call(
        body, name="matmul_tn", grid=(N // tn, nk),
        in_specs=[pl.BlockSpec((tk, Ka), lambda n, k: (k, 0)), pl.BlockSpec((tk, tn), lambda n, k: (k, n))],
        out_specs=pl.BlockSpec((Ka, tn), lambda n, k: (0, n)),
        out_shape=jax.ShapeDtypeStruct((Ka, N), BF16),
        scratch_shapes=[pltpu.VMEM((Ka, tn), F32)],
        compiler_params=_params("parallel", "arbitrary"),
    )(a, b)


def _qkv_fwd(x, g, w, tm):
    T, D = x.shape
    N = w.shape[1]

    def body(x_ref, g_ref, w_ref, o_ref, h_ref):
        xv = x_ref[...]
        hb = (xv * _rstd(xv) * g_ref[...]).astype(BF16)
        h_ref[...] = hb
        o_ref[...] = _dot(hb, w_ref[...])

    return pl.pallas_call(
        body, name="qkv_fwd", grid=(T // tm,),
        in_specs=[pl.BlockSpec((tm, D), lambda i: (i, 0)), pl.BlockSpec((1, D), lambda i: (0, 0)),
                  pl.BlockSpec((D, N), lambda i: (0, 0))],
        out_specs=[pl.BlockSpec((tm, N), lambda i: (i, 0)), pl.BlockSpec((tm, D), lambda i: (i, 0))],
        out_shape=[jax.ShapeDtypeStruct((T, N), F32), jax.ShapeDtypeStruct((T, D), BF16)],
        compiler_params=_params("parallel"),
    )(x, g, w)


def _attn_prep(qkv, gains2, tm):
    T = qkv.shape[0]
    scale = HEAD_DIM ** -0.5

    def body(qkv_ref, g_ref, qa_ref, ka_ref, va_ref, qb_ref, kb_ref, vb_ref):
        lo = _lo_mask((tm, PAIR))

        def normed(c, gi, mult):
            xv = qkv_ref[:, c * PAIR:(c + 1) * PAIR]
            r = lax.rsqrt(_half_sum(xv * xv, lo) * (1.0 / HEAD_DIM) + EPS)
            y = xv * r * g_ref[gi:gi + 1, :]
            return y * mult if mult != 1.0 else y

        def both_halves(v):
            sw = pltpu.roll(v, HEAD_DIM, 1)
            return jnp.where(lo, v, sw), jnp.where(lo, sw, v)

        for c in range(4):
            qa_ref[c] = normed(c, 0, scale).astype(BF16)
            ka_ref[c] = normed(4 + c, 1, 1.0).astype(BF16)
            va_ref[c] = qkv_ref[:, (8 + c) * PAIR:(9 + c) * PAIR].astype(BF16)
            qb_ref[c] = normed(12 + c, 2, scale).astype(BF16)
        k0, k1 = both_halves(normed(16, 3, 1.0))
        kb_ref[0] = k0.astype(BF16)
        kb_ref[1] = k1.astype(BF16)
        v0, v1 = both_halves(qkv_ref[:, 17 * PAIR:18 * PAIR])
        vb_ref[0] = v0.astype(BF16)
        vb_ref[1] = v1.astype(BF16)

    four = pl.BlockSpec((4, tm, PAIR), lambda i: (0, i, 0))
    two = pl.BlockSpec((2, tm, PAIR), lambda i: (0, i, 0))
    s4 = jax.ShapeDtypeStruct((4, T, PAIR), BF16)
    s2 = jax.ShapeDtypeStruct((2, T, PAIR), BF16)
    return pl.pallas_call(
        body, name="attn_prep", grid=(T // tm,),
        in_specs=[pl.BlockSpec((tm, qkv.shape[1]), lambda i: (i, 0)), pl.BlockSpec((4, PAIR), lambda i: (0, 0))],
        out_specs=[four, four, four, four, two, two],
        out_shape=[s4, s4, s4, s4, s2, s2],
        compiler_params=_params("parallel"),
    )(qkv, gains2)


BLOCKS_PER_ITER = 2


def _loop_blocks(nb, body, init):
    u = BLOCKS_PER_ITER if nb % BLOCKS_PER_ITER == 0 else 1

    def outer(i, carry):
        for k in range(u):
            carry = body(i * u + k, carry)
        return carry

    return lax.fori_loop(0, nb // u, outer, init)


def _attn_masks(b, L, R, W):
    col = lax.broadcasted_iota(jnp.int32, (BQ, W), 1) + (b * BQ - R)
    return (col >= 0) & (col < L)


def _attn_fwd(q, kp, vp, bias, sink, R, pairs_per_kv, pairs_per_bias):
    N, L, _ = q.shape
    W = BQ + 2 * R
    nb = L // BQ

    def body(sink_ref, q_ref, k_ref, v_ref, bias_ref, o_ref, lse_ref):
        n = pl.program_id(0)
        lo_q = _lo_mask((BQ, PAIR))
        sk = (sink_ref[2 * n], sink_ref[2 * n + 1])

        def blk(b, carry):
            q0 = pl.multiple_of(b * BQ, BQ)
            qv = q_ref[pl.ds(q0, BQ), :]
            kw = k_ref[pl.ds(q0, W), :]
            vw = v_ref[pl.ds(q0, W), :]
            valid = _attn_masks(b, L, R, W)
            outs, lses = [], []
            for h in range(2):
                qh = jnp.where(lo_q, qv, jnp.zeros_like(qv)) if h == 0 else jnp.where(lo_q, jnp.zeros_like(qv), qv)
                s = jnp.where(valid, _dot_nt(qh, kw) + bias_ref[h], NEG)
                m = jnp.maximum(jnp.max(s, axis=1, keepdims=True), sk[h])
                p = jnp.exp(s - m)
                l = jnp.sum(p, axis=1, keepdims=True) + jnp.exp(sk[h] - m)
                outs.append(_dot(p.astype(BF16), vw) / l)
                lses.append(m + jnp.log(l))
            o_ref[pl.ds(q0, BQ), :] = jnp.where(lo_q, outs[0], outs[1])
            lse_ref[pl.ds(q0, BQ), :] = jnp.where(lo_q, lses[0], lses[1])
            return carry

        _loop_blocks(nb, blk, 0)

    qspec = pl.BlockSpec((None, L, PAIR), lambda n: (n, 0, 0))
    kspec = pl.BlockSpec((None, L + 2 * R, PAIR), lambda n: (n // pairs_per_kv, 0, 0))
    return pl.pallas_call(
        body, name="attn_fwd", grid=(N,),
        in_specs=[pl.BlockSpec(memory_space=pltpu.SMEM), qspec, kspec, kspec,
                  pl.BlockSpec((2, BQ, W), lambda n: (n // pairs_per_bias, 0, 0))],
        out_specs=[qspec, qspec],
        out_shape=[jax.ShapeDtypeStruct((N, L, PAIR), F32), jax.ShapeDtypeStruct((N, L, PAIR), F32)],
        compiler_params=_params("parallel"),
    )(sink, q, kp, vp, bias)


def _attn_bwd(q, kp, vp, bias, sink, o, lse, do, R, pairs_per_kv, pairs_per_bias):
    N, L, _ = q.shape
    Nk = kp.shape[0]
    Hb = bias.shape[0]
    W = BQ + 2 * R
    nb = L // BQ

    def body(sink_ref, q_ref, k_ref, v_ref, bias_ref, o_ref, lse_ref, do_ref,
             dq_ref, dk_ref, dv_ref, dbias_ref, dsink_ref):
        n = pl.program_id(0)
        lo_q = _lo_mask((BQ, PAIR))
        lo_w = _lo_mask((W, PAIR))
        sk = (sink_ref[2 * n], sink_ref[2 * n + 1])

        @pl.when(n % pairs_per_kv == 0)
        def _():
            dk_ref[...] = jnp.zeros_like(dk_ref)
            dv_ref[...] = jnp.zeros_like(dv_ref)

        @pl.when(n % pairs_per_bias == 0)
        def _():
            dbias_ref[...] = jnp.zeros_like(dbias_ref)

        def blk(b, dsk):
            q0 = pl.multiple_of(b * BQ, BQ)
            qv = q_ref[pl.ds(q0, BQ), :]
            kw = k_ref[pl.ds(q0, W), :]
            vw = v_ref[pl.ds(q0, W), :]
            dov = do_ref[pl.ds(q0, BQ), :]
            lsev = lse_ref[pl.ds(q0, BQ), :]
            delta2 = _half_sum(dov * o_ref[pl.ds(q0, BQ), :], lo_q)
            dob = dov.astype(BF16)
            valid = _attn_masks(b, L, R, W)
            zq = jnp.zeros_like(qv)
            zd = jnp.zeros_like(dob)
            dq_h, dk_h, dv_h, dsk_new = [], [], [], []
            for h in range(2):
                sel = lo_q if h == 0 else jnp.logical_not(lo_q)
                qh = jnp.where(sel, qv, zq)
                doh = jnp.where(sel, dob, zd)
                lse_h = jnp.max(jnp.where(sel, lsev, NEG), axis=1, keepdims=True)
                delta_h = jnp.max(jnp.where(sel, delta2, NEG), axis=1, keepdims=True)
                s = jnp.where(valid, _dot_nt(qh, kw) + bias_ref[h], NEG)
                p = jnp.exp(s - lse_h)
                dp = _dot_nt(doh, vw)
                ds = p * (dp - delta_h)
                dsb = ds.astype(BF16)
                dbias_ref[h] += ds
                dq_h.append(_dot(dsb, kw))
                dk_h.append(_dot_tn(dsb, qh))
                dv_h.append(_dot_tn(p.astype(BF16), doh))
                dsk_new.append(dsk[h] - jnp.exp(sk[h] - lse_h) * delta_h)
            dq_ref[pl.ds(q0, BQ), :] = jnp.where(lo_q, dq_h[0], dq_h[1])
            dk_ref[pl.ds(q0, W), :] += jnp.where(lo_w, dk_h[0], dk_h[1])
            dv_ref[pl.ds(q0, W), :] += jnp.where(lo_w, dv_h[0], dv_h[1])
            return tuple(dsk_new)

        zero = jnp.zeros((BQ, 1), F32)
        d0, d1 = _loop_blocks(nb, blk, (zero, zero))
        lane = lax.broadcasted_iota(jnp.int32, (8, PAIR), 1)
        dsink_ref[...] = jnp.where(lane < HEAD_DIM, jnp.sum(d0, axis=0, keepdims=True),
                                   jnp.sum(d1, axis=0, keepdims=True))

    qspec = pl.BlockSpec((None, L, PAIR), lambda n: (n, 0, 0))
    kspec = pl.BlockSpec((None, L + 2 * R, PAIR), lambda n: (n // pairs_per_kv, 0, 0))
    bspec = pl.BlockSpec((2, BQ, W), lambda n: (n // pairs_per_bias, 0, 0))
    return pl.pallas_call(
        body, name="attn_bwd", grid=(N,),
        in_specs=[pl.BlockSpec(memory_space=pltpu.SMEM), qspec, kspec, kspec, bspec, qspec, qspec, qspec],
        out_specs=[qspec, kspec, kspec, bspec, pl.BlockSpec((None, 8, PAIR), lambda n: (n, 0, 0))],
        out_shape=[jax.ShapeDtypeStruct((N, L, PAIR), F32),
                   jax.ShapeDtypeStruct((Nk, L + 2 * R, PAIR), F32),
                   jax.ShapeDtypeStruct((Nk, L + 2 * R, PAIR), F32),
                   jax.ShapeDtypeStruct((Hb, BQ, W), F32),
                   jax.ShapeDtypeStruct((N, 8, PAIR), F32)],
        compiler_params=_params("arbitrary"),
    )(sink, q, kp, vp, bias, o, lse, do)


def _attn_merge(o1, l1, o4, l4, o16, l16, ob, tm):
    T = o1.shape[1]

    def body(o1_ref, l1_ref, o4_ref, l4_ref, o16_ref, l16_ref, ob_ref, oa_ref, la_ref, cat_ref):
        for c in range(4):
            a, b, d = l1_ref[c], l4_ref[c], l16_ref[c]
            m = jnp.maximum(jnp.maximum(a, b), d)
            wa, wb, wd = jnp.exp(a - m), jnp.exp(b - m), jnp.exp(d - m)
            z = wa + wb + wd
            o = (wa * o1_ref[c] + wb * o4_ref[c] + wd * o16_ref[c]) / z
            oa_ref[c] = o
            la_ref[c] = m + jnp.log(z)
            cat_ref[:, c * PAIR:(c + 1) * PAIR] = o.astype(BF16)
            cat_ref[:, (4 + c) * PAIR:(5 + c) * PAIR] = ob_ref[c].astype(BF16)

    four = pl.BlockSpec((4, tm, PAIR), lambda i: (0, i, 0))
    s4 = jax.ShapeDtypeStruct((4, T, PAIR), F32)
    return pl.pallas_call(
        body, name="attn_merge", grid=(T // tm,),
        in_specs=[four] * 7,
        out_specs=[four, four, pl.BlockSpec((tm, 8 * PAIR), lambda i: (i, 0))],
        out_shape=[s4, s4, jax.ShapeDtypeStruct((T, 8 * PAIR), BF16)],
        compiler_params=_params("parallel"),
    )(o1, l1, o4, l4, o16, l16, ob)


def _weight_arg(w, blk):
    if blk is None:
        return pl.BlockSpec(w.shape, lambda i: (0, 0)), (lambda ref: ref[...])
    D = w.shape[2]
    return (pl.BlockSpec((N_DEV, 128, D), lambda i: (0, blk, 0)),
            lambda ref: ref[...].reshape(N_DEV * 128, D))


def _oproj_fwd(x, o_cat, w, blk, tm):
    T, D = x.shape
    wspec, wload = _weight_arg(w, blk)

    def body(x_ref, o_ref, w_ref, out_ref):
        out_ref[...] = x_ref[...] + _dot(o_ref[...], wload(w_ref))

    tok = pl.BlockSpec((tm, D), lambda i: (i, 0))
    return pl.pallas_call(
        body, name="oproj_fwd", grid=(T // tm,),
        in_specs=[tok, pl.BlockSpec((tm, o_cat.shape[1]), lambda i: (i, 0)), wspec],
        out_specs=tok, out_shape=jax.ShapeDtypeStruct((T, D), F32),
        compiler_params=_params("parallel"),
    )(x, o_cat, w)


def _oproj_bwd(dx, w, blk, tm):
    T, D = dx.shape
    wspec, wload = _weight_arg(w, blk)

    def body(dx_ref, w_ref, dxb_ref, do_ref):
        db = dx_ref[...].astype(BF16)
        dxb_ref[...] = db
        do = _dot_nt(db, wload(w_ref))
        for c in range(8):
            do_ref[c] = do[:, c * PAIR:(c + 1) * PAIR]

    tok = pl.BlockSpec((tm, D), lambda i: (i, 0))
    return pl.pallas_call(
        body, name="oproj_bwd", grid=(T // tm,),
        in_specs=[tok, wspec],
        out_specs=[tok, pl.BlockSpec((8, tm, PAIR), lambda i: (0, i, 0))],
        out_shape=[jax.ShapeDtypeStruct((T, D), BF16), jax.ShapeDtypeStruct((8, T, PAIR), F32)],
        compiler_params=_params("parallel"),
    )(dx, w)


def _attn_post(qkv, gains2, dqa, dka, dva, dqb, dkb, dvb, tm):
    T, NQ = qkv.shape
    scale = HEAD_DIM ** -0.5

    def body(qkv_ref, g_ref, qa1, qa4, qa16, ka1, ka4, ka16, va1, va4, va16, qb_ref, kb_ref, vb_ref,
             out_ref, dg_ref):
        lo = _lo_mask((tm, PAIR))

        @pl.when(pl.program_id(0) == 0)
        def _():
            dg_ref[...] = jnp.zeros_like(dg_ref)

        def norm_bwd(c, gi, dy):
            xv = qkv_ref[:, c * PAIR:(c + 1) * PAIR]
            r = lax.rsqrt(_half_sum(xv * xv, lo) * (1.0 / HEAD_DIM) + EPS)
            xn = xv * r
            dg_ref[gi:gi + 1, :] += jnp.sum(dy * xn, axis=0, keepdims=True)
            dxn = dy * g_ref[gi:gi + 1, :]
            dx = r * (dxn - xn * (_half_sum(dxn * xn, lo) * (1.0 / HEAD_DIM)))
            out_ref[:, c * PAIR:(c + 1) * PAIR] = dx.astype(BF16)

        def fold(v):
            return v + pltpu.roll(v, HEAD_DIM, 1)

        for c in range(4):
            norm_bwd(c, 0, (qa1[c] + qa4[c] + qa16[c]) * scale)
            norm_bwd(4 + c, 1, ka1[c] + ka4[c] + ka16[c])
            out_ref[:, (8 + c) * PAIR:(9 + c) * PAIR] = (va1[c] + va4[c] + va16[c]).astype(BF16)
            norm_bwd(12 + c, 2, qb_ref[c] * scale)
        norm_bwd(16, 3, jnp.where(lo, fold(kb_ref[0]), fold(kb_ref[1])))
        out_ref[:, 17 * PAIR:18 * PAIR] = jnp.where(lo, fold(vb_ref[0]), fold(vb_ref[1])).astype(BF16)

    four = pl.BlockSpec((4, tm, PAIR), lambda i: (0, i, 0))
    two = pl.BlockSpec((2, tm, PAIR), lambda i: (0, i, 0))
    return pl.pallas_call(
        body, name="attn_post", grid=(T // tm,),
        in_specs=[pl.BlockSpec((tm, NQ), lambda i: (i, 0)), pl.BlockSpec((4, PAIR), lambda i: (0, 0))]
        + [four] * 10 + [two, two],
        out_specs=[pl.BlockSpec((tm, NQ), lambda i: (i, 0)), pl.BlockSpec((4, PAIR), lambda i: (0, 0))],
        out_shape=[jax.ShapeDtypeStruct((T, NQ), BF16), jax.ShapeDtypeStruct((4, PAIR), F32)],
        compiler_params=_params("arbitrary"),
    )(qkv, gains2, *dqa, *dka, *dva, dqb, dkb, dvb)


def _dense_norm_bwd(dres, dz, w, blk, x, g, tm):
    T, D = x.shape
    N = dz.shape[1]
    wspec, wload = _weight_arg(w, blk)

    def body(dres_ref, dz_ref, w_ref, x_ref, g_ref, dx_ref, dgn_ref):
        i = pl.program_id(0)
        dx, dg = _norm_bwd(_dot_nt(dz_ref[...], wload(w_ref)), x_ref[...], g_ref[...])
        dx_ref[...] = dres_ref[...] + dx

        @pl.when(i == 0)
        def _():
            dgn_ref[...] = dg

        @pl.when(i > 0)
        def _():
            dgn_ref[...] += dg

    tok = pl.BlockSpec((tm, D), lambda i: (i, 0))
    row = pl.BlockSpec((1, D), lambda i: (0, 0))
    return pl.pallas_call(
        body, name="dense_norm_bwd", grid=(T // tm,),
        in_specs=[tok, pl.BlockSpec((tm, N), lambda i: (i, 0)), wspec, tok, row],
        out_specs=[tok, row],
        out_shape=[jax.ShapeDtypeStruct((T, D), F32), jax.ShapeDtypeStruct((1, D), F32)],
        compiler_params=_params("arbitrary"),
    )(dres, dz, w, x, g)


def _bias_reduce(onehot, dbm):
    Hb, K = dbm.shape

    def body(oh_ref, d_ref, out_ref):
        oh = oh_ref[...]
        d = d_ref[...]
        hi = d.astype(BF16)
        r1 = d - hi.astype(F32)
        mid = r1.astype(BF16)
        low = (r1 - mid.astype(F32)).astype(BF16)
        out_ref[...] = _dot_nt(hi, oh) + _dot_nt(mid, oh) + _dot_nt(low, oh)

    vm = pl.BlockSpec(memory_space=pltpu.VMEM)
    return pl.pallas_call(
        body, name="bias_reduce", in_specs=[vm, vm], out_specs=vm,
        out_shape=jax.ShapeDtypeStruct((Hb, 128), F32),
        compiler_params=pltpu.CompilerParams(vmem_limit_bytes=VMEM_LIMIT),
    )(onehot, dbm)


def _ple_fwd(x, g, wg, blk, p, wp, target, tm):
    T, D = x.shape
    P = p.shape[1]
    with_loss = target is not None
    wspec, wload = _weight_arg(wg, blk)

    def body(*refs):
        if with_loss:
            x_ref, g_ref, wg_ref, p_ref, wp_ref, t_ref, y_ref, hn_ref, gate_ref, pp_ref, pb_ref, loss_ref = refs
        else:
            x_ref, g_ref, wg_ref, p_ref, wp_ref, y_ref, hn_ref, gate_ref, pp_ref, pb_ref = refs
        i = pl.program_id(0)
        xv = x_ref[...]
        hb = (xv * _rstd(xv) * g_ref[...]).astype(BF16)
        hn_ref[...] = hb
        gate = _sigmoid(_dot(hb, wload(wg_ref)))
        pb = p_ref[...].astype(BF16)
        pb_ref[...] = pb
        pp = _dot(pb, wp_ref[...])
        gate_ref[...] = gate
        pp_ref[...] = pp
        y = xv + gate * pp
        if with_loss:
            err = y - t_ref[...]
            y_ref[...] = err * (1.0 / D)
            part = jnp.broadcast_to(0.5 * jnp.sum(jnp.sum(err * err, axis=1, keepdims=True) * (1.0 / D),
                                                  axis=0, keepdims=True), (1, 128))

            @pl.when(i == 0)
            def _():
                loss_ref[...] = part

            @pl.when(i > 0)
            def _():
                loss_ref[...] += part
        else:
            y_ref[...] = y

    tok = pl.BlockSpec((tm, D), lambda i: (i, 0))
    ptok = pl.BlockSpec((tm, P), lambda i: (i, 0))
    in_specs = [tok, pl.BlockSpec((1, D), lambda i: (0, 0)), wspec, ptok,
                pl.BlockSpec((P, D), lambda i: (0, 0))]
    out_specs = [tok, tok, tok, tok, ptok]
    out_shape = [jax.ShapeDtypeStruct((T, D), F32), jax.ShapeDtypeStruct((T, D), BF16),
                 jax.ShapeDtypeStruct((T, D), F32), jax.ShapeDtypeStruct((T, D), F32),
                 jax.ShapeDtypeStruct((T, P), BF16)]
    args = [x, g, wg, p, wp]
    if with_loss:
        in_specs.append(tok)
        out_specs.append(pl.BlockSpec((1, 128), lambda i: (0, 0)))
        out_shape.append(jax.ShapeDtypeStruct((1, 128), F32))
        args.append(target)
    return pl.pallas_call(
        body, name="ple_fwd_loss" if with_loss else "ple_fwd", grid=(T // tm,),
        in_specs=in_specs, out_specs=out_specs, out_shape=out_shape,
        compiler_params=_params("arbitrary" if with_loss else "parallel"),
    )(*args)


def _ple_bwd(dy, gate, pp, tm, dep=None):
    T, D = dy.shape

    def body(dy_ref, gate_ref, pp_ref, dgl_ref, dpp_ref):
        d = dy_ref[...]
        gt = gate_ref[...]
        dgl_ref[...] = (d * pp_ref[...] * gt * (1.0 - gt)).astype(BF16)
        dpp_ref[...] = (d * gt).astype(BF16)

    tok = pl.BlockSpec((tm, D), lambda i: (i, 0))
    body, in_specs, args = _with_dep(body, dep, [tok, tok, tok], [dy, gate, pp])
    return pl.pallas_call(
        body, name="ple_bwd", grid=(T // tm,), in_specs=in_specs, out_specs=[tok, tok],
        out_shape=[jax.ShapeDtypeStruct((T, D), BF16), jax.ShapeDtypeStruct((T, D), BF16)],
        compiler_params=_params("parallel"),
    )(*args)


def _adamw(w, g, m, v):
    shape = w.shape
    C = shape[-1]
    w2, g2, m2, v2 = (a.reshape(-1, C) for a in (w, g, m, v))
    Rn = w2.shape[0]
    tr = Rn
    for cand in (512, 352, 256):
        if Rn % cand == 0:
            tr = cand
            break
    c1 = 1.0 - ADAM_B1 ** ADAM_STEP
    c2 = 1.0 - ADAM_B2 ** ADAM_STEP

    def body(w_ref, g_ref, m_ref, v_ref, d_ref, nm_ref, nv_ref):
        gv = g_ref[...]
        mn = ADAM_B1 * m_ref[...] + (1.0 - ADAM_B1) * gv
        vn = ADAM_B2 * v_ref[...] + (1.0 - ADAM_B2) * (gv * gv)
        d_ref[...] = -ADAM_LR * ((mn / c1) / (jnp.sqrt(vn / c2) + ADAM_EPS) + ADAM_WD * w_ref[...])
        nm_ref[...] = mn
        nv_ref[...] = vn

    spec = pl.BlockSpec((tr, C), lambda i: (i, 0))
    sh = jax.ShapeDtypeStruct((Rn, C), F32)
    d, nm, nv = pl.pallas_call(
        body, name="adamw", grid=(Rn // tr,), in_specs=[spec] * 4, out_specs=[spec] * 3, out_shape=[sh] * 3,
        compiler_params=_params("parallel"),
    )(w2, g2, m2, v2)
    return d.reshape(shape), nm.reshape(shape), nv.reshape(shape)


def _my_place():
    x, y, c = lax.axis_index("x"), lax.axis_index("y"), lax.axis_index("c")
    chips = [(1 - x, y), (x, 1 - y), (1 - x, 1 - y)]
    return x, y, c, chips


def _all_gather(flat):
    R, Wd = flat.shape

    def body(x_ref, out_ref, send_sems, recv_sems, local_sem):
        x, y, c, chips = _my_place()
        me, sibling = (x, y, c), (x, y, 1 - c)

        def rows(px, py, pc):
            return out_ref.at[4 * px + 2 * py + pc]

        def copy(k, block, to, src=None):
            return pltpu.make_async_remote_copy(
                src_ref=rows(*block) if src is None else src, dst_ref=rows(*block),
                send_sem=send_sems.at[k], recv_sem=recv_sems.at[k], device_id=to, device_id_type=MESH)

        mine = pltpu.make_async_copy(x_ref, rows(*me), local_sem)
        mine.start()
        first = [copy(0, me, sibling, src=x_ref)]
        first += [copy(1 + j, me, (*chip, c), src=x_ref) for j, chip in enumerate(chips)]
        for cp in first:
            cp.start()
        passed = [copy(4 + j, (*chip, c), sibling) for j, chip in enumerate(chips)]
        for j, chip in enumerate(chips):
            copy(1 + j, (*chip, c), me).wait_recv()
            passed[j].start()
        copy(0, sibling, me).wait_recv()
        for j, chip in enumerate(chips):
            copy(4 + j, (*chip, 1 - c), me).wait_recv()
        for cp in first + passed:
            cp.wait_send()
        mine.wait()

    return pl.pallas_call(
        body, name="all_gather",
        in_specs=[pl.BlockSpec(memory_space=pl.ANY)], out_specs=pl.BlockSpec(memory_space=pl.ANY),
        out_shape=jax.ShapeDtypeStruct((N_DEV, R, Wd), flat.dtype),
        scratch_shapes=[pltpu.SemaphoreType.DMA((7,)), pltpu.SemaphoreType.DMA((7,)), pltpu.SemaphoreType.DMA],
    )(flat)


def _reduce_scatter(gparts, tr):
    _, R, Wd = gparts.shape
    nt = R // tr

    def body(g_ref, out_ref, a_ref, p_ref, b_ref, vb, vo_b, vo_f, d2d_send, d2d_recv, ici_send, ici_recv):
        x, y, c, chips = _my_place()
        sibling = (x, y, 1 - c)
        allchips = [(x, y)] + chips

        def dev(chip, pc):
            return 4 * chip[0] + 2 * chip[1] + pc

        d2d = [pltpu.make_async_remote_copy(
            src_ref=g_ref.at[dev(q, 1 - c)], dst_ref=a_ref.at[a], send_sem=d2d_send.at[a], recv_sem=d2d_recv.at[a],
            device_id=sibling, device_id_type=MESH) for a, q in enumerate(allchips)]
        for cp in d2d:
            cp.start()

        def add_tiles(srcs, dst, vo):
            def step(t, carry):
                r = pl.ds(pl.multiple_of(t * tr, tr), tr)
                acc = None
                for s_i, src in enumerate(srcs):
                    pltpu.sync_copy(src.at[r], vb.at[s_i])
                for s_i in range(len(srcs)):
                    term = vb[s_i].astype(F32)
                    acc = term if acc is None else acc + term
                vo[...] = acc.astype(vo.dtype)
                pltpu.sync_copy(vo, dst.at[r])
                return carry

            lax.fori_loop(0, nt, step, 0)

        ici = []
        for j, q in enumerate(chips):
            d2d[j + 1].wait_recv()
            add_tiles([g_ref.at[dev(q, c)], a_ref.at[j + 1]], p_ref.at[j], vo_b)
            cp = pltpu.make_async_remote_copy(
                src_ref=p_ref.at[j], dst_ref=b_ref.at[j], send_sem=ici_send.at[j], recv_sem=ici_recv.at[j],
                device_id=(*q, c), device_id_type=MESH)
            cp.start()
            ici.append(cp)
        d2d[0].wait_recv()
        for cp in ici:
            cp.wait_recv()
        add_tiles([g_ref.at[dev((x, y), c)], a_ref.at[0], b_ref.at[0], b_ref.at[1], b_ref.at[2]], out_ref, vo_f)
        for cp in d2d + ici:
            cp.wait_send()

    hbm = pl.BlockSpec(memory_space=pl.ANY)
    out, _, _, _ = pl.pallas_call(
        body, name="reduce_scatter",
        in_specs=[hbm], out_specs=[hbm, hbm, hbm, hbm],
        out_shape=[jax.ShapeDtypeStruct((R, Wd), F32), jax.ShapeDtypeStruct((4, R, Wd), BF16),
                   jax.ShapeDtypeStruct((3, R, Wd), BF16), jax.ShapeDtypeStruct((3, R, Wd), BF16)],
        scratch_shapes=[pltpu.VMEM((5, tr, Wd), BF16), pltpu.VMEM((tr, Wd), BF16), pltpu.VMEM((tr, Wd), F32),
                        pltpu.SemaphoreType.DMA((4,)), pltpu.SemaphoreType.DMA((4,)),
                        pltpu.SemaphoreType.DMA((3,)), pltpu.SemaphoreType.DMA((3,))],
        compiler_params=pltpu.CompilerParams(vmem_limit_bytes=VMEM_LIMIT),
    )(gparts)
    return out


def _peer(x, y, c, k):
    return (x ^ ((k >> 2) & 1), y ^ ((k >> 1) & 1), c ^ (k & 1))


HBM_SPEC = pl.BlockSpec(memory_space=pltpu.HBM)
SEM_SPEC = pl.BlockSpec(memory_space=pltpu.SEMAPHORE)


def _exchange_refs(srcs, lands, m, k, x, y, c, scatter):
    peer = _peer(x, y, c, k)
    if scatter:
        return srcs[m].at[4 * peer[0] + 2 * peer[1] + peer[2]], lands[m].at[k - 1], peer
    return srcs[m], lands[m].at[4 * x + 2 * y + c], peer


def _exchange_start(arrs, land_shapes, scatter, name):
    n = len(arrs)

    def body(*refs):
        srcs, lands = refs[:n], refs[n:2 * n]
        send_sems, recv_sems = refs[2 * n], refs[2 * n + 1]
        token = refs[-1]
        x, y, c, _ = _my_place()
        for m in range(n):
            for k in range(1, N_DEV):
                src, dst, peer = _exchange_refs(srcs, lands, m, k, x, y, c, scatter)
                pltpu.make_async_remote_copy(
                    src_ref=src, dst_ref=dst, send_sem=send_sems.at[7 * m + k - 1],
                    recv_sem=recv_sems.at[7 * m + k - 1], device_id=peer, device_id_type=MESH).start()
        token[...] = jnp.zeros_like(token)

    zones = [lax.empty(s_, a.dtype) for s_, a in zip(land_shapes, arrs)]
    outs = pl.pallas_call(
        body, name=name,
        out_shape=(pltpu.SemaphoreType.DMA((7 * n,)), pltpu.SemaphoreType.DMA((7 * n,)),
                   *[pltpu.HBM(a.shape, a.dtype) for a in arrs], *[pltpu.HBM(z.shape, z.dtype) for z in zones],
                   jax.ShapeDtypeStruct((8, 128), F32)),
        in_specs=[HBM_SPEC] * (2 * n),
        out_specs=(SEM_SPEC, SEM_SPEC, *[HBM_SPEC] * (2 * n), pl.BlockSpec(memory_space=pltpu.VMEM)),
        input_output_aliases={m: 2 + m for m in range(2 * n)},
        compiler_params=pltpu.CompilerParams(has_side_effects=pltpu.SideEffectType.DATAFLOW_SIDE_EFFECTING),
    )(*[pltpu.with_memory_space_constraint(a, pltpu.HBM) for a in arrs],
      *[pltpu.with_memory_space_constraint(z, pltpu.HBM) for z in zones])
    return outs[0], outs[1], list(outs[2:2 + n]), list(outs[2 + n:2 + 2 * n]), outs[-1]


def _exchange_wait(send_sems, recv_sems, arrs, zones, after, scatter, name):
    n = len(arrs)

    def body(*refs):
        srcs, lands = refs[:n], refs[n:2 * n]
        send_sems, recv_sems = refs[2 * n], refs[2 * n + 1]
        x, y, c, _ = _my_place()
        for m in range(n):
            for k in range(1, N_DEV):
                src, dst, peer = _exchange_refs(srcs, lands, m, k, x, y, c, scatter)
                cp = pltpu.make_async_remote_copy(
                    src_ref=src, dst_ref=dst, send_sem=send_sems.at[7 * m + k - 1],
                    recv_sem=recv_sems.at[7 * m + k - 1], device_id=peer, device_id_type=MESH)
                cp.wait_send()
                cp.wait_recv()

    outs = pl.pallas_call(
        body, name=name,
        out_shape=tuple(pltpu.HBM(a.shape, a.dtype) for a in list(arrs) + list(zones)),
        in_specs=[HBM_SPEC] * (2 * n) + [SEM_SPEC, SEM_SPEC, pl.BlockSpec(memory_space=pl.ANY)],
        out_specs=tuple([HBM_SPEC] * (2 * n)),
        input_output_aliases={m: m for m in range(2 * n)},
        compiler_params=pltpu.CompilerParams(has_side_effects=pltpu.SideEffectType.DATAFLOW_SIDE_EFFECTING),
    )(*arrs, *zones, send_sems, recv_sems, after)
    return list(outs[n:])


def _sum_parts(own, parts, tr):
    R, W = own.shape

    def body(own_ref, parts_ref, out_ref):
        acc = own_ref[...].astype(F32)
        for k in range(N_DEV - 1):
            acc = acc + parts_ref[k].astype(F32)
        out_ref[...] = acc

    return pl.pallas_call(
        body, name="sum_parts", grid=(R // tr,),
        in_specs=[pl.BlockSpec((tr, W), lambda i: (i, 0)), pl.BlockSpec((N_DEV - 1, tr, W), lambda i: (0, i, 0))],
        out_specs=pl.BlockSpec((tr, W), lambda i: (i, 0)),
        out_shape=jax.ShapeDtypeStruct((R, W), F32),
        compiler_params=_params("parallel"),
    )(own, parts)


def _all_reduce_small(v):
    Rn, Wd = v.shape

    def body(v_ref, out_ref, gat_ref, send_sems, recv_sems):
        x, y, c, _ = _my_place()
        me = 4 * x + 2 * y + c
        gat_ref[me] = v_ref[...]
        copies = []
        for k in range(1, N_DEV):
            fx, fy, fc = (k >> 2) & 1, (k >> 1) & 1, k & 1
            peer = (x ^ fx, y ^ fy, c ^ fc)
            cp = pltpu.make_async_remote_copy(
                src_ref=v_ref, dst_ref=gat_ref.at[me], send_sem=send_sems.at[k - 1], recv_sem=recv_sems.at[k - 1],
                device_id=peer, device_id_type=MESH)
            cp.start()
            copies.append(cp)
        for cp in copies:
            cp.wait_recv()
        for cp in copies:
            cp.wait_send()
        acc = gat_ref[0]
        for k in range(1, N_DEV):
            acc = acc + gat_ref[k]
        out_ref[...] = acc

    vm = pl.BlockSpec(memory_space=pltpu.VMEM)
    return pl.pallas_call(
        body, name="all_reduce_small", in_specs=[vm], out_specs=vm,
        out_shape=jax.ShapeDtypeStruct((Rn, Wd), F32),
        scratch_shapes=[pltpu.VMEM((N_DEV, Rn, Wd), F32), pltpu.SemaphoreType.DMA((7,)),
                        pltpu.SemaphoreType.DMA((7,))],
    )(v)


def _t5_bucket(rel):
    half = N_BUCKETS // 2
    max_exact = half // 2
    ret = jnp.where(rel > 0, half, 0)
    n = jnp.abs(rel)
    nf = jnp.maximum(n, 1).astype(F32)
    large = max_exact + (jnp.log(nf / max_exact) / math.log(MAX_DISTANCE / max_exact)
                         * (half - max_exact)).astype(jnp.int32)
    large = jnp.minimum(large, half - 1)
    return ret + jnp.where(n < max_exact, n, large)


def _band(R, d):
    W = BQ + 2 * R
    rel = jnp.arange(W)[None, :] - R - jnp.arange(BQ)[:, None]
    return _t5_bucket(rel * d), jnp.abs(rel) <= R


def _onehot(R, d):
    bkt, in_band = _band(R, d)
    return ((bkt.reshape(1, -1) == jnp.arange(128)[:, None]) & in_band.reshape(1, -1)).astype(BF16)


def _bias_expand(table_t, onehot):
    H = table_t.shape[0]
    K = onehot.shape[1]

    def body(t_ref, oh_ref, out_ref):
        oh = oh_ref[...]
        t = t_ref[...]
        hi = t.astype(BF16)
        r1 = t - hi.astype(F32)
        mid = r1.astype(BF16)
        low = (r1 - mid.astype(F32)).astype(BF16)
        marked = _dot(jnp.ones(t.shape, BF16), oh) > 0.5
        out_ref[...] = jnp.where(marked, _dot(hi, oh) + _dot(mid, oh) + _dot(low, oh), NEG)

    vm = pl.BlockSpec(memory_space=pltpu.VMEM)
    return pl.pallas_call(
        body, name="bias_expand", in_specs=[vm, vm], out_specs=vm,
        out_shape=jax.ShapeDtypeStruct((H, K), F32),
        compiler_params=pltpu.CompilerParams(vmem_limit_bytes=VMEM_LIMIT),
    )(table_t, onehot)


def _bias_matrix(table, R, d):
    table_t = jnp.pad(table.T, ((0, 0), (0, 128 - N_BUCKETS)))
    return _bias_expand(table_t, _onehot(R, d)).reshape(table.shape[1], BQ, BQ + 2 * R)


def _bias_grad(dbm, R, d):
    return _bias_reduce(_onehot(R, d), dbm.reshape(dbm.shape[0], -1))[:, :N_BUCKETS].T


def _deint(a, d):
    if d == 1:
        return a
    H, T, X = a.shape
    return a.reshape(H, T // d, d, X).transpose(0, 2, 1, 3).reshape(H * d, T // d, X)


def _reint(a, d):
    if d == 1:
        return a
    Hd, L, X = a.shape
    return a.reshape(Hd // d, d, L, X).transpose(0, 2, 1, 3).reshape(Hd // d, L * d, X)


def _pad_rows(a, R):
    return jnp.pad(a, ((0, 0), (R, R), (0, 0)))


def _tile2(gain):
    return jnp.concatenate([gain, gain])


ROW_W_O, ROW_GATE, ROW_QKV, ROW_PROJ, B_ROWS = 768, 896, 1024, 1312, 1344
BLK_W_O, BLK_GATE = ROW_W_O // 128, ROW_GATE // 128


def _pack_layer(wts, i):
    a = jnp.stack([wts["ffn1_w_in"][i], wts["ffn2_w_in"][i]])
    D = a.shape[1]
    b = jnp.concatenate([
        wts["ffn1_w_out"][i], wts["ffn2_w_out"][i],
        jnp.zeros((ROW_W_O - 2 * wts["ffn1_w_out"].shape[1], D), a.dtype),
        wts["w_o"][i], wts["w_ple_gate"][i], wts["w_qkv"][i].reshape(-1, D), wts["w_ple_proj"][i].reshape(-1, D)])
    return a, b


def _unpack_layer(ra, rb, like):
    n_out = like["ffn1_w_out"].shape[1]
    return {"ffn1_w_in": ra[0], "ffn2_w_in": ra[1], "ffn1_w_out": rb[:n_out], "ffn2_w_out": rb[n_out:2 * n_out],
            "w_o": rb[ROW_W_O:ROW_GATE], "w_ple_gate": rb[ROW_GATE:ROW_QKV],
            "w_qkv": rb[ROW_QKV:ROW_PROJ].reshape(like["w_qkv"].shape[1:]),
            "w_ple_proj": rb[ROW_PROJ:B_ROWS].reshape(like["w_ple_proj"].shape[1:])}


def _col_sharded(gb, r0, r1, rows):
    return gb[:, r0:r1].reshape(N_DEV, rows, -1).transpose(1, 0, 2).reshape(rows, -1)


def _to_col_shards(g):
    rows = g.shape[0]
    return g.reshape(rows, N_DEV, -1).transpose(1, 0, 2).reshape(N_DEV, -1, 1024)


def _layer_weights(ga, gb, p_dim):
    return dict(ga=ga, gb=gb, w_qkv=_col_sharded(gb, ROW_QKV, ROW_PROJ, ga.shape[2]),
                w_proj=_col_sharded(gb, ROW_PROJ, B_ROWS, p_dim))


def _layer_fwd(x, p, w, sm, i, target, tm, biases, dep=None):
    ga, gb = w["ga"], w["gb"]
    saved = {}
    saved["x0"] = x
    x1, saved["h1"], saved["zg1"], saved["zu1"], saved["s1"] = _ffn_fwd(
        x, sm["norm_ffn1"][i][None], ga, gb, 0, tm, dep)
    saved["x1"] = x1
    qkv, saved["hm"] = _qkv_fwd(x1, sm["norm_mix"][i][None], w["w_qkv"], tm)
    saved["qkv"] = qkv
    gains2 = jnp.stack([_tile2(sm[k][i]) for k in ("q_norm_a", "k_norm_a", "q_norm_b", "k_norm_b")])
    saved["gains2"] = gains2
    qa, ka, va, qb, kb, vb = _attn_prep(qkv, gains2, tm)
    no_sink = jnp.full((8,), NEG, F32)
    branches = []
    outs = []
    for (R, d), bias in zip(DILATED, biases[:3]):
        qd, kd, vd = _deint(qa, d), _pad_rows(_deint(ka, d), R), _pad_rows(_deint(va, d), R)
        sink = jnp.tile(no_sink, d)
        o, lse = _attn_fwd(qd, kd, vd, bias, sink, R, 1, d)
        branches.append((qd, kd, vd, bias, sink, R, d))
        outs += [_reint(o, d), _reint(lse, d)]
    bias_b = biases[3]
    kbp, vbp = _pad_rows(kb, SWA_RADIUS), _pad_rows(vb, SWA_RADIUS)
    sink_b = sm["sink_b"][i]
    ob, lb = _attn_fwd(qb, kbp, vbp, bias_b, sink_b, SWA_RADIUS, 2, 1)
    oa, la, o_cat = _attn_merge(*outs, ob, tm)
    saved.update(branches=branches, b=(qb, kbp, vbp, bias_b, sink_b), oa=oa, la=la, ob=ob, lb=lb, o_cat=o_cat)
    x2 = _oproj_fwd(x1, o_cat, gb, BLK_W_O, tm)
    saved["x2"] = x2
    x3, saved["h2"], saved["zg2"], saved["zu2"], saved["s2"] = _ffn_fwd(
        x2, sm["norm_ffn2"][i][None], ga, gb, 1, tm)
    saved["x3"] = x3
    res = _ple_fwd(x3, sm["norm_ple"][i][None], gb, BLK_GATE, p, w["w_proj"], target, tm)
    y, saved["hp"], saved["gate"], saved["pp"], saved["pb"] = res[:5]
    loss = res[5] if target is not None else None
    return y, loss, saved


def _layer_bwd(dy, w, sm, i, sv, tm, dep=None):
    ga, gb = w["ga"], w["gb"]
    gs = {}
    D = dy.shape[1]
    dgl, dpp = _ple_bwd(dy, sv["gate"], sv["pp"], tm, dep)
    d_gate = _matmul_tn(sv["hp"], dgl, D, tm)
    d_proj = _matmul_tn(sv["pb"], dpp, D, tm)
    dx3, gs["norm_ple"] = _dense_norm_bwd(dy, dgl, gb, BLK_GATE, sv["x3"], sm["norm_ple"][i][None], tm)
    dx2, dyb, dzg, dzu, gs["norm_ffn2"] = _ffn_bwd(dx3, sv["x2"], sm["norm_ffn2"][i][None], sv["zg2"], sv["zu2"],
                                                   ga, gb, 1, tm)
    dwg2, dwu2, dwo2 = _ffn_dw(sv["h2"], dzg, dzu, sv["s2"], dyb, tm)
    dx2b, do = _oproj_bwd(dx2, gb, BLK_W_O, tm)
    d_wo = _matmul_tn(sv["o_cat"], dx2b, D, tm)
    do_a, do_b = do[:4], do[4:]
    dqa, dka, dva = [], [], []
    drel_a = 0.0
    for qd, kd, vd, bias, sink, R, d in sv["branches"]:
        dq, dk, dv, dbm, _ = _attn_bwd(qd, kd, vd, bias, sink, _deint(sv["oa"], d), _deint(sv["la"], d),
                                        _deint(do_a, d), R, 1, d)
        L = qd.shape[1]
        dqa.append(_reint(dq, d))
        dka.append(_reint(dk[:, R:R + L], d))
        dva.append(_reint(dv[:, R:R + L], d))
        drel_a = drel_a + _bias_grad(dbm, R, d)
    qb, kbp, vbp, bias_b, sink_b = sv["b"]
    dqb, dkb, dvb, dbm_b, dsink = _attn_bwd(qb, kbp, vbp, bias_b, sink_b, sv["ob"], sv["lb"], do_b,
                                            SWA_RADIUS, 2, 1)
    T = qb.shape[1]
    drel_b = _bias_grad(dbm_b, SWA_RADIUS, 1)
    gs["rel_bias"] = jnp.concatenate([drel_a, drel_b], axis=1)
    gs["sink_b"] = jnp.stack([dsink[:, 0, 0], dsink[:, 0, HEAD_DIM]], axis=1).reshape(-1)
    dqkv, dgains2 = _attn_post(sv["qkv"], sv["gains2"], dqa, dka, dva, dqb,
                               dkb[:, SWA_RADIUS:SWA_RADIUS + T], dvb[:, SWA_RADIUS:SWA_RADIUS + T], tm // 2)
    dgains = dgains2[:, :HEAD_DIM] + dgains2[:, HEAD_DIM:]
    for k, name in enumerate(("q_norm_a", "k_norm_a", "q_norm_b", "k_norm_b")):
        gs[name] = dgains[k]
    d_qkv = _matmul_tn(sv["hm"], dqkv, dqkv.shape[1] // 2, tm)
    dx1, gs["norm_mix"] = _dense_norm_bwd(dx2, dqkv, w["w_qkv"], None, sv["x1"], sm["norm_mix"][i][None], tm)
    dx0, dyb, dzg, dzu, gs["norm_ffn1"] = _ffn_bwd(dx1, sv["x0"], sm["norm_ffn1"][i][None], sv["zg1"], sv["zu1"],
                                                   ga, gb, 0, tm)
    dwg1, dwu1, dwo1 = _ffn_dw(sv["h1"], dzg, dzu, sv["s1"], dyb, tm)
    da = jnp.stack([jnp.concatenate([dwg1, dwu1]), jnp.concatenate([dwg2, dwu2])], axis=1)
    half = dwo1.shape[1] // 2
    db = jnp.concatenate([
        dwo1.reshape(N_DEV, half, D), dwo2.reshape(N_DEV, half, D),
        jnp.zeros((N_DEV, ROW_W_O - 2 * half, D), BF16),
        d_wo.reshape(N_DEV, -1, D), d_gate.reshape(N_DEV, -1, D), _to_col_shards(d_qkv), _to_col_shards(d_proj)],
        axis=1)
    return dx0, (da, db), gs


def _bias_matrices(rel_bias):
    biases = [_bias_matrix(rel_bias[:, :8], R, d) for R, d in DILATED]
    biases.append(_bias_matrix(rel_bias[:, 8:], SWA_RADIUS, 1))
    return biases


def _stack_small(per_layer):
    small = {}
    for k, v in per_layer.items():
        if k == "rel_bias":
            small[k] = sum(v.values())
        else:
            small[k] = jnp.stack([v[i].reshape(-1) for i in sorted(v)])
    return small


TM = 512
RS_TILES = (512, 448)


def _pack_small(d, extra=None):
    parts = [d[k].reshape(-1) for k in SMALL]
    if extra is not None:
        parts.append(extra.reshape(-1))
    flat = jnp.concatenate(parts)
    return jnp.pad(flat, (0, SMALL_ROWS * 128 - flat.shape[0])).reshape(SMALL_ROWS, 128)


def _unpack_small(buf, like):
    flat = buf.reshape(-1)
    out, off = {}, 0
    for k in SMALL:
        n = like[k].size
        out[k] = flat[off:off + n].reshape(like[k].shape)
        off += n
    return out, flat[off]


def kernel(x, p, rel_bias, norm_ffn1, ffn1_w_in, ffn1_w_out, norm_mix, w_qkv, q_norm_a, k_norm_a, q_norm_b, k_norm_b, sink_b, w_o, norm_ffn2, ffn2_w_in, ffn2_w_out, norm_ple, w_ple_gate, w_ple_proj, loss_target, m_rel_bias, m_norm_ffn1, m_ffn1_w_in, m_ffn1_w_out, m_norm_mix, m_w_qkv, m_q_norm_a, m_k_norm_a, m_q_norm_b, m_k_norm_b, m_sink_b, m_w_o, m_norm_ffn2, m_ffn2_w_in, m_ffn2_w_out, m_norm_ple, m_w_ple_gate, m_w_ple_proj, v_rel_bias, v_norm_ffn1, v_ffn1_w_in, v_ffn1_w_out, v_norm_mix, v_w_qkv, v_q_norm_a, v_k_norm_a, v_q_norm_b, v_k_norm_b, v_sink_b, v_w_o, v_norm_ffn2, v_ffn2_w_in, v_ffn2_w_out, v_norm_ple, v_w_ple_gate, v_w_ple_proj):
    wts = dict(rel_bias=rel_bias, norm_ffn1=norm_ffn1, ffn1_w_in=ffn1_w_in, ffn1_w_out=ffn1_w_out,
               norm_mix=norm_mix, w_qkv=w_qkv, q_norm_a=q_norm_a, k_norm_a=k_norm_a, q_norm_b=q_norm_b,
               k_norm_b=k_norm_b, sink_b=sink_b, w_o=w_o, norm_ffn2=norm_ffn2, ffn2_w_in=ffn2_w_in,
               ffn2_w_out=ffn2_w_out, norm_ple=norm_ple, w_ple_gate=w_ple_gate, w_ple_proj=w_ple_proj)
    mom = dict(rel_bias=m_rel_bias, norm_ffn1=m_norm_ffn1, ffn1_w_in=m_ffn1_w_in, ffn1_w_out=m_ffn1_w_out,
               norm_mix=m_norm_mix, w_qkv=m_w_qkv, q_norm_a=m_q_norm_a, k_norm_a=m_k_norm_a, q_norm_b=m_q_norm_b,
               k_norm_b=m_k_norm_b, sink_b=m_sink_b, w_o=m_w_o, norm_ffn2=m_norm_ffn2, ffn2_w_in=m_ffn2_w_in,
               ffn2_w_out=m_ffn2_w_out, norm_ple=m_norm_ple, w_ple_gate=m_w_ple_gate, w_ple_proj=m_w_ple_proj)
    var = dict(rel_bias=v_rel_bias, norm_ffn1=v_norm_ffn1, ffn1_w_in=v_ffn1_w_in, ffn1_w_out=v_ffn1_w_out,
               norm_mix=v_norm_mix, w_qkv=v_w_qkv, q_norm_a=v_q_norm_a, k_norm_a=v_k_norm_a, q_norm_b=v_q_norm_b,
               k_norm_b=v_k_norm_b, sink_b=v_sink_b, w_o=v_w_o, norm_ffn2=v_norm_ffn2, ffn2_w_in=v_ffn2_w_in,
               ffn2_w_out=v_ffn2_w_out, norm_ple=v_norm_ple, w_ple_gate=v_w_ple_gate, w_ple_proj=v_w_ple_proj)
    sm = {k: wts[k] for k in SMALL}
    p_dim = p.shape[-1]
    me = 4 * lax.axis_index("x") + 2 * lax.axis_index("y") + lax.axis_index("c")
    packed = []
    for i in range(2):
        a, b = _pack_layer(wts, i)
        packed.append([a.reshape(-1, a.shape[-1]).astype(BF16), b.astype(BF16)])
    a_shape = (2, ffn1_w_in.shape[1], ffn1_w_in.shape[2])

    def weights_of(zones):
        return _layer_weights(zones[0].reshape((N_DEV,) + a_shape), zones[1], p_dim)

    w0 = weights_of([_all_gather(t) for t in packed[0]])
    zone_shapes = [(N_DEV,) + t.shape for t in packed[1]]
    ssem, rsem, thru, zones, token = _exchange_start(packed[1], zone_shapes, False, "gather_start")
    biases = _bias_matrices(rel_bias)
    x1, _, sv0 = _layer_fwd(x[0], p[0, 0], w0, sm, 0, None, TM, biases, dep=token)
    zones = _exchange_wait(ssem, rsem, thru, zones, x1, False, "gather_wait")
    w1 = weights_of([lax.dynamic_update_index_in_dim(z, t, me, 0) for z, t in zip(zones, packed[1])])
    dy, loss, sv1 = _layer_fwd(x1, p[1, 0], w1, sm, 1, loss_target[0], TM, biases)

    dx1, g1, gs1 = _layer_bwd(dy, w1, sm, 1, sv1, TM)
    g1 = [g1[0].reshape(N_DEV, -1, g1[0].shape[-1]), g1[1]]
    slot_shapes = [(N_DEV - 1,) + t.shape[1:] for t in g1]
    ssem, rsem, thru, slots, token = _exchange_start(g1, slot_shapes, True, "scatter_start")
    dx, g0, gs0 = _layer_bwd(dx1, w0, sm, 0, sv0, TM, dep=token)
    slots = _exchange_wait(ssem, rsem, thru, slots, dx, True, "scatter_wait")
    r1 = [_sum_parts(lax.dynamic_index_in_dim(t, me, 0, keepdims=False), s_, tr)
          for t, s_, tr in zip(g1, slots, RS_TILES)]
    g0 = [g0[0].reshape(N_DEV, -1, g0[0].shape[-1]), g0[1]]
    r0 = [_reduce_scatter(t, tr) for t, tr in zip(g0, RS_TILES)]

    gsmall = _stack_small({k: {0: gs0[k], 1: gs1[k]} for k in gs0})
    small_sum, loss_sum = _unpack_small(_all_reduce_small(_pack_small(gsmall, loss[0, :1])), sm)

    grads = dict(small_sum)
    layers = [_unpack_layer(r[0].reshape(a_shape), r[1], wts) for r in (r0, r1)]
    for k in BIG:
        grads[k] = jnp.stack([layers[0][k], layers[1][k]])

    delta, new_m, new_v = {}, {}, {}
    for k in BIG:
        delta[k], new_m[k], new_v[k] = _adamw(wts[k], grads[k], mom[k], var[k])
    zeros = {k: jnp.zeros_like(wts[k]) for k in SMALL}
    ds, ms, vs = _adamw(_pack_small(wts), _pack_small(small_sum), _pack_small(mom), _pack_small(var))
    for packed, dst in ((ds, delta), (ms, new_m), (vs, new_v)):
        dst.update(_unpack_small(packed, zeros)[0])

    return (loss_sum, dx[None], *[grads[k] for k in WEIGHTS], *[delta[k] for k in WEIGHTS],
            *[new_m[k] for k in WEIGHTS], *[new_v[k] for k in WEIGHTS])
```

```python
import functools
import math

import jax
import jax.numpy as jnp
from jax import lax
from jax.experimental import pallas as pl
from jax.experimental.pallas import tpu as pltpu

F32 = jnp.float32
BF16 = jnp.bfloat16

N_DEV = 8
HEAD_DIM = 64
PAIR = 2 * HEAD_DIM
BQ = 128
N_BUCKETS = 32
MAX_DISTANCE = 1024
DILATED = ((64, 1), (64, 4), (64, 16))
SWA_RADIUS = 128
EPS = 1e-6
NEG = -1e30
ADAM_LR, ADAM_B1, ADAM_B2, ADAM_EPS, ADAM_WD, ADAM_STEP = 0.001, 0.9, 0.999, 1e-08, 0.01, 10
VMEM_LIMIT = 56 * 1024 * 1024
AXES = ("x", "y", "c")
MESH = pl.DeviceIdType.MESH

BIG = ("ffn1_w_in", "ffn1_w_out", "w_qkv", "w_o", "ffn2_w_in", "ffn2_w_out", "w_ple_gate", "w_ple_proj")
SMALL = ("rel_bias", "norm_ffn1", "norm_mix", "q_norm_a", "k_norm_a", "q_norm_b", "k_norm_b", "sink_b",
         "norm_ffn2", "norm_ple")
WEIGHTS = ("rel_bias", "norm_ffn1", "ffn1_w_in", "ffn1_w_out", "norm_mix", "w_qkv", "q_norm_a", "k_norm_a",
           "q_norm_b", "k_norm_b", "sink_b", "w_o", "norm_ffn2", "ffn2_w_in", "ffn2_w_out", "norm_ple",
           "w_ple_gate", "w_ple_proj")
SMALL_ROWS = 96


def _params(*sem):
    return pltpu.CompilerParams(dimension_semantics=sem, vmem_limit_bytes=VMEM_LIMIT)


def _dot(a, b):
    return jnp.dot(a, b, preferred_element_type=F32)


def _dot_nt(a, b):
    return lax.dot_general(a, b, (((1,), (1,)), ((), ())), preferred_element_type=F32)


def _dot_tn(a, b):
    return lax.dot_general(a, b, (((0,), (0,)), ((), ())), preferred_element_type=F32)


def _sigmoid(x):
    return 1.0 / (1.0 + jnp.exp(-x))


def _rstd(xv):
    return lax.rsqrt(jnp.mean(xv * xv, axis=-1, keepdims=True) + EPS)


def _norm_bwd(dh, xv, gv):
    r = _rstd(xv)
    xn = xv * r
    dg = jnp.sum(dh * xn, axis=0, keepdims=True)
    dxn = dh * gv
    dx = r * (dxn - xn * jnp.mean(dxn * xn, axis=-1, keepdims=True))
    return dx, dg


def _lo_mask(shape):
    return lax.broadcasted_iota(jnp.int32, shape, len(shape) - 1) < HEAD_DIM


def _half_sum(t, lo):
    s0 = jnp.sum(jnp.where(lo, t, 0.0), axis=1, keepdims=True)
    s1 = jnp.sum(jnp.where(lo, 0.0, t), axis=1, keepdims=True)
    return jnp.where(lo, s0, s1)


def _ffn_weight_specs(f, nj, D, C):
    return [pl.BlockSpec((None, None, D, C), lambda i, j: (j, f, 0, 0)),
            pl.BlockSpec((None, None, D, C), lambda i, j: (j + nj, f, 0, 0)),
            pl.BlockSpec((2, C // 2, D), lambda i, j: (j, f, 0))]


def _with_dep(body, dep, in_specs, args):
    if dep is None:
        return body, in_specs, args

    def body_after(dep_ref, *refs):
        body(*refs)

    return body_after, [pl.BlockSpec(memory_space=pl.ANY)] + in_specs, [dep] + args


def _ffn_fwd(x, g, ga, gb, f, tm, dep=None):
    T, D = x.shape
    nj, C = ga.shape[0] // 2, ga.shape[3]

    def body(x_ref, g_ref, wg_ref, wu_ref, wo_ref, xo_ref, h_ref, zg_ref, zu_ref, s_ref, h_scr, acc):
        j = pl.program_id(1)

        @pl.when(j == 0)
        def _():
            xv = x_ref[...]
            hb = (xv * _rstd(xv) * g_ref[...]).astype(BF16)
            h_scr[...] = hb
            h_ref[...] = hb
            acc[...] = jnp.zeros_like(acc)

        hb = h_scr[...]
        gt = _dot(hb, wg_ref[...])
        up = _dot(hb, wu_ref[...])
        s = (gt * _sigmoid(gt) * up).astype(BF16)
        zg_ref[...] = gt.astype(BF16)
        zu_ref[...] = up.astype(BF16)
        s_ref[...] = s
        acc[...] += _dot(s, wo_ref[...].reshape(C, D))

        @pl.when(j == nj - 1)
        def _():
            xo_ref[...] = x_ref[...] + 0.5 * acc[...]

    tok = pl.BlockSpec((tm, D), lambda i, j: (i, 0))
    chunk = pl.BlockSpec((None, tm, C), lambda i, j: (j, i, 0))
    in_specs = [tok, pl.BlockSpec((1, D), lambda i, j: (0, 0))] + _ffn_weight_specs(f, nj, D, C)
    body, in_specs, args = _with_dep(body, dep, in_specs, [x, g, ga, ga, gb])
    return pl.pallas_call(
        body, name="ffn_fwd", grid=(T // tm, nj),
        in_specs=in_specs,
        out_specs=[tok, tok, chunk, chunk, chunk],
        out_shape=[jax.ShapeDtypeStruct((T, D), F32), jax.ShapeDtypeStruct((T, D), BF16),
                   jax.ShapeDtypeStruct((nj, T, C), BF16), jax.ShapeDtypeStruct((nj, T, C), BF16),
                   jax.ShapeDtypeStruct((nj, T, C), BF16)],
        scratch_shapes=[pltpu.VMEM((tm, D), BF16), pltpu.VMEM((tm, D), F32)],
        compiler_params=_params("parallel", "arbitrary"),
    )(*args)


def _ffn_bwd(dxo, x, g, zg, zu, ga, gb, f, tm):
    T, D = x.shape
    nj, C = ga.shape[0] // 2, ga.shape[3]

    def body(dxo_ref, x_ref, g_ref, zg_ref, zu_ref, wg_ref, wu_ref, wo_ref,
             dx_ref, dy_ref, dzg_ref, dzu_ref, dgn_ref, dy_scr, acc):
        i, j = pl.program_id(0), pl.program_id(1)

        @pl.when(j == 0)
        def _():
            dyb = (0.5 * dxo_ref[...]).astype(BF16)
            dy_scr[...] = dyb
            dy_ref[...] = dyb
            acc[...] = jnp.zeros_like(acc)

        ds = _dot_nt(dy_scr[...], wo_ref[...].reshape(C, D))
        gt = zg_ref[...].astype(F32)
        up = zu_ref[...].astype(F32)
        sg = _sigmoid(gt)
        dgt = (ds * up * (sg * (1.0 + gt * (1.0 - sg)))).astype(BF16)
        dup = (ds * (gt * sg)).astype(BF16)
        dzg_ref[...] = dgt
        dzu_ref[...] = dup
        acc[...] += _dot_nt(dgt, wg_ref[...]) + _dot_nt(dup, wu_ref[...])

        @pl.when(j == nj - 1)
        def _():
            dx, dg = _norm_bwd(acc[...], x_ref[...], g_ref[...])
            dx_ref[...] = dxo_ref[...] + dx

            @pl.when(i == 0)
            def _():
                dgn_ref[...] = dg

            @pl.when(i > 0)
            def _():
                dgn_ref[...] += dg

    tok = pl.BlockSpec((tm, D), lambda i, j: (i, 0))
    chunk = pl.BlockSpec((None, tm, C), lambda i, j: (j, i, 0))
    row = pl.BlockSpec((1, D), lambda i, j: (0, 0))
    return pl.pallas_call(
        body, name="ffn_bwd", grid=(T // tm, nj),
        in_specs=[tok, tok, row, chunk, chunk] + _ffn_weight_specs(f, nj, D, C),
        out_specs=[tok, tok, chunk, chunk, row],
        out_shape=[jax.ShapeDtypeStruct((T, D), F32), jax.ShapeDtypeStruct((T, D), BF16),
                   jax.ShapeDtypeStruct((nj, T, C), BF16), jax.ShapeDtypeStruct((nj, T, C), BF16),
                   jax.ShapeDtypeStruct((1, D), F32)],
        scratch_shapes=[pltpu.VMEM((tm, D), BF16), pltpu.VMEM((tm, D), F32)],
        compiler_params=_params("arbitrary", "arbitrary"),
    )(dxo, x, g, zg, zu, ga, ga, gb)


def _ffn_dw(h, dzg, dzu, s, dy, tk):
    T, D = h.shape
    nj, C = s.shape[0], s.shape[2]
    nk = T // tk

    def body(h_ref, dzg_ref, dzu_ref, s_ref, dy_ref, dwg_ref, dwu_ref, dwo_ref, ag, au, ao):
        k = pl.program_id(1)

        @pl.when(k == 0)
        def _():
            ag[...] = jnp.zeros_like(ag)
            au[...] = jnp.zeros_like(au)
            ao[...] = jnp.zeros_like(ao)

        hb = h_ref[...]
        ag[...] += _dot_tn(hb, dzg_ref[...])
        au[...] += _dot_tn(hb, dzu_ref[...])
        ao[...] += _dot_tn(s_ref[...], dy_ref[...])

        @pl.when(k == nk - 1)
        def _():
            dwg_ref[...] = ag[...].astype(BF16)
            dwu_ref[...] = au[...].astype(BF16)
            dwo_ref[...] = ao[...].astype(BF16)

    tok = pl.BlockSpec((tk, D), lambda j, k: (k, 0))
    chunk = pl.BlockSpec((None, tk, C), lambda j, k: (j, k, 0))
    return pl.pallas_call(
        body, name="ffn_dw", grid=(nj, nk),
        in_specs=[tok, chunk, chunk, chunk, tok],
        out_specs=[pl.BlockSpec((None, D, C), lambda j, k: (j, 0, 0)),
                   pl.BlockSpec((None, D, C), lambda j, k: (j, 0, 0)),
                   pl.BlockSpec((None, C, D), lambda j, k: (j, 0, 0))],
        out_shape=[jax.ShapeDtypeStruct((nj, D, C), BF16), jax.ShapeDtypeStruct((nj, D, C), BF16),
                   jax.ShapeDtypeStruct((nj, C, D), BF16)],
        scratch_shapes=[pltpu.VMEM((D, C), F32), pltpu.VMEM((D, C), F32), pltpu.VMEM((C, D), F32)],
        compiler_params=_params("parallel", "arbitrary"),
    )(h, dzg, dzu, s, dy)


def _matmul_tn(a, b, tn, tk):
    T, Ka = a.shape
    N = b.shape[1]
    nk = T // tk

    def body(a_ref, b_ref, o_ref, acc):
        k = pl.program_id(1)

        @pl.when(k == 0)
        def _():
            acc[...] = jnp.zeros_like(acc)

        acc[...] += _dot_tn(a_ref[...], b_ref[...])

        @pl.when(k == nk - 1)
        def _():
            o_ref[...] = acc[...].astype(BF16)

    return pl.pallas_call(
        body, name="matmul_tn", grid=(N // tn, nk),
        in_specs=[pl.BlockSpec((tk, Ka), lambda n, k: (k, 0)), pl.BlockSpec((tk, tn), lambda n, k: (k, n))],
        out_specs=pl.BlockSpec((Ka, tn), lambda n, k: (0, n)),
        out_shape=jax.ShapeDtypeStruct((Ka, N), BF16),
        scratch_shapes=[pltpu.VMEM((Ka, tn), F32)],
        compiler_params=_params("parallel", "arbitrary"),
    )(a, b)


def _qkv_fwd(x, g, w, tm):
    T, D = x.shape
    N = w.shape[1]

    def body(x_ref, g_ref, w_ref, o_ref, h_ref):
        xv = x_ref[...]
        hb = (xv * _rstd(xv) * g_ref[...]).astype(BF16)
        h_ref[...] = hb
        o_ref[...] = _dot(hb, w_ref[...])

    return pl.pallas_call(
        body, name="qkv_fwd", grid=(T // tm,),
        in_specs=[pl.BlockSpec((tm, D), lambda i: (i, 0)), pl.BlockSpec((1, D), lambda i: (0, 0)),
                  pl.BlockSpec((D, N), lambda i: (0, 0))],
        out_specs=[pl.BlockSpec((tm, N), lambda i: (i, 0)), pl.BlockSpec((tm, D), lambda i: (i, 0))],
        out_shape=[jax.ShapeDtypeStruct((T, N), F32), jax.ShapeDtypeStruct((T, D), BF16)],
        compiler_params=_params("parallel"),
    )(x, g, w)


def _attn_prep(qkv, gains2, tm):
    T = qkv.shape[0]
    scale = HEAD_DIM ** -0.5

    def body(qkv_ref, g_ref, qa_ref, ka_ref, va_ref, qb_ref, kb_ref, vb_ref):
        lo = _lo_mask((tm, PAIR))

        def normed(c, gi, mult):
            xv = qkv_ref[:, c * PAIR:(c + 1) * PAIR]
            r = lax.rsqrt(_half_sum(xv * xv, lo) * (1.0 / HEAD_DIM) + EPS)
            y = xv * r * g_ref[gi:gi + 1, :]
            return y * mult if mult != 1.0 else y

        def both_halves(v):
            sw = pltpu.roll(v, HEAD_DIM, 1)
            return jnp.where(lo, v, sw), jnp.where(lo, sw, v)

        for c in range(4):
            qa_ref[c] = normed(c, 0, scale).astype(BF16)
            ka_ref[c] = normed(4 + c, 1, 1.0).astype(BF16)
            va_ref[c] = qkv_ref[:, (8 + c) * PAIR:(9 + c) * PAIR].astype(BF16)
            qb_ref[c] = normed(12 + c, 2, scale).astype(BF16)
        k0, k1 = both_halves(normed(16, 3, 1.0))
        kb_ref[0] = k0.astype(BF16)
        kb_ref[1] = k1.astype(BF16)
        v0, v1 = both_halves(qkv_ref[:, 17 * PAIR:18 * PAIR])
        vb_ref[0] = v0.astype(BF16)
        vb_ref[1] = v1.astype(BF16)

    four = pl.BlockSpec((4, tm, PAIR), lambda i: (0, i, 0))
    two = pl.BlockSpec((2, tm, PAIR), lambda i: (0, i, 0))
    s4 = jax.ShapeDtypeStruct((4, T, PAIR), BF16)
    s2 = jax.ShapeDtypeStruct((2, T, PAIR), BF16)
    return pl.pallas_call(
        body, name="attn_prep", grid=(T // tm,),
        in_specs=[pl.BlockSpec((tm, qkv.shape[1]), lambda i: (i, 0)), pl.BlockSpec((4, PAIR), lambda i: (0, 0))],
        out_specs=[four, four, four, four, two, two],
        out_shape=[s4, s4, s4, s4, s2, s2],
        compiler_params=_params("parallel"),
    )(qkv, gains2)


def _loop_blocks(nb, body, init, per_iter):
    u = math.gcd(nb, per_iter)

    def outer(i, carry):
        for k in range(u):
            carry = body(i * u + k, carry)
        return carry

    return lax.fori_loop(0, nb // u, outer, init)


def _edge_variant(b, nb):
    return (b == 0).astype(jnp.int32) + 2 * (b == nb - 1).astype(jnp.int32)


def _stack_heads(v, lo):
    z = jnp.zeros_like(v)
    return jnp.concatenate([jnp.where(lo, v, z), jnp.where(lo, z, v)], axis=0)


def _unstack_heads(v2, lo):
    return jnp.where(lo, v2[:BQ], v2[BQ:])


def _row_vector(v, lo):
    r = lax.broadcasted_iota(jnp.int32, (BQ, PAIR), 0)
    ln = lax.broadcasted_iota(jnp.int32, (BQ, PAIR), 1)
    diag = (ln % HEAD_DIM) == (r % HEAD_DIM)
    top = jnp.sum(jnp.where(diag & (r < HEAD_DIM), v, 0.0), axis=0, keepdims=True)
    bot = jnp.sum(jnp.where(diag & (r >= HEAD_DIM), v, 0.0), axis=0, keepdims=True)
    top8, bot8 = jnp.broadcast_to(top, (8, PAIR)), jnp.broadcast_to(bot, (8, PAIR))
    lo8 = _lo_mask((8, PAIR))
    head0 = jnp.where(lo8, top8, pltpu.roll(bot8, HEAD_DIM, 1))
    head1 = jnp.where(lo8, pltpu.roll(top8, HEAD_DIM, 1), bot8)
    return jnp.concatenate([head0, head1], axis=1)[:1]


def _attn_fwd(q, kp, vp, bias4, sink, R, pairs_per_kv, pairs_per_bias):
    N, L, _ = q.shape
    W = BQ + 2 * R
    nb = L // BQ

    def body(sink_ref, q_ref, k_ref, v_ref, bias_ref, o_ref, lse_ref):
        n = pl.program_id(0)
        lo_q = _lo_mask((BQ, PAIR))
        first = lax.broadcasted_iota(jnp.int32, (2 * BQ, 1), 0) < BQ
        sk = jnp.where(first, sink_ref[2 * n], sink_ref[2 * n + 1])

        def blk(b, carry):
            q0 = pl.multiple_of(b * BQ, BQ)
            q2 = _stack_heads(q_ref[pl.ds(q0, BQ), :], lo_q)
            kw = k_ref[pl.ds(q0, W), :]
            vw = v_ref[pl.ds(q0, W), :]
            s = _dot_nt(q2, kw) + bias_ref[_edge_variant(b, nb)]
            m = jnp.maximum(jnp.max(s, axis=1, keepdims=True), sk)
            p = jnp.exp(s - m)
            l = jnp.sum(p, axis=1, keepdims=True) + jnp.exp(sk - m)
            o2 = _dot(p.astype(BF16), vw) / l
            o_ref[pl.ds(q0, BQ), :] = _unstack_heads(o2, lo_q)
            lse_ref[pl.ds(q0, BQ), :] = _unstack_heads(jnp.broadcast_to(m + jnp.log(l), (2 * BQ, PAIR)), lo_q)
            return carry

        _loop_blocks(nb, blk, 0, 4)

    qspec = pl.BlockSpec((None, L, PAIR), lambda n: (n, 0, 0))
    kspec = pl.BlockSpec((None, L + 2 * R, PAIR), lambda n: (n // pairs_per_kv, 0, 0))
    return pl.pallas_call(
        body, name="attn_fwd", grid=(N,),
        in_specs=[pl.BlockSpec(memory_space=pltpu.SMEM), qspec, kspec, kspec,
                  pl.BlockSpec((None, 4, 2 * BQ, W), lambda n: (n // pairs_per_bias, 0, 0, 0))],
        out_specs=[qspec, qspec],
        out_shape=[jax.ShapeDtypeStruct((N, L, PAIR), F32), jax.ShapeDtypeStruct((N, L, PAIR), F32)],
        compiler_params=_params("parallel"),
    )(sink, q, kp, vp, bias4)


def _attn_bwd(q, kp, vp, bias4t, sink, o, lse, do, R, pairs_per_kv, pairs_per_bias):
    N, L, _ = q.shape
    Nk = kp.shape[0]
    Pb = bias4t.shape[0]
    W = BQ + 2 * R
    nb = L // BQ

    def body(sink_ref, q_ref, k_ref, v_ref, bias_ref, o_ref, lse_ref, do_ref,
             dq_ref, dk_ref, dv_ref, dbias_ref, dsink_ref):
        n = pl.program_id(0)
        lo_q = _lo_mask((BQ, PAIR))
        first = lax.broadcasted_iota(jnp.int32, (1, 2 * BQ), 1) < BQ
        sk = jnp.where(first, sink_ref[2 * n], sink_ref[2 * n + 1])

        @pl.when(n % pairs_per_kv == 0)
        def _():
            dk_ref[...] = jnp.zeros_like(dk_ref)
            dv_ref[...] = jnp.zeros_like(dv_ref)

        @pl.when(n % pairs_per_bias == 0)
        def _():
            dbias_ref[...] = jnp.zeros_like(dbias_ref)

        def blk(b, dsk):
            q0 = pl.multiple_of(b * BQ, BQ)
            q2 = _stack_heads(q_ref[pl.ds(q0, BQ), :], lo_q)
            kw = k_ref[pl.ds(q0, W), :]
            vw = v_ref[pl.ds(q0, W), :]
            dov = do_ref[pl.ds(q0, BQ), :]
            lse = _row_vector(lse_ref[pl.ds(q0, BQ), :], lo_q)
            delta = _row_vector(_half_sum(dov * o_ref[pl.ds(q0, BQ), :], lo_q), lo_q)
            do2 = _stack_heads(dov.astype(BF16), lo_q)
            st = _dot_nt(kw, q2) + bias_ref[_edge_variant(b, nb)]
            pt = jnp.exp(st - lse)
            dst = pt * (_dot_nt(vw, do2) - delta)
            dstb = dst.astype(BF16)
            dbias_ref[...] += dst
            dk_ref[pl.ds(q0, W), :] += _dot(dstb, q2)
            dv_ref[pl.ds(q0, W), :] += _dot(pt.astype(BF16), do2)
            dq_ref[pl.ds(q0, BQ), :] = _unstack_heads(_dot_tn(dstb, kw), lo_q)
            return dsk - jnp.exp(sk - lse) * delta

        dsk = _loop_blocks(nb, blk, jnp.zeros((1, 2 * BQ), F32), 4)
        dsink_ref[...] = jnp.broadcast_to(dsk, (8, 2 * BQ))

    qspec = pl.BlockSpec((None, L, PAIR), lambda n: (n, 0, 0))
    kspec = pl.BlockSpec((None, L + 2 * R, PAIR), lambda n: (n // pairs_per_kv, 0, 0))
    return pl.pallas_call(
        body, name="attn_bwd", grid=(N,),
        in_specs=[pl.BlockSpec(memory_space=pltpu.SMEM), qspec, kspec, kspec,
                  pl.BlockSpec((None, 4, W, 2 * BQ), lambda n: (n // pairs_per_bias, 0, 0, 0)),
                  qspec, qspec, qspec],
        out_specs=[qspec, kspec, kspec, pl.BlockSpec((None, W, 2 * BQ), lambda n: (n // pairs_per_bias, 0, 0)),
                   pl.BlockSpec((None, 8, 2 * BQ), lambda n: (n, 0, 0))],
        out_shape=[jax.ShapeDtypeStruct((N, L, PAIR), F32),
                   jax.ShapeDtypeStruct((Nk, L + 2 * R, PAIR), F32),
                   jax.ShapeDtypeStruct((Nk, L + 2 * R, PAIR), F32),
                   jax.ShapeDtypeStruct((Pb, W, 2 * BQ), F32),
                   jax.ShapeDtypeStruct((N, 8, 2 * BQ), F32)],
        compiler_params=_params("arbitrary"),
    )(sink, q, kp, vp, bias4t, o, lse, do)


def _attn_merge(o1, l1, o4, l4, o16, l16, ob, tm):
    T = o1.shape[1]

    def body(o1_ref, l1_ref, o4_ref, l4_ref, o16_ref, l16_ref, ob_ref, oa_ref, la_ref, cat_ref):
        for c in range(4):
            a, b, d = l1_ref[c], l4_ref[c], l16_ref[c]
            m = jnp.maximum(jnp.maximum(a, b), d)
            wa, wb, wd = jnp.exp(a - m), jnp.exp(b - m), jnp.exp(d - m)
            z = wa + wb + wd
            o = (wa * o1_ref[c] + wb * o4_ref[c] + wd * o16_ref[c]) / z
            oa_ref[c] = o
            la_ref[c] = m + jnp.log(z)
            cat_ref[:, c * PAIR:(c + 1) * PAIR] = o.astype(BF16)
            cat_ref[:, (4 + c) * PAIR:(5 + c) * PAIR] = ob_ref[c].astype(BF16)

    four = pl.BlockSpec((4, tm, PAIR), lambda i: (0, i, 0))
    s4 = jax.ShapeDtypeStruct((4, T, PAIR), F32)
    return pl.pallas_call(
        body, name="attn_merge", grid=(T // tm,),
        in_specs=[four] * 7,
        out_specs=[four, four, pl.BlockSpec((tm, 8 * PAIR), lambda i: (i, 0))],
        out_shape=[s4, s4, jax.ShapeDtypeStruct((T, 8 * PAIR), BF16)],
        compiler_params=_params("parallel"),
    )(o1, l1, o4, l4, o16, l16, ob)


def _weight_arg(w, blk):
    if blk is None:
        return pl.BlockSpec(w.shape, lambda i: (0, 0)), (lambda ref: ref[...])
    D = w.shape[2]
    return (pl.BlockSpec((N_DEV, 128, D), lambda i: (0, blk, 0)),
            lambda ref: ref[...].reshape(N_DEV * 128, D))


def _oproj_fwd(x, o_cat, w, blk, tm):
    T, D = x.shape
    wspec, wload = _weight_arg(w, blk)

    def body(x_ref, o_ref, w_ref, out_ref):
        out_ref[...] = x_ref[...] + _dot(o_ref[...], wload(w_ref))

    tok = pl.BlockSpec((tm, D), lambda i: (i, 0))
    return pl.pallas_call(
        body, name="oproj_fwd", grid=(T // tm,),
        in_specs=[tok, pl.BlockSpec((tm, o_cat.shape[1]), lambda i: (i, 0)), wspec],
        out_specs=tok, out_shape=jax.ShapeDtypeStruct((T, D), F32),
        compiler_params=_params("parallel"),
    )(x, o_cat, w)


def _oproj_bwd(dx, w, blk, tm):
    T, D = dx.shape
    wspec, wload = _weight_arg(w, blk)

    def body(dx_ref, w_ref, dxb_ref, do_ref):
        db = dx_ref[...].astype(BF16)
        dxb_ref[...] = db
        do = _dot_nt(db, wload(w_ref))
        for c in range(8):
            do_ref[c] = do[:, c * PAIR:(c + 1) * PAIR]

    tok = pl.BlockSpec((tm, D), lambda i: (i, 0))
    return pl.pallas_call(
        body, name="oproj_bwd", grid=(T // tm,),
        in_specs=[tok, wspec],
        out_specs=[tok, pl.BlockSpec((8, tm, PAIR), lambda i: (0, i, 0))],
        out_shape=[jax.ShapeDtypeStruct((T, D), BF16), jax.ShapeDtypeStruct((8, T, PAIR), F32)],
        compiler_params=_params("parallel"),
    )(dx, w)


def _attn_post(qkv, gains2, dqa, dka, dva, dqb, dkb, dvb, tm):
    T, NQ = qkv.shape
    scale = HEAD_DIM ** -0.5

    def body(qkv_ref, g_ref, qa1, qa4, qa16, ka1, ka4, ka16, va1, va4, va16, qb_ref, kb_ref, vb_ref,
             out_ref, dg_ref):
        lo = _lo_mask((tm, PAIR))

        @pl.when(pl.program_id(0) == 0)
        def _():
            dg_ref[...] = jnp.zeros_like(dg_ref)

        def norm_bwd(c, gi, dy):
            xv = qkv_ref[:, c * PAIR:(c + 1) * PAIR]
            r = lax.rsqrt(_half_sum(xv * xv, lo) * (1.0 / HEAD_DIM) + EPS)
            xn = xv * r
            dg_ref[gi:gi + 1, :] += jnp.sum(dy * xn, axis=0, keepdims=True)
            dxn = dy * g_ref[gi:gi + 1, :]
            dx = r * (dxn - xn * (_half_sum(dxn * xn, lo) * (1.0 / HEAD_DIM)))
            out_ref[:, c * PAIR:(c + 1) * PAIR] = dx.astype(BF16)

        def fold(v):
            return v + pltpu.roll(v, HEAD_DIM, 1)

        for c in range(4):
            norm_bwd(c, 0, (qa1[c] + qa4[c] + qa16[c]) * scale)
            norm_bwd(4 + c, 1, ka1[c] + ka4[c] + ka16[c])
            out_ref[:, (8 + c) * PAIR:(9 + c) * PAIR] = (va1[c] + va4[c] + va16[c]).astype(BF16)
            norm_bwd(12 + c, 2, qb_ref[c] * scale)
        norm_bwd(16, 3, jnp.where(lo, fold(kb_ref[0]), fold(kb_ref[1])))
        out_ref[:, 17 * PAIR:18 * PAIR] = jnp.where(lo, fold(vb_ref[0]), fold(vb_ref[1])).astype(BF16)

    four = pl.BlockSpec((4, tm, PAIR), lambda i: (0, i, 0))
    two = pl.BlockSpec((2, tm, PAIR), lambda i: (0, i, 0))
    return pl.pallas_call(
        body, name="attn_post", grid=(T // tm,),
        in_specs=[pl.BlockSpec((tm, NQ), lambda i: (i, 0)), pl.BlockSpec((4, PAIR), lambda i: (0, 0))]
        + [four] * 10 + [two, two],
        out_specs=[pl.BlockSpec((tm, NQ), lambda i: (i, 0)), pl.BlockSpec((4, PAIR), lambda i: (0, 0))],
        out_shape=[jax.ShapeDtypeStruct((T, NQ), BF16), jax.ShapeDtypeStruct((4, PAIR), F32)],
        compiler_params=_params("arbitrary"),
    )(qkv, gains2, *dqa, *dka, *dva, dqb, dkb, dvb)


def _dense_norm_bwd(dres, dz, w, blk, x, g, tm):
    T, D = x.shape
    N = dz.shape[1]
    wspec, wload = _weight_arg(w, blk)

    def body(dres_ref, dz_ref, w_ref, x_ref, g_ref, dx_ref, dgn_ref):
        i = pl.program_id(0)
        dx, dg = _norm_bwd(_dot_nt(dz_ref[...], wload(w_ref)), x_ref[...], g_ref[...])
        dx_ref[...] = dres_ref[...] + dx

        @pl.when(i == 0)
        def _():
            dgn_ref[...] = dg

        @pl.when(i > 0)
        def _():
            dgn_ref[...] += dg

    tok = pl.BlockSpec((tm, D), lambda i: (i, 0))
    row = pl.BlockSpec((1, D), lambda i: (0, 0))
    return pl.pallas_call(
        body, name="dense_norm_bwd", grid=(T // tm,),
        in_specs=[tok, pl.BlockSpec((tm, N), lambda i: (i, 0)), wspec, tok, row],
        out_specs=[tok, row],
        out_shape=[jax.ShapeDtypeStruct((T, D), F32), jax.ShapeDtypeStruct((1, D), F32)],
        compiler_params=_params("arbitrary"),
    )(dres, dz, w, x, g)


def _bias_reduce(onehot, dbm):
    Hb, K = dbm.shape

    def body(oh_ref, d_ref, out_ref):
        oh = oh_ref[...]
        d = d_ref[...]
        hi = d.astype(BF16)
        r1 = d - hi.astype(F32)
        mid = r1.astype(BF16)
        low = (r1 - mid.astype(F32)).astype(BF16)
        out_ref[...] = _dot_nt(hi, oh) + _dot_nt(mid, oh) + _dot_nt(low, oh)

    vm = pl.BlockSpec(memory_space=pltpu.VMEM)
    return pl.pallas_call(
        body, name="bias_reduce", in_specs=[vm, vm], out_specs=vm,
        out_shape=jax.ShapeDtypeStruct((Hb, 128), F32),
        compiler_params=pltpu.CompilerParams(vmem_limit_bytes=VMEM_LIMIT),
    )(onehot, dbm)


def _ple_fwd(x, g, wg, blk, p, wp, target, tm):
    T, D = x.shape
    P = p.shape[1]
    with_loss = target is not None
    wspec, wload = _weight_arg(wg, blk)

    def body(*refs):
        if with_loss:
            x_ref, g_ref, wg_ref, p_ref, wp_ref, t_ref, y_ref, hn_ref, gate_ref, pp_ref, pb_ref, loss_ref = refs
        else:
            x_ref, g_ref, wg_ref, p_ref, wp_ref, y_ref, hn_ref, gate_ref, pp_ref, pb_ref = refs
        i = pl.program_id(0)
        xv = x_ref[...]
        hb = (xv * _rstd(xv) * g_ref[...]).astype(BF16)
        hn_ref[...] = hb
        gate = _sigmoid(_dot(hb, wload(wg_ref)))
        pb = p_ref[...].astype(BF16)
        pb_ref[...] = pb
        pp = _dot(pb, wp_ref[...])
        gate_ref[...] = gate
        pp_ref[...] = pp
        y = xv + gate * pp
        if with_loss:
            err = y - t_ref[...]
            y_ref[...] = err * (1.0 / D)
            part = jnp.broadcast_to(0.5 * jnp.sum(jnp.sum(err * err, axis=1, keepdims=True) * (1.0 / D),
                                                  axis=0, keepdims=True), (1, 128))

            @pl.when(i == 0)
            def _():
                loss_ref[...] = part

            @pl.when(i > 0)
            def _():
                loss_ref[...] += part
        else:
            y_ref[...] = y

    tok = pl.BlockSpec((tm, D), lambda i: (i, 0))
    ptok = pl.BlockSpec((tm, P), lambda i: (i, 0))
    in_specs = [tok, pl.BlockSpec((1, D), lambda i: (0, 0)), wspec, ptok,
                pl.BlockSpec((P, D), lambda i: (0, 0))]
    out_specs = [tok, tok, tok, tok, ptok]
    out_shape = [jax.ShapeDtypeStruct((T, D), F32), jax.ShapeDtypeStruct((T, D), BF16),
                 jax.ShapeDtypeStruct((T, D), F32), jax.ShapeDtypeStruct((T, D), F32),
                 jax.ShapeDtypeStruct((T, P), BF16)]
    args = [x, g, wg, p, wp]
    if with_loss:
        in_specs.append(tok)
        out_specs.append(pl.BlockSpec((1, 128), lambda i: (0, 0)))
        out_shape.append(jax.ShapeDtypeStruct((1, 128), F32))
        args.append(target)
    return pl.pallas_call(
        body, name="ple_fwd_loss" if with_loss else "ple_fwd", grid=(T // tm,),
        in_specs=in_specs, out_specs=out_specs, out_shape=out_shape,
        compiler_params=_params("arbitrary" if with_loss else "parallel"),
    )(*args)


def _ple_bwd(dy, gate, pp, tm, dep=None):
    T, D = dy.shape

    def body(dy_ref, gate_ref, pp_ref, dgl_ref, dpp_ref):
        d = dy_ref[...]
        gt = gate_ref[...]
        dgl_ref[...] = (d * pp_ref[...] * gt * (1.0 - gt)).astype(BF16)
        dpp_ref[...] = (d * gt).astype(BF16)

    tok = pl.BlockSpec((tm, D), lambda i: (i, 0))
    body, in_specs, args = _with_dep(body, dep, [tok, tok, tok], [dy, gate, pp])
    return pl.pallas_call(
        body, name="ple_bwd", grid=(T // tm,), in_specs=in_specs, out_specs=[tok, tok],
        out_shape=[jax.ShapeDtypeStruct((T, D), BF16), jax.ShapeDtypeStruct((T, D), BF16)],
        compiler_params=_params("parallel"),
    )(*args)


def _adamw(w, g, m, v):
    shape = w.shape
    C = shape[-1]
    w2, g2, m2, v2 = (a.reshape(-1, C) for a in (w, g, m, v))
    Rn = w2.shape[0]
    tr = Rn
    for cand in (512, 352, 256):
        if Rn % cand == 0:
            tr = cand
            break
    c1 = 1.0 - ADAM_B1 ** ADAM_STEP
    c2 = 1.0 - ADAM_B2 ** ADAM_STEP

    def body(w_ref, g_ref, m_ref, v_ref, d_ref, nm_ref, nv_ref):
        gv = g_ref[...]
        mn = ADAM_B1 * m_ref[...] + (1.0 - ADAM_B1) * gv
        vn = ADAM_B2 * v_ref[...] + (1.0 - ADAM_B2) * (gv * gv)
        d_ref[...] = -ADAM_LR * ((mn / c1) / (jnp.sqrt(vn / c2) + ADAM_EPS) + ADAM_WD * w_ref[...])
        nm_ref[...] = mn
        nv_ref[...] = vn

    spec = pl.BlockSpec((tr, C), lambda i: (i, 0))
    sh = jax.ShapeDtypeStruct((Rn, C), F32)
    d, nm, nv = pl.pallas_call(
        body, name="adamw", grid=(Rn // tr,), in_specs=[spec] * 4, out_specs=[spec] * 3, out_shape=[sh] * 3,
        compiler_params=_params("parallel"),
    )(w2, g2, m2, v2)
    return d.reshape(shape), nm.reshape(shape), nv.reshape(shape)


def _my_place():
    x, y, c = lax.axis_index("x"), lax.axis_index("y"), lax.axis_index("c")
    chips = [(1 - x, y), (x, 1 - y), (1 - x, 1 - y)]
    return x, y, c, chips


def _all_gather(flat):
    R, Wd = flat.shape

    def body(x_ref, out_ref, send_sems, recv_sems, local_sem):
        x, y, c, chips = _my_place()
        me, sibling = (x, y, c), (x, y, 1 - c)

        def rows(px, py, pc):
            return out_ref.at[4 * px + 2 * py + pc]

        def copy(k, block, to, src=None):
            return pltpu.make_async_remote_copy(
                src_ref=rows(*block) if src is None else src, dst_ref=rows(*block),
                send_sem=send_sems.at[k], recv_sem=recv_sems.at[k], device_id=to, device_id_type=MESH)

        mine = pltpu.make_async_copy(x_ref, rows(*me), local_sem)
        mine.start()
        first = [copy(0, me, sibling, src=x_ref)]
        first += [copy(1 + j, me, (*chip, c), src=x_ref) for j, chip in enumerate(chips)]
        for cp in first:
            cp.start()
        passed = [copy(4 + j, (*chip, c), sibling) for j, chip in enumerate(chips)]
        for j, chip in enumerate(chips):
            copy(1 + j, (*chip, c), me).wait_recv()
            passed[j].start()
        copy(0, sibling, me).wait_recv()
        for j, chip in enumerate(chips):
            copy(4 + j, (*chip, 1 - c), me).wait_recv()
        for cp in first + passed:
            cp.wait_send()
        mine.wait()

    return pl.pallas_call(
        body, name="all_gather",
        in_specs=[pl.BlockSpec(memory_space=pl.ANY)], out_specs=pl.BlockSpec(memory_space=pl.ANY),
        out_shape=jax.ShapeDtypeStruct((N_DEV, R, Wd), flat.dtype),
        scratch_shapes=[pltpu.SemaphoreType.DMA((7,)), pltpu.SemaphoreType.DMA((7,)), pltpu.SemaphoreType.DMA],
    )(flat)


def _reduce_scatter(gparts, tr):
    _, R, Wd = gparts.shape
    nt = R // tr

    def body(g_ref, out_ref, a_ref, p_ref, b_ref, vb, vo_b, vo_f, d2d_send, d2d_recv, ici_send, ici_recv):
        x, y, c, chips = _my_place()
        sibling = (x, y, 1 - c)
        allchips = [(x, y)] + chips

        def dev(chip, pc):
            return 4 * chip[0] + 2 * chip[1] + pc

        d2d = [pltpu.make_async_remote_copy(
            src_ref=g_ref.at[dev(q, 1 - c)], dst_ref=a_ref.at[a], send_sem=d2d_send.at[a], recv_sem=d2d_recv.at[a],
            device_id=sibling, device_id_type=MESH) for a, q in enumerate(allchips)]
        for cp in d2d:
            cp.start()

        def add_tiles(srcs, dst, vo):
            def step(t, carry):
                r = pl.ds(pl.multiple_of(t * tr, tr), tr)
                acc = None
                for s_i, src in enumerate(srcs):
                    pltpu.sync_copy(src.at[r], vb.at[s_i])
                for s_i in range(len(srcs)):
                    term = vb[s_i].astype(F32)
                    acc = term if acc is None else acc + term
                vo[...] = acc.astype(vo.dtype)
                pltpu.sync_copy(vo, dst.at[r])
                return carry

            lax.fori_loop(0, nt, step, 0)

        ici = []
        for j, q in enumerate(chips):
            d2d[j + 1].wait_recv()
            add_tiles([g_ref.at[dev(q, c)], a_ref.at[j + 1]], p_ref.at[j], vo_b)
            cp = pltpu.make_async_remote_copy(
                src_ref=p_ref.at[j], dst_ref=b_ref.at[j], send_sem=ici_send.at[j], recv_sem=ici_recv.at[j],
                device_id=(*q, c), device_id_type=MESH)
            cp.start()
            ici.append(cp)
        d2d[0].wait_recv()
        for cp in ici:
            cp.wait_recv()
        add_tiles([g_ref.at[dev((x, y), c)], a_ref.at[0], b_ref.at[0], b_ref.at[1], b_ref.at[2]], out_ref, vo_f)
        for cp in d2d + ici:
            cp.wait_send()

    hbm = pl.BlockSpec(memory_space=pl.ANY)
    out, _, _, _ = pl.pallas_call(
        body, name="reduce_scatter",
        in_specs=[hbm], out_specs=[hbm, hbm, hbm, hbm],
        out_shape=[jax.ShapeDtypeStruct((R, Wd), F32), jax.ShapeDtypeStruct((4, R, Wd), BF16),
                   jax.ShapeDtypeStruct((3, R, Wd), BF16), jax.ShapeDtypeStruct((3, R, Wd), BF16)],
        scratch_shapes=[pltpu.VMEM((5, tr, Wd), BF16), pltpu.VMEM((tr, Wd), BF16), pltpu.VMEM((tr, Wd), F32),
                        pltpu.SemaphoreType.DMA((4,)), pltpu.SemaphoreType.DMA((4,)),
                        pltpu.SemaphoreType.DMA((3,)), pltpu.SemaphoreType.DMA((3,))],
        compiler_params=pltpu.CompilerParams(vmem_limit_bytes=VMEM_LIMIT),
    )(gparts)
    return out


def _peer(x, y, c, k):
    return (x ^ ((k >> 2) & 1), y ^ ((k >> 1) & 1), c ^ (k & 1))


HBM_SPEC = pl.BlockSpec(memory_space=pltpu.HBM)
SEM_SPEC = pl.BlockSpec(memory_space=pltpu.SEMAPHORE)


def _exchange_refs(srcs, lands, m, k, x, y, c, scatter):
    peer = _peer(x, y, c, k)
    if scatter:
        return srcs[m].at[4 * peer[0] + 2 * peer[1] + peer[2]], lands[m].at[k - 1], peer
    return srcs[m], lands[m].at[4 * x + 2 * y + c], peer


def _exchange_start(arrs, land_shapes, scatter, name):
    n = len(arrs)

    def body(*refs):
        srcs, lands = refs[:n], refs[n:2 * n]
        send_sems, recv_sems = refs[2 * n], refs[2 * n + 1]
        token = refs[-1]
        x, y, c, _ = _my_place()
        for m in range(n):
            for k in range(1, N_DEV):
                src, dst, peer = _exchange_refs(srcs, lands, m, k, x, y, c, scatter)
                pltpu.make_async_remote_copy(
                    src_ref=src, dst_ref=dst, send_sem=send_sems.at[7 * m + k - 1],
                    recv_sem=recv_sems.at[7 * m + k - 1], device_id=peer, device_id_type=MESH).start()
        token[...] = jnp.zeros_like(token)

    zones = [lax.empty(s_, a.dtype) for s_, a in zip(land_shapes, arrs)]
    outs = pl.pallas_call(
        body, name=name,
        out_shape=(pltpu.SemaphoreType.DMA((7 * n,)), pltpu.SemaphoreType.DMA((7 * n,)),
                   *[pltpu.HBM(a.shape, a.dtype) for a in arrs], *[pltpu.HBM(z.shape, z.dtype) for z in zones],
                   jax.ShapeDtypeStruct((8, 128), F32)),
        in_specs=[HBM_SPEC] * (2 * n),
        out_specs=(SEM_SPEC, SEM_SPEC, *[HBM_SPEC] * (2 * n), pl.BlockSpec(memory_space=pltpu.VMEM)),
        input_output_aliases={m: 2 + m for m in range(2 * n)},
        compiler_params=pltpu.CompilerParams(has_side_effects=pltpu.SideEffectType.DATAFLOW_SIDE_EFFECTING),
    )(*[pltpu.with_memory_space_constraint(a, pltpu.HBM) for a in arrs],
      *[pltpu.with_memory_space_constraint(z, pltpu.HBM) for z in zones])
    return outs[0], outs[1], list(outs[2:2 + n]), list(outs[2 + n:2 + 2 * n]), outs[-1]


def _exchange_wait(send_sems, recv_sems, arrs, zones, after, scatter, name):
    n = len(arrs)

    def body(*refs):
        srcs, lands = refs[:n], refs[n:2 * n]
        send_sems, recv_sems = refs[2 * n], refs[2 * n + 1]
        x, y, c, _ = _my_place()
        for m in range(n):
            for k in range(1, N_DEV):
                src, dst, peer = _exchange_refs(srcs, lands, m, k, x, y, c, scatter)
                cp = pltpu.make_async_remote_copy(
                    src_ref=src, dst_ref=dst, send_sem=send_sems.at[7 * m + k - 1],
                    recv_sem=recv_sems.at[7 * m + k - 1], device_id=peer, device_id_type=MESH)
                cp.wait_send()
                cp.wait_recv()

    outs = pl.pallas_call(
        body, name=name,
        out_shape=tuple(pltpu.HBM(a.shape, a.dtype) for a in list(arrs) + list(zones)),
        in_specs=[HBM_SPEC] * (2 * n) + [SEM_SPEC, SEM_SPEC, pl.BlockSpec(memory_space=pl.ANY)],
        out_specs=tuple([HBM_SPEC] * (2 * n)),
        input_output_aliases={m: m for m in range(2 * n)},
        compiler_params=pltpu.CompilerParams(has_side_effects=pltpu.SideEffectType.DATAFLOW_SIDE_EFFECTING),
    )(*arrs, *zones, send_sems, recv_sems, after)
    return list(outs[n:])


def _sum_parts(own, parts, tr):
    R, W = own.shape

    def body(own_ref, parts_ref, out_ref):
        acc = own_ref[...].astype(F32)
        for k in range(N_DEV - 1):
            acc = acc + parts_ref[k].astype(F32)
        out_ref[...] = acc

    return pl.pallas_call(
        body, name="sum_parts", grid=(R // tr,),
        in_specs=[pl.BlockSpec((tr, W), lambda i: (i, 0)), pl.BlockSpec((N_DEV - 1, tr, W), lambda i: (0, i, 0))],
        out_specs=pl.BlockSpec((tr, W), lambda i: (i, 0)),
        out_shape=jax.ShapeDtypeStruct((R, W), F32),
        compiler_params=_params("parallel"),
    )(own, parts)


def _all_reduce_small(v):
    Rn, Wd = v.shape

    def body(v_ref, out_ref, gat_ref, send_sems, recv_sems):
        x, y, c, _ = _my_place()
        me = 4 * x + 2 * y + c
        gat_ref[me] = v_ref[...]
        copies = []
        for k in range(1, N_DEV):
            fx, fy, fc = (k >> 2) & 1, (k >> 1) & 1, k & 1
            peer = (x ^ fx, y ^ fy, c ^ fc)
            cp = pltpu.make_async_remote_copy(
                src_ref=v_ref, dst_ref=gat_ref.at[me], send_sem=send_sems.at[k - 1], recv_sem=recv_sems.at[k - 1],
                device_id=peer, device_id_type=MESH)
            cp.start()
            copies.append(cp)
        for cp in copies:
            cp.wait_recv()
        for cp in copies:
            cp.wait_send()
        acc = gat_ref[0]
        for k in range(1, N_DEV):
            acc = acc + gat_ref[k]
        out_ref[...] = acc

    vm = pl.BlockSpec(memory_space=pltpu.VMEM)
    return pl.pallas_call(
        body, name="all_reduce_small", in_specs=[vm], out_specs=vm,
        out_shape=jax.ShapeDtypeStruct((Rn, Wd), F32),
        scratch_shapes=[pltpu.VMEM((N_DEV, Rn, Wd), F32), pltpu.SemaphoreType.DMA((7,)),
                        pltpu.SemaphoreType.DMA((7,))],
    )(v)


def _t5_bucket(rel):
    half = N_BUCKETS // 2
    max_exact = half // 2
    ret = jnp.where(rel > 0, half, 0)
    n = jnp.abs(rel)
    nf = jnp.maximum(n, 1).astype(F32)
    large = max_exact + (jnp.log(nf / max_exact) / math.log(MAX_DISTANCE / max_exact)
                         * (half - max_exact)).astype(jnp.int32)
    large = jnp.minimum(large, half - 1)
    return ret + jnp.where(n < max_exact, n, large)


def _band(R, d):
    W = BQ + 2 * R
    rel = jnp.arange(W)[None, :] - R - jnp.arange(BQ)[:, None]
    return _t5_bucket(rel * d), jnp.abs(rel) <= R


def _onehot(R, d):
    bkt, in_band = _band(R, d)
    return ((bkt.reshape(1, -1) == jnp.arange(128)[:, None]) & in_band.reshape(1, -1)).astype(BF16)


def _bias_expand(table_t, onehot):
    H = table_t.shape[0]
    K = onehot.shape[1]

    def body(t_ref, oh_ref, out_ref):
        oh = oh_ref[...]
        t = t_ref[...]
        hi = t.astype(BF16)
        r1 = t - hi.astype(F32)
        mid = r1.astype(BF16)
        low = (r1 - mid.astype(F32)).astype(BF16)
        marked = _dot(jnp.ones(t.shape, BF16), oh) > 0.5
        out_ref[...] = jnp.where(marked, _dot(hi, oh) + _dot(mid, oh) + _dot(low, oh), NEG)

    vm = pl.BlockSpec(memory_space=pltpu.VMEM)
    return pl.pallas_call(
        body, name="bias_expand", in_specs=[vm, vm], out_specs=vm,
        out_shape=jax.ShapeDtypeStruct((H, K), F32),
        compiler_params=pltpu.CompilerParams(vmem_limit_bytes=VMEM_LIMIT),
    )(table_t, onehot)


def _bias_matrix(table, R, d):
    table_t = jnp.pad(table.T, ((0, 0), (0, 128 - N_BUCKETS)))
    return _bias_expand(table_t, _onehot(R, d)).reshape(table.shape[1], BQ, BQ + 2 * R)


def _bias_variants(base, R):
    H, _, W = base.shape
    col = jnp.arange(W)
    before, after = col < R, col >= BQ + R
    masks = jnp.stack([jnp.zeros_like(before), before, after, before | after])
    v = jnp.where(masks[None, :, None, :], NEG, base[:, None])
    v = v.reshape(H // 2, 2, 4, BQ, W).transpose(0, 2, 1, 3, 4).reshape(H // 2, 4, 2 * BQ, W)
    return v, v.transpose(0, 1, 3, 2)


def _bias_grad(dbt, R, d):
    P, W, _ = dbt.shape
    dbm = dbt.reshape(P, W, 2, BQ).transpose(0, 2, 3, 1).reshape(2 * P, BQ * W)
    return _bias_reduce(_onehot(R, d), dbm)[:, :N_BUCKETS].T


def _deint(a, d):
    if d == 1:
        return a
    H, T, X = a.shape
    return a.reshape(H, T // d, d, X).transpose(0, 2, 1, 3).reshape(H * d, T // d, X)


def _reint(a, d):
    if d == 1:
        return a
    Hd, L, X = a.shape
    return a.reshape(Hd // d, d, L, X).transpose(0, 2, 1, 3).reshape(Hd // d, L * d, X)


def _pad_rows(a, R):
    return jnp.pad(a, ((0, 0), (R, R), (0, 0)))


def _tile2(gain):
    return jnp.concatenate([gain, gain])


ROW_W_O, ROW_GATE, ROW_QKV, ROW_PROJ, B_ROWS = 768, 896, 1024, 1312, 1344
BLK_W_O, BLK_GATE = ROW_W_O // 128, ROW_GATE // 128


def _pack_layer(wts, i):
    a = jnp.stack([wts["ffn1_w_in"][i], wts["ffn2_w_in"][i]])
    D = a.shape[1]
    b = jnp.concatenate([
        wts["ffn1_w_out"][i], wts["ffn2_w_out"][i],
        jnp.zeros((ROW_W_O - 2 * wts["ffn1_w_out"].shape[1], D), a.dtype),
        wts["w_o"][i], wts["w_ple_gate"][i], wts["w_qkv"][i].reshape(-1, D), wts["w_ple_proj"][i].reshape(-1, D)])
    return a, b


def _unpack_layer(ra, rb, like):
    n_out = like["ffn1_w_out"].shape[1]
    return {"ffn1_w_in": ra[0], "ffn2_w_in": ra[1], "ffn1_w_out": rb[:n_out], "ffn2_w_out": rb[n_out:2 * n_out],
            "w_o": rb[ROW_W_O:ROW_GATE], "w_ple_gate": rb[ROW_GATE:ROW_QKV],
            "w_qkv": rb[ROW_QKV:ROW_PROJ].reshape(like["w_qkv"].shape[1:]),
            "w_ple_proj": rb[ROW_PROJ:B_ROWS].reshape(like["w_ple_proj"].shape[1:])}


def _col_sharded(gb, r0, r1, rows):
    return gb[:, r0:r1].reshape(N_DEV, rows, -1).transpose(1, 0, 2).reshape(rows, -1)


def _to_col_shards(g):
    rows = g.shape[0]
    return g.reshape(rows, N_DEV, -1).transpose(1, 0, 2).reshape(N_DEV, -1, 1024)


def _layer_weights(ga, gb, p_dim):
    return dict(ga=ga, gb=gb, w_qkv=_col_sharded(gb, ROW_QKV, ROW_PROJ, ga.shape[2]),
                w_proj=_col_sharded(gb, ROW_PROJ, B_ROWS, p_dim))


def _layer_fwd(x, p, w, sm, i, target, tm, biases, dep=None):
    ga, gb = w["ga"], w["gb"]
    saved = {}
    saved["x0"] = x
    x1, saved["h1"], saved["zg1"], saved["zu1"], saved["s1"] = _ffn_fwd(
        x, sm["norm_ffn1"][i][None], ga, gb, 0, tm, dep)
    saved["x1"] = x1
    qkv, saved["hm"] = _qkv_fwd(x1, sm["norm_mix"][i][None], w["w_qkv"], tm)
    saved["qkv"] = qkv
    gains2 = jnp.stack([_tile2(sm[k][i]) for k in ("q_norm_a", "k_norm_a", "q_norm_b", "k_norm_b")])
    saved["gains2"] = gains2
    qa, ka, va, qb, kb, vb = _attn_prep(qkv, gains2, tm)
    no_sink = jnp.full((8,), NEG, F32)
    branches = []
    outs = []
    for (R, d), bias in zip(DILATED, biases[:3]):
        qd, kd, vd = _deint(qa, d), _pad_rows(_deint(ka, d), R), _pad_rows(_deint(va, d), R)
        sink = jnp.tile(no_sink, d)
        o, lse = _attn_fwd(qd, kd, vd, bias[0], sink, R, 1, d)
        branches.append((qd, kd, vd, bias, sink, R, d))
        outs += [_reint(o, d), _reint(lse, d)]
    bias_b = biases[3]
    kbp, vbp = _pad_rows(kb, SWA_RADIUS), _pad_rows(vb, SWA_RADIUS)
    sink_b = sm["sink_b"][i]
    ob, lb = _attn_fwd(qb, kbp, vbp, bias_b[0], sink_b, SWA_RADIUS, 2, 1)
    oa, la, o_cat = _attn_merge(*outs, ob, tm)
    saved.update(branches=branches, b=(qb, kbp, vbp, bias_b, sink_b), oa=oa, la=la, ob=ob, lb=lb, o_cat=o_cat)
    x2 = _oproj_fwd(x1, o_cat, gb, BLK_W_O, tm)
    saved["x2"] = x2
    x3, saved["h2"], saved["zg2"], saved["zu2"], saved["s2"] = _ffn_fwd(
        x2, sm["norm_ffn2"][i][None], ga, gb, 1, tm)
    saved["x3"] = x3
    res = _ple_fwd(x3, sm["norm_ple"][i][None], gb, BLK_GATE, p, w["w_proj"], target, tm)
    y, saved["hp"], saved["gate"], saved["pp"], saved["pb"] = res[:5]
    loss = res[5] if target is not None else None
    return y, loss, saved


def _layer_bwd(dy, w, sm, i, sv, tm, dep=None):
    ga, gb = w["ga"], w["gb"]
    gs = {}
    D = dy.shape[1]
    dgl, dpp = _ple_bwd(dy, sv["gate"], sv["pp"], tm, dep)
    d_gate = _matmul_tn(sv["hp"], dgl, D, tm)
    d_proj = _matmul_tn(sv["pb"], dpp, D, tm)
    dx3, gs["norm_ple"] = _dense_norm_bwd(dy, dgl, gb, BLK_GATE, sv["x3"], sm["norm_ple"][i][None], tm)
    dx2, dyb, dzg, dzu, gs["norm_ffn2"] = _ffn_bwd(dx3, sv["x2"], sm["norm_ffn2"][i][None], sv["zg2"], sv["zu2"],
                                                   ga, gb, 1, tm)
    dwg2, dwu2, dwo2 = _ffn_dw(sv["h2"], dzg, dzu, sv["s2"], dyb, tm)
    dx2b, do = _oproj_bwd(dx2, gb, BLK_W_O, tm)
    d_wo = _matmul_tn(sv["o_cat"], dx2b, D, tm)
    do_a, do_b = do[:4], do[4:]
    dqa, dka, dva = [], [], []
    drel_a = 0.0
    for qd, kd, vd, bias, sink, R, d in sv["branches"]:
        dq, dk, dv, dbm, _ = _attn_bwd(qd, kd, vd, bias[1], sink, _deint(sv["oa"], d), _deint(sv["la"], d),
                                        _deint(do_a, d), R, 1, d)
        L = qd.shape[1]
        dqa.append(_reint(dq, d))
        dka.append(_reint(dk[:, R:R + L], d))
        dva.append(_reint(dv[:, R:R + L], d))
        drel_a = drel_a + _bias_grad(dbm, R, d)
    qb, kbp, vbp, bias_b, sink_b = sv["b"]
    dqb, dkb, dvb, dbm_b, dsink = _attn_bwd(qb, kbp, vbp, bias_b[1], sink_b, sv["ob"], sv["lb"], do_b,
                                            SWA_RADIUS, 2, 1)
    T = qb.shape[1]
    drel_b = _bias_grad(dbm_b, SWA_RADIUS, 1)
    gs["rel_bias"] = jnp.concatenate([drel_a, drel_b], axis=1)
    gs["sink_b"] = jnp.sum(dsink[:, 0].reshape(-1, 2, BQ), axis=2).reshape(-1)
    dqkv, dgains2 = _attn_post(sv["qkv"], sv["gains2"], dqa, dka, dva, dqb,
                               dkb[:, SWA_RADIUS:SWA_RADIUS + T], dvb[:, SWA_RADIUS:SWA_RADIUS + T], tm // 2)
    dgains = dgains2[:, :HEAD_DIM] + dgains2[:, HEAD_DIM:]
    for k, name in enumerate(("q_norm_a", "k_norm_a", "q_norm_b", "k_norm_b")):
        gs[name] = dgains[k]
    d_qkv = _matmul_tn(sv["hm"], dqkv, dqkv.shape[1] // 2, tm)
    dx1, gs["norm_mix"] = _dense_norm_bwd(dx2, dqkv, w["w_qkv"], None, sv["x1"], sm["norm_mix"][i][None], tm)
    dx0, dyb, dzg, dzu, gs["norm_ffn1"] = _ffn_bwd(dx1, sv["x0"], sm["norm_ffn1"][i][None], sv["zg1"], sv["zu1"],
                                                   ga, gb, 0, tm)
    dwg1, dwu1, dwo1 = _ffn_dw(sv["h1"], dzg, dzu, sv["s1"], dyb, tm)
    da = jnp.stack([jnp.concatenate([dwg1, dwu1]), jnp.concatenate([dwg2, dwu2])], axis=1)
    half = dwo1.shape[1] // 2
    db = jnp.concatenate([
        dwo1.reshape(N_DEV, half, D), dwo2.reshape(N_DEV, half, D),
        jnp.zeros((N_DEV, ROW_W_O - 2 * half, D), BF16),
        d_wo.reshape(N_DEV, -1, D), d_gate.reshape(N_DEV, -1, D), _to_col_shards(d_qkv), _to_col_shards(d_proj)],
        axis=1)
    return dx0, (da, db), gs


def _bias_matrices(rel_bias):
    biases = [_bias_variants(_bias_matrix(rel_bias[:, :8], R, d), R) for R, d in DILATED]
    biases.append(_bias_variants(_bias_matrix(rel_bias[:, 8:], SWA_RADIUS, 1), SWA_RADIUS))
    return biases


def _stack_small(per_layer):
    small = {}
    for k, v in per_layer.items():
        if k == "rel_bias":
            small[k] = sum(v.values())
        else:
            small[k] = jnp.stack([v[i].reshape(-1) for i in sorted(v)])
    return small


TM = 512
RS_TILES = (512, 448)


def _pack_small(d, extra=None):
    parts = [d[k].reshape(-1) for k in SMALL]
    if extra is not None:
        parts.append(extra.reshape(-1))
    flat = jnp.concatenate(parts)
    return jnp.pad(flat, (0, SMALL_ROWS * 128 - flat.shape[0])).reshape(SMALL_ROWS, 128)


def _unpack_small(buf, like):
    flat = buf.reshape(-1)
    out, off = {}, 0
    for k in SMALL:
        n = like[k].size
        out[k] = flat[off:off + n].reshape(like[k].shape)
        off += n
    return out, flat[off]


def kernel(x, p, rel_bias, norm_ffn1, ffn1_w_in, ffn1_w_out, norm_mix, w_qkv, q_norm_a, k_norm_a, q_norm_b, k_norm_b, sink_b, w_o, norm_ffn2, ffn2_w_in, ffn2_w_out, norm_ple, w_ple_gate, w_ple_proj, loss_target, m_rel_bias, m_norm_ffn1, m_ffn1_w_in, m_ffn1_w_out, m_norm_mix, m_w_qkv, m_q_norm_a, m_k_norm_a, m_q_norm_b, m_k_norm_b, m_sink_b, m_w_o, m_norm_ffn2, m_ffn2_w_in, m_ffn2_w_out, m_norm_ple, m_w_ple_gate, m_w_ple_proj, v_rel_bias, v_norm_ffn1, v_ffn1_w_in, v_ffn1_w_out, v_norm_mix, v_w_qkv, v_q_norm_a, v_k_norm_a, v_q_norm_b, v_k_norm_b, v_sink_b, v_w_o, v_norm_ffn2, v_ffn2_w_in, v_ffn2_w_out, v_norm_ple, v_w_ple_gate, v_w_ple_proj):
    wts = dict(rel_bias=rel_bias, norm_ffn1=norm_ffn1, ffn1_w_in=ffn1_w_in, ffn1_w_out=ffn1_w_out,
               norm_mix=norm_mix, w_qkv=w_qkv, q_norm_a=q_norm_a, k_norm_a=k_norm_a, q_norm_b=q_norm_b,
               k_norm_b=k_norm_b, sink_b=sink_b, w_o=w_o, norm_ffn2=norm_ffn2, ffn2_w_in=ffn2_w_in,
               ffn2_w_out=ffn2_w_out, norm_ple=norm_ple, w_ple_gate=w_ple_gate, w_ple_proj=w_ple_proj)
    mom = dict(rel_bias=m_rel_bias, norm_ffn1=m_norm_ffn1, ffn1_w_in=m_ffn1_w_in, ffn1_w_out=m_ffn1_w_out,
               norm_mix=m_norm_mix, w_qkv=m_w_qkv, q_norm_a=m_q_norm_a, k_norm_a=m_k_norm_a, q_norm_b=m_q_norm_b,
               k_norm_b=m_k_norm_b, sink_b=m_sink_b, w_o=m_w_o, norm_ffn2=m_norm_ffn2, ffn2_w_in=m_ffn2_w_in,
               ffn2_w_out=m_ffn2_w_out, norm_ple=m_norm_ple, w_ple_gate=m_w_ple_gate, w_ple_proj=m_w_ple_proj)
    var = dict(rel_bias=v_rel_bias, norm_ffn1=v_norm_ffn1, ffn1_w_in=v_ffn1_w_in, ffn1_w_out=v_ffn1_w_out,
               norm_mix=v_norm_mix, w_qkv=v_w_qkv, q_norm_a=v_q_norm_a, k_norm_a=v_k_norm_a, q_norm_b=v_q_norm_b,
               k_norm_b=v_k_norm_b, sink_b=v_sink_b, w_o=v_w_o, norm_ffn2=v_norm_ffn2, ffn2_w_in=v_ffn2_w_in,
               ffn2_w_out=v_ffn2_w_out, norm_ple=v_norm_ple, w_ple_gate=v_w_ple_gate, w_ple_proj=v_w_ple_proj)
    sm = {k: wts[k] for k in SMALL}
    p_dim = p.shape[-1]
    me = 4 * lax.axis_index("x") + 2 * lax.axis_index("y") + lax.axis_index("c")
    packed = []
    for i in range(2):
        a, b = _pack_layer(wts, i)
        packed.append([a.reshape(-1, a.shape[-1]).astype(BF16), b.astype(BF16)])
    a_shape = (2, ffn1_w_in.shape[1], ffn1_w_in.shape[2])

    def weights_of(zones):
        return _layer_weights(zones[0].reshape((N_DEV,) + a_shape), zones[1], p_dim)

    w0 = weights_of([_all_gather(t) for t in packed[0]])
    zone_shapes = [(N_DEV,) + t.shape for t in packed[1]]
    ssem, rsem, thru, zones, token = _exchange_start(packed[1], zone_shapes, False, "gather_start")
    biases = _bias_matrices(rel_bias)
    x1, _, sv0 = _layer_fwd(x[0], p[0, 0], w0, sm, 0, None, TM, biases, dep=token)
    zones = _exchange_wait(ssem, rsem, thru, zones, x1, False, "gather_wait")
    w1 = weights_of([lax.dynamic_update_index_in_dim(z, t, me, 0) for z, t in zip(zones, packed[1])])
    dy, loss, sv1 = _layer_fwd(x1, p[1, 0], w1, sm, 1, loss_target[0], TM, biases)

    dx1, g1, gs1 = _layer_bwd(dy, w1, sm, 1, sv1, TM)
    g1 = [g1[0].reshape(N_DEV, -1, g1[0].shape[-1]), g1[1]]
    slot_shapes = [(N_DEV - 1,) + t.shape[1:] for t in g1]
    ssem, rsem, thru, slots, token = _exchange_start(g1, slot_shapes, True, "scatter_start")
    dx, g0, gs0 = _layer_bwd(dx1, w0, sm, 0, sv0, TM, dep=token)
    slots = _exchange_wait(ssem, rsem, thru, slots, dx, True, "scatter_wait")
    r1 = [_sum_parts(lax.dynamic_index_in_dim(t, me, 0, keepdims=False), s_, tr)
          for t, s_, tr in zip(g1, slots, RS_TILES)]
    g0 = [g0[0].reshape(N_DEV, -1, g0[0].shape[-1]), g0[1]]
    r0 = [_reduce_scatter(t, tr) for t, tr in zip(g0, RS_TILES)]

    gsmall = _stack_small({k: {0: gs0[k], 1: gs1[k]} for k in gs0})
    small_sum, loss_sum = _unpack_small(_all_reduce_small(_pack_small(gsmall, loss[0, :1])), sm)

    grads = dict(small_sum)
    layers = [_unpack_layer(r[0].reshape(a_shape), r[1], wts) for r in (r0, r1)]
    for k in BIG:
        grads[k] = jnp.stack([layers[0][k], layers[1][k]])

    delta, new_m, new_v = {}, {}, {}
    for k in BIG:
        delta[k], new_m[k], new_v[k] = _adamw(wts[k], grads[k], mom[k], var[k])
    zeros = {k: jnp.zeros_like(wts[k]) for k in SMALL}
    ds, ms, vs = _adamw(_pack_small(wts), _pack_small(small_sum), _pack_small(mom), _pack_small(var))
    for packed, dst in ((ds, delta), (ms, new_m), (vs, new_v)):
        dst.update(_unpack_small(packed, zeros)[0])

    return (loss_sum, dx[None], *[grads[k] for k in WEIGHTS], *[delta[k] for k in WEIGHTS],
            *[new_m[k] for k in WEIGHTS], *[new_v[k] for k in WEIGHTS])
```

```python
import functools
import math

import jax
import jax.numpy as jnp
from jax import lax
from jax.experimental import pallas as pl
from jax.experimental.pallas import tpu as pltpu

F32 = jnp.float32
BF16 = jnp.bfloat16

N_DEV = 8
HEAD_DIM = 64
PAIR = 2 * HEAD_DIM
BQ = 128
N_BUCKETS = 32
MAX_DISTANCE = 1024
DILATED = ((64, 1), (64, 4), (64, 16))
SWA_RADIUS = 128
EPS = 1e-6
NEG = -1e30
ADAM_LR, ADAM_B1, ADAM_B2, ADAM_EPS, ADAM_WD, ADAM_STEP = 0.001, 0.9, 0.999, 1e-08, 0.01, 10
VMEM_LIMIT = 56 * 1024 * 1024
AXES = ("x", "y", "c")
MESH = pl.DeviceIdType.MESH

BIG = ("ffn1_w_in", "ffn1_w_out", "w_qkv", "w_o", "ffn2_w_in", "ffn2_w_out", "w_ple_gate", "w_ple_proj")
SMALL = ("rel_bias", "norm_ffn1", "norm_mix", "q_norm_a", "k_norm_a", "q_norm_b", "k_norm_b", "sink_b",
         "norm_ffn2", "norm_ple")
WEIGHTS = ("rel_bias", "norm_ffn1", "ffn1_w_in", "ffn1_w_out", "norm_mix", "w_qkv", "q_norm_a", "k_norm_a",
           "q_norm_b", "k_norm_b", "sink_b", "w_o", "norm_ffn2", "ffn2_w_in", "ffn2_w_out", "norm_ple",
           "w_ple_gate", "w_ple_proj")
SMALL_ROWS = 96


def _params(*sem):
    return pltpu.CompilerParams(dimension_semantics=sem, vmem_limit_bytes=VMEM_LIMIT)


def _dot(a, b):
    return jnp.dot(a, b, preferred_element_type=F32)


def _dot_nt(a, b):
    return lax.dot_general(a, b, (((1,), (1,)), ((), ())), preferred_element_type=F32)


def _dot_tn(a, b):
    return lax.dot_general(a, b, (((0,), (0,)), ((), ())), preferred_element_type=F32)


def _sigmoid(x):
    return 1.0 / (1.0 + jnp.exp(-x))


def _rstd(xv):
    return lax.rsqrt(jnp.mean(xv * xv, axis=-1, keepdims=True) + EPS)


def _norm_bwd(dh, xv, gv):
    r = _rstd(xv)
    xn = xv * r
    dg = jnp.sum(dh * xn, axis=0, keepdims=True)
    dxn = dh * gv
    dx = r * (dxn - xn * jnp.mean(dxn * xn, axis=-1, keepdims=True))
    return dx, dg


def _lo_mask(shape):
    return lax.broadcasted_iota(jnp.int32, shape, len(shape) - 1) < HEAD_DIM


def _half_sum(t, lo):
    s0 = jnp.sum(jnp.where(lo, t, 0.0), axis=1, keepdims=True)
    s1 = jnp.sum(jnp.where(lo, 0.0, t), axis=1, keepdims=True)
    return jnp.where(lo, s0, s1)


FFN_PARTS = 2


def _ffn_weight_specs(f, nj, D, C):
    return [pl.BlockSpec((None, None, D, C), lambda i, j: (j, f, 0, 0)),
            pl.BlockSpec((None, None, D, C), lambda i, j: (j + nj, f, 0, 0)),
            pl.BlockSpec((2, C // 2, D), lambda i, j: (j, f, 0))]


def _with_dep(body, dep, in_specs, args):
    if dep is None:
        return body, in_specs, args

    def body_after(dep_ref, *refs):
        body(*refs)

    return body_after, [pl.BlockSpec(memory_space=pl.ANY)] + in_specs, [dep] + args


def _ffn_fwd(x, g, ga, gb, f, tm, dep=None):
    T, D = x.shape
    nj, C = ga.shape[0] // 2, ga.shape[3]

    def body(x_ref, g_ref, wg_ref, wu_ref, wo_ref, xo_ref, h_ref, zg_ref, zu_ref, s_ref, h_scr, acc):
        j = pl.program_id(1)

        @pl.when(j == 0)
        def _():
            xv = x_ref[...]
            hb = (xv * _rstd(xv) * g_ref[...]).astype(BF16)
            h_scr[...] = hb
            h_ref[...] = hb
            acc[...] = jnp.zeros_like(acc)

        wo = wo_ref[...].reshape(C, D)
        for part in range(FFN_PARTS):
            sl = pl.ds(part * (tm // FFN_PARTS), tm // FFN_PARTS)
            hb = h_scr[sl, :]
            gt = _dot(hb, wg_ref[...])
            up = _dot(hb, wu_ref[...])
            s = (gt * _sigmoid(gt) * up).astype(BF16)
            zg_ref[sl, :] = gt.astype(BF16)
            zu_ref[sl, :] = up.astype(BF16)
            s_ref[sl, :] = s
            acc[sl, :] += _dot(s, wo)

        @pl.when(j == nj - 1)
        def _():
            xo_ref[...] = x_ref[...] + 0.5 * acc[...]

    tok = pl.BlockSpec((tm, D), lambda i, j: (i, 0))
    chunk = pl.BlockSpec((None, tm, C), lambda i, j: (j, i, 0))
    in_specs = [tok, pl.BlockSpec((1, D), lambda i, j: (0, 0))] + _ffn_weight_specs(f, nj, D, C)
    body, in_specs, args = _with_dep(body, dep, in_specs, [x, g, ga, ga, gb])
    return pl.pallas_call(
        body, name="ffn_fwd", grid=(T // tm, nj),
        in_specs=in_specs,
        out_specs=[tok, tok, chunk, chunk, chunk],
        out_shape=[jax.ShapeDtypeStruct((T, D), F32), jax.ShapeDtypeStruct((T, D), BF16),
                   jax.ShapeDtypeStruct((nj, T, C), BF16), jax.ShapeDtypeStruct((nj, T, C), BF16),
                   jax.ShapeDtypeStruct((nj, T, C), BF16)],
        scratch_shapes=[pltpu.VMEM((tm, D), BF16), pltpu.VMEM((tm, D), F32)],
        compiler_params=_params("parallel", "arbitrary"),
    )(*args)


def _ffn_bwd(dxo, x, g, zg, zu, ga, gb, f, tm, dep=None):
    T, D = x.shape
    nj, C = ga.shape[0] // 2, ga.shape[3]

    def body(dxo_ref, x_ref, g_ref, zg_ref, zu_ref, wg_ref, wu_ref, wo_ref,
             dx_ref, dy_ref, dzg_ref, dzu_ref, dgn_ref, dy_scr, acc):
        i, j = pl.program_id(0), pl.program_id(1)

        @pl.when(j == 0)
        def _():
            dyb = (0.5 * dxo_ref[...]).astype(BF16)
            dy_scr[...] = dyb
            dy_ref[...] = dyb
            acc[...] = jnp.zeros_like(acc)

        wo = wo_ref[...].reshape(C, D)
        for part in range(FFN_PARTS):
            sl = pl.ds(part * (tm // FFN_PARTS), tm // FFN_PARTS)
            ds = _dot_nt(dy_scr[sl, :], wo)
            gt = zg_ref[sl, :].astype(F32)
            up = zu_ref[sl, :].astype(F32)
            sg = _sigmoid(gt)
            dgt = (ds * up * (sg * (1.0 + gt * (1.0 - sg)))).astype(BF16)
            dup = (ds * (gt * sg)).astype(BF16)
            dzg_ref[sl, :] = dgt
            dzu_ref[sl, :] = dup
            acc[sl, :] += _dot_nt(dgt, wg_ref[...]) + _dot_nt(dup, wu_ref[...])

        @pl.when(j == nj - 1)
        def _():
            dx, dg = _norm_bwd(acc[...], x_ref[...], g_ref[...])
            dx_ref[...] = dxo_ref[...] + dx

            @pl.when(i == 0)
            def _():
                dgn_ref[...] = dg

            @pl.when(i > 0)
            def _():
                dgn_ref[...] += dg

    tok = pl.BlockSpec((tm, D), lambda i, j: (i, 0))
    chunk = pl.BlockSpec((None, tm, C), lambda i, j: (j, i, 0))
    row = pl.BlockSpec((1, D), lambda i, j: (0, 0))
    in_specs = [tok, tok, row, chunk, chunk] + _ffn_weight_specs(f, nj, D, C)
    body, in_specs, args = _with_dep(body, dep, in_specs, [dxo, x, g, zg, zu, ga, ga, gb])
    return pl.pallas_call(
        body, name="ffn_bwd", grid=(T // tm, nj),
        in_specs=in_specs,
        out_specs=[tok, tok, chunk, chunk, row],
        out_shape=[jax.ShapeDtypeStruct((T, D), F32), jax.ShapeDtypeStruct((T, D), BF16),
                   jax.ShapeDtypeStruct((nj, T, C), BF16), jax.ShapeDtypeStruct((nj, T, C), BF16),
                   jax.ShapeDtypeStruct((1, D), F32)],
        scratch_shapes=[pltpu.VMEM((tm, D), BF16), pltpu.VMEM((tm, D), F32)],
        compiler_params=_params("arbitrary", "arbitrary"),
    )(*args)


def _ffn_dw(h, dzg, dzu, s, dy, tk):
    T, D = h.shape
    nj, C = s.shape[0], s.shape[2]
    nk = T // tk

    def body(h_ref, dzg_ref, dzu_ref, s_ref, dy_ref, dwg_ref, dwu_ref, dwo_ref, ag, au, ao):
        k = pl.program_id(1)

        @pl.when(k == 0)
        def _():
            ag[...] = jnp.zeros_like(ag)
            au[...] = jnp.zeros_like(au)
            ao[...] = jnp.zeros_like(ao)

        hb = h_ref[...]
        ag[...] += _dot_tn(hb, dzg_ref[...])
        au[...] += _dot_tn(hb, dzu_ref[...])
        ao[...] += _dot_tn(s_ref[...], dy_ref[...])

        @pl.when(k == nk - 1)
        def _():
            dwg_ref[...] = ag[...].astype(BF16)
            dwu_ref[...] = au[...].astype(BF16)
            dwo_ref[...] = ao[...].astype(BF16)

    tok = pl.BlockSpec((tk, D), lambda j, k: (k, 0))
    chunk = pl.BlockSpec((None, tk, C), lambda j, k: (j, k, 0))
    return pl.pallas_call(
        body, name="ffn_dw", grid=(nj, nk),
        in_specs=[tok, chunk, chunk, chunk, tok],
        out_specs=[pl.BlockSpec((None, D, C), lambda j, k: (j, 0, 0)),
                   pl.BlockSpec((None, D, C), lambda j, k: (j, 0, 0)),
                   pl.BlockSpec((None, C, D), lambda j, k: (j, 0, 0))],
        out_shape=[jax.ShapeDtypeStruct((nj, D, C), BF16), jax.ShapeDtypeStruct((nj, D, C), BF16),
                   jax.ShapeDtypeStruct((nj, C, D), BF16)],
        scratch_shapes=[pltpu.VMEM((D, C), F32), pltpu.VMEM((D, C), F32), pltpu.VMEM((C, D), F32)],
        compiler_params=_params("parallel", "arbitrary"),
    )(h, dzg, dzu, s, dy)


def _matmul_tn(a, b, tn, tk):
    T, Ka = a.shape
    N = b.shape[1]
    nk = T // tk

    def body(a_ref, b_ref, o_ref, acc):
        k = pl.program_id(1)

        @pl.when(k == 0)
        def _():
            acc[...] = jnp.zeros_like(acc)

        acc[...] += _dot_tn(a_ref[...], b_ref[...])

        @pl.when(k == nk - 1)
        def _():
            o_ref[...] = acc[...].astype(BF16)

    return pl.pallas_call(
        body, name="matmul_tn", grid=(N // tn, nk),
        in_specs=[pl.BlockSpec((tk, Ka), lambda n, k: (k, 0)), pl.BlockSpec((tk, tn), lambda n, k: (k, n))],
        out_specs=pl.BlockSpec((Ka, tn), lambda n, k: (0, n)),
        out_shape=jax.ShapeDtypeStruct((Ka, N), BF16),
        scratch_shapes=[pltpu.VMEM((Ka, tn), F32)],
        compiler_params=_params("parallel", "arbitrary"),
    )(a, b)


def _qkv_fwd(x, g, w, tm):
    T, D = x.shape
    N = w.shape[1]

    def body(x_ref, g_ref, w_ref, o_ref, h_ref):
        xv = x_ref[...]
        hb = (xv * _rstd(xv) * g_ref[...]).astype(BF16)
        h_ref[...] = hb
        o_ref[...] = _dot(hb, w_ref[...])

    return pl.pallas_call(
        body, name="qkv_fwd", grid=(T // tm,),
        in_specs=[pl.BlockSpec((tm, D), lambda i: (i, 0)), pl.BlockSpec((1, D), lambda i: (0, 0)),
                  pl.BlockSpec((D, N), lambda i: (0, 0))],
        out_specs=[pl.BlockSpec((tm, N), lambda i: (i, 0)), pl.BlockSpec((tm, D), lambda i: (i, 0))],
        out_shape=[jax.ShapeDtypeStruct((T, N), F32), jax.ShapeDtypeStruct((T, D), BF16)],
        compiler_params=_params("parallel"),
    )(x, g, w)


def _attn_prep(qkv, gains2, tm):
    T = qkv.shape[0]
    scale = HEAD_DIM ** -0.5

    def body(qkv_ref, g_ref, qa_ref, ka_ref, va_ref, qb_ref, kb_ref, vb_ref):
        lo = _lo_mask((tm, PAIR))

        def normed(c, gi, mult):
            xv = qkv_ref[:, c * PAIR:(c + 1) * PAIR]
            r = lax.rsqrt(_half_sum(xv * xv, lo) * (1.0 / HEAD_DIM) + EPS)
            y = xv * r * g_ref[gi:gi + 1, :]
            return y * mult if mult != 1.0 else y

        def both_halves(v):
            sw = pltpu.roll(v, HEAD_DIM, 1)
            return jnp.where(lo, v, sw), jnp.where(lo, sw, v)

        for c in range(4):
            qa_ref[c] = normed(c, 0, scale).astype(BF16)
            ka_ref[c] = normed(4 + c, 1, 1.0).astype(BF16)
            va_ref[c] = qkv_ref[:, (8 + c) * PAIR:(9 + c) * PAIR].astype(BF16)
            qb_ref[c] = normed(12 + c, 2, scale).astype(BF16)
        k0, k1 = both_halves(normed(16, 3, 1.0))
        kb_ref[0] = k0.astype(BF16)
        kb_ref[1] = k1.astype(BF16)
        v0, v1 = both_halves(qkv_ref[:, 17 * PAIR:18 * PAIR])
        vb_ref[0] = v0.astype(BF16)
        vb_ref[1] = v1.astype(BF16)

    four = pl.BlockSpec((4, tm, PAIR), lambda i: (0, i, 0))
    two = pl.BlockSpec((2, tm, PAIR), lambda i: (0, i, 0))
    s4 = jax.ShapeDtypeStruct((4, T, PAIR), BF16)
    s2 = jax.ShapeDtypeStruct((2, T, PAIR), BF16)
    return pl.pallas_call(
        body, name="attn_prep", grid=(T // tm,),
        in_specs=[pl.BlockSpec((tm, qkv.shape[1]), lambda i: (i, 0)), pl.BlockSpec((4, PAIR), lambda i: (0, 0))],
        out_specs=[four, four, four, four, two, two],
        out_shape=[s4, s4, s4, s4, s2, s2],
        compiler_params=_params("parallel"),
    )(qkv, gains2)


def _loop_blocks(nb, body, init, per_iter):
    u = math.gcd(nb, per_iter)

    def outer(i, carry):
        for k in range(u):
            carry = body(i * u + k, carry)
        return carry

    return lax.fori_loop(0, nb // u, outer, init)


def _edge_variant(b, nb):
    return (b == 0).astype(jnp.int32) + 2 * (b == nb - 1).astype(jnp.int32)


def _stack_heads(v, lo):
    z = jnp.zeros_like(v)
    return jnp.concatenate([jnp.where(lo, v, z), jnp.where(lo, z, v)], axis=0)


def _unstack_heads(v2, lo):
    return jnp.where(lo, v2[:BQ], v2[BQ:])


def _row_vector(v, lo):
    r = lax.broadcasted_iota(jnp.int32, (BQ, PAIR), 0)
    ln = lax.broadcasted_iota(jnp.int32, (BQ, PAIR), 1)
    diag = (ln % HEAD_DIM) == (r % HEAD_DIM)
    top = jnp.sum(jnp.where(diag & (r < HEAD_DIM), v, 0.0), axis=0, keepdims=True)
    bot = jnp.sum(jnp.where(diag & (r >= HEAD_DIM), v, 0.0), axis=0, keepdims=True)
    top8, bot8 = jnp.broadcast_to(top, (8, PAIR)), jnp.broadcast_to(bot, (8, PAIR))
    lo8 = _lo_mask((8, PAIR))
    head0 = jnp.where(lo8, top8, pltpu.roll(bot8, HEAD_DIM, 1))
    head1 = jnp.where(lo8, pltpu.roll(top8, HEAD_DIM, 1), bot8)
    return jnp.concatenate([head0, head1], axis=1)[:1]


def _attn_fwd(q, kp, vp, bias4, sink, R, pairs_per_kv, pairs_per_bias):
    N, L, _ = q.shape
    W = BQ + 2 * R
    nb = L // BQ

    def body(sink_ref, q_ref, k_ref, v_ref, bias_ref, o_ref, lse_ref):
        n = pl.program_id(0)
        lo_q = _lo_mask((BQ, PAIR))
        first = lax.broadcasted_iota(jnp.int32, (2 * BQ, 1), 0) < BQ
        sk = jnp.where(first, sink_ref[2 * n], sink_ref[2 * n + 1])

        def blk(b, carry):
            q0 = pl.multiple_of(b * BQ, BQ)
            q2 = _stack_heads(q_ref[pl.ds(q0, BQ), :], lo_q)
            kw = k_ref[pl.ds(q0, W), :]
            vw = v_ref[pl.ds(q0, W), :]
            s = _dot_nt(q2, kw) + bias_ref[_edge_variant(b, nb)]
            m = jnp.maximum(jnp.max(s, axis=1, keepdims=True), sk)
            p = jnp.exp(s - m)
            l = jnp.sum(p, axis=1, keepdims=True) + jnp.exp(sk - m)
            o2 = _dot(p.astype(BF16), vw) / l
            o_ref[pl.ds(q0, BQ), :] = _unstack_heads(o2, lo_q)
            lse_ref[pl.ds(q0, BQ), :] = _unstack_heads(jnp.broadcast_to(m + jnp.log(l), (2 * BQ, PAIR)), lo_q)
            return carry

        _loop_blocks(nb, blk, 0, 4)

    qspec = pl.BlockSpec((None, L, PAIR), lambda n: (n, 0, 0))
    kspec = pl.BlockSpec((None, L + 2 * R, PAIR), lambda n: (n // pairs_per_kv, 0, 0))
    return pl.pallas_call(
        body, name="attn_fwd", grid=(N,),
        in_specs=[pl.BlockSpec(memory_space=pltpu.SMEM), qspec, kspec, kspec,
                  pl.BlockSpec((None, 4, 2 * BQ, W), lambda n: (n // pairs_per_bias, 0, 0, 0))],
        out_specs=[qspec, qspec],
        out_shape=[jax.ShapeDtypeStruct((N, L, PAIR), F32), jax.ShapeDtypeStruct((N, L, PAIR), F32)],
        compiler_params=_params("parallel"),
    )(sink, q, kp, vp, bias4)


def _attn_bwd(q, kp, vp, bias4t, sink, o, lse, do, R, pairs_per_kv, pairs_per_bias):
    N, L, _ = q.shape
    Nk = kp.shape[0]
    Pb = bias4t.shape[0]
    W = BQ + 2 * R
    nb = L // BQ

    def body(sink_ref, q_ref, k_ref, v_ref, bias_ref, o_ref, lse_ref, do_ref,
             dq_ref, dk_ref, dv_ref, dbias_ref, dsink_ref):
        n = pl.program_id(0)
        lo_q = _lo_mask((BQ, PAIR))
        first = lax.broadcasted_iota(jnp.int32, (1, 2 * BQ), 1) < BQ
        sk = jnp.where(first, sink_ref[2 * n], sink_ref[2 * n + 1])

        @pl.when(n % pairs_per_kv == 0)
        def _():
            dk_ref[...] = jnp.zeros_like(dk_ref)
            dv_ref[...] = jnp.zeros_like(dv_ref)

        @pl.when(n % pairs_per_bias == 0)
        def _():
            dbias_ref[...] = jnp.zeros_like(dbias_ref)

        def blk(b, dsk):
            q0 = pl.multiple_of(b * BQ, BQ)
            q2 = _stack_heads(q_ref[pl.ds(q0, BQ), :], lo_q)
            kw = k_ref[pl.ds(q0, W), :]
            vw = v_ref[pl.ds(q0, W), :]
            dov = do_ref[pl.ds(q0, BQ), :]
            lse = _row_vector(lse_ref[pl.ds(q0, BQ), :], lo_q)
            delta = _row_vector(_half_sum(dov * o_ref[pl.ds(q0, BQ), :], lo_q), lo_q)
            do2 = _stack_heads(dov.astype(BF16), lo_q)
            st = _dot_nt(kw, q2) + bias_ref[_edge_variant(b, nb)]
            pt = jnp.exp(st - lse)
            dst = pt * (_dot_nt(vw, do2) - delta)
            dstb = dst.astype(BF16)
            dbias_ref[...] += dst
            dk_ref[pl.ds(q0, W), :] += _dot(dstb, q2)
            dv_ref[pl.ds(q0, W), :] += _dot(pt.astype(BF16), do2)
            dq_ref[pl.ds(q0, BQ), :] = _unstack_heads(_dot_tn(dstb, kw), lo_q)
            return dsk - jnp.exp(sk - lse) * delta

        dsk = _loop_blocks(nb, blk, jnp.zeros((1, 2 * BQ), F32), 4)
        dsink_ref[...] = jnp.broadcast_to(dsk, (8, 2 * BQ))

    qspec = pl.BlockSpec((None, L, PAIR), lambda n: (n, 0, 0))
    kspec = pl.BlockSpec((None, L + 2 * R, PAIR), lambda n: (n // pairs_per_kv, 0, 0))
    return pl.pallas_call(
        body, name="attn_bwd", grid=(N,),
        in_specs=[pl.BlockSpec(memory_space=pltpu.SMEM), qspec, kspec, kspec,
                  pl.BlockSpec((None, 4, W, 2 * BQ), lambda n: (n // pairs_per_bias, 0, 0, 0)),
                  qspec, qspec, qspec],
        out_specs=[qspec, kspec, kspec, pl.BlockSpec((None, W, 2 * BQ), lambda n: (n // pairs_per_bias, 0, 0)),
                   pl.BlockSpec((None, 8, 2 * BQ), lambda n: (n, 0, 0))],
        out_shape=[jax.ShapeDtypeStruct((N, L, PAIR), F32),
                   jax.ShapeDtypeStruct((Nk, L + 2 * R, PAIR), F32),
                   jax.ShapeDtypeStruct((Nk, L + 2 * R, PAIR), F32),
                   jax.ShapeDtypeStruct((Pb, W, 2 * BQ), F32),
                   jax.ShapeDtypeStruct((N, 8, 2 * BQ), F32)],
        compiler_params=_params("arbitrary"),
    )(sink, q, kp, vp, bias4t, o, lse, do)


def _attn_merge(o1, l1, o4, l4, o16, l16, ob, tm):
    T = o1.shape[1]

    def body(o1_ref, l1_ref, o4_ref, l4_ref, o16_ref, l16_ref, ob_ref, oa_ref, la_ref, cat_ref):
        for c in range(4):
            a, b, d = l1_ref[c], l4_ref[c], l16_ref[c]
            m = jnp.maximum(jnp.maximum(a, b), d)
            wa, wb, wd = jnp.exp(a - m), jnp.exp(b - m), jnp.exp(d - m)
            z = wa + wb + wd
            o = (wa * o1_ref[c] + wb * o4_ref[c] + wd * o16_ref[c]) / z
            oa_ref[c] = o
            la_ref[c] = m + jnp.log(z)
            cat_ref[:, c * PAIR:(c + 1) * PAIR] = o.astype(BF16)
            cat_ref[:, (4 + c) * PAIR:(5 + c) * PAIR] = ob_ref[c].astype(BF16)

    four = pl.BlockSpec((4, tm, PAIR), lambda i: (0, i, 0))
    s4 = jax.ShapeDtypeStruct((4, T, PAIR), F32)
    return pl.pallas_call(
        body, name="attn_merge", grid=(T // tm,),
        in_specs=[four] * 7,
        out_specs=[four, four, pl.BlockSpec((tm, 8 * PAIR), lambda i: (i, 0))],
        out_shape=[s4, s4, jax.ShapeDtypeStruct((T, 8 * PAIR), BF16)],
        compiler_params=_params("parallel"),
    )(o1, l1, o4, l4, o16, l16, ob)


def _weight_arg(w, blk):
    if blk is None:
        return pl.BlockSpec(w.shape, lambda i: (0, 0)), (lambda ref: ref[...])
    D = w.shape[2]
    return (pl.BlockSpec((N_DEV, 128, D), lambda i: (0, blk, 0)),
            lambda ref: ref[...].reshape(N_DEV * 128, D))


def _oproj_fwd(x, o_cat, w, blk, tm):
    T, D = x.shape
    wspec, wload = _weight_arg(w, blk)

    def body(x_ref, o_ref, w_ref, out_ref):
        out_ref[...] = x_ref[...] + _dot(o_ref[...], wload(w_ref))

    tok = pl.BlockSpec((tm, D), lambda i: (i, 0))
    return pl.pallas_call(
        body, name="oproj_fwd", grid=(T // tm,),
        in_specs=[tok, pl.BlockSpec((tm, o_cat.shape[1]), lambda i: (i, 0)), wspec],
        out_specs=tok, out_shape=jax.ShapeDtypeStruct((T, D), F32),
        compiler_params=_params("parallel"),
    )(x, o_cat, w)


def _oproj_bwd(dx, w, blk, tm):
    T, D = dx.shape
    wspec, wload = _weight_arg(w, blk)

    def body(dx_ref, w_ref, dxb_ref, do_ref):
        db = dx_ref[...].astype(BF16)
        dxb_ref[...] = db
        do = _dot_nt(db, wload(w_ref))
        for c in range(8):
            do_ref[c] = do[:, c * PAIR:(c + 1) * PAIR]

    tok = pl.BlockSpec((tm, D), lambda i: (i, 0))
    return pl.pallas_call(
        body, name="oproj_bwd", grid=(T // tm,),
        in_specs=[tok, wspec],
        out_specs=[tok, pl.BlockSpec((8, tm, PAIR), lambda i: (0, i, 0))],
        out_shape=[jax.ShapeDtypeStruct((T, D), BF16), jax.ShapeDtypeStruct((8, T, PAIR), F32)],
        compiler_params=_params("parallel"),
    )(dx, w)


def _attn_post(qkv, gains2, dqa, dka, dva, dqb, dkb, dvb, tm):
    T, NQ = qkv.shape
    scale = HEAD_DIM ** -0.5

    def body(qkv_ref, g_ref, qa1, qa4, qa16, ka1, ka4, ka16, va1, va4, va16, qb_ref, kb_ref, vb_ref,
             out_ref, dg_ref):
        lo = _lo_mask((tm, PAIR))

        @pl.when(pl.program_id(0) == 0)
        def _():
            dg_ref[...] = jnp.zeros_like(dg_ref)

        def norm_bwd(c, gi, dy):
            xv = qkv_ref[:, c * PAIR:(c + 1) * PAIR]
            r = lax.rsqrt(_half_sum(xv * xv, lo) * (1.0 / HEAD_DIM) + EPS)
            xn = xv * r
            dg_ref[gi:gi + 1, :] += jnp.sum(dy * xn, axis=0, keepdims=True)
            dxn = dy * g_ref[gi:gi + 1, :]
            dx = r * (dxn - xn * (_half_sum(dxn * xn, lo) * (1.0 / HEAD_DIM)))
            out_ref[:, c * PAIR:(c + 1) * PAIR] = dx.astype(BF16)

        def fold(v):
            return v + pltpu.roll(v, HEAD_DIM, 1)

        for c in range(4):
            norm_bwd(c, 0, (qa1[c] + qa4[c] + qa16[c]) * scale)
            norm_bwd(4 + c, 1, ka1[c] + ka4[c] + ka16[c])
            out_ref[:, (8 + c) * PAIR:(9 + c) * PAIR] = (va1[c] + va4[c] + va16[c]).astype(BF16)
            norm_bwd(12 + c, 2, qb_ref[c] * scale)
        norm_bwd(16, 3, jnp.where(lo, fold(kb_ref[0]), fold(kb_ref[1])))
        out_ref[:, 17 * PAIR:18 * PAIR] = jnp.where(lo, fold(vb_ref[0]), fold(vb_ref[1])).astype(BF16)

    four = pl.BlockSpec((4, tm, PAIR), lambda i: (0, i, 0))
    two = pl.BlockSpec((2, tm, PAIR), lambda i: (0, i, 0))
    return pl.pallas_call(
        body, name="attn_post", grid=(T // tm,),
        in_specs=[pl.BlockSpec((tm, NQ), lambda i: (i, 0)), pl.BlockSpec((4, PAIR), lambda i: (0, 0))]
        + [four] * 10 + [two, two],
        out_specs=[pl.BlockSpec((tm, NQ), lambda i: (i, 0)), pl.BlockSpec((4, PAIR), lambda i: (0, 0))],
        out_shape=[jax.ShapeDtypeStruct((T, NQ), BF16), jax.ShapeDtypeStruct((4, PAIR), F32)],
        compiler_params=_params("arbitrary"),
    )(qkv, gains2, *dqa, *dka, *dva, dqb, dkb, dvb)


def _dense_norm_bwd(dres, dz, w, blk, x, g, tm):
    T, D = x.shape
    N = dz.shape[1]
    wspec, wload = _weight_arg(w, blk)

    def body(dres_ref, dz_ref, w_ref, x_ref, g_ref, dx_ref, dgn_ref):
        i = pl.program_id(0)
        dx, dg = _norm_bwd(_dot_nt(dz_ref[...], wload(w_ref)), x_ref[...], g_ref[...])
        dx_ref[...] = dres_ref[...] + dx

        @pl.when(i == 0)
        def _():
            dgn_ref[...] = dg

        @pl.when(i > 0)
        def _():
            dgn_ref[...] += dg

    tok = pl.BlockSpec((tm, D), lambda i: (i, 0))
    row = pl.BlockSpec((1, D), lambda i: (0, 0))
    return pl.pallas_call(
        body, name="dense_norm_bwd", grid=(T // tm,),
        in_specs=[tok, pl.BlockSpec((tm, N), lambda i: (i, 0)), wspec, tok, row],
        out_specs=[tok, row],
        out_shape=[jax.ShapeDtypeStruct((T, D), F32), jax.ShapeDtypeStruct((1, D), F32)],
        compiler_params=_params("arbitrary"),
    )(dres, dz, w, x, g)


def _bias_reduce(onehot, dbm):
    Hb, K = dbm.shape

    def body(oh_ref, d_ref, out_ref):
        oh = oh_ref[...]
        d = d_ref[...]
        hi = d.astype(BF16)
        r1 = d - hi.astype(F32)
        mid = r1.astype(BF16)
        low = (r1 - mid.astype(F32)).astype(BF16)
        out_ref[...] = _dot_nt(hi, oh) + _dot_nt(mid, oh) + _dot_nt(low, oh)

    vm = pl.BlockSpec(memory_space=pltpu.VMEM)
    return pl.pallas_call(
        body, name="bias_reduce", in_specs=[vm, vm], out_specs=vm,
        out_shape=jax.ShapeDtypeStruct((Hb, 128), F32),
        compiler_params=pltpu.CompilerParams(vmem_limit_bytes=VMEM_LIMIT),
    )(onehot, dbm)


def _ple_fwd(x, g, wg, blk, p, wp, target, tm):
    T, D = x.shape
    P = p.shape[1]
    with_loss = target is not None
    wspec, wload = _weight_arg(wg, blk)

    def body(*refs):
        if with_loss:
            x_ref, g_ref, wg_ref, p_ref, wp_ref, t_ref, y_ref, hn_ref, gate_ref, pp_ref, pb_ref, loss_ref = refs
        else:
            x_ref, g_ref, wg_ref, p_ref, wp_ref, y_ref, hn_ref, gate_ref, pp_ref, pb_ref = refs
        i = pl.program_id(0)
        xv = x_ref[...]
        hb = (xv * _rstd(xv) * g_ref[...]).astype(BF16)
        hn_ref[...] = hb
        gate = _sigmoid(_dot(hb, wload(wg_ref)))
        pb = p_ref[...].astype(BF16)
        pb_ref[...] = pb
        pp = _dot(pb, wp_ref[...])
        gate_ref[...] = gate
        pp_ref[...] = pp
        y = xv + gate * pp
        if with_loss:
            err = y - t_ref[...]
            y_ref[...] = err * (1.0 / D)
            part = jnp.broadcast_to(0.5 * jnp.sum(jnp.sum(err * err, axis=1, keepdims=True) * (1.0 / D),
                                                  axis=0, keepdims=True), (1, 128))

            @pl.when(i == 0)
            def _():
                loss_ref[...] = part

            @pl.when(i > 0)
            def _():
                loss_ref[...] += part
        else:
            y_ref[...] = y

    tok = pl.BlockSpec((tm, D), lambda i: (i, 0))
    ptok = pl.BlockSpec((tm, P), lambda i: (i, 0))
    in_specs = [tok, pl.BlockSpec((1, D), lambda i: (0, 0)), wspec, ptok,
                pl.BlockSpec((P, D), lambda i: (0, 0))]
    out_specs = [tok, tok, tok, tok, ptok]
    out_shape = [jax.ShapeDtypeStruct((T, D), F32), jax.ShapeDtypeStruct((T, D), BF16),
                 jax.ShapeDtypeStruct((T, D), F32), jax.ShapeDtypeStruct((T, D), F32),
                 jax.ShapeDtypeStruct((T, P), BF16)]
    args = [x, g, wg, p, wp]
    if with_loss:
        in_specs.append(tok)
        out_specs.append(pl.BlockSpec((1, 128), lambda i: (0, 0)))
        out_shape.append(jax.ShapeDtypeStruct((1, 128), F32))
        args.append(target)
    return pl.pallas_call(
        body, name="ple_fwd_loss" if with_loss else "ple_fwd", grid=(T // tm,),
        in_specs=in_specs, out_specs=out_specs, out_shape=out_shape,
        compiler_params=_params("arbitrary" if with_loss else "parallel"),
    )(*args)


def _ple_bwd(dy, gate, pp, tm, dep=None):
    T, D = dy.shape

    def body(dy_ref, gate_ref, pp_ref, dgl_ref, dpp_ref):
        d = dy_ref[...]
        gt = gate_ref[...]
        dgl_ref[...] = (d * pp_ref[...] * gt * (1.0 - gt)).astype(BF16)
        dpp_ref[...] = (d * gt).astype(BF16)

    tok = pl.BlockSpec((tm, D), lambda i: (i, 0))
    body, in_specs, args = _with_dep(body, dep, [tok, tok, tok], [dy, gate, pp])
    return pl.pallas_call(
        body, name="ple_bwd", grid=(T // tm,), in_specs=in_specs, out_specs=[tok, tok],
        out_shape=[jax.ShapeDtypeStruct((T, D), BF16), jax.ShapeDtypeStruct((T, D), BF16)],
        compiler_params=_params("parallel"),
    )(*args)


def _adamw(w, g, m, v):
    shape = w.shape
    C = shape[-1]
    w2, g2, m2, v2 = (a.reshape(-1, C) for a in (w, g, m, v))
    Rn = w2.shape[0]
    tr = Rn
    for cand in (512, 352, 256):
        if Rn % cand == 0:
            tr = cand
            break
    c1 = 1.0 - ADAM_B1 ** ADAM_STEP
    c2 = 1.0 - ADAM_B2 ** ADAM_STEP

    def body(w_ref, g_ref, m_ref, v_ref, d_ref, nm_ref, nv_ref):
        gv = g_ref[...]
        mn = ADAM_B1 * m_ref[...] + (1.0 - ADAM_B1) * gv
        vn = ADAM_B2 * v_ref[...] + (1.0 - ADAM_B2) * (gv * gv)
        d_ref[...] = -ADAM_LR * ((mn / c1) / (jnp.sqrt(vn / c2) + ADAM_EPS) + ADAM_WD * w_ref[...])
        nm_ref[...] = mn
        nv_ref[...] = vn

    spec = pl.BlockSpec((tr, C), lambda i: (i, 0))
    sh = jax.ShapeDtypeStruct((Rn, C), F32)
    d, nm, nv = pl.pallas_call(
        body, name="adamw", grid=(Rn // tr,), in_specs=[spec] * 4, out_specs=[spec] * 3, out_shape=[sh] * 3,
        compiler_params=_params("parallel"),
    )(w2, g2, m2, v2)
    return d.reshape(shape), nm.reshape(shape), nv.reshape(shape)


def _my_place():
    x, y, c = lax.axis_index("x"), lax.axis_index("y"), lax.axis_index("c")
    chips = [(1 - x, y), (x, 1 - y), (1 - x, 1 - y)]
    return x, y, c, chips


def _all_gather(flat):
    R, Wd = flat.shape

    def body(x_ref, out_ref, send_sems, recv_sems, local_sem):
        x, y, c, chips = _my_place()
        me, sibling = (x, y, c), (x, y, 1 - c)

        def rows(px, py, pc):
            return out_ref.at[4 * px + 2 * py + pc]

        def copy(k, block, to, src=None):
            return pltpu.make_async_remote_copy(
                src_ref=rows(*block) if src is None else src, dst_ref=rows(*block),
                send_sem=send_sems.at[k], recv_sem=recv_sems.at[k], device_id=to, device_id_type=MESH)

        mine = pltpu.make_async_copy(x_ref, rows(*me), local_sem)
        mine.start()
        first = [copy(0, me, sibling, src=x_ref)]
        first += [copy(1 + j, me, (*chip, c), src=x_ref) for j, chip in enumerate(chips)]
        for cp in first:
            cp.start()
        passed = [copy(4 + j, (*chip, c), sibling) for j, chip in enumerate(chips)]
        for j, chip in enumerate(chips):
            copy(1 + j, (*chip, c), me).wait_recv()
            passed[j].start()
        copy(0, sibling, me).wait_recv()
        for j, chip in enumerate(chips):
            copy(4 + j, (*chip, 1 - c), me).wait_recv()
        for cp in first + passed:
            cp.wait_send()
        mine.wait()

    return pl.pallas_call(
        body, name="all_gather",
        in_specs=[pl.BlockSpec(memory_space=pl.ANY)], out_specs=pl.BlockSpec(memory_space=pl.ANY),
        out_shape=jax.ShapeDtypeStruct((N_DEV, R, Wd), flat.dtype),
        scratch_shapes=[pltpu.SemaphoreType.DMA((7,)), pltpu.SemaphoreType.DMA((7,)), pltpu.SemaphoreType.DMA],
    )(flat)


def _reduce_scatter(gparts, tr):
    _, R, Wd = gparts.shape
    nt = R // tr

    def body(g_ref, out_ref, a_ref, p_ref, b_ref, vb, vo_b, vo_f, d2d_send, d2d_recv, ici_send, ici_recv):
        x, y, c, chips = _my_place()
        sibling = (x, y, 1 - c)
        allchips = [(x, y)] + chips

        def dev(chip, pc):
            return 4 * chip[0] + 2 * chip[1] + pc

        d2d = [pltpu.make_async_remote_copy(
            src_ref=g_ref.at[dev(q, 1 - c)], dst_ref=a_ref.at[a], send_sem=d2d_send.at[a], recv_sem=d2d_recv.at[a],
            device_id=sibling, device_id_type=MESH) for a, q in enumerate(allchips)]
        for cp in d2d:
            cp.start()

        def add_tiles(srcs, dst, vo):
            def step(t, carry):
                r = pl.ds(pl.multiple_of(t * tr, tr), tr)
                acc = None
                for s_i, src in enumerate(srcs):
                    pltpu.sync_copy(src.at[r], vb.at[s_i])
                for s_i in range(len(srcs)):
                    term = vb[s_i].astype(F32)
                    acc = term if acc is None else acc + term
                vo[...] = acc.astype(vo.dtype)
                pltpu.sync_copy(vo, dst.at[r])
                return carry

            lax.fori_loop(0, nt, step, 0)

        ici = []
        for j, q in enumerate(chips):
            d2d[j + 1].wait_recv()
            add_tiles([g_ref.at[dev(q, c)], a_ref.at[j + 1]], p_ref.at[j], vo_b)
            cp = pltpu.make_async_remote_copy(
                src_ref=p_ref.at[j], dst_ref=b_ref.at[j], send_sem=ici_send.at[j], recv_sem=ici_recv.at[j],
                device_id=(*q, c), device_id_type=MESH)
            cp.start()
            ici.append(cp)
        d2d[0].wait_recv()
        for cp in ici:
            cp.wait_recv()
        add_tiles([g_ref.at[dev((x, y), c)], a_ref.at[0], b_ref.at[0], b_ref.at[1], b_ref.at[2]], out_ref, vo_f)
        for cp in d2d + ici:
            cp.wait_send()

    hbm = pl.BlockSpec(memory_space=pl.ANY)
    out, _, _, _ = pl.pallas_call(
        body, name="reduce_scatter",
        in_specs=[hbm], out_specs=[hbm, hbm, hbm, hbm],
        out_shape=[jax.ShapeDtypeStruct((R, Wd), F32), jax.ShapeDtypeStruct((4, R, Wd), BF16),
                   jax.ShapeDtypeStruct((3, R, Wd), BF16), jax.ShapeDtypeStruct((3, R, Wd), BF16)],
        scratch_shapes=[pltpu.VMEM((5, tr, Wd), BF16), pltpu.VMEM((tr, Wd), BF16), pltpu.VMEM((tr, Wd), F32),
                        pltpu.SemaphoreType.DMA((4,)), pltpu.SemaphoreType.DMA((4,)),
                        pltpu.SemaphoreType.DMA((3,)), pltpu.SemaphoreType.DMA((3,))],
        compiler_params=pltpu.CompilerParams(vmem_limit_bytes=VMEM_LIMIT),
    )(gparts)
    return out


def _peer(x, y, c, k):
    return (x ^ ((k >> 2) & 1), y ^ ((k >> 1) & 1), c ^ (k & 1))


HBM_SPEC = pl.BlockSpec(memory_space=pltpu.HBM)
SEM_SPEC = pl.BlockSpec(memory_space=pltpu.SEMAPHORE)


def _exchange_refs(srcs, lands, m, k, x, y, c, scatter):
    peer = _peer(x, y, c, k)
    if scatter:
        return srcs[m].at[4 * peer[0] + 2 * peer[1] + peer[2]], lands[m].at[k - 1], peer
    return srcs[m], lands[m].at[4 * x + 2 * y + c], peer


def _exchange_start(arrs, land_shapes, scatter, name):
    n = len(arrs)

    def body(*refs):
        srcs, lands = refs[:n], refs[n:2 * n]
        send_sems, recv_sems = refs[2 * n], refs[2 * n + 1]
        token = refs[-1]
        x, y, c, _ = _my_place()
        for m in range(n):
            for k in range(1, N_DEV):
                src, dst, peer = _exchange_refs(srcs, lands, m, k, x, y, c, scatter)
                pltpu.make_async_remote_copy(
                    src_ref=src, dst_ref=dst, send_sem=send_sems.at[7 * m + k - 1],
                    recv_sem=recv_sems.at[7 * m + k - 1], device_id=peer, device_id_type=MESH).start()
        token[...] = jnp.zeros_like(token)

    zones = [lax.empty(s_, a.dtype) for s_, a in zip(land_shapes, arrs)]
    outs = pl.pallas_call(
        body, name=name,
        out_shape=(pltpu.SemaphoreType.DMA((7 * n,)), pltpu.SemaphoreType.DMA((7 * n,)),
                   *[pltpu.HBM(a.shape, a.dtype) for a in arrs], *[pltpu.HBM(z.shape, z.dtype) for z in zones],
                   jax.ShapeDtypeStruct((8, 128), F32)),
        in_specs=[HBM_SPEC] * (2 * n),
        out_specs=(SEM_SPEC, SEM_SPEC, *[HBM_SPEC] * (2 * n), pl.BlockSpec(memory_space=pltpu.VMEM)),
        input_output_aliases={m: 2 + m for m in range(2 * n)},
        compiler_params=pltpu.CompilerParams(has_side_effects=pltpu.SideEffectType.DATAFLOW_SIDE_EFFECTING),
    )(*[pltpu.with_memory_space_constraint(a, pltpu.HBM) for a in arrs],
      *[pltpu.with_memory_space_constraint(z, pltpu.HBM) for z in zones])
    return outs[0], outs[1], list(outs[2:2 + n]), list(outs[2 + n:2 + 2 * n]), outs[-1]


def _exchange_wait(send_sems, recv_sems, arrs, zones, after, scatter, name):
    n = len(arrs)

    def body(*refs):
        srcs, lands = refs[:n], refs[n:2 * n]
        send_sems, recv_sems = refs[2 * n], refs[2 * n + 1]
        x, y, c, _ = _my_place()
        for m in range(n):
            for k in range(1, N_DEV):
                src, dst, peer = _exchange_refs(srcs, lands, m, k, x, y, c, scatter)
                cp = pltpu.make_async_remote_copy(
                    src_ref=src, dst_ref=dst, send_sem=send_sems.at[7 * m + k - 1],
                    recv_sem=recv_sems.at[7 * m + k - 1], device_id=peer, device_id_type=MESH)
                cp.wait_send()
                cp.wait_recv()

    outs = pl.pallas_call(
        body, name=name,
        out_shape=tuple(pltpu.HBM(a.shape, a.dtype) for a in list(arrs) + list(zones)),
        in_specs=[HBM_SPEC] * (2 * n) + [SEM_SPEC, SEM_SPEC, pl.BlockSpec(memory_space=pl.ANY)],
        out_specs=tuple([HBM_SPEC] * (2 * n)),
        input_output_aliases={m: m for m in range(2 * n)},
        compiler_params=pltpu.CompilerParams(has_side_effects=pltpu.SideEffectType.DATAFLOW_SIDE_EFFECTING),
    )(*arrs, *zones, send_sems, recv_sems, after)
    return list(outs[n:])


def _sum_parts(own, parts, tr):
    R, W = own.shape

    def body(own_ref, parts_ref, out_ref):
        acc = own_ref[...].astype(F32)
        for k in range(N_DEV - 1):
            acc = acc + parts_ref[k].astype(F32)
        out_ref[...] = acc

    return pl.pallas_call(
        body, name="sum_parts", grid=(R // tr,),
        in_specs=[pl.BlockSpec((tr, W), lambda i: (i, 0)), pl.BlockSpec((N_DEV - 1, tr, W), lambda i: (0, i, 0))],
        out_specs=pl.BlockSpec((tr, W), lambda i: (i, 0)),
        out_shape=jax.ShapeDtypeStruct((R, W), F32),
        compiler_params=_params("parallel"),
    )(own, parts)


def _all_reduce_small(v):
    Rn, Wd = v.shape

    def body(v_ref, out_ref, gat_ref, send_sems, recv_sems):
        x, y, c, _ = _my_place()
        me = 4 * x + 2 * y + c
        gat_ref[me] = v_ref[...]
        copies = []
        for k in range(1, N_DEV):
            fx, fy, fc = (k >> 2) & 1, (k >> 1) & 1, k & 1
            peer = (x ^ fx, y ^ fy, c ^ fc)
            cp = pltpu.make_async_remote_copy(
                src_ref=v_ref, dst_ref=gat_ref.at[me], send_sem=send_sems.at[k - 1], recv_sem=recv_sems.at[k - 1],
                device_id=peer, device_id_type=MESH)
            cp.start()
            copies.append(cp)
        for cp in copies:
            cp.wait_recv()
        for cp in copies:
            cp.wait_send()
        acc = gat_ref[0]
        for k in range(1, N_DEV):
            acc = acc + gat_ref[k]
        out_ref[...] = acc

    vm = pl.BlockSpec(memory_space=pltpu.VMEM)
    return pl.pallas_call(
        body, name="all_reduce_small", in_specs=[vm], out_specs=vm,
        out_shape=jax.ShapeDtypeStruct((Rn, Wd), F32),
        scratch_shapes=[pltpu.VMEM((N_DEV, Rn, Wd), F32), pltpu.SemaphoreType.DMA((7,)),
                        pltpu.SemaphoreType.DMA((7,))],
    )(v)


def _t5_bucket(rel):
    half = N_BUCKETS // 2
    max_exact = half // 2
    ret = jnp.where(rel > 0, half, 0)
    n = jnp.abs(rel)
    nf = jnp.maximum(n, 1).astype(F32)
    large = max_exact + (jnp.log(nf / max_exact) / math.log(MAX_DISTANCE / max_exact)
                         * (half - max_exact)).astype(jnp.int32)
    large = jnp.minimum(large, half - 1)
    return ret + jnp.where(n < max_exact, n, large)


def _band(R, d):
    W = BQ + 2 * R
    rel = jnp.arange(W)[None, :] - R - jnp.arange(BQ)[:, None]
    return _t5_bucket(rel * d), jnp.abs(rel) <= R


def _onehot(R, d):
    bkt, in_band = _band(R, d)
    return ((bkt.reshape(1, -1) == jnp.arange(128)[:, None]) & in_band.reshape(1, -1)).astype(BF16)


def _bias_expand(table_t, onehot):
    H = table_t.shape[0]
    K = onehot.shape[1]

    def body(t_ref, oh_ref, out_ref):
        oh = oh_ref[...]
        t = t_ref[...]
        hi = t.astype(BF16)
        r1 = t - hi.astype(F32)
        mid = r1.astype(BF16)
        low = (r1 - mid.astype(F32)).astype(BF16)
        marked = _dot(jnp.ones(t.shape, BF16), oh) > 0.5
        out_ref[...] = jnp.where(marked, _dot(hi, oh) + _dot(mid, oh) + _dot(low, oh), NEG)

    vm = pl.BlockSpec(memory_space=pltpu.VMEM)
    return pl.pallas_call(
        body, name="bias_expand", in_specs=[vm, vm], out_specs=vm,
        out_shape=jax.ShapeDtypeStruct((H, K), F32),
        compiler_params=pltpu.CompilerParams(vmem_limit_bytes=VMEM_LIMIT),
    )(table_t, onehot)


def _bias_matrix(table, R, d):
    table_t = jnp.pad(table.T, ((0, 0), (0, 128 - N_BUCKETS)))
    return _bias_expand(table_t, _onehot(R, d)).reshape(table.shape[1], BQ, BQ + 2 * R)


def _bias_variants(base, R):
    H, _, W = base.shape
    col = jnp.arange(W)
    before, after = col < R, col >= BQ + R
    masks = jnp.stack([jnp.zeros_like(before), before, after, before | after])
    v = jnp.where(masks[None, :, None, :], NEG, base[:, None])
    v = v.reshape(H // 2, 2, 4, BQ, W).transpose(0, 2, 1, 3, 4).reshape(H // 2, 4, 2 * BQ, W)
    return v, v.transpose(0, 1, 3, 2)


def _bias_grad(dbt, R, d):
    P, W, _ = dbt.shape
    dbm = dbt.reshape(P, W, 2, BQ).transpose(0, 2, 3, 1).reshape(2 * P, BQ * W)
    return _bias_reduce(_onehot(R, d), dbm)[:, :N_BUCKETS].T


def _deint(a, d):
    if d == 1:
        return a
    H, T, X = a.shape
    return a.reshape(H, T // d, d, X).transpose(0, 2, 1, 3).reshape(H * d, T // d, X)


def _reint(a, d):
    if d == 1:
        return a
    Hd, L, X = a.shape
    return a.reshape(Hd // d, d, L, X).transpose(0, 2, 1, 3).reshape(Hd // d, L * d, X)


def _pad_rows(a, R):
    return jnp.pad(a, ((0, 0), (R, R), (0, 0)))


def _tile2(gain):
    return jnp.concatenate([gain, gain])


ROW_W_O, ROW_GATE, ROW_QKV, ROW_PROJ, B_ROWS = 768, 896, 1024, 1312, 1344
BLK_W_O, BLK_GATE = ROW_W_O // 128, ROW_GATE // 128


def _pack_layer(wts, i):
    a = jnp.stack([wts["ffn1_w_in"][i], wts["ffn2_w_in"][i]])
    D = a.shape[1]
    b = jnp.concatenate([
        wts["ffn1_w_out"][i], wts["ffn2_w_out"][i],
        jnp.zeros((ROW_W_O - 2 * wts["ffn1_w_out"].shape[1], D), a.dtype),
        wts["w_o"][i], wts["w_ple_gate"][i], wts["w_qkv"][i].reshape(-1, D), wts["w_ple_proj"][i].reshape(-1, D)])
    return a, b


def _unpack_layer(sums, like):
    w_in1, w_out1, w_in2, rest = sums
    n_out = like["ffn1_w_out"].shape[1]
    rows = lambda r0, r1: rest[r0 - n_out:r1 - n_out]
    return {"ffn1_w_in": w_in1, "ffn2_w_in": w_in2, "ffn1_w_out": w_out1, "ffn2_w_out": rows(n_out, 2 * n_out),
            "w_o": rows(ROW_W_O, ROW_GATE), "w_ple_gate": rows(ROW_GATE, ROW_QKV),
            "w_qkv": rows(ROW_QKV, ROW_PROJ).reshape(like["w_qkv"].shape[1:]),
            "w_ple_proj": rows(ROW_PROJ, B_ROWS).reshape(like["w_ple_proj"].shape[1:])}


def _col_sharded(gb, r0, r1, rows):
    return gb[:, r0:r1].reshape(N_DEV, rows, -1).transpose(1, 0, 2).reshape(rows, -1)


def _to_col_shards(g):
    rows = g.shape[0]
    return g.reshape(rows, N_DEV, -1).transpose(1, 0, 2).reshape(N_DEV, -1, 1024)


def _layer_weights(ga, gb, p_dim):
    return dict(ga=ga, gb=gb, w_qkv=_col_sharded(gb, ROW_QKV, ROW_PROJ, ga.shape[2]),
                w_proj=_col_sharded(gb, ROW_PROJ, B_ROWS, p_dim))


def _layer_fwd(x, p, w, sm, i, target, tm, biases, dep=None):
    ga, gb = w["ga"], w["gb"]
    saved = {}
    saved["x0"] = x
    x1, saved["h1"], saved["zg1"], saved["zu1"], saved["s1"] = _ffn_fwd(
        x, sm["norm_ffn1"][i][None], ga, gb, 0, tm, dep)
    saved["x1"] = x1
    qkv, saved["hm"] = _qkv_fwd(x1, sm["norm_mix"][i][None], w["w_qkv"], tm)
    saved["qkv"] = qkv
    gains2 = jnp.stack([_tile2(sm[k][i]) for k in ("q_norm_a", "k_norm_a", "q_norm_b", "k_norm_b")])
    saved["gains2"] = gains2
    qa, ka, va, qb, kb, vb = _attn_prep(qkv, gains2, tm)
    no_sink = jnp.full((8,), NEG, F32)
    branches = []
    outs = []
    for (R, d), bias in zip(DILATED, biases[:3]):
        qd, kd, vd = _deint(qa, d), _pad_rows(_deint(ka, d), R), _pad_rows(_deint(va, d), R)
        sink = jnp.tile(no_sink, d)
        o, lse = _attn_fwd(qd, kd, vd, bias[0], sink, R, 1, d)
        branches.append((qd, kd, vd, bias, sink, R, d))
        outs += [_reint(o, d), _reint(lse, d)]
    bias_b = biases[3]
    kbp, vbp = _pad_rows(kb, SWA_RADIUS), _pad_rows(vb, SWA_RADIUS)
    sink_b = sm["sink_b"][i]
    ob, lb = _attn_fwd(qb, kbp, vbp, bias_b[0], sink_b, SWA_RADIUS, 2, 1)
    oa, la, o_cat = _attn_merge(*outs, ob, tm)
    saved.update(branches=branches, b=(qb, kbp, vbp, bias_b, sink_b), oa=oa, la=la, ob=ob, lb=lb, o_cat=o_cat)
    x2 = _oproj_fwd(x1, o_cat, gb, BLK_W_O, tm)
    saved["x2"] = x2
    x3, saved["h2"], saved["zg2"], saved["zu2"], saved["s2"] = _ffn_fwd(
        x2, sm["norm_ffn2"][i][None], ga, gb, 1, tm)
    saved["x3"] = x3
    res = _ple_fwd(x3, sm["norm_ple"][i][None], gb, BLK_GATE, p, w["w_proj"], target, tm)
    y, saved["hp"], saved["gate"], saved["pp"], saved["pb"] = res[:5]
    loss = res[5] if target is not None else None
    return y, loss, saved


def _layer_bwd(dy, w, sm, i, sv, tm, dep=None, on_early=None):
    ga, gb = w["ga"], w["gb"]
    gs = {}
    D = dy.shape[1]
    dgl, dpp = _ple_bwd(dy, sv["gate"], sv["pp"], tm, dep)
    d_gate = _matmul_tn(sv["hp"], dgl, D, 2 * tm)
    d_proj = _matmul_tn(sv["pb"], dpp, D, 2 * tm)
    dx3, gs["norm_ple"] = _dense_norm_bwd(dy, dgl, gb, BLK_GATE, sv["x3"], sm["norm_ple"][i][None], tm)
    dx2, dyb, dzg, dzu, gs["norm_ffn2"] = _ffn_bwd(dx3, sv["x2"], sm["norm_ffn2"][i][None], sv["zg2"], sv["zu2"],
                                                   ga, gb, 1, tm)
    dwg2, dwu2, dwo2 = _ffn_dw(sv["h2"], dzg, dzu, sv["s2"], dyb, 2 * tm)
    dx2b, do = _oproj_bwd(dx2, gb, BLK_W_O, tm)
    d_wo = _matmul_tn(sv["o_cat"], dx2b, D, 2 * tm)
    do_a, do_b = do[:4], do[4:]
    dqa, dka, dva, dbias = [], [], [], []
    for qd, kd, vd, bias, sink, R, d in sv["branches"]:
        dq, dk, dv, dbm, _ = _attn_bwd(qd, kd, vd, bias[1], sink, _deint(sv["oa"], d), _deint(sv["la"], d),
                                        _deint(do_a, d), R, 1, d)
        L = qd.shape[1]
        dqa.append(_reint(dq, d))
        dka.append(_reint(dk[:, R:R + L], d))
        dva.append(_reint(dv[:, R:R + L], d))
        dbias.append(dbm)
    qb, kbp, vbp, bias_b, sink_b = sv["b"]
    dqb, dkb, dvb, dbm_b, dsink = _attn_bwd(qb, kbp, vbp, bias_b[1], sink_b, sv["ob"], sv["lb"], do_b,
                                            SWA_RADIUS, 2, 1)
    T = qb.shape[1]
    gs["rel_bias"] = dbias + [dbm_b]
    gs["sink_b"] = jnp.sum(dsink[:, 0].reshape(-1, 2, BQ), axis=2).reshape(-1)
    dqkv, dgains2 = _attn_post(sv["qkv"], sv["gains2"], dqa, dka, dva, dqb,
                               dkb[:, SWA_RADIUS:SWA_RADIUS + T], dvb[:, SWA_RADIUS:SWA_RADIUS + T], tm // 2)
    dgains = dgains2[:, :HEAD_DIM] + dgains2[:, HEAD_DIM:]
    for k, name in enumerate(("q_norm_a", "k_norm_a", "q_norm_b", "k_norm_b")):
        gs[name] = dgains[k]
    d_qkv = _matmul_tn(sv["hm"], dqkv, dqkv.shape[1] // 2, 2 * tm)
    half = dwo2.shape[1] // 2
    early = [jnp.concatenate([dwg2, dwu2]), jnp.concatenate([
        dwo2.reshape(N_DEV, half, D), jnp.zeros((N_DEV, ROW_W_O - 2 * half, D), BF16),
        d_wo.reshape(N_DEV, -1, D), d_gate.reshape(N_DEV, -1, D), _to_col_shards(d_qkv), _to_col_shards(d_proj)],
        axis=1)]
    token = None if on_early is None else on_early(early)
    dx1, gs["norm_mix"] = _dense_norm_bwd(dx2, dqkv, w["w_qkv"], None, sv["x1"], sm["norm_mix"][i][None], tm)
    dx0, dyb, dzg, dzu, gs["norm_ffn1"] = _ffn_bwd(dx1, sv["x0"], sm["norm_ffn1"][i][None], sv["zg1"], sv["zu1"],
                                                   ga, gb, 0, tm, token)
    dwg1, dwu1, dwo1 = _ffn_dw(sv["h1"], dzg, dzu, sv["s1"], dyb, 2 * tm)
    late = [jnp.concatenate([dwg1, dwu1]), dwo1.reshape(N_DEV, half, D)]
    return dx0, (late, early), gs


def _bias_matrices(rel_bias):
    biases = [_bias_variants(_bias_matrix(rel_bias[:, :8], R, d), R) for R, d in DILATED]
    biases.append(_bias_variants(_bias_matrix(rel_bias[:, 8:], SWA_RADIUS, 1), SWA_RADIUS))
    return biases


def _stack_small(per_layer):
    small = {}
    for k, v in per_layer.items():
        if k == "rel_bias":
            per_branch = [sum(parts) for parts in zip(*v.values())]
            drel_a = sum(_bias_grad(t, R, d) for t, (R, d) in zip(per_branch[:3], DILATED))
            small[k] = jnp.concatenate([drel_a, _bias_grad(per_branch[3], SWA_RADIUS, 1)], axis=1)
        else:
            small[k] = jnp.stack([v[i].reshape(-1) for i in sorted(v)])
    return small


TM = 512
SUM_TILES = (512, 352, 512, 496)


def _pack_small(d, extra=None):
    parts = [d[k].reshape(-1) for k in SMALL]
    if extra is not None:
        parts.append(extra.reshape(-1))
    flat = jnp.concatenate(parts)
    return jnp.pad(flat, (0, SMALL_ROWS * 128 - flat.shape[0])).reshape(SMALL_ROWS, 128)


def _unpack_small(buf, like):
    flat = buf.reshape(-1)
    out, off = {}, 0
    for k in SMALL:
        n = like[k].size
        out[k] = flat[off:off + n].reshape(like[k].shape)
        off += n
    return out, flat[off]


def kernel(x, p, rel_bias, norm_ffn1, ffn1_w_in, ffn1_w_out, norm_mix, w_qkv, q_norm_a, k_norm_a, q_norm_b, k_norm_b, sink_b, w_o, norm_ffn2, ffn2_w_in, ffn2_w_out, norm_ple, w_ple_gate, w_ple_proj, loss_target, m_rel_bias, m_norm_ffn1, m_ffn1_w_in, m_ffn1_w_out, m_norm_mix, m_w_qkv, m_q_norm_a, m_k_norm_a, m_q_norm_b, m_k_norm_b, m_sink_b, m_w_o, m_norm_ffn2, m_ffn2_w_in, m_ffn2_w_out, m_norm_ple, m_w_ple_gate, m_w_ple_proj, v_rel_bias, v_norm_ffn1, v_ffn1_w_in, v_ffn1_w_out, v_norm_mix, v_w_qkv, v_q_norm_a, v_k_norm_a, v_q_norm_b, v_k_norm_b, v_sink_b, v_w_o, v_norm_ffn2, v_ffn2_w_in, v_ffn2_w_out, v_norm_ple, v_w_ple_gate, v_w_ple_proj):
    wts = dict(rel_bias=rel_bias, norm_ffn1=norm_ffn1, ffn1_w_in=ffn1_w_in, ffn1_w_out=ffn1_w_out,
               norm_mix=norm_mix, w_qkv=w_qkv, q_norm_a=q_norm_a, k_norm_a=k_norm_a, q_norm_b=q_norm_b,
               k_norm_b=k_norm_b, sink_b=sink_b, w_o=w_o, norm_ffn2=norm_ffn2, ffn2_w_in=ffn2_w_in,
               ffn2_w_out=ffn2_w_out, norm_ple=norm_ple, w_ple_gate=w_ple_gate, w_ple_proj=w_ple_proj)
    mom = dict(rel_bias=m_rel_bias, norm_ffn1=m_norm_ffn1, ffn1_w_in=m_ffn1_w_in, ffn1_w_out=m_ffn1_w_out,
               norm_mix=m_norm_mix, w_qkv=m_w_qkv, q_norm_a=m_q_norm_a, k_norm_a=m_k_norm_a, q_norm_b=m_q_norm_b,
               k_norm_b=m_k_norm_b, sink_b=m_sink_b, w_o=m_w_o, norm_ffn2=m_norm_ffn2, ffn2_w_in=m_ffn2_w_in,
               ffn2_w_out=m_ffn2_w_out, norm_ple=m_norm_ple, w_ple_gate=m_w_ple_gate, w_ple_proj=m_w_ple_proj)
    var = dict(rel_bias=v_rel_bias, norm_ffn1=v_norm_ffn1, ffn1_w_in=v_ffn1_w_in, ffn1_w_out=v_ffn1_w_out,
               norm_mix=v_norm_mix, w_qkv=v_w_qkv, q_norm_a=v_q_norm_a, k_norm_a=v_k_norm_a, q_norm_b=v_q_norm_b,
               k_norm_b=v_k_norm_b, sink_b=v_sink_b, w_o=v_w_o, norm_ffn2=v_norm_ffn2, ffn2_w_in=v_ffn2_w_in,
               ffn2_w_out=v_ffn2_w_out, norm_ple=v_norm_ple, w_ple_gate=v_w_ple_gate, w_ple_proj=v_w_ple_proj)
    sm = {k: wts[k] for k in SMALL}
    p_dim = p.shape[-1]
    me = 4 * lax.axis_index("x") + 2 * lax.axis_index("y") + lax.axis_index("c")
    packed = []
    for i in range(2):
        a, b = _pack_layer(wts, i)
        packed.append([a.reshape(-1, a.shape[-1]).astype(BF16), b.astype(BF16)])
    a_shape = (2, ffn1_w_in.shape[1], ffn1_w_in.shape[2])

    def weights_of(zones):
        return _layer_weights(zones[0].reshape((N_DEV,) + a_shape), zones[1], p_dim)

    w0 = weights_of([_all_gather(t) for t in packed[0]])
    zone_shapes = [(N_DEV,) + t.shape for t in packed[1]]
    ssem, rsem, thru, zones, token = _exchange_start(packed[1], zone_shapes, False, "gather_start")
    biases = _bias_matrices(rel_bias)
    x1, _, sv0 = _layer_fwd(x[0], p[0, 0], w0, sm, 0, None, TM, biases, dep=token)
    zones = _exchange_wait(ssem, rsem, thru, zones, x1, False, "gather_wait")
    w1 = weights_of([lax.dynamic_update_index_in_dim(z, t, me, 0) for z, t in zip(zones, packed[1])])
    dy, loss, sv1 = _layer_fwd(x1, p[1, 0], w1, sm, 1, loss_target[0], TM, biases)

    def slots_for(arrs):
        return [(N_DEV - 1,) + t.shape[1:] for t in arrs]

    dx1, (late1, early1), gs1 = _layer_bwd(dy, w1, sm, 1, sv1, TM)
    g1 = late1 + early1
    ex1 = _exchange_start(g1, slots_for(g1), True, "scatter_start")
    held = {}

    def on_early(early):
        held["slots1"] = _exchange_wait(*ex1[:4], early[1], True, "scatter_wait")
        held["ex0"] = _exchange_start(early, slots_for(early), True, "scatter_early_start")
        return held["ex0"][4]

    dx, (late0, early0), gs0 = _layer_bwd(dx1, w0, sm, 0, sv0, TM, dep=ex1[4], on_early=on_early)
    slots0 = _exchange_wait(*held["ex0"][:4], late0[0], True, "scatter_early_wait")

    def summed(arrs, slots, tiles):
        return [_sum_parts(lax.dynamic_index_in_dim(t, me, 0, keepdims=False), s_, tr)
                for t, s_, tr in zip(arrs, slots, tiles)]

    r1 = summed(g1, held["slots1"], SUM_TILES)
    r0 = [_reduce_scatter(t, tr) for t, tr in zip(late0, SUM_TILES[:2])] + summed(early0, slots0, SUM_TILES[2:])

    gsmall = _stack_small({k: {0: gs0[k], 1: gs1[k]} for k in gs0})
    small_sum, loss_sum = _unpack_small(_all_reduce_small(_pack_small(gsmall, loss[0, :1])), sm)

    grads = dict(small_sum)
    layers = [_unpack_layer(r, wts) for r in (r0, r1)]
    for k in BIG:
        grads[k] = jnp.stack([layers[0][k], layers[1][k]])

    delta, new_m, new_v = {}, {}, {}
    for k in BIG:
        delta[k], new_m[k], new_v[k] = _adamw(wts[k], grads[k], mom[k], var[k])
    zeros = {k: jnp.zeros_like(wts[k]) for k in SMALL}
    ds, ms, vs = _adamw(_pack_small(wts), _pack_small(small_sum), _pack_small(mom), _pack_small(var))
    for packed, dst in ((ds, delta), (ms, new_m), (vs, new_v)):
        dst.update(_unpack_small(packed, zeros)[0])

    return (loss_sum, dx[None], *[grads[k] for k in WEIGHTS], *[delta[k] for k in WEIGHTS],
            *[new_m[k] for k in WEIGHTS], *[new_v[k] for k in WEIGHTS])
```

```python
import functools
import math

import jax
import jax.numpy as jnp
from jax import lax
from jax.experimental import pallas as pl
from jax.experimental.pallas import tpu as pltpu

F32 = jnp.float32
BF16 = jnp.bfloat16

N_DEV = 8
HEAD_DIM = 64
PAIR = 2 * HEAD_DIM
BQ = 128
N_BUCKETS = 32
MAX_DISTANCE = 1024
DILATED = ((64, 1), (64, 4), (64, 16))
SWA_RADIUS = 128
EPS = 1e-6
NEG = -1e30
ADAM_LR, ADAM_B1, ADAM_B2, ADAM_EPS, ADAM_WD, ADAM_STEP = 0.001, 0.9, 0.999, 1e-08, 0.01, 10
VMEM_LIMIT = 56 * 1024 * 1024
AXES = ("x", "y", "c")
MESH = pl.DeviceIdType.MESH

BIG = ("ffn1_w_in", "ffn1_w_out", "w_qkv", "w_o", "ffn2_w_in", "ffn2_w_out", "w_ple_gate", "w_ple_proj")
SMALL = ("rel_bias", "norm_ffn1", "norm_mix", "q_norm_a", "k_norm_a", "q_norm_b", "k_norm_b", "sink_b",
         "norm_ffn2", "norm_ple")
WEIGHTS = ("rel_bias", "norm_ffn1", "ffn1_w_in", "ffn1_w_out", "norm_mix", "w_qkv", "q_norm_a", "k_norm_a",
           "q_norm_b", "k_norm_b", "sink_b", "w_o", "norm_ffn2", "ffn2_w_in", "ffn2_w_out", "norm_ple",
           "w_ple_gate", "w_ple_proj")
SMALL_ROWS = 96


def _params(*sem):
    return pltpu.CompilerParams(dimension_semantics=sem, vmem_limit_bytes=VMEM_LIMIT)


def _dot(a, b):
    return jnp.dot(a, b, preferred_element_type=F32)


def _dot_nt(a, b):
    return lax.dot_general(a, b, (((1,), (1,)), ((), ())), preferred_element_type=F32)


def _dot_tn(a, b):
    return lax.dot_general(a, b, (((0,), (0,)), ((), ())), preferred_element_type=F32)


def _sigmoid(x):
    return 1.0 / (1.0 + jnp.exp(-x))


def _rstd(xv):
    return lax.rsqrt(jnp.mean(xv * xv, axis=-1, keepdims=True) + EPS)


def _norm_bwd(dh, xv, gv):
    r = _rstd(xv)
    xn = xv * r
    dg = jnp.sum(dh * xn, axis=0, keepdims=True)
    dxn = dh * gv
    dx = r * (dxn - xn * jnp.mean(dxn * xn, axis=-1, keepdims=True))
    return dx, dg


def _lo_mask(shape):
    return lax.broadcasted_iota(jnp.int32, shape, len(shape) - 1) < HEAD_DIM


def _half_sum(t, lo):
    s0 = jnp.sum(jnp.where(lo, t, 0.0), axis=1, keepdims=True)
    s1 = jnp.sum(jnp.where(lo, 0.0, t), axis=1, keepdims=True)
    return jnp.where(lo, s0, s1)


FFN_PARTS = 2


def _ffn_weight_specs(f, nj, D, C):
    return [pl.BlockSpec((None, None, D, C), lambda i, j: (j, f, 0, 0)),
            pl.BlockSpec((None, None, D, C), lambda i, j: (j + nj, f, 0, 0)),
            pl.BlockSpec((2, C // 2, D), lambda i, j: (j, f, 0))]


def _with_dep(body, dep, in_specs, args):
    if dep is None:
        return body, in_specs, args

    def body_after(dep_ref, *refs):
        body(*refs)

    return body_after, [pl.BlockSpec(memory_space=pl.ANY)] + in_specs, [dep] + args


def _ffn_fwd(x, g, ga, gb, f, tm, dep=None):
    T, D = x.shape
    nj, C = ga.shape[0] // 2, ga.shape[3]

    def body(x_ref, g_ref, wg_ref, wu_ref, wo_ref, xo_ref, h_ref, zg_ref, zu_ref, s_ref, h_scr, acc):
        j = pl.program_id(1)

        @pl.when(j == 0)
        def _():
            xv = x_ref[...]
            hb = (xv * _rstd(xv) * g_ref[...]).astype(BF16)
            h_scr[...] = hb
            h_ref[...] = hb
            acc[...] = jnp.zeros_like(acc)

        wo = wo_ref[...].reshape(C, D)
        for part in range(FFN_PARTS):
            sl = pl.ds(part * (tm // FFN_PARTS), tm // FFN_PARTS)
            hb = h_scr[sl, :]
            gt = _dot(hb, wg_ref[...])
            up = _dot(hb, wu_ref[...])
            s = (gt * _sigmoid(gt) * up).astype(BF16)
            zg_ref[sl, :] = gt.astype(BF16)
            zu_ref[sl, :] = up.astype(BF16)
            s_ref[sl, :] = s
            acc[sl, :] += _dot(s, wo)

        @pl.when(j == nj - 1)
        def _():
            xo_ref[...] = x_ref[...] + 0.5 * acc[...]

    tok = pl.BlockSpec((tm, D), lambda i, j: (i, 0))
    chunk = pl.BlockSpec((None, tm, C), lambda i, j: (j, i, 0))
    in_specs = [tok, pl.BlockSpec((1, D), lambda i, j: (0, 0))] + _ffn_weight_specs(f, nj, D, C)
    body, in_specs, args = _with_dep(body, dep, in_specs, [x, g, ga, ga, gb])
    return pl.pallas_call(
        body, name="ffn_fwd", grid=(T // tm, nj),
        in_specs=in_specs,
        out_specs=[tok, tok, chunk, chunk, chunk],
        out_shape=[jax.ShapeDtypeStruct((T, D), F32), jax.ShapeDtypeStruct((T, D), BF16),
                   jax.ShapeDtypeStruct((nj, T, C), BF16), jax.ShapeDtypeStruct((nj, T, C), BF16),
                   jax.ShapeDtypeStruct((nj, T, C), BF16)],
        scratch_shapes=[pltpu.VMEM((tm, D), BF16), pltpu.VMEM((tm, D), F32)],
        compiler_params=_params("parallel", "arbitrary"),
    )(*args)


def _ffn_bwd(dxo, x, g, zg, zu, ga, gb, f, tm, dep=None):
    T, D = x.shape
    nj, C = ga.shape[0] // 2, ga.shape[3]

    def body(dxo_ref, x_ref, g_ref, zg_ref, zu_ref, wg_ref, wu_ref, wo_ref,
             dx_ref, dy_ref, dzg_ref, dzu_ref, dgn_ref, dy_scr, acc):
        i, j = pl.program_id(0), pl.program_id(1)

        @pl.when(j == 0)
        def _():
            dyb = (0.5 * dxo_ref[...]).astype(BF16)
            dy_scr[...] = dyb
            dy_ref[...] = dyb
            acc[...] = jnp.zeros_like(acc)

        wo = wo_ref[...].reshape(C, D)
        for part in range(FFN_PARTS):
            sl = pl.ds(part * (tm // FFN_PARTS), tm // FFN_PARTS)
            ds = _dot_nt(dy_scr[sl, :], wo)
            gt = zg_ref[sl, :].astype(F32)
            up = zu_ref[sl, :].astype(F32)
            sg = _sigmoid(gt)
            dgt = (ds * up * (sg * (1.0 + gt * (1.0 - sg)))).astype(BF16)
            dup = (ds * (gt * sg)).astype(BF16)
            dzg_ref[sl, :] = dgt
            dzu_ref[sl, :] = dup
            acc[sl, :] += _dot_nt(dgt, wg_ref[...]) + _dot_nt(dup, wu_ref[...])

        @pl.when(j == nj - 1)
        def _():
            dx, dg = _norm_bwd(acc[...], x_ref[...], g_ref[...])
            dx_ref[...] = dxo_ref[...] + dx

            @pl.when(i == 0)
            def _():
                dgn_ref[...] = dg

            @pl.when(i > 0)
            def _():
                dgn_ref[...] += dg

    tok = pl.BlockSpec((tm, D), lambda i, j: (i, 0))
    chunk = pl.BlockSpec((None, tm, C), lambda i, j: (j, i, 0))
    row = pl.BlockSpec((1, D), lambda i, j: (0, 0))
    in_specs = [tok, tok, row, chunk, chunk] + _ffn_weight_specs(f, nj, D, C)
    body, in_specs, args = _with_dep(body, dep, in_specs, [dxo, x, g, zg, zu, ga, ga, gb])
    return pl.pallas_call(
        body, name="ffn_bwd", grid=(T // tm, nj),
        in_specs=in_specs,
        out_specs=[tok, tok, chunk, chunk, row],
        out_shape=[jax.ShapeDtypeStruct((T, D), F32), jax.ShapeDtypeStruct((T, D), BF16),
                   jax.ShapeDtypeStruct((nj, T, C), BF16), jax.ShapeDtypeStruct((nj, T, C), BF16),
                   jax.ShapeDtypeStruct((1, D), F32)],
        scratch_shapes=[pltpu.VMEM((tm, D), BF16), pltpu.VMEM((tm, D), F32)],
        compiler_params=_params("arbitrary", "arbitrary"),
    )(*args)


def _ffn_dw_in(h, dzg, dzu, tk):
    T, D = h.shape
    nj, C = dzg.shape[0], dzg.shape[2]
    nk = T // tk

    def body(h_ref, dzg_ref, dzu_ref, dw_ref, acc):
        n, k = pl.program_id(0), pl.program_id(1)

        @pl.when(k == 0)
        def _():
            acc[...] = jnp.zeros_like(acc)

        @pl.when(n < nj)
        def _():
            acc[...] += _dot_tn(h_ref[...], dzg_ref[...])

        @pl.when(n >= nj)
        def _():
            acc[...] += _dot_tn(h_ref[...], dzu_ref[...])

        @pl.when(k == nk - 1)
        def _():
            dw_ref[...] = acc[...].astype(BF16)

    chunk = pl.BlockSpec((None, tk, C), lambda n, k: (n % nj, k, 0))
    return pl.pallas_call(
        body, name="ffn_dw_in", grid=(2 * nj, nk),
        in_specs=[pl.BlockSpec((tk, D), lambda n, k: (k, 0)), chunk, chunk],
        out_specs=pl.BlockSpec((None, D, C), lambda n, k: (n, 0, 0)),
        out_shape=jax.ShapeDtypeStruct((2 * nj, D, C), BF16),
        scratch_shapes=[pltpu.VMEM((D, C), F32)],
        compiler_params=_params("parallel", "arbitrary"),
    )(h, dzg, dzu)


def _ffn_dw_out(s, dy, tk):
    nj, T, C = s.shape
    D = dy.shape[1]
    nk = T // tk

    def body(s_ref, dy_ref, dw_ref, acc):
        k = pl.program_id(1)

        @pl.when(k == 0)
        def _():
            acc[...] = jnp.zeros_like(acc)

        acc[...] += _dot_tn(s_ref[...], dy_ref[...])

        @pl.when(k == nk - 1)
        def _():
            dw_ref[...] = acc[...].astype(BF16)

    return pl.pallas_call(
        body, name="ffn_dw_out", grid=(nj, nk),
        in_specs=[pl.BlockSpec((None, tk, C), lambda j, k: (j, k, 0)), pl.BlockSpec((tk, D), lambda j, k: (k, 0))],
        out_specs=pl.BlockSpec((None, C, D), lambda j, k: (j, 0, 0)),
        out_shape=jax.ShapeDtypeStruct((nj, C, D), BF16),
        scratch_shapes=[pltpu.VMEM((C, D), F32)],
        compiler_params=_params("parallel", "arbitrary"),
    )(s, dy)


def _matmul_tn(a, b, tn, tk):
    T, Ka = a.shape
    N = b.shape[1]
    nk = T // tk

    def body(a_ref, b_ref, o_ref, acc):
        k = pl.program_id(1)

        @pl.when(k == 0)
        def _():
            acc[...] = jnp.zeros_like(acc)

        acc[...] += _dot_tn(a_ref[...], b_ref[...])

        @pl.when(k == nk - 1)
        def _():
            o_ref[...] = acc[...].astype(BF16)

    return pl.pallas_call(
        body, name="matmul_tn", grid=(N // tn, nk),
        in_specs=[pl.BlockSpec((tk, Ka), lambda n, k: (k, 0)), pl.BlockSpec((tk, tn), lambda n, k: (k, n))],
        out_specs=pl.BlockSpec((Ka, tn), lambda n, k: (0, n)),
        out_shape=jax.ShapeDtypeStruct((Ka, N), BF16),
        scratch_shapes=[pltpu.VMEM((Ka, tn), F32)],
        compiler_params=_params("parallel", "arbitrary"),
    )(a, b)


def _qkv_fwd(x, g, w, tm):
    T, D = x.shape
    N = w.shape[1]

    def body(x_ref, g_ref, w_ref, o_ref, h_ref):
        xv = x_ref[...]
        hb = (xv * _rstd(xv) * g_ref[...]).astype(BF16)
        h_ref[...] = hb
        o_ref[...] = _dot(hb, w_ref[...])

    return pl.pallas_call(
        body, name="qkv_fwd", grid=(T // tm,),
        in_specs=[pl.BlockSpec((tm, D), lambda i: (i, 0)), pl.BlockSpec((1, D), lambda i: (0, 0)),
                  pl.BlockSpec((D, N), lambda i: (0, 0))],
        out_specs=[pl.BlockSpec((tm, N), lambda i: (i, 0)), pl.BlockSpec((tm, D), lambda i: (i, 0))],
        out_shape=[jax.ShapeDtypeStruct((T, N), F32), jax.ShapeDtypeStruct((T, D), BF16)],
        compiler_params=_params("parallel"),
    )(x, g, w)


def _attn_prep(qkv, gains2, tm):
    T = qkv.shape[0]
    scale = HEAD_DIM ** -0.5

    def body(qkv_ref, g_ref, qa_ref, ka_ref, va_ref, qb_ref, kb_ref, vb_ref):
        lo = _lo_mask((tm, PAIR))

        def normed(c, gi, mult):
            xv = qkv_ref[:, c * PAIR:(c + 1) * PAIR]
            r = lax.rsqrt(_half_sum(xv * xv, lo) * (1.0 / HEAD_DIM) + EPS)
            y = xv * r * g_ref[gi:gi + 1, :]
            return y * mult if mult != 1.0 else y

        def both_halves(v):
            sw = pltpu.roll(v, HEAD_DIM, 1)
            return jnp.where(lo, v, sw), jnp.where(lo, sw, v)

        for c in range(4):
            qa_ref[c] = normed(c, 0, scale).astype(BF16)
            ka_ref[c] = normed(4 + c, 1, 1.0).astype(BF16)
            va_ref[c] = qkv_ref[:, (8 + c) * PAIR:(9 + c) * PAIR].astype(BF16)
            qb_ref[c] = normed(12 + c, 2, scale).astype(BF16)
        k0, k1 = both_halves(normed(16, 3, 1.0))
        kb_ref[0] = k0.astype(BF16)
        kb_ref[1] = k1.astype(BF16)
        v0, v1 = both_halves(qkv_ref[:, 17 * PAIR:18 * PAIR])
        vb_ref[0] = v0.astype(BF16)
        vb_ref[1] = v1.astype(BF16)

    four = pl.BlockSpec((4, tm, PAIR), lambda i: (0, i, 0))
    two = pl.BlockSpec((2, tm, PAIR), lambda i: (0, i, 0))
    s4 = jax.ShapeDtypeStruct((4, T, PAIR), BF16)
    s2 = jax.ShapeDtypeStruct((2, T, PAIR), BF16)
    return pl.pallas_call(
        body, name="attn_prep", grid=(T // tm,),
        in_specs=[pl.BlockSpec((tm, qkv.shape[1]), lambda i: (i, 0)), pl.BlockSpec((4, PAIR), lambda i: (0, 0))],
        out_specs=[four, four, four, four, two, two],
        out_shape=[s4, s4, s4, s4, s2, s2],
        compiler_params=_params("parallel"),
    )(qkv, gains2)


def _loop_blocks(nb, body, init, per_iter):
    u = math.gcd(nb, per_iter)

    def outer(i, carry):
        for k in range(u):
            carry = body(i * u + k, carry)
        return carry

    return lax.fori_loop(0, nb // u, outer, init)


def _edge_variant(b, nb):
    return (b == 0).astype(jnp.int32) + 2 * (b == nb - 1).astype(jnp.int32)


def _stack_heads(v, lo):
    z = jnp.zeros_like(v)
    return jnp.concatenate([jnp.where(lo, v, z), jnp.where(lo, z, v)], axis=0)


def _unstack_heads(v2, lo):
    return jnp.where(lo, v2[:BQ], v2[BQ:])


def _row_vector(v, lo):
    r = lax.broadcasted_iota(jnp.int32, (BQ, PAIR), 0)
    ln = lax.broadcasted_iota(jnp.int32, (BQ, PAIR), 1)
    diag = (ln % HEAD_DIM) == (r % HEAD_DIM)
    top = jnp.sum(jnp.where(diag & (r < HEAD_DIM), v, 0.0), axis=0, keepdims=True)
    bot = jnp.sum(jnp.where(diag & (r >= HEAD_DIM), v, 0.0), axis=0, keepdims=True)
    top8, bot8 = jnp.broadcast_to(top, (8, PAIR)), jnp.broadcast_to(bot, (8, PAIR))
    lo8 = _lo_mask((8, PAIR))
    head0 = jnp.where(lo8, top8, pltpu.roll(bot8, HEAD_DIM, 1))
    head1 = jnp.where(lo8, pltpu.roll(top8, HEAD_DIM, 1), bot8)
    return jnp.concatenate([head0, head1], axis=1)[:1]


def _units_per_step(nb, pairs_per_kv):
    return max(1, 16 // nb) if pairs_per_kv == 1 else 1


def _attn_fwd(q, kp, vp, bias4, sink, R, pairs_per_kv, pairs_per_bias):
    N, L, _ = q.shape
    W = BQ + 2 * R
    nb = L // BQ
    G = _units_per_step(nb, pairs_per_kv)

    def body(sink_ref, q_ref, k_ref, v_ref, bias_ref, o_ref, lse_ref):
        n = pl.program_id(0)
        lo_q = _lo_mask((BQ, PAIR))
        first = lax.broadcasted_iota(jnp.int32, (2 * BQ, 1), 0) < BQ

        def blk(f, carry):
            g, b = f // nb, f % nb
            u = n * G + g
            sk = jnp.where(first, sink_ref[2 * u], sink_ref[2 * u + 1])
            q0 = pl.multiple_of(b * BQ, BQ)
            q2 = _stack_heads(q_ref[g, pl.ds(q0, BQ), :], lo_q)
            kw = k_ref[g, pl.ds(q0, W), :]
            vw = v_ref[g, pl.ds(q0, W), :]
            s = _dot_nt(q2, kw) + bias_ref[_edge_variant(b, nb)]
            m = jnp.maximum(jnp.max(s, axis=1, keepdims=True), sk)
            p = jnp.exp(s - m)
            l = jnp.sum(p, axis=1, keepdims=True) + jnp.exp(sk - m)
            o2 = _dot(p.astype(BF16), vw) / l
            o_ref[g, pl.ds(q0, BQ), :] = _unstack_heads(o2, lo_q)
            lse_ref[g, pl.ds(q0, BQ), :] = _unstack_heads(jnp.broadcast_to(m + jnp.log(l), (2 * BQ, PAIR)), lo_q)
            return carry

        _loop_blocks(G * nb, blk, 0, 4)

    qspec = pl.BlockSpec((G, L, PAIR), lambda n: (n, 0, 0))
    kspec = pl.BlockSpec((G, L + 2 * R, PAIR), lambda n: (n // pairs_per_kv, 0, 0))
    return pl.pallas_call(
        body, name="attn_fwd", grid=(N // G,),
        in_specs=[pl.BlockSpec(memory_space=pltpu.SMEM), qspec, kspec, kspec,
                  pl.BlockSpec((None, 4, 2 * BQ, W), lambda n: (n * G // pairs_per_bias, 0, 0, 0))],
        out_specs=[qspec, qspec],
        out_shape=[jax.ShapeDtypeStruct((N, L, PAIR), F32), jax.ShapeDtypeStruct((N, L, PAIR), F32)],
        compiler_params=_params("parallel"),
    )(sink, q, kp, vp, bias4)


def _attn_bwd(q, kp, vp, bias4t, sink, o, lse, do, R, pairs_per_kv, pairs_per_bias):
    N, L, _ = q.shape
    Nk = kp.shape[0]
    Pb = bias4t.shape[0]
    W = BQ + 2 * R
    nb = L // BQ
    G = _units_per_step(nb, pairs_per_kv)

    def body(sink_ref, q_ref, k_ref, v_ref, bias_ref, o_ref, lse_ref, do_ref,
             dq_ref, dk_ref, dv_ref, dbias_ref, dsink_ref):
        n = pl.program_id(0)
        lo_q = _lo_mask((BQ, PAIR))
        first = lax.broadcasted_iota(jnp.int32, (1, 2 * BQ), 1) < BQ
        dsink_ref[...] = jnp.zeros_like(dsink_ref)

        @pl.when(n % pairs_per_kv == 0)
        def _():
            dk_ref[...] = jnp.zeros_like(dk_ref)
            dv_ref[...] = jnp.zeros_like(dv_ref)

        @pl.when((n * G) % pairs_per_bias == 0)
        def _():
            dbias_ref[...] = jnp.zeros_like(dbias_ref)

        def blk(f, carry):
            g, b = f // nb, f % nb
            u = n * G + g
            sk = jnp.where(first, sink_ref[2 * u], sink_ref[2 * u + 1])
            q0 = pl.multiple_of(b * BQ, BQ)
            q2 = _stack_heads(q_ref[g, pl.ds(q0, BQ), :], lo_q)
            kw = k_ref[g, pl.ds(q0, W), :]
            vw = v_ref[g, pl.ds(q0, W), :]
            dov = do_ref[g, pl.ds(q0, BQ), :]
            lse = _row_vector(lse_ref[g, pl.ds(q0, BQ), :], lo_q)
            delta = _row_vector(_half_sum(dov * o_ref[g, pl.ds(q0, BQ), :], lo_q), lo_q)
            do2 = _stack_heads(dov.astype(BF16), lo_q)
            st = _dot_nt(kw, q2) + bias_ref[_edge_variant(b, nb)]
            pt = jnp.exp(st - lse)
            dst = pt * (_dot_nt(vw, do2) - delta)
            dstb = dst.astype(BF16)
            dbias_ref[...] += dst
            dk_ref[g, pl.ds(q0, W), :] += _dot(dstb, q2)
            dv_ref[g, pl.ds(q0, W), :] += _dot(pt.astype(BF16), do2)
            dq_ref[g, pl.ds(q0, BQ), :] = _unstack_heads(_dot_tn(dstb, kw), lo_q)
            dsink_ref[g, pl.ds(0, 1), :] -= jnp.exp(sk - lse) * delta
            return carry

        _loop_blocks(G * nb, blk, 0, 4)

    qspec = pl.BlockSpec((G, L, PAIR), lambda n: (n, 0, 0))
    kspec = pl.BlockSpec((G, L + 2 * R, PAIR), lambda n: (n // pairs_per_kv, 0, 0))
    return pl.pallas_call(
        body, name="attn_bwd", grid=(N // G,),
        in_specs=[pl.BlockSpec(memory_space=pltpu.SMEM), qspec, kspec, kspec,
                  pl.BlockSpec((None, 4, W, 2 * BQ), lambda n: (n * G // pairs_per_bias, 0, 0, 0)),
                  qspec, qspec, qspec],
        out_specs=[qspec, kspec, kspec,
                   pl.BlockSpec((None, W, 2 * BQ), lambda n: (n * G // pairs_per_bias, 0, 0)),
                   pl.BlockSpec((G, 8, 2 * BQ), lambda n: (n, 0, 0))],
        out_shape=[jax.ShapeDtypeStruct((N, L, PAIR), F32),
                   jax.ShapeDtypeStruct((Nk, L + 2 * R, PAIR), F32),
                   jax.ShapeDtypeStruct((Nk, L + 2 * R, PAIR), F32),
                   jax.ShapeDtypeStruct((Pb, W, 2 * BQ), F32),
                   jax.ShapeDtypeStruct((N, 8, 2 * BQ), F32)],
        compiler_params=_params("arbitrary"),
    )(sink, q, kp, vp, bias4t, o, lse, do)


def _attn_merge(o1, l1, o4, l4, o16, l16, ob, tm):
    T = o1.shape[1]

    def body(o1_ref, l1_ref, o4_ref, l4_ref, o16_ref, l16_ref, ob_ref, oa_ref, la_ref, cat_ref):
        for c in range(4):
            a, b, d = l1_ref[c], l4_ref[c], l16_ref[c]
            m = jnp.maximum(jnp.maximum(a, b), d)
            wa, wb, wd = jnp.exp(a - m), jnp.exp(b - m), jnp.exp(d - m)
            z = wa + wb + wd
            o = (wa * o1_ref[c] + wb * o4_ref[c] + wd * o16_ref[c]) / z
            oa_ref[c] = o
            la_ref[c] = m + jnp.log(z)
            cat_ref[:, c * PAIR:(c + 1) * PAIR] = o.astype(BF16)
            cat_ref[:, (4 + c) * PAIR:(5 + c) * PAIR] = ob_ref[c].astype(BF16)

    four = pl.BlockSpec((4, tm, PAIR), lambda i: (0, i, 0))
    s4 = jax.ShapeDtypeStruct((4, T, PAIR), F32)
    return pl.pallas_call(
        body, name="attn_merge", grid=(T // tm,),
        in_specs=[four] * 7,
        out_specs=[four, four, pl.BlockSpec((tm, 8 * PAIR), lambda i: (i, 0))],
        out_shape=[s4, s4, jax.ShapeDtypeStruct((T, 8 * PAIR), BF16)],
        compiler_params=_params("parallel"),
    )(o1, l1, o4, l4, o16, l16, ob)


def _weight_arg(w, blk):
    if blk is None:
        return pl.BlockSpec(w.shape, lambda i: (0, 0)), (lambda ref: ref[...])
    D = w.shape[2]
    return (pl.BlockSpec((N_DEV, 128, D), lambda i: (0, blk, 0)),
            lambda ref: ref[...].reshape(N_DEV * 128, D))


def _oproj_fwd(x, o_cat, w, blk, tm):
    T, D = x.shape
    wspec, wload = _weight_arg(w, blk)

    def body(x_ref, o_ref, w_ref, out_ref):
        out_ref[...] = x_ref[...] + _dot(o_ref[...], wload(w_ref))

    tok = pl.BlockSpec((tm, D), lambda i: (i, 0))
    return pl.pallas_call(
        body, name="oproj_fwd", grid=(T // tm,),
        in_specs=[tok, pl.BlockSpec((tm, o_cat.shape[1]), lambda i: (i, 0)), wspec],
        out_specs=tok, out_shape=jax.ShapeDtypeStruct((T, D), F32),
        compiler_params=_params("parallel"),
    )(x, o_cat, w)


def _oproj_bwd(dx, w, blk, tm):
    T, D = dx.shape
    wspec, wload = _weight_arg(w, blk)

    def body(dx_ref, w_ref, dxb_ref, do_ref):
        db = dx_ref[...].astype(BF16)
        dxb_ref[...] = db
        do = _dot_nt(db, wload(w_ref))
        for c in range(8):
            do_ref[c] = do[:, c * PAIR:(c + 1) * PAIR]

    tok = pl.BlockSpec((tm, D), lambda i: (i, 0))
    return pl.pallas_call(
        body, name="oproj_bwd", grid=(T // tm,),
        in_specs=[tok, wspec],
        out_specs=[tok, pl.BlockSpec((8, tm, PAIR), lambda i: (0, i, 0))],
        out_shape=[jax.ShapeDtypeStruct((T, D), BF16), jax.ShapeDtypeStruct((8, T, PAIR), F32)],
        compiler_params=_params("parallel"),
    )(dx, w)


def _attn_post(qkv, gains2, dqa, dka, dva, dqb, dkb, dvb, tm):
    T, NQ = qkv.shape
    scale = HEAD_DIM ** -0.5

    def body(qkv_ref, g_ref, qa1, qa4, qa16, ka1, ka4, ka16, va1, va4, va16, qb_ref, kb_ref, vb_ref,
             out_ref, dg_ref):
        lo = _lo_mask((tm, PAIR))

        @pl.when(pl.program_id(0) == 0)
        def _():
            dg_ref[...] = jnp.zeros_like(dg_ref)

        def norm_bwd(c, gi, dy):
            xv = qkv_ref[:, c * PAIR:(c + 1) * PAIR]
            r = lax.rsqrt(_half_sum(xv * xv, lo) * (1.0 / HEAD_DIM) + EPS)
            xn = xv * r
            dg_ref[gi:gi + 1, :] += jnp.sum(dy * xn, axis=0, keepdims=True)
            dxn = dy * g_ref[gi:gi + 1, :]
            dx = r * (dxn - xn * (_half_sum(dxn * xn, lo) * (1.0 / HEAD_DIM)))
            out_ref[:, c * PAIR:(c + 1) * PAIR] = dx.astype(BF16)

        def fold(v):
            return v + pltpu.roll(v, HEAD_DIM, 1)

        for c in range(4):
            norm_bwd(c, 0, (qa1[c] + qa4[c] + qa16[c]) * scale)
            norm_bwd(4 + c, 1, ka1[c] + ka4[c] + ka16[c])
            out_ref[:, (8 + c) * PAIR:(9 + c) * PAIR] = (va1[c] + va4[c] + va16[c]).astype(BF16)
            norm_bwd(12 + c, 2, qb_ref[c] * scale)
        norm_bwd(16, 3, jnp.where(lo, fold(kb_ref[0]), fold(kb_ref[1])))
        out_ref[:, 17 * PAIR:18 * PAIR] = jnp.where(lo, fold(vb_ref[0]), fold(vb_ref[1])).astype(BF16)

    four = pl.BlockSpec((4, tm, PAIR), lambda i: (0, i, 0))
    two = pl.BlockSpec((2, tm, PAIR), lambda i: (0, i, 0))
    return pl.pallas_call(
        body, name="attn_post", grid=(T // tm,),
        in_specs=[pl.BlockSpec((tm, NQ), lambda i: (i, 0)), pl.BlockSpec((4, PAIR), lambda i: (0, 0))]
        + [four] * 10 + [two, two],
        out_specs=[pl.BlockSpec((tm, NQ), lambda i: (i, 0)), pl.BlockSpec((4, PAIR), lambda i: (0, 0))],
        out_shape=[jax.ShapeDtypeStruct((T, NQ), BF16), jax.ShapeDtypeStruct((4, PAIR), F32)],
        compiler_params=_params("arbitrary"),
    )(qkv, gains2, *dqa, *dka, *dva, dqb, dkb, dvb)


def _dense_norm_bwd(dres, dz, w, blk, x, g, tm):
    T, D = x.shape
    N = dz.shape[1]
    wspec, wload = _weight_arg(w, blk)

    def body(dres_ref, dz_ref, w_ref, x_ref, g_ref, dx_ref, dgn_ref):
        i = pl.program_id(0)
        dx, dg = _norm_bwd(_dot_nt(dz_ref[...], wload(w_ref)), x_ref[...], g_ref[...])
        dx_ref[...] = dres_ref[...] + dx

        @pl.when(i == 0)
        def _():
            dgn_ref[...] = dg

        @pl.when(i > 0)
        def _():
            dgn_ref[...] += dg

    tok = pl.BlockSpec((tm, D), lambda i: (i, 0))
    row = pl.BlockSpec((1, D), lambda i: (0, 0))
    return pl.pallas_call(
        body, name="dense_norm_bwd", grid=(T // tm,),
        in_specs=[tok, pl.BlockSpec((tm, N), lambda i: (i, 0)), wspec, tok, row],
        out_specs=[tok, row],
        out_shape=[jax.ShapeDtypeStruct((T, D), F32), jax.ShapeDtypeStruct((1, D), F32)],
        compiler_params=_params("arbitrary"),
    )(dres, dz, w, x, g)


def _bias_reduce(onehot, dbm):
    Hb, K = dbm.shape

    def body(oh_ref, d_ref, out_ref):
        oh = oh_ref[...]
        d = d_ref[...]
        hi = d.astype(BF16)
        r1 = d - hi.astype(F32)
        mid = r1.astype(BF16)
        low = (r1 - mid.astype(F32)).astype(BF16)
        out_ref[...] = _dot_nt(hi, oh) + _dot_nt(mid, oh) + _dot_nt(low, oh)

    vm = pl.BlockSpec(memory_space=pltpu.VMEM)
    return pl.pallas_call(
        body, name="bias_reduce", in_specs=[vm, vm], out_specs=vm,
        out_shape=jax.ShapeDtypeStruct((Hb, 128), F32),
        compiler_params=pltpu.CompilerParams(vmem_limit_bytes=VMEM_LIMIT),
    )(onehot, dbm)


def _ple_fwd(x, g, wg, blk, p, wp, target, tm):
    T, D = x.shape
    P = p.shape[1]
    with_loss = target is not None
    wspec, wload = _weight_arg(wg, blk)

    def body(*refs):
        if with_loss:
            x_ref, g_ref, wg_ref, p_ref, wp_ref, t_ref, y_ref, hn_ref, gate_ref, pp_ref, pb_ref, loss_ref = refs
        else:
            x_ref, g_ref, wg_ref, p_ref, wp_ref, y_ref, hn_ref, gate_ref, pp_ref, pb_ref = refs
        i = pl.program_id(0)
        xv = x_ref[...]
        hb = (xv * _rstd(xv) * g_ref[...]).astype(BF16)
        hn_ref[...] = hb
        gate = _sigmoid(_dot(hb, wload(wg_ref)))
        pb = p_ref[...].astype(BF16)
        pb_ref[...] = pb
        pp = _dot(pb, wp_ref[...])
        gate_ref[...] = gate
        pp_ref[...] = pp
        y = xv + gate * pp
        if with_loss:
            err = y - t_ref[...]
            y_ref[...] = err * (1.0 / D)
            part = jnp.broadcast_to(0.5 * jnp.sum(jnp.sum(err * err, axis=1, keepdims=True) * (1.0 / D),
                                                  axis=0, keepdims=True), (1, 128))

            @pl.when(i == 0)
            def _():
                loss_ref[...] = part

            @pl.when(i > 0)
            def _():
                loss_ref[...] += part
        else:
            y_ref[...] = y

    tok = pl.BlockSpec((tm, D), lambda i: (i, 0))
    ptok = pl.BlockSpec((tm, P), lambda i: (i, 0))
    in_specs = [tok, pl.BlockSpec((1, D), lambda i: (0, 0)), wspec, ptok,
                pl.BlockSpec((P, D), lambda i: (0, 0))]
    out_specs = [tok, tok, tok, tok, ptok]
    out_shape = [jax.ShapeDtypeStruct((T, D), F32), jax.ShapeDtypeStruct((T, D), BF16),
                 jax.ShapeDtypeStruct((T, D), F32), jax.ShapeDtypeStruct((T, D), F32),
                 jax.ShapeDtypeStruct((T, P), BF16)]
    args = [x, g, wg, p, wp]
    if with_loss:
        in_specs.append(tok)
        out_specs.append(pl.BlockSpec((1, 128), lambda i: (0, 0)))
        out_shape.append(jax.ShapeDtypeStruct((1, 128), F32))
        args.append(target)
    return pl.pallas_call(
        body, name="ple_fwd_loss" if with_loss else "ple_fwd", grid=(T // tm,),
        in_specs=in_specs, out_specs=out_specs, out_shape=out_shape,
        compiler_params=_params("arbitrary" if with_loss else "parallel"),
    )(*args)


def _ple_bwd(dy, gate, pp, tm, dep=None):
    T, D = dy.shape

    def body(dy_ref, gate_ref, pp_ref, dgl_ref, dpp_ref):
        d = dy_ref[...]
        gt = gate_ref[...]
        dgl_ref[...] = (d * pp_ref[...] * gt * (1.0 - gt)).astype(BF16)
        dpp_ref[...] = (d * gt).astype(BF16)

    tok = pl.BlockSpec((tm, D), lambda i: (i, 0))
    body, in_specs, args = _with_dep(body, dep, [tok, tok, tok], [dy, gate, pp])
    return pl.pallas_call(
        body, name="ple_bwd", grid=(T // tm,), in_specs=in_specs, out_specs=[tok, tok],
        out_shape=[jax.ShapeDtypeStruct((T, D), BF16), jax.ShapeDtypeStruct((T, D), BF16)],
        compiler_params=_params("parallel"),
    )(*args)


def _adamw(w, g, m, v):
    shape = w.shape
    C = shape[-1]
    w2, g2, m2, v2 = (a.reshape(-1, C) for a in (w, g, m, v))
    Rn = w2.shape[0]
    tr = Rn
    for cand in (512, 352, 256):
        if Rn % cand == 0:
            tr = cand
            break
    c1 = 1.0 - ADAM_B1 ** ADAM_STEP
    c2 = 1.0 - ADAM_B2 ** ADAM_STEP

    def body(w_ref, g_ref, m_ref, v_ref, d_ref, nm_ref, nv_ref):
        gv = g_ref[...]
        mn = ADAM_B1 * m_ref[...] + (1.0 - ADAM_B1) * gv
        vn = ADAM_B2 * v_ref[...] + (1.0 - ADAM_B2) * (gv * gv)
        d_ref[...] = -ADAM_LR * ((mn / c1) / (jnp.sqrt(vn / c2) + ADAM_EPS) + ADAM_WD * w_ref[...])
        nm_ref[...] = mn
        nv_ref[...] = vn

    spec = pl.BlockSpec((tr, C), lambda i: (i, 0))
    sh = jax.ShapeDtypeStruct((Rn, C), F32)
    d, nm, nv = pl.pallas_call(
        body, name="adamw", grid=(Rn // tr,), in_specs=[spec] * 4, out_specs=[spec] * 3, out_shape=[sh] * 3,
        compiler_params=_params("parallel"),
    )(w2, g2, m2, v2)
    return d.reshape(shape), nm.reshape(shape), nv.reshape(shape)


def _my_place():
    x, y, c = lax.axis_index("x"), lax.axis_index("y"), lax.axis_index("c")
    chips = [(1 - x, y), (x, 1 - y), (1 - x, 1 - y)]
    return x, y, c, chips


def _all_gather(flat):
    R, Wd = flat.shape

    def body(x_ref, out_ref, send_sems, recv_sems, local_sem):
        x, y, c, chips = _my_place()
        me, sibling = (x, y, c), (x, y, 1 - c)

        def rows(px, py, pc):
            return out_ref.at[4 * px + 2 * py + pc]

        def copy(k, block, to, src=None):
            return pltpu.make_async_remote_copy(
                src_ref=rows(*block) if src is None else src, dst_ref=rows(*block),
                send_sem=send_sems.at[k], recv_sem=recv_sems.at[k], device_id=to, device_id_type=MESH)

        mine = pltpu.make_async_copy(x_ref, rows(*me), local_sem)
        mine.start()
        first = [copy(0, me, sibling, src=x_ref)]
        first += [copy(1 + j, me, (*chip, c), src=x_ref) for j, chip in enumerate(chips)]
        for cp in first:
            cp.start()
        passed = [copy(4 + j, (*chip, c), sibling) for j, chip in enumerate(chips)]
        for j, chip in enumerate(chips):
            copy(1 + j, (*chip, c), me).wait_recv()
            passed[j].start()
        copy(0, sibling, me).wait_recv()
        for j, chip in enumerate(chips):
            copy(4 + j, (*chip, 1 - c), me).wait_recv()
        for cp in first + passed:
            cp.wait_send()
        mine.wait()

    return pl.pallas_call(
        body, name="all_gather",
        in_specs=[pl.BlockSpec(memory_space=pl.ANY)], out_specs=pl.BlockSpec(memory_space=pl.ANY),
        out_shape=jax.ShapeDtypeStruct((N_DEV, R, Wd), flat.dtype),
        scratch_shapes=[pltpu.SemaphoreType.DMA((7,)), pltpu.SemaphoreType.DMA((7,)), pltpu.SemaphoreType.DMA],
    )(flat)


def _reduce_scatter(gparts, tr):
    _, R, Wd = gparts.shape
    nt = R // tr

    def body(g_ref, out_ref, a_ref, p_ref, b_ref, vb, vo_b, vo_f, d2d_send, d2d_recv, ici_send, ici_recv):
        x, y, c, chips = _my_place()
        sibling = (x, y, 1 - c)
        allchips = [(x, y)] + chips

        def dev(chip, pc):
            return 4 * chip[0] + 2 * chip[1] + pc

        d2d = [pltpu.make_async_remote_copy(
            src_ref=g_ref.at[dev(q, 1 - c)], dst_ref=a_ref.at[a], send_sem=d2d_send.at[a], recv_sem=d2d_recv.at[a],
            device_id=sibling, device_id_type=MESH) for a, q in enumerate(allchips)]
        for cp in d2d:
            cp.start()

        def add_tiles(srcs, dst, vo):
            def step(t, carry):
                r = pl.ds(pl.multiple_of(t * tr, tr), tr)
                acc = None
                for s_i, src in enumerate(srcs):
                    pltpu.sync_copy(src.at[r], vb.at[s_i])
                for s_i in range(len(srcs)):
                    term = vb[s_i].astype(F32)
                    acc = term if acc is None else acc + term
                vo[...] = acc.astype(vo.dtype)
                pltpu.sync_copy(vo, dst.at[r])
                return carry

            lax.fori_loop(0, nt, step, 0)

        ici = []
        for j, q in enumerate(chips):
            d2d[j + 1].wait_recv()
            add_tiles([g_ref.at[dev(q, c)], a_ref.at[j + 1]], p_ref.at[j], vo_b)
            cp = pltpu.make_async_remote_copy(
                src_ref=p_ref.at[j], dst_ref=b_ref.at[j], send_sem=ici_send.at[j], recv_sem=ici_recv.at[j],
                device_id=(*q, c), device_id_type=MESH)
            cp.start()
            ici.append(cp)
        d2d[0].wait_recv()
        for cp in ici:
            cp.wait_recv()
        add_tiles([g_ref.at[dev((x, y), c)], a_ref.at[0], b_ref.at[0], b_ref.at[1], b_ref.at[2]], out_ref, vo_f)
        for cp in d2d + ici:
            cp.wait_send()

    hbm = pl.BlockSpec(memory_space=pl.ANY)
    out, _, _, _ = pl.pallas_call(
        body, name="reduce_scatter",
        in_specs=[hbm], out_specs=[hbm, hbm, hbm, hbm],
        out_shape=[jax.ShapeDtypeStruct((R, Wd), F32), jax.ShapeDtypeStruct((4, R, Wd), BF16),
                   jax.ShapeDtypeStruct((3, R, Wd), BF16), jax.ShapeDtypeStruct((3, R, Wd), BF16)],
        scratch_shapes=[pltpu.VMEM((5, tr, Wd), BF16), pltpu.VMEM((tr, Wd), BF16), pltpu.VMEM((tr, Wd), F32),
                        pltpu.SemaphoreType.DMA((4,)), pltpu.SemaphoreType.DMA((4,)),
                        pltpu.SemaphoreType.DMA((3,)), pltpu.SemaphoreType.DMA((3,))],
        compiler_params=pltpu.CompilerParams(vmem_limit_bytes=VMEM_LIMIT),
    )(gparts)
    return out


def _peer(x, y, c, k):
    return (x ^ ((k >> 2) & 1), y ^ ((k >> 1) & 1), c ^ (k & 1))


HBM_SPEC = pl.BlockSpec(memory_space=pltpu.HBM)
SEM_SPEC = pl.BlockSpec(memory_space=pltpu.SEMAPHORE)


def _exchange_refs(srcs, lands, m, k, x, y, c, scatter):
    peer = _peer(x, y, c, k)
    if scatter:
        return srcs[m].at[4 * peer[0] + 2 * peer[1] + peer[2]], lands[m].at[k - 1], peer
    return srcs[m], lands[m].at[4 * x + 2 * y + c], peer


def _exchange_start(arrs, land_shapes, scatter, name):
    n = len(arrs)

    def body(*refs):
        srcs, lands = refs[:n], refs[n:2 * n]
        send_sems, recv_sems = refs[2 * n], refs[2 * n + 1]
        token = refs[-1]
        x, y, c, _ = _my_place()
        for m in range(n):
            for k in range(1, N_DEV):
                src, dst, peer = _exchange_refs(srcs, lands, m, k, x, y, c, scatter)
                pltpu.make_async_remote_copy(
                    src_ref=src, dst_ref=dst, send_sem=send_sems.at[7 * m + k - 1],
                    recv_sem=recv_sems.at[7 * m + k - 1], device_id=peer, device_id_type=MESH).start()
        token[...] = jnp.zeros_like(token)

    zones = [lax.empty(s_, a.dtype) for s_, a in zip(land_shapes, arrs)]
    outs = pl.pallas_call(
        body, name=name,
        out_shape=(pltpu.SemaphoreType.DMA((7 * n,)), pltpu.SemaphoreType.DMA((7 * n,)),
                   *[pltpu.HBM(a.shape, a.dtype) for a in arrs], *[pltpu.HBM(z.shape, z.dtype) for z in zones],
                   jax.ShapeDtypeStruct((8, 128), F32)),
        in_specs=[HBM_SPEC] * (2 * n),
        out_specs=(SEM_SPEC, SEM_SPEC, *[HBM_SPEC] * (2 * n), pl.BlockSpec(memory_space=pltpu.VMEM)),
        input_output_aliases={m: 2 + m for m in range(2 * n)},
        compiler_params=pltpu.CompilerParams(has_side_effects=pltpu.SideEffectType.DATAFLOW_SIDE_EFFECTING),
    )(*[pltpu.with_memory_space_constraint(a, pltpu.HBM) for a in arrs],
      *[pltpu.with_memory_space_constraint(z, pltpu.HBM) for z in zones])
    return outs[0], outs[1], list(outs[2:2 + n]), list(outs[2 + n:2 + 2 * n]), outs[-1]


def _exchange_wait(send_sems, recv_sems, arrs, zones, after, scatter, name):
    n = len(arrs)

    def body(*refs):
        srcs, lands = refs[:n], refs[n:2 * n]
        send_sems, recv_sems = refs[2 * n], refs[2 * n + 1]
        x, y, c, _ = _my_place()
        for m in range(n):
            for k in range(1, N_DEV):
                src, dst, peer = _exchange_refs(srcs, lands, m, k, x, y, c, scatter)
                cp = pltpu.make_async_remote_copy(
                    src_ref=src, dst_ref=dst, send_sem=send_sems.at[7 * m + k - 1],
                    recv_sem=recv_sems.at[7 * m + k - 1], device_id=peer, device_id_type=MESH)
                cp.wait_send()
                cp.wait_recv()

    outs = pl.pallas_call(
        body, name=name,
        out_shape=tuple(pltpu.HBM(a.shape, a.dtype) for a in list(arrs) + list(zones)),
        in_specs=[HBM_SPEC] * (2 * n) + [SEM_SPEC, SEM_SPEC, pl.BlockSpec(memory_space=pl.ANY)],
        out_specs=tuple([HBM_SPEC] * (2 * n)),
        input_output_aliases={m: m for m in range(2 * n)},
        compiler_params=pltpu.CompilerParams(has_side_effects=pltpu.SideEffectType.DATAFLOW_SIDE_EFFECTING),
    )(*arrs, *zones, send_sems, recv_sems, after)
    return list(outs[n:])


def _sum_parts(own, parts, tr):
    R, W = own.shape

    def body(own_ref, parts_ref, out_ref):
        acc = own_ref[...].astype(F32)
        for k in range(N_DEV - 1):
            acc = acc + parts_ref[k].astype(F32)
        out_ref[...] = acc

    return pl.pallas_call(
        body, name="sum_parts", grid=(R // tr,),
        in_specs=[pl.BlockSpec((tr, W), lambda i: (i, 0)), pl.BlockSpec((N_DEV - 1, tr, W), lambda i: (0, i, 0))],
        out_specs=pl.BlockSpec((tr, W), lambda i: (i, 0)),
        out_shape=jax.ShapeDtypeStruct((R, W), F32),
        compiler_params=_params("parallel"),
    )(own, parts)


def _all_reduce_small(v):
    Rn, Wd = v.shape

    def body(v_ref, out_ref, gat_ref, send_sems, recv_sems):
        x, y, c, _ = _my_place()
        me = 4 * x + 2 * y + c
        gat_ref[me] = v_ref[...]
        copies = []
        for k in range(1, N_DEV):
            fx, fy, fc = (k >> 2) & 1, (k >> 1) & 1, k & 1
            peer = (x ^ fx, y ^ fy, c ^ fc)
            cp = pltpu.make_async_remote_copy(
                src_ref=v_ref, dst_ref=gat_ref.at[me], send_sem=send_sems.at[k - 1], recv_sem=recv_sems.at[k - 1],
                device_id=peer, device_id_type=MESH)
            cp.start()
            copies.append(cp)
        for cp in copies:
            cp.wait_recv()
        for cp in copies:
            cp.wait_send()
        acc = gat_ref[0]
        for k in range(1, N_DEV):
            acc = acc + gat_ref[k]
        out_ref[...] = acc

    vm = pl.BlockSpec(memory_space=pltpu.VMEM)
    return pl.pallas_call(
        body, name="all_reduce_small", in_specs=[vm], out_specs=vm,
        out_shape=jax.ShapeDtypeStruct((Rn, Wd), F32),
        scratch_shapes=[pltpu.VMEM((N_DEV, Rn, Wd), F32), pltpu.SemaphoreType.DMA((7,)),
                        pltpu.SemaphoreType.DMA((7,))],
    )(v)


def _t5_bucket(rel):
    half = N_BUCKETS // 2
    max_exact = half // 2
    ret = jnp.where(rel > 0, half, 0)
    n = jnp.abs(rel)
    nf = jnp.maximum(n, 1).astype(F32)
    large = max_exact + (jnp.log(nf / max_exact) / math.log(MAX_DISTANCE / max_exact)
                         * (half - max_exact)).astype(jnp.int32)
    large = jnp.minimum(large, half - 1)
    return ret + jnp.where(n < max_exact, n, large)


def _band(R, d):
    W = BQ + 2 * R
    rel = jnp.arange(W)[None, :] - R - jnp.arange(BQ)[:, None]
    return _t5_bucket(rel * d), jnp.abs(rel) <= R


def _onehot(R, d):
    bkt, in_band = _band(R, d)
    return ((bkt.reshape(1, -1) == jnp.arange(128)[:, None]) & in_band.reshape(1, -1)).astype(BF16)


def _bias_expand(table_t, onehot):
    H = table_t.shape[0]
    K = onehot.shape[1]

    def body(t_ref, oh_ref, out_ref):
        oh = oh_ref[...]
        t = t_ref[...]
        hi = t.astype(BF16)
        r1 = t - hi.astype(F32)
        mid = r1.astype(BF16)
        low = (r1 - mid.astype(F32)).astype(BF16)
        marked = _dot(jnp.ones(t.shape, BF16), oh) > 0.5
        out_ref[...] = jnp.where(marked, _dot(hi, oh) + _dot(mid, oh) + _dot(low, oh), NEG)

    vm = pl.BlockSpec(memory_space=pltpu.VMEM)
    return pl.pallas_call(
        body, name="bias_expand", in_specs=[vm, vm], out_specs=vm,
        out_shape=jax.ShapeDtypeStruct((H, K), F32),
        compiler_params=pltpu.CompilerParams(vmem_limit_bytes=VMEM_LIMIT),
    )(table_t, onehot)


def _bias_matrix(table, R, d):
    table_t = jnp.pad(table.T, ((0, 0), (0, 128 - N_BUCKETS)))
    return _bias_expand(table_t, _onehot(R, d)).reshape(table.shape[1], BQ, BQ + 2 * R)


def _bias_variants(base, R):
    H, _, W = base.shape
    col = jnp.arange(W)
    before, after = col < R, col >= BQ + R
    masks = jnp.stack([jnp.zeros_like(before), before, after, before | after])
    v = jnp.where(masks[None, :, None, :], NEG, base[:, None])
    v = v.reshape(H // 2, 2, 4, BQ, W).transpose(0, 2, 1, 3, 4).reshape(H // 2, 4, 2 * BQ, W)
    return v, v.transpose(0, 1, 3, 2)


def _bias_grad(dbt, R, d):
    P, W, _ = dbt.shape
    dbm = dbt.reshape(P, W, 2, BQ).transpose(0, 2, 3, 1).reshape(2 * P, BQ * W)
    return _bias_reduce(_onehot(R, d), dbm)[:, :N_BUCKETS].T


def _deint(a, d):
    if d == 1:
        return a
    H, T, X = a.shape
    return a.reshape(H, T // d, d, X).transpose(0, 2, 1, 3).reshape(H * d, T // d, X)


def _reint(a, d):
    if d == 1:
        return a
    Hd, L, X = a.shape
    return a.reshape(Hd // d, d, L, X).transpose(0, 2, 1, 3).reshape(Hd // d, L * d, X)


def _pad_rows(a, R):
    return jnp.pad(a, ((0, 0), (R, R), (0, 0)))


def _tile2(gain):
    return jnp.concatenate([gain, gain])


ROW_W_O, ROW_GATE, ROW_QKV, ROW_PROJ, B_ROWS = 768, 896, 1024, 1312, 1344
BLK_W_O, BLK_GATE = ROW_W_O // 128, ROW_GATE // 128


def _pack_layer(wts, i):
    a = jnp.stack([wts["ffn1_w_in"][i], wts["ffn2_w_in"][i]])
    D = a.shape[1]
    b = jnp.concatenate([
        wts["ffn1_w_out"][i], wts["ffn2_w_out"][i],
        jnp.zeros((ROW_W_O - 2 * wts["ffn1_w_out"].shape[1], D), a.dtype),
        wts["w_o"][i], wts["w_ple_gate"][i], wts["w_qkv"][i].reshape(-1, D), wts["w_ple_proj"][i].reshape(-1, D)])
    return a, b


def _unpack_layer(sums, like):
    w_in1, w_out1, w_in2, rest = sums
    n_out = like["ffn1_w_out"].shape[1]
    rows = lambda r0, r1: rest[r0 - n_out:r1 - n_out]
    return {"ffn1_w_in": w_in1, "ffn2_w_in": w_in2, "ffn1_w_out": w_out1, "ffn2_w_out": rows(n_out, 2 * n_out),
            "w_o": rows(ROW_W_O, ROW_GATE), "w_ple_gate": rows(ROW_GATE, ROW_QKV),
            "w_qkv": rows(ROW_QKV, ROW_PROJ).reshape(like["w_qkv"].shape[1:]),
            "w_ple_proj": rows(ROW_PROJ, B_ROWS).reshape(like["w_ple_proj"].shape[1:])}


def _col_sharded(gb, r0, r1, rows):
    return gb[:, r0:r1].reshape(N_DEV, rows, -1).transpose(1, 0, 2).reshape(rows, -1)


def _to_col_shards(g):
    rows = g.shape[0]
    return g.reshape(rows, N_DEV, -1).transpose(1, 0, 2).reshape(N_DEV, -1, 1024)


def _layer_weights(ga, gb, p_dim):
    return dict(ga=ga, gb=gb, w_qkv=_col_sharded(gb, ROW_QKV, ROW_PROJ, ga.shape[2]),
                w_proj=_col_sharded(gb, ROW_PROJ, B_ROWS, p_dim))


def _layer_fwd(x, p, w, sm, i, target, tm, biases, dep=None):
    ga, gb = w["ga"], w["gb"]
    saved = {}
    saved["x0"] = x
    x1, saved["h1"], saved["zg1"], saved["zu1"], saved["s1"] = _ffn_fwd(
        x, sm["norm_ffn1"][i][None], ga, gb, 0, tm, dep)
    saved["x1"] = x1
    qkv, saved["hm"] = _qkv_fwd(x1, sm["norm_mix"][i][None], w["w_qkv"], tm)
    saved["qkv"] = qkv
    gains2 = jnp.stack([_tile2(sm[k][i]) for k in ("q_norm_a", "k_norm_a", "q_norm_b", "k_norm_b")])
    saved["gains2"] = gains2
    qa, ka, va, qb, kb, vb = _attn_prep(qkv, gains2, tm)
    no_sink = jnp.full((8,), NEG, F32)
    branches = []
    outs = []
    for (R, d), bias in zip(DILATED, biases[:3]):
        qd, kd, vd = _deint(qa, d), _pad_rows(_deint(ka, d), R), _pad_rows(_deint(va, d), R)
        sink = jnp.tile(no_sink, d)
        o, lse = _attn_fwd(qd, kd, vd, bias[0], sink, R, 1, d)
        branches.append((qd, kd, vd, bias, sink, R, d))
        outs += [_reint(o, d), _reint(lse, d)]
    bias_b = biases[3]
    kbp, vbp = _pad_rows(kb, SWA_RADIUS), _pad_rows(vb, SWA_RADIUS)
    sink_b = sm["sink_b"][i]
    ob, lb = _attn_fwd(qb, kbp, vbp, bias_b[0], sink_b, SWA_RADIUS, 2, 1)
    oa, la, o_cat = _attn_merge(*outs, ob, tm)
    saved.update(branches=branches, b=(qb, kbp, vbp, bias_b, sink_b), oa=oa, la=la, ob=ob, lb=lb, o_cat=o_cat)
    x2 = _oproj_fwd(x1, o_cat, gb, BLK_W_O, tm)
    saved["x2"] = x2
    x3, saved["h2"], saved["zg2"], saved["zu2"], saved["s2"] = _ffn_fwd(
        x2, sm["norm_ffn2"][i][None], ga, gb, 1, tm)
    saved["x3"] = x3
    res = _ple_fwd(x3, sm["norm_ple"][i][None], gb, BLK_GATE, p, w["w_proj"], target, tm)
    y, saved["hp"], saved["gate"], saved["pp"], saved["pb"] = res[:5]
    loss = res[5] if target is not None else None
    return y, loss, saved


def _layer_bwd(dy, w, sm, i, sv, tm, dep=None, on_early=None):
    ga, gb = w["ga"], w["gb"]
    gs = {}
    D = dy.shape[1]
    dgl, dpp = _ple_bwd(dy, sv["gate"], sv["pp"], tm, dep)
    d_gate = _matmul_tn(sv["hp"], dgl, D, 2 * tm)
    d_proj = _matmul_tn(sv["pb"], dpp, D, 2 * tm)
    dx3, gs["norm_ple"] = _dense_norm_bwd(dy, dgl, gb, BLK_GATE, sv["x3"], sm["norm_ple"][i][None], tm)
    dx2, dyb, dzg, dzu, gs["norm_ffn2"] = _ffn_bwd(dx3, sv["x2"], sm["norm_ffn2"][i][None], sv["zg2"], sv["zu2"],
                                                   ga, gb, 1, tm)
    dwin2, dwo2 = _ffn_dw_in(sv["h2"], dzg, dzu, 2 * tm), _ffn_dw_out(sv["s2"], dyb, 2 * tm)
    dx2b, do = _oproj_bwd(dx2, gb, BLK_W_O, tm)
    d_wo = _matmul_tn(sv["o_cat"], dx2b, D, 2 * tm)
    do_a, do_b = do[:4], do[4:]
    dqa, dka, dva, dbias = [], [], [], []
    for qd, kd, vd, bias, sink, R, d in sv["branches"]:
        dq, dk, dv, dbm, _ = _attn_bwd(qd, kd, vd, bias[1], sink, _deint(sv["oa"], d), _deint(sv["la"], d),
                                        _deint(do_a, d), R, 1, d)
        L = qd.shape[1]
        dqa.append(_reint(dq, d))
        dka.append(_reint(dk[:, R:R + L], d))
        dva.append(_reint(dv[:, R:R + L], d))
        dbias.append(dbm)
    qb, kbp, vbp, bias_b, sink_b = sv["b"]
    dqb, dkb, dvb, dbm_b, dsink = _attn_bwd(qb, kbp, vbp, bias_b[1], sink_b, sv["ob"], sv["lb"], do_b,
                                            SWA_RADIUS, 2, 1)
    T = qb.shape[1]
    gs["rel_bias"] = dbias + [dbm_b]
    gs["sink_b"] = jnp.sum(dsink[:, 0].reshape(-1, 2, BQ), axis=2).reshape(-1)
    dqkv, dgains2 = _attn_post(sv["qkv"], sv["gains2"], dqa, dka, dva, dqb,
                               dkb[:, SWA_RADIUS:SWA_RADIUS + T], dvb[:, SWA_RADIUS:SWA_RADIUS + T], tm // 2)
    dgains = dgains2[:, :HEAD_DIM] + dgains2[:, HEAD_DIM:]
    for k, name in enumerate(("q_norm_a", "k_norm_a", "q_norm_b", "k_norm_b")):
        gs[name] = dgains[k]
    d_qkv = _matmul_tn(sv["hm"], dqkv, dqkv.shape[1] // 2, 2 * tm)
    half = dwo2.shape[1] // 2
    early = [dwin2, jnp.concatenate([
        dwo2.reshape(N_DEV, half, D), jnp.zeros((N_DEV, ROW_W_O - 2 * half, D), BF16),
        d_wo.reshape(N_DEV, -1, D), d_gate.reshape(N_DEV, -1, D), _to_col_shards(d_qkv), _to_col_shards(d_proj)],
        axis=1)]
    token = None if on_early is None else on_early(early)
    dx1, gs["norm_mix"] = _dense_norm_bwd(dx2, dqkv, w["w_qkv"], None, sv["x1"], sm["norm_mix"][i][None], tm)
    dx0, dyb, dzg, dzu, gs["norm_ffn1"] = _ffn_bwd(dx1, sv["x0"], sm["norm_ffn1"][i][None], sv["zg1"], sv["zu1"],
                                                   ga, gb, 0, tm, token)
    dwin1, dwo1 = _ffn_dw_in(sv["h1"], dzg, dzu, 2 * tm), _ffn_dw_out(sv["s1"], dyb, 2 * tm)
    late = [dwin1, dwo1.reshape(N_DEV, half, D)]
    return dx0, (late, early), gs


def _bias_matrices(rel_bias):
    biases = [_bias_variants(_bias_matrix(rel_bias[:, :8], R, d), R) for R, d in DILATED]
    biases.append(_bias_variants(_bias_matrix(rel_bias[:, 8:], SWA_RADIUS, 1), SWA_RADIUS))
    return biases


def _stack_small(per_layer):
    small = {}
    for k, v in per_layer.items():
        if k == "rel_bias":
            per_branch = [sum(parts) for parts in zip(*v.values())]
            drel_a = sum(_bias_grad(t, R, d) for t, (R, d) in zip(per_branch[:3], DILATED))
            small[k] = jnp.concatenate([drel_a, _bias_grad(per_branch[3], SWA_RADIUS, 1)], axis=1)
        else:
            small[k] = jnp.stack([v[i].reshape(-1) for i in sorted(v)])
    return small


TM = 512
SUM_TILES = (512, 352, 512, 496)


def _pack_small(d, extra=None):
    parts = [d[k].reshape(-1) for k in SMALL]
    if extra is not None:
        parts.append(extra.reshape(-1))
    flat = jnp.concatenate(parts)
    return jnp.pad(flat, (0, SMALL_ROWS * 128 - flat.shape[0])).reshape(SMALL_ROWS, 128)


def _unpack_small(buf, like):
    flat = buf.reshape(-1)
    out, off = {}, 0
    for k in SMALL:
        n = like[k].size
        out[k] = flat[off:off + n].reshape(like[k].shape)
        off += n
    return out, flat[off]


def kernel(x, p, rel_bias, norm_ffn1, ffn1_w_in, ffn1_w_out, norm_mix, w_qkv, q_norm_a, k_norm_a, q_norm_b, k_norm_b, sink_b, w_o, norm_ffn2, ffn2_w_in, ffn2_w_out, norm_ple, w_ple_gate, w_ple_proj, loss_target, m_rel_bias, m_norm_ffn1, m_ffn1_w_in, m_ffn1_w_out, m_norm_mix, m_w_qkv, m_q_norm_a, m_k_norm_a, m_q_norm_b, m_k_norm_b, m_sink_b, m_w_o, m_norm_ffn2, m_ffn2_w_in, m_ffn2_w_out, m_norm_ple, m_w_ple_gate, m_w_ple_proj, v_rel_bias, v_norm_ffn1, v_ffn1_w_in, v_ffn1_w_out, v_norm_mix, v_w_qkv, v_q_norm_a, v_k_norm_a, v_q_norm_b, v_k_norm_b, v_sink_b, v_w_o, v_norm_ffn2, v_ffn2_w_in, v_ffn2_w_out, v_norm_ple, v_w_ple_gate, v_w_ple_proj):
    wts = dict(rel_bias=rel_bias, norm_ffn1=norm_ffn1, ffn1_w_in=ffn1_w_in, ffn1_w_out=ffn1_w_out,
               norm_mix=norm_mix, w_qkv=w_qkv, q_norm_a=q_norm_a, k_norm_a=k_norm_a, q_norm_b=q_norm_b,
               k_norm_b=k_norm_b, sink_b=sink_b, w_o=w_o, norm_ffn2=norm_ffn2, ffn2_w_in=ffn2_w_in,
               ffn2_w_out=ffn2_w_out, norm_ple=norm_ple, w_ple_gate=w_ple_gate, w_ple_proj=w_ple_proj)
    mom = dict(rel_bias=m_rel_bias, norm_ffn1=m_norm_ffn1, ffn1_w_in=m_ffn1_w_in, ffn1_w_out=m_ffn1_w_out,
               norm_mix=m_norm_mix, w_qkv=m_w_qkv, q_norm_a=m_q_norm_a, k_norm_a=m_k_norm_a, q_norm_b=m_q_norm_b,
               k_norm_b=m_k_norm_b, sink_b=m_sink_b, w_o=m_w_o, norm_ffn2=m_norm_ffn2, ffn2_w_in=m_ffn2_w_in,
               ffn2_w_out=m_ffn2_w_out, norm_ple=m_norm_ple, w_ple_gate=m_w_ple_gate, w_ple_proj=m_w_ple_proj)
    var = dict(rel_bias=v_rel_bias, norm_ffn1=v_norm_ffn1, ffn1_w_in=v_ffn1_w_in, ffn1_w_out=v_ffn1_w_out,
               norm_mix=v_norm_mix, w_qkv=v_w_qkv, q_norm_a=v_q_norm_a, k_norm_a=v_k_norm_a, q_norm_b=v_q_norm_b,
               k_norm_b=v_k_norm_b, sink_b=v_sink_b, w_o=v_w_o, norm_ffn2=v_norm_ffn2, ffn2_w_in=v_ffn2_w_in,
               ffn2_w_out=v_ffn2_w_out, norm_ple=v_norm_ple, w_ple_gate=v_w_ple_gate, w_ple_proj=v_w_ple_proj)
    sm = {k: wts[k] for k in SMALL}
    p_dim = p.shape[-1]
    me = 4 * lax.axis_index("x") + 2 * lax.axis_index("y") + lax.axis_index("c")
    packed = []
    for i in range(2):
        a, b = _pack_layer(wts, i)
        packed.append([a.reshape(-1, a.shape[-1]).astype(BF16), b.astype(BF16)])
    a_shape = (2, ffn1_w_in.shape[1], ffn1_w_in.shape[2])

    def weights_of(zones):
        return _layer_weights(zones[0].reshape((N_DEV,) + a_shape), zones[1], p_dim)

    w0 = weights_of([_all_gather(t) for t in packed[0]])
    zone_shapes = [(N_DEV,) + t.shape for t in packed[1]]
    ssem, rsem, thru, zones, token = _exchange_start(packed[1], zone_shapes, False, "gather_start")
    biases = _bias_matrices(rel_bias)
    x1, _, sv0 = _layer_fwd(x[0], p[0, 0], w0, sm, 0, None, TM, biases, dep=token)
    zones = _exchange_wait(ssem, rsem, thru, zones, x1, False, "gather_wait")
    w1 = weights_of([lax.dynamic_update_index_in_dim(z, t, me, 0) for z, t in zip(zones, packed[1])])
    dy, loss, sv1 = _layer_fwd(x1, p[1, 0], w1, sm, 1, loss_target[0], TM, biases)

    def slots_for(arrs):
        return [(N_DEV - 1,) + t.shape[1:] for t in arrs]

    dx1, (late1, early1), gs1 = _layer_bwd(dy, w1, sm, 1, sv1, TM)
    g1 = late1 + early1
    ex1 = _exchange_start(g1, slots_for(g1), True, "scatter_start")
    held = {}

    def on_early(early):
        held["slots1"] = _exchange_wait(*ex1[:4], early[1], True, "scatter_wait")
        held["ex0"] = _exchange_start(early, slots_for(early), True, "scatter_early_start")
        return held["ex0"][4]

    dx, (late0, early0), gs0 = _layer_bwd(dx1, w0, sm, 0, sv0, TM, dep=ex1[4], on_early=on_early)
    slots0 = _exchange_wait(*held["ex0"][:4], late0[0], True, "scatter_early_wait")

    def summed(arrs, slots, tiles):
        return [_sum_parts(lax.dynamic_index_in_dim(t, me, 0, keepdims=False), s_, tr)
                for t, s_, tr in zip(arrs, slots, tiles)]

    r1 = summed(g1, held["slots1"], SUM_TILES)
    r0 = [_reduce_scatter(t, tr) for t, tr in zip(late0, SUM_TILES[:2])] + summed(early0, slots0, SUM_TILES[2:])

    gsmall = _stack_small({k: {0: gs0[k], 1: gs1[k]} for k in gs0})
    small_sum, loss_sum = _unpack_small(_all_reduce_small(_pack_small(gsmall, loss[0, :1])), sm)

    grads = dict(small_sum)
    layers = [_unpack_layer(r, wts) for r in (r0, r1)]
    for k in BIG:
        grads[k] = jnp.stack([layers[0][k], layers[1][k]])

    delta, new_m, new_v = {}, {}, {}
    for k in BIG:
        delta[k], new_m[k], new_v[k] = _adamw(wts[k], grads[k], mom[k], var[k])
    zeros = {k: jnp.zeros_like(wts[k]) for k in SMALL}
    ds, ms, vs = _adamw(_pack_small(wts), _pack_small(small_sum), _pack_small(mom), _pack_small(var))
    for packed, dst in ((ds, delta), (ms, new_m), (vs, new_v)):
        dst.update(_unpack_small(packed, zeros)[0])

    return (loss_sum, dx[None], *[grads[k] for k in WEIGHTS], *[delta[k] for k in WEIGHTS],
            *[new_m[k] for k in WEIGHTS], *[new_v[k] for k in WEIGHTS])
```

```python
import functools
import math

import jax
import jax.numpy as jnp
from jax import lax
from jax.experimental import pallas as pl
from jax.experimental.pallas import tpu as pltpu

F32 = jnp.float32
BF16 = jnp.bfloat16

N_DEV = 8
HEAD_DIM = 64
PAIR = 2 * HEAD_DIM
BQ = 128
N_BUCKETS = 32
MAX_DISTANCE = 1024
DILATED = ((64, 1), (64, 4), (64, 16))
SWA_RADIUS = 128
EPS = 1e-6
NEG = -1e30
ADAM_LR, ADAM_B1, ADAM_B2, ADAM_EPS, ADAM_WD, ADAM_STEP = 0.001, 0.9, 0.999, 1e-08, 0.01, 10
VMEM_LIMIT = 56 * 1024 * 1024
AXES = ("x", "y", "c")
MESH = pl.DeviceIdType.MESH

BIG = ("ffn1_w_in", "ffn1_w_out", "w_qkv", "w_o", "ffn2_w_in", "ffn2_w_out", "w_ple_gate", "w_ple_proj")
SMALL = ("rel_bias", "norm_ffn1", "norm_mix", "q_norm_a", "k_norm_a", "q_norm_b", "k_norm_b", "sink_b",
         "norm_ffn2", "norm_ple")
WEIGHTS = ("rel_bias", "norm_ffn1", "ffn1_w_in", "ffn1_w_out", "norm_mix", "w_qkv", "q_norm_a", "k_norm_a",
           "q_norm_b", "k_norm_b", "sink_b", "w_o", "norm_ffn2", "ffn2_w_in", "ffn2_w_out", "norm_ple",
           "w_ple_gate", "w_ple_proj")
SMALL_ROWS = 96


def _params(*sem):
    return pltpu.CompilerParams(dimension_semantics=sem, vmem_limit_bytes=VMEM_LIMIT)


def _dot(a, b):
    return jnp.dot(a, b, preferred_element_type=F32)


def _dot_nt(a, b):
    return lax.dot_general(a, b, (((1,), (1,)), ((), ())), preferred_element_type=F32)


def _dot_tn(a, b):
    return lax.dot_general(a, b, (((0,), (0,)), ((), ())), preferred_element_type=F32)


def _sigmoid(x):
    return 1.0 / (1.0 + jnp.exp(-x))


def _rstd(xv):
    return lax.rsqrt(jnp.mean(xv * xv, axis=-1, keepdims=True) + EPS)


def _norm_bwd(dh, xv, gv):
    r = _rstd(xv)
    xn = xv * r
    dg = jnp.sum(dh * xn, axis=0, keepdims=True)
    dxn = dh * gv
    dx = r * (dxn - xn * jnp.mean(dxn * xn, axis=-1, keepdims=True))
    return dx, dg


def _lo_mask(shape):
    return lax.broadcasted_iota(jnp.int32, shape, len(shape) - 1) < HEAD_DIM


def _half_sum(t, lo):
    s0 = jnp.sum(jnp.where(lo, t, 0.0), axis=1, keepdims=True)
    s1 = jnp.sum(jnp.where(lo, 0.0, t), axis=1, keepdims=True)
    return jnp.where(lo, s0, s1)


FFN_PARTS = 2


def _ffn_weight_specs(f, nj, D, C):
    return [pl.BlockSpec((None, None, D, C), lambda i, j: (j, f, 0, 0)),
            pl.BlockSpec((None, None, D, C), lambda i, j: (j + nj, f, 0, 0)),
            pl.BlockSpec((2, C // 2, D), lambda i, j: (j, f, 0))]


def _with_dep(body, dep, in_specs, args):
    if dep is None:
        return body, in_specs, args

    def body_after(dep_ref, *refs):
        body(*refs)

    return body_after, [pl.BlockSpec(memory_space=pl.ANY)] + in_specs, [dep] + args


def _ffn_fwd(x, g, ga, gb, f, tm, dep=None):
    T, D = x.shape
    nj, C = ga.shape[0] // 2, ga.shape[3]

    def body(x_ref, g_ref, wg_ref, wu_ref, wo_ref, xo_ref, h_ref, zg_ref, zu_ref, s_ref, h_scr, acc):
        j = pl.program_id(1)

        @pl.when(j == 0)
        def _():
            xv = x_ref[...]
            hb = (xv * _rstd(xv) * g_ref[...]).astype(BF16)
            h_scr[...] = hb
            h_ref[...] = hb
            acc[...] = jnp.zeros_like(acc)

        wo = wo_ref[...].reshape(C, D)
        for part in range(FFN_PARTS):
            sl = pl.ds(part * (tm // FFN_PARTS), tm // FFN_PARTS)
            hb = h_scr[sl, :]
            gt = _dot(hb, wg_ref[...])
            up = _dot(hb, wu_ref[...])
            s = (gt * _sigmoid(gt) * up).astype(BF16)
            zg_ref[sl, :] = gt.astype(BF16)
            zu_ref[sl, :] = up.astype(BF16)
            s_ref[sl, :] = s
            acc[sl, :] += _dot(s, wo)

        @pl.when(j == nj - 1)
        def _():
            xo_ref[...] = x_ref[...] + 0.5 * acc[...]

    tok = pl.BlockSpec((tm, D), lambda i, j: (i, 0))
    chunk = pl.BlockSpec((None, tm, C), lambda i, j: (j, i, 0))
    in_specs = [tok, pl.BlockSpec((1, D), lambda i, j: (0, 0))] + _ffn_weight_specs(f, nj, D, C)
    body, in_specs, args = _with_dep(body, dep, in_specs, [x, g, ga, ga, gb])
    return pl.pallas_call(
        body, name="ffn_fwd", grid=(T // tm, nj),
        in_specs=in_specs,
        out_specs=[tok, tok, chunk, chunk, chunk],
        out_shape=[jax.ShapeDtypeStruct((T, D), F32), jax.ShapeDtypeStruct((T, D), BF16),
                   jax.ShapeDtypeStruct((nj, T, C), BF16), jax.ShapeDtypeStruct((nj, T, C), BF16),
                   jax.ShapeDtypeStruct((nj, T, C), BF16)],
        scratch_shapes=[pltpu.VMEM((tm, D), BF16), pltpu.VMEM((tm, D), F32)],
        compiler_params=_params("parallel", "arbitrary"),
    )(*args)


def _ffn_bwd(dxo, x, g, zg, zu, ga, gb, f, tm, dep=None):
    T, D = x.shape
    nj, C = ga.shape[0] // 2, ga.shape[3]

    def body(dxo_ref, x_ref, g_ref, zg_ref, zu_ref, wg_ref, wu_ref, wo_ref,
             dx_ref, dy_ref, dzg_ref, dzu_ref, dgn_ref, dy_scr, acc):
        i, j = pl.program_id(0), pl.program_id(1)

        @pl.when(j == 0)
        def _():
            dyb = (0.5 * dxo_ref[...]).astype(BF16)
            dy_scr[...] = dyb
            dy_ref[...] = dyb
            acc[...] = jnp.zeros_like(acc)

        wo = wo_ref[...].reshape(C, D)
        for part in range(FFN_PARTS):
            sl = pl.ds(part * (tm // FFN_PARTS), tm // FFN_PARTS)
            ds = _dot_nt(dy_scr[sl, :], wo)
            gt = zg_ref[sl, :].astype(F32)
            up = zu_ref[sl, :].astype(F32)
            sg = _sigmoid(gt)
            dgt = (ds * up * (sg * (1.0 + gt * (1.0 - sg)))).astype(BF16)
            dup = (ds * (gt * sg)).astype(BF16)
            dzg_ref[sl, :] = dgt
            dzu_ref[sl, :] = dup
            acc[sl, :] += _dot_nt(dgt, wg_ref[...]) + _dot_nt(dup, wu_ref[...])

        @pl.when(j == nj - 1)
        def _():
            dx, dg = _norm_bwd(acc[...], x_ref[...], g_ref[...])
            dx_ref[...] = dxo_ref[...] + dx

            @pl.when(i == 0)
            def _():
                dgn_ref[...] = dg

            @pl.when(i > 0)
            def _():
                dgn_ref[...] += dg

    tok = pl.BlockSpec((tm, D), lambda i, j: (i, 0))
    chunk = pl.BlockSpec((None, tm, C), lambda i, j: (j, i, 0))
    row = pl.BlockSpec((1, D), lambda i, j: (0, 0))
    in_specs = [tok, tok, row, chunk, chunk] + _ffn_weight_specs(f, nj, D, C)
    body, in_specs, args = _with_dep(body, dep, in_specs, [dxo, x, g, zg, zu, ga, ga, gb])
    return pl.pallas_call(
        body, name="ffn_bwd", grid=(T // tm, nj),
        in_specs=in_specs,
        out_specs=[tok, tok, chunk, chunk, row],
        out_shape=[jax.ShapeDtypeStruct((T, D), F32), jax.ShapeDtypeStruct((T, D), BF16),
                   jax.ShapeDtypeStruct((nj, T, C), BF16), jax.ShapeDtypeStruct((nj, T, C), BF16),
                   jax.ShapeDtypeStruct((1, D), F32)],
        scratch_shapes=[pltpu.VMEM((tm, D), BF16), pltpu.VMEM((tm, D), F32)],
        compiler_params=_params("arbitrary", "arbitrary"),
    )(*args)


def _ffn_dw(h, dzg, dzu, s, dy, tk):
    T, D = h.shape
    nj, C = s.shape[0], s.shape[2]
    nk = T // tk

    def body(h_ref, dzg_ref, dzu_ref, s_ref, dy_ref, dwin_ref, dwo_ref, ag, au, ao):
        k = pl.program_id(1)

        @pl.when(k == 0)
        def _():
            ag[...] = jnp.zeros_like(ag)
            au[...] = jnp.zeros_like(au)
            ao[...] = jnp.zeros_like(ao)

        hb = h_ref[...]
        ag[...] += _dot_tn(hb, dzg_ref[...])
        au[...] += _dot_tn(hb, dzu_ref[...])
        ao[...] += _dot_tn(s_ref[...], dy_ref[...])

        @pl.when(k == nk - 1)
        def _():
            dwin_ref[0] = ag[...].astype(BF16)
            dwin_ref[1] = au[...].astype(BF16)
            dwo_ref[...] = ao[...].astype(BF16)

    tok = pl.BlockSpec((tk, D), lambda j, k: (k, 0))
    chunk = pl.BlockSpec((None, tk, C), lambda j, k: (j, k, 0))
    dwin, dwo = pl.pallas_call(
        body, name="ffn_dw", grid=(nj, nk),
        in_specs=[tok, chunk, chunk, chunk, tok],
        out_specs=[pl.BlockSpec((2, None, D, C), lambda j, k: (0, j, 0, 0)),
                   pl.BlockSpec((None, C, D), lambda j, k: (j, 0, 0))],
        out_shape=[jax.ShapeDtypeStruct((2, nj, D, C), BF16), jax.ShapeDtypeStruct((nj, C, D), BF16)],
        scratch_shapes=[pltpu.VMEM((D, C), F32), pltpu.VMEM((D, C), F32), pltpu.VMEM((C, D), F32)],
        compiler_params=_params("parallel", "arbitrary"),
    )(h, dzg, dzu, s, dy)
    return dwin.reshape(2 * nj, D, C), dwo


def _matmul_tn(a, b, tn, tk):
    T, Ka = a.shape
    N = b.shape[1]
    nk = T // tk

    def body(a_ref, b_ref, o_ref, acc):
        k = pl.program_id(1)

        @pl.when(k == 0)
        def _():
            acc[...] = jnp.zeros_like(acc)

        acc[...] += _dot_tn(a_ref[...], b_ref[...])

        @pl.when(k == nk - 1)
        def _():
            o_ref[...] = acc[...].astype(BF16)

    return pl.pallas_call(
        body, name="matmul_tn", grid=(N // tn, nk),
        in_specs=[pl.BlockSpec((tk, Ka), lambda n, k: (k, 0)), pl.BlockSpec((tk, tn), lambda n, k: (k, n))],
        out_specs=pl.BlockSpec((Ka, tn), lambda n, k: (0, n)),
        out_shape=jax.ShapeDtypeStruct((Ka, N), BF16),
        scratch_shapes=[pltpu.VMEM((Ka, tn), F32)],
        compiler_params=_params("parallel", "arbitrary"),
    )(a, b)


def _qkv_fwd(x, g, w, tm):
    T, D = x.shape
    N = w.shape[1]

    def body(x_ref, g_ref, w_ref, o_ref, h_ref):
        xv = x_ref[...]
        hb = (xv * _rstd(xv) * g_ref[...]).astype(BF16)
        h_ref[...] = hb
        o_ref[...] = _dot(hb, w_ref[...])

    return pl.pallas_call(
        body, name="qkv_fwd", grid=(T // tm,),
        in_specs=[pl.BlockSpec((tm, D), lambda i: (i, 0)), pl.BlockSpec((1, D), lambda i: (0, 0)),
                  pl.BlockSpec((D, N), lambda i: (0, 0))],
        out_specs=[pl.BlockSpec((tm, N), lambda i: (i, 0)), pl.BlockSpec((tm, D), lambda i: (i, 0))],
        out_shape=[jax.ShapeDtypeStruct((T, N), F32), jax.ShapeDtypeStruct((T, D), BF16)],
        compiler_params=_params("parallel"),
    )(x, g, w)


def _attn_prep(qkv, gains2, tm):
    T = qkv.shape[0]
    scale = HEAD_DIM ** -0.5

    def body(qkv_ref, g_ref, qa_ref, ka_ref, va_ref, qb_ref, kb_ref, vb_ref):
        lo = _lo_mask((tm, PAIR))

        def normed(c, gi, mult):
            xv = qkv_ref[:, c * PAIR:(c + 1) * PAIR]
            r = lax.rsqrt(_half_sum(xv * xv, lo) * (1.0 / HEAD_DIM) + EPS)
            y = xv * r * g_ref[gi:gi + 1, :]
            return y * mult if mult != 1.0 else y

        def both_halves(v):
            sw = pltpu.roll(v, HEAD_DIM, 1)
            return jnp.where(lo, v, sw), jnp.where(lo, sw, v)

        for c in range(4):
            qa_ref[c] = normed(c, 0, scale).astype(BF16)
            ka_ref[c] = normed(4 + c, 1, 1.0).astype(BF16)
            va_ref[c] = qkv_ref[:, (8 + c) * PAIR:(9 + c) * PAIR].astype(BF16)
            qb_ref[c] = normed(12 + c, 2, scale).astype(BF16)
        k0, k1 = both_halves(normed(16, 3, 1.0))
        kb_ref[0] = k0.astype(BF16)
        kb_ref[1] = k1.astype(BF16)
        v0, v1 = both_halves(qkv_ref[:, 17 * PAIR:18 * PAIR])
        vb_ref[0] = v0.astype(BF16)
        vb_ref[1] = v1.astype(BF16)

    four = pl.BlockSpec((4, tm, PAIR), lambda i: (0, i, 0))
    two = pl.BlockSpec((2, tm, PAIR), lambda i: (0, i, 0))
    s4 = jax.ShapeDtypeStruct((4, T, PAIR), BF16)
    s2 = jax.ShapeDtypeStruct((2, T, PAIR), BF16)
    return pl.pallas_call(
        body, name="attn_prep", grid=(T // tm,),
        in_specs=[pl.BlockSpec((tm, qkv.shape[1]), lambda i: (i, 0)), pl.BlockSpec((4, PAIR), lambda i: (0, 0))],
        out_specs=[four, four, four, four, two, two],
        out_shape=[s4, s4, s4, s4, s2, s2],
        compiler_params=_params("parallel"),
    )(qkv, gains2)


def _loop_blocks(nb, body, init, per_iter):
    u = math.gcd(nb, per_iter)

    def outer(i, carry):
        for k in range(u):
            carry = body(i * u + k, carry)
        return carry

    return lax.fori_loop(0, nb // u, outer, init)


def _edge_variant(b, nb):
    return (b == 0).astype(jnp.int32) + 2 * (b == nb - 1).astype(jnp.int32)


def _stack_heads(v, lo):
    z = jnp.zeros_like(v)
    return jnp.concatenate([jnp.where(lo, v, z), jnp.where(lo, z, v)], axis=0)


def _unstack_heads(v2, lo):
    return jnp.where(lo, v2[:BQ], v2[BQ:])


def _row_vector(v, lo):
    r = lax.broadcasted_iota(jnp.int32, (BQ, PAIR), 0)
    ln = lax.broadcasted_iota(jnp.int32, (BQ, PAIR), 1)
    diag = (ln % HEAD_DIM) == (r % HEAD_DIM)
    top = jnp.sum(jnp.where(diag & (r < HEAD_DIM), v, 0.0), axis=0, keepdims=True)
    bot = jnp.sum(jnp.where(diag & (r >= HEAD_DIM), v, 0.0), axis=0, keepdims=True)
    top8, bot8 = jnp.broadcast_to(top, (8, PAIR)), jnp.broadcast_to(bot, (8, PAIR))
    lo8 = _lo_mask((8, PAIR))
    head0 = jnp.where(lo8, top8, pltpu.roll(bot8, HEAD_DIM, 1))
    head1 = jnp.where(lo8, pltpu.roll(top8, HEAD_DIM, 1), bot8)
    return jnp.concatenate([head0, head1], axis=1)[:1]


def _units_per_step(nb, pairs_per_kv):
    return max(1, 16 // nb) if pairs_per_kv == 1 else 1


def _attn_fwd(q, kp, vp, bias4, sink, R, pairs_per_kv, pairs_per_bias):
    N, L, _ = q.shape
    W = BQ + 2 * R
    nb = L // BQ
    G = _units_per_step(nb, pairs_per_kv)

    def body(sink_ref, q_ref, k_ref, v_ref, bias_ref, o_ref, lse_ref):
        n = pl.program_id(0)
        lo_q = _lo_mask((BQ, PAIR))
        first = lax.broadcasted_iota(jnp.int32, (2 * BQ, 1), 0) < BQ

        def blk(f, carry):
            g, b = f // nb, f % nb
            u = n * G + g
            sk = jnp.where(first, sink_ref[2 * u], sink_ref[2 * u + 1])
            q0 = pl.multiple_of(b * BQ, BQ)
            q2 = _stack_heads(q_ref[g, pl.ds(q0, BQ), :], lo_q)
            kw = k_ref[g, pl.ds(q0, W), :]
            vw = v_ref[g, pl.ds(q0, W), :]
            s = _dot_nt(q2, kw) + bias_ref[_edge_variant(b, nb)]
            m = jnp.maximum(jnp.max(s, axis=1, keepdims=True), sk)
            p = jnp.exp(s - m)
            l = jnp.sum(p, axis=1, keepdims=True) + jnp.exp(sk - m)
            o2 = _dot(p.astype(BF16), vw) / l
            o_ref[g, pl.ds(q0, BQ), :] = _unstack_heads(o2, lo_q)
            lse_ref[g, pl.ds(q0, BQ), :] = _unstack_heads(jnp.broadcast_to(m + jnp.log(l), (2 * BQ, PAIR)), lo_q)
            return carry

        _loop_blocks(G * nb, blk, 0, 4)

    qspec = pl.BlockSpec((G, L, PAIR), lambda n: (n, 0, 0))
    kspec = pl.BlockSpec((G, L + 2 * R, PAIR), lambda n: (n // pairs_per_kv, 0, 0))
    return pl.pallas_call(
        body, name="attn_fwd", grid=(N // G,),
        in_specs=[pl.BlockSpec(memory_space=pltpu.SMEM), qspec, kspec, kspec,
                  pl.BlockSpec((None, 4, 2 * BQ, W), lambda n: (n * G // pairs_per_bias, 0, 0, 0))],
        out_specs=[qspec, qspec],
        out_shape=[jax.ShapeDtypeStruct((N, L, PAIR), F32), jax.ShapeDtypeStruct((N, L, PAIR), F32)],
        compiler_params=_params("parallel"),
    )(sink, q, kp, vp, bias4)


def _attn_bwd(q, kp, vp, bias4t, sink, o, lse, do, R, pairs_per_kv, pairs_per_bias):
    N, L, _ = q.shape
    Nk = kp.shape[0]
    Pb = bias4t.shape[0]
    W = BQ + 2 * R
    nb = L // BQ
    G = _units_per_step(nb, pairs_per_kv)

    def body(sink_ref, q_ref, k_ref, v_ref, bias_ref, o_ref, lse_ref, do_ref,
             dq_ref, dk_ref, dv_ref, dbias_ref, dsink_ref):
        n = pl.program_id(0)
        lo_q = _lo_mask((BQ, PAIR))
        first = lax.broadcasted_iota(jnp.int32, (1, 2 * BQ), 1) < BQ
        dsink_ref[...] = jnp.zeros_like(dsink_ref)

        @pl.when(n % pairs_per_kv == 0)
        def _():
            dk_ref[...] = jnp.zeros_like(dk_ref)
            dv_ref[...] = jnp.zeros_like(dv_ref)

        @pl.when((n * G) % pairs_per_bias == 0)
        def _():
            dbias_ref[...] = jnp.zeros_like(dbias_ref)

        def blk(f, carry):
            g, b = f // nb, f % nb
            u = n * G + g
            sk = jnp.where(first, sink_ref[2 * u], sink_ref[2 * u + 1])
            q0 = pl.multiple_of(b * BQ, BQ)
            q2 = _stack_heads(q_ref[g, pl.ds(q0, BQ), :], lo_q)
            kw = k_ref[g, pl.ds(q0, W), :]
            vw = v_ref[g, pl.ds(q0, W), :]
            dov = do_ref[g, pl.ds(q0, BQ), :]
            lse = _row_vector(lse_ref[g, pl.ds(q0, BQ), :], lo_q)
            delta = _row_vector(_half_sum(dov * o_ref[g, pl.ds(q0, BQ), :], lo_q), lo_q)
            do2 = _stack_heads(dov.astype(BF16), lo_q)
            st = _dot_nt(kw, q2) + bias_ref[_edge_variant(b, nb)]
            pt = jnp.exp(st - lse)
            dst = pt * (_dot_nt(vw, do2) - delta)
            dstb = dst.astype(BF16)
            dbias_ref[...] += dst
            dk_ref[g, pl.ds(q0, W), :] += _dot(dstb, q2)
            dv_ref[g, pl.ds(q0, W), :] += _dot(pt.astype(BF16), do2)
            dq_ref[g, pl.ds(q0, BQ), :] = _unstack_heads(_dot_tn(dstb, kw), lo_q)
            dsink_ref[g, pl.ds(0, 1), :] -= jnp.exp(sk - lse) * delta
            return carry

        _loop_blocks(G * nb, blk, 0, 4)

    qspec = pl.BlockSpec((G, L, PAIR), lambda n: (n, 0, 0))
    kspec = pl.BlockSpec((G, L + 2 * R, PAIR), lambda n: (n // pairs_per_kv, 0, 0))
    return pl.pallas_call(
        body, name="attn_bwd", grid=(N // G,),
        in_specs=[pl.BlockSpec(memory_space=pltpu.SMEM), qspec, kspec, kspec,
                  pl.BlockSpec((None, 4, W, 2 * BQ), lambda n: (n * G // pairs_per_bias, 0, 0, 0)),
                  qspec, qspec, qspec],
        out_specs=[qspec, kspec, kspec,
                   pl.BlockSpec((None, W, 2 * BQ), lambda n: (n * G // pairs_per_bias, 0, 0)),
                   pl.BlockSpec((G, 8, 2 * BQ), lambda n: (n, 0, 0))],
        out_shape=[jax.ShapeDtypeStruct((N, L, PAIR), F32),
                   jax.ShapeDtypeStruct((Nk, L + 2 * R, PAIR), F32),
                   jax.ShapeDtypeStruct((Nk, L + 2 * R, PAIR), F32),
                   jax.ShapeDtypeStruct((Pb, W, 2 * BQ), F32),
                   jax.ShapeDtypeStruct((N, 8, 2 * BQ), F32)],
        compiler_params=_params("arbitrary"),
    )(sink, q, kp, vp, bias4t, o, lse, do)


def _attn_merge(o1, l1, o4, l4, o16, l16, ob, tm):
    T = o1.shape[1]

    def body(o1_ref, l1_ref, o4_ref, l4_ref, o16_ref, l16_ref, ob_ref, oa_ref, la_ref, cat_ref):
        for c in range(4):
            a, b, d = l1_ref[c], l4_ref[c], l16_ref[c]
            m = jnp.maximum(jnp.maximum(a, b), d)
            wa, wb, wd = jnp.exp(a - m), jnp.exp(b - m), jnp.exp(d - m)
            z = wa + wb + wd
            o = (wa * o1_ref[c] + wb * o4_ref[c] + wd * o16_ref[c]) / z
            oa_ref[c] = o
            la_ref[c] = m + jnp.log(z)
            cat_ref[:, c * PAIR:(c + 1) * PAIR] = o.astype(BF16)
            cat_ref[:, (4 + c) * PAIR:(5 + c) * PAIR] = ob_ref[c].astype(BF16)

    four = pl.BlockSpec((4, tm, PAIR), lambda i: (0, i, 0))
    s4 = jax.ShapeDtypeStruct((4, T, PAIR), F32)
    return pl.pallas_call(
        body, name="attn_merge", grid=(T // tm,),
        in_specs=[four] * 7,
        out_specs=[four, four, pl.BlockSpec((tm, 8 * PAIR), lambda i: (i, 0))],
        out_shape=[s4, s4, jax.ShapeDtypeStruct((T, 8 * PAIR), BF16)],
        compiler_params=_params("parallel"),
    )(o1, l1, o4, l4, o16, l16, ob)


def _weight_arg(w, blk):
    if blk is None:
        return pl.BlockSpec(w.shape, lambda i: (0, 0)), (lambda ref: ref[...])
    D = w.shape[2]
    return (pl.BlockSpec((N_DEV, 128, D), lambda i: (0, blk, 0)),
            lambda ref: ref[...].reshape(N_DEV * 128, D))


def _oproj_fwd(x, o_cat, w, blk, tm):
    T, D = x.shape
    wspec, wload = _weight_arg(w, blk)

    def body(x_ref, o_ref, w_ref, out_ref):
        out_ref[...] = x_ref[...] + _dot(o_ref[...], wload(w_ref))

    tok = pl.BlockSpec((tm, D), lambda i: (i, 0))
    return pl.pallas_call(
        body, name="oproj_fwd", grid=(T // tm,),
        in_specs=[tok, pl.BlockSpec((tm, o_cat.shape[1]), lambda i: (i, 0)), wspec],
        out_specs=tok, out_shape=jax.ShapeDtypeStruct((T, D), F32),
        compiler_params=_params("parallel"),
    )(x, o_cat, w)


def _oproj_bwd(dx, w, blk, tm, dep=None):
    T, D = dx.shape
    wspec, wload = _weight_arg(w, blk)

    def body(dx_ref, w_ref, dxb_ref, do_ref):
        db = dx_ref[...].astype(BF16)
        dxb_ref[...] = db
        do = _dot_nt(db, wload(w_ref))
        for c in range(8):
            do_ref[c] = do[:, c * PAIR:(c + 1) * PAIR]

    tok = pl.BlockSpec((tm, D), lambda i: (i, 0))
    body, in_specs, args = _with_dep(body, dep, [tok, wspec], [dx, w])
    return pl.pallas_call(
        body, name="oproj_bwd", grid=(T // tm,),
        in_specs=in_specs,
        out_specs=[tok, pl.BlockSpec((8, tm, PAIR), lambda i: (0, i, 0))],
        out_shape=[jax.ShapeDtypeStruct((T, D), BF16), jax.ShapeDtypeStruct((8, T, PAIR), F32)],
        compiler_params=_params("parallel"),
    )(*args)


def _attn_post(qkv, gains2, dqa, dka, dva, dqb, dkb, dvb, tm):
    T, NQ = qkv.shape
    scale = HEAD_DIM ** -0.5

    def body(qkv_ref, g_ref, qa1, qa4, qa16, ka1, ka4, ka16, va1, va4, va16, qb_ref, kb_ref, vb_ref,
             out_ref, dg_ref):
        lo = _lo_mask((tm, PAIR))

        @pl.when(pl.program_id(0) == 0)
        def _():
            dg_ref[...] = jnp.zeros_like(dg_ref)

        def norm_bwd(c, gi, dy):
            xv = qkv_ref[:, c * PAIR:(c + 1) * PAIR]
            r = lax.rsqrt(_half_sum(xv * xv, lo) * (1.0 / HEAD_DIM) + EPS)
            xn = xv * r
            dg_ref[gi:gi + 1, :] += jnp.sum(dy * xn, axis=0, keepdims=True)
            dxn = dy * g_ref[gi:gi + 1, :]
            dx = r * (dxn - xn * (_half_sum(dxn * xn, lo) * (1.0 / HEAD_DIM)))
            out_ref[:, c * PAIR:(c + 1) * PAIR] = dx.astype(BF16)

        def fold(v):
            return v + pltpu.roll(v, HEAD_DIM, 1)

        for c in range(4):
            norm_bwd(c, 0, (qa1[c] + qa4[c] + qa16[c]) * scale)
            norm_bwd(4 + c, 1, ka1[c] + ka4[c] + ka16[c])
            out_ref[:, (8 + c) * PAIR:(9 + c) * PAIR] = (va1[c] + va4[c] + va16[c]).astype(BF16)
            norm_bwd(12 + c, 2, qb_ref[c] * scale)
        norm_bwd(16, 3, jnp.where(lo, fold(kb_ref[0]), fold(kb_ref[1])))
        out_ref[:, 17 * PAIR:18 * PAIR] = jnp.where(lo, fold(vb_ref[0]), fold(vb_ref[1])).astype(BF16)

    four = pl.BlockSpec((4, tm, PAIR), lambda i: (0, i, 0))
    two = pl.BlockSpec((2, tm, PAIR), lambda i: (0, i, 0))
    return pl.pallas_call(
        body, name="attn_post", grid=(T // tm,),
        in_specs=[pl.BlockSpec((tm, NQ), lambda i: (i, 0)), pl.BlockSpec((4, PAIR), lambda i: (0, 0))]
        + [four] * 10 + [two, two],
        out_specs=[pl.BlockSpec((tm, NQ), lambda i: (i, 0)), pl.BlockSpec((4, PAIR), lambda i: (0, 0))],
        out_shape=[jax.ShapeDtypeStruct((T, NQ), BF16), jax.ShapeDtypeStruct((4, PAIR), F32)],
        compiler_params=_params("arbitrary"),
    )(qkv, gains2, *dqa, *dka, *dva, dqb, dkb, dvb)


def _dense_norm_bwd(dres, dz, w, blk, x, g, tm):
    T, D = x.shape
    N = dz.shape[1]
    wspec, wload = _weight_arg(w, blk)

    def body(dres_ref, dz_ref, w_ref, x_ref, g_ref, dx_ref, dgn_ref):
        i = pl.program_id(0)
        dx, dg = _norm_bwd(_dot_nt(dz_ref[...], wload(w_ref)), x_ref[...], g_ref[...])
        dx_ref[...] = dres_ref[...] + dx

        @pl.when(i == 0)
        def _():
            dgn_ref[...] = dg

        @pl.when(i > 0)
        def _():
            dgn_ref[...] += dg

    tok = pl.BlockSpec((tm, D), lambda i: (i, 0))
    row = pl.BlockSpec((1, D), lambda i: (0, 0))
    return pl.pallas_call(
        body, name="dense_norm_bwd", grid=(T // tm,),
        in_specs=[tok, pl.BlockSpec((tm, N), lambda i: (i, 0)), wspec, tok, row],
        out_specs=[tok, row],
        out_shape=[jax.ShapeDtypeStruct((T, D), F32), jax.ShapeDtypeStruct((1, D), F32)],
        compiler_params=_params("arbitrary"),
    )(dres, dz, w, x, g)


def _bias_reduce(onehot, dbm):
    Hb, K = dbm.shape

    def body(oh_ref, d_ref, out_ref):
        oh = oh_ref[...]
        d = d_ref[...]
        hi = d.astype(BF16)
        r1 = d - hi.astype(F32)
        mid = r1.astype(BF16)
        low = (r1 - mid.astype(F32)).astype(BF16)
        out_ref[...] = _dot_nt(hi, oh) + _dot_nt(mid, oh) + _dot_nt(low, oh)

    vm = pl.BlockSpec(memory_space=pltpu.VMEM)
    return pl.pallas_call(
        body, name="bias_reduce", in_specs=[vm, vm], out_specs=vm,
        out_shape=jax.ShapeDtypeStruct((Hb, 128), F32),
        compiler_params=pltpu.CompilerParams(vmem_limit_bytes=VMEM_LIMIT),
    )(onehot, dbm)


def _ple_fwd(x, g, wg, blk, p, wp, target, tm):
    T, D = x.shape
    P = p.shape[1]
    with_loss = target is not None
    wspec, wload = _weight_arg(wg, blk)

    def body(*refs):
        if with_loss:
            x_ref, g_ref, wg_ref, p_ref, wp_ref, t_ref, y_ref, hn_ref, gate_ref, pp_ref, pb_ref, loss_ref = refs
        else:
            x_ref, g_ref, wg_ref, p_ref, wp_ref, y_ref, hn_ref, gate_ref, pp_ref, pb_ref = refs
        i = pl.program_id(0)
        xv = x_ref[...]
        hb = (xv * _rstd(xv) * g_ref[...]).astype(BF16)
        hn_ref[...] = hb
        gate = _sigmoid(_dot(hb, wload(wg_ref)))
        pb = p_ref[...].astype(BF16)
        pb_ref[...] = pb
        pp = _dot(pb, wp_ref[...])
        gate_ref[...] = gate
        pp_ref[...] = pp
        y = xv + gate * pp
        if with_loss:
            err = y - t_ref[...]
            y_ref[...] = err * (1.0 / D)
            part = jnp.broadcast_to(0.5 * jnp.sum(jnp.sum(err * err, axis=1, keepdims=True) * (1.0 / D),
                                                  axis=0, keepdims=True), (1, 128))

            @pl.when(i == 0)
            def _():
                loss_ref[...] = part

            @pl.when(i > 0)
            def _():
                loss_ref[...] += part
        else:
            y_ref[...] = y

    tok = pl.BlockSpec((tm, D), lambda i: (i, 0))
    ptok = pl.BlockSpec((tm, P), lambda i: (i, 0))
    in_specs = [tok, pl.BlockSpec((1, D), lambda i: (0, 0)), wspec, ptok,
                pl.BlockSpec((P, D), lambda i: (0, 0))]
    out_specs = [tok, tok, tok, tok, ptok]
    out_shape = [jax.ShapeDtypeStruct((T, D), F32), jax.ShapeDtypeStruct((T, D), BF16),
                 jax.ShapeDtypeStruct((T, D), F32), jax.ShapeDtypeStruct((T, D), F32),
                 jax.ShapeDtypeStruct((T, P), BF16)]
    args = [x, g, wg, p, wp]
    if with_loss:
        in_specs.append(tok)
        out_specs.append(pl.BlockSpec((1, 128), lambda i: (0, 0)))
        out_shape.append(jax.ShapeDtypeStruct((1, 128), F32))
        args.append(target)
    return pl.pallas_call(
        body, name="ple_fwd_loss" if with_loss else "ple_fwd", grid=(T // tm,),
        in_specs=in_specs, out_specs=out_specs, out_shape=out_shape,
        compiler_params=_params("arbitrary" if with_loss else "parallel"),
    )(*args)


def _ple_bwd(dy, gate, pp, tm, dep=None):
    T, D = dy.shape

    def body(dy_ref, gate_ref, pp_ref, dgl_ref, dpp_ref):
        d = dy_ref[...]
        gt = gate_ref[...]
        dgl_ref[...] = (d * pp_ref[...] * gt * (1.0 - gt)).astype(BF16)
        dpp_ref[...] = (d * gt).astype(BF16)

    tok = pl.BlockSpec((tm, D), lambda i: (i, 0))
    body, in_specs, args = _with_dep(body, dep, [tok, tok, tok], [dy, gate, pp])
    return pl.pallas_call(
        body, name="ple_bwd", grid=(T // tm,), in_specs=in_specs, out_specs=[tok, tok],
        out_shape=[jax.ShapeDtypeStruct((T, D), BF16), jax.ShapeDtypeStruct((T, D), BF16)],
        compiler_params=_params("parallel"),
    )(*args)


def _adamw(w, g, m, v):
    shape = w.shape
    C = shape[-1]
    w2, g2, m2, v2 = (a.reshape(-1, C) for a in (w, g, m, v))
    Rn = w2.shape[0]
    tr = Rn
    for cand in (512, 352, 256):
        if Rn % cand == 0:
            tr = cand
            break
    c1 = 1.0 - ADAM_B1 ** ADAM_STEP
    c2 = 1.0 - ADAM_B2 ** ADAM_STEP

    def body(w_ref, g_ref, m_ref, v_ref, d_ref, nm_ref, nv_ref):
        gv = g_ref[...]
        mn = ADAM_B1 * m_ref[...] + (1.0 - ADAM_B1) * gv
        vn = ADAM_B2 * v_ref[...] + (1.0 - ADAM_B2) * (gv * gv)
        d_ref[...] = -ADAM_LR * ((mn / c1) / (jnp.sqrt(vn / c2) + ADAM_EPS) + ADAM_WD * w_ref[...])
        nm_ref[...] = mn
        nv_ref[...] = vn

    spec = pl.BlockSpec((tr, C), lambda i: (i, 0))
    sh = jax.ShapeDtypeStruct((Rn, C), F32)
    d, nm, nv = pl.pallas_call(
        body, name="adamw", grid=(Rn // tr,), in_specs=[spec] * 4, out_specs=[spec] * 3, out_shape=[sh] * 3,
        compiler_params=_params("parallel"),
    )(w2, g2, m2, v2)
    return d.reshape(shape), nm.reshape(shape), nv.reshape(shape)


def _my_place():
    x, y, c = lax.axis_index("x"), lax.axis_index("y"), lax.axis_index("c")
    chips = [(1 - x, y), (x, 1 - y), (1 - x, 1 - y)]
    return x, y, c, chips


def _all_gather(flat):
    R, Wd = flat.shape

    def body(x_ref, out_ref, send_sems, recv_sems, local_sem):
        x, y, c, chips = _my_place()
        me, sibling = (x, y, c), (x, y, 1 - c)

        def rows(px, py, pc):
            return out_ref.at[4 * px + 2 * py + pc]

        def copy(k, block, to, src=None):
            return pltpu.make_async_remote_copy(
                src_ref=rows(*block) if src is None else src, dst_ref=rows(*block),
                send_sem=send_sems.at[k], recv_sem=recv_sems.at[k], device_id=to, device_id_type=MESH)

        mine = pltpu.make_async_copy(x_ref, rows(*me), local_sem)
        mine.start()
        first = [copy(0, me, sibling, src=x_ref)]
        first += [copy(1 + j, me, (*chip, c), src=x_ref) for j, chip in enumerate(chips)]
        for cp in first:
            cp.start()
        passed = [copy(4 + j, (*chip, c), sibling) for j, chip in enumerate(chips)]
        for j, chip in enumerate(chips):
            copy(1 + j, (*chip, c), me).wait_recv()
            passed[j].start()
        copy(0, sibling, me).wait_recv()
        for j, chip in enumerate(chips):
            copy(4 + j, (*chip, 1 - c), me).wait_recv()
        for cp in first + passed:
            cp.wait_send()
        mine.wait()

    return pl.pallas_call(
        body, name="all_gather",
        in_specs=[pl.BlockSpec(memory_space=pl.ANY)], out_specs=pl.BlockSpec(memory_space=pl.ANY),
        out_shape=jax.ShapeDtypeStruct((N_DEV, R, Wd), flat.dtype),
        scratch_shapes=[pltpu.SemaphoreType.DMA((7,)), pltpu.SemaphoreType.DMA((7,)), pltpu.SemaphoreType.DMA],
    )(flat)


def _reduce_scatter(gparts, tr):
    _, R, Wd = gparts.shape
    nt = R // tr

    def body(g_ref, out_ref, a_ref, p_ref, b_ref, vb, vo_b, vo_f, d2d_send, d2d_recv, ici_send, ici_recv):
        x, y, c, chips = _my_place()
        sibling = (x, y, 1 - c)
        allchips = [(x, y)] + chips

        def dev(chip, pc):
            return 4 * chip[0] + 2 * chip[1] + pc

        d2d = [pltpu.make_async_remote_copy(
            src_ref=g_ref.at[dev(q, 1 - c)], dst_ref=a_ref.at[a], send_sem=d2d_send.at[a], recv_sem=d2d_recv.at[a],
            device_id=sibling, device_id_type=MESH) for a, q in enumerate(allchips)]
        for cp in d2d:
            cp.start()

        def add_tiles(srcs, dst, vo):
            def step(t, carry):
                r = pl.ds(pl.multiple_of(t * tr, tr), tr)
                acc = None
                for s_i, src in enumerate(srcs):
                    pltpu.sync_copy(src.at[r], vb.at[s_i])
                for s_i in range(len(srcs)):
                    term = vb[s_i].astype(F32)
                    acc = term if acc is None else acc + term
                vo[...] = acc.astype(vo.dtype)
                pltpu.sync_copy(vo, dst.at[r])
                return carry

            lax.fori_loop(0, nt, step, 0)

        ici = []
        for j, q in enumerate(chips):
            d2d[j + 1].wait_recv()
            add_tiles([g_ref.at[dev(q, c)], a_ref.at[j + 1]], p_ref.at[j], vo_b)
            cp = pltpu.make_async_remote_copy(
                src_ref=p_ref.at[j], dst_ref=b_ref.at[j], send_sem=ici_send.at[j], recv_sem=ici_recv.at[j],
                device_id=(*q, c), device_id_type=MESH)
            cp.start()
            ici.append(cp)
        d2d[0].wait_recv()
        for cp in ici:
            cp.wait_recv()
        add_tiles([g_ref.at[dev((x, y), c)], a_ref.at[0], b_ref.at[0], b_ref.at[1], b_ref.at[2]], out_ref, vo_f)
        for cp in d2d + ici:
            cp.wait_send()

    hbm = pl.BlockSpec(memory_space=pl.ANY)
    out, _, _, _ = pl.pallas_call(
        body, name="reduce_scatter",
        in_specs=[hbm], out_specs=[hbm, hbm, hbm, hbm],
        out_shape=[jax.ShapeDtypeStruct((R, Wd), F32), jax.ShapeDtypeStruct((4, R, Wd), BF16),
                   jax.ShapeDtypeStruct((3, R, Wd), BF16), jax.ShapeDtypeStruct((3, R, Wd), BF16)],
        scratch_shapes=[pltpu.VMEM((5, tr, Wd), BF16), pltpu.VMEM((tr, Wd), BF16), pltpu.VMEM((tr, Wd), F32),
                        pltpu.SemaphoreType.DMA((4,)), pltpu.SemaphoreType.DMA((4,)),
                        pltpu.SemaphoreType.DMA((3,)), pltpu.SemaphoreType.DMA((3,))],
        compiler_params=pltpu.CompilerParams(vmem_limit_bytes=VMEM_LIMIT),
    )(gparts)
    return out


def _peer(x, y, c, k):
    return (x ^ ((k >> 2) & 1), y ^ ((k >> 1) & 1), c ^ (k & 1))


HBM_SPEC = pl.BlockSpec(memory_space=pltpu.HBM)
SEM_SPEC = pl.BlockSpec(memory_space=pltpu.SEMAPHORE)


def _exchange_refs(srcs, lands, m, k, x, y, c, scatter):
    peer = _peer(x, y, c, k)
    if scatter:
        return srcs[m].at[4 * peer[0] + 2 * peer[1] + peer[2]], lands[m].at[k - 1], peer
    return srcs[m], lands[m].at[4 * x + 2 * y + c], peer


def _exchange_start(arrs, land_shapes, scatter, name):
    n = len(arrs)

    def body(*refs):
        srcs, lands = refs[:n], refs[n:2 * n]
        send_sems, recv_sems = refs[2 * n], refs[2 * n + 1]
        token = refs[-1]
        x, y, c, _ = _my_place()
        for m in range(n):
            for k in range(1, N_DEV):
                src, dst, peer = _exchange_refs(srcs, lands, m, k, x, y, c, scatter)
                pltpu.make_async_remote_copy(
                    src_ref=src, dst_ref=dst, send_sem=send_sems.at[7 * m + k - 1],
                    recv_sem=recv_sems.at[7 * m + k - 1], device_id=peer, device_id_type=MESH).start()
        token[...] = jnp.zeros_like(token)

    zones = [lax.empty(s_, a.dtype) for s_, a in zip(land_shapes, arrs)]
    outs = pl.pallas_call(
        body, name=name,
        out_shape=(pltpu.SemaphoreType.DMA((7 * n,)), pltpu.SemaphoreType.DMA((7 * n,)),
                   *[pltpu.HBM(a.shape, a.dtype) for a in arrs], *[pltpu.HBM(z.shape, z.dtype) for z in zones],
                   jax.ShapeDtypeStruct((8, 128), F32)),
        in_specs=[HBM_SPEC] * (2 * n),
        out_specs=(SEM_SPEC, SEM_SPEC, *[HBM_SPEC] * (2 * n), pl.BlockSpec(memory_space=pltpu.VMEM)),
        input_output_aliases={m: 2 + m for m in range(2 * n)},
        compiler_params=pltpu.CompilerParams(has_side_effects=pltpu.SideEffectType.DATAFLOW_SIDE_EFFECTING),
    )(*[pltpu.with_memory_space_constraint(a, pltpu.HBM) for a in arrs],
      *[pltpu.with_memory_space_constraint(z, pltpu.HBM) for z in zones])
    return outs[0], outs[1], list(outs[2:2 + n]), list(outs[2 + n:2 + 2 * n]), outs[-1]


def _exchange_wait(send_sems, recv_sems, arrs, zones, after, scatter, name):
    n = len(arrs)

    def body(*refs):
        srcs, lands = refs[:n], refs[n:2 * n]
        send_sems, recv_sems = refs[2 * n], refs[2 * n + 1]
        x, y, c, _ = _my_place()
        for m in range(n):
            for k in range(1, N_DEV):
                src, dst, peer = _exchange_refs(srcs, lands, m, k, x, y, c, scatter)
                cp = pltpu.make_async_remote_copy(
                    src_ref=src, dst_ref=dst, send_sem=send_sems.at[7 * m + k - 1],
                    recv_sem=recv_sems.at[7 * m + k - 1], device_id=peer, device_id_type=MESH)
                cp.wait_send()
                cp.wait_recv()

    outs = pl.pallas_call(
        body, name=name,
        out_shape=tuple(pltpu.HBM(a.shape, a.dtype) for a in list(arrs) + list(zones)),
        in_specs=[HBM_SPEC] * (2 * n) + [SEM_SPEC, SEM_SPEC, pl.BlockSpec(memory_space=pl.ANY)],
        out_specs=tuple([HBM_SPEC] * (2 * n)),
        input_output_aliases={m: m for m in range(2 * n)},
        compiler_params=pltpu.CompilerParams(has_side_effects=pltpu.SideEffectType.DATAFLOW_SIDE_EFFECTING),
    )(*arrs, *zones, send_sems, recv_sems, after)
    return list(outs[n:])


def _sum_parts(own, parts, tr):
    R, W = own.shape

    def body(own_ref, parts_ref, out_ref):
        acc = own_ref[...].astype(F32)
        for k in range(N_DEV - 1):
            acc = acc + parts_ref[k].astype(F32)
        out_ref[...] = acc

    return pl.pallas_call(
        body, name="sum_parts", grid=(R // tr,),
        in_specs=[pl.BlockSpec((tr, W), lambda i: (i, 0)), pl.BlockSpec((N_DEV - 1, tr, W), lambda i: (0, i, 0))],
        out_specs=pl.BlockSpec((tr, W), lambda i: (i, 0)),
        out_shape=jax.ShapeDtypeStruct((R, W), F32),
        compiler_params=_params("parallel"),
    )(own, parts)


def _all_reduce_small(v):
    Rn, Wd = v.shape

    def body(v_ref, out_ref, gat_ref, send_sems, recv_sems):
        x, y, c, _ = _my_place()
        me = 4 * x + 2 * y + c
        gat_ref[me] = v_ref[...]
        copies = []
        for k in range(1, N_DEV):
            fx, fy, fc = (k >> 2) & 1, (k >> 1) & 1, k & 1
            peer = (x ^ fx, y ^ fy, c ^ fc)
            cp = pltpu.make_async_remote_copy(
                src_ref=v_ref, dst_ref=gat_ref.at[me], send_sem=send_sems.at[k - 1], recv_sem=recv_sems.at[k - 1],
                device_id=peer, device_id_type=MESH)
            cp.start()
            copies.append(cp)
        for cp in copies:
            cp.wait_recv()
        for cp in copies:
            cp.wait_send()
        acc = gat_ref[0]
        for k in range(1, N_DEV):
            acc = acc + gat_ref[k]
        out_ref[...] = acc

    vm = pl.BlockSpec(memory_space=pltpu.VMEM)
    return pl.pallas_call(
        body, name="all_reduce_small", in_specs=[vm], out_specs=vm,
        out_shape=jax.ShapeDtypeStruct((Rn, Wd), F32),
        scratch_shapes=[pltpu.VMEM((N_DEV, Rn, Wd), F32), pltpu.SemaphoreType.DMA((7,)),
                        pltpu.SemaphoreType.DMA((7,))],
    )(v)


def _t5_bucket(rel):
    half = N_BUCKETS // 2
    max_exact = half // 2
    ret = jnp.where(rel > 0, half, 0)
    n = jnp.abs(rel)
    nf = jnp.maximum(n, 1).astype(F32)
    large = max_exact + (jnp.log(nf / max_exact) / math.log(MAX_DISTANCE / max_exact)
                         * (half - max_exact)).astype(jnp.int32)
    large = jnp.minimum(large, half - 1)
    return ret + jnp.where(n < max_exact, n, large)


def _band(R, d):
    W = BQ + 2 * R
    rel = jnp.arange(W)[None, :] - R - jnp.arange(BQ)[:, None]
    return _t5_bucket(rel * d), jnp.abs(rel) <= R


def _onehot(R, d):
    bkt, in_band = _band(R, d)
    return ((bkt.reshape(1, -1) == jnp.arange(128)[:, None]) & in_band.reshape(1, -1)).astype(BF16)


def _bias_expand(table_t, onehot):
    H = table_t.shape[0]
    K = onehot.shape[1]

    def body(t_ref, oh_ref, out_ref):
        oh = oh_ref[...]
        t = t_ref[...]
        hi = t.astype(BF16)
        r1 = t - hi.astype(F32)
        mid = r1.astype(BF16)
        low = (r1 - mid.astype(F32)).astype(BF16)
        marked = _dot(jnp.ones(t.shape, BF16), oh) > 0.5
        out_ref[...] = jnp.where(marked, _dot(hi, oh) + _dot(mid, oh) + _dot(low, oh), NEG)

    vm = pl.BlockSpec(memory_space=pltpu.VMEM)
    return pl.pallas_call(
        body, name="bias_expand", in_specs=[vm, vm], out_specs=vm,
        out_shape=jax.ShapeDtypeStruct((H, K), F32),
        compiler_params=pltpu.CompilerParams(vmem_limit_bytes=VMEM_LIMIT),
    )(table_t, onehot)


def _bias_matrix(table, R, d):
    table_t = jnp.pad(table.T, ((0, 0), (0, 128 - N_BUCKETS)))
    return _bias_expand(table_t, _onehot(R, d)).reshape(table.shape[1], BQ, BQ + 2 * R)


def _bias_variants(base, R):
    H, _, W = base.shape
    col = jnp.arange(W)
    before, after = col < R, col >= BQ + R
    masks = jnp.stack([jnp.zeros_like(before), before, after, before | after])
    v = jnp.where(masks[None, :, None, :], NEG, base[:, None])
    v = v.reshape(H // 2, 2, 4, BQ, W).transpose(0, 2, 1, 3, 4).reshape(H // 2, 4, 2 * BQ, W)
    return v, v.transpose(0, 1, 3, 2)


def _bias_grad(dbt, R, d):
    P, W, _ = dbt.shape
    dbm = dbt.reshape(P, W, 2, BQ).transpose(0, 2, 3, 1).reshape(2 * P, BQ * W)
    return _bias_reduce(_onehot(R, d), dbm)[:, :N_BUCKETS].T


def _deint(a, d):
    if d == 1:
        return a
    H, T, X = a.shape
    return a.reshape(H, T // d, d, X).transpose(0, 2, 1, 3).reshape(H * d, T // d, X)


def _reint(a, d):
    if d == 1:
        return a
    Hd, L, X = a.shape
    return a.reshape(Hd // d, d, L, X).transpose(0, 2, 1, 3).reshape(Hd // d, L * d, X)


def _pad_rows(a, R):
    return jnp.pad(a, ((0, 0), (R, R), (0, 0)))


def _tile2(gain):
    return jnp.concatenate([gain, gain])


ROW_W_O, ROW_GATE, ROW_QKV, ROW_PROJ, B_ROWS = 768, 896, 1024, 1312, 1344
BLK_W_O, BLK_GATE = ROW_W_O // 128, ROW_GATE // 128


def _pack_layer(wts, i):
    a = jnp.stack([wts["ffn1_w_in"][i], wts["ffn2_w_in"][i]])
    D = a.shape[1]
    b = jnp.concatenate([
        wts["ffn1_w_out"][i], wts["ffn2_w_out"][i],
        jnp.zeros((ROW_W_O - 2 * wts["ffn1_w_out"].shape[1], D), a.dtype),
        wts["w_o"][i], wts["w_ple_gate"][i], wts["w_qkv"][i].reshape(-1, D), wts["w_ple_proj"][i].reshape(-1, D)])
    return a, b


def _unpack_layer(sums, like):
    w_in2, b1, b2, w_in1, w_out1 = sums
    n_out, n_sq = like["ffn1_w_out"].shape[1], like["w_o"].shape[1]
    return {"ffn1_w_in": w_in1, "ffn2_w_in": w_in2, "ffn1_w_out": w_out1, "ffn2_w_out": b1[:n_out],
            "w_ple_gate": b1[n_out:n_out + n_sq],
            "w_ple_proj": b1[n_out + n_sq:].reshape(like["w_ple_proj"].shape[1:]),
            "w_o": b2[:n_sq], "w_qkv": b2[n_sq:].reshape(like["w_qkv"].shape[1:])}


def _col_sharded(gb, r0, r1, rows):
    return gb[:, r0:r1].reshape(N_DEV, rows, -1).transpose(1, 0, 2).reshape(rows, -1)


def _to_col_shards(g):
    rows = g.shape[0]
    return g.reshape(rows, N_DEV, -1).transpose(1, 0, 2).reshape(N_DEV, -1, 1024)


def _layer_weights(ga, gb, p_dim):
    return dict(ga=ga, gb=gb, w_qkv=_col_sharded(gb, ROW_QKV, ROW_PROJ, ga.shape[2]),
                w_proj=_col_sharded(gb, ROW_PROJ, B_ROWS, p_dim))


def _layer_fwd(x, p, w, sm, i, target, tm, biases, dep=None):
    ga, gb = w["ga"], w["gb"]
    saved = {}
    saved["x0"] = x
    x1, saved["h1"], saved["zg1"], saved["zu1"], saved["s1"] = _ffn_fwd(
        x, sm["norm_ffn1"][i][None], ga, gb, 0, tm, dep)
    saved["x1"] = x1
    qkv, saved["hm"] = _qkv_fwd(x1, sm["norm_mix"][i][None], w["w_qkv"], tm)
    saved["qkv"] = qkv
    gains2 = jnp.stack([_tile2(sm[k][i]) for k in ("q_norm_a", "k_norm_a", "q_norm_b", "k_norm_b")])
    saved["gains2"] = gains2
    qa, ka, va, qb, kb, vb = _attn_prep(qkv, gains2, tm)
    no_sink = jnp.full((8,), NEG, F32)
    branches = []
    outs = []
    for (R, d), bias in zip(DILATED, biases[:3]):
        qd, kd, vd = _deint(qa, d), _pad_rows(_deint(ka, d), R), _pad_rows(_deint(va, d), R)
        sink = jnp.tile(no_sink, d)
        o, lse = _attn_fwd(qd, kd, vd, bias[0], sink, R, 1, d)
        branches.append((qd, kd, vd, bias, sink, R, d))
        outs += [_reint(o, d), _reint(lse, d)]
    bias_b = biases[3]
    kbp, vbp = _pad_rows(kb, SWA_RADIUS), _pad_rows(vb, SWA_RADIUS)
    sink_b = sm["sink_b"][i]
    ob, lb = _attn_fwd(qb, kbp, vbp, bias_b[0], sink_b, SWA_RADIUS, 2, 1)
    oa, la, o_cat = _attn_merge(*outs, ob, tm)
    saved.update(branches=branches, b=(qb, kbp, vbp, bias_b, sink_b), oa=oa, la=la, ob=ob, lb=lb, o_cat=o_cat)
    x2 = _oproj_fwd(x1, o_cat, gb, BLK_W_O, tm)
    saved["x2"] = x2
    x3, saved["h2"], saved["zg2"], saved["zu2"], saved["s2"] = _ffn_fwd(
        x2, sm["norm_ffn2"][i][None], ga, gb, 1, tm)
    saved["x3"] = x3
    res = _ple_fwd(x3, sm["norm_ple"][i][None], gb, BLK_GATE, p, w["w_proj"], target, tm)
    y, saved["hp"], saved["gate"], saved["pp"], saved["pb"] = res[:5]
    loss = res[5] if target is not None else None
    return y, loss, saved


def _layer_bwd(dy, w, sm, i, sv, tm, dep=None, on_ready=None):
    ga, gb = w["ga"], w["gb"]
    gs = {}
    D = dy.shape[1]
    dgl, dpp = _ple_bwd(dy, sv["gate"], sv["pp"], tm, dep)
    d_gate = _matmul_tn(sv["hp"], dgl, D, 2 * tm)
    d_proj = _matmul_tn(sv["pb"], dpp, D, 2 * tm)
    dx3, gs["norm_ple"] = _dense_norm_bwd(dy, dgl, gb, BLK_GATE, sv["x3"], sm["norm_ple"][i][None], tm)
    dx2, dyb, dzg, dzu, gs["norm_ffn2"] = _ffn_bwd(dx3, sv["x2"], sm["norm_ffn2"][i][None], sv["zg2"], sv["zu2"],
                                                   ga, gb, 1, tm)
    dwin2, dwo2 = _ffn_dw(sv["h2"], dzg, dzu, sv["s2"], dyb, 2 * tm)
    half = dwo2.shape[1] // 2
    after_ffn2 = [dwin2, jnp.concatenate([dwo2.reshape(N_DEV, half, D), d_gate.reshape(N_DEV, -1, D),
                                          _to_col_shards(d_proj)], axis=1)]
    token = None if on_ready is None else on_ready(0, after_ffn2)
    dx2b, do = _oproj_bwd(dx2, gb, BLK_W_O, tm, token)
    d_wo = _matmul_tn(sv["o_cat"], dx2b, D, 2 * tm)
    do_a, do_b = do[:4], do[4:]
    dqa, dka, dva, dbias = [], [], [], []
    for qd, kd, vd, bias, sink, R, d in sv["branches"]:
        dq, dk, dv, dbm, _ = _attn_bwd(qd, kd, vd, bias[1], sink, _deint(sv["oa"], d), _deint(sv["la"], d),
                                        _deint(do_a, d), R, 1, d)
        L = qd.shape[1]
        dqa.append(_reint(dq, d))
        dka.append(_reint(dk[:, R:R + L], d))
        dva.append(_reint(dv[:, R:R + L], d))
        dbias.append(dbm)
    qb, kbp, vbp, bias_b, sink_b = sv["b"]
    dqb, dkb, dvb, dbm_b, dsink = _attn_bwd(qb, kbp, vbp, bias_b[1], sink_b, sv["ob"], sv["lb"], do_b,
                                            SWA_RADIUS, 2, 1)
    T = qb.shape[1]
    gs["rel_bias"] = dbias + [dbm_b]
    gs["sink_b"] = jnp.sum(dsink[:, 0].reshape(-1, 2, BQ), axis=2).reshape(-1)
    dqkv, dgains2 = _attn_post(sv["qkv"], sv["gains2"], dqa, dka, dva, dqb,
                               dkb[:, SWA_RADIUS:SWA_RADIUS + T], dvb[:, SWA_RADIUS:SWA_RADIUS + T], tm // 2)
    dgains = dgains2[:, :HEAD_DIM] + dgains2[:, HEAD_DIM:]
    for k, name in enumerate(("q_norm_a", "k_norm_a", "q_norm_b", "k_norm_b")):
        gs[name] = dgains[k]
    d_qkv = _matmul_tn(sv["hm"], dqkv, dqkv.shape[1] // 2, 2 * tm)
    after_mixer = [jnp.concatenate([d_wo.reshape(N_DEV, -1, D), _to_col_shards(d_qkv)], axis=1)]
    token = None if on_ready is None else on_ready(1, after_mixer)
    dx1, gs["norm_mix"] = _dense_norm_bwd(dx2, dqkv, w["w_qkv"], None, sv["x1"], sm["norm_mix"][i][None], tm)
    dx0, dyb, dzg, dzu, gs["norm_ffn1"] = _ffn_bwd(dx1, sv["x0"], sm["norm_ffn1"][i][None], sv["zg1"], sv["zu1"],
                                                   ga, gb, 0, tm, token)
    dwin1, dwo1 = _ffn_dw(sv["h1"], dzg, dzu, sv["s1"], dyb, 2 * tm)
    return dx0, (after_ffn2, after_mixer, [dwin1, dwo1.reshape(N_DEV, half, D)]), gs


def _bias_matrices(rel_bias):
    biases = [_bias_variants(_bias_matrix(rel_bias[:, :8], R, d), R) for R, d in DILATED]
    biases.append(_bias_variants(_bias_matrix(rel_bias[:, 8:], SWA_RADIUS, 1), SWA_RADIUS))
    return biases


def _stack_small(per_layer):
    small = {}
    for k, v in per_layer.items():
        if k == "rel_bias":
            per_branch = [sum(parts) for parts in zip(*v.values())]
            drel_a = sum(_bias_grad(t, R, d) for t, (R, d) in zip(per_branch[:3], DILATED))
            small[k] = jnp.concatenate([drel_a, _bias_grad(per_branch[3], SWA_RADIUS, 1)], axis=1)
        else:
            small[k] = jnp.stack([v[i].reshape(-1) for i in sorted(v)])
    return small


TM = 512
SUM_TILES = (512, 512, 416, 512, 352)


def _pack_small(d, extra=None):
    parts = [d[k].reshape(-1) for k in SMALL]
    if extra is not None:
        parts.append(extra.reshape(-1))
    flat = jnp.concatenate(parts)
    return jnp.pad(flat, (0, SMALL_ROWS * 128 - flat.shape[0])).reshape(SMALL_ROWS, 128)


def _unpack_small(buf, like):
    flat = buf.reshape(-1)
    out, off = {}, 0
    for k in SMALL:
        n = like[k].size
        out[k] = flat[off:off + n].reshape(like[k].shape)
        off += n
    return out, flat[off]


def kernel(x, p, rel_bias, norm_ffn1, ffn1_w_in, ffn1_w_out, norm_mix, w_qkv, q_norm_a, k_norm_a, q_norm_b, k_norm_b, sink_b, w_o, norm_ffn2, ffn2_w_in, ffn2_w_out, norm_ple, w_ple_gate, w_ple_proj, loss_target, m_rel_bias, m_norm_ffn1, m_ffn1_w_in, m_ffn1_w_out, m_norm_mix, m_w_qkv, m_q_norm_a, m_k_norm_a, m_q_norm_b, m_k_norm_b, m_sink_b, m_w_o, m_norm_ffn2, m_ffn2_w_in, m_ffn2_w_out, m_norm_ple, m_w_ple_gate, m_w_ple_proj, v_rel_bias, v_norm_ffn1, v_ffn1_w_in, v_ffn1_w_out, v_norm_mix, v_w_qkv, v_q_norm_a, v_k_norm_a, v_q_norm_b, v_k_norm_b, v_sink_b, v_w_o, v_norm_ffn2, v_ffn2_w_in, v_ffn2_w_out, v_norm_ple, v_w_ple_gate, v_w_ple_proj):
    wts = dict(rel_bias=rel_bias, norm_ffn1=norm_ffn1, ffn1_w_in=ffn1_w_in, ffn1_w_out=ffn1_w_out,
               norm_mix=norm_mix, w_qkv=w_qkv, q_norm_a=q_norm_a, k_norm_a=k_norm_a, q_norm_b=q_norm_b,
               k_norm_b=k_norm_b, sink_b=sink_b, w_o=w_o, norm_ffn2=norm_ffn2, ffn2_w_in=ffn2_w_in,
               ffn2_w_out=ffn2_w_out, norm_ple=norm_ple, w_ple_gate=w_ple_gate, w_ple_proj=w_ple_proj)
    mom = dict(rel_bias=m_rel_bias, norm_ffn1=m_norm_ffn1, ffn1_w_in=m_ffn1_w_in, ffn1_w_out=m_ffn1_w_out,
               norm_mix=m_norm_mix, w_qkv=m_w_qkv, q_norm_a=m_q_norm_a, k_norm_a=m_k_norm_a, q_norm_b=m_q_norm_b,
               k_norm_b=m_k_norm_b, sink_b=m_sink_b, w_o=m_w_o, norm_ffn2=m_norm_ffn2, ffn2_w_in=m_ffn2_w_in,
               ffn2_w_out=m_ffn2_w_out, norm_ple=m_norm_ple, w_ple_gate=m_w_ple_gate, w_ple_proj=m_w_ple_proj)
    var = dict(rel_bias=v_rel_bias, norm_ffn1=v_norm_ffn1, ffn1_w_in=v_ffn1_w_in, ffn1_w_out=v_ffn1_w_out,
               norm_mix=v_norm_mix, w_qkv=v_w_qkv, q_norm_a=v_q_norm_a, k_norm_a=v_k_norm_a, q_norm_b=v_q_norm_b,
               k_norm_b=v_k_norm_b, sink_b=v_sink_b, w_o=v_w_o, norm_ffn2=v_norm_ffn2, ffn2_w_in=v_ffn2_w_in,
               ffn2_w_out=v_ffn2_w_out, norm_ple=v_norm_ple, w_ple_gate=v_w_ple_gate, w_ple_proj=v_w_ple_proj)
    sm = {k: wts[k] for k in SMALL}
    p_dim = p.shape[-1]
    me = 4 * lax.axis_index("x") + 2 * lax.axis_index("y") + lax.axis_index("c")
    packed = []
    for i in range(2):
        a, b = _pack_layer(wts, i)
        packed.append([a.reshape(-1, a.shape[-1]).astype(BF16), b.astype(BF16)])
    a_shape = (2, ffn1_w_in.shape[1], ffn1_w_in.shape[2])

    def weights_of(zones):
        return _layer_weights(zones[0].reshape((N_DEV,) + a_shape), zones[1], p_dim)

    w0 = weights_of([_all_gather(t) for t in packed[0]])
    zone_shapes = [(N_DEV,) + t.shape for t in packed[1]]
    ssem, rsem, thru, zones, token = _exchange_start(packed[1], zone_shapes, False, "gather_start")
    biases = _bias_matrices(rel_bias)
    x1, _, sv0 = _layer_fwd(x[0], p[0, 0], w0, sm, 0, None, TM, biases, dep=token)
    zones = _exchange_wait(ssem, rsem, thru, zones, x1, False, "gather_wait")
    w1 = weights_of([lax.dynamic_update_index_in_dim(z, t, me, 0) for z, t in zip(zones, packed[1])])
    dy, loss, sv1 = _layer_fwd(x1, p[1, 0], w1, sm, 1, loss_target[0], TM, biases)

    def slots_for(arrs):
        return [(N_DEV - 1,) + t.shape[1:] for t in arrs]

    dx1, groups1, gs1 = _layer_bwd(dy, w1, sm, 1, sv1, TM)
    g1 = groups1[0] + groups1[1] + groups1[2]
    ex1 = _exchange_start(g1, slots_for(g1), True, "scatter_start")
    held = {}

    def on_ready(stage, group):
        if stage == 1:
            held["slots1"] = _exchange_wait(*ex1[:4], group[0], True, "scatter_wait")
        held[stage] = _exchange_start(group, slots_for(group), True, f"scatter_start_{stage}")
        return held[stage][4]

    dx, groups0, gs0 = _layer_bwd(dx1, w0, sm, 0, sv0, TM, dep=ex1[4], on_ready=on_ready)
    last = groups0[2]
    slots0 = [_exchange_wait(*held[stage][:4], last[0], True, f"scatter_wait_{stage}") for stage in (0, 1)]

    def summed(arrs, slots, tiles):
        return [_sum_parts(lax.dynamic_index_in_dim(t, me, 0, keepdims=False), s_, tr)
                for t, s_, tr in zip(arrs, slots, tiles)]

    r1 = summed(g1, held["slots1"], SUM_TILES)
    r0 = (summed(groups0[0], slots0[0], SUM_TILES[:2]) + summed(groups0[1], slots0[1], SUM_TILES[2:3])
          + [_reduce_scatter(t, tr) for t, tr in zip(last, SUM_TILES[3:])])

    gsmall = _stack_small({k: {0: gs0[k], 1: gs1[k]} for k in gs0})
    small_sum, loss_sum = _unpack_small(_all_reduce_small(_pack_small(gsmall, loss[0, :1])), sm)

    grads = dict(small_sum)
    layers = [_unpack_layer(r, wts) for r in (r0, r1)]
    for k in BIG:
        grads[k] = jnp.stack([layers[0][k], layers[1][k]])

    delta, new_m, new_v = {}, {}, {}
    for k in BIG:
        delta[k], new_m[k], new_v[k] = _adamw(wts[k], grads[k], mom[k], var[k])
    zeros = {k: jnp.zeros_like(wts[k]) for k in SMALL}
    ds, ms, vs = _adamw(_pack_small(wts), _pack_small(small_sum), _pack_small(mom), _pack_small(var))
    for packed, dst in ((ds, delta), (ms, new_m), (vs, new_v)):
        dst.update(_unpack_small(packed, zeros)[0])

    return (loss_sum, dx[None], *[grads[k] for k in WEIGHTS], *[delta[k] for k in WEIGHTS],
            *[new_m[k] for k in WEIGHTS], *[new_v[k] for k in WEIGHTS])
```

```python
import functools
import math

import jax
import jax.numpy as jnp
from jax import lax
from jax.experimental import pallas as pl
from jax.experimental.pallas import tpu as pltpu

F32 = jnp.float32
BF16 = jnp.bfloat16

N_DEV = 8
HEAD_DIM = 64
PAIR = 2 * HEAD_DIM
BQ = 128
N_BUCKETS = 32
MAX_DISTANCE = 1024
DILATED = ((64, 1), (64, 4), (64, 16))
SWA_RADIUS = 128
EPS = 1e-6
NEG = -1e30
ADAM_LR, ADAM_B1, ADAM_B2, ADAM_EPS, ADAM_WD, ADAM_STEP = 0.001, 0.9, 0.999, 1e-08, 0.01, 10
VMEM_LIMIT = 56 * 1024 * 1024
AXES = ("x", "y", "c")
MESH = pl.DeviceIdType.MESH

BIG = ("ffn1_w_in", "ffn1_w_out", "w_qkv", "w_o", "ffn2_w_in", "ffn2_w_out", "w_ple_gate", "w_ple_proj")
SMALL = ("rel_bias", "norm_ffn1", "norm_mix", "q_norm_a", "k_norm_a", "q_norm_b", "k_norm_b", "sink_b",
         "norm_ffn2", "norm_ple")
WEIGHTS = ("rel_bias", "norm_ffn1", "ffn1_w_in", "ffn1_w_out", "norm_mix", "w_qkv", "q_norm_a", "k_norm_a",
           "q_norm_b", "k_norm_b", "sink_b", "w_o", "norm_ffn2", "ffn2_w_in", "ffn2_w_out", "norm_ple",
           "w_ple_gate", "w_ple_proj")
SMALL_ROWS = 96


def _params(*sem):
    return pltpu.CompilerParams(dimension_semantics=sem, vmem_limit_bytes=VMEM_LIMIT)


def _dot(a, b):
    return jnp.dot(a, b, preferred_element_type=F32)


def _dot_nt(a, b):
    return lax.dot_general(a, b, (((1,), (1,)), ((), ())), preferred_element_type=F32)


def _dot_tn(a, b):
    return lax.dot_general(a, b, (((0,), (0,)), ((), ())), preferred_element_type=F32)


def _sigmoid(x):
    return 1.0 / (1.0 + jnp.exp(-x))


def _rstd(xv):
    return lax.rsqrt(jnp.mean(xv * xv, axis=-1, keepdims=True) + EPS)


def _norm_bwd(dh, xv, gv):
    r = _rstd(xv)
    xn = xv * r
    dg = jnp.sum(dh * xn, axis=0, keepdims=True)
    dxn = dh * gv
    dx = r * (dxn - xn * jnp.mean(dxn * xn, axis=-1, keepdims=True))
    return dx, dg


def _lo_mask(shape):
    return lax.broadcasted_iota(jnp.int32, shape, len(shape) - 1) < HEAD_DIM


def _half_sum(t, lo):
    s0 = jnp.sum(jnp.where(lo, t, 0.0), axis=1, keepdims=True)
    s1 = jnp.sum(jnp.where(lo, 0.0, t), axis=1, keepdims=True)
    return jnp.where(lo, s0, s1)


FFN_PARTS = 2


def _ffn_weight_specs(f, nj, D, C):
    return [pl.BlockSpec((None, None, D, C), lambda i, j: (j, f, 0, 0)),
            pl.BlockSpec((None, None, D, C), lambda i, j: (j + nj, f, 0, 0)),
            pl.BlockSpec((2, C // 2, D), lambda i, j: (j, f, 0))]


def _with_dep(body, dep, in_specs, args):
    if dep is None:
        return body, in_specs, args

    def body_after(dep_ref, *refs):
        body(*refs)

    return body_after, [pl.BlockSpec(memory_space=pl.ANY)] + in_specs, [dep] + args


def _ffn_fwd(x, g, ga, gb, f, tm, dep=None):
    T, D = x.shape
    nj, C = ga.shape[0] // 2, ga.shape[3]

    def body(x_ref, g_ref, wg_ref, wu_ref, wo_ref, xo_ref, h_ref, zg_ref, zu_ref, s_ref, h_scr, acc):
        j = pl.program_id(1)

        @pl.when(j == 0)
        def _():
            xv = x_ref[...]
            hb = (xv * _rstd(xv) * g_ref[...]).astype(BF16)
            h_scr[...] = hb
            h_ref[...] = hb
            acc[...] = jnp.zeros_like(acc)

        wo = wo_ref[...].reshape(C, D)
        for part in range(FFN_PARTS):
            sl = pl.ds(part * (tm // FFN_PARTS), tm // FFN_PARTS)
            hb = h_scr[sl, :]
            gt = _dot(hb, wg_ref[...])
            up = _dot(hb, wu_ref[...])
            s = (gt * _sigmoid(gt) * up).astype(BF16)
            zg_ref[sl, :] = gt.astype(BF16)
            zu_ref[sl, :] = up.astype(BF16)
            s_ref[sl, :] = s
            acc[sl, :] += _dot(s, wo)

        @pl.when(j == nj - 1)
        def _():
            xo_ref[...] = x_ref[...] + 0.5 * acc[...]

    tok = pl.BlockSpec((tm, D), lambda i, j: (i, 0))
    chunk = pl.BlockSpec((None, tm, C), lambda i, j: (j, i, 0))
    in_specs = [tok, pl.BlockSpec((1, D), lambda i, j: (0, 0))] + _ffn_weight_specs(f, nj, D, C)
    body, in_specs, args = _with_dep(body, dep, in_specs, [x, g, ga, ga, gb])
    return pl.pallas_call(
        body, name="ffn_fwd", grid=(T // tm, nj),
        in_specs=in_specs,
        out_specs=[tok, tok, chunk, chunk, chunk],
        out_shape=[jax.ShapeDtypeStruct((T, D), F32), jax.ShapeDtypeStruct((T, D), BF16),
                   jax.ShapeDtypeStruct((nj, T, C), BF16), jax.ShapeDtypeStruct((nj, T, C), BF16),
                   jax.ShapeDtypeStruct((nj, T, C), BF16)],
        scratch_shapes=[pltpu.VMEM((tm, D), BF16), pltpu.VMEM((tm, D), F32)],
        compiler_params=_params("parallel", "arbitrary"),
    )(*args)


def _ffn_bwd(dxo, x, g, zg, zu, ga, gb, f, tm, dep=None):
    T, D = x.shape
    nj, C = ga.shape[0] // 2, ga.shape[3]

    def body(dxo_ref, x_ref, g_ref, zg_ref, zu_ref, wg_ref, wu_ref, wo_ref,
             dx_ref, dy_ref, dzg_ref, dzu_ref, dgn_ref, dy_scr, acc):
        i, j = pl.program_id(0), pl.program_id(1)

        @pl.when(j == 0)
        def _():
            dyb = (0.5 * dxo_ref[...]).astype(BF16)
            dy_scr[...] = dyb
            dy_ref[...] = dyb
            acc[...] = jnp.zeros_like(acc)

        wo = wo_ref[...].reshape(C, D)
        for part in range(FFN_PARTS):
            sl = pl.ds(part * (tm // FFN_PARTS), tm // FFN_PARTS)
            ds = _dot_nt(dy_scr[sl, :], wo)
            gt = zg_ref[sl, :].astype(F32)
            up = zu_ref[sl, :].astype(F32)
            sg = _sigmoid(gt)
            dgt = (ds * up * (sg * (1.0 + gt * (1.0 - sg)))).astype(BF16)
            dup = (ds * (gt * sg)).astype(BF16)
            dzg_ref[sl, :] = dgt
            dzu_ref[sl, :] = dup
            acc[sl, :] += _dot_nt(dgt, wg_ref[...]) + _dot_nt(dup, wu_ref[...])

        @pl.when(j == nj - 1)
        def _():
            dx, dg = _norm_bwd(acc[...], x_ref[...], g_ref[...])
            dx_ref[...] = dxo_ref[...] + dx

            @pl.when(i == 0)
            def _():
                dgn_ref[...] = dg

            @pl.when(i > 0)
            def _():
                dgn_ref[...] += dg

    tok = pl.BlockSpec((tm, D), lambda i, j: (i, 0))
    chunk = pl.BlockSpec((None, tm, C), lambda i, j: (j, i, 0))
    row = pl.BlockSpec((1, D), lambda i, j: (0, 0))
    in_specs = [tok, tok, row, chunk, chunk] + _ffn_weight_specs(f, nj, D, C)
    body, in_specs, args = _with_dep(body, dep, in_specs, [dxo, x, g, zg, zu, ga, ga, gb])
    return pl.pallas_call(
        body, name="ffn_bwd", grid=(T // tm, nj),
        in_specs=in_specs,
        out_specs=[tok, tok, chunk, chunk, row],
        out_shape=[jax.ShapeDtypeStruct((T, D), F32), jax.ShapeDtypeStruct((T, D), BF16),
                   jax.ShapeDtypeStruct((nj, T, C), BF16), jax.ShapeDtypeStruct((nj, T, C), BF16),
                   jax.ShapeDtypeStruct((1, D), F32)],
        scratch_shapes=[pltpu.VMEM((tm, D), BF16), pltpu.VMEM((tm, D), F32)],
        compiler_params=_params("arbitrary", "arbitrary"),
    )(*args)


def _ffn_dw(h, dzg, dzu, s, dy, tk):
    T, D = h.shape
    nj, C = s.shape[0], s.shape[2]
    nk = T // tk

    def body(h_ref, dzg_ref, dzu_ref, s_ref, dy_ref, dwin_ref, dwo_ref, ag, au, ao):
        k = pl.program_id(1)

        @pl.when(k == 0)
        def _():
            ag[...] = jnp.zeros_like(ag)
            au[...] = jnp.zeros_like(au)
            ao[...] = jnp.zeros_like(ao)

        hb = h_ref[...]
        ag[...] += _dot_tn(hb, dzg_ref[...])
        au[...] += _dot_tn(hb, dzu_ref[...])
        ao[...] += _dot_tn(s_ref[...], dy_ref[...])

        @pl.when(k == nk - 1)
        def _():
            dwin_ref[0] = ag[...].astype(BF16)
            dwin_ref[1] = au[...].astype(BF16)
            dwo_ref[...] = ao[...].astype(BF16)

    tok = pl.BlockSpec((tk, D), lambda j, k: (k, 0))
    chunk = pl.BlockSpec((None, tk, C), lambda j, k: (j, k, 0))
    dwin, dwo = pl.pallas_call(
        body, name="ffn_dw", grid=(nj, nk),
        in_specs=[tok, chunk, chunk, chunk, tok],
        out_specs=[pl.BlockSpec((2, None, D, C), lambda j, k: (0, j, 0, 0)),
                   pl.BlockSpec((None, C, D), lambda j, k: (j, 0, 0))],
        out_shape=[jax.ShapeDtypeStruct((2, nj, D, C), BF16), jax.ShapeDtypeStruct((nj, C, D), BF16)],
        scratch_shapes=[pltpu.VMEM((D, C), F32), pltpu.VMEM((D, C), F32), pltpu.VMEM((C, D), F32)],
        compiler_params=_params("parallel", "arbitrary"),
    )(h, dzg, dzu, s, dy)
    return dwin.reshape(2 * nj, D, C), dwo


def _matmul_tn(a, b, tn, tk):
    T, Ka = a.shape
    N = b.shape[1]
    nk = T // tk

    def body(a_ref, b_ref, o_ref, acc):
        k = pl.program_id(1)

        @pl.when(k == 0)
        def _():
            acc[...] = jnp.zeros_like(acc)

        acc[...] += _dot_tn(a_ref[...], b_ref[...])

        @pl.when(k == nk - 1)
        def _():
            o_ref[...] = acc[...].astype(BF16)

    return pl.pallas_call(
        body, name="matmul_tn", grid=(N // tn, nk),
        in_specs=[pl.BlockSpec((tk, Ka), lambda n, k: (k, 0)), pl.BlockSpec((tk, tn), lambda n, k: (k, n))],
        out_specs=pl.BlockSpec((Ka, tn), lambda n, k: (0, n)),
        out_shape=jax.ShapeDtypeStruct((Ka, N), BF16),
        scratch_shapes=[pltpu.VMEM((Ka, tn), F32)],
        compiler_params=_params("parallel", "arbitrary"),
    )(a, b)


def _qkv_fwd(x, g, w, tm):
    T, D = x.shape
    N = w.shape[1]

    def body(x_ref, g_ref, w_ref, o_ref, h_ref):
        xv = x_ref[...]
        hb = (xv * _rstd(xv) * g_ref[...]).astype(BF16)
        h_ref[...] = hb
        o_ref[...] = _dot(hb, w_ref[...])

    return pl.pallas_call(
        body, name="qkv_fwd", grid=(T // tm,),
        in_specs=[pl.BlockSpec((tm, D), lambda i: (i, 0)), pl.BlockSpec((1, D), lambda i: (0, 0)),
                  pl.BlockSpec((D, N), lambda i: (0, 0))],
        out_specs=[pl.BlockSpec((tm, N), lambda i: (i, 0)), pl.BlockSpec((tm, D), lambda i: (i, 0))],
        out_shape=[jax.ShapeDtypeStruct((T, N), F32), jax.ShapeDtypeStruct((T, D), BF16)],
        compiler_params=_params("parallel"),
    )(x, g, w)


def _attn_prep(qkv, gains2, tm):
    T = qkv.shape[0]
    scale = HEAD_DIM ** -0.5

    def body(qkv_ref, g_ref, qa_ref, ka_ref, va_ref, qb_ref, kb_ref, vb_ref):
        lo = _lo_mask((tm, PAIR))

        def normed(c, gi, mult):
            xv = qkv_ref[:, c * PAIR:(c + 1) * PAIR]
            r = lax.rsqrt(_half_sum(xv * xv, lo) * (1.0 / HEAD_DIM) + EPS)
            y = xv * r * g_ref[gi:gi + 1, :]
            return y * mult if mult != 1.0 else y

        def both_halves(v):
            sw = pltpu.roll(v, HEAD_DIM, 1)
            return jnp.where(lo, v, sw), jnp.where(lo, sw, v)

        for c in range(4):
            qa_ref[c] = normed(c, 0, scale).astype(BF16)
            ka_ref[c] = normed(4 + c, 1, 1.0).astype(BF16)
            va_ref[c] = qkv_ref[:, (8 + c) * PAIR:(9 + c) * PAIR].astype(BF16)
            qb_ref[c] = normed(12 + c, 2, scale).astype(BF16)
        k0, k1 = both_halves(normed(16, 3, 1.0))
        kb_ref[0] = k0.astype(BF16)
        kb_ref[1] = k1.astype(BF16)
        v0, v1 = both_halves(qkv_ref[:, 17 * PAIR:18 * PAIR])
        vb_ref[0] = v0.astype(BF16)
        vb_ref[1] = v1.astype(BF16)

    four = pl.BlockSpec((4, tm, PAIR), lambda i: (0, i, 0))
    two = pl.BlockSpec((2, tm, PAIR), lambda i: (0, i, 0))
    s4 = jax.ShapeDtypeStruct((4, T, PAIR), BF16)
    s2 = jax.ShapeDtypeStruct((2, T, PAIR), BF16)
    return pl.pallas_call(
        body, name="attn_prep", grid=(T // tm,),
        in_specs=[pl.BlockSpec((tm, qkv.shape[1]), lambda i: (i, 0)), pl.BlockSpec((4, PAIR), lambda i: (0, 0))],
        out_specs=[four, four, four, four, two, two],
        out_shape=[s4, s4, s4, s4, s2, s2],
        compiler_params=_params("parallel"),
    )(qkv, gains2)


def _loop_blocks(nb, body, init, per_iter):
    u = math.gcd(nb, per_iter)

    def outer(i, carry):
        for k in range(u):
            carry = body(i * u + k, carry)
        return carry

    return lax.fori_loop(0, nb // u, outer, init)


def _edge_variant(b, nb):
    return (b == 0).astype(jnp.int32) + 2 * (b == nb - 1).astype(jnp.int32)


def _stack_heads(v, lo):
    z = jnp.zeros_like(v)
    return jnp.concatenate([jnp.where(lo, v, z), jnp.where(lo, z, v)], axis=0)


def _unstack_heads(v2, lo):
    return jnp.where(lo, v2[:BQ], v2[BQ:])


def _row_vector(v, lo):
    r = lax.broadcasted_iota(jnp.int32, (BQ, PAIR), 0)
    ln = lax.broadcasted_iota(jnp.int32, (BQ, PAIR), 1)
    diag = (ln % HEAD_DIM) == (r % HEAD_DIM)
    top = jnp.sum(jnp.where(diag & (r < HEAD_DIM), v, 0.0), axis=0, keepdims=True)
    bot = jnp.sum(jnp.where(diag & (r >= HEAD_DIM), v, 0.0), axis=0, keepdims=True)
    top8, bot8 = jnp.broadcast_to(top, (8, PAIR)), jnp.broadcast_to(bot, (8, PAIR))
    lo8 = _lo_mask((8, PAIR))
    head0 = jnp.where(lo8, top8, pltpu.roll(bot8, HEAD_DIM, 1))
    head1 = jnp.where(lo8, pltpu.roll(top8, HEAD_DIM, 1), bot8)
    return jnp.concatenate([head0, head1], axis=1)[:1]


def _units_per_step(nb, pairs_per_kv):
    return max(1, 16 // nb) if pairs_per_kv == 1 else 1


def _attn_fwd(q, kp, vp, bias4, sink, R, pairs_per_kv, pairs_per_bias):
    N, L, _ = q.shape
    W = BQ + 2 * R
    nb = L // BQ
    G = _units_per_step(nb, pairs_per_kv)

    def body(sink_ref, q_ref, k_ref, v_ref, bias_ref, o_ref, lse_ref):
        n = pl.program_id(0)
        lo_q = _lo_mask((BQ, PAIR))
        first = lax.broadcasted_iota(jnp.int32, (2 * BQ, 1), 0) < BQ

        def blk(f, carry):
            g, b = f // nb, f % nb
            u = n * G + g
            sk = jnp.where(first, sink_ref[2 * u], sink_ref[2 * u + 1])
            q0 = pl.multiple_of(b * BQ, BQ)
            q2 = _stack_heads(q_ref[g, pl.ds(q0, BQ), :], lo_q)
            kw = k_ref[g, pl.ds(q0, W), :]
            vw = v_ref[g, pl.ds(q0, W), :]
            s = _dot_nt(q2, kw) + bias_ref[_edge_variant(b, nb)]
            m = jnp.maximum(jnp.max(s, axis=1, keepdims=True), sk)
            p = jnp.exp(s - m)
            l = jnp.sum(p, axis=1, keepdims=True) + jnp.exp(sk - m)
            o2 = _dot(p.astype(BF16), vw) / l
            o_ref[g, pl.ds(q0, BQ), :] = _unstack_heads(o2, lo_q)
            lse_ref[g, pl.ds(q0, BQ), :] = _unstack_heads(jnp.broadcast_to(m + jnp.log(l), (2 * BQ, PAIR)), lo_q)
            return carry

        _loop_blocks(G * nb, blk, 0, 4)

    qspec = pl.BlockSpec((G, L, PAIR), lambda n: (n, 0, 0))
    kspec = pl.BlockSpec((G, L + 2 * R, PAIR), lambda n: (n // pairs_per_kv, 0, 0))
    return pl.pallas_call(
        body, name="attn_fwd", grid=(N // G,),
        in_specs=[pl.BlockSpec(memory_space=pltpu.SMEM), qspec, kspec, kspec,
                  pl.BlockSpec((None, 4, 2 * BQ, W), lambda n: (n * G // pairs_per_bias, 0, 0, 0))],
        out_specs=[qspec, qspec],
        out_shape=[jax.ShapeDtypeStruct((N, L, PAIR), F32), jax.ShapeDtypeStruct((N, L, PAIR), F32)],
        compiler_params=_params("parallel"),
    )(sink, q, kp, vp, bias4)


def _attn_bwd(q, kp, vp, bias4t, sink, o, lse, do, R, pairs_per_kv, pairs_per_bias):
    N, L, _ = q.shape
    Nk = kp.shape[0]
    Pb = bias4t.shape[0]
    W = BQ + 2 * R
    nb = L // BQ
    G = _units_per_step(nb, pairs_per_kv)

    def body(sink_ref, q_ref, k_ref, v_ref, bias_ref, o_ref, lse_ref, do_ref,
             dq_ref, dk_ref, dv_ref, dbias_ref, dsink_ref):
        n = pl.program_id(0)
        lo_q = _lo_mask((BQ, PAIR))
        first = lax.broadcasted_iota(jnp.int32, (1, 2 * BQ), 1) < BQ
        dsink_ref[...] = jnp.zeros_like(dsink_ref)

        @pl.when(n % pairs_per_kv == 0)
        def _():
            dk_ref[...] = jnp.zeros_like(dk_ref)
            dv_ref[...] = jnp.zeros_like(dv_ref)

        @pl.when((n * G) % pairs_per_bias == 0)
        def _():
            dbias_ref[...] = jnp.zeros_like(dbias_ref)

        def blk(f, carry):
            g, b = f // nb, f % nb
            u = n * G + g
            sk = jnp.where(first, sink_ref[2 * u], sink_ref[2 * u + 1])
            q0 = pl.multiple_of(b * BQ, BQ)
            q2 = _stack_heads(q_ref[g, pl.ds(q0, BQ), :], lo_q)
            kw = k_ref[g, pl.ds(q0, W), :]
            vw = v_ref[g, pl.ds(q0, W), :]
            dov = do_ref[g, pl.ds(q0, BQ), :]
            lse = _row_vector(lse_ref[g, pl.ds(q0, BQ), :], lo_q)
            delta = _row_vector(_half_sum(dov * o_ref[g, pl.ds(q0, BQ), :], lo_q), lo_q)
            do2 = _stack_heads(dov.astype(BF16), lo_q)
            st = _dot_nt(kw, q2) + bias_ref[_edge_variant(b, nb)]
            pt = jnp.exp(st - lse)
            dst = pt * (_dot_nt(vw, do2) - delta)
            dstb = dst.astype(BF16)
            dbias_ref[...] += dst
            dk_ref[g, pl.ds(q0, W), :] += _dot(dstb, q2)
            dv_ref[g, pl.ds(q0, W), :] += _dot(pt.astype(BF16), do2)
            dq_ref[g, pl.ds(q0, BQ), :] = _unstack_heads(_dot_tn(dstb, kw), lo_q)
            dsink_ref[g, pl.ds(0, 1), :] -= jnp.exp(sk - lse) * delta
            return carry

        _loop_blocks(G * nb, blk, 0, 4)

    qspec = pl.BlockSpec((G, L, PAIR), lambda n: (n, 0, 0))
    kspec = pl.BlockSpec((G, L + 2 * R, PAIR), lambda n: (n // pairs_per_kv, 0, 0))
    return pl.pallas_call(
        body, name="attn_bwd", grid=(N // G,),
        in_specs=[pl.BlockSpec(memory_space=pltpu.SMEM), qspec, kspec, kspec,
                  pl.BlockSpec((None, 4, W, 2 * BQ), lambda n: (n * G // pairs_per_bias, 0, 0, 0)),
                  qspec, qspec, qspec],
        out_specs=[qspec, kspec, kspec,
                   pl.BlockSpec((None, W, 2 * BQ), lambda n: (n * G // pairs_per_bias, 0, 0)),
                   pl.BlockSpec((G, 8, 2 * BQ), lambda n: (n, 0, 0))],
        out_shape=[jax.ShapeDtypeStruct((N, L, PAIR), F32),
                   jax.ShapeDtypeStruct((Nk, L + 2 * R, PAIR), F32),
                   jax.ShapeDtypeStruct((Nk, L + 2 * R, PAIR), F32),
                   jax.ShapeDtypeStruct((Pb, W, 2 * BQ), F32),
                   jax.ShapeDtypeStruct((N, 8, 2 * BQ), F32)],
        compiler_params=_params("arbitrary"),
    )(sink, q, kp, vp, bias4t, o, lse, do)


def _attn_merge(o1, l1, o4, l4, o16, l16, ob, tm):
    T = o1.shape[1]

    def body(o1_ref, l1_ref, o4_ref, l4_ref, o16_ref, l16_ref, ob_ref, oa_ref, la_ref, cat_ref):
        for c in range(4):
            a, b, d = l1_ref[c], l4_ref[c], l16_ref[c]
            m = jnp.maximum(jnp.maximum(a, b), d)
            wa, wb, wd = jnp.exp(a - m), jnp.exp(b - m), jnp.exp(d - m)
            z = wa + wb + wd
            o = (wa * o1_ref[c] + wb * o4_ref[c] + wd * o16_ref[c]) / z
            oa_ref[c] = o
            la_ref[c] = m + jnp.log(z)
            cat_ref[:, c * PAIR:(c + 1) * PAIR] = o.astype(BF16)
            cat_ref[:, (4 + c) * PAIR:(5 + c) * PAIR] = ob_ref[c].astype(BF16)

    four = pl.BlockSpec((4, tm, PAIR), lambda i: (0, i, 0))
    s4 = jax.ShapeDtypeStruct((4, T, PAIR), F32)
    return pl.pallas_call(
        body, name="attn_merge", grid=(T // tm,),
        in_specs=[four] * 7,
        out_specs=[four, four, pl.BlockSpec((tm, 8 * PAIR), lambda i: (i, 0))],
        out_shape=[s4, s4, jax.ShapeDtypeStruct((T, 8 * PAIR), BF16)],
        compiler_params=_params("parallel"),
    )(o1, l1, o4, l4, o16, l16, ob)


def _weight_arg(w, blk):
    if blk is None:
        return pl.BlockSpec(w.shape, lambda i: (0, 0)), (lambda ref: ref[...])
    D = w.shape[2]
    return (pl.BlockSpec((N_DEV, 128, D), lambda i: (0, blk, 0)),
            lambda ref: ref[...].reshape(N_DEV * 128, D))


def _oproj_fwd(x, o_cat, w, blk, tm):
    T, D = x.shape
    wspec, wload = _weight_arg(w, blk)

    def body(x_ref, o_ref, w_ref, out_ref):
        out_ref[...] = x_ref[...] + _dot(o_ref[...], wload(w_ref))

    tok = pl.BlockSpec((tm, D), lambda i: (i, 0))
    return pl.pallas_call(
        body, name="oproj_fwd", grid=(T // tm,),
        in_specs=[tok, pl.BlockSpec((tm, o_cat.shape[1]), lambda i: (i, 0)), wspec],
        out_specs=tok, out_shape=jax.ShapeDtypeStruct((T, D), F32),
        compiler_params=_params("parallel"),
    )(x, o_cat, w)


def _oproj_bwd(dx, w, blk, tm, dep=None):
    T, D = dx.shape
    wspec, wload = _weight_arg(w, blk)

    def body(dx_ref, w_ref, dxb_ref, do_ref):
        db = dx_ref[...].astype(BF16)
        dxb_ref[...] = db
        do = _dot_nt(db, wload(w_ref))
        for c in range(8):
            do_ref[c] = do[:, c * PAIR:(c + 1) * PAIR]

    tok = pl.BlockSpec((tm, D), lambda i: (i, 0))
    body, in_specs, args = _with_dep(body, dep, [tok, wspec], [dx, w])
    return pl.pallas_call(
        body, name="oproj_bwd", grid=(T // tm,),
        in_specs=in_specs,
        out_specs=[tok, pl.BlockSpec((8, tm, PAIR), lambda i: (0, i, 0))],
        out_shape=[jax.ShapeDtypeStruct((T, D), BF16), jax.ShapeDtypeStruct((8, T, PAIR), F32)],
        compiler_params=_params("parallel"),
    )(*args)


def _attn_post(qkv, gains2, dqa, dka, dva, dqb, dkb, dvb, tm):
    T, NQ = qkv.shape
    scale = HEAD_DIM ** -0.5

    def body(qkv_ref, g_ref, qa1, qa4, qa16, ka1, ka4, ka16, va1, va4, va16, qb_ref, kb_ref, vb_ref,
             out_ref, dg_ref):
        lo = _lo_mask((tm, PAIR))

        @pl.when(pl.program_id(0) == 0)
        def _():
            dg_ref[...] = jnp.zeros_like(dg_ref)

        def norm_bwd(c, gi, dy):
            xv = qkv_ref[:, c * PAIR:(c + 1) * PAIR]
            r = lax.rsqrt(_half_sum(xv * xv, lo) * (1.0 / HEAD_DIM) + EPS)
            xn = xv * r
            dg_ref[gi:gi + 1, :] += jnp.sum(dy * xn, axis=0, keepdims=True)
            dxn = dy * g_ref[gi:gi + 1, :]
            dx = r * (dxn - xn * (_half_sum(dxn * xn, lo) * (1.0 / HEAD_DIM)))
            out_ref[:, c * PAIR:(c + 1) * PAIR] = dx.astype(BF16)

        def fold(v):
            return v + pltpu.roll(v, HEAD_DIM, 1)

        for c in range(4):
            norm_bwd(c, 0, (qa1[c] + qa4[c] + qa16[c]) * scale)
            norm_bwd(4 + c, 1, ka1[c] + ka4[c] + ka16[c])
            out_ref[:, (8 + c) * PAIR:(9 + c) * PAIR] = (va1[c] + va4[c] + va16[c]).astype(BF16)
            norm_bwd(12 + c, 2, qb_ref[c] * scale)
        norm_bwd(16, 3, jnp.where(lo, fold(kb_ref[0]), fold(kb_ref[1])))
        out_ref[:, 17 * PAIR:18 * PAIR] = jnp.where(lo, fold(vb_ref[0]), fold(vb_ref[1])).astype(BF16)

    four = pl.BlockSpec((4, tm, PAIR), lambda i: (0, i, 0))
    two = pl.BlockSpec((2, tm, PAIR), lambda i: (0, i, 0))
    return pl.pallas_call(
        body, name="attn_post", grid=(T // tm,),
        in_specs=[pl.BlockSpec((tm, NQ), lambda i: (i, 0)), pl.BlockSpec((4, PAIR), lambda i: (0, 0))]
        + [four] * 10 + [two, two],
        out_specs=[pl.BlockSpec((tm, NQ), lambda i: (i, 0)), pl.BlockSpec((4, PAIR), lambda i: (0, 0))],
        out_shape=[jax.ShapeDtypeStruct((T, NQ), BF16), jax.ShapeDtypeStruct((4, PAIR), F32)],
        compiler_params=_params("arbitrary"),
    )(qkv, gains2, *dqa, *dka, *dva, dqb, dkb, dvb)


def _dense_norm_bwd(dres, dz, w, blk, x, g, tm):
    T, D = x.shape
    N = dz.shape[1]
    wspec, wload = _weight_arg(w, blk)

    def body(dres_ref, dz_ref, w_ref, x_ref, g_ref, dx_ref, dgn_ref):
        i = pl.program_id(0)
        dx, dg = _norm_bwd(_dot_nt(dz_ref[...], wload(w_ref)), x_ref[...], g_ref[...])
        dx_ref[...] = dres_ref[...] + dx

        @pl.when(i == 0)
        def _():
            dgn_ref[...] = dg

        @pl.when(i > 0)
        def _():
            dgn_ref[...] += dg

    tok = pl.BlockSpec((tm, D), lambda i: (i, 0))
    row = pl.BlockSpec((1, D), lambda i: (0, 0))
    return pl.pallas_call(
        body, name="dense_norm_bwd", grid=(T // tm,),
        in_specs=[tok, pl.BlockSpec((tm, N), lambda i: (i, 0)), wspec, tok, row],
        out_specs=[tok, row],
        out_shape=[jax.ShapeDtypeStruct((T, D), F32), jax.ShapeDtypeStruct((1, D), F32)],
        compiler_params=_params("arbitrary"),
    )(dres, dz, w, x, g)


def _bias_reduce(onehot, dbm):
    Hb, K = dbm.shape

    def body(oh_ref, d_ref, out_ref):
        oh = oh_ref[...]
        d = d_ref[...]
        hi = d.astype(BF16)
        r1 = d - hi.astype(F32)
        mid = r1.astype(BF16)
        low = (r1 - mid.astype(F32)).astype(BF16)
        out_ref[...] = _dot_nt(hi, oh) + _dot_nt(mid, oh) + _dot_nt(low, oh)

    vm = pl.BlockSpec(memory_space=pltpu.VMEM)
    return pl.pallas_call(
        body, name="bias_reduce", in_specs=[vm, vm], out_specs=vm,
        out_shape=jax.ShapeDtypeStruct((Hb, 128), F32),
        compiler_params=pltpu.CompilerParams(vmem_limit_bytes=VMEM_LIMIT),
    )(onehot, dbm)


def _ple_fwd(x, g, wg, blk, p, wp, target, tm):
    T, D = x.shape
    P = p.shape[1]
    with_loss = target is not None
    wspec, wload = _weight_arg(wg, blk)

    def body(*refs):
        if with_loss:
            x_ref, g_ref, wg_ref, p_ref, wp_ref, t_ref, y_ref, hn_ref, gate_ref, pp_ref, pb_ref, loss_ref = refs
        else:
            x_ref, g_ref, wg_ref, p_ref, wp_ref, y_ref, hn_ref, gate_ref, pp_ref, pb_ref = refs
        i = pl.program_id(0)
        xv = x_ref[...]
        hb = (xv * _rstd(xv) * g_ref[...]).astype(BF16)
        hn_ref[...] = hb
        gate = _sigmoid(_dot(hb, wload(wg_ref)))
        pb = p_ref[...].astype(BF16)
        pb_ref[...] = pb
        pp = _dot(pb, wp_ref[...])
        gate_ref[...] = gate
        pp_ref[...] = pp
        y = xv + gate * pp
        if with_loss:
            err = y - t_ref[...]
            y_ref[...] = err * (1.0 / D)
            part = jnp.broadcast_to(0.5 * jnp.sum(jnp.sum(err * err, axis=1, keepdims=True) * (1.0 / D),
                                                  axis=0, keepdims=True), (1, 128))

            @pl.when(i == 0)
            def _():
                loss_ref[...] = part

            @pl.when(i > 0)
            def _():
                loss_ref[...] += part
        else:
            y_ref[...] = y

    tok = pl.BlockSpec((tm, D), lambda i: (i, 0))
    ptok = pl.BlockSpec((tm, P), lambda i: (i, 0))
    in_specs = [tok, pl.BlockSpec((1, D), lambda i: (0, 0)), wspec, ptok,
                pl.BlockSpec((P, D), lambda i: (0, 0))]
    out_specs = [tok, tok, tok, tok, ptok]
    out_shape = [jax.ShapeDtypeStruct((T, D), F32), jax.ShapeDtypeStruct((T, D), BF16),
                 jax.ShapeDtypeStruct((T, D), F32), jax.ShapeDtypeStruct((T, D), F32),
                 jax.ShapeDtypeStruct((T, P), BF16)]
    args = [x, g, wg, p, wp]
    if with_loss:
        in_specs.append(tok)
        out_specs.append(pl.BlockSpec((1, 128), lambda i: (0, 0)))
        out_shape.append(jax.ShapeDtypeStruct((1, 128), F32))
        args.append(target)
    return pl.pallas_call(
        body, name="ple_fwd_loss" if with_loss else "ple_fwd", grid=(T // tm,),
        in_specs=in_specs, out_specs=out_specs, out_shape=out_shape,
        compiler_params=_params("arbitrary" if with_loss else "parallel"),
    )(*args)


def _ple_bwd(dy, gate, pp, tm, dep=None):
    T, D = dy.shape

    def body(dy_ref, gate_ref, pp_ref, dgl_ref, dpp_ref):
        d = dy_ref[...]
        gt = gate_ref[...]
        dgl_ref[...] = (d * pp_ref[...] * gt * (1.0 - gt)).astype(BF16)
        dpp_ref[...] = (d * gt).astype(BF16)

    tok = pl.BlockSpec((tm, D), lambda i: (i, 0))
    body, in_specs, args = _with_dep(body, dep, [tok, tok, tok], [dy, gate, pp])
    return pl.pallas_call(
        body, name="ple_bwd", grid=(T // tm,), in_specs=in_specs, out_specs=[tok, tok],
        out_shape=[jax.ShapeDtypeStruct((T, D), BF16), jax.ShapeDtypeStruct((T, D), BF16)],
        compiler_params=_params("parallel"),
    )(*args)


def _adamw(w, g, m, v):
    shape = w.shape
    C = shape[-1]
    w2, g2, m2, v2 = (a.reshape(-1, C) for a in (w, g, m, v))
    Rn = w2.shape[0]
    tr = Rn
    for cand in (512, 352, 256):
        if Rn % cand == 0:
            tr = cand
            break
    c1 = 1.0 - ADAM_B1 ** ADAM_STEP
    c2 = 1.0 - ADAM_B2 ** ADAM_STEP

    def body(w_ref, g_ref, m_ref, v_ref, d_ref, nm_ref, nv_ref):
        gv = g_ref[...]
        mn = ADAM_B1 * m_ref[...] + (1.0 - ADAM_B1) * gv
        vn = ADAM_B2 * v_ref[...] + (1.0 - ADAM_B2) * (gv * gv)
        d_ref[...] = -ADAM_LR * ((mn / c1) / (jnp.sqrt(vn / c2) + ADAM_EPS) + ADAM_WD * w_ref[...])
        nm_ref[...] = mn
        nv_ref[...] = vn

    spec = pl.BlockSpec((tr, C), lambda i: (i, 0))
    sh = jax.ShapeDtypeStruct((Rn, C), F32)
    d, nm, nv = pl.pallas_call(
        body, name="adamw", grid=(Rn // tr,), in_specs=[spec] * 4, out_specs=[spec] * 3, out_shape=[sh] * 3,
        compiler_params=_params("parallel"),
    )(w2, g2, m2, v2)
    return d.reshape(shape), nm.reshape(shape), nv.reshape(shape)


def _my_place():
    x, y, c = lax.axis_index("x"), lax.axis_index("y"), lax.axis_index("c")
    chips = [(1 - x, y), (x, 1 - y), (1 - x, 1 - y)]
    return x, y, c, chips


def _all_gather(flat):
    R, Wd = flat.shape

    def body(x_ref, out_ref, send_sems, recv_sems, local_sem):
        x, y, c, chips = _my_place()
        me, sibling = (x, y, c), (x, y, 1 - c)

        def rows(px, py, pc):
            return out_ref.at[4 * px + 2 * py + pc]

        def copy(k, block, to, src=None):
            return pltpu.make_async_remote_copy(
                src_ref=rows(*block) if src is None else src, dst_ref=rows(*block),
                send_sem=send_sems.at[k], recv_sem=recv_sems.at[k], device_id=to, device_id_type=MESH)

        mine = pltpu.make_async_copy(x_ref, rows(*me), local_sem)
        mine.start()
        first = [copy(0, me, sibling, src=x_ref)]
        first += [copy(1 + j, me, (*chip, c), src=x_ref) for j, chip in enumerate(chips)]
        for cp in first:
            cp.start()
        passed = [copy(4 + j, (*chip, c), sibling) for j, chip in enumerate(chips)]
        for j, chip in enumerate(chips):
            copy(1 + j, (*chip, c), me).wait_recv()
            passed[j].start()
        copy(0, sibling, me).wait_recv()
        for j, chip in enumerate(chips):
            copy(4 + j, (*chip, 1 - c), me).wait_recv()
        for cp in first + passed:
            cp.wait_send()
        mine.wait()

    return pl.pallas_call(
        body, name="all_gather",
        in_specs=[pl.BlockSpec(memory_space=pl.ANY)], out_specs=pl.BlockSpec(memory_space=pl.ANY),
        out_shape=jax.ShapeDtypeStruct((N_DEV, R, Wd), flat.dtype),
        scratch_shapes=[pltpu.SemaphoreType.DMA((7,)), pltpu.SemaphoreType.DMA((7,)), pltpu.SemaphoreType.DMA],
    )(flat)


def _reduce_scatter(gparts, tr):
    _, R, Wd = gparts.shape
    nt = R // tr

    def body(g_ref, out_ref, a_ref, p_ref, b_ref, vb, vo_b, vo_f, d2d_send, d2d_recv, ici_send, ici_recv):
        x, y, c, chips = _my_place()
        sibling = (x, y, 1 - c)
        allchips = [(x, y)] + chips

        def dev(chip, pc):
            return 4 * chip[0] + 2 * chip[1] + pc

        d2d = [pltpu.make_async_remote_copy(
            src_ref=g_ref.at[dev(q, 1 - c)], dst_ref=a_ref.at[a], send_sem=d2d_send.at[a], recv_sem=d2d_recv.at[a],
            device_id=sibling, device_id_type=MESH) for a, q in enumerate(allchips)]
        for cp in d2d:
            cp.start()

        def add_tiles(srcs, dst, vo):
            def step(t, carry):
                r = pl.ds(pl.multiple_of(t * tr, tr), tr)
                acc = None
                for s_i, src in enumerate(srcs):
                    pltpu.sync_copy(src.at[r], vb.at[s_i])
                for s_i in range(len(srcs)):
                    term = vb[s_i].astype(F32)
                    acc = term if acc is None else acc + term
                vo[...] = acc.astype(vo.dtype)
                pltpu.sync_copy(vo, dst.at[r])
                return carry

            lax.fori_loop(0, nt, step, 0)

        ici = []
        for j, q in enumerate(chips):
            d2d[j + 1].wait_recv()
            add_tiles([g_ref.at[dev(q, c)], a_ref.at[j + 1]], p_ref.at[j], vo_b)
            cp = pltpu.make_async_remote_copy(
                src_ref=p_ref.at[j], dst_ref=b_ref.at[j], send_sem=ici_send.at[j], recv_sem=ici_recv.at[j],
                device_id=(*q, c), device_id_type=MESH)
            cp.start()
            ici.append(cp)
        d2d[0].wait_recv()
        for cp in ici:
            cp.wait_recv()
        add_tiles([g_ref.at[dev((x, y), c)], a_ref.at[0], b_ref.at[0], b_ref.at[1], b_ref.at[2]], out_ref, vo_f)
        for cp in d2d + ici:
            cp.wait_send()

    hbm = pl.BlockSpec(memory_space=pl.ANY)
    out, _, _, _ = pl.pallas_call(
        body, name="reduce_scatter",
        in_specs=[hbm], out_specs=[hbm, hbm, hbm, hbm],
        out_shape=[jax.ShapeDtypeStruct((R, Wd), F32), jax.ShapeDtypeStruct((4, R, Wd), BF16),
                   jax.ShapeDtypeStruct((3, R, Wd), BF16), jax.ShapeDtypeStruct((3, R, Wd), BF16)],
        scratch_shapes=[pltpu.VMEM((5, tr, Wd), BF16), pltpu.VMEM((tr, Wd), BF16), pltpu.VMEM((tr, Wd), F32),
                        pltpu.SemaphoreType.DMA((4,)), pltpu.SemaphoreType.DMA((4,)),
                        pltpu.SemaphoreType.DMA((3,)), pltpu.SemaphoreType.DMA((3,))],
        compiler_params=pltpu.CompilerParams(vmem_limit_bytes=VMEM_LIMIT),
    )(gparts)
    return out


def _peer(x, y, c, k):
    return (x ^ ((k >> 2) & 1), y ^ ((k >> 1) & 1), c ^ (k & 1))


HBM_SPEC = pl.BlockSpec(memory_space=pltpu.HBM)
SEM_SPEC = pl.BlockSpec(memory_space=pltpu.SEMAPHORE)


def _exchange_refs(srcs, lands, m, k, x, y, c, scatter):
    peer = _peer(x, y, c, k)
    if scatter:
        return srcs[m].at[4 * peer[0] + 2 * peer[1] + peer[2]], lands[m].at[k - 1], peer
    return srcs[m], lands[m].at[4 * x + 2 * y + c], peer


def _exchange_start(arrs, land_shapes, scatter, name):
    n = len(arrs)

    def body(*refs):
        srcs, lands = refs[:n], refs[n:2 * n]
        send_sems, recv_sems = refs[2 * n], refs[2 * n + 1]
        token = refs[-1]
        x, y, c, _ = _my_place()
        for m in range(n):
            for k in range(1, N_DEV):
                src, dst, peer = _exchange_refs(srcs, lands, m, k, x, y, c, scatter)
                pltpu.make_async_remote_copy(
                    src_ref=src, dst_ref=dst, send_sem=send_sems.at[7 * m + k - 1],
                    recv_sem=recv_sems.at[7 * m + k - 1], device_id=peer, device_id_type=MESH).start()
        token[...] = jnp.zeros_like(token)

    zones = [lax.empty(s_, a.dtype) for s_, a in zip(land_shapes, arrs)]
    outs = pl.pallas_call(
        body, name=name,
        out_shape=(pltpu.SemaphoreType.DMA((7 * n,)), pltpu.SemaphoreType.DMA((7 * n,)),
                   *[pltpu.HBM(a.shape, a.dtype) for a in arrs], *[pltpu.HBM(z.shape, z.dtype) for z in zones],
                   jax.ShapeDtypeStruct((8, 128), F32)),
        in_specs=[HBM_SPEC] * (2 * n),
        out_specs=(SEM_SPEC, SEM_SPEC, *[HBM_SPEC] * (2 * n), pl.BlockSpec(memory_space=pltpu.VMEM)),
        input_output_aliases={m: 2 + m for m in range(2 * n)},
        compiler_params=pltpu.CompilerParams(has_side_effects=pltpu.SideEffectType.DATAFLOW_SIDE_EFFECTING),
    )(*[pltpu.with_memory_space_constraint(a, pltpu.HBM) for a in arrs],
      *[pltpu.with_memory_space_constraint(z, pltpu.HBM) for z in zones])
    return outs[0], outs[1], list(outs[2:2 + n]), list(outs[2 + n:2 + 2 * n]), outs[-1]


def _exchange_wait(send_sems, recv_sems, arrs, zones, after, scatter, name):
    n = len(arrs)

    def body(*refs):
        srcs, lands = refs[:n], refs[n:2 * n]
        send_sems, recv_sems = refs[2 * n], refs[2 * n + 1]
        x, y, c, _ = _my_place()
        for m in range(n):
            for k in range(1, N_DEV):
                src, dst, peer = _exchange_refs(srcs, lands, m, k, x, y, c, scatter)
                cp = pltpu.make_async_remote_copy(
                    src_ref=src, dst_ref=dst, send_sem=send_sems.at[7 * m + k - 1],
                    recv_sem=recv_sems.at[7 * m + k - 1], device_id=peer, device_id_type=MESH)
                cp.wait_send()
                cp.wait_recv()

    outs = pl.pallas_call(
        body, name=name,
        out_shape=tuple(pltpu.HBM(a.shape, a.dtype) for a in list(arrs) + list(zones)),
        in_specs=[HBM_SPEC] * (2 * n) + [SEM_SPEC, SEM_SPEC, pl.BlockSpec(memory_space=pl.ANY)],
        out_specs=tuple([HBM_SPEC] * (2 * n)),
        input_output_aliases={m: m for m in range(2 * n)},
        compiler_params=pltpu.CompilerParams(has_side_effects=pltpu.SideEffectType.DATAFLOW_SIDE_EFFECTING),
    )(*arrs, *zones, send_sems, recv_sems, after)
    return list(outs[n:])


def _sum_parts(own, parts, tr, dep=None):
    R, W = own.shape

    def body(own_ref, parts_ref, out_ref):
        acc = own_ref[...].astype(F32)
        for k in range(N_DEV - 1):
            acc = acc + parts_ref[k].astype(F32)
        out_ref[...] = acc

    in_specs = [pl.BlockSpec((tr, W), lambda i: (i, 0)), pl.BlockSpec((N_DEV - 1, tr, W), lambda i: (0, i, 0))]
    body, in_specs, args = _with_dep(body, dep, in_specs, [own, parts])
    return pl.pallas_call(
        body, name="sum_parts", grid=(R // tr,),
        in_specs=in_specs,
        out_specs=pl.BlockSpec((tr, W), lambda i: (i, 0)),
        out_shape=jax.ShapeDtypeStruct((R, W), F32),
        compiler_params=_params("parallel"),
    )(*args)


def _all_reduce_small(v):
    Rn, Wd = v.shape

    def body(v_ref, out_ref, gat_ref, send_sems, recv_sems):
        x, y, c, _ = _my_place()
        me = 4 * x + 2 * y + c
        gat_ref[me] = v_ref[...]
        copies = []
        for k in range(1, N_DEV):
            fx, fy, fc = (k >> 2) & 1, (k >> 1) & 1, k & 1
            peer = (x ^ fx, y ^ fy, c ^ fc)
            cp = pltpu.make_async_remote_copy(
                src_ref=v_ref, dst_ref=gat_ref.at[me], send_sem=send_sems.at[k - 1], recv_sem=recv_sems.at[k - 1],
                device_id=peer, device_id_type=MESH)
            cp.start()
            copies.append(cp)
        for cp in copies:
            cp.wait_recv()
        for cp in copies:
            cp.wait_send()
        acc = gat_ref[0]
        for k in range(1, N_DEV):
            acc = acc + gat_ref[k]
        out_ref[...] = acc

    vm = pl.BlockSpec(memory_space=pltpu.VMEM)
    return pl.pallas_call(
        body, name="all_reduce_small", in_specs=[vm], out_specs=vm,
        out_shape=jax.ShapeDtypeStruct((Rn, Wd), F32),
        scratch_shapes=[pltpu.VMEM((N_DEV, Rn, Wd), F32), pltpu.SemaphoreType.DMA((7,)),
                        pltpu.SemaphoreType.DMA((7,))],
    )(v)


def _t5_bucket(rel):
    half = N_BUCKETS // 2
    max_exact = half // 2
    ret = jnp.where(rel > 0, half, 0)
    n = jnp.abs(rel)
    nf = jnp.maximum(n, 1).astype(F32)
    large = max_exact + (jnp.log(nf / max_exact) / math.log(MAX_DISTANCE / max_exact)
                         * (half - max_exact)).astype(jnp.int32)
    large = jnp.minimum(large, half - 1)
    return ret + jnp.where(n < max_exact, n, large)


def _band(R, d):
    W = BQ + 2 * R
    rel = jnp.arange(W)[None, :] - R - jnp.arange(BQ)[:, None]
    return _t5_bucket(rel * d), jnp.abs(rel) <= R


def _onehot(R, d):
    bkt, in_band = _band(R, d)
    return ((bkt.reshape(1, -1) == jnp.arange(128)[:, None]) & in_band.reshape(1, -1)).astype(BF16)


def _bias_expand(table_t, onehot):
    H = table_t.shape[0]
    K = onehot.shape[1]

    def body(t_ref, oh_ref, out_ref):
        oh = oh_ref[...]
        t = t_ref[...]
        hi = t.astype(BF16)
        r1 = t - hi.astype(F32)
        mid = r1.astype(BF16)
        low = (r1 - mid.astype(F32)).astype(BF16)
        marked = _dot(jnp.ones(t.shape, BF16), oh) > 0.5
        out_ref[...] = jnp.where(marked, _dot(hi, oh) + _dot(mid, oh) + _dot(low, oh), NEG)

    vm = pl.BlockSpec(memory_space=pltpu.VMEM)
    return pl.pallas_call(
        body, name="bias_expand", in_specs=[vm, vm], out_specs=vm,
        out_shape=jax.ShapeDtypeStruct((H, K), F32),
        compiler_params=pltpu.CompilerParams(vmem_limit_bytes=VMEM_LIMIT),
    )(table_t, onehot)


def _bias_matrix(table, R, d):
    table_t = jnp.pad(table.T, ((0, 0), (0, 128 - N_BUCKETS)))
    return _bias_expand(table_t, _onehot(R, d)).reshape(table.shape[1], BQ, BQ + 2 * R)


def _bias_variants(base, R):
    H, _, W = base.shape
    col = jnp.arange(W)
    before, after = col < R, col >= BQ + R
    masks = jnp.stack([jnp.zeros_like(before), before, after, before | after])
    v = jnp.where(masks[None, :, None, :], NEG, base[:, None])
    v = v.reshape(H // 2, 2, 4, BQ, W).transpose(0, 2, 1, 3, 4).reshape(H // 2, 4, 2 * BQ, W)
    return v, v.transpose(0, 1, 3, 2)


def _bias_grad(dbt, R, d):
    P, W, _ = dbt.shape
    dbm = dbt.reshape(P, W, 2, BQ).transpose(0, 2, 3, 1).reshape(2 * P, BQ * W)
    return _bias_reduce(_onehot(R, d), dbm)[:, :N_BUCKETS].T


def _deint(a, d):
    if d == 1:
        return a
    H, T, X = a.shape
    return a.reshape(H, T // d, d, X).transpose(0, 2, 1, 3).reshape(H * d, T // d, X)


def _reint(a, d):
    if d == 1:
        return a
    Hd, L, X = a.shape
    return a.reshape(Hd // d, d, L, X).transpose(0, 2, 1, 3).reshape(Hd // d, L * d, X)


def _pad_rows(a, R):
    return jnp.pad(a, ((0, 0), (R, R), (0, 0)))


def _tile2(gain):
    return jnp.concatenate([gain, gain])


ROW_W_O, ROW_GATE, ROW_QKV, ROW_PROJ, B_ROWS = 768, 896, 1024, 1312, 1344
BLK_W_O, BLK_GATE = ROW_W_O // 128, ROW_GATE // 128


def _pack_layer(wts, i):
    a = jnp.stack([wts["ffn1_w_in"][i], wts["ffn2_w_in"][i]])
    D = a.shape[1]
    b = jnp.concatenate([
        wts["ffn1_w_out"][i], wts["ffn2_w_out"][i],
        jnp.zeros((ROW_W_O - 2 * wts["ffn1_w_out"].shape[1], D), a.dtype),
        wts["w_o"][i], wts["w_ple_gate"][i], wts["w_qkv"][i].reshape(-1, D), wts["w_ple_proj"][i].reshape(-1, D)])
    return a, b


def _unpack_layer(sums, like):
    w_in2, b1, b2, w_in1, w_out1 = sums
    n_out, n_sq = like["ffn1_w_out"].shape[1], like["w_o"].shape[1]
    out = {}
    if w_in2 is not None:
        out.update(ffn2_w_in=w_in2, ffn2_w_out=b1[:n_out], w_ple_gate=b1[n_out:n_out + n_sq],
                   w_ple_proj=b1[n_out + n_sq:].reshape(like["w_ple_proj"].shape[1:]))
    if b2 is not None:
        out.update(w_o=b2[:n_sq], w_qkv=b2[n_sq:].reshape(like["w_qkv"].shape[1:]))
    if w_in1 is not None:
        out.update(ffn1_w_in=w_in1, ffn1_w_out=w_out1)
    return out


def _col_sharded(gb, r0, r1, rows):
    return gb[:, r0:r1].reshape(N_DEV, rows, -1).transpose(1, 0, 2).reshape(rows, -1)


def _to_col_shards(g):
    rows = g.shape[0]
    return g.reshape(rows, N_DEV, -1).transpose(1, 0, 2).reshape(N_DEV, -1, 1024)


def _layer_weights(ga, gb, p_dim):
    return dict(ga=ga, gb=gb, w_qkv=_col_sharded(gb, ROW_QKV, ROW_PROJ, ga.shape[2]),
                w_proj=_col_sharded(gb, ROW_PROJ, B_ROWS, p_dim))


def _layer_fwd(x, p, w, sm, i, target, tm, biases, dep=None):
    ga, gb = w["ga"], w["gb"]
    saved = {}
    saved["x0"] = x
    x1, saved["h1"], saved["zg1"], saved["zu1"], saved["s1"] = _ffn_fwd(
        x, sm["norm_ffn1"][i][None], ga, gb, 0, tm, dep)
    saved["x1"] = x1
    qkv, saved["hm"] = _qkv_fwd(x1, sm["norm_mix"][i][None], w["w_qkv"], tm)
    saved["qkv"] = qkv
    gains2 = jnp.stack([_tile2(sm[k][i]) for k in ("q_norm_a", "k_norm_a", "q_norm_b", "k_norm_b")])
    saved["gains2"] = gains2
    qa, ka, va, qb, kb, vb = _attn_prep(qkv, gains2, tm)
    no_sink = jnp.full((8,), NEG, F32)
    branches = []
    outs = []
    for (R, d), bias in zip(DILATED, biases[:3]):
        qd, kd, vd = _deint(qa, d), _pad_rows(_deint(ka, d), R), _pad_rows(_deint(va, d), R)
        sink = jnp.tile(no_sink, d)
        o, lse = _attn_fwd(qd, kd, vd, bias[0], sink, R, 1, d)
        branches.append((qd, kd, vd, bias, sink, R, d))
        outs += [_reint(o, d), _reint(lse, d)]
    bias_b = biases[3]
    kbp, vbp = _pad_rows(kb, SWA_RADIUS), _pad_rows(vb, SWA_RADIUS)
    sink_b = sm["sink_b"][i]
    ob, lb = _attn_fwd(qb, kbp, vbp, bias_b[0], sink_b, SWA_RADIUS, 2, 1)
    oa, la, o_cat = _attn_merge(*outs, ob, tm)
    saved.update(branches=branches, b=(qb, kbp, vbp, bias_b, sink_b), oa=oa, la=la, ob=ob, lb=lb, o_cat=o_cat)
    x2 = _oproj_fwd(x1, o_cat, gb, BLK_W_O, tm)
    saved["x2"] = x2
    x3, saved["h2"], saved["zg2"], saved["zu2"], saved["s2"] = _ffn_fwd(
        x2, sm["norm_ffn2"][i][None], ga, gb, 1, tm)
    saved["x3"] = x3
    res = _ple_fwd(x3, sm["norm_ple"][i][None], gb, BLK_GATE, p, w["w_proj"], target, tm)
    y, saved["hp"], saved["gate"], saved["pp"], saved["pb"] = res[:5]
    loss = res[5] if target is not None else None
    return y, loss, saved


def _layer_bwd(dy, w, sm, i, sv, tm, dep=None, on_ready=None):
    ga, gb = w["ga"], w["gb"]
    gs = {}
    D = dy.shape[1]
    dgl, dpp = _ple_bwd(dy, sv["gate"], sv["pp"], tm, dep)
    d_gate = _matmul_tn(sv["hp"], dgl, D, 2 * tm)
    d_proj = _matmul_tn(sv["pb"], dpp, D, 2 * tm)
    dx3, gs["norm_ple"] = _dense_norm_bwd(dy, dgl, gb, BLK_GATE, sv["x3"], sm["norm_ple"][i][None], tm)
    dx2, dyb, dzg, dzu, gs["norm_ffn2"] = _ffn_bwd(dx3, sv["x2"], sm["norm_ffn2"][i][None], sv["zg2"], sv["zu2"],
                                                   ga, gb, 1, tm)
    dwin2, dwo2 = _ffn_dw(sv["h2"], dzg, dzu, sv["s2"], dyb, 2 * tm)
    half = dwo2.shape[1] // 2
    after_ffn2 = [dwin2, jnp.concatenate([dwo2.reshape(N_DEV, half, D), d_gate.reshape(N_DEV, -1, D),
                                          _to_col_shards(d_proj)], axis=1)]
    token = None if on_ready is None else on_ready(0, after_ffn2)
    dx2b, do = _oproj_bwd(dx2, gb, BLK_W_O, tm, token)
    d_wo = _matmul_tn(sv["o_cat"], dx2b, D, 2 * tm)
    do_a, do_b = do[:4], do[4:]
    dqa, dka, dva, dbias = [], [], [], []
    for qd, kd, vd, bias, sink, R, d in sv["branches"]:
        dq, dk, dv, dbm, _ = _attn_bwd(qd, kd, vd, bias[1], sink, _deint(sv["oa"], d), _deint(sv["la"], d),
                                        _deint(do_a, d), R, 1, d)
        L = qd.shape[1]
        dqa.append(_reint(dq, d))
        dka.append(_reint(dk[:, R:R + L], d))
        dva.append(_reint(dv[:, R:R + L], d))
        dbias.append(dbm)
    qb, kbp, vbp, bias_b, sink_b = sv["b"]
    dqb, dkb, dvb, dbm_b, dsink = _attn_bwd(qb, kbp, vbp, bias_b[1], sink_b, sv["ob"], sv["lb"], do_b,
                                            SWA_RADIUS, 2, 1)
    T = qb.shape[1]
    gs["rel_bias"] = dbias + [dbm_b]
    gs["sink_b"] = jnp.sum(dsink[:, 0].reshape(-1, 2, BQ), axis=2).reshape(-1)
    dqkv, dgains2 = _attn_post(sv["qkv"], sv["gains2"], dqa, dka, dva, dqb,
                               dkb[:, SWA_RADIUS:SWA_RADIUS + T], dvb[:, SWA_RADIUS:SWA_RADIUS + T], tm // 2)
    dgains = dgains2[:, :HEAD_DIM] + dgains2[:, HEAD_DIM:]
    for k, name in enumerate(("q_norm_a", "k_norm_a", "q_norm_b", "k_norm_b")):
        gs[name] = dgains[k]
    d_qkv = _matmul_tn(sv["hm"], dqkv, dqkv.shape[1] // 2, 2 * tm)
    after_mixer = [jnp.concatenate([d_wo.reshape(N_DEV, -1, D), _to_col_shards(d_qkv)], axis=1)]
    token = None if on_ready is None else on_ready(1, after_mixer)
    dx1, gs["norm_mix"] = _dense_norm_bwd(dx2, dqkv, w["w_qkv"], None, sv["x1"], sm["norm_mix"][i][None], tm)
    dx0, dyb, dzg, dzu, gs["norm_ffn1"] = _ffn_bwd(dx1, sv["x0"], sm["norm_ffn1"][i][None], sv["zg1"], sv["zu1"],
                                                   ga, gb, 0, tm, token)
    dwin1, dwo1 = _ffn_dw(sv["h1"], dzg, dzu, sv["s1"], dyb, 2 * tm)
    return dx0, (after_ffn2, after_mixer, [dwin1, dwo1.reshape(N_DEV, half, D)]), gs


def _bias_matrices(rel_bias):
    biases = [_bias_variants(_bias_matrix(rel_bias[:, :8], R, d), R) for R, d in DILATED]
    biases.append(_bias_variants(_bias_matrix(rel_bias[:, 8:], SWA_RADIUS, 1), SWA_RADIUS))
    return biases


def _stack_small(per_layer):
    small = {}
    for k, v in per_layer.items():
        if k == "rel_bias":
            per_branch = [sum(parts) for parts in zip(*v.values())]
            drel_a = sum(_bias_grad(t, R, d) for t, (R, d) in zip(per_branch[:3], DILATED))
            small[k] = jnp.concatenate([drel_a, _bias_grad(per_branch[3], SWA_RADIUS, 1)], axis=1)
        else:
            small[k] = jnp.stack([v[i].reshape(-1) for i in sorted(v)])
    return small


TM = 512
SUM_TILES = (512, 512, 416, 512, 352)
LAST_GROUP = ("ffn1_w_in", "ffn1_w_out")


def _pack_small(d, extra=None):
    parts = [d[k].reshape(-1) for k in SMALL]
    if extra is not None:
        parts.append(extra.reshape(-1))
    flat = jnp.concatenate(parts)
    return jnp.pad(flat, (0, SMALL_ROWS * 128 - flat.shape[0])).reshape(SMALL_ROWS, 128)


def _unpack_small(buf, like):
    flat = buf.reshape(-1)
    out, off = {}, 0
    for k in SMALL:
        n = like[k].size
        out[k] = flat[off:off + n].reshape(like[k].shape)
        off += n
    return out, flat[off]


def kernel(x, p, rel_bias, norm_ffn1, ffn1_w_in, ffn1_w_out, norm_mix, w_qkv, q_norm_a, k_norm_a, q_norm_b, k_norm_b, sink_b, w_o, norm_ffn2, ffn2_w_in, ffn2_w_out, norm_ple, w_ple_gate, w_ple_proj, loss_target, m_rel_bias, m_norm_ffn1, m_ffn1_w_in, m_ffn1_w_out, m_norm_mix, m_w_qkv, m_q_norm_a, m_k_norm_a, m_q_norm_b, m_k_norm_b, m_sink_b, m_w_o, m_norm_ffn2, m_ffn2_w_in, m_ffn2_w_out, m_norm_ple, m_w_ple_gate, m_w_ple_proj, v_rel_bias, v_norm_ffn1, v_ffn1_w_in, v_ffn1_w_out, v_norm_mix, v_w_qkv, v_q_norm_a, v_k_norm_a, v_q_norm_b, v_k_norm_b, v_sink_b, v_w_o, v_norm_ffn2, v_ffn2_w_in, v_ffn2_w_out, v_norm_ple, v_w_ple_gate, v_w_ple_proj):
    wts = dict(rel_bias=rel_bias, norm_ffn1=norm_ffn1, ffn1_w_in=ffn1_w_in, ffn1_w_out=ffn1_w_out,
               norm_mix=norm_mix, w_qkv=w_qkv, q_norm_a=q_norm_a, k_norm_a=k_norm_a, q_norm_b=q_norm_b,
               k_norm_b=k_norm_b, sink_b=sink_b, w_o=w_o, norm_ffn2=norm_ffn2, ffn2_w_in=ffn2_w_in,
               ffn2_w_out=ffn2_w_out, norm_ple=norm_ple, w_ple_gate=w_ple_gate, w_ple_proj=w_ple_proj)
    mom = dict(rel_bias=m_rel_bias, norm_ffn1=m_norm_ffn1, ffn1_w_in=m_ffn1_w_in, ffn1_w_out=m_ffn1_w_out,
               norm_mix=m_norm_mix, w_qkv=m_w_qkv, q_norm_a=m_q_norm_a, k_norm_a=m_k_norm_a, q_norm_b=m_q_norm_b,
               k_norm_b=m_k_norm_b, sink_b=m_sink_b, w_o=m_w_o, norm_ffn2=m_norm_ffn2, ffn2_w_in=m_ffn2_w_in,
               ffn2_w_out=m_ffn2_w_out, norm_ple=m_norm_ple, w_ple_gate=m_w_ple_gate, w_ple_proj=m_w_ple_proj)
    var = dict(rel_bias=v_rel_bias, norm_ffn1=v_norm_ffn1, ffn1_w_in=v_ffn1_w_in, ffn1_w_out=v_ffn1_w_out,
               norm_mix=v_norm_mix, w_qkv=v_w_qkv, q_norm_a=v_q_norm_a, k_norm_a=v_k_norm_a, q_norm_b=v_q_norm_b,
               k_norm_b=v_k_norm_b, sink_b=v_sink_b, w_o=v_w_o, norm_ffn2=v_norm_ffn2, ffn2_w_in=v_ffn2_w_in,
               ffn2_w_out=v_ffn2_w_out, norm_ple=v_norm_ple, w_ple_gate=v_w_ple_gate, w_ple_proj=v_w_ple_proj)
    sm = {k: wts[k] for k in SMALL}
    p_dim = p.shape[-1]
    me = 4 * lax.axis_index("x") + 2 * lax.axis_index("y") + lax.axis_index("c")
    packed = []
    for i in range(2):
        a, b = _pack_layer(wts, i)
        packed.append([a.reshape(-1, a.shape[-1]).astype(BF16), b.astype(BF16)])
    a_shape = (2, ffn1_w_in.shape[1], ffn1_w_in.shape[2])

    def weights_of(zones):
        return _layer_weights(zones[0].reshape((N_DEV,) + a_shape), zones[1], p_dim)

    w0 = weights_of([_all_gather(t) for t in packed[0]])
    zone_shapes = [(N_DEV,) + t.shape for t in packed[1]]
    ssem, rsem, thru, zones, token = _exchange_start(packed[1], zone_shapes, False, "gather_start")
    biases = _bias_matrices(rel_bias)
    x1, _, sv0 = _layer_fwd(x[0], p[0, 0], w0, sm, 0, None, TM, biases, dep=token)
    zones = _exchange_wait(ssem, rsem, thru, zones, x1, False, "gather_wait")
    w1 = weights_of([lax.dynamic_update_index_in_dim(z, t, me, 0) for z, t in zip(zones, packed[1])])
    dy, loss, sv1 = _layer_fwd(x1, p[1, 0], w1, sm, 1, loss_target[0], TM, biases)

    def slots_for(arrs):
        return [(N_DEV - 1,) + t.shape[1:] for t in arrs]

    dx1, groups1, gs1 = _layer_bwd(dy, w1, sm, 1, sv1, TM)
    g1 = groups1[0] + groups1[1] + groups1[2]
    ex1 = _exchange_start(g1, slots_for(g1), True, "scatter_start")
    held = {}

    def on_ready(stage, group):
        if stage == 1:
            held["slots1"] = _exchange_wait(*ex1[:4], group[0], True, "scatter_wait")
        held[stage] = _exchange_start(group, slots_for(group), True, f"scatter_start_{stage}")
        return held[stage][4]

    dx, groups0, gs0 = _layer_bwd(dx1, w0, sm, 0, sv0, TM, dep=ex1[4], on_ready=on_ready)
    last = groups0[2]
    slots0 = [_exchange_wait(*held[stage][:4], last[0], True, f"scatter_wait_{stage}") for stage in (0, 1)]

    ex_last = _exchange_start(last, slots_for(last), True, "scatter_start_2")

    def summed(arrs, slots, tiles, dep=None):
        return [_sum_parts(lax.dynamic_index_in_dim(t, me, 0, keepdims=False), s_, tr, dep)
                for t, s_, tr in zip(arrs, slots, tiles)]

    r1 = summed(g1, held["slots1"], SUM_TILES, ex_last[4])
    r0 = summed(groups0[0], slots0[0], SUM_TILES[:2]) + summed(groups0[1], slots0[1], SUM_TILES[2:3])

    gsmall = _stack_small({k: {0: gs0[k], 1: gs1[k]} for k in gs0})
    small_sum, loss_sum = _unpack_small(_all_reduce_small(_pack_small(gsmall, loss[0, :1])), sm)

    def update(names, layers):
        for k in names:
            grads[k] = jnp.stack([layers[0][k], layers[1][k]])
            delta[k], new_m[k], new_v[k] = _adamw(wts[k], grads[k], mom[k], var[k])

    grads, delta, new_m, new_v = dict(small_sum), {}, {}, {}
    layer1 = _unpack_layer(r1, wts)
    update([k for k in BIG if k not in LAST_GROUP], [_unpack_layer(r0 + [None, None], wts), layer1])
    zeros = {k: jnp.zeros_like(wts[k]) for k in SMALL}
    ds, ms, vs = _adamw(_pack_small(wts), _pack_small(small_sum), _pack_small(mom), _pack_small(var))
    for packed, dst in ((ds, delta), (ms, new_m), (vs, new_v)):
        dst.update(_unpack_small(packed, zeros)[0])
    slots_last = _exchange_wait(*ex_last[:4], ds, True, "scatter_wait_2")
    update(LAST_GROUP, [_unpack_layer([None, None, None] + summed(last, slots_last, SUM_TILES[3:]), wts), layer1])

    return (loss_sum, dx[None], *[grads[k] for k in WEIGHTS], *[delta[k] for k in WEIGHTS],
            *[new_m[k] for k in WEIGHTS], *[new_v[k] for k in WEIGHTS])
```

```python
import functools
import math

import jax
import jax.numpy as jnp
from jax import lax
from jax.experimental import pallas as pl
from jax.experimental.pallas import tpu as pltpu

F32 = jnp.float32
BF16 = jnp.bfloat16

N_DEV = 8
HEAD_DIM = 64
PAIR = 2 * HEAD_DIM
BQ = 128
N_BUCKETS = 32
MAX_DISTANCE = 1024
DILATED = ((64, 1), (64, 4), (64, 16))
SWA_RADIUS = 128
EPS = 1e-6
NEG = -1e30
ADAM_LR, ADAM_B1, ADAM_B2, ADAM_EPS, ADAM_WD, ADAM_STEP = 0.001, 0.9, 0.999, 1e-08, 0.01, 10
VMEM_LIMIT = 56 * 1024 * 1024
AXES = ("x", "y", "c")
MESH = pl.DeviceIdType.MESH

BIG = ("ffn1_w_in", "ffn1_w_out", "w_qkv", "w_o", "ffn2_w_in", "ffn2_w_out", "w_ple_gate", "w_ple_proj")
SMALL = ("rel_bias", "norm_ffn1", "norm_mix", "q_norm_a", "k_norm_a", "q_norm_b", "k_norm_b", "sink_b",
         "norm_ffn2", "norm_ple")
WEIGHTS = ("rel_bias", "norm_ffn1", "ffn1_w_in", "ffn1_w_out", "norm_mix", "w_qkv", "q_norm_a", "k_norm_a",
           "q_norm_b", "k_norm_b", "sink_b", "w_o", "norm_ffn2", "ffn2_w_in", "ffn2_w_out", "norm_ple",
           "w_ple_gate", "w_ple_proj")
SMALL_ROWS = 96


def _params(*sem):
    return pltpu.CompilerParams(dimension_semantics=sem, vmem_limit_bytes=VMEM_LIMIT)


def _dot(a, b):
    return jnp.dot(a, b, preferred_element_type=F32)


def _dot_nt(a, b):
    return lax.dot_general(a, b, (((1,), (1,)), ((), ())), preferred_element_type=F32)


def _dot_tn(a, b):
    return lax.dot_general(a, b, (((0,), (0,)), ((), ())), preferred_element_type=F32)


def _sigmoid(x):
    return 1.0 / (1.0 + jnp.exp(-x))


def _rstd(xv):
    return lax.rsqrt(jnp.mean(xv * xv, axis=-1, keepdims=True) + EPS)


def _norm_bwd(dh, xv, gv):
    r = _rstd(xv)
    xn = xv * r
    dg = jnp.sum(dh * xn, axis=0, keepdims=True)
    dxn = dh * gv
    dx = r * (dxn - xn * jnp.mean(dxn * xn, axis=-1, keepdims=True))
    return dx, dg


def _lo_mask(shape):
    return lax.broadcasted_iota(jnp.int32, shape, len(shape) - 1) < HEAD_DIM


def _half_sum(t, lo):
    s0 = jnp.sum(jnp.where(lo, t, 0.0), axis=1, keepdims=True)
    s1 = jnp.sum(jnp.where(lo, 0.0, t), axis=1, keepdims=True)
    return jnp.where(lo, s0, s1)


FFN_PARTS = 2


def _ffn_weight_specs(f, nj, D, C):
    return [pl.BlockSpec((None, None, D, C), lambda i, j: (j, f, 0, 0)),
            pl.BlockSpec((None, None, D, C), lambda i, j: (j + nj, f, 0, 0)),
            pl.BlockSpec((2, C // 2, D), lambda i, j: (j, f, 0))]


def _with_dep(body, dep, in_specs, args):
    if dep is None:
        return body, in_specs, args

    def body_after(dep_ref, *refs):
        body(*refs)

    return body_after, [pl.BlockSpec(memory_space=pl.ANY)] + in_specs, [dep] + args


def _ffn_fwd(x, g, ga, gb, f, tm, dep=None):
    T, D = x.shape
    nj, C = ga.shape[0] // 2, ga.shape[3]

    def body(x_ref, g_ref, wg_ref, wu_ref, wo_ref, xo_ref, h_ref, zg_ref, zu_ref, s_ref, h_scr, acc):
        j = pl.program_id(1)

        @pl.when(j == 0)
        def _():
            xv = x_ref[...]
            hb = (xv * _rstd(xv) * g_ref[...]).astype(BF16)
            h_scr[...] = hb
            h_ref[...] = hb
            acc[...] = jnp.zeros_like(acc)

        wo = wo_ref[...].reshape(C, D)
        for part in range(FFN_PARTS):
            sl = pl.ds(part * (tm // FFN_PARTS), tm // FFN_PARTS)
            hb = h_scr[sl, :]
            gt = _dot(hb, wg_ref[...])
            up = _dot(hb, wu_ref[...])
            s = (gt * _sigmoid(gt) * up).astype(BF16)
            zg_ref[sl, :] = gt.astype(BF16)
            zu_ref[sl, :] = up.astype(BF16)
            s_ref[sl, :] = s
            acc[sl, :] += _dot(s, wo)

        @pl.when(j == nj - 1)
        def _():
            xo_ref[...] = x_ref[...] + 0.5 * acc[...]

    tok = pl.BlockSpec((tm, D), lambda i, j: (i, 0))
    chunk = pl.BlockSpec((None, tm, C), lambda i, j: (j, i, 0))
    in_specs = [tok, pl.BlockSpec((1, D), lambda i, j: (0, 0))] + _ffn_weight_specs(f, nj, D, C)
    body, in_specs, args = _with_dep(body, dep, in_specs, [x, g, ga, ga, gb])
    return pl.pallas_call(
        body, name="ffn_fwd", grid=(T // tm, nj),
        in_specs=in_specs,
        out_specs=[tok, tok, chunk, chunk, chunk],
        out_shape=[jax.ShapeDtypeStruct((T, D), F32), jax.ShapeDtypeStruct((T, D), BF16),
                   jax.ShapeDtypeStruct((nj, T, C), BF16), jax.ShapeDtypeStruct((nj, T, C), BF16),
                   jax.ShapeDtypeStruct((nj, T, C), BF16)],
        scratch_shapes=[pltpu.VMEM((tm, D), BF16), pltpu.VMEM((tm, D), F32)],
        compiler_params=_params("parallel", "arbitrary"),
    )(*args)


def _ffn_bwd(dxo, x, g, zg, zu, ga, gb, f, tm, dep=None):
    T, D = x.shape
    nj, C = ga.shape[0] // 2, ga.shape[3]

    def body(dxo_ref, x_ref, g_ref, zg_ref, zu_ref, wg_ref, wu_ref, wo_ref,
             dx_ref, dy_ref, dzg_ref, dzu_ref, dgn_ref, dy_scr, acc):
        i, j = pl.program_id(0), pl.program_id(1)

        @pl.when(j == 0)
        def _():
            dyb = (0.5 * dxo_ref[...]).astype(BF16)
            dy_scr[...] = dyb
            dy_ref[...] = dyb
            acc[...] = jnp.zeros_like(acc)

        wo = wo_ref[...].reshape(C, D)
        for part in range(FFN_PARTS):
            sl = pl.ds(part * (tm // FFN_PARTS), tm // FFN_PARTS)
            ds = _dot_nt(dy_scr[sl, :], wo)
            gt = zg_ref[sl, :].astype(F32)
            up = zu_ref[sl, :].astype(F32)
            sg = _sigmoid(gt)
            dgt = (ds * up * (sg * (1.0 + gt * (1.0 - sg)))).astype(BF16)
            dup = (ds * (gt * sg)).astype(BF16)
            dzg_ref[sl, :] = dgt
            dzu_ref[sl, :] = dup
            acc[sl, :] += _dot_nt(dgt, wg_ref[...]) + _dot_nt(dup, wu_ref[...])

        @pl.when(j == nj - 1)
        def _():
            dx, dg = _norm_bwd(acc[...], x_ref[...], g_ref[...])
            dx_ref[...] = dxo_ref[...] + dx

            @pl.when(i == 0)
            def _():
                dgn_ref[...] = dg

            @pl.when(i > 0)
            def _():
                dgn_ref[...] += dg

    tok = pl.BlockSpec((tm, D), lambda i, j: (i, 0))
    chunk = pl.BlockSpec((None, tm, C), lambda i, j: (j, i, 0))
    row = pl.BlockSpec((1, D), lambda i, j: (0, 0))
    in_specs = [tok, tok, row, chunk, chunk] + _ffn_weight_specs(f, nj, D, C)
    body, in_specs, args = _with_dep(body, dep, in_specs, [dxo, x, g, zg, zu, ga, ga, gb])
    return pl.pallas_call(
        body, name="ffn_bwd", grid=(T // tm, nj),
        in_specs=in_specs,
        out_specs=[tok, tok, chunk, chunk, row],
        out_shape=[jax.ShapeDtypeStruct((T, D), F32), jax.ShapeDtypeStruct((T, D), BF16),
                   jax.ShapeDtypeStruct((nj, T, C), BF16), jax.ShapeDtypeStruct((nj, T, C), BF16),
                   jax.ShapeDtypeStruct((1, D), F32)],
        scratch_shapes=[pltpu.VMEM((tm, D), BF16), pltpu.VMEM((tm, D), F32)],
        compiler_params=_params("arbitrary", "arbitrary"),
    )(*args)


def _ffn_dw(h, dzg, dzu, s, dy, tk, dep=None):
    T, D = h.shape
    nj, C = s.shape[0], s.shape[2]
    nk = T // tk

    def body(h_ref, dzg_ref, dzu_ref, s_ref, dy_ref, dwin_ref, dwo_ref, ag, au, ao):
        k = pl.program_id(1)

        @pl.when(k == 0)
        def _():
            ag[...] = jnp.zeros_like(ag)
            au[...] = jnp.zeros_like(au)
            ao[...] = jnp.zeros_like(ao)

        hb = h_ref[...]
        ag[...] += _dot_tn(hb, dzg_ref[...])
        au[...] += _dot_tn(hb, dzu_ref[...])
        ao[...] += _dot_tn(s_ref[...], dy_ref[...])

        @pl.when(k == nk - 1)
        def _():
            dwin_ref[0] = ag[...].astype(BF16)
            dwin_ref[1] = au[...].astype(BF16)
            dwo_ref[...] = ao[...].astype(BF16)

    tok = pl.BlockSpec((tk, D), lambda j, k: (k, 0))
    chunk = pl.BlockSpec((None, tk, C), lambda j, k: (j, k, 0))
    body, in_specs, args = _with_dep(body, dep, [tok, chunk, chunk, chunk, tok], [h, dzg, dzu, s, dy])
    dwin, dwo = pl.pallas_call(
        body, name="ffn_dw", grid=(nj, nk),
        in_specs=in_specs,
        out_specs=[pl.BlockSpec((2, None, D, C), lambda j, k: (0, j, 0, 0)),
                   pl.BlockSpec((None, C, D), lambda j, k: (j, 0, 0))],
        out_shape=[jax.ShapeDtypeStruct((2, nj, D, C), BF16), jax.ShapeDtypeStruct((nj, C, D), BF16)],
        scratch_shapes=[pltpu.VMEM((D, C), F32), pltpu.VMEM((D, C), F32), pltpu.VMEM((C, D), F32)],
        compiler_params=_params("parallel", "arbitrary"),
    )(*args)
    return dwin.reshape(2 * nj, D, C), dwo


def _matmul_tn(a, b, tn, tk):
    T, Ka = a.shape
    N = b.shape[1]
    nk = T // tk

    def body(a_ref, b_ref, o_ref, acc):
        k = pl.program_id(1)

        @pl.when(k == 0)
        def _():
            acc[...] = jnp.zeros_like(acc)

        acc[...] += _dot_tn(a_ref[...], b_ref[...])

        @pl.when(k == nk - 1)
        def _():
            o_ref[...] = acc[...].astype(BF16)

    return pl.pallas_call(
        body, name="matmul_tn", grid=(N // tn, nk),
        in_specs=[pl.BlockSpec((tk, Ka), lambda n, k: (k, 0)), pl.BlockSpec((tk, tn), lambda n, k: (k, n))],
        out_specs=pl.BlockSpec((Ka, tn), lambda n, k: (0, n)),
        out_shape=jax.ShapeDtypeStruct((Ka, N), BF16),
        scratch_shapes=[pltpu.VMEM((Ka, tn), F32)],
        compiler_params=_params("parallel", "arbitrary"),
    )(a, b)


def _qkv_fwd(x, g, w, tm):
    T, D = x.shape
    N = w.shape[1]

    def body(x_ref, g_ref, w_ref, o_ref, h_ref):
        xv = x_ref[...]
        hb = (xv * _rstd(xv) * g_ref[...]).astype(BF16)
        h_ref[...] = hb
        o_ref[...] = _dot(hb, w_ref[...])

    return pl.pallas_call(
        body, name="qkv_fwd", grid=(T // tm,),
        in_specs=[pl.BlockSpec((tm, D), lambda i: (i, 0)), pl.BlockSpec((1, D), lambda i: (0, 0)),
                  pl.BlockSpec((D, N), lambda i: (0, 0))],
        out_specs=[pl.BlockSpec((tm, N), lambda i: (i, 0)), pl.BlockSpec((tm, D), lambda i: (i, 0))],
        out_shape=[jax.ShapeDtypeStruct((T, N), F32), jax.ShapeDtypeStruct((T, D), BF16)],
        compiler_params=_params("parallel"),
    )(x, g, w)


def _attn_prep(qkv, gains2, tm):
    T = qkv.shape[0]
    scale = HEAD_DIM ** -0.5

    def body(qkv_ref, g_ref, qa_ref, ka_ref, va_ref, qb_ref, kb_ref, vb_ref):
        lo = _lo_mask((tm, PAIR))

        def normed(c, gi, mult):
            xv = qkv_ref[:, c * PAIR:(c + 1) * PAIR]
            r = lax.rsqrt(_half_sum(xv * xv, lo) * (1.0 / HEAD_DIM) + EPS)
            y = xv * r * g_ref[gi:gi + 1, :]
            return y * mult if mult != 1.0 else y

        def both_halves(v):
            sw = pltpu.roll(v, HEAD_DIM, 1)
            return jnp.where(lo, v, sw), jnp.where(lo, sw, v)

        for c in range(4):
            qa_ref[c] = normed(c, 0, scale).astype(BF16)
            ka_ref[c] = normed(4 + c, 1, 1.0).astype(BF16)
            va_ref[c] = qkv_ref[:, (8 + c) * PAIR:(9 + c) * PAIR].astype(BF16)
            qb_ref[c] = normed(12 + c, 2, scale).astype(BF16)
        k0, k1 = both_halves(normed(16, 3, 1.0))
        kb_ref[0] = k0.astype(BF16)
        kb_ref[1] = k1.astype(BF16)
        v0, v1 = both_halves(qkv_ref[:, 17 * PAIR:18 * PAIR])
        vb_ref[0] = v0.astype(BF16)
        vb_ref[1] = v1.astype(BF16)

    four = pl.BlockSpec((4, tm, PAIR), lambda i: (0, i, 0))
    two = pl.BlockSpec((2, tm, PAIR), lambda i: (0, i, 0))
    s4 = jax.ShapeDtypeStruct((4, T, PAIR), BF16)
    s2 = jax.ShapeDtypeStruct((2, T, PAIR), BF16)
    return pl.pallas_call(
        body, name="attn_prep", grid=(T // tm,),
        in_specs=[pl.BlockSpec((tm, qkv.shape[1]), lambda i: (i, 0)), pl.BlockSpec((4, PAIR), lambda i: (0, 0))],
        out_specs=[four, four, four, four, two, two],
        out_shape=[s4, s4, s4, s4, s2, s2],
        compiler_params=_params("parallel"),
    )(qkv, gains2)


def _loop_blocks(nb, body, init, per_iter):
    u = math.gcd(nb, per_iter)

    def outer(i, carry):
        for k in range(u):
            carry = body(i * u + k, carry)
        return carry

    return lax.fori_loop(0, nb // u, outer, init)


def _edge_variant(b, nb):
    return (b == 0).astype(jnp.int32) + 2 * (b == nb - 1).astype(jnp.int32)


def _stack_heads(v, lo):
    z = jnp.zeros_like(v)
    return jnp.concatenate([jnp.where(lo, v, z), jnp.where(lo, z, v)], axis=0)


def _unstack_heads(v2, lo):
    return jnp.where(lo, v2[:BQ], v2[BQ:])


def _row_vector(v, lo):
    r = lax.broadcasted_iota(jnp.int32, (BQ, PAIR), 0)
    ln = lax.broadcasted_iota(jnp.int32, (BQ, PAIR), 1)
    diag = (ln % HEAD_DIM) == (r % HEAD_DIM)
    top = jnp.sum(jnp.where(diag & (r < HEAD_DIM), v, 0.0), axis=0, keepdims=True)
    bot = jnp.sum(jnp.where(diag & (r >= HEAD_DIM), v, 0.0), axis=0, keepdims=True)
    top8, bot8 = jnp.broadcast_to(top, (8, PAIR)), jnp.broadcast_to(bot, (8, PAIR))
    lo8 = _lo_mask((8, PAIR))
    head0 = jnp.where(lo8, top8, pltpu.roll(bot8, HEAD_DIM, 1))
    head1 = jnp.where(lo8, pltpu.roll(top8, HEAD_DIM, 1), bot8)
    return jnp.concatenate([head0, head1], axis=1)[:1]


def _units_per_step(nb, pairs_per_kv):
    return max(1, 16 // nb) if pairs_per_kv == 1 else 1


def _attn_fwd(q, kp, vp, bias4, sink, R, pairs_per_kv, pairs_per_bias):
    N, L, _ = q.shape
    W = BQ + 2 * R
    nb = L // BQ
    G = _units_per_step(nb, pairs_per_kv)

    def body(sink_ref, q_ref, k_ref, v_ref, bias_ref, o_ref, lse_ref):
        n = pl.program_id(0)
        lo_q = _lo_mask((BQ, PAIR))
        first = lax.broadcasted_iota(jnp.int32, (2 * BQ, 1), 0) < BQ

        def blk(f, carry):
            g, b = f // nb, f % nb
            u = n * G + g
            sk = jnp.where(first, sink_ref[2 * u], sink_ref[2 * u + 1])
            q0 = pl.multiple_of(b * BQ, BQ)
            q2 = _stack_heads(q_ref[g, pl.ds(q0, BQ), :], lo_q)
            kw = k_ref[g, pl.ds(q0, W), :]
            vw = v_ref[g, pl.ds(q0, W), :]
            s = _dot_nt(q2, kw) + bias_ref[_edge_variant(b, nb)]
            m = jnp.maximum(jnp.max(s, axis=1, keepdims=True), sk)
            p = jnp.exp(s - m)
            l = jnp.sum(p, axis=1, keepdims=True) + jnp.exp(sk - m)
            o2 = _dot(p.astype(BF16), vw) / l
            o_ref[g, pl.ds(q0, BQ), :] = _unstack_heads(o2, lo_q)
            lse_ref[g, pl.ds(q0, BQ), :] = _unstack_heads(jnp.broadcast_to(m + jnp.log(l), (2 * BQ, PAIR)), lo_q)
            return carry

        _loop_blocks(G * nb, blk, 0, 4)

    qspec = pl.BlockSpec((G, L, PAIR), lambda n: (n, 0, 0))
    kspec = pl.BlockSpec((G, L + 2 * R, PAIR), lambda n: (n // pairs_per_kv, 0, 0))
    return pl.pallas_call(
        body, name="attn_fwd", grid=(N // G,),
        in_specs=[pl.BlockSpec(memory_space=pltpu.SMEM), qspec, kspec, kspec,
                  pl.BlockSpec((None, 4, 2 * BQ, W), lambda n: (n * G // pairs_per_bias, 0, 0, 0))],
        out_specs=[qspec, qspec],
        out_shape=[jax.ShapeDtypeStruct((N, L, PAIR), F32), jax.ShapeDtypeStruct((N, L, PAIR), F32)],
        compiler_params=_params("parallel"),
    )(sink, q, kp, vp, bias4)


def _attn_bwd(q, kp, vp, bias4t, sink, o, lse, do, R, pairs_per_kv, pairs_per_bias):
    N, L, _ = q.shape
    Nk = kp.shape[0]
    Pb = bias4t.shape[0]
    W = BQ + 2 * R
    nb = L // BQ
    G = _units_per_step(nb, pairs_per_kv)

    def body(sink_ref, q_ref, k_ref, v_ref, bias_ref, o_ref, lse_ref, do_ref,
             dq_ref, dk_ref, dv_ref, dbias_ref, dsink_ref):
        n = pl.program_id(0)
        lo_q = _lo_mask((BQ, PAIR))
        first = lax.broadcasted_iota(jnp.int32, (1, 2 * BQ), 1) < BQ
        dsink_ref[...] = jnp.zeros_like(dsink_ref)

        @pl.when(n % pairs_per_kv == 0)
        def _():
            dk_ref[...] = jnp.zeros_like(dk_ref)
            dv_ref[...] = jnp.zeros_like(dv_ref)

        @pl.when((n * G) % pairs_per_bias == 0)
        def _():
            dbias_ref[...] = jnp.zeros_like(dbias_ref)

        def blk(f, carry):
            g, b = f // nb, f % nb
            u = n * G + g
            sk = jnp.where(first, sink_ref[2 * u], sink_ref[2 * u + 1])
            q0 = pl.multiple_of(b * BQ, BQ)
            q2 = _stack_heads(q_ref[g, pl.ds(q0, BQ), :], lo_q)
            kw = k_ref[g, pl.ds(q0, W), :]
            vw = v_ref[g, pl.ds(q0, W), :]
            dov = do_ref[g, pl.ds(q0, BQ), :]
            lse = _row_vector(lse_ref[g, pl.ds(q0, BQ), :], lo_q)
            delta = _row_vector(_half_sum(dov * o_ref[g, pl.ds(q0, BQ), :], lo_q), lo_q)
            do2 = _stack_heads(dov.astype(BF16), lo_q)
            st = _dot_nt(kw, q2) + bias_ref[_edge_variant(b, nb)]
            pt = jnp.exp(st - lse)
            dst = pt * (_dot_nt(vw, do2) - delta)
            dstb = dst.astype(BF16)
            dbias_ref[...] += dst
            dk_ref[g, pl.ds(q0, W), :] += _dot(dstb, q2)
            dv_ref[g, pl.ds(q0, W), :] += _dot(pt.astype(BF16), do2)
            dq_ref[g, pl.ds(q0, BQ), :] = _unstack_heads(_dot_tn(dstb, kw), lo_q)
            dsink_ref[g, pl.ds(0, 1), :] -= jnp.exp(sk - lse) * delta
            return carry

        _loop_blocks(G * nb, blk, 0, 4)

    qspec = pl.BlockSpec((G, L, PAIR), lambda n: (n, 0, 0))
    kspec = pl.BlockSpec((G, L + 2 * R, PAIR), lambda n: (n // pairs_per_kv, 0, 0))
    return pl.pallas_call(
        body, name="attn_bwd", grid=(N // G,),
        in_specs=[pl.BlockSpec(memory_space=pltpu.SMEM), qspec, kspec, kspec,
                  pl.BlockSpec((None, 4, W, 2 * BQ), lambda n: (n * G // pairs_per_bias, 0, 0, 0)),
                  qspec, qspec, qspec],
        out_specs=[qspec, kspec, kspec,
                   pl.BlockSpec((None, W, 2 * BQ), lambda n: (n * G // pairs_per_bias, 0, 0)),
                   pl.BlockSpec((G, 8, 2 * BQ), lambda n: (n, 0, 0))],
        out_shape=[jax.ShapeDtypeStruct((N, L, PAIR), F32),
                   jax.ShapeDtypeStruct((Nk, L + 2 * R, PAIR), F32),
                   jax.ShapeDtypeStruct((Nk, L + 2 * R, PAIR), F32),
                   jax.ShapeDtypeStruct((Pb, W, 2 * BQ), F32),
                   jax.ShapeDtypeStruct((N, 8, 2 * BQ), F32)],
        compiler_params=_params("arbitrary"),
    )(sink, q, kp, vp, bias4t, o, lse, do)


def _attn_merge(o1, l1, o4, l4, o16, l16, ob, tm):
    T = o1.shape[1]

    def body(o1_ref, l1_ref, o4_ref, l4_ref, o16_ref, l16_ref, ob_ref, oa_ref, la_ref, cat_ref):
        for c in range(4):
            a, b, d = l1_ref[c], l4_ref[c], l16_ref[c]
            m = jnp.maximum(jnp.maximum(a, b), d)
            wa, wb, wd = jnp.exp(a - m), jnp.exp(b - m), jnp.exp(d - m)
            z = wa + wb + wd
            o = (wa * o1_ref[c] + wb * o4_ref[c] + wd * o16_ref[c]) / z
            oa_ref[c] = o
            la_ref[c] = m + jnp.log(z)
            cat_ref[:, c * PAIR:(c + 1) * PAIR] = o.astype(BF16)
            cat_ref[:, (4 + c) * PAIR:(5 + c) * PAIR] = ob_ref[c].astype(BF16)

    four = pl.BlockSpec((4, tm, PAIR), lambda i: (0, i, 0))
    s4 = jax.ShapeDtypeStruct((4, T, PAIR), F32)
    return pl.pallas_call(
        body, name="attn_merge", grid=(T // tm,),
        in_specs=[four] * 7,
        out_specs=[four, four, pl.BlockSpec((tm, 8 * PAIR), lambda i: (i, 0))],
        out_shape=[s4, s4, jax.ShapeDtypeStruct((T, 8 * PAIR), BF16)],
        compiler_params=_params("parallel"),
    )(o1, l1, o4, l4, o16, l16, ob)


def _weight_arg(w, blk):
    if blk is None:
        return pl.BlockSpec(w.shape, lambda i: (0, 0)), (lambda ref: ref[...])
    D = w.shape[2]
    return (pl.BlockSpec((N_DEV, 128, D), lambda i: (0, blk, 0)),
            lambda ref: ref[...].reshape(N_DEV * 128, D))


def _oproj_fwd(x, o_cat, w, blk, tm):
    T, D = x.shape
    wspec, wload = _weight_arg(w, blk)

    def body(x_ref, o_ref, w_ref, out_ref):
        out_ref[...] = x_ref[...] + _dot(o_ref[...], wload(w_ref))

    tok = pl.BlockSpec((tm, D), lambda i: (i, 0))
    return pl.pallas_call(
        body, name="oproj_fwd", grid=(T // tm,),
        in_specs=[tok, pl.BlockSpec((tm, o_cat.shape[1]), lambda i: (i, 0)), wspec],
        out_specs=tok, out_shape=jax.ShapeDtypeStruct((T, D), F32),
        compiler_params=_params("parallel"),
    )(x, o_cat, w)


def _oproj_bwd(dx, w, blk, tm, dep=None):
    T, D = dx.shape
    wspec, wload = _weight_arg(w, blk)

    def body(dx_ref, w_ref, dxb_ref, do_ref):
        db = dx_ref[...].astype(BF16)
        dxb_ref[...] = db
        do = _dot_nt(db, wload(w_ref))
        for c in range(8):
            do_ref[c] = do[:, c * PAIR:(c + 1) * PAIR]

    tok = pl.BlockSpec((tm, D), lambda i: (i, 0))
    body, in_specs, args = _with_dep(body, dep, [tok, wspec], [dx, w])
    return pl.pallas_call(
        body, name="oproj_bwd", grid=(T // tm,),
        in_specs=in_specs,
        out_specs=[tok, pl.BlockSpec((8, tm, PAIR), lambda i: (0, i, 0))],
        out_shape=[jax.ShapeDtypeStruct((T, D), BF16), jax.ShapeDtypeStruct((8, T, PAIR), F32)],
        compiler_params=_params("parallel"),
    )(*args)


def _attn_post(qkv, gains2, dqa, dka, dva, dqb, dkb, dvb, tm):
    T, NQ = qkv.shape
    scale = HEAD_DIM ** -0.5

    def body(qkv_ref, g_ref, qa1, qa4, qa16, ka1, ka4, ka16, va1, va4, va16, qb_ref, kb_ref, vb_ref,
             out_ref, dg_ref):
        lo = _lo_mask((tm, PAIR))

        @pl.when(pl.program_id(0) == 0)
        def _():
            dg_ref[...] = jnp.zeros_like(dg_ref)

        def norm_bwd(c, gi, dy):
            xv = qkv_ref[:, c * PAIR:(c + 1) * PAIR]
            r = lax.rsqrt(_half_sum(xv * xv, lo) * (1.0 / HEAD_DIM) + EPS)
            xn = xv * r
            dg_ref[gi:gi + 1, :] += jnp.sum(dy * xn, axis=0, keepdims=True)
            dxn = dy * g_ref[gi:gi + 1, :]
            dx = r * (dxn - xn * (_half_sum(dxn * xn, lo) * (1.0 / HEAD_DIM)))
            out_ref[:, c * PAIR:(c + 1) * PAIR] = dx.astype(BF16)

        def fold(v):
            return v + pltpu.roll(v, HEAD_DIM, 1)

        for c in range(4):
            norm_bwd(c, 0, (qa1[c] + qa4[c] + qa16[c]) * scale)
            norm_bwd(4 + c, 1, ka1[c] + ka4[c] + ka16[c])
            out_ref[:, (8 + c) * PAIR:(9 + c) * PAIR] = (va1[c] + va4[c] + va16[c]).astype(BF16)
            norm_bwd(12 + c, 2, qb_ref[c] * scale)
        norm_bwd(16, 3, jnp.where(lo, fold(kb_ref[0]), fold(kb_ref[1])))
        out_ref[:, 17 * PAIR:18 * PAIR] = jnp.where(lo, fold(vb_ref[0]), fold(vb_ref[1])).astype(BF16)

    four = pl.BlockSpec((4, tm, PAIR), lambda i: (0, i, 0))
    two = pl.BlockSpec((2, tm, PAIR), lambda i: (0, i, 0))
    return pl.pallas_call(
        body, name="attn_post", grid=(T // tm,),
        in_specs=[pl.BlockSpec((tm, NQ), lambda i: (i, 0)), pl.BlockSpec((4, PAIR), lambda i: (0, 0))]
        + [four] * 10 + [two, two],
        out_specs=[pl.BlockSpec((tm, NQ), lambda i: (i, 0)), pl.BlockSpec((4, PAIR), lambda i: (0, 0))],
        out_shape=[jax.ShapeDtypeStruct((T, NQ), BF16), jax.ShapeDtypeStruct((4, PAIR), F32)],
        compiler_params=_params("arbitrary"),
    )(qkv, gains2, *dqa, *dka, *dva, dqb, dkb, dvb)


def _dense_norm_bwd(dres, dz, w, blk, x, g, tm):
    T, D = x.shape
    N = dz.shape[1]
    wspec, wload = _weight_arg(w, blk)

    def body(dres_ref, dz_ref, w_ref, x_ref, g_ref, dx_ref, dgn_ref):
        i = pl.program_id(0)
        dx, dg = _norm_bwd(_dot_nt(dz_ref[...], wload(w_ref)), x_ref[...], g_ref[...])
        dx_ref[...] = dres_ref[...] + dx

        @pl.when(i == 0)
        def _():
            dgn_ref[...] = dg

        @pl.when(i > 0)
        def _():
            dgn_ref[...] += dg

    tok = pl.BlockSpec((tm, D), lambda i: (i, 0))
    row = pl.BlockSpec((1, D), lambda i: (0, 0))
    return pl.pallas_call(
        body, name="dense_norm_bwd", grid=(T // tm,),
        in_specs=[tok, pl.BlockSpec((tm, N), lambda i: (i, 0)), wspec, tok, row],
        out_specs=[tok, row],
        out_shape=[jax.ShapeDtypeStruct((T, D), F32), jax.ShapeDtypeStruct((1, D), F32)],
        compiler_params=_params("arbitrary"),
    )(dres, dz, w, x, g)


def _bias_reduce(onehot, dbm):
    Hb, K = dbm.shape

    def body(oh_ref, d_ref, out_ref):
        oh = oh_ref[...]
        d = d_ref[...]
        hi = d.astype(BF16)
        r1 = d - hi.astype(F32)
        mid = r1.astype(BF16)
        low = (r1 - mid.astype(F32)).astype(BF16)
        out_ref[...] = _dot_nt(hi, oh) + _dot_nt(mid, oh) + _dot_nt(low, oh)

    vm = pl.BlockSpec(memory_space=pltpu.VMEM)
    return pl.pallas_call(
        body, name="bias_reduce", in_specs=[vm, vm], out_specs=vm,
        out_shape=jax.ShapeDtypeStruct((Hb, 128), F32),
        compiler_params=pltpu.CompilerParams(vmem_limit_bytes=VMEM_LIMIT),
    )(onehot, dbm)


def _ple_fwd(x, g, wg, blk, p, wp, target, tm):
    T, D = x.shape
    P = p.shape[1]
    with_loss = target is not None
    wspec, wload = _weight_arg(wg, blk)

    def body(*refs):
        if with_loss:
            x_ref, g_ref, wg_ref, p_ref, wp_ref, t_ref, y_ref, hn_ref, gate_ref, pp_ref, pb_ref, loss_ref = refs
        else:
            x_ref, g_ref, wg_ref, p_ref, wp_ref, y_ref, hn_ref, gate_ref, pp_ref, pb_ref = refs
        i = pl.program_id(0)
        xv = x_ref[...]
        hb = (xv * _rstd(xv) * g_ref[...]).astype(BF16)
        hn_ref[...] = hb
        gate = _sigmoid(_dot(hb, wload(wg_ref)))
        pb = p_ref[...].astype(BF16)
        pb_ref[...] = pb
        pp = _dot(pb, wp_ref[...])
        gate_ref[...] = gate
        pp_ref[...] = pp
        y = xv + gate * pp
        if with_loss:
            err = y - t_ref[...]
            y_ref[...] = err * (1.0 / D)
            part = jnp.broadcast_to(0.5 * jnp.sum(jnp.sum(err * err, axis=1, keepdims=True) * (1.0 / D),
                                                  axis=0, keepdims=True), (1, 128))

            @pl.when(i == 0)
            def _():
                loss_ref[...] = part

            @pl.when(i > 0)
            def _():
                loss_ref[...] += part
        else:
            y_ref[...] = y

    tok = pl.BlockSpec((tm, D), lambda i: (i, 0))
    ptok = pl.BlockSpec((tm, P), lambda i: (i, 0))
    in_specs = [tok, pl.BlockSpec((1, D), lambda i: (0, 0)), wspec, ptok,
                pl.BlockSpec((P, D), lambda i: (0, 0))]
    out_specs = [tok, tok, tok, tok, ptok]
    out_shape = [jax.ShapeDtypeStruct((T, D), F32), jax.ShapeDtypeStruct((T, D), BF16),
                 jax.ShapeDtypeStruct((T, D), F32), jax.ShapeDtypeStruct((T, D), F32),
                 jax.ShapeDtypeStruct((T, P), BF16)]
    args = [x, g, wg, p, wp]
    if with_loss:
        in_specs.append(tok)
        out_specs.append(pl.BlockSpec((1, 128), lambda i: (0, 0)))
        out_shape.append(jax.ShapeDtypeStruct((1, 128), F32))
        args.append(target)
    return pl.pallas_call(
        body, name="ple_fwd_loss" if with_loss else "ple_fwd", grid=(T // tm,),
        in_specs=in_specs, out_specs=out_specs, out_shape=out_shape,
        compiler_params=_params("arbitrary" if with_loss else "parallel"),
    )(*args)


def _ple_bwd(dy, gate, pp, tm, dep=None):
    T, D = dy.shape

    def body(dy_ref, gate_ref, pp_ref, dgl_ref, dpp_ref):
        d = dy_ref[...]
        gt = gate_ref[...]
        dgl_ref[...] = (d * pp_ref[...] * gt * (1.0 - gt)).astype(BF16)
        dpp_ref[...] = (d * gt).astype(BF16)

    tok = pl.BlockSpec((tm, D), lambda i: (i, 0))
    body, in_specs, args = _with_dep(body, dep, [tok, tok, tok], [dy, gate, pp])
    return pl.pallas_call(
        body, name="ple_bwd", grid=(T // tm,), in_specs=in_specs, out_specs=[tok, tok],
        out_shape=[jax.ShapeDtypeStruct((T, D), BF16), jax.ShapeDtypeStruct((T, D), BF16)],
        compiler_params=_params("parallel"),
    )(*args)


def _adamw(w, g, m, v):
    shape = w.shape
    C = shape[-1]
    w2, g2, m2, v2 = (a.reshape(-1, C) for a in (w, g, m, v))
    Rn = w2.shape[0]
    tr = Rn
    for cand in (512, 352, 256):
        if Rn % cand == 0:
            tr = cand
            break
    c1 = 1.0 - ADAM_B1 ** ADAM_STEP
    c2 = 1.0 - ADAM_B2 ** ADAM_STEP

    def body(w_ref, g_ref, m_ref, v_ref, d_ref, nm_ref, nv_ref):
        gv = g_ref[...]
        mn = ADAM_B1 * m_ref[...] + (1.0 - ADAM_B1) * gv
        vn = ADAM_B2 * v_ref[...] + (1.0 - ADAM_B2) * (gv * gv)
        d_ref[...] = -ADAM_LR * ((mn / c1) / (jnp.sqrt(vn / c2) + ADAM_EPS) + ADAM_WD * w_ref[...])
        nm_ref[...] = mn
        nv_ref[...] = vn

    spec = pl.BlockSpec((tr, C), lambda i: (i, 0))
    sh = jax.ShapeDtypeStruct((Rn, C), F32)
    d, nm, nv = pl.pallas_call(
        body, name="adamw", grid=(Rn // tr,), in_specs=[spec] * 4, out_specs=[spec] * 3, out_shape=[sh] * 3,
        compiler_params=_params("parallel"),
    )(w2, g2, m2, v2)
    return d.reshape(shape), nm.reshape(shape), nv.reshape(shape)


def _my_place():
    x, y, c = lax.axis_index("x"), lax.axis_index("y"), lax.axis_index("c")
    chips = [(1 - x, y), (x, 1 - y), (1 - x, 1 - y)]
    return x, y, c, chips


def _all_gather(flat):
    R, Wd = flat.shape

    def body(x_ref, out_ref, send_sems, recv_sems, local_sem):
        x, y, c, chips = _my_place()
        me, sibling = (x, y, c), (x, y, 1 - c)

        def rows(px, py, pc):
            return out_ref.at[4 * px + 2 * py + pc]

        def copy(k, block, to, src=None):
            return pltpu.make_async_remote_copy(
                src_ref=rows(*block) if src is None else src, dst_ref=rows(*block),
                send_sem=send_sems.at[k], recv_sem=recv_sems.at[k], device_id=to, device_id_type=MESH)

        mine = pltpu.make_async_copy(x_ref, rows(*me), local_sem)
        mine.start()
        first = [copy(0, me, sibling, src=x_ref)]
        first += [copy(1 + j, me, (*chip, c), src=x_ref) for j, chip in enumerate(chips)]
        for cp in first:
            cp.start()
        passed = [copy(4 + j, (*chip, c), sibling) for j, chip in enumerate(chips)]
        for j, chip in enumerate(chips):
            copy(1 + j, (*chip, c), me).wait_recv()
            passed[j].start()
        copy(0, sibling, me).wait_recv()
        for j, chip in enumerate(chips):
            copy(4 + j, (*chip, 1 - c), me).wait_recv()
        for cp in first + passed:
            cp.wait_send()
        mine.wait()

    return pl.pallas_call(
        body, name="all_gather",
        in_specs=[pl.BlockSpec(memory_space=pl.ANY)], out_specs=pl.BlockSpec(memory_space=pl.ANY),
        out_shape=jax.ShapeDtypeStruct((N_DEV, R, Wd), flat.dtype),
        scratch_shapes=[pltpu.SemaphoreType.DMA((7,)), pltpu.SemaphoreType.DMA((7,)), pltpu.SemaphoreType.DMA],
    )(flat)


def _reduce_scatter(gparts, tr):
    _, R, Wd = gparts.shape
    nt = R // tr

    def body(g_ref, out_ref, a_ref, p_ref, b_ref, vb, vo_b, vo_f, d2d_send, d2d_recv, ici_send, ici_recv):
        x, y, c, chips = _my_place()
        sibling = (x, y, 1 - c)
        allchips = [(x, y)] + chips

        def dev(chip, pc):
            return 4 * chip[0] + 2 * chip[1] + pc

        d2d = [pltpu.make_async_remote_copy(
            src_ref=g_ref.at[dev(q, 1 - c)], dst_ref=a_ref.at[a], send_sem=d2d_send.at[a], recv_sem=d2d_recv.at[a],
            device_id=sibling, device_id_type=MESH) for a, q in enumerate(allchips)]
        for cp in d2d:
            cp.start()

        def add_tiles(srcs, dst, vo):
            def step(t, carry):
                r = pl.ds(pl.multiple_of(t * tr, tr), tr)
                acc = None
                for s_i, src in enumerate(srcs):
                    pltpu.sync_copy(src.at[r], vb.at[s_i])
                for s_i in range(len(srcs)):
                    term = vb[s_i].astype(F32)
                    acc = term if acc is None else acc + term
                vo[...] = acc.astype(vo.dtype)
                pltpu.sync_copy(vo, dst.at[r])
                return carry

            lax.fori_loop(0, nt, step, 0)

        ici = []
        for j, q in enumerate(chips):
            d2d[j + 1].wait_recv()
            add_tiles([g_ref.at[dev(q, c)], a_ref.at[j + 1]], p_ref.at[j], vo_b)
            cp = pltpu.make_async_remote_copy(
                src_ref=p_ref.at[j], dst_ref=b_ref.at[j], send_sem=ici_send.at[j], recv_sem=ici_recv.at[j],
                device_id=(*q, c), device_id_type=MESH)
            cp.start()
            ici.append(cp)
        d2d[0].wait_recv()
        for cp in ici:
            cp.wait_recv()
        add_tiles([g_ref.at[dev((x, y), c)], a_ref.at[0], b_ref.at[0], b_ref.at[1], b_ref.at[2]], out_ref, vo_f)
        for cp in d2d + ici:
            cp.wait_send()

    hbm = pl.BlockSpec(memory_space=pl.ANY)
    out, _, _, _ = pl.pallas_call(
        body, name="reduce_scatter",
        in_specs=[hbm], out_specs=[hbm, hbm, hbm, hbm],
        out_shape=[jax.ShapeDtypeStruct((R, Wd), F32), jax.ShapeDtypeStruct((4, R, Wd), BF16),
                   jax.ShapeDtypeStruct((3, R, Wd), BF16), jax.ShapeDtypeStruct((3, R, Wd), BF16)],
        scratch_shapes=[pltpu.VMEM((5, tr, Wd), BF16), pltpu.VMEM((tr, Wd), BF16), pltpu.VMEM((tr, Wd), F32),
                        pltpu.SemaphoreType.DMA((4,)), pltpu.SemaphoreType.DMA((4,)),
                        pltpu.SemaphoreType.DMA((3,)), pltpu.SemaphoreType.DMA((3,))],
        compiler_params=pltpu.CompilerParams(vmem_limit_bytes=VMEM_LIMIT),
    )(gparts)
    return out


def _peer(x, y, c, k):
    return (x ^ ((k >> 2) & 1), y ^ ((k >> 1) & 1), c ^ (k & 1))


HBM_SPEC = pl.BlockSpec(memory_space=pltpu.HBM)
SEM_SPEC = pl.BlockSpec(memory_space=pltpu.SEMAPHORE)


def _exchange_refs(srcs, lands, m, k, x, y, c, scatter):
    peer = _peer(x, y, c, k)
    if scatter:
        return srcs[m].at[4 * peer[0] + 2 * peer[1] + peer[2]], lands[m].at[k - 1], peer
    return srcs[m], lands[m].at[4 * x + 2 * y + c], peer


def _exchange_start(arrs, land_shapes, scatter, name):
    n = len(arrs)

    def body(*refs):
        srcs, lands = refs[:n], refs[n:2 * n]
        send_sems, recv_sems = refs[2 * n], refs[2 * n + 1]
        token = refs[-1]
        x, y, c, _ = _my_place()
        for m in range(n):
            for k in range(1, N_DEV):
                src, dst, peer = _exchange_refs(srcs, lands, m, k, x, y, c, scatter)
                pltpu.make_async_remote_copy(
                    src_ref=src, dst_ref=dst, send_sem=send_sems.at[7 * m + k - 1],
                    recv_sem=recv_sems.at[7 * m + k - 1], device_id=peer, device_id_type=MESH).start()
        token[...] = jnp.zeros_like(token)

    zones = [lax.empty(s_, a.dtype) for s_, a in zip(land_shapes, arrs)]
    outs = pl.pallas_call(
        body, name=name,
        out_shape=(pltpu.SemaphoreType.DMA((7 * n,)), pltpu.SemaphoreType.DMA((7 * n,)),
                   *[pltpu.HBM(a.shape, a.dtype) for a in arrs], *[pltpu.HBM(z.shape, z.dtype) for z in zones],
                   jax.ShapeDtypeStruct((8, 128), F32)),
        in_specs=[HBM_SPEC] * (2 * n),
        out_specs=(SEM_SPEC, SEM_SPEC, *[HBM_SPEC] * (2 * n), pl.BlockSpec(memory_space=pltpu.VMEM)),
        input_output_aliases={m: 2 + m for m in range(2 * n)},
        compiler_params=pltpu.CompilerParams(has_side_effects=pltpu.SideEffectType.DATAFLOW_SIDE_EFFECTING),
    )(*[pltpu.with_memory_space_constraint(a, pltpu.HBM) for a in arrs],
      *[pltpu.with_memory_space_constraint(z, pltpu.HBM) for z in zones])
    return outs[0], outs[1], list(outs[2:2 + n]), list(outs[2 + n:2 + 2 * n]), outs[-1]


def _exchange_wait(send_sems, recv_sems, arrs, zones, after, scatter, name):
    n = len(arrs)

    def body(*refs):
        srcs, lands = refs[:n], refs[n:2 * n]
        send_sems, recv_sems = refs[2 * n], refs[2 * n + 1]
        x, y, c, _ = _my_place()
        for m in range(n):
            for k in range(1, N_DEV):
                src, dst, peer = _exchange_refs(srcs, lands, m, k, x, y, c, scatter)
                cp = pltpu.make_async_remote_copy(
                    src_ref=src, dst_ref=dst, send_sem=send_sems.at[7 * m + k - 1],
                    recv_sem=recv_sems.at[7 * m + k - 1], device_id=peer, device_id_type=MESH)
                cp.wait_send()
                cp.wait_recv()

    outs = pl.pallas_call(
        body, name=name,
        out_shape=tuple(pltpu.HBM(a.shape, a.dtype) for a in list(arrs) + list(zones)),
        in_specs=[HBM_SPEC] * (2 * n) + [SEM_SPEC, SEM_SPEC, pl.BlockSpec(memory_space=pl.ANY)],
        out_specs=tuple([HBM_SPEC] * (2 * n)),
        input_output_aliases={m: m for m in range(2 * n)},
        compiler_params=pltpu.CompilerParams(has_side_effects=pltpu.SideEffectType.DATAFLOW_SIDE_EFFECTING),
    )(*arrs, *zones, send_sems, recv_sems, after)
    return list(outs[n:])


def _sum_parts(own, parts, tr, dep=None):
    R, W = own.shape

    def body(own_ref, parts_ref, out_ref):
        acc = own_ref[...].astype(F32)
        for k in range(N_DEV - 1):
            acc = acc + parts_ref[k].astype(F32)
        out_ref[...] = acc

    in_specs = [pl.BlockSpec((tr, W), lambda i: (i, 0)), pl.BlockSpec((N_DEV - 1, tr, W), lambda i: (0, i, 0))]
    body, in_specs, args = _with_dep(body, dep, in_specs, [own, parts])
    return pl.pallas_call(
        body, name="sum_parts", grid=(R // tr,),
        in_specs=in_specs,
        out_specs=pl.BlockSpec((tr, W), lambda i: (i, 0)),
        out_shape=jax.ShapeDtypeStruct((R, W), F32),
        compiler_params=_params("parallel"),
    )(*args)


def _all_reduce_small(v):
    Rn, Wd = v.shape

    def body(v_ref, out_ref, gat_ref, send_sems, recv_sems):
        x, y, c, _ = _my_place()
        me = 4 * x + 2 * y + c
        gat_ref[me] = v_ref[...]
        copies = []
        for k in range(1, N_DEV):
            fx, fy, fc = (k >> 2) & 1, (k >> 1) & 1, k & 1
            peer = (x ^ fx, y ^ fy, c ^ fc)
            cp = pltpu.make_async_remote_copy(
                src_ref=v_ref, dst_ref=gat_ref.at[me], send_sem=send_sems.at[k - 1], recv_sem=recv_sems.at[k - 1],
                device_id=peer, device_id_type=MESH)
            cp.start()
            copies.append(cp)
        for cp in copies:
            cp.wait_recv()
        for cp in copies:
            cp.wait_send()
        acc = gat_ref[0]
        for k in range(1, N_DEV):
            acc = acc + gat_ref[k]
        out_ref[...] = acc

    vm = pl.BlockSpec(memory_space=pltpu.VMEM)
    return pl.pallas_call(
        body, name="all_reduce_small", in_specs=[vm], out_specs=vm,
        out_shape=jax.ShapeDtypeStruct((Rn, Wd), F32),
        scratch_shapes=[pltpu.VMEM((N_DEV, Rn, Wd), F32), pltpu.SemaphoreType.DMA((7,)),
                        pltpu.SemaphoreType.DMA((7,))],
    )(v)


def _t5_bucket(rel):
    half = N_BUCKETS // 2
    max_exact = half // 2
    ret = jnp.where(rel > 0, half, 0)
    n = jnp.abs(rel)
    nf = jnp.maximum(n, 1).astype(F32)
    large = max_exact + (jnp.log(nf / max_exact) / math.log(MAX_DISTANCE / max_exact)
                         * (half - max_exact)).astype(jnp.int32)
    large = jnp.minimum(large, half - 1)
    return ret + jnp.where(n < max_exact, n, large)


def _band(R, d):
    W = BQ + 2 * R
    rel = jnp.arange(W)[None, :] - R - jnp.arange(BQ)[:, None]
    return _t5_bucket(rel * d), jnp.abs(rel) <= R


def _onehot(R, d):
    bkt, in_band = _band(R, d)
    return ((bkt.reshape(1, -1) == jnp.arange(128)[:, None]) & in_band.reshape(1, -1)).astype(BF16)


def _bias_expand(table_t, onehot):
    H = table_t.shape[0]
    K = onehot.shape[1]

    def body(t_ref, oh_ref, out_ref):
        oh = oh_ref[...]
        t = t_ref[...]
        hi = t.astype(BF16)
        r1 = t - hi.astype(F32)
        mid = r1.astype(BF16)
        low = (r1 - mid.astype(F32)).astype(BF16)
        marked = _dot(jnp.ones(t.shape, BF16), oh) > 0.5
        out_ref[...] = jnp.where(marked, _dot(hi, oh) + _dot(mid, oh) + _dot(low, oh), NEG)

    vm = pl.BlockSpec(memory_space=pltpu.VMEM)
    return pl.pallas_call(
        body, name="bias_expand", in_specs=[vm, vm], out_specs=vm,
        out_shape=jax.ShapeDtypeStruct((H, K), F32),
        compiler_params=pltpu.CompilerParams(vmem_limit_bytes=VMEM_LIMIT),
    )(table_t, onehot)


def _bias_matrix(table, R, d):
    table_t = jnp.pad(table.T, ((0, 0), (0, 128 - N_BUCKETS)))
    return _bias_expand(table_t, _onehot(R, d)).reshape(table.shape[1], BQ, BQ + 2 * R)


def _bias_variants(base, R):
    H, _, W = base.shape
    col = jnp.arange(W)
    before, after = col < R, col >= BQ + R
    masks = jnp.stack([jnp.zeros_like(before), before, after, before | after])
    v = jnp.where(masks[None, :, None, :], NEG, base[:, None])
    v = v.reshape(H // 2, 2, 4, BQ, W).transpose(0, 2, 1, 3, 4).reshape(H // 2, 4, 2 * BQ, W)
    return v, v.transpose(0, 1, 3, 2)


def _bias_grad(dbt, R, d):
    P, W, _ = dbt.shape
    dbm = dbt.reshape(P, W, 2, BQ).transpose(0, 2, 3, 1).reshape(2 * P, BQ * W)
    return _bias_reduce(_onehot(R, d), dbm)[:, :N_BUCKETS].T


def _deint(a, d):
    if d == 1:
        return a
    H, T, X = a.shape
    return a.reshape(H, T // d, d, X).transpose(0, 2, 1, 3).reshape(H * d, T // d, X)


def _reint(a, d):
    if d == 1:
        return a
    Hd, L, X = a.shape
    return a.reshape(Hd // d, d, L, X).transpose(0, 2, 1, 3).reshape(Hd // d, L * d, X)


def _pad_rows(a, R):
    return jnp.pad(a, ((0, 0), (R, R), (0, 0)))


def _tile2(gain):
    return jnp.concatenate([gain, gain])


ROW_W_O, ROW_GATE, ROW_QKV, ROW_PROJ, B_ROWS = 768, 896, 1024, 1312, 1344
BLK_W_O, BLK_GATE = ROW_W_O // 128, ROW_GATE // 128


def _pack_layer(wts, i):
    a = jnp.stack([wts["ffn1_w_in"][i], wts["ffn2_w_in"][i]])
    D = a.shape[1]
    b = jnp.concatenate([
        wts["ffn1_w_out"][i], wts["ffn2_w_out"][i],
        jnp.zeros((ROW_W_O - 2 * wts["ffn1_w_out"].shape[1], D), a.dtype),
        wts["w_o"][i], wts["w_ple_gate"][i], wts["w_qkv"][i].reshape(-1, D), wts["w_ple_proj"][i].reshape(-1, D)])
    return a, b


def _unpack_layer(sums, like):
    w_in2, b1, b2, w_in1, w_out1 = sums
    n_out, n_sq = like["ffn1_w_out"].shape[1], like["w_o"].shape[1]
    out = {}
    if w_in2 is not None:
        out.update(ffn2_w_in=w_in2, ffn2_w_out=b1[:n_out], w_ple_gate=b1[n_out:n_out + n_sq],
                   w_ple_proj=b1[n_out + n_sq:].reshape(like["w_ple_proj"].shape[1:]))
    if b2 is not None:
        out.update(w_o=b2[:n_sq], w_qkv=b2[n_sq:].reshape(like["w_qkv"].shape[1:]))
    if w_in1 is not None:
        out.update(ffn1_w_in=w_in1, ffn1_w_out=w_out1)
    return out


def _col_sharded(gb, r0, r1, rows):
    return gb[:, r0:r1].reshape(N_DEV, rows, -1).transpose(1, 0, 2).reshape(rows, -1)


def _to_col_shards(g):
    rows = g.shape[0]
    return g.reshape(rows, N_DEV, -1).transpose(1, 0, 2).reshape(N_DEV, -1, 1024)


def _layer_weights(ga, gb, p_dim):
    return dict(ga=ga, gb=gb, w_qkv=_col_sharded(gb, ROW_QKV, ROW_PROJ, ga.shape[2]),
                w_proj=_col_sharded(gb, ROW_PROJ, B_ROWS, p_dim))


def _layer_fwd(x, p, w, sm, i, target, tm, biases, dep=None):
    ga, gb = w["ga"], w["gb"]
    saved = {}
    saved["x0"] = x
    x1, saved["h1"], saved["zg1"], saved["zu1"], saved["s1"] = _ffn_fwd(
        x, sm["norm_ffn1"][i][None], ga, gb, 0, tm, dep)
    saved["x1"] = x1
    qkv, saved["hm"] = _qkv_fwd(x1, sm["norm_mix"][i][None], w["w_qkv"], tm)
    saved["qkv"] = qkv
    gains2 = jnp.stack([_tile2(sm[k][i]) for k in ("q_norm_a", "k_norm_a", "q_norm_b", "k_norm_b")])
    saved["gains2"] = gains2
    qa, ka, va, qb, kb, vb = _attn_prep(qkv, gains2, tm)
    no_sink = jnp.full((8,), NEG, F32)
    branches = []
    outs = []
    for (R, d), bias in zip(DILATED, biases[:3]):
        qd, kd, vd = _deint(qa, d), _pad_rows(_deint(ka, d), R), _pad_rows(_deint(va, d), R)
        sink = jnp.tile(no_sink, d)
        o, lse = _attn_fwd(qd, kd, vd, bias[0], sink, R, 1, d)
        branches.append((qd, kd, vd, bias, sink, R, d))
        outs += [_reint(o, d), _reint(lse, d)]
    bias_b = biases[3]
    kbp, vbp = _pad_rows(kb, SWA_RADIUS), _pad_rows(vb, SWA_RADIUS)
    sink_b = sm["sink_b"][i]
    ob, lb = _attn_fwd(qb, kbp, vbp, bias_b[0], sink_b, SWA_RADIUS, 2, 1)
    oa, la, o_cat = _attn_merge(*outs, ob, tm)
    saved.update(branches=branches, b=(qb, kbp, vbp, bias_b, sink_b), oa=oa, la=la, ob=ob, lb=lb, o_cat=o_cat)
    x2 = _oproj_fwd(x1, o_cat, gb, BLK_W_O, tm)
    saved["x2"] = x2
    x3, saved["h2"], saved["zg2"], saved["zu2"], saved["s2"] = _ffn_fwd(
        x2, sm["norm_ffn2"][i][None], ga, gb, 1, tm)
    saved["x3"] = x3
    res = _ple_fwd(x3, sm["norm_ple"][i][None], gb, BLK_GATE, p, w["w_proj"], target, tm)
    y, saved["hp"], saved["gate"], saved["pp"], saved["pb"] = res[:5]
    loss = res[5] if target is not None else None
    return y, loss, saved


def _layer_bwd(dy, w, sm, i, sv, tm, dep=None, on_ready=None, on_small=None):
    ga, gb = w["ga"], w["gb"]
    gs = {}
    D = dy.shape[1]
    dgl, dpp = _ple_bwd(dy, sv["gate"], sv["pp"], tm, dep)
    d_gate = _matmul_tn(sv["hp"], dgl, D, 2 * tm)
    d_proj = _matmul_tn(sv["pb"], dpp, D, 2 * tm)
    dx3, gs["norm_ple"] = _dense_norm_bwd(dy, dgl, gb, BLK_GATE, sv["x3"], sm["norm_ple"][i][None], tm)
    dx2, dyb, dzg, dzu, gs["norm_ffn2"] = _ffn_bwd(dx3, sv["x2"], sm["norm_ffn2"][i][None], sv["zg2"], sv["zu2"],
                                                   ga, gb, 1, tm)
    dwin2, dwo2 = _ffn_dw(sv["h2"], dzg, dzu, sv["s2"], dyb, 2 * tm)
    half = dwo2.shape[1] // 2
    after_ffn2 = [dwin2, jnp.concatenate([dwo2.reshape(N_DEV, half, D), d_gate.reshape(N_DEV, -1, D),
                                          _to_col_shards(d_proj)], axis=1)]
    token = None if on_ready is None else on_ready(0, after_ffn2)
    dx2b, do = _oproj_bwd(dx2, gb, BLK_W_O, tm, token)
    d_wo = _matmul_tn(sv["o_cat"], dx2b, D, 2 * tm)
    do_a, do_b = do[:4], do[4:]
    dqa, dka, dva, dbias = [], [], [], []
    for qd, kd, vd, bias, sink, R, d in sv["branches"]:
        dq, dk, dv, dbm, _ = _attn_bwd(qd, kd, vd, bias[1], sink, _deint(sv["oa"], d), _deint(sv["la"], d),
                                        _deint(do_a, d), R, 1, d)
        L = qd.shape[1]
        dqa.append(_reint(dq, d))
        dka.append(_reint(dk[:, R:R + L], d))
        dva.append(_reint(dv[:, R:R + L], d))
        dbias.append(dbm)
    qb, kbp, vbp, bias_b, sink_b = sv["b"]
    dqb, dkb, dvb, dbm_b, dsink = _attn_bwd(qb, kbp, vbp, bias_b[1], sink_b, sv["ob"], sv["lb"], do_b,
                                            SWA_RADIUS, 2, 1)
    T = qb.shape[1]
    gs["rel_bias"] = dbias + [dbm_b]
    gs["sink_b"] = jnp.sum(dsink[:, 0].reshape(-1, 2, BQ), axis=2).reshape(-1)
    dqkv, dgains2 = _attn_post(sv["qkv"], sv["gains2"], dqa, dka, dva, dqb,
                               dkb[:, SWA_RADIUS:SWA_RADIUS + T], dvb[:, SWA_RADIUS:SWA_RADIUS + T], tm // 2)
    dgains = dgains2[:, :HEAD_DIM] + dgains2[:, HEAD_DIM:]
    for k, name in enumerate(("q_norm_a", "k_norm_a", "q_norm_b", "k_norm_b")):
        gs[name] = dgains[k]
    d_qkv = _matmul_tn(sv["hm"], dqkv, dqkv.shape[1] // 2, 2 * tm)
    after_mixer = [jnp.concatenate([d_wo.reshape(N_DEV, -1, D), _to_col_shards(d_qkv)], axis=1)]
    token = None if on_ready is None else on_ready(1, after_mixer)
    dx1, gs["norm_mix"] = _dense_norm_bwd(dx2, dqkv, w["w_qkv"], None, sv["x1"], sm["norm_mix"][i][None], tm)
    dx0, dyb, dzg, dzu, gs["norm_ffn1"] = _ffn_bwd(dx1, sv["x0"], sm["norm_ffn1"][i][None], sv["zg1"], sv["zu1"],
                                                   ga, gb, 0, tm, token)
    token = None if on_small is None else on_small(gs)
    dwin1, dwo1 = _ffn_dw(sv["h1"], dzg, dzu, sv["s1"], dyb, 2 * tm, token)
    return dx0, (after_ffn2, after_mixer, [dwin1, dwo1.reshape(N_DEV, half, D)]), gs


def _bias_matrices(rel_bias):
    biases = [_bias_variants(_bias_matrix(rel_bias[:, :8], R, d), R) for R, d in DILATED]
    biases.append(_bias_variants(_bias_matrix(rel_bias[:, 8:], SWA_RADIUS, 1), SWA_RADIUS))
    return biases


def _stack_small(per_layer):
    small = {}
    for k, v in per_layer.items():
        if k == "rel_bias":
            per_branch = [sum(parts) for parts in zip(*v.values())]
            drel_a = sum(_bias_grad(t, R, d) for t, (R, d) in zip(per_branch[:3], DILATED))
            small[k] = jnp.concatenate([drel_a, _bias_grad(per_branch[3], SWA_RADIUS, 1)], axis=1)
        else:
            small[k] = jnp.stack([v[i].reshape(-1) for i in sorted(v)])
    return small


TM = 512
SUM_TILES = (512, 512, 416, 512, 352)
LAST_GROUP = ("ffn1_w_in", "ffn1_w_out")


def _pack_small(d, extra=None):
    parts = [d[k].reshape(-1) for k in SMALL]
    if extra is not None:
        parts.append(extra.reshape(-1))
    flat = jnp.concatenate(parts)
    return jnp.pad(flat, (0, SMALL_ROWS * 128 - flat.shape[0])).reshape(SMALL_ROWS, 128)


def _unpack_small(buf, like):
    flat = buf.reshape(-1)
    out, off = {}, 0
    for k in SMALL:
        n = like[k].size
        out[k] = flat[off:off + n].reshape(like[k].shape)
        off += n
    return out, flat[off]


def kernel(x, p, rel_bias, norm_ffn1, ffn1_w_in, ffn1_w_out, norm_mix, w_qkv, q_norm_a, k_norm_a, q_norm_b, k_norm_b, sink_b, w_o, norm_ffn2, ffn2_w_in, ffn2_w_out, norm_ple, w_ple_gate, w_ple_proj, loss_target, m_rel_bias, m_norm_ffn1, m_ffn1_w_in, m_ffn1_w_out, m_norm_mix, m_w_qkv, m_q_norm_a, m_k_norm_a, m_q_norm_b, m_k_norm_b, m_sink_b, m_w_o, m_norm_ffn2, m_ffn2_w_in, m_ffn2_w_out, m_norm_ple, m_w_ple_gate, m_w_ple_proj, v_rel_bias, v_norm_ffn1, v_ffn1_w_in, v_ffn1_w_out, v_norm_mix, v_w_qkv, v_q_norm_a, v_k_norm_a, v_q_norm_b, v_k_norm_b, v_sink_b, v_w_o, v_norm_ffn2, v_ffn2_w_in, v_ffn2_w_out, v_norm_ple, v_w_ple_gate, v_w_ple_proj):
    wts = dict(rel_bias=rel_bias, norm_ffn1=norm_ffn1, ffn1_w_in=ffn1_w_in, ffn1_w_out=ffn1_w_out,
               norm_mix=norm_mix, w_qkv=w_qkv, q_norm_a=q_norm_a, k_norm_a=k_norm_a, q_norm_b=q_norm_b,
               k_norm_b=k_norm_b, sink_b=sink_b, w_o=w_o, norm_ffn2=norm_ffn2, ffn2_w_in=ffn2_w_in,
               ffn2_w_out=ffn2_w_out, norm_ple=norm_ple, w_ple_gate=w_ple_gate, w_ple_proj=w_ple_proj)
    mom = dict(rel_bias=m_rel_bias, norm_ffn1=m_norm_ffn1, ffn1_w_in=m_ffn1_w_in, ffn1_w_out=m_ffn1_w_out,
               norm_mix=m_norm_mix, w_qkv=m_w_qkv, q_norm_a=m_q_norm_a, k_norm_a=m_k_norm_a, q_norm_b=m_q_norm_b,
               k_norm_b=m_k_norm_b, sink_b=m_sink_b, w_o=m_w_o, norm_ffn2=m_norm_ffn2, ffn2_w_in=m_ffn2_w_in,
               ffn2_w_out=m_ffn2_w_out, norm_ple=m_norm_ple, w_ple_gate=m_w_ple_gate, w_ple_proj=m_w_ple_proj)
    var = dict(rel_bias=v_rel_bias, norm_ffn1=v_norm_ffn1, ffn1_w_in=v_ffn1_w_in, ffn1_w_out=v_ffn1_w_out,
               norm_mix=v_norm_mix, w_qkv=v_w_qkv, q_norm_a=v_q_norm_a, k_norm_a=v_k_norm_a, q_norm_b=v_q_norm_b,
               k_norm_b=v_k_norm_b, sink_b=v_sink_b, w_o=v_w_o, norm_ffn2=v_norm_ffn2, ffn2_w_in=v_ffn2_w_in,
               ffn2_w_out=v_ffn2_w_out, norm_ple=v_norm_ple, w_ple_gate=v_w_ple_gate, w_ple_proj=v_w_ple_proj)
    sm = {k: wts[k] for k in SMALL}
    p_dim = p.shape[-1]
    me = 4 * lax.axis_index("x") + 2 * lax.axis_index("y") + lax.axis_index("c")
    packed = []
    for i in range(2):
        a, b = _pack_layer(wts, i)
        packed.append([a.reshape(-1, a.shape[-1]).astype(BF16), b.astype(BF16)])
    a_shape = (2, ffn1_w_in.shape[1], ffn1_w_in.shape[2])

    def weights_of(zones):
        return _layer_weights(zones[0].reshape((N_DEV,) + a_shape), zones[1], p_dim)

    w0 = weights_of([_all_gather(t) for t in packed[0]])
    zone_shapes = [(N_DEV,) + t.shape for t in packed[1]]
    ssem, rsem, thru, zones, token = _exchange_start(packed[1], zone_shapes, False, "gather_start")
    biases = _bias_matrices(rel_bias)
    x1, _, sv0 = _layer_fwd(x[0], p[0, 0], w0, sm, 0, None, TM, biases, dep=token)
    zones = _exchange_wait(ssem, rsem, thru, zones, x1, False, "gather_wait")
    w1 = weights_of([lax.dynamic_update_index_in_dim(z, t, me, 0) for z, t in zip(zones, packed[1])])
    dy, loss, sv1 = _layer_fwd(x1, p[1, 0], w1, sm, 1, loss_target[0], TM, biases)

    def slots_for(arrs):
        return [(N_DEV - 1,) + t.shape[1:] for t in arrs]

    dx1, groups1, gs1 = _layer_bwd(dy, w1, sm, 1, sv1, TM)
    g1 = groups1[0] + groups1[1] + groups1[2]
    ex1 = _exchange_start(g1, slots_for(g1), True, "scatter_start")
    held = {}

    def on_ready(stage, group):
        if stage == 1:
            held["slots1"] = _exchange_wait(*ex1[:4], group[0], True, "scatter_wait")
        held[stage] = _exchange_start(group, slots_for(group), True, f"scatter_start_{stage}")
        return held[stage][4]

    def on_small(gs0):
        gsmall = _stack_small({k: {0: gs0[k], 1: gs1[k]} for k in gs0})
        held["small"] = _all_reduce_small(_pack_small(gsmall, loss[0, :1]))
        return held["small"]

    dx, groups0, _ = _layer_bwd(dx1, w0, sm, 0, sv0, TM, dep=ex1[4], on_ready=on_ready, on_small=on_small)
    last = groups0[2]
    slots0 = [_exchange_wait(*held[stage][:4], last[0], True, f"scatter_wait_{stage}") for stage in (0, 1)]

    ex_last = _exchange_start(last, slots_for(last), True, "scatter_start_2")

    def summed(arrs, slots, tiles, dep=None):
        return [_sum_parts(lax.dynamic_index_in_dim(t, me, 0, keepdims=False), s_, tr, dep)
                for t, s_, tr in zip(arrs, slots, tiles)]

    r1 = summed(g1, held["slots1"], SUM_TILES, ex_last[4])
    r0 = summed(groups0[0], slots0[0], SUM_TILES[:2]) + summed(groups0[1], slots0[1], SUM_TILES[2:3])

    small_sum, loss_sum = _unpack_small(held["small"], sm)

    def update(names, layers):
        for k in names:
            grads[k] = jnp.stack([layers[0][k], layers[1][k]])
            delta[k], new_m[k], new_v[k] = _adamw(wts[k], grads[k], mom[k], var[k])

    grads, delta, new_m, new_v = dict(small_sum), {}, {}, {}
    layer1 = _unpack_layer(r1, wts)
    update([k for k in BIG if k not in LAST_GROUP], [_unpack_layer(r0 + [None, None], wts), layer1])
    zeros = {k: jnp.zeros_like(wts[k]) for k in SMALL}
    ds, ms, vs = _adamw(_pack_small(wts), _pack_small(small_sum), _pack_small(mom), _pack_small(var))
    for packed, dst in ((ds, delta), (ms, new_m), (vs, new_v)):
        dst.update(_unpack_small(packed, zeros)[0])
    slots_last = _exchange_wait(*ex_last[:4], ds, True, "scatter_wait_2")
    update(LAST_GROUP, [_unpack_layer([None, None, None] + summed(last, slots_last, SUM_TILES[3:]), wts), layer1])

    return (loss_sum, dx[None], *[grads[k] for k in WEIGHTS], *[delta[k] for k in WEIGHTS],
            *[new_m[k] for k in WEIGHTS], *[new_v[k] for k in WEIGHTS])
```

```python
import functools
import math

import jax
import jax.numpy as jnp
from jax import lax
from jax.experimental import pallas as pl
from jax.experimental.pallas import tpu as pltpu

F32 = jnp.float32
BF16 = jnp.bfloat16

N_DEV = 8
HEAD_DIM = 64
PAIR = 2 * HEAD_DIM
BQ = 128
N_BUCKETS = 32
MAX_DISTANCE = 1024
DILATED = ((64, 1), (64, 4), (64, 16))
SWA_RADIUS = 128
EPS = 1e-6
NEG = -1e30
ADAM_LR, ADAM_B1, ADAM_B2, ADAM_EPS, ADAM_WD, ADAM_STEP = 0.001, 0.9, 0.999, 1e-08, 0.01, 10
VMEM_LIMIT = 56 * 1024 * 1024
AXES = ("x", "y", "c")
MESH = pl.DeviceIdType.MESH

BIG = ("ffn1_w_in", "ffn1_w_out", "w_qkv", "w_o", "ffn2_w_in", "ffn2_w_out", "w_ple_gate", "w_ple_proj")
SMALL = ("rel_bias", "norm_ffn1", "norm_mix", "q_norm_a", "k_norm_a", "q_norm_b", "k_norm_b", "sink_b",
         "norm_ffn2", "norm_ple")
WEIGHTS = ("rel_bias", "norm_ffn1", "ffn1_w_in", "ffn1_w_out", "norm_mix", "w_qkv", "q_norm_a", "k_norm_a",
           "q_norm_b", "k_norm_b", "sink_b", "w_o", "norm_ffn2", "ffn2_w_in", "ffn2_w_out", "norm_ple",
           "w_ple_gate", "w_ple_proj")
SMALL_ROWS = 96


def _params(*sem):
    return pltpu.CompilerParams(dimension_semantics=sem, vmem_limit_bytes=VMEM_LIMIT)


def _dot(a, b):
    return jnp.dot(a, b, preferred_element_type=F32)


def _dot_nt(a, b):
    return lax.dot_general(a, b, (((1,), (1,)), ((), ())), preferred_element_type=F32)


def _dot_tn(a, b):
    return lax.dot_general(a, b, (((0,), (0,)), ((), ())), preferred_element_type=F32)


def _sigmoid(x):
    return 1.0 / (1.0 + jnp.exp(-x))


def _rstd(xv):
    return lax.rsqrt(jnp.mean(xv * xv, axis=-1, keepdims=True) + EPS)


def _norm_bwd(dh, xv, gv):
    r = _rstd(xv)
    xn = xv * r
    dg = jnp.sum(dh * xn, axis=0, keepdims=True)
    dxn = dh * gv
    dx = r * (dxn - xn * jnp.mean(dxn * xn, axis=-1, keepdims=True))
    return dx, dg


def _lo_mask(shape):
    return lax.broadcasted_iota(jnp.int32, shape, len(shape) - 1) < HEAD_DIM


def _half_sum(t, lo):
    s0 = jnp.sum(jnp.where(lo, t, 0.0), axis=1, keepdims=True)
    s1 = jnp.sum(jnp.where(lo, 0.0, t), axis=1, keepdims=True)
    return jnp.where(lo, s0, s1)


FFN_PARTS = 2


def _ffn_weight_specs(f, nj, D, C):
    return [pl.BlockSpec((None, None, D, C), lambda i, j: (j, f, 0, 0)),
            pl.BlockSpec((None, None, D, C), lambda i, j: (j + nj, f, 0, 0)),
            pl.BlockSpec((2, C // 2, D), lambda i, j: (j, f, 0))]


def _with_dep(body, dep, in_specs, args):
    if dep is None:
        return body, in_specs, args

    def body_after(dep_ref, *refs):
        body(*refs)

    return body_after, [pl.BlockSpec(memory_space=pl.ANY)] + in_specs, [dep] + args


def _ffn_fwd(x, g, ga, gb, f, tm, dep=None):
    T, D = x.shape
    nj, C = ga.shape[0] // 2, ga.shape[3]

    def body(x_ref, g_ref, wg_ref, wu_ref, wo_ref, xo_ref, h_ref, zg_ref, zu_ref, s_ref, h_scr, acc):
        j = pl.program_id(1)

        @pl.when(j == 0)
        def _():
            xv = x_ref[...]
            hb = (xv * _rstd(xv) * g_ref[...]).astype(BF16)
            h_scr[...] = hb
            h_ref[...] = hb
            acc[...] = jnp.zeros_like(acc)

        wo = wo_ref[...].reshape(C, D)
        for part in range(FFN_PARTS):
            sl = pl.ds(part * (tm // FFN_PARTS), tm // FFN_PARTS)
            hb = h_scr[sl, :]
            gt = _dot(hb, wg_ref[...])
            up = _dot(hb, wu_ref[...])
            s = (gt * _sigmoid(gt) * up).astype(BF16)
            zg_ref[sl, :] = gt.astype(BF16)
            zu_ref[sl, :] = up.astype(BF16)
            s_ref[sl, :] = s
            acc[sl, :] += _dot(s, wo)

        @pl.when(j == nj - 1)
        def _():
            xo_ref[...] = x_ref[...] + 0.5 * acc[...]

    tok = pl.BlockSpec((tm, D), lambda i, j: (i, 0))
    chunk = pl.BlockSpec((None, tm, C), lambda i, j: (j, i, 0))
    in_specs = [tok, pl.BlockSpec((1, D), lambda i, j: (0, 0))] + _ffn_weight_specs(f, nj, D, C)
    body, in_specs, args = _with_dep(body, dep, in_specs, [x, g, ga, ga, gb])
    return pl.pallas_call(
        body, name="ffn_fwd", grid=(T // tm, nj),
        in_specs=in_specs,
        out_specs=[tok, tok, chunk, chunk, chunk],
        out_shape=[jax.ShapeDtypeStruct((T, D), F32), jax.ShapeDtypeStruct((T, D), BF16),
                   jax.ShapeDtypeStruct((nj, T, C), BF16), jax.ShapeDtypeStruct((nj, T, C), BF16),
                   jax.ShapeDtypeStruct((nj, T, C), BF16)],
        scratch_shapes=[pltpu.VMEM((tm, D), BF16), pltpu.VMEM((tm, D), F32)],
        compiler_params=_params("parallel", "arbitrary"),
    )(*args)


def _ffn_bwd(dxo, x, g, zg, zu, ga, gb, f, tm, dep=None):
    T, D = x.shape
    nj, C = ga.shape[0] // 2, ga.shape[3]

    def body(dxo_ref, x_ref, g_ref, zg_ref, zu_ref, wg_ref, wu_ref, wo_ref,
             dx_ref, dy_ref, dzg_ref, dzu_ref, dgn_ref, dy_scr, acc):
        i, j = pl.program_id(0), pl.program_id(1)

        @pl.when(j == 0)
        def _():
            dyb = (0.5 * dxo_ref[...]).astype(BF16)
            dy_scr[...] = dyb
            dy_ref[...] = dyb
            acc[...] = jnp.zeros_like(acc)

        wo = wo_ref[...].reshape(C, D)
        for part in range(FFN_PARTS):
            sl = pl.ds(part * (tm // FFN_PARTS), tm // FFN_PARTS)
            ds = _dot_nt(dy_scr[sl, :], wo)
            gt = zg_ref[sl, :].astype(F32)
            up = zu_ref[sl, :].astype(F32)
            sg = _sigmoid(gt)
            dgt = (ds * up * (sg * (1.0 + gt * (1.0 - sg)))).astype(BF16)
            dup = (ds * (gt * sg)).astype(BF16)
            dzg_ref[sl, :] = dgt
            dzu_ref[sl, :] = dup
            acc[sl, :] += _dot_nt(dgt, wg_ref[...]) + _dot_nt(dup, wu_ref[...])

        @pl.when(j == nj - 1)
        def _():
            dx, dg = _norm_bwd(acc[...], x_ref[...], g_ref[...])
            dx_ref[...] = dxo_ref[...] + dx

            @pl.when(i == 0)
            def _():
                dgn_ref[...] = dg

            @pl.when(i > 0)
            def _():
                dgn_ref[...] += dg

    tok = pl.BlockSpec((tm, D), lambda i, j: (i, 0))
    chunk = pl.BlockSpec((None, tm, C), lambda i, j: (j, i, 0))
    row = pl.BlockSpec((1, D), lambda i, j: (0, 0))
    in_specs = [tok, tok, row, chunk, chunk] + _ffn_weight_specs(f, nj, D, C)
    body, in_specs, args = _with_dep(body, dep, in_specs, [dxo, x, g, zg, zu, ga, ga, gb])
    return pl.pallas_call(
        body, name="ffn_bwd", grid=(T // tm, nj),
        in_specs=in_specs,
        out_specs=[tok, tok, chunk, chunk, row],
        out_shape=[jax.ShapeDtypeStruct((T, D), F32), jax.ShapeDtypeStruct((T, D), BF16),
                   jax.ShapeDtypeStruct((nj, T, C), BF16), jax.ShapeDtypeStruct((nj, T, C), BF16),
                   jax.ShapeDtypeStruct((1, D), F32)],
        scratch_shapes=[pltpu.VMEM((tm, D), BF16), pltpu.VMEM((tm, D), F32)],
        compiler_params=_params("arbitrary", "arbitrary"),
    )(*args)


def _ffn_bwd_dz(dxo, zg, zu, gb, f, tm, dep=None):
    T, D = dxo.shape
    nj, C = zg.shape[0], zg.shape[2]

    def body(dxo_ref, zg_ref, zu_ref, wo_ref, dy_ref, dzg_ref, dzu_ref, dy_scr):
        @pl.when(pl.program_id(1) == 0)
        def _():
            dyb = (0.5 * dxo_ref[...]).astype(BF16)
            dy_scr[...] = dyb
            dy_ref[...] = dyb

        wo = wo_ref[...].reshape(C, D)
        for part in range(FFN_PARTS):
            sl = pl.ds(part * (tm // FFN_PARTS), tm // FFN_PARTS)
            ds = _dot_nt(dy_scr[sl, :], wo)
            gt = zg_ref[sl, :].astype(F32)
            up = zu_ref[sl, :].astype(F32)
            sg = _sigmoid(gt)
            dzg_ref[sl, :] = (ds * up * (sg * (1.0 + gt * (1.0 - sg)))).astype(BF16)
            dzu_ref[sl, :] = (ds * (gt * sg)).astype(BF16)

    tok = pl.BlockSpec((tm, D), lambda i, j: (i, 0))
    chunk = pl.BlockSpec((None, tm, C), lambda i, j: (j, i, 0))
    in_specs = [tok, chunk, chunk, _ffn_weight_specs(f, nj, D, C)[2]]
    body, in_specs, args = _with_dep(body, dep, in_specs, [dxo, zg, zu, gb])
    return pl.pallas_call(
        body, name="ffn_bwd_dz", grid=(T // tm, nj),
        in_specs=in_specs, out_specs=[tok, chunk, chunk],
        out_shape=[jax.ShapeDtypeStruct((T, D), BF16), jax.ShapeDtypeStruct((nj, T, C), BF16),
                   jax.ShapeDtypeStruct((nj, T, C), BF16)],
        scratch_shapes=[pltpu.VMEM((tm, D), BF16)],
        compiler_params=_params("parallel", "arbitrary"),
    )(*args)


def _ffn_bwd_dx(dxo, x, g, dzg, dzu, ga, f, tm, dep=None):
    T, D = x.shape
    nj, C = ga.shape[0] // 2, ga.shape[3]

    def body(dxo_ref, x_ref, g_ref, dzg_ref, dzu_ref, wg_ref, wu_ref, dx_ref, dgn_ref, acc):
        i, j = pl.program_id(0), pl.program_id(1)

        @pl.when(j == 0)
        def _():
            acc[...] = jnp.zeros_like(acc)

        acc[...] += _dot_nt(dzg_ref[...], wg_ref[...]) + _dot_nt(dzu_ref[...], wu_ref[...])

        @pl.when(j == nj - 1)
        def _():
            dx, dg = _norm_bwd(acc[...], x_ref[...], g_ref[...])
            dx_ref[...] = dxo_ref[...] + dx

            @pl.when(i == 0)
            def _():
                dgn_ref[...] = dg

            @pl.when(i > 0)
            def _():
                dgn_ref[...] += dg

    tok = pl.BlockSpec((tm, D), lambda i, j: (i, 0))
    chunk = pl.BlockSpec((None, tm, C), lambda i, j: (j, i, 0))
    row = pl.BlockSpec((1, D), lambda i, j: (0, 0))
    in_specs = [tok, tok, row, chunk, chunk] + _ffn_weight_specs(f, nj, D, C)[:2]
    body, in_specs, args = _with_dep(body, dep, in_specs, [dxo, x, g, dzg, dzu, ga, ga])
    return pl.pallas_call(
        body, name="ffn_bwd_dx", grid=(T // tm, nj),
        in_specs=in_specs, out_specs=[tok, row],
        out_shape=[jax.ShapeDtypeStruct((T, D), F32), jax.ShapeDtypeStruct((1, D), F32)],
        scratch_shapes=[pltpu.VMEM((tm, D), F32)],
        compiler_params=_params("arbitrary", "arbitrary"),
    )(*args)


def _ffn_dw(h, dzg, dzu, s, dy, tk, dep=None):
    T, D = h.shape
    nj, C = s.shape[0], s.shape[2]
    nk = T // tk

    def body(h_ref, dzg_ref, dzu_ref, s_ref, dy_ref, dwin_ref, dwo_ref, ag, au, ao):
        k = pl.program_id(1)

        @pl.when(k == 0)
        def _():
            ag[...] = jnp.zeros_like(ag)
            au[...] = jnp.zeros_like(au)
            ao[...] = jnp.zeros_like(ao)

        hb = h_ref[...]
        ag[...] += _dot_tn(hb, dzg_ref[...])
        au[...] += _dot_tn(hb, dzu_ref[...])
        ao[...] += _dot_tn(s_ref[...], dy_ref[...])

        @pl.when(k == nk - 1)
        def _():
            dwin_ref[0] = ag[...].astype(BF16)
            dwin_ref[1] = au[...].astype(BF16)
            dwo_ref[...] = ao[...].astype(BF16)

    tok = pl.BlockSpec((tk, D), lambda j, k: (k, 0))
    chunk = pl.BlockSpec((None, tk, C), lambda j, k: (j, k, 0))
    body, in_specs, args = _with_dep(body, dep, [tok, chunk, chunk, chunk, tok], [h, dzg, dzu, s, dy])
    dwin, dwo = pl.pallas_call(
        body, name="ffn_dw", grid=(nj, nk),
        in_specs=in_specs,
        out_specs=[pl.BlockSpec((2, None, D, C), lambda j, k: (0, j, 0, 0)),
                   pl.BlockSpec((None, C, D), lambda j, k: (j, 0, 0))],
        out_shape=[jax.ShapeDtypeStruct((2, nj, D, C), BF16), jax.ShapeDtypeStruct((nj, C, D), BF16)],
        scratch_shapes=[pltpu.VMEM((D, C), F32), pltpu.VMEM((D, C), F32), pltpu.VMEM((C, D), F32)],
        compiler_params=_params("parallel", "arbitrary"),
    )(*args)
    return dwin.reshape(2 * nj, D, C), dwo


def _matmul_tn(a, b, tn, tk):
    T, Ka = a.shape
    N = b.shape[1]
    nk = T // tk

    def body(a_ref, b_ref, o_ref, acc):
        k = pl.program_id(1)

        @pl.when(k == 0)
        def _():
            acc[...] = jnp.zeros_like(acc)

        acc[...] += _dot_tn(a_ref[...], b_ref[...])

        @pl.when(k == nk - 1)
        def _():
            o_ref[...] = acc[...].astype(BF16)

    return pl.pallas_call(
        body, name="matmul_tn", grid=(N // tn, nk),
        in_specs=[pl.BlockSpec((tk, Ka), lambda n, k: (k, 0)), pl.BlockSpec((tk, tn), lambda n, k: (k, n))],
        out_specs=pl.BlockSpec((Ka, tn), lambda n, k: (0, n)),
        out_shape=jax.ShapeDtypeStruct((Ka, N), BF16),
        scratch_shapes=[pltpu.VMEM((Ka, tn), F32)],
        compiler_params=_params("parallel", "arbitrary"),
    )(a, b)


def _qkv_fwd(x, g, w, tm):
    T, D = x.shape
    N = w.shape[1]

    def body(x_ref, g_ref, w_ref, o_ref, h_ref):
        xv = x_ref[...]
        hb = (xv * _rstd(xv) * g_ref[...]).astype(BF16)
        h_ref[...] = hb
        o_ref[...] = _dot(hb, w_ref[...])

    return pl.pallas_call(
        body, name="qkv_fwd", grid=(T // tm,),
        in_specs=[pl.BlockSpec((tm, D), lambda i: (i, 0)), pl.BlockSpec((1, D), lambda i: (0, 0)),
                  pl.BlockSpec((D, N), lambda i: (0, 0))],
        out_specs=[pl.BlockSpec((tm, N), lambda i: (i, 0)), pl.BlockSpec((tm, D), lambda i: (i, 0))],
        out_shape=[jax.ShapeDtypeStruct((T, N), F32), jax.ShapeDtypeStruct((T, D), BF16)],
        compiler_params=_params("parallel"),
    )(x, g, w)


def _attn_prep(qkv, gains2, tm):
    T = qkv.shape[0]
    scale = HEAD_DIM ** -0.5

    def body(qkv_ref, g_ref, qa_ref, ka_ref, va_ref, qb_ref, kb_ref, vb_ref):
        lo = _lo_mask((tm, PAIR))

        def normed(c, gi, mult):
            xv = qkv_ref[:, c * PAIR:(c + 1) * PAIR]
            r = lax.rsqrt(_half_sum(xv * xv, lo) * (1.0 / HEAD_DIM) + EPS)
            y = xv * r * g_ref[gi:gi + 1, :]
            return y * mult if mult != 1.0 else y

        def both_halves(v):
            sw = pltpu.roll(v, HEAD_DIM, 1)
            return jnp.where(lo, v, sw), jnp.where(lo, sw, v)

        for c in range(4):
            qa_ref[c] = normed(c, 0, scale).astype(BF16)
            ka_ref[c] = normed(4 + c, 1, 1.0).astype(BF16)
            va_ref[c] = qkv_ref[:, (8 + c) * PAIR:(9 + c) * PAIR].astype(BF16)
            qb_ref[c] = normed(12 + c, 2, scale).astype(BF16)
        k0, k1 = both_halves(normed(16, 3, 1.0))
        kb_ref[0] = k0.astype(BF16)
        kb_ref[1] = k1.astype(BF16)
        v0, v1 = both_halves(qkv_ref[:, 17 * PAIR:18 * PAIR])
        vb_ref[0] = v0.astype(BF16)
        vb_ref[1] = v1.astype(BF16)

    four = pl.BlockSpec((4, tm, PAIR), lambda i: (0, i, 0))
    two = pl.BlockSpec((2, tm, PAIR), lambda i: (0, i, 0))
    s4 = jax.ShapeDtypeStruct((4, T, PAIR), BF16)
    s2 = jax.ShapeDtypeStruct((2, T, PAIR), BF16)
    return pl.pallas_call(
        body, name="attn_prep", grid=(T // tm,),
        in_specs=[pl.BlockSpec((tm, qkv.shape[1]), lambda i: (i, 0)), pl.BlockSpec((4, PAIR), lambda i: (0, 0))],
        out_specs=[four, four, four, four, two, two],
        out_shape=[s4, s4, s4, s4, s2, s2],
        compiler_params=_params("parallel"),
    )(qkv, gains2)


def _loop_blocks(nb, body, init, per_iter):
    u = math.gcd(nb, per_iter)

    def outer(i, carry):
        for k in range(u):
            carry = body(i * u + k, carry)
        return carry

    return lax.fori_loop(0, nb // u, outer, init)


def _edge_variant(b, nb):
    return (b == 0).astype(jnp.int32) + 2 * (b == nb - 1).astype(jnp.int32)


def _stack_heads(v, lo):
    z = jnp.zeros_like(v)
    return jnp.concatenate([jnp.where(lo, v, z), jnp.where(lo, z, v)], axis=0)


def _unstack_heads(v2, lo):
    return jnp.where(lo, v2[:BQ], v2[BQ:])


def _row_vector(v, lo):
    r = lax.broadcasted_iota(jnp.int32, (BQ, PAIR), 0)
    ln = lax.broadcasted_iota(jnp.int32, (BQ, PAIR), 1)
    diag = (ln % HEAD_DIM) == (r % HEAD_DIM)
    top = jnp.sum(jnp.where(diag & (r < HEAD_DIM), v, 0.0), axis=0, keepdims=True)
    bot = jnp.sum(jnp.where(diag & (r >= HEAD_DIM), v, 0.0), axis=0, keepdims=True)
    top8, bot8 = jnp.broadcast_to(top, (8, PAIR)), jnp.broadcast_to(bot, (8, PAIR))
    lo8 = _lo_mask((8, PAIR))
    head0 = jnp.where(lo8, top8, pltpu.roll(bot8, HEAD_DIM, 1))
    head1 = jnp.where(lo8, pltpu.roll(top8, HEAD_DIM, 1), bot8)
    return jnp.concatenate([head0, head1], axis=1)[:1]


def _units_per_step(nb, pairs_per_kv):
    return max(1, 16 // nb) if pairs_per_kv == 1 else 1


def _attn_fwd(q, kp, vp, bias4, sink, R, pairs_per_kv, pairs_per_bias):
    N, L, _ = q.shape
    W = BQ + 2 * R
    nb = L // BQ
    G = _units_per_step(nb, pairs_per_kv)

    def body(sink_ref, q_ref, k_ref, v_ref, bias_ref, o_ref, lse_ref):
        n = pl.program_id(0)
        lo_q = _lo_mask((BQ, PAIR))
        first = lax.broadcasted_iota(jnp.int32, (2 * BQ, 1), 0) < BQ

        def blk(f, carry):
            g, b = f // nb, f % nb
            u = n * G + g
            sk = jnp.where(first, sink_ref[2 * u], sink_ref[2 * u + 1])
            q0 = pl.multiple_of(b * BQ, BQ)
            q2 = _stack_heads(q_ref[g, pl.ds(q0, BQ), :], lo_q)
            kw = k_ref[g, pl.ds(q0, W), :]
            vw = v_ref[g, pl.ds(q0, W), :]
            s = _dot_nt(q2, kw) + bias_ref[_edge_variant(b, nb)]
            m = jnp.maximum(jnp.max(s, axis=1, keepdims=True), sk)
            p = jnp.exp(s - m)
            l = jnp.sum(p, axis=1, keepdims=True) + jnp.exp(sk - m)
            o2 = _dot(p.astype(BF16), vw) / l
            o_ref[g, pl.ds(q0, BQ), :] = _unstack_heads(o2, lo_q)
            lse_ref[g, pl.ds(q0, BQ), :] = _unstack_heads(jnp.broadcast_to(m + jnp.log(l), (2 * BQ, PAIR)), lo_q)
            return carry

        _loop_blocks(G * nb, blk, 0, 4)

    qspec = pl.BlockSpec((G, L, PAIR), lambda n: (n, 0, 0))
    kspec = pl.BlockSpec((G, L + 2 * R, PAIR), lambda n: (n // pairs_per_kv, 0, 0))
    return pl.pallas_call(
        body, name="attn_fwd", grid=(N // G,),
        in_specs=[pl.BlockSpec(memory_space=pltpu.SMEM), qspec, kspec, kspec,
                  pl.BlockSpec((None, 4, 2 * BQ, W), lambda n: (n * G // pairs_per_bias, 0, 0, 0))],
        out_specs=[qspec, qspec],
        out_shape=[jax.ShapeDtypeStruct((N, L, PAIR), F32), jax.ShapeDtypeStruct((N, L, PAIR), F32)],
        compiler_params=_params("parallel"),
    )(sink, q, kp, vp, bias4)


def _attn_bwd(q, kp, vp, bias4t, sink, o, lse, do, R, pairs_per_kv, pairs_per_bias):
    N, L, _ = q.shape
    Nk = kp.shape[0]
    Pb = bias4t.shape[0]
    W = BQ + 2 * R
    nb = L // BQ
    G = _units_per_step(nb, pairs_per_kv)

    def body(sink_ref, q_ref, k_ref, v_ref, bias_ref, o_ref, lse_ref, do_ref,
             dq_ref, dk_ref, dv_ref, dbias_ref, dsink_ref):
        n = pl.program_id(0)
        lo_q = _lo_mask((BQ, PAIR))
        first = lax.broadcasted_iota(jnp.int32, (1, 2 * BQ), 1) < BQ
        dsink_ref[...] = jnp.zeros_like(dsink_ref)

        @pl.when(n % pairs_per_kv == 0)
        def _():
            dk_ref[...] = jnp.zeros_like(dk_ref)
            dv_ref[...] = jnp.zeros_like(dv_ref)

        @pl.when((n * G) % pairs_per_bias == 0)
        def _():
            dbias_ref[...] = jnp.zeros_like(dbias_ref)

        def blk(f, carry):
            g, b = f // nb, f % nb
            u = n * G + g
            sk = jnp.where(first, sink_ref[2 * u], sink_ref[2 * u + 1])
            q0 = pl.multiple_of(b * BQ, BQ)
            q2 = _stack_heads(q_ref[g, pl.ds(q0, BQ), :], lo_q)
            kw = k_ref[g, pl.ds(q0, W), :]
            vw = v_ref[g, pl.ds(q0, W), :]
            dov = do_ref[g, pl.ds(q0, BQ), :]
            lse = _row_vector(lse_ref[g, pl.ds(q0, BQ), :], lo_q)
            delta = _row_vector(_half_sum(dov * o_ref[g, pl.ds(q0, BQ), :], lo_q), lo_q)
            do2 = _stack_heads(dov.astype(BF16), lo_q)
            st = _dot_nt(kw, q2) + bias_ref[_edge_variant(b, nb)]
            pt = jnp.exp(st - lse)
            dst = pt * (_dot_nt(vw, do2) - delta)
            dstb = dst.astype(BF16)
            dbias_ref[...] += dst
            dk_ref[g, pl.ds(q0, W), :] += _dot(dstb, q2)
            dv_ref[g, pl.ds(q0, W), :] += _dot(pt.astype(BF16), do2)
            dq_ref[g, pl.ds(q0, BQ), :] = _unstack_heads(_dot_tn(dstb, kw), lo_q)
            dsink_ref[g, pl.ds(0, 1), :] -= jnp.exp(sk - lse) * delta
            return carry

        _loop_blocks(G * nb, blk, 0, 4)

    qspec = pl.BlockSpec((G, L, PAIR), lambda n: (n, 0, 0))
    kspec = pl.BlockSpec((G, L + 2 * R, PAIR), lambda n: (n // pairs_per_kv, 0, 0))
    return pl.pallas_call(
        body, name="attn_bwd", grid=(N // G,),
        in_specs=[pl.BlockSpec(memory_space=pltpu.SMEM), qspec, kspec, kspec,
                  pl.BlockSpec((None, 4, W, 2 * BQ), lambda n: (n * G // pairs_per_bias, 0, 0, 0)),
                  qspec, qspec, qspec],
        out_specs=[qspec, kspec, kspec,
                   pl.BlockSpec((None, W, 2 * BQ), lambda n: (n * G // pairs_per_bias, 0, 0)),
                   pl.BlockSpec((G, 8, 2 * BQ), lambda n: (n, 0, 0))],
        out_shape=[jax.ShapeDtypeStruct((N, L, PAIR), F32),
                   jax.ShapeDtypeStruct((Nk, L + 2 * R, PAIR), F32),
                   jax.ShapeDtypeStruct((Nk, L + 2 * R, PAIR), F32),
                   jax.ShapeDtypeStruct((Pb, W, 2 * BQ), F32),
                   jax.ShapeDtypeStruct((N, 8, 2 * BQ), F32)],
        compiler_params=_params("arbitrary"),
    )(sink, q, kp, vp, bias4t, o, lse, do)


def _attn_merge(o1, l1, o4, l4, o16, l16, ob, tm):
    T = o1.shape[1]

    def body(o1_ref, l1_ref, o4_ref, l4_ref, o16_ref, l16_ref, ob_ref, oa_ref, la_ref, cat_ref):
        for c in range(4):
            a, b, d = l1_ref[c], l4_ref[c], l16_ref[c]
            m = jnp.maximum(jnp.maximum(a, b), d)
            wa, wb, wd = jnp.exp(a - m), jnp.exp(b - m), jnp.exp(d - m)
            z = wa + wb + wd
            o = (wa * o1_ref[c] + wb * o4_ref[c] + wd * o16_ref[c]) / z
            oa_ref[c] = o
            la_ref[c] = m + jnp.log(z)
            cat_ref[:, c * PAIR:(c + 1) * PAIR] = o.astype(BF16)
            cat_ref[:, (4 + c) * PAIR:(5 + c) * PAIR] = ob_ref[c].astype(BF16)

    four = pl.BlockSpec((4, tm, PAIR), lambda i: (0, i, 0))
    s4 = jax.ShapeDtypeStruct((4, T, PAIR), F32)
    return pl.pallas_call(
        body, name="attn_merge", grid=(T // tm,),
        in_specs=[four] * 7,
        out_specs=[four, four, pl.BlockSpec((tm, 8 * PAIR), lambda i: (i, 0))],
        out_shape=[s4, s4, jax.ShapeDtypeStruct((T, 8 * PAIR), BF16)],
        compiler_params=_params("parallel"),
    )(o1, l1, o4, l4, o16, l16, ob)


def _weight_arg(w, blk):
    if blk is None:
        return pl.BlockSpec(w.shape, lambda i: (0, 0)), (lambda ref: ref[...])
    D = w.shape[2]
    return (pl.BlockSpec((N_DEV, 128, D), lambda i: (0, blk, 0)),
            lambda ref: ref[...].reshape(N_DEV * 128, D))


def _oproj_fwd(x, o_cat, w, blk, tm):
    T, D = x.shape
    wspec, wload = _weight_arg(w, blk)

    def body(x_ref, o_ref, w_ref, out_ref):
        out_ref[...] = x_ref[...] + _dot(o_ref[...], wload(w_ref))

    tok = pl.BlockSpec((tm, D), lambda i: (i, 0))
    return pl.pallas_call(
        body, name="oproj_fwd", grid=(T // tm,),
        in_specs=[tok, pl.BlockSpec((tm, o_cat.shape[1]), lambda i: (i, 0)), wspec],
        out_specs=tok, out_shape=jax.ShapeDtypeStruct((T, D), F32),
        compiler_params=_params("parallel"),
    )(x, o_cat, w)


def _oproj_bwd(dx, w, blk, tm, dep=None):
    T, D = dx.shape
    wspec, wload = _weight_arg(w, blk)

    def body(dx_ref, w_ref, dxb_ref, do_ref):
        db = dx_ref[...].astype(BF16)
        dxb_ref[...] = db
        do = _dot_nt(db, wload(w_ref))
        for c in range(8):
            do_ref[c] = do[:, c * PAIR:(c + 1) * PAIR]

    tok = pl.BlockSpec((tm, D), lambda i: (i, 0))
    body, in_specs, args = _with_dep(body, dep, [tok, wspec], [dx, w])
    return pl.pallas_call(
        body, name="oproj_bwd", grid=(T // tm,),
        in_specs=in_specs,
        out_specs=[tok, pl.BlockSpec((8, tm, PAIR), lambda i: (0, i, 0))],
        out_shape=[jax.ShapeDtypeStruct((T, D), BF16), jax.ShapeDtypeStruct((8, T, PAIR), F32)],
        compiler_params=_params("parallel"),
    )(*args)


def _attn_post(qkv, gains2, dqa, dka, dva, dqb, dkb, dvb, tm):
    T, NQ = qkv.shape
    scale = HEAD_DIM ** -0.5

    def body(qkv_ref, g_ref, qa1, qa4, qa16, ka1, ka4, ka16, va1, va4, va16, qb_ref, kb_ref, vb_ref,
             out_ref, dg_ref):
        lo = _lo_mask((tm, PAIR))

        @pl.when(pl.program_id(0) == 0)
        def _():
            dg_ref[...] = jnp.zeros_like(dg_ref)

        def norm_bwd(c, gi, dy):
            xv = qkv_ref[:, c * PAIR:(c + 1) * PAIR]
            r = lax.rsqrt(_half_sum(xv * xv, lo) * (1.0 / HEAD_DIM) + EPS)
            xn = xv * r
            dg_ref[gi:gi + 1, :] += jnp.sum(dy * xn, axis=0, keepdims=True)
            dxn = dy * g_ref[gi:gi + 1, :]
            dx = r * (dxn - xn * (_half_sum(dxn * xn, lo) * (1.0 / HEAD_DIM)))
            out_ref[:, c * PAIR:(c + 1) * PAIR] = dx.astype(BF16)

        def fold(v):
            return v + pltpu.roll(v, HEAD_DIM, 1)

        for c in range(4):
            norm_bwd(c, 0, (qa1[c] + qa4[c] + qa16[c]) * scale)
            norm_bwd(4 + c, 1, ka1[c] + ka4[c] + ka16[c])
            out_ref[:, (8 + c) * PAIR:(9 + c) * PAIR] = (va1[c] + va4[c] + va16[c]).astype(BF16)
            norm_bwd(12 + c, 2, qb_ref[c] * scale)
        norm_bwd(16, 3, jnp.where(lo, fold(kb_ref[0]), fold(kb_ref[1])))
        out_ref[:, 17 * PAIR:18 * PAIR] = jnp.where(lo, fold(vb_ref[0]), fold(vb_ref[1])).astype(BF16)

    four = pl.BlockSpec((4, tm, PAIR), lambda i: (0, i, 0))
    two = pl.BlockSpec((2, tm, PAIR), lambda i: (0, i, 0))
    return pl.pallas_call(
        body, name="attn_post", grid=(T // tm,),
        in_specs=[pl.BlockSpec((tm, NQ), lambda i: (i, 0)), pl.BlockSpec((4, PAIR), lambda i: (0, 0))]
        + [four] * 10 + [two, two],
        out_specs=[pl.BlockSpec((tm, NQ), lambda i: (i, 0)), pl.BlockSpec((4, PAIR), lambda i: (0, 0))],
        out_shape=[jax.ShapeDtypeStruct((T, NQ), BF16), jax.ShapeDtypeStruct((4, PAIR), F32)],
        compiler_params=_params("arbitrary"),
    )(qkv, gains2, *dqa, *dka, *dva, dqb, dkb, dvb)


def _dense_norm_bwd(dres, dz, w, blk, x, g, tm):
    T, D = x.shape
    N = dz.shape[1]
    wspec, wload = _weight_arg(w, blk)

    def body(dres_ref, dz_ref, w_ref, x_ref, g_ref, dx_ref, dgn_ref):
        i = pl.program_id(0)
        dx, dg = _norm_bwd(_dot_nt(dz_ref[...], wload(w_ref)), x_ref[...], g_ref[...])
        dx_ref[...] = dres_ref[...] + dx

        @pl.when(i == 0)
        def _():
            dgn_ref[...] = dg

        @pl.when(i > 0)
        def _():
            dgn_ref[...] += dg

    tok = pl.BlockSpec((tm, D), lambda i: (i, 0))
    row = pl.BlockSpec((1, D), lambda i: (0, 0))
    return pl.pallas_call(
        body, name="dense_norm_bwd", grid=(T // tm,),
        in_specs=[tok, pl.BlockSpec((tm, N), lambda i: (i, 0)), wspec, tok, row],
        out_specs=[tok, row],
        out_shape=[jax.ShapeDtypeStruct((T, D), F32), jax.ShapeDtypeStruct((1, D), F32)],
        compiler_params=_params("arbitrary"),
    )(dres, dz, w, x, g)


def _bias_reduce(onehot, dbm):
    Hb, K = dbm.shape

    def body(oh_ref, d_ref, out_ref):
        oh = oh_ref[...]
        d = d_ref[...]
        hi = d.astype(BF16)
        r1 = d - hi.astype(F32)
        mid = r1.astype(BF16)
        low = (r1 - mid.astype(F32)).astype(BF16)
        out_ref[...] = _dot_nt(hi, oh) + _dot_nt(mid, oh) + _dot_nt(low, oh)

    vm = pl.BlockSpec(memory_space=pltpu.VMEM)
    return pl.pallas_call(
        body, name="bias_reduce", in_specs=[vm, vm], out_specs=vm,
        out_shape=jax.ShapeDtypeStruct((Hb, 128), F32),
        compiler_params=pltpu.CompilerParams(vmem_limit_bytes=VMEM_LIMIT),
    )(onehot, dbm)


def _ple_fwd(x, g, wg, blk, p, wp, target, tm):
    T, D = x.shape
    P = p.shape[1]
    with_loss = target is not None
    wspec, wload = _weight_arg(wg, blk)

    def body(*refs):
        if with_loss:
            x_ref, g_ref, wg_ref, p_ref, wp_ref, t_ref, y_ref, hn_ref, gate_ref, pp_ref, pb_ref, loss_ref = refs
        else:
            x_ref, g_ref, wg_ref, p_ref, wp_ref, y_ref, hn_ref, gate_ref, pp_ref, pb_ref = refs
        i = pl.program_id(0)
        xv = x_ref[...]
        hb = (xv * _rstd(xv) * g_ref[...]).astype(BF16)
        hn_ref[...] = hb
        gate = _sigmoid(_dot(hb, wload(wg_ref)))
        pb = p_ref[...].astype(BF16)
        pb_ref[...] = pb
        pp = _dot(pb, wp_ref[...])
        gate_ref[...] = gate
        pp_ref[...] = pp
        y = xv + gate * pp
        if with_loss:
            err = y - t_ref[...]
            y_ref[...] = err * (1.0 / D)
            part = jnp.broadcast_to(0.5 * jnp.sum(jnp.sum(err * err, axis=1, keepdims=True) * (1.0 / D),
                                                  axis=0, keepdims=True), (1, 128))

            @pl.when(i == 0)
            def _():
                loss_ref[...] = part

            @pl.when(i > 0)
            def _():
                loss_ref[...] += part
        else:
            y_ref[...] = y

    tok = pl.BlockSpec((tm, D), lambda i: (i, 0))
    ptok = pl.BlockSpec((tm, P), lambda i: (i, 0))
    in_specs = [tok, pl.BlockSpec((1, D), lambda i: (0, 0)), wspec, ptok,
                pl.BlockSpec((P, D), lambda i: (0, 0))]
    out_specs = [tok, tok, tok, tok, ptok]
    out_shape = [jax.ShapeDtypeStruct((T, D), F32), jax.ShapeDtypeStruct((T, D), BF16),
                 jax.ShapeDtypeStruct((T, D), F32), jax.ShapeDtypeStruct((T, D), F32),
                 jax.ShapeDtypeStruct((T, P), BF16)]
    args = [x, g, wg, p, wp]
    if with_loss:
        in_specs.append(tok)
        out_specs.append(pl.BlockSpec((1, 128), lambda i: (0, 0)))
        out_shape.append(jax.ShapeDtypeStruct((1, 128), F32))
        args.append(target)
    return pl.pallas_call(
        body, name="ple_fwd_loss" if with_loss else "ple_fwd", grid=(T // tm,),
        in_specs=in_specs, out_specs=out_specs, out_shape=out_shape,
        compiler_params=_params("arbitrary" if with_loss else "parallel"),
    )(*args)


def _ple_bwd(dy, gate, pp, tm, dep=None):
    T, D = dy.shape

    def body(dy_ref, gate_ref, pp_ref, dgl_ref, dpp_ref):
        d = dy_ref[...]
        gt = gate_ref[...]
        dgl_ref[...] = (d * pp_ref[...] * gt * (1.0 - gt)).astype(BF16)
        dpp_ref[...] = (d * gt).astype(BF16)

    tok = pl.BlockSpec((tm, D), lambda i: (i, 0))
    body, in_specs, args = _with_dep(body, dep, [tok, tok, tok], [dy, gate, pp])
    return pl.pallas_call(
        body, name="ple_bwd", grid=(T // tm,), in_specs=in_specs, out_specs=[tok, tok],
        out_shape=[jax.ShapeDtypeStruct((T, D), BF16), jax.ShapeDtypeStruct((T, D), BF16)],
        compiler_params=_params("parallel"),
    )(*args)


def _adamw(w, g, m, v):
    shape = w.shape
    C = shape[-1]
    w2, g2, m2, v2 = (a.reshape(-1, C) for a in (w, g, m, v))
    Rn = w2.shape[0]
    tr = Rn
    for cand in (512, 352, 256):
        if Rn % cand == 0:
            tr = cand
            break
    c1 = 1.0 - ADAM_B1 ** ADAM_STEP
    c2 = 1.0 - ADAM_B2 ** ADAM_STEP

    def body(w_ref, g_ref, m_ref, v_ref, d_ref, nm_ref, nv_ref):
        gv = g_ref[...]
        mn = ADAM_B1 * m_ref[...] + (1.0 - ADAM_B1) * gv
        vn = ADAM_B2 * v_ref[...] + (1.0 - ADAM_B2) * (gv * gv)
        d_ref[...] = -ADAM_LR * ((mn / c1) / (jnp.sqrt(vn / c2) + ADAM_EPS) + ADAM_WD * w_ref[...])
        nm_ref[...] = mn
        nv_ref[...] = vn

    spec = pl.BlockSpec((tr, C), lambda i: (i, 0))
    sh = jax.ShapeDtypeStruct((Rn, C), F32)
    d, nm, nv = pl.pallas_call(
        body, name="adamw", grid=(Rn // tr,), in_specs=[spec] * 4, out_specs=[spec] * 3, out_shape=[sh] * 3,
        compiler_params=_params("parallel"),
    )(w2, g2, m2, v2)
    return d.reshape(shape), nm.reshape(shape), nv.reshape(shape)


def _my_place():
    x, y, c = lax.axis_index("x"), lax.axis_index("y"), lax.axis_index("c")
    chips = [(1 - x, y), (x, 1 - y), (1 - x, 1 - y)]
    return x, y, c, chips


def _all_gather(flat):
    R, Wd = flat.shape

    def body(x_ref, out_ref, send_sems, recv_sems, local_sem):
        x, y, c, chips = _my_place()
        me, sibling = (x, y, c), (x, y, 1 - c)

        def rows(px, py, pc):
            return out_ref.at[4 * px + 2 * py + pc]

        def copy(k, block, to, src=None):
            return pltpu.make_async_remote_copy(
                src_ref=rows(*block) if src is None else src, dst_ref=rows(*block),
                send_sem=send_sems.at[k], recv_sem=recv_sems.at[k], device_id=to, device_id_type=MESH)

        mine = pltpu.make_async_copy(x_ref, rows(*me), local_sem)
        mine.start()
        first = [copy(0, me, sibling, src=x_ref)]
        first += [copy(1 + j, me, (*chip, c), src=x_ref) for j, chip in enumerate(chips)]
        for cp in first:
            cp.start()
        passed = [copy(4 + j, (*chip, c), sibling) for j, chip in enumerate(chips)]
        for j, chip in enumerate(chips):
            copy(1 + j, (*chip, c), me).wait_recv()
            passed[j].start()
        copy(0, sibling, me).wait_recv()
        for j, chip in enumerate(chips):
            copy(4 + j, (*chip, 1 - c), me).wait_recv()
        for cp in first + passed:
            cp.wait_send()
        mine.wait()

    return pl.pallas_call(
        body, name="all_gather",
        in_specs=[pl.BlockSpec(memory_space=pl.ANY)], out_specs=pl.BlockSpec(memory_space=pl.ANY),
        out_shape=jax.ShapeDtypeStruct((N_DEV, R, Wd), flat.dtype),
        scratch_shapes=[pltpu.SemaphoreType.DMA((7,)), pltpu.SemaphoreType.DMA((7,)), pltpu.SemaphoreType.DMA],
    )(flat)


def _reduce_scatter(gparts, tr):
    _, R, Wd = gparts.shape
    nt = R // tr

    def body(g_ref, out_ref, a_ref, p_ref, b_ref, vb, vo_b, vo_f, d2d_send, d2d_recv, ici_send, ici_recv):
        x, y, c, chips = _my_place()
        sibling = (x, y, 1 - c)
        allchips = [(x, y)] + chips

        def dev(chip, pc):
            return 4 * chip[0] + 2 * chip[1] + pc

        d2d = [pltpu.make_async_remote_copy(
            src_ref=g_ref.at[dev(q, 1 - c)], dst_ref=a_ref.at[a], send_sem=d2d_send.at[a], recv_sem=d2d_recv.at[a],
            device_id=sibling, device_id_type=MESH) for a, q in enumerate(allchips)]
        for cp in d2d:
            cp.start()

        def add_tiles(srcs, dst, vo):
            def step(t, carry):
                r = pl.ds(pl.multiple_of(t * tr, tr), tr)
                acc = None
                for s_i, src in enumerate(srcs):
                    pltpu.sync_copy(src.at[r], vb.at[s_i])
                for s_i in range(len(srcs)):
                    term = vb[s_i].astype(F32)
                    acc = term if acc is None else acc + term
                vo[...] = acc.astype(vo.dtype)
                pltpu.sync_copy(vo, dst.at[r])
                return carry

            lax.fori_loop(0, nt, step, 0)

        ici = []
        for j, q in enumerate(chips):
            d2d[j + 1].wait_recv()
            add_tiles([g_ref.at[dev(q, c)], a_ref.at[j + 1]], p_ref.at[j], vo_b)
            cp = pltpu.make_async_remote_copy(
                src_ref=p_ref.at[j], dst_ref=b_ref.at[j], send_sem=ici_send.at[j], recv_sem=ici_recv.at[j],
                device_id=(*q, c), device_id_type=MESH)
            cp.start()
            ici.append(cp)
        d2d[0].wait_recv()
        for cp in ici:
            cp.wait_recv()
        add_tiles([g_ref.at[dev((x, y), c)], a_ref.at[0], b_ref.at[0], b_ref.at[1], b_ref.at[2]], out_ref, vo_f)
        for cp in d2d + ici:
            cp.wait_send()

    hbm = pl.BlockSpec(memory_space=pl.ANY)
    out, _, _, _ = pl.pallas_call(
        body, name="reduce_scatter",
        in_specs=[hbm], out_specs=[hbm, hbm, hbm, hbm],
        out_shape=[jax.ShapeDtypeStruct((R, Wd), F32), jax.ShapeDtypeStruct((4, R, Wd), BF16),
                   jax.ShapeDtypeStruct((3, R, Wd), BF16), jax.ShapeDtypeStruct((3, R, Wd), BF16)],
        scratch_shapes=[pltpu.VMEM((5, tr, Wd), BF16), pltpu.VMEM((tr, Wd), BF16), pltpu.VMEM((tr, Wd), F32),
                        pltpu.SemaphoreType.DMA((4,)), pltpu.SemaphoreType.DMA((4,)),
                        pltpu.SemaphoreType.DMA((3,)), pltpu.SemaphoreType.DMA((3,))],
        compiler_params=pltpu.CompilerParams(vmem_limit_bytes=VMEM_LIMIT),
    )(gparts)
    return out


def _peer(x, y, c, k):
    return (x ^ ((k >> 2) & 1), y ^ ((k >> 1) & 1), c ^ (k & 1))


HBM_SPEC = pl.BlockSpec(memory_space=pltpu.HBM)
SEM_SPEC = pl.BlockSpec(memory_space=pltpu.SEMAPHORE)


def _exchange_refs(srcs, lands, m, k, x, y, c, scatter):
    peer = _peer(x, y, c, k)
    if scatter:
        return srcs[m].at[4 * peer[0] + 2 * peer[1] + peer[2]], lands[m].at[k - 1], peer
    return srcs[m], lands[m].at[4 * x + 2 * y + c], peer


def _exchange_start(arrs, land_shapes, scatter, name):
    n = len(arrs)

    def body(*refs):
        srcs, lands = refs[:n], refs[n:2 * n]
        send_sems, recv_sems = refs[2 * n], refs[2 * n + 1]
        token = refs[-1]
        x, y, c, _ = _my_place()
        for m in range(n):
            for k in range(1, N_DEV):
                src, dst, peer = _exchange_refs(srcs, lands, m, k, x, y, c, scatter)
                pltpu.make_async_remote_copy(
                    src_ref=src, dst_ref=dst, send_sem=send_sems.at[7 * m + k - 1],
                    recv_sem=recv_sems.at[7 * m + k - 1], device_id=peer, device_id_type=MESH).start()
        token[...] = jnp.zeros_like(token)

    zones = [lax.empty(s_, a.dtype) for s_, a in zip(land_shapes, arrs)]
    outs = pl.pallas_call(
        body, name=name,
        out_shape=(pltpu.SemaphoreType.DMA((7 * n,)), pltpu.SemaphoreType.DMA((7 * n,)),
                   *[pltpu.HBM(a.shape, a.dtype) for a in arrs], *[pltpu.HBM(z.shape, z.dtype) for z in zones],
                   jax.ShapeDtypeStruct((8, 128), F32)),
        in_specs=[HBM_SPEC] * (2 * n),
        out_specs=(SEM_SPEC, SEM_SPEC, *[HBM_SPEC] * (2 * n), pl.BlockSpec(memory_space=pltpu.VMEM)),
        input_output_aliases={m: 2 + m for m in range(2 * n)},
        compiler_params=pltpu.CompilerParams(has_side_effects=pltpu.SideEffectType.DATAFLOW_SIDE_EFFECTING),
    )(*[pltpu.with_memory_space_constraint(a, pltpu.HBM) for a in arrs],
      *[pltpu.with_memory_space_constraint(z, pltpu.HBM) for z in zones])
    return outs[0], outs[1], list(outs[2:2 + n]), list(outs[2 + n:2 + 2 * n]), outs[-1]


def _exchange_wait(send_sems, recv_sems, arrs, zones, after, scatter, name):
    n = len(arrs)

    def body(*refs):
        srcs, lands = refs[:n], refs[n:2 * n]
        send_sems, recv_sems = refs[2 * n], refs[2 * n + 1]
        x, y, c, _ = _my_place()
        for m in range(n):
            for k in range(1, N_DEV):
                src, dst, peer = _exchange_refs(srcs, lands, m, k, x, y, c, scatter)
                cp = pltpu.make_async_remote_copy(
                    src_ref=src, dst_ref=dst, send_sem=send_sems.at[7 * m + k - 1],
                    recv_sem=recv_sems.at[7 * m + k - 1], device_id=peer, device_id_type=MESH)
                cp.wait_send()
                cp.wait_recv()

    outs = pl.pallas_call(
        body, name=name,
        out_shape=tuple(pltpu.HBM(a.shape, a.dtype) for a in list(arrs) + list(zones)),
        in_specs=[HBM_SPEC] * (2 * n) + [SEM_SPEC, SEM_SPEC, pl.BlockSpec(memory_space=pl.ANY)],
        out_specs=tuple([HBM_SPEC] * (2 * n)),
        input_output_aliases={m: m for m in range(2 * n)},
        compiler_params=pltpu.CompilerParams(has_side_effects=pltpu.SideEffectType.DATAFLOW_SIDE_EFFECTING),
    )(*arrs, *zones, send_sems, recv_sems, after)
    return list(outs[n:])


def _sum_parts(own, parts, tr, dep=None):
    R, W = own.shape

    def body(own_ref, parts_ref, out_ref):
        acc = own_ref[...].astype(F32)
        for k in range(N_DEV - 1):
            acc = acc + parts_ref[k].astype(F32)
        out_ref[...] = acc

    in_specs = [pl.BlockSpec((tr, W), lambda i: (i, 0)), pl.BlockSpec((N_DEV - 1, tr, W), lambda i: (0, i, 0))]
    body, in_specs, args = _with_dep(body, dep, in_specs, [own, parts])
    return pl.pallas_call(
        body, name="sum_parts", grid=(R // tr,),
        in_specs=in_specs,
        out_specs=pl.BlockSpec((tr, W), lambda i: (i, 0)),
        out_shape=jax.ShapeDtypeStruct((R, W), F32),
        compiler_params=_params("parallel"),
    )(*args)


def _all_reduce_small(v, dep=None):
    Rn, Wd = v.shape

    def body(v_ref, out_ref, gat_ref, send_sems, recv_sems):
        x, y, c, _ = _my_place()
        me = 4 * x + 2 * y + c
        gat_ref[me] = v_ref[...]
        copies = []
        for k in range(1, N_DEV):
            fx, fy, fc = (k >> 2) & 1, (k >> 1) & 1, k & 1
            peer = (x ^ fx, y ^ fy, c ^ fc)
            cp = pltpu.make_async_remote_copy(
                src_ref=v_ref, dst_ref=gat_ref.at[me], send_sem=send_sems.at[k - 1], recv_sem=recv_sems.at[k - 1],
                device_id=peer, device_id_type=MESH)
            cp.start()
            copies.append(cp)
        for cp in copies:
            cp.wait_recv()
        for cp in copies:
            cp.wait_send()
        acc = gat_ref[0]
        for k in range(1, N_DEV):
            acc = acc + gat_ref[k]
        out_ref[...] = acc

    vm = pl.BlockSpec(memory_space=pltpu.VMEM)
    body, in_specs, args = _with_dep(body, dep, [vm], [v])
    return pl.pallas_call(
        body, name="all_reduce_small", in_specs=in_specs, out_specs=vm,
        out_shape=jax.ShapeDtypeStruct((Rn, Wd), F32),
        scratch_shapes=[pltpu.VMEM((N_DEV, Rn, Wd), F32), pltpu.SemaphoreType.DMA((7,)),
                        pltpu.SemaphoreType.DMA((7,))],
    )(*args)


def _t5_bucket(rel):
    half = N_BUCKETS // 2
    max_exact = half // 2
    ret = jnp.where(rel > 0, half, 0)
    n = jnp.abs(rel)
    nf = jnp.maximum(n, 1).astype(F32)
    large = max_exact + (jnp.log(nf / max_exact) / math.log(MAX_DISTANCE / max_exact)
                         * (half - max_exact)).astype(jnp.int32)
    large = jnp.minimum(large, half - 1)
    return ret + jnp.where(n < max_exact, n, large)


def _band(R, d):
    W = BQ + 2 * R
    rel = jnp.arange(W)[None, :] - R - jnp.arange(BQ)[:, None]
    return _t5_bucket(rel * d), jnp.abs(rel) <= R


def _onehot(R, d):
    bkt, in_band = _band(R, d)
    return ((bkt.reshape(1, -1) == jnp.arange(128)[:, None]) & in_band.reshape(1, -1)).astype(BF16)


def _bias_expand(table_t, onehot):
    H = table_t.shape[0]
    K = onehot.shape[1]

    def body(t_ref, oh_ref, out_ref):
        oh = oh_ref[...]
        t = t_ref[...]
        hi = t.astype(BF16)
        r1 = t - hi.astype(F32)
        mid = r1.astype(BF16)
        low = (r1 - mid.astype(F32)).astype(BF16)
        marked = _dot(jnp.ones(t.shape, BF16), oh) > 0.5
        out_ref[...] = jnp.where(marked, _dot(hi, oh) + _dot(mid, oh) + _dot(low, oh), NEG)

    vm = pl.BlockSpec(memory_space=pltpu.VMEM)
    return pl.pallas_call(
        body, name="bias_expand", in_specs=[vm, vm], out_specs=vm,
        out_shape=jax.ShapeDtypeStruct((H, K), F32),
        compiler_params=pltpu.CompilerParams(vmem_limit_bytes=VMEM_LIMIT),
    )(table_t, onehot)


def _bias_matrix(table, R, d):
    table_t = jnp.pad(table.T, ((0, 0), (0, 128 - N_BUCKETS)))
    return _bias_expand(table_t, _onehot(R, d)).reshape(table.shape[1], BQ, BQ + 2 * R)


def _bias_variants(base, R):
    H, _, W = base.shape
    col = jnp.arange(W)
    before, after = col < R, col >= BQ + R
    masks = jnp.stack([jnp.zeros_like(before), before, after, before | after])
    v = jnp.where(masks[None, :, None, :], NEG, base[:, None])
    v = v.reshape(H // 2, 2, 4, BQ, W).transpose(0, 2, 1, 3, 4).reshape(H // 2, 4, 2 * BQ, W)
    return v, v.transpose(0, 1, 3, 2)


def _bias_grad(dbt, R, d):
    P, W, _ = dbt.shape
    dbm = dbt.reshape(P, W, 2, BQ).transpose(0, 2, 3, 1).reshape(2 * P, BQ * W)
    return _bias_reduce(_onehot(R, d), dbm)[:, :N_BUCKETS].T


def _deint(a, d):
    if d == 1:
        return a
    H, T, X = a.shape
    return a.reshape(H, T // d, d, X).transpose(0, 2, 1, 3).reshape(H * d, T // d, X)


def _reint(a, d):
    if d == 1:
        return a
    Hd, L, X = a.shape
    return a.reshape(Hd // d, d, L, X).transpose(0, 2, 1, 3).reshape(Hd // d, L * d, X)


def _pad_rows(a, R):
    return jnp.pad(a, ((0, 0), (R, R), (0, 0)))


def _tile2(gain):
    return jnp.concatenate([gain, gain])


ROW_W_O, ROW_GATE, ROW_QKV, ROW_PROJ, B_ROWS = 768, 896, 1024, 1312, 1344
BLK_W_O, BLK_GATE = ROW_W_O // 128, ROW_GATE // 128


def _pack_layer(wts, i):
    a = jnp.stack([wts["ffn1_w_in"][i], wts["ffn2_w_in"][i]])
    D = a.shape[1]
    b = jnp.concatenate([
        wts["ffn1_w_out"][i], wts["ffn2_w_out"][i],
        jnp.zeros((ROW_W_O - 2 * wts["ffn1_w_out"].shape[1], D), a.dtype),
        wts["w_o"][i], wts["w_ple_gate"][i], wts["w_qkv"][i].reshape(-1, D), wts["w_ple_proj"][i].reshape(-1, D)])
    return a, b


def _unpack_layer(sums, like):
    w_in2, b1, b2, w_in1, w_out1 = sums
    n_out, n_sq = like["ffn1_w_out"].shape[1], like["w_o"].shape[1]
    out = {}
    if w_in2 is not None:
        out.update(ffn2_w_in=w_in2, ffn2_w_out=b1[:n_out], w_ple_gate=b1[n_out:n_out + n_sq],
                   w_ple_proj=b1[n_out + n_sq:].reshape(like["w_ple_proj"].shape[1:]))
    if b2 is not None:
        out.update(w_o=b2[:n_sq], w_qkv=b2[n_sq:].reshape(like["w_qkv"].shape[1:]))
    if w_in1 is not None:
        out.update(ffn1_w_in=w_in1, ffn1_w_out=w_out1)
    return out


def _col_sharded(gb, r0, r1, rows):
    return gb[:, r0:r1].reshape(N_DEV, rows, -1).transpose(1, 0, 2).reshape(rows, -1)


def _to_col_shards(g):
    rows = g.shape[0]
    return g.reshape(rows, N_DEV, -1).transpose(1, 0, 2).reshape(N_DEV, -1, 1024)


def _layer_weights(ga, gb, p_dim):
    return dict(ga=ga, gb=gb, w_qkv=_col_sharded(gb, ROW_QKV, ROW_PROJ, ga.shape[2]),
                w_proj=_col_sharded(gb, ROW_PROJ, B_ROWS, p_dim))


def _layer_fwd(x, p, w, sm, i, target, tm, biases, dep=None):
    ga, gb = w["ga"], w["gb"]
    saved = {}
    saved["x0"] = x
    x1, saved["h1"], saved["zg1"], saved["zu1"], saved["s1"] = _ffn_fwd(
        x, sm["norm_ffn1"][i][None], ga, gb, 0, tm, dep)
    saved["x1"] = x1
    qkv, saved["hm"] = _qkv_fwd(x1, sm["norm_mix"][i][None], w["w_qkv"], tm)
    saved["qkv"] = qkv
    gains2 = jnp.stack([_tile2(sm[k][i]) for k in ("q_norm_a", "k_norm_a", "q_norm_b", "k_norm_b")])
    saved["gains2"] = gains2
    qa, ka, va, qb, kb, vb = _attn_prep(qkv, gains2, tm)
    no_sink = jnp.full((8,), NEG, F32)
    branches = []
    outs = []
    for (R, d), bias in zip(DILATED, biases[:3]):
        qd, kd, vd = _deint(qa, d), _pad_rows(_deint(ka, d), R), _pad_rows(_deint(va, d), R)
        sink = jnp.tile(no_sink, d)
        o, lse = _attn_fwd(qd, kd, vd, bias[0], sink, R, 1, d)
        branches.append((qd, kd, vd, bias, sink, R, d))
        outs += [_reint(o, d), _reint(lse, d)]
    bias_b = biases[3]
    kbp, vbp = _pad_rows(kb, SWA_RADIUS), _pad_rows(vb, SWA_RADIUS)
    sink_b = sm["sink_b"][i]
    ob, lb = _attn_fwd(qb, kbp, vbp, bias_b[0], sink_b, SWA_RADIUS, 2, 1)
    oa, la, o_cat = _attn_merge(*outs, ob, tm)
    saved.update(branches=branches, b=(qb, kbp, vbp, bias_b, sink_b), oa=oa, la=la, ob=ob, lb=lb, o_cat=o_cat)
    x2 = _oproj_fwd(x1, o_cat, gb, BLK_W_O, tm)
    saved["x2"] = x2
    x3, saved["h2"], saved["zg2"], saved["zu2"], saved["s2"] = _ffn_fwd(
        x2, sm["norm_ffn2"][i][None], ga, gb, 1, tm)
    saved["x3"] = x3
    res = _ple_fwd(x3, sm["norm_ple"][i][None], gb, BLK_GATE, p, w["w_proj"], target, tm)
    y, saved["hp"], saved["gate"], saved["pp"], saved["pb"] = res[:5]
    loss = res[5] if target is not None else None
    return y, loss, saved


def _layer_bwd(dy, w, sm, i, sv, tm, dep=None, on_ready=None, on_small=None, on_last=None):
    ga, gb = w["ga"], w["gb"]
    gs = {}
    D = dy.shape[1]
    dgl, dpp = _ple_bwd(dy, sv["gate"], sv["pp"], tm, dep)
    d_gate = _matmul_tn(sv["hp"], dgl, D, 2 * tm)
    d_proj = _matmul_tn(sv["pb"], dpp, D, 2 * tm)
    dx3, gs["norm_ple"] = _dense_norm_bwd(dy, dgl, gb, BLK_GATE, sv["x3"], sm["norm_ple"][i][None], tm)
    dx2, dyb, dzg, dzu, gs["norm_ffn2"] = _ffn_bwd(dx3, sv["x2"], sm["norm_ffn2"][i][None], sv["zg2"], sv["zu2"],
                                                   ga, gb, 1, tm)
    dwin2, dwo2 = _ffn_dw(sv["h2"], dzg, dzu, sv["s2"], dyb, 2 * tm)
    half = dwo2.shape[1] // 2
    after_ffn2 = [dwin2, jnp.concatenate([dwo2.reshape(N_DEV, half, D), d_gate.reshape(N_DEV, -1, D),
                                          _to_col_shards(d_proj)], axis=1)]
    token = None if on_ready is None else on_ready(0, after_ffn2)
    dx2b, do = _oproj_bwd(dx2, gb, BLK_W_O, tm, token)
    d_wo = _matmul_tn(sv["o_cat"], dx2b, D, 2 * tm)
    do_a, do_b = do[:4], do[4:]
    dqa, dka, dva, dbias = [], [], [], []
    for qd, kd, vd, bias, sink, R, d in sv["branches"]:
        dq, dk, dv, dbm, _ = _attn_bwd(qd, kd, vd, bias[1], sink, _deint(sv["oa"], d), _deint(sv["la"], d),
                                        _deint(do_a, d), R, 1, d)
        L = qd.shape[1]
        dqa.append(_reint(dq, d))
        dka.append(_reint(dk[:, R:R + L], d))
        dva.append(_reint(dv[:, R:R + L], d))
        dbias.append(dbm)
    qb, kbp, vbp, bias_b, sink_b = sv["b"]
    dqb, dkb, dvb, dbm_b, dsink = _attn_bwd(qb, kbp, vbp, bias_b[1], sink_b, sv["ob"], sv["lb"], do_b,
                                            SWA_RADIUS, 2, 1)
    T = qb.shape[1]
    gs["rel_bias"] = dbias + [dbm_b]
    gs["sink_b"] = jnp.sum(dsink[:, 0].reshape(-1, 2, BQ), axis=2).reshape(-1)
    dqkv, dgains2 = _attn_post(sv["qkv"], sv["gains2"], dqa, dka, dva, dqb,
                               dkb[:, SWA_RADIUS:SWA_RADIUS + T], dvb[:, SWA_RADIUS:SWA_RADIUS + T], tm // 2)
    dgains = dgains2[:, :HEAD_DIM] + dgains2[:, HEAD_DIM:]
    for k, name in enumerate(("q_norm_a", "k_norm_a", "q_norm_b", "k_norm_b")):
        gs[name] = dgains[k]
    d_qkv = _matmul_tn(sv["hm"], dqkv, dqkv.shape[1] // 2, 2 * tm)
    after_mixer = [jnp.concatenate([d_wo.reshape(N_DEV, -1, D), _to_col_shards(d_qkv)], axis=1)]
    token = None if on_ready is None else on_ready(1, after_mixer)
    dx1, gs["norm_mix"] = _dense_norm_bwd(dx2, dqkv, w["w_qkv"], None, sv["x1"], sm["norm_mix"][i][None], tm)
    g1 = sm["norm_ffn1"][i][None]
    if on_last is None:
        dx0, dyb, dzg, dzu, gs["norm_ffn1"] = _ffn_bwd(dx1, sv["x0"], g1, sv["zg1"], sv["zu1"], ga, gb, 0, tm, token)
        dwin1, dwo1 = _ffn_dw(sv["h1"], dzg, dzu, sv["s1"], dyb, 2 * tm)
        return dx0, (after_ffn2, after_mixer, [dwin1, dwo1.reshape(N_DEV, half, D)]), gs
    dyb, dzg, dzu = _ffn_bwd_dz(dx1, sv["zg1"], sv["zu1"], gb, 0, tm, token)
    dwin1, dwo1 = _ffn_dw(sv["h1"], dzg, dzu, sv["s1"], dyb, 2 * tm, on_small(gs))
    last = [dwin1, dwo1.reshape(N_DEV, half, D)]
    dx0, gs["norm_ffn1"] = _ffn_bwd_dx(dx1, sv["x0"], g1, dzg, dzu, ga, 0, tm, on_last(last))
    return dx0, (after_ffn2, after_mixer, last), gs


def _bias_matrices(rel_bias):
    biases = [_bias_variants(_bias_matrix(rel_bias[:, :8], R, d), R) for R, d in DILATED]
    biases.append(_bias_variants(_bias_matrix(rel_bias[:, 8:], SWA_RADIUS, 1), SWA_RADIUS))
    return biases


def _stack_small(per_layer):
    small = {}
    for k, v in per_layer.items():
        if k == "rel_bias":
            per_branch = [sum(parts) for parts in zip(*v.values())]
            drel_a = sum(_bias_grad(t, R, d) for t, (R, d) in zip(per_branch[:3], DILATED))
            small[k] = jnp.concatenate([drel_a, _bias_grad(per_branch[3], SWA_RADIUS, 1)], axis=1)
        else:
            small[k] = jnp.stack([v[i].reshape(-1) for i in sorted(v)])
    return small


TM = 512
SUM_TILES = (512, 512, 416, 512, 352)
LAST_GROUP = ("ffn1_w_in", "ffn1_w_out")


def _pack_small(d, extra=None):
    parts = [d[k].reshape(-1) for k in SMALL]
    if extra is not None:
        parts.append(extra.reshape(-1))
    flat = jnp.concatenate(parts)
    return jnp.pad(flat, (0, SMALL_ROWS * 128 - flat.shape[0])).reshape(SMALL_ROWS, 128)


def _unpack_small(buf, like):
    flat = buf.reshape(-1)
    out, off = {}, 0
    for k in SMALL:
        n = like[k].size
        out[k] = flat[off:off + n].reshape(like[k].shape)
        off += n
    return out, flat[off]


def kernel(x, p, rel_bias, norm_ffn1, ffn1_w_in, ffn1_w_out, norm_mix, w_qkv, q_norm_a, k_norm_a, q_norm_b, k_norm_b, sink_b, w_o, norm_ffn2, ffn2_w_in, ffn2_w_out, norm_ple, w_ple_gate, w_ple_proj, loss_target, m_rel_bias, m_norm_ffn1, m_ffn1_w_in, m_ffn1_w_out, m_norm_mix, m_w_qkv, m_q_norm_a, m_k_norm_a, m_q_norm_b, m_k_norm_b, m_sink_b, m_w_o, m_norm_ffn2, m_ffn2_w_in, m_ffn2_w_out, m_norm_ple, m_w_ple_gate, m_w_ple_proj, v_rel_bias, v_norm_ffn1, v_ffn1_w_in, v_ffn1_w_out, v_norm_mix, v_w_qkv, v_q_norm_a, v_k_norm_a, v_q_norm_b, v_k_norm_b, v_sink_b, v_w_o, v_norm_ffn2, v_ffn2_w_in, v_ffn2_w_out, v_norm_ple, v_w_ple_gate, v_w_ple_proj):
    wts = dict(rel_bias=rel_bias, norm_ffn1=norm_ffn1, ffn1_w_in=ffn1_w_in, ffn1_w_out=ffn1_w_out,
               norm_mix=norm_mix, w_qkv=w_qkv, q_norm_a=q_norm_a, k_norm_a=k_norm_a, q_norm_b=q_norm_b,
               k_norm_b=k_norm_b, sink_b=sink_b, w_o=w_o, norm_ffn2=norm_ffn2, ffn2_w_in=ffn2_w_in,
               ffn2_w_out=ffn2_w_out, norm_ple=norm_ple, w_ple_gate=w_ple_gate, w_ple_proj=w_ple_proj)
    mom = dict(rel_bias=m_rel_bias, norm_ffn1=m_norm_ffn1, ffn1_w_in=m_ffn1_w_in, ffn1_w_out=m_ffn1_w_out,
               norm_mix=m_norm_mix, w_qkv=m_w_qkv, q_norm_a=m_q_norm_a, k_norm_a=m_k_norm_a, q_norm_b=m_q_norm_b,
               k_norm_b=m_k_norm_b, sink_b=m_sink_b, w_o=m_w_o, norm_ffn2=m_norm_ffn2, ffn2_w_in=m_ffn2_w_in,
               ffn2_w_out=m_ffn2_w_out, norm_ple=m_norm_ple, w_ple_gate=m_w_ple_gate, w_ple_proj=m_w_ple_proj)
    var = dict(rel_bias=v_rel_bias, norm_ffn1=v_norm_ffn1, ffn1_w_in=v_ffn1_w_in, ffn1_w_out=v_ffn1_w_out,
               norm_mix=v_norm_mix, w_qkv=v_w_qkv, q_norm_a=v_q_norm_a, k_norm_a=v_k_norm_a, q_norm_b=v_q_norm_b,
               k_norm_b=v_k_norm_b, sink_b=v_sink_b, w_o=v_w_o, norm_ffn2=v_norm_ffn2, ffn2_w_in=v_ffn2_w_in,
               ffn2_w_out=v_ffn2_w_out, norm_ple=v_norm_ple, w_ple_gate=v_w_ple_gate, w_ple_proj=v_w_ple_proj)
    sm = {k: wts[k] for k in SMALL}
    p_dim = p.shape[-1]
    me = 4 * lax.axis_index("x") + 2 * lax.axis_index("y") + lax.axis_index("c")
    packed = []
    for i in range(2):
        a, b = _pack_layer(wts, i)
        packed.append([a.reshape(-1, a.shape[-1]).astype(BF16), b.astype(BF16)])
    a_shape = (2, ffn1_w_in.shape[1], ffn1_w_in.shape[2])

    def weights_of(zones):
        return _layer_weights(zones[0].reshape((N_DEV,) + a_shape), zones[1], p_dim)

    w0 = weights_of([_all_gather(t) for t in packed[0]])
    zone_shapes = [(N_DEV,) + t.shape for t in packed[1]]
    ssem, rsem, thru, zones, token = _exchange_start(packed[1], zone_shapes, False, "gather_start")
    biases = _bias_matrices(rel_bias)
    x1, _, sv0 = _layer_fwd(x[0], p[0, 0], w0, sm, 0, None, TM, biases, dep=token)
    zones = _exchange_wait(ssem, rsem, thru, zones, x1, False, "gather_wait")
    w1 = weights_of([lax.dynamic_update_index_in_dim(z, t, me, 0) for z, t in zip(zones, packed[1])])
    dy, loss, sv1 = _layer_fwd(x1, p[1, 0], w1, sm, 1, loss_target[0], TM, biases)

    def slots_for(arrs):
        return [(N_DEV - 1,) + t.shape[1:] for t in arrs]

    dx1, groups1, gs1 = _layer_bwd(dy, w1, sm, 1, sv1, TM)
    g1 = groups1[0] + groups1[1] + groups1[2]
    ex1 = _exchange_start(g1, slots_for(g1), True, "scatter_start")
    held = {}

    def on_ready(stage, group):
        if stage == 1:
            held["slots1"] = _exchange_wait(*ex1[:4], group[0], True, "scatter_wait")
        held[stage] = _exchange_start(group, slots_for(group), True, f"scatter_start_{stage}")
        return held[stage][4]

    def on_small(gs0):
        part = dict(gs0, norm_ffn1=jnp.zeros_like(gs1["norm_ffn1"]))
        gsmall = _stack_small({k: {0: part[k], 1: gs1[k]} for k in part})
        held["small"] = _all_reduce_small(_pack_small(gsmall, loss[0, :1]))
        return held["small"]

    def on_last(group):
        held["last"] = _exchange_start(group, slots_for(group), True, "scatter_start_2")
        return held["last"][4]

    dx, groups0, gs0 = _layer_bwd(dx1, w0, sm, 0, sv0, TM, dep=ex1[4], on_ready=on_ready, on_small=on_small,
                                  on_last=on_last)
    last = groups0[2]
    slots0 = [_exchange_wait(*held[stage][:4], last[0], True, f"scatter_wait_{stage}") for stage in (0, 1)]

    def summed(arrs, slots, tiles):
        return [_sum_parts(lax.dynamic_index_in_dim(t, me, 0, keepdims=False), s_, tr)
                for t, s_, tr in zip(arrs, slots, tiles)]

    r1 = summed(g1, held["slots1"], SUM_TILES)
    r0 = summed(groups0[0], slots0[0], SUM_TILES[:2]) + summed(groups0[1], slots0[1], SUM_TILES[2:3])

    def update(names, layers):
        for k in names:
            grads[k] = jnp.stack([layers[0][k], layers[1][k]])
            delta[k], new_m[k], new_v[k] = _adamw(wts[k], grads[k], mom[k], var[k])

    grads, delta, new_m, new_v = {}, {}, {}, {}
    layer1 = _unpack_layer(r1, wts)
    update([k for k in BIG if k not in LAST_GROUP], [_unpack_layer(r0 + [None, None], wts), layer1])

    slots_last = _exchange_wait(*held["last"][:4], delta["ffn2_w_in"], True, "scatter_wait_2")
    update(LAST_GROUP, [_unpack_layer([None, None, None] + summed(last, slots_last, SUM_TILES[3:]), wts), layer1])
    late = _all_reduce_small(gs0["norm_ffn1"].reshape(-1, 128), dep=slots_last[0])
    small_sum, loss_sum = _unpack_small(held["small"], sm)
    small_sum["norm_ffn1"] = small_sum["norm_ffn1"].at[0].add(late.reshape(-1))
    grads.update(small_sum)
    zeros = {k: jnp.zeros_like(wts[k]) for k in SMALL}
    ds, ms, vs = _adamw(_pack_small(wts), _pack_small(small_sum), _pack_small(mom), _pack_small(var))
    for packed, dst in ((ds, delta), (ms, new_m), (vs, new_v)):
        dst.update(_unpack_small(packed, zeros)[0])

    return (loss_sum, dx[None], *[grads[k] for k in WEIGHTS], *[delta[k] for k in WEIGHTS],
            *[new_m[k] for k in WEIGHTS], *[new_v[k] for k in WEIGHTS])
```

```python
import functools
import math

import jax
import jax.numpy as jnp
from jax import lax
from jax.experimental import pallas as pl
from jax.experimental.pallas import tpu as pltpu

F32 = jnp.float32
BF16 = jnp.bfloat16

N_DEV = 8
HEAD_DIM = 64
PAIR = 2 * HEAD_DIM
BQ = 128
N_BUCKETS = 32
MAX_DISTANCE = 1024
DILATED = ((64, 1), (64, 4), (64, 16))
SWA_RADIUS = 128
EPS = 1e-6
NEG = -1e30
ADAM_LR, ADAM_B1, ADAM_B2, ADAM_EPS, ADAM_WD, ADAM_STEP = 0.001, 0.9, 0.999, 1e-08, 0.01, 10
VMEM_LIMIT = 56 * 1024 * 1024
AXES = ("x", "y", "c")
MESH = pl.DeviceIdType.MESH

BIG = ("ffn1_w_in", "ffn1_w_out", "w_qkv", "w_o", "ffn2_w_in", "ffn2_w_out", "w_ple_gate", "w_ple_proj")
SMALL = ("rel_bias", "norm_ffn1", "norm_mix", "q_norm_a", "k_norm_a", "q_norm_b", "k_norm_b", "sink_b",
         "norm_ffn2", "norm_ple")
WEIGHTS = ("rel_bias", "norm_ffn1", "ffn1_w_in", "ffn1_w_out", "norm_mix", "w_qkv", "q_norm_a", "k_norm_a",
           "q_norm_b", "k_norm_b", "sink_b", "w_o", "norm_ffn2", "ffn2_w_in", "ffn2_w_out", "norm_ple",
           "w_ple_gate", "w_ple_proj")
SMALL_ROWS = 96


def _params(*sem):
    return pltpu.CompilerParams(dimension_semantics=sem, vmem_limit_bytes=VMEM_LIMIT)


def _dot(a, b):
    return jnp.dot(a, b, preferred_element_type=F32)


def _dot_nt(a, b):
    return lax.dot_general(a, b, (((1,), (1,)), ((), ())), preferred_element_type=F32)


def _dot_tn(a, b):
    return lax.dot_general(a, b, (((0,), (0,)), ((), ())), preferred_element_type=F32)


def _sigmoid(x):
    return 1.0 / (1.0 + jnp.exp(-x))


def _rstd(xv):
    return lax.rsqrt(jnp.mean(xv * xv, axis=-1, keepdims=True) + EPS)


def _norm_bwd(dh, xv, gv):
    r = _rstd(xv)
    xn = xv * r
    dg = jnp.sum(dh * xn, axis=0, keepdims=True)
    dxn = dh * gv
    dx = r * (dxn - xn * jnp.mean(dxn * xn, axis=-1, keepdims=True))
    return dx, dg


def _lo_mask(shape):
    return lax.broadcasted_iota(jnp.int32, shape, len(shape) - 1) < HEAD_DIM


def _half_sum(t, lo):
    s0 = jnp.sum(jnp.where(lo, t, 0.0), axis=1, keepdims=True)
    s1 = jnp.sum(jnp.where(lo, 0.0, t), axis=1, keepdims=True)
    return jnp.where(lo, s0, s1)


FFN_PARTS = 2


def _ffn_weight_specs(f, nj, D, C):
    return [pl.BlockSpec((None, None, D, C), lambda i, j: (j, f, 0, 0)),
            pl.BlockSpec((None, None, D, C), lambda i, j: (j + nj, f, 0, 0)),
            pl.BlockSpec((2, C // 2, D), lambda i, j: (j, f, 0))]


def _with_dep(body, dep, in_specs, args):
    if dep is None:
        return body, in_specs, args

    def body_after(dep_ref, *refs):
        body(*refs)

    return body_after, [pl.BlockSpec(memory_space=pl.ANY)] + in_specs, [dep] + args


def _ffn_fwd(x, g, ga, gb, f, tm, dep=None):
    T, D = x.shape
    nj, C = ga.shape[0] // 2, ga.shape[3]

    def body(x_ref, g_ref, wg_ref, wu_ref, wo_ref, xo_ref, h_ref, zg_ref, zu_ref, s_ref, h_scr, acc):
        j = pl.program_id(1)

        @pl.when(j == 0)
        def _():
            xv = x_ref[...]
            hb = (xv * _rstd(xv) * g_ref[...]).astype(BF16)
            h_scr[...] = hb
            h_ref[...] = hb
            acc[...] = jnp.zeros_like(acc)

        wo = wo_ref[...].reshape(C, D)
        for part in range(FFN_PARTS):
            sl = pl.ds(part * (tm // FFN_PARTS), tm // FFN_PARTS)
            hb = h_scr[sl, :]
            gt = _dot(hb, wg_ref[...])
            up = _dot(hb, wu_ref[...])
            s = (gt * _sigmoid(gt) * up).astype(BF16)
            zg_ref[sl, :] = gt.astype(BF16)
            zu_ref[sl, :] = up.astype(BF16)
            s_ref[sl, :] = s
            acc[sl, :] += _dot(s, wo)

        @pl.when(j == nj - 1)
        def _():
            xo_ref[...] = x_ref[...] + 0.5 * acc[...]

    tok = pl.BlockSpec((tm, D), lambda i, j: (i, 0))
    chunk = pl.BlockSpec((None, tm, C), lambda i, j: (j, i, 0))
    in_specs = [tok, pl.BlockSpec((1, D), lambda i, j: (0, 0))] + _ffn_weight_specs(f, nj, D, C)
    body, in_specs, args = _with_dep(body, dep, in_specs, [x, g, ga, ga, gb])
    return pl.pallas_call(
        body, name="ffn_fwd", grid=(T // tm, nj),
        in_specs=in_specs,
        out_specs=[tok, tok, chunk, chunk, chunk],
        out_shape=[jax.ShapeDtypeStruct((T, D), F32), jax.ShapeDtypeStruct((T, D), BF16),
                   jax.ShapeDtypeStruct((nj, T, C), BF16), jax.ShapeDtypeStruct((nj, T, C), BF16),
                   jax.ShapeDtypeStruct((nj, T, C), BF16)],
        scratch_shapes=[pltpu.VMEM((tm, D), BF16), pltpu.VMEM((tm, D), F32)],
        compiler_params=_params("parallel", "arbitrary"),
    )(*args)


def _ffn_bwd(dxo, x, g, zg, zu, ga, gb, f, tm, dep=None):
    T, D = x.shape
    nj, C = ga.shape[0] // 2, ga.shape[3]

    def body(dxo_ref, x_ref, g_ref, zg_ref, zu_ref, wg_ref, wu_ref, wo_ref,
             dx_ref, dy_ref, dzg_ref, dzu_ref, dgn_ref, dy_scr, acc):
        i, j = pl.program_id(0), pl.program_id(1)

        @pl.when(j == 0)
        def _():
            dyb = (0.5 * dxo_ref[...]).astype(BF16)
            dy_scr[...] = dyb
            dy_ref[...] = dyb
            acc[...] = jnp.zeros_like(acc)

        wo = wo_ref[...].reshape(C, D)
        for part in range(FFN_PARTS):
            sl = pl.ds(part * (tm // FFN_PARTS), tm // FFN_PARTS)
            ds = _dot_nt(dy_scr[sl, :], wo)
            gt = zg_ref[sl, :].astype(F32)
            up = zu_ref[sl, :].astype(F32)
            sg = _sigmoid(gt)
            dgt = (ds * up * (sg * (1.0 + gt * (1.0 - sg)))).astype(BF16)
            dup = (ds * (gt * sg)).astype(BF16)
            dzg_ref[sl, :] = dgt
            dzu_ref[sl, :] = dup
            acc[sl, :] += _dot_nt(dgt, wg_ref[...]) + _dot_nt(dup, wu_ref[...])

        @pl.when(j == nj - 1)
        def _():
            dx, dg = _norm_bwd(acc[...], x_ref[...], g_ref[...])
            dx_ref[...] = dxo_ref[...] + dx

            @pl.when(i == 0)
            def _():
                dgn_ref[...] = dg

            @pl.when(i > 0)
            def _():
                dgn_ref[...] += dg

    tok = pl.BlockSpec((tm, D), lambda i, j: (i, 0))
    chunk = pl.BlockSpec((None, tm, C), lambda i, j: (j, i, 0))
    row = pl.BlockSpec((1, D), lambda i, j: (0, 0))
    in_specs = [tok, tok, row, chunk, chunk] + _ffn_weight_specs(f, nj, D, C)
    body, in_specs, args = _with_dep(body, dep, in_specs, [dxo, x, g, zg, zu, ga, ga, gb])
    return pl.pallas_call(
        body, name="ffn_bwd", grid=(T // tm, nj),
        in_specs=in_specs,
        out_specs=[tok, tok, chunk, chunk, row],
        out_shape=[jax.ShapeDtypeStruct((T, D), F32), jax.ShapeDtypeStruct((T, D), BF16),
                   jax.ShapeDtypeStruct((nj, T, C), BF16), jax.ShapeDtypeStruct((nj, T, C), BF16),
                   jax.ShapeDtypeStruct((1, D), F32)],
        scratch_shapes=[pltpu.VMEM((tm, D), BF16), pltpu.VMEM((tm, D), F32)],
        compiler_params=_params("arbitrary", "arbitrary"),
    )(*args)


def _ffn_bwd_dz(dxo, zg, zu, gb, f, tm, dep=None):
    T, D = dxo.shape
    nj, C = zg.shape[0], zg.shape[2]

    def body(dxo_ref, zg_ref, zu_ref, wo_ref, dy_ref, dzg_ref, dzu_ref, dy_scr):
        @pl.when(pl.program_id(1) == 0)
        def _():
            dyb = (0.5 * dxo_ref[...]).astype(BF16)
            dy_scr[...] = dyb
            dy_ref[...] = dyb

        wo = wo_ref[...].reshape(C, D)
        for part in range(FFN_PARTS):
            sl = pl.ds(part * (tm // FFN_PARTS), tm // FFN_PARTS)
            ds = _dot_nt(dy_scr[sl, :], wo)
            gt = zg_ref[sl, :].astype(F32)
            up = zu_ref[sl, :].astype(F32)
            sg = _sigmoid(gt)
            dzg_ref[sl, :] = (ds * up * (sg * (1.0 + gt * (1.0 - sg)))).astype(BF16)
            dzu_ref[sl, :] = (ds * (gt * sg)).astype(BF16)

    tok = pl.BlockSpec((tm, D), lambda i, j: (i, 0))
    chunk = pl.BlockSpec((None, tm, C), lambda i, j: (j, i, 0))
    in_specs = [tok, chunk, chunk, _ffn_weight_specs(f, nj, D, C)[2]]
    body, in_specs, args = _with_dep(body, dep, in_specs, [dxo, zg, zu, gb])
    return pl.pallas_call(
        body, name="ffn_bwd_dz", grid=(T // tm, nj),
        in_specs=in_specs, out_specs=[tok, chunk, chunk],
        out_shape=[jax.ShapeDtypeStruct((T, D), BF16), jax.ShapeDtypeStruct((nj, T, C), BF16),
                   jax.ShapeDtypeStruct((nj, T, C), BF16)],
        scratch_shapes=[pltpu.VMEM((tm, D), BF16)],
        compiler_params=_params("parallel", "arbitrary"),
    )(*args)


def _ffn_bwd_dx(dxo, x, g, dzg, dzu, ga, f, tm, dep=None):
    T, D = x.shape
    nj, C = ga.shape[0] // 2, ga.shape[3]

    def body(dxo_ref, x_ref, g_ref, dzg_ref, dzu_ref, wg_ref, wu_ref, dx_ref, dgn_ref, acc):
        i, j = pl.program_id(0), pl.program_id(1)

        @pl.when(j == 0)
        def _():
            acc[...] = jnp.zeros_like(acc)

        acc[...] += _dot_nt(dzg_ref[...], wg_ref[...]) + _dot_nt(dzu_ref[...], wu_ref[...])

        @pl.when(j == nj - 1)
        def _():
            dx, dg = _norm_bwd(acc[...], x_ref[...], g_ref[...])
            dx_ref[...] = dxo_ref[...] + dx

            @pl.when(i == 0)
            def _():
                dgn_ref[...] = dg

            @pl.when(i > 0)
            def _():
                dgn_ref[...] += dg

    tok = pl.BlockSpec((tm, D), lambda i, j: (i, 0))
    chunk = pl.BlockSpec((None, tm, C), lambda i, j: (j, i, 0))
    row = pl.BlockSpec((1, D), lambda i, j: (0, 0))
    in_specs = [tok, tok, row, chunk, chunk] + _ffn_weight_specs(f, nj, D, C)[:2]
    body, in_specs, args = _with_dep(body, dep, in_specs, [dxo, x, g, dzg, dzu, ga, ga])
    return pl.pallas_call(
        body, name="ffn_bwd_dx", grid=(T // tm, nj),
        in_specs=in_specs, out_specs=[tok, row],
        out_shape=[jax.ShapeDtypeStruct((T, D), F32), jax.ShapeDtypeStruct((1, D), F32)],
        scratch_shapes=[pltpu.VMEM((tm, D), F32)],
        compiler_params=_params("arbitrary", "arbitrary"),
    )(*args)


def _ffn_dw(h, dzg, dzu, s, dy, tk, dep=None):
    T, D = h.shape
    nj, C = s.shape[0], s.shape[2]
    nk = T // tk

    def body(h_ref, dzg_ref, dzu_ref, s_ref, dy_ref, dwin_ref, dwo_ref, ag, au, ao):
        k = pl.program_id(1)

        @pl.when(k == 0)
        def _():
            ag[...] = jnp.zeros_like(ag)
            au[...] = jnp.zeros_like(au)
            ao[...] = jnp.zeros_like(ao)

        hb = h_ref[...]
        ag[...] += _dot_tn(hb, dzg_ref[...])
        au[...] += _dot_tn(hb, dzu_ref[...])
        ao[...] += _dot_tn(s_ref[...], dy_ref[...])

        @pl.when(k == nk - 1)
        def _():
            dwin_ref[0] = ag[...].astype(BF16)
            dwin_ref[1] = au[...].astype(BF16)
            dwo_ref[...] = ao[...].astype(BF16)

    tok = pl.BlockSpec((tk, D), lambda j, k: (k, 0))
    chunk = pl.BlockSpec((None, tk, C), lambda j, k: (j, k, 0))
    body, in_specs, args = _with_dep(body, dep, [tok, chunk, chunk, chunk, tok], [h, dzg, dzu, s, dy])
    dwin, dwo = pl.pallas_call(
        body, name="ffn_dw", grid=(nj, nk),
        in_specs=in_specs,
        out_specs=[pl.BlockSpec((2, None, D, C), lambda j, k: (0, j, 0, 0)),
                   pl.BlockSpec((None, C, D), lambda j, k: (j, 0, 0))],
        out_shape=[jax.ShapeDtypeStruct((2, nj, D, C), BF16), jax.ShapeDtypeStruct((nj, C, D), BF16)],
        scratch_shapes=[pltpu.VMEM((D, C), F32), pltpu.VMEM((D, C), F32), pltpu.VMEM((C, D), F32)],
        compiler_params=_params("parallel", "arbitrary"),
    )(*args)
    return dwin.reshape(2 * nj, D, C), dwo


def _matmul_tn(a, b, tn, tk):
    T, Ka = a.shape
    N = b.shape[1]
    nk = T // tk

    def body(a_ref, b_ref, o_ref, acc):
        k = pl.program_id(1)

        @pl.when(k == 0)
        def _():
            acc[...] = jnp.zeros_like(acc)

        acc[...] += _dot_tn(a_ref[...], b_ref[...])

        @pl.when(k == nk - 1)
        def _():
            o_ref[...] = acc[...].astype(BF16)

    return pl.pallas_call(
        body, name="matmul_tn", grid=(N // tn, nk),
        in_specs=[pl.BlockSpec((tk, Ka), lambda n, k: (k, 0)), pl.BlockSpec((tk, tn), lambda n, k: (k, n))],
        out_specs=pl.BlockSpec((Ka, tn), lambda n, k: (0, n)),
        out_shape=jax.ShapeDtypeStruct((Ka, N), BF16),
        scratch_shapes=[pltpu.VMEM((Ka, tn), F32)],
        compiler_params=_params("parallel", "arbitrary"),
    )(a, b)


def _qkv_fwd(x, g, w, tm):
    T, D = x.shape
    N = w.shape[1]

    def body(x_ref, g_ref, w_ref, o_ref, h_ref):
        xv = x_ref[...]
        hb = (xv * _rstd(xv) * g_ref[...]).astype(BF16)
        h_ref[...] = hb
        o_ref[...] = _dot(hb, w_ref[...])

    return pl.pallas_call(
        body, name="qkv_fwd", grid=(T // tm,),
        in_specs=[pl.BlockSpec((tm, D), lambda i: (i, 0)), pl.BlockSpec((1, D), lambda i: (0, 0)),
                  pl.BlockSpec((D, N), lambda i: (0, 0))],
        out_specs=[pl.BlockSpec((tm, N), lambda i: (i, 0)), pl.BlockSpec((tm, D), lambda i: (i, 0))],
        out_shape=[jax.ShapeDtypeStruct((T, N), F32), jax.ShapeDtypeStruct((T, D), BF16)],
        compiler_params=_params("parallel"),
    )(x, g, w)


def _attn_prep(qkv, gains2, tm):
    T = qkv.shape[0]
    scale = HEAD_DIM ** -0.5

    def body(qkv_ref, g_ref, qa_ref, ka_ref, va_ref, qb_ref, kb_ref, vb_ref):
        lo = _lo_mask((tm, PAIR))

        def normed(c, gi, mult):
            xv = qkv_ref[:, c * PAIR:(c + 1) * PAIR]
            r = lax.rsqrt(_half_sum(xv * xv, lo) * (1.0 / HEAD_DIM) + EPS)
            y = xv * r * g_ref[gi:gi + 1, :]
            return y * mult if mult != 1.0 else y

        def both_halves(v):
            sw = pltpu.roll(v, HEAD_DIM, 1)
            return jnp.where(lo, v, sw), jnp.where(lo, sw, v)

        for c in range(4):
            qa_ref[c] = normed(c, 0, scale).astype(BF16)
            ka_ref[c] = normed(4 + c, 1, 1.0).astype(BF16)
            va_ref[c] = qkv_ref[:, (8 + c) * PAIR:(9 + c) * PAIR].astype(BF16)
            qb_ref[c] = normed(12 + c, 2, scale).astype(BF16)
        k0, k1 = both_halves(normed(16, 3, 1.0))
        kb_ref[0] = k0.astype(BF16)
        kb_ref[1] = k1.astype(BF16)
        v0, v1 = both_halves(qkv_ref[:, 17 * PAIR:18 * PAIR])
        vb_ref[0] = v0.astype(BF16)
        vb_ref[1] = v1.astype(BF16)

    four = pl.BlockSpec((4, tm, PAIR), lambda i: (0, i, 0))
    two = pl.BlockSpec((2, tm, PAIR), lambda i: (0, i, 0))
    s4 = jax.ShapeDtypeStruct((4, T, PAIR), BF16)
    s2 = jax.ShapeDtypeStruct((2, T, PAIR), BF16)
    return pl.pallas_call(
        body, name="attn_prep", grid=(T // tm,),
        in_specs=[pl.BlockSpec((tm, qkv.shape[1]), lambda i: (i, 0)), pl.BlockSpec((4, PAIR), lambda i: (0, 0))],
        out_specs=[four, four, four, four, two, two],
        out_shape=[s4, s4, s4, s4, s2, s2],
        compiler_params=_params("parallel"),
    )(qkv, gains2)


def _loop_blocks(nb, body, init, per_iter):
    u = math.gcd(nb, per_iter)

    def outer(i, carry):
        for k in range(u):
            carry = body(i * u + k, carry)
        return carry

    return lax.fori_loop(0, nb // u, outer, init)


def _key_window(b, nb, L, R, W):
    start = pl.multiple_of(jnp.clip(b * BQ - R, 0, L - W), HEAD_DIM)
    return start, jnp.where(b == 0, 1, jnp.where(b == nb - 1, 2, 0))


def _stack_heads(v, lo):
    z = jnp.zeros_like(v)
    return jnp.concatenate([jnp.where(lo, v, z), jnp.where(lo, z, v)], axis=0)


def _unstack_heads(v2, lo):
    return jnp.where(lo, v2[:BQ], v2[BQ:])


def _row_vector(v, lo):
    r = lax.broadcasted_iota(jnp.int32, (BQ, PAIR), 0)
    ln = lax.broadcasted_iota(jnp.int32, (BQ, PAIR), 1)
    diag = (ln % HEAD_DIM) == (r % HEAD_DIM)
    top = jnp.sum(jnp.where(diag & (r < HEAD_DIM), v, 0.0), axis=0, keepdims=True)
    bot = jnp.sum(jnp.where(diag & (r >= HEAD_DIM), v, 0.0), axis=0, keepdims=True)
    top8, bot8 = jnp.broadcast_to(top, (8, PAIR)), jnp.broadcast_to(bot, (8, PAIR))
    lo8 = _lo_mask((8, PAIR))
    head0 = jnp.where(lo8, top8, pltpu.roll(bot8, HEAD_DIM, 1))
    head1 = jnp.where(lo8, pltpu.roll(top8, HEAD_DIM, 1), bot8)
    return jnp.concatenate([head0, head1], axis=1)[:1]


def _units_per_step(nb, pairs_per_kv):
    return max(1, 16 // nb) if pairs_per_kv == 1 else 1


def _attn_fwd(q, kp, vp, bias4, sink, R, pairs_per_kv, pairs_per_bias):
    N, L, _ = q.shape
    W = BQ + 2 * R
    nb = L // BQ
    assert L >= W and nb >= 2
    G = _units_per_step(nb, pairs_per_kv)

    def body(sink_ref, q_ref, k_ref, v_ref, bias_ref, o_ref, lse_ref):
        n = pl.program_id(0)
        lo_q = _lo_mask((BQ, PAIR))
        first = lax.broadcasted_iota(jnp.int32, (2 * BQ, 1), 0) < BQ

        def blk(f, carry):
            g, b = f // nb, f % nb
            u = n * G + g
            sk = jnp.where(first, sink_ref[2 * u], sink_ref[2 * u + 1])
            q0 = pl.multiple_of(b * BQ, BQ)
            q2 = _stack_heads(q_ref[g, pl.ds(q0, BQ), :], lo_q)
            k0, variant = _key_window(b, nb, L, R, W)
            kw = k_ref[g, pl.ds(k0, W), :]
            vw = v_ref[g, pl.ds(k0, W), :]
            s = _dot_nt(q2, kw) + bias_ref[variant]
            m = jnp.maximum(jnp.max(s, axis=1, keepdims=True), sk)
            p = jnp.exp(s - m)
            l = jnp.sum(p, axis=1, keepdims=True) + jnp.exp(sk - m)
            o2 = _dot(p.astype(BF16), vw) / l
            o_ref[g, pl.ds(q0, BQ), :] = _unstack_heads(o2, lo_q)
            lse_ref[g, pl.ds(q0, BQ), :] = _unstack_heads(jnp.broadcast_to(m + jnp.log(l), (2 * BQ, PAIR)), lo_q)
            return carry

        _loop_blocks(G * nb, blk, 0, 4)

    qspec = pl.BlockSpec((G, L, PAIR), lambda n: (n, 0, 0))
    kspec = pl.BlockSpec((G, L, PAIR), lambda n: (n // pairs_per_kv, 0, 0))
    return pl.pallas_call(
        body, name="attn_fwd", grid=(N // G,),
        in_specs=[pl.BlockSpec(memory_space=pltpu.SMEM), qspec, kspec, kspec,
                  pl.BlockSpec((None, 3, 2 * BQ, W), lambda n: (n * G // pairs_per_bias, 0, 0, 0))],
        out_specs=[qspec, qspec],
        out_shape=[jax.ShapeDtypeStruct((N, L, PAIR), F32), jax.ShapeDtypeStruct((N, L, PAIR), F32)],
        compiler_params=_params("parallel"),
    )(sink, q, kp, vp, bias4)


def _attn_bwd(q, kp, vp, bias4t, sink, o, lse, do, R, pairs_per_kv, pairs_per_bias):
    N, L, _ = q.shape
    Nk = kp.shape[0]
    Pb = bias4t.shape[0]
    W = BQ + 2 * R
    nb = L // BQ
    assert L >= W and nb >= 2
    G = _units_per_step(nb, pairs_per_kv)

    def body(sink_ref, q_ref, k_ref, v_ref, bias_ref, o_ref, lse_ref, do_ref,
             dq_ref, dk_ref, dv_ref, dbias_ref, dsink_ref):
        n = pl.program_id(0)
        lo_q = _lo_mask((BQ, PAIR))
        first = lax.broadcasted_iota(jnp.int32, (1, 2 * BQ), 1) < BQ
        dsink_ref[...] = jnp.zeros_like(dsink_ref)

        @pl.when(n % pairs_per_kv == 0)
        def _():
            dk_ref[...] = jnp.zeros_like(dk_ref)
            dv_ref[...] = jnp.zeros_like(dv_ref)

        @pl.when((n * G) % pairs_per_bias == 0)
        def _():
            dbias_ref[...] = jnp.zeros_like(dbias_ref)

        def blk(f, carry):
            g, b = f // nb, f % nb
            u = n * G + g
            sk = jnp.where(first, sink_ref[2 * u], sink_ref[2 * u + 1])
            q0 = pl.multiple_of(b * BQ, BQ)
            q2 = _stack_heads(q_ref[g, pl.ds(q0, BQ), :], lo_q)
            k0, variant = _key_window(b, nb, L, R, W)
            kw = k_ref[g, pl.ds(k0, W), :]
            vw = v_ref[g, pl.ds(k0, W), :]
            dov = do_ref[g, pl.ds(q0, BQ), :]
            lse = _row_vector(lse_ref[g, pl.ds(q0, BQ), :], lo_q)
            delta = _row_vector(_half_sum(dov * o_ref[g, pl.ds(q0, BQ), :], lo_q), lo_q)
            do2 = _stack_heads(dov.astype(BF16), lo_q)
            st = _dot_nt(kw, q2) + bias_ref[variant]
            pt = jnp.exp(st - lse)
            dst = pt * (_dot_nt(vw, do2) - delta)
            dstb = dst.astype(BF16)
            dbias_ref[variant] += dst
            dk_ref[g, pl.ds(k0, W), :] += _dot(dstb, q2)
            dv_ref[g, pl.ds(k0, W), :] += _dot(pt.astype(BF16), do2)
            dq_ref[g, pl.ds(q0, BQ), :] = _unstack_heads(_dot_tn(dstb, kw), lo_q)
            dsink_ref[g, pl.ds(0, 1), :] -= jnp.exp(sk - lse) * delta
            return carry

        _loop_blocks(G * nb, blk, 0, 4)

    qspec = pl.BlockSpec((G, L, PAIR), lambda n: (n, 0, 0))
    kspec = pl.BlockSpec((G, L, PAIR), lambda n: (n // pairs_per_kv, 0, 0))
    return pl.pallas_call(
        body, name="attn_bwd", grid=(N // G,),
        in_specs=[pl.BlockSpec(memory_space=pltpu.SMEM), qspec, kspec, kspec,
                  pl.BlockSpec((None, 3, W, 2 * BQ), lambda n: (n * G // pairs_per_bias, 0, 0, 0)),
                  qspec, qspec, qspec],
        out_specs=[qspec, kspec, kspec,
                   pl.BlockSpec((None, 3, W, 2 * BQ), lambda n: (n * G // pairs_per_bias, 0, 0, 0)),
                   pl.BlockSpec((G, 8, 2 * BQ), lambda n: (n, 0, 0))],
        out_shape=[jax.ShapeDtypeStruct((N, L, PAIR), F32),
                   jax.ShapeDtypeStruct((Nk, L, PAIR), F32),
                   jax.ShapeDtypeStruct((Nk, L, PAIR), F32),
                   jax.ShapeDtypeStruct((Pb, 3, W, 2 * BQ), F32),
                   jax.ShapeDtypeStruct((N, 8, 2 * BQ), F32)],
        compiler_params=_params("arbitrary"),
    )(sink, q, kp, vp, bias4t, o, lse, do)


def _attn_merge(o1, l1, o4, l4, o16, l16, ob, tm):
    T = o1.shape[1]

    def body(o1_ref, l1_ref, o4_ref, l4_ref, o16_ref, l16_ref, ob_ref, oa_ref, la_ref, cat_ref):
        for c in range(4):
            a, b, d = l1_ref[c], l4_ref[c], l16_ref[c]
            m = jnp.maximum(jnp.maximum(a, b), d)
            wa, wb, wd = jnp.exp(a - m), jnp.exp(b - m), jnp.exp(d - m)
            z = wa + wb + wd
            o = (wa * o1_ref[c] + wb * o4_ref[c] + wd * o16_ref[c]) / z
            oa_ref[c] = o
            la_ref[c] = m + jnp.log(z)
            cat_ref[:, c * PAIR:(c + 1) * PAIR] = o.astype(BF16)
            cat_ref[:, (4 + c) * PAIR:(5 + c) * PAIR] = ob_ref[c].astype(BF16)

    four = pl.BlockSpec((4, tm, PAIR), lambda i: (0, i, 0))
    s4 = jax.ShapeDtypeStruct((4, T, PAIR), F32)
    return pl.pallas_call(
        body, name="attn_merge", grid=(T // tm,),
        in_specs=[four] * 7,
        out_specs=[four, four, pl.BlockSpec((tm, 8 * PAIR), lambda i: (i, 0))],
        out_shape=[s4, s4, jax.ShapeDtypeStruct((T, 8 * PAIR), BF16)],
        compiler_params=_params("parallel"),
    )(o1, l1, o4, l4, o16, l16, ob)


def _weight_arg(w, blk):
    if blk is None:
        return pl.BlockSpec(w.shape, lambda i: (0, 0)), (lambda ref: ref[...])
    D = w.shape[2]
    return (pl.BlockSpec((N_DEV, 128, D), lambda i: (0, blk, 0)),
            lambda ref: ref[...].reshape(N_DEV * 128, D))


def _oproj_fwd(x, o_cat, w, blk, tm):
    T, D = x.shape
    wspec, wload = _weight_arg(w, blk)

    def body(x_ref, o_ref, w_ref, out_ref):
        out_ref[...] = x_ref[...] + _dot(o_ref[...], wload(w_ref))

    tok = pl.BlockSpec((tm, D), lambda i: (i, 0))
    return pl.pallas_call(
        body, name="oproj_fwd", grid=(T // tm,),
        in_specs=[tok, pl.BlockSpec((tm, o_cat.shape[1]), lambda i: (i, 0)), wspec],
        out_specs=tok, out_shape=jax.ShapeDtypeStruct((T, D), F32),
        compiler_params=_params("parallel"),
    )(x, o_cat, w)


def _oproj_bwd(dx, w, blk, tm, dep=None):
    T, D = dx.shape
    wspec, wload = _weight_arg(w, blk)

    def body(dx_ref, w_ref, dxb_ref, do_ref):
        db = dx_ref[...].astype(BF16)
        dxb_ref[...] = db
        do = _dot_nt(db, wload(w_ref))
        for c in range(8):
            do_ref[c] = do[:, c * PAIR:(c + 1) * PAIR]

    tok = pl.BlockSpec((tm, D), lambda i: (i, 0))
    body, in_specs, args = _with_dep(body, dep, [tok, wspec], [dx, w])
    return pl.pallas_call(
        body, name="oproj_bwd", grid=(T // tm,),
        in_specs=in_specs,
        out_specs=[tok, pl.BlockSpec((8, tm, PAIR), lambda i: (0, i, 0))],
        out_shape=[jax.ShapeDtypeStruct((T, D), BF16), jax.ShapeDtypeStruct((8, T, PAIR), F32)],
        compiler_params=_params("parallel"),
    )(*args)


def _attn_post(qkv, gains2, dqa, dka, dva, dqb, dkb, dvb, tm):
    T, NQ = qkv.shape
    scale = HEAD_DIM ** -0.5

    def body(qkv_ref, g_ref, qa1, qa4, qa16, ka1, ka4, ka16, va1, va4, va16, qb_ref, kb_ref, vb_ref,
             out_ref, dg_ref):
        lo = _lo_mask((tm, PAIR))

        @pl.when(pl.program_id(0) == 0)
        def _():
            dg_ref[...] = jnp.zeros_like(dg_ref)

        def norm_bwd(c, gi, dy):
            xv = qkv_ref[:, c * PAIR:(c + 1) * PAIR]
            r = lax.rsqrt(_half_sum(xv * xv, lo) * (1.0 / HEAD_DIM) + EPS)
            xn = xv * r
            dg_ref[gi:gi + 1, :] += jnp.sum(dy * xn, axis=0, keepdims=True)
            dxn = dy * g_ref[gi:gi + 1, :]
            dx = r * (dxn - xn * (_half_sum(dxn * xn, lo) * (1.0 / HEAD_DIM)))
            out_ref[:, c * PAIR:(c + 1) * PAIR] = dx.astype(BF16)

        def fold(v):
            return v + pltpu.roll(v, HEAD_DIM, 1)

        for c in range(4):
            norm_bwd(c, 0, (qa1[c] + qa4[c] + qa16[c]) * scale)
            norm_bwd(4 + c, 1, ka1[c] + ka4[c] + ka16[c])
            out_ref[:, (8 + c) * PAIR:(9 + c) * PAIR] = (va1[c] + va4[c] + va16[c]).astype(BF16)
            norm_bwd(12 + c, 2, qb_ref[c] * scale)
        norm_bwd(16, 3, jnp.where(lo, fold(kb_ref[0]), fold(kb_ref[1])))
        out_ref[:, 17 * PAIR:18 * PAIR] = jnp.where(lo, fold(vb_ref[0]), fold(vb_ref[1])).astype(BF16)

    four = pl.BlockSpec((4, tm, PAIR), lambda i: (0, i, 0))
    two = pl.BlockSpec((2, tm, PAIR), lambda i: (0, i, 0))
    return pl.pallas_call(
        body, name="attn_post", grid=(T // tm,),
        in_specs=[pl.BlockSpec((tm, NQ), lambda i: (i, 0)), pl.BlockSpec((4, PAIR), lambda i: (0, 0))]
        + [four] * 10 + [two, two],
        out_specs=[pl.BlockSpec((tm, NQ), lambda i: (i, 0)), pl.BlockSpec((4, PAIR), lambda i: (0, 0))],
        out_shape=[jax.ShapeDtypeStruct((T, NQ), BF16), jax.ShapeDtypeStruct((4, PAIR), F32)],
        compiler_params=_params("arbitrary"),
    )(qkv, gains2, *dqa, *dka, *dva, dqb, dkb, dvb)


def _dense_norm_bwd(dres, dz, w, blk, x, g, tm):
    T, D = x.shape
    N = dz.shape[1]
    wspec, wload = _weight_arg(w, blk)

    def body(dres_ref, dz_ref, w_ref, x_ref, g_ref, dx_ref, dgn_ref):
        i = pl.program_id(0)
        dx, dg = _norm_bwd(_dot_nt(dz_ref[...], wload(w_ref)), x_ref[...], g_ref[...])
        dx_ref[...] = dres_ref[...] + dx

        @pl.when(i == 0)
        def _():
            dgn_ref[...] = dg

        @pl.when(i > 0)
        def _():
            dgn_ref[...] += dg

    tok = pl.BlockSpec((tm, D), lambda i: (i, 0))
    row = pl.BlockSpec((1, D), lambda i: (0, 0))
    return pl.pallas_call(
        body, name="dense_norm_bwd", grid=(T // tm,),
        in_specs=[tok, pl.BlockSpec((tm, N), lambda i: (i, 0)), wspec, tok, row],
        out_specs=[tok, row],
        out_shape=[jax.ShapeDtypeStruct((T, D), F32), jax.ShapeDtypeStruct((1, D), F32)],
        compiler_params=_params("arbitrary"),
    )(dres, dz, w, x, g)


def _bias_reduce(onehot, dbm):
    Hb, K = dbm.shape

    def body(oh_ref, d_ref, out_ref):
        oh = oh_ref[...]
        d = d_ref[...]
        hi = d.astype(BF16)
        r1 = d - hi.astype(F32)
        mid = r1.astype(BF16)
        low = (r1 - mid.astype(F32)).astype(BF16)
        out_ref[...] = _dot_nt(hi, oh) + _dot_nt(mid, oh) + _dot_nt(low, oh)

    vm = pl.BlockSpec(memory_space=pltpu.VMEM)
    return pl.pallas_call(
        body, name="bias_reduce", in_specs=[vm, vm], out_specs=vm,
        out_shape=jax.ShapeDtypeStruct((Hb, 128), F32),
        compiler_params=pltpu.CompilerParams(vmem_limit_bytes=VMEM_LIMIT),
    )(onehot, dbm)


def _ple_fwd(x, g, wg, blk, p, wp, target, tm):
    T, D = x.shape
    P = p.shape[1]
    with_loss = target is not None
    wspec, wload = _weight_arg(wg, blk)

    def body(*refs):
        if with_loss:
            x_ref, g_ref, wg_ref, p_ref, wp_ref, t_ref, y_ref, hn_ref, gate_ref, pp_ref, pb_ref, loss_ref = refs
        else:
            x_ref, g_ref, wg_ref, p_ref, wp_ref, y_ref, hn_ref, gate_ref, pp_ref, pb_ref = refs
        i = pl.program_id(0)
        xv = x_ref[...]
        hb = (xv * _rstd(xv) * g_ref[...]).astype(BF16)
        hn_ref[...] = hb
        gate = _sigmoid(_dot(hb, wload(wg_ref)))
        pb = p_ref[...].astype(BF16)
        pb_ref[...] = pb
        pp = _dot(pb, wp_ref[...])
        gate_ref[...] = gate
        pp_ref[...] = pp
        y = xv + gate * pp
        if with_loss:
            err = y - t_ref[...]
            y_ref[...] = err * (1.0 / D)
            part = jnp.broadcast_to(0.5 * jnp.sum(jnp.sum(err * err, axis=1, keepdims=True) * (1.0 / D),
                                                  axis=0, keepdims=True), (1, 128))

            @pl.when(i == 0)
            def _():
                loss_ref[...] = part

            @pl.when(i > 0)
            def _():
                loss_ref[...] += part
        else:
            y_ref[...] = y

    tok = pl.BlockSpec((tm, D), lambda i: (i, 0))
    ptok = pl.BlockSpec((tm, P), lambda i: (i, 0))
    in_specs = [tok, pl.BlockSpec((1, D), lambda i: (0, 0)), wspec, ptok,
                pl.BlockSpec((P, D), lambda i: (0, 0))]
    out_specs = [tok, tok, tok, tok, ptok]
    out_shape = [jax.ShapeDtypeStruct((T, D), F32), jax.ShapeDtypeStruct((T, D), BF16),
                 jax.ShapeDtypeStruct((T, D), F32), jax.ShapeDtypeStruct((T, D), F32),
                 jax.ShapeDtypeStruct((T, P), BF16)]
    args = [x, g, wg, p, wp]
    if with_loss:
        in_specs.append(tok)
        out_specs.append(pl.BlockSpec((1, 128), lambda i: (0, 0)))
        out_shape.append(jax.ShapeDtypeStruct((1, 128), F32))
        args.append(target)
    return pl.pallas_call(
        body, name="ple_fwd_loss" if with_loss else "ple_fwd", grid=(T // tm,),
        in_specs=in_specs, out_specs=out_specs, out_shape=out_shape,
        compiler_params=_params("arbitrary" if with_loss else "parallel"),
    )(*args)


def _ple_bwd(dy, gate, pp, tm, dep=None):
    T, D = dy.shape

    def body(dy_ref, gate_ref, pp_ref, dgl_ref, dpp_ref):
        d = dy_ref[...]
        gt = gate_ref[...]
        dgl_ref[...] = (d * pp_ref[...] * gt * (1.0 - gt)).astype(BF16)
        dpp_ref[...] = (d * gt).astype(BF16)

    tok = pl.BlockSpec((tm, D), lambda i: (i, 0))
    body, in_specs, args = _with_dep(body, dep, [tok, tok, tok], [dy, gate, pp])
    return pl.pallas_call(
        body, name="ple_bwd", grid=(T // tm,), in_specs=in_specs, out_specs=[tok, tok],
        out_shape=[jax.ShapeDtypeStruct((T, D), BF16), jax.ShapeDtypeStruct((T, D), BF16)],
        compiler_params=_params("parallel"),
    )(*args)


def _adamw(w, g, m, v):
    shape = w.shape
    C = shape[-1]
    w2, g2, m2, v2 = (a.reshape(-1, C) for a in (w, g, m, v))
    Rn = w2.shape[0]
    tr = Rn
    for cand in (512, 352, 256):
        if Rn % cand == 0:
            tr = cand
            break
    c1 = 1.0 - ADAM_B1 ** ADAM_STEP
    c2 = 1.0 - ADAM_B2 ** ADAM_STEP

    def body(w_ref, g_ref, m_ref, v_ref, d_ref, nm_ref, nv_ref):
        gv = g_ref[...]
        mn = ADAM_B1 * m_ref[...] + (1.0 - ADAM_B1) * gv
        vn = ADAM_B2 * v_ref[...] + (1.0 - ADAM_B2) * (gv * gv)
        d_ref[...] = -ADAM_LR * ((mn / c1) / (jnp.sqrt(vn / c2) + ADAM_EPS) + ADAM_WD * w_ref[...])
        nm_ref[...] = mn
        nv_ref[...] = vn

    spec = pl.BlockSpec((tr, C), lambda i: (i, 0))
    sh = jax.ShapeDtypeStruct((Rn, C), F32)
    d, nm, nv = pl.pallas_call(
        body, name="adamw", grid=(Rn // tr,), in_specs=[spec] * 4, out_specs=[spec] * 3, out_shape=[sh] * 3,
        compiler_params=_params("parallel"),
    )(w2, g2, m2, v2)
    return d.reshape(shape), nm.reshape(shape), nv.reshape(shape)


def _my_place():
    x, y, c = lax.axis_index("x"), lax.axis_index("y"), lax.axis_index("c")
    chips = [(1 - x, y), (x, 1 - y), (1 - x, 1 - y)]
    return x, y, c, chips


def _all_gather(flat):
    R, Wd = flat.shape

    def body(x_ref, out_ref, send_sems, recv_sems, local_sem):
        x, y, c, chips = _my_place()
        me, sibling = (x, y, c), (x, y, 1 - c)

        def rows(px, py, pc):
            return out_ref.at[4 * px + 2 * py + pc]

        def copy(k, block, to, src=None):
            return pltpu.make_async_remote_copy(
                src_ref=rows(*block) if src is None else src, dst_ref=rows(*block),
                send_sem=send_sems.at[k], recv_sem=recv_sems.at[k], device_id=to, device_id_type=MESH)

        mine = pltpu.make_async_copy(x_ref, rows(*me), local_sem)
        mine.start()
        first = [copy(0, me, sibling, src=x_ref)]
        first += [copy(1 + j, me, (*chip, c), src=x_ref) for j, chip in enumerate(chips)]
        for cp in first:
            cp.start()
        passed = [copy(4 + j, (*chip, c), sibling) for j, chip in enumerate(chips)]
        for j, chip in enumerate(chips):
            copy(1 + j, (*chip, c), me).wait_recv()
            passed[j].start()
        copy(0, sibling, me).wait_recv()
        for j, chip in enumerate(chips):
            copy(4 + j, (*chip, 1 - c), me).wait_recv()
        for cp in first + passed:
            cp.wait_send()
        mine.wait()

    return pl.pallas_call(
        body, name="all_gather",
        in_specs=[pl.BlockSpec(memory_space=pl.ANY)], out_specs=pl.BlockSpec(memory_space=pl.ANY),
        out_shape=jax.ShapeDtypeStruct((N_DEV, R, Wd), flat.dtype),
        scratch_shapes=[pltpu.SemaphoreType.DMA((7,)), pltpu.SemaphoreType.DMA((7,)), pltpu.SemaphoreType.DMA],
    )(flat)


def _reduce_scatter(gparts, tr):
    _, R, Wd = gparts.shape
    nt = R // tr

    def body(g_ref, out_ref, a_ref, p_ref, b_ref, vb, vo_b, vo_f, d2d_send, d2d_recv, ici_send, ici_recv):
        x, y, c, chips = _my_place()
        sibling = (x, y, 1 - c)
        allchips = [(x, y)] + chips

        def dev(chip, pc):
            return 4 * chip[0] + 2 * chip[1] + pc

        d2d = [pltpu.make_async_remote_copy(
            src_ref=g_ref.at[dev(q, 1 - c)], dst_ref=a_ref.at[a], send_sem=d2d_send.at[a], recv_sem=d2d_recv.at[a],
            device_id=sibling, device_id_type=MESH) for a, q in enumerate(allchips)]
        for cp in d2d:
            cp.start()

        def add_tiles(srcs, dst, vo):
            def step(t, carry):
                r = pl.ds(pl.multiple_of(t * tr, tr), tr)
                acc = None
                for s_i, src in enumerate(srcs):
                    pltpu.sync_copy(src.at[r], vb.at[s_i])
                for s_i in range(len(srcs)):
                    term = vb[s_i].astype(F32)
                    acc = term if acc is None else acc + term
                vo[...] = acc.astype(vo.dtype)
                pltpu.sync_copy(vo, dst.at[r])
                return carry

            lax.fori_loop(0, nt, step, 0)

        ici = []
        for j, q in enumerate(chips):
            d2d[j + 1].wait_recv()
            add_tiles([g_ref.at[dev(q, c)], a_ref.at[j + 1]], p_ref.at[j], vo_b)
            cp = pltpu.make_async_remote_copy(
                src_ref=p_ref.at[j], dst_ref=b_ref.at[j], send_sem=ici_send.at[j], recv_sem=ici_recv.at[j],
                device_id=(*q, c), device_id_type=MESH)
            cp.start()
            ici.append(cp)
        d2d[0].wait_recv()
        for cp in ici:
            cp.wait_recv()
        add_tiles([g_ref.at[dev((x, y), c)], a_ref.at[0], b_ref.at[0], b_ref.at[1], b_ref.at[2]], out_ref, vo_f)
        for cp in d2d + ici:
            cp.wait_send()

    hbm = pl.BlockSpec(memory_space=pl.ANY)
    out, _, _, _ = pl.pallas_call(
        body, name="reduce_scatter",
        in_specs=[hbm], out_specs=[hbm, hbm, hbm, hbm],
        out_shape=[jax.ShapeDtypeStruct((R, Wd), F32), jax.ShapeDtypeStruct((4, R, Wd), BF16),
                   jax.ShapeDtypeStruct((3, R, Wd), BF16), jax.ShapeDtypeStruct((3, R, Wd), BF16)],
        scratch_shapes=[pltpu.VMEM((5, tr, Wd), BF16), pltpu.VMEM((tr, Wd), BF16), pltpu.VMEM((tr, Wd), F32),
                        pltpu.SemaphoreType.DMA((4,)), pltpu.SemaphoreType.DMA((4,)),
                        pltpu.SemaphoreType.DMA((3,)), pltpu.SemaphoreType.DMA((3,))],
        compiler_params=pltpu.CompilerParams(vmem_limit_bytes=VMEM_LIMIT),
    )(gparts)
    return out


def _peer(x, y, c, k):
    return (x ^ ((k >> 2) & 1), y ^ ((k >> 1) & 1), c ^ (k & 1))


HBM_SPEC = pl.BlockSpec(memory_space=pltpu.HBM)
SEM_SPEC = pl.BlockSpec(memory_space=pltpu.SEMAPHORE)


def _exchange_refs(srcs, lands, m, k, x, y, c, scatter):
    peer = _peer(x, y, c, k)
    if scatter:
        return srcs[m].at[4 * peer[0] + 2 * peer[1] + peer[2]], lands[m].at[k - 1], peer
    return srcs[m], lands[m].at[4 * x + 2 * y + c], peer


def _exchange_start(arrs, land_shapes, scatter, name):
    n = len(arrs)

    def body(*refs):
        srcs, lands = refs[:n], refs[n:2 * n]
        send_sems, recv_sems = refs[2 * n], refs[2 * n + 1]
        token = refs[-1]
        x, y, c, _ = _my_place()
        for m in range(n):
            for k in range(1, N_DEV):
                src, dst, peer = _exchange_refs(srcs, lands, m, k, x, y, c, scatter)
                pltpu.make_async_remote_copy(
                    src_ref=src, dst_ref=dst, send_sem=send_sems.at[7 * m + k - 1],
                    recv_sem=recv_sems.at[7 * m + k - 1], device_id=peer, device_id_type=MESH).start()
        token[...] = jnp.zeros_like(token)

    zones = [lax.empty(s_, a.dtype) for s_, a in zip(land_shapes, arrs)]
    outs = pl.pallas_call(
        body, name=name,
        out_shape=(pltpu.SemaphoreType.DMA((7 * n,)), pltpu.SemaphoreType.DMA((7 * n,)),
                   *[pltpu.HBM(a.shape, a.dtype) for a in arrs], *[pltpu.HBM(z.shape, z.dtype) for z in zones],
                   jax.ShapeDtypeStruct((8, 128), F32)),
        in_specs=[HBM_SPEC] * (2 * n),
        out_specs=(SEM_SPEC, SEM_SPEC, *[HBM_SPEC] * (2 * n), pl.BlockSpec(memory_space=pltpu.VMEM)),
        input_output_aliases={m: 2 + m for m in range(2 * n)},
        compiler_params=pltpu.CompilerParams(has_side_effects=pltpu.SideEffectType.DATAFLOW_SIDE_EFFECTING),
    )(*[pltpu.with_memory_space_constraint(a, pltpu.HBM) for a in arrs],
      *[pltpu.with_memory_space_constraint(z, pltpu.HBM) for z in zones])
    return outs[0], outs[1], list(outs[2:2 + n]), list(outs[2 + n:2 + 2 * n]), outs[-1]


def _exchange_wait(send_sems, recv_sems, arrs, zones, after, scatter, name):
    n = len(arrs)

    def body(*refs):
        srcs, lands = refs[:n], refs[n:2 * n]
        send_sems, recv_sems = refs[2 * n], refs[2 * n + 1]
        x, y, c, _ = _my_place()
        for m in range(n):
            for k in range(1, N_DEV):
                src, dst, peer = _exchange_refs(srcs, lands, m, k, x, y, c, scatter)
                cp = pltpu.make_async_remote_copy(
                    src_ref=src, dst_ref=dst, send_sem=send_sems.at[7 * m + k - 1],
                    recv_sem=recv_sems.at[7 * m + k - 1], device_id=peer, device_id_type=MESH)
                cp.wait_send()
                cp.wait_recv()

    outs = pl.pallas_call(
        body, name=name,
        out_shape=tuple(pltpu.HBM(a.shape, a.dtype) for a in list(arrs) + list(zones)),
        in_specs=[HBM_SPEC] * (2 * n) + [SEM_SPEC, SEM_SPEC, pl.BlockSpec(memory_space=pl.ANY)],
        out_specs=tuple([HBM_SPEC] * (2 * n)),
        input_output_aliases={m: m for m in range(2 * n)},
        compiler_params=pltpu.CompilerParams(has_side_effects=pltpu.SideEffectType.DATAFLOW_SIDE_EFFECTING),
    )(*arrs, *zones, send_sems, recv_sems, after)
    return list(outs[n:])


def _sum_parts(own, parts, tr, dep=None):
    R, W = own.shape

    def body(own_ref, parts_ref, out_ref):
        acc = own_ref[...].astype(F32)
        for k in range(N_DEV - 1):
            acc = acc + parts_ref[k].astype(F32)
        out_ref[...] = acc

    in_specs = [pl.BlockSpec((tr, W), lambda i: (i, 0)), pl.BlockSpec((N_DEV - 1, tr, W), lambda i: (0, i, 0))]
    body, in_specs, args = _with_dep(body, dep, in_specs, [own, parts])
    return pl.pallas_call(
        body, name="sum_parts", grid=(R // tr,),
        in_specs=in_specs,
        out_specs=pl.BlockSpec((tr, W), lambda i: (i, 0)),
        out_shape=jax.ShapeDtypeStruct((R, W), F32),
        compiler_params=_params("parallel"),
    )(*args)


def _all_reduce_small(v, dep=None):
    Rn, Wd = v.shape

    def body(v_ref, out_ref, gat_ref, send_sems, recv_sems):
        x, y, c, _ = _my_place()
        me = 4 * x + 2 * y + c
        gat_ref[me] = v_ref[...]
        copies = []
        for k in range(1, N_DEV):
            fx, fy, fc = (k >> 2) & 1, (k >> 1) & 1, k & 1
            peer = (x ^ fx, y ^ fy, c ^ fc)
            cp = pltpu.make_async_remote_copy(
                src_ref=v_ref, dst_ref=gat_ref.at[me], send_sem=send_sems.at[k - 1], recv_sem=recv_sems.at[k - 1],
                device_id=peer, device_id_type=MESH)
            cp.start()
            copies.append(cp)
        for cp in copies:
            cp.wait_recv()
        for cp in copies:
            cp.wait_send()
        acc = gat_ref[0]
        for k in range(1, N_DEV):
            acc = acc + gat_ref[k]
        out_ref[...] = acc

    vm = pl.BlockSpec(memory_space=pltpu.VMEM)
    body, in_specs, args = _with_dep(body, dep, [vm], [v])
    return pl.pallas_call(
        body, name="all_reduce_small", in_specs=in_specs, out_specs=vm,
        out_shape=jax.ShapeDtypeStruct((Rn, Wd), F32),
        scratch_shapes=[pltpu.VMEM((N_DEV, Rn, Wd), F32), pltpu.SemaphoreType.DMA((7,)),
                        pltpu.SemaphoreType.DMA((7,))],
    )(*args)


def _t5_bucket(rel):
    half = N_BUCKETS // 2
    max_exact = half // 2
    ret = jnp.where(rel > 0, half, 0)
    n = jnp.abs(rel)
    nf = jnp.maximum(n, 1).astype(F32)
    large = max_exact + (jnp.log(nf / max_exact) / math.log(MAX_DISTANCE / max_exact)
                         * (half - max_exact)).astype(jnp.int32)
    large = jnp.minimum(large, half - 1)
    return ret + jnp.where(n < max_exact, n, large)


def _band(R, d):
    W = BQ + 2 * R
    rel = jnp.arange(W)[None, :] - R - jnp.arange(BQ)[:, None]
    return _t5_bucket(rel * d), jnp.abs(rel) <= R


def _onehot(R, d):
    bkt, in_band = _band(R, d)
    return ((bkt.reshape(1, -1) == jnp.arange(128)[:, None]) & in_band.reshape(1, -1)).astype(BF16)


def _bias_expand(table_t, onehot):
    H = table_t.shape[0]
    K = onehot.shape[1]

    def body(t_ref, oh_ref, out_ref):
        oh = oh_ref[...]
        t = t_ref[...]
        hi = t.astype(BF16)
        r1 = t - hi.astype(F32)
        mid = r1.astype(BF16)
        low = (r1 - mid.astype(F32)).astype(BF16)
        marked = _dot(jnp.ones(t.shape, BF16), oh) > 0.5
        out_ref[...] = jnp.where(marked, _dot(hi, oh) + _dot(mid, oh) + _dot(low, oh), NEG)

    vm = pl.BlockSpec(memory_space=pltpu.VMEM)
    return pl.pallas_call(
        body, name="bias_expand", in_specs=[vm, vm], out_specs=vm,
        out_shape=jax.ShapeDtypeStruct((H, K), F32),
        compiler_params=pltpu.CompilerParams(vmem_limit_bytes=VMEM_LIMIT),
    )(table_t, onehot)


def _bias_matrix(table, R, d):
    table_t = jnp.pad(table.T, ((0, 0), (0, 128 - N_BUCKETS)))
    return _bias_expand(table_t, _onehot(R, d)).reshape(table.shape[1], BQ, BQ + 2 * R)


def _bias_variants(base, R):
    H, _, W = base.shape
    fill = jnp.full((H, BQ, R), NEG, F32)
    first = jnp.concatenate([base[:, :, R:], fill], axis=2)
    last = jnp.concatenate([fill, base[:, :, :W - R]], axis=2)
    v = jnp.stack([base, first, last], axis=1)
    v = v.reshape(H // 2, 2, 3, BQ, W).transpose(0, 2, 1, 3, 4).reshape(H // 2, 3, 2 * BQ, W)
    return v, v.transpose(0, 1, 3, 2)


def _bias_grad(dbt, R, d):
    P, _, W, _ = dbt.shape
    dbt = dbt[:, 0].at[:, R:].add(dbt[:, 1, :W - R]).at[:, :W - R].add(dbt[:, 2, R:])
    dbm = dbt.reshape(P, W, 2, BQ).transpose(0, 2, 3, 1).reshape(2 * P, BQ * W)
    return _bias_reduce(_onehot(R, d), dbm)[:, :N_BUCKETS].T


def _deint(a, d):
    if d == 1:
        return a
    H, T, X = a.shape
    return a.reshape(H, T // d, d, X).transpose(0, 2, 1, 3).reshape(H * d, T // d, X)


def _reint(a, d):
    if d == 1:
        return a
    Hd, L, X = a.shape
    return a.reshape(Hd // d, d, L, X).transpose(0, 2, 1, 3).reshape(Hd // d, L * d, X)


def _tile2(gain):
    return jnp.concatenate([gain, gain])


ROW_W_O, ROW_GATE, ROW_QKV, ROW_PROJ, B_ROWS = 768, 896, 1024, 1312, 1344
BLK_W_O, BLK_GATE = ROW_W_O // 128, ROW_GATE // 128


def _pack_layer(wts, i):
    a = jnp.stack([wts["ffn1_w_in"][i], wts["ffn2_w_in"][i]])
    D = a.shape[1]
    b = jnp.concatenate([
        wts["ffn1_w_out"][i], wts["ffn2_w_out"][i],
        jnp.zeros((ROW_W_O - 2 * wts["ffn1_w_out"].shape[1], D), a.dtype),
        wts["w_o"][i], wts["w_ple_gate"][i], wts["w_qkv"][i].reshape(-1, D), wts["w_ple_proj"][i].reshape(-1, D)])
    return a, b


def _unpack_layer(sums, like):
    w_in2, b1, b2, w_in1, w_out1 = sums
    n_out, n_sq = like["ffn1_w_out"].shape[1], like["w_o"].shape[1]
    out = {}
    if w_in2 is not None:
        out.update(ffn2_w_in=w_in2, ffn2_w_out=b1[:n_out], w_ple_gate=b1[n_out:n_out + n_sq],
                   w_ple_proj=b1[n_out + n_sq:].reshape(like["w_ple_proj"].shape[1:]))
    if b2 is not None:
        out.update(w_o=b2[:n_sq], w_qkv=b2[n_sq:].reshape(like["w_qkv"].shape[1:]))
    if w_in1 is not None:
        out.update(ffn1_w_in=w_in1, ffn1_w_out=w_out1)
    return out


def _col_sharded(gb, r0, r1, rows):
    return gb[:, r0:r1].reshape(N_DEV, rows, -1).transpose(1, 0, 2).reshape(rows, -1)


def _to_col_shards(g):
    rows = g.shape[0]
    return g.reshape(rows, N_DEV, -1).transpose(1, 0, 2).reshape(N_DEV, -1, 1024)


def _layer_weights(ga, gb, p_dim):
    return dict(ga=ga, gb=gb, w_qkv=_col_sharded(gb, ROW_QKV, ROW_PROJ, ga.shape[2]),
                w_proj=_col_sharded(gb, ROW_PROJ, B_ROWS, p_dim))


def _layer_fwd(x, p, w, sm, i, target, tm, biases, dep=None):
    ga, gb = w["ga"], w["gb"]
    saved = {}
    saved["x0"] = x
    x1, saved["h1"], saved["zg1"], saved["zu1"], saved["s1"] = _ffn_fwd(
        x, sm["norm_ffn1"][i][None], ga, gb, 0, tm, dep)
    saved["x1"] = x1
    qkv, saved["hm"] = _qkv_fwd(x1, sm["norm_mix"][i][None], w["w_qkv"], tm)
    saved["qkv"] = qkv
    gains2 = jnp.stack([_tile2(sm[k][i]) for k in ("q_norm_a", "k_norm_a", "q_norm_b", "k_norm_b")])
    saved["gains2"] = gains2
    qa, ka, va, qb, kb, vb = _attn_prep(qkv, gains2, tm)
    no_sink = jnp.full((8,), NEG, F32)
    branches = []
    outs = []
    for (R, d), bias in zip(DILATED, biases[:3]):
        qd, kd, vd = _deint(qa, d), _deint(ka, d), _deint(va, d)
        sink = jnp.tile(no_sink, d)
        o, lse = _attn_fwd(qd, kd, vd, bias[0], sink, R, 1, d)
        branches.append((qd, kd, vd, bias, sink, R, d))
        outs += [_reint(o, d), _reint(lse, d)]
    bias_b = biases[3]
    sink_b = sm["sink_b"][i]
    ob, lb = _attn_fwd(qb, kb, vb, bias_b[0], sink_b, SWA_RADIUS, 2, 1)
    oa, la, o_cat = _attn_merge(*outs, ob, tm)
    saved.update(branches=branches, b=(qb, kb, vb, bias_b, sink_b), oa=oa, la=la, ob=ob, lb=lb, o_cat=o_cat)
    x2 = _oproj_fwd(x1, o_cat, gb, BLK_W_O, tm)
    saved["x2"] = x2
    x3, saved["h2"], saved["zg2"], saved["zu2"], saved["s2"] = _ffn_fwd(
        x2, sm["norm_ffn2"][i][None], ga, gb, 1, tm)
    saved["x3"] = x3
    res = _ple_fwd(x3, sm["norm_ple"][i][None], gb, BLK_GATE, p, w["w_proj"], target, tm)
    y, saved["hp"], saved["gate"], saved["pp"], saved["pb"] = res[:5]
    loss = res[5] if target is not None else None
    return y, loss, saved


def _layer_bwd(dy, w, sm, i, sv, tm, dep=None, on_ready=None, on_small=None, on_last=None):
    ga, gb = w["ga"], w["gb"]
    gs = {}
    D = dy.shape[1]
    dgl, dpp = _ple_bwd(dy, sv["gate"], sv["pp"], tm, dep)
    d_gate = _matmul_tn(sv["hp"], dgl, D, 2 * tm)
    d_proj = _matmul_tn(sv["pb"], dpp, D, 2 * tm)
    dx3, gs["norm_ple"] = _dense_norm_bwd(dy, dgl, gb, BLK_GATE, sv["x3"], sm["norm_ple"][i][None], tm)
    dx2, dyb, dzg, dzu, gs["norm_ffn2"] = _ffn_bwd(dx3, sv["x2"], sm["norm_ffn2"][i][None], sv["zg2"], sv["zu2"],
                                                   ga, gb, 1, tm)
    dwin2, dwo2 = _ffn_dw(sv["h2"], dzg, dzu, sv["s2"], dyb, 2 * tm)
    half = dwo2.shape[1] // 2
    after_ffn2 = [dwin2, jnp.concatenate([dwo2.reshape(N_DEV, half, D), d_gate.reshape(N_DEV, -1, D),
                                          _to_col_shards(d_proj)], axis=1)]
    token = None if on_ready is None else on_ready(0, after_ffn2)
    dx2b, do = _oproj_bwd(dx2, gb, BLK_W_O, tm, token)
    d_wo = _matmul_tn(sv["o_cat"], dx2b, D, 2 * tm)
    do_a, do_b = do[:4], do[4:]
    dqa, dka, dva, dbias = [], [], [], []
    for qd, kd, vd, bias, sink, R, d in sv["branches"]:
        dq, dk, dv, dbm, _ = _attn_bwd(qd, kd, vd, bias[1], sink, _deint(sv["oa"], d), _deint(sv["la"], d),
                                        _deint(do_a, d), R, 1, d)
        dqa.append(_reint(dq, d))
        dka.append(_reint(dk, d))
        dva.append(_reint(dv, d))
        dbias.append(dbm)
    qb, kb, vb, bias_b, sink_b = sv["b"]
    dqb, dkb, dvb, dbm_b, dsink = _attn_bwd(qb, kb, vb, bias_b[1], sink_b, sv["ob"], sv["lb"], do_b,
                                            SWA_RADIUS, 2, 1)
    gs["rel_bias"] = dbias + [dbm_b]
    gs["sink_b"] = jnp.sum(dsink[:, 0].reshape(-1, 2, BQ), axis=2).reshape(-1)
    dqkv, dgains2 = _attn_post(sv["qkv"], sv["gains2"], dqa, dka, dva, dqb,
                               dkb, dvb, tm // 2)
    dgains = dgains2[:, :HEAD_DIM] + dgains2[:, HEAD_DIM:]
    for k, name in enumerate(("q_norm_a", "k_norm_a", "q_norm_b", "k_norm_b")):
        gs[name] = dgains[k]
    d_qkv = _matmul_tn(sv["hm"], dqkv, dqkv.shape[1] // 2, 2 * tm)
    after_mixer = [jnp.concatenate([d_wo.reshape(N_DEV, -1, D), _to_col_shards(d_qkv)], axis=1)]
    token = None if on_ready is None else on_ready(1, after_mixer)
    dx1, gs["norm_mix"] = _dense_norm_bwd(dx2, dqkv, w["w_qkv"], None, sv["x1"], sm["norm_mix"][i][None], tm)
    g1 = sm["norm_ffn1"][i][None]
    if on_last is None:
        dx0, dyb, dzg, dzu, gs["norm_ffn1"] = _ffn_bwd(dx1, sv["x0"], g1, sv["zg1"], sv["zu1"], ga, gb, 0, tm, token)
        dwin1, dwo1 = _ffn_dw(sv["h1"], dzg, dzu, sv["s1"], dyb, 2 * tm)
        return dx0, (after_ffn2, after_mixer, [dwin1, dwo1.reshape(N_DEV, half, D)]), gs
    dyb, dzg, dzu = _ffn_bwd_dz(dx1, sv["zg1"], sv["zu1"], gb, 0, tm, token)
    dwin1, dwo1 = _ffn_dw(sv["h1"], dzg, dzu, sv["s1"], dyb, 2 * tm, on_small(gs))
    last = [dwin1, dwo1.reshape(N_DEV, half, D)]
    dx0, gs["norm_ffn1"] = _ffn_bwd_dx(dx1, sv["x0"], g1, dzg, dzu, ga, 0, tm, on_last(last))
    return dx0, (after_ffn2, after_mixer, last), gs


def _bias_matrices(rel_bias):
    biases = [_bias_variants(_bias_matrix(rel_bias[:, :8], R, d), R) for R, d in DILATED]
    biases.append(_bias_variants(_bias_matrix(rel_bias[:, 8:], SWA_RADIUS, 1), SWA_RADIUS))
    return biases


def _stack_small(per_layer):
    small = {}
    for k, v in per_layer.items():
        if k == "rel_bias":
            per_branch = [sum(parts) for parts in zip(*v.values())]
            drel_a = sum(_bias_grad(t, R, d) for t, (R, d) in zip(per_branch[:3], DILATED))
            small[k] = jnp.concatenate([drel_a, _bias_grad(per_branch[3], SWA_RADIUS, 1)], axis=1)
        else:
            small[k] = jnp.stack([v[i].reshape(-1) for i in sorted(v)])
    return small


TM = 512
SUM_TILES = (512, 512, 416, 512, 352)
LAST_GROUP = ("ffn1_w_in", "ffn1_w_out")


def _pack_small(d, extra=None):
    parts = [d[k].reshape(-1) for k in SMALL]
    if extra is not None:
        parts.append(extra.reshape(-1))
    flat = jnp.concatenate(parts)
    return jnp.pad(flat, (0, SMALL_ROWS * 128 - flat.shape[0])).reshape(SMALL_ROWS, 128)


def _unpack_small(buf, like):
    flat = buf.reshape(-1)
    out, off = {}, 0
    for k in SMALL:
        n = like[k].size
        out[k] = flat[off:off + n].reshape(like[k].shape)
        off += n
    return out, flat[off]


def kernel(x, p, rel_bias, norm_ffn1, ffn1_w_in, ffn1_w_out, norm_mix, w_qkv, q_norm_a, k_norm_a, q_norm_b, k_norm_b, sink_b, w_o, norm_ffn2, ffn2_w_in, ffn2_w_out, norm_ple, w_ple_gate, w_ple_proj, loss_target, m_rel_bias, m_norm_ffn1, m_ffn1_w_in, m_ffn1_w_out, m_norm_mix, m_w_qkv, m_q_norm_a, m_k_norm_a, m_q_norm_b, m_k_norm_b, m_sink_b, m_w_o, m_norm_ffn2, m_ffn2_w_in, m_ffn2_w_out, m_norm_ple, m_w_ple_gate, m_w_ple_proj, v_rel_bias, v_norm_ffn1, v_ffn1_w_in, v_ffn1_w_out, v_norm_mix, v_w_qkv, v_q_norm_a, v_k_norm_a, v_q_norm_b, v_k_norm_b, v_sink_b, v_w_o, v_norm_ffn2, v_ffn2_w_in, v_ffn2_w_out, v_norm_ple, v_w_ple_gate, v_w_ple_proj):
    wts = dict(rel_bias=rel_bias, norm_ffn1=norm_ffn1, ffn1_w_in=ffn1_w_in, ffn1_w_out=ffn1_w_out,
               norm_mix=norm_mix, w_qkv=w_qkv, q_norm_a=q_norm_a, k_norm_a=k_norm_a, q_norm_b=q_norm_b,
               k_norm_b=k_norm_b, sink_b=sink_b, w_o=w_o, norm_ffn2=norm_ffn2, ffn2_w_in=ffn2_w_in,
               ffn2_w_out=ffn2_w_out, norm_ple=norm_ple, w_ple_gate=w_ple_gate, w_ple_proj=w_ple_proj)
    mom = dict(rel_bias=m_rel_bias, norm_ffn1=m_norm_ffn1, ffn1_w_in=m_ffn1_w_in, ffn1_w_out=m_ffn1_w_out,
               norm_mix=m_norm_mix, w_qkv=m_w_qkv, q_norm_a=m_q_norm_a, k_norm_a=m_k_norm_a, q_norm_b=m_q_norm_b,
               k_norm_b=m_k_norm_b, sink_b=m_sink_b, w_o=m_w_o, norm_ffn2=m_norm_ffn2, ffn2_w_in=m_ffn2_w_in,
               ffn2_w_out=m_ffn2_w_out, norm_ple=m_norm_ple, w_ple_gate=m_w_ple_gate, w_ple_proj=m_w_ple_proj)
    var = dict(rel_bias=v_rel_bias, norm_ffn1=v_norm_ffn1, ffn1_w_in=v_ffn1_w_in, ffn1_w_out=v_ffn1_w_out,
               norm_mix=v_norm_mix, w_qkv=v_w_qkv, q_norm_a=v_q_norm_a, k_norm_a=v_k_norm_a, q_norm_b=v_q_norm_b,
               k_norm_b=v_k_norm_b, sink_b=v_sink_b, w_o=v_w_o, norm_ffn2=v_norm_ffn2, ffn2_w_in=v_ffn2_w_in,
               ffn2_w_out=v_ffn2_w_out, norm_ple=v_norm_ple, w_ple_gate=v_w_ple_gate, w_ple_proj=v_w_ple_proj)
    sm = {k: wts[k] for k in SMALL}
    p_dim = p.shape[-1]
    me = 4 * lax.axis_index("x") + 2 * lax.axis_index("y") + lax.axis_index("c")
    packed = []
    for i in range(2):
        a, b = _pack_layer(wts, i)
        packed.append([a.reshape(-1, a.shape[-1]).astype(BF16), b.astype(BF16)])
    a_shape = (2, ffn1_w_in.shape[1], ffn1_w_in.shape[2])

    def weights_of(zones):
        return _layer_weights(zones[0].reshape((N_DEV,) + a_shape), zones[1], p_dim)

    w0 = weights_of([_all_gather(t) for t in packed[0]])
    zone_shapes = [(N_DEV,) + t.shape for t in packed[1]]
    ssem, rsem, thru, zones, token = _exchange_start(packed[1], zone_shapes, False, "gather_start")
    biases = _bias_matrices(rel_bias)
    x1, _, sv0 = _layer_fwd(x[0], p[0, 0], w0, sm, 0, None, TM, biases, dep=token)
    zones = _exchange_wait(ssem, rsem, thru, zones, x1, False, "gather_wait")
    w1 = weights_of([lax.dynamic_update_index_in_dim(z, t, me, 0) for z, t in zip(zones, packed[1])])
    dy, loss, sv1 = _layer_fwd(x1, p[1, 0], w1, sm, 1, loss_target[0], TM, biases)

    def slots_for(arrs):
        return [(N_DEV - 1,) + t.shape[1:] for t in arrs]

    dx1, groups1, gs1 = _layer_bwd(dy, w1, sm, 1, sv1, TM)
    g1 = groups1[0] + groups1[1] + groups1[2]
    ex1 = _exchange_start(g1, slots_for(g1), True, "scatter_start")
    held = {}

    def on_ready(stage, group):
        if stage == 1:
            held["slots1"] = _exchange_wait(*ex1[:4], group[0], True, "scatter_wait")
        held[stage] = _exchange_start(group, slots_for(group), True, f"scatter_start_{stage}")
        return held[stage][4]

    def on_small(gs0):
        part = dict(gs0, norm_ffn1=jnp.zeros_like(gs1["norm_ffn1"]))
        gsmall = _stack_small({k: {0: part[k], 1: gs1[k]} for k in part})
        held["small"] = _all_reduce_small(_pack_small(gsmall, loss[0, :1]))
        return held["small"]

    def on_last(group):
        held["last"] = _exchange_start(group, slots_for(group), True, "scatter_start_2")
        return held["last"][4]

    dx, groups0, gs0 = _layer_bwd(dx1, w0, sm, 0, sv0, TM, dep=ex1[4], on_ready=on_ready, on_small=on_small,
                                  on_last=on_last)
    last = groups0[2]
    slots0 = [_exchange_wait(*held[stage][:4], last[0], True, f"scatter_wait_{stage}") for stage in (0, 1)]

    def summed(arrs, slots, tiles, dep=None):
        return [_sum_parts(lax.dynamic_index_in_dim(t, me, 0, keepdims=False), s_, tr, dep)
                for t, s_, tr in zip(arrs, slots, tiles)]

    cover = held["last"][4]
    r1 = summed(g1, held["slots1"], SUM_TILES, cover)
    r0 = summed(groups0[0], slots0[0], SUM_TILES[:2], cover) + summed(groups0[1], slots0[1], SUM_TILES[2:3], cover)

    def update(names, layers):
        for k in names:
            grads[k] = jnp.stack([layers[0][k], layers[1][k]])
            delta[k], new_m[k], new_v[k] = _adamw(wts[k], grads[k], mom[k], var[k])

    grads, delta, new_m, new_v = {}, {}, {}, {}
    layer1 = _unpack_layer(r1, wts)
    update([k for k in BIG if k not in LAST_GROUP], [_unpack_layer(r0 + [None, None], wts), layer1])

    slots_last = _exchange_wait(*held["last"][:4], delta["ffn2_w_in"], True, "scatter_wait_2")
    update(LAST_GROUP, [_unpack_layer([None, None, None] + summed(last, slots_last, SUM_TILES[3:]), wts), layer1])
    late = _all_reduce_small(gs0["norm_ffn1"].reshape(-1, 128), dep=slots_last[0])
    small_sum, loss_sum = _unpack_small(held["small"], sm)
    small_sum["norm_ffn1"] = small_sum["norm_ffn1"].at[0].add(late.reshape(-1))
    grads.update(small_sum)
    zeros = {k: jnp.zeros_like(wts[k]) for k in SMALL}
    ds, ms, vs = _adamw(_pack_small(wts), _pack_small(small_sum), _pack_small(mom), _pack_small(var))
    for packed, dst in ((ds, delta), (ms, new_m), (vs, new_v)):
        dst.update(_unpack_small(packed, zeros)[0])

    return (loss_sum, dx[None], *[grads[k] for k in WEIGHTS], *[delta[k] for k in WEIGHTS],
            *[new_m[k] for k in WEIGHTS], *[new_v[k] for k in WEIGHTS])
```

```python
import functools
import math

import jax
import jax.numpy as jnp
from jax import lax
from jax.experimental import pallas as pl
from jax.experimental.pallas import tpu as pltpu

F32 = jnp.float32
BF16 = jnp.bfloat16

N_DEV = 8
HEAD_DIM = 64
PAIR = 2 * HEAD_DIM
BQ = 128
N_BUCKETS = 32
MAX_DISTANCE = 1024
DILATED = ((64, 1), (64, 4), (64, 16))
SWA_RADIUS = 128
EPS = 1e-6
NEG = -1e30
ADAM_LR, ADAM_B1, ADAM_B2, ADAM_EPS, ADAM_WD, ADAM_STEP = 0.001, 0.9, 0.999, 1e-08, 0.01, 10
VMEM_LIMIT = 56 * 1024 * 1024
AXES = ("x", "y", "c")
MESH = pl.DeviceIdType.MESH

BIG = ("ffn1_w_in", "ffn1_w_out", "w_qkv", "w_o", "ffn2_w_in", "ffn2_w_out", "w_ple_gate", "w_ple_proj")
SMALL = ("rel_bias", "norm_ffn1", "norm_mix", "q_norm_a", "k_norm_a", "q_norm_b", "k_norm_b", "sink_b",
         "norm_ffn2", "norm_ple")
WEIGHTS = ("rel_bias", "norm_ffn1", "ffn1_w_in", "ffn1_w_out", "norm_mix", "w_qkv", "q_norm_a", "k_norm_a",
           "q_norm_b", "k_norm_b", "sink_b", "w_o", "norm_ffn2", "ffn2_w_in", "ffn2_w_out", "norm_ple",
           "w_ple_gate", "w_ple_proj")
SMALL_ROWS = 96


def _params(*sem):
    return pltpu.CompilerParams(dimension_semantics=sem, vmem_limit_bytes=VMEM_LIMIT)


def _dot(a, b):
    return jnp.dot(a, b, preferred_element_type=F32)


def _dot_nt(a, b):
    return lax.dot_general(a, b, (((1,), (1,)), ((), ())), preferred_element_type=F32)


def _dot_tn(a, b):
    return lax.dot_general(a, b, (((0,), (0,)), ((), ())), preferred_element_type=F32)


def _sigmoid(x):
    return 1.0 / (1.0 + jnp.exp(-x))


def _rstd(xv):
    return lax.rsqrt(jnp.mean(xv * xv, axis=-1, keepdims=True) + EPS)


def _norm_bwd(dh, xv, gv):
    r = _rstd(xv)
    xn = xv * r
    dg = jnp.sum(dh * xn, axis=0, keepdims=True)
    dxn = dh * gv
    dx = r * (dxn - xn * jnp.mean(dxn * xn, axis=-1, keepdims=True))
    return dx, dg


def _lo_mask(shape):
    return lax.broadcasted_iota(jnp.int32, shape, len(shape) - 1) < HEAD_DIM


def _half_sum(t, lo):
    s0 = jnp.sum(jnp.where(lo, t, 0.0), axis=1, keepdims=True)
    s1 = jnp.sum(jnp.where(lo, 0.0, t), axis=1, keepdims=True)
    return jnp.where(lo, s0, s1)


FFN_PARTS = 2


def _ffn_weight_specs(f, nj, D, C):
    return [pl.BlockSpec((None, None, D, C), lambda i, j: (j, f, 0, 0)),
            pl.BlockSpec((None, None, D, C), lambda i, j: (j + nj, f, 0, 0)),
            pl.BlockSpec((2, C // 2, D), lambda i, j: (j, f, 0))]


def _with_dep(body, dep, in_specs, args):
    if dep is None:
        return body, in_specs, args

    def body_after(dep_ref, *refs):
        body(*refs)

    return body_after, [pl.BlockSpec(memory_space=pl.ANY)] + in_specs, [dep] + args


def _ffn_fwd(x, g, ga, gb, f, tm, dep=None):
    T, D = x.shape
    nj, C = ga.shape[0] // 2, ga.shape[3]

    def body(x_ref, g_ref, wg_ref, wu_ref, wo_ref, xo_ref, h_ref, zg_ref, zu_ref, s_ref, h_scr, acc):
        j = pl.program_id(1)

        @pl.when(j == 0)
        def _():
            xv = x_ref[...]
            hb = (xv * _rstd(xv) * g_ref[...]).astype(BF16)
            h_scr[...] = hb
            h_ref[...] = hb
            acc[...] = jnp.zeros_like(acc)

        wo = wo_ref[...].reshape(C, D)
        for part in range(FFN_PARTS):
            sl = pl.ds(part * (tm // FFN_PARTS), tm // FFN_PARTS)
            hb = h_scr[sl, :]
            gt = _dot(hb, wg_ref[...])
            up = _dot(hb, wu_ref[...])
            s = (gt * _sigmoid(gt) * up).astype(BF16)
            zg_ref[sl, :] = gt.astype(BF16)
            zu_ref[sl, :] = up.astype(BF16)
            s_ref[sl, :] = s
            acc[sl, :] += _dot(s, wo)

        @pl.when(j == nj - 1)
        def _():
            xo_ref[...] = x_ref[...] + 0.5 * acc[...]

    tok = pl.BlockSpec((tm, D), lambda i, j: (i, 0))
    chunk = pl.BlockSpec((None, tm, C), lambda i, j: (j, i, 0))
    in_specs = [tok, pl.BlockSpec((1, D), lambda i, j: (0, 0))] + _ffn_weight_specs(f, nj, D, C)
    body, in_specs, args = _with_dep(body, dep, in_specs, [x, g, ga, ga, gb])
    return pl.pallas_call(
        body, name="ffn_fwd", grid=(T // tm, nj),
        in_specs=in_specs,
        out_specs=[tok, tok, chunk, chunk, chunk],
        out_shape=[jax.ShapeDtypeStruct((T, D), F32), jax.ShapeDtypeStruct((T, D), BF16),
                   jax.ShapeDtypeStruct((nj, T, C), BF16), jax.ShapeDtypeStruct((nj, T, C), BF16),
                   jax.ShapeDtypeStruct((nj, T, C), BF16)],
        scratch_shapes=[pltpu.VMEM((tm, D), BF16), pltpu.VMEM((tm, D), F32)],
        compiler_params=_params("parallel", "arbitrary"),
    )(*args)


def _ffn_bwd(dxo, x, g, zg, zu, ga, gb, f, tm, dep=None):
    T, D = x.shape
    nj, C = ga.shape[0] // 2, ga.shape[3]

    def body(dxo_ref, x_ref, g_ref, zg_ref, zu_ref, wg_ref, wu_ref, wo_ref,
             dx_ref, dy_ref, dzg_ref, dzu_ref, dgn_ref, dy_scr, acc):
        i, j = pl.program_id(0), pl.program_id(1)

        @pl.when(j == 0)
        def _():
            dyb = (0.5 * dxo_ref[...]).astype(BF16)
            dy_scr[...] = dyb
            dy_ref[...] = dyb
            acc[...] = jnp.zeros_like(acc)

        wo = wo_ref[...].reshape(C, D)
        for part in range(FFN_PARTS):
            sl = pl.ds(part * (tm // FFN_PARTS), tm // FFN_PARTS)
            ds = _dot_nt(dy_scr[sl, :], wo)
            gt = zg_ref[sl, :].astype(F32)
            up = zu_ref[sl, :].astype(F32)
            sg = _sigmoid(gt)
            dgt = (ds * up * (sg * (1.0 + gt * (1.0 - sg)))).astype(BF16)
            dup = (ds * (gt * sg)).astype(BF16)
            dzg_ref[sl, :] = dgt
            dzu_ref[sl, :] = dup
            acc[sl, :] += _dot_nt(dgt, wg_ref[...]) + _dot_nt(dup, wu_ref[...])

        @pl.when(j == nj - 1)
        def _():
            dx, dg = _norm_bwd(acc[...], x_ref[...], g_ref[...])
            dx_ref[...] = dxo_ref[...] + dx

            @pl.when(i == 0)
            def _():
                dgn_ref[...] = dg

            @pl.when(i > 0)
            def _():
                dgn_ref[...] += dg

    tok = pl.BlockSpec((tm, D), lambda i, j: (i, 0))
    chunk = pl.BlockSpec((None, tm, C), lambda i, j: (j, i, 0))
    row = pl.BlockSpec((1, D), lambda i, j: (0, 0))
    in_specs = [tok, tok, row, chunk, chunk] + _ffn_weight_specs(f, nj, D, C)
    body, in_specs, args = _with_dep(body, dep, in_specs, [dxo, x, g, zg, zu, ga, ga, gb])
    return pl.pallas_call(
        body, name="ffn_bwd", grid=(T // tm, nj),
        in_specs=in_specs,
        out_specs=[tok, tok, chunk, chunk, row],
        out_shape=[jax.ShapeDtypeStruct((T, D), F32), jax.ShapeDtypeStruct((T, D), BF16),
                   jax.ShapeDtypeStruct((nj, T, C), BF16), jax.ShapeDtypeStruct((nj, T, C), BF16),
                   jax.ShapeDtypeStruct((1, D), F32)],
        scratch_shapes=[pltpu.VMEM((tm, D), BF16), pltpu.VMEM((tm, D), F32)],
        compiler_params=_params("arbitrary", "arbitrary"),
    )(*args)


def _ffn_bwd_dz(dxo, zg, zu, gb, f, tm, dep=None):
    T, D = dxo.shape
    nj, C = zg.shape[0], zg.shape[2]

    def body(dxo_ref, zg_ref, zu_ref, wo_ref, dy_ref, dzg_ref, dzu_ref, dy_scr):
        @pl.when(pl.program_id(1) == 0)
        def _():
            dyb = (0.5 * dxo_ref[...]).astype(BF16)
            dy_scr[...] = dyb
            dy_ref[...] = dyb

        wo = wo_ref[...].reshape(C, D)
        for part in range(FFN_PARTS):
            sl = pl.ds(part * (tm // FFN_PARTS), tm // FFN_PARTS)
            ds = _dot_nt(dy_scr[sl, :], wo)
            gt = zg_ref[sl, :].astype(F32)
            up = zu_ref[sl, :].astype(F32)
            sg = _sigmoid(gt)
            dzg_ref[sl, :] = (ds * up * (sg * (1.0 + gt * (1.0 - sg)))).astype(BF16)
            dzu_ref[sl, :] = (ds * (gt * sg)).astype(BF16)

    tok = pl.BlockSpec((tm, D), lambda i, j: (i, 0))
    chunk = pl.BlockSpec((None, tm, C), lambda i, j: (j, i, 0))
    in_specs = [tok, chunk, chunk, _ffn_weight_specs(f, nj, D, C)[2]]
    body, in_specs, args = _with_dep(body, dep, in_specs, [dxo, zg, zu, gb])
    return pl.pallas_call(
        body, name="ffn_bwd_dz", grid=(T // tm, nj),
        in_specs=in_specs, out_specs=[tok, chunk, chunk],
        out_shape=[jax.ShapeDtypeStruct((T, D), BF16), jax.ShapeDtypeStruct((nj, T, C), BF16),
                   jax.ShapeDtypeStruct((nj, T, C), BF16)],
        scratch_shapes=[pltpu.VMEM((tm, D), BF16)],
        compiler_params=_params("parallel", "arbitrary"),
    )(*args)


def _ffn_bwd_dx(dxo, x, g, dzg, dzu, ga, f, tm, dep=None):
    T, D = x.shape
    nj, C = ga.shape[0] // 2, ga.shape[3]

    def body(dxo_ref, x_ref, g_ref, dzg_ref, dzu_ref, wg_ref, wu_ref, dx_ref, dgn_ref, acc):
        i, j = pl.program_id(0), pl.program_id(1)

        @pl.when(j == 0)
        def _():
            acc[...] = jnp.zeros_like(acc)

        acc[...] += _dot_nt(dzg_ref[...], wg_ref[...]) + _dot_nt(dzu_ref[...], wu_ref[...])

        @pl.when(j == nj - 1)
        def _():
            dx, dg = _norm_bwd(acc[...], x_ref[...], g_ref[...])
            dx_ref[...] = dxo_ref[...] + dx

            @pl.when(i == 0)
            def _():
                dgn_ref[...] = dg

            @pl.when(i > 0)
            def _():
                dgn_ref[...] += dg

    tok = pl.BlockSpec((tm, D), lambda i, j: (i, 0))
    chunk = pl.BlockSpec((None, tm, C), lambda i, j: (j, i, 0))
    row = pl.BlockSpec((1, D), lambda i, j: (0, 0))
    in_specs = [tok, tok, row, chunk, chunk] + _ffn_weight_specs(f, nj, D, C)[:2]
    body, in_specs, args = _with_dep(body, dep, in_specs, [dxo, x, g, dzg, dzu, ga, ga])
    return pl.pallas_call(
        body, name="ffn_bwd_dx", grid=(T // tm, nj),
        in_specs=in_specs, out_specs=[tok, row],
        out_shape=[jax.ShapeDtypeStruct((T, D), F32), jax.ShapeDtypeStruct((1, D), F32)],
        scratch_shapes=[pltpu.VMEM((tm, D), F32)],
        compiler_params=_params("arbitrary", "arbitrary"),
    )(*args)


def _ffn_dw(h, dzg, dzu, s, dy, tk, dep=None):
    T, D = h.shape
    nj, C = s.shape[0], s.shape[2]
    nk = T // tk

    def body(h_ref, dzg_ref, dzu_ref, s_ref, dy_ref, dwin_ref, dwo_ref, ag, au, ao):
        k = pl.program_id(1)

        @pl.when(k == 0)
        def _():
            ag[...] = jnp.zeros_like(ag)
            au[...] = jnp.zeros_like(au)
            ao[...] = jnp.zeros_like(ao)

        hb = h_ref[...]
        ag[...] += _dot_tn(hb, dzg_ref[...])
        au[...] += _dot_tn(hb, dzu_ref[...])
        ao[...] += _dot_tn(s_ref[...], dy_ref[...])

        @pl.when(k == nk - 1)
        def _():
            dwin_ref[0] = ag[...].astype(BF16)
            dwin_ref[1] = au[...].astype(BF16)
            dwo_ref[...] = ao[...].astype(BF16)

    tok = pl.BlockSpec((tk, D), lambda j, k: (k, 0))
    chunk = pl.BlockSpec((None, tk, C), lambda j, k: (j, k, 0))
    body, in_specs, args = _with_dep(body, dep, [tok, chunk, chunk, chunk, tok], [h, dzg, dzu, s, dy])
    dwin, dwo = pl.pallas_call(
        body, name="ffn_dw", grid=(nj, nk),
        in_specs=in_specs,
        out_specs=[pl.BlockSpec((2, None, D, C), lambda j, k: (0, j, 0, 0)),
                   pl.BlockSpec((None, C, D), lambda j, k: (j, 0, 0))],
        out_shape=[jax.ShapeDtypeStruct((2, nj, D, C), BF16), jax.ShapeDtypeStruct((nj, C, D), BF16)],
        scratch_shapes=[pltpu.VMEM((D, C), F32), pltpu.VMEM((D, C), F32), pltpu.VMEM((C, D), F32)],
        compiler_params=_params("parallel", "arbitrary"),
    )(*args)
    return dwin.reshape(2 * nj, D, C), dwo


def _matmul_tn(a, b, tn, tk):
    T, Ka = a.shape
    N = b.shape[1]
    nk = T // tk

    def body(a_ref, b_ref, o_ref, acc):
        k = pl.program_id(1)

        @pl.when(k == 0)
        def _():
            acc[...] = jnp.zeros_like(acc)

        acc[...] += _dot_tn(a_ref[...], b_ref[...])

        @pl.when(k == nk - 1)
        def _():
            o_ref[...] = acc[...].astype(BF16)

    return pl.pallas_call(
        body, name="matmul_tn", grid=(N // tn, nk),
        in_specs=[pl.BlockSpec((tk, Ka), lambda n, k: (k, 0)), pl.BlockSpec((tk, tn), lambda n, k: (k, n))],
        out_specs=pl.BlockSpec((Ka, tn), lambda n, k: (0, n)),
        out_shape=jax.ShapeDtypeStruct((Ka, N), BF16),
        scratch_shapes=[pltpu.VMEM((Ka, tn), F32)],
        compiler_params=_params("parallel", "arbitrary"),
    )(a, b)


def _qkv_fwd(x, g, w, tm):
    T, D = x.shape
    N = w.shape[1]

    def body(x_ref, g_ref, w_ref, o_ref, h_ref):
        xv = x_ref[...]
        hb = (xv * _rstd(xv) * g_ref[...]).astype(BF16)
        h_ref[...] = hb
        o_ref[...] = _dot(hb, w_ref[...])

    return pl.pallas_call(
        body, name="qkv_fwd", grid=(T // tm,),
        in_specs=[pl.BlockSpec((tm, D), lambda i: (i, 0)), pl.BlockSpec((1, D), lambda i: (0, 0)),
                  pl.BlockSpec((D, N), lambda i: (0, 0))],
        out_specs=[pl.BlockSpec((tm, N), lambda i: (i, 0)), pl.BlockSpec((tm, D), lambda i: (i, 0))],
        out_shape=[jax.ShapeDtypeStruct((T, N), F32), jax.ShapeDtypeStruct((T, D), BF16)],
        compiler_params=_params("parallel"),
    )(x, g, w)


def _attn_prep(qkv, gains2, tm):
    T = qkv.shape[0]
    scale = HEAD_DIM ** -0.5

    def body(qkv_ref, g_ref, qa_ref, ka_ref, va_ref, qb_ref, kb_ref, vb_ref):
        lo = _lo_mask((tm, PAIR))

        def normed(c, gi, mult):
            xv = qkv_ref[:, c * PAIR:(c + 1) * PAIR]
            r = lax.rsqrt(_half_sum(xv * xv, lo) * (1.0 / HEAD_DIM) + EPS)
            y = xv * r * g_ref[gi:gi + 1, :]
            return y * mult if mult != 1.0 else y

        def both_halves(v):
            sw = pltpu.roll(v, HEAD_DIM, 1)
            return jnp.where(lo, v, sw), jnp.where(lo, sw, v)

        for c in range(4):
            qa_ref[c] = normed(c, 0, scale).astype(BF16)
            ka_ref[c] = normed(4 + c, 1, 1.0).astype(BF16)
            va_ref[c] = qkv_ref[:, (8 + c) * PAIR:(9 + c) * PAIR].astype(BF16)
            qb_ref[c] = normed(12 + c, 2, scale).astype(BF16)
        k0, k1 = both_halves(normed(16, 3, 1.0))
        kb_ref[0] = k0.astype(BF16)
        kb_ref[1] = k1.astype(BF16)
        v0, v1 = both_halves(qkv_ref[:, 17 * PAIR:18 * PAIR])
        vb_ref[0] = v0.astype(BF16)
        vb_ref[1] = v1.astype(BF16)

    four = pl.BlockSpec((4, tm, PAIR), lambda i: (0, i, 0))
    two = pl.BlockSpec((2, tm, PAIR), lambda i: (0, i, 0))
    s4 = jax.ShapeDtypeStruct((4, T, PAIR), BF16)
    s2 = jax.ShapeDtypeStruct((2, T, PAIR), BF16)
    return pl.pallas_call(
        body, name="attn_prep", grid=(T // tm,),
        in_specs=[pl.BlockSpec((tm, qkv.shape[1]), lambda i: (i, 0)), pl.BlockSpec((4, PAIR), lambda i: (0, 0))],
        out_specs=[four, four, four, four, two, two],
        out_shape=[s4, s4, s4, s4, s2, s2],
        compiler_params=_params("parallel"),
    )(qkv, gains2)


def _loop_blocks(nb, body, init, per_iter):
    u = math.gcd(nb, per_iter)

    def outer(i, carry):
        for k in range(u):
            carry = body(i * u + k, carry)
        return carry

    return lax.fori_loop(0, nb // u, outer, init)


def _key_window(b, nb, L, R, W):
    start = pl.multiple_of(jnp.clip(b * BQ - R, 0, L - W), HEAD_DIM)
    return start, jnp.where(b == 0, 1, jnp.where(b == nb - 1, 2, 0))


def _stack_heads(v, lo):
    z = jnp.zeros_like(v)
    return jnp.concatenate([jnp.where(lo, v, z), jnp.where(lo, z, v)], axis=0)


def _unstack_heads(v2, lo):
    return jnp.where(lo, v2[:BQ], v2[BQ:])


def _row_vector(v, lo):
    r = lax.broadcasted_iota(jnp.int32, (BQ, PAIR), 0)
    ln = lax.broadcasted_iota(jnp.int32, (BQ, PAIR), 1)
    diag = (ln % HEAD_DIM) == (r % HEAD_DIM)
    top = jnp.sum(jnp.where(diag & (r < HEAD_DIM), v, 0.0), axis=0, keepdims=True)
    bot = jnp.sum(jnp.where(diag & (r >= HEAD_DIM), v, 0.0), axis=0, keepdims=True)
    top8, bot8 = jnp.broadcast_to(top, (8, PAIR)), jnp.broadcast_to(bot, (8, PAIR))
    lo8 = _lo_mask((8, PAIR))
    head0 = jnp.where(lo8, top8, pltpu.roll(bot8, HEAD_DIM, 1))
    head1 = jnp.where(lo8, pltpu.roll(top8, HEAD_DIM, 1), bot8)
    return jnp.concatenate([head0, head1], axis=1)[:1]


def _units_per_step(nb, pairs_per_kv):
    return max(1, 16 // nb) if pairs_per_kv == 1 else 1


def _attn_fwd(q, kp, vp, bias4, sink, R, pairs_per_kv, pairs_per_bias):
    N, L, _ = q.shape
    W = BQ + 2 * R
    nb = L // BQ
    assert L >= W and nb >= 2
    G = _units_per_step(nb, pairs_per_kv)

    def body(sink_ref, q_ref, k_ref, v_ref, bias_ref, o_ref, lse_ref):
        n = pl.program_id(0)
        lo_q = _lo_mask((BQ, PAIR))
        first = lax.broadcasted_iota(jnp.int32, (2 * BQ, 1), 0) < BQ

        def blk(f, carry):
            g, b = f // nb, f % nb
            u = n * G + g
            sk = jnp.where(first, sink_ref[2 * u], sink_ref[2 * u + 1])
            q0 = pl.multiple_of(b * BQ, BQ)
            q2 = _stack_heads(q_ref[g, pl.ds(q0, BQ), :], lo_q)
            k0, variant = _key_window(b, nb, L, R, W)
            kw = k_ref[g, pl.ds(k0, W), :]
            vw = v_ref[g, pl.ds(k0, W), :]
            s = _dot_nt(q2, kw) + bias_ref[variant]
            m = jnp.maximum(jnp.max(s, axis=1, keepdims=True), sk)
            p = jnp.exp(s - m)
            l = jnp.sum(p, axis=1, keepdims=True) + jnp.exp(sk - m)
            o2 = _dot(p.astype(BF16), vw) / l
            o_ref[g, pl.ds(q0, BQ), :] = _unstack_heads(o2, lo_q)
            lse_ref[g, pl.ds(q0, BQ), :] = _unstack_heads(jnp.broadcast_to(m + jnp.log(l), (2 * BQ, PAIR)), lo_q)
            return carry

        _loop_blocks(G * nb, blk, 0, 4)

    qspec = pl.BlockSpec((G, L, PAIR), lambda n: (n, 0, 0))
    kspec = pl.BlockSpec((G, L, PAIR), lambda n: (n // pairs_per_kv, 0, 0))
    return pl.pallas_call(
        body, name="attn_fwd", grid=(N // G,),
        in_specs=[pl.BlockSpec(memory_space=pltpu.SMEM), qspec, kspec, kspec,
                  pl.BlockSpec((None, 3, 2 * BQ, W), lambda n: (n * G // pairs_per_bias, 0, 0, 0))],
        out_specs=[qspec, qspec],
        out_shape=[jax.ShapeDtypeStruct((N, L, PAIR), F32), jax.ShapeDtypeStruct((N, L, PAIR), F32)],
        compiler_params=_params("parallel"),
    )(sink, q, kp, vp, bias4)


def _attn_bwd(q, kp, vp, bias4t, sink, o, lse, do, R, pairs_per_kv, pairs_per_bias):
    N, L, _ = q.shape
    Nk = kp.shape[0]
    Pb = bias4t.shape[0]
    W = BQ + 2 * R
    nb = L // BQ
    assert L >= W and nb >= 2
    G = _units_per_step(nb, pairs_per_kv)

    def body(sink_ref, q_ref, k_ref, v_ref, bias_ref, o_ref, lse_ref, do_ref,
             dq_ref, dk_ref, dv_ref, dbias_ref, dsink_ref, dk_acc, dv_acc):
        n = pl.program_id(0)
        lo_q = _lo_mask((BQ, PAIR))
        first = lax.broadcasted_iota(jnp.int32, (1, 2 * BQ), 1) < BQ
        dsink_ref[...] = jnp.zeros_like(dsink_ref)

        @pl.when(n % pairs_per_kv == 0)
        def _():
            dk_acc[...] = jnp.zeros_like(dk_acc)
            dv_acc[...] = jnp.zeros_like(dv_acc)

        @pl.when((n * G) % pairs_per_bias == 0)
        def _():
            dbias_ref[...] = jnp.zeros_like(dbias_ref)

        def blk(f, carry):
            g, b = f // nb, f % nb
            u = n * G + g
            sk = jnp.where(first, sink_ref[2 * u], sink_ref[2 * u + 1])
            q0 = pl.multiple_of(b * BQ, BQ)
            q2 = _stack_heads(q_ref[g, pl.ds(q0, BQ), :], lo_q)
            k0, variant = _key_window(b, nb, L, R, W)
            kw = k_ref[g, pl.ds(k0, W), :]
            vw = v_ref[g, pl.ds(k0, W), :]
            dov = do_ref[g, pl.ds(q0, BQ), :]
            lse = _row_vector(lse_ref[g, pl.ds(q0, BQ), :], lo_q)
            delta = _row_vector(_half_sum(dov.astype(F32) * o_ref[g, pl.ds(q0, BQ), :], lo_q), lo_q)
            do2 = _stack_heads(dov.astype(BF16), lo_q)
            st = _dot_nt(kw, q2) + bias_ref[variant]
            pt = jnp.exp(st - lse)
            dst = pt * (_dot_nt(vw, do2) - delta)
            dstb = dst.astype(BF16)
            dbias_ref[variant] += dst
            dk_acc[g, pl.ds(k0, W), :] += _dot(dstb, q2)
            dv_acc[g, pl.ds(k0, W), :] += _dot(pt.astype(BF16), do2)
            dq_ref[g, pl.ds(q0, BQ), :] = _unstack_heads(_dot_tn(dstb, kw), lo_q).astype(BF16)
            dsink_ref[g, pl.ds(0, 1), :] -= jnp.exp(sk - lse) * delta
            return carry

        _loop_blocks(G * nb, blk, 0, 4)
        dk_ref[...] = dk_acc[...].astype(BF16)
        dv_ref[...] = dv_acc[...].astype(BF16)

    qspec = pl.BlockSpec((G, L, PAIR), lambda n: (n, 0, 0))
    kspec = pl.BlockSpec((G, L, PAIR), lambda n: (n // pairs_per_kv, 0, 0))
    return pl.pallas_call(
        body, name="attn_bwd", grid=(N // G,),
        in_specs=[pl.BlockSpec(memory_space=pltpu.SMEM), qspec, kspec, kspec,
                  pl.BlockSpec((None, 3, W, 2 * BQ), lambda n: (n * G // pairs_per_bias, 0, 0, 0)),
                  qspec, qspec, qspec],
        out_specs=[qspec, kspec, kspec,
                   pl.BlockSpec((None, 3, W, 2 * BQ), lambda n: (n * G // pairs_per_bias, 0, 0, 0)),
                   pl.BlockSpec((G, 8, 2 * BQ), lambda n: (n, 0, 0))],
        out_shape=[jax.ShapeDtypeStruct((N, L, PAIR), BF16),
                   jax.ShapeDtypeStruct((Nk, L, PAIR), BF16),
                   jax.ShapeDtypeStruct((Nk, L, PAIR), BF16),
                   jax.ShapeDtypeStruct((Pb, 3, W, 2 * BQ), F32),
                   jax.ShapeDtypeStruct((N, 8, 2 * BQ), F32)],
        scratch_shapes=[pltpu.VMEM((G, L, PAIR), F32), pltpu.VMEM((G, L, PAIR), F32)],
        compiler_params=_params("arbitrary"),
    )(sink, q, kp, vp, bias4t, o, lse, do)


def _attn_merge(o1, l1, o4, l4, o16, l16, ob, tm):
    T = o1.shape[1]

    def body(o1_ref, l1_ref, o4_ref, l4_ref, o16_ref, l16_ref, ob_ref, oa_ref, la_ref, cat_ref):
        for c in range(4):
            a, b, d = l1_ref[c], l4_ref[c], l16_ref[c]
            m = jnp.maximum(jnp.maximum(a, b), d)
            wa, wb, wd = jnp.exp(a - m), jnp.exp(b - m), jnp.exp(d - m)
            z = wa + wb + wd
            o = (wa * o1_ref[c] + wb * o4_ref[c] + wd * o16_ref[c]) / z
            oa_ref[c] = o
            la_ref[c] = m + jnp.log(z)
            cat_ref[:, c * PAIR:(c + 1) * PAIR] = o.astype(BF16)
            cat_ref[:, (4 + c) * PAIR:(5 + c) * PAIR] = ob_ref[c].astype(BF16)

    four = pl.BlockSpec((4, tm, PAIR), lambda i: (0, i, 0))
    s4 = jax.ShapeDtypeStruct((4, T, PAIR), F32)
    return pl.pallas_call(
        body, name="attn_merge", grid=(T // tm,),
        in_specs=[four] * 7,
        out_specs=[four, four, pl.BlockSpec((tm, 8 * PAIR), lambda i: (i, 0))],
        out_shape=[s4, s4, jax.ShapeDtypeStruct((T, 8 * PAIR), BF16)],
        compiler_params=_params("parallel"),
    )(o1, l1, o4, l4, o16, l16, ob)


def _weight_arg(w, blk):
    if blk is None:
        return pl.BlockSpec(w.shape, lambda i: (0, 0)), (lambda ref: ref[...])
    D = w.shape[2]
    return (pl.BlockSpec((N_DEV, 128, D), lambda i: (0, blk, 0)),
            lambda ref: ref[...].reshape(N_DEV * 128, D))


def _oproj_fwd(x, o_cat, w, blk, tm):
    T, D = x.shape
    wspec, wload = _weight_arg(w, blk)

    def body(x_ref, o_ref, w_ref, out_ref):
        out_ref[...] = x_ref[...] + _dot(o_ref[...], wload(w_ref))

    tok = pl.BlockSpec((tm, D), lambda i: (i, 0))
    return pl.pallas_call(
        body, name="oproj_fwd", grid=(T // tm,),
        in_specs=[tok, pl.BlockSpec((tm, o_cat.shape[1]), lambda i: (i, 0)), wspec],
        out_specs=tok, out_shape=jax.ShapeDtypeStruct((T, D), F32),
        compiler_params=_params("parallel"),
    )(x, o_cat, w)


def _oproj_bwd(dx, w, blk, tm, dep=None):
    T, D = dx.shape
    wspec, wload = _weight_arg(w, blk)

    def body(dx_ref, w_ref, dxb_ref, do_ref):
        db = dx_ref[...].astype(BF16)
        dxb_ref[...] = db
        do = _dot_nt(db, wload(w_ref))
        for c in range(8):
            do_ref[c] = do[:, c * PAIR:(c + 1) * PAIR].astype(BF16)

    tok = pl.BlockSpec((tm, D), lambda i: (i, 0))
    body, in_specs, args = _with_dep(body, dep, [tok, wspec], [dx, w])
    return pl.pallas_call(
        body, name="oproj_bwd", grid=(T // tm,),
        in_specs=in_specs,
        out_specs=[tok, pl.BlockSpec((8, tm, PAIR), lambda i: (0, i, 0))],
        out_shape=[jax.ShapeDtypeStruct((T, D), BF16), jax.ShapeDtypeStruct((8, T, PAIR), BF16)],
        compiler_params=_params("parallel"),
    )(*args)


def _attn_post(qkv, gains2, dqa, dka, dva, dqb, dkb, dvb, tm):
    T, NQ = qkv.shape
    scale = HEAD_DIM ** -0.5

    def body(qkv_ref, g_ref, qa1, qa4, qa16, ka1, ka4, ka16, va1, va4, va16, qb_ref, kb_ref, vb_ref,
             out_ref, dg_ref):
        lo = _lo_mask((tm, PAIR))

        @pl.when(pl.program_id(0) == 0)
        def _():
            dg_ref[...] = jnp.zeros_like(dg_ref)

        def norm_bwd(c, gi, dy):
            xv = qkv_ref[:, c * PAIR:(c + 1) * PAIR]
            r = lax.rsqrt(_half_sum(xv * xv, lo) * (1.0 / HEAD_DIM) + EPS)
            xn = xv * r
            dg_ref[gi:gi + 1, :] += jnp.sum(dy * xn, axis=0, keepdims=True)
            dxn = dy * g_ref[gi:gi + 1, :]
            dx = r * (dxn - xn * (_half_sum(dxn * xn, lo) * (1.0 / HEAD_DIM)))
            out_ref[:, c * PAIR:(c + 1) * PAIR] = dx.astype(BF16)

        def fold(v):
            return v + pltpu.roll(v, HEAD_DIM, 1)

        def total(*refs_c):
            acc = refs_c[0].astype(F32)
            for r in refs_c[1:]:
                acc = acc + r.astype(F32)
            return acc

        for c in range(4):
            norm_bwd(c, 0, total(qa1[c], qa4[c], qa16[c]) * scale)
            norm_bwd(4 + c, 1, total(ka1[c], ka4[c], ka16[c]))
            out_ref[:, (8 + c) * PAIR:(9 + c) * PAIR] = total(va1[c], va4[c], va16[c]).astype(BF16)
            norm_bwd(12 + c, 2, total(qb_ref[c]) * scale)
        norm_bwd(16, 3, jnp.where(lo, fold(total(kb_ref[0])), fold(total(kb_ref[1]))))
        out_ref[:, 17 * PAIR:18 * PAIR] = jnp.where(lo, fold(total(vb_ref[0])), fold(total(vb_ref[1]))).astype(BF16)

    four = pl.BlockSpec((4, tm, PAIR), lambda i: (0, i, 0))
    two = pl.BlockSpec((2, tm, PAIR), lambda i: (0, i, 0))
    return pl.pallas_call(
        body, name="attn_post", grid=(T // tm,),
        in_specs=[pl.BlockSpec((tm, NQ), lambda i: (i, 0)), pl.BlockSpec((4, PAIR), lambda i: (0, 0))]
        + [four] * 10 + [two, two],
        out_specs=[pl.BlockSpec((tm, NQ), lambda i: (i, 0)), pl.BlockSpec((4, PAIR), lambda i: (0, 0))],
        out_shape=[jax.ShapeDtypeStruct((T, NQ), BF16), jax.ShapeDtypeStruct((4, PAIR), F32)],
        compiler_params=_params("arbitrary"),
    )(qkv, gains2, *dqa, *dka, *dva, dqb, dkb, dvb)


def _dense_norm_bwd(dres, dz, w, blk, x, g, tm):
    T, D = x.shape
    N = dz.shape[1]
    wspec, wload = _weight_arg(w, blk)

    def body(dres_ref, dz_ref, w_ref, x_ref, g_ref, dx_ref, dgn_ref):
        i = pl.program_id(0)
        dx, dg = _norm_bwd(_dot_nt(dz_ref[...], wload(w_ref)), x_ref[...], g_ref[...])
        dx_ref[...] = dres_ref[...] + dx

        @pl.when(i == 0)
        def _():
            dgn_ref[...] = dg

        @pl.when(i > 0)
        def _():
            dgn_ref[...] += dg

    tok = pl.BlockSpec((tm, D), lambda i: (i, 0))
    row = pl.BlockSpec((1, D), lambda i: (0, 0))
    return pl.pallas_call(
        body, name="dense_norm_bwd", grid=(T // tm,),
        in_specs=[tok, pl.BlockSpec((tm, N), lambda i: (i, 0)), wspec, tok, row],
        out_specs=[tok, row],
        out_shape=[jax.ShapeDtypeStruct((T, D), F32), jax.ShapeDtypeStruct((1, D), F32)],
        compiler_params=_params("arbitrary"),
    )(dres, dz, w, x, g)


def _bias_reduce(onehot, dbm):
    Hb, K = dbm.shape

    def body(oh_ref, d_ref, out_ref):
        oh = oh_ref[...]
        d = d_ref[...]
        hi = d.astype(BF16)
        r1 = d - hi.astype(F32)
        mid = r1.astype(BF16)
        low = (r1 - mid.astype(F32)).astype(BF16)
        out_ref[...] = _dot_nt(hi, oh) + _dot_nt(mid, oh) + _dot_nt(low, oh)

    vm = pl.BlockSpec(memory_space=pltpu.VMEM)
    return pl.pallas_call(
        body, name="bias_reduce", in_specs=[vm, vm], out_specs=vm,
        out_shape=jax.ShapeDtypeStruct((Hb, 128), F32),
        compiler_params=pltpu.CompilerParams(vmem_limit_bytes=VMEM_LIMIT),
    )(onehot, dbm)


def _ple_fwd(x, g, wg, blk, p, wp, target, tm):
    T, D = x.shape
    P = p.shape[1]
    with_loss = target is not None
    wspec, wload = _weight_arg(wg, blk)

    def body(*refs):
        if with_loss:
            x_ref, g_ref, wg_ref, p_ref, wp_ref, t_ref, y_ref, hn_ref, gate_ref, pp_ref, pb_ref, loss_ref = refs
        else:
            x_ref, g_ref, wg_ref, p_ref, wp_ref, y_ref, hn_ref, gate_ref, pp_ref, pb_ref = refs
        i = pl.program_id(0)
        xv = x_ref[...]
        hb = (xv * _rstd(xv) * g_ref[...]).astype(BF16)
        hn_ref[...] = hb
        gate = _sigmoid(_dot(hb, wload(wg_ref)))
        pb = p_ref[...].astype(BF16)
        pb_ref[...] = pb
        pp = _dot(pb, wp_ref[...])
        gate_ref[...] = gate
        pp_ref[...] = pp
        y = xv + gate * pp
        if with_loss:
            err = y - t_ref[...]
            y_ref[...] = err * (1.0 / D)
            part = jnp.broadcast_to(0.5 * jnp.sum(jnp.sum(err * err, axis=1, keepdims=True) * (1.0 / D),
                                                  axis=0, keepdims=True), (1, 128))

            @pl.when(i == 0)
            def _():
                loss_ref[...] = part

            @pl.when(i > 0)
            def _():
                loss_ref[...] += part
        else:
            y_ref[...] = y

    tok = pl.BlockSpec((tm, D), lambda i: (i, 0))
    ptok = pl.BlockSpec((tm, P), lambda i: (i, 0))
    in_specs = [tok, pl.BlockSpec((1, D), lambda i: (0, 0)), wspec, ptok,
                pl.BlockSpec((P, D), lambda i: (0, 0))]
    out_specs = [tok, tok, tok, tok, ptok]
    out_shape = [jax.ShapeDtypeStruct((T, D), F32), jax.ShapeDtypeStruct((T, D), BF16),
                 jax.ShapeDtypeStruct((T, D), F32), jax.ShapeDtypeStruct((T, D), F32),
                 jax.ShapeDtypeStruct((T, P), BF16)]
    args = [x, g, wg, p, wp]
    if with_loss:
        in_specs.append(tok)
        out_specs.append(pl.BlockSpec((1, 128), lambda i: (0, 0)))
        out_shape.append(jax.ShapeDtypeStruct((1, 128), F32))
        args.append(target)
    return pl.pallas_call(
        body, name="ple_fwd_loss" if with_loss else "ple_fwd", grid=(T // tm,),
        in_specs=in_specs, out_specs=out_specs, out_shape=out_shape,
        compiler_params=_params("arbitrary" if with_loss else "parallel"),
    )(*args)


def _ple_bwd(dy, gate, pp, tm, dep=None):
    T, D = dy.shape

    def body(dy_ref, gate_ref, pp_ref, dgl_ref, dpp_ref):
        d = dy_ref[...]
        gt = gate_ref[...]
        dgl_ref[...] = (d * pp_ref[...] * gt * (1.0 - gt)).astype(BF16)
        dpp_ref[...] = (d * gt).astype(BF16)

    tok = pl.BlockSpec((tm, D), lambda i: (i, 0))
    body, in_specs, args = _with_dep(body, dep, [tok, tok, tok], [dy, gate, pp])
    return pl.pallas_call(
        body, name="ple_bwd", grid=(T // tm,), in_specs=in_specs, out_specs=[tok, tok],
        out_shape=[jax.ShapeDtypeStruct((T, D), BF16), jax.ShapeDtypeStruct((T, D), BF16)],
        compiler_params=_params("parallel"),
    )(*args)


def _adamw(w, g, m, v):
    shape = w.shape
    C = shape[-1]
    w2, g2, m2, v2 = (a.reshape(-1, C) for a in (w, g, m, v))
    Rn = w2.shape[0]
    tr = Rn
    for cand in (512, 352, 256):
        if Rn % cand == 0:
            tr = cand
            break
    c1 = 1.0 - ADAM_B1 ** ADAM_STEP
    c2 = 1.0 - ADAM_B2 ** ADAM_STEP

    def body(w_ref, g_ref, m_ref, v_ref, d_ref, nm_ref, nv_ref):
        gv = g_ref[...]
        mn = ADAM_B1 * m_ref[...] + (1.0 - ADAM_B1) * gv
        vn = ADAM_B2 * v_ref[...] + (1.0 - ADAM_B2) * (gv * gv)
        d_ref[...] = -ADAM_LR * ((mn / c1) / (jnp.sqrt(vn / c2) + ADAM_EPS) + ADAM_WD * w_ref[...])
        nm_ref[...] = mn
        nv_ref[...] = vn

    spec = pl.BlockSpec((tr, C), lambda i: (i, 0))
    sh = jax.ShapeDtypeStruct((Rn, C), F32)
    d, nm, nv = pl.pallas_call(
        body, name="adamw", grid=(Rn // tr,), in_specs=[spec] * 4, out_specs=[spec] * 3, out_shape=[sh] * 3,
        compiler_params=_params("parallel"),
    )(w2, g2, m2, v2)
    return d.reshape(shape), nm.reshape(shape), nv.reshape(shape)


def _my_place():
    x, y, c = lax.axis_index("x"), lax.axis_index("y"), lax.axis_index("c")
    chips = [(1 - x, y), (x, 1 - y), (1 - x, 1 - y)]
    return x, y, c, chips


def _all_gather(flat):
    R, Wd = flat.shape

    def body(x_ref, out_ref, send_sems, recv_sems, local_sem):
        x, y, c, chips = _my_place()
        me, sibling = (x, y, c), (x, y, 1 - c)

        def rows(px, py, pc):
            return out_ref.at[4 * px + 2 * py + pc]

        def copy(k, block, to, src=None):
            return pltpu.make_async_remote_copy(
                src_ref=rows(*block) if src is None else src, dst_ref=rows(*block),
                send_sem=send_sems.at[k], recv_sem=recv_sems.at[k], device_id=to, device_id_type=MESH)

        mine = pltpu.make_async_copy(x_ref, rows(*me), local_sem)
        mine.start()
        first = [copy(0, me, sibling, src=x_ref)]
        first += [copy(1 + j, me, (*chip, c), src=x_ref) for j, chip in enumerate(chips)]
        for cp in first:
            cp.start()
        passed = [copy(4 + j, (*chip, c), sibling) for j, chip in enumerate(chips)]
        for j, chip in enumerate(chips):
            copy(1 + j, (*chip, c), me).wait_recv()
            passed[j].start()
        copy(0, sibling, me).wait_recv()
        for j, chip in enumerate(chips):
            copy(4 + j, (*chip, 1 - c), me).wait_recv()
        for cp in first + passed:
            cp.wait_send()
        mine.wait()

    return pl.pallas_call(
        body, name="all_gather",
        in_specs=[pl.BlockSpec(memory_space=pl.ANY)], out_specs=pl.BlockSpec(memory_space=pl.ANY),
        out_shape=jax.ShapeDtypeStruct((N_DEV, R, Wd), flat.dtype),
        scratch_shapes=[pltpu.SemaphoreType.DMA((7,)), pltpu.SemaphoreType.DMA((7,)), pltpu.SemaphoreType.DMA],
    )(flat)


def _reduce_scatter(gparts, tr):
    _, R, Wd = gparts.shape
    nt = R // tr

    def body(g_ref, out_ref, a_ref, p_ref, b_ref, vb, vo_b, vo_f, d2d_send, d2d_recv, ici_send, ici_recv):
        x, y, c, chips = _my_place()
        sibling = (x, y, 1 - c)
        allchips = [(x, y)] + chips

        def dev(chip, pc):
            return 4 * chip[0] + 2 * chip[1] + pc

        d2d = [pltpu.make_async_remote_copy(
            src_ref=g_ref.at[dev(q, 1 - c)], dst_ref=a_ref.at[a], send_sem=d2d_send.at[a], recv_sem=d2d_recv.at[a],
            device_id=sibling, device_id_type=MESH) for a, q in enumerate(allchips)]
        for cp in d2d:
            cp.start()

        def add_tiles(srcs, dst, vo):
            def step(t, carry):
                r = pl.ds(pl.multiple_of(t * tr, tr), tr)
                acc = None
                for s_i, src in enumerate(srcs):
                    pltpu.sync_copy(src.at[r], vb.at[s_i])
                for s_i in range(len(srcs)):
                    term = vb[s_i].astype(F32)
                    acc = term if acc is None else acc + term
                vo[...] = acc.astype(vo.dtype)
                pltpu.sync_copy(vo, dst.at[r])
                return carry

            lax.fori_loop(0, nt, step, 0)

        ici = []
        for j, q in enumerate(chips):
            d2d[j + 1].wait_recv()
            add_tiles([g_ref.at[dev(q, c)], a_ref.at[j + 1]], p_ref.at[j], vo_b)
            cp = pltpu.make_async_remote_copy(
                src_ref=p_ref.at[j], dst_ref=b_ref.at[j], send_sem=ici_send.at[j], recv_sem=ici_recv.at[j],
                device_id=(*q, c), device_id_type=MESH)
            cp.start()
            ici.append(cp)
        d2d[0].wait_recv()
        for cp in ici:
            cp.wait_recv()
        add_tiles([g_ref.at[dev((x, y), c)], a_ref.at[0], b_ref.at[0], b_ref.at[1], b_ref.at[2]], out_ref, vo_f)
        for cp in d2d + ici:
            cp.wait_send()

    hbm = pl.BlockSpec(memory_space=pl.ANY)
    out, _, _, _ = pl.pallas_call(
        body, name="reduce_scatter",
        in_specs=[hbm], out_specs=[hbm, hbm, hbm, hbm],
        out_shape=[jax.ShapeDtypeStruct((R, Wd), F32), jax.ShapeDtypeStruct((4, R, Wd), BF16),
                   jax.ShapeDtypeStruct((3, R, Wd), BF16), jax.ShapeDtypeStruct((3, R, Wd), BF16)],
        scratch_shapes=[pltpu.VMEM((5, tr, Wd), BF16), pltpu.VMEM((tr, Wd), BF16), pltpu.VMEM((tr, Wd), F32),
                        pltpu.SemaphoreType.DMA((4,)), pltpu.SemaphoreType.DMA((4,)),
                        pltpu.SemaphoreType.DMA((3,)), pltpu.SemaphoreType.DMA((3,))],
        compiler_params=pltpu.CompilerParams(vmem_limit_bytes=VMEM_LIMIT),
    )(gparts)
    return out


def _peer(x, y, c, k):
    return (x ^ ((k >> 2) & 1), y ^ ((k >> 1) & 1), c ^ (k & 1))


HBM_SPEC = pl.BlockSpec(memory_space=pltpu.HBM)
SEM_SPEC = pl.BlockSpec(memory_space=pltpu.SEMAPHORE)


def _exchange_refs(srcs, lands, m, k, x, y, c, scatter):
    peer = _peer(x, y, c, k)
    if scatter:
        return srcs[m].at[4 * peer[0] + 2 * peer[1] + peer[2]], lands[m].at[k - 1], peer
    return srcs[m], lands[m].at[4 * x + 2 * y + c], peer


def _exchange_start(arrs, land_shapes, scatter, name):
    n = len(arrs)

    def body(*refs):
        srcs, lands = refs[:n], refs[n:2 * n]
        send_sems, recv_sems = refs[2 * n], refs[2 * n + 1]
        token = refs[-1]
        x, y, c, _ = _my_place()
        for m in range(n):
            for k in range(1, N_DEV):
                src, dst, peer = _exchange_refs(srcs, lands, m, k, x, y, c, scatter)
                pltpu.make_async_remote_copy(
                    src_ref=src, dst_ref=dst, send_sem=send_sems.at[7 * m + k - 1],
                    recv_sem=recv_sems.at[7 * m + k - 1], device_id=peer, device_id_type=MESH).start()
        token[...] = jnp.zeros_like(token)

    zones = [lax.empty(s_, a.dtype) for s_, a in zip(land_shapes, arrs)]
    outs = pl.pallas_call(
        body, name=name,
        out_shape=(pltpu.SemaphoreType.DMA((7 * n,)), pltpu.SemaphoreType.DMA((7 * n,)),
                   *[pltpu.HBM(a.shape, a.dtype) for a in arrs], *[pltpu.HBM(z.shape, z.dtype) for z in zones],
                   jax.ShapeDtypeStruct((8, 128), F32)),
        in_specs=[HBM_SPEC] * (2 * n),
        out_specs=(SEM_SPEC, SEM_SPEC, *[HBM_SPEC] * (2 * n), pl.BlockSpec(memory_space=pltpu.VMEM)),
        input_output_aliases={m: 2 + m for m in range(2 * n)},
        compiler_params=pltpu.CompilerParams(has_side_effects=pltpu.SideEffectType.DATAFLOW_SIDE_EFFECTING),
    )(*[pltpu.with_memory_space_constraint(a, pltpu.HBM) for a in arrs],
      *[pltpu.with_memory_space_constraint(z, pltpu.HBM) for z in zones])
    return outs[0], outs[1], list(outs[2:2 + n]), list(outs[2 + n:2 + 2 * n]), outs[-1]


def _exchange_wait(send_sems, recv_sems, arrs, zones, after, scatter, name):
    n = len(arrs)

    def body(*refs):
        srcs, lands = refs[:n], refs[n:2 * n]
        send_sems, recv_sems = refs[2 * n], refs[2 * n + 1]
        x, y, c, _ = _my_place()
        for m in range(n):
            for k in range(1, N_DEV):
                src, dst, peer = _exchange_refs(srcs, lands, m, k, x, y, c, scatter)
                cp = pltpu.make_async_remote_copy(
                    src_ref=src, dst_ref=dst, send_sem=send_sems.at[7 * m + k - 1],
                    recv_sem=recv_sems.at[7 * m + k - 1], device_id=peer, device_id_type=MESH)
                cp.wait_send()
                cp.wait_recv()

    outs = pl.pallas_call(
        body, name=name,
        out_shape=tuple(pltpu.HBM(a.shape, a.dtype) for a in list(arrs) + list(zones)),
        in_specs=[HBM_SPEC] * (2 * n) + [SEM_SPEC, SEM_SPEC, pl.BlockSpec(memory_space=pl.ANY)],
        out_specs=tuple([HBM_SPEC] * (2 * n)),
        input_output_aliases={m: m for m in range(2 * n)},
        compiler_params=pltpu.CompilerParams(has_side_effects=pltpu.SideEffectType.DATAFLOW_SIDE_EFFECTING),
    )(*arrs, *zones, send_sems, recv_sems, after)
    return list(outs[n:])


def _sum_parts(own, parts, tr, dep=None):
    R, W = own.shape

    def body(own_ref, parts_ref, out_ref):
        acc = own_ref[...].astype(F32)
        for k in range(N_DEV - 1):
            acc = acc + parts_ref[k].astype(F32)
        out_ref[...] = acc

    in_specs = [pl.BlockSpec((tr, W), lambda i: (i, 0)), pl.BlockSpec((N_DEV - 1, tr, W), lambda i: (0, i, 0))]
    body, in_specs, args = _with_dep(body, dep, in_specs, [own, parts])
    return pl.pallas_call(
        body, name="sum_parts", grid=(R // tr,),
        in_specs=in_specs,
        out_specs=pl.BlockSpec((tr, W), lambda i: (i, 0)),
        out_shape=jax.ShapeDtypeStruct((R, W), F32),
        compiler_params=_params("parallel"),
    )(*args)


def _all_reduce_small(v, dep=None):
    Rn, Wd = v.shape

    def body(v_ref, out_ref, gat_ref, send_sems, recv_sems):
        x, y, c, _ = _my_place()
        me = 4 * x + 2 * y + c
        gat_ref[me] = v_ref[...]
        copies = []
        for k in range(1, N_DEV):
            fx, fy, fc = (k >> 2) & 1, (k >> 1) & 1, k & 1
            peer = (x ^ fx, y ^ fy, c ^ fc)
            cp = pltpu.make_async_remote_copy(
                src_ref=v_ref, dst_ref=gat_ref.at[me], send_sem=send_sems.at[k - 1], recv_sem=recv_sems.at[k - 1],
                device_id=peer, device_id_type=MESH)
            cp.start()
            copies.append(cp)
        for cp in copies:
            cp.wait_recv()
        for cp in copies:
            cp.wait_send()
        acc = gat_ref[0]
        for k in range(1, N_DEV):
            acc = acc + gat_ref[k]
        out_ref[...] = acc

    vm = pl.BlockSpec(memory_space=pltpu.VMEM)
    body, in_specs, args = _with_dep(body, dep, [vm], [v])
    return pl.pallas_call(
        body, name="all_reduce_small", in_specs=in_specs, out_specs=vm,
        out_shape=jax.ShapeDtypeStruct((Rn, Wd), F32),
        scratch_shapes=[pltpu.VMEM((N_DEV, Rn, Wd), F32), pltpu.SemaphoreType.DMA((7,)),
                        pltpu.SemaphoreType.DMA((7,))],
    )(*args)


def _t5_bucket(rel):
    half = N_BUCKETS // 2
    max_exact = half // 2
    ret = jnp.where(rel > 0, half, 0)
    n = jnp.abs(rel)
    nf = jnp.maximum(n, 1).astype(F32)
    large = max_exact + (jnp.log(nf / max_exact) / math.log(MAX_DISTANCE / max_exact)
                         * (half - max_exact)).astype(jnp.int32)
    large = jnp.minimum(large, half - 1)
    return ret + jnp.where(n < max_exact, n, large)


def _band(R, d):
    W = BQ + 2 * R
    rel = jnp.arange(W)[None, :] - R - jnp.arange(BQ)[:, None]
    return _t5_bucket(rel * d), jnp.abs(rel) <= R


def _onehot(R, d):
    bkt, in_band = _band(R, d)
    return ((bkt.reshape(1, -1) == jnp.arange(128)[:, None]) & in_band.reshape(1, -1)).astype(BF16)


def _bias_expand(table_t, onehot):
    H = table_t.shape[0]
    K = onehot.shape[1]

    def body(t_ref, oh_ref, out_ref):
        oh = oh_ref[...]
        t = t_ref[...]
        hi = t.astype(BF16)
        r1 = t - hi.astype(F32)
        mid = r1.astype(BF16)
        low = (r1 - mid.astype(F32)).astype(BF16)
        marked = _dot(jnp.ones(t.shape, BF16), oh) > 0.5
        out_ref[...] = jnp.where(marked, _dot(hi, oh) + _dot(mid, oh) + _dot(low, oh), NEG)

    vm = pl.BlockSpec(memory_space=pltpu.VMEM)
    return pl.pallas_call(
        body, name="bias_expand", in_specs=[vm, vm], out_specs=vm,
        out_shape=jax.ShapeDtypeStruct((H, K), F32),
        compiler_params=pltpu.CompilerParams(vmem_limit_bytes=VMEM_LIMIT),
    )(table_t, onehot)


def _bias_matrix(table, R, d):
    table_t = jnp.pad(table.T, ((0, 0), (0, 128 - N_BUCKETS)))
    return _bias_expand(table_t, _onehot(R, d)).reshape(table.shape[1], BQ, BQ + 2 * R)


def _bias_variants(base, R):
    H, _, W = base.shape
    fill = jnp.full((H, BQ, R), NEG, F32)
    first = jnp.concatenate([base[:, :, R:], fill], axis=2)
    last = jnp.concatenate([fill, base[:, :, :W - R]], axis=2)
    v = jnp.stack([base, first, last], axis=1)
    v = v.reshape(H // 2, 2, 3, BQ, W).transpose(0, 2, 1, 3, 4).reshape(H // 2, 3, 2 * BQ, W)
    return v, v.transpose(0, 1, 3, 2)


def _bias_grad(dbt, R, d):
    P, _, W, _ = dbt.shape
    dbt = dbt[:, 0].at[:, R:].add(dbt[:, 1, :W - R]).at[:, :W - R].add(dbt[:, 2, R:])
    dbm = dbt.reshape(P, W, 2, BQ).transpose(0, 2, 3, 1).reshape(2 * P, BQ * W)
    return _bias_reduce(_onehot(R, d), dbm)[:, :N_BUCKETS].T


def _deint(a, d):
    if d == 1:
        return a
    H, T, X = a.shape
    return a.reshape(H, T // d, d, X).transpose(0, 2, 1, 3).reshape(H * d, T // d, X)


def _reint(a, d):
    if d == 1:
        return a
    Hd, L, X = a.shape
    return a.reshape(Hd // d, d, L, X).transpose(0, 2, 1, 3).reshape(Hd // d, L * d, X)


def _tile2(gain):
    return jnp.concatenate([gain, gain])


ROW_W_O, ROW_GATE, ROW_QKV, ROW_PROJ, B_ROWS = 768, 896, 1024, 1312, 1344
BLK_W_O, BLK_GATE = ROW_W_O // 128, ROW_GATE // 128


def _pack_layer(wts, i):
    a = jnp.stack([wts["ffn1_w_in"][i], wts["ffn2_w_in"][i]])
    D = a.shape[1]
    b = jnp.concatenate([
        wts["ffn1_w_out"][i], wts["ffn2_w_out"][i],
        jnp.zeros((ROW_W_O - 2 * wts["ffn1_w_out"].shape[1], D), a.dtype),
        wts["w_o"][i], wts["w_ple_gate"][i], wts["w_qkv"][i].reshape(-1, D), wts["w_ple_proj"][i].reshape(-1, D)])
    return a, b


def _unpack_layer(sums, like):
    w_in2, b1, b2, w_in1, w_out1 = sums
    n_out, n_sq = like["ffn1_w_out"].shape[1], like["w_o"].shape[1]
    out = {}
    if w_in2 is not None:
        out.update(ffn2_w_in=w_in2, ffn2_w_out=b1[:n_out], w_ple_gate=b1[n_out:n_out + n_sq],
                   w_ple_proj=b1[n_out + n_sq:].reshape(like["w_ple_proj"].shape[1:]))
    if b2 is not None:
        out.update(w_o=b2[:n_sq], w_qkv=b2[n_sq:].reshape(like["w_qkv"].shape[1:]))
    if w_in1 is not None:
        out.update(ffn1_w_in=w_in1, ffn1_w_out=w_out1)
    return out


def _col_sharded(gb, r0, r1, rows):
    return gb[:, r0:r1].reshape(N_DEV, rows, -1).transpose(1, 0, 2).reshape(rows, -1)


def _to_col_shards(g):
    rows = g.shape[0]
    return g.reshape(rows, N_DEV, -1).transpose(1, 0, 2).reshape(N_DEV, -1, 1024)


def _layer_weights(ga, gb, p_dim):
    return dict(ga=ga, gb=gb, w_qkv=_col_sharded(gb, ROW_QKV, ROW_PROJ, ga.shape[2]),
                w_proj=_col_sharded(gb, ROW_PROJ, B_ROWS, p_dim))


def _layer_fwd(x, p, w, sm, i, target, tm, biases, dep=None):
    ga, gb = w["ga"], w["gb"]
    saved = {}
    saved["x0"] = x
    x1, saved["h1"], saved["zg1"], saved["zu1"], saved["s1"] = _ffn_fwd(
        x, sm["norm_ffn1"][i][None], ga, gb, 0, tm, dep)
    saved["x1"] = x1
    qkv, saved["hm"] = _qkv_fwd(x1, sm["norm_mix"][i][None], w["w_qkv"], tm)
    saved["qkv"] = qkv
    gains2 = jnp.stack([_tile2(sm[k][i]) for k in ("q_norm_a", "k_norm_a", "q_norm_b", "k_norm_b")])
    saved["gains2"] = gains2
    qa, ka, va, qb, kb, vb = _attn_prep(qkv, gains2, tm)
    no_sink = jnp.full((8,), NEG, F32)
    branches = []
    outs = []
    for (R, d), bias in zip(DILATED, biases[:3]):
        qd, kd, vd = _deint(qa, d), _deint(ka, d), _deint(va, d)
        sink = jnp.tile(no_sink, d)
        o, lse = _attn_fwd(qd, kd, vd, bias[0], sink, R, 1, d)
        branches.append((qd, kd, vd, bias, sink, R, d))
        outs += [_reint(o, d), _reint(lse, d)]
    bias_b = biases[3]
    sink_b = sm["sink_b"][i]
    ob, lb = _attn_fwd(qb, kb, vb, bias_b[0], sink_b, SWA_RADIUS, 2, 1)
    oa, la, o_cat = _attn_merge(*outs, ob, tm)
    saved.update(branches=branches, b=(qb, kb, vb, bias_b, sink_b), oa=oa, la=la, ob=ob, lb=lb, o_cat=o_cat)
    x2 = _oproj_fwd(x1, o_cat, gb, BLK_W_O, tm)
    saved["x2"] = x2
    x3, saved["h2"], saved["zg2"], saved["zu2"], saved["s2"] = _ffn_fwd(
        x2, sm["norm_ffn2"][i][None], ga, gb, 1, tm)
    saved["x3"] = x3
    res = _ple_fwd(x3, sm["norm_ple"][i][None], gb, BLK_GATE, p, w["w_proj"], target, tm)
    y, saved["hp"], saved["gate"], saved["pp"], saved["pb"] = res[:5]
    loss = res[5] if target is not None else None
    return y, loss, saved


def _layer_bwd(dy, w, sm, i, sv, tm, dep=None, on_ready=None, on_small=None, on_last=None):
    ga, gb = w["ga"], w["gb"]
    gs = {}
    D = dy.shape[1]
    dgl, dpp = _ple_bwd(dy, sv["gate"], sv["pp"], tm, dep)
    d_gate = _matmul_tn(sv["hp"], dgl, D, 2 * tm)
    d_proj = _matmul_tn(sv["pb"], dpp, D, 2 * tm)
    dx3, gs["norm_ple"] = _dense_norm_bwd(dy, dgl, gb, BLK_GATE, sv["x3"], sm["norm_ple"][i][None], tm)
    dx2, dyb, dzg, dzu, gs["norm_ffn2"] = _ffn_bwd(dx3, sv["x2"], sm["norm_ffn2"][i][None], sv["zg2"], sv["zu2"],
                                                   ga, gb, 1, tm)
    dwin2, dwo2 = _ffn_dw(sv["h2"], dzg, dzu, sv["s2"], dyb, 2 * tm)
    half = dwo2.shape[1] // 2
    after_ffn2 = [dwin2, jnp.concatenate([dwo2.reshape(N_DEV, half, D), d_gate.reshape(N_DEV, -1, D),
                                          _to_col_shards(d_proj)], axis=1)]
    token = None if on_ready is None else on_ready(0, after_ffn2)
    dx2b, do = _oproj_bwd(dx2, gb, BLK_W_O, tm, token)
    d_wo = _matmul_tn(sv["o_cat"], dx2b, D, 2 * tm)
    do_a, do_b = do[:4], do[4:]
    dqa, dka, dva, dbias = [], [], [], []
    for qd, kd, vd, bias, sink, R, d in sv["branches"]:
        dq, dk, dv, dbm, _ = _attn_bwd(qd, kd, vd, bias[1], sink, _deint(sv["oa"], d), _deint(sv["la"], d),
                                        _deint(do_a, d), R, 1, d)
        dqa.append(_reint(dq, d))
        dka.append(_reint(dk, d))
        dva.append(_reint(dv, d))
        dbias.append(dbm)
    qb, kb, vb, bias_b, sink_b = sv["b"]
    dqb, dkb, dvb, dbm_b, dsink = _attn_bwd(qb, kb, vb, bias_b[1], sink_b, sv["ob"], sv["lb"], do_b,
                                            SWA_RADIUS, 2, 1)
    gs["rel_bias"] = dbias + [dbm_b]
    gs["sink_b"] = jnp.sum(dsink[:, 0].reshape(-1, 2, BQ), axis=2).reshape(-1)
    dqkv, dgains2 = _attn_post(sv["qkv"], sv["gains2"], dqa, dka, dva, dqb,
                               dkb, dvb, tm // 2)
    dgains = dgains2[:, :HEAD_DIM] + dgains2[:, HEAD_DIM:]
    for k, name in enumerate(("q_norm_a", "k_norm_a", "q_norm_b", "k_norm_b")):
        gs[name] = dgains[k]
    d_qkv = _matmul_tn(sv["hm"], dqkv, dqkv.shape[1] // 2, 2 * tm)
    after_mixer = [jnp.concatenate([d_wo.reshape(N_DEV, -1, D), _to_col_shards(d_qkv)], axis=1)]
    token = None if on_ready is None else on_ready(1, after_mixer)
    dx1, gs["norm_mix"] = _dense_norm_bwd(dx2, dqkv, w["w_qkv"], None, sv["x1"], sm["norm_mix"][i][None], tm)
    g1 = sm["norm_ffn1"][i][None]
    if on_last is None:
        dx0, dyb, dzg, dzu, gs["norm_ffn1"] = _ffn_bwd(dx1, sv["x0"], g1, sv["zg1"], sv["zu1"], ga, gb, 0, tm, token)
        dwin1, dwo1 = _ffn_dw(sv["h1"], dzg, dzu, sv["s1"], dyb, 2 * tm)
        return dx0, (after_ffn2, after_mixer, [dwin1, dwo1.reshape(N_DEV, half, D)]), gs
    dyb, dzg, dzu = _ffn_bwd_dz(dx1, sv["zg1"], sv["zu1"], gb, 0, tm, token)
    dwin1, dwo1 = _ffn_dw(sv["h1"], dzg, dzu, sv["s1"], dyb, 2 * tm, on_small(gs))
    last = [dwin1, dwo1.reshape(N_DEV, half, D)]
    dx0, gs["norm_ffn1"] = _ffn_bwd_dx(dx1, sv["x0"], g1, dzg, dzu, ga, 0, tm, on_last(last))
    return dx0, (after_ffn2, after_mixer, last), gs


def _bias_matrices(rel_bias):
    biases = [_bias_variants(_bias_matrix(rel_bias[:, :8], R, d), R) for R, d in DILATED]
    biases.append(_bias_variants(_bias_matrix(rel_bias[:, 8:], SWA_RADIUS, 1), SWA_RADIUS))
    return biases


def _stack_small(per_layer):
    small = {}
    for k, v in per_layer.items():
        if k == "rel_bias":
            per_branch = [sum(parts) for parts in zip(*v.values())]
            drel_a = sum(_bias_grad(t, R, d) for t, (R, d) in zip(per_branch[:3], DILATED))
            small[k] = jnp.concatenate([drel_a, _bias_grad(per_branch[3], SWA_RADIUS, 1)], axis=1)
        else:
            small[k] = jnp.stack([v[i].reshape(-1) for i in sorted(v)])
    return small


TM = 512
SUM_TILES = (512, 512, 416, 512, 352)
LAST_GROUP = ("ffn1_w_in", "ffn1_w_out")


def _pack_small(d, extra=None):
    parts = [d[k].reshape(-1) for k in SMALL]
    if extra is not None:
        parts.append(extra.reshape(-1))
    flat = jnp.concatenate(parts)
    return jnp.pad(flat, (0, SMALL_ROWS * 128 - flat.shape[0])).reshape(SMALL_ROWS, 128)


def _unpack_small(buf, like):
    flat = buf.reshape(-1)
    out, off = {}, 0
    for k in SMALL:
        n = like[k].size
        out[k] = flat[off:off + n].reshape(like[k].shape)
        off += n
    return out, flat[off]


def kernel(x, p, rel_bias, norm_ffn1, ffn1_w_in, ffn1_w_out, norm_mix, w_qkv, q_norm_a, k_norm_a, q_norm_b, k_norm_b, sink_b, w_o, norm_ffn2, ffn2_w_in, ffn2_w_out, norm_ple, w_ple_gate, w_ple_proj, loss_target, m_rel_bias, m_norm_ffn1, m_ffn1_w_in, m_ffn1_w_out, m_norm_mix, m_w_qkv, m_q_norm_a, m_k_norm_a, m_q_norm_b, m_k_norm_b, m_sink_b, m_w_o, m_norm_ffn2, m_ffn2_w_in, m_ffn2_w_out, m_norm_ple, m_w_ple_gate, m_w_ple_proj, v_rel_bias, v_norm_ffn1, v_ffn1_w_in, v_ffn1_w_out, v_norm_mix, v_w_qkv, v_q_norm_a, v_k_norm_a, v_q_norm_b, v_k_norm_b, v_sink_b, v_w_o, v_norm_ffn2, v_ffn2_w_in, v_ffn2_w_out, v_norm_ple, v_w_ple_gate, v_w_ple_proj):
    wts = dict(rel_bias=rel_bias, norm_ffn1=norm_ffn1, ffn1_w_in=ffn1_w_in, ffn1_w_out=ffn1_w_out,
               norm_mix=norm_mix, w_qkv=w_qkv, q_norm_a=q_norm_a, k_norm_a=k_norm_a, q_norm_b=q_norm_b,
               k_norm_b=k_norm_b, sink_b=sink_b, w_o=w_o, norm_ffn2=norm_ffn2, ffn2_w_in=ffn2_w_in,
               ffn2_w_out=ffn2_w_out, norm_ple=norm_ple, w_ple_gate=w_ple_gate, w_ple_proj=w_ple_proj)
    mom = dict(rel_bias=m_rel_bias, norm_ffn1=m_norm_ffn1, ffn1_w_in=m_ffn1_w_in, ffn1_w_out=m_ffn1_w_out,
               norm_mix=m_norm_mix, w_qkv=m_w_qkv, q_norm_a=m_q_norm_a, k_norm_a=m_k_norm_a, q_norm_b=m_q_norm_b,
               k_norm_b=m_k_norm_b, sink_b=m_sink_b, w_o=m_w_o, norm_ffn2=m_norm_ffn2, ffn2_w_in=m_ffn2_w_in,
               ffn2_w_out=m_ffn2_w_out, norm_ple=m_norm_ple, w_ple_gate=m_w_ple_gate, w_ple_proj=m_w_ple_proj)
    var = dict(rel_bias=v_rel_bias, norm_ffn1=v_norm_ffn1, ffn1_w_in=v_ffn1_w_in, ffn1_w_out=v_ffn1_w_out,
               norm_mix=v_norm_mix, w_qkv=v_w_qkv, q_norm_a=v_q_norm_a, k_norm_a=v_k_norm_a, q_norm_b=v_q_norm_b,
               k_norm_b=v_k_norm_b, sink_b=v_sink_b, w_o=v_w_o, norm_ffn2=v_norm_ffn2, ffn2_w_in=v_ffn2_w_in,
               ffn2_w_out=v_ffn2_w_out, norm_ple=v_norm_ple, w_ple_gate=v_w_ple_gate, w_ple_proj=v_w_ple_proj)
    sm = {k: wts[k] for k in SMALL}
    p_dim = p.shape[-1]
    me = 4 * lax.axis_index("x") + 2 * lax.axis_index("y") + lax.axis_index("c")
    packed = []
    for i in range(2):
        a, b = _pack_layer(wts, i)
        packed.append([a.reshape(-1, a.shape[-1]).astype(BF16), b.astype(BF16)])
    a_shape = (2, ffn1_w_in.shape[1], ffn1_w_in.shape[2])

    def weights_of(zones):
        return _layer_weights(zones[0].reshape((N_DEV,) + a_shape), zones[1], p_dim)

    w0 = weights_of([_all_gather(t) for t in packed[0]])
    zone_shapes = [(N_DEV,) + t.shape for t in packed[1]]
    ssem, rsem, thru, zones, token = _exchange_start(packed[1], zone_shapes, False, "gather_start")
    biases = _bias_matrices(rel_bias)
    x1, _, sv0 = _layer_fwd(x[0], p[0, 0], w0, sm, 0, None, TM, biases, dep=token)
    zones = _exchange_wait(ssem, rsem, thru, zones, x1, False, "gather_wait")
    w1 = weights_of([lax.dynamic_update_index_in_dim(z, t, me, 0) for z, t in zip(zones, packed[1])])
    dy, loss, sv1 = _layer_fwd(x1, p[1, 0], w1, sm, 1, loss_target[0], TM, biases)

    def slots_for(arrs):
        return [(N_DEV - 1,) + t.shape[1:] for t in arrs]

    dx1, groups1, gs1 = _layer_bwd(dy, w1, sm, 1, sv1, TM)
    g1 = groups1[0] + groups1[1] + groups1[2]
    ex1 = _exchange_start(g1, slots_for(g1), True, "scatter_start")
    held = {}

    def on_ready(stage, group):
        if stage == 1:
            held["slots1"] = _exchange_wait(*ex1[:4], group[0], True, "scatter_wait")
        held[stage] = _exchange_start(group, slots_for(group), True, f"scatter_start_{stage}")
        return held[stage][4]

    def on_small(gs0):
        part = dict(gs0, norm_ffn1=jnp.zeros_like(gs1["norm_ffn1"]))
        gsmall = _stack_small({k: {0: part[k], 1: gs1[k]} for k in part})
        held["small"] = _all_reduce_small(_pack_small(gsmall, loss[0, :1]))
        return held["small"]

    def on_last(group):
        held["last"] = _exchange_start(group, slots_for(group), True, "scatter_start_2")
        return held["last"][4]

    dx, groups0, gs0 = _layer_bwd(dx1, w0, sm, 0, sv0, TM, dep=ex1[4], on_ready=on_ready, on_small=on_small,
                                  on_last=on_last)
    last = groups0[2]
    slots0 = [_exchange_wait(*held[stage][:4], last[0], True, f"scatter_wait_{stage}") for stage in (0, 1)]

    def summed(arrs, slots, tiles, dep=None):
        return [_sum_parts(lax.dynamic_index_in_dim(t, me, 0, keepdims=False), s_, tr, dep)
                for t, s_, tr in zip(arrs, slots, tiles)]

    cover = held["last"][4]
    r1 = summed(g1, held["slots1"], SUM_TILES, cover)
    r0 = summed(groups0[0], slots0[0], SUM_TILES[:2], cover) + summed(groups0[1], slots0[1], SUM_TILES[2:3], cover)

    def update(names, layers):
        for k in names:
            grads[k] = jnp.stack([layers[0][k], layers[1][k]])
            delta[k], new_m[k], new_v[k] = _adamw(wts[k], grads[k], mom[k], var[k])

    grads, delta, new_m, new_v = {}, {}, {}, {}
    layer1 = _unpack_layer(r1, wts)
    update([k for k in BIG if k not in LAST_GROUP], [_unpack_layer(r0 + [None, None], wts), layer1])

    slots_last = _exchange_wait(*held["last"][:4], delta["ffn2_w_in"], True, "scatter_wait_2")
    update(LAST_GROUP, [_unpack_layer([None, None, None] + summed(last, slots_last, SUM_TILES[3:]), wts), layer1])
    late = _all_reduce_small(gs0["norm_ffn1"].reshape(-1, 128), dep=slots_last[0])
    small_sum, loss_sum = _unpack_small(held["small"], sm)
    small_sum["norm_ffn1"] = small_sum["norm_ffn1"].at[0].add(late.reshape(-1))
    grads.update(small_sum)
    zeros = {k: jnp.zeros_like(wts[k]) for k in SMALL}
    ds, ms, vs = _adamw(_pack_small(wts), _pack_small(small_sum), _pack_small(mom), _pack_small(var))
    for packed, dst in ((ds, delta), (ms, new_m), (vs, new_v)):
        dst.update(_unpack_small(packed, zeros)[0])

    return (loss_sum, dx[None], *[grads[k] for k in WEIGHTS], *[delta[k] for k in WEIGHTS],
            *[new_m[k] for k in WEIGHTS], *[new_v[k] for k in WEIGHTS])
```

```python
import functools
import math

import jax
import jax.numpy as jnp
from jax import lax
from jax.experimental import pallas as pl
from jax.experimental.pallas import tpu as pltpu

F32 = jnp.float32
BF16 = jnp.bfloat16

N_DEV = 8
HEAD_DIM = 64
PAIR = 2 * HEAD_DIM
BQ = 128
N_BUCKETS = 32
MAX_DISTANCE = 1024
DILATED = ((64, 1), (64, 4), (64, 16))
SWA_RADIUS = 128
EPS = 1e-6
NEG = -1e30
ADAM_LR, ADAM_B1, ADAM_B2, ADAM_EPS, ADAM_WD, ADAM_STEP = 0.001, 0.9, 0.999, 1e-08, 0.01, 10
VMEM_LIMIT = 56 * 1024 * 1024
AXES = ("x", "y", "c")
MESH = pl.DeviceIdType.MESH

BIG = ("ffn1_w_in", "ffn1_w_out", "w_qkv", "w_o", "ffn2_w_in", "ffn2_w_out", "w_ple_gate", "w_ple_proj")
SMALL = ("rel_bias", "norm_ffn1", "norm_mix", "q_norm_a", "k_norm_a", "q_norm_b", "k_norm_b", "sink_b",
         "norm_ffn2", "norm_ple")
WEIGHTS = ("rel_bias", "norm_ffn1", "ffn1_w_in", "ffn1_w_out", "norm_mix", "w_qkv", "q_norm_a", "k_norm_a",
           "q_norm_b", "k_norm_b", "sink_b", "w_o", "norm_ffn2", "ffn2_w_in", "ffn2_w_out", "norm_ple",
           "w_ple_gate", "w_ple_proj")
SMALL_ROWS = 96


def _params(*sem):
    return pltpu.CompilerParams(dimension_semantics=sem, vmem_limit_bytes=VMEM_LIMIT)


def _dot(a, b):
    return jnp.dot(a, b, preferred_element_type=F32)


def _dot_nt(a, b):
    return lax.dot_general(a, b, (((1,), (1,)), ((), ())), preferred_element_type=F32)


def _dot_tn(a, b):
    return lax.dot_general(a, b, (((0,), (0,)), ((), ())), preferred_element_type=F32)


def _sigmoid(x):
    return 1.0 / (1.0 + jnp.exp(-x))


def _rstd(xv):
    return lax.rsqrt(jnp.mean(xv * xv, axis=-1, keepdims=True) + EPS)


def _norm_bwd(dh, xv, gv):
    r = _rstd(xv)
    xn = xv * r
    dg = jnp.sum(dh * xn, axis=0, keepdims=True)
    dxn = dh * gv
    dx = r * (dxn - xn * jnp.mean(dxn * xn, axis=-1, keepdims=True))
    return dx, dg


def _lo_mask(shape):
    return lax.broadcasted_iota(jnp.int32, shape, len(shape) - 1) < HEAD_DIM


def _half_sum(t, lo):
    s0 = jnp.sum(jnp.where(lo, t, 0.0), axis=1, keepdims=True)
    s1 = jnp.sum(jnp.where(lo, 0.0, t), axis=1, keepdims=True)
    return jnp.where(lo, s0, s1)


FFN_PARTS = 2


def _ffn_weight_specs(f, nj, D, C):
    return [pl.BlockSpec((None, None, D, C), lambda i, j: (j, f, 0, 0)),
            pl.BlockSpec((None, None, D, C), lambda i, j: (j + nj, f, 0, 0)),
            pl.BlockSpec((2, C // 2, D), lambda i, j: (j, f, 0))]


def _with_dep(body, dep, in_specs, args):
    if dep is None:
        return body, in_specs, args

    def body_after(dep_ref, *refs):
        body(*refs)

    return body_after, [pl.BlockSpec(memory_space=pl.ANY)] + in_specs, [dep] + args


def _ffn_fwd(x, g, ga, gb, f, tm, dep=None):
    T, D = x.shape
    nj, C = ga.shape[0] // 2, ga.shape[3]

    def body(x_ref, g_ref, wg_ref, wu_ref, wo_ref, xo_ref, h_ref, zg_ref, zu_ref, s_ref, h_scr, acc):
        j = pl.program_id(1)

        @pl.when(j == 0)
        def _():
            xv = x_ref[...]
            hb = (xv * _rstd(xv) * g_ref[...]).astype(BF16)
            h_scr[...] = hb
            h_ref[...] = hb
            acc[...] = jnp.zeros_like(acc)

        wo = wo_ref[...].reshape(C, D)
        for part in range(FFN_PARTS):
            sl = pl.ds(part * (tm // FFN_PARTS), tm // FFN_PARTS)
            hb = h_scr[sl, :]
            gt = _dot(hb, wg_ref[...])
            up = _dot(hb, wu_ref[...])
            s = (gt * _sigmoid(gt) * up).astype(BF16)
            zg_ref[sl, :] = gt.astype(BF16)
            zu_ref[sl, :] = up.astype(BF16)
            s_ref[sl, :] = s
            acc[sl, :] += _dot(s, wo)

        @pl.when(j == nj - 1)
        def _():
            xo_ref[...] = x_ref[...] + 0.5 * acc[...]

    tok = pl.BlockSpec((tm, D), lambda i, j: (i, 0))
    chunk = pl.BlockSpec((None, tm, C), lambda i, j: (j, i, 0))
    in_specs = [tok, pl.BlockSpec((1, D), lambda i, j: (0, 0))] + _ffn_weight_specs(f, nj, D, C)
    body, in_specs, args = _with_dep(body, dep, in_specs, [x, g, ga, ga, gb])
    return pl.pallas_call(
        body, name="ffn_fwd", grid=(T // tm, nj),
        in_specs=in_specs,
        out_specs=[tok, tok, chunk, chunk, chunk],
        out_shape=[jax.ShapeDtypeStruct((T, D), F32), jax.ShapeDtypeStruct((T, D), BF16),
                   jax.ShapeDtypeStruct((nj, T, C), BF16), jax.ShapeDtypeStruct((nj, T, C), BF16),
                   jax.ShapeDtypeStruct((nj, T, C), BF16)],
        scratch_shapes=[pltpu.VMEM((tm, D), BF16), pltpu.VMEM((tm, D), F32)],
        compiler_params=_params("parallel", "arbitrary"),
    )(*args)


def _ffn_bwd(dxo, x, g, zg, zu, ga, gb, f, tm, dep=None):
    T, D = x.shape
    nj, C = ga.shape[0] // 2, ga.shape[3]

    def body(dxo_ref, x_ref, g_ref, zg_ref, zu_ref, wg_ref, wu_ref, wo_ref,
             dx_ref, dy_ref, dzg_ref, dzu_ref, dgn_ref, dy_scr, acc):
        i, j = pl.program_id(0), pl.program_id(1)

        @pl.when(j == 0)
        def _():
            dyb = (0.5 * dxo_ref[...]).astype(BF16)
            dy_scr[...] = dyb
            dy_ref[...] = dyb
            acc[...] = jnp.zeros_like(acc)

        wo = wo_ref[...].reshape(C, D)
        for part in range(FFN_PARTS):
            sl = pl.ds(part * (tm // FFN_PARTS), tm // FFN_PARTS)
            ds = _dot_nt(dy_scr[sl, :], wo)
            gt = zg_ref[sl, :].astype(F32)
            up = zu_ref[sl, :].astype(F32)
            sg = _sigmoid(gt)
            dgt = (ds * up * (sg * (1.0 + gt * (1.0 - sg)))).astype(BF16)
            dup = (ds * (gt * sg)).astype(BF16)
            dzg_ref[sl, :] = dgt
            dzu_ref[sl, :] = dup
            acc[sl, :] += _dot_nt(dgt, wg_ref[...]) + _dot_nt(dup, wu_ref[...])

        @pl.when(j == nj - 1)
        def _():
            dx, dg = _norm_bwd(acc[...], x_ref[...], g_ref[...])
            dx_ref[...] = dxo_ref[...] + dx

            @pl.when(i == 0)
            def _():
                dgn_ref[...] = dg

            @pl.when(i > 0)
            def _():
                dgn_ref[...] += dg

    tok = pl.BlockSpec((tm, D), lambda i, j: (i, 0))
    chunk = pl.BlockSpec((None, tm, C), lambda i, j: (j, i, 0))
    row = pl.BlockSpec((1, D), lambda i, j: (0, 0))
    in_specs = [tok, tok, row, chunk, chunk] + _ffn_weight_specs(f, nj, D, C)
    body, in_specs, args = _with_dep(body, dep, in_specs, [dxo, x, g, zg, zu, ga, ga, gb])
    return pl.pallas_call(
        body, name="ffn_bwd", grid=(T // tm, nj),
        in_specs=in_specs,
        out_specs=[tok, tok, chunk, chunk, row],
        out_shape=[jax.ShapeDtypeStruct((T, D), F32), jax.ShapeDtypeStruct((T, D), BF16),
                   jax.ShapeDtypeStruct((nj, T, C), BF16), jax.ShapeDtypeStruct((nj, T, C), BF16),
                   jax.ShapeDtypeStruct((1, D), F32)],
        scratch_shapes=[pltpu.VMEM((tm, D), BF16), pltpu.VMEM((tm, D), F32)],
        compiler_params=_params("arbitrary", "arbitrary"),
    )(*args)


def _ffn_bwd_dz(dxo, zg, zu, gb, f, tm, dep=None):
    T, D = dxo.shape
    nj, C = zg.shape[0], zg.shape[2]

    def body(dxo_ref, zg_ref, zu_ref, wo_ref, dy_ref, dzg_ref, dzu_ref, dy_scr):
        @pl.when(pl.program_id(1) == 0)
        def _():
            dyb = (0.5 * dxo_ref[...]).astype(BF16)
            dy_scr[...] = dyb
            dy_ref[...] = dyb

        wo = wo_ref[...].reshape(C, D)
        for part in range(FFN_PARTS):
            sl = pl.ds(part * (tm // FFN_PARTS), tm // FFN_PARTS)
            ds = _dot_nt(dy_scr[sl, :], wo)
            gt = zg_ref[sl, :].astype(F32)
            up = zu_ref[sl, :].astype(F32)
            sg = _sigmoid(gt)
            dzg_ref[sl, :] = (ds * up * (sg * (1.0 + gt * (1.0 - sg)))).astype(BF16)
            dzu_ref[sl, :] = (ds * (gt * sg)).astype(BF16)

    tok = pl.BlockSpec((tm, D), lambda i, j: (i, 0))
    chunk = pl.BlockSpec((None, tm, C), lambda i, j: (j, i, 0))
    in_specs = [tok, chunk, chunk, _ffn_weight_specs(f, nj, D, C)[2]]
    body, in_specs, args = _with_dep(body, dep, in_specs, [dxo, zg, zu, gb])
    return pl.pallas_call(
        body, name="ffn_bwd_dz", grid=(T // tm, nj),
        in_specs=in_specs, out_specs=[tok, chunk, chunk],
        out_shape=[jax.ShapeDtypeStruct((T, D), BF16), jax.ShapeDtypeStruct((nj, T, C), BF16),
                   jax.ShapeDtypeStruct((nj, T, C), BF16)],
        scratch_shapes=[pltpu.VMEM((tm, D), BF16)],
        compiler_params=_params("parallel", "arbitrary"),
    )(*args)


def _ffn_bwd_dx(dxo, x, g, dzg, dzu, ga, f, tm, dep=None):
    T, D = x.shape
    nj, C = ga.shape[0] // 2, ga.shape[3]

    def body(dxo_ref, x_ref, g_ref, dzg_ref, dzu_ref, wg_ref, wu_ref, dx_ref, dgn_ref, acc):
        i, j = pl.program_id(0), pl.program_id(1)

        @pl.when(j == 0)
        def _():
            acc[...] = jnp.zeros_like(acc)

        acc[...] += _dot_nt(dzg_ref[...], wg_ref[...]) + _dot_nt(dzu_ref[...], wu_ref[...])

        @pl.when(j == nj - 1)
        def _():
            dx, dg = _norm_bwd(acc[...], x_ref[...], g_ref[...])
            dx_ref[...] = dxo_ref[...] + dx

            @pl.when(i == 0)
            def _():
                dgn_ref[...] = dg

            @pl.when(i > 0)
            def _():
                dgn_ref[...] += dg

    tok = pl.BlockSpec((tm, D), lambda i, j: (i, 0))
    chunk = pl.BlockSpec((None, tm, C), lambda i, j: (j, i, 0))
    row = pl.BlockSpec((1, D), lambda i, j: (0, 0))
    in_specs = [tok, tok, row, chunk, chunk] + _ffn_weight_specs(f, nj, D, C)[:2]
    body, in_specs, args = _with_dep(body, dep, in_specs, [dxo, x, g, dzg, dzu, ga, ga])
    return pl.pallas_call(
        body, name="ffn_bwd_dx", grid=(T // tm, nj),
        in_specs=in_specs, out_specs=[tok, row],
        out_shape=[jax.ShapeDtypeStruct((T, D), F32), jax.ShapeDtypeStruct((1, D), F32)],
        scratch_shapes=[pltpu.VMEM((tm, D), F32)],
        compiler_params=_params("arbitrary", "arbitrary"),
    )(*args)


def _ffn_dw(h, dzg, dzu, s, dy, tk, dep=None):
    T, D = h.shape
    nj, C = s.shape[0], s.shape[2]
    nk = T // tk

    def body(h_ref, dzg_ref, dzu_ref, s_ref, dy_ref, dwin_ref, dwo_ref, ag, au, ao):
        k = pl.program_id(1)

        @pl.when(k == 0)
        def _():
            ag[...] = jnp.zeros_like(ag)
            au[...] = jnp.zeros_like(au)
            ao[...] = jnp.zeros_like(ao)

        hb = h_ref[...]
        ag[...] += _dot_tn(hb, dzg_ref[...])
        au[...] += _dot_tn(hb, dzu_ref[...])
        ao[...] += _dot_tn(s_ref[...], dy_ref[...])

        @pl.when(k == nk - 1)
        def _():
            dwin_ref[0] = ag[...].astype(BF16)
            dwin_ref[1] = au[...].astype(BF16)
            dwo_ref[...] = ao[...].astype(BF16)

    tok = pl.BlockSpec((tk, D), lambda j, k: (k, 0))
    chunk = pl.BlockSpec((None, tk, C), lambda j, k: (j, k, 0))
    body, in_specs, args = _with_dep(body, dep, [tok, chunk, chunk, chunk, tok], [h, dzg, dzu, s, dy])
    dwin, dwo = pl.pallas_call(
        body, name="ffn_dw", grid=(nj, nk),
        in_specs=in_specs,
        out_specs=[pl.BlockSpec((2, None, D, C), lambda j, k: (0, j, 0, 0)),
                   pl.BlockSpec((None, C, D), lambda j, k: (j, 0, 0))],
        out_shape=[jax.ShapeDtypeStruct((2, nj, D, C), BF16), jax.ShapeDtypeStruct((nj, C, D), BF16)],
        scratch_shapes=[pltpu.VMEM((D, C), F32), pltpu.VMEM((D, C), F32), pltpu.VMEM((C, D), F32)],
        compiler_params=_params("parallel", "arbitrary"),
    )(*args)
    return dwin.reshape(2 * nj, D, C), dwo


def _matmul_tn(a, b, tn, tk):
    T, Ka = a.shape
    N = b.shape[1]
    nk = T // tk

    def body(a_ref, b_ref, o_ref, acc):
        k = pl.program_id(1)

        @pl.when(k == 0)
        def _():
            acc[...] = jnp.zeros_like(acc)

        acc[...] += _dot_tn(a_ref[...], b_ref[...])

        @pl.when(k == nk - 1)
        def _():
            o_ref[...] = acc[...].astype(BF16)

    return pl.pallas_call(
        body, name="matmul_tn", grid=(N // tn, nk),
        in_specs=[pl.BlockSpec((tk, Ka), lambda n, k: (k, 0)), pl.BlockSpec((tk, tn), lambda n, k: (k, n))],
        out_specs=pl.BlockSpec((Ka, tn), lambda n, k: (0, n)),
        out_shape=jax.ShapeDtypeStruct((Ka, N), BF16),
        scratch_shapes=[pltpu.VMEM((Ka, tn), F32)],
        compiler_params=_params("parallel", "arbitrary"),
    )(a, b)


def _qkv_fwd(x, g, w, tm):
    T, D = x.shape
    N = w.shape[1]

    def body(x_ref, g_ref, w_ref, o_ref, h_ref):
        xv = x_ref[...]
        hb = (xv * _rstd(xv) * g_ref[...]).astype(BF16)
        h_ref[...] = hb
        o_ref[...] = _dot(hb, w_ref[...])

    return pl.pallas_call(
        body, name="qkv_fwd", grid=(T // tm,),
        in_specs=[pl.BlockSpec((tm, D), lambda i: (i, 0)), pl.BlockSpec((1, D), lambda i: (0, 0)),
                  pl.BlockSpec((D, N), lambda i: (0, 0))],
        out_specs=[pl.BlockSpec((tm, N), lambda i: (i, 0)), pl.BlockSpec((tm, D), lambda i: (i, 0))],
        out_shape=[jax.ShapeDtypeStruct((T, N), F32), jax.ShapeDtypeStruct((T, D), BF16)],
        compiler_params=_params("parallel"),
    )(x, g, w)


DILS = tuple(d for _, d in DILATED)


def _spread_specs(tm, T, dtype):
    specs = [pl.BlockSpec((4, d, tm // d, PAIR), lambda i: (0, 0, i, 0)) for d in DILS]
    shapes = [jax.ShapeDtypeStruct((4, d, T // d, PAIR), dtype) for d in DILS]
    return specs, shapes


def _spread(tile, y, outs, c, dtype):
    tm = y.shape[0]
    tile[...] = y
    for out, d in zip(outs, DILS):
        for r in range(d):
            out[c, r] = tile[pl.ds(r, tm // d, stride=d), :].astype(dtype)


def _collect(tile, ins, c):
    tm = tile.shape[0]
    first = True
    for ref, d in zip(ins, DILS):
        for r in range(d):
            rows = pl.ds(r, tm // d, stride=d) if d > 1 else pl.ds(0, tm)
            part = ref[c, r].astype(F32)
            tile[rows, :] = part if first else tile[rows, :] + part
        first = False
    return tile[...]


def _attn_prep(qkv, gains2, tm):
    T = qkv.shape[0]
    scale = HEAD_DIM ** -0.5
    n = len(DILS)

    def body(qkv_ref, g_ref, qb_ref, kb_ref, vb_ref, *rest):
        outs, tile = rest[:-1], rest[-1]
        lo = _lo_mask((tm, PAIR))

        def spread(kind, c, y):
            _spread(tile, y, outs[kind * n:(kind + 1) * n], c, BF16)

        def normed(c, gi, mult):
            xv = qkv_ref[:, c * PAIR:(c + 1) * PAIR]
            r = lax.rsqrt(_half_sum(xv * xv, lo) * (1.0 / HEAD_DIM) + EPS)
            y = xv * r * g_ref[gi:gi + 1, :]
            return y * mult if mult != 1.0 else y

        def both_halves(v):
            sw = pltpu.roll(v, HEAD_DIM, 1)
            return jnp.where(lo, v, sw), jnp.where(lo, sw, v)

        for c in range(4):
            spread(0, c, normed(c, 0, scale))
            spread(1, c, normed(4 + c, 1, 1.0))
            spread(2, c, qkv_ref[:, (8 + c) * PAIR:(9 + c) * PAIR])
            qb_ref[c] = normed(12 + c, 2, scale).astype(BF16)
        k0, k1 = both_halves(normed(16, 3, 1.0))
        kb_ref[0] = k0.astype(BF16)
        kb_ref[1] = k1.astype(BF16)
        v0, v1 = both_halves(qkv_ref[:, 17 * PAIR:18 * PAIR])
        vb_ref[0] = v0.astype(BF16)
        vb_ref[1] = v1.astype(BF16)

    four = pl.BlockSpec((4, tm, PAIR), lambda i: (0, i, 0))
    two = pl.BlockSpec((2, tm, PAIR), lambda i: (0, i, 0))
    s4 = jax.ShapeDtypeStruct((4, T, PAIR), BF16)
    s2 = jax.ShapeDtypeStruct((2, T, PAIR), BF16)
    specs, shapes = _spread_specs(tm, T, BF16)
    res = pl.pallas_call(
        body, name="attn_prep", grid=(T // tm,),
        in_specs=[pl.BlockSpec((tm, qkv.shape[1]), lambda i: (i, 0)), pl.BlockSpec((4, PAIR), lambda i: (0, 0))],
        out_specs=[four, two, two] + specs * 3,
        out_shape=[s4, s2, s2] + shapes * 3,
        scratch_shapes=[pltpu.VMEM((tm, PAIR), F32)],
        compiler_params=_params("parallel"),
    )(qkv, gains2)
    qb, kb, vb = res[:3]
    per_d = [tuple(res[3 + kind * n + di].reshape(4 * d, T // d, PAIR) for kind in range(3))
             for di, d in enumerate(DILS)]
    return qb, kb, vb, per_d


def _loop_blocks(nb, body, init, per_iter):
    u = math.gcd(nb, per_iter)

    def outer(i, carry):
        for k in range(u):
            carry = body(i * u + k, carry)
        return carry

    return lax.fori_loop(0, nb // u, outer, init)


def _key_window(b, nb, L, R, W):
    start = pl.multiple_of(jnp.clip(b * BQ - R, 0, L - W), HEAD_DIM)
    return start, jnp.where(b == 0, 1, jnp.where(b == nb - 1, 2, 0))


def _stack_heads(v, lo):
    z = jnp.zeros_like(v)
    return jnp.concatenate([jnp.where(lo, v, z), jnp.where(lo, z, v)], axis=0)


def _unstack_heads(v2, lo):
    return jnp.where(lo, v2[:BQ], v2[BQ:])


def _row_vector(v, lo):
    r = lax.broadcasted_iota(jnp.int32, (BQ, PAIR), 0)
    ln = lax.broadcasted_iota(jnp.int32, (BQ, PAIR), 1)
    diag = (ln % HEAD_DIM) == (r % HEAD_DIM)
    top = jnp.sum(jnp.where(diag & (r < HEAD_DIM), v, 0.0), axis=0, keepdims=True)
    bot = jnp.sum(jnp.where(diag & (r >= HEAD_DIM), v, 0.0), axis=0, keepdims=True)
    top8, bot8 = jnp.broadcast_to(top, (8, PAIR)), jnp.broadcast_to(bot, (8, PAIR))
    lo8 = _lo_mask((8, PAIR))
    head0 = jnp.where(lo8, top8, pltpu.roll(bot8, HEAD_DIM, 1))
    head1 = jnp.where(lo8, pltpu.roll(top8, HEAD_DIM, 1), bot8)
    return jnp.concatenate([head0, head1], axis=1)[:1]


def _units_per_step(nb, pairs_per_kv):
    return max(1, 16 // nb) if pairs_per_kv == 1 else 1


def _attn_fwd(q, kp, vp, bias4, sink, R, pairs_per_kv, pairs_per_bias):
    N, L, _ = q.shape
    W = BQ + 2 * R
    nb = L // BQ
    assert L >= W and nb >= 2
    G = _units_per_step(nb, pairs_per_kv)

    def body(sink_ref, q_ref, k_ref, v_ref, bias_ref, o_ref, lse_ref):
        n = pl.program_id(0)
        lo_q = _lo_mask((BQ, PAIR))
        first = lax.broadcasted_iota(jnp.int32, (2 * BQ, 1), 0) < BQ

        def blk(f, carry):
            g, b = f // nb, f % nb
            u = n * G + g
            sk = jnp.where(first, sink_ref[2 * u], sink_ref[2 * u + 1])
            q0 = pl.multiple_of(b * BQ, BQ)
            q2 = _stack_heads(q_ref[g, pl.ds(q0, BQ), :], lo_q)
            k0, variant = _key_window(b, nb, L, R, W)
            kw = k_ref[g, pl.ds(k0, W), :]
            vw = v_ref[g, pl.ds(k0, W), :]
            s = _dot_nt(q2, kw) + bias_ref[variant]
            m = jnp.maximum(jnp.max(s, axis=1, keepdims=True), sk)
            p = jnp.exp(s - m)
            l = jnp.sum(p, axis=1, keepdims=True) + jnp.exp(sk - m)
            o2 = _dot(p.astype(BF16), vw) / l
            o_ref[g, pl.ds(q0, BQ), :] = _unstack_heads(o2, lo_q)
            lse_ref[g, pl.ds(q0, BQ), :] = _unstack_heads(jnp.broadcast_to(m + jnp.log(l), (2 * BQ, PAIR)), lo_q)
            return carry

        _loop_blocks(G * nb, blk, 0, 4)

    qspec = pl.BlockSpec((G, L, PAIR), lambda n: (n, 0, 0))
    kspec = pl.BlockSpec((G, L, PAIR), lambda n: (n // pairs_per_kv, 0, 0))
    return pl.pallas_call(
        body, name="attn_fwd", grid=(N // G,),
        in_specs=[pl.BlockSpec(memory_space=pltpu.SMEM), qspec, kspec, kspec,
                  pl.BlockSpec((None, 3, 2 * BQ, W), lambda n: (n * G // pairs_per_bias, 0, 0, 0))],
        out_specs=[qspec, qspec],
        out_shape=[jax.ShapeDtypeStruct((N, L, PAIR), F32), jax.ShapeDtypeStruct((N, L, PAIR), F32)],
        compiler_params=_params("parallel"),
    )(sink, q, kp, vp, bias4)


def _attn_bwd(q, kp, vp, bias4t, sink, o, lse, do, R, pairs_per_kv, pairs_per_bias):
    N, L, _ = q.shape
    Nk = kp.shape[0]
    Pb = bias4t.shape[0]
    W = BQ + 2 * R
    nb = L // BQ
    assert L >= W and nb >= 2
    G = _units_per_step(nb, pairs_per_kv)

    def body(sink_ref, q_ref, k_ref, v_ref, bias_ref, o_ref, lse_ref, do_ref,
             dq_ref, dk_ref, dv_ref, dbias_ref, dsink_ref, dk_acc, dv_acc):
        n = pl.program_id(0)
        lo_q = _lo_mask((BQ, PAIR))
        first = lax.broadcasted_iota(jnp.int32, (1, 2 * BQ), 1) < BQ
        dsink_ref[...] = jnp.zeros_like(dsink_ref)

        @pl.when(n % pairs_per_kv == 0)
        def _():
            dk_acc[...] = jnp.zeros_like(dk_acc)
            dv_acc[...] = jnp.zeros_like(dv_acc)

        @pl.when((n * G) % pairs_per_bias == 0)
        def _():
            dbias_ref[...] = jnp.zeros_like(dbias_ref)

        def blk(f, carry):
            g, b = f // nb, f % nb
            u = n * G + g
            sk = jnp.where(first, sink_ref[2 * u], sink_ref[2 * u + 1])
            q0 = pl.multiple_of(b * BQ, BQ)
            q2 = _stack_heads(q_ref[g, pl.ds(q0, BQ), :], lo_q)
            k0, variant = _key_window(b, nb, L, R, W)
            kw = k_ref[g, pl.ds(k0, W), :]
            vw = v_ref[g, pl.ds(k0, W), :]
            dov = do_ref[g, pl.ds(q0, BQ), :]
            lse = _row_vector(lse_ref[g, pl.ds(q0, BQ), :], lo_q)
            delta = _row_vector(_half_sum(dov.astype(F32) * o_ref[g, pl.ds(q0, BQ), :], lo_q), lo_q)
            do2 = _stack_heads(dov.astype(BF16), lo_q)
            st = _dot_nt(kw, q2) + bias_ref[variant]
            pt = jnp.exp(st - lse)
            dst = pt * (_dot_nt(vw, do2) - delta)
            dstb = dst.astype(BF16)
            dbias_ref[variant] += dst
            dk_acc[g, pl.ds(k0, W), :] += _dot(dstb, q2)
            dv_acc[g, pl.ds(k0, W), :] += _dot(pt.astype(BF16), do2)
            dq_ref[g, pl.ds(q0, BQ), :] = _unstack_heads(_dot_tn(dstb, kw), lo_q).astype(BF16)
            dsink_ref[g, pl.ds(0, 1), :] -= jnp.exp(sk - lse) * delta
            return carry

        _loop_blocks(G * nb, blk, 0, 4)
        dk_ref[...] = dk_acc[...].astype(BF16)
        dv_ref[...] = dv_acc[...].astype(BF16)

    qspec = pl.BlockSpec((G, L, PAIR), lambda n: (n, 0, 0))
    kspec = pl.BlockSpec((G, L, PAIR), lambda n: (n // pairs_per_kv, 0, 0))
    return pl.pallas_call(
        body, name="attn_bwd", grid=(N // G,),
        in_specs=[pl.BlockSpec(memory_space=pltpu.SMEM), qspec, kspec, kspec,
                  pl.BlockSpec((None, 3, W, 2 * BQ), lambda n: (n * G // pairs_per_bias, 0, 0, 0)),
                  qspec, qspec, qspec],
        out_specs=[qspec, kspec, kspec,
                   pl.BlockSpec((None, 3, W, 2 * BQ), lambda n: (n * G // pairs_per_bias, 0, 0, 0)),
                   pl.BlockSpec((G, 8, 2 * BQ), lambda n: (n, 0, 0))],
        out_shape=[jax.ShapeDtypeStruct((N, L, PAIR), BF16),
                   jax.ShapeDtypeStruct((Nk, L, PAIR), BF16),
                   jax.ShapeDtypeStruct((Nk, L, PAIR), BF16),
                   jax.ShapeDtypeStruct((Pb, 3, W, 2 * BQ), F32),
                   jax.ShapeDtypeStruct((N, 8, 2 * BQ), F32)],
        scratch_shapes=[pltpu.VMEM((G, L, PAIR), F32), pltpu.VMEM((G, L, PAIR), F32)],
        compiler_params=_params("arbitrary"),
    )(sink, q, kp, vp, bias4t, o, lse, do)


def _attn_merge(branch_outs, ob, tm):
    T = ob.shape[1]
    n = len(DILS)

    def body(*refs):
        o_in, l_in, ob_ref = refs[:n], refs[n:2 * n], refs[2 * n]
        o_out, l_out, cat_ref = refs[2 * n + 1:3 * n + 1], refs[3 * n + 1:4 * n + 1], refs[4 * n + 1]
        tiles = refs[4 * n + 2:]
        for c in range(4):
            o_nat, l_nat = [], []
            for di, d in enumerate(DILS):
                for kind, (src, dst) in enumerate(((o_in[di], o_nat), (l_in[di], l_nat))):
                    tile = tiles[2 * di + kind]
                    if d == 1:
                        dst.append(src[c, 0])
                    else:
                        for r in range(d):
                            tile[pl.ds(r, tm // d, stride=d), :] = src[c, r]
                        dst.append(tile[...])
            m = functools.reduce(jnp.maximum, l_nat)
            ws = [jnp.exp(l - m) for l in l_nat]
            z = sum(ws)
            o = sum(w * t for w, t in zip(ws, o_nat)) / z
            cat_ref[:, c * PAIR:(c + 1) * PAIR] = o.astype(BF16)
            cat_ref[:, (4 + c) * PAIR:(5 + c) * PAIR] = ob_ref[c].astype(BF16)
            _spread(tiles[0], o, o_out, c, F32)
            _spread(tiles[1], m + jnp.log(z), l_out, c, F32)

    specs, shapes = _spread_specs(tm, T, F32)
    four = pl.BlockSpec((4, tm, PAIR), lambda i: (0, i, 0))
    o_views = [o.reshape(4, d, T // d, PAIR) for (o, _), d in zip(branch_outs, DILS)]
    l_views = [l.reshape(4, d, T // d, PAIR) for (_, l), d in zip(branch_outs, DILS)]
    res = pl.pallas_call(
        body, name="attn_merge", grid=(T // tm,),
        in_specs=specs + specs + [four],
        out_specs=specs + specs + [pl.BlockSpec((tm, 8 * PAIR), lambda i: (i, 0))],
        out_shape=shapes + shapes + [jax.ShapeDtypeStruct((T, 8 * PAIR), BF16)],
        scratch_shapes=[pltpu.VMEM((tm, PAIR), F32)] * (2 * n),
        compiler_params=_params("parallel"),
    )(*o_views, *l_views, ob)
    merged = [(res[di].reshape(4 * d, T // d, PAIR), res[n + di].reshape(4 * d, T // d, PAIR))
              for di, d in enumerate(DILS)]
    return merged, res[2 * n]


def _weight_arg(w, blk):
    if blk is None:
        return pl.BlockSpec(w.shape, lambda i: (0, 0)), (lambda ref: ref[...])
    D = w.shape[2]
    return (pl.BlockSpec((N_DEV, 128, D), lambda i: (0, blk, 0)),
            lambda ref: ref[...].reshape(N_DEV * 128, D))


def _oproj_fwd(x, o_cat, w, blk, tm):
    T, D = x.shape
    wspec, wload = _weight_arg(w, blk)

    def body(x_ref, o_ref, w_ref, out_ref):
        out_ref[...] = x_ref[...] + _dot(o_ref[...], wload(w_ref))

    tok = pl.BlockSpec((tm, D), lambda i: (i, 0))
    return pl.pallas_call(
        body, name="oproj_fwd", grid=(T // tm,),
        in_specs=[tok, pl.BlockSpec((tm, o_cat.shape[1]), lambda i: (i, 0)), wspec],
        out_specs=tok, out_shape=jax.ShapeDtypeStruct((T, D), F32),
        compiler_params=_params("parallel"),
    )(x, o_cat, w)


def _oproj_bwd(dx, w, blk, tm, dep=None):
    T, D = dx.shape
    wspec, wload = _weight_arg(w, blk)

    def body(dx_ref, w_ref, dxb_ref, dob_ref, *rest):
        doa_refs, tile = rest[:-1], rest[-1]
        db = dx_ref[...].astype(BF16)
        dxb_ref[...] = db
        do = _dot_nt(db, wload(w_ref))
        for c in range(4):
            _spread(tile, do[:, c * PAIR:(c + 1) * PAIR], doa_refs, c, BF16)
            dob_ref[c] = do[:, (4 + c) * PAIR:(5 + c) * PAIR].astype(BF16)

    tok = pl.BlockSpec((tm, D), lambda i: (i, 0))
    specs, shapes = _spread_specs(tm, T, BF16)
    body, in_specs, args = _with_dep(body, dep, [tok, wspec], [dx, w])
    res = pl.pallas_call(
        body, name="oproj_bwd", grid=(T // tm,),
        in_specs=in_specs,
        out_specs=[tok, pl.BlockSpec((4, tm, PAIR), lambda i: (0, i, 0))] + specs,
        out_shape=[jax.ShapeDtypeStruct((T, D), BF16), jax.ShapeDtypeStruct((4, T, PAIR), BF16)] + shapes,
        scratch_shapes=[pltpu.VMEM((tm, PAIR), F32)],
        compiler_params=_params("parallel"),
    )(*args)
    return res[0], res[1], [t.reshape(4 * d, T // d, PAIR) for t, d in zip(res[2:], DILS)]


def _attn_post(qkv, gains2, dqa, dka, dva, dqb, dkb, dvb, tm):
    T, NQ = qkv.shape
    scale = HEAD_DIM ** -0.5

    n = len(DILS)

    def body(qkv_ref, g_ref, *rest):
        dq_refs, dk_refs, dv_refs = rest[:n], rest[n:2 * n], rest[2 * n:3 * n]
        qb_ref, kb_ref, vb_ref, out_ref, dg_ref, tile = rest[3 * n:]
        lo = _lo_mask((tm, PAIR))

        @pl.when(pl.program_id(0) == 0)
        def _():
            dg_ref[...] = jnp.zeros_like(dg_ref)

        def norm_bwd(c, gi, dy):
            xv = qkv_ref[:, c * PAIR:(c + 1) * PAIR]
            r = lax.rsqrt(_half_sum(xv * xv, lo) * (1.0 / HEAD_DIM) + EPS)
            xn = xv * r
            dg_ref[gi:gi + 1, :] += jnp.sum(dy * xn, axis=0, keepdims=True)
            dxn = dy * g_ref[gi:gi + 1, :]
            dx = r * (dxn - xn * (_half_sum(dxn * xn, lo) * (1.0 / HEAD_DIM)))
            out_ref[:, c * PAIR:(c + 1) * PAIR] = dx.astype(BF16)

        def fold(v):
            return v + pltpu.roll(v, HEAD_DIM, 1)

        for c in range(4):
            norm_bwd(c, 0, _collect(tile, dq_refs, c) * scale)
            norm_bwd(4 + c, 1, _collect(tile, dk_refs, c))
            out_ref[:, (8 + c) * PAIR:(9 + c) * PAIR] = _collect(tile, dv_refs, c).astype(BF16)
            norm_bwd(12 + c, 2, qb_ref[c].astype(F32) * scale)
        kb, vb = kb_ref[...].astype(F32), vb_ref[...].astype(F32)
        norm_bwd(16, 3, jnp.where(lo, fold(kb[0]), fold(kb[1])))
        out_ref[:, 17 * PAIR:18 * PAIR] = jnp.where(lo, fold(vb[0]), fold(vb[1])).astype(BF16)

    four = pl.BlockSpec((4, tm, PAIR), lambda i: (0, i, 0))
    two = pl.BlockSpec((2, tm, PAIR), lambda i: (0, i, 0))
    specs, _ = _spread_specs(tm, T, BF16)
    views = [t.reshape(4, d, T // d, PAIR) for group in (dqa, dka, dva) for t, d in zip(group, DILS)]
    return pl.pallas_call(
        body, name="attn_post", grid=(T // tm,),
        in_specs=[pl.BlockSpec((tm, NQ), lambda i: (i, 0)), pl.BlockSpec((4, PAIR), lambda i: (0, 0))]
        + specs * 3 + [four, two, two],
        out_specs=[pl.BlockSpec((tm, NQ), lambda i: (i, 0)), pl.BlockSpec((4, PAIR), lambda i: (0, 0))],
        out_shape=[jax.ShapeDtypeStruct((T, NQ), BF16), jax.ShapeDtypeStruct((4, PAIR), F32)],
        scratch_shapes=[pltpu.VMEM((tm, PAIR), F32)],
        compiler_params=_params("arbitrary"),
    )(qkv, gains2, *views, dqb, dkb, dvb)


def _dense_norm_bwd(dres, dz, w, blk, x, g, tm):
    T, D = x.shape
    N = dz.shape[1]
    wspec, wload = _weight_arg(w, blk)

    def body(dres_ref, dz_ref, w_ref, x_ref, g_ref, dx_ref, dgn_ref):
        i = pl.program_id(0)
        dx, dg = _norm_bwd(_dot_nt(dz_ref[...], wload(w_ref)), x_ref[...], g_ref[...])
        dx_ref[...] = dres_ref[...] + dx

        @pl.when(i == 0)
        def _():
            dgn_ref[...] = dg

        @pl.when(i > 0)
        def _():
            dgn_ref[...] += dg

    tok = pl.BlockSpec((tm, D), lambda i: (i, 0))
    row = pl.BlockSpec((1, D), lambda i: (0, 0))
    return pl.pallas_call(
        body, name="dense_norm_bwd", grid=(T // tm,),
        in_specs=[tok, pl.BlockSpec((tm, N), lambda i: (i, 0)), wspec, tok, row],
        out_specs=[tok, row],
        out_shape=[jax.ShapeDtypeStruct((T, D), F32), jax.ShapeDtypeStruct((1, D), F32)],
        compiler_params=_params("arbitrary"),
    )(dres, dz, w, x, g)


def _bias_reduce(onehot, dbm):
    Hb, K = dbm.shape

    def body(oh_ref, d_ref, out_ref):
        oh = oh_ref[...]
        d = d_ref[...]
        hi = d.astype(BF16)
        r1 = d - hi.astype(F32)
        mid = r1.astype(BF16)
        low = (r1 - mid.astype(F32)).astype(BF16)
        out_ref[...] = _dot_nt(hi, oh) + _dot_nt(mid, oh) + _dot_nt(low, oh)

    vm = pl.BlockSpec(memory_space=pltpu.VMEM)
    return pl.pallas_call(
        body, name="bias_reduce", in_specs=[vm, vm], out_specs=vm,
        out_shape=jax.ShapeDtypeStruct((Hb, 128), F32),
        compiler_params=pltpu.CompilerParams(vmem_limit_bytes=VMEM_LIMIT),
    )(onehot, dbm)


def _ple_fwd(x, g, wg, blk, p, wp, target, tm):
    T, D = x.shape
    P = p.shape[1]
    with_loss = target is not None
    wspec, wload = _weight_arg(wg, blk)

    def body(*refs):
        if with_loss:
            x_ref, g_ref, wg_ref, p_ref, wp_ref, t_ref, y_ref, hn_ref, gate_ref, pp_ref, pb_ref, loss_ref = refs
        else:
            x_ref, g_ref, wg_ref, p_ref, wp_ref, y_ref, hn_ref, gate_ref, pp_ref, pb_ref = refs
        i = pl.program_id(0)
        xv = x_ref[...]
        hb = (xv * _rstd(xv) * g_ref[...]).astype(BF16)
        hn_ref[...] = hb
        gate = _sigmoid(_dot(hb, wload(wg_ref)))
        pb = p_ref[...].astype(BF16)
        pb_ref[...] = pb
        pp = _dot(pb, wp_ref[...])
        gate_ref[...] = gate
        pp_ref[...] = pp
        y = xv + gate * pp
        if with_loss:
            err = y - t_ref[...]
            y_ref[...] = err * (1.0 / D)
            part = jnp.broadcast_to(0.5 * jnp.sum(jnp.sum(err * err, axis=1, keepdims=True) * (1.0 / D),
                                                  axis=0, keepdims=True), (1, 128))

            @pl.when(i == 0)
            def _():
                loss_ref[...] = part

            @pl.when(i > 0)
            def _():
                loss_ref[...] += part
        else:
            y_ref[...] = y

    tok = pl.BlockSpec((tm, D), lambda i: (i, 0))
    ptok = pl.BlockSpec((tm, P), lambda i: (i, 0))
    in_specs = [tok, pl.BlockSpec((1, D), lambda i: (0, 0)), wspec, ptok,
                pl.BlockSpec((P, D), lambda i: (0, 0))]
    out_specs = [tok, tok, tok, tok, ptok]
    out_shape = [jax.ShapeDtypeStruct((T, D), F32), jax.ShapeDtypeStruct((T, D), BF16),
                 jax.ShapeDtypeStruct((T, D), F32), jax.ShapeDtypeStruct((T, D), F32),
                 jax.ShapeDtypeStruct((T, P), BF16)]
    args = [x, g, wg, p, wp]
    if with_loss:
        in_specs.append(tok)
        out_specs.append(pl.BlockSpec((1, 128), lambda i: (0, 0)))
        out_shape.append(jax.ShapeDtypeStruct((1, 128), F32))
        args.append(target)
    return pl.pallas_call(
        body, name="ple_fwd_loss" if with_loss else "ple_fwd", grid=(T // tm,),
        in_specs=in_specs, out_specs=out_specs, out_shape=out_shape,
        compiler_params=_params("arbitrary" if with_loss else "parallel"),
    )(*args)


def _ple_bwd(dy, gate, pp, tm, dep=None):
    T, D = dy.shape

    def body(dy_ref, gate_ref, pp_ref, dgl_ref, dpp_ref):
        d = dy_ref[...]
        gt = gate_ref[...]
        dgl_ref[...] = (d * pp_ref[...] * gt * (1.0 - gt)).astype(BF16)
        dpp_ref[...] = (d * gt).astype(BF16)

    tok = pl.BlockSpec((tm, D), lambda i: (i, 0))
    body, in_specs, args = _with_dep(body, dep, [tok, tok, tok], [dy, gate, pp])
    return pl.pallas_call(
        body, name="ple_bwd", grid=(T // tm,), in_specs=in_specs, out_specs=[tok, tok],
        out_shape=[jax.ShapeDtypeStruct((T, D), BF16), jax.ShapeDtypeStruct((T, D), BF16)],
        compiler_params=_params("parallel"),
    )(*args)


def _adamw(w, g, m, v):
    shape = w.shape
    C = shape[-1]
    w2, g2, m2, v2 = (a.reshape(-1, C) for a in (w, g, m, v))
    Rn = w2.shape[0]
    tr = Rn
    for cand in (512, 352, 256):
        if Rn % cand == 0:
            tr = cand
            break
    c1 = 1.0 - ADAM_B1 ** ADAM_STEP
    c2 = 1.0 - ADAM_B2 ** ADAM_STEP

    def body(w_ref, g_ref, m_ref, v_ref, d_ref, nm_ref, nv_ref):
        gv = g_ref[...]
        mn = ADAM_B1 * m_ref[...] + (1.0 - ADAM_B1) * gv
        vn = ADAM_B2 * v_ref[...] + (1.0 - ADAM_B2) * (gv * gv)
        d_ref[...] = -ADAM_LR * ((mn / c1) / (jnp.sqrt(vn / c2) + ADAM_EPS) + ADAM_WD * w_ref[...])
        nm_ref[...] = mn
        nv_ref[...] = vn

    spec = pl.BlockSpec((tr, C), lambda i: (i, 0))
    sh = jax.ShapeDtypeStruct((Rn, C), F32)
    d, nm, nv = pl.pallas_call(
        body, name="adamw", grid=(Rn // tr,), in_specs=[spec] * 4, out_specs=[spec] * 3, out_shape=[sh] * 3,
        compiler_params=_params("parallel"),
    )(w2, g2, m2, v2)
    return d.reshape(shape), nm.reshape(shape), nv.reshape(shape)


def _my_place():
    x, y, c = lax.axis_index("x"), lax.axis_index("y"), lax.axis_index("c")
    chips = [(1 - x, y), (x, 1 - y), (1 - x, 1 - y)]
    return x, y, c, chips


def _all_gather(flat):
    R, Wd = flat.shape

    def body(x_ref, out_ref, send_sems, recv_sems, local_sem):
        x, y, c, chips = _my_place()
        me, sibling = (x, y, c), (x, y, 1 - c)

        def rows(px, py, pc):
            return out_ref.at[4 * px + 2 * py + pc]

        def copy(k, block, to, src=None):
            return pltpu.make_async_remote_copy(
                src_ref=rows(*block) if src is None else src, dst_ref=rows(*block),
                send_sem=send_sems.at[k], recv_sem=recv_sems.at[k], device_id=to, device_id_type=MESH)

        mine = pltpu.make_async_copy(x_ref, rows(*me), local_sem)
        mine.start()
        first = [copy(0, me, sibling, src=x_ref)]
        first += [copy(1 + j, me, (*chip, c), src=x_ref) for j, chip in enumerate(chips)]
        for cp in first:
            cp.start()
        passed = [copy(4 + j, (*chip, c), sibling) for j, chip in enumerate(chips)]
        for j, chip in enumerate(chips):
            copy(1 + j, (*chip, c), me).wait_recv()
            passed[j].start()
        copy(0, sibling, me).wait_recv()
        for j, chip in enumerate(chips):
            copy(4 + j, (*chip, 1 - c), me).wait_recv()
        for cp in first + passed:
            cp.wait_send()
        mine.wait()

    return pl.pallas_call(
        body, name="all_gather",
        in_specs=[pl.BlockSpec(memory_space=pl.ANY)], out_specs=pl.BlockSpec(memory_space=pl.ANY),
        out_shape=jax.ShapeDtypeStruct((N_DEV, R, Wd), flat.dtype),
        scratch_shapes=[pltpu.SemaphoreType.DMA((7,)), pltpu.SemaphoreType.DMA((7,)), pltpu.SemaphoreType.DMA],
    )(flat)


def _reduce_scatter(gparts, tr):
    _, R, Wd = gparts.shape
    nt = R // tr

    def body(g_ref, out_ref, a_ref, p_ref, b_ref, vb, vo_b, vo_f, d2d_send, d2d_recv, ici_send, ici_recv):
        x, y, c, chips = _my_place()
        sibling = (x, y, 1 - c)
        allchips = [(x, y)] + chips

        def dev(chip, pc):
            return 4 * chip[0] + 2 * chip[1] + pc

        d2d = [pltpu.make_async_remote_copy(
            src_ref=g_ref.at[dev(q, 1 - c)], dst_ref=a_ref.at[a], send_sem=d2d_send.at[a], recv_sem=d2d_recv.at[a],
            device_id=sibling, device_id_type=MESH) for a, q in enumerate(allchips)]
        for cp in d2d:
            cp.start()

        def add_tiles(srcs, dst, vo):
            def step(t, carry):
                r = pl.ds(pl.multiple_of(t * tr, tr), tr)
                acc = None
                for s_i, src in enumerate(srcs):
                    pltpu.sync_copy(src.at[r], vb.at[s_i])
                for s_i in range(len(srcs)):
                    term = vb[s_i].astype(F32)
                    acc = term if acc is None else acc + term
                vo[...] = acc.astype(vo.dtype)
                pltpu.sync_copy(vo, dst.at[r])
                return carry

            lax.fori_loop(0, nt, step, 0)

        ici = []
        for j, q in enumerate(chips):
            d2d[j + 1].wait_recv()
            add_tiles([g_ref.at[dev(q, c)], a_ref.at[j + 1]], p_ref.at[j], vo_b)
            cp = pltpu.make_async_remote_copy(
                src_ref=p_ref.at[j], dst_ref=b_ref.at[j], send_sem=ici_send.at[j], recv_sem=ici_recv.at[j],
                device_id=(*q, c), device_id_type=MESH)
            cp.start()
            ici.append(cp)
        d2d[0].wait_recv()
        for cp in ici:
            cp.wait_recv()
        add_tiles([g_ref.at[dev((x, y), c)], a_ref.at[0], b_ref.at[0], b_ref.at[1], b_ref.at[2]], out_ref, vo_f)
        for cp in d2d + ici:
            cp.wait_send()

    hbm = pl.BlockSpec(memory_space=pl.ANY)
    out, _, _, _ = pl.pallas_call(
        body, name="reduce_scatter",
        in_specs=[hbm], out_specs=[hbm, hbm, hbm, hbm],
        out_shape=[jax.ShapeDtypeStruct((R, Wd), F32), jax.ShapeDtypeStruct((4, R, Wd), BF16),
                   jax.ShapeDtypeStruct((3, R, Wd), BF16), jax.ShapeDtypeStruct((3, R, Wd), BF16)],
        scratch_shapes=[pltpu.VMEM((5, tr, Wd), BF16), pltpu.VMEM((tr, Wd), BF16), pltpu.VMEM((tr, Wd), F32),
                        pltpu.SemaphoreType.DMA((4,)), pltpu.SemaphoreType.DMA((4,)),
                        pltpu.SemaphoreType.DMA((3,)), pltpu.SemaphoreType.DMA((3,))],
        compiler_params=pltpu.CompilerParams(vmem_limit_bytes=VMEM_LIMIT),
    )(gparts)
    return out


def _peer(x, y, c, k):
    return (x ^ ((k >> 2) & 1), y ^ ((k >> 1) & 1), c ^ (k & 1))


HBM_SPEC = pl.BlockSpec(memory_space=pltpu.HBM)
SEM_SPEC = pl.BlockSpec(memory_space=pltpu.SEMAPHORE)


def _exchange_refs(srcs, lands, m, k, x, y, c, scatter):
    peer = _peer(x, y, c, k)
    if scatter:
        return srcs[m].at[4 * peer[0] + 2 * peer[1] + peer[2]], lands[m].at[k - 1], peer
    return srcs[m], lands[m].at[4 * x + 2 * y + c], peer


def _exchange_start(arrs, land_shapes, scatter, name):
    n = len(arrs)

    def body(*refs):
        srcs, lands = refs[:n], refs[n:2 * n]
        send_sems, recv_sems = refs[2 * n], refs[2 * n + 1]
        token = refs[-1]
        x, y, c, _ = _my_place()
        for m in range(n):
            for k in range(1, N_DEV):
                src, dst, peer = _exchange_refs(srcs, lands, m, k, x, y, c, scatter)
                pltpu.make_async_remote_copy(
                    src_ref=src, dst_ref=dst, send_sem=send_sems.at[7 * m + k - 1],
                    recv_sem=recv_sems.at[7 * m + k - 1], device_id=peer, device_id_type=MESH).start()
        token[...] = jnp.zeros_like(token)

    zones = [lax.empty(s_, a.dtype) for s_, a in zip(land_shapes, arrs)]
    outs = pl.pallas_call(
        body, name=name,
        out_shape=(pltpu.SemaphoreType.DMA((7 * n,)), pltpu.SemaphoreType.DMA((7 * n,)),
                   *[pltpu.HBM(a.shape, a.dtype) for a in arrs], *[pltpu.HBM(z.shape, z.dtype) for z in zones],
                   jax.ShapeDtypeStruct((8, 128), F32)),
        in_specs=[HBM_SPEC] * (2 * n),
        out_specs=(SEM_SPEC, SEM_SPEC, *[HBM_SPEC] * (2 * n), pl.BlockSpec(memory_space=pltpu.VMEM)),
        input_output_aliases={m: 2 + m for m in range(2 * n)},
        compiler_params=pltpu.CompilerParams(has_side_effects=pltpu.SideEffectType.DATAFLOW_SIDE_EFFECTING),
    )(*[pltpu.with_memory_space_constraint(a, pltpu.HBM) for a in arrs],
      *[pltpu.with_memory_space_constraint(z, pltpu.HBM) for z in zones])
    return outs[0], outs[1], list(outs[2:2 + n]), list(outs[2 + n:2 + 2 * n]), outs[-1]


def _exchange_wait(send_sems, recv_sems, arrs, zones, after, scatter, name):
    n = len(arrs)

    def body(*refs):
        srcs, lands = refs[:n], refs[n:2 * n]
        send_sems, recv_sems = refs[2 * n], refs[2 * n + 1]
        x, y, c, _ = _my_place()
        for m in range(n):
            for k in range(1, N_DEV):
                src, dst, peer = _exchange_refs(srcs, lands, m, k, x, y, c, scatter)
                cp = pltpu.make_async_remote_copy(
                    src_ref=src, dst_ref=dst, send_sem=send_sems.at[7 * m + k - 1],
                    recv_sem=recv_sems.at[7 * m + k - 1], device_id=peer, device_id_type=MESH)
                cp.wait_send()
                cp.wait_recv()

    outs = pl.pallas_call(
        body, name=name,
        out_shape=tuple(pltpu.HBM(a.shape, a.dtype) for a in list(arrs) + list(zones)),
        in_specs=[HBM_SPEC] * (2 * n) + [SEM_SPEC, SEM_SPEC, pl.BlockSpec(memory_space=pl.ANY)],
        out_specs=tuple([HBM_SPEC] * (2 * n)),
        input_output_aliases={m: m for m in range(2 * n)},
        compiler_params=pltpu.CompilerParams(has_side_effects=pltpu.SideEffectType.DATAFLOW_SIDE_EFFECTING),
    )(*arrs, *zones, send_sems, recv_sems, after)
    return list(outs[n:])


def _sum_parts(own, parts, tr, dep=None):
    R, W = own.shape

    def body(own_ref, parts_ref, out_ref):
        acc = own_ref[...].astype(F32)
        for k in range(N_DEV - 1):
            acc = acc + parts_ref[k].astype(F32)
        out_ref[...] = acc

    in_specs = [pl.BlockSpec((tr, W), lambda i: (i, 0)), pl.BlockSpec((N_DEV - 1, tr, W), lambda i: (0, i, 0))]
    body, in_specs, args = _with_dep(body, dep, in_specs, [own, parts])
    return pl.pallas_call(
        body, name="sum_parts", grid=(R // tr,),
        in_specs=in_specs,
        out_specs=pl.BlockSpec((tr, W), lambda i: (i, 0)),
        out_shape=jax.ShapeDtypeStruct((R, W), F32),
        compiler_params=_params("parallel"),
    )(*args)


def _all_reduce_small(v, dep=None):
    Rn, Wd = v.shape

    def body(v_ref, out_ref, gat_ref, send_sems, recv_sems):
        x, y, c, _ = _my_place()
        me = 4 * x + 2 * y + c
        gat_ref[me] = v_ref[...]
        copies = []
        for k in range(1, N_DEV):
            fx, fy, fc = (k >> 2) & 1, (k >> 1) & 1, k & 1
            peer = (x ^ fx, y ^ fy, c ^ fc)
            cp = pltpu.make_async_remote_copy(
                src_ref=v_ref, dst_ref=gat_ref.at[me], send_sem=send_sems.at[k - 1], recv_sem=recv_sems.at[k - 1],
                device_id=peer, device_id_type=MESH)
            cp.start()
            copies.append(cp)
        for cp in copies:
            cp.wait_recv()
        for cp in copies:
            cp.wait_send()
        acc = gat_ref[0]
        for k in range(1, N_DEV):
            acc = acc + gat_ref[k]
        out_ref[...] = acc

    vm = pl.BlockSpec(memory_space=pltpu.VMEM)
    body, in_specs, args = _with_dep(body, dep, [vm], [v])
    return pl.pallas_call(
        body, name="all_reduce_small", in_specs=in_specs, out_specs=vm,
        out_shape=jax.ShapeDtypeStruct((Rn, Wd), F32),
        scratch_shapes=[pltpu.VMEM((N_DEV, Rn, Wd), F32), pltpu.SemaphoreType.DMA((7,)),
                        pltpu.SemaphoreType.DMA((7,))],
    )(*args)


def _t5_bucket(rel):
    half = N_BUCKETS // 2
    max_exact = half // 2
    ret = jnp.where(rel > 0, half, 0)
    n = jnp.abs(rel)
    nf = jnp.maximum(n, 1).astype(F32)
    large = max_exact + (jnp.log(nf / max_exact) / math.log(MAX_DISTANCE / max_exact)
                         * (half - max_exact)).astype(jnp.int32)
    large = jnp.minimum(large, half - 1)
    return ret + jnp.where(n < max_exact, n, large)


def _band(R, d):
    W = BQ + 2 * R
    rel = jnp.arange(W)[None, :] - R - jnp.arange(BQ)[:, None]
    return _t5_bucket(rel * d), jnp.abs(rel) <= R


def _onehot(R, d):
    bkt, in_band = _band(R, d)
    return ((bkt.reshape(1, -1) == jnp.arange(128)[:, None]) & in_band.reshape(1, -1)).astype(BF16)


def _bias_expand(table_t, onehot):
    H = table_t.shape[0]
    K = onehot.shape[1]

    def body(t_ref, oh_ref, out_ref):
        oh = oh_ref[...]
        t = t_ref[...]
        hi = t.astype(BF16)
        r1 = t - hi.astype(F32)
        mid = r1.astype(BF16)
        low = (r1 - mid.astype(F32)).astype(BF16)
        marked = _dot(jnp.ones(t.shape, BF16), oh) > 0.5
        out_ref[...] = jnp.where(marked, _dot(hi, oh) + _dot(mid, oh) + _dot(low, oh), NEG)

    vm = pl.BlockSpec(memory_space=pltpu.VMEM)
    return pl.pallas_call(
        body, name="bias_expand", in_specs=[vm, vm], out_specs=vm,
        out_shape=jax.ShapeDtypeStruct((H, K), F32),
        compiler_params=pltpu.CompilerParams(vmem_limit_bytes=VMEM_LIMIT),
    )(table_t, onehot)


def _bias_matrix(table, R, d):
    table_t = jnp.pad(table.T, ((0, 0), (0, 128 - N_BUCKETS)))
    return _bias_expand(table_t, _onehot(R, d)).reshape(table.shape[1], BQ, BQ + 2 * R)


def _bias_variants(base, R):
    H, _, W = base.shape
    fill = jnp.full((H, BQ, R), NEG, F32)
    first = jnp.concatenate([base[:, :, R:], fill], axis=2)
    last = jnp.concatenate([fill, base[:, :, :W - R]], axis=2)
    v = jnp.stack([base, first, last], axis=1)
    v = v.reshape(H // 2, 2, 3, BQ, W).transpose(0, 2, 1, 3, 4).reshape(H // 2, 3, 2 * BQ, W)
    return v, v.transpose(0, 1, 3, 2)


def _bias_grad(dbt, R, d):
    P, _, W, _ = dbt.shape
    dbt = dbt[:, 0].at[:, R:].add(dbt[:, 1, :W - R]).at[:, :W - R].add(dbt[:, 2, R:])
    dbm = dbt.reshape(P, W, 2, BQ).transpose(0, 2, 3, 1).reshape(2 * P, BQ * W)
    return _bias_reduce(_onehot(R, d), dbm)[:, :N_BUCKETS].T


def _tile2(gain):
    return jnp.concatenate([gain, gain])


ROW_W_O, ROW_GATE, ROW_QKV, ROW_PROJ, B_ROWS = 768, 896, 1024, 1312, 1344
BLK_W_O, BLK_GATE = ROW_W_O // 128, ROW_GATE // 128


def _pack_layer(wts, i):
    a = jnp.stack([wts["ffn1_w_in"][i], wts["ffn2_w_in"][i]])
    D = a.shape[1]
    b = jnp.concatenate([
        wts["ffn1_w_out"][i], wts["ffn2_w_out"][i],
        jnp.zeros((ROW_W_O - 2 * wts["ffn1_w_out"].shape[1], D), a.dtype),
        wts["w_o"][i], wts["w_ple_gate"][i], wts["w_qkv"][i].reshape(-1, D), wts["w_ple_proj"][i].reshape(-1, D)])
    return a, b


def _unpack_layer(sums, like):
    w_in2, b1, b2, w_in1, w_out1 = sums
    n_out, n_sq = like["ffn1_w_out"].shape[1], like["w_o"].shape[1]
    out = {}
    if w_in2 is not None:
        out.update(ffn2_w_in=w_in2, ffn2_w_out=b1[:n_out], w_ple_gate=b1[n_out:n_out + n_sq],
                   w_ple_proj=b1[n_out + n_sq:].reshape(like["w_ple_proj"].shape[1:]))
    if b2 is not None:
        out.update(w_o=b2[:n_sq], w_qkv=b2[n_sq:].reshape(like["w_qkv"].shape[1:]))
    if w_in1 is not None:
        out.update(ffn1_w_in=w_in1, ffn1_w_out=w_out1)
    return out


def _col_sharded(gb, r0, r1, rows):
    return gb[:, r0:r1].reshape(N_DEV, rows, -1).transpose(1, 0, 2).reshape(rows, -1)


def _to_col_shards(g):
    rows = g.shape[0]
    return g.reshape(rows, N_DEV, -1).transpose(1, 0, 2).reshape(N_DEV, -1, 1024)


def _layer_weights(ga, gb, p_dim):
    return dict(ga=ga, gb=gb, w_qkv=_col_sharded(gb, ROW_QKV, ROW_PROJ, ga.shape[2]),
                w_proj=_col_sharded(gb, ROW_PROJ, B_ROWS, p_dim))


def _layer_fwd(x, p, w, sm, i, target, tm, biases, dep=None):
    ga, gb = w["ga"], w["gb"]
    saved = {}
    saved["x0"] = x
    x1, saved["h1"], saved["zg1"], saved["zu1"], saved["s1"] = _ffn_fwd(
        x, sm["norm_ffn1"][i][None], ga, gb, 0, tm, dep)
    saved["x1"] = x1
    qkv, saved["hm"] = _qkv_fwd(x1, sm["norm_mix"][i][None], w["w_qkv"], tm)
    saved["qkv"] = qkv
    gains2 = jnp.stack([_tile2(sm[k][i]) for k in ("q_norm_a", "k_norm_a", "q_norm_b", "k_norm_b")])
    saved["gains2"] = gains2
    qb, kb, vb, qkv_d = _attn_prep(qkv, gains2, tm)
    no_sink = jnp.full((8,), NEG, F32)
    branches = []
    outs = []
    for (R, d), bias, (qd, kd, vd) in zip(DILATED, biases[:3], qkv_d):
        sink = jnp.tile(no_sink, d)
        outs.append(_attn_fwd(qd, kd, vd, bias[0], sink, R, 1, d))
        branches.append((qd, kd, vd, bias, sink, R, d))
    bias_b = biases[3]
    sink_b = sm["sink_b"][i]
    ob, lb = _attn_fwd(qb, kb, vb, bias_b[0], sink_b, SWA_RADIUS, 2, 1)
    merged, o_cat = _attn_merge(outs, ob, tm)
    saved.update(branches=branches, b=(qb, kb, vb, bias_b, sink_b), merged=merged, ob=ob, lb=lb, o_cat=o_cat)
    x2 = _oproj_fwd(x1, o_cat, gb, BLK_W_O, tm)
    saved["x2"] = x2
    x3, saved["h2"], saved["zg2"], saved["zu2"], saved["s2"] = _ffn_fwd(
        x2, sm["norm_ffn2"][i][None], ga, gb, 1, tm)
    saved["x3"] = x3
    res = _ple_fwd(x3, sm["norm_ple"][i][None], gb, BLK_GATE, p, w["w_proj"], target, tm)
    y, saved["hp"], saved["gate"], saved["pp"], saved["pb"] = res[:5]
    loss = res[5] if target is not None else None
    return y, loss, saved


def _layer_bwd(dy, w, sm, i, sv, tm, dep=None, on_ready=None, on_small=None, on_last=None):
    ga, gb = w["ga"], w["gb"]
    gs = {}
    D = dy.shape[1]
    dgl, dpp = _ple_bwd(dy, sv["gate"], sv["pp"], tm, dep)
    d_gate = _matmul_tn(sv["hp"], dgl, D, 2 * tm)
    d_proj = _matmul_tn(sv["pb"], dpp, D, 2 * tm)
    dx3, gs["norm_ple"] = _dense_norm_bwd(dy, dgl, gb, BLK_GATE, sv["x3"], sm["norm_ple"][i][None], tm)
    dx2, dyb, dzg, dzu, gs["norm_ffn2"] = _ffn_bwd(dx3, sv["x2"], sm["norm_ffn2"][i][None], sv["zg2"], sv["zu2"],
                                                   ga, gb, 1, tm)
    dwin2, dwo2 = _ffn_dw(sv["h2"], dzg, dzu, sv["s2"], dyb, 2 * tm)
    half = dwo2.shape[1] // 2
    after_ffn2 = [dwin2, jnp.concatenate([dwo2.reshape(N_DEV, half, D), d_gate.reshape(N_DEV, -1, D),
                                          _to_col_shards(d_proj)], axis=1)]
    token = None if on_ready is None else on_ready(0, after_ffn2)
    dx2b, do_b, do_a = _oproj_bwd(dx2, gb, BLK_W_O, tm, token)
    d_wo = _matmul_tn(sv["o_cat"], dx2b, D, 2 * tm)
    dqa, dka, dva, dbias = [], [], [], []
    for (qd, kd, vd, bias, sink, R, d), (oa, la), do_d in zip(sv["branches"], sv["merged"], do_a):
        dq, dk, dv, dbm, _ = _attn_bwd(qd, kd, vd, bias[1], sink, oa, la, do_d, R, 1, d)
        dqa.append(dq)
        dka.append(dk)
        dva.append(dv)
        dbias.append(dbm)
    qb, kb, vb, bias_b, sink_b = sv["b"]
    dqb, dkb, dvb, dbm_b, dsink = _attn_bwd(qb, kb, vb, bias_b[1], sink_b, sv["ob"], sv["lb"], do_b,
                                            SWA_RADIUS, 2, 1)
    gs["rel_bias"] = dbias + [dbm_b]
    gs["sink_b"] = jnp.sum(dsink[:, 0].reshape(-1, 2, BQ), axis=2).reshape(-1)
    dqkv, dgains2 = _attn_post(sv["qkv"], sv["gains2"], dqa, dka, dva, dqb,
                               dkb, dvb, tm // 2)
    dgains = dgains2[:, :HEAD_DIM] + dgains2[:, HEAD_DIM:]
    for k, name in enumerate(("q_norm_a", "k_norm_a", "q_norm_b", "k_norm_b")):
        gs[name] = dgains[k]
    d_qkv = _matmul_tn(sv["hm"], dqkv, dqkv.shape[1] // 2, 2 * tm)
    after_mixer = [jnp.concatenate([d_wo.reshape(N_DEV, -1, D), _to_col_shards(d_qkv)], axis=1)]
    token = None if on_ready is None else on_ready(1, after_mixer)
    dx1, gs["norm_mix"] = _dense_norm_bwd(dx2, dqkv, w["w_qkv"], None, sv["x1"], sm["norm_mix"][i][None], tm)
    g1 = sm["norm_ffn1"][i][None]
    if on_last is None:
        dx0, dyb, dzg, dzu, gs["norm_ffn1"] = _ffn_bwd(dx1, sv["x0"], g1, sv["zg1"], sv["zu1"], ga, gb, 0, tm, token)
        dwin1, dwo1 = _ffn_dw(sv["h1"], dzg, dzu, sv["s1"], dyb, 2 * tm)
        return dx0, (after_ffn2, after_mixer, [dwin1, dwo1.reshape(N_DEV, half, D)]), gs
    dyb, dzg, dzu = _ffn_bwd_dz(dx1, sv["zg1"], sv["zu1"], gb, 0, tm, token)
    dwin1, dwo1 = _ffn_dw(sv["h1"], dzg, dzu, sv["s1"], dyb, 2 * tm, on_small(gs))
    last = [dwin1, dwo1.reshape(N_DEV, half, D)]
    dx0, gs["norm_ffn1"] = _ffn_bwd_dx(dx1, sv["x0"], g1, dzg, dzu, ga, 0, tm, on_last(last))
    return dx0, (after_ffn2, after_mixer, last), gs


def _bias_matrices(rel_bias):
    biases = [_bias_variants(_bias_matrix(rel_bias[:, :8], R, d), R) for R, d in DILATED]
    biases.append(_bias_variants(_bias_matrix(rel_bias[:, 8:], SWA_RADIUS, 1), SWA_RADIUS))
    return biases


def _stack_small(per_layer):
    small = {}
    for k, v in per_layer.items():
        if k == "rel_bias":
            per_branch = [sum(parts) for parts in zip(*v.values())]
            drel_a = sum(_bias_grad(t, R, d) for t, (R, d) in zip(per_branch[:3], DILATED))
            small[k] = jnp.concatenate([drel_a, _bias_grad(per_branch[3], SWA_RADIUS, 1)], axis=1)
        else:
            small[k] = jnp.stack([v[i].reshape(-1) for i in sorted(v)])
    return small


TM = 512
SUM_TILES = (512, 512, 416, 512, 352)
LAST_GROUP = ("ffn1_w_in", "ffn1_w_out")


def _pack_small(d, extra=None):
    parts = [d[k].reshape(-1) for k in SMALL]
    if extra is not None:
        parts.append(extra.reshape(-1))
    flat = jnp.concatenate(parts)
    return jnp.pad(flat, (0, SMALL_ROWS * 128 - flat.shape[0])).reshape(SMALL_ROWS, 128)


def _unpack_small(buf, like):
    flat = buf.reshape(-1)
    out, off = {}, 0
    for k in SMALL:
        n = like[k].size
        out[k] = flat[off:off + n].reshape(like[k].shape)
        off += n
    return out, flat[off]


def kernel(x, p, rel_bias, norm_ffn1, ffn1_w_in, ffn1_w_out, norm_mix, w_qkv, q_norm_a, k_norm_a, q_norm_b, k_norm_b, sink_b, w_o, norm_ffn2, ffn2_w_in, ffn2_w_out, norm_ple, w_ple_gate, w_ple_proj, loss_target, m_rel_bias, m_norm_ffn1, m_ffn1_w_in, m_ffn1_w_out, m_norm_mix, m_w_qkv, m_q_norm_a, m_k_norm_a, m_q_norm_b, m_k_norm_b, m_sink_b, m_w_o, m_norm_ffn2, m_ffn2_w_in, m_ffn2_w_out, m_norm_ple, m_w_ple_gate, m_w_ple_proj, v_rel_bias, v_norm_ffn1, v_ffn1_w_in, v_ffn1_w_out, v_norm_mix, v_w_qkv, v_q_norm_a, v_k_norm_a, v_q_norm_b, v_k_norm_b, v_sink_b, v_w_o, v_norm_ffn2, v_ffn2_w_in, v_ffn2_w_out, v_norm_ple, v_w_ple_gate, v_w_ple_proj):
    wts = dict(rel_bias=rel_bias, norm_ffn1=norm_ffn1, ffn1_w_in=ffn1_w_in, ffn1_w_out=ffn1_w_out,
               norm_mix=norm_mix, w_qkv=w_qkv, q_norm_a=q_norm_a, k_norm_a=k_norm_a, q_norm_b=q_norm_b,
               k_norm_b=k_norm_b, sink_b=sink_b, w_o=w_o, norm_ffn2=norm_ffn2, ffn2_w_in=ffn2_w_in,
               ffn2_w_out=ffn2_w_out, norm_ple=norm_ple, w_ple_gate=w_ple_gate, w_ple_proj=w_ple_proj)
    mom = dict(rel_bias=m_rel_bias, norm_ffn1=m_norm_ffn1, ffn1_w_in=m_ffn1_w_in, ffn1_w_out=m_ffn1_w_out,
               norm_mix=m_norm_mix, w_qkv=m_w_qkv, q_norm_a=m_q_norm_a, k_norm_a=m_k_norm_a, q_norm_b=m_q_norm_b,
               k_norm_b=m_k_norm_b, sink_b=m_sink_b, w_o=m_w_o, norm_ffn2=m_norm_ffn2, ffn2_w_in=m_ffn2_w_in,
               ffn2_w_out=m_ffn2_w_out, norm_ple=m_norm_ple, w_ple_gate=m_w_ple_gate, w_ple_proj=m_w_ple_proj)
    var = dict(rel_bias=v_rel_bias, norm_ffn1=v_norm_ffn1, ffn1_w_in=v_ffn1_w_in, ffn1_w_out=v_ffn1_w_out,
               norm_mix=v_norm_mix, w_qkv=v_w_qkv, q_norm_a=v_q_norm_a, k_norm_a=v_k_norm_a, q_norm_b=v_q_norm_b,
               k_norm_b=v_k_norm_b, sink_b=v_sink_b, w_o=v_w_o, norm_ffn2=v_norm_ffn2, ffn2_w_in=v_ffn2_w_in,
               ffn2_w_out=v_ffn2_w_out, norm_ple=v_norm_ple, w_ple_gate=v_w_ple_gate, w_ple_proj=v_w_ple_proj)
    sm = {k: wts[k] for k in SMALL}
    p_dim = p.shape[-1]
    me = 4 * lax.axis_index("x") + 2 * lax.axis_index("y") + lax.axis_index("c")
    packed = []
    for i in range(2):
        a, b = _pack_layer(wts, i)
        packed.append([a.reshape(-1, a.shape[-1]).astype(BF16), b.astype(BF16)])
    a_shape = (2, ffn1_w_in.shape[1], ffn1_w_in.shape[2])

    def weights_of(zones):
        return _layer_weights(zones[0].reshape((N_DEV,) + a_shape), zones[1], p_dim)

    w0 = weights_of([_all_gather(t) for t in packed[0]])
    zone_shapes = [(N_DEV,) + t.shape for t in packed[1]]
    ssem, rsem, thru, zones, token = _exchange_start(packed[1], zone_shapes, False, "gather_start")
    biases = _bias_matrices(rel_bias)
    x1, _, sv0 = _layer_fwd(x[0], p[0, 0], w0, sm, 0, None, TM, biases, dep=token)
    zones = _exchange_wait(ssem, rsem, thru, zones, x1, False, "gather_wait")
    w1 = weights_of([lax.dynamic_update_index_in_dim(z, t, me, 0) for z, t in zip(zones, packed[1])])
    dy, loss, sv1 = _layer_fwd(x1, p[1, 0], w1, sm, 1, loss_target[0], TM, biases)

    def slots_for(arrs):
        return [(N_DEV - 1,) + t.shape[1:] for t in arrs]

    dx1, groups1, gs1 = _layer_bwd(dy, w1, sm, 1, sv1, TM)
    g1 = groups1[0] + groups1[1] + groups1[2]
    ex1 = _exchange_start(g1, slots_for(g1), True, "scatter_start")
    held = {}

    def on_ready(stage, group):
        if stage == 1:
            held["slots1"] = _exchange_wait(*ex1[:4], group[0], True, "scatter_wait")
        held[stage] = _exchange_start(group, slots_for(group), True, f"scatter_start_{stage}")
        return held[stage][4]

    def on_small(gs0):
        part = dict(gs0, norm_ffn1=jnp.zeros_like(gs1["norm_ffn1"]))
        gsmall = _stack_small({k: {0: part[k], 1: gs1[k]} for k in part})
        held["small"] = _all_reduce_small(_pack_small(gsmall, loss[0, :1]))
        return held["small"]

    def on_last(group):
        held["last"] = _exchange_start(group, slots_for(group), True, "scatter_start_2")
        return held["last"][4]

    dx, groups0, gs0 = _layer_bwd(dx1, w0, sm, 0, sv0, TM, dep=ex1[4], on_ready=on_ready, on_small=on_small,
                                  on_last=on_last)
    last = groups0[2]
    slots0 = [_exchange_wait(*held[stage][:4], last[0], True, f"scatter_wait_{stage}") for stage in (0, 1)]

    def summed(arrs, slots, tiles, dep=None):
        return [_sum_parts(lax.dynamic_index_in_dim(t, me, 0, keepdims=False), s_, tr, dep)
                for t, s_, tr in zip(arrs, slots, tiles)]

    cover = held["last"][4]
    r1 = summed(g1, held["slots1"], SUM_TILES, cover)
    r0 = summed(groups0[0], slots0[0], SUM_TILES[:2], cover) + summed(groups0[1], slots0[1], SUM_TILES[2:3], cover)

    def update(names, layers):
        for k in names:
            grads[k] = jnp.stack([layers[0][k], layers[1][k]])
            delta[k], new_m[k], new_v[k] = _adamw(wts[k], grads[k], mom[k], var[k])

    grads, delta, new_m, new_v = {}, {}, {}, {}
    layer1 = _unpack_layer(r1, wts)
    update([k for k in BIG if k not in LAST_GROUP], [_unpack_layer(r0 + [None, None], wts), layer1])

    slots_last = _exchange_wait(*held["last"][:4], delta["ffn2_w_in"], True, "scatter_wait_2")
    update(LAST_GROUP, [_unpack_layer([None, None, None] + summed(last, slots_last, SUM_TILES[3:]), wts), layer1])
    late = _all_reduce_small(gs0["norm_ffn1"].reshape(-1, 128), dep=slots_last[0])
    small_sum, loss_sum = _unpack_small(held["small"], sm)
    small_sum["norm_ffn1"] = small_sum["norm_ffn1"].at[0].add(late.reshape(-1))
    grads.update(small_sum)
    zeros = {k: jnp.zeros_like(wts[k]) for k in SMALL}
    ds, ms, vs = _adamw(_pack_small(wts), _pack_small(small_sum), _pack_small(mom), _pack_small(var))
    for packed, dst in ((ds, delta), (ms, new_m), (vs, new_v)):
        dst.update(_unpack_small(packed, zeros)[0])

    return (loss_sum, dx[None], *[grads[k] for k in WEIGHTS], *[delta[k] for k in WEIGHTS],
            *[new_m[k] for k in WEIGHTS], *[new_v[k] for k in WEIGHTS])
```

```python
import functools
import math

import jax
import jax.numpy as jnp
from jax import lax
from jax.experimental import pallas as pl
from jax.experimental.pallas import tpu as pltpu

F32 = jnp.float32
BF16 = jnp.bfloat16

N_DEV = 8
HEAD_DIM = 64
PAIR = 2 * HEAD_DIM
BQ = 128
N_BUCKETS = 32
MAX_DISTANCE = 1024
DILATED = ((64, 1), (64, 4), (64, 16))
SWA_RADIUS = 128
EPS = 1e-6
NEG = -1e30
ADAM_LR, ADAM_B1, ADAM_B2, ADAM_EPS, ADAM_WD, ADAM_STEP = 0.001, 0.9, 0.999, 1e-08, 0.01, 10
VMEM_LIMIT = 56 * 1024 * 1024
AXES = ("x", "y", "c")
MESH = pl.DeviceIdType.MESH

BIG = ("ffn1_w_in", "ffn1_w_out", "w_qkv", "w_o", "ffn2_w_in", "ffn2_w_out", "w_ple_gate", "w_ple_proj")
SMALL = ("rel_bias", "norm_ffn1", "norm_mix", "q_norm_a", "k_norm_a", "q_norm_b", "k_norm_b", "sink_b",
         "norm_ffn2", "norm_ple")
WEIGHTS = ("rel_bias", "norm_ffn1", "ffn1_w_in", "ffn1_w_out", "norm_mix", "w_qkv", "q_norm_a", "k_norm_a",
           "q_norm_b", "k_norm_b", "sink_b", "w_o", "norm_ffn2", "ffn2_w_in", "ffn2_w_out", "norm_ple",
           "w_ple_gate", "w_ple_proj")
SMALL_ROWS = 96


def _params(*sem):
    return pltpu.CompilerParams(dimension_semantics=sem, vmem_limit_bytes=VMEM_LIMIT)


def _dot(a, b):
    return jnp.dot(a, b, preferred_element_type=F32)


def _dot_nt(a, b):
    return lax.dot_general(a, b, (((1,), (1,)), ((), ())), preferred_element_type=F32)


def _dot_tn(a, b):
    return lax.dot_general(a, b, (((0,), (0,)), ((), ())), preferred_element_type=F32)


def _sigmoid(x):
    return 1.0 / (1.0 + jnp.exp(-x))


def _rstd(xv):
    return lax.rsqrt(jnp.mean(xv * xv, axis=-1, keepdims=True) + EPS)


def _norm_bwd(dh, xv, gv):
    r = _rstd(xv)
    xn = xv * r
    dg = jnp.sum(dh * xn, axis=0, keepdims=True)
    dxn = dh * gv
    dx = r * (dxn - xn * jnp.mean(dxn * xn, axis=-1, keepdims=True))
    return dx, dg


def _lo_mask(shape):
    return lax.broadcasted_iota(jnp.int32, shape, len(shape) - 1) < HEAD_DIM


def _half_sum(t, lo):
    s0 = jnp.sum(jnp.where(lo, t, 0.0), axis=1, keepdims=True)
    s1 = jnp.sum(jnp.where(lo, 0.0, t), axis=1, keepdims=True)
    return jnp.where(lo, s0, s1)


FFN_PARTS = 2


def _ffn_weight_specs(f, nj, D, C):
    return [pl.BlockSpec((None, None, D, C), lambda i, j: (j, f, 0, 0)),
            pl.BlockSpec((None, None, D, C), lambda i, j: (j + nj, f, 0, 0)),
            pl.BlockSpec((2, C // 2, D), lambda i, j: (j, f, 0))]


def _with_dep(body, dep, in_specs, args):
    if dep is None:
        return body, in_specs, args

    def body_after(dep_ref, *refs):
        body(*refs)

    return body_after, [pl.BlockSpec(memory_space=pl.ANY)] + in_specs, [dep] + args


def _ffn_fwd(x, g, ga, gb, f, tm, dep=None):
    T, D = x.shape
    nj, C = ga.shape[0] // 2, ga.shape[3]

    def body(x_ref, g_ref, wg_ref, wu_ref, wo_ref, xo_ref, h_ref, zg_ref, zu_ref, s_ref, h_scr, acc):
        j = pl.program_id(1)

        @pl.when(j == 0)
        def _():
            xv = x_ref[...]
            hb = (xv * _rstd(xv) * g_ref[...]).astype(BF16)
            h_scr[...] = hb
            h_ref[...] = hb
            acc[...] = jnp.zeros_like(acc)

        wo = wo_ref[...].reshape(C, D)
        for part in range(FFN_PARTS):
            sl = pl.ds(part * (tm // FFN_PARTS), tm // FFN_PARTS)
            hb = h_scr[sl, :]
            gt = _dot(hb, wg_ref[...])
            up = _dot(hb, wu_ref[...])
            s = (gt * _sigmoid(gt) * up).astype(BF16)
            zg_ref[sl, :] = gt.astype(BF16)
            zu_ref[sl, :] = up.astype(BF16)
            s_ref[sl, :] = s
            acc[sl, :] += _dot(s, wo)

        @pl.when(j == nj - 1)
        def _():
            xo_ref[...] = x_ref[...] + 0.5 * acc[...]

    tok = pl.BlockSpec((tm, D), lambda i, j: (i, 0))
    chunk = pl.BlockSpec((None, tm, C), lambda i, j: (j, i, 0))
    in_specs = [tok, pl.BlockSpec((1, D), lambda i, j: (0, 0))] + _ffn_weight_specs(f, nj, D, C)
    body, in_specs, args = _with_dep(body, dep, in_specs, [x, g, ga, ga, gb])
    return pl.pallas_call(
        body, name="ffn_fwd", grid=(T // tm, nj),
        in_specs=in_specs,
        out_specs=[tok, tok, chunk, chunk, chunk],
        out_shape=[jax.ShapeDtypeStruct((T, D), F32), jax.ShapeDtypeStruct((T, D), BF16),
                   jax.ShapeDtypeStruct((nj, T, C), BF16), jax.ShapeDtypeStruct((nj, T, C), BF16),
                   jax.ShapeDtypeStruct((nj, T, C), BF16)],
        scratch_shapes=[pltpu.VMEM((tm, D), BF16), pltpu.VMEM((tm, D), F32)],
        compiler_params=_params("parallel", "arbitrary"),
    )(*args)


def _ffn_bwd(dxo, x, g, zg, zu, ga, gb, f, tm, dep=None):
    T, D = x.shape
    nj, C = ga.shape[0] // 2, ga.shape[3]

    def body(dxo_ref, x_ref, g_ref, zg_ref, zu_ref, wg_ref, wu_ref, wo_ref,
             dx_ref, dy_ref, dzg_ref, dzu_ref, dgn_ref, dy_scr, acc):
        i, j = pl.program_id(0), pl.program_id(1)

        @pl.when(j == 0)
        def _():
            dyb = (0.5 * dxo_ref[...]).astype(BF16)
            dy_scr[...] = dyb
            dy_ref[...] = dyb
            acc[...] = jnp.zeros_like(acc)

        wo = wo_ref[...].reshape(C, D)
        for part in range(FFN_PARTS):
            sl = pl.ds(part * (tm // FFN_PARTS), tm // FFN_PARTS)
            ds = _dot_nt(dy_scr[sl, :], wo)
            gt = zg_ref[sl, :].astype(F32)
            up = zu_ref[sl, :].astype(F32)
            sg = _sigmoid(gt)
            dgt = (ds * up * (sg * (1.0 + gt * (1.0 - sg)))).astype(BF16)
            dup = (ds * (gt * sg)).astype(BF16)
            dzg_ref[sl, :] = dgt
            dzu_ref[sl, :] = dup
            acc[sl, :] += _dot_nt(dgt, wg_ref[...]) + _dot_nt(dup, wu_ref[...])

        @pl.when(j == nj - 1)
        def _():
            dx, dg = _norm_bwd(acc[...], x_ref[...], g_ref[...])
            dx_ref[...] = dxo_ref[...] + dx

            @pl.when(i == 0)
            def _():
                dgn_ref[...] = dg

            @pl.when(i > 0)
            def _():
                dgn_ref[...] += dg

    tok = pl.BlockSpec((tm, D), lambda i, j: (i, 0))
    chunk = pl.BlockSpec((None, tm, C), lambda i, j: (j, i, 0))
    row = pl.BlockSpec((1, D), lambda i, j: (0, 0))
    in_specs = [tok, tok, row, chunk, chunk] + _ffn_weight_specs(f, nj, D, C)
    body, in_specs, args = _with_dep(body, dep, in_specs, [dxo, x, g, zg, zu, ga, ga, gb])
    return pl.pallas_call(
        body, name="ffn_bwd", grid=(T // tm, nj),
        in_specs=in_specs,
        out_specs=[tok, tok, chunk, chunk, row],
        out_shape=[jax.ShapeDtypeStruct((T, D), F32), jax.ShapeDtypeStruct((T, D), BF16),
                   jax.ShapeDtypeStruct((nj, T, C), BF16), jax.ShapeDtypeStruct((nj, T, C), BF16),
                   jax.ShapeDtypeStruct((1, D), F32)],
        scratch_shapes=[pltpu.VMEM((tm, D), BF16), pltpu.VMEM((tm, D), F32)],
        compiler_params=_params("arbitrary", "arbitrary"),
    )(*args)


def _ffn_bwd_dz(dxo, zg, zu, gb, f, tm, dep=None):
    T, D = dxo.shape
    nj, C = zg.shape[0], zg.shape[2]

    def body(dxo_ref, zg_ref, zu_ref, wo_ref, dy_ref, dzg_ref, dzu_ref, dy_scr):
        @pl.when(pl.program_id(1) == 0)
        def _():
            dyb = (0.5 * dxo_ref[...]).astype(BF16)
            dy_scr[...] = dyb
            dy_ref[...] = dyb

        wo = wo_ref[...].reshape(C, D)
        for part in range(FFN_PARTS):
            sl = pl.ds(part * (tm // FFN_PARTS), tm // FFN_PARTS)
            ds = _dot_nt(dy_scr[sl, :], wo)
            gt = zg_ref[sl, :].astype(F32)
            up = zu_ref[sl, :].astype(F32)
            sg = _sigmoid(gt)
            dzg_ref[sl, :] = (ds * up * (sg * (1.0 + gt * (1.0 - sg)))).astype(BF16)
            dzu_ref[sl, :] = (ds * (gt * sg)).astype(BF16)

    tok = pl.BlockSpec((tm, D), lambda i, j: (i, 0))
    chunk = pl.BlockSpec((None, tm, C), lambda i, j: (j, i, 0))
    in_specs = [tok, chunk, chunk, _ffn_weight_specs(f, nj, D, C)[2]]
    body, in_specs, args = _with_dep(body, dep, in_specs, [dxo, zg, zu, gb])
    return pl.pallas_call(
        body, name="ffn_bwd_dz", grid=(T // tm, nj),
        in_specs=in_specs, out_specs=[tok, chunk, chunk],
        out_shape=[jax.ShapeDtypeStruct((T, D), BF16), jax.ShapeDtypeStruct((nj, T, C), BF16),
                   jax.ShapeDtypeStruct((nj, T, C), BF16)],
        scratch_shapes=[pltpu.VMEM((tm, D), BF16)],
        compiler_params=_params("parallel", "arbitrary"),
    )(*args)


def _ffn_bwd_dx(dxo, x, g, dzg, dzu, ga, f, tm, dep=None):
    T, D = x.shape
    nj, C = ga.shape[0] // 2, ga.shape[3]

    def body(dxo_ref, x_ref, g_ref, dzg_ref, dzu_ref, wg_ref, wu_ref, dx_ref, dgn_ref, acc):
        i, j = pl.program_id(0), pl.program_id(1)

        @pl.when(j == 0)
        def _():
            acc[...] = jnp.zeros_like(acc)

        acc[...] += _dot_nt(dzg_ref[...], wg_ref[...]) + _dot_nt(dzu_ref[...], wu_ref[...])

        @pl.when(j == nj - 1)
        def _():
            dx, dg = _norm_bwd(acc[...], x_ref[...], g_ref[...])
            dx_ref[...] = dxo_ref[...] + dx

            @pl.when(i == 0)
            def _():
                dgn_ref[...] = dg

            @pl.when(i > 0)
            def _():
                dgn_ref[...] += dg

    tok = pl.BlockSpec((tm, D), lambda i, j: (i, 0))
    chunk = pl.BlockSpec((None, tm, C), lambda i, j: (j, i, 0))
    row = pl.BlockSpec((1, D), lambda i, j: (0, 0))
    in_specs = [tok, tok, row, chunk, chunk] + _ffn_weight_specs(f, nj, D, C)[:2]
    body, in_specs, args = _with_dep(body, dep, in_specs, [dxo, x, g, dzg, dzu, ga, ga])
    return pl.pallas_call(
        body, name="ffn_bwd_dx", grid=(T // tm, nj),
        in_specs=in_specs, out_specs=[tok, row],
        out_shape=[jax.ShapeDtypeStruct((T, D), F32), jax.ShapeDtypeStruct((1, D), F32)],
        scratch_shapes=[pltpu.VMEM((tm, D), F32)],
        compiler_params=_params("arbitrary", "arbitrary"),
    )(*args)


def _ffn_dw(h, dzg, dzu, s, dy, tk, dep=None):
    T, D = h.shape
    nj, C = s.shape[0], s.shape[2]
    nk = T // tk

    def body(h_ref, dzg_ref, dzu_ref, s_ref, dy_ref, dwin_ref, dwo_ref, ag, au, ao):
        k = pl.program_id(1)

        @pl.when(k == 0)
        def _():
            ag[...] = jnp.zeros_like(ag)
            au[...] = jnp.zeros_like(au)
            ao[...] = jnp.zeros_like(ao)

        hb = h_ref[...]
        ag[...] += _dot_tn(hb, dzg_ref[...])
        au[...] += _dot_tn(hb, dzu_ref[...])
        ao[...] += _dot_tn(s_ref[...], dy_ref[...])

        @pl.when(k == nk - 1)
        def _():
            dwin_ref[0] = ag[...].astype(BF16)
            dwin_ref[1] = au[...].astype(BF16)
            dwo_ref[...] = ao[...].astype(BF16)

    tok = pl.BlockSpec((tk, D), lambda j, k: (k, 0))
    chunk = pl.BlockSpec((None, tk, C), lambda j, k: (j, k, 0))
    body, in_specs, args = _with_dep(body, dep, [tok, chunk, chunk, chunk, tok], [h, dzg, dzu, s, dy])
    dwin, dwo = pl.pallas_call(
        body, name="ffn_dw", grid=(nj, nk),
        in_specs=in_specs,
        out_specs=[pl.BlockSpec((2, None, D, C), lambda j, k: (0, j, 0, 0)),
                   pl.BlockSpec((None, C, D), lambda j, k: (j, 0, 0))],
        out_shape=[jax.ShapeDtypeStruct((2, nj, D, C), BF16), jax.ShapeDtypeStruct((nj, C, D), BF16)],
        scratch_shapes=[pltpu.VMEM((D, C), F32), pltpu.VMEM((D, C), F32), pltpu.VMEM((C, D), F32)],
        compiler_params=_params("parallel", "arbitrary"),
    )(*args)
    return dwin.reshape(2 * nj, D, C), dwo


def _matmul_tn(a, b, tn, tk):
    T, Ka = a.shape
    N = b.shape[1]
    nk = T // tk

    def body(a_ref, b_ref, o_ref, acc):
        k = pl.program_id(1)

        @pl.when(k == 0)
        def _():
            acc[...] = jnp.zeros_like(acc)

        acc[...] += _dot_tn(a_ref[...], b_ref[...])

        @pl.when(k == nk - 1)
        def _():
            o_ref[...] = acc[...].astype(BF16)

    return pl.pallas_call(
        body, name="matmul_tn", grid=(N // tn, nk),
        in_specs=[pl.BlockSpec((tk, Ka), lambda n, k: (k, 0)), pl.BlockSpec((tk, tn), lambda n, k: (k, n))],
        out_specs=pl.BlockSpec((Ka, tn), lambda n, k: (0, n)),
        out_shape=jax.ShapeDtypeStruct((Ka, N), BF16),
        scratch_shapes=[pltpu.VMEM((Ka, tn), F32)],
        compiler_params=_params("parallel", "arbitrary"),
    )(a, b)


def _qkv_fwd(x, g, w, tm):
    T, D = x.shape
    N = w.shape[1]

    def body(x_ref, g_ref, w_ref, o_ref, h_ref):
        xv = x_ref[...]
        hb = (xv * _rstd(xv) * g_ref[...]).astype(BF16)
        h_ref[...] = hb
        o_ref[...] = _dot(hb, w_ref[...])

    return pl.pallas_call(
        body, name="qkv_fwd", grid=(T // tm,),
        in_specs=[pl.BlockSpec((tm, D), lambda i: (i, 0)), pl.BlockSpec((1, D), lambda i: (0, 0)),
                  pl.BlockSpec((D, N), lambda i: (0, 0))],
        out_specs=[pl.BlockSpec((tm, N), lambda i: (i, 0)), pl.BlockSpec((tm, D), lambda i: (i, 0))],
        out_shape=[jax.ShapeDtypeStruct((T, N), F32), jax.ShapeDtypeStruct((T, D), BF16)],
        compiler_params=_params("parallel"),
    )(x, g, w)


DILS = tuple(d for _, d in DILATED)


def _spread_specs(tm, T, dtype):
    specs = [pl.BlockSpec((4, d, tm // d, PAIR), lambda i: (0, 0, i, 0)) for d in DILS]
    shapes = [jax.ShapeDtypeStruct((4, d, T // d, PAIR), dtype) for d in DILS]
    return specs, shapes


def _spread(tile, y, outs, c, dtype):
    tm = y.shape[0]
    tile[...] = y
    for out, d in zip(outs, DILS):
        for r in range(d):
            out[c, r] = tile[pl.ds(r, tm // d, stride=d), :].astype(dtype)


def _collect(tile, ins, c):
    tm = tile.shape[0]
    first = True
    for ref, d in zip(ins, DILS):
        for r in range(d):
            rows = pl.ds(r, tm // d, stride=d) if d > 1 else pl.ds(0, tm)
            part = ref[c, r].astype(F32)
            tile[rows, :] = part if first else tile[rows, :] + part
        first = False
    return tile[...]


def _attn_prep(qkv, gains2, tm):
    T = qkv.shape[0]
    scale = HEAD_DIM ** -0.5
    n = len(DILS)

    def body(qkv_ref, g_ref, qb_ref, kb_ref, vb_ref, *rest):
        outs, tile = rest[:-1], rest[-1]
        lo = _lo_mask((tm, PAIR))

        def spread(kind, c, y):
            _spread(tile, y, outs[kind * n:(kind + 1) * n], c, BF16)

        def normed(c, gi, mult):
            xv = qkv_ref[:, c * PAIR:(c + 1) * PAIR]
            r = lax.rsqrt(_half_sum(xv * xv, lo) * (1.0 / HEAD_DIM) + EPS)
            y = xv * r * g_ref[gi:gi + 1, :]
            return y * mult if mult != 1.0 else y

        def both_halves(v):
            sw = pltpu.roll(v, HEAD_DIM, 1)
            return jnp.where(lo, v, sw), jnp.where(lo, sw, v)

        for c in range(4):
            spread(0, c, normed(c, 0, scale))
            spread(1, c, normed(4 + c, 1, 1.0))
            spread(2, c, qkv_ref[:, (8 + c) * PAIR:(9 + c) * PAIR])
            qb_ref[c] = normed(12 + c, 2, scale).astype(BF16)
        k0, k1 = both_halves(normed(16, 3, 1.0))
        kb_ref[0] = k0.astype(BF16)
        kb_ref[1] = k1.astype(BF16)
        v0, v1 = both_halves(qkv_ref[:, 17 * PAIR:18 * PAIR])
        vb_ref[0] = v0.astype(BF16)
        vb_ref[1] = v1.astype(BF16)

    four = pl.BlockSpec((4, tm, PAIR), lambda i: (0, i, 0))
    two = pl.BlockSpec((2, tm, PAIR), lambda i: (0, i, 0))
    s4 = jax.ShapeDtypeStruct((4, T, PAIR), BF16)
    s2 = jax.ShapeDtypeStruct((2, T, PAIR), BF16)
    specs, shapes = _spread_specs(tm, T, BF16)
    res = pl.pallas_call(
        body, name="attn_prep", grid=(T // tm,),
        in_specs=[pl.BlockSpec((tm, qkv.shape[1]), lambda i: (i, 0)), pl.BlockSpec((4, PAIR), lambda i: (0, 0))],
        out_specs=[four, two, two] + specs * 3,
        out_shape=[s4, s2, s2] + shapes * 3,
        scratch_shapes=[pltpu.VMEM((tm, PAIR), F32)],
        compiler_params=_params("parallel"),
    )(qkv, gains2)
    qb, kb, vb = res[:3]
    per_d = [tuple(res[3 + kind * n + di].reshape(4 * d, T // d, PAIR) for kind in range(3))
             for di, d in enumerate(DILS)]
    return qb, kb, vb, per_d


def _loop_blocks(nb, body, init, per_iter):
    u = math.gcd(nb, per_iter)

    def outer(i, carry):
        for k in range(u):
            carry = body(i * u + k, carry)
        return carry

    return lax.fori_loop(0, nb // u, outer, init)


def _key_window(b, nb, L, R, W):
    start = pl.multiple_of(jnp.clip(b * BQ - R, 0, L - W), HEAD_DIM)
    return start, jnp.where(b == 0, 1, jnp.where(b == nb - 1, 2, 0))


def _stack_heads(v, lo):
    z = jnp.zeros_like(v)
    return jnp.concatenate([jnp.where(lo, v, z), jnp.where(lo, z, v)], axis=0)


def _unstack_heads(v2, lo):
    return jnp.where(lo, v2[:BQ], v2[BQ:])


def _row_vector(v, lo):
    r = lax.broadcasted_iota(jnp.int32, (BQ, PAIR), 0)
    ln = lax.broadcasted_iota(jnp.int32, (BQ, PAIR), 1)
    diag = (ln % HEAD_DIM) == (r % HEAD_DIM)
    top = jnp.sum(jnp.where(diag & (r < HEAD_DIM), v, 0.0), axis=0, keepdims=True)
    bot = jnp.sum(jnp.where(diag & (r >= HEAD_DIM), v, 0.0), axis=0, keepdims=True)
    top8, bot8 = jnp.broadcast_to(top, (8, PAIR)), jnp.broadcast_to(bot, (8, PAIR))
    lo8 = _lo_mask((8, PAIR))
    head0 = jnp.where(lo8, top8, pltpu.roll(bot8, HEAD_DIM, 1))
    head1 = jnp.where(lo8, pltpu.roll(top8, HEAD_DIM, 1), bot8)
    return jnp.concatenate([head0, head1], axis=1)[:1]


def _units_per_step(nb, pairs_per_kv):
    return max(1, 16 // nb) if pairs_per_kv == 1 else 1


def _attn_fwd(q, kp, vp, bias4, sink, R, pairs_per_kv, pairs_per_bias):
    N, L, _ = q.shape
    W = BQ + 2 * R
    nb = L // BQ
    assert L >= W and nb >= 2
    G = _units_per_step(nb, pairs_per_kv)

    def body(sink_ref, q_ref, k_ref, v_ref, bias_ref, o_ref, lse_ref):
        n = pl.program_id(0)
        lo_q = _lo_mask((BQ, PAIR))
        first = lax.broadcasted_iota(jnp.int32, (2 * BQ, 1), 0) < BQ

        def blk(f, carry):
            g, b = f // nb, f % nb
            u = n * G + g
            sk = jnp.where(first, sink_ref[2 * u], sink_ref[2 * u + 1])
            q0 = pl.multiple_of(b * BQ, BQ)
            q2 = _stack_heads(q_ref[g, pl.ds(q0, BQ), :], lo_q)
            k0, variant = _key_window(b, nb, L, R, W)
            kw = k_ref[g, pl.ds(k0, W), :]
            vw = v_ref[g, pl.ds(k0, W), :]
            s = _dot_nt(q2, kw) + bias_ref[variant]
            m = jnp.maximum(jnp.max(s, axis=1, keepdims=True), sk)
            p = jnp.exp(s - m)
            l = jnp.sum(p, axis=1, keepdims=True) + jnp.exp(sk - m)
            o2 = _dot(p.astype(BF16), vw) / l
            o_ref[g, pl.ds(q0, BQ), :] = _unstack_heads(o2, lo_q)
            lse_ref[g, pl.ds(q0, BQ), :] = _unstack_heads(jnp.broadcast_to(m + jnp.log(l), (2 * BQ, PAIR)), lo_q)
            return carry

        _loop_blocks(G * nb, blk, 0, 4)

    qspec = pl.BlockSpec((G, L, PAIR), lambda n: (n, 0, 0))
    kspec = pl.BlockSpec((G, L, PAIR), lambda n: (n // pairs_per_kv, 0, 0))
    return pl.pallas_call(
        body, name="attn_fwd", grid=(N // G,),
        in_specs=[pl.BlockSpec(memory_space=pltpu.SMEM), qspec, kspec, kspec,
                  pl.BlockSpec((None, 3, 2 * BQ, W), lambda n: (n * G // pairs_per_bias, 0, 0, 0))],
        out_specs=[qspec, qspec],
        out_shape=[jax.ShapeDtypeStruct((N, L, PAIR), F32), jax.ShapeDtypeStruct((N, L, PAIR), F32)],
        compiler_params=_params("parallel"),
    )(sink, q, kp, vp, bias4)


def _attn_bwd(q, kp, vp, bias4t, sink, o, lse, do, R, pairs_per_kv, pairs_per_bias):
    N, L, _ = q.shape
    Nk = kp.shape[0]
    Pb = bias4t.shape[0]
    W = BQ + 2 * R
    nb = L // BQ
    assert L >= W and nb >= 2
    G = _units_per_step(nb, pairs_per_kv)

    def body(sink_ref, q_ref, k_ref, v_ref, bias_ref, o_ref, lse_ref, do_ref,
             dq_ref, dk_ref, dv_ref, dbias_ref, dsink_ref, dk_acc, dv_acc):
        n = pl.program_id(0)
        lo_q = _lo_mask((BQ, PAIR))
        first = lax.broadcasted_iota(jnp.int32, (1, 2 * BQ), 1) < BQ
        dsink_ref[...] = jnp.zeros_like(dsink_ref)

        @pl.when(n % pairs_per_kv == 0)
        def _():
            dk_acc[...] = jnp.zeros_like(dk_acc)
            dv_acc[...] = jnp.zeros_like(dv_acc)

        @pl.when((n * G) % pairs_per_bias == 0)
        def _():
            dbias_ref[...] = jnp.zeros_like(dbias_ref)

        def blk(f, carry):
            g, b = f // nb, f % nb
            u = n * G + g
            sk = jnp.where(first, sink_ref[2 * u], sink_ref[2 * u + 1])
            q0 = pl.multiple_of(b * BQ, BQ)
            q2 = _stack_heads(q_ref[g, pl.ds(q0, BQ), :], lo_q)
            k0, variant = _key_window(b, nb, L, R, W)
            kw = k_ref[g, pl.ds(k0, W), :]
            vw = v_ref[g, pl.ds(k0, W), :]
            dov = do_ref[g, pl.ds(q0, BQ), :]
            lse = _row_vector(lse_ref[g, pl.ds(q0, BQ), :], lo_q)
            delta = _row_vector(_half_sum(dov.astype(F32) * o_ref[g, pl.ds(q0, BQ), :], lo_q), lo_q)
            do2 = _stack_heads(dov.astype(BF16), lo_q)
            st = _dot_nt(kw, q2) + bias_ref[variant]
            pt = jnp.exp(st - lse)
            dst = pt * (_dot_nt(vw, do2) - delta)
            dstb = dst.astype(BF16)
            dbias_ref[variant] += dst
            dk_acc[g, pl.ds(k0, W), :] += _dot(dstb, q2)
            dv_acc[g, pl.ds(k0, W), :] += _dot(pt.astype(BF16), do2)
            dq_ref[g, pl.ds(q0, BQ), :] = _unstack_heads(_dot_tn(dstb, kw), lo_q).astype(BF16)
            dsink_ref[g, pl.ds(0, 1), :] -= jnp.exp(sk - lse) * delta
            return carry

        _loop_blocks(G * nb, blk, 0, 4)
        dk_ref[...] = dk_acc[...].astype(BF16)
        dv_ref[...] = dv_acc[...].astype(BF16)

    qspec = pl.BlockSpec((G, L, PAIR), lambda n: (n, 0, 0))
    kspec = pl.BlockSpec((G, L, PAIR), lambda n: (n // pairs_per_kv, 0, 0))
    return pl.pallas_call(
        body, name="attn_bwd", grid=(N // G,),
        in_specs=[pl.BlockSpec(memory_space=pltpu.SMEM), qspec, kspec, kspec,
                  pl.BlockSpec((None, 3, W, 2 * BQ), lambda n: (n * G // pairs_per_bias, 0, 0, 0)),
                  qspec, qspec, qspec],
        out_specs=[qspec, kspec, kspec,
                   pl.BlockSpec((None, 3, W, 2 * BQ), lambda n: (n * G // pairs_per_bias, 0, 0, 0)),
                   pl.BlockSpec((G, 8, 2 * BQ), lambda n: (n, 0, 0))],
        out_shape=[jax.ShapeDtypeStruct((N, L, PAIR), BF16),
                   jax.ShapeDtypeStruct((Nk, L, PAIR), BF16),
                   jax.ShapeDtypeStruct((Nk, L, PAIR), BF16),
                   jax.ShapeDtypeStruct((Pb, 3, W, 2 * BQ), F32),
                   jax.ShapeDtypeStruct((N, 8, 2 * BQ), F32)],
        scratch_shapes=[pltpu.VMEM((G, L, PAIR), F32), pltpu.VMEM((G, L, PAIR), F32)],
        compiler_params=_params("arbitrary"),
    )(sink, q, kp, vp, bias4t, o, lse, do)


def _attn_merge(branch_outs, ob, tm):
    T = ob.shape[1]
    n = len(DILS)

    def body(*refs):
        o_in, l_in, ob_ref = refs[:n], refs[n:2 * n], refs[2 * n]
        o_out, l_out, cat_ref = refs[2 * n + 1:3 * n + 1], refs[3 * n + 1:4 * n + 1], refs[4 * n + 1]
        tiles = refs[4 * n + 2:]
        for c in range(4):
            o_nat, l_nat = [], []
            for di, d in enumerate(DILS):
                for kind, (src, dst) in enumerate(((o_in[di], o_nat), (l_in[di], l_nat))):
                    tile = tiles[2 * di + kind]
                    if d == 1:
                        dst.append(src[c, 0])
                    else:
                        for r in range(d):
                            tile[pl.ds(r, tm // d, stride=d), :] = src[c, r]
                        dst.append(tile[...])
            m = functools.reduce(jnp.maximum, l_nat)
            ws = [jnp.exp(l - m) for l in l_nat]
            z = sum(ws)
            o = sum(w * t for w, t in zip(ws, o_nat)) / z
            cat_ref[:, c * PAIR:(c + 1) * PAIR] = o.astype(BF16)
            cat_ref[:, (4 + c) * PAIR:(5 + c) * PAIR] = ob_ref[c].astype(BF16)
            _spread(tiles[0], o, o_out, c, F32)
            _spread(tiles[1], m + jnp.log(z), l_out, c, F32)

    specs, shapes = _spread_specs(tm, T, F32)
    four = pl.BlockSpec((4, tm, PAIR), lambda i: (0, i, 0))
    o_views = [o.reshape(4, d, T // d, PAIR) for (o, _), d in zip(branch_outs, DILS)]
    l_views = [l.reshape(4, d, T // d, PAIR) for (_, l), d in zip(branch_outs, DILS)]
    res = pl.pallas_call(
        body, name="attn_merge", grid=(T // tm,),
        in_specs=specs + specs + [four],
        out_specs=specs + specs + [pl.BlockSpec((tm, 8 * PAIR), lambda i: (i, 0))],
        out_shape=shapes + shapes + [jax.ShapeDtypeStruct((T, 8 * PAIR), BF16)],
        scratch_shapes=[pltpu.VMEM((tm, PAIR), F32)] * (2 * n),
        compiler_params=_params("parallel"),
    )(*o_views, *l_views, ob)
    merged = [(res[di].reshape(4 * d, T // d, PAIR), res[n + di].reshape(4 * d, T // d, PAIR))
              for di, d in enumerate(DILS)]
    return merged, res[2 * n]


def _weight_arg(w, blk):
    if blk is None:
        return pl.BlockSpec(w.shape, lambda i: (0, 0)), (lambda ref: ref[...])
    D = w.shape[2]
    return (pl.BlockSpec((N_DEV, 128, D), lambda i: (0, blk, 0)),
            lambda ref: ref[...].reshape(N_DEV * 128, D))


def _oproj_fwd(x, o_cat, w, blk, tm):
    T, D = x.shape
    wspec, wload = _weight_arg(w, blk)

    def body(x_ref, o_ref, w_ref, out_ref):
        out_ref[...] = x_ref[...] + _dot(o_ref[...], wload(w_ref))

    tok = pl.BlockSpec((tm, D), lambda i: (i, 0))
    return pl.pallas_call(
        body, name="oproj_fwd", grid=(T // tm,),
        in_specs=[tok, pl.BlockSpec((tm, o_cat.shape[1]), lambda i: (i, 0)), wspec],
        out_specs=tok, out_shape=jax.ShapeDtypeStruct((T, D), F32),
        compiler_params=_params("parallel"),
    )(x, o_cat, w)


def _oproj_bwd(dx, w, blk, tm, dep=None):
    T, D = dx.shape
    wspec, wload = _weight_arg(w, blk)

    def body(dx_ref, w_ref, dxb_ref, dob_ref, *rest):
        doa_refs, tile = rest[:-1], rest[-1]
        db = dx_ref[...].astype(BF16)
        dxb_ref[...] = db
        do = _dot_nt(db, wload(w_ref))
        for c in range(4):
            _spread(tile, do[:, c * PAIR:(c + 1) * PAIR], doa_refs, c, BF16)
            dob_ref[c] = do[:, (4 + c) * PAIR:(5 + c) * PAIR].astype(BF16)

    tok = pl.BlockSpec((tm, D), lambda i: (i, 0))
    specs, shapes = _spread_specs(tm, T, BF16)
    body, in_specs, args = _with_dep(body, dep, [tok, wspec], [dx, w])
    res = pl.pallas_call(
        body, name="oproj_bwd", grid=(T // tm,),
        in_specs=in_specs,
        out_specs=[tok, pl.BlockSpec((4, tm, PAIR), lambda i: (0, i, 0))] + specs,
        out_shape=[jax.ShapeDtypeStruct((T, D), BF16), jax.ShapeDtypeStruct((4, T, PAIR), BF16)] + shapes,
        scratch_shapes=[pltpu.VMEM((tm, PAIR), F32)],
        compiler_params=_params("parallel"),
    )(*args)
    return res[0], res[1], [t.reshape(4 * d, T // d, PAIR) for t, d in zip(res[2:], DILS)]


def _attn_post(qkv, gains2, dqa, dka, dva, dqb, dkb, dvb, tm):
    T, NQ = qkv.shape
    scale = HEAD_DIM ** -0.5

    n = len(DILS)

    def body(qkv_ref, g_ref, *rest):
        dq_refs, dk_refs, dv_refs = rest[:n], rest[n:2 * n], rest[2 * n:3 * n]
        qb_ref, kb_ref, vb_ref, out_ref, dg_ref, tile = rest[3 * n:]
        lo = _lo_mask((tm, PAIR))

        @pl.when(pl.program_id(0) == 0)
        def _():
            dg_ref[...] = jnp.zeros_like(dg_ref)

        def norm_bwd(c, gi, dy):
            xv = qkv_ref[:, c * PAIR:(c + 1) * PAIR]
            r = lax.rsqrt(_half_sum(xv * xv, lo) * (1.0 / HEAD_DIM) + EPS)
            xn = xv * r
            dg_ref[gi:gi + 1, :] += jnp.sum(dy * xn, axis=0, keepdims=True)
            dxn = dy * g_ref[gi:gi + 1, :]
            dx = r * (dxn - xn * (_half_sum(dxn * xn, lo) * (1.0 / HEAD_DIM)))
            out_ref[:, c * PAIR:(c + 1) * PAIR] = dx.astype(BF16)

        def fold(v):
            return v + pltpu.roll(v, HEAD_DIM, 1)

        for c in range(4):
            norm_bwd(c, 0, _collect(tile, dq_refs, c) * scale)
            norm_bwd(4 + c, 1, _collect(tile, dk_refs, c))
            out_ref[:, (8 + c) * PAIR:(9 + c) * PAIR] = _collect(tile, dv_refs, c).astype(BF16)
            norm_bwd(12 + c, 2, qb_ref[c].astype(F32) * scale)
        kb, vb = kb_ref[...].astype(F32), vb_ref[...].astype(F32)
        norm_bwd(16, 3, jnp.where(lo, fold(kb[0]), fold(kb[1])))
        out_ref[:, 17 * PAIR:18 * PAIR] = jnp.where(lo, fold(vb[0]), fold(vb[1])).astype(BF16)

    four = pl.BlockSpec((4, tm, PAIR), lambda i: (0, i, 0))
    two = pl.BlockSpec((2, tm, PAIR), lambda i: (0, i, 0))
    specs, _ = _spread_specs(tm, T, BF16)
    views = [t.reshape(4, d, T // d, PAIR) for group in (dqa, dka, dva) for t, d in zip(group, DILS)]
    return pl.pallas_call(
        body, name="attn_post", grid=(T // tm,),
        in_specs=[pl.BlockSpec((tm, NQ), lambda i: (i, 0)), pl.BlockSpec((4, PAIR), lambda i: (0, 0))]
        + specs * 3 + [four, two, two],
        out_specs=[pl.BlockSpec((tm, NQ), lambda i: (i, 0)), pl.BlockSpec((4, PAIR), lambda i: (0, 0))],
        out_shape=[jax.ShapeDtypeStruct((T, NQ), BF16), jax.ShapeDtypeStruct((4, PAIR), F32)],
        scratch_shapes=[pltpu.VMEM((tm, PAIR), F32)],
        compiler_params=_params("arbitrary"),
    )(qkv, gains2, *views, dqb, dkb, dvb)


def _dense_norm_bwd(dres, dz, w, blk, x, g, tm):
    T, D = x.shape
    N = dz.shape[1]
    wspec, wload = _weight_arg(w, blk)

    def body(dres_ref, dz_ref, w_ref, x_ref, g_ref, dx_ref, dgn_ref):
        i = pl.program_id(0)
        dx, dg = _norm_bwd(_dot_nt(dz_ref[...], wload(w_ref)), x_ref[...], g_ref[...])
        dx_ref[...] = dres_ref[...] + dx

        @pl.when(i == 0)
        def _():
            dgn_ref[...] = dg

        @pl.when(i > 0)
        def _():
            dgn_ref[...] += dg

    tok = pl.BlockSpec((tm, D), lambda i: (i, 0))
    row = pl.BlockSpec((1, D), lambda i: (0, 0))
    return pl.pallas_call(
        body, name="dense_norm_bwd", grid=(T // tm,),
        in_specs=[tok, pl.BlockSpec((tm, N), lambda i: (i, 0)), wspec, tok, row],
        out_specs=[tok, row],
        out_shape=[jax.ShapeDtypeStruct((T, D), F32), jax.ShapeDtypeStruct((1, D), F32)],
        compiler_params=_params("arbitrary"),
    )(dres, dz, w, x, g)


def _bias_reduce(onehot, dbm):
    Hb, K = dbm.shape

    def body(oh_ref, d_ref, out_ref):
        oh = oh_ref[...]
        d = d_ref[...]
        hi = d.astype(BF16)
        r1 = d - hi.astype(F32)
        mid = r1.astype(BF16)
        low = (r1 - mid.astype(F32)).astype(BF16)
        out_ref[...] = _dot_nt(hi, oh) + _dot_nt(mid, oh) + _dot_nt(low, oh)

    vm = pl.BlockSpec(memory_space=pltpu.VMEM)
    return pl.pallas_call(
        body, name="bias_reduce", in_specs=[vm, vm], out_specs=vm,
        out_shape=jax.ShapeDtypeStruct((Hb, 128), F32),
        compiler_params=pltpu.CompilerParams(vmem_limit_bytes=VMEM_LIMIT),
    )(onehot, dbm)


def _ple_fwd(x, g, wg, blk, p, wp, target, tm):
    T, D = x.shape
    P = p.shape[1]
    with_loss = target is not None
    wspec, wload = _weight_arg(wg, blk)

    def body(*refs):
        if with_loss:
            x_ref, g_ref, wg_ref, p_ref, wp_ref, t_ref, y_ref, hn_ref, gate_ref, pp_ref, pb_ref, loss_ref = refs
        else:
            x_ref, g_ref, wg_ref, p_ref, wp_ref, y_ref, hn_ref, gate_ref, pp_ref, pb_ref = refs
        i = pl.program_id(0)
        xv = x_ref[...]
        hb = (xv * _rstd(xv) * g_ref[...]).astype(BF16)
        hn_ref[...] = hb
        gate = _sigmoid(_dot(hb, wload(wg_ref)))
        pb = p_ref[...].astype(BF16)
        pb_ref[...] = pb
        pp = _dot(pb, wp_ref[...])
        gate_ref[...] = gate
        pp_ref[...] = pp
        y = xv + gate * pp
        if with_loss:
            err = y - t_ref[...]
            y_ref[...] = err * (1.0 / D)
            part = jnp.broadcast_to(0.5 * jnp.sum(jnp.sum(err * err, axis=1, keepdims=True) * (1.0 / D),
                                                  axis=0, keepdims=True), (1, 128))

            @pl.when(i == 0)
            def _():
                loss_ref[...] = part

            @pl.when(i > 0)
            def _():
                loss_ref[...] += part
        else:
            y_ref[...] = y

    tok = pl.BlockSpec((tm, D), lambda i: (i, 0))
    ptok = pl.BlockSpec((tm, P), lambda i: (i, 0))
    in_specs = [tok, pl.BlockSpec((1, D), lambda i: (0, 0)), wspec, ptok,
                pl.BlockSpec((P, D), lambda i: (0, 0))]
    out_specs = [tok, tok, tok, tok, ptok]
    out_shape = [jax.ShapeDtypeStruct((T, D), F32), jax.ShapeDtypeStruct((T, D), BF16),
                 jax.ShapeDtypeStruct((T, D), F32), jax.ShapeDtypeStruct((T, D), F32),
                 jax.ShapeDtypeStruct((T, P), BF16)]
    args = [x, g, wg, p, wp]
    if with_loss:
        in_specs.append(tok)
        out_specs.append(pl.BlockSpec((1, 128), lambda i: (0, 0)))
        out_shape.append(jax.ShapeDtypeStruct((1, 128), F32))
        args.append(target)
    return pl.pallas_call(
        body, name="ple_fwd_loss" if with_loss else "ple_fwd", grid=(T // tm,),
        in_specs=in_specs, out_specs=out_specs, out_shape=out_shape,
        compiler_params=_params("arbitrary" if with_loss else "parallel"),
    )(*args)


def _ple_bwd(dy, gate, pp, tm, dep=None):
    T, D = dy.shape

    def body(dy_ref, gate_ref, pp_ref, dgl_ref, dpp_ref):
        d = dy_ref[...]
        gt = gate_ref[...]
        dgl_ref[...] = (d * pp_ref[...] * gt * (1.0 - gt)).astype(BF16)
        dpp_ref[...] = (d * gt).astype(BF16)

    tok = pl.BlockSpec((tm, D), lambda i: (i, 0))
    body, in_specs, args = _with_dep(body, dep, [tok, tok, tok], [dy, gate, pp])
    return pl.pallas_call(
        body, name="ple_bwd", grid=(T // tm,), in_specs=in_specs, out_specs=[tok, tok],
        out_shape=[jax.ShapeDtypeStruct((T, D), BF16), jax.ShapeDtypeStruct((T, D), BF16)],
        compiler_params=_params("parallel"),
    )(*args)


def _adamw(w, g, m, v):
    shape = w.shape
    C = shape[-1]
    w2, g2, m2, v2 = (a.reshape(-1, C) for a in (w, g, m, v))
    Rn = w2.shape[0]
    tr = Rn
    for cand in (512, 352, 256):
        if Rn % cand == 0:
            tr = cand
            break
    c1 = 1.0 - ADAM_B1 ** ADAM_STEP
    c2 = 1.0 - ADAM_B2 ** ADAM_STEP

    def body(w_ref, g_ref, m_ref, v_ref, d_ref, nm_ref, nv_ref):
        gv = g_ref[...]
        mn = ADAM_B1 * m_ref[...] + (1.0 - ADAM_B1) * gv
        vn = ADAM_B2 * v_ref[...] + (1.0 - ADAM_B2) * (gv * gv)
        d_ref[...] = -ADAM_LR * ((mn / c1) / (jnp.sqrt(vn / c2) + ADAM_EPS) + ADAM_WD * w_ref[...])
        nm_ref[...] = mn
        nv_ref[...] = vn

    spec = pl.BlockSpec((tr, C), lambda i: (i, 0))
    sh = jax.ShapeDtypeStruct((Rn, C), F32)
    d, nm, nv = pl.pallas_call(
        body, name="adamw", grid=(Rn // tr,), in_specs=[spec] * 4, out_specs=[spec] * 3, out_shape=[sh] * 3,
        compiler_params=_params("parallel"),
    )(w2, g2, m2, v2)
    return d.reshape(shape), nm.reshape(shape), nv.reshape(shape)


def _my_place():
    x, y, c = lax.axis_index("x"), lax.axis_index("y"), lax.axis_index("c")
    chips = [(1 - x, y), (x, 1 - y), (1 - x, 1 - y)]
    return x, y, c, chips


def _all_gather(flat):
    R, Wd = flat.shape

    def body(x_ref, out_ref, send_sems, recv_sems, local_sem):
        x, y, c, chips = _my_place()
        me, sibling = (x, y, c), (x, y, 1 - c)

        def rows(px, py, pc):
            return out_ref.at[4 * px + 2 * py + pc]

        def copy(k, block, to, src=None):
            return pltpu.make_async_remote_copy(
                src_ref=rows(*block) if src is None else src, dst_ref=rows(*block),
                send_sem=send_sems.at[k], recv_sem=recv_sems.at[k], device_id=to, device_id_type=MESH)

        mine = pltpu.make_async_copy(x_ref, rows(*me), local_sem)
        mine.start()
        first = [copy(0, me, sibling, src=x_ref)]
        first += [copy(1 + j, me, (*chip, c), src=x_ref) for j, chip in enumerate(chips)]
        for cp in first:
            cp.start()
        passed = [copy(4 + j, (*chip, c), sibling) for j, chip in enumerate(chips)]
        for j, chip in enumerate(chips):
            copy(1 + j, (*chip, c), me).wait_recv()
            passed[j].start()
        copy(0, sibling, me).wait_recv()
        for j, chip in enumerate(chips):
            copy(4 + j, (*chip, 1 - c), me).wait_recv()
        for cp in first + passed:
            cp.wait_send()
        mine.wait()

    return pl.pallas_call(
        body, name="all_gather",
        in_specs=[pl.BlockSpec(memory_space=pl.ANY)], out_specs=pl.BlockSpec(memory_space=pl.ANY),
        out_shape=jax.ShapeDtypeStruct((N_DEV, R, Wd), flat.dtype),
        scratch_shapes=[pltpu.SemaphoreType.DMA((7,)), pltpu.SemaphoreType.DMA((7,)), pltpu.SemaphoreType.DMA],
    )(flat)


def _peer(x, y, c, k):
    return (x ^ ((k >> 2) & 1), y ^ ((k >> 1) & 1), c ^ (k & 1))


HBM_SPEC = pl.BlockSpec(memory_space=pltpu.HBM)
SEM_SPEC = pl.BlockSpec(memory_space=pltpu.SEMAPHORE)


def _exchange_refs(srcs, lands, m, k, x, y, c, scatter):
    peer = _peer(x, y, c, k)
    if scatter:
        return srcs[m].at[4 * peer[0] + 2 * peer[1] + peer[2]], lands[m].at[k - 1], peer
    return srcs[m], lands[m].at[4 * x + 2 * y + c], peer


def _exchange_start(arrs, land_shapes, scatter, name):
    n = len(arrs)

    def body(*refs):
        srcs, lands = refs[:n], refs[n:2 * n]
        send_sems, recv_sems = refs[2 * n], refs[2 * n + 1]
        token = refs[-1]
        x, y, c, _ = _my_place()
        for m in range(n):
            for k in range(1, N_DEV):
                src, dst, peer = _exchange_refs(srcs, lands, m, k, x, y, c, scatter)
                pltpu.make_async_remote_copy(
                    src_ref=src, dst_ref=dst, send_sem=send_sems.at[7 * m + k - 1],
                    recv_sem=recv_sems.at[7 * m + k - 1], device_id=peer, device_id_type=MESH).start()
        token[...] = jnp.zeros_like(token)

    zones = [lax.empty(s_, a.dtype) for s_, a in zip(land_shapes, arrs)]
    outs = pl.pallas_call(
        body, name=name,
        out_shape=(pltpu.SemaphoreType.DMA((7 * n,)), pltpu.SemaphoreType.DMA((7 * n,)),
                   *[pltpu.HBM(a.shape, a.dtype) for a in arrs], *[pltpu.HBM(z.shape, z.dtype) for z in zones],
                   jax.ShapeDtypeStruct((8, 128), F32)),
        in_specs=[HBM_SPEC] * (2 * n),
        out_specs=(SEM_SPEC, SEM_SPEC, *[HBM_SPEC] * (2 * n), pl.BlockSpec(memory_space=pltpu.VMEM)),
        input_output_aliases={m: 2 + m for m in range(2 * n)},
        compiler_params=pltpu.CompilerParams(has_side_effects=pltpu.SideEffectType.DATAFLOW_SIDE_EFFECTING),
    )(*[pltpu.with_memory_space_constraint(a, pltpu.HBM) for a in arrs],
      *[pltpu.with_memory_space_constraint(z, pltpu.HBM) for z in zones])
    return outs[0], outs[1], list(outs[2:2 + n]), list(outs[2 + n:2 + 2 * n]), outs[-1]


def _exchange_wait(send_sems, recv_sems, arrs, zones, after, scatter, name):
    n = len(arrs)

    def body(*refs):
        srcs, lands = refs[:n], refs[n:2 * n]
        send_sems, recv_sems = refs[2 * n], refs[2 * n + 1]
        x, y, c, _ = _my_place()
        for m in range(n):
            for k in range(1, N_DEV):
                src, dst, peer = _exchange_refs(srcs, lands, m, k, x, y, c, scatter)
                cp = pltpu.make_async_remote_copy(
                    src_ref=src, dst_ref=dst, send_sem=send_sems.at[7 * m + k - 1],
                    recv_sem=recv_sems.at[7 * m + k - 1], device_id=peer, device_id_type=MESH)
                cp.wait_send()
                cp.wait_recv()

    outs = pl.pallas_call(
        body, name=name,
        out_shape=tuple(pltpu.HBM(a.shape, a.dtype) for a in list(arrs) + list(zones)),
        in_specs=[HBM_SPEC] * (2 * n) + [SEM_SPEC, SEM_SPEC, pl.BlockSpec(memory_space=pl.ANY)],
        out_specs=tuple([HBM_SPEC] * (2 * n)),
        input_output_aliases={m: m for m in range(2 * n)},
        compiler_params=pltpu.CompilerParams(has_side_effects=pltpu.SideEffectType.DATAFLOW_SIDE_EFFECTING),
    )(*arrs, *zones, send_sems, recv_sems, after)
    return list(outs[n:])


def _sum_parts(own, parts, tr, dep=None):
    R, W = own.shape

    def body(own_ref, parts_ref, out_ref):
        acc = own_ref[...].astype(F32)
        for k in range(N_DEV - 1):
            acc = acc + parts_ref[k].astype(F32)
        out_ref[...] = acc

    in_specs = [pl.BlockSpec((tr, W), lambda i: (i, 0)), pl.BlockSpec((N_DEV - 1, tr, W), lambda i: (0, i, 0))]
    body, in_specs, args = _with_dep(body, dep, in_specs, [own, parts])
    return pl.pallas_call(
        body, name="sum_parts", grid=(R // tr,),
        in_specs=in_specs,
        out_specs=pl.BlockSpec((tr, W), lambda i: (i, 0)),
        out_shape=jax.ShapeDtypeStruct((R, W), F32),
        compiler_params=_params("parallel"),
    )(*args)


def _all_reduce_small(v, dep=None):
    Rn, Wd = v.shape

    def body(v_ref, out_ref, gat_ref, send_sems, recv_sems):
        x, y, c, _ = _my_place()
        me = 4 * x + 2 * y + c
        gat_ref[me] = v_ref[...]
        copies = []
        for k in range(1, N_DEV):
            fx, fy, fc = (k >> 2) & 1, (k >> 1) & 1, k & 1
            peer = (x ^ fx, y ^ fy, c ^ fc)
            cp = pltpu.make_async_remote_copy(
                src_ref=v_ref, dst_ref=gat_ref.at[me], send_sem=send_sems.at[k - 1], recv_sem=recv_sems.at[k - 1],
                device_id=peer, device_id_type=MESH)
            cp.start()
            copies.append(cp)
        for cp in copies:
            cp.wait_recv()
        for cp in copies:
            cp.wait_send()
        acc = gat_ref[0]
        for k in range(1, N_DEV):
            acc = acc + gat_ref[k]
        out_ref[...] = acc

    vm = pl.BlockSpec(memory_space=pltpu.VMEM)
    body, in_specs, args = _with_dep(body, dep, [vm], [v])
    return pl.pallas_call(
        body, name="all_reduce_small", in_specs=in_specs, out_specs=vm,
        out_shape=jax.ShapeDtypeStruct((Rn, Wd), F32),
        scratch_shapes=[pltpu.VMEM((N_DEV, Rn, Wd), F32), pltpu.SemaphoreType.DMA((7,)),
                        pltpu.SemaphoreType.DMA((7,))],
    )(*args)


def _t5_bucket(rel):
    half = N_BUCKETS // 2
    max_exact = half // 2
    ret = jnp.where(rel > 0, half, 0)
    n = jnp.abs(rel)
    nf = jnp.maximum(n, 1).astype(F32)
    large = max_exact + (jnp.log(nf / max_exact) / math.log(MAX_DISTANCE / max_exact)
                         * (half - max_exact)).astype(jnp.int32)
    large = jnp.minimum(large, half - 1)
    return ret + jnp.where(n < max_exact, n, large)


def _band(R, d):
    W = BQ + 2 * R
    rel = jnp.arange(W)[None, :] - R - jnp.arange(BQ)[:, None]
    return _t5_bucket(rel * d), jnp.abs(rel) <= R


def _onehot(R, d):
    bkt, in_band = _band(R, d)
    return ((bkt.reshape(1, -1) == jnp.arange(128)[:, None]) & in_band.reshape(1, -1)).astype(BF16)


def _bias_expand(table_t, onehot):
    H = table_t.shape[0]
    K = onehot.shape[1]

    def body(t_ref, oh_ref, out_ref):
        oh = oh_ref[...]
        t = t_ref[...]
        hi = t.astype(BF16)
        r1 = t - hi.astype(F32)
        mid = r1.astype(BF16)
        low = (r1 - mid.astype(F32)).astype(BF16)
        marked = _dot(jnp.ones(t.shape, BF16), oh) > 0.5
        out_ref[...] = jnp.where(marked, _dot(hi, oh) + _dot(mid, oh) + _dot(low, oh), NEG)

    vm = pl.BlockSpec(memory_space=pltpu.VMEM)
    return pl.pallas_call(
        body, name="bias_expand", in_specs=[vm, vm], out_specs=vm,
        out_shape=jax.ShapeDtypeStruct((H, K), F32),
        compiler_params=pltpu.CompilerParams(vmem_limit_bytes=VMEM_LIMIT),
    )(table_t, onehot)


def _bias_matrix(table, R, d):
    table_t = jnp.pad(table.T, ((0, 0), (0, 128 - N_BUCKETS)))
    return _bias_expand(table_t, _onehot(R, d)).reshape(table.shape[1], BQ, BQ + 2 * R)


def _bias_variants(base, R):
    H, _, W = base.shape
    fill = jnp.full((H, BQ, R), NEG, F32)
    first = jnp.concatenate([base[:, :, R:], fill], axis=2)
    last = jnp.concatenate([fill, base[:, :, :W - R]], axis=2)
    v = jnp.stack([base, first, last], axis=1)
    v = v.reshape(H // 2, 2, 3, BQ, W).transpose(0, 2, 1, 3, 4).reshape(H // 2, 3, 2 * BQ, W)
    return v, v.transpose(0, 1, 3, 2)


def _bias_grad(dbt, R, d):
    P, _, W, _ = dbt.shape
    dbt = dbt[:, 0].at[:, R:].add(dbt[:, 1, :W - R]).at[:, :W - R].add(dbt[:, 2, R:])
    dbm = dbt.reshape(P, W, 2, BQ).transpose(0, 2, 3, 1).reshape(2 * P, BQ * W)
    return _bias_reduce(_onehot(R, d), dbm)[:, :N_BUCKETS].T


def _tile2(gain):
    return jnp.concatenate([gain, gain])


ROW_W_O, ROW_GATE, ROW_QKV, ROW_PROJ, B_ROWS = 768, 896, 1024, 1312, 1344
BLK_W_O, BLK_GATE = ROW_W_O // 128, ROW_GATE // 128


def _pack_layer(wts, i):
    a = jnp.stack([wts["ffn1_w_in"][i], wts["ffn2_w_in"][i]])
    D = a.shape[1]
    b = jnp.concatenate([
        wts["ffn1_w_out"][i], wts["ffn2_w_out"][i],
        jnp.zeros((ROW_W_O - 2 * wts["ffn1_w_out"].shape[1], D), a.dtype),
        wts["w_o"][i], wts["w_ple_gate"][i], wts["w_qkv"][i].reshape(-1, D), wts["w_ple_proj"][i].reshape(-1, D)])
    return a, b


def _unpack_layer(sums, like):
    w_in2, b1, b2, w_in1, w_out1 = sums
    n_out, n_sq = like["ffn1_w_out"].shape[1], like["w_o"].shape[1]
    out = {}
    if w_in2 is not None:
        out.update(ffn2_w_in=w_in2, ffn2_w_out=b1[:n_out], w_ple_gate=b1[n_out:n_out + n_sq],
                   w_ple_proj=b1[n_out + n_sq:].reshape(like["w_ple_proj"].shape[1:]))
    if b2 is not None:
        out.update(w_o=b2[:n_sq], w_qkv=b2[n_sq:].reshape(like["w_qkv"].shape[1:]))
    if w_in1 is not None:
        out.update(ffn1_w_in=w_in1, ffn1_w_out=w_out1)
    return out


def _col_sharded(gb, r0, r1, rows):
    return gb[:, r0:r1].reshape(N_DEV, rows, -1).transpose(1, 0, 2).reshape(rows, -1)


def _to_col_shards(g):
    rows = g.shape[0]
    return g.reshape(rows, N_DEV, -1).transpose(1, 0, 2).reshape(N_DEV, -1, 1024)


def _layer_weights(ga, gb, p_dim):
    return dict(ga=ga, gb=gb, w_qkv=_col_sharded(gb, ROW_QKV, ROW_PROJ, ga.shape[2]),
                w_proj=_col_sharded(gb, ROW_PROJ, B_ROWS, p_dim))


def _layer_fwd(x, p, w, sm, i, target, tm, biases, dep=None):
    ga, gb = w["ga"], w["gb"]
    saved = {}
    saved["x0"] = x
    x1, saved["h1"], saved["zg1"], saved["zu1"], saved["s1"] = _ffn_fwd(
        x, sm["norm_ffn1"][i][None], ga, gb, 0, tm, dep)
    saved["x1"] = x1
    qkv, saved["hm"] = _qkv_fwd(x1, sm["norm_mix"][i][None], w["w_qkv"], tm)
    saved["qkv"] = qkv
    gains2 = jnp.stack([_tile2(sm[k][i]) for k in ("q_norm_a", "k_norm_a", "q_norm_b", "k_norm_b")])
    saved["gains2"] = gains2
    qb, kb, vb, qkv_d = _attn_prep(qkv, gains2, tm)
    no_sink = jnp.full((8,), NEG, F32)
    branches = []
    outs = []
    for (R, d), bias, (qd, kd, vd) in zip(DILATED, biases[:3], qkv_d):
        sink = jnp.tile(no_sink, d)
        outs.append(_attn_fwd(qd, kd, vd, bias[0], sink, R, 1, d))
        branches.append((qd, kd, vd, bias, sink, R, d))
    bias_b = biases[3]
    sink_b = sm["sink_b"][i]
    ob, lb = _attn_fwd(qb, kb, vb, bias_b[0], sink_b, SWA_RADIUS, 2, 1)
    merged, o_cat = _attn_merge(outs, ob, tm)
    saved.update(branches=branches, b=(qb, kb, vb, bias_b, sink_b), merged=merged, ob=ob, lb=lb, o_cat=o_cat)
    x2 = _oproj_fwd(x1, o_cat, gb, BLK_W_O, tm)
    saved["x2"] = x2
    x3, saved["h2"], saved["zg2"], saved["zu2"], saved["s2"] = _ffn_fwd(
        x2, sm["norm_ffn2"][i][None], ga, gb, 1, tm)
    saved["x3"] = x3
    res = _ple_fwd(x3, sm["norm_ple"][i][None], gb, BLK_GATE, p, w["w_proj"], target, tm)
    y, saved["hp"], saved["gate"], saved["pp"], saved["pb"] = res[:5]
    loss = res[5] if target is not None else None
    return y, loss, saved


def _layer_bwd(dy, w, sm, i, sv, tm, dep=None, on_ready=None, on_small=None, on_last=None):
    ga, gb = w["ga"], w["gb"]
    gs = {}
    D = dy.shape[1]
    dgl, dpp = _ple_bwd(dy, sv["gate"], sv["pp"], tm, dep)
    d_gate = _matmul_tn(sv["hp"], dgl, D, 2 * tm)
    d_proj = _matmul_tn(sv["pb"], dpp, D, 2 * tm)
    dx3, gs["norm_ple"] = _dense_norm_bwd(dy, dgl, gb, BLK_GATE, sv["x3"], sm["norm_ple"][i][None], tm)
    dx2, dyb, dzg, dzu, gs["norm_ffn2"] = _ffn_bwd(dx3, sv["x2"], sm["norm_ffn2"][i][None], sv["zg2"], sv["zu2"],
                                                   ga, gb, 1, tm)
    dwin2, dwo2 = _ffn_dw(sv["h2"], dzg, dzu, sv["s2"], dyb, 2 * tm)
    half = dwo2.shape[1] // 2
    after_ffn2 = [dwin2, jnp.concatenate([dwo2.reshape(N_DEV, half, D), d_gate.reshape(N_DEV, -1, D),
                                          _to_col_shards(d_proj)], axis=1)]
    token = None if on_ready is None else on_ready(0, after_ffn2)
    dx2b, do_b, do_a = _oproj_bwd(dx2, gb, BLK_W_O, tm, token)
    d_wo = _matmul_tn(sv["o_cat"], dx2b, D, 2 * tm)
    dqa, dka, dva, dbias = [], [], [], []
    for (qd, kd, vd, bias, sink, R, d), (oa, la), do_d in zip(sv["branches"], sv["merged"], do_a):
        dq, dk, dv, dbm, _ = _attn_bwd(qd, kd, vd, bias[1], sink, oa, la, do_d, R, 1, d)
        dqa.append(dq)
        dka.append(dk)
        dva.append(dv)
        dbias.append(dbm)
    qb, kb, vb, bias_b, sink_b = sv["b"]
    dqb, dkb, dvb, dbm_b, dsink = _attn_bwd(qb, kb, vb, bias_b[1], sink_b, sv["ob"], sv["lb"], do_b,
                                            SWA_RADIUS, 2, 1)
    gs["rel_bias"] = dbias + [dbm_b]
    gs["sink_b"] = jnp.sum(dsink[:, 0].reshape(-1, 2, BQ), axis=2).reshape(-1)
    dqkv, dgains2 = _attn_post(sv["qkv"], sv["gains2"], dqa, dka, dva, dqb,
                               dkb, dvb, tm // 2)
    dgains = dgains2[:, :HEAD_DIM] + dgains2[:, HEAD_DIM:]
    for k, name in enumerate(("q_norm_a", "k_norm_a", "q_norm_b", "k_norm_b")):
        gs[name] = dgains[k]
    d_qkv = _matmul_tn(sv["hm"], dqkv, dqkv.shape[1] // 2, 2 * tm)
    after_mixer = [jnp.concatenate([d_wo.reshape(N_DEV, -1, D), _to_col_shards(d_qkv)], axis=1)]
    token = None if on_ready is None else on_ready(1, after_mixer)
    dx1, gs["norm_mix"] = _dense_norm_bwd(dx2, dqkv, w["w_qkv"], None, sv["x1"], sm["norm_mix"][i][None], tm)
    g1 = sm["norm_ffn1"][i][None]
    if on_last is None:
        dx0, dyb, dzg, dzu, gs["norm_ffn1"] = _ffn_bwd(dx1, sv["x0"], g1, sv["zg1"], sv["zu1"], ga, gb, 0, tm, token)
        dwin1, dwo1 = _ffn_dw(sv["h1"], dzg, dzu, sv["s1"], dyb, 2 * tm)
        return dx0, (after_ffn2, after_mixer, [dwin1, dwo1.reshape(N_DEV, half, D)]), gs
    dyb, dzg, dzu = _ffn_bwd_dz(dx1, sv["zg1"], sv["zu1"], gb, 0, tm, token)
    dwin1, dwo1 = _ffn_dw(sv["h1"], dzg, dzu, sv["s1"], dyb, 2 * tm, on_small(gs))
    last = [dwin1, dwo1.reshape(N_DEV, half, D)]
    dx0, gs["norm_ffn1"] = _ffn_bwd_dx(dx1, sv["x0"], g1, dzg, dzu, ga, 0, tm, on_last(last))
    return dx0, (after_ffn2, after_mixer, last), gs


def _bias_matrices(rel_bias):
    biases = [_bias_variants(_bias_matrix(rel_bias[:, :8], R, d), R) for R, d in DILATED]
    biases.append(_bias_variants(_bias_matrix(rel_bias[:, 8:], SWA_RADIUS, 1), SWA_RADIUS))
    return biases


def _stack_small(per_layer):
    small = {}
    for k, v in per_layer.items():
        if k == "rel_bias":
            per_branch = [sum(parts) for parts in zip(*v.values())]
            drel_a = sum(_bias_grad(t, R, d) for t, (R, d) in zip(per_branch[:3], DILATED))
            small[k] = jnp.concatenate([drel_a, _bias_grad(per_branch[3], SWA_RADIUS, 1)], axis=1)
        else:
            small[k] = jnp.stack([v[i].reshape(-1) for i in sorted(v)])
    return small


TM = 512
SUM_TILES = (512, 512, 416, 512, 352)
LAST_GROUP = ("ffn1_w_in", "ffn1_w_out")


def _pack_small(d, extra=None):
    parts = [d[k].reshape(-1) for k in SMALL]
    if extra is not None:
        parts.append(extra.reshape(-1))
    flat = jnp.concatenate(parts)
    return jnp.pad(flat, (0, SMALL_ROWS * 128 - flat.shape[0])).reshape(SMALL_ROWS, 128)


def _unpack_small(buf, like):
    flat = buf.reshape(-1)
    out, off = {}, 0
    for k in SMALL:
        n = like[k].size
        out[k] = flat[off:off + n].reshape(like[k].shape)
        off += n
    return out, flat[off]


def kernel(x, p, rel_bias, norm_ffn1, ffn1_w_in, ffn1_w_out, norm_mix, w_qkv, q_norm_a, k_norm_a, q_norm_b, k_norm_b, sink_b, w_o, norm_ffn2, ffn2_w_in, ffn2_w_out, norm_ple, w_ple_gate, w_ple_proj, loss_target, m_rel_bias, m_norm_ffn1, m_ffn1_w_in, m_ffn1_w_out, m_norm_mix, m_w_qkv, m_q_norm_a, m_k_norm_a, m_q_norm_b, m_k_norm_b, m_sink_b, m_w_o, m_norm_ffn2, m_ffn2_w_in, m_ffn2_w_out, m_norm_ple, m_w_ple_gate, m_w_ple_proj, v_rel_bias, v_norm_ffn1, v_ffn1_w_in, v_ffn1_w_out, v_norm_mix, v_w_qkv, v_q_norm_a, v_k_norm_a, v_q_norm_b, v_k_norm_b, v_sink_b, v_w_o, v_norm_ffn2, v_ffn2_w_in, v_ffn2_w_out, v_norm_ple, v_w_ple_gate, v_w_ple_proj):
    wts = dict(rel_bias=rel_bias, norm_ffn1=norm_ffn1, ffn1_w_in=ffn1_w_in, ffn1_w_out=ffn1_w_out,
               norm_mix=norm_mix, w_qkv=w_qkv, q_norm_a=q_norm_a, k_norm_a=k_norm_a, q_norm_b=q_norm_b,
               k_norm_b=k_norm_b, sink_b=sink_b, w_o=w_o, norm_ffn2=norm_ffn2, ffn2_w_in=ffn2_w_in,
               ffn2_w_out=ffn2_w_out, norm_ple=norm_ple, w_ple_gate=w_ple_gate, w_ple_proj=w_ple_proj)
    mom = dict(rel_bias=m_rel_bias, norm_ffn1=m_norm_ffn1, ffn1_w_in=m_ffn1_w_in, ffn1_w_out=m_ffn1_w_out,
               norm_mix=m_norm_mix, w_qkv=m_w_qkv, q_norm_a=m_q_norm_a, k_norm_a=m_k_norm_a, q_norm_b=m_q_norm_b,
               k_norm_b=m_k_norm_b, sink_b=m_sink_b, w_o=m_w_o, norm_ffn2=m_norm_ffn2, ffn2_w_in=m_ffn2_w_in,
               ffn2_w_out=m_ffn2_w_out, norm_ple=m_norm_ple, w_ple_gate=m_w_ple_gate, w_ple_proj=m_w_ple_proj)
    var = dict(rel_bias=v_rel_bias, norm_ffn1=v_norm_ffn1, ffn1_w_in=v_ffn1_w_in, ffn1_w_out=v_ffn1_w_out,
               norm_mix=v_norm_mix, w_qkv=v_w_qkv, q_norm_a=v_q_norm_a, k_norm_a=v_k_norm_a, q_norm_b=v_q_norm_b,
               k_norm_b=v_k_norm_b, sink_b=v_sink_b, w_o=v_w_o, norm_ffn2=v_norm_ffn2, ffn2_w_in=v_ffn2_w_in,
               ffn2_w_out=v_ffn2_w_out, norm_ple=v_norm_ple, w_ple_gate=v_w_ple_gate, w_ple_proj=v_w_ple_proj)
    sm = {k: wts[k] for k in SMALL}
    p_dim = p.shape[-1]
    me = 4 * lax.axis_index("x") + 2 * lax.axis_index("y") + lax.axis_index("c")
    packed = []
    for i in range(2):
        a, b = _pack_layer(wts, i)
        packed.append([a.reshape(-1, a.shape[-1]).astype(BF16), b.astype(BF16)])
    a_shape = (2, ffn1_w_in.shape[1], ffn1_w_in.shape[2])

    def weights_of(zones):
        return _layer_weights(zones[0].reshape((N_DEV,) + a_shape), zones[1], p_dim)

    w0 = weights_of([_all_gather(t) for t in packed[0]])
    zone_shapes = [(N_DEV,) + t.shape for t in packed[1]]
    ssem, rsem, thru, zones, token = _exchange_start(packed[1], zone_shapes, False, "gather_start")
    biases = _bias_matrices(rel_bias)
    x1, _, sv0 = _layer_fwd(x[0], p[0, 0], w0, sm, 0, None, TM, biases, dep=token)
    zones = _exchange_wait(ssem, rsem, thru, zones, x1, False, "gather_wait")
    w1 = weights_of([lax.dynamic_update_index_in_dim(z, t, me, 0) for z, t in zip(zones, packed[1])])
    dy, loss, sv1 = _layer_fwd(x1, p[1, 0], w1, sm, 1, loss_target[0], TM, biases)

    def slots_for(arrs):
        return [(N_DEV - 1,) + t.shape[1:] for t in arrs]

    held1, held = {}, {}

    def on_ready1(stage, group):
        held1[stage] = _exchange_start(group, slots_for(group), True, f"scatter1_start_{stage}")
        return held1[stage][4]

    dx1, groups1, gs1 = _layer_bwd(dy, w1, sm, 1, sv1, TM, on_ready=on_ready1)
    on_ready1(2, groups1[2])
    g1 = groups1[0] + groups1[1] + groups1[2]

    def on_ready(stage, group):
        if stage == 1:
            held["slots1"] = [t for st in (0, 1, 2)
                              for t in _exchange_wait(*held1[st][:4], group[0], True, f"scatter1_wait_{st}")]
        held[stage] = _exchange_start(group, slots_for(group), True, f"scatter_start_{stage}")
        return held[stage][4]

    def on_small(gs0):
        part = dict(gs0, norm_ffn1=jnp.zeros_like(gs1["norm_ffn1"]))
        gsmall = _stack_small({k: {0: part[k], 1: gs1[k]} for k in part})
        held["small"] = _all_reduce_small(_pack_small(gsmall, loss[0, :1]))
        return held["small"]

    def on_last(group):
        held["last"] = _exchange_start(group, slots_for(group), True, "scatter_start_2")
        return held["last"][4]

    dx, groups0, gs0 = _layer_bwd(dx1, w0, sm, 0, sv0, TM, dep=held1[2][4], on_ready=on_ready, on_small=on_small,
                                  on_last=on_last)
    last = groups0[2]
    slots0 = [_exchange_wait(*held[stage][:4], last[0], True, f"scatter_wait_{stage}") for stage in (0, 1)]

    def summed(arrs, slots, tiles, dep=None):
        return [_sum_parts(lax.dynamic_index_in_dim(t, me, 0, keepdims=False), s_, tr, dep)
                for t, s_, tr in zip(arrs, slots, tiles)]

    cover = held["last"][4]
    r1 = summed(g1, held["slots1"], SUM_TILES, cover)
    r0 = summed(groups0[0], slots0[0], SUM_TILES[:2], cover) + summed(groups0[1], slots0[1], SUM_TILES[2:3], cover)

    def update(names, layers):
        for k in names:
            grads[k] = jnp.stack([layers[0][k], layers[1][k]])
            delta[k], new_m[k], new_v[k] = _adamw(wts[k], grads[k], mom[k], var[k])

    grads, delta, new_m, new_v = {}, {}, {}, {}
    layer1 = _unpack_layer(r1, wts)
    update([k for k in BIG if k not in LAST_GROUP], [_unpack_layer(r0 + [None, None], wts), layer1])

    slots_last = _exchange_wait(*held["last"][:4], delta["ffn2_w_in"], True, "scatter_wait_2")
    update(LAST_GROUP, [_unpack_layer([None, None, None] + summed(last, slots_last, SUM_TILES[3:]), wts), layer1])
    late = _all_reduce_small(gs0["norm_ffn1"].reshape(-1, 128), dep=slots_last[0])
    small_sum, loss_sum = _unpack_small(held["small"], sm)
    small_sum["norm_ffn1"] = small_sum["norm_ffn1"].at[0].add(late.reshape(-1))
    grads.update(small_sum)
    zeros = {k: jnp.zeros_like(wts[k]) for k in SMALL}
    ds, ms, vs = _adamw(_pack_small(wts), _pack_small(small_sum), _pack_small(mom), _pack_small(var))
    for packed, dst in ((ds, delta), (ms, new_m), (vs, new_v)):
        dst.update(_unpack_small(packed, zeros)[0])

    return (loss_sum, dx[None], *[grads[k] for k in WEIGHTS], *[delta[k] for k in WEIGHTS],
            *[new_m[k] for k in WEIGHTS], *[new_v[k] for k in WEIGHTS])
```

```python
import functools
import math

import jax
import jax.numpy as jnp
from jax import lax
from jax.experimental import pallas as pl
from jax.experimental.pallas import tpu as pltpu

F32 = jnp.float32
BF16 = jnp.bfloat16

N_DEV = 8
HEAD_DIM = 64
PAIR = 2 * HEAD_DIM
BQ = 128
N_BUCKETS = 32
MAX_DISTANCE = 1024
DILATED = ((64, 1), (64, 4), (64, 16))
SWA_RADIUS = 128
EPS = 1e-6
NEG = -1e30
ADAM_LR, ADAM_B1, ADAM_B2, ADAM_EPS, ADAM_WD, ADAM_STEP = 0.001, 0.9, 0.999, 1e-08, 0.01, 10
VMEM_LIMIT = 56 * 1024 * 1024
AXES = ("x", "y", "c")
MESH = pl.DeviceIdType.MESH

BIG = ("ffn1_w_in", "ffn1_w_out", "w_qkv", "w_o", "ffn2_w_in", "ffn2_w_out", "w_ple_gate", "w_ple_proj")
SMALL = ("rel_bias", "norm_ffn1", "norm_mix", "q_norm_a", "k_norm_a", "q_norm_b", "k_norm_b", "sink_b",
         "norm_ffn2", "norm_ple")
WEIGHTS = ("rel_bias", "norm_ffn1", "ffn1_w_in", "ffn1_w_out", "norm_mix", "w_qkv", "q_norm_a", "k_norm_a",
           "q_norm_b", "k_norm_b", "sink_b", "w_o", "norm_ffn2", "ffn2_w_in", "ffn2_w_out", "norm_ple",
           "w_ple_gate", "w_ple_proj")
SMALL_ROWS = 96


def _params(*sem):
    return pltpu.CompilerParams(dimension_semantics=sem, vmem_limit_bytes=VMEM_LIMIT)


def _dot(a, b):
    return jnp.dot(a, b, preferred_element_type=F32)


def _dot_nt(a, b):
    return lax.dot_general(a, b, (((1,), (1,)), ((), ())), preferred_element_type=F32)


def _dot_tn(a, b):
    return lax.dot_general(a, b, (((0,), (0,)), ((), ())), preferred_element_type=F32)


def _sigmoid(x):
    return 1.0 / (1.0 + jnp.exp(-x))


def _rstd(xv):
    return lax.rsqrt(jnp.mean(xv * xv, axis=-1, keepdims=True) + EPS)


def _norm_bwd(dh, xv, gv):
    r = _rstd(xv)
    xn = xv * r
    dg = jnp.sum(dh * xn, axis=0, keepdims=True)
    dxn = dh * gv
    dx = r * (dxn - xn * jnp.mean(dxn * xn, axis=-1, keepdims=True))
    return dx, dg


def _lo_mask(shape):
    return lax.broadcasted_iota(jnp.int32, shape, len(shape) - 1) < HEAD_DIM


def _half_sum(t, lo):
    s0 = jnp.sum(jnp.where(lo, t, 0.0), axis=1, keepdims=True)
    s1 = jnp.sum(jnp.where(lo, 0.0, t), axis=1, keepdims=True)
    return jnp.where(lo, s0, s1)


FFN_PARTS = 2


def _ffn_weight_specs(f, nj, D, C):
    return [pl.BlockSpec((None, None, D, C), lambda i, j: (j, f, 0, 0)),
            pl.BlockSpec((None, None, D, C), lambda i, j: (j + nj, f, 0, 0)),
            pl.BlockSpec((2, C // 2, D), lambda i, j: (j, f, 0))]


def _with_dep(body, dep, in_specs, args):
    if dep is None:
        return body, in_specs, args

    def body_after(dep_ref, *refs):
        body(*refs)

    return body_after, [pl.BlockSpec(memory_space=pl.ANY)] + in_specs, [dep] + args


def _ffn_fwd(x, g, ga, gb, f, tm, dep=None):
    T, D = x.shape
    nj, C = ga.shape[0] // 2, ga.shape[3]

    def body(x_ref, g_ref, wg_ref, wu_ref, wo_ref, xo_ref, h_ref, zg_ref, zu_ref, s_ref, h_scr, acc):
        j = pl.program_id(1)

        @pl.when(j == 0)
        def _():
            xv = x_ref[...]
            hb = (xv * _rstd(xv) * g_ref[...]).astype(BF16)
            h_scr[...] = hb
            h_ref[...] = hb
            acc[...] = jnp.zeros_like(acc)

        wo = wo_ref[...].reshape(C, D)
        for part in range(FFN_PARTS):
            sl = pl.ds(part * (tm // FFN_PARTS), tm // FFN_PARTS)
            hb = h_scr[sl, :]
            gt = _dot(hb, wg_ref[...])
            up = _dot(hb, wu_ref[...])
            s = (gt * _sigmoid(gt) * up).astype(BF16)
            zg_ref[sl, :] = gt.astype(BF16)
            zu_ref[sl, :] = up.astype(BF16)
            s_ref[sl, :] = s
            acc[sl, :] += _dot(s, wo)

        @pl.when(j == nj - 1)
        def _():
            xo_ref[...] = x_ref[...] + 0.5 * acc[...]

    tok = pl.BlockSpec((tm, D), lambda i, j: (i, 0))
    chunk = pl.BlockSpec((None, tm, C), lambda i, j: (j, i, 0))
    in_specs = [tok, pl.BlockSpec((1, D), lambda i, j: (0, 0))] + _ffn_weight_specs(f, nj, D, C)
    body, in_specs, args = _with_dep(body, dep, in_specs, [x, g, ga, ga, gb])
    return pl.pallas_call(
        body, name="ffn_fwd", grid=(T // tm, nj),
        in_specs=in_specs,
        out_specs=[tok, tok, chunk, chunk, chunk],
        out_shape=[jax.ShapeDtypeStruct((T, D), F32), jax.ShapeDtypeStruct((T, D), BF16),
                   jax.ShapeDtypeStruct((nj, T, C), BF16), jax.ShapeDtypeStruct((nj, T, C), BF16),
                   jax.ShapeDtypeStruct((nj, T, C), BF16)],
        scratch_shapes=[pltpu.VMEM((tm, D), BF16), pltpu.VMEM((tm, D), F32)],
        compiler_params=_params("parallel", "arbitrary"),
    )(*args)


def _ffn_bwd(dxo, x, g, zg, zu, ga, gb, f, tm, dep=None):
    T, D = x.shape
    nj, C = ga.shape[0] // 2, ga.shape[3]

    def body(dxo_ref, x_ref, g_ref, zg_ref, zu_ref, wg_ref, wu_ref, wo_ref,
             dx_ref, dy_ref, dzg_ref, dzu_ref, dgn_ref, dy_scr, acc):
        i, j = pl.program_id(0), pl.program_id(1)

        @pl.when(j == 0)
        def _():
            dyb = (0.5 * dxo_ref[...]).astype(BF16)
            dy_scr[...] = dyb
            dy_ref[...] = dyb
            acc[...] = jnp.zeros_like(acc)

        wo = wo_ref[...].reshape(C, D)
        for part in range(FFN_PARTS):
            sl = pl.ds(part * (tm // FFN_PARTS), tm // FFN_PARTS)
            ds = _dot_nt(dy_scr[sl, :], wo)
            gt = zg_ref[sl, :].astype(F32)
            up = zu_ref[sl, :].astype(F32)
            sg = _sigmoid(gt)
            dgt = (ds * up * (sg * (1.0 + gt * (1.0 - sg)))).astype(BF16)
            dup = (ds * (gt * sg)).astype(BF16)
            dzg_ref[sl, :] = dgt
            dzu_ref[sl, :] = dup
            acc[sl, :] += _dot_nt(dgt, wg_ref[...]) + _dot_nt(dup, wu_ref[...])

        @pl.when(j == nj - 1)
        def _():
            dx, dg = _norm_bwd(acc[...], x_ref[...], g_ref[...])
            dx_ref[...] = dxo_ref[...] + dx

            @pl.when(i == 0)
            def _():
                dgn_ref[...] = dg

            @pl.when(i > 0)
            def _():
                dgn_ref[...] += dg

    tok = pl.BlockSpec((tm, D), lambda i, j: (i, 0))
    chunk = pl.BlockSpec((None, tm, C), lambda i, j: (j, i, 0))
    row = pl.BlockSpec((1, D), lambda i, j: (0, 0))
    in_specs = [tok, tok, row, chunk, chunk] + _ffn_weight_specs(f, nj, D, C)
    body, in_specs, args = _with_dep(body, dep, in_specs, [dxo, x, g, zg, zu, ga, ga, gb])
    return pl.pallas_call(
        body, name="ffn_bwd", grid=(T // tm, nj),
        in_specs=in_specs,
        out_specs=[tok, tok, chunk, chunk, row],
        out_shape=[jax.ShapeDtypeStruct((T, D), F32), jax.ShapeDtypeStruct((T, D), BF16),
                   jax.ShapeDtypeStruct((nj, T, C), BF16), jax.ShapeDtypeStruct((nj, T, C), BF16),
                   jax.ShapeDtypeStruct((1, D), F32)],
        scratch_shapes=[pltpu.VMEM((tm, D), BF16), pltpu.VMEM((tm, D), F32)],
        compiler_params=_params("arbitrary", "arbitrary"),
    )(*args)


def _ffn_bwd_dz(dxo, zg, zu, gb, f, tm, dep=None):
    T, D = dxo.shape
    nj, C = zg.shape[0], zg.shape[2]

    def body(dxo_ref, zg_ref, zu_ref, wo_ref, dy_ref, dzg_ref, dzu_ref, dy_scr):
        @pl.when(pl.program_id(1) == 0)
        def _():
            dyb = (0.5 * dxo_ref[...]).astype(BF16)
            dy_scr[...] = dyb
            dy_ref[...] = dyb

        wo = wo_ref[...].reshape(C, D)
        for part in range(FFN_PARTS):
            sl = pl.ds(part * (tm // FFN_PARTS), tm // FFN_PARTS)
            ds = _dot_nt(dy_scr[sl, :], wo)
            gt = zg_ref[sl, :].astype(F32)
            up = zu_ref[sl, :].astype(F32)
            sg = _sigmoid(gt)
            dzg_ref[sl, :] = (ds * up * (sg * (1.0 + gt * (1.0 - sg)))).astype(BF16)
            dzu_ref[sl, :] = (ds * (gt * sg)).astype(BF16)

    tok = pl.BlockSpec((tm, D), lambda i, j: (i, 0))
    chunk = pl.BlockSpec((None, tm, C), lambda i, j: (j, i, 0))
    in_specs = [tok, chunk, chunk, _ffn_weight_specs(f, nj, D, C)[2]]
    body, in_specs, args = _with_dep(body, dep, in_specs, [dxo, zg, zu, gb])
    return pl.pallas_call(
        body, name="ffn_bwd_dz", grid=(T // tm, nj),
        in_specs=in_specs, out_specs=[tok, chunk, chunk],
        out_shape=[jax.ShapeDtypeStruct((T, D), BF16), jax.ShapeDtypeStruct((nj, T, C), BF16),
                   jax.ShapeDtypeStruct((nj, T, C), BF16)],
        scratch_shapes=[pltpu.VMEM((tm, D), BF16)],
        compiler_params=_params("parallel", "arbitrary"),
    )(*args)


def _ffn_bwd_dx(dxo, x, g, dzg, dzu, ga, f, tm, dep=None):
    T, D = x.shape
    nj, C = ga.shape[0] // 2, ga.shape[3]

    def body(dxo_ref, x_ref, g_ref, dzg_ref, dzu_ref, wg_ref, wu_ref, dx_ref, dgn_ref, acc):
        i, j = pl.program_id(0), pl.program_id(1)

        @pl.when(j == 0)
        def _():
            acc[...] = jnp.zeros_like(acc)

        acc[...] += _dot_nt(dzg_ref[...], wg_ref[...]) + _dot_nt(dzu_ref[...], wu_ref[...])

        @pl.when(j == nj - 1)
        def _():
            dx, dg = _norm_bwd(acc[...], x_ref[...], g_ref[...])
            dx_ref[...] = dxo_ref[...] + dx

            @pl.when(i == 0)
            def _():
                dgn_ref[...] = dg

            @pl.when(i > 0)
            def _():
                dgn_ref[...] += dg

    tok = pl.BlockSpec((tm, D), lambda i, j: (i, 0))
    chunk = pl.BlockSpec((None, tm, C), lambda i, j: (j, i, 0))
    row = pl.BlockSpec((1, D), lambda i, j: (0, 0))
    in_specs = [tok, tok, row, chunk, chunk] + _ffn_weight_specs(f, nj, D, C)[:2]
    body, in_specs, args = _with_dep(body, dep, in_specs, [dxo, x, g, dzg, dzu, ga, ga])
    return pl.pallas_call(
        body, name="ffn_bwd_dx", grid=(T // tm, nj),
        in_specs=in_specs, out_specs=[tok, row],
        out_shape=[jax.ShapeDtypeStruct((T, D), F32), jax.ShapeDtypeStruct((1, D), F32)],
        scratch_shapes=[pltpu.VMEM((tm, D), F32)],
        compiler_params=_params("arbitrary", "arbitrary"),
    )(*args)


def _ffn_dw(h, dzg, dzu, s, dy, tk, dep=None):
    T, D = h.shape
    nj, C = s.shape[0], s.shape[2]
    nk = T // tk

    def body(h_ref, dzg_ref, dzu_ref, s_ref, dy_ref, dwin_ref, dwo_ref, ag, au, ao):
        k = pl.program_id(1)

        @pl.when(k == 0)
        def _():
            ag[...] = jnp.zeros_like(ag)
            au[...] = jnp.zeros_like(au)
            ao[...] = jnp.zeros_like(ao)

        hb = h_ref[...]
        ag[...] += _dot_tn(hb, dzg_ref[...])
        au[...] += _dot_tn(hb, dzu_ref[...])
        ao[...] += _dot_tn(s_ref[...], dy_ref[...])

        @pl.when(k == nk - 1)
        def _():
            dwin_ref[0] = ag[...].astype(BF16)
            dwin_ref[1] = au[...].astype(BF16)
            dwo_ref[...] = ao[...].astype(BF16)

    tok = pl.BlockSpec((tk, D), lambda j, k: (k, 0))
    chunk = pl.BlockSpec((None, tk, C), lambda j, k: (j, k, 0))
    body, in_specs, args = _with_dep(body, dep, [tok, chunk, chunk, chunk, tok], [h, dzg, dzu, s, dy])
    dwin, dwo = pl.pallas_call(
        body, name="ffn_dw", grid=(nj, nk),
        in_specs=in_specs,
        out_specs=[pl.BlockSpec((2, None, D, C), lambda j, k: (0, j, 0, 0)),
                   pl.BlockSpec((None, C, D), lambda j, k: (j, 0, 0))],
        out_shape=[jax.ShapeDtypeStruct((2, nj, D, C), BF16), jax.ShapeDtypeStruct((nj, C, D), BF16)],
        scratch_shapes=[pltpu.VMEM((D, C), F32), pltpu.VMEM((D, C), F32), pltpu.VMEM((C, D), F32)],
        compiler_params=_params("parallel", "arbitrary"),
    )(*args)
    return dwin.reshape(2 * nj, D, C), dwo


def _matmul_tn(a, b, tn, tk):
    T, Ka = a.shape
    N = b.shape[1]
    nk = T // tk

    def body(a_ref, b_ref, o_ref, acc):
        k = pl.program_id(1)

        @pl.when(k == 0)
        def _():
            acc[...] = jnp.zeros_like(acc)

        acc[...] += _dot_tn(a_ref[...], b_ref[...])

        @pl.when(k == nk - 1)
        def _():
            o_ref[...] = acc[...].astype(BF16)

    return pl.pallas_call(
        body, name="matmul_tn", grid=(N // tn, nk),
        in_specs=[pl.BlockSpec((tk, Ka), lambda n, k: (k, 0)), pl.BlockSpec((tk, tn), lambda n, k: (k, n))],
        out_specs=pl.BlockSpec((Ka, tn), lambda n, k: (0, n)),
        out_shape=jax.ShapeDtypeStruct((Ka, N), BF16),
        scratch_shapes=[pltpu.VMEM((Ka, tn), F32)],
        compiler_params=_params("parallel", "arbitrary"),
    )(a, b)


def _qkv_fwd(x, g, w, tm):
    T, D = x.shape
    N = w.shape[1]

    def body(x_ref, g_ref, w_ref, o_ref, h_ref):
        xv = x_ref[...]
        hb = (xv * _rstd(xv) * g_ref[...]).astype(BF16)
        h_ref[...] = hb
        o_ref[...] = _dot(hb, w_ref[...])

    return pl.pallas_call(
        body, name="qkv_fwd", grid=(T // tm,),
        in_specs=[pl.BlockSpec((tm, D), lambda i: (i, 0)), pl.BlockSpec((1, D), lambda i: (0, 0)),
                  pl.BlockSpec((D, N), lambda i: (0, 0))],
        out_specs=[pl.BlockSpec((tm, N), lambda i: (i, 0)), pl.BlockSpec((tm, D), lambda i: (i, 0))],
        out_shape=[jax.ShapeDtypeStruct((T, N), F32), jax.ShapeDtypeStruct((T, D), BF16)],
        compiler_params=_params("parallel"),
    )(x, g, w)


DILS = tuple(d for _, d in DILATED)


def _spread_specs(tm, T, dtype):
    specs = [pl.BlockSpec((4, d, tm // d, PAIR), lambda i: (0, 0, i, 0)) for d in DILS]
    shapes = [jax.ShapeDtypeStruct((4, d, T // d, PAIR), dtype) for d in DILS]
    return specs, shapes


def _spread(tile, y, outs, c, dtype):
    tm = y.shape[0]
    tile[...] = y
    for out, d in zip(outs, DILS):
        for r in range(d):
            out[c, r] = tile[pl.ds(r, tm // d, stride=d), :].astype(dtype)


def _collect(tile, ins, c):
    tm = tile.shape[0]
    first = True
    for ref, d in zip(ins, DILS):
        for r in range(d):
            rows = pl.ds(r, tm // d, stride=d) if d > 1 else pl.ds(0, tm)
            part = ref[c, r].astype(F32)
            tile[rows, :] = part if first else tile[rows, :] + part
        first = False
    return tile[...]


def _attn_prep(qkv, gains2, tm):
    T = qkv.shape[0]
    scale = HEAD_DIM ** -0.5
    n = len(DILS)

    def body(qkv_ref, g_ref, qb_ref, kb_ref, vb_ref, *rest):
        outs, tile = rest[:-1], rest[-1]
        lo = _lo_mask((tm, PAIR))

        def spread(kind, c, y):
            _spread(tile, y, outs[kind * n:(kind + 1) * n], c, BF16)

        def normed(c, gi, mult):
            xv = qkv_ref[:, c * PAIR:(c + 1) * PAIR]
            r = lax.rsqrt(_half_sum(xv * xv, lo) * (1.0 / HEAD_DIM) + EPS)
            y = xv * r * g_ref[gi:gi + 1, :]
            return y * mult if mult != 1.0 else y

        def both_halves(v):
            sw = pltpu.roll(v, HEAD_DIM, 1)
            return jnp.where(lo, v, sw), jnp.where(lo, sw, v)

        for c in range(4):
            spread(0, c, normed(c, 0, scale))
            spread(1, c, normed(4 + c, 1, 1.0))
            spread(2, c, qkv_ref[:, (8 + c) * PAIR:(9 + c) * PAIR])
            qb_ref[c] = normed(12 + c, 2, scale).astype(BF16)
        k0, k1 = both_halves(normed(16, 3, 1.0))
        kb_ref[0] = k0.astype(BF16)
        kb_ref[1] = k1.astype(BF16)
        v0, v1 = both_halves(qkv_ref[:, 17 * PAIR:18 * PAIR])
        vb_ref[0] = v0.astype(BF16)
        vb_ref[1] = v1.astype(BF16)

    four = pl.BlockSpec((4, tm, PAIR), lambda i: (0, i, 0))
    two = pl.BlockSpec((2, tm, PAIR), lambda i: (0, i, 0))
    s4 = jax.ShapeDtypeStruct((4, T, PAIR), BF16)
    s2 = jax.ShapeDtypeStruct((2, T, PAIR), BF16)
    specs, shapes = _spread_specs(tm, T, BF16)
    res = pl.pallas_call(
        body, name="attn_prep", grid=(T // tm,),
        in_specs=[pl.BlockSpec((tm, qkv.shape[1]), lambda i: (i, 0)), pl.BlockSpec((4, PAIR), lambda i: (0, 0))],
        out_specs=[four, two, two] + specs * 3,
        out_shape=[s4, s2, s2] + shapes * 3,
        scratch_shapes=[pltpu.VMEM((tm, PAIR), F32)],
        compiler_params=_params("parallel"),
    )(qkv, gains2)
    qb, kb, vb = res[:3]
    per_d = [tuple(res[3 + kind * n + di].reshape(4 * d, T // d, PAIR) for kind in range(3))
             for di, d in enumerate(DILS)]
    return qb, kb, vb, per_d


def _loop_blocks(nb, body, init, per_iter):
    u = math.gcd(nb, per_iter)

    def outer(i, carry):
        for k in range(u):
            carry = body(i * u + k, carry)
        return carry

    return lax.fori_loop(0, nb // u, outer, init)


def _key_window(b, nb, L, R, W):
    start = pl.multiple_of(jnp.clip(b * BQ - R, 0, L - W), HEAD_DIM)
    return start, jnp.where(b == 0, 1, jnp.where(b == nb - 1, 2, 0))


def _stack_heads(v, lo):
    z = jnp.zeros_like(v)
    return jnp.concatenate([jnp.where(lo, v, z), jnp.where(lo, z, v)], axis=0)


def _unstack_heads(v2, lo):
    return jnp.where(lo, v2[:BQ], v2[BQ:])


def _row_vector(v, lo):
    r = lax.broadcasted_iota(jnp.int32, (BQ, PAIR), 0)
    ln = lax.broadcasted_iota(jnp.int32, (BQ, PAIR), 1)
    diag = (ln % HEAD_DIM) == (r % HEAD_DIM)
    top = jnp.sum(jnp.where(diag & (r < HEAD_DIM), v, 0.0), axis=0, keepdims=True)
    bot = jnp.sum(jnp.where(diag & (r >= HEAD_DIM), v, 0.0), axis=0, keepdims=True)
    top8, bot8 = jnp.broadcast_to(top, (8, PAIR)), jnp.broadcast_to(bot, (8, PAIR))
    lo8 = _lo_mask((8, PAIR))
    head0 = jnp.where(lo8, top8, pltpu.roll(bot8, HEAD_DIM, 1))
    head1 = jnp.where(lo8, pltpu.roll(top8, HEAD_DIM, 1), bot8)
    return jnp.concatenate([head0, head1], axis=1)[:1]


def _units_per_step(nb, pairs_per_kv):
    return max(1, 16 // nb) if pairs_per_kv == 1 else 1


def _attn_fwd(q, kp, vp, bias4, sink, R, pairs_per_kv, pairs_per_bias):
    N, L, _ = q.shape
    W = BQ + 2 * R
    nb = L // BQ
    assert L >= W and nb >= 2
    G = _units_per_step(nb, pairs_per_kv)

    def body(sink_ref, q_ref, k_ref, v_ref, bias_ref, o_ref, lse_ref):
        n = pl.program_id(0)
        lo_q = _lo_mask((BQ, PAIR))
        first = lax.broadcasted_iota(jnp.int32, (2 * BQ, 1), 0) < BQ

        def blk(f, carry):
            g, b = f // nb, f % nb
            u = n * G + g
            sk = jnp.where(first, sink_ref[2 * u], sink_ref[2 * u + 1])
            q0 = pl.multiple_of(b * BQ, BQ)
            q2 = _stack_heads(q_ref[g, pl.ds(q0, BQ), :], lo_q)
            k0, variant = _key_window(b, nb, L, R, W)
            kw = k_ref[g, pl.ds(k0, W), :]
            vw = v_ref[g, pl.ds(k0, W), :]
            s = _dot_nt(q2, kw) + bias_ref[variant]
            m = jnp.maximum(jnp.max(s, axis=1, keepdims=True), sk)
            p = jnp.exp(s - m)
            l = jnp.sum(p, axis=1, keepdims=True) + jnp.exp(sk - m)
            o2 = _dot(p.astype(BF16), vw) / l
            o_ref[g, pl.ds(q0, BQ), :] = _unstack_heads(o2, lo_q)
            lse_ref[g, pl.ds(q0, BQ), :] = _unstack_heads(jnp.broadcast_to(m + jnp.log(l), (2 * BQ, PAIR)), lo_q)
            return carry

        _loop_blocks(G * nb, blk, 0, 4)

    qspec = pl.BlockSpec((G, L, PAIR), lambda n: (n, 0, 0))
    kspec = pl.BlockSpec((G, L, PAIR), lambda n: (n // pairs_per_kv, 0, 0))
    return pl.pallas_call(
        body, name="attn_fwd", grid=(N // G,),
        in_specs=[pl.BlockSpec(memory_space=pltpu.SMEM), qspec, kspec, kspec,
                  pl.BlockSpec((None, 3, 2 * BQ, W), lambda n: (n * G // pairs_per_bias, 0, 0, 0))],
        out_specs=[qspec, qspec],
        out_shape=[jax.ShapeDtypeStruct((N, L, PAIR), F32), jax.ShapeDtypeStruct((N, L, PAIR), F32)],
        compiler_params=_params("parallel"),
    )(sink, q, kp, vp, bias4)


def _attn_bwd(q, kp, vp, bias4t, sink, o, lse, do, R, pairs_per_kv, pairs_per_bias):
    N, L, _ = q.shape
    Nk = kp.shape[0]
    Pb = bias4t.shape[0]
    W = BQ + 2 * R
    nb = L // BQ
    assert L >= W and nb >= 2
    G = _units_per_step(nb, pairs_per_kv)

    def body(sink_ref, q_ref, k_ref, v_ref, bias_ref, o_ref, lse_ref, do_ref,
             dq_ref, dk_ref, dv_ref, dbias_ref, dsink_ref, dk_acc, dv_acc):
        n = pl.program_id(0)
        lo_q = _lo_mask((BQ, PAIR))
        first = lax.broadcasted_iota(jnp.int32, (1, 2 * BQ), 1) < BQ
        dsink_ref[...] = jnp.zeros_like(dsink_ref)

        @pl.when(n % pairs_per_kv == 0)
        def _():
            dk_acc[...] = jnp.zeros_like(dk_acc)
            dv_acc[...] = jnp.zeros_like(dv_acc)

        @pl.when((n * G) % pairs_per_bias == 0)
        def _():
            dbias_ref[...] = jnp.zeros_like(dbias_ref)

        def blk(f, carry):
            g, b = f // nb, f % nb
            u = n * G + g
            sk = jnp.where(first, sink_ref[2 * u], sink_ref[2 * u + 1])
            q0 = pl.multiple_of(b * BQ, BQ)
            q2 = _stack_heads(q_ref[g, pl.ds(q0, BQ), :], lo_q)
            k0, variant = _key_window(b, nb, L, R, W)
            kw = k_ref[g, pl.ds(k0, W), :]
            vw = v_ref[g, pl.ds(k0, W), :]
            dov = do_ref[g, pl.ds(q0, BQ), :]
            lse = _row_vector(lse_ref[g, pl.ds(q0, BQ), :], lo_q)
            delta = _row_vector(_half_sum(dov.astype(F32) * o_ref[g, pl.ds(q0, BQ), :], lo_q), lo_q)
            do2 = _stack_heads(dov.astype(BF16), lo_q)
            st = _dot_nt(kw, q2) + bias_ref[variant]
            pt = jnp.exp(st - lse)
            dst = pt * (_dot_nt(vw, do2) - delta)
            dstb = dst.astype(BF16)
            dbias_ref[variant] += dst
            dk_acc[g, pl.ds(k0, W), :] += _dot(dstb, q2)
            dv_acc[g, pl.ds(k0, W), :] += _dot(pt.astype(BF16), do2)
            dq_ref[g, pl.ds(q0, BQ), :] = _unstack_heads(_dot_tn(dstb, kw), lo_q).astype(BF16)
            dsink_ref[g, pl.ds(0, 1), :] -= jnp.exp(sk - lse) * delta
            return carry

        _loop_blocks(G * nb, blk, 0, 4)
        dk_ref[...] = dk_acc[...].astype(BF16)
        dv_ref[...] = dv_acc[...].astype(BF16)

    qspec = pl.BlockSpec((G, L, PAIR), lambda n: (n, 0, 0))
    kspec = pl.BlockSpec((G, L, PAIR), lambda n: (n // pairs_per_kv, 0, 0))
    return pl.pallas_call(
        body, name="attn_bwd", grid=(N // G,),
        in_specs=[pl.BlockSpec(memory_space=pltpu.SMEM), qspec, kspec, kspec,
                  pl.BlockSpec((None, 3, W, 2 * BQ), lambda n: (n * G // pairs_per_bias, 0, 0, 0)),
                  qspec, qspec, qspec],
        out_specs=[qspec, kspec, kspec,
                   pl.BlockSpec((None, 3, W, 2 * BQ), lambda n: (n * G // pairs_per_bias, 0, 0, 0)),
                   pl.BlockSpec((G, 8, 2 * BQ), lambda n: (n, 0, 0))],
        out_shape=[jax.ShapeDtypeStruct((N, L, PAIR), BF16),
                   jax.ShapeDtypeStruct((Nk, L, PAIR), BF16),
                   jax.ShapeDtypeStruct((Nk, L, PAIR), BF16),
                   jax.ShapeDtypeStruct((Pb, 3, W, 2 * BQ), F32),
                   jax.ShapeDtypeStruct((N, 8, 2 * BQ), F32)],
        scratch_shapes=[pltpu.VMEM((G, L, PAIR), F32), pltpu.VMEM((G, L, PAIR), F32)],
        compiler_params=_params("arbitrary"),
    )(sink, q, kp, vp, bias4t, o, lse, do)


def _attn_merge(branch_outs, ob, tm):
    T = ob.shape[1]
    n = len(DILS)

    def body(*refs):
        o_in, l_in, ob_ref = refs[:n], refs[n:2 * n], refs[2 * n]
        o_out, l_out, cat_ref = refs[2 * n + 1:3 * n + 1], refs[3 * n + 1:4 * n + 1], refs[4 * n + 1]
        tiles = refs[4 * n + 2:]
        for c in range(4):
            o_nat, l_nat = [], []
            for di, d in enumerate(DILS):
                for kind, (src, dst) in enumerate(((o_in[di], o_nat), (l_in[di], l_nat))):
                    tile = tiles[2 * di + kind]
                    if d == 1:
                        dst.append(src[c, 0])
                    else:
                        for r in range(d):
                            tile[pl.ds(r, tm // d, stride=d), :] = src[c, r]
                        dst.append(tile[...])
            m = functools.reduce(jnp.maximum, l_nat)
            ws = [jnp.exp(l - m) for l in l_nat]
            z = sum(ws)
            o = sum(w * t for w, t in zip(ws, o_nat)) / z
            cat_ref[:, c * PAIR:(c + 1) * PAIR] = o.astype(BF16)
            cat_ref[:, (4 + c) * PAIR:(5 + c) * PAIR] = ob_ref[c].astype(BF16)
            _spread(tiles[0], o, o_out, c, F32)
            _spread(tiles[1], m + jnp.log(z), l_out, c, F32)

    specs, shapes = _spread_specs(tm, T, F32)
    four = pl.BlockSpec((4, tm, PAIR), lambda i: (0, i, 0))
    o_views = [o.reshape(4, d, T // d, PAIR) for (o, _), d in zip(branch_outs, DILS)]
    l_views = [l.reshape(4, d, T // d, PAIR) for (_, l), d in zip(branch_outs, DILS)]
    res = pl.pallas_call(
        body, name="attn_merge", grid=(T // tm,),
        in_specs=specs + specs + [four],
        out_specs=specs + specs + [pl.BlockSpec((tm, 8 * PAIR), lambda i: (i, 0))],
        out_shape=shapes + shapes + [jax.ShapeDtypeStruct((T, 8 * PAIR), BF16)],
        scratch_shapes=[pltpu.VMEM((tm, PAIR), F32)] * (2 * n),
        compiler_params=_params("parallel"),
    )(*o_views, *l_views, ob)
    merged = [(res[di].reshape(4 * d, T // d, PAIR), res[n + di].reshape(4 * d, T // d, PAIR))
              for di, d in enumerate(DILS)]
    return merged, res[2 * n]


def _weight_arg(w, blk):
    if blk is None:
        return pl.BlockSpec(w.shape, lambda i: (0, 0)), (lambda ref: ref[...])
    D = w.shape[2]
    return (pl.BlockSpec((N_DEV, 128, D), lambda i: (0, blk, 0)),
            lambda ref: ref[...].reshape(N_DEV * 128, D))


def _oproj_fwd(x, o_cat, w, blk, tm):
    T, D = x.shape
    wspec, wload = _weight_arg(w, blk)

    def body(x_ref, o_ref, w_ref, out_ref):
        out_ref[...] = x_ref[...] + _dot(o_ref[...], wload(w_ref))

    tok = pl.BlockSpec((tm, D), lambda i: (i, 0))
    return pl.pallas_call(
        body, name="oproj_fwd", grid=(T // tm,),
        in_specs=[tok, pl.BlockSpec((tm, o_cat.shape[1]), lambda i: (i, 0)), wspec],
        out_specs=tok, out_shape=jax.ShapeDtypeStruct((T, D), F32),
        compiler_params=_params("parallel"),
    )(x, o_cat, w)


def _oproj_bwd(dx, w, blk, tm, dep=None):
    T, D = dx.shape
    wspec, wload = _weight_arg(w, blk)

    def body(dx_ref, w_ref, dxb_ref, dob_ref, *rest):
        doa_refs, tile = rest[:-1], rest[-1]
        db = dx_ref[...].astype(BF16)
        dxb_ref[...] = db
        do = _dot_nt(db, wload(w_ref))
        for c in range(4):
            _spread(tile, do[:, c * PAIR:(c + 1) * PAIR], doa_refs, c, BF16)
            dob_ref[c] = do[:, (4 + c) * PAIR:(5 + c) * PAIR].astype(BF16)

    tok = pl.BlockSpec((tm, D), lambda i: (i, 0))
    specs, shapes = _spread_specs(tm, T, BF16)
    body, in_specs, args = _with_dep(body, dep, [tok, wspec], [dx, w])
    res = pl.pallas_call(
        body, name="oproj_bwd", grid=(T // tm,),
        in_specs=in_specs,
        out_specs=[tok, pl.BlockSpec((4, tm, PAIR), lambda i: (0, i, 0))] + specs,
        out_shape=[jax.ShapeDtypeStruct((T, D), BF16), jax.ShapeDtypeStruct((4, T, PAIR), BF16)] + shapes,
        scratch_shapes=[pltpu.VMEM((tm, PAIR), F32)],
        compiler_params=_params("parallel"),
    )(*args)
    return res[0], res[1], [t.reshape(4 * d, T // d, PAIR) for t, d in zip(res[2:], DILS)]


def _attn_post(qkv, gains2, dqa, dka, dva, dqb, dkb, dvb, tm):
    T, NQ = qkv.shape
    scale = HEAD_DIM ** -0.5

    n = len(DILS)

    def body(qkv_ref, g_ref, *rest):
        dq_refs, dk_refs, dv_refs = rest[:n], rest[n:2 * n], rest[2 * n:3 * n]
        qb_ref, kb_ref, vb_ref, out_ref, dg_ref, tile = rest[3 * n:]
        lo = _lo_mask((tm, PAIR))

        @pl.when(pl.program_id(0) == 0)
        def _():
            dg_ref[...] = jnp.zeros_like(dg_ref)

        def norm_bwd(c, gi, dy):
            xv = qkv_ref[:, c * PAIR:(c + 1) * PAIR]
            r = lax.rsqrt(_half_sum(xv * xv, lo) * (1.0 / HEAD_DIM) + EPS)
            xn = xv * r
            dg_ref[gi:gi + 1, :] += jnp.sum(dy * xn, axis=0, keepdims=True)
            dxn = dy * g_ref[gi:gi + 1, :]
            dx = r * (dxn - xn * (_half_sum(dxn * xn, lo) * (1.0 / HEAD_DIM)))
            out_ref[:, c * PAIR:(c + 1) * PAIR] = dx.astype(BF16)

        def fold(v):
            return v + pltpu.roll(v, HEAD_DIM, 1)

        for c in range(4):
            norm_bwd(c, 0, _collect(tile, dq_refs, c) * scale)
            norm_bwd(4 + c, 1, _collect(tile, dk_refs, c))
            out_ref[:, (8 + c) * PAIR:(9 + c) * PAIR] = _collect(tile, dv_refs, c).astype(BF16)
            norm_bwd(12 + c, 2, qb_ref[c].astype(F32) * scale)
        kb, vb = kb_ref[...].astype(F32), vb_ref[...].astype(F32)
        norm_bwd(16, 3, jnp.where(lo, fold(kb[0]), fold(kb[1])))
        out_ref[:, 17 * PAIR:18 * PAIR] = jnp.where(lo, fold(vb[0]), fold(vb[1])).astype(BF16)

    four = pl.BlockSpec((4, tm, PAIR), lambda i: (0, i, 0))
    two = pl.BlockSpec((2, tm, PAIR), lambda i: (0, i, 0))
    specs, _ = _spread_specs(tm, T, BF16)
    views = [t.reshape(4, d, T // d, PAIR) for group in (dqa, dka, dva) for t, d in zip(group, DILS)]
    return pl.pallas_call(
        body, name="attn_post", grid=(T // tm,),
        in_specs=[pl.BlockSpec((tm, NQ), lambda i: (i, 0)), pl.BlockSpec((4, PAIR), lambda i: (0, 0))]
        + specs * 3 + [four, two, two],
        out_specs=[pl.BlockSpec((tm, NQ), lambda i: (i, 0)), pl.BlockSpec((4, PAIR), lambda i: (0, 0))],
        out_shape=[jax.ShapeDtypeStruct((T, NQ), BF16), jax.ShapeDtypeStruct((4, PAIR), F32)],
        scratch_shapes=[pltpu.VMEM((tm, PAIR), F32)],
        compiler_params=_params("arbitrary"),
    )(qkv, gains2, *views, dqb, dkb, dvb)


def _dense_norm_bwd(dres, dz, w, blk, x, g, tm):
    T, D = x.shape
    N = dz.shape[1]
    wspec, wload = _weight_arg(w, blk)

    def body(dres_ref, dz_ref, w_ref, x_ref, g_ref, dx_ref, dgn_ref):
        i = pl.program_id(0)
        dx, dg = _norm_bwd(_dot_nt(dz_ref[...], wload(w_ref)), x_ref[...], g_ref[...])
        dx_ref[...] = dres_ref[...] + dx

        @pl.when(i == 0)
        def _():
            dgn_ref[...] = dg

        @pl.when(i > 0)
        def _():
            dgn_ref[...] += dg

    tok = pl.BlockSpec((tm, D), lambda i: (i, 0))
    row = pl.BlockSpec((1, D), lambda i: (0, 0))
    return pl.pallas_call(
        body, name="dense_norm_bwd", grid=(T // tm,),
        in_specs=[tok, pl.BlockSpec((tm, N), lambda i: (i, 0)), wspec, tok, row],
        out_specs=[tok, row],
        out_shape=[jax.ShapeDtypeStruct((T, D), F32), jax.ShapeDtypeStruct((1, D), F32)],
        compiler_params=_params("arbitrary"),
    )(dres, dz, w, x, g)


def _bias_reduce(onehot, dbm):
    Hb, K = dbm.shape

    def body(oh_ref, d_ref, out_ref):
        oh = oh_ref[...]
        d = d_ref[...]
        hi = d.astype(BF16)
        r1 = d - hi.astype(F32)
        mid = r1.astype(BF16)
        low = (r1 - mid.astype(F32)).astype(BF16)
        out_ref[...] = _dot_nt(hi, oh) + _dot_nt(mid, oh) + _dot_nt(low, oh)

    vm = pl.BlockSpec(memory_space=pltpu.VMEM)
    return pl.pallas_call(
        body, name="bias_reduce", in_specs=[vm, vm], out_specs=vm,
        out_shape=jax.ShapeDtypeStruct((Hb, 128), F32),
        compiler_params=pltpu.CompilerParams(vmem_limit_bytes=VMEM_LIMIT),
    )(onehot, dbm)


def _ple_fwd(x, g, wg, blk, p, wp, target, tm):
    T, D = x.shape
    P = p.shape[1]
    with_loss = target is not None
    wspec, wload = _weight_arg(wg, blk)

    def body(*refs):
        if with_loss:
            x_ref, g_ref, wg_ref, p_ref, wp_ref, t_ref, y_ref, hn_ref, gate_ref, pp_ref, pb_ref, loss_ref = refs
        else:
            x_ref, g_ref, wg_ref, p_ref, wp_ref, y_ref, hn_ref, gate_ref, pp_ref, pb_ref = refs
        i = pl.program_id(0)
        xv = x_ref[...]
        hb = (xv * _rstd(xv) * g_ref[...]).astype(BF16)
        hn_ref[...] = hb
        gate = _sigmoid(_dot(hb, wload(wg_ref)))
        pb = p_ref[...].astype(BF16)
        pb_ref[...] = pb
        pp = _dot(pb, wp_ref[...])
        gate_ref[...] = gate
        pp_ref[...] = pp
        y = xv + gate * pp
        if with_loss:
            err = y - t_ref[...]
            y_ref[...] = err * (1.0 / D)
            part = jnp.broadcast_to(0.5 * jnp.sum(jnp.sum(err * err, axis=1, keepdims=True) * (1.0 / D),
                                                  axis=0, keepdims=True), (1, 128))

            @pl.when(i == 0)
            def _():
                loss_ref[...] = part

            @pl.when(i > 0)
            def _():
                loss_ref[...] += part
        else:
            y_ref[...] = y

    tok = pl.BlockSpec((tm, D), lambda i: (i, 0))
    ptok = pl.BlockSpec((tm, P), lambda i: (i, 0))
    in_specs = [tok, pl.BlockSpec((1, D), lambda i: (0, 0)), wspec, ptok,
                pl.BlockSpec((P, D), lambda i: (0, 0))]
    out_specs = [tok, tok, tok, tok, ptok]
    out_shape = [jax.ShapeDtypeStruct((T, D), F32), jax.ShapeDtypeStruct((T, D), BF16),
                 jax.ShapeDtypeStruct((T, D), F32), jax.ShapeDtypeStruct((T, D), F32),
                 jax.ShapeDtypeStruct((T, P), BF16)]
    args = [x, g, wg, p, wp]
    if with_loss:
        in_specs.append(tok)
        out_specs.append(pl.BlockSpec((1, 128), lambda i: (0, 0)))
        out_shape.append(jax.ShapeDtypeStruct((1, 128), F32))
        args.append(target)
    return pl.pallas_call(
        body, name="ple_fwd_loss" if with_loss else "ple_fwd", grid=(T // tm,),
        in_specs=in_specs, out_specs=out_specs, out_shape=out_shape,
        compiler_params=_params("arbitrary" if with_loss else "parallel"),
    )(*args)


def _ple_bwd(dy, gate, pp, tm, dep=None):
    T, D = dy.shape

    def body(dy_ref, gate_ref, pp_ref, dgl_ref, dpp_ref):
        d = dy_ref[...]
        gt = gate_ref[...]
        dgl_ref[...] = (d * pp_ref[...] * gt * (1.0 - gt)).astype(BF16)
        dpp_ref[...] = (d * gt).astype(BF16)

    tok = pl.BlockSpec((tm, D), lambda i: (i, 0))
    body, in_specs, args = _with_dep(body, dep, [tok, tok, tok], [dy, gate, pp])
    return pl.pallas_call(
        body, name="ple_bwd", grid=(T // tm,), in_specs=in_specs, out_specs=[tok, tok],
        out_shape=[jax.ShapeDtypeStruct((T, D), BF16), jax.ShapeDtypeStruct((T, D), BF16)],
        compiler_params=_params("parallel"),
    )(*args)


def _adamw(w, g, m, v):
    shape = w.shape
    C = shape[-1]
    w2, g2, m2, v2 = (a.reshape(-1, C) for a in (w, g, m, v))
    Rn = w2.shape[0]
    tr = Rn
    for cand in (512, 352, 256):
        if Rn % cand == 0:
            tr = cand
            break
    c1 = 1.0 - ADAM_B1 ** ADAM_STEP
    c2 = 1.0 - ADAM_B2 ** ADAM_STEP

    def body(w_ref, g_ref, m_ref, v_ref, d_ref, nm_ref, nv_ref):
        gv = g_ref[...]
        mn = ADAM_B1 * m_ref[...] + (1.0 - ADAM_B1) * gv
        vn = ADAM_B2 * v_ref[...] + (1.0 - ADAM_B2) * (gv * gv)
        d_ref[...] = -ADAM_LR * ((mn / c1) / (jnp.sqrt(vn / c2) + ADAM_EPS) + ADAM_WD * w_ref[...])
        nm_ref[...] = mn
        nv_ref[...] = vn

    spec = pl.BlockSpec((tr, C), lambda i: (i, 0))
    sh = jax.ShapeDtypeStruct((Rn, C), F32)
    d, nm, nv = pl.pallas_call(
        body, name="adamw", grid=(Rn // tr,), in_specs=[spec] * 4, out_specs=[spec] * 3, out_shape=[sh] * 3,
        compiler_params=_params("parallel"),
    )(w2, g2, m2, v2)
    return d.reshape(shape), nm.reshape(shape), nv.reshape(shape)


def _my_place():
    x, y, c = lax.axis_index("x"), lax.axis_index("y"), lax.axis_index("c")
    chips = [(1 - x, y), (x, 1 - y), (1 - x, 1 - y)]
    return x, y, c, chips


def _all_gather(flat):
    R, Wd = flat.shape

    def body(x_ref, out_ref, send_sems, recv_sems, local_sem):
        x, y, c, chips = _my_place()
        me, sibling = (x, y, c), (x, y, 1 - c)

        def rows(px, py, pc):
            return out_ref.at[4 * px + 2 * py + pc]

        def copy(k, block, to, src=None):
            return pltpu.make_async_remote_copy(
                src_ref=rows(*block) if src is None else src, dst_ref=rows(*block),
                send_sem=send_sems.at[k], recv_sem=recv_sems.at[k], device_id=to, device_id_type=MESH)

        mine = pltpu.make_async_copy(x_ref, rows(*me), local_sem)
        mine.start()
        first = [copy(0, me, sibling, src=x_ref)]
        first += [copy(1 + j, me, (*chip, c), src=x_ref) for j, chip in enumerate(chips)]
        for cp in first:
            cp.start()
        passed = [copy(4 + j, (*chip, c), sibling) for j, chip in enumerate(chips)]
        for j, chip in enumerate(chips):
            copy(1 + j, (*chip, c), me).wait_recv()
            passed[j].start()
        copy(0, sibling, me).wait_recv()
        for j, chip in enumerate(chips):
            copy(4 + j, (*chip, 1 - c), me).wait_recv()
        for cp in first + passed:
            cp.wait_send()
        mine.wait()

    return pl.pallas_call(
        body, name="all_gather",
        in_specs=[pl.BlockSpec(memory_space=pl.ANY)], out_specs=pl.BlockSpec(memory_space=pl.ANY),
        out_shape=jax.ShapeDtypeStruct((N_DEV, R, Wd), flat.dtype),
        scratch_shapes=[pltpu.SemaphoreType.DMA((7,)), pltpu.SemaphoreType.DMA((7,)), pltpu.SemaphoreType.DMA],
    )(flat)


def _peer(x, y, c, k):
    return (x ^ ((k >> 2) & 1), y ^ ((k >> 1) & 1), c ^ (k & 1))


HBM_SPEC = pl.BlockSpec(memory_space=pltpu.HBM)
SEM_SPEC = pl.BlockSpec(memory_space=pltpu.SEMAPHORE)


def _exchange_refs(srcs, lands, m, k, x, y, c, scatter):
    peer = _peer(x, y, c, k)
    if scatter:
        return srcs[m].at[4 * peer[0] + 2 * peer[1] + peer[2]], lands[m].at[k - 1], peer
    return srcs[m], lands[m].at[4 * x + 2 * y + c], peer


def _exchange_start(arrs, land_shapes, scatter, name):
    n = len(arrs)

    def body(*refs):
        srcs, lands = refs[:n], refs[n:2 * n]
        send_sems, recv_sems = refs[2 * n], refs[2 * n + 1]
        token = refs[-1]
        x, y, c, _ = _my_place()
        for m in range(n):
            for k in range(1, N_DEV):
                src, dst, peer = _exchange_refs(srcs, lands, m, k, x, y, c, scatter)
                pltpu.make_async_remote_copy(
                    src_ref=src, dst_ref=dst, send_sem=send_sems.at[7 * m + k - 1],
                    recv_sem=recv_sems.at[7 * m + k - 1], device_id=peer, device_id_type=MESH).start()
        token[...] = jnp.zeros_like(token)

    zones = [lax.empty(s_, a.dtype) for s_, a in zip(land_shapes, arrs)]
    outs = pl.pallas_call(
        body, name=name,
        out_shape=(pltpu.SemaphoreType.DMA((7 * n,)), pltpu.SemaphoreType.DMA((7 * n,)),
                   *[pltpu.HBM(a.shape, a.dtype) for a in arrs], *[pltpu.HBM(z.shape, z.dtype) for z in zones],
                   jax.ShapeDtypeStruct((8, 128), F32)),
        in_specs=[HBM_SPEC] * (2 * n),
        out_specs=(SEM_SPEC, SEM_SPEC, *[HBM_SPEC] * (2 * n), pl.BlockSpec(memory_space=pltpu.VMEM)),
        input_output_aliases={m: 2 + m for m in range(2 * n)},
        compiler_params=pltpu.CompilerParams(has_side_effects=pltpu.SideEffectType.DATAFLOW_SIDE_EFFECTING),
    )(*[pltpu.with_memory_space_constraint(a, pltpu.HBM) for a in arrs],
      *[pltpu.with_memory_space_constraint(z, pltpu.HBM) for z in zones])
    return outs[0], outs[1], list(outs[2:2 + n]), list(outs[2 + n:2 + 2 * n]), outs[-1]


def _exchange_wait(send_sems, recv_sems, arrs, zones, after, scatter, name):
    n = len(arrs)
    afters = list(after) if isinstance(after, (list, tuple)) else [after]

    def body(*refs):
        srcs, lands = refs[:n], refs[n:2 * n]
        send_sems, recv_sems = refs[2 * n], refs[2 * n + 1]
        x, y, c, _ = _my_place()
        for m in range(n):
            for k in range(1, N_DEV):
                src, dst, peer = _exchange_refs(srcs, lands, m, k, x, y, c, scatter)
                cp = pltpu.make_async_remote_copy(
                    src_ref=src, dst_ref=dst, send_sem=send_sems.at[7 * m + k - 1],
                    recv_sem=recv_sems.at[7 * m + k - 1], device_id=peer, device_id_type=MESH)
                cp.wait_send()
                cp.wait_recv()

    outs = pl.pallas_call(
        body, name=name,
        out_shape=tuple(pltpu.HBM(a.shape, a.dtype) for a in list(arrs) + list(zones)),
        in_specs=[HBM_SPEC] * (2 * n) + [SEM_SPEC, SEM_SPEC] + [pl.BlockSpec(memory_space=pl.ANY)] * len(afters),
        out_specs=tuple([HBM_SPEC] * (2 * n)),
        input_output_aliases={m: m for m in range(2 * n)},
        compiler_params=pltpu.CompilerParams(has_side_effects=pltpu.SideEffectType.DATAFLOW_SIDE_EFFECTING),
    )(*arrs, *zones, send_sems, recv_sems, *afters)
    return list(outs[n:])


def _sum_parts(own, parts, tr, dep=None):
    R, W = own.shape

    def body(own_ref, parts_ref, out_ref):
        acc = own_ref[...].astype(F32)
        for k in range(N_DEV - 1):
            acc = acc + parts_ref[k].astype(F32)
        out_ref[...] = acc

    in_specs = [pl.BlockSpec((tr, W), lambda i: (i, 0)), pl.BlockSpec((N_DEV - 1, tr, W), lambda i: (0, i, 0))]
    body, in_specs, args = _with_dep(body, dep, in_specs, [own, parts])
    return pl.pallas_call(
        body, name="sum_parts", grid=(R // tr,),
        in_specs=in_specs,
        out_specs=pl.BlockSpec((tr, W), lambda i: (i, 0)),
        out_shape=jax.ShapeDtypeStruct((R, W), F32),
        compiler_params=_params("parallel"),
    )(*args)


def _all_reduce_small(v, dep=None):
    Rn, Wd = v.shape

    def body(v_ref, out_ref, gat_ref, send_sems, recv_sems):
        x, y, c, _ = _my_place()
        me = 4 * x + 2 * y + c
        gat_ref[me] = v_ref[...]
        copies = []
        for k in range(1, N_DEV):
            fx, fy, fc = (k >> 2) & 1, (k >> 1) & 1, k & 1
            peer = (x ^ fx, y ^ fy, c ^ fc)
            cp = pltpu.make_async_remote_copy(
                src_ref=v_ref, dst_ref=gat_ref.at[me], send_sem=send_sems.at[k - 1], recv_sem=recv_sems.at[k - 1],
                device_id=peer, device_id_type=MESH)
            cp.start()
            copies.append(cp)
        for cp in copies:
            cp.wait_recv()
        for cp in copies:
            cp.wait_send()
        acc = gat_ref[0]
        for k in range(1, N_DEV):
            acc = acc + gat_ref[k]
        out_ref[...] = acc

    vm = pl.BlockSpec(memory_space=pltpu.VMEM)
    body, in_specs, args = _with_dep(body, dep, [vm], [v])
    return pl.pallas_call(
        body, name="all_reduce_small", in_specs=in_specs, out_specs=vm,
        out_shape=jax.ShapeDtypeStruct((Rn, Wd), F32),
        scratch_shapes=[pltpu.VMEM((N_DEV, Rn, Wd), F32), pltpu.SemaphoreType.DMA((7,)),
                        pltpu.SemaphoreType.DMA((7,))],
    )(*args)


def _t5_bucket(rel):
    half = N_BUCKETS // 2
    max_exact = half // 2
    ret = jnp.where(rel > 0, half, 0)
    n = jnp.abs(rel)
    nf = jnp.maximum(n, 1).astype(F32)
    large = max_exact + (jnp.log(nf / max_exact) / math.log(MAX_DISTANCE / max_exact)
                         * (half - max_exact)).astype(jnp.int32)
    large = jnp.minimum(large, half - 1)
    return ret + jnp.where(n < max_exact, n, large)


def _band(R, d):
    W = BQ + 2 * R
    rel = jnp.arange(W)[None, :] - R - jnp.arange(BQ)[:, None]
    return _t5_bucket(rel * d), jnp.abs(rel) <= R


def _onehot(R, d):
    bkt, in_band = _band(R, d)
    return ((bkt.reshape(1, -1) == jnp.arange(128)[:, None]) & in_band.reshape(1, -1)).astype(BF16)


def _bias_expand(table_t, onehot):
    H = table_t.shape[0]
    K = onehot.shape[1]

    def body(t_ref, oh_ref, out_ref):
        oh = oh_ref[...]
        t = t_ref[...]
        hi = t.astype(BF16)
        r1 = t - hi.astype(F32)
        mid = r1.astype(BF16)
        low = (r1 - mid.astype(F32)).astype(BF16)
        marked = _dot(jnp.ones(t.shape, BF16), oh) > 0.5
        out_ref[...] = jnp.where(marked, _dot(hi, oh) + _dot(mid, oh) + _dot(low, oh), NEG)

    vm = pl.BlockSpec(memory_space=pltpu.VMEM)
    return pl.pallas_call(
        body, name="bias_expand", in_specs=[vm, vm], out_specs=vm,
        out_shape=jax.ShapeDtypeStruct((H, K), F32),
        compiler_params=pltpu.CompilerParams(vmem_limit_bytes=VMEM_LIMIT),
    )(table_t, onehot)


def _bias_matrix(table, R, d):
    table_t = jnp.pad(table.T, ((0, 0), (0, 128 - N_BUCKETS)))
    return _bias_expand(table_t, _onehot(R, d)).reshape(table.shape[1], BQ, BQ + 2 * R)


def _bias_variants(base, R):
    H, _, W = base.shape
    fill = jnp.full((H, BQ, R), NEG, F32)
    first = jnp.concatenate([base[:, :, R:], fill], axis=2)
    last = jnp.concatenate([fill, base[:, :, :W - R]], axis=2)
    v = jnp.stack([base, first, last], axis=1)
    v = v.reshape(H // 2, 2, 3, BQ, W).transpose(0, 2, 1, 3, 4).reshape(H // 2, 3, 2 * BQ, W)
    return v, v.transpose(0, 1, 3, 2)


def _bias_grad(dbt, R, d):
    P, _, W, _ = dbt.shape
    dbt = dbt[:, 0].at[:, R:].add(dbt[:, 1, :W - R]).at[:, :W - R].add(dbt[:, 2, R:])
    dbm = dbt.reshape(P, W, 2, BQ).transpose(0, 2, 3, 1).reshape(2 * P, BQ * W)
    return _bias_reduce(_onehot(R, d), dbm)[:, :N_BUCKETS].T


def _tile2(gain):
    return jnp.concatenate([gain, gain])


ROW_W_O, ROW_GATE, ROW_QKV, ROW_PROJ, B_ROWS = 768, 896, 1024, 1312, 1344
BLK_W_O, BLK_GATE = ROW_W_O // 128, ROW_GATE // 128


def _pack_layer(wts, i):
    a = jnp.stack([wts["ffn1_w_in"][i], wts["ffn2_w_in"][i]])
    D = a.shape[1]
    b = jnp.concatenate([
        wts["ffn1_w_out"][i], wts["ffn2_w_out"][i],
        jnp.zeros((ROW_W_O - 2 * wts["ffn1_w_out"].shape[1], D), a.dtype),
        wts["w_o"][i], wts["w_ple_gate"][i], wts["w_qkv"][i].reshape(-1, D), wts["w_ple_proj"][i].reshape(-1, D)])
    return a, b


def _unpack_layer(sums, like):
    w_in2, b1, b2, w_in1, w_out1 = sums
    n_out, n_sq = like["ffn1_w_out"].shape[1], like["w_o"].shape[1]
    out = {}
    if w_in2 is not None:
        out.update(ffn2_w_in=w_in2, ffn2_w_out=b1[:n_out], w_ple_gate=b1[n_out:n_out + n_sq],
                   w_ple_proj=b1[n_out + n_sq:].reshape(like["w_ple_proj"].shape[1:]))
    if b2 is not None:
        out.update(w_o=b2[:n_sq], w_qkv=b2[n_sq:].reshape(like["w_qkv"].shape[1:]))
    if w_in1 is not None:
        out.update(ffn1_w_in=w_in1, ffn1_w_out=w_out1)
    return out


def _col_sharded(gb, r0, r1, rows):
    return gb[:, r0:r1].reshape(N_DEV, rows, -1).transpose(1, 0, 2).reshape(rows, -1)


def _to_col_shards(g):
    rows = g.shape[0]
    return g.reshape(rows, N_DEV, -1).transpose(1, 0, 2).reshape(N_DEV, -1, 1024)


def _layer_weights(ga, gb, p_dim):
    return dict(ga=ga, gb=gb, w_qkv=_col_sharded(gb, ROW_QKV, ROW_PROJ, ga.shape[2]),
                w_proj=_col_sharded(gb, ROW_PROJ, B_ROWS, p_dim))


def _layer_fwd(x, p, w, sm, i, target, tm, biases, dep=None):
    ga, gb = w["ga"], w["gb"]
    saved = {}
    saved["x0"] = x
    x1, saved["h1"], saved["zg1"], saved["zu1"], saved["s1"] = _ffn_fwd(
        x, sm["norm_ffn1"][i][None], ga, gb, 0, tm, dep)
    saved["x1"] = x1
    qkv, saved["hm"] = _qkv_fwd(x1, sm["norm_mix"][i][None], w["w_qkv"], tm)
    saved["qkv"] = qkv
    gains2 = jnp.stack([_tile2(sm[k][i]) for k in ("q_norm_a", "k_norm_a", "q_norm_b", "k_norm_b")])
    saved["gains2"] = gains2
    qb, kb, vb, qkv_d = _attn_prep(qkv, gains2, tm)
    no_sink = jnp.full((8,), NEG, F32)
    branches = []
    outs = []
    for (R, d), bias, (qd, kd, vd) in zip(DILATED, biases[:3], qkv_d):
        sink = jnp.tile(no_sink, d)
        outs.append(_attn_fwd(qd, kd, vd, bias[0], sink, R, 1, d))
        branches.append((qd, kd, vd, bias, sink, R, d))
    bias_b = biases[3]
    sink_b = sm["sink_b"][i]
    ob, lb = _attn_fwd(qb, kb, vb, bias_b[0], sink_b, SWA_RADIUS, 2, 1)
    merged, o_cat = _attn_merge(outs, ob, tm)
    saved.update(branches=branches, b=(qb, kb, vb, bias_b, sink_b), merged=merged, ob=ob, lb=lb, o_cat=o_cat)
    x2 = _oproj_fwd(x1, o_cat, gb, BLK_W_O, tm)
    saved["x2"] = x2
    x3, saved["h2"], saved["zg2"], saved["zu2"], saved["s2"] = _ffn_fwd(
        x2, sm["norm_ffn2"][i][None], ga, gb, 1, tm)
    saved["x3"] = x3
    res = _ple_fwd(x3, sm["norm_ple"][i][None], gb, BLK_GATE, p, w["w_proj"], target, tm)
    y, saved["hp"], saved["gate"], saved["pp"], saved["pb"] = res[:5]
    loss = res[5] if target is not None else None
    return y, loss, saved


def _layer_bwd(dy, w, sm, i, sv, tm, dep=None, on_ready=None, on_small=None, on_last=None):
    ga, gb = w["ga"], w["gb"]
    gs = {}
    D = dy.shape[1]
    dgl, dpp = _ple_bwd(dy, sv["gate"], sv["pp"], tm, dep)
    d_gate = _matmul_tn(sv["hp"], dgl, D, 2 * tm)
    d_proj = _matmul_tn(sv["pb"], dpp, D, 2 * tm)
    dx3, gs["norm_ple"] = _dense_norm_bwd(dy, dgl, gb, BLK_GATE, sv["x3"], sm["norm_ple"][i][None], tm)
    dx2, dyb, dzg, dzu, gs["norm_ffn2"] = _ffn_bwd(dx3, sv["x2"], sm["norm_ffn2"][i][None], sv["zg2"], sv["zu2"],
                                                   ga, gb, 1, tm)
    dwin2, dwo2 = _ffn_dw(sv["h2"], dzg, dzu, sv["s2"], dyb, 2 * tm)
    half = dwo2.shape[1] // 2
    after_ffn2 = [dwin2, jnp.concatenate([dwo2.reshape(N_DEV, half, D), d_gate.reshape(N_DEV, -1, D),
                                          _to_col_shards(d_proj)], axis=1)]
    token = None if on_ready is None else on_ready(0, after_ffn2)
    dx2b, do_b, do_a = _oproj_bwd(dx2, gb, BLK_W_O, tm, token)
    d_wo = _matmul_tn(sv["o_cat"], dx2b, D, 2 * tm)
    dqa, dka, dva, dbias = [], [], [], []
    for (qd, kd, vd, bias, sink, R, d), (oa, la), do_d in zip(sv["branches"], sv["merged"], do_a):
        dq, dk, dv, dbm, _ = _attn_bwd(qd, kd, vd, bias[1], sink, oa, la, do_d, R, 1, d)
        dqa.append(dq)
        dka.append(dk)
        dva.append(dv)
        dbias.append(dbm)
    qb, kb, vb, bias_b, sink_b = sv["b"]
    dqb, dkb, dvb, dbm_b, dsink = _attn_bwd(qb, kb, vb, bias_b[1], sink_b, sv["ob"], sv["lb"], do_b,
                                            SWA_RADIUS, 2, 1)
    gs["rel_bias"] = dbias + [dbm_b]
    gs["sink_b"] = jnp.sum(dsink[:, 0].reshape(-1, 2, BQ), axis=2).reshape(-1)
    dqkv, dgains2 = _attn_post(sv["qkv"], sv["gains2"], dqa, dka, dva, dqb,
                               dkb, dvb, tm // 2)
    dgains = dgains2[:, :HEAD_DIM] + dgains2[:, HEAD_DIM:]
    for k, name in enumerate(("q_norm_a", "k_norm_a", "q_norm_b", "k_norm_b")):
        gs[name] = dgains[k]
    d_qkv = _matmul_tn(sv["hm"], dqkv, dqkv.shape[1] // 2, 2 * tm)
    after_mixer = [jnp.concatenate([d_wo.reshape(N_DEV, -1, D), _to_col_shards(d_qkv)], axis=1)]
    token = None if on_ready is None else on_ready(1, after_mixer)
    dx1, gs["norm_mix"] = _dense_norm_bwd(dx2, dqkv, w["w_qkv"], None, sv["x1"], sm["norm_mix"][i][None], tm)
    g1 = sm["norm_ffn1"][i][None]
    if on_last is None:
        dx0, dyb, dzg, dzu, gs["norm_ffn1"] = _ffn_bwd(dx1, sv["x0"], g1, sv["zg1"], sv["zu1"], ga, gb, 0, tm, token)
        dwin1, dwo1 = _ffn_dw(sv["h1"], dzg, dzu, sv["s1"], dyb, 2 * tm)
        return dx0, (after_ffn2, after_mixer, [dwin1, dwo1.reshape(N_DEV, half, D)]), gs
    dyb, dzg, dzu = _ffn_bwd_dz(dx1, sv["zg1"], sv["zu1"], gb, 0, tm, token)
    dwin1, dwo1 = _ffn_dw(sv["h1"], dzg, dzu, sv["s1"], dyb, 2 * tm, on_small(gs))
    last = [dwin1, dwo1.reshape(N_DEV, half, D)]
    dx0, gs["norm_ffn1"] = _ffn_bwd_dx(dx1, sv["x0"], g1, dzg, dzu, ga, 0, tm, on_last(last))
    return dx0, (after_ffn2, after_mixer, last), gs


def _bias_matrices(rel_bias):
    biases = [_bias_variants(_bias_matrix(rel_bias[:, :8], R, d), R) for R, d in DILATED]
    biases.append(_bias_variants(_bias_matrix(rel_bias[:, 8:], SWA_RADIUS, 1), SWA_RADIUS))
    return biases


def _stack_small(per_layer):
    small = {}
    for k, v in per_layer.items():
        if k == "rel_bias":
            per_branch = [sum(parts) for parts in zip(*v.values())]
            drel_a = sum(_bias_grad(t, R, d) for t, (R, d) in zip(per_branch[:3], DILATED))
            small[k] = jnp.concatenate([drel_a, _bias_grad(per_branch[3], SWA_RADIUS, 1)], axis=1)
        else:
            small[k] = jnp.stack([v[i].reshape(-1) for i in sorted(v)])
    return small


TM = 512
SUM_TILES = (512, 512, 416, 512, 352)
LAST_GROUP = ("ffn1_w_in", "ffn1_w_out")


def _pack_small(d, extra=None):
    parts = [d[k].reshape(-1) for k in SMALL]
    if extra is not None:
        parts.append(extra.reshape(-1))
    flat = jnp.concatenate(parts)
    return jnp.pad(flat, (0, SMALL_ROWS * 128 - flat.shape[0])).reshape(SMALL_ROWS, 128)


def _unpack_small(buf, like):
    flat = buf.reshape(-1)
    out, off = {}, 0
    for k in SMALL:
        n = like[k].size
        out[k] = flat[off:off + n].reshape(like[k].shape)
        off += n
    return out, flat[off]


def kernel(x, p, rel_bias, norm_ffn1, ffn1_w_in, ffn1_w_out, norm_mix, w_qkv, q_norm_a, k_norm_a, q_norm_b, k_norm_b, sink_b, w_o, norm_ffn2, ffn2_w_in, ffn2_w_out, norm_ple, w_ple_gate, w_ple_proj, loss_target, m_rel_bias, m_norm_ffn1, m_ffn1_w_in, m_ffn1_w_out, m_norm_mix, m_w_qkv, m_q_norm_a, m_k_norm_a, m_q_norm_b, m_k_norm_b, m_sink_b, m_w_o, m_norm_ffn2, m_ffn2_w_in, m_ffn2_w_out, m_norm_ple, m_w_ple_gate, m_w_ple_proj, v_rel_bias, v_norm_ffn1, v_ffn1_w_in, v_ffn1_w_out, v_norm_mix, v_w_qkv, v_q_norm_a, v_k_norm_a, v_q_norm_b, v_k_norm_b, v_sink_b, v_w_o, v_norm_ffn2, v_ffn2_w_in, v_ffn2_w_out, v_norm_ple, v_w_ple_gate, v_w_ple_proj):
    wts = dict(rel_bias=rel_bias, norm_ffn1=norm_ffn1, ffn1_w_in=ffn1_w_in, ffn1_w_out=ffn1_w_out,
               norm_mix=norm_mix, w_qkv=w_qkv, q_norm_a=q_norm_a, k_norm_a=k_norm_a, q_norm_b=q_norm_b,
               k_norm_b=k_norm_b, sink_b=sink_b, w_o=w_o, norm_ffn2=norm_ffn2, ffn2_w_in=ffn2_w_in,
               ffn2_w_out=ffn2_w_out, norm_ple=norm_ple, w_ple_gate=w_ple_gate, w_ple_proj=w_ple_proj)
    mom = dict(rel_bias=m_rel_bias, norm_ffn1=m_norm_ffn1, ffn1_w_in=m_ffn1_w_in, ffn1_w_out=m_ffn1_w_out,
               norm_mix=m_norm_mix, w_qkv=m_w_qkv, q_norm_a=m_q_norm_a, k_norm_a=m_k_norm_a, q_norm_b=m_q_norm_b,
               k_norm_b=m_k_norm_b, sink_b=m_sink_b, w_o=m_w_o, norm_ffn2=m_norm_ffn2, ffn2_w_in=m_ffn2_w_in,
               ffn2_w_out=m_ffn2_w_out, norm_ple=m_norm_ple, w_ple_gate=m_w_ple_gate, w_ple_proj=m_w_ple_proj)
    var = dict(rel_bias=v_rel_bias, norm_ffn1=v_norm_ffn1, ffn1_w_in=v_ffn1_w_in, ffn1_w_out=v_ffn1_w_out,
               norm_mix=v_norm_mix, w_qkv=v_w_qkv, q_norm_a=v_q_norm_a, k_norm_a=v_k_norm_a, q_norm_b=v_q_norm_b,
               k_norm_b=v_k_norm_b, sink_b=v_sink_b, w_o=v_w_o, norm_ffn2=v_norm_ffn2, ffn2_w_in=v_ffn2_w_in,
               ffn2_w_out=v_ffn2_w_out, norm_ple=v_norm_ple, w_ple_gate=v_w_ple_gate, w_ple_proj=v_w_ple_proj)
    sm = {k: wts[k] for k in SMALL}
    p_dim = p.shape[-1]
    me = 4 * lax.axis_index("x") + 2 * lax.axis_index("y") + lax.axis_index("c")
    packed = []
    for i in range(2):
        a, b = _pack_layer(wts, i)
        packed.append([a.reshape(-1, a.shape[-1]).astype(BF16), b.astype(BF16)])
    a_shape = (2, ffn1_w_in.shape[1], ffn1_w_in.shape[2])

    def weights_of(zones):
        return _layer_weights(zones[0].reshape((N_DEV,) + a_shape), zones[1], p_dim)

    w0 = weights_of([_all_gather(t) for t in packed[0]])
    zone_shapes = [(N_DEV,) + t.shape for t in packed[1]]
    ssem, rsem, thru, zones, token = _exchange_start(packed[1], zone_shapes, False, "gather_start")
    biases = _bias_matrices(rel_bias)
    x1, _, sv0 = _layer_fwd(x[0], p[0, 0], w0, sm, 0, None, TM, biases, dep=token)
    zones = _exchange_wait(ssem, rsem, thru, zones, x1, False, "gather_wait")
    w1 = weights_of([lax.dynamic_update_index_in_dim(z, t, me, 0) for z, t in zip(zones, packed[1])])
    dy, loss, sv1 = _layer_fwd(x1, p[1, 0], w1, sm, 1, loss_target[0], TM, biases)

    def slots_for(arrs):
        return [(N_DEV - 1,) + t.shape[1:] for t in arrs]

    held1, held = {}, {}

    def on_ready1(stage, group):
        held1[stage] = _exchange_start(group, slots_for(group), True, f"scatter1_start_{stage}")
        return held1[stage][4]

    dx1, groups1, gs1 = _layer_bwd(dy, w1, sm, 1, sv1, TM, on_ready=on_ready1)
    on_ready1(2, groups1[2])
    g1 = groups1[0] + groups1[1] + groups1[2]

    def on_ready(stage, group):
        if stage == 1:
            held["slots1"] = [t for st in (0, 1, 2)
                              for t in _exchange_wait(*held1[st][:4], group[0], True, f"scatter1_wait_{st}")]
        held[stage] = _exchange_start(group, slots_for(group), True, f"scatter_start_{stage}")
        return held[stage][4]

    def on_small(gs0):
        part = dict(gs0, norm_ffn1=jnp.zeros_like(gs1["norm_ffn1"]))
        gsmall = _stack_small({k: {0: part[k], 1: gs1[k]} for k in part})
        held["small"] = _all_reduce_small(_pack_small(gsmall, loss[0, :1]))
        return held["small"]

    def on_last(group):
        held["last"] = _exchange_start(group, slots_for(group), True, "scatter_start_2")
        return held["last"][4]

    dx, groups0, gs0 = _layer_bwd(dx1, w0, sm, 0, sv0, TM, dep=held1[2][4], on_ready=on_ready, on_small=on_small,
                                  on_last=on_last)
    last = groups0[2]
    slots0 = [_exchange_wait(*held[stage][:4], last[0], True, f"scatter_wait_{stage}") for stage in (0, 1)]

    def summed(arrs, slots, tiles, dep=None):
        return [_sum_parts(lax.dynamic_index_in_dim(t, me, 0, keepdims=False), s_, tr, dep)
                for t, s_, tr in zip(arrs, slots, tiles)]

    cover = held["last"][4]
    r1 = summed(g1, held["slots1"], SUM_TILES, cover)
    r0 = summed(groups0[0], slots0[0], SUM_TILES[:2], cover) + summed(groups0[1], slots0[1], SUM_TILES[2:3], cover)

    def update(names, layers):
        for k in names:
            grads[k] = jnp.stack([layers[0][k], layers[1][k]])
            delta[k], new_m[k], new_v[k] = _adamw(wts[k], grads[k], mom[k], var[k])

    grads, delta, new_m, new_v = {}, {}, {}, {}
    layer1 = _unpack_layer(r1, wts)
    update([k for k in BIG if k not in LAST_GROUP], [_unpack_layer(r0 + [None, None], wts), layer1])

    cover_done = [dx] + [delta[k] for k in BIG if k not in LAST_GROUP]
    slots_last = _exchange_wait(*held["last"][:4], cover_done, True, "scatter_wait_2")
    update(LAST_GROUP, [_unpack_layer([None, None, None] + summed(last, slots_last, SUM_TILES[3:]), wts), layer1])
    late = _all_reduce_small(gs0["norm_ffn1"].reshape(-1, 128), dep=slots_last[0])
    small_sum, loss_sum = _unpack_small(held["small"], sm)
    small_sum["norm_ffn1"] = small_sum["norm_ffn1"].at[0].add(late.reshape(-1))
    grads.update(small_sum)
    zeros = {k: jnp.zeros_like(wts[k]) for k in SMALL}
    ds, ms, vs = _adamw(_pack_small(wts), _pack_small(small_sum), _pack_small(mom), _pack_small(var))
    for packed, dst in ((ds, delta), (ms, new_m), (vs, new_v)):
        dst.update(_unpack_small(packed, zeros)[0])

    return (loss_sum, dx[None], *[grads[k] for k in WEIGHTS], *[delta[k] for k in WEIGHTS],
            *[new_m[k] for k in WEIGHTS], *[new_v[k] for k in WEIGHTS])
```

```python
import functools
import math

import jax
import jax.numpy as jnp
from jax import lax
from jax.experimental import pallas as pl
from jax.experimental.pallas import tpu as pltpu

F32 = jnp.float32
BF16 = jnp.bfloat16

N_DEV = 8
HEAD_DIM = 64
PAIR = 2 * HEAD_DIM
BQ = 128
N_BUCKETS = 32
MAX_DISTANCE = 1024
DILATED = ((64, 1), (64, 4), (64, 16))
SWA_RADIUS = 128
EPS = 1e-6
NEG = -1e30
ADAM_LR, ADAM_B1, ADAM_B2, ADAM_EPS, ADAM_WD, ADAM_STEP = 0.001, 0.9, 0.999, 1e-08, 0.01, 10
VMEM_LIMIT = 56 * 1024 * 1024
AXES = ("x", "y", "c")
MESH = pl.DeviceIdType.MESH

BIG = ("ffn1_w_in", "ffn1_w_out", "w_qkv", "w_o", "ffn2_w_in", "ffn2_w_out", "w_ple_gate", "w_ple_proj")
SMALL = ("rel_bias", "norm_ffn1", "norm_mix", "q_norm_a", "k_norm_a", "q_norm_b", "k_norm_b", "sink_b",
         "norm_ffn2", "norm_ple")
WEIGHTS = ("rel_bias", "norm_ffn1", "ffn1_w_in", "ffn1_w_out", "norm_mix", "w_qkv", "q_norm_a", "k_norm_a",
           "q_norm_b", "k_norm_b", "sink_b", "w_o", "norm_ffn2", "ffn2_w_in", "ffn2_w_out", "norm_ple",
           "w_ple_gate", "w_ple_proj")
SMALL_ROWS = 96


def _params(*sem):
    return pltpu.CompilerParams(dimension_semantics=sem, vmem_limit_bytes=VMEM_LIMIT)


def _dot(a, b):
    return jnp.dot(a, b, preferred_element_type=F32)


def _dot_nt(a, b):
    return lax.dot_general(a, b, (((1,), (1,)), ((), ())), preferred_element_type=F32)


def _dot_tn(a, b):
    return lax.dot_general(a, b, (((0,), (0,)), ((), ())), preferred_element_type=F32)


def _sigmoid(x):
    return 1.0 / (1.0 + jnp.exp(-x))


def _rstd(xv):
    return lax.rsqrt(jnp.mean(xv * xv, axis=-1, keepdims=True) + EPS)


def _norm_bwd(dh, xv, gv):
    r = _rstd(xv)
    xn = xv * r
    dg = jnp.sum(dh * xn, axis=0, keepdims=True)
    dxn = dh * gv
    dx = r * (dxn - xn * jnp.mean(dxn * xn, axis=-1, keepdims=True))
    return dx, dg


def _lo_mask(shape):
    return lax.broadcasted_iota(jnp.int32, shape, len(shape) - 1) < HEAD_DIM


def _half_sum(t, lo):
    s0 = jnp.sum(jnp.where(lo, t, 0.0), axis=1, keepdims=True)
    s1 = jnp.sum(jnp.where(lo, 0.0, t), axis=1, keepdims=True)
    return jnp.where(lo, s0, s1)


FFN_PARTS = 2


def _ffn_weight_specs(f, nj, D, C):
    return [pl.BlockSpec((None, None, D, C), lambda i, j: (j, f, 0, 0)),
            pl.BlockSpec((None, None, D, C), lambda i, j: (j + nj, f, 0, 0)),
            pl.BlockSpec((2, C // 2, D), lambda i, j: (j, f, 0))]


def _with_dep(body, dep, in_specs, args):
    if dep is None:
        return body, in_specs, args

    def body_after(dep_ref, *refs):
        body(*refs)

    return body_after, [pl.BlockSpec(memory_space=pl.ANY)] + in_specs, [dep] + args


def _ffn_fwd(x, g, ga, gb, f, tm, dep=None):
    T, D = x.shape
    nj, C = ga.shape[0] // 2, ga.shape[3]

    def body(x_ref, g_ref, wg_ref, wu_ref, wo_ref, xo_ref, h_ref, zg_ref, zu_ref, s_ref, h_scr, acc):
        j = pl.program_id(1)

        @pl.when(j == 0)
        def _():
            xv = x_ref[...]
            hb = (xv * _rstd(xv) * g_ref[...]).astype(BF16)
            h_scr[...] = hb
            h_ref[...] = hb
            acc[...] = jnp.zeros_like(acc)

        wo = wo_ref[...].reshape(C, D)
        for part in range(FFN_PARTS):
            sl = pl.ds(part * (tm // FFN_PARTS), tm // FFN_PARTS)
            hb = h_scr[sl, :]
            gt = _dot(hb, wg_ref[...])
            up = _dot(hb, wu_ref[...])
            s = (gt * _sigmoid(gt) * up).astype(BF16)
            zg_ref[sl, :] = gt.astype(BF16)
            zu_ref[sl, :] = up.astype(BF16)
            s_ref[sl, :] = s
            acc[sl, :] += _dot(s, wo)

        @pl.when(j == nj - 1)
        def _():
            xo_ref[...] = x_ref[...] + 0.5 * acc[...]

    tok = pl.BlockSpec((tm, D), lambda i, j: (i, 0))
    chunk = pl.BlockSpec((None, tm, C), lambda i, j: (j, i, 0))
    in_specs = [tok, pl.BlockSpec((1, D), lambda i, j: (0, 0))] + _ffn_weight_specs(f, nj, D, C)
    body, in_specs, args = _with_dep(body, dep, in_specs, [x, g, ga, ga, gb])
    return pl.pallas_call(
        body, name="ffn_fwd", grid=(T // tm, nj),
        in_specs=in_specs,
        out_specs=[tok, tok, chunk, chunk, chunk],
        out_shape=[jax.ShapeDtypeStruct((T, D), F32), jax.ShapeDtypeStruct((T, D), BF16),
                   jax.ShapeDtypeStruct((nj, T, C), BF16), jax.ShapeDtypeStruct((nj, T, C), BF16),
                   jax.ShapeDtypeStruct((nj, T, C), BF16)],
        scratch_shapes=[pltpu.VMEM((tm, D), BF16), pltpu.VMEM((tm, D), F32)],
        compiler_params=_params("parallel", "arbitrary"),
    )(*args)


def _ffn_bwd(dxo, x, g, zg, zu, ga, gb, f, tm, dep=None):
    T, D = x.shape
    nj, C = ga.shape[0] // 2, ga.shape[3]

    def body(dxo_ref, x_ref, g_ref, zg_ref, zu_ref, wg_ref, wu_ref, wo_ref,
             dx_ref, dy_ref, dzg_ref, dzu_ref, dgn_ref, dy_scr, acc):
        i, j = pl.program_id(0), pl.program_id(1)

        @pl.when(j == 0)
        def _():
            dyb = (0.5 * dxo_ref[...]).astype(BF16)
            dy_scr[...] = dyb
            dy_ref[...] = dyb
            acc[...] = jnp.zeros_like(acc)

        wo = wo_ref[...].reshape(C, D)
        for part in range(FFN_PARTS):
            sl = pl.ds(part * (tm // FFN_PARTS), tm // FFN_PARTS)
            ds = _dot_nt(dy_scr[sl, :], wo)
            gt = zg_ref[sl, :].astype(F32)
            up = zu_ref[sl, :].astype(F32)
            sg = _sigmoid(gt)
            dgt = (ds * up * (sg * (1.0 + gt * (1.0 - sg)))).astype(BF16)
            dup = (ds * (gt * sg)).astype(BF16)
            dzg_ref[sl, :] = dgt
            dzu_ref[sl, :] = dup
            acc[sl, :] += _dot_nt(dgt, wg_ref[...]) + _dot_nt(dup, wu_ref[...])

        @pl.when(j == nj - 1)
        def _():
            dx, dg = _norm_bwd(acc[...], x_ref[...], g_ref[...])
            dx_ref[...] = dxo_ref[...] + dx

            @pl.when(i == 0)
            def _():
                dgn_ref[...] = dg

            @pl.when(i > 0)
            def _():
                dgn_ref[...] += dg

    tok = pl.BlockSpec((tm, D), lambda i, j: (i, 0))
    chunk = pl.BlockSpec((None, tm, C), lambda i, j: (j, i, 0))
    row = pl.BlockSpec((1, D), lambda i, j: (0, 0))
    in_specs = [tok, tok, row, chunk, chunk] + _ffn_weight_specs(f, nj, D, C)
    body, in_specs, args = _with_dep(body, dep, in_specs, [dxo, x, g, zg, zu, ga, ga, gb])
    return pl.pallas_call(
        body, name="ffn_bwd", grid=(T // tm, nj),
        in_specs=in_specs,
        out_specs=[tok, tok, chunk, chunk, row],
        out_shape=[jax.ShapeDtypeStruct((T, D), F32), jax.ShapeDtypeStruct((T, D), BF16),
                   jax.ShapeDtypeStruct((nj, T, C), BF16), jax.ShapeDtypeStruct((nj, T, C), BF16),
                   jax.ShapeDtypeStruct((1, D), F32)],
        scratch_shapes=[pltpu.VMEM((tm, D), BF16), pltpu.VMEM((tm, D), F32)],
        compiler_params=_params("arbitrary", "arbitrary"),
    )(*args)


def _ffn_bwd_dz(dxo, zg, zu, gb, f, tm, dep=None):
    T, D = dxo.shape
    nj, C = zg.shape[0], zg.shape[2]

    def body(dxo_ref, zg_ref, zu_ref, wo_ref, dy_ref, dzg_ref, dzu_ref, dy_scr):
        @pl.when(pl.program_id(1) == 0)
        def _():
            dyb = (0.5 * dxo_ref[...]).astype(BF16)
            dy_scr[...] = dyb
            dy_ref[...] = dyb

        wo = wo_ref[...].reshape(C, D)
        for part in range(FFN_PARTS):
            sl = pl.ds(part * (tm // FFN_PARTS), tm // FFN_PARTS)
            ds = _dot_nt(dy_scr[sl, :], wo)
            gt = zg_ref[sl, :].astype(F32)
            up = zu_ref[sl, :].astype(F32)
            sg = _sigmoid(gt)
            dzg_ref[sl, :] = (ds * up * (sg * (1.0 + gt * (1.0 - sg)))).astype(BF16)
            dzu_ref[sl, :] = (ds * (gt * sg)).astype(BF16)

    tok = pl.BlockSpec((tm, D), lambda i, j: (i, 0))
    chunk = pl.BlockSpec((None, tm, C), lambda i, j: (j, i, 0))
    in_specs = [tok, chunk, chunk, _ffn_weight_specs(f, nj, D, C)[2]]
    body, in_specs, args = _with_dep(body, dep, in_specs, [dxo, zg, zu, gb])
    return pl.pallas_call(
        body, name="ffn_bwd_dz", grid=(T // tm, nj),
        in_specs=in_specs, out_specs=[tok, chunk, chunk],
        out_shape=[jax.ShapeDtypeStruct((T, D), BF16), jax.ShapeDtypeStruct((nj, T, C), BF16),
                   jax.ShapeDtypeStruct((nj, T, C), BF16)],
        scratch_shapes=[pltpu.VMEM((tm, D), BF16)],
        compiler_params=_params("parallel", "arbitrary"),
    )(*args)


def _ffn_bwd_dx(dxo, x, g, dzg, dzu, ga, f, tm, dep=None):
    T, D = x.shape
    nj, C = ga.shape[0] // 2, ga.shape[3]

    def body(dxo_ref, x_ref, g_ref, dzg_ref, dzu_ref, wg_ref, wu_ref, dx_ref, dgn_ref, acc):
        i, j = pl.program_id(0), pl.program_id(1)

        @pl.when(j == 0)
        def _():
            acc[...] = jnp.zeros_like(acc)

        acc[...] += _dot_nt(dzg_ref[...], wg_ref[...]) + _dot_nt(dzu_ref[...], wu_ref[...])

        @pl.when(j == nj - 1)
        def _():
            dx, dg = _norm_bwd(acc[...], x_ref[...], g_ref[...])
            dx_ref[...] = dxo_ref[...] + dx

            @pl.when(i == 0)
            def _():
                dgn_ref[...] = dg

            @pl.when(i > 0)
            def _():
                dgn_ref[...] += dg

    tok = pl.BlockSpec((tm, D), lambda i, j: (i, 0))
    chunk = pl.BlockSpec((None, tm, C), lambda i, j: (j, i, 0))
    row = pl.BlockSpec((1, D), lambda i, j: (0, 0))
    in_specs = [tok, tok, row, chunk, chunk] + _ffn_weight_specs(f, nj, D, C)[:2]
    body, in_specs, args = _with_dep(body, dep, in_specs, [dxo, x, g, dzg, dzu, ga, ga])
    return pl.pallas_call(
        body, name="ffn_bwd_dx", grid=(T // tm, nj),
        in_specs=in_specs, out_specs=[tok, row],
        out_shape=[jax.ShapeDtypeStruct((T, D), F32), jax.ShapeDtypeStruct((1, D), F32)],
        scratch_shapes=[pltpu.VMEM((tm, D), F32)],
        compiler_params=_params("arbitrary", "arbitrary"),
    )(*args)


def _ffn_dw(h, dzg, dzu, s, dy, tk, dep=None):
    T, D = h.shape
    nj, C = s.shape[0], s.shape[2]
    nk = T // tk

    def body(h_ref, dzg_ref, dzu_ref, s_ref, dy_ref, dwin_ref, dwo_ref, ag, au, ao):
        k = pl.program_id(1)

        @pl.when(k == 0)
        def _():
            ag[...] = jnp.zeros_like(ag)
            au[...] = jnp.zeros_like(au)
            ao[...] = jnp.zeros_like(ao)

        hb = h_ref[...]
        ag[...] += _dot_tn(hb, dzg_ref[...])
        au[...] += _dot_tn(hb, dzu_ref[...])
        ao[...] += _dot_tn(s_ref[...], dy_ref[...])

        @pl.when(k == nk - 1)
        def _():
            dwin_ref[0] = ag[...].astype(BF16)
            dwin_ref[1] = au[...].astype(BF16)
            dwo_ref[...] = ao[...].astype(BF16)

    tok = pl.BlockSpec((tk, D), lambda j, k: (k, 0))
    chunk = pl.BlockSpec((None, tk, C), lambda j, k: (j, k, 0))
    body, in_specs, args = _with_dep(body, dep, [tok, chunk, chunk, chunk, tok], [h, dzg, dzu, s, dy])
    dwin, dwo = pl.pallas_call(
        body, name="ffn_dw", grid=(nj, nk),
        in_specs=in_specs,
        out_specs=[pl.BlockSpec((2, None, D, C), lambda j, k: (0, j, 0, 0)),
                   pl.BlockSpec((None, C, D), lambda j, k: (j, 0, 0))],
        out_shape=[jax.ShapeDtypeStruct((2, nj, D, C), BF16), jax.ShapeDtypeStruct((nj, C, D), BF16)],
        scratch_shapes=[pltpu.VMEM((D, C), F32), pltpu.VMEM((D, C), F32), pltpu.VMEM((C, D), F32)],
        compiler_params=_params("parallel", "arbitrary"),
    )(*args)
    return dwin.reshape(2 * nj, D, C), dwo


def _matmul_tn(a, b, tn, tk):
    T, Ka = a.shape
    N = b.shape[1]
    nk = T // tk

    def body(a_ref, b_ref, o_ref, acc):
        k = pl.program_id(1)

        @pl.when(k == 0)
        def _():
            acc[...] = jnp.zeros_like(acc)

        acc[...] += _dot_tn(a_ref[...], b_ref[...])

        @pl.when(k == nk - 1)
        def _():
            o_ref[...] = acc[...].astype(BF16)

    return pl.pallas_call(
        body, name="matmul_tn", grid=(N // tn, nk),
        in_specs=[pl.BlockSpec((tk, Ka), lambda n, k: (k, 0)), pl.BlockSpec((tk, tn), lambda n, k: (k, n))],
        out_specs=pl.BlockSpec((Ka, tn), lambda n, k: (0, n)),
        out_shape=jax.ShapeDtypeStruct((Ka, N), BF16),
        scratch_shapes=[pltpu.VMEM((Ka, tn), F32)],
        compiler_params=_params("parallel", "arbitrary"),
    )(a, b)


def _qkv_fwd(x, g, w, tm):
    T, D = x.shape
    N = w.shape[1]

    def body(x_ref, g_ref, w_ref, o_ref, h_ref):
        xv = x_ref[...]
        hb = (xv * _rstd(xv) * g_ref[...]).astype(BF16)
        h_ref[...] = hb
        o_ref[...] = _dot(hb, w_ref[...])

    return pl.pallas_call(
        body, name="qkv_fwd", grid=(T // tm,),
        in_specs=[pl.BlockSpec((tm, D), lambda i: (i, 0)), pl.BlockSpec((1, D), lambda i: (0, 0)),
                  pl.BlockSpec((D, N), lambda i: (0, 0))],
        out_specs=[pl.BlockSpec((tm, N), lambda i: (i, 0)), pl.BlockSpec((tm, D), lambda i: (i, 0))],
        out_shape=[jax.ShapeDtypeStruct((T, N), F32), jax.ShapeDtypeStruct((T, D), BF16)],
        compiler_params=_params("parallel"),
    )(x, g, w)


DILS = tuple(d for _, d in DILATED)


def _spread_specs(tm, T, dtype):
    specs = [pl.BlockSpec((4, d, tm // d, PAIR), lambda i: (0, 0, i, 0)) for d in DILS]
    shapes = [jax.ShapeDtypeStruct((4, d, T // d, PAIR), dtype) for d in DILS]
    return specs, shapes


def _spread(tile, y, outs, c, dtype):
    tm = y.shape[0]
    tile[...] = y
    for out, d in zip(outs, DILS):
        for r in range(d):
            out[c, r] = tile[pl.ds(r, tm // d, stride=d), :].astype(dtype)


def _collect(tile, ins, c):
    tm = tile.shape[0]
    first = True
    for ref, d in zip(ins, DILS):
        for r in range(d):
            rows = pl.ds(r, tm // d, stride=d) if d > 1 else pl.ds(0, tm)
            part = ref[c, r].astype(F32)
            tile[rows, :] = part if first else tile[rows, :] + part
        first = False
    return tile[...]


def _attn_prep(qkv, gains2, tm):
    T = qkv.shape[0]
    scale = HEAD_DIM ** -0.5
    n = len(DILS)

    def body(qkv_ref, g_ref, qb_ref, kb_ref, vb_ref, *rest):
        outs, tile = rest[:-1], rest[-1]
        lo = _lo_mask((tm, PAIR))

        def spread(kind, c, y):
            _spread(tile, y, outs[kind * n:(kind + 1) * n], c, BF16)

        def normed(c, gi, mult):
            xv = qkv_ref[:, c * PAIR:(c + 1) * PAIR]
            r = lax.rsqrt(_half_sum(xv * xv, lo) * (1.0 / HEAD_DIM) + EPS)
            y = xv * r * g_ref[gi:gi + 1, :]
            return y * mult if mult != 1.0 else y

        def both_halves(v):
            sw = pltpu.roll(v, HEAD_DIM, 1)
            return jnp.where(lo, v, sw), jnp.where(lo, sw, v)

        for c in range(4):
            spread(0, c, normed(c, 0, scale))
            spread(1, c, normed(4 + c, 1, 1.0))
            spread(2, c, qkv_ref[:, (8 + c) * PAIR:(9 + c) * PAIR])
            qb_ref[c] = normed(12 + c, 2, scale).astype(BF16)
        k0, k1 = both_halves(normed(16, 3, 1.0))
        kb_ref[0] = k0.astype(BF16)
        kb_ref[1] = k1.astype(BF16)
        v0, v1 = both_halves(qkv_ref[:, 17 * PAIR:18 * PAIR])
        vb_ref[0] = v0.astype(BF16)
        vb_ref[1] = v1.astype(BF16)

    four = pl.BlockSpec((4, tm, PAIR), lambda i: (0, i, 0))
    two = pl.BlockSpec((2, tm, PAIR), lambda i: (0, i, 0))
    s4 = jax.ShapeDtypeStruct((4, T, PAIR), BF16)
    s2 = jax.ShapeDtypeStruct((2, T, PAIR), BF16)
    specs, shapes = _spread_specs(tm, T, BF16)
    res = pl.pallas_call(
        body, name="attn_prep", grid=(T // tm,),
        in_specs=[pl.BlockSpec((tm, qkv.shape[1]), lambda i: (i, 0)), pl.BlockSpec((4, PAIR), lambda i: (0, 0))],
        out_specs=[four, two, two] + specs * 3,
        out_shape=[s4, s2, s2] + shapes * 3,
        scratch_shapes=[pltpu.VMEM((tm, PAIR), F32)],
        compiler_params=_params("parallel"),
    )(qkv, gains2)
    qb, kb, vb = res[:3]
    per_d = [tuple(res[3 + kind * n + di].reshape(4 * d, T // d, PAIR) for kind in range(3))
             for di, d in enumerate(DILS)]
    return qb, kb, vb, per_d


def _loop_blocks(nb, body, init, per_iter):
    u = math.gcd(nb, per_iter)

    def outer(i, carry):
        for k in range(u):
            carry = body(i * u + k, carry)
        return carry

    return lax.fori_loop(0, nb // u, outer, init)


def _key_window(b, nb, L, R, W):
    start = pl.multiple_of(jnp.clip(b * BQ - R, 0, L - W), HEAD_DIM)
    return start, jnp.where(b == 0, 1, jnp.where(b == nb - 1, 2, 0))


def _stack_heads(v, lo):
    z = jnp.zeros_like(v)
    return jnp.concatenate([jnp.where(lo, v, z), jnp.where(lo, z, v)], axis=0)


def _unstack_heads(v2, lo):
    return jnp.where(lo, v2[:BQ], v2[BQ:])


def _row_vector(v, lo):
    r = lax.broadcasted_iota(jnp.int32, (BQ, PAIR), 0)
    ln = lax.broadcasted_iota(jnp.int32, (BQ, PAIR), 1)
    diag = (ln % HEAD_DIM) == (r % HEAD_DIM)
    top = jnp.sum(jnp.where(diag & (r < HEAD_DIM), v, 0.0), axis=0, keepdims=True)
    bot = jnp.sum(jnp.where(diag & (r >= HEAD_DIM), v, 0.0), axis=0, keepdims=True)
    top8, bot8 = jnp.broadcast_to(top, (8, PAIR)), jnp.broadcast_to(bot, (8, PAIR))
    lo8 = _lo_mask((8, PAIR))
    head0 = jnp.where(lo8, top8, pltpu.roll(bot8, HEAD_DIM, 1))
    head1 = jnp.where(lo8, pltpu.roll(top8, HEAD_DIM, 1), bot8)
    return jnp.concatenate([head0, head1], axis=1)[:1]


def _units_per_step(nb, pairs_per_kv):
    return max(1, 16 // nb) if pairs_per_kv == 1 else 1


def _attn_fwd(q, kp, vp, bias4, sink, R, pairs_per_kv, pairs_per_bias):
    N, L, _ = q.shape
    W = BQ + 2 * R
    nb = L // BQ
    assert L >= W and nb >= 2
    G = _units_per_step(nb, pairs_per_kv)

    def body(sink_ref, q_ref, k_ref, v_ref, bias_ref, o_ref, lse_ref):
        n = pl.program_id(0)
        lo_q = _lo_mask((BQ, PAIR))
        first = lax.broadcasted_iota(jnp.int32, (2 * BQ, 1), 0) < BQ

        def blk(f, carry):
            g, b = f // nb, f % nb
            u = n * G + g
            sk = jnp.where(first, sink_ref[2 * u], sink_ref[2 * u + 1])
            q0 = pl.multiple_of(b * BQ, BQ)
            q2 = _stack_heads(q_ref[g, pl.ds(q0, BQ), :], lo_q)
            k0, variant = _key_window(b, nb, L, R, W)
            kw = k_ref[g, pl.ds(k0, W), :]
            vw = v_ref[g, pl.ds(k0, W), :]
            s = _dot_nt(q2, kw) + bias_ref[variant]
            m = jnp.maximum(jnp.max(s, axis=1, keepdims=True), sk)
            p = jnp.exp(s - m)
            l = jnp.sum(p, axis=1, keepdims=True) + jnp.exp(sk - m)
            o2 = _dot(p.astype(BF16), vw) / l
            o_ref[g, pl.ds(q0, BQ), :] = _unstack_heads(o2, lo_q)
            lse_ref[g, pl.ds(q0, BQ), :] = _unstack_heads(jnp.broadcast_to(m + jnp.log(l), (2 * BQ, PAIR)), lo_q)
            return carry

        _loop_blocks(G * nb, blk, 0, 4)

    qspec = pl.BlockSpec((G, L, PAIR), lambda n: (n, 0, 0))
    kspec = pl.BlockSpec((G, L, PAIR), lambda n: (n // pairs_per_kv, 0, 0))
    return pl.pallas_call(
        body, name="attn_fwd", grid=(N // G,),
        in_specs=[pl.BlockSpec(memory_space=pltpu.SMEM), qspec, kspec, kspec,
                  pl.BlockSpec((None, 3, 2 * BQ, W), lambda n: (n * G // pairs_per_bias, 0, 0, 0))],
        out_specs=[qspec, qspec],
        out_shape=[jax.ShapeDtypeStruct((N, L, PAIR), F32), jax.ShapeDtypeStruct((N, L, PAIR), F32)],
        compiler_params=_params("parallel"),
    )(sink, q, kp, vp, bias4)


def _attn_bwd(q, kp, vp, bias4t, sink, o, lse, do, R, pairs_per_kv, pairs_per_bias):
    N, L, _ = q.shape
    Nk = kp.shape[0]
    Pb = bias4t.shape[0]
    W = BQ + 2 * R
    nb = L // BQ
    assert L >= W and nb >= 2
    G = _units_per_step(nb, pairs_per_kv)

    def body(sink_ref, q_ref, k_ref, v_ref, bias_ref, o_ref, lse_ref, do_ref,
             dq_ref, dk_ref, dv_ref, dbias_ref, dsink_ref, dk_acc, dv_acc):
        n = pl.program_id(0)
        lo_q = _lo_mask((BQ, PAIR))
        first = lax.broadcasted_iota(jnp.int32, (1, 2 * BQ), 1) < BQ
        dsink_ref[...] = jnp.zeros_like(dsink_ref)

        @pl.when(n % pairs_per_kv == 0)
        def _():
            dk_acc[...] = jnp.zeros_like(dk_acc)
            dv_acc[...] = jnp.zeros_like(dv_acc)

        @pl.when((n * G) % pairs_per_bias == 0)
        def _():
            dbias_ref[...] = jnp.zeros_like(dbias_ref)

        def blk(f, carry):
            g, b = f // nb, f % nb
            u = n * G + g
            sk = jnp.where(first, sink_ref[2 * u], sink_ref[2 * u + 1])
            q0 = pl.multiple_of(b * BQ, BQ)
            q2 = _stack_heads(q_ref[g, pl.ds(q0, BQ), :], lo_q)
            k0, variant = _key_window(b, nb, L, R, W)
            kw = k_ref[g, pl.ds(k0, W), :]
            vw = v_ref[g, pl.ds(k0, W), :]
            dov = do_ref[g, pl.ds(q0, BQ), :]
            lse = _row_vector(lse_ref[g, pl.ds(q0, BQ), :], lo_q)
            delta = _row_vector(_half_sum(dov.astype(F32) * o_ref[g, pl.ds(q0, BQ), :], lo_q), lo_q)
            do2 = _stack_heads(dov.astype(BF16), lo_q)
            st = _dot_nt(kw, q2) + bias_ref[variant]
            pt = jnp.exp(st - lse)
            dst = pt * (_dot_nt(vw, do2) - delta)
            dstb = dst.astype(BF16)
            dbias_ref[variant] += dst
            dk_acc[g, pl.ds(k0, W), :] += _dot(dstb, q2)
            dv_acc[g, pl.ds(k0, W), :] += _dot(pt.astype(BF16), do2)
            dq_ref[g, pl.ds(q0, BQ), :] = _unstack_heads(_dot_tn(dstb, kw), lo_q).astype(BF16)
            dsink_ref[g, pl.ds(0, 1), :] -= jnp.exp(sk - lse) * delta
            return carry

        _loop_blocks(G * nb, blk, 0, 4)
        dk_ref[...] = dk_acc[...].astype(BF16)
        dv_ref[...] = dv_acc[...].astype(BF16)

    qspec = pl.BlockSpec((G, L, PAIR), lambda n: (n, 0, 0))
    kspec = pl.BlockSpec((G, L, PAIR), lambda n: (n // pairs_per_kv, 0, 0))
    return pl.pallas_call(
        body, name="attn_bwd", grid=(N // G,),
        in_specs=[pl.BlockSpec(memory_space=pltpu.SMEM), qspec, kspec, kspec,
                  pl.BlockSpec((None, 3, W, 2 * BQ), lambda n: (n * G // pairs_per_bias, 0, 0, 0)),
                  qspec, qspec, qspec],
        out_specs=[qspec, kspec, kspec,
                   pl.BlockSpec((None, 3, W, 2 * BQ), lambda n: (n * G // pairs_per_bias, 0, 0, 0)),
                   pl.BlockSpec((G, 8, 2 * BQ), lambda n: (n, 0, 0))],
        out_shape=[jax.ShapeDtypeStruct((N, L, PAIR), BF16),
                   jax.ShapeDtypeStruct((Nk, L, PAIR), BF16),
                   jax.ShapeDtypeStruct((Nk, L, PAIR), BF16),
                   jax.ShapeDtypeStruct((Pb, 3, W, 2 * BQ), F32),
                   jax.ShapeDtypeStruct((N, 8, 2 * BQ), F32)],
        scratch_shapes=[pltpu.VMEM((G, L, PAIR), F32), pltpu.VMEM((G, L, PAIR), F32)],
        compiler_params=_params("arbitrary"),
    )(sink, q, kp, vp, bias4t, o, lse, do)


def _attn_merge(branch_outs, ob, tm):
    T = ob.shape[1]
    n = len(DILS)

    def body(*refs):
        o_in, l_in, ob_ref = refs[:n], refs[n:2 * n], refs[2 * n]
        o_out, l_out, cat_ref = refs[2 * n + 1:3 * n + 1], refs[3 * n + 1:4 * n + 1], refs[4 * n + 1]
        tiles = refs[4 * n + 2:]
        for c in range(4):
            o_nat, l_nat = [], []
            for di, d in enumerate(DILS):
                for kind, (src, dst) in enumerate(((o_in[di], o_nat), (l_in[di], l_nat))):
                    tile = tiles[2 * di + kind]
                    if d == 1:
                        dst.append(src[c, 0])
                    else:
                        for r in range(d):
                            tile[pl.ds(r, tm // d, stride=d), :] = src[c, r]
                        dst.append(tile[...])
            m = functools.reduce(jnp.maximum, l_nat)
            ws = [jnp.exp(l - m) for l in l_nat]
            z = sum(ws)
            o = sum(w * t for w, t in zip(ws, o_nat)) / z
            cat_ref[:, c * PAIR:(c + 1) * PAIR] = o.astype(BF16)
            cat_ref[:, (4 + c) * PAIR:(5 + c) * PAIR] = ob_ref[c].astype(BF16)
            _spread(tiles[0], o, o_out, c, F32)
            _spread(tiles[1], m + jnp.log(z), l_out, c, F32)

    specs, shapes = _spread_specs(tm, T, F32)
    four = pl.BlockSpec((4, tm, PAIR), lambda i: (0, i, 0))
    o_views = [o.reshape(4, d, T // d, PAIR) for (o, _), d in zip(branch_outs, DILS)]
    l_views = [l.reshape(4, d, T // d, PAIR) for (_, l), d in zip(branch_outs, DILS)]
    res = pl.pallas_call(
        body, name="attn_merge", grid=(T // tm,),
        in_specs=specs + specs + [four],
        out_specs=specs + specs + [pl.BlockSpec((tm, 8 * PAIR), lambda i: (i, 0))],
        out_shape=shapes + shapes + [jax.ShapeDtypeStruct((T, 8 * PAIR), BF16)],
        scratch_shapes=[pltpu.VMEM((tm, PAIR), F32)] * (2 * n),
        compiler_params=_params("parallel"),
    )(*o_views, *l_views, ob)
    merged = [(res[di].reshape(4 * d, T // d, PAIR), res[n + di].reshape(4 * d, T // d, PAIR))
              for di, d in enumerate(DILS)]
    return merged, res[2 * n]


def _weight_arg(w, blk):
    if blk is None:
        return pl.BlockSpec(w.shape, lambda i: (0, 0)), (lambda ref: ref[...])
    D = w.shape[2]
    return (pl.BlockSpec((N_DEV, 128, D), lambda i: (0, blk, 0)),
            lambda ref: ref[...].reshape(N_DEV * 128, D))


def _oproj_fwd(x, o_cat, w, blk, tm):
    T, D = x.shape
    wspec, wload = _weight_arg(w, blk)

    def body(x_ref, o_ref, w_ref, out_ref):
        out_ref[...] = x_ref[...] + _dot(o_ref[...], wload(w_ref))

    tok = pl.BlockSpec((tm, D), lambda i: (i, 0))
    return pl.pallas_call(
        body, name="oproj_fwd", grid=(T // tm,),
        in_specs=[tok, pl.BlockSpec((tm, o_cat.shape[1]), lambda i: (i, 0)), wspec],
        out_specs=tok, out_shape=jax.ShapeDtypeStruct((T, D), F32),
        compiler_params=_params("parallel"),
    )(x, o_cat, w)


def _oproj_bwd(dx, w, blk, tm, dep=None):
    T, D = dx.shape
    wspec, wload = _weight_arg(w, blk)

    def body(dx_ref, w_ref, dxb_ref, dob_ref, *rest):
        doa_refs, tile = rest[:-1], rest[-1]
        db = dx_ref[...].astype(BF16)
        dxb_ref[...] = db
        do = _dot_nt(db, wload(w_ref))
        for c in range(4):
            _spread(tile, do[:, c * PAIR:(c + 1) * PAIR], doa_refs, c, BF16)
            dob_ref[c] = do[:, (4 + c) * PAIR:(5 + c) * PAIR].astype(BF16)

    tok = pl.BlockSpec((tm, D), lambda i: (i, 0))
    specs, shapes = _spread_specs(tm, T, BF16)
    body, in_specs, args = _with_dep(body, dep, [tok, wspec], [dx, w])
    res = pl.pallas_call(
        body, name="oproj_bwd", grid=(T // tm,),
        in_specs=in_specs,
        out_specs=[tok, pl.BlockSpec((4, tm, PAIR), lambda i: (0, i, 0))] + specs,
        out_shape=[jax.ShapeDtypeStruct((T, D), BF16), jax.ShapeDtypeStruct((4, T, PAIR), BF16)] + shapes,
        scratch_shapes=[pltpu.VMEM((tm, PAIR), F32)],
        compiler_params=_params("parallel"),
    )(*args)
    return res[0], res[1], [t.reshape(4 * d, T // d, PAIR) for t, d in zip(res[2:], DILS)]


def _attn_post(qkv, gains2, dqa, dka, dva, dqb, dkb, dvb, tm):
    T, NQ = qkv.shape
    scale = HEAD_DIM ** -0.5

    n = len(DILS)

    def body(qkv_ref, g_ref, *rest):
        dq_refs, dk_refs, dv_refs = rest[:n], rest[n:2 * n], rest[2 * n:3 * n]
        qb_ref, kb_ref, vb_ref, out_ref, dg_ref, tile = rest[3 * n:]
        lo = _lo_mask((tm, PAIR))

        @pl.when(pl.program_id(0) == 0)
        def _():
            dg_ref[...] = jnp.zeros_like(dg_ref)

        def norm_bwd(c, gi, dy):
            xv = qkv_ref[:, c * PAIR:(c + 1) * PAIR]
            r = lax.rsqrt(_half_sum(xv * xv, lo) * (1.0 / HEAD_DIM) + EPS)
            xn = xv * r
            dg_ref[gi:gi + 1, :] += jnp.sum(dy * xn, axis=0, keepdims=True)
            dxn = dy * g_ref[gi:gi + 1, :]
            dx = r * (dxn - xn * (_half_sum(dxn * xn, lo) * (1.0 / HEAD_DIM)))
            out_ref[:, c * PAIR:(c + 1) * PAIR] = dx.astype(BF16)

        def fold(v):
            return v + pltpu.roll(v, HEAD_DIM, 1)

        for c in range(4):
            norm_bwd(c, 0, _collect(tile, dq_refs, c) * scale)
            norm_bwd(4 + c, 1, _collect(tile, dk_refs, c))
            out_ref[:, (8 + c) * PAIR:(9 + c) * PAIR] = _collect(tile, dv_refs, c).astype(BF16)
            norm_bwd(12 + c, 2, qb_ref[c].astype(F32) * scale)
        kb, vb = kb_ref[...].astype(F32), vb_ref[...].astype(F32)
        norm_bwd(16, 3, jnp.where(lo, fold(kb[0]), fold(kb[1])))
        out_ref[:, 17 * PAIR:18 * PAIR] = jnp.where(lo, fold(vb[0]), fold(vb[1])).astype(BF16)

    four = pl.BlockSpec((4, tm, PAIR), lambda i: (0, i, 0))
    two = pl.BlockSpec((2, tm, PAIR), lambda i: (0, i, 0))
    specs, _ = _spread_specs(tm, T, BF16)
    views = [t.reshape(4, d, T // d, PAIR) for group in (dqa, dka, dva) for t, d in zip(group, DILS)]
    return pl.pallas_call(
        body, name="attn_post", grid=(T // tm,),
        in_specs=[pl.BlockSpec((tm, NQ), lambda i: (i, 0)), pl.BlockSpec((4, PAIR), lambda i: (0, 0))]
        + specs * 3 + [four, two, two],
        out_specs=[pl.BlockSpec((tm, NQ), lambda i: (i, 0)), pl.BlockSpec((4, PAIR), lambda i: (0, 0))],
        out_shape=[jax.ShapeDtypeStruct((T, NQ), BF16), jax.ShapeDtypeStruct((4, PAIR), F32)],
        scratch_shapes=[pltpu.VMEM((tm, PAIR), F32)],
        compiler_params=_params("arbitrary"),
    )(qkv, gains2, *views, dqb, dkb, dvb)


def _dense_norm_bwd(dres, dz, w, blk, x, g, tm):
    T, D = x.shape
    N = dz.shape[1]
    wspec, wload = _weight_arg(w, blk)

    def body(dres_ref, dz_ref, w_ref, x_ref, g_ref, dx_ref, dgn_ref):
        i = pl.program_id(0)
        dx, dg = _norm_bwd(_dot_nt(dz_ref[...], wload(w_ref)), x_ref[...], g_ref[...])
        dx_ref[...] = dres_ref[...] + dx

        @pl.when(i == 0)
        def _():
            dgn_ref[...] = dg

        @pl.when(i > 0)
        def _():
            dgn_ref[...] += dg

    tok = pl.BlockSpec((tm, D), lambda i: (i, 0))
    row = pl.BlockSpec((1, D), lambda i: (0, 0))
    return pl.pallas_call(
        body, name="dense_norm_bwd", grid=(T // tm,),
        in_specs=[tok, pl.BlockSpec((tm, N), lambda i: (i, 0)), wspec, tok, row],
        out_specs=[tok, row],
        out_shape=[jax.ShapeDtypeStruct((T, D), F32), jax.ShapeDtypeStruct((1, D), F32)],
        compiler_params=_params("arbitrary"),
    )(dres, dz, w, x, g)


def _bias_reduce(onehot, dbm):
    Hb, K = dbm.shape

    def body(oh_ref, d_ref, out_ref):
        oh = oh_ref[...]
        d = d_ref[...]
        hi = d.astype(BF16)
        r1 = d - hi.astype(F32)
        mid = r1.astype(BF16)
        low = (r1 - mid.astype(F32)).astype(BF16)
        out_ref[...] = _dot_nt(hi, oh) + _dot_nt(mid, oh) + _dot_nt(low, oh)

    vm = pl.BlockSpec(memory_space=pltpu.VMEM)
    return pl.pallas_call(
        body, name="bias_reduce", in_specs=[vm, vm], out_specs=vm,
        out_shape=jax.ShapeDtypeStruct((Hb, 128), F32),
        compiler_params=pltpu.CompilerParams(vmem_limit_bytes=VMEM_LIMIT),
    )(onehot, dbm)


def _ple_fwd(x, g, wg, blk, p, wp, target, tm):
    T, D = x.shape
    P = p.shape[1]
    with_loss = target is not None
    wspec, wload = _weight_arg(wg, blk)

    def body(*refs):
        if with_loss:
            x_ref, g_ref, wg_ref, p_ref, wp_ref, t_ref, y_ref, hn_ref, gate_ref, pp_ref, pb_ref, loss_ref = refs
        else:
            x_ref, g_ref, wg_ref, p_ref, wp_ref, y_ref, hn_ref, gate_ref, pp_ref, pb_ref = refs
        i = pl.program_id(0)
        xv = x_ref[...]
        hb = (xv * _rstd(xv) * g_ref[...]).astype(BF16)
        hn_ref[...] = hb
        gate = _sigmoid(_dot(hb, wload(wg_ref)))
        pb = p_ref[...].astype(BF16)
        pb_ref[...] = pb
        pp = _dot(pb, wp_ref[...])
        gate_ref[...] = gate
        pp_ref[...] = pp
        y = xv + gate * pp
        if with_loss:
            err = y - t_ref[...]
            y_ref[...] = err * (1.0 / D)
            part = jnp.broadcast_to(0.5 * jnp.sum(jnp.sum(err * err, axis=1, keepdims=True) * (1.0 / D),
                                                  axis=0, keepdims=True), (1, 128))

            @pl.when(i == 0)
            def _():
                loss_ref[...] = part

            @pl.when(i > 0)
            def _():
                loss_ref[...] += part
        else:
            y_ref[...] = y

    tok = pl.BlockSpec((tm, D), lambda i: (i, 0))
    ptok = pl.BlockSpec((tm, P), lambda i: (i, 0))
    in_specs = [tok, pl.BlockSpec((1, D), lambda i: (0, 0)), wspec, ptok,
                pl.BlockSpec((P, D), lambda i: (0, 0))]
    out_specs = [tok, tok, tok, tok, ptok]
    out_shape = [jax.ShapeDtypeStruct((T, D), F32), jax.ShapeDtypeStruct((T, D), BF16),
                 jax.ShapeDtypeStruct((T, D), F32), jax.ShapeDtypeStruct((T, D), F32),
                 jax.ShapeDtypeStruct((T, P), BF16)]
    args = [x, g, wg, p, wp]
    if with_loss:
        in_specs.append(tok)
        out_specs.append(pl.BlockSpec((1, 128), lambda i: (0, 0)))
        out_shape.append(jax.ShapeDtypeStruct((1, 128), F32))
        args.append(target)
    return pl.pallas_call(
        body, name="ple_fwd_loss" if with_loss else "ple_fwd", grid=(T // tm,),
        in_specs=in_specs, out_specs=out_specs, out_shape=out_shape,
        compiler_params=_params("arbitrary" if with_loss else "parallel"),
    )(*args)


def _ple_bwd(dy, gate, pp, tm, dep=None):
    T, D = dy.shape

    def body(dy_ref, gate_ref, pp_ref, dgl_ref, dpp_ref):
        d = dy_ref[...]
        gt = gate_ref[...]
        dgl_ref[...] = (d * pp_ref[...] * gt * (1.0 - gt)).astype(BF16)
        dpp_ref[...] = (d * gt).astype(BF16)

    tok = pl.BlockSpec((tm, D), lambda i: (i, 0))
    body, in_specs, args = _with_dep(body, dep, [tok, tok, tok], [dy, gate, pp])
    return pl.pallas_call(
        body, name="ple_bwd", grid=(T // tm,), in_specs=in_specs, out_specs=[tok, tok],
        out_shape=[jax.ShapeDtypeStruct((T, D), BF16), jax.ShapeDtypeStruct((T, D), BF16)],
        compiler_params=_params("parallel"),
    )(*args)


def _adamw(w, g, m, v):
    shape = w.shape
    C = shape[-1]
    w2, g2, m2, v2 = (a.reshape(-1, C) for a in (w, g, m, v))
    Rn = w2.shape[0]
    tr = Rn
    for cand in (512, 352, 256):
        if Rn % cand == 0:
            tr = cand
            break
    c1 = 1.0 - ADAM_B1 ** ADAM_STEP
    c2 = 1.0 - ADAM_B2 ** ADAM_STEP

    def body(w_ref, g_ref, m_ref, v_ref, d_ref, nm_ref, nv_ref):
        gv = g_ref[...]
        mn = ADAM_B1 * m_ref[...] + (1.0 - ADAM_B1) * gv
        vn = ADAM_B2 * v_ref[...] + (1.0 - ADAM_B2) * (gv * gv)
        d_ref[...] = -ADAM_LR * ((mn / c1) / (jnp.sqrt(vn / c2) + ADAM_EPS) + ADAM_WD * w_ref[...])
        nm_ref[...] = mn
        nv_ref[...] = vn

    spec = pl.BlockSpec((tr, C), lambda i: (i, 0))
    sh = jax.ShapeDtypeStruct((Rn, C), F32)
    d, nm, nv = pl.pallas_call(
        body, name="adamw", grid=(Rn // tr,), in_specs=[spec] * 4, out_specs=[spec] * 3, out_shape=[sh] * 3,
        compiler_params=_params("parallel"),
    )(w2, g2, m2, v2)
    return d.reshape(shape), nm.reshape(shape), nv.reshape(shape)


def _my_place():
    x, y, c = lax.axis_index("x"), lax.axis_index("y"), lax.axis_index("c")
    chips = [(1 - x, y), (x, 1 - y), (1 - x, 1 - y)]
    return x, y, c, chips


def _all_gather(arrs):
    n = len(arrs)

    def body(*refs):
        x_refs, out_refs = refs[:n], refs[n:2 * n]
        send_sems, recv_sems, local_sems = refs[2 * n:]
        x, y, c, chips = _my_place()
        me, sibling = (x, y, c), (x, y, 1 - c)

        def copy(m, k, block, to, src=None):
            rows = out_refs[m].at[4 * block[0] + 2 * block[1] + block[2]]
            return pltpu.make_async_remote_copy(
                src_ref=rows if src is None else src, dst_ref=rows,
                send_sem=send_sems.at[7 * m + k], recv_sem=recv_sems.at[7 * m + k], device_id=to, device_id_type=MESH)

        mine = [pltpu.make_async_copy(x_refs[m], out_refs[m].at[4 * x + 2 * y + c], local_sems.at[m])
                for m in range(n)]
        for cp in mine:
            cp.start()
        first = []
        for m in range(n):
            first.append(copy(m, 0, me, sibling, src=x_refs[m]))
            first += [copy(m, 1 + j, me, (*chip, c), src=x_refs[m]) for j, chip in enumerate(chips)]
        for cp in first:
            cp.start()
        passed = []
        for m in range(n):
            for j, chip in enumerate(chips):
                copy(m, 1 + j, (*chip, c), me).wait_recv()
                cp = copy(m, 4 + j, (*chip, c), sibling)
                cp.start()
                passed.append(cp)
        for m in range(n):
            copy(m, 0, sibling, me).wait_recv()
            for j, chip in enumerate(chips):
                copy(m, 4 + j, (*chip, 1 - c), me).wait_recv()
        for cp in first + passed:
            cp.wait_send()
        for cp in mine:
            cp.wait()

    hbm = pl.BlockSpec(memory_space=pl.ANY)
    return pl.pallas_call(
        body, name="all_gather", in_specs=[hbm] * n, out_specs=[hbm] * n,
        out_shape=[jax.ShapeDtypeStruct((N_DEV,) + a.shape, a.dtype) for a in arrs],
        scratch_shapes=[pltpu.SemaphoreType.DMA((7 * n,)), pltpu.SemaphoreType.DMA((7 * n,)),
                        pltpu.SemaphoreType.DMA((n,))],
    )(*arrs)


def _peer(x, y, c, k):
    return (x ^ ((k >> 2) & 1), y ^ ((k >> 1) & 1), c ^ (k & 1))


HBM_SPEC = pl.BlockSpec(memory_space=pltpu.HBM)
SEM_SPEC = pl.BlockSpec(memory_space=pltpu.SEMAPHORE)


def _exchange_refs(srcs, lands, m, k, x, y, c, scatter):
    peer = _peer(x, y, c, k)
    if scatter:
        return srcs[m].at[4 * peer[0] + 2 * peer[1] + peer[2]], lands[m].at[k - 1], peer
    return srcs[m], lands[m].at[4 * x + 2 * y + c], peer


def _exchange_start(arrs, land_shapes, scatter, name):
    n = len(arrs)

    def body(*refs):
        srcs, lands = refs[:n], refs[n:2 * n]
        send_sems, recv_sems = refs[2 * n], refs[2 * n + 1]
        token = refs[-1]
        x, y, c, _ = _my_place()
        for m in range(n):
            for k in range(1, N_DEV):
                src, dst, peer = _exchange_refs(srcs, lands, m, k, x, y, c, scatter)
                pltpu.make_async_remote_copy(
                    src_ref=src, dst_ref=dst, send_sem=send_sems.at[7 * m + k - 1],
                    recv_sem=recv_sems.at[7 * m + k - 1], device_id=peer, device_id_type=MESH).start()
        token[...] = jnp.zeros_like(token)

    zones = [lax.empty(s_, a.dtype) for s_, a in zip(land_shapes, arrs)]
    outs = pl.pallas_call(
        body, name=name,
        out_shape=(pltpu.SemaphoreType.DMA((7 * n,)), pltpu.SemaphoreType.DMA((7 * n,)),
                   *[pltpu.HBM(a.shape, a.dtype) for a in arrs], *[pltpu.HBM(z.shape, z.dtype) for z in zones],
                   jax.ShapeDtypeStruct((8, 128), F32)),
        in_specs=[HBM_SPEC] * (2 * n),
        out_specs=(SEM_SPEC, SEM_SPEC, *[HBM_SPEC] * (2 * n), pl.BlockSpec(memory_space=pltpu.VMEM)),
        input_output_aliases={m: 2 + m for m in range(2 * n)},
        compiler_params=pltpu.CompilerParams(has_side_effects=pltpu.SideEffectType.DATAFLOW_SIDE_EFFECTING),
    )(*[pltpu.with_memory_space_constraint(a, pltpu.HBM) for a in arrs],
      *[pltpu.with_memory_space_constraint(z, pltpu.HBM) for z in zones])
    return outs[0], outs[1], list(outs[2:2 + n]), list(outs[2 + n:2 + 2 * n]), outs[-1]


def _exchange_wait(send_sems, recv_sems, arrs, zones, after, scatter, name):
    n = len(arrs)
    afters = list(after) if isinstance(after, (list, tuple)) else [after]

    def body(*refs):
        srcs, lands = refs[:n], refs[n:2 * n]
        send_sems, recv_sems = refs[2 * n], refs[2 * n + 1]
        x, y, c, _ = _my_place()
        for m in range(n):
            for k in range(1, N_DEV):
                src, dst, peer = _exchange_refs(srcs, lands, m, k, x, y, c, scatter)
                cp = pltpu.make_async_remote_copy(
                    src_ref=src, dst_ref=dst, send_sem=send_sems.at[7 * m + k - 1],
                    recv_sem=recv_sems.at[7 * m + k - 1], device_id=peer, device_id_type=MESH)
                cp.wait_send()
                cp.wait_recv()

    outs = pl.pallas_call(
        body, name=name,
        out_shape=tuple(pltpu.HBM(a.shape, a.dtype) for a in list(arrs) + list(zones)),
        in_specs=[HBM_SPEC] * (2 * n) + [SEM_SPEC, SEM_SPEC] + [pl.BlockSpec(memory_space=pl.ANY)] * len(afters),
        out_specs=tuple([HBM_SPEC] * (2 * n)),
        input_output_aliases={m: m for m in range(2 * n)},
        compiler_params=pltpu.CompilerParams(has_side_effects=pltpu.SideEffectType.DATAFLOW_SIDE_EFFECTING),
    )(*arrs, *zones, send_sems, recv_sems, *afters)
    return list(outs[n:])


def _sum_parts(own, parts, tr, dep=None):
    R, W = own.shape

    def body(own_ref, parts_ref, out_ref):
        acc = own_ref[...].astype(F32)
        for k in range(N_DEV - 1):
            acc = acc + parts_ref[k].astype(F32)
        out_ref[...] = acc

    in_specs = [pl.BlockSpec((tr, W), lambda i: (i, 0)), pl.BlockSpec((N_DEV - 1, tr, W), lambda i: (0, i, 0))]
    body, in_specs, args = _with_dep(body, dep, in_specs, [own, parts])
    return pl.pallas_call(
        body, name="sum_parts", grid=(R // tr,),
        in_specs=in_specs,
        out_specs=pl.BlockSpec((tr, W), lambda i: (i, 0)),
        out_shape=jax.ShapeDtypeStruct((R, W), F32),
        compiler_params=_params("parallel"),
    )(*args)


def _all_reduce_small(v, dep=None):
    Rn, Wd = v.shape

    def body(v_ref, out_ref, gat_ref, send_sems, recv_sems):
        x, y, c, _ = _my_place()
        me = 4 * x + 2 * y + c
        gat_ref[me] = v_ref[...]
        copies = []
        for k in range(1, N_DEV):
            fx, fy, fc = (k >> 2) & 1, (k >> 1) & 1, k & 1
            peer = (x ^ fx, y ^ fy, c ^ fc)
            cp = pltpu.make_async_remote_copy(
                src_ref=v_ref, dst_ref=gat_ref.at[me], send_sem=send_sems.at[k - 1], recv_sem=recv_sems.at[k - 1],
                device_id=peer, device_id_type=MESH)
            cp.start()
            copies.append(cp)
        for cp in copies:
            cp.wait_recv()
        for cp in copies:
            cp.wait_send()
        acc = gat_ref[0]
        for k in range(1, N_DEV):
            acc = acc + gat_ref[k]
        out_ref[...] = acc

    vm = pl.BlockSpec(memory_space=pltpu.VMEM)
    body, in_specs, args = _with_dep(body, dep, [vm], [v])
    return pl.pallas_call(
        body, name="all_reduce_small", in_specs=in_specs, out_specs=vm,
        out_shape=jax.ShapeDtypeStruct((Rn, Wd), F32),
        scratch_shapes=[pltpu.VMEM((N_DEV, Rn, Wd), F32), pltpu.SemaphoreType.DMA((7,)),
                        pltpu.SemaphoreType.DMA((7,))],
    )(*args)


def _t5_bucket(rel):
    half = N_BUCKETS // 2
    max_exact = half // 2
    ret = jnp.where(rel > 0, half, 0)
    n = jnp.abs(rel)
    nf = jnp.maximum(n, 1).astype(F32)
    large = max_exact + (jnp.log(nf / max_exact) / math.log(MAX_DISTANCE / max_exact)
                         * (half - max_exact)).astype(jnp.int32)
    large = jnp.minimum(large, half - 1)
    return ret + jnp.where(n < max_exact, n, large)


def _band(R, d):
    W = BQ + 2 * R
    rel = jnp.arange(W)[None, :] - R - jnp.arange(BQ)[:, None]
    return _t5_bucket(rel * d), jnp.abs(rel) <= R


def _onehot(R, d):
    bkt, in_band = _band(R, d)
    return ((bkt.reshape(1, -1) == jnp.arange(128)[:, None]) & in_band.reshape(1, -1)).astype(BF16)


def _bias_expand(table_t, onehot):
    H = table_t.shape[0]
    K = onehot.shape[1]

    def body(t_ref, oh_ref, out_ref):
        oh = oh_ref[...]
        t = t_ref[...]
        hi = t.astype(BF16)
        r1 = t - hi.astype(F32)
        mid = r1.astype(BF16)
        low = (r1 - mid.astype(F32)).astype(BF16)
        marked = _dot(jnp.ones(t.shape, BF16), oh) > 0.5
        out_ref[...] = jnp.where(marked, _dot(hi, oh) + _dot(mid, oh) + _dot(low, oh), NEG)

    vm = pl.BlockSpec(memory_space=pltpu.VMEM)
    return pl.pallas_call(
        body, name="bias_expand", in_specs=[vm, vm], out_specs=vm,
        out_shape=jax.ShapeDtypeStruct((H, K), F32),
        compiler_params=pltpu.CompilerParams(vmem_limit_bytes=VMEM_LIMIT),
    )(table_t, onehot)


def _bias_matrix(table, R, d):
    table_t = jnp.pad(table.T, ((0, 0), (0, 128 - N_BUCKETS)))
    return _bias_expand(table_t, _onehot(R, d)).reshape(table.shape[1], BQ, BQ + 2 * R)


def _bias_variants(base, R):
    H, _, W = base.shape
    fill = jnp.full((H, BQ, R), NEG, F32)
    first = jnp.concatenate([base[:, :, R:], fill], axis=2)
    last = jnp.concatenate([fill, base[:, :, :W - R]], axis=2)
    v = jnp.stack([base, first, last], axis=1)
    v = v.reshape(H // 2, 2, 3, BQ, W).transpose(0, 2, 1, 3, 4).reshape(H // 2, 3, 2 * BQ, W)
    return v, v.transpose(0, 1, 3, 2)


def _bias_grad(dbt, R, d):
    P, _, W, _ = dbt.shape
    dbt = dbt[:, 0].at[:, R:].add(dbt[:, 1, :W - R]).at[:, :W - R].add(dbt[:, 2, R:])
    dbm = dbt.reshape(P, W, 2, BQ).transpose(0, 2, 3, 1).reshape(2 * P, BQ * W)
    return _bias_reduce(_onehot(R, d), dbm)[:, :N_BUCKETS].T


def _tile2(gain):
    return jnp.concatenate([gain, gain])


ROW_W_O, ROW_GATE, ROW_QKV, ROW_PROJ, B_ROWS = 768, 896, 1024, 1312, 1344
BLK_W_O, BLK_GATE = ROW_W_O // 128, ROW_GATE // 128


def _pack_layer(wts, i):
    a = jnp.stack([wts["ffn1_w_in"][i], wts["ffn2_w_in"][i]])
    D = a.shape[1]
    b = jnp.concatenate([
        wts["ffn1_w_out"][i], wts["ffn2_w_out"][i],
        jnp.zeros((ROW_W_O - 2 * wts["ffn1_w_out"].shape[1], D), a.dtype),
        wts["w_o"][i], wts["w_ple_gate"][i], wts["w_qkv"][i].reshape(-1, D), wts["w_ple_proj"][i].reshape(-1, D)])
    return a, b


def _unpack_layer(sums, like):
    w_in2, b1, b2, w_in1, w_out1 = sums
    n_out, n_sq = like["ffn1_w_out"].shape[1], like["w_o"].shape[1]
    out = {}
    if w_in2 is not None:
        out.update(ffn2_w_in=w_in2, ffn2_w_out=b1[:n_out], w_ple_gate=b1[n_out:n_out + n_sq],
                   w_ple_proj=b1[n_out + n_sq:].reshape(like["w_ple_proj"].shape[1:]))
    if b2 is not None:
        out.update(w_o=b2[:n_sq], w_qkv=b2[n_sq:].reshape(like["w_qkv"].shape[1:]))
    if w_in1 is not None:
        out.update(ffn1_w_in=w_in1, ffn1_w_out=w_out1)
    return out


def _col_sharded(gb, r0, r1, rows):
    return gb[:, r0:r1].reshape(N_DEV, rows, -1).transpose(1, 0, 2).reshape(rows, -1)


def _to_col_shards(g):
    rows = g.shape[0]
    return g.reshape(rows, N_DEV, -1).transpose(1, 0, 2).reshape(N_DEV, -1, 1024)


def _layer_weights(ga, gb, p_dim):
    return dict(ga=ga, gb=gb, w_qkv=_col_sharded(gb, ROW_QKV, ROW_PROJ, ga.shape[2]),
                w_proj=_col_sharded(gb, ROW_PROJ, B_ROWS, p_dim))


def _layer_fwd(x, p, w, sm, i, target, tm, biases, dep=None):
    ga, gb = w["ga"], w["gb"]
    saved = {}
    saved["x0"] = x
    x1, saved["h1"], saved["zg1"], saved["zu1"], saved["s1"] = _ffn_fwd(
        x, sm["norm_ffn1"][i][None], ga, gb, 0, 2 * tm, dep)
    saved["x1"] = x1
    qkv, saved["hm"] = _qkv_fwd(x1, sm["norm_mix"][i][None], w["w_qkv"], tm)
    saved["qkv"] = qkv
    gains2 = jnp.stack([_tile2(sm[k][i]) for k in ("q_norm_a", "k_norm_a", "q_norm_b", "k_norm_b")])
    saved["gains2"] = gains2
    qb, kb, vb, qkv_d = _attn_prep(qkv, gains2, tm)
    no_sink = jnp.full((8,), NEG, F32)
    branches = []
    outs = []
    for (R, d), bias, (qd, kd, vd) in zip(DILATED, biases[:3], qkv_d):
        sink = jnp.tile(no_sink, d)
        outs.append(_attn_fwd(qd, kd, vd, bias[0], sink, R, 1, d))
        branches.append((qd, kd, vd, bias, sink, R, d))
    bias_b = biases[3]
    sink_b = sm["sink_b"][i]
    ob, lb = _attn_fwd(qb, kb, vb, bias_b[0], sink_b, SWA_RADIUS, 2, 1)
    merged, o_cat = _attn_merge(outs, ob, tm)
    saved.update(branches=branches, b=(qb, kb, vb, bias_b, sink_b), merged=merged, ob=ob, lb=lb, o_cat=o_cat)
    x2 = _oproj_fwd(x1, o_cat, gb, BLK_W_O, tm)
    saved["x2"] = x2
    x3, saved["h2"], saved["zg2"], saved["zu2"], saved["s2"] = _ffn_fwd(
        x2, sm["norm_ffn2"][i][None], ga, gb, 1, 2 * tm)
    saved["x3"] = x3
    res = _ple_fwd(x3, sm["norm_ple"][i][None], gb, BLK_GATE, p, w["w_proj"], target, tm)
    y, saved["hp"], saved["gate"], saved["pp"], saved["pb"] = res[:5]
    loss = res[5] if target is not None else None
    return y, loss, saved


def _layer_bwd(dy, w, sm, i, sv, tm, dep=None, on_ready=None, on_small=None, on_last=None):
    ga, gb = w["ga"], w["gb"]
    gs = {}
    D = dy.shape[1]
    dgl, dpp = _ple_bwd(dy, sv["gate"], sv["pp"], tm, dep)
    d_gate = _matmul_tn(sv["hp"], dgl, D, 2 * tm)
    d_proj = _matmul_tn(sv["pb"], dpp, D, 2 * tm)
    dx3, gs["norm_ple"] = _dense_norm_bwd(dy, dgl, gb, BLK_GATE, sv["x3"], sm["norm_ple"][i][None], tm)
    dx2, dyb, dzg, dzu, gs["norm_ffn2"] = _ffn_bwd(dx3, sv["x2"], sm["norm_ffn2"][i][None], sv["zg2"], sv["zu2"],
                                                   ga, gb, 1, tm)
    dwin2, dwo2 = _ffn_dw(sv["h2"], dzg, dzu, sv["s2"], dyb, 2 * tm)
    half = dwo2.shape[1] // 2
    after_ffn2 = [dwin2, jnp.concatenate([dwo2.reshape(N_DEV, half, D), d_gate.reshape(N_DEV, -1, D),
                                          _to_col_shards(d_proj)], axis=1)]
    token = None if on_ready is None else on_ready(0, after_ffn2)
    dx2b, do_b, do_a = _oproj_bwd(dx2, gb, BLK_W_O, tm, token)
    d_wo = _matmul_tn(sv["o_cat"], dx2b, D, 2 * tm)
    dqa, dka, dva, dbias = [], [], [], []
    for (qd, kd, vd, bias, sink, R, d), (oa, la), do_d in zip(sv["branches"], sv["merged"], do_a):
        dq, dk, dv, dbm, _ = _attn_bwd(qd, kd, vd, bias[1], sink, oa, la, do_d, R, 1, d)
        dqa.append(dq)
        dka.append(dk)
        dva.append(dv)
        dbias.append(dbm)
    qb, kb, vb, bias_b, sink_b = sv["b"]
    dqb, dkb, dvb, dbm_b, dsink = _attn_bwd(qb, kb, vb, bias_b[1], sink_b, sv["ob"], sv["lb"], do_b,
                                            SWA_RADIUS, 2, 1)
    gs["rel_bias"] = dbias + [dbm_b]
    gs["sink_b"] = jnp.sum(dsink[:, 0].reshape(-1, 2, BQ), axis=2).reshape(-1)
    dqkv, dgains2 = _attn_post(sv["qkv"], sv["gains2"], dqa, dka, dva, dqb,
                               dkb, dvb, tm // 2)
    dgains = dgains2[:, :HEAD_DIM] + dgains2[:, HEAD_DIM:]
    for k, name in enumerate(("q_norm_a", "k_norm_a", "q_norm_b", "k_norm_b")):
        gs[name] = dgains[k]
    d_qkv = _matmul_tn(sv["hm"], dqkv, dqkv.shape[1] // 2, 2 * tm)
    after_mixer = [jnp.concatenate([d_wo.reshape(N_DEV, -1, D), _to_col_shards(d_qkv)], axis=1)]
    token = None if on_ready is None else on_ready(1, after_mixer)
    dx1, gs["norm_mix"] = _dense_norm_bwd(dx2, dqkv, w["w_qkv"], None, sv["x1"], sm["norm_mix"][i][None], tm)
    g1 = sm["norm_ffn1"][i][None]
    if on_last is None:
        dx0, dyb, dzg, dzu, gs["norm_ffn1"] = _ffn_bwd(dx1, sv["x0"], g1, sv["zg1"], sv["zu1"], ga, gb, 0, tm, token)
        dwin1, dwo1 = _ffn_dw(sv["h1"], dzg, dzu, sv["s1"], dyb, 2 * tm)
        return dx0, (after_ffn2, after_mixer, [dwin1, dwo1.reshape(N_DEV, half, D)]), gs
    dyb, dzg, dzu = _ffn_bwd_dz(dx1, sv["zg1"], sv["zu1"], gb, 0, tm, token)
    dwin1, dwo1 = _ffn_dw(sv["h1"], dzg, dzu, sv["s1"], dyb, 2 * tm, on_small(gs))
    last = [dwin1, dwo1.reshape(N_DEV, half, D)]
    dx0, gs["norm_ffn1"] = _ffn_bwd_dx(dx1, sv["x0"], g1, dzg, dzu, ga, 0, tm, on_last(last))
    return dx0, (after_ffn2, after_mixer, last), gs


def _bias_matrices(rel_bias):
    biases = [_bias_variants(_bias_matrix(rel_bias[:, :8], R, d), R) for R, d in DILATED]
    biases.append(_bias_variants(_bias_matrix(rel_bias[:, 8:], SWA_RADIUS, 1), SWA_RADIUS))
    return biases


def _stack_small(per_layer):
    small = {}
    for k, v in per_layer.items():
        if k == "rel_bias":
            per_branch = [sum(parts) for parts in zip(*v.values())]
            drel_a = sum(_bias_grad(t, R, d) for t, (R, d) in zip(per_branch[:3], DILATED))
            small[k] = jnp.concatenate([drel_a, _bias_grad(per_branch[3], SWA_RADIUS, 1)], axis=1)
        else:
            small[k] = jnp.stack([v[i].reshape(-1) for i in sorted(v)])
    return small


TM = 512
SUM_TILES = (512, 512, 416, 512, 352)
LAST_GROUP = ("ffn1_w_in", "ffn1_w_out")


def _pack_small(d, extra=None):
    parts = [d[k].reshape(-1) for k in SMALL]
    if extra is not None:
        parts.append(extra.reshape(-1))
    flat = jnp.concatenate(parts)
    return jnp.pad(flat, (0, SMALL_ROWS * 128 - flat.shape[0])).reshape(SMALL_ROWS, 128)


def _unpack_small(buf, like):
    flat = buf.reshape(-1)
    out, off = {}, 0
    for k in SMALL:
        n = like[k].size
        out[k] = flat[off:off + n].reshape(like[k].shape)
        off += n
    return out, flat[off]


def kernel(x, p, rel_bias, norm_ffn1, ffn1_w_in, ffn1_w_out, norm_mix, w_qkv, q_norm_a, k_norm_a, q_norm_b, k_norm_b, sink_b, w_o, norm_ffn2, ffn2_w_in, ffn2_w_out, norm_ple, w_ple_gate, w_ple_proj, loss_target, m_rel_bias, m_norm_ffn1, m_ffn1_w_in, m_ffn1_w_out, m_norm_mix, m_w_qkv, m_q_norm_a, m_k_norm_a, m_q_norm_b, m_k_norm_b, m_sink_b, m_w_o, m_norm_ffn2, m_ffn2_w_in, m_ffn2_w_out, m_norm_ple, m_w_ple_gate, m_w_ple_proj, v_rel_bias, v_norm_ffn1, v_ffn1_w_in, v_ffn1_w_out, v_norm_mix, v_w_qkv, v_q_norm_a, v_k_norm_a, v_q_norm_b, v_k_norm_b, v_sink_b, v_w_o, v_norm_ffn2, v_ffn2_w_in, v_ffn2_w_out, v_norm_ple, v_w_ple_gate, v_w_ple_proj):
    wts = dict(rel_bias=rel_bias, norm_ffn1=norm_ffn1, ffn1_w_in=ffn1_w_in, ffn1_w_out=ffn1_w_out,
               norm_mix=norm_mix, w_qkv=w_qkv, q_norm_a=q_norm_a, k_norm_a=k_norm_a, q_norm_b=q_norm_b,
               k_norm_b=k_norm_b, sink_b=sink_b, w_o=w_o, norm_ffn2=norm_ffn2, ffn2_w_in=ffn2_w_in,
               ffn2_w_out=ffn2_w_out, norm_ple=norm_ple, w_ple_gate=w_ple_gate, w_ple_proj=w_ple_proj)
    mom = dict(rel_bias=m_rel_bias, norm_ffn1=m_norm_ffn1, ffn1_w_in=m_ffn1_w_in, ffn1_w_out=m_ffn1_w_out,
               norm_mix=m_norm_mix, w_qkv=m_w_qkv, q_norm_a=m_q_norm_a, k_norm_a=m_k_norm_a, q_norm_b=m_q_norm_b,
               k_norm_b=m_k_norm_b, sink_b=m_sink_b, w_o=m_w_o, norm_ffn2=m_norm_ffn2, ffn2_w_in=m_ffn2_w_in,
               ffn2_w_out=m_ffn2_w_out, norm_ple=m_norm_ple, w_ple_gate=m_w_ple_gate, w_ple_proj=m_w_ple_proj)
    var = dict(rel_bias=v_rel_bias, norm_ffn1=v_norm_ffn1, ffn1_w_in=v_ffn1_w_in, ffn1_w_out=v_ffn1_w_out,
               norm_mix=v_norm_mix, w_qkv=v_w_qkv, q_norm_a=v_q_norm_a, k_norm_a=v_k_norm_a, q_norm_b=v_q_norm_b,
               k_norm_b=v_k_norm_b, sink_b=v_sink_b, w_o=v_w_o, norm_ffn2=v_norm_ffn2, ffn2_w_in=v_ffn2_w_in,
               ffn2_w_out=v_ffn2_w_out, norm_ple=v_norm_ple, w_ple_gate=v_w_ple_gate, w_ple_proj=v_w_ple_proj)
    sm = {k: wts[k] for k in SMALL}
    p_dim = p.shape[-1]
    me = 4 * lax.axis_index("x") + 2 * lax.axis_index("y") + lax.axis_index("c")
    packed = []
    for i in range(2):
        a, b = _pack_layer(wts, i)
        packed.append([a.reshape(-1, a.shape[-1]).astype(BF16), b.astype(BF16)])
    a_shape = (2, ffn1_w_in.shape[1], ffn1_w_in.shape[2])

    def weights_of(zones):
        return _layer_weights(zones[0].reshape((N_DEV,) + a_shape), zones[1], p_dim)

    w0 = weights_of(_all_gather(packed[0]))
    zone_shapes = [(N_DEV,) + t.shape for t in packed[1]]
    ssem, rsem, thru, zones, token = _exchange_start(packed[1], zone_shapes, False, "gather_start")
    biases = _bias_matrices(rel_bias)
    x1, _, sv0 = _layer_fwd(x[0], p[0, 0], w0, sm, 0, None, TM, biases, dep=token)
    zones = _exchange_wait(ssem, rsem, thru, zones, x1, False, "gather_wait")
    w1 = weights_of([lax.dynamic_update_index_in_dim(z, t, me, 0) for z, t in zip(zones, packed[1])])
    dy, loss, sv1 = _layer_fwd(x1, p[1, 0], w1, sm, 1, loss_target[0], TM, biases)

    def slots_for(arrs):
        return [(N_DEV - 1,) + t.shape[1:] for t in arrs]

    held1, held = {}, {}

    def on_ready1(stage, group):
        held1[stage] = _exchange_start(group, slots_for(group), True, f"scatter1_start_{stage}")
        return held1[stage][4]

    dx1, groups1, gs1 = _layer_bwd(dy, w1, sm, 1, sv1, TM, on_ready=on_ready1)
    on_ready1(2, groups1[2])
    g1 = groups1[0] + groups1[1] + groups1[2]

    def on_ready(stage, group):
        if stage == 1:
            held["slots1"] = [t for st in (0, 1, 2)
                              for t in _exchange_wait(*held1[st][:4], group[0], True, f"scatter1_wait_{st}")]
        held[stage] = _exchange_start(group, slots_for(group), True, f"scatter_start_{stage}")
        return held[stage][4]

    def on_small(gs0):
        part = dict(gs0, norm_ffn1=jnp.zeros_like(gs1["norm_ffn1"]))
        gsmall = _stack_small({k: {0: part[k], 1: gs1[k]} for k in part})
        held["small"] = _all_reduce_small(_pack_small(gsmall, loss[0, :1]))
        return held["small"]

    def on_last(group):
        held["last"] = _exchange_start(group, slots_for(group), True, "scatter_start_2")
        return held["last"][4]

    dx, groups0, gs0 = _layer_bwd(dx1, w0, sm, 0, sv0, TM, dep=held1[2][4], on_ready=on_ready, on_small=on_small,
                                  on_last=on_last)
    last = groups0[2]
    slots0 = [_exchange_wait(*held[stage][:4], last[0], True, f"scatter_wait_{stage}") for stage in (0, 1)]

    def summed(arrs, slots, tiles, dep=None):
        return [_sum_parts(lax.dynamic_index_in_dim(t, me, 0, keepdims=False), s_, tr, dep)
                for t, s_, tr in zip(arrs, slots, tiles)]

    cover = held["last"][4]
    r1 = summed(g1, held["slots1"], SUM_TILES, cover)
    r0 = summed(groups0[0], slots0[0], SUM_TILES[:2], cover) + summed(groups0[1], slots0[1], SUM_TILES[2:3], cover)

    def update(names, layers):
        for k in names:
            grads[k] = jnp.stack([layers[0][k], layers[1][k]])
            delta[k], new_m[k], new_v[k] = _adamw(wts[k], grads[k], mom[k], var[k])

    grads, delta, new_m, new_v = {}, {}, {}, {}
    layer1 = _unpack_layer(r1, wts)
    update([k for k in BIG if k not in LAST_GROUP], [_unpack_layer(r0 + [None, None], wts), layer1])

    cover_done = [dx] + [delta[k] for k in BIG if k not in LAST_GROUP]
    slots_last = _exchange_wait(*held["last"][:4], cover_done, True, "scatter_wait_2")
    update(LAST_GROUP, [_unpack_layer([None, None, None] + summed(last, slots_last, SUM_TILES[3:]), wts), layer1])
    late = _all_reduce_small(gs0["norm_ffn1"].reshape(-1, 128), dep=slots_last[0])
    small_sum, loss_sum = _unpack_small(held["small"], sm)
    small_sum["norm_ffn1"] = small_sum["norm_ffn1"].at[0].add(late.reshape(-1))
    grads.update(small_sum)
    zeros = {k: jnp.zeros_like(wts[k]) for k in SMALL}
    ds, ms, vs = _adamw(_pack_small(wts), _pack_small(small_sum), _pack_small(mom), _pack_small(var))
    for packed, dst in ((ds, delta), (ms, new_m), (vs, new_v)):
        dst.update(_unpack_small(packed, zeros)[0])

    return (loss_sum, dx[None], *[grads[k] for k in WEIGHTS], *[delta[k] for k in WEIGHTS],
            *[new_m[k] for k in WEIGHTS], *[new_v[k] for k in WEIGHTS])
```

```python
import functools
import math

import jax
import jax.numpy as jnp
from jax import lax
from jax.experimental import pallas as pl
from jax.experimental.pallas import tpu as pltpu

F32 = jnp.float32
BF16 = jnp.bfloat16

N_DEV = 8
HEAD_DIM = 64
PAIR = 2 * HEAD_DIM
BQ = 128
N_BUCKETS = 32
MAX_DISTANCE = 1024
DILATED = ((64, 1), (64, 4), (64, 16))
SWA_RADIUS = 128
EPS = 1e-6
NEG = -1e30
ADAM_LR, ADAM_B1, ADAM_B2, ADAM_EPS, ADAM_WD, ADAM_STEP = 0.001, 0.9, 0.999, 1e-08, 0.01, 10
VMEM_LIMIT = 56 * 1024 * 1024
AXES = ("x", "y", "c")
MESH = pl.DeviceIdType.MESH

BIG = ("ffn1_w_in", "ffn1_w_out", "w_qkv", "w_o", "ffn2_w_in", "ffn2_w_out", "w_ple_gate", "w_ple_proj")
SMALL = ("rel_bias", "norm_ffn1", "norm_mix", "q_norm_a", "k_norm_a", "q_norm_b", "k_norm_b", "sink_b",
         "norm_ffn2", "norm_ple")
WEIGHTS = ("rel_bias", "norm_ffn1", "ffn1_w_in", "ffn1_w_out", "norm_mix", "w_qkv", "q_norm_a", "k_norm_a",
           "q_norm_b", "k_norm_b", "sink_b", "w_o", "norm_ffn2", "ffn2_w_in", "ffn2_w_out", "norm_ple",
           "w_ple_gate", "w_ple_proj")
SMALL_ROWS = 96


def _params(*sem):
    return pltpu.CompilerParams(dimension_semantics=sem, vmem_limit_bytes=VMEM_LIMIT)


def _dot(a, b):
    return jnp.dot(a, b, preferred_element_type=F32)


def _dot_nt(a, b):
    return lax.dot_general(a, b, (((1,), (1,)), ((), ())), preferred_element_type=F32)


def _dot_tn(a, b):
    return lax.dot_general(a, b, (((0,), (0,)), ((), ())), preferred_element_type=F32)


def _sigmoid(x):
    return 1.0 / (1.0 + jnp.exp(-x))


def _rstd(xv):
    return lax.rsqrt(jnp.mean(xv * xv, axis=-1, keepdims=True) + EPS)


def _norm_bwd(dh, xv, gv):
    r = _rstd(xv)
    xn = xv * r
    dg = jnp.sum(dh * xn, axis=0, keepdims=True)
    dxn = dh * gv
    dx = r * (dxn - xn * jnp.mean(dxn * xn, axis=-1, keepdims=True))
    return dx, dg


def _lo_mask(shape):
    return lax.broadcasted_iota(jnp.int32, shape, len(shape) - 1) < HEAD_DIM


def _half_sum(t, lo):
    s0 = jnp.sum(jnp.where(lo, t, 0.0), axis=1, keepdims=True)
    s1 = jnp.sum(jnp.where(lo, 0.0, t), axis=1, keepdims=True)
    return jnp.where(lo, s0, s1)


FFN_PARTS = 2


def _ffn_weight_specs(f, nj, D, C):
    return [pl.BlockSpec((None, None, D, C), lambda i, j: (j, f, 0, 0)),
            pl.BlockSpec((None, None, D, C), lambda i, j: (j + nj, f, 0, 0)),
            pl.BlockSpec((2, C // 2, D), lambda i, j: (j, f, 0))]


def _with_dep(body, dep, in_specs, args):
    if dep is None:
        return body, in_specs, args

    def body_after(dep_ref, *refs):
        body(*refs)

    return body_after, [pl.BlockSpec(memory_space=pl.ANY)] + in_specs, [dep] + args


def _ffn_fwd(x, g, ga, gb, f, tm, dep=None):
    T, D = x.shape
    nj, C = ga.shape[0] // 2, ga.shape[3]

    def body(x_ref, g_ref, wg_ref, wu_ref, wo_ref, xo_ref, h_ref, zg_ref, zu_ref, s_ref, h_scr, acc):
        j = pl.program_id(1)

        @pl.when(j == 0)
        def _():
            xv = x_ref[...]
            hb = (xv * _rstd(xv) * g_ref[...]).astype(BF16)
            h_scr[...] = hb
            h_ref[...] = hb
            acc[...] = jnp.zeros_like(acc)

        wo = wo_ref[...].reshape(C, D)
        for part in range(FFN_PARTS):
            sl = pl.ds(part * (tm // FFN_PARTS), tm // FFN_PARTS)
            hb = h_scr[sl, :]
            gt = _dot(hb, wg_ref[...])
            up = _dot(hb, wu_ref[...])
            s = (gt * _sigmoid(gt) * up).astype(BF16)
            zg_ref[sl, :] = gt.astype(BF16)
            zu_ref[sl, :] = up.astype(BF16)
            s_ref[sl, :] = s
            acc[sl, :] += _dot(s, wo)

        @pl.when(j == nj - 1)
        def _():
            xo_ref[...] = x_ref[...] + 0.5 * acc[...]

    tok = pl.BlockSpec((tm, D), lambda i, j: (i, 0))
    chunk = pl.BlockSpec((None, tm, C), lambda i, j: (j, i, 0))
    in_specs = [tok, pl.BlockSpec((1, D), lambda i, j: (0, 0))] + _ffn_weight_specs(f, nj, D, C)
    body, in_specs, args = _with_dep(body, dep, in_specs, [x, g, ga, ga, gb])
    return pl.pallas_call(
        body, name="ffn_fwd", grid=(T // tm, nj),
        in_specs=in_specs,
        out_specs=[tok, tok, chunk, chunk, chunk],
        out_shape=[jax.ShapeDtypeStruct((T, D), F32), jax.ShapeDtypeStruct((T, D), BF16),
                   jax.ShapeDtypeStruct((nj, T, C), BF16), jax.ShapeDtypeStruct((nj, T, C), BF16),
                   jax.ShapeDtypeStruct((nj, T, C), BF16)],
        scratch_shapes=[pltpu.VMEM((tm, D), BF16), pltpu.VMEM((tm, D), F32)],
        compiler_params=_params("parallel", "arbitrary"),
    )(*args)


def _ffn_bwd(dxo, x, g, zg, zu, ga, gb, f, tm, dep=None):
    T, D = x.shape
    nj, C = ga.shape[0] // 2, ga.shape[3]

    def body(dxo_ref, x_ref, g_ref, zg_ref, zu_ref, wg_ref, wu_ref, wo_ref,
             dx_ref, dy_ref, dzg_ref, dzu_ref, dgn_ref, dy_scr, acc):
        i, j = pl.program_id(0), pl.program_id(1)

        @pl.when(j == 0)
        def _():
            dyb = (0.5 * dxo_ref[...]).astype(BF16)
            dy_scr[...] = dyb
            dy_ref[...] = dyb
            acc[...] = jnp.zeros_like(acc)

        wo = wo_ref[...].reshape(C, D)
        for part in range(FFN_PARTS):
            sl = pl.ds(part * (tm // FFN_PARTS), tm // FFN_PARTS)
            ds = _dot_nt(dy_scr[sl, :], wo)
            gt = zg_ref[sl, :].astype(F32)
            up = zu_ref[sl, :].astype(F32)
            sg = _sigmoid(gt)
            dgt = (ds * up * (sg * (1.0 + gt * (1.0 - sg)))).astype(BF16)
            dup = (ds * (gt * sg)).astype(BF16)
            dzg_ref[sl, :] = dgt
            dzu_ref[sl, :] = dup
            acc[sl, :] += _dot_nt(dgt, wg_ref[...]) + _dot_nt(dup, wu_ref[...])

        @pl.when(j == nj - 1)
        def _():
            dx, dg = _norm_bwd(acc[...], x_ref[...], g_ref[...])
            dx_ref[...] = dxo_ref[...] + dx

            @pl.when(i == 0)
            def _():
                dgn_ref[...] = dg

            @pl.when(i > 0)
            def _():
                dgn_ref[...] += dg

    tok = pl.BlockSpec((tm, D), lambda i, j: (i, 0))
    chunk = pl.BlockSpec((None, tm, C), lambda i, j: (j, i, 0))
    row = pl.BlockSpec((1, D), lambda i, j: (0, 0))
    in_specs = [tok, tok, row, chunk, chunk] + _ffn_weight_specs(f, nj, D, C)
    body, in_specs, args = _with_dep(body, dep, in_specs, [dxo, x, g, zg, zu, ga, ga, gb])
    return pl.pallas_call(
        body, name="ffn_bwd", grid=(T // tm, nj),
        in_specs=in_specs,
        out_specs=[tok, tok, chunk, chunk, row],
        out_shape=[jax.ShapeDtypeStruct((T, D), F32), jax.ShapeDtypeStruct((T, D), BF16),
                   jax.ShapeDtypeStruct((nj, T, C), BF16), jax.ShapeDtypeStruct((nj, T, C), BF16),
                   jax.ShapeDtypeStruct((1, D), F32)],
        scratch_shapes=[pltpu.VMEM((tm, D), BF16), pltpu.VMEM((tm, D), F32)],
        compiler_params=_params("arbitrary", "arbitrary"),
    )(*args)


def _ffn_bwd_dz(dxo, zg, zu, gb, f, tm, dep=None):
    T, D = dxo.shape
    nj, C = zg.shape[0], zg.shape[2]

    def body(dxo_ref, zg_ref, zu_ref, wo_ref, dy_ref, dzg_ref, dzu_ref, dy_scr):
        @pl.when(pl.program_id(1) == 0)
        def _():
            dyb = (0.5 * dxo_ref[...]).astype(BF16)
            dy_scr[...] = dyb
            dy_ref[...] = dyb

        wo = wo_ref[...].reshape(C, D)
        for part in range(FFN_PARTS):
            sl = pl.ds(part * (tm // FFN_PARTS), tm // FFN_PARTS)
            ds = _dot_nt(dy_scr[sl, :], wo)
            gt = zg_ref[sl, :].astype(F32)
            up = zu_ref[sl, :].astype(F32)
            sg = _sigmoid(gt)
            dzg_ref[sl, :] = (ds * up * (sg * (1.0 + gt * (1.0 - sg)))).astype(BF16)
            dzu_ref[sl, :] = (ds * (gt * sg)).astype(BF16)

    tok = pl.BlockSpec((tm, D), lambda i, j: (i, 0))
    chunk = pl.BlockSpec((None, tm, C), lambda i, j: (j, i, 0))
    in_specs = [tok, chunk, chunk, _ffn_weight_specs(f, nj, D, C)[2]]
    body, in_specs, args = _with_dep(body, dep, in_specs, [dxo, zg, zu, gb])
    return pl.pallas_call(
        body, name="ffn_bwd_dz", grid=(T // tm, nj),
        in_specs=in_specs, out_specs=[tok, chunk, chunk],
        out_shape=[jax.ShapeDtypeStruct((T, D), BF16), jax.ShapeDtypeStruct((nj, T, C), BF16),
                   jax.ShapeDtypeStruct((nj, T, C), BF16)],
        scratch_shapes=[pltpu.VMEM((tm, D), BF16)],
        compiler_params=_params("parallel", "arbitrary"),
    )(*args)


def _ffn_bwd_dx(dxo, x, g, dzg, dzu, ga, f, tm, dep=None):
    T, D = x.shape
    nj, C = ga.shape[0] // 2, ga.shape[3]

    def body(dxo_ref, x_ref, g_ref, dzg_ref, dzu_ref, wg_ref, wu_ref, dx_ref, dgn_ref, acc):
        i, j = pl.program_id(0), pl.program_id(1)

        @pl.when(j == 0)
        def _():
            acc[...] = jnp.zeros_like(acc)

        acc[...] += _dot_nt(dzg_ref[...], wg_ref[...]) + _dot_nt(dzu_ref[...], wu_ref[...])

        @pl.when(j == nj - 1)
        def _():
            dx, dg = _norm_bwd(acc[...], x_ref[...], g_ref[...])
            dx_ref[...] = dxo_ref[...] + dx

            @pl.when(i == 0)
            def _():
                dgn_ref[...] = dg

            @pl.when(i > 0)
            def _():
                dgn_ref[...] += dg

    tok = pl.BlockSpec((tm, D), lambda i, j: (i, 0))
    chunk = pl.BlockSpec((None, tm, C), lambda i, j: (j, i, 0))
    row = pl.BlockSpec((1, D), lambda i, j: (0, 0))
    in_specs = [tok, tok, row, chunk, chunk] + _ffn_weight_specs(f, nj, D, C)[:2]
    body, in_specs, args = _with_dep(body, dep, in_specs, [dxo, x, g, dzg, dzu, ga, ga])
    return pl.pallas_call(
        body, name="ffn_bwd_dx", grid=(T // tm, nj),
        in_specs=in_specs, out_specs=[tok, row],
        out_shape=[jax.ShapeDtypeStruct((T, D), F32), jax.ShapeDtypeStruct((1, D), F32)],
        scratch_shapes=[pltpu.VMEM((tm, D), F32)],
        compiler_params=_params("arbitrary", "arbitrary"),
    )(*args)


def _ffn_dw(h, dzg, dzu, s, dy, tk, dep=None):
    T, D = h.shape
    nj, C = s.shape[0], s.shape[2]
    nk = T // tk

    def body(h_ref, dzg_ref, dzu_ref, s_ref, dy_ref, dwin_ref, dwo_ref, ag, au, ao):
        k = pl.program_id(1)

        @pl.when(k == 0)
        def _():
            ag[...] = jnp.zeros_like(ag)
            au[...] = jnp.zeros_like(au)
            ao[...] = jnp.zeros_like(ao)

        hb = h_ref[...]
        ag[...] += _dot_tn(hb, dzg_ref[...])
        au[...] += _dot_tn(hb, dzu_ref[...])
        ao[...] += _dot_tn(s_ref[...], dy_ref[...])

        @pl.when(k == nk - 1)
        def _():
            dwin_ref[0] = ag[...].astype(BF16)
            dwin_ref[1] = au[...].astype(BF16)
            dwo_ref[...] = ao[...].astype(BF16)

    tok = pl.BlockSpec((tk, D), lambda j, k: (k, 0))
    chunk = pl.BlockSpec((None, tk, C), lambda j, k: (j, k, 0))
    body, in_specs, args = _with_dep(body, dep, [tok, chunk, chunk, chunk, tok], [h, dzg, dzu, s, dy])
    dwin, dwo = pl.pallas_call(
        body, name="ffn_dw", grid=(nj, nk),
        in_specs=in_specs,
        out_specs=[pl.BlockSpec((2, None, D, C), lambda j, k: (0, j, 0, 0)),
                   pl.BlockSpec((None, C, D), lambda j, k: (j, 0, 0))],
        out_shape=[jax.ShapeDtypeStruct((2, nj, D, C), BF16), jax.ShapeDtypeStruct((nj, C, D), BF16)],
        scratch_shapes=[pltpu.VMEM((D, C), F32), pltpu.VMEM((D, C), F32), pltpu.VMEM((C, D), F32)],
        compiler_params=_params("parallel", "arbitrary"),
    )(*args)
    return dwin.reshape(2 * nj, D, C), dwo


def _matmul_tn(a, b, tn, tk):
    T, Ka = a.shape
    N = b.shape[1]
    nk = T // tk

    def body(a_ref, b_ref, o_ref, acc):
        k = pl.program_id(1)

        @pl.when(k == 0)
        def _():
            acc[...] = jnp.zeros_like(acc)

        acc[...] += _dot_tn(a_ref[...], b_ref[...])

        @pl.when(k == nk - 1)
        def _():
            o_ref[...] = acc[...].astype(BF16)

    return pl.pallas_call(
        body, name="matmul_tn", grid=(N // tn, nk),
        in_specs=[pl.BlockSpec((tk, Ka), lambda n, k: (k, 0)), pl.BlockSpec((tk, tn), lambda n, k: (k, n))],
        out_specs=pl.BlockSpec((Ka, tn), lambda n, k: (0, n)),
        out_shape=jax.ShapeDtypeStruct((Ka, N), BF16),
        scratch_shapes=[pltpu.VMEM((Ka, tn), F32)],
        compiler_params=_params("parallel", "arbitrary"),
    )(a, b)


def _qkv_fwd(x, g, w, tm):
    T, D = x.shape
    N = w.shape[1]

    def body(x_ref, g_ref, w_ref, o_ref, h_ref):
        xv = x_ref[...]
        hb = (xv * _rstd(xv) * g_ref[...]).astype(BF16)
        h_ref[...] = hb
        o_ref[...] = _dot(hb, w_ref[...])

    return pl.pallas_call(
        body, name="qkv_fwd", grid=(T // tm,),
        in_specs=[pl.BlockSpec((tm, D), lambda i: (i, 0)), pl.BlockSpec((1, D), lambda i: (0, 0)),
                  pl.BlockSpec((D, N), lambda i: (0, 0))],
        out_specs=[pl.BlockSpec((tm, N), lambda i: (i, 0)), pl.BlockSpec((tm, D), lambda i: (i, 0))],
        out_shape=[jax.ShapeDtypeStruct((T, N), F32), jax.ShapeDtypeStruct((T, D), BF16)],
        compiler_params=_params("parallel"),
    )(x, g, w)


DILS = tuple(d for _, d in DILATED)


def _spread_specs(tm, T, dtype):
    specs = [pl.BlockSpec((4, d, tm // d, PAIR), lambda i: (0, 0, i, 0)) for d in DILS]
    shapes = [jax.ShapeDtypeStruct((4, d, T // d, PAIR), dtype) for d in DILS]
    return specs, shapes


def _spread(tile, y, outs, c, dtype):
    tm = y.shape[0]
    tile[...] = y
    for out, d in zip(outs, DILS):
        for r in range(d):
            out[c, r] = tile[pl.ds(r, tm // d, stride=d), :].astype(dtype)


def _collect(tile, ins, c):
    tm = tile.shape[0]
    first = True
    for ref, d in zip(ins, DILS):
        for r in range(d):
            rows = pl.ds(r, tm // d, stride=d) if d > 1 else pl.ds(0, tm)
            part = ref[c, r].astype(F32)
            tile[rows, :] = part if first else tile[rows, :] + part
        first = False
    return tile[...]


def _attn_prep(qkv, gains2, tm):
    T = qkv.shape[0]
    scale = HEAD_DIM ** -0.5
    n = len(DILS)

    def body(qkv_ref, g_ref, qb_ref, kb_ref, vb_ref, *rest):
        outs, tile = rest[:-1], rest[-1]
        lo = _lo_mask((tm, PAIR))

        def spread(kind, c, y):
            _spread(tile, y, outs[kind * n:(kind + 1) * n], c, BF16)

        def normed(c, gi, mult):
            xv = qkv_ref[:, c * PAIR:(c + 1) * PAIR]
            r = lax.rsqrt(_half_sum(xv * xv, lo) * (1.0 / HEAD_DIM) + EPS)
            y = xv * r * g_ref[gi:gi + 1, :]
            return y * mult if mult != 1.0 else y

        def both_halves(v):
            sw = pltpu.roll(v, HEAD_DIM, 1)
            return jnp.where(lo, v, sw), jnp.where(lo, sw, v)

        for c in range(4):
            spread(0, c, normed(c, 0, scale))
            spread(1, c, normed(4 + c, 1, 1.0))
            spread(2, c, qkv_ref[:, (8 + c) * PAIR:(9 + c) * PAIR])
            qb_ref[c] = normed(12 + c, 2, scale).astype(BF16)
        k0, k1 = both_halves(normed(16, 3, 1.0))
        kb_ref[0] = k0.astype(BF16)
        kb_ref[1] = k1.astype(BF16)
        v0, v1 = both_halves(qkv_ref[:, 17 * PAIR:18 * PAIR])
        vb_ref[0] = v0.astype(BF16)
        vb_ref[1] = v1.astype(BF16)

    four = pl.BlockSpec((4, tm, PAIR), lambda i: (0, i, 0))
    two = pl.BlockSpec((2, tm, PAIR), lambda i: (0, i, 0))
    s4 = jax.ShapeDtypeStruct((4, T, PAIR), BF16)
    s2 = jax.ShapeDtypeStruct((2, T, PAIR), BF16)
    specs, shapes = _spread_specs(tm, T, BF16)
    res = pl.pallas_call(
        body, name="attn_prep", grid=(T // tm,),
        in_specs=[pl.BlockSpec((tm, qkv.shape[1]), lambda i: (i, 0)), pl.BlockSpec((4, PAIR), lambda i: (0, 0))],
        out_specs=[four, two, two] + specs * 3,
        out_shape=[s4, s2, s2] + shapes * 3,
        scratch_shapes=[pltpu.VMEM((tm, PAIR), F32)],
        compiler_params=_params("parallel"),
    )(qkv, gains2)
    qb, kb, vb = res[:3]
    per_d = [tuple(res[3 + kind * n + di].reshape(4 * d, T // d, PAIR) for kind in range(3))
             for di, d in enumerate(DILS)]
    return qb, kb, vb, per_d


def _loop_blocks(nb, body, init, per_iter):
    u = math.gcd(nb, per_iter)

    def outer(i, carry):
        for k in range(u):
            carry = body(i * u + k, carry)
        return carry

    return lax.fori_loop(0, nb // u, outer, init)


def _key_window(b, nb, L, R, W):
    start = pl.multiple_of(jnp.clip(b * BQ - R, 0, L - W), HEAD_DIM)
    return start, jnp.where(b == 0, 1, jnp.where(b == nb - 1, 2, 0))


def _stack_heads(v, lo):
    z = jnp.zeros_like(v)
    return jnp.concatenate([jnp.where(lo, v, z), jnp.where(lo, z, v)], axis=0)


def _unstack_heads(v2, lo):
    return jnp.where(lo, v2[:BQ], v2[BQ:])


def _row_vector(v, lo):
    r = lax.broadcasted_iota(jnp.int32, (BQ, PAIR), 0)
    ln = lax.broadcasted_iota(jnp.int32, (BQ, PAIR), 1)
    diag = (ln % HEAD_DIM) == (r % HEAD_DIM)
    top = jnp.sum(jnp.where(diag & (r < HEAD_DIM), v, 0.0), axis=0, keepdims=True)
    bot = jnp.sum(jnp.where(diag & (r >= HEAD_DIM), v, 0.0), axis=0, keepdims=True)
    top8, bot8 = jnp.broadcast_to(top, (8, PAIR)), jnp.broadcast_to(bot, (8, PAIR))
    lo8 = _lo_mask((8, PAIR))
    head0 = jnp.where(lo8, top8, pltpu.roll(bot8, HEAD_DIM, 1))
    head1 = jnp.where(lo8, pltpu.roll(top8, HEAD_DIM, 1), bot8)
    return jnp.concatenate([head0, head1], axis=1)[:1]


def _units_per_step(nb, pairs_per_kv):
    return max(1, 16 // nb) if pairs_per_kv == 1 else 1


def _attn_fwd(q, kp, vp, bias4, sink, R, pairs_per_kv, pairs_per_bias):
    N, L, _ = q.shape
    W = BQ + 2 * R
    nb = L // BQ
    assert L >= W and nb >= 2
    G = _units_per_step(nb, pairs_per_kv)

    def body(sink_ref, q_ref, k_ref, v_ref, bias_ref, o_ref, lse_ref):
        n = pl.program_id(0)
        lo_q = _lo_mask((BQ, PAIR))
        first = lax.broadcasted_iota(jnp.int32, (2 * BQ, 1), 0) < BQ

        def blk(f, carry):
            g, b = f // nb, f % nb
            u = n * G + g
            sk = jnp.where(first, sink_ref[2 * u], sink_ref[2 * u + 1])
            q0 = pl.multiple_of(b * BQ, BQ)
            q2 = _stack_heads(q_ref[g, pl.ds(q0, BQ), :], lo_q)
            k0, variant = _key_window(b, nb, L, R, W)
            kw = k_ref[g, pl.ds(k0, W), :]
            vw = v_ref[g, pl.ds(k0, W), :]
            s = _dot_nt(q2, kw) + bias_ref[variant]
            m = jnp.maximum(jnp.max(s, axis=1, keepdims=True), sk)
            p = jnp.exp(s - m)
            l = jnp.sum(p, axis=1, keepdims=True) + jnp.exp(sk - m)
            o2 = _dot(p.astype(BF16), vw) / l
            o_ref[g, pl.ds(q0, BQ), :] = _unstack_heads(o2, lo_q)
            lse_ref[g, pl.ds(q0, BQ), :] = _unstack_heads(jnp.broadcast_to(m + jnp.log(l), (2 * BQ, PAIR)), lo_q)
            return carry

        _loop_blocks(G * nb, blk, 0, 4)

    qspec = pl.BlockSpec((G, L, PAIR), lambda n: (n, 0, 0))
    kspec = pl.BlockSpec((G, L, PAIR), lambda n: (n // pairs_per_kv, 0, 0))
    return pl.pallas_call(
        body, name="attn_fwd", grid=(N // G,),
        in_specs=[pl.BlockSpec(memory_space=pltpu.SMEM), qspec, kspec, kspec,
                  pl.BlockSpec((None, 3, 2 * BQ, W), lambda n: (n * G // pairs_per_bias, 0, 0, 0))],
        out_specs=[qspec, qspec],
        out_shape=[jax.ShapeDtypeStruct((N, L, PAIR), F32), jax.ShapeDtypeStruct((N, L, PAIR), F32)],
        compiler_params=_params("parallel"),
    )(sink, q, kp, vp, bias4)


def _attn_bwd(q, kp, vp, bias4t, sink, o, lse, do, R, pairs_per_kv, pairs_per_bias):
    N, L, _ = q.shape
    Nk = kp.shape[0]
    Pb = bias4t.shape[0]
    W = BQ + 2 * R
    nb = L // BQ
    assert L >= W and nb >= 2
    G = _units_per_step(nb, pairs_per_kv)

    def body(sink_ref, q_ref, k_ref, v_ref, bias_ref, o_ref, lse_ref, do_ref,
             dq_ref, dk_ref, dv_ref, dbias_ref, dsink_ref, dk_acc, dv_acc):
        n = pl.program_id(0)
        lo_q = _lo_mask((BQ, PAIR))
        first = lax.broadcasted_iota(jnp.int32, (1, 2 * BQ), 1) < BQ
        dsink_ref[...] = jnp.zeros_like(dsink_ref)

        @pl.when(n % pairs_per_kv == 0)
        def _():
            dk_acc[...] = jnp.zeros_like(dk_acc)
            dv_acc[...] = jnp.zeros_like(dv_acc)

        @pl.when((n * G) % pairs_per_bias == 0)
        def _():
            dbias_ref[...] = jnp.zeros_like(dbias_ref)

        def blk(f, carry):
            g, b = f // nb, f % nb
            u = n * G + g
            sk = jnp.where(first, sink_ref[2 * u], sink_ref[2 * u + 1])
            q0 = pl.multiple_of(b * BQ, BQ)
            q2 = _stack_heads(q_ref[g, pl.ds(q0, BQ), :], lo_q)
            k0, variant = _key_window(b, nb, L, R, W)
            kw = k_ref[g, pl.ds(k0, W), :]
            vw = v_ref[g, pl.ds(k0, W), :]
            dov = do_ref[g, pl.ds(q0, BQ), :]
            lse = _row_vector(lse_ref[g, pl.ds(q0, BQ), :], lo_q)
            delta = _row_vector(_half_sum(dov.astype(F32) * o_ref[g, pl.ds(q0, BQ), :], lo_q), lo_q)
            do2 = _stack_heads(dov.astype(BF16), lo_q)
            st = _dot_nt(kw, q2) + bias_ref[variant]
            pt = jnp.exp(st - lse)
            dst = pt * (_dot_nt(vw, do2) - delta)
            dstb = dst.astype(BF16)
            dbias_ref[variant] += dst
            dk_acc[g, pl.ds(k0, W), :] += _dot(dstb, q2)
            dv_acc[g, pl.ds(k0, W), :] += _dot(pt.astype(BF16), do2)
            dq_ref[g, pl.ds(q0, BQ), :] = _unstack_heads(_dot_tn(dstb, kw), lo_q).astype(BF16)
            dsink_ref[g, pl.ds(0, 1), :] -= jnp.exp(sk - lse) * delta
            return carry

        _loop_blocks(G * nb, blk, 0, 4)
        dk_ref[...] = dk_acc[...].astype(BF16)
        dv_ref[...] = dv_acc[...].astype(BF16)

    qspec = pl.BlockSpec((G, L, PAIR), lambda n: (n, 0, 0))
    kspec = pl.BlockSpec((G, L, PAIR), lambda n: (n // pairs_per_kv, 0, 0))
    return pl.pallas_call(
        body, name="attn_bwd", grid=(N // G,),
        in_specs=[pl.BlockSpec(memory_space=pltpu.SMEM), qspec, kspec, kspec,
                  pl.BlockSpec((None, 3, W, 2 * BQ), lambda n: (n * G // pairs_per_bias, 0, 0, 0)),
                  qspec, qspec, qspec],
        out_specs=[qspec, kspec, kspec,
                   pl.BlockSpec((None, 3, W, 2 * BQ), lambda n: (n * G // pairs_per_bias, 0, 0, 0)),
                   pl.BlockSpec((G, 8, 2 * BQ), lambda n: (n, 0, 0))],
        out_shape=[jax.ShapeDtypeStruct((N, L, PAIR), BF16),
                   jax.ShapeDtypeStruct((Nk, L, PAIR), BF16),
                   jax.ShapeDtypeStruct((Nk, L, PAIR), BF16),
                   jax.ShapeDtypeStruct((Pb, 3, W, 2 * BQ), F32),
                   jax.ShapeDtypeStruct((N, 8, 2 * BQ), F32)],
        scratch_shapes=[pltpu.VMEM((G, L, PAIR), F32), pltpu.VMEM((G, L, PAIR), F32)],
        compiler_params=_params("arbitrary"),
    )(sink, q, kp, vp, bias4t, o, lse, do)


def _attn_merge(branch_outs, ob, tm):
    T = ob.shape[1]
    n = len(DILS)

    def body(*refs):
        o_in, l_in, ob_ref = refs[:n], refs[n:2 * n], refs[2 * n]
        o_out, l_out, cat_ref = refs[2 * n + 1:3 * n + 1], refs[3 * n + 1:4 * n + 1], refs[4 * n + 1]
        tiles = refs[4 * n + 2:]
        for c in range(4):
            o_nat, l_nat = [], []
            for di, d in enumerate(DILS):
                for kind, (src, dst) in enumerate(((o_in[di], o_nat), (l_in[di], l_nat))):
                    tile = tiles[2 * di + kind]
                    if d == 1:
                        dst.append(src[c, 0])
                    else:
                        for r in range(d):
                            tile[pl.ds(r, tm // d, stride=d), :] = src[c, r]
                        dst.append(tile[...])
            m = functools.reduce(jnp.maximum, l_nat)
            ws = [jnp.exp(l - m) for l in l_nat]
            z = sum(ws)
            o = sum(w * t for w, t in zip(ws, o_nat)) / z
            cat_ref[:, c * PAIR:(c + 1) * PAIR] = o.astype(BF16)
            cat_ref[:, (4 + c) * PAIR:(5 + c) * PAIR] = ob_ref[c].astype(BF16)
            _spread(tiles[0], o, o_out, c, F32)
            _spread(tiles[1], m + jnp.log(z), l_out, c, F32)

    specs, shapes = _spread_specs(tm, T, F32)
    four = pl.BlockSpec((4, tm, PAIR), lambda i: (0, i, 0))
    o_views = [o.reshape(4, d, T // d, PAIR) for (o, _), d in zip(branch_outs, DILS)]
    l_views = [l.reshape(4, d, T // d, PAIR) for (_, l), d in zip(branch_outs, DILS)]
    res = pl.pallas_call(
        body, name="attn_merge", grid=(T // tm,),
        in_specs=specs + specs + [four],
        out_specs=specs + specs + [pl.BlockSpec((tm, 8 * PAIR), lambda i: (i, 0))],
        out_shape=shapes + shapes + [jax.ShapeDtypeStruct((T, 8 * PAIR), BF16)],
        scratch_shapes=[pltpu.VMEM((tm, PAIR), F32)] * (2 * n),
        compiler_params=_params("parallel"),
    )(*o_views, *l_views, ob)
    merged = [(res[di].reshape(4 * d, T // d, PAIR), res[n + di].reshape(4 * d, T // d, PAIR))
              for di, d in enumerate(DILS)]
    return merged, res[2 * n]


def _weight_arg(w, blk):
    if blk is None:
        return pl.BlockSpec(w.shape, lambda i: (0, 0)), (lambda ref: ref[...])
    D = w.shape[2]
    return (pl.BlockSpec((N_DEV, 128, D), lambda i: (0, blk, 0)),
            lambda ref: ref[...].reshape(N_DEV * 128, D))


def _oproj_fwd(x, o_cat, w, blk, tm):
    T, D = x.shape
    wspec, wload = _weight_arg(w, blk)

    def body(x_ref, o_ref, w_ref, out_ref):
        out_ref[...] = x_ref[...] + _dot(o_ref[...], wload(w_ref))

    tok = pl.BlockSpec((tm, D), lambda i: (i, 0))
    return pl.pallas_call(
        body, name="oproj_fwd", grid=(T // tm,),
        in_specs=[tok, pl.BlockSpec((tm, o_cat.shape[1]), lambda i: (i, 0)), wspec],
        out_specs=tok, out_shape=jax.ShapeDtypeStruct((T, D), F32),
        compiler_params=_params("parallel"),
    )(x, o_cat, w)


def _oproj_bwd(dx, w, blk, tm, dep=None):
    T, D = dx.shape
    wspec, wload = _weight_arg(w, blk)

    def body(dx_ref, w_ref, dxb_ref, dob_ref, *rest):
        doa_refs, tile = rest[:-1], rest[-1]
        db = dx_ref[...].astype(BF16)
        dxb_ref[...] = db
        do = _dot_nt(db, wload(w_ref))
        for c in range(4):
            _spread(tile, do[:, c * PAIR:(c + 1) * PAIR], doa_refs, c, BF16)
            dob_ref[c] = do[:, (4 + c) * PAIR:(5 + c) * PAIR].astype(BF16)

    tok = pl.BlockSpec((tm, D), lambda i: (i, 0))
    specs, shapes = _spread_specs(tm, T, BF16)
    body, in_specs, args = _with_dep(body, dep, [tok, wspec], [dx, w])
    res = pl.pallas_call(
        body, name="oproj_bwd", grid=(T // tm,),
        in_specs=in_specs,
        out_specs=[tok, pl.BlockSpec((4, tm, PAIR), lambda i: (0, i, 0))] + specs,
        out_shape=[jax.ShapeDtypeStruct((T, D), BF16), jax.ShapeDtypeStruct((4, T, PAIR), BF16)] + shapes,
        scratch_shapes=[pltpu.VMEM((tm, PAIR), F32)],
        compiler_params=_params("parallel"),
    )(*args)
    return res[0], res[1], [t.reshape(4 * d, T // d, PAIR) for t, d in zip(res[2:], DILS)]


def _attn_post(qkv, gains2, dqa, dka, dva, dqb, dkb, dvb, tm):
    T, NQ = qkv.shape
    scale = HEAD_DIM ** -0.5

    n = len(DILS)

    def body(qkv_ref, g_ref, *rest):
        dq_refs, dk_refs, dv_refs = rest[:n], rest[n:2 * n], rest[2 * n:3 * n]
        qb_ref, kb_ref, vb_ref, out_ref, dg_ref, tile = rest[3 * n:]
        lo = _lo_mask((tm, PAIR))

        @pl.when(pl.program_id(0) == 0)
        def _():
            dg_ref[...] = jnp.zeros_like(dg_ref)

        def norm_bwd(c, gi, dy):
            xv = qkv_ref[:, c * PAIR:(c + 1) * PAIR]
            r = lax.rsqrt(_half_sum(xv * xv, lo) * (1.0 / HEAD_DIM) + EPS)
            xn = xv * r
            dg_ref[gi:gi + 1, :] += jnp.sum(dy * xn, axis=0, keepdims=True)
            dxn = dy * g_ref[gi:gi + 1, :]
            dx = r * (dxn - xn * (_half_sum(dxn * xn, lo) * (1.0 / HEAD_DIM)))
            out_ref[:, c * PAIR:(c + 1) * PAIR] = dx.astype(BF16)

        def fold(v):
            return v + pltpu.roll(v, HEAD_DIM, 1)

        for c in range(4):
            norm_bwd(c, 0, _collect(tile, dq_refs, c) * scale)
            norm_bwd(4 + c, 1, _collect(tile, dk_refs, c))
            out_ref[:, (8 + c) * PAIR:(9 + c) * PAIR] = _collect(tile, dv_refs, c).astype(BF16)
            norm_bwd(12 + c, 2, qb_ref[c].astype(F32) * scale)
        kb, vb = kb_ref[...].astype(F32), vb_ref[...].astype(F32)
        norm_bwd(16, 3, jnp.where(lo, fold(kb[0]), fold(kb[1])))
        out_ref[:, 17 * PAIR:18 * PAIR] = jnp.where(lo, fold(vb[0]), fold(vb[1])).astype(BF16)

    four = pl.BlockSpec((4, tm, PAIR), lambda i: (0, i, 0))
    two = pl.BlockSpec((2, tm, PAIR), lambda i: (0, i, 0))
    specs, _ = _spread_specs(tm, T, BF16)
    views = [t.reshape(4, d, T // d, PAIR) for group in (dqa, dka, dva) for t, d in zip(group, DILS)]
    return pl.pallas_call(
        body, name="attn_post", grid=(T // tm,),
        in_specs=[pl.BlockSpec((tm, NQ), lambda i: (i, 0)), pl.BlockSpec((4, PAIR), lambda i: (0, 0))]
        + specs * 3 + [four, two, two],
        out_specs=[pl.BlockSpec((tm, NQ), lambda i: (i, 0)), pl.BlockSpec((4, PAIR), lambda i: (0, 0))],
        out_shape=[jax.ShapeDtypeStruct((T, NQ), BF16), jax.ShapeDtypeStruct((4, PAIR), F32)],
        scratch_shapes=[pltpu.VMEM((tm, PAIR), F32)],
        compiler_params=_params("arbitrary"),
    )(qkv, gains2, *views, dqb, dkb, dvb)


def _dense_norm_bwd(dres, dz, w, blk, x, g, tm):
    T, D = x.shape
    N = dz.shape[1]
    wspec, wload = _weight_arg(w, blk)

    def body(dres_ref, dz_ref, w_ref, x_ref, g_ref, dx_ref, dgn_ref):
        i = pl.program_id(0)
        dx, dg = _norm_bwd(_dot_nt(dz_ref[...], wload(w_ref)), x_ref[...], g_ref[...])
        dx_ref[...] = dres_ref[...] + dx

        @pl.when(i == 0)
        def _():
            dgn_ref[...] = dg

        @pl.when(i > 0)
        def _():
            dgn_ref[...] += dg

    tok = pl.BlockSpec((tm, D), lambda i: (i, 0))
    row = pl.BlockSpec((1, D), lambda i: (0, 0))
    return pl.pallas_call(
        body, name="dense_norm_bwd", grid=(T // tm,),
        in_specs=[tok, pl.BlockSpec((tm, N), lambda i: (i, 0)), wspec, tok, row],
        out_specs=[tok, row],
        out_shape=[jax.ShapeDtypeStruct((T, D), F32), jax.ShapeDtypeStruct((1, D), F32)],
        compiler_params=_params("arbitrary"),
    )(dres, dz, w, x, g)


def _bias_reduce(onehot, dbm):
    Hb, K = dbm.shape

    def body(oh_ref, d_ref, out_ref):
        oh = oh_ref[...]
        d = d_ref[...]
        hi = d.astype(BF16)
        r1 = d - hi.astype(F32)
        mid = r1.astype(BF16)
        low = (r1 - mid.astype(F32)).astype(BF16)
        out_ref[...] = _dot_nt(hi, oh) + _dot_nt(mid, oh) + _dot_nt(low, oh)

    vm = pl.BlockSpec(memory_space=pltpu.VMEM)
    return pl.pallas_call(
        body, name="bias_reduce", in_specs=[vm, vm], out_specs=vm,
        out_shape=jax.ShapeDtypeStruct((Hb, 128), F32),
        compiler_params=pltpu.CompilerParams(vmem_limit_bytes=VMEM_LIMIT),
    )(onehot, dbm)


def _ple_fwd(x, g, wg, blk, p, wp, target, tm):
    T, D = x.shape
    P = p.shape[1]
    with_loss = target is not None
    wspec, wload = _weight_arg(wg, blk)

    def body(*refs):
        if with_loss:
            x_ref, g_ref, wg_ref, p_ref, wp_ref, t_ref, y_ref, hn_ref, gate_ref, pp_ref, pb_ref, loss_ref = refs
        else:
            x_ref, g_ref, wg_ref, p_ref, wp_ref, y_ref, hn_ref, gate_ref, pp_ref, pb_ref = refs
        i = pl.program_id(0)
        xv = x_ref[...]
        hb = (xv * _rstd(xv) * g_ref[...]).astype(BF16)
        hn_ref[...] = hb
        gate = _sigmoid(_dot(hb, wload(wg_ref)))
        pb = p_ref[...].astype(BF16)
        pb_ref[...] = pb
        pp = _dot(pb, wp_ref[...])
        gate_ref[...] = gate
        pp_ref[...] = pp
        y = xv + gate * pp
        if with_loss:
            err = y - t_ref[...]
            y_ref[...] = err * (1.0 / D)
            part = jnp.broadcast_to(0.5 * jnp.sum(jnp.sum(err * err, axis=1, keepdims=True) * (1.0 / D),
                                                  axis=0, keepdims=True), (1, 128))

            @pl.when(i == 0)
            def _():
                loss_ref[...] = part

            @pl.when(i > 0)
            def _():
                loss_ref[...] += part
        else:
            y_ref[...] = y

    tok = pl.BlockSpec((tm, D), lambda i: (i, 0))
    ptok = pl.BlockSpec((tm, P), lambda i: (i, 0))
    in_specs = [tok, pl.BlockSpec((1, D), lambda i: (0, 0)), wspec, ptok,
                pl.BlockSpec((P, D), lambda i: (0, 0))]
    out_specs = [tok, tok, tok, tok, ptok]
    out_shape = [jax.ShapeDtypeStruct((T, D), F32), jax.ShapeDtypeStruct((T, D), BF16),
                 jax.ShapeDtypeStruct((T, D), F32), jax.ShapeDtypeStruct((T, D), F32),
                 jax.ShapeDtypeStruct((T, P), BF16)]
    args = [x, g, wg, p, wp]
    if with_loss:
        in_specs.append(tok)
        out_specs.append(pl.BlockSpec((1, 128), lambda i: (0, 0)))
        out_shape.append(jax.ShapeDtypeStruct((1, 128), F32))
        args.append(target)
    return pl.pallas_call(
        body, name="ple_fwd_loss" if with_loss else "ple_fwd", grid=(T // tm,),
        in_specs=in_specs, out_specs=out_specs, out_shape=out_shape,
        compiler_params=_params("arbitrary" if with_loss else "parallel"),
    )(*args)


def _ple_bwd(dy, gate, pp, tm, dep=None):
    T, D = dy.shape

    def body(dy_ref, gate_ref, pp_ref, dgl_ref, dpp_ref):
        d = dy_ref[...]
        gt = gate_ref[...]
        dgl_ref[...] = (d * pp_ref[...] * gt * (1.0 - gt)).astype(BF16)
        dpp_ref[...] = (d * gt).astype(BF16)

    tok = pl.BlockSpec((tm, D), lambda i: (i, 0))
    body, in_specs, args = _with_dep(body, dep, [tok, tok, tok], [dy, gate, pp])
    return pl.pallas_call(
        body, name="ple_bwd", grid=(T // tm,), in_specs=in_specs, out_specs=[tok, tok],
        out_shape=[jax.ShapeDtypeStruct((T, D), BF16), jax.ShapeDtypeStruct((T, D), BF16)],
        compiler_params=_params("parallel"),
    )(*args)


def _adamw(w, g, m, v):
    shape = w.shape
    C = shape[-1]
    w2, g2, m2, v2 = (a.reshape(-1, C) for a in (w, g, m, v))
    Rn = w2.shape[0]
    tr = Rn
    for cand in (512, 352, 256):
        if Rn % cand == 0:
            tr = cand
            break
    c1 = 1.0 - ADAM_B1 ** ADAM_STEP
    c2 = 1.0 - ADAM_B2 ** ADAM_STEP

    def body(w_ref, g_ref, m_ref, v_ref, d_ref, nm_ref, nv_ref):
        gv = g_ref[...]
        mn = ADAM_B1 * m_ref[...] + (1.0 - ADAM_B1) * gv
        vn = ADAM_B2 * v_ref[...] + (1.0 - ADAM_B2) * (gv * gv)
        d_ref[...] = -ADAM_LR * ((mn / c1) / (jnp.sqrt(vn / c2) + ADAM_EPS) + ADAM_WD * w_ref[...])
        nm_ref[...] = mn
        nv_ref[...] = vn

    spec = pl.BlockSpec((tr, C), lambda i: (i, 0))
    sh = jax.ShapeDtypeStruct((Rn, C), F32)
    d, nm, nv = pl.pallas_call(
        body, name="adamw", grid=(Rn // tr,), in_specs=[spec] * 4, out_specs=[spec] * 3, out_shape=[sh] * 3,
        compiler_params=_params("parallel"),
    )(w2, g2, m2, v2)
    return d.reshape(shape), nm.reshape(shape), nv.reshape(shape)


def _my_place():
    x, y, c = lax.axis_index("x"), lax.axis_index("y"), lax.axis_index("c")
    chips = [(1 - x, y), (x, 1 - y), (1 - x, 1 - y)]
    return x, y, c, chips


def _all_gather(arrs):
    n = len(arrs)

    def body(*refs):
        x_refs, out_refs = refs[:n], refs[n:2 * n]
        send_sems, recv_sems, local_sems = refs[2 * n:]
        x, y, c, chips = _my_place()
        me, sibling = (x, y, c), (x, y, 1 - c)

        def copy(m, k, block, to, src=None):
            rows = out_refs[m].at[4 * block[0] + 2 * block[1] + block[2]]
            return pltpu.make_async_remote_copy(
                src_ref=rows if src is None else src, dst_ref=rows,
                send_sem=send_sems.at[7 * m + k], recv_sem=recv_sems.at[7 * m + k], device_id=to, device_id_type=MESH)

        mine = [pltpu.make_async_copy(x_refs[m], out_refs[m].at[4 * x + 2 * y + c], local_sems.at[m])
                for m in range(n)]
        for cp in mine:
            cp.start()
        first = []
        for m in range(n):
            first.append(copy(m, 0, me, sibling, src=x_refs[m]))
            first += [copy(m, 1 + j, me, (*chip, c), src=x_refs[m]) for j, chip in enumerate(chips)]
        for cp in first:
            cp.start()
        passed = []
        for m in range(n):
            for j, chip in enumerate(chips):
                copy(m, 1 + j, (*chip, c), me).wait_recv()
                cp = copy(m, 4 + j, (*chip, c), sibling)
                cp.start()
                passed.append(cp)
        for m in range(n):
            copy(m, 0, sibling, me).wait_recv()
            for j, chip in enumerate(chips):
                copy(m, 4 + j, (*chip, 1 - c), me).wait_recv()
        for cp in first + passed:
            cp.wait_send()
        for cp in mine:
            cp.wait()

    hbm = pl.BlockSpec(memory_space=pl.ANY)
    return pl.pallas_call(
        body, name="all_gather", in_specs=[hbm] * n, out_specs=[hbm] * n,
        out_shape=[jax.ShapeDtypeStruct((N_DEV,) + a.shape, a.dtype) for a in arrs],
        scratch_shapes=[pltpu.SemaphoreType.DMA((7 * n,)), pltpu.SemaphoreType.DMA((7 * n,)),
                        pltpu.SemaphoreType.DMA((n,))],
    )(*arrs)


def _peer(x, y, c, k):
    return (x ^ ((k >> 2) & 1), y ^ ((k >> 1) & 1), c ^ (k & 1))


HBM_SPEC = pl.BlockSpec(memory_space=pltpu.HBM)
SEM_SPEC = pl.BlockSpec(memory_space=pltpu.SEMAPHORE)


def _exchange_refs(srcs, lands, m, k, x, y, c, scatter):
    peer = _peer(x, y, c, k)
    if scatter:
        return srcs[m].at[4 * peer[0] + 2 * peer[1] + peer[2]], lands[m].at[k - 1], peer
    return srcs[m], lands[m].at[4 * x + 2 * y + c], peer


def _exchange_start(arrs, land_shapes, scatter, name):
    n = len(arrs)

    def body(*refs):
        srcs, lands = refs[:n], refs[n:2 * n]
        send_sems, recv_sems = refs[2 * n], refs[2 * n + 1]
        token = refs[-1]
        x, y, c, _ = _my_place()
        for m in range(n):
            for k in range(1, N_DEV):
                src, dst, peer = _exchange_refs(srcs, lands, m, k, x, y, c, scatter)
                pltpu.make_async_remote_copy(
                    src_ref=src, dst_ref=dst, send_sem=send_sems.at[7 * m + k - 1],
                    recv_sem=recv_sems.at[7 * m + k - 1], device_id=peer, device_id_type=MESH).start()
        token[...] = jnp.zeros_like(token)

    zones = [lax.empty(s_, a.dtype) for s_, a in zip(land_shapes, arrs)]
    outs = pl.pallas_call(
        body, name=name,
        out_shape=(pltpu.SemaphoreType.DMA((7 * n,)), pltpu.SemaphoreType.DMA((7 * n,)),
                   *[pltpu.HBM(a.shape, a.dtype) for a in arrs], *[pltpu.HBM(z.shape, z.dtype) for z in zones],
                   jax.ShapeDtypeStruct((8, 128), F32)),
        in_specs=[HBM_SPEC] * (2 * n),
        out_specs=(SEM_SPEC, SEM_SPEC, *[HBM_SPEC] * (2 * n), pl.BlockSpec(memory_space=pltpu.VMEM)),
        input_output_aliases={m: 2 + m for m in range(2 * n)},
        compiler_params=pltpu.CompilerParams(has_side_effects=pltpu.SideEffectType.DATAFLOW_SIDE_EFFECTING),
    )(*[pltpu.with_memory_space_constraint(a, pltpu.HBM) for a in arrs],
      *[pltpu.with_memory_space_constraint(z, pltpu.HBM) for z in zones])
    return outs[0], outs[1], list(outs[2:2 + n]), list(outs[2 + n:2 + 2 * n]), outs[-1]


def _exchange_wait(send_sems, recv_sems, arrs, zones, after, scatter, name):
    n = len(arrs)
    afters = list(after) if isinstance(after, (list, tuple)) else [after]

    def body(*refs):
        srcs, lands = refs[:n], refs[n:2 * n]
        send_sems, recv_sems = refs[2 * n], refs[2 * n + 1]
        x, y, c, _ = _my_place()
        for m in range(n):
            for k in range(1, N_DEV):
                src, dst, peer = _exchange_refs(srcs, lands, m, k, x, y, c, scatter)
                cp = pltpu.make_async_remote_copy(
                    src_ref=src, dst_ref=dst, send_sem=send_sems.at[7 * m + k - 1],
                    recv_sem=recv_sems.at[7 * m + k - 1], device_id=peer, device_id_type=MESH)
                cp.wait_send()
                cp.wait_recv()

    outs = pl.pallas_call(
        body, name=name,
        out_shape=tuple(pltpu.HBM(a.shape, a.dtype) for a in list(arrs) + list(zones)),
        in_specs=[HBM_SPEC] * (2 * n) + [SEM_SPEC, SEM_SPEC] + [pl.BlockSpec(memory_space=pl.ANY)] * len(afters),
        out_specs=tuple([HBM_SPEC] * (2 * n)),
        input_output_aliases={m: m for m in range(2 * n)},
        compiler_params=pltpu.CompilerParams(has_side_effects=pltpu.SideEffectType.DATAFLOW_SIDE_EFFECTING),
    )(*arrs, *zones, send_sems, recv_sems, *afters)
    return list(outs[n:])


def _sum_parts(own, parts, tr, dep=None):
    R, W = own.shape

    def body(own_ref, parts_ref, out_ref):
        acc = own_ref[...].astype(F32)
        for k in range(N_DEV - 1):
            acc = acc + parts_ref[k].astype(F32)
        out_ref[...] = acc

    in_specs = [pl.BlockSpec((tr, W), lambda i: (i, 0)), pl.BlockSpec((N_DEV - 1, tr, W), lambda i: (0, i, 0))]
    body, in_specs, args = _with_dep(body, dep, in_specs, [own, parts])
    return pl.pallas_call(
        body, name="sum_parts", grid=(R // tr,),
        in_specs=in_specs,
        out_specs=pl.BlockSpec((tr, W), lambda i: (i, 0)),
        out_shape=jax.ShapeDtypeStruct((R, W), F32),
        compiler_params=_params("parallel"),
    )(*args)


def _all_reduce_small(v, dep=None):
    Rn, Wd = v.shape

    def body(v_ref, out_ref, gat_ref, send_sems, recv_sems):
        x, y, c, _ = _my_place()
        me = 4 * x + 2 * y + c
        gat_ref[me] = v_ref[...]
        copies = []
        for k in range(1, N_DEV):
            fx, fy, fc = (k >> 2) & 1, (k >> 1) & 1, k & 1
            peer = (x ^ fx, y ^ fy, c ^ fc)
            cp = pltpu.make_async_remote_copy(
                src_ref=v_ref, dst_ref=gat_ref.at[me], send_sem=send_sems.at[k - 1], recv_sem=recv_sems.at[k - 1],
                device_id=peer, device_id_type=MESH)
            cp.start()
            copies.append(cp)
        for cp in copies:
            cp.wait_recv()
        for cp in copies:
            cp.wait_send()
        acc = gat_ref[0]
        for k in range(1, N_DEV):
            acc = acc + gat_ref[k]
        out_ref[...] = acc

    vm = pl.BlockSpec(memory_space=pltpu.VMEM)
    body, in_specs, args = _with_dep(body, dep, [vm], [v])
    return pl.pallas_call(
        body, name="all_reduce_small", in_specs=in_specs, out_specs=vm,
        out_shape=jax.ShapeDtypeStruct((Rn, Wd), F32),
        scratch_shapes=[pltpu.VMEM((N_DEV, Rn, Wd), F32), pltpu.SemaphoreType.DMA((7,)),
                        pltpu.SemaphoreType.DMA((7,))],
    )(*args)


def _t5_bucket(rel):
    half = N_BUCKETS // 2
    max_exact = half // 2
    ret = jnp.where(rel > 0, half, 0)
    n = jnp.abs(rel)
    nf = jnp.maximum(n, 1).astype(F32)
    large = max_exact + (jnp.log(nf / max_exact) / math.log(MAX_DISTANCE / max_exact)
                         * (half - max_exact)).astype(jnp.int32)
    large = jnp.minimum(large, half - 1)
    return ret + jnp.where(n < max_exact, n, large)


def _band(R, d):
    W = BQ + 2 * R
    rel = jnp.arange(W)[None, :] - R - jnp.arange(BQ)[:, None]
    return _t5_bucket(rel * d), jnp.abs(rel) <= R


def _onehot(R, d):
    bkt, in_band = _band(R, d)
    return ((bkt.reshape(1, -1) == jnp.arange(128)[:, None]) & in_band.reshape(1, -1)).astype(BF16)


def _bias_expand(table_t, onehot):
    H = table_t.shape[0]
    K = onehot.shape[1]

    def body(t_ref, oh_ref, out_ref):
        oh = oh_ref[...]
        t = t_ref[...]
        hi = t.astype(BF16)
        r1 = t - hi.astype(F32)
        mid = r1.astype(BF16)
        low = (r1 - mid.astype(F32)).astype(BF16)
        marked = _dot(jnp.ones(t.shape, BF16), oh) > 0.5
        out_ref[...] = jnp.where(marked, _dot(hi, oh) + _dot(mid, oh) + _dot(low, oh), NEG)

    vm = pl.BlockSpec(memory_space=pltpu.VMEM)
    return pl.pallas_call(
        body, name="bias_expand", in_specs=[vm, vm], out_specs=vm,
        out_shape=jax.ShapeDtypeStruct((H, K), F32),
        compiler_params=pltpu.CompilerParams(vmem_limit_bytes=VMEM_LIMIT),
    )(table_t, onehot)


def _bias_matrix(table, R, d):
    table_t = jnp.pad(table.T, ((0, 0), (0, 128 - N_BUCKETS)))
    return _bias_expand(table_t, _onehot(R, d)).reshape(table.shape[1], BQ, BQ + 2 * R)


def _bias_variants(base, R):
    H, _, W = base.shape
    fill = jnp.full((H, BQ, R), NEG, F32)
    first = jnp.concatenate([base[:, :, R:], fill], axis=2)
    last = jnp.concatenate([fill, base[:, :, :W - R]], axis=2)
    v = jnp.stack([base, first, last], axis=1)
    v = v.reshape(H // 2, 2, 3, BQ, W).transpose(0, 2, 1, 3, 4).reshape(H // 2, 3, 2 * BQ, W)
    return v, v.transpose(0, 1, 3, 2)


def _bias_grad(dbt, R, d):
    P, _, W, _ = dbt.shape
    dbt = dbt[:, 0].at[:, R:].add(dbt[:, 1, :W - R]).at[:, :W - R].add(dbt[:, 2, R:])
    dbm = dbt.reshape(P, W, 2, BQ).transpose(0, 2, 3, 1).reshape(2 * P, BQ * W)
    return _bias_reduce(_onehot(R, d), dbm)[:, :N_BUCKETS].T


def _tile2(gain):
    return jnp.concatenate([gain, gain])


ROW_W_O, ROW_GATE, ROW_QKV, ROW_PROJ, B_ROWS = 768, 896, 1024, 1312, 1344
BLK_W_O, BLK_GATE = ROW_W_O // 128, ROW_GATE // 128


def _pack_layer(wts, i):
    a = jnp.stack([wts["ffn1_w_in"][i], wts["ffn2_w_in"][i]])
    D = a.shape[1]
    b = jnp.concatenate([
        wts["ffn1_w_out"][i], wts["ffn2_w_out"][i],
        jnp.zeros((ROW_W_O - 2 * wts["ffn1_w_out"].shape[1], D), a.dtype),
        wts["w_o"][i], wts["w_ple_gate"][i], wts["w_qkv"][i].reshape(-1, D), wts["w_ple_proj"][i].reshape(-1, D)])
    return a, b


def _unpack_layer(sums, like):
    w_in2, b1, b2, w_in1, w_out1 = sums
    n_out, n_sq = like["ffn1_w_out"].shape[1], like["w_o"].shape[1]
    out = {}
    if w_in2 is not None:
        out.update(ffn2_w_in=w_in2, ffn2_w_out=b1[:n_out], w_ple_gate=b1[n_out:n_out + n_sq],
                   w_ple_proj=b1[n_out + n_sq:].reshape(like["w_ple_proj"].shape[1:]))
    if b2 is not None:
        out.update(w_o=b2[:n_sq], w_qkv=b2[n_sq:].reshape(like["w_qkv"].shape[1:]))
    if w_in1 is not None:
        out.update(ffn1_w_in=w_in1, ffn1_w_out=w_out1)
    return out


def _col_sharded(gb, r0, r1, rows):
    return gb[:, r0:r1].reshape(N_DEV, rows, -1).transpose(1, 0, 2).reshape(rows, -1)


def _to_col_shards(g):
    rows = g.shape[0]
    return g.reshape(rows, N_DEV, -1).transpose(1, 0, 2).reshape(N_DEV, -1, 1024)


def _layer_weights(ga, gb, p_dim):
    return dict(ga=ga, gb=gb, w_qkv=_col_sharded(gb, ROW_QKV, ROW_PROJ, ga.shape[2]),
                w_proj=_col_sharded(gb, ROW_PROJ, B_ROWS, p_dim))


def _layer_fwd(x, p, w, sm, i, target, tm, biases, dep=None):
    ga, gb = w["ga"], w["gb"]
    saved = {}
    saved["x0"] = x
    x1, saved["h1"], saved["zg1"], saved["zu1"], saved["s1"] = _ffn_fwd(
        x, sm["norm_ffn1"][i][None], ga, gb, 0, 2 * tm, dep)
    saved["x1"] = x1
    qkv, saved["hm"] = _qkv_fwd(x1, sm["norm_mix"][i][None], w["w_qkv"], 2 * tm)
    saved["qkv"] = qkv
    gains2 = jnp.stack([_tile2(sm[k][i]) for k in ("q_norm_a", "k_norm_a", "q_norm_b", "k_norm_b")])
    saved["gains2"] = gains2
    qb, kb, vb, qkv_d = _attn_prep(qkv, gains2, tm)
    no_sink = jnp.full((8,), NEG, F32)
    branches = []
    outs = []
    for (R, d), bias, (qd, kd, vd) in zip(DILATED, biases[:3], qkv_d):
        sink = jnp.tile(no_sink, d)
        outs.append(_attn_fwd(qd, kd, vd, bias[0], sink, R, 1, d))
        branches.append((qd, kd, vd, bias, sink, R, d))
    bias_b = biases[3]
    sink_b = sm["sink_b"][i]
    ob, lb = _attn_fwd(qb, kb, vb, bias_b[0], sink_b, SWA_RADIUS, 2, 1)
    merged, o_cat = _attn_merge(outs, ob, tm)
    saved.update(branches=branches, b=(qb, kb, vb, bias_b, sink_b), merged=merged, ob=ob, lb=lb, o_cat=o_cat)
    x2 = _oproj_fwd(x1, o_cat, gb, BLK_W_O, 2 * tm)
    saved["x2"] = x2
    x3, saved["h2"], saved["zg2"], saved["zu2"], saved["s2"] = _ffn_fwd(
        x2, sm["norm_ffn2"][i][None], ga, gb, 1, 2 * tm)
    saved["x3"] = x3
    res = _ple_fwd(x3, sm["norm_ple"][i][None], gb, BLK_GATE, p, w["w_proj"], target, tm)
    y, saved["hp"], saved["gate"], saved["pp"], saved["pb"] = res[:5]
    loss = res[5] if target is not None else None
    return y, loss, saved


def _layer_bwd(dy, w, sm, i, sv, tm, dep=None, on_ready=None, on_small=None, on_last=None):
    ga, gb = w["ga"], w["gb"]
    gs = {}
    D = dy.shape[1]
    dgl, dpp = _ple_bwd(dy, sv["gate"], sv["pp"], tm, dep)
    d_gate = _matmul_tn(sv["hp"], dgl, D, 2 * tm)
    d_proj = _matmul_tn(sv["pb"], dpp, D, 2 * tm)
    dx3, gs["norm_ple"] = _dense_norm_bwd(dy, dgl, gb, BLK_GATE, sv["x3"], sm["norm_ple"][i][None], 2 * tm)
    dx2, dyb, dzg, dzu, gs["norm_ffn2"] = _ffn_bwd(dx3, sv["x2"], sm["norm_ffn2"][i][None], sv["zg2"], sv["zu2"],
                                                   ga, gb, 1, tm)
    dwin2, dwo2 = _ffn_dw(sv["h2"], dzg, dzu, sv["s2"], dyb, 2 * tm)
    half = dwo2.shape[1] // 2
    after_ffn2 = [dwin2, jnp.concatenate([dwo2.reshape(N_DEV, half, D), d_gate.reshape(N_DEV, -1, D),
                                          _to_col_shards(d_proj)], axis=1)]
    token = None if on_ready is None else on_ready(0, after_ffn2)
    dx2b, do_b, do_a = _oproj_bwd(dx2, gb, BLK_W_O, tm, token)
    d_wo = _matmul_tn(sv["o_cat"], dx2b, D, 2 * tm)
    dqa, dka, dva, dbias = [], [], [], []
    for (qd, kd, vd, bias, sink, R, d), (oa, la), do_d in zip(sv["branches"], sv["merged"], do_a):
        dq, dk, dv, dbm, _ = _attn_bwd(qd, kd, vd, bias[1], sink, oa, la, do_d, R, 1, d)
        dqa.append(dq)
        dka.append(dk)
        dva.append(dv)
        dbias.append(dbm)
    qb, kb, vb, bias_b, sink_b = sv["b"]
    dqb, dkb, dvb, dbm_b, dsink = _attn_bwd(qb, kb, vb, bias_b[1], sink_b, sv["ob"], sv["lb"], do_b,
                                            SWA_RADIUS, 2, 1)
    gs["rel_bias"] = dbias + [dbm_b]
    gs["sink_b"] = jnp.sum(dsink[:, 0].reshape(-1, 2, BQ), axis=2).reshape(-1)
    dqkv, dgains2 = _attn_post(sv["qkv"], sv["gains2"], dqa, dka, dva, dqb,
                               dkb, dvb, tm // 2)
    dgains = dgains2[:, :HEAD_DIM] + dgains2[:, HEAD_DIM:]
    for k, name in enumerate(("q_norm_a", "k_norm_a", "q_norm_b", "k_norm_b")):
        gs[name] = dgains[k]
    d_qkv = _matmul_tn(sv["hm"], dqkv, dqkv.shape[1] // 2, 2 * tm)
    after_mixer = [jnp.concatenate([d_wo.reshape(N_DEV, -1, D), _to_col_shards(d_qkv)], axis=1)]
    token = None if on_ready is None else on_ready(1, after_mixer)
    dx1, gs["norm_mix"] = _dense_norm_bwd(dx2, dqkv, w["w_qkv"], None, sv["x1"], sm["norm_mix"][i][None], 2 * tm)
    g1 = sm["norm_ffn1"][i][None]
    if on_last is None:
        dx0, dyb, dzg, dzu, gs["norm_ffn1"] = _ffn_bwd(dx1, sv["x0"], g1, sv["zg1"], sv["zu1"], ga, gb, 0, tm, token)
        dwin1, dwo1 = _ffn_dw(sv["h1"], dzg, dzu, sv["s1"], dyb, 2 * tm)
        return dx0, (after_ffn2, after_mixer, [dwin1, dwo1.reshape(N_DEV, half, D)]), gs
    dyb, dzg, dzu = _ffn_bwd_dz(dx1, sv["zg1"], sv["zu1"], gb, 0, 2 * tm, token)
    dwin1, dwo1 = _ffn_dw(sv["h1"], dzg, dzu, sv["s1"], dyb, 2 * tm, on_small(gs))
    last = [dwin1, dwo1.reshape(N_DEV, half, D)]
    dx0, gs["norm_ffn1"] = _ffn_bwd_dx(dx1, sv["x0"], g1, dzg, dzu, ga, 0, 2 * tm, on_last(last))
    return dx0, (after_ffn2, after_mixer, last), gs


def _bias_matrices(rel_bias):
    biases = [_bias_variants(_bias_matrix(rel_bias[:, :8], R, d), R) for R, d in DILATED]
    biases.append(_bias_variants(_bias_matrix(rel_bias[:, 8:], SWA_RADIUS, 1), SWA_RADIUS))
    return biases


def _stack_small(per_layer):
    small = {}
    for k, v in per_layer.items():
        if k == "rel_bias":
            per_branch = [sum(parts) for parts in zip(*v.values())]
            drel_a = sum(_bias_grad(t, R, d) for t, (R, d) in zip(per_branch[:3], DILATED))
            small[k] = jnp.concatenate([drel_a, _bias_grad(per_branch[3], SWA_RADIUS, 1)], axis=1)
        else:
            small[k] = jnp.stack([v[i].reshape(-1) for i in sorted(v)])
    return small


TM = 512
SUM_TILES = (512, 512, 416, 512, 352)
LAST_GROUP = ("ffn1_w_in", "ffn1_w_out")


def _pack_small(d, extra=None):
    parts = [d[k].reshape(-1) for k in SMALL]
    if extra is not None:
        parts.append(extra.reshape(-1))
    flat = jnp.concatenate(parts)
    return jnp.pad(flat, (0, SMALL_ROWS * 128 - flat.shape[0])).reshape(SMALL_ROWS, 128)


def _unpack_small(buf, like):
    flat = buf.reshape(-1)
    out, off = {}, 0
    for k in SMALL:
        n = like[k].size
        out[k] = flat[off:off + n].reshape(like[k].shape)
        off += n
    return out, flat[off]


def kernel(x, p, rel_bias, norm_ffn1, ffn1_w_in, ffn1_w_out, norm_mix, w_qkv, q_norm_a, k_norm_a, q_norm_b, k_norm_b, sink_b, w_o, norm_ffn2, ffn2_w_in, ffn2_w_out, norm_ple, w_ple_gate, w_ple_proj, loss_target, m_rel_bias, m_norm_ffn1, m_ffn1_w_in, m_ffn1_w_out, m_norm_mix, m_w_qkv, m_q_norm_a, m_k_norm_a, m_q_norm_b, m_k_norm_b, m_sink_b, m_w_o, m_norm_ffn2, m_ffn2_w_in, m_ffn2_w_out, m_norm_ple, m_w_ple_gate, m_w_ple_proj, v_rel_bias, v_norm_ffn1, v_ffn1_w_in, v_ffn1_w_out, v_norm_mix, v_w_qkv, v_q_norm_a, v_k_norm_a, v_q_norm_b, v_k_norm_b, v_sink_b, v_w_o, v_norm_ffn2, v_ffn2_w_in, v_ffn2_w_out, v_norm_ple, v_w_ple_gate, v_w_ple_proj):
    wts = dict(rel_bias=rel_bias, norm_ffn1=norm_ffn1, ffn1_w_in=ffn1_w_in, ffn1_w_out=ffn1_w_out,
               norm_mix=norm_mix, w_qkv=w_qkv, q_norm_a=q_norm_a, k_norm_a=k_norm_a, q_norm_b=q_norm_b,
               k_norm_b=k_norm_b, sink_b=sink_b, w_o=w_o, norm_ffn2=norm_ffn2, ffn2_w_in=ffn2_w_in,
               ffn2_w_out=ffn2_w_out, norm_ple=norm_ple, w_ple_gate=w_ple_gate, w_ple_proj=w_ple_proj)
    mom = dict(rel_bias=m_rel_bias, norm_ffn1=m_norm_ffn1, ffn1_w_in=m_ffn1_w_in, ffn1_w_out=m_ffn1_w_out,
               norm_mix=m_norm_mix, w_qkv=m_w_qkv, q_norm_a=m_q_norm_a, k_norm_a=m_k_norm_a, q_norm_b=m_q_norm_b,
               k_norm_b=m_k_norm_b, sink_b=m_sink_b, w_o=m_w_o, norm_ffn2=m_norm_ffn2, ffn2_w_in=m_ffn2_w_in,
               ffn2_w_out=m_ffn2_w_out, norm_ple=m_norm_ple, w_ple_gate=m_w_ple_gate, w_ple_proj=m_w_ple_proj)
    var = dict(rel_bias=v_rel_bias, norm_ffn1=v_norm_ffn1, ffn1_w_in=v_ffn1_w_in, ffn1_w_out=v_ffn1_w_out,
               norm_mix=v_norm_mix, w_qkv=v_w_qkv, q_norm_a=v_q_norm_a, k_norm_a=v_k_norm_a, q_norm_b=v_q_norm_b,
               k_norm_b=v_k_norm_b, sink_b=v_sink_b, w_o=v_w_o, norm_ffn2=v_norm_ffn2, ffn2_w_in=v_ffn2_w_in,
               ffn2_w_out=v_ffn2_w_out, norm_ple=v_norm_ple, w_ple_gate=v_w_ple_gate, w_ple_proj=v_w_ple_proj)
    sm = {k: wts[k] for k in SMALL}
    p_dim = p.shape[-1]
    me = 4 * lax.axis_index("x") + 2 * lax.axis_index("y") + lax.axis_index("c")
    packed = []
    for i in range(2):
        a, b = _pack_layer(wts, i)
        packed.append([a.reshape(-1, a.shape[-1]).astype(BF16), b.astype(BF16)])
    a_shape = (2, ffn1_w_in.shape[1], ffn1_w_in.shape[2])

    def weights_of(zones):
        return _layer_weights(zones[0].reshape((N_DEV,) + a_shape), zones[1], p_dim)

    w0 = weights_of(_all_gather(packed[0]))
    zone_shapes = [(N_DEV,) + t.shape for t in packed[1]]
    ssem, rsem, thru, zones, token = _exchange_start(packed[1], zone_shapes, False, "gather_start")
    biases = _bias_matrices(rel_bias)
    x1, _, sv0 = _layer_fwd(x[0], p[0, 0], w0, sm, 0, None, TM, biases, dep=token)
    zones = _exchange_wait(ssem, rsem, thru, zones, x1, False, "gather_wait")
    w1 = weights_of([lax.dynamic_update_index_in_dim(z, t, me, 0) for z, t in zip(zones, packed[1])])
    dy, loss, sv1 = _layer_fwd(x1, p[1, 0], w1, sm, 1, loss_target[0], TM, biases)

    def slots_for(arrs):
        return [(N_DEV - 1,) + t.shape[1:] for t in arrs]

    held1, held = {}, {}

    def on_ready1(stage, group):
        held1[stage] = _exchange_start(group, slots_for(group), True, f"scatter1_start_{stage}")
        return held1[stage][4]

    dx1, groups1, gs1 = _layer_bwd(dy, w1, sm, 1, sv1, TM, on_ready=on_ready1)
    on_ready1(2, groups1[2])
    g1 = groups1[0] + groups1[1] + groups1[2]

    def on_ready(stage, group):
        if stage == 1:
            held["slots1"] = [t for st in (0, 1, 2)
                              for t in _exchange_wait(*held1[st][:4], group[0], True, f"scatter1_wait_{st}")]
        held[stage] = _exchange_start(group, slots_for(group), True, f"scatter_start_{stage}")
        return held[stage][4]

    def on_small(gs0):
        part = dict(gs0, norm_ffn1=jnp.zeros_like(gs1["norm_ffn1"]))
        gsmall = _stack_small({k: {0: part[k], 1: gs1[k]} for k in part})
        held["small"] = _all_reduce_small(_pack_small(gsmall, loss[0, :1]))
        return held["small"]

    def on_last(group):
        held["last"] = _exchange_start(group, slots_for(group), True, "scatter_start_2")
        return held["last"][4]

    dx, groups0, gs0 = _layer_bwd(dx1, w0, sm, 0, sv0, TM, dep=held1[2][4], on_ready=on_ready, on_small=on_small,
                                  on_last=on_last)
    last = groups0[2]
    slots0 = [_exchange_wait(*held[stage][:4], last[0], True, f"scatter_wait_{stage}") for stage in (0, 1)]

    def summed(arrs, slots, tiles, dep=None):
        return [_sum_parts(lax.dynamic_index_in_dim(t, me, 0, keepdims=False), s_, tr, dep)
                for t, s_, tr in zip(arrs, slots, tiles)]

    cover = held["last"][4]
    r1 = summed(g1, held["slots1"], SUM_TILES, cover)
    r0 = summed(groups0[0], slots0[0], SUM_TILES[:2], cover) + summed(groups0[1], slots0[1], SUM_TILES[2:3], cover)

    def update(names, layers):
        for k in names:
            grads[k] = jnp.stack([layers[0][k], layers[1][k]])
            delta[k], new_m[k], new_v[k] = _adamw(wts[k], grads[k], mom[k], var[k])

    grads, delta, new_m, new_v = {}, {}, {}, {}
    layer1 = _unpack_layer(r1, wts)
    update([k for k in BIG if k not in LAST_GROUP], [_unpack_layer(r0 + [None, None], wts), layer1])

    cover_done = [dx] + [delta[k] for k in BIG if k not in LAST_GROUP]
    slots_last = _exchange_wait(*held["last"][:4], cover_done, True, "scatter_wait_2")
    update(LAST_GROUP, [_unpack_layer([None, None, None] + summed(last, slots_last, SUM_TILES[3:]), wts), layer1])
    late = _all_reduce_small(gs0["norm_ffn1"].reshape(-1, 128), dep=slots_last[0])
    small_sum, loss_sum = _unpack_small(held["small"], sm)
    small_sum["norm_ffn1"] = small_sum["norm_ffn1"].at[0].add(late.reshape(-1))
    grads.update(small_sum)
    zeros = {k: jnp.zeros_like(wts[k]) for k in SMALL}
    ds, ms, vs = _adamw(_pack_small(wts), _pack_small(small_sum), _pack_small(mom), _pack_small(var))
    for packed, dst in ((ds, delta), (ms, new_m), (vs, new_v)):
        dst.update(_unpack_small(packed, zeros)[0])

    return (loss_sum, dx[None], *[grads[k] for k in WEIGHTS], *[delta[k] for k in WEIGHTS],
            *[new_m[k] for k in WEIGHTS], *[new_v[k] for k in WEIGHTS])
```

```python
import functools
import math

import jax
import jax.numpy as jnp
from jax import lax
from jax.experimental import pallas as pl
from jax.experimental.pallas import tpu as pltpu

F32 = jnp.float32
BF16 = jnp.bfloat16

N_DEV = 8
HEAD_DIM = 64
PAIR = 2 * HEAD_DIM
BQ = 128
N_BUCKETS = 32
MAX_DISTANCE = 1024
DILATED = ((64, 1), (64, 4), (64, 16))
SWA_RADIUS = 128
EPS = 1e-6
NEG = -1e30
ADAM_LR, ADAM_B1, ADAM_B2, ADAM_EPS, ADAM_WD, ADAM_STEP = 0.001, 0.9, 0.999, 1e-08, 0.01, 10
VMEM_LIMIT = 56 * 1024 * 1024
AXES = ("x", "y", "c")
MESH = pl.DeviceIdType.MESH

BIG = ("ffn1_w_in", "ffn1_w_out", "w_qkv", "w_o", "ffn2_w_in", "ffn2_w_out", "w_ple_gate", "w_ple_proj")
SMALL = ("rel_bias", "norm_ffn1", "norm_mix", "q_norm_a", "k_norm_a", "q_norm_b", "k_norm_b", "sink_b",
         "norm_ffn2", "norm_ple")
WEIGHTS = ("rel_bias", "norm_ffn1", "ffn1_w_in", "ffn1_w_out", "norm_mix", "w_qkv", "q_norm_a", "k_norm_a",
           "q_norm_b", "k_norm_b", "sink_b", "w_o", "norm_ffn2", "ffn2_w_in", "ffn2_w_out", "norm_ple",
           "w_ple_gate", "w_ple_proj")
SMALL_ROWS = 96


def _params(*sem):
    return pltpu.CompilerParams(dimension_semantics=sem, vmem_limit_bytes=VMEM_LIMIT)


def _dot(a, b):
    return jnp.dot(a, b, preferred_element_type=F32)


def _dot_nt(a, b):
    return lax.dot_general(a, b, (((1,), (1,)), ((), ())), preferred_element_type=F32)


def _dot_tn(a, b):
    return lax.dot_general(a, b, (((0,), (0,)), ((), ())), preferred_element_type=F32)


def _sigmoid(x):
    return 1.0 / (1.0 + jnp.exp(-x))


def _rstd(xv):
    return lax.rsqrt(jnp.mean(xv * xv, axis=-1, keepdims=True) + EPS)


def _norm_bwd(dh, xv, gv):
    r = _rstd(xv)
    xn = xv * r
    dg = jnp.sum(dh * xn, axis=0, keepdims=True)
    dxn = dh * gv
    dx = r * (dxn - xn * jnp.mean(dxn * xn, axis=-1, keepdims=True))
    return dx, dg


def _lo_mask(shape):
    return lax.broadcasted_iota(jnp.int32, shape, len(shape) - 1) < HEAD_DIM


def _half_sum(t, lo):
    s0 = jnp.sum(jnp.where(lo, t, 0.0), axis=1, keepdims=True)
    s1 = jnp.sum(jnp.where(lo, 0.0, t), axis=1, keepdims=True)
    return jnp.where(lo, s0, s1)


FFN_PARTS = 2


def _ffn_weight_specs(f, nj, D, C):
    return [pl.BlockSpec((None, None, D, C), lambda i, j: (j, f, 0, 0)),
            pl.BlockSpec((None, None, D, C), lambda i, j: (j + nj, f, 0, 0)),
            pl.BlockSpec((2, C // 2, D), lambda i, j: (j, f, 0))]


def _with_dep(body, dep, in_specs, args):
    if dep is None:
        return body, in_specs, args

    def body_after(dep_ref, *refs):
        body(*refs)

    return body_after, [pl.BlockSpec(memory_space=pl.ANY)] + in_specs, [dep] + args


def _ffn_fwd(x, g, ga, gb, f, tm, dep=None):
    T, D = x.shape
    nj, C = ga.shape[0] // 2, ga.shape[3]

    def body(x_ref, g_ref, wg_ref, wu_ref, wo_ref, xo_ref, h_ref, zg_ref, zu_ref, s_ref, h_scr, acc):
        j = pl.program_id(1)

        @pl.when(j == 0)
        def _():
            xv = x_ref[...]
            hb = (xv * _rstd(xv) * g_ref[...]).astype(BF16)
            h_scr[...] = hb
            h_ref[...] = hb
            acc[...] = jnp.zeros_like(acc)

        wo = wo_ref[...].reshape(C, D)
        for part in range(FFN_PARTS):
            sl = pl.ds(part * (tm // FFN_PARTS), tm // FFN_PARTS)
            hb = h_scr[sl, :]
            gt = _dot(hb, wg_ref[...])
            up = _dot(hb, wu_ref[...])
            s = (gt * _sigmoid(gt) * up).astype(BF16)
            zg_ref[sl, :] = gt.astype(BF16)
            zu_ref[sl, :] = up.astype(BF16)
            s_ref[sl, :] = s
            acc[sl, :] += _dot(s, wo)

        @pl.when(j == nj - 1)
        def _():
            xo_ref[...] = x_ref[...] + 0.5 * acc[...]

    tok = pl.BlockSpec((tm, D), lambda i, j: (i, 0))
    chunk = pl.BlockSpec((None, tm, C), lambda i, j: (j, i, 0))
    in_specs = [tok, pl.BlockSpec((1, D), lambda i, j: (0, 0))] + _ffn_weight_specs(f, nj, D, C)
    body, in_specs, args = _with_dep(body, dep, in_specs, [x, g, ga, ga, gb])
    return pl.pallas_call(
        body, name="ffn_fwd", grid=(T // tm, nj),
        in_specs=in_specs,
        out_specs=[tok, tok, chunk, chunk, chunk],
        out_shape=[jax.ShapeDtypeStruct((T, D), F32), jax.ShapeDtypeStruct((T, D), BF16),
                   jax.ShapeDtypeStruct((nj, T, C), BF16), jax.ShapeDtypeStruct((nj, T, C), BF16),
                   jax.ShapeDtypeStruct((nj, T, C), BF16)],
        scratch_shapes=[pltpu.VMEM((tm, D), BF16), pltpu.VMEM((tm, D), F32)],
        compiler_params=_params("parallel", "arbitrary"),
    )(*args)


def _ffn_bwd(dxo, x, g, zg, zu, ga, gb, f, tm, dep=None):
    T, D = x.shape
    nj, C = ga.shape[0] // 2, ga.shape[3]

    def body(dxo_ref, x_ref, g_ref, zg_ref, zu_ref, wg_ref, wu_ref, wo_ref,
             dx_ref, dy_ref, dzg_ref, dzu_ref, dgn_ref, dy_scr, acc):
        i, j = pl.program_id(0), pl.program_id(1)

        @pl.when(j == 0)
        def _():
            dyb = (0.5 * dxo_ref[...]).astype(BF16)
            dy_scr[...] = dyb
            dy_ref[...] = dyb
            acc[...] = jnp.zeros_like(acc)

        wo = wo_ref[...].reshape(C, D)
        for part in range(FFN_PARTS):
            sl = pl.ds(part * (tm // FFN_PARTS), tm // FFN_PARTS)
            ds = _dot_nt(dy_scr[sl, :], wo)
            gt = zg_ref[sl, :].astype(F32)
            up = zu_ref[sl, :].astype(F32)
            sg = _sigmoid(gt)
            dgt = (ds * up * (sg * (1.0 + gt * (1.0 - sg)))).astype(BF16)
            dup = (ds * (gt * sg)).astype(BF16)
            dzg_ref[sl, :] = dgt
            dzu_ref[sl, :] = dup
            acc[sl, :] += _dot_nt(dgt, wg_ref[...]) + _dot_nt(dup, wu_ref[...])

        @pl.when(j == nj - 1)
        def _():
            dx, dg = _norm_bwd(acc[...], x_ref[...], g_ref[...])
            dx_ref[...] = dxo_ref[...] + dx

            @pl.when(i == 0)
            def _():
                dgn_ref[...] = dg

            @pl.when(i > 0)
            def _():
                dgn_ref[...] += dg

    tok = pl.BlockSpec((tm, D), lambda i, j: (i, 0))
    chunk = pl.BlockSpec((None, tm, C), lambda i, j: (j, i, 0))
    row = pl.BlockSpec((1, D), lambda i, j: (0, 0))
    in_specs = [tok, tok, row, chunk, chunk] + _ffn_weight_specs(f, nj, D, C)
    body, in_specs, args = _with_dep(body, dep, in_specs, [dxo, x, g, zg, zu, ga, ga, gb])
    return pl.pallas_call(
        body, name="ffn_bwd", grid=(T // tm, nj),
        in_specs=in_specs,
        out_specs=[tok, tok, chunk, chunk, row],
        out_shape=[jax.ShapeDtypeStruct((T, D), F32), jax.ShapeDtypeStruct((T, D), BF16),
                   jax.ShapeDtypeStruct((nj, T, C), BF16), jax.ShapeDtypeStruct((nj, T, C), BF16),
                   jax.ShapeDtypeStruct((1, D), F32)],
        scratch_shapes=[pltpu.VMEM((tm, D), BF16), pltpu.VMEM((tm, D), F32)],
        compiler_params=_params("arbitrary", "arbitrary"),
    )(*args)


def _ffn_bwd_dz(dxo, zg, zu, gb, f, tm, dep=None):
    T, D = dxo.shape
    nj, C = zg.shape[0], zg.shape[2]

    def body(dxo_ref, zg_ref, zu_ref, wo_ref, dy_ref, dzg_ref, dzu_ref, dy_scr):
        @pl.when(pl.program_id(1) == 0)
        def _():
            dyb = (0.5 * dxo_ref[...]).astype(BF16)
            dy_scr[...] = dyb
            dy_ref[...] = dyb

        wo = wo_ref[...].reshape(C, D)
        for part in range(FFN_PARTS):
            sl = pl.ds(part * (tm // FFN_PARTS), tm // FFN_PARTS)
            ds = _dot_nt(dy_scr[sl, :], wo)
            gt = zg_ref[sl, :].astype(F32)
            up = zu_ref[sl, :].astype(F32)
            sg = _sigmoid(gt)
            dzg_ref[sl, :] = (ds * up * (sg * (1.0 + gt * (1.0 - sg)))).astype(BF16)
            dzu_ref[sl, :] = (ds * (gt * sg)).astype(BF16)

    tok = pl.BlockSpec((tm, D), lambda i, j: (i, 0))
    chunk = pl.BlockSpec((None, tm, C), lambda i, j: (j, i, 0))
    in_specs = [tok, chunk, chunk, _ffn_weight_specs(f, nj, D, C)[2]]
    body, in_specs, args = _with_dep(body, dep, in_specs, [dxo, zg, zu, gb])
    return pl.pallas_call(
        body, name="ffn_bwd_dz", grid=(T // tm, nj),
        in_specs=in_specs, out_specs=[tok, chunk, chunk],
        out_shape=[jax.ShapeDtypeStruct((T, D), BF16), jax.ShapeDtypeStruct((nj, T, C), BF16),
                   jax.ShapeDtypeStruct((nj, T, C), BF16)],
        scratch_shapes=[pltpu.VMEM((tm, D), BF16)],
        compiler_params=_params("parallel", "arbitrary"),
    )(*args)


def _ffn_bwd_dx(dxo, x, g, dzg, dzu, ga, f, tm, dep=None):
    T, D = x.shape
    nj, C = ga.shape[0] // 2, ga.shape[3]

    def body(dxo_ref, x_ref, g_ref, dzg_ref, dzu_ref, wg_ref, wu_ref, dx_ref, dgn_ref, acc):
        i, j = pl.program_id(0), pl.program_id(1)

        @pl.when(j == 0)
        def _():
            acc[...] = jnp.zeros_like(acc)

        acc[...] += _dot_nt(dzg_ref[...], wg_ref[...]) + _dot_nt(dzu_ref[...], wu_ref[...])

        @pl.when(j == nj - 1)
        def _():
            dx, dg = _norm_bwd(acc[...], x_ref[...], g_ref[...])
            dx_ref[...] = dxo_ref[...] + dx

            @pl.when(i == 0)
            def _():
                dgn_ref[...] = dg

            @pl.when(i > 0)
            def _():
                dgn_ref[...] += dg

    tok = pl.BlockSpec((tm, D), lambda i, j: (i, 0))
    chunk = pl.BlockSpec((None, tm, C), lambda i, j: (j, i, 0))
    row = pl.BlockSpec((1, D), lambda i, j: (0, 0))
    in_specs = [tok, tok, row, chunk, chunk] + _ffn_weight_specs(f, nj, D, C)[:2]
    body, in_specs, args = _with_dep(body, dep, in_specs, [dxo, x, g, dzg, dzu, ga, ga])
    return pl.pallas_call(
        body, name="ffn_bwd_dx", grid=(T // tm, nj),
        in_specs=in_specs, out_specs=[tok, row],
        out_shape=[jax.ShapeDtypeStruct((T, D), F32), jax.ShapeDtypeStruct((1, D), F32)],
        scratch_shapes=[pltpu.VMEM((tm, D), F32)],
        compiler_params=_params("arbitrary", "arbitrary"),
    )(*args)


def _ffn_dw(h, dzg, dzu, s, dy, tk, dep=None):
    T, D = h.shape
    nj, C = s.shape[0], s.shape[2]
    nk = T // tk

    def body(h_ref, dzg_ref, dzu_ref, s_ref, dy_ref, dwin_ref, dwo_ref, ag, au, ao):
        k = pl.program_id(1)

        @pl.when(k == 0)
        def _():
            ag[...] = jnp.zeros_like(ag)
            au[...] = jnp.zeros_like(au)
            ao[...] = jnp.zeros_like(ao)

        hb = h_ref[...]
        ag[...] += _dot_tn(hb, dzg_ref[...])
        au[...] += _dot_tn(hb, dzu_ref[...])
        ao[...] += _dot_tn(s_ref[...], dy_ref[...])

        @pl.when(k == nk - 1)
        def _():
            dwin_ref[0] = ag[...].astype(BF16)
            dwin_ref[1] = au[...].astype(BF16)
            dwo_ref[...] = ao[...].astype(BF16)

    tok = pl.BlockSpec((tk, D), lambda j, k: (k, 0))
    chunk = pl.BlockSpec((None, tk, C), lambda j, k: (j, k, 0))
    body, in_specs, args = _with_dep(body, dep, [tok, chunk, chunk, chunk, tok], [h, dzg, dzu, s, dy])
    dwin, dwo = pl.pallas_call(
        body, name="ffn_dw", grid=(nj, nk),
        in_specs=in_specs,
        out_specs=[pl.BlockSpec((2, None, D, C), lambda j, k: (0, j, 0, 0)),
                   pl.BlockSpec((None, C, D), lambda j, k: (j, 0, 0))],
        out_shape=[jax.ShapeDtypeStruct((2, nj, D, C), BF16), jax.ShapeDtypeStruct((nj, C, D), BF16)],
        scratch_shapes=[pltpu.VMEM((D, C), F32), pltpu.VMEM((D, C), F32), pltpu.VMEM((C, D), F32)],
        compiler_params=_params("parallel", "arbitrary"),
    )(*args)
    return dwin.reshape(2 * nj, D, C), dwo


def _matmul_tn(a, b, tn, tk):
    T, Ka = a.shape
    N = b.shape[1]
    nk = T // tk

    def body(a_ref, b_ref, o_ref, acc):
        k = pl.program_id(1)

        @pl.when(k == 0)
        def _():
            acc[...] = jnp.zeros_like(acc)

        acc[...] += _dot_tn(a_ref[...], b_ref[...])

        @pl.when(k == nk - 1)
        def _():
            o_ref[...] = acc[...].astype(BF16)

    return pl.pallas_call(
        body, name="matmul_tn", grid=(N // tn, nk),
        in_specs=[pl.BlockSpec((tk, Ka), lambda n, k: (k, 0)), pl.BlockSpec((tk, tn), lambda n, k: (k, n))],
        out_specs=pl.BlockSpec((Ka, tn), lambda n, k: (0, n)),
        out_shape=jax.ShapeDtypeStruct((Ka, N), BF16),
        scratch_shapes=[pltpu.VMEM((Ka, tn), F32)],
        compiler_params=_params("parallel", "arbitrary"),
    )(a, b)


def _qkv_fwd(x, g, w, tm):
    T, D = x.shape
    N = w.shape[1]

    def body(x_ref, g_ref, w_ref, o_ref, h_ref):
        xv = x_ref[...]
        hb = (xv * _rstd(xv) * g_ref[...]).astype(BF16)
        h_ref[...] = hb
        o_ref[...] = _dot(hb, w_ref[...])

    return pl.pallas_call(
        body, name="qkv_fwd", grid=(T // tm,),
        in_specs=[pl.BlockSpec((tm, D), lambda i: (i, 0)), pl.BlockSpec((1, D), lambda i: (0, 0)),
                  pl.BlockSpec((D, N), lambda i: (0, 0))],
        out_specs=[pl.BlockSpec((tm, N), lambda i: (i, 0)), pl.BlockSpec((tm, D), lambda i: (i, 0))],
        out_shape=[jax.ShapeDtypeStruct((T, N), F32), jax.ShapeDtypeStruct((T, D), BF16)],
        compiler_params=_params("parallel"),
    )(x, g, w)


DILS = tuple(d for _, d in DILATED)


def _spread_specs(tm, T, dtype):
    specs = [pl.BlockSpec((4, d, tm // d, PAIR), lambda i: (0, 0, i, 0)) for d in DILS]
    shapes = [jax.ShapeDtypeStruct((4, d, T // d, PAIR), dtype) for d in DILS]
    return specs, shapes


def _spread(tile, y, outs, c, dtype):
    tm = y.shape[0]
    tile[...] = y
    for out, d in zip(outs, DILS):
        for r in range(d):
            out[c, r] = tile[pl.ds(r, tm // d, stride=d), :].astype(dtype)


def _collect(tile, ins, c):
    tm = tile.shape[0]
    first = True
    for ref, d in zip(ins, DILS):
        for r in range(d):
            rows = pl.ds(r, tm // d, stride=d) if d > 1 else pl.ds(0, tm)
            part = ref[c, r].astype(F32)
            tile[rows, :] = part if first else tile[rows, :] + part
        first = False
    return tile[...]


def _attn_prep(qkv, gains2, tm):
    T = qkv.shape[0]
    scale = HEAD_DIM ** -0.5
    n = len(DILS)

    def body(qkv_ref, g_ref, qb_ref, kb_ref, vb_ref, *rest):
        outs, tile = rest[:-1], rest[-1]
        lo = _lo_mask((tm, PAIR))

        def spread(kind, c, y):
            _spread(tile, y, outs[kind * n:(kind + 1) * n], c, BF16)

        def normed(c, gi, mult):
            xv = qkv_ref[:, c * PAIR:(c + 1) * PAIR]
            r = lax.rsqrt(_half_sum(xv * xv, lo) * (1.0 / HEAD_DIM) + EPS)
            y = xv * r * g_ref[gi:gi + 1, :]
            return y * mult if mult != 1.0 else y

        def both_halves(v):
            sw = pltpu.roll(v, HEAD_DIM, 1)
            return jnp.where(lo, v, sw), jnp.where(lo, sw, v)

        for c in range(4):
            spread(0, c, normed(c, 0, scale))
            spread(1, c, normed(4 + c, 1, 1.0))
            spread(2, c, qkv_ref[:, (8 + c) * PAIR:(9 + c) * PAIR])
            qb_ref[c] = normed(12 + c, 2, scale).astype(BF16)
        k0, k1 = both_halves(normed(16, 3, 1.0))
        kb_ref[0] = k0.astype(BF16)
        kb_ref[1] = k1.astype(BF16)
        v0, v1 = both_halves(qkv_ref[:, 17 * PAIR:18 * PAIR])
        vb_ref[0] = v0.astype(BF16)
        vb_ref[1] = v1.astype(BF16)

    four = pl.BlockSpec((4, tm, PAIR), lambda i: (0, i, 0))
    two = pl.BlockSpec((2, tm, PAIR), lambda i: (0, i, 0))
    s4 = jax.ShapeDtypeStruct((4, T, PAIR), BF16)
    s2 = jax.ShapeDtypeStruct((2, T, PAIR), BF16)
    specs, shapes = _spread_specs(tm, T, BF16)
    res = pl.pallas_call(
        body, name="attn_prep", grid=(T // tm,),
        in_specs=[pl.BlockSpec((tm, qkv.shape[1]), lambda i: (i, 0)), pl.BlockSpec((4, PAIR), lambda i: (0, 0))],
        out_specs=[four, two, two] + specs * 3,
        out_shape=[s4, s2, s2] + shapes * 3,
        scratch_shapes=[pltpu.VMEM((tm, PAIR), F32)],
        compiler_params=_params("parallel"),
    )(qkv, gains2)
    qb, kb, vb = res[:3]
    per_d = [tuple(res[3 + kind * n + di].reshape(4 * d, T // d, PAIR) for kind in range(3))
             for di, d in enumerate(DILS)]
    return qb, kb, vb, per_d


def _loop_blocks(nb, body, init, per_iter):
    u = math.gcd(nb, per_iter)

    def outer(i, carry):
        for k in range(u):
            carry = body(i * u + k, carry)
        return carry

    return lax.fori_loop(0, nb // u, outer, init)


def _key_window(b, nb, L, R, W):
    start = pl.multiple_of(jnp.clip(b * BQ - R, 0, L - W), HEAD_DIM)
    return start, jnp.where(b == 0, 1, jnp.where(b == nb - 1, 2, 0))


def _stack_heads(v, lo):
    z = jnp.zeros_like(v)
    return jnp.concatenate([jnp.where(lo, v, z), jnp.where(lo, z, v)], axis=0)


def _unstack_heads(v2, lo):
    return jnp.where(lo, v2[:BQ], v2[BQ:])


def _row_vector(v, lo):
    r = lax.broadcasted_iota(jnp.int32, (BQ, PAIR), 0)
    ln = lax.broadcasted_iota(jnp.int32, (BQ, PAIR), 1)
    diag = (ln % HEAD_DIM) == (r % HEAD_DIM)
    top = jnp.sum(jnp.where(diag & (r < HEAD_DIM), v, 0.0), axis=0, keepdims=True)
    bot = jnp.sum(jnp.where(diag & (r >= HEAD_DIM), v, 0.0), axis=0, keepdims=True)
    top8, bot8 = jnp.broadcast_to(top, (8, PAIR)), jnp.broadcast_to(bot, (8, PAIR))
    lo8 = _lo_mask((8, PAIR))
    head0 = jnp.where(lo8, top8, pltpu.roll(bot8, HEAD_DIM, 1))
    head1 = jnp.where(lo8, pltpu.roll(top8, HEAD_DIM, 1), bot8)
    return jnp.concatenate([head0, head1], axis=1)[:1]


def _units_per_step(nb, pairs_per_kv):
    return max(1, 16 // nb) if pairs_per_kv == 1 else 1


def _attn_fwd(q, kp, vp, bias4, sink, R, pairs_per_kv, pairs_per_bias):
    N, L, _ = q.shape
    W = BQ + 2 * R
    nb = L // BQ
    assert L >= W and nb >= 2
    G = _units_per_step(nb, pairs_per_kv)

    def body(sink_ref, q_ref, k_ref, v_ref, bias_ref, o_ref, lse_ref):
        n = pl.program_id(0)
        lo_q = _lo_mask((BQ, PAIR))
        first = lax.broadcasted_iota(jnp.int32, (2 * BQ, 1), 0) < BQ

        def blk(f, carry):
            g, b = f // nb, f % nb
            u = n * G + g
            sk = jnp.where(first, sink_ref[2 * u], sink_ref[2 * u + 1])
            q0 = pl.multiple_of(b * BQ, BQ)
            q2 = _stack_heads(q_ref[g, pl.ds(q0, BQ), :], lo_q)
            k0, variant = _key_window(b, nb, L, R, W)
            kw = k_ref[g, pl.ds(k0, W), :]
            vw = v_ref[g, pl.ds(k0, W), :]
            s = _dot_nt(q2, kw) + bias_ref[variant]
            m = jnp.maximum(jnp.max(s, axis=1, keepdims=True), sk)
            p = jnp.exp(s - m)
            l = jnp.sum(p, axis=1, keepdims=True) + jnp.exp(sk - m)
            o2 = _dot(p.astype(BF16), vw) / l
            o_ref[g, pl.ds(q0, BQ), :] = _unstack_heads(o2, lo_q)
            lse_ref[g, pl.ds(q0, BQ), :] = _unstack_heads(jnp.broadcast_to(m + jnp.log(l), (2 * BQ, PAIR)), lo_q)
            return carry

        _loop_blocks(G * nb, blk, 0, 4)

    qspec = pl.BlockSpec((G, L, PAIR), lambda n: (n, 0, 0))
    kspec = pl.BlockSpec((G, L, PAIR), lambda n: (n // pairs_per_kv, 0, 0))
    return pl.pallas_call(
        body, name="attn_fwd", grid=(N // G,),
        in_specs=[pl.BlockSpec(memory_space=pltpu.SMEM), qspec, kspec, kspec,
                  pl.BlockSpec((None, 3, 2 * BQ, W), lambda n: (n * G // pairs_per_bias, 0, 0, 0))],
        out_specs=[qspec, qspec],
        out_shape=[jax.ShapeDtypeStruct((N, L, PAIR), F32), jax.ShapeDtypeStruct((N, L, PAIR), F32)],
        compiler_params=_params("parallel"),
    )(sink, q, kp, vp, bias4)


def _attn_bwd(q, kp, vp, bias4t, sink, o, lse, do, R, pairs_per_kv, pairs_per_bias):
    N, L, _ = q.shape
    Nk = kp.shape[0]
    Pb = bias4t.shape[0]
    W = BQ + 2 * R
    nb = L // BQ
    assert L >= W and nb >= 2
    G = _units_per_step(nb, pairs_per_kv)

    def body(sink_ref, q_ref, k_ref, v_ref, bias_ref, o_ref, lse_ref, do_ref,
             dq_ref, dk_ref, dv_ref, dbias_ref, dsink_ref, dk_acc, dv_acc):
        n = pl.program_id(0)
        lo_q = _lo_mask((BQ, PAIR))
        first = lax.broadcasted_iota(jnp.int32, (1, 2 * BQ), 1) < BQ
        dsink_ref[...] = jnp.zeros_like(dsink_ref)

        @pl.when(n % pairs_per_kv == 0)
        def _():
            dk_acc[...] = jnp.zeros_like(dk_acc)
            dv_acc[...] = jnp.zeros_like(dv_acc)

        @pl.when((n * G) % pairs_per_bias == 0)
        def _():
            dbias_ref[...] = jnp.zeros_like(dbias_ref)

        def blk(f, carry):
            g, b = f // nb, f % nb
            u = n * G + g
            sk = jnp.where(first, sink_ref[2 * u], sink_ref[2 * u + 1])
            q0 = pl.multiple_of(b * BQ, BQ)
            q2 = _stack_heads(q_ref[g, pl.ds(q0, BQ), :], lo_q)
            k0, variant = _key_window(b, nb, L, R, W)
            kw = k_ref[g, pl.ds(k0, W), :]
            vw = v_ref[g, pl.ds(k0, W), :]
            dov = do_ref[g, pl.ds(q0, BQ), :]
            lse = _row_vector(lse_ref[g, pl.ds(q0, BQ), :], lo_q)
            delta = _row_vector(_half_sum(dov.astype(F32) * o_ref[g, pl.ds(q0, BQ), :], lo_q), lo_q)
            do2 = _stack_heads(dov.astype(BF16), lo_q)
            st = _dot_nt(kw, q2) + bias_ref[variant]
            pt = jnp.exp(st - lse)
            dst = pt * (_dot_nt(vw, do2) - delta)
            dstb = dst.astype(BF16)
            dbias_ref[variant] += dst
            dk_acc[g, pl.ds(k0, W), :] += _dot(dstb, q2)
            dv_acc[g, pl.ds(k0, W), :] += _dot(pt.astype(BF16), do2)
            dq_ref[g, pl.ds(q0, BQ), :] = _unstack_heads(_dot_tn(dstb, kw), lo_q).astype(BF16)
            dsink_ref[g, pl.ds(0, 1), :] -= jnp.exp(sk - lse) * delta
            return carry

        _loop_blocks(G * nb, blk, 0, 4)
        dk_ref[...] = dk_acc[...].astype(BF16)
        dv_ref[...] = dv_acc[...].astype(BF16)

    qspec = pl.BlockSpec((G, L, PAIR), lambda n: (n, 0, 0))
    kspec = pl.BlockSpec((G, L, PAIR), lambda n: (n // pairs_per_kv, 0, 0))
    return pl.pallas_call(
        body, name="attn_bwd", grid=(N // G,),
        in_specs=[pl.BlockSpec(memory_space=pltpu.SMEM), qspec, kspec, kspec,
                  pl.BlockSpec((None, 3, W, 2 * BQ), lambda n: (n * G // pairs_per_bias, 0, 0, 0)),
                  qspec, qspec, qspec],
        out_specs=[qspec, kspec, kspec,
                   pl.BlockSpec((None, 3, W, 2 * BQ), lambda n: (n * G // pairs_per_bias, 0, 0, 0)),
                   pl.BlockSpec((G, 8, 2 * BQ), lambda n: (n, 0, 0))],
        out_shape=[jax.ShapeDtypeStruct((N, L, PAIR), BF16),
                   jax.ShapeDtypeStruct((Nk, L, PAIR), BF16),
                   jax.ShapeDtypeStruct((Nk, L, PAIR), BF16),
                   jax.ShapeDtypeStruct((Pb, 3, W, 2 * BQ), F32),
                   jax.ShapeDtypeStruct((N, 8, 2 * BQ), F32)],
        scratch_shapes=[pltpu.VMEM((G, L, PAIR), F32), pltpu.VMEM((G, L, PAIR), F32)],
        compiler_params=_params("arbitrary"),
    )(sink, q, kp, vp, bias4t, o, lse, do)


def _attn_merge(branch_outs, ob, tm):
    T = ob.shape[1]
    n = len(DILS)

    def body(*refs):
        o_in, l_in, ob_ref = refs[:n], refs[n:2 * n], refs[2 * n]
        o_out, l_out, cat_ref = refs[2 * n + 1:3 * n + 1], refs[3 * n + 1:4 * n + 1], refs[4 * n + 1]
        tiles = refs[4 * n + 2:]
        for c in range(4):
            o_nat, l_nat = [], []
            for di, d in enumerate(DILS):
                for kind, (src, dst) in enumerate(((o_in[di], o_nat), (l_in[di], l_nat))):
                    tile = tiles[2 * di + kind]
                    if d == 1:
                        dst.append(src[c, 0])
                    else:
                        for r in range(d):
                            tile[pl.ds(r, tm // d, stride=d), :] = src[c, r]
                        dst.append(tile[...])
            m = functools.reduce(jnp.maximum, l_nat)
            ws = [jnp.exp(l - m) for l in l_nat]
            z = sum(ws)
            o = sum(w * t for w, t in zip(ws, o_nat)) / z
            cat_ref[:, c * PAIR:(c + 1) * PAIR] = o.astype(BF16)
            cat_ref[:, (4 + c) * PAIR:(5 + c) * PAIR] = ob_ref[c].astype(BF16)
            _spread(tiles[0], o, o_out, c, F32)
            _spread(tiles[1], m + jnp.log(z), l_out, c, F32)

    specs, shapes = _spread_specs(tm, T, F32)
    four = pl.BlockSpec((4, tm, PAIR), lambda i: (0, i, 0))
    o_views = [o.reshape(4, d, T // d, PAIR) for (o, _), d in zip(branch_outs, DILS)]
    l_views = [l.reshape(4, d, T // d, PAIR) for (_, l), d in zip(branch_outs, DILS)]
    res = pl.pallas_call(
        body, name="attn_merge", grid=(T // tm,),
        in_specs=specs + specs + [four],
        out_specs=specs + specs + [pl.BlockSpec((tm, 8 * PAIR), lambda i: (i, 0))],
        out_shape=shapes + shapes + [jax.ShapeDtypeStruct((T, 8 * PAIR), BF16)],
        scratch_shapes=[pltpu.VMEM((tm, PAIR), F32)] * (2 * n),
        compiler_params=_params("parallel"),
    )(*o_views, *l_views, ob)
    merged = [(res[di].reshape(4 * d, T // d, PAIR), res[n + di].reshape(4 * d, T // d, PAIR))
              for di, d in enumerate(DILS)]
    return merged, res[2 * n]


def _weight_arg(w, blk):
    if blk is None:
        return pl.BlockSpec(w.shape, lambda i: (0, 0)), (lambda ref: ref[...])
    D = w.shape[2]
    return (pl.BlockSpec((N_DEV, 128, D), lambda i: (0, blk, 0)),
            lambda ref: ref[...].reshape(N_DEV * 128, D))


def _oproj_fwd(x, o_cat, w, blk, tm):
    T, D = x.shape
    wspec, wload = _weight_arg(w, blk)

    def body(x_ref, o_ref, w_ref, out_ref):
        out_ref[...] = x_ref[...] + _dot(o_ref[...], wload(w_ref))

    tok = pl.BlockSpec((tm, D), lambda i: (i, 0))
    return pl.pallas_call(
        body, name="oproj_fwd", grid=(T // tm,),
        in_specs=[tok, pl.BlockSpec((tm, o_cat.shape[1]), lambda i: (i, 0)), wspec],
        out_specs=tok, out_shape=jax.ShapeDtypeStruct((T, D), F32),
        compiler_params=_params("parallel"),
    )(x, o_cat, w)


def _oproj_bwd(dx, w, blk, tm, dep=None):
    T, D = dx.shape
    wspec, wload = _weight_arg(w, blk)

    def body(dx_ref, w_ref, dxb_ref, dob_ref, *rest):
        doa_refs, tile = rest[:-1], rest[-1]
        db = dx_ref[...].astype(BF16)
        dxb_ref[...] = db
        do = _dot_nt(db, wload(w_ref))
        for c in range(4):
            _spread(tile, do[:, c * PAIR:(c + 1) * PAIR], doa_refs, c, BF16)
            dob_ref[c] = do[:, (4 + c) * PAIR:(5 + c) * PAIR].astype(BF16)

    tok = pl.BlockSpec((tm, D), lambda i: (i, 0))
    specs, shapes = _spread_specs(tm, T, BF16)
    body, in_specs, args = _with_dep(body, dep, [tok, wspec], [dx, w])
    res = pl.pallas_call(
        body, name="oproj_bwd", grid=(T // tm,),
        in_specs=in_specs,
        out_specs=[tok, pl.BlockSpec((4, tm, PAIR), lambda i: (0, i, 0))] + specs,
        out_shape=[jax.ShapeDtypeStruct((T, D), BF16), jax.ShapeDtypeStruct((4, T, PAIR), BF16)] + shapes,
        scratch_shapes=[pltpu.VMEM((tm, PAIR), F32)],
        compiler_params=_params("parallel"),
    )(*args)
    return res[0], res[1], [t.reshape(4 * d, T // d, PAIR) for t, d in zip(res[2:], DILS)]


def _attn_post(qkv, gains2, dqa, dka, dva, dqb, dkb, dvb, tm):
    T, NQ = qkv.shape
    scale = HEAD_DIM ** -0.5

    n = len(DILS)

    def body(qkv_ref, g_ref, *rest):
        dq_refs, dk_refs, dv_refs = rest[:n], rest[n:2 * n], rest[2 * n:3 * n]
        qb_ref, kb_ref, vb_ref, out_ref, dg_ref, tile = rest[3 * n:]
        lo = _lo_mask((tm, PAIR))

        @pl.when(pl.program_id(0) == 0)
        def _():
            dg_ref[...] = jnp.zeros_like(dg_ref)

        def norm_bwd(c, gi, dy):
            xv = qkv_ref[:, c * PAIR:(c + 1) * PAIR]
            r = lax.rsqrt(_half_sum(xv * xv, lo) * (1.0 / HEAD_DIM) + EPS)
            xn = xv * r
            dg_ref[gi:gi + 1, :] += jnp.sum(dy * xn, axis=0, keepdims=True)
            dxn = dy * g_ref[gi:gi + 1, :]
            dx = r * (dxn - xn * (_half_sum(dxn * xn, lo) * (1.0 / HEAD_DIM)))
            out_ref[:, c * PAIR:(c + 1) * PAIR] = dx.astype(BF16)

        def fold(v):
            return v + pltpu.roll(v, HEAD_DIM, 1)

        for c in range(4):
            norm_bwd(c, 0, _collect(tile, dq_refs, c) * scale)
            norm_bwd(4 + c, 1, _collect(tile, dk_refs, c))
            out_ref[:, (8 + c) * PAIR:(9 + c) * PAIR] = _collect(tile, dv_refs, c).astype(BF16)
            norm_bwd(12 + c, 2, qb_ref[c].astype(F32) * scale)
        kb, vb = kb_ref[...].astype(F32), vb_ref[...].astype(F32)
        norm_bwd(16, 3, jnp.where(lo, fold(kb[0]), fold(kb[1])))
        out_ref[:, 17 * PAIR:18 * PAIR] = jnp.where(lo, fold(vb[0]), fold(vb[1])).astype(BF16)

    four = pl.BlockSpec((4, tm, PAIR), lambda i: (0, i, 0))
    two = pl.BlockSpec((2, tm, PAIR), lambda i: (0, i, 0))
    specs, _ = _spread_specs(tm, T, BF16)
    views = [t.reshape(4, d, T // d, PAIR) for group in (dqa, dka, dva) for t, d in zip(group, DILS)]
    return pl.pallas_call(
        body, name="attn_post", grid=(T // tm,),
        in_specs=[pl.BlockSpec((tm, NQ), lambda i: (i, 0)), pl.BlockSpec((4, PAIR), lambda i: (0, 0))]
        + specs * 3 + [four, two, two],
        out_specs=[pl.BlockSpec((tm, NQ), lambda i: (i, 0)), pl.BlockSpec((4, PAIR), lambda i: (0, 0))],
        out_shape=[jax.ShapeDtypeStruct((T, NQ), BF16), jax.ShapeDtypeStruct((4, PAIR), F32)],
        scratch_shapes=[pltpu.VMEM((tm, PAIR), F32)],
        compiler_params=_params("arbitrary"),
    )(qkv, gains2, *views, dqb, dkb, dvb)


def _dense_norm_bwd(dres, dz, w, blk, x, g, tm):
    T, D = x.shape
    N = dz.shape[1]
    wspec, wload = _weight_arg(w, blk)

    def body(dres_ref, dz_ref, w_ref, x_ref, g_ref, dx_ref, dgn_ref):
        i = pl.program_id(0)
        dx, dg = _norm_bwd(_dot_nt(dz_ref[...], wload(w_ref)), x_ref[...], g_ref[...])
        dx_ref[...] = dres_ref[...] + dx

        @pl.when(i == 0)
        def _():
            dgn_ref[...] = dg

        @pl.when(i > 0)
        def _():
            dgn_ref[...] += dg

    tok = pl.BlockSpec((tm, D), lambda i: (i, 0))
    row = pl.BlockSpec((1, D), lambda i: (0, 0))
    return pl.pallas_call(
        body, name="dense_norm_bwd", grid=(T // tm,),
        in_specs=[tok, pl.BlockSpec((tm, N), lambda i: (i, 0)), wspec, tok, row],
        out_specs=[tok, row],
        out_shape=[jax.ShapeDtypeStruct((T, D), F32), jax.ShapeDtypeStruct((1, D), F32)],
        compiler_params=_params("arbitrary"),
    )(dres, dz, w, x, g)


def _bias_reduce(onehot, dbm):
    Hb, K = dbm.shape

    def body(oh_ref, d_ref, out_ref):
        oh = oh_ref[...]
        d = d_ref[...]
        hi = d.astype(BF16)
        r1 = d - hi.astype(F32)
        mid = r1.astype(BF16)
        low = (r1 - mid.astype(F32)).astype(BF16)
        out_ref[...] = _dot_nt(hi, oh) + _dot_nt(mid, oh) + _dot_nt(low, oh)

    vm = pl.BlockSpec(memory_space=pltpu.VMEM)
    return pl.pallas_call(
        body, name="bias_reduce", in_specs=[vm, vm], out_specs=vm,
        out_shape=jax.ShapeDtypeStruct((Hb, 128), F32),
        compiler_params=pltpu.CompilerParams(vmem_limit_bytes=VMEM_LIMIT),
    )(onehot, dbm)


def _ple_fwd(x, g, wg, blk, p, wp, target, tm):
    T, D = x.shape
    P = p.shape[1]
    with_loss = target is not None
    wspec, wload = _weight_arg(wg, blk)

    def body(*refs):
        if with_loss:
            x_ref, g_ref, wg_ref, p_ref, wp_ref, t_ref, y_ref, hn_ref, gate_ref, pp_ref, pb_ref, loss_ref = refs
        else:
            x_ref, g_ref, wg_ref, p_ref, wp_ref, y_ref, hn_ref, gate_ref, pp_ref, pb_ref = refs
        i = pl.program_id(0)
        xv = x_ref[...]
        hb = (xv * _rstd(xv) * g_ref[...]).astype(BF16)
        hn_ref[...] = hb
        gate = _sigmoid(_dot(hb, wload(wg_ref)))
        pb = p_ref[...].astype(BF16)
        pb_ref[...] = pb
        pp = _dot(pb, wp_ref[...])
        gate_ref[...] = gate
        pp_ref[...] = pp
        y = xv + gate * pp
        if with_loss:
            err = y - t_ref[...]
            y_ref[...] = err * (1.0 / D)
            part = jnp.broadcast_to(0.5 * jnp.sum(jnp.sum(err * err, axis=1, keepdims=True) * (1.0 / D),
                                                  axis=0, keepdims=True), (1, 128))

            @pl.when(i == 0)
            def _():
                loss_ref[...] = part

            @pl.when(i > 0)
            def _():
                loss_ref[...] += part
        else:
            y_ref[...] = y

    tok = pl.BlockSpec((tm, D), lambda i: (i, 0))
    ptok = pl.BlockSpec((tm, P), lambda i: (i, 0))
    in_specs = [tok, pl.BlockSpec((1, D), lambda i: (0, 0)), wspec, ptok,
                pl.BlockSpec((P, D), lambda i: (0, 0))]
    out_specs = [tok, tok, tok, tok, ptok]
    out_shape = [jax.ShapeDtypeStruct((T, D), F32), jax.ShapeDtypeStruct((T, D), BF16),
                 jax.ShapeDtypeStruct((T, D), F32), jax.ShapeDtypeStruct((T, D), F32),
                 jax.ShapeDtypeStruct((T, P), BF16)]
    args = [x, g, wg, p, wp]
    if with_loss:
        in_specs.append(tok)
        out_specs.append(pl.BlockSpec((1, 128), lambda i: (0, 0)))
        out_shape.append(jax.ShapeDtypeStruct((1, 128), F32))
        args.append(target)
    return pl.pallas_call(
        body, name="ple_fwd_loss" if with_loss else "ple_fwd", grid=(T // tm,),
        in_specs=in_specs, out_specs=out_specs, out_shape=out_shape,
        compiler_params=_params("arbitrary" if with_loss else "parallel"),
    )(*args)


def _ple_bwd(dy, gate, pp, tm, dep=None):
    T, D = dy.shape

    def body(dy_ref, gate_ref, pp_ref, dgl_ref, dpp_ref):
        d = dy_ref[...]
        gt = gate_ref[...]
        dgl_ref[...] = (d * pp_ref[...] * gt * (1.0 - gt)).astype(BF16)
        dpp_ref[...] = (d * gt).astype(BF16)

    tok = pl.BlockSpec((tm, D), lambda i: (i, 0))
    body, in_specs, args = _with_dep(body, dep, [tok, tok, tok], [dy, gate, pp])
    return pl.pallas_call(
        body, name="ple_bwd", grid=(T // tm,), in_specs=in_specs, out_specs=[tok, tok],
        out_shape=[jax.ShapeDtypeStruct((T, D), BF16), jax.ShapeDtypeStruct((T, D), BF16)],
        compiler_params=_params("parallel"),
    )(*args)


def _adamw(w, g, m, v):
    shape = w.shape
    C = shape[-1]
    w2, g2, m2, v2 = (a.reshape(-1, C) for a in (w, g, m, v))
    Rn = w2.shape[0]
    tr = Rn
    for cand in (512, 352, 256):
        if Rn % cand == 0:
            tr = cand
            break
    c1 = 1.0 - ADAM_B1 ** ADAM_STEP
    c2 = 1.0 - ADAM_B2 ** ADAM_STEP

    def body(w_ref, g_ref, m_ref, v_ref, d_ref, nm_ref, nv_ref):
        gv = g_ref[...]
        mn = ADAM_B1 * m_ref[...] + (1.0 - ADAM_B1) * gv
        vn = ADAM_B2 * v_ref[...] + (1.0 - ADAM_B2) * (gv * gv)
        d_ref[...] = -ADAM_LR * ((mn / c1) / (jnp.sqrt(vn / c2) + ADAM_EPS) + ADAM_WD * w_ref[...])
        nm_ref[...] = mn
        nv_ref[...] = vn

    spec = pl.BlockSpec((tr, C), lambda i: (i, 0))
    sh = jax.ShapeDtypeStruct((Rn, C), F32)
    d, nm, nv = pl.pallas_call(
        body, name="adamw", grid=(Rn // tr,), in_specs=[spec] * 4, out_specs=[spec] * 3, out_shape=[sh] * 3,
        compiler_params=_params("parallel"),
    )(w2, g2, m2, v2)
    return d.reshape(shape), nm.reshape(shape), nv.reshape(shape)


def _my_place():
    x, y, c = lax.axis_index("x"), lax.axis_index("y"), lax.axis_index("c")
    chips = [(1 - x, y), (x, 1 - y), (1 - x, 1 - y)]
    return x, y, c, chips


def _all_gather(arrs):
    n = len(arrs)

    def body(*refs):
        x_refs, out_refs = refs[:n], refs[n:2 * n]
        send_sems, recv_sems, local_sems = refs[2 * n:]
        x, y, c, chips = _my_place()
        me, sibling = (x, y, c), (x, y, 1 - c)

        def copy(m, k, block, to, src=None):
            rows = out_refs[m].at[4 * block[0] + 2 * block[1] + block[2]]
            return pltpu.make_async_remote_copy(
                src_ref=rows if src is None else src, dst_ref=rows,
                send_sem=send_sems.at[7 * m + k], recv_sem=recv_sems.at[7 * m + k], device_id=to, device_id_type=MESH)

        mine = [pltpu.make_async_copy(x_refs[m], out_refs[m].at[4 * x + 2 * y + c], local_sems.at[m])
                for m in range(n)]
        for cp in mine:
            cp.start()
        first = []
        for m in range(n):
            first.append(copy(m, 0, me, sibling, src=x_refs[m]))
            first += [copy(m, 1 + j, me, (*chip, c), src=x_refs[m]) for j, chip in enumerate(chips)]
        for cp in first:
            cp.start()
        passed = []
        for m in range(n):
            for j, chip in enumerate(chips):
                copy(m, 1 + j, (*chip, c), me).wait_recv()
                cp = copy(m, 4 + j, (*chip, c), sibling)
                cp.start()
                passed.append(cp)
        for m in range(n):
            copy(m, 0, sibling, me).wait_recv()
            for j, chip in enumerate(chips):
                copy(m, 4 + j, (*chip, 1 - c), me).wait_recv()
        for cp in first + passed:
            cp.wait_send()
        for cp in mine:
            cp.wait()

    hbm = pl.BlockSpec(memory_space=pl.ANY)
    return pl.pallas_call(
        body, name="all_gather", in_specs=[hbm] * n, out_specs=[hbm] * n,
        out_shape=[jax.ShapeDtypeStruct((N_DEV,) + a.shape, a.dtype) for a in arrs],
        scratch_shapes=[pltpu.SemaphoreType.DMA((7 * n,)), pltpu.SemaphoreType.DMA((7 * n,)),
                        pltpu.SemaphoreType.DMA((n,))],
    )(*arrs)


def _peer(x, y, c, k):
    return (x ^ ((k >> 2) & 1), y ^ ((k >> 1) & 1), c ^ (k & 1))


HBM_SPEC = pl.BlockSpec(memory_space=pltpu.HBM)
SEM_SPEC = pl.BlockSpec(memory_space=pltpu.SEMAPHORE)


def _exchange_refs(srcs, lands, m, k, x, y, c, scatter):
    peer = _peer(x, y, c, k)
    if scatter:
        return srcs[m].at[4 * peer[0] + 2 * peer[1] + peer[2]], lands[m].at[k - 1], peer
    return srcs[m], lands[m].at[4 * x + 2 * y + c], peer


def _exchange_start(arrs, land_shapes, scatter, name):
    n = len(arrs)

    def body(*refs):
        srcs, lands = refs[:n], refs[n:2 * n]
        send_sems, recv_sems = refs[2 * n], refs[2 * n + 1]
        token = refs[-1]
        x, y, c, _ = _my_place()
        for m in range(n):
            for k in range(1, N_DEV):
                src, dst, peer = _exchange_refs(srcs, lands, m, k, x, y, c, scatter)
                pltpu.make_async_remote_copy(
                    src_ref=src, dst_ref=dst, send_sem=send_sems.at[7 * m + k - 1],
                    recv_sem=recv_sems.at[7 * m + k - 1], device_id=peer, device_id_type=MESH).start()
        token[...] = jnp.zeros_like(token)

    zones = [lax.empty(s_, a.dtype) for s_, a in zip(land_shapes, arrs)]
    outs = pl.pallas_call(
        body, name=name,
        out_shape=(pltpu.SemaphoreType.DMA((7 * n,)), pltpu.SemaphoreType.DMA((7 * n,)),
                   *[pltpu.HBM(a.shape, a.dtype) for a in arrs], *[pltpu.HBM(z.shape, z.dtype) for z in zones],
                   jax.ShapeDtypeStruct((8, 128), F32)),
        in_specs=[HBM_SPEC] * (2 * n),
        out_specs=(SEM_SPEC, SEM_SPEC, *[HBM_SPEC] * (2 * n), pl.BlockSpec(memory_space=pltpu.VMEM)),
        input_output_aliases={m: 2 + m for m in range(2 * n)},
        compiler_params=pltpu.CompilerParams(has_side_effects=pltpu.SideEffectType.DATAFLOW_SIDE_EFFECTING),
    )(*[pltpu.with_memory_space_constraint(a, pltpu.HBM) for a in arrs],
      *[pltpu.with_memory_space_constraint(z, pltpu.HBM) for z in zones])
    return outs[0], outs[1], list(outs[2:2 + n]), list(outs[2 + n:2 + 2 * n]), outs[-1]


def _exchange_wait(send_sems, recv_sems, arrs, zones, after, scatter, name):
    n = len(arrs)
    afters = list(after) if isinstance(after, (list, tuple)) else [after]

    def body(*refs):
        srcs, lands = refs[:n], refs[n:2 * n]
        send_sems, recv_sems = refs[2 * n], refs[2 * n + 1]
        x, y, c, _ = _my_place()
        for m in range(n):
            for k in range(1, N_DEV):
                src, dst, peer = _exchange_refs(srcs, lands, m, k, x, y, c, scatter)
                cp = pltpu.make_async_remote_copy(
                    src_ref=src, dst_ref=dst, send_sem=send_sems.at[7 * m + k - 1],
                    recv_sem=recv_sems.at[7 * m + k - 1], device_id=peer, device_id_type=MESH)
                cp.wait_send()
                cp.wait_recv()

    outs = pl.pallas_call(
        body, name=name,
        out_shape=tuple(pltpu.HBM(a.shape, a.dtype) for a in list(arrs) + list(zones)),
        in_specs=[HBM_SPEC] * (2 * n) + [SEM_SPEC, SEM_SPEC] + [pl.BlockSpec(memory_space=pl.ANY)] * len(afters),
        out_specs=tuple([HBM_SPEC] * (2 * n)),
        input_output_aliases={m: m for m in range(2 * n)},
        compiler_params=pltpu.CompilerParams(has_side_effects=pltpu.SideEffectType.DATAFLOW_SIDE_EFFECTING),
    )(*arrs, *zones, send_sems, recv_sems, *afters)
    return list(outs[n:])


def _sum_parts(own, parts, tr, dep=None):
    R, W = own.shape

    def body(own_ref, parts_ref, out_ref):
        acc = own_ref[...].astype(F32)
        for k in range(N_DEV - 1):
            acc = acc + parts_ref[k].astype(F32)
        out_ref[...] = acc

    in_specs = [pl.BlockSpec((tr, W), lambda i: (i, 0)), pl.BlockSpec((N_DEV - 1, tr, W), lambda i: (0, i, 0))]
    body, in_specs, args = _with_dep(body, dep, in_specs, [own, parts])
    return pl.pallas_call(
        body, name="sum_parts", grid=(R // tr,),
        in_specs=in_specs,
        out_specs=pl.BlockSpec((tr, W), lambda i: (i, 0)),
        out_shape=jax.ShapeDtypeStruct((R, W), F32),
        compiler_params=_params("parallel"),
    )(*args)


def _all_reduce_small(v, dep=None):
    Rn, Wd = v.shape

    def body(v_ref, out_ref, gat_ref, send_sems, recv_sems):
        x, y, c, _ = _my_place()
        me = 4 * x + 2 * y + c
        gat_ref[me] = v_ref[...]
        copies = []
        for k in range(1, N_DEV):
            fx, fy, fc = (k >> 2) & 1, (k >> 1) & 1, k & 1
            peer = (x ^ fx, y ^ fy, c ^ fc)
            cp = pltpu.make_async_remote_copy(
                src_ref=v_ref, dst_ref=gat_ref.at[me], send_sem=send_sems.at[k - 1], recv_sem=recv_sems.at[k - 1],
                device_id=peer, device_id_type=MESH)
            cp.start()
            copies.append(cp)
        for cp in copies:
            cp.wait_recv()
        for cp in copies:
            cp.wait_send()
        acc = gat_ref[0]
        for k in range(1, N_DEV):
            acc = acc + gat_ref[k]
        out_ref[...] = acc

    vm = pl.BlockSpec(memory_space=pltpu.VMEM)
    body, in_specs, args = _with_dep(body, dep, [vm], [v])
    return pl.pallas_call(
        body, name="all_reduce_small", in_specs=in_specs, out_specs=vm,
        out_shape=jax.ShapeDtypeStruct((Rn, Wd), F32),
        scratch_shapes=[pltpu.VMEM((N_DEV, Rn, Wd), F32), pltpu.SemaphoreType.DMA((7,)),
                        pltpu.SemaphoreType.DMA((7,))],
    )(*args)


def _t5_bucket(rel):
    half = N_BUCKETS // 2
    max_exact = half // 2
    ret = jnp.where(rel > 0, half, 0)
    n = jnp.abs(rel)
    nf = jnp.maximum(n, 1).astype(F32)
    large = max_exact + (jnp.log(nf / max_exact) / math.log(MAX_DISTANCE / max_exact)
                         * (half - max_exact)).astype(jnp.int32)
    large = jnp.minimum(large, half - 1)
    return ret + jnp.where(n < max_exact, n, large)


def _band(R, d):
    W = BQ + 2 * R
    rel = jnp.arange(W)[None, :] - R - jnp.arange(BQ)[:, None]
    return _t5_bucket(rel * d), jnp.abs(rel) <= R


def _onehot(R, d):
    bkt, in_band = _band(R, d)
    return ((bkt.reshape(1, -1) == jnp.arange(128)[:, None]) & in_band.reshape(1, -1)).astype(BF16)


def _bias_expand(table_t, onehot):
    H = table_t.shape[0]
    K = onehot.shape[1]

    def body(t_ref, oh_ref, out_ref):
        oh = oh_ref[...]
        t = t_ref[...]
        hi = t.astype(BF16)
        r1 = t - hi.astype(F32)
        mid = r1.astype(BF16)
        low = (r1 - mid.astype(F32)).astype(BF16)
        marked = _dot(jnp.ones(t.shape, BF16), oh) > 0.5
        out_ref[...] = jnp.where(marked, _dot(hi, oh) + _dot(mid, oh) + _dot(low, oh), NEG)

    vm = pl.BlockSpec(memory_space=pltpu.VMEM)
    return pl.pallas_call(
        body, name="bias_expand", in_specs=[vm, vm], out_specs=vm,
        out_shape=jax.ShapeDtypeStruct((H, K), F32),
        compiler_params=pltpu.CompilerParams(vmem_limit_bytes=VMEM_LIMIT),
    )(table_t, onehot)


def _bias_matrix(table, R, d):
    table_t = jnp.pad(table.T, ((0, 0), (0, 128 - N_BUCKETS)))
    return _bias_expand(table_t, _onehot(R, d)).reshape(table.shape[1], BQ, BQ + 2 * R)


def _bias_variants(base, R):
    H, _, W = base.shape
    fill = jnp.full((H, BQ, R), NEG, F32)
    first = jnp.concatenate([base[:, :, R:], fill], axis=2)
    last = jnp.concatenate([fill, base[:, :, :W - R]], axis=2)
    v = jnp.stack([base, first, last], axis=1)
    v = v.reshape(H // 2, 2, 3, BQ, W).transpose(0, 2, 1, 3, 4).reshape(H // 2, 3, 2 * BQ, W)
    return v, v.transpose(0, 1, 3, 2)


def _bias_grad(dbt, R, d):
    P, _, W, _ = dbt.shape
    dbt = dbt[:, 0].at[:, R:].add(dbt[:, 1, :W - R]).at[:, :W - R].add(dbt[:, 2, R:])
    dbm = dbt.reshape(P, W, 2, BQ).transpose(0, 2, 3, 1).reshape(2 * P, BQ * W)
    return _bias_reduce(_onehot(R, d), dbm)[:, :N_BUCKETS].T


def _tile2(gain):
    return jnp.concatenate([gain, gain])


ROW_W_O, ROW_GATE, ROW_QKV, ROW_PROJ, B_ROWS = 768, 896, 1024, 1312, 1344
BLK_W_O, BLK_GATE = ROW_W_O // 128, ROW_GATE // 128


def _pack_layer(wts, i):
    a = jnp.stack([wts["ffn1_w_in"][i], wts["ffn2_w_in"][i]])
    D = a.shape[1]
    b = jnp.concatenate([
        wts["ffn1_w_out"][i], wts["ffn2_w_out"][i],
        jnp.zeros((ROW_W_O - 2 * wts["ffn1_w_out"].shape[1], D), a.dtype),
        wts["w_o"][i], wts["w_ple_gate"][i], wts["w_qkv"][i].reshape(-1, D), wts["w_ple_proj"][i].reshape(-1, D)])
    return a, b


def _unpack_layer(sums, like):
    w_in2, b1, b2, w_in1, w_out1 = sums
    n_out, n_sq = like["ffn1_w_out"].shape[1], like["w_o"].shape[1]
    out = {}
    if w_in2 is not None:
        out.update(ffn2_w_in=w_in2, ffn2_w_out=b1[:n_out], w_ple_gate=b1[n_out:n_out + n_sq],
                   w_ple_proj=b1[n_out + n_sq:].reshape(like["w_ple_proj"].shape[1:]))
    if b2 is not None:
        out.update(w_o=b2[:n_sq], w_qkv=b2[n_sq:].reshape(like["w_qkv"].shape[1:]))
    if w_in1 is not None:
        out.update(ffn1_w_in=w_in1, ffn1_w_out=w_out1)
    return out


def _col_sharded(gb, r0, r1, rows):
    return gb[:, r0:r1].reshape(N_DEV, rows, -1).transpose(1, 0, 2).reshape(rows, -1)


def _to_col_shards(g):
    rows = g.shape[0]
    return g.reshape(rows, N_DEV, -1).transpose(1, 0, 2).reshape(N_DEV, -1, 1024)


def _layer_weights(ga, gb, p_dim):
    return dict(ga=ga, gb=gb, w_qkv=_col_sharded(gb, ROW_QKV, ROW_PROJ, ga.shape[2]),
                w_proj=_col_sharded(gb, ROW_PROJ, B_ROWS, p_dim))


def _layer_fwd(x, p, w, sm, i, target, tm, biases, dep=None):
    ga, gb = w["ga"], w["gb"]
    saved = {}
    saved["x0"] = x
    x1, saved["h1"], saved["zg1"], saved["zu1"], saved["s1"] = _ffn_fwd(
        x, sm["norm_ffn1"][i][None], ga, gb, 0, 2 * tm, dep)
    saved["x1"] = x1
    qkv, saved["hm"] = _qkv_fwd(x1, sm["norm_mix"][i][None], w["w_qkv"], 2 * tm)
    saved["qkv"] = qkv
    gains2 = jnp.stack([_tile2(sm[k][i]) for k in ("q_norm_a", "k_norm_a", "q_norm_b", "k_norm_b")])
    saved["gains2"] = gains2
    qb, kb, vb, qkv_d = _attn_prep(qkv, gains2, tm)
    no_sink = jnp.full((8,), NEG, F32)
    branches = []
    outs = []
    for (R, d), bias, (qd, kd, vd) in zip(DILATED, biases[:3], qkv_d):
        sink = jnp.tile(no_sink, d)
        outs.append(_attn_fwd(qd, kd, vd, bias[0], sink, R, 1, d))
        branches.append((qd, kd, vd, bias, sink, R, d))
    bias_b = biases[3]
    sink_b = sm["sink_b"][i]
    ob, lb = _attn_fwd(qb, kb, vb, bias_b[0], sink_b, SWA_RADIUS, 2, 1)
    merged, o_cat = _attn_merge(outs, ob, tm)
    saved.update(branches=branches, b=(qb, kb, vb, bias_b, sink_b), merged=merged, ob=ob, lb=lb, o_cat=o_cat)
    x2 = _oproj_fwd(x1, o_cat, gb, BLK_W_O, 2 * tm)
    saved["x2"] = x2
    x3, saved["h2"], saved["zg2"], saved["zu2"], saved["s2"] = _ffn_fwd(
        x2, sm["norm_ffn2"][i][None], ga, gb, 1, 2 * tm)
    saved["x3"] = x3
    res = _ple_fwd(x3, sm["norm_ple"][i][None], gb, BLK_GATE, p, w["w_proj"], target, tm)
    y, saved["hp"], saved["gate"], saved["pp"], saved["pb"] = res[:5]
    loss = res[5] if target is not None else None
    return y, loss, saved


def _layer_bwd(dy, w, sm, i, sv, tm, dep=None, on_ready=None, on_small=None, on_last=None):
    ga, gb = w["ga"], w["gb"]
    gs = {}
    D = dy.shape[1]
    dgl, dpp = _ple_bwd(dy, sv["gate"], sv["pp"], tm, dep)
    d_gate = _matmul_tn(sv["hp"], dgl, D, 2 * tm)
    d_proj = _matmul_tn(sv["pb"], dpp, D, 2 * tm)
    dx3, gs["norm_ple"] = _dense_norm_bwd(dy, dgl, gb, BLK_GATE, sv["x3"], sm["norm_ple"][i][None], 2 * tm)
    dx2, dyb, dzg, dzu, gs["norm_ffn2"] = _ffn_bwd(dx3, sv["x2"], sm["norm_ffn2"][i][None], sv["zg2"], sv["zu2"],
                                                   ga, gb, 1, tm)
    dwin2, dwo2 = _ffn_dw(sv["h2"], dzg, dzu, sv["s2"], dyb, 2 * tm)
    half = dwo2.shape[1] // 2
    after_ffn2 = [dwin2, jnp.concatenate([dwo2.reshape(N_DEV, half, D), d_gate.reshape(N_DEV, -1, D),
                                          _to_col_shards(d_proj)], axis=1)]
    token = None if on_ready is None else on_ready(0, after_ffn2)
    dx2b, do_b, do_a = _oproj_bwd(dx2, gb, BLK_W_O, tm, token)
    d_wo = _matmul_tn(sv["o_cat"], dx2b, D, 2 * tm)
    dqa, dka, dva, dbias = [], [], [], []
    for (qd, kd, vd, bias, sink, R, d), (oa, la), do_d in zip(sv["branches"], sv["merged"], do_a):
        dq, dk, dv, dbm, _ = _attn_bwd(qd, kd, vd, bias[1], sink, oa, la, do_d, R, 1, d)
        dqa.append(dq)
        dka.append(dk)
        dva.append(dv)
        dbias.append(dbm)
    qb, kb, vb, bias_b, sink_b = sv["b"]
    dqb, dkb, dvb, dbm_b, dsink = _attn_bwd(qb, kb, vb, bias_b[1], sink_b, sv["ob"], sv["lb"], do_b,
                                            SWA_RADIUS, 2, 1)
    gs["rel_bias"] = dbias + [dbm_b]
    gs["sink_b"] = jnp.sum(dsink[:, 0].reshape(-1, 2, BQ), axis=2).reshape(-1)
    dqkv, dgains2 = _attn_post(sv["qkv"], sv["gains2"], dqa, dka, dva, dqb,
                               dkb, dvb, tm)
    dgains = dgains2[:, :HEAD_DIM] + dgains2[:, HEAD_DIM:]
    for k, name in enumerate(("q_norm_a", "k_norm_a", "q_norm_b", "k_norm_b")):
        gs[name] = dgains[k]
    d_qkv = _matmul_tn(sv["hm"], dqkv, dqkv.shape[1] // 2, 2 * tm)
    after_mixer = [jnp.concatenate([d_wo.reshape(N_DEV, -1, D), _to_col_shards(d_qkv)], axis=1)]
    token = None if on_ready is None else on_ready(1, after_mixer)
    dx1, gs["norm_mix"] = _dense_norm_bwd(dx2, dqkv, w["w_qkv"], None, sv["x1"], sm["norm_mix"][i][None], 2 * tm)
    g1 = sm["norm_ffn1"][i][None]
    if on_last is None:
        dx0, dyb, dzg, dzu, gs["norm_ffn1"] = _ffn_bwd(dx1, sv["x0"], g1, sv["zg1"], sv["zu1"], ga, gb, 0, tm, token)
        dwin1, dwo1 = _ffn_dw(sv["h1"], dzg, dzu, sv["s1"], dyb, 2 * tm)
        return dx0, (after_ffn2, after_mixer, [dwin1, dwo1.reshape(N_DEV, half, D)]), gs
    dyb, dzg, dzu = _ffn_bwd_dz(dx1, sv["zg1"], sv["zu1"], gb, 0, 2 * tm, token)
    dwin1, dwo1 = _ffn_dw(sv["h1"], dzg, dzu, sv["s1"], dyb, 2 * tm, on_small(gs))
    last = [dwin1, dwo1.reshape(N_DEV, half, D)]
    dx0, gs["norm_ffn1"] = _ffn_bwd_dx(dx1, sv["x0"], g1, dzg, dzu, ga, 0, 2 * tm, on_last(last))
    return dx0, (after_ffn2, after_mixer, last), gs


def _bias_matrices(rel_bias):
    biases = [_bias_variants(_bias_matrix(rel_bias[:, :8], R, d), R) for R, d in DILATED]
    biases.append(_bias_variants(_bias_matrix(rel_bias[:, 8:], SWA_RADIUS, 1), SWA_RADIUS))
    return biases


def _stack_small(per_layer):
    small = {}
    for k, v in per_layer.items():
        if k == "rel_bias":
            per_branch = [sum(parts) for parts in zip(*v.values())]
            drel_a = sum(_bias_grad(t, R, d) for t, (R, d) in zip(per_branch[:3], DILATED))
            small[k] = jnp.concatenate([drel_a, _bias_grad(per_branch[3], SWA_RADIUS, 1)], axis=1)
        else:
            small[k] = jnp.stack([v[i].reshape(-1) for i in sorted(v)])
    return small


TM = 512
SUM_TILES = (512, 512, 416, 512, 352)
LAST_GROUP = ("ffn1_w_in", "ffn1_w_out")


def _pack_small(d, extra=None):
    parts = [d[k].reshape(-1) for k in SMALL]
    if extra is not None:
        parts.append(extra.reshape(-1))
    flat = jnp.concatenate(parts)
    return jnp.pad(flat, (0, SMALL_ROWS * 128 - flat.shape[0])).reshape(SMALL_ROWS, 128)


def _unpack_small(buf, like):
    flat = buf.reshape(-1)
    out, off = {}, 0
    for k in SMALL:
        n = like[k].size
        out[k] = flat[off:off + n].reshape(like[k].shape)
        off += n
    return out, flat[off]


def kernel(x, p, rel_bias, norm_ffn1, ffn1_w_in, ffn1_w_out, norm_mix, w_qkv, q_norm_a, k_norm_a, q_norm_b, k_norm_b, sink_b, w_o, norm_ffn2, ffn2_w_in, ffn2_w_out, norm_ple, w_ple_gate, w_ple_proj, loss_target, m_rel_bias, m_norm_ffn1, m_ffn1_w_in, m_ffn1_w_out, m_norm_mix, m_w_qkv, m_q_norm_a, m_k_norm_a, m_q_norm_b, m_k_norm_b, m_sink_b, m_w_o, m_norm_ffn2, m_ffn2_w_in, m_ffn2_w_out, m_norm_ple, m_w_ple_gate, m_w_ple_proj, v_rel_bias, v_norm_ffn1, v_ffn1_w_in, v_ffn1_w_out, v_norm_mix, v_w_qkv, v_q_norm_a, v_k_norm_a, v_q_norm_b, v_k_norm_b, v_sink_b, v_w_o, v_norm_ffn2, v_ffn2_w_in, v_ffn2_w_out, v_norm_ple, v_w_ple_gate, v_w_ple_proj):
    wts = dict(rel_bias=rel_bias, norm_ffn1=norm_ffn1, ffn1_w_in=ffn1_w_in, ffn1_w_out=ffn1_w_out,
               norm_mix=norm_mix, w_qkv=w_qkv, q_norm_a=q_norm_a, k_norm_a=k_norm_a, q_norm_b=q_norm_b,
               k_norm_b=k_norm_b, sink_b=sink_b, w_o=w_o, norm_ffn2=norm_ffn2, ffn2_w_in=ffn2_w_in,
               ffn2_w_out=ffn2_w_out, norm_ple=norm_ple, w_ple_gate=w_ple_gate, w_ple_proj=w_ple_proj)
    mom = dict(rel_bias=m_rel_bias, norm_ffn1=m_norm_ffn1, ffn1_w_in=m_ffn1_w_in, ffn1_w_out=m_ffn1_w_out,
               norm_mix=m_norm_mix, w_qkv=m_w_qkv, q_norm_a=m_q_norm_a, k_norm_a=m_k_norm_a, q_norm_b=m_q_norm_b,
               k_norm_b=m_k_norm_b, sink_b=m_sink_b, w_o=m_w_o, norm_ffn2=m_norm_ffn2, ffn2_w_in=m_ffn2_w_in,
               ffn2_w_out=m_ffn2_w_out, norm_ple=m_norm_ple, w_ple_gate=m_w_ple_gate, w_ple_proj=m_w_ple_proj)
    var = dict(rel_bias=v_rel_bias, norm_ffn1=v_norm_ffn1, ffn1_w_in=v_ffn1_w_in, ffn1_w_out=v_ffn1_w_out,
               norm_mix=v_norm_mix, w_qkv=v_w_qkv, q_norm_a=v_q_norm_a, k_norm_a=v_k_norm_a, q_norm_b=v_q_norm_b,
               k_norm_b=v_k_norm_b, sink_b=v_sink_b, w_o=v_w_o, norm_ffn2=v_norm_ffn2, ffn2_w_in=v_ffn2_w_in,
               ffn2_w_out=v_ffn2_w_out, norm_ple=v_norm_ple, w_ple_gate=v_w_ple_gate, w_ple_proj=v_w_ple_proj)
    sm = {k: wts[k] for k in SMALL}
    p_dim = p.shape[-1]
    me = 4 * lax.axis_index("x") + 2 * lax.axis_index("y") + lax.axis_index("c")
    packed = []
    for i in range(2):
        a, b = _pack_layer(wts, i)
        packed.append([a.reshape(-1, a.shape[-1]).astype(BF16), b.astype(BF16)])
    a_shape = (2, ffn1_w_in.shape[1], ffn1_w_in.shape[2])

    def weights_of(zones):
        return _layer_weights(zones[0].reshape((N_DEV,) + a_shape), zones[1], p_dim)

    w0 = weights_of(_all_gather(packed[0]))
    zone_shapes = [(N_DEV,) + t.shape for t in packed[1]]
    ssem, rsem, thru, zones, token = _exchange_start(packed[1], zone_shapes, False, "gather_start")
    biases = _bias_matrices(rel_bias)
    x1, _, sv0 = _layer_fwd(x[0], p[0, 0], w0, sm, 0, None, TM, biases, dep=token)
    zones = _exchange_wait(ssem, rsem, thru, zones, x1, False, "gather_wait")
    w1 = weights_of([lax.dynamic_update_index_in_dim(z, t, me, 0) for z, t in zip(zones, packed[1])])
    dy, loss, sv1 = _layer_fwd(x1, p[1, 0], w1, sm, 1, loss_target[0], TM, biases)

    def slots_for(arrs):
        return [(N_DEV - 1,) + t.shape[1:] for t in arrs]

    held1, held = {}, {}

    def on_ready1(stage, group):
        held1[stage] = _exchange_start(group, slots_for(group), True, f"scatter1_start_{stage}")
        return held1[stage][4]

    dx1, groups1, gs1 = _layer_bwd(dy, w1, sm, 1, sv1, TM, on_ready=on_ready1)
    on_ready1(2, groups1[2])
    g1 = groups1[0] + groups1[1] + groups1[2]

    def on_ready(stage, group):
        if stage == 1:
            held["slots1"] = [t for st in (0, 1, 2)
                              for t in _exchange_wait(*held1[st][:4], group[0], True, f"scatter1_wait_{st}")]
        held[stage] = _exchange_start(group, slots_for(group), True, f"scatter_start_{stage}")
        return held[stage][4]

    def on_small(gs0):
        part = dict(gs0, norm_ffn1=jnp.zeros_like(gs1["norm_ffn1"]))
        gsmall = _stack_small({k: {0: part[k], 1: gs1[k]} for k in part})
        held["small"] = _all_reduce_small(_pack_small(gsmall, loss[0, :1]))
        return held["small"]

    def on_last(group):
        held["last"] = _exchange_start(group, slots_for(group), True, "scatter_start_2")
        return held["last"][4]

    dx, groups0, gs0 = _layer_bwd(dx1, w0, sm, 0, sv0, TM, dep=held1[2][4], on_ready=on_ready, on_small=on_small,
                                  on_last=on_last)
    last = groups0[2]
    slots0 = [_exchange_wait(*held[stage][:4], last[0], True, f"scatter_wait_{stage}") for stage in (0, 1)]

    def summed(arrs, slots, tiles, dep=None):
        return [_sum_parts(lax.dynamic_index_in_dim(t, me, 0, keepdims=False), s_, tr, dep)
                for t, s_, tr in zip(arrs, slots, tiles)]

    cover = held["last"][4]
    r1 = summed(g1, held["slots1"], SUM_TILES, cover)
    r0 = summed(groups0[0], slots0[0], SUM_TILES[:2], cover) + summed(groups0[1], slots0[1], SUM_TILES[2:3], cover)

    def update(names, layers):
        for k in names:
            grads[k] = jnp.stack([layers[0][k], layers[1][k]])
            delta[k], new_m[k], new_v[k] = _adamw(wts[k], grads[k], mom[k], var[k])

    grads, delta, new_m, new_v = {}, {}, {}, {}
    layer1 = _unpack_layer(r1, wts)
    update([k for k in BIG if k not in LAST_GROUP], [_unpack_layer(r0 + [None, None], wts), layer1])

    cover_done = [dx] + [delta[k] for k in BIG if k not in LAST_GROUP]
    slots_last = _exchange_wait(*held["last"][:4], cover_done, True, "scatter_wait_2")
    update(LAST_GROUP, [_unpack_layer([None, None, None] + summed(last, slots_last, SUM_TILES[3:]), wts), layer1])
    late = _all_reduce_small(gs0["norm_ffn1"].reshape(-1, 128), dep=slots_last[0])
    small_sum, loss_sum = _unpack_small(held["small"], sm)
    small_sum["norm_ffn1"] = small_sum["norm_ffn1"].at[0].add(late.reshape(-1))
    grads.update(small_sum)
    zeros = {k: jnp.zeros_like(wts[k]) for k in SMALL}
    ds, ms, vs = _adamw(_pack_small(wts), _pack_small(small_sum), _pack_small(mom), _pack_small(var))
    for packed, dst in ((ds, delta), (ms, new_m), (vs, new_v)):
        dst.update(_unpack_small(packed, zeros)[0])

    return (loss_sum, dx[None], *[grads[k] for k in WEIGHTS], *[delta[k] for k in WEIGHTS],
            *[new_m[k] for k in WEIGHTS], *[new_v[k] for k in WEIGHTS])
```

```python
import functools
import math

import jax
import jax.numpy as jnp
from jax import lax
from jax.experimental import pallas as pl
from jax.experimental.pallas import tpu as pltpu

F32 = jnp.float32
BF16 = jnp.bfloat16

N_DEV = 8
HEAD_DIM = 64
PAIR = 2 * HEAD_DIM
BQ = 128
N_BUCKETS = 32
MAX_DISTANCE = 1024
DILATED = ((64, 1), (64, 4), (64, 16))
SWA_RADIUS = 128
EPS = 1e-6
NEG = -1e30
ADAM_LR, ADAM_B1, ADAM_B2, ADAM_EPS, ADAM_WD, ADAM_STEP = 0.001, 0.9, 0.999, 1e-08, 0.01, 10
VMEM_LIMIT = 56 * 1024 * 1024
MESH = pl.DeviceIdType.MESH

BIG = ("ffn1_w_in", "ffn1_w_out", "w_qkv", "w_o", "ffn2_w_in", "ffn2_w_out", "w_ple_gate", "w_ple_proj")
SMALL = ("rel_bias", "norm_ffn1", "norm_mix", "q_norm_a", "k_norm_a", "q_norm_b", "k_norm_b", "sink_b",
         "norm_ffn2", "norm_ple")
WEIGHTS = ("rel_bias", "norm_ffn1", "ffn1_w_in", "ffn1_w_out", "norm_mix", "w_qkv", "q_norm_a", "k_norm_a",
           "q_norm_b", "k_norm_b", "sink_b", "w_o", "norm_ffn2", "ffn2_w_in", "ffn2_w_out", "norm_ple",
           "w_ple_gate", "w_ple_proj")
SMALL_ROWS = 96


def _params(*sem):
    return pltpu.CompilerParams(dimension_semantics=sem, vmem_limit_bytes=VMEM_LIMIT)


def _dot(a, b):
    return jnp.dot(a, b, preferred_element_type=F32)


def _dot_nt(a, b):
    return lax.dot_general(a, b, (((1,), (1,)), ((), ())), preferred_element_type=F32)


def _dot_tn(a, b):
    return lax.dot_general(a, b, (((0,), (0,)), ((), ())), preferred_element_type=F32)


def _sigmoid(x):
    return 1.0 / (1.0 + jnp.exp(-x))


def _rstd(xv):
    return lax.rsqrt(jnp.mean(xv * xv, axis=-1, keepdims=True) + EPS)


def _norm_bwd(dh, xv, gv):
    r = _rstd(xv)
    xn = xv * r
    dg = jnp.sum(dh * xn, axis=0, keepdims=True)
    dxn = dh * gv
    dx = r * (dxn - xn * jnp.mean(dxn * xn, axis=-1, keepdims=True))
    return dx, dg


def _lo_mask(shape):
    return lax.broadcasted_iota(jnp.int32, shape, len(shape) - 1) < HEAD_DIM


def _half_sum(t, lo):
    s0 = jnp.sum(jnp.where(lo, t, 0.0), axis=1, keepdims=True)
    s1 = jnp.sum(jnp.where(lo, 0.0, t), axis=1, keepdims=True)
    return jnp.where(lo, s0, s1)


FFN_PARTS = 2


def _ffn_weight_specs(f, nj, D, C):
    return [pl.BlockSpec((None, None, D, C), lambda i, j: (j, f, 0, 0)),
            pl.BlockSpec((None, None, D, C), lambda i, j: (j + nj, f, 0, 0)),
            pl.BlockSpec((2, C // 2, D), lambda i, j: (j, f, 0))]


def _with_dep(body, dep, in_specs, args):
    if dep is None:
        return body, in_specs, args

    def body_after(dep_ref, *refs):
        body(*refs)

    return body_after, [pl.BlockSpec(memory_space=pl.ANY)] + in_specs, [dep] + args


def _ffn_fwd(x, g, ga, gb, f, tm, dep=None):
    T, D = x.shape
    nj, C = ga.shape[0] // 2, ga.shape[3]

    def body(x_ref, g_ref, wg_ref, wu_ref, wo_ref, xo_ref, h_ref, zg_ref, zu_ref, s_ref, h_scr, acc):
        j = pl.program_id(1)

        @pl.when(j == 0)
        def _():
            xv = x_ref[...]
            hb = (xv * _rstd(xv) * g_ref[...]).astype(BF16)
            h_scr[...] = hb
            h_ref[...] = hb
            acc[...] = jnp.zeros_like(acc)

        wo = wo_ref[...].reshape(C, D)
        for part in range(FFN_PARTS):
            sl = pl.ds(part * (tm // FFN_PARTS), tm // FFN_PARTS)
            hb = h_scr[sl, :]
            gt = _dot(hb, wg_ref[...])
            up = _dot(hb, wu_ref[...])
            s = (gt * _sigmoid(gt) * up).astype(BF16)
            zg_ref[sl, :] = gt.astype(BF16)
            zu_ref[sl, :] = up.astype(BF16)
            s_ref[sl, :] = s
            acc[sl, :] += _dot(s, wo)

        @pl.when(j == nj - 1)
        def _():
            xo_ref[...] = x_ref[...] + 0.5 * acc[...]

    tok = pl.BlockSpec((tm, D), lambda i, j: (i, 0))
    chunk = pl.BlockSpec((None, tm, C), lambda i, j: (j, i, 0))
    in_specs = [tok, pl.BlockSpec((1, D), lambda i, j: (0, 0))] + _ffn_weight_specs(f, nj, D, C)
    body, in_specs, args = _with_dep(body, dep, in_specs, [x, g, ga, ga, gb])
    return pl.pallas_call(
        body, name="ffn_fwd", grid=(T // tm, nj),
        in_specs=in_specs,
        out_specs=[tok, tok, chunk, chunk, chunk],
        out_shape=[jax.ShapeDtypeStruct((T, D), F32), jax.ShapeDtypeStruct((T, D), BF16),
                   jax.ShapeDtypeStruct((nj, T, C), BF16), jax.ShapeDtypeStruct((nj, T, C), BF16),
                   jax.ShapeDtypeStruct((nj, T, C), BF16)],
        scratch_shapes=[pltpu.VMEM((tm, D), BF16), pltpu.VMEM((tm, D), F32)],
        compiler_params=_params("parallel", "arbitrary"),
    )(*args)


def _ffn_bwd(dxo, x, g, zg, zu, ga, gb, f, tm, dep=None):
    T, D = x.shape
    nj, C = ga.shape[0] // 2, ga.shape[3]

    def body(dxo_ref, x_ref, g_ref, zg_ref, zu_ref, wg_ref, wu_ref, wo_ref,
             dx_ref, dy_ref, dzg_ref, dzu_ref, dgn_ref, dy_scr, acc):
        i, j = pl.program_id(0), pl.program_id(1)

        @pl.when(j == 0)
        def _():
            dyb = (0.5 * dxo_ref[...]).astype(BF16)
            dy_scr[...] = dyb
            dy_ref[...] = dyb
            acc[...] = jnp.zeros_like(acc)

        wo = wo_ref[...].reshape(C, D)
        for part in range(FFN_PARTS):
            sl = pl.ds(part * (tm // FFN_PARTS), tm // FFN_PARTS)
            ds = _dot_nt(dy_scr[sl, :], wo)
            gt = zg_ref[sl, :].astype(F32)
            up = zu_ref[sl, :].astype(F32)
            sg = _sigmoid(gt)
            dgt = (ds * up * (sg * (1.0 + gt * (1.0 - sg)))).astype(BF16)
            dup = (ds * (gt * sg)).astype(BF16)
            dzg_ref[sl, :] = dgt
            dzu_ref[sl, :] = dup
            acc[sl, :] += _dot_nt(dgt, wg_ref[...]) + _dot_nt(dup, wu_ref[...])

        @pl.when(j == nj - 1)
        def _():
            dx, dg = _norm_bwd(acc[...], x_ref[...], g_ref[...])
            dx_ref[...] = dxo_ref[...] + dx

            @pl.when(i == 0)
            def _():
                dgn_ref[...] = dg

            @pl.when(i > 0)
            def _():
                dgn_ref[...] += dg

    tok = pl.BlockSpec((tm, D), lambda i, j: (i, 0))
    chunk = pl.BlockSpec((None, tm, C), lambda i, j: (j, i, 0))
    row = pl.BlockSpec((1, D), lambda i, j: (0, 0))
    in_specs = [tok, tok, row, chunk, chunk] + _ffn_weight_specs(f, nj, D, C)
    body, in_specs, args = _with_dep(body, dep, in_specs, [dxo, x, g, zg, zu, ga, ga, gb])
    return pl.pallas_call(
        body, name="ffn_bwd", grid=(T // tm, nj),
        in_specs=in_specs,
        out_specs=[tok, tok, chunk, chunk, row],
        out_shape=[jax.ShapeDtypeStruct((T, D), F32), jax.ShapeDtypeStruct((T, D), BF16),
                   jax.ShapeDtypeStruct((nj, T, C), BF16), jax.ShapeDtypeStruct((nj, T, C), BF16),
                   jax.ShapeDtypeStruct((1, D), F32)],
        scratch_shapes=[pltpu.VMEM((tm, D), BF16), pltpu.VMEM((tm, D), F32)],
        compiler_params=_params("arbitrary", "arbitrary"),
    )(*args)


def _ffn_bwd_dz(dxo, zg, zu, gb, f, tm, dep=None):
    T, D = dxo.shape
    nj, C = zg.shape[0], zg.shape[2]

    def body(dxo_ref, zg_ref, zu_ref, wo_ref, dy_ref, dzg_ref, dzu_ref, dy_scr):
        @pl.when(pl.program_id(1) == 0)
        def _():
            dyb = (0.5 * dxo_ref[...]).astype(BF16)
            dy_scr[...] = dyb
            dy_ref[...] = dyb

        wo = wo_ref[...].reshape(C, D)
        for part in range(FFN_PARTS):
            sl = pl.ds(part * (tm // FFN_PARTS), tm // FFN_PARTS)
            ds = _dot_nt(dy_scr[sl, :], wo)
            gt = zg_ref[sl, :].astype(F32)
            up = zu_ref[sl, :].astype(F32)
            sg = _sigmoid(gt)
            dzg_ref[sl, :] = (ds * up * (sg * (1.0 + gt * (1.0 - sg)))).astype(BF16)
            dzu_ref[sl, :] = (ds * (gt * sg)).astype(BF16)

    tok = pl.BlockSpec((tm, D), lambda i, j: (i, 0))
    chunk = pl.BlockSpec((None, tm, C), lambda i, j: (j, i, 0))
    in_specs = [tok, chunk, chunk, _ffn_weight_specs(f, nj, D, C)[2]]
    body, in_specs, args = _with_dep(body, dep, in_specs, [dxo, zg, zu, gb])
    return pl.pallas_call(
        body, name="ffn_bwd_dz", grid=(T // tm, nj),
        in_specs=in_specs, out_specs=[tok, chunk, chunk],
        out_shape=[jax.ShapeDtypeStruct((T, D), BF16), jax.ShapeDtypeStruct((nj, T, C), BF16),
                   jax.ShapeDtypeStruct((nj, T, C), BF16)],
        scratch_shapes=[pltpu.VMEM((tm, D), BF16)],
        compiler_params=_params("parallel", "arbitrary"),
    )(*args)


def _ffn_bwd_dx(dxo, x, g, dzg, dzu, ga, f, tm, dep=None):
    T, D = x.shape
    nj, C = ga.shape[0] // 2, ga.shape[3]

    def body(dxo_ref, x_ref, g_ref, dzg_ref, dzu_ref, wg_ref, wu_ref, dx_ref, dgn_ref, acc):
        i, j = pl.program_id(0), pl.program_id(1)

        @pl.when(j == 0)
        def _():
            acc[...] = jnp.zeros_like(acc)

        acc[...] += _dot_nt(dzg_ref[...], wg_ref[...]) + _dot_nt(dzu_ref[...], wu_ref[...])

        @pl.when(j == nj - 1)
        def _():
            dx, dg = _norm_bwd(acc[...], x_ref[...], g_ref[...])
            dx_ref[...] = dxo_ref[...] + dx

            @pl.when(i == 0)
            def _():
                dgn_ref[...] = dg

            @pl.when(i > 0)
            def _():
                dgn_ref[...] += dg

    tok = pl.BlockSpec((tm, D), lambda i, j: (i, 0))
    chunk = pl.BlockSpec((None, tm, C), lambda i, j: (j, i, 0))
    row = pl.BlockSpec((1, D), lambda i, j: (0, 0))
    in_specs = [tok, tok, row, chunk, chunk] + _ffn_weight_specs(f, nj, D, C)[:2]
    body, in_specs, args = _with_dep(body, dep, in_specs, [dxo, x, g, dzg, dzu, ga, ga])
    return pl.pallas_call(
        body, name="ffn_bwd_dx", grid=(T // tm, nj),
        in_specs=in_specs, out_specs=[tok, row],
        out_shape=[jax.ShapeDtypeStruct((T, D), F32), jax.ShapeDtypeStruct((1, D), F32)],
        scratch_shapes=[pltpu.VMEM((tm, D), F32)],
        compiler_params=_params("arbitrary", "arbitrary"),
    )(*args)


def _ffn_dw(h, dzg, dzu, s, dy, tk, dep=None):
    T, D = h.shape
    nj, C = s.shape[0], s.shape[2]
    nk = T // tk

    def body(h_ref, dzg_ref, dzu_ref, s_ref, dy_ref, dwin_ref, dwo_ref, ag, au, ao):
        k = pl.program_id(1)

        @pl.when(k == 0)
        def _():
            ag[...] = jnp.zeros_like(ag)
            au[...] = jnp.zeros_like(au)
            ao[...] = jnp.zeros_like(ao)

        hb = h_ref[...]
        ag[...] += _dot_tn(hb, dzg_ref[...])
        au[...] += _dot_tn(hb, dzu_ref[...])
        ao[...] += _dot_tn(s_ref[...], dy_ref[...])

        @pl.when(k == nk - 1)
        def _():
            dwin_ref[0] = ag[...].astype(BF16)
            dwin_ref[1] = au[...].astype(BF16)
            dwo_ref[...] = ao[...].astype(BF16)

    tok = pl.BlockSpec((tk, D), lambda j, k: (k, 0))
    chunk = pl.BlockSpec((None, tk, C), lambda j, k: (j, k, 0))
    body, in_specs, args = _with_dep(body, dep, [tok, chunk, chunk, chunk, tok], [h, dzg, dzu, s, dy])
    dwin, dwo = pl.pallas_call(
        body, name="ffn_dw", grid=(nj, nk),
        in_specs=in_specs,
        out_specs=[pl.BlockSpec((2, None, D, C), lambda j, k: (0, j, 0, 0)),
                   pl.BlockSpec((None, C, D), lambda j, k: (j, 0, 0))],
        out_shape=[jax.ShapeDtypeStruct((2, nj, D, C), BF16), jax.ShapeDtypeStruct((nj, C, D), BF16)],
        scratch_shapes=[pltpu.VMEM((D, C), F32), pltpu.VMEM((D, C), F32), pltpu.VMEM((C, D), F32)],
        compiler_params=_params("parallel", "arbitrary"),
    )(*args)
    return dwin.reshape(2 * nj, D, C), dwo


def _matmul_tn(a, b, tn, tk):
    T, Ka = a.shape
    N = b.shape[1]
    nk = T // tk

    def body(a_ref, b_ref, o_ref, acc):
        k = pl.program_id(1)

        @pl.when(k == 0)
        def _():
            acc[...] = jnp.zeros_like(acc)

        acc[...] += _dot_tn(a_ref[...], b_ref[...])

        @pl.when(k == nk - 1)
        def _():
            o_ref[...] = acc[...].astype(BF16)

    return pl.pallas_call(
        body, name="matmul_tn", grid=(N // tn, nk),
        in_specs=[pl.BlockSpec((tk, Ka), lambda n, k: (k, 0)), pl.BlockSpec((tk, tn), lambda n, k: (k, n))],
        out_specs=pl.BlockSpec((Ka, tn), lambda n, k: (0, n)),
        out_shape=jax.ShapeDtypeStruct((Ka, N), BF16),
        scratch_shapes=[pltpu.VMEM((Ka, tn), F32)],
        compiler_params=_params("parallel", "arbitrary"),
    )(a, b)


def _qkv_fwd(x, g, w, tm):
    T, D = x.shape
    N = w.shape[1]

    def body(x_ref, g_ref, w_ref, o_ref, h_ref):
        xv = x_ref[...]
        hb = (xv * _rstd(xv) * g_ref[...]).astype(BF16)
        h_ref[...] = hb
        o_ref[...] = _dot(hb, w_ref[...])

    return pl.pallas_call(
        body, name="qkv_fwd", grid=(T // tm,),
        in_specs=[pl.BlockSpec((tm, D), lambda i: (i, 0)), pl.BlockSpec((1, D), lambda i: (0, 0)),
                  pl.BlockSpec((D, N), lambda i: (0, 0))],
        out_specs=[pl.BlockSpec((tm, N), lambda i: (i, 0)), pl.BlockSpec((tm, D), lambda i: (i, 0))],
        out_shape=[jax.ShapeDtypeStruct((T, N), F32), jax.ShapeDtypeStruct((T, D), BF16)],
        compiler_params=_params("parallel"),
    )(x, g, w)


DILS = tuple(d for _, d in DILATED)


def _spread_specs(tm, T, dtype):
    specs = [pl.BlockSpec((4, d, tm // d, PAIR), lambda i: (0, 0, i, 0)) for d in DILS]
    shapes = [jax.ShapeDtypeStruct((4, d, T // d, PAIR), dtype) for d in DILS]
    return specs, shapes


def _spread(tile, y, outs, c, dtype):
    tm = y.shape[0]
    tile[...] = y
    for out, d in zip(outs, DILS):
        for r in range(d):
            out[c, r] = tile[pl.ds(r, tm // d, stride=d), :].astype(dtype)


def _collect(tile, ins, c):
    tm = tile.shape[0]
    first = True
    for ref, d in zip(ins, DILS):
        for r in range(d):
            rows = pl.ds(r, tm // d, stride=d) if d > 1 else pl.ds(0, tm)
            part = ref[c, r].astype(F32)
            tile[rows, :] = part if first else tile[rows, :] + part
        first = False
    return tile[...]


def _attn_prep(qkv, gains2, tm):
    T = qkv.shape[0]
    scale = HEAD_DIM ** -0.5
    n = len(DILS)

    def body(qkv_ref, g_ref, qb_ref, kb_ref, vb_ref, *rest):
        outs, tile = rest[:-1], rest[-1]
        lo = _lo_mask((tm, PAIR))

        def spread(kind, c, y):
            _spread(tile, y, outs[kind * n:(kind + 1) * n], c, BF16)

        def normed(c, gi, mult):
            xv = qkv_ref[:, c * PAIR:(c + 1) * PAIR]
            r = lax.rsqrt(_half_sum(xv * xv, lo) * (1.0 / HEAD_DIM) + EPS)
            y = xv * r * g_ref[gi:gi + 1, :]
            return y * mult if mult != 1.0 else y

        def both_halves(v):
            sw = pltpu.roll(v, HEAD_DIM, 1)
            return jnp.where(lo, v, sw), jnp.where(lo, sw, v)

        for c in range(4):
            spread(0, c, normed(c, 0, scale))
            spread(1, c, normed(4 + c, 1, 1.0))
            spread(2, c, qkv_ref[:, (8 + c) * PAIR:(9 + c) * PAIR])
            qb_ref[c] = normed(12 + c, 2, scale).astype(BF16)
        k0, k1 = both_halves(normed(16, 3, 1.0))
        kb_ref[0] = k0.astype(BF16)
        kb_ref[1] = k1.astype(BF16)
        v0, v1 = both_halves(qkv_ref[:, 17 * PAIR:18 * PAIR])
        vb_ref[0] = v0.astype(BF16)
        vb_ref[1] = v1.astype(BF16)

    four = pl.BlockSpec((4, tm, PAIR), lambda i: (0, i, 0))
    two = pl.BlockSpec((2, tm, PAIR), lambda i: (0, i, 0))
    s4 = jax.ShapeDtypeStruct((4, T, PAIR), BF16)
    s2 = jax.ShapeDtypeStruct((2, T, PAIR), BF16)
    specs, shapes = _spread_specs(tm, T, BF16)
    res = pl.pallas_call(
        body, name="attn_prep", grid=(T // tm,),
        in_specs=[pl.BlockSpec((tm, qkv.shape[1]), lambda i: (i, 0)), pl.BlockSpec((4, PAIR), lambda i: (0, 0))],
        out_specs=[four, two, two] + specs * 3,
        out_shape=[s4, s2, s2] + shapes * 3,
        scratch_shapes=[pltpu.VMEM((tm, PAIR), F32)],
        compiler_params=_params("parallel"),
    )(qkv, gains2)
    qb, kb, vb = res[:3]
    per_d = [tuple(res[3 + kind * n + di].reshape(4 * d, T // d, PAIR) for kind in range(3))
             for di, d in enumerate(DILS)]
    return qb, kb, vb, per_d


def _loop_blocks(nb, body, init, per_iter):
    u = math.gcd(nb, per_iter)

    def outer(i, carry):
        for k in range(u):
            carry = body(i * u + k, carry)
        return carry

    return lax.fori_loop(0, nb // u, outer, init)


def _key_window(b, nb, L, R, W):
    start = pl.multiple_of(jnp.clip(b * BQ - R, 0, L - W), HEAD_DIM)
    return start, jnp.where(b == 0, 1, jnp.where(b == nb - 1, 2, 0))


def _stack_heads(v, lo):
    z = jnp.zeros_like(v)
    return jnp.concatenate([jnp.where(lo, v, z), jnp.where(lo, z, v)], axis=0)


def _unstack_heads(v2, lo):
    return jnp.where(lo, v2[:BQ], v2[BQ:])


def _row_vector(v, lo):
    r = lax.broadcasted_iota(jnp.int32, (BQ, PAIR), 0)
    ln = lax.broadcasted_iota(jnp.int32, (BQ, PAIR), 1)
    diag = (ln % HEAD_DIM) == (r % HEAD_DIM)
    top = jnp.sum(jnp.where(diag & (r < HEAD_DIM), v, 0.0), axis=0, keepdims=True)
    bot = jnp.sum(jnp.where(diag & (r >= HEAD_DIM), v, 0.0), axis=0, keepdims=True)
    top8, bot8 = jnp.broadcast_to(top, (8, PAIR)), jnp.broadcast_to(bot, (8, PAIR))
    lo8 = _lo_mask((8, PAIR))
    head0 = jnp.where(lo8, top8, pltpu.roll(bot8, HEAD_DIM, 1))
    head1 = jnp.where(lo8, pltpu.roll(top8, HEAD_DIM, 1), bot8)
    return jnp.concatenate([head0, head1], axis=1)[:1]


def _units_per_step(nb, pairs_per_kv):
    return max(1, 16 // nb) if pairs_per_kv == 1 else 1


def _attn_fwd(q, kp, vp, bias4, sink, R, pairs_per_kv, pairs_per_bias):
    N, L, _ = q.shape
    W = BQ + 2 * R
    nb = L // BQ
    assert L >= W and nb >= 2
    G = _units_per_step(nb, pairs_per_kv)

    def body(sink_ref, q_ref, k_ref, v_ref, bias_ref, o_ref, lse_ref):
        n = pl.program_id(0)
        lo_q = _lo_mask((BQ, PAIR))
        first = lax.broadcasted_iota(jnp.int32, (2 * BQ, 1), 0) < BQ

        def blk(f, carry):
            g, b = f // nb, f % nb
            u = n * G + g
            sk = jnp.where(first, sink_ref[2 * u], sink_ref[2 * u + 1])
            q0 = pl.multiple_of(b * BQ, BQ)
            q2 = _stack_heads(q_ref[g, pl.ds(q0, BQ), :], lo_q)
            k0, variant = _key_window(b, nb, L, R, W)
            kw = k_ref[g, pl.ds(k0, W), :]
            vw = v_ref[g, pl.ds(k0, W), :]
            s = _dot_nt(q2, kw) + bias_ref[variant]
            m = jnp.maximum(jnp.max(s, axis=1, keepdims=True), sk)
            p = jnp.exp(s - m)
            l = jnp.sum(p, axis=1, keepdims=True) + jnp.exp(sk - m)
            o2 = _dot(p.astype(BF16), vw) / l
            o_ref[g, pl.ds(q0, BQ), :] = _unstack_heads(o2, lo_q)
            lse_ref[g, pl.ds(q0, BQ), :] = _unstack_heads(jnp.broadcast_to(m + jnp.log(l), (2 * BQ, PAIR)), lo_q)
            return carry

        _loop_blocks(G * nb, blk, 0, 4)

    qspec = pl.BlockSpec((G, L, PAIR), lambda n: (n, 0, 0))
    kspec = pl.BlockSpec((G, L, PAIR), lambda n: (n // pairs_per_kv, 0, 0))
    return pl.pallas_call(
        body, name="attn_fwd", grid=(N // G,),
        in_specs=[pl.BlockSpec(memory_space=pltpu.SMEM), qspec, kspec, kspec,
                  pl.BlockSpec((None, 3, 2 * BQ, W), lambda n: (n * G // pairs_per_bias, 0, 0, 0))],
        out_specs=[qspec, qspec],
        out_shape=[jax.ShapeDtypeStruct((N, L, PAIR), F32), jax.ShapeDtypeStruct((N, L, PAIR), F32)],
        compiler_params=_params("parallel"),
    )(sink, q, kp, vp, bias4)


def _attn_bwd(q, kp, vp, bias4t, sink, o, lse, do, R, pairs_per_kv, pairs_per_bias):
    N, L, _ = q.shape
    Nk = kp.shape[0]
    Pb = bias4t.shape[0]
    W = BQ + 2 * R
    nb = L // BQ
    assert L >= W and nb >= 2
    G = _units_per_step(nb, pairs_per_kv)

    def body(sink_ref, q_ref, k_ref, v_ref, bias_ref, o_ref, lse_ref, do_ref,
             dq_ref, dk_ref, dv_ref, dbias_ref, dsink_ref, dk_acc, dv_acc):
        n = pl.program_id(0)
        lo_q = _lo_mask((BQ, PAIR))
        first = lax.broadcasted_iota(jnp.int32, (1, 2 * BQ), 1) < BQ
        dsink_ref[...] = jnp.zeros_like(dsink_ref)

        @pl.when(n % pairs_per_kv == 0)
        def _():
            dk_acc[...] = jnp.zeros_like(dk_acc)
            dv_acc[...] = jnp.zeros_like(dv_acc)

        @pl.when((n * G) % pairs_per_bias == 0)
        def _():
            dbias_ref[...] = jnp.zeros_like(dbias_ref)

        def blk(f, carry):
            g, b = f // nb, f % nb
            u = n * G + g
            sk = jnp.where(first, sink_ref[2 * u], sink_ref[2 * u + 1])
            q0 = pl.multiple_of(b * BQ, BQ)
            q2 = _stack_heads(q_ref[g, pl.ds(q0, BQ), :], lo_q)
            k0, variant = _key_window(b, nb, L, R, W)
            kw = k_ref[g, pl.ds(k0, W), :]
            vw = v_ref[g, pl.ds(k0, W), :]
            dov = do_ref[g, pl.ds(q0, BQ), :]
            lse = _row_vector(lse_ref[g, pl.ds(q0, BQ), :], lo_q)
            delta = _row_vector(_half_sum(dov.astype(F32) * o_ref[g, pl.ds(q0, BQ), :], lo_q), lo_q)
            do2 = _stack_heads(dov.astype(BF16), lo_q)
            st = _dot_nt(kw, q2) + bias_ref[variant]
            pt = jnp.exp(st - lse)
            dst = pt * (_dot_nt(vw, do2) - delta)
            dstb = dst.astype(BF16)
            dbias_ref[variant] += dst
            dk_acc[g, pl.ds(k0, W), :] += _dot(dstb, q2)
            dv_acc[g, pl.ds(k0, W), :] += _dot(pt.astype(BF16), do2)
            dq_ref[g, pl.ds(q0, BQ), :] = _unstack_heads(_dot_tn(dstb, kw), lo_q).astype(BF16)
            dsink_ref[g, pl.ds(0, 1), :] -= jnp.exp(sk - lse) * delta
            return carry

        _loop_blocks(G * nb, blk, 0, 4)
        dk_ref[...] = dk_acc[...].astype(BF16)
        dv_ref[...] = dv_acc[...].astype(BF16)

    qspec = pl.BlockSpec((G, L, PAIR), lambda n: (n, 0, 0))
    kspec = pl.BlockSpec((G, L, PAIR), lambda n: (n // pairs_per_kv, 0, 0))
    return pl.pallas_call(
        body, name="attn_bwd", grid=(N // G,),
        in_specs=[pl.BlockSpec(memory_space=pltpu.SMEM), qspec, kspec, kspec,
                  pl.BlockSpec((None, 3, W, 2 * BQ), lambda n: (n * G // pairs_per_bias, 0, 0, 0)),
                  qspec, qspec, qspec],
        out_specs=[qspec, kspec, kspec,
                   pl.BlockSpec((None, 3, W, 2 * BQ), lambda n: (n * G // pairs_per_bias, 0, 0, 0)),
                   pl.BlockSpec((G, 8, 2 * BQ), lambda n: (n, 0, 0))],
        out_shape=[jax.ShapeDtypeStruct((N, L, PAIR), BF16),
                   jax.ShapeDtypeStruct((Nk, L, PAIR), BF16),
                   jax.ShapeDtypeStruct((Nk, L, PAIR), BF16),
                   jax.ShapeDtypeStruct((Pb, 3, W, 2 * BQ), F32),
                   jax.ShapeDtypeStruct((N, 8, 2 * BQ), F32)],
        scratch_shapes=[pltpu.VMEM((G, L, PAIR), F32), pltpu.VMEM((G, L, PAIR), F32)],
        compiler_params=_params("arbitrary"),
    )(sink, q, kp, vp, bias4t, o, lse, do)


def _attn_merge(branch_outs, ob, tm):
    T = ob.shape[1]
    n = len(DILS)

    def body(*refs):
        o_in, l_in, ob_ref = refs[:n], refs[n:2 * n], refs[2 * n]
        o_out, l_out, cat_ref = refs[2 * n + 1:3 * n + 1], refs[3 * n + 1:4 * n + 1], refs[4 * n + 1]
        tiles = refs[4 * n + 2:]
        for c in range(4):
            o_nat, l_nat = [], []
            for di, d in enumerate(DILS):
                for kind, (src, dst) in enumerate(((o_in[di], o_nat), (l_in[di], l_nat))):
                    tile = tiles[2 * di + kind]
                    if d == 1:
                        dst.append(src[c, 0])
                    else:
                        for r in range(d):
                            tile[pl.ds(r, tm // d, stride=d), :] = src[c, r]
                        dst.append(tile[...])
            m = functools.reduce(jnp.maximum, l_nat)
            ws = [jnp.exp(l - m) for l in l_nat]
            z = sum(ws)
            o = sum(w * t for w, t in zip(ws, o_nat)) / z
            cat_ref[:, c * PAIR:(c + 1) * PAIR] = o.astype(BF16)
            cat_ref[:, (4 + c) * PAIR:(5 + c) * PAIR] = ob_ref[c].astype(BF16)
            _spread(tiles[0], o, o_out, c, F32)
            _spread(tiles[1], m + jnp.log(z), l_out, c, F32)

    specs, shapes = _spread_specs(tm, T, F32)
    four = pl.BlockSpec((4, tm, PAIR), lambda i: (0, i, 0))
    o_views = [o.reshape(4, d, T // d, PAIR) for (o, _), d in zip(branch_outs, DILS)]
    l_views = [l.reshape(4, d, T // d, PAIR) for (_, l), d in zip(branch_outs, DILS)]
    res = pl.pallas_call(
        body, name="attn_merge", grid=(T // tm,),
        in_specs=specs + specs + [four],
        out_specs=specs + specs + [pl.BlockSpec((tm, 8 * PAIR), lambda i: (i, 0))],
        out_shape=shapes + shapes + [jax.ShapeDtypeStruct((T, 8 * PAIR), BF16)],
        scratch_shapes=[pltpu.VMEM((tm, PAIR), F32)] * (2 * n),
        compiler_params=_params("parallel"),
    )(*o_views, *l_views, ob)
    merged = [(res[di].reshape(4 * d, T // d, PAIR), res[n + di].reshape(4 * d, T // d, PAIR))
              for di, d in enumerate(DILS)]
    return merged, res[2 * n]


def _weight_arg(w, blk):
    if blk is None:
        return pl.BlockSpec(w.shape, lambda i: (0, 0)), (lambda ref: ref[...])
    D = w.shape[2]
    return (pl.BlockSpec((N_DEV, 128, D), lambda i: (0, blk, 0)),
            lambda ref: ref[...].reshape(N_DEV * 128, D))


def _oproj_fwd(x, o_cat, w, blk, tm):
    T, D = x.shape
    wspec, wload = _weight_arg(w, blk)

    def body(x_ref, o_ref, w_ref, out_ref):
        out_ref[...] = x_ref[...] + _dot(o_ref[...], wload(w_ref))

    tok = pl.BlockSpec((tm, D), lambda i: (i, 0))
    return pl.pallas_call(
        body, name="oproj_fwd", grid=(T // tm,),
        in_specs=[tok, pl.BlockSpec((tm, o_cat.shape[1]), lambda i: (i, 0)), wspec],
        out_specs=tok, out_shape=jax.ShapeDtypeStruct((T, D), F32),
        compiler_params=_params("parallel"),
    )(x, o_cat, w)


def _oproj_bwd(dx, w, blk, tm, dep=None):
    T, D = dx.shape
    wspec, wload = _weight_arg(w, blk)

    def body(dx_ref, w_ref, dxb_ref, dob_ref, *rest):
        doa_refs, tile = rest[:-1], rest[-1]
        db = dx_ref[...].astype(BF16)
        dxb_ref[...] = db
        do = _dot_nt(db, wload(w_ref))
        for c in range(4):
            _spread(tile, do[:, c * PAIR:(c + 1) * PAIR], doa_refs, c, BF16)
            dob_ref[c] = do[:, (4 + c) * PAIR:(5 + c) * PAIR].astype(BF16)

    tok = pl.BlockSpec((tm, D), lambda i: (i, 0))
    specs, shapes = _spread_specs(tm, T, BF16)
    body, in_specs, args = _with_dep(body, dep, [tok, wspec], [dx, w])
    res = pl.pallas_call(
        body, name="oproj_bwd", grid=(T // tm,),
        in_specs=in_specs,
        out_specs=[tok, pl.BlockSpec((4, tm, PAIR), lambda i: (0, i, 0))] + specs,
        out_shape=[jax.ShapeDtypeStruct((T, D), BF16), jax.ShapeDtypeStruct((4, T, PAIR), BF16)] + shapes,
        scratch_shapes=[pltpu.VMEM((tm, PAIR), F32)],
        compiler_params=_params("parallel"),
    )(*args)
    return res[0], res[1], [t.reshape(4 * d, T // d, PAIR) for t, d in zip(res[2:], DILS)]


def _attn_post(qkv, gains2, dqa, dka, dva, dqb, dkb, dvb, tm):
    T, NQ = qkv.shape
    scale = HEAD_DIM ** -0.5

    n = len(DILS)

    def body(qkv_ref, g_ref, *rest):
        dq_refs, dk_refs, dv_refs = rest[:n], rest[n:2 * n], rest[2 * n:3 * n]
        qb_ref, kb_ref, vb_ref, out_ref, dg_ref, tile = rest[3 * n:]
        lo = _lo_mask((tm, PAIR))

        @pl.when(pl.program_id(0) == 0)
        def _():
            dg_ref[...] = jnp.zeros_like(dg_ref)

        def norm_bwd(c, gi, dy):
            xv = qkv_ref[:, c * PAIR:(c + 1) * PAIR]
            r = lax.rsqrt(_half_sum(xv * xv, lo) * (1.0 / HEAD_DIM) + EPS)
            xn = xv * r
            dg_ref[gi:gi + 1, :] += jnp.sum(dy * xn, axis=0, keepdims=True)
            dxn = dy * g_ref[gi:gi + 1, :]
            dx = r * (dxn - xn * (_half_sum(dxn * xn, lo) * (1.0 / HEAD_DIM)))
            out_ref[:, c * PAIR:(c + 1) * PAIR] = dx.astype(BF16)

        def fold(v):
            return v + pltpu.roll(v, HEAD_DIM, 1)

        for c in range(4):
            norm_bwd(c, 0, _collect(tile, dq_refs, c) * scale)
            norm_bwd(4 + c, 1, _collect(tile, dk_refs, c))
            out_ref[:, (8 + c) * PAIR:(9 + c) * PAIR] = _collect(tile, dv_refs, c).astype(BF16)
            norm_bwd(12 + c, 2, qb_ref[c].astype(F32) * scale)
        kb, vb = kb_ref[...].astype(F32), vb_ref[...].astype(F32)
        norm_bwd(16, 3, jnp.where(lo, fold(kb[0]), fold(kb[1])))
        out_ref[:, 17 * PAIR:18 * PAIR] = jnp.where(lo, fold(vb[0]), fold(vb[1])).astype(BF16)

    four = pl.BlockSpec((4, tm, PAIR), lambda i: (0, i, 0))
    two = pl.BlockSpec((2, tm, PAIR), lambda i: (0, i, 0))
    specs, _ = _spread_specs(tm, T, BF16)
    views = [t.reshape(4, d, T // d, PAIR) for group in (dqa, dka, dva) for t, d in zip(group, DILS)]
    return pl.pallas_call(
        body, name="attn_post", grid=(T // tm,),
        in_specs=[pl.BlockSpec((tm, NQ), lambda i: (i, 0)), pl.BlockSpec((4, PAIR), lambda i: (0, 0))]
        + specs * 3 + [four, two, two],
        out_specs=[pl.BlockSpec((tm, NQ), lambda i: (i, 0)), pl.BlockSpec((4, PAIR), lambda i: (0, 0))],
        out_shape=[jax.ShapeDtypeStruct((T, NQ), BF16), jax.ShapeDtypeStruct((4, PAIR), F32)],
        scratch_shapes=[pltpu.VMEM((tm, PAIR), F32)],
        compiler_params=_params("arbitrary"),
    )(qkv, gains2, *views, dqb, dkb, dvb)


def _dense_norm_bwd(dres, dz, w, blk, x, g, tm):
    T, D = x.shape
    N = dz.shape[1]
    wspec, wload = _weight_arg(w, blk)

    def body(dres_ref, dz_ref, w_ref, x_ref, g_ref, dx_ref, dgn_ref):
        i = pl.program_id(0)
        dx, dg = _norm_bwd(_dot_nt(dz_ref[...], wload(w_ref)), x_ref[...], g_ref[...])
        dx_ref[...] = dres_ref[...] + dx

        @pl.when(i == 0)
        def _():
            dgn_ref[...] = dg

        @pl.when(i > 0)
        def _():
            dgn_ref[...] += dg

    tok = pl.BlockSpec((tm, D), lambda i: (i, 0))
    row = pl.BlockSpec((1, D), lambda i: (0, 0))
    return pl.pallas_call(
        body, name="dense_norm_bwd", grid=(T // tm,),
        in_specs=[tok, pl.BlockSpec((tm, N), lambda i: (i, 0)), wspec, tok, row],
        out_specs=[tok, row],
        out_shape=[jax.ShapeDtypeStruct((T, D), F32), jax.ShapeDtypeStruct((1, D), F32)],
        compiler_params=_params("arbitrary"),
    )(dres, dz, w, x, g)


def _bias_reduce(onehot, dbm):
    Hb, K = dbm.shape

    def body(oh_ref, d_ref, out_ref):
        oh = oh_ref[...]
        d = d_ref[...]
        hi = d.astype(BF16)
        r1 = d - hi.astype(F32)
        mid = r1.astype(BF16)
        low = (r1 - mid.astype(F32)).astype(BF16)
        out_ref[...] = _dot_nt(hi, oh) + _dot_nt(mid, oh) + _dot_nt(low, oh)

    vm = pl.BlockSpec(memory_space=pltpu.VMEM)
    return pl.pallas_call(
        body, name="bias_reduce", in_specs=[vm, vm], out_specs=vm,
        out_shape=jax.ShapeDtypeStruct((Hb, N_BUCKETS), F32),
        compiler_params=pltpu.CompilerParams(vmem_limit_bytes=VMEM_LIMIT),
    )(onehot, dbm)


def _ple_fwd(x, g, wg, blk, p, wp, target, tm):
    T, D = x.shape
    P = p.shape[1]
    with_loss = target is not None
    wspec, wload = _weight_arg(wg, blk)

    def body(*refs):
        if with_loss:
            x_ref, g_ref, wg_ref, p_ref, wp_ref, t_ref, y_ref, hn_ref, gate_ref, pp_ref, pb_ref, loss_ref = refs
        else:
            x_ref, g_ref, wg_ref, p_ref, wp_ref, y_ref, hn_ref, gate_ref, pp_ref, pb_ref = refs
        i = pl.program_id(0)
        xv = x_ref[...]
        hb = (xv * _rstd(xv) * g_ref[...]).astype(BF16)
        hn_ref[...] = hb
        gate = _sigmoid(_dot(hb, wload(wg_ref)))
        pb = p_ref[...].astype(BF16)
        pb_ref[...] = pb
        pp = _dot(pb, wp_ref[...])
        gate_ref[...] = gate
        pp_ref[...] = pp
        y = xv + gate * pp
        if with_loss:
            err = y - t_ref[...]
            y_ref[...] = err * (1.0 / D)
            part = jnp.broadcast_to(0.5 * jnp.sum(jnp.sum(err * err, axis=1, keepdims=True) * (1.0 / D),
                                                  axis=0, keepdims=True), (1, 128))

            @pl.when(i == 0)
            def _():
                loss_ref[...] = part

            @pl.when(i > 0)
            def _():
                loss_ref[...] += part
        else:
            y_ref[...] = y

    tok = pl.BlockSpec((tm, D), lambda i: (i, 0))
    ptok = pl.BlockSpec((tm, P), lambda i: (i, 0))
    in_specs = [tok, pl.BlockSpec((1, D), lambda i: (0, 0)), wspec, ptok,
                pl.BlockSpec((P, D), lambda i: (0, 0))]
    out_specs = [tok, tok, tok, tok, ptok]
    out_shape = [jax.ShapeDtypeStruct((T, D), F32), jax.ShapeDtypeStruct((T, D), BF16),
                 jax.ShapeDtypeStruct((T, D), F32), jax.ShapeDtypeStruct((T, D), F32),
                 jax.ShapeDtypeStruct((T, P), BF16)]
    args = [x, g, wg, p, wp]
    if with_loss:
        in_specs.append(tok)
        out_specs.append(pl.BlockSpec((1, 128), lambda i: (0, 0)))
        out_shape.append(jax.ShapeDtypeStruct((1, 128), F32))
        args.append(target)
    return pl.pallas_call(
        body, name="ple_fwd_loss" if with_loss else "ple_fwd", grid=(T // tm,),
        in_specs=in_specs, out_specs=out_specs, out_shape=out_shape,
        compiler_params=_params("arbitrary" if with_loss else "parallel"),
    )(*args)


def _ple_bwd(dy, gate, pp, tm, dep=None):
    T, D = dy.shape

    def body(dy_ref, gate_ref, pp_ref, dgl_ref, dpp_ref):
        d = dy_ref[...]
        gt = gate_ref[...]
        dgl_ref[...] = (d * pp_ref[...] * gt * (1.0 - gt)).astype(BF16)
        dpp_ref[...] = (d * gt).astype(BF16)

    tok = pl.BlockSpec((tm, D), lambda i: (i, 0))
    body, in_specs, args = _with_dep(body, dep, [tok, tok, tok], [dy, gate, pp])
    return pl.pallas_call(
        body, name="ple_bwd", grid=(T // tm,), in_specs=in_specs, out_specs=[tok, tok],
        out_shape=[jax.ShapeDtypeStruct((T, D), BF16), jax.ShapeDtypeStruct((T, D), BF16)],
        compiler_params=_params("parallel"),
    )(*args)


def _adamw(w, g, m, v):
    shape = w.shape
    C = shape[-1]
    w2, g2, m2, v2 = (a.reshape(-1, C) for a in (w, g, m, v))
    Rn = w2.shape[0]
    tr = Rn
    for cand in (512, 352, 256):
        if Rn % cand == 0:
            tr = cand
            break
    c1 = 1.0 - ADAM_B1 ** ADAM_STEP
    c2 = 1.0 - ADAM_B2 ** ADAM_STEP

    def body(w_ref, g_ref, m_ref, v_ref, d_ref, nm_ref, nv_ref):
        gv = g_ref[...]
        mn = ADAM_B1 * m_ref[...] + (1.0 - ADAM_B1) * gv
        vn = ADAM_B2 * v_ref[...] + (1.0 - ADAM_B2) * (gv * gv)
        d_ref[...] = -ADAM_LR * ((mn / c1) / (jnp.sqrt(vn / c2) + ADAM_EPS) + ADAM_WD * w_ref[...])
        nm_ref[...] = mn
        nv_ref[...] = vn

    spec = pl.BlockSpec((tr, C), lambda i: (i, 0))
    sh = jax.ShapeDtypeStruct((Rn, C), F32)
    d, nm, nv = pl.pallas_call(
        body, name="adamw", grid=(Rn // tr,), in_specs=[spec] * 4, out_specs=[spec] * 3, out_shape=[sh] * 3,
        compiler_params=_params("parallel"),
    )(w2, g2, m2, v2)
    return d.reshape(shape), nm.reshape(shape), nv.reshape(shape)


def _my_place():
    x, y, c = lax.axis_index("x"), lax.axis_index("y"), lax.axis_index("c")
    chips = [(1 - x, y), (x, 1 - y), (1 - x, 1 - y)]
    return x, y, c, chips


def _all_gather(arrs):
    n = len(arrs)

    def body(*refs):
        x_refs, out_refs = refs[:n], refs[n:2 * n]
        send_sems, recv_sems, local_sems = refs[2 * n:]
        x, y, c, chips = _my_place()
        me, sibling = (x, y, c), (x, y, 1 - c)

        def copy(m, k, block, to, src=None):
            rows = out_refs[m].at[4 * block[0] + 2 * block[1] + block[2]]
            return pltpu.make_async_remote_copy(
                src_ref=rows if src is None else src, dst_ref=rows,
                send_sem=send_sems.at[7 * m + k], recv_sem=recv_sems.at[7 * m + k], device_id=to, device_id_type=MESH)

        mine = [pltpu.make_async_copy(x_refs[m], out_refs[m].at[4 * x + 2 * y + c], local_sems.at[m])
                for m in range(n)]
        for cp in mine:
            cp.start()
        first = []
        for m in range(n):
            first.append(copy(m, 0, me, sibling, src=x_refs[m]))
            first += [copy(m, 1 + j, me, (*chip, c), src=x_refs[m]) for j, chip in enumerate(chips)]
        for cp in first:
            cp.start()
        passed = []
        for m in range(n):
            for j, chip in enumerate(chips):
                copy(m, 1 + j, (*chip, c), me).wait_recv()
                cp = copy(m, 4 + j, (*chip, c), sibling)
                cp.start()
                passed.append(cp)
        for m in range(n):
            copy(m, 0, sibling, me).wait_recv()
            for j, chip in enumerate(chips):
                copy(m, 4 + j, (*chip, 1 - c), me).wait_recv()
        for cp in first + passed:
            cp.wait_send()
        for cp in mine:
            cp.wait()

    hbm = pl.BlockSpec(memory_space=pl.ANY)
    return pl.pallas_call(
        body, name="all_gather", in_specs=[hbm] * n, out_specs=[hbm] * n,
        out_shape=[jax.ShapeDtypeStruct((N_DEV,) + a.shape, a.dtype) for a in arrs],
        scratch_shapes=[pltpu.SemaphoreType.DMA((7 * n,)), pltpu.SemaphoreType.DMA((7 * n,)),
                        pltpu.SemaphoreType.DMA((n,))],
    )(*arrs)


def _peer(x, y, c, k):
    return (x ^ ((k >> 2) & 1), y ^ ((k >> 1) & 1), c ^ (k & 1))


HBM_SPEC = pl.BlockSpec(memory_space=pltpu.HBM)
SEM_SPEC = pl.BlockSpec(memory_space=pltpu.SEMAPHORE)


def _exchange_refs(srcs, lands, m, k, x, y, c, scatter):
    peer = _peer(x, y, c, k)
    if scatter:
        return srcs[m].at[4 * peer[0] + 2 * peer[1] + peer[2]], lands[m].at[k - 1], peer
    return srcs[m], lands[m].at[4 * x + 2 * y + c], peer


def _exchange_start(arrs, land_shapes, scatter, name):
    n = len(arrs)

    def body(*refs):
        srcs, lands = refs[:n], refs[n:2 * n]
        send_sems, recv_sems = refs[2 * n], refs[2 * n + 1]
        token = refs[-1]
        x, y, c, _ = _my_place()
        for m in range(n):
            for k in range(1, N_DEV):
                src, dst, peer = _exchange_refs(srcs, lands, m, k, x, y, c, scatter)
                pltpu.make_async_remote_copy(
                    src_ref=src, dst_ref=dst, send_sem=send_sems.at[7 * m + k - 1],
                    recv_sem=recv_sems.at[7 * m + k - 1], device_id=peer, device_id_type=MESH).start()
        token[...] = jnp.zeros_like(token)

    zones = [lax.empty(s_, a.dtype) for s_, a in zip(land_shapes, arrs)]
    outs = pl.pallas_call(
        body, name=name,
        out_shape=(pltpu.SemaphoreType.DMA((7 * n,)), pltpu.SemaphoreType.DMA((7 * n,)),
                   *[pltpu.HBM(a.shape, a.dtype) for a in arrs], *[pltpu.HBM(z.shape, z.dtype) for z in zones],
                   jax.ShapeDtypeStruct((8, 128), F32)),
        in_specs=[HBM_SPEC] * (2 * n),
        out_specs=(SEM_SPEC, SEM_SPEC, *[HBM_SPEC] * (2 * n), pl.BlockSpec(memory_space=pltpu.VMEM)),
        input_output_aliases={m: 2 + m for m in range(2 * n)},
        compiler_params=pltpu.CompilerParams(has_side_effects=pltpu.SideEffectType.DATAFLOW_SIDE_EFFECTING),
    )(*[pltpu.with_memory_space_constraint(a, pltpu.HBM) for a in arrs],
      *[pltpu.with_memory_space_constraint(z, pltpu.HBM) for z in zones])
    return outs[0], outs[1], list(outs[2:2 + n]), list(outs[2 + n:2 + 2 * n]), outs[-1]


def _exchange_wait(send_sems, recv_sems, arrs, zones, after, scatter, name):
    n = len(arrs)
    afters = list(after) if isinstance(after, (list, tuple)) else [after]

    def body(*refs):
        srcs, lands = refs[:n], refs[n:2 * n]
        send_sems, recv_sems = refs[2 * n], refs[2 * n + 1]
        x, y, c, _ = _my_place()
        for m in range(n):
            for k in range(1, N_DEV):
                src, dst, peer = _exchange_refs(srcs, lands, m, k, x, y, c, scatter)
                cp = pltpu.make_async_remote_copy(
                    src_ref=src, dst_ref=dst, send_sem=send_sems.at[7 * m + k - 1],
                    recv_sem=recv_sems.at[7 * m + k - 1], device_id=peer, device_id_type=MESH)
                cp.wait_send()
                cp.wait_recv()

    outs = pl.pallas_call(
        body, name=name,
        out_shape=tuple(pltpu.HBM(a.shape, a.dtype) for a in list(arrs) + list(zones)),
        in_specs=[HBM_SPEC] * (2 * n) + [SEM_SPEC, SEM_SPEC] + [pl.BlockSpec(memory_space=pl.ANY)] * len(afters),
        out_specs=tuple([HBM_SPEC] * (2 * n)),
        input_output_aliases={m: m for m in range(2 * n)},
        compiler_params=pltpu.CompilerParams(has_side_effects=pltpu.SideEffectType.DATAFLOW_SIDE_EFFECTING),
    )(*arrs, *zones, send_sems, recv_sems, *afters)
    return list(outs[n:])


def _sum_parts(own, parts, tr, dep=None):
    R, W = own.shape

    def body(own_ref, parts_ref, out_ref):
        acc = own_ref[...].astype(F32)
        for k in range(N_DEV - 1):
            acc = acc + parts_ref[k].astype(F32)
        out_ref[...] = acc

    in_specs = [pl.BlockSpec((tr, W), lambda i: (i, 0)), pl.BlockSpec((N_DEV - 1, tr, W), lambda i: (0, i, 0))]
    body, in_specs, args = _with_dep(body, dep, in_specs, [own, parts])
    return pl.pallas_call(
        body, name="sum_parts", grid=(R // tr,),
        in_specs=in_specs,
        out_specs=pl.BlockSpec((tr, W), lambda i: (i, 0)),
        out_shape=jax.ShapeDtypeStruct((R, W), F32),
        compiler_params=_params("parallel"),
    )(*args)


def _all_reduce_small(v, dep=None):
    Rn, Wd = v.shape

    def body(v_ref, out_ref, gat_ref, send_sems, recv_sems):
        x, y, c, _ = _my_place()
        me = 4 * x + 2 * y + c
        gat_ref[me] = v_ref[...]
        copies = []
        for k in range(1, N_DEV):
            fx, fy, fc = (k >> 2) & 1, (k >> 1) & 1, k & 1
            peer = (x ^ fx, y ^ fy, c ^ fc)
            cp = pltpu.make_async_remote_copy(
                src_ref=v_ref, dst_ref=gat_ref.at[me], send_sem=send_sems.at[k - 1], recv_sem=recv_sems.at[k - 1],
                device_id=peer, device_id_type=MESH)
            cp.start()
            copies.append(cp)
        for cp in copies:
            cp.wait_recv()
        for cp in copies:
            cp.wait_send()
        acc = gat_ref[0]
        for k in range(1, N_DEV):
            acc = acc + gat_ref[k]
        out_ref[...] = acc

    vm = pl.BlockSpec(memory_space=pltpu.VMEM)
    body, in_specs, args = _with_dep(body, dep, [vm], [v])
    return pl.pallas_call(
        body, name="all_reduce_small", in_specs=in_specs, out_specs=vm,
        out_shape=jax.ShapeDtypeStruct((Rn, Wd), F32),
        scratch_shapes=[pltpu.VMEM((N_DEV, Rn, Wd), F32), pltpu.SemaphoreType.DMA((7,)),
                        pltpu.SemaphoreType.DMA((7,))],
    )(*args)


def _t5_bucket(rel):
    half = N_BUCKETS // 2
    max_exact = half // 2
    ret = jnp.where(rel > 0, half, 0)
    n = jnp.abs(rel)
    nf = jnp.maximum(n, 1).astype(F32)
    large = max_exact + (jnp.log(nf / max_exact) / math.log(MAX_DISTANCE / max_exact)
                         * (half - max_exact)).astype(jnp.int32)
    large = jnp.minimum(large, half - 1)
    return ret + jnp.where(n < max_exact, n, large)


def _band(R, d):
    W = BQ + 2 * R
    rel = jnp.arange(W)[None, :] - R - jnp.arange(BQ)[:, None]
    return _t5_bucket(rel * d), jnp.abs(rel) <= R


def _onehot(R, d):
    bkt, in_band = _band(R, d)
    return ((bkt.reshape(1, -1) == jnp.arange(N_BUCKETS)[:, None]) & in_band.reshape(1, -1)).astype(BF16)


def _bias_expand(table_t, onehot):
    H = table_t.shape[0]
    K = onehot.shape[1]

    def body(t_ref, oh_ref, out_ref):
        oh = oh_ref[...]
        t = t_ref[...]
        hi = t.astype(BF16)
        r1 = t - hi.astype(F32)
        mid = r1.astype(BF16)
        low = (r1 - mid.astype(F32)).astype(BF16)
        marked = _dot(jnp.ones(t.shape, BF16), oh) > 0.5
        out_ref[...] = jnp.where(marked, _dot(hi, oh) + _dot(mid, oh) + _dot(low, oh), NEG)

    vm = pl.BlockSpec(memory_space=pltpu.VMEM)
    return pl.pallas_call(
        body, name="bias_expand", in_specs=[vm, vm], out_specs=vm,
        out_shape=jax.ShapeDtypeStruct((H, K), F32),
        compiler_params=pltpu.CompilerParams(vmem_limit_bytes=VMEM_LIMIT),
    )(table_t, onehot)


def _bias_matrix(table, R, d):
    return _bias_expand(table.T, _onehot(R, d)).reshape(table.shape[1], BQ, BQ + 2 * R)


def _bias_variants(base, R):
    H, _, W = base.shape
    fill = jnp.full((H, BQ, R), NEG, F32)
    first = jnp.concatenate([base[:, :, R:], fill], axis=2)
    last = jnp.concatenate([fill, base[:, :, :W - R]], axis=2)
    v = jnp.stack([base, first, last], axis=1)
    v = v.reshape(H // 2, 2, 3, BQ, W).transpose(0, 2, 1, 3, 4).reshape(H // 2, 3, 2 * BQ, W)
    return v, v.transpose(0, 1, 3, 2)


def _bias_grad(dbt, R, d):
    P, _, W, _ = dbt.shape
    dbt = dbt[:, 0].at[:, R:].add(dbt[:, 1, :W - R]).at[:, :W - R].add(dbt[:, 2, R:])
    dbm = dbt.reshape(P, W, 2, BQ).transpose(0, 2, 3, 1).reshape(2 * P, BQ * W)
    return _bias_reduce(_onehot(R, d), dbm).T


def _tile2(gain):
    return jnp.concatenate([gain, gain])


ROW_W_O, ROW_GATE, ROW_QKV, ROW_PROJ, B_ROWS = 768, 896, 1024, 1312, 1344
BLK_W_O, BLK_GATE = ROW_W_O // 128, ROW_GATE // 128


def _pack_layer(wts, i):
    a = jnp.stack([wts["ffn1_w_in"][i], wts["ffn2_w_in"][i]])
    D = a.shape[1]
    b = jnp.concatenate([
        wts["ffn1_w_out"][i], wts["ffn2_w_out"][i],
        jnp.zeros((ROW_W_O - 2 * wts["ffn1_w_out"].shape[1], D), a.dtype),
        wts["w_o"][i], wts["w_ple_gate"][i], wts["w_qkv"][i].reshape(-1, D), wts["w_ple_proj"][i].reshape(-1, D)])
    return a, b


def _unpack_layer(sums, like):
    w_in2, b1, b2, w_in1, w_out1 = sums
    n_out, n_sq = like["ffn1_w_out"].shape[1], like["w_o"].shape[1]
    out = {}
    if w_in2 is not None:
        out.update(ffn2_w_in=w_in2, ffn2_w_out=b1[:n_out], w_ple_gate=b1[n_out:n_out + n_sq],
                   w_ple_proj=b1[n_out + n_sq:].reshape(like["w_ple_proj"].shape[1:]))
    if b2 is not None:
        out.update(w_o=b2[:n_sq], w_qkv=b2[n_sq:].reshape(like["w_qkv"].shape[1:]))
    if w_in1 is not None:
        out.update(ffn1_w_in=w_in1, ffn1_w_out=w_out1)
    return out


def _col_sharded(gb, r0, r1, rows):
    return gb[:, r0:r1].reshape(N_DEV, rows, -1).transpose(1, 0, 2).reshape(rows, -1)


def _to_col_shards(g):
    rows = g.shape[0]
    return g.reshape(rows, N_DEV, -1).transpose(1, 0, 2).reshape(N_DEV, -1, 1024)


def _layer_weights(ga, gb, p_dim):
    return dict(ga=ga, gb=gb, w_qkv=_col_sharded(gb, ROW_QKV, ROW_PROJ, ga.shape[2]),
                w_proj=_col_sharded(gb, ROW_PROJ, B_ROWS, p_dim))


def _layer_fwd(x, p, w, sm, i, target, tm, biases, dep=None):
    ga, gb = w["ga"], w["gb"]
    saved = {}
    saved["x0"] = x
    x1, saved["h1"], saved["zg1"], saved["zu1"], saved["s1"] = _ffn_fwd(
        x, sm["norm_ffn1"][i][None], ga, gb, 0, 2 * tm, dep)
    saved["x1"] = x1
    qkv, saved["hm"] = _qkv_fwd(x1, sm["norm_mix"][i][None], w["w_qkv"], 2 * tm)
    saved["qkv"] = qkv
    gains2 = jnp.stack([_tile2(sm[k][i]) for k in ("q_norm_a", "k_norm_a", "q_norm_b", "k_norm_b")])
    saved["gains2"] = gains2
    qb, kb, vb, qkv_d = _attn_prep(qkv, gains2, tm)
    no_sink = jnp.full((8,), NEG, F32)
    branches = []
    outs = []
    for (R, d), bias, (qd, kd, vd) in zip(DILATED, biases[:3], qkv_d):
        sink = jnp.tile(no_sink, d)
        outs.append(_attn_fwd(qd, kd, vd, bias[0], sink, R, 1, d))
        branches.append((qd, kd, vd, bias, sink, R, d))
    bias_b = biases[3]
    sink_b = sm["sink_b"][i]
    ob, lb = _attn_fwd(qb, kb, vb, bias_b[0], sink_b, SWA_RADIUS, 2, 1)
    merged, o_cat = _attn_merge(outs, ob, tm)
    saved.update(branches=branches, b=(qb, kb, vb, bias_b, sink_b), merged=merged, ob=ob, lb=lb, o_cat=o_cat)
    x2 = _oproj_fwd(x1, o_cat, gb, BLK_W_O, 2 * tm)
    saved["x2"] = x2
    x3, saved["h2"], saved["zg2"], saved["zu2"], saved["s2"] = _ffn_fwd(
        x2, sm["norm_ffn2"][i][None], ga, gb, 1, 2 * tm)
    saved["x3"] = x3
    res = _ple_fwd(x3, sm["norm_ple"][i][None], gb, BLK_GATE, p, w["w_proj"], target, tm)
    y, saved["hp"], saved["gate"], saved["pp"], saved["pb"] = res[:5]
    loss = res[5] if target is not None else None
    return y, loss, saved


def _layer_bwd(dy, w, sm, i, sv, tm, dep=None, on_ready=None, on_small=None, on_last=None):
    ga, gb = w["ga"], w["gb"]
    gs = {}
    D = dy.shape[1]
    dgl, dpp = _ple_bwd(dy, sv["gate"], sv["pp"], tm, dep)
    d_gate = _matmul_tn(sv["hp"], dgl, D, 2 * tm)
    d_proj = _matmul_tn(sv["pb"], dpp, D, 2 * tm)
    dx3, gs["norm_ple"] = _dense_norm_bwd(dy, dgl, gb, BLK_GATE, sv["x3"], sm["norm_ple"][i][None], 2 * tm)
    dx2, dyb, dzg, dzu, gs["norm_ffn2"] = _ffn_bwd(dx3, sv["x2"], sm["norm_ffn2"][i][None], sv["zg2"], sv["zu2"],
                                                   ga, gb, 1, tm)
    dwin2, dwo2 = _ffn_dw(sv["h2"], dzg, dzu, sv["s2"], dyb, 2 * tm)
    half = dwo2.shape[1] // 2
    after_ffn2 = [dwin2, jnp.concatenate([dwo2.reshape(N_DEV, half, D), d_gate.reshape(N_DEV, -1, D),
                                          _to_col_shards(d_proj)], axis=1)]
    token = None if on_ready is None else on_ready(0, after_ffn2)
    dx2b, do_b, do_a = _oproj_bwd(dx2, gb, BLK_W_O, tm, token)
    d_wo = _matmul_tn(sv["o_cat"], dx2b, D, 2 * tm)
    dqa, dka, dva, dbias = [], [], [], []
    for (qd, kd, vd, bias, sink, R, d), (oa, la), do_d in zip(sv["branches"], sv["merged"], do_a):
        dq, dk, dv, dbm, _ = _attn_bwd(qd, kd, vd, bias[1], sink, oa, la, do_d, R, 1, d)
        dqa.append(dq)
        dka.append(dk)
        dva.append(dv)
        dbias.append(dbm)
    qb, kb, vb, bias_b, sink_b = sv["b"]
    dqb, dkb, dvb, dbm_b, dsink = _attn_bwd(qb, kb, vb, bias_b[1], sink_b, sv["ob"], sv["lb"], do_b,
                                            SWA_RADIUS, 2, 1)
    gs["rel_bias"] = dbias + [dbm_b]
    gs["sink_b"] = jnp.sum(dsink[:, 0].reshape(-1, 2, BQ), axis=2).reshape(-1)
    dqkv, dgains2 = _attn_post(sv["qkv"], sv["gains2"], dqa, dka, dva, dqb,
                               dkb, dvb, tm)
    dgains = dgains2[:, :HEAD_DIM] + dgains2[:, HEAD_DIM:]
    for k, name in enumerate(("q_norm_a", "k_norm_a", "q_norm_b", "k_norm_b")):
        gs[name] = dgains[k]
    d_qkv = _matmul_tn(sv["hm"], dqkv, dqkv.shape[1] // 2, 2 * tm)
    after_mixer = [jnp.concatenate([d_wo.reshape(N_DEV, -1, D), _to_col_shards(d_qkv)], axis=1)]
    token = None if on_ready is None else on_ready(1, after_mixer)
    dx1, gs["norm_mix"] = _dense_norm_bwd(dx2, dqkv, w["w_qkv"], None, sv["x1"], sm["norm_mix"][i][None], 2 * tm)
    g1 = sm["norm_ffn1"][i][None]
    if on_last is None:
        dx0, dyb, dzg, dzu, gs["norm_ffn1"] = _ffn_bwd(dx1, sv["x0"], g1, sv["zg1"], sv["zu1"], ga, gb, 0, tm, token)
        dwin1, dwo1 = _ffn_dw(sv["h1"], dzg, dzu, sv["s1"], dyb, 2 * tm)
        return dx0, (after_ffn2, after_mixer, [dwin1, dwo1.reshape(N_DEV, half, D)]), gs
    dyb, dzg, dzu = _ffn_bwd_dz(dx1, sv["zg1"], sv["zu1"], gb, 0, 2 * tm, token)
    dwin1, dwo1 = _ffn_dw(sv["h1"], dzg, dzu, sv["s1"], dyb, 2 * tm, on_small(gs))
    last = [dwin1, dwo1.reshape(N_DEV, half, D)]
    dx0, gs["norm_ffn1"] = _ffn_bwd_dx(dx1, sv["x0"], g1, dzg, dzu, ga, 0, 2 * tm, on_last(last))
    return dx0, (after_ffn2, after_mixer, last), gs


def _bias_matrices(rel_bias):
    biases = [_bias_variants(_bias_matrix(rel_bias[:, :8], R, d), R) for R, d in DILATED]
    biases.append(_bias_variants(_bias_matrix(rel_bias[:, 8:], SWA_RADIUS, 1), SWA_RADIUS))
    return biases


def _stack_small(per_layer):
    small = {}
    for k, v in per_layer.items():
        if k == "rel_bias":
            per_branch = [sum(parts) for parts in zip(*v.values())]
            drel_a = sum(_bias_grad(t, R, d) for t, (R, d) in zip(per_branch[:3], DILATED))
            small[k] = jnp.concatenate([drel_a, _bias_grad(per_branch[3], SWA_RADIUS, 1)], axis=1)
        else:
            small[k] = jnp.stack([v[i].reshape(-1) for i in sorted(v)])
    return small


TM = 512
SUM_TILES = (512, 512, 416, 512, 352)
LAST_GROUP = ("ffn1_w_in", "ffn1_w_out")


def _pack_small(d, extra=None):
    parts = [d[k].reshape(-1) for k in SMALL]
    if extra is not None:
        parts.append(extra.reshape(-1))
    flat = jnp.concatenate(parts)
    return jnp.pad(flat, (0, SMALL_ROWS * 128 - flat.shape[0])).reshape(SMALL_ROWS, 128)


def _unpack_small(buf, like):
    flat = buf.reshape(-1)
    out, off = {}, 0
    for k in SMALL:
        n = like[k].size
        out[k] = flat[off:off + n].reshape(like[k].shape)
        off += n
    return out, flat[off]


def kernel(x, p, rel_bias, norm_ffn1, ffn1_w_in, ffn1_w_out, norm_mix, w_qkv, q_norm_a, k_norm_a, q_norm_b, k_norm_b, sink_b, w_o, norm_ffn2, ffn2_w_in, ffn2_w_out, norm_ple, w_ple_gate, w_ple_proj, loss_target, m_rel_bias, m_norm_ffn1, m_ffn1_w_in, m_ffn1_w_out, m_norm_mix, m_w_qkv, m_q_norm_a, m_k_norm_a, m_q_norm_b, m_k_norm_b, m_sink_b, m_w_o, m_norm_ffn2, m_ffn2_w_in, m_ffn2_w_out, m_norm_ple, m_w_ple_gate, m_w_ple_proj, v_rel_bias, v_norm_ffn1, v_ffn1_w_in, v_ffn1_w_out, v_norm_mix, v_w_qkv, v_q_norm_a, v_k_norm_a, v_q_norm_b, v_k_norm_b, v_sink_b, v_w_o, v_norm_ffn2, v_ffn2_w_in, v_ffn2_w_out, v_norm_ple, v_w_ple_gate, v_w_ple_proj):
    wts = dict(rel_bias=rel_bias, norm_ffn1=norm_ffn1, ffn1_w_in=ffn1_w_in, ffn1_w_out=ffn1_w_out,
               norm_mix=norm_mix, w_qkv=w_qkv, q_norm_a=q_norm_a, k_norm_a=k_norm_a, q_norm_b=q_norm_b,
               k_norm_b=k_norm_b, sink_b=sink_b, w_o=w_o, norm_ffn2=norm_ffn2, ffn2_w_in=ffn2_w_in,
               ffn2_w_out=ffn2_w_out, norm_ple=norm_ple, w_ple_gate=w_ple_gate, w_ple_proj=w_ple_proj)
    mom = dict(rel_bias=m_rel_bias, norm_ffn1=m_norm_ffn1, ffn1_w_in=m_ffn1_w_in, ffn1_w_out=m_ffn1_w_out,
               norm_mix=m_norm_mix, w_qkv=m_w_qkv, q_norm_a=m_q_norm_a, k_norm_a=m_k_norm_a, q_norm_b=m_q_norm_b,
               k_norm_b=m_k_norm_b, sink_b=m_sink_b, w_o=m_w_o, norm_ffn2=m_norm_ffn2, ffn2_w_in=m_ffn2_w_in,
               ffn2_w_out=m_ffn2_w_out, norm_ple=m_norm_ple, w_ple_gate=m_w_ple_gate, w_ple_proj=m_w_ple_proj)
    var = dict(rel_bias=v_rel_bias, norm_ffn1=v_norm_ffn1, ffn1_w_in=v_ffn1_w_in, ffn1_w_out=v_ffn1_w_out,
               norm_mix=v_norm_mix, w_qkv=v_w_qkv, q_norm_a=v_q_norm_a, k_norm_a=v_k_norm_a, q_norm_b=v_q_norm_b,
               k_norm_b=v_k_norm_b, sink_b=v_sink_b, w_o=v_w_o, norm_ffn2=v_norm_ffn2, ffn2_w_in=v_ffn2_w_in,
               ffn2_w_out=v_ffn2_w_out, norm_ple=v_norm_ple, w_ple_gate=v_w_ple_gate, w_ple_proj=v_w_ple_proj)
    sm = {k: wts[k] for k in SMALL}
    p_dim = p.shape[-1]
    me = 4 * lax.axis_index("x") + 2 * lax.axis_index("y") + lax.axis_index("c")
    packed = []
    for i in range(2):
        a, b = _pack_layer(wts, i)
        packed.append([a.reshape(-1, a.shape[-1]).astype(BF16), b.astype(BF16)])
    a_shape = (2, ffn1_w_in.shape[1], ffn1_w_in.shape[2])

    def weights_of(zones):
        return _layer_weights(zones[0].reshape((N_DEV,) + a_shape), zones[1], p_dim)

    w0 = weights_of(_all_gather(packed[0]))
    zone_shapes = [(N_DEV,) + t.shape for t in packed[1]]
    ssem, rsem, thru, zones, token = _exchange_start(packed[1], zone_shapes, False, "gather_start")
    biases = _bias_matrices(rel_bias)
    x1, _, sv0 = _layer_fwd(x[0], p[0, 0], w0, sm, 0, None, TM, biases, dep=token)
    zones = _exchange_wait(ssem, rsem, thru, zones, x1, False, "gather_wait")
    w1 = weights_of([lax.dynamic_update_index_in_dim(z, t, me, 0) for z, t in zip(zones, packed[1])])
    dy, loss, sv1 = _layer_fwd(x1, p[1, 0], w1, sm, 1, loss_target[0], TM, biases)

    def slots_for(arrs):
        return [(N_DEV - 1,) + t.shape[1:] for t in arrs]

    held1, held = {}, {}

    def on_ready1(stage, group):
        held1[stage] = _exchange_start(group, slots_for(group), True, f"scatter1_start_{stage}")
        return held1[stage][4]

    dx1, groups1, gs1 = _layer_bwd(dy, w1, sm, 1, sv1, TM, on_ready=on_ready1)
    on_ready1(2, groups1[2])
    g1 = groups1[0] + groups1[1] + groups1[2]

    def on_ready(stage, group):
        if stage == 1:
            held["slots1"] = [t for st in (0, 1, 2)
                              for t in _exchange_wait(*held1[st][:4], group[0], True, f"scatter1_wait_{st}")]
        held[stage] = _exchange_start(group, slots_for(group), True, f"scatter_start_{stage}")
        return held[stage][4]

    def on_small(gs0):
        part = dict(gs0, norm_ffn1=jnp.zeros_like(gs1["norm_ffn1"]))
        gsmall = _stack_small({k: {0: part[k], 1: gs1[k]} for k in part})
        held["small"] = _all_reduce_small(_pack_small(gsmall, loss[0, :1]))
        return held["small"]

    def on_last(group):
        held["last"] = _exchange_start(group, slots_for(group), True, "scatter_start_2")
        return held["last"][4]

    dx, groups0, gs0 = _layer_bwd(dx1, w0, sm, 0, sv0, TM, dep=held1[2][4], on_ready=on_ready, on_small=on_small,
                                  on_last=on_last)
    last = groups0[2]
    slots0 = [_exchange_wait(*held[stage][:4], last[0], True, f"scatter_wait_{stage}") for stage in (0, 1)]

    def summed(arrs, slots, tiles, dep=None):
        return [_sum_parts(lax.dynamic_index_in_dim(t, me, 0, keepdims=False), s_, tr, dep)
                for t, s_, tr in zip(arrs, slots, tiles)]

    cover = held["last"][4]
    r1 = summed(g1, held["slots1"], SUM_TILES, cover)
    r0 = summed(groups0[0], slots0[0], SUM_TILES[:2], cover) + summed(groups0[1], slots0[1], SUM_TILES[2:3], cover)

    def update(names, layers):
        for k in names:
            grads[k] = jnp.stack([layers[0][k], layers[1][k]])
            delta[k], new_m[k], new_v[k] = _adamw(wts[k], grads[k], mom[k], var[k])

    grads, delta, new_m, new_v = {}, {}, {}, {}
    layer1 = _unpack_layer(r1, wts)
    update([k for k in BIG if k not in LAST_GROUP], [_unpack_layer(r0 + [None, None], wts), layer1])

    cover_done = [dx] + [delta[k] for k in BIG if k not in LAST_GROUP]
    slots_last = _exchange_wait(*held["last"][:4], cover_done, True, "scatter_wait_2")
    update(LAST_GROUP, [_unpack_layer([None, None, None] + summed(last, slots_last, SUM_TILES[3:]), wts), layer1])
    late = _all_reduce_small(gs0["norm_ffn1"].reshape(-1, 128), dep=slots_last[0])
    small_sum, loss_sum = _unpack_small(held["small"], sm)
    small_sum["norm_ffn1"] = small_sum["norm_ffn1"].at[0].add(late.reshape(-1))
    grads.update(small_sum)
    zeros = {k: jnp.zeros_like(wts[k]) for k in SMALL}
    ds, ms, vs = _adamw(_pack_small(wts), _pack_small(small_sum), _pack_small(mom), _pack_small(var))
    for packed, dst in ((ds, delta), (ms, new_m), (vs, new_v)):
        dst.update(_unpack_small(packed, zeros)[0])

    return (loss_sum, dx[None], *[grads[k] for k in WEIGHTS], *[delta[k] for k in WEIGHTS],
            *[new_m[k] for k in WEIGHTS], *[new_v[k] for k in WEIGHTS])
```

```python
import functools
import math

import jax
import jax.numpy as jnp
from jax import lax
from jax.experimental import pallas as pl
from jax.experimental.pallas import tpu as pltpu

F32 = jnp.float32
BF16 = jnp.bfloat16

N_DEV = 8
HEAD_DIM = 64
PAIR = 2 * HEAD_DIM
BQ = 128
N_BUCKETS = 32
MAX_DISTANCE = 1024
DILATED = ((64, 1), (64, 4), (64, 16))
SWA_RADIUS = 128
EPS = 1e-6
NEG = -1e30
ADAM_LR, ADAM_B1, ADAM_B2, ADAM_EPS, ADAM_WD, ADAM_STEP = 0.001, 0.9, 0.999, 1e-08, 0.01, 10
VMEM_LIMIT = 56 * 1024 * 1024
MESH = pl.DeviceIdType.MESH

BIG = ("ffn1_w_in", "ffn1_w_out", "w_qkv", "w_o", "ffn2_w_in", "ffn2_w_out", "w_ple_gate", "w_ple_proj")
SMALL = ("rel_bias", "norm_ffn1", "norm_mix", "q_norm_a", "k_norm_a", "q_norm_b", "k_norm_b", "sink_b",
         "norm_ffn2", "norm_ple")
WEIGHTS = ("rel_bias", "norm_ffn1", "ffn1_w_in", "ffn1_w_out", "norm_mix", "w_qkv", "q_norm_a", "k_norm_a",
           "q_norm_b", "k_norm_b", "sink_b", "w_o", "norm_ffn2", "ffn2_w_in", "ffn2_w_out", "norm_ple",
           "w_ple_gate", "w_ple_proj")
SMALL_ROWS = 96


def _params(*sem):
    return pltpu.CompilerParams(dimension_semantics=sem, vmem_limit_bytes=VMEM_LIMIT)


def _dot(a, b):
    return jnp.dot(a, b, preferred_element_type=F32)


def _dot_nt(a, b):
    return lax.dot_general(a, b, (((1,), (1,)), ((), ())), preferred_element_type=F32)


def _dot_tn(a, b):
    return lax.dot_general(a, b, (((0,), (0,)), ((), ())), preferred_element_type=F32)


def _sigmoid(x):
    return 1.0 / (1.0 + jnp.exp(-x))


def _rstd(xv):
    return lax.rsqrt(jnp.mean(xv * xv, axis=-1, keepdims=True) + EPS)


def _norm_bwd(dh, xv, gv):
    r = _rstd(xv)
    xn = xv * r
    dg = jnp.sum(dh * xn, axis=0, keepdims=True)
    dxn = dh * gv
    dx = r * (dxn - xn * jnp.mean(dxn * xn, axis=-1, keepdims=True))
    return dx, dg


def _lo_mask(shape):
    return lax.broadcasted_iota(jnp.int32, shape, len(shape) - 1) < HEAD_DIM


def _half_sum(t, lo):
    s0 = jnp.sum(jnp.where(lo, t, 0.0), axis=1, keepdims=True)
    s1 = jnp.sum(jnp.where(lo, 0.0, t), axis=1, keepdims=True)
    return jnp.where(lo, s0, s1)


FFN_PARTS = 2


def _ffn_weight_specs(f, nj, D, C):
    return [pl.BlockSpec((None, None, D, C), lambda i, j: (j, f, 0, 0)),
            pl.BlockSpec((None, None, D, C), lambda i, j: (j + nj, f, 0, 0)),
            pl.BlockSpec((2, C // 2, D), lambda i, j: (j, f, 0))]


def _with_dep(body, dep, in_specs, args):
    if dep is None:
        return body, in_specs, args

    def body_after(dep_ref, *refs):
        body(*refs)

    return body_after, [pl.BlockSpec(memory_space=pl.ANY)] + in_specs, [dep] + args


def _ffn_fwd(x, g, ga, gb, f, tm, dep=None):
    T, D = x.shape
    nj, C = ga.shape[0] // 2, ga.shape[3]

    def body(x_ref, g_ref, wg_ref, wu_ref, wo_ref, xo_ref, h_ref, zg_ref, zu_ref, s_ref, h_scr, acc):
        j = pl.program_id(1)

        @pl.when(j == 0)
        def _():
            xv = x_ref[...]
            hb = (xv * _rstd(xv) * g_ref[...]).astype(BF16)
            h_scr[...] = hb
            h_ref[...] = hb
            acc[...] = jnp.zeros_like(acc)

        wo = wo_ref[...].reshape(C, D)
        for part in range(FFN_PARTS):
            sl = pl.ds(part * (tm // FFN_PARTS), tm // FFN_PARTS)
            hb = h_scr[sl, :]
            gt = _dot(hb, wg_ref[...])
            up = _dot(hb, wu_ref[...])
            s = (gt * _sigmoid(gt) * up).astype(BF16)
            zg_ref[sl, :] = gt.astype(BF16)
            zu_ref[sl, :] = up.astype(BF16)
            s_ref[sl, :] = s
            acc[sl, :] += _dot(s, wo)

        @pl.when(j == nj - 1)
        def _():
            xo_ref[...] = x_ref[...] + 0.5 * acc[...]

    tok = pl.BlockSpec((tm, D), lambda i, j: (i, 0))
    chunk = pl.BlockSpec((None, tm, C), lambda i, j: (j, i, 0))
    in_specs = [tok, pl.BlockSpec((1, D), lambda i, j: (0, 0))] + _ffn_weight_specs(f, nj, D, C)
    body, in_specs, args = _with_dep(body, dep, in_specs, [x, g, ga, ga, gb])
    return pl.pallas_call(
        body, name="ffn_fwd", grid=(T // tm, nj),
        in_specs=in_specs,
        out_specs=[tok, tok, chunk, chunk, chunk],
        out_shape=[jax.ShapeDtypeStruct((T, D), F32), jax.ShapeDtypeStruct((T, D), BF16),
                   jax.ShapeDtypeStruct((nj, T, C), BF16), jax.ShapeDtypeStruct((nj, T, C), BF16),
                   jax.ShapeDtypeStruct((nj, T, C), BF16)],
        scratch_shapes=[pltpu.VMEM((tm, D), BF16), pltpu.VMEM((tm, D), F32)],
        compiler_params=_params("parallel", "arbitrary"),
    )(*args)


def _ffn_bwd(dxo, x, g, zg, zu, ga, gb, f, tm, dep=None):
    T, D = x.shape
    nj, C = ga.shape[0] // 2, ga.shape[3]

    def body(dxo_ref, x_ref, g_ref, zg_ref, zu_ref, wg_ref, wu_ref, wo_ref,
             dx_ref, dy_ref, dzg_ref, dzu_ref, dgn_ref, dy_scr, acc):
        i, j = pl.program_id(0), pl.program_id(1)

        @pl.when(j == 0)
        def _():
            dyb = (0.5 * dxo_ref[...]).astype(BF16)
            dy_scr[...] = dyb
            dy_ref[...] = dyb
            acc[...] = jnp.zeros_like(acc)

        wo = wo_ref[...].reshape(C, D)
        for part in range(FFN_PARTS):
            sl = pl.ds(part * (tm // FFN_PARTS), tm // FFN_PARTS)
            ds = _dot_nt(dy_scr[sl, :], wo)
            gt = zg_ref[sl, :].astype(F32)
            up = zu_ref[sl, :].astype(F32)
            sg = _sigmoid(gt)
            dgt = (ds * up * (sg * (1.0 + gt * (1.0 - sg)))).astype(BF16)
            dup = (ds * (gt * sg)).astype(BF16)
            dzg_ref[sl, :] = dgt
            dzu_ref[sl, :] = dup
            acc[sl, :] += _dot_nt(dgt, wg_ref[...]) + _dot_nt(dup, wu_ref[...])

        @pl.when(j == nj - 1)
        def _():
            dx, dg = _norm_bwd(acc[...], x_ref[...], g_ref[...])
            dx_ref[...] = dxo_ref[...] + dx

            @pl.when(i == 0)
            def _():
                dgn_ref[...] = dg

            @pl.when(i > 0)
            def _():
                dgn_ref[...] += dg

    tok = pl.BlockSpec((tm, D), lambda i, j: (i, 0))
    chunk = pl.BlockSpec((None, tm, C), lambda i, j: (j, i, 0))
    row = pl.BlockSpec((1, D), lambda i, j: (0, 0))
    in_specs = [tok, tok, row, chunk, chunk] + _ffn_weight_specs(f, nj, D, C)
    body, in_specs, args = _with_dep(body, dep, in_specs, [dxo, x, g, zg, zu, ga, ga, gb])
    return pl.pallas_call(
        body, name="ffn_bwd", grid=(T // tm, nj),
        in_specs=in_specs,
        out_specs=[tok, tok, chunk, chunk, row],
        out_shape=[jax.ShapeDtypeStruct((T, D), F32), jax.ShapeDtypeStruct((T, D), BF16),
                   jax.ShapeDtypeStruct((nj, T, C), BF16), jax.ShapeDtypeStruct((nj, T, C), BF16),
                   jax.ShapeDtypeStruct((1, D), F32)],
        scratch_shapes=[pltpu.VMEM((tm, D), BF16), pltpu.VMEM((tm, D), F32)],
        compiler_params=_params("arbitrary", "arbitrary"),
    )(*args)


def _ffn_bwd_dz(dxo, zg, zu, gb, f, tm, dep=None):
    T, D = dxo.shape
    nj, C = zg.shape[0], zg.shape[2]

    def body(dxo_ref, zg_ref, zu_ref, wo_ref, dy_ref, dzg_ref, dzu_ref, dy_scr):
        @pl.when(pl.program_id(1) == 0)
        def _():
            dyb = (0.5 * dxo_ref[...]).astype(BF16)
            dy_scr[...] = dyb
            dy_ref[...] = dyb

        wo = wo_ref[...].reshape(C, D)
        for part in range(FFN_PARTS):
            sl = pl.ds(part * (tm // FFN_PARTS), tm // FFN_PARTS)
            ds = _dot_nt(dy_scr[sl, :], wo)
            gt = zg_ref[sl, :].astype(F32)
            up = zu_ref[sl, :].astype(F32)
            sg = _sigmoid(gt)
            dzg_ref[sl, :] = (ds * up * (sg * (1.0 + gt * (1.0 - sg)))).astype(BF16)
            dzu_ref[sl, :] = (ds * (gt * sg)).astype(BF16)

    tok = pl.BlockSpec((tm, D), lambda i, j: (i, 0))
    chunk = pl.BlockSpec((None, tm, C), lambda i, j: (j, i, 0))
    in_specs = [tok, chunk, chunk, _ffn_weight_specs(f, nj, D, C)[2]]
    body, in_specs, args = _with_dep(body, dep, in_specs, [dxo, zg, zu, gb])
    return pl.pallas_call(
        body, name="ffn_bwd_dz", grid=(T // tm, nj),
        in_specs=in_specs, out_specs=[tok, chunk, chunk],
        out_shape=[jax.ShapeDtypeStruct((T, D), BF16), jax.ShapeDtypeStruct((nj, T, C), BF16),
                   jax.ShapeDtypeStruct((nj, T, C), BF16)],
        scratch_shapes=[pltpu.VMEM((tm, D), BF16)],
        compiler_params=_params("parallel", "arbitrary"),
    )(*args)


def _ffn_bwd_dx(dxo, x, g, dzg, dzu, ga, f, tm, dep=None):
    T, D = x.shape
    nj, C = ga.shape[0] // 2, ga.shape[3]

    def body(dxo_ref, x_ref, g_ref, dzg_ref, dzu_ref, wg_ref, wu_ref, dx_ref, dgn_ref, acc):
        i, j = pl.program_id(0), pl.program_id(1)

        @pl.when(j == 0)
        def _():
            acc[...] = jnp.zeros_like(acc)

        acc[...] += _dot_nt(dzg_ref[...], wg_ref[...]) + _dot_nt(dzu_ref[...], wu_ref[...])

        @pl.when(j == nj - 1)
        def _():
            dx, dg = _norm_bwd(acc[...], x_ref[...], g_ref[...])
            dx_ref[...] = dxo_ref[...] + dx

            @pl.when(i == 0)
            def _():
                dgn_ref[...] = dg

            @pl.when(i > 0)
            def _():
                dgn_ref[...] += dg

    tok = pl.BlockSpec((tm, D), lambda i, j: (i, 0))
    chunk = pl.BlockSpec((None, tm, C), lambda i, j: (j, i, 0))
    row = pl.BlockSpec((1, D), lambda i, j: (0, 0))
    in_specs = [tok, tok, row, chunk, chunk] + _ffn_weight_specs(f, nj, D, C)[:2]
    body, in_specs, args = _with_dep(body, dep, in_specs, [dxo, x, g, dzg, dzu, ga, ga])
    return pl.pallas_call(
        body, name="ffn_bwd_dx", grid=(T // tm, nj),
        in_specs=in_specs, out_specs=[tok, row],
        out_shape=[jax.ShapeDtypeStruct((T, D), F32), jax.ShapeDtypeStruct((1, D), F32)],
        scratch_shapes=[pltpu.VMEM((tm, D), F32)],
        compiler_params=_params("arbitrary", "arbitrary"),
    )(*args)


def _ffn_dw(h, dzg, dzu, s, dy, tk, dep=None):
    T, D = h.shape
    nj, C = s.shape[0], s.shape[2]
    nk = T // tk

    def body(h_ref, dzg_ref, dzu_ref, s_ref, dy_ref, dwin_ref, dwo_ref, ag, au, ao):
        k = pl.program_id(1)

        @pl.when(k == 0)
        def _():
            ag[...] = jnp.zeros_like(ag)
            au[...] = jnp.zeros_like(au)
            ao[...] = jnp.zeros_like(ao)

        hb = h_ref[...]
        ag[...] += _dot_tn(hb, dzg_ref[...])
        au[...] += _dot_tn(hb, dzu_ref[...])
        ao[...] += _dot_tn(s_ref[...], dy_ref[...])

        @pl.when(k == nk - 1)
        def _():
            dwin_ref[0] = ag[...].astype(BF16)
            dwin_ref[1] = au[...].astype(BF16)
            dwo_ref[...] = ao[...].astype(BF16)

    tok = pl.BlockSpec((tk, D), lambda j, k: (k, 0))
    chunk = pl.BlockSpec((None, tk, C), lambda j, k: (j, k, 0))
    body, in_specs, args = _with_dep(body, dep, [tok, chunk, chunk, chunk, tok], [h, dzg, dzu, s, dy])
    dwin, dwo = pl.pallas_call(
        body, name="ffn_dw", grid=(nj, nk),
        in_specs=in_specs,
        out_specs=[pl.BlockSpec((2, None, D, C), lambda j, k: (0, j, 0, 0)),
                   pl.BlockSpec((None, C, D), lambda j, k: (j, 0, 0))],
        out_shape=[jax.ShapeDtypeStruct((2, nj, D, C), BF16), jax.ShapeDtypeStruct((nj, C, D), BF16)],
        scratch_shapes=[pltpu.VMEM((D, C), F32), pltpu.VMEM((D, C), F32), pltpu.VMEM((C, D), F32)],
        compiler_params=_params("parallel", "arbitrary"),
    )(*args)
    return dwin.reshape(2 * nj, D, C), dwo


def _matmul_tn(a, b, tn, tk):
    T, Ka = a.shape
    N = b.shape[1]
    nk = T // tk

    def body(a_ref, b_ref, o_ref, acc):
        k = pl.program_id(1)

        @pl.when(k == 0)
        def _():
            acc[...] = jnp.zeros_like(acc)

        acc[...] += _dot_tn(a_ref[...], b_ref[...])

        @pl.when(k == nk - 1)
        def _():
            o_ref[...] = acc[...].astype(BF16)

    return pl.pallas_call(
        body, name="matmul_tn", grid=(N // tn, nk),
        in_specs=[pl.BlockSpec((tk, Ka), lambda n, k: (k, 0)), pl.BlockSpec((tk, tn), lambda n, k: (k, n))],
        out_specs=pl.BlockSpec((Ka, tn), lambda n, k: (0, n)),
        out_shape=jax.ShapeDtypeStruct((Ka, N), BF16),
        scratch_shapes=[pltpu.VMEM((Ka, tn), F32)],
        compiler_params=_params("parallel", "arbitrary"),
    )(a, b)


def _qkv_fwd(x, g, w, tm):
    T, D = x.shape
    N = w.shape[1]

    def body(x_ref, g_ref, w_ref, o_ref, h_ref):
        xv = x_ref[...]
        hb = (xv * _rstd(xv) * g_ref[...]).astype(BF16)
        h_ref[...] = hb
        o_ref[...] = _dot(hb, w_ref[...])

    return pl.pallas_call(
        body, name="qkv_fwd", grid=(T // tm,),
        in_specs=[pl.BlockSpec((tm, D), lambda i: (i, 0)), pl.BlockSpec((1, D), lambda i: (0, 0)),
                  pl.BlockSpec((D, N), lambda i: (0, 0))],
        out_specs=[pl.BlockSpec((tm, N), lambda i: (i, 0)), pl.BlockSpec((tm, D), lambda i: (i, 0))],
        out_shape=[jax.ShapeDtypeStruct((T, N), F32), jax.ShapeDtypeStruct((T, D), BF16)],
        compiler_params=_params("parallel"),
    )(x, g, w)


DILS = tuple(d for _, d in DILATED)


def _spread_specs(tm, T, dtype):
    specs = [pl.BlockSpec((4, d, tm // d, PAIR), lambda i: (0, 0, i, 0)) for d in DILS]
    shapes = [jax.ShapeDtypeStruct((4, d, T // d, PAIR), dtype) for d in DILS]
    return specs, shapes


def _spread(tile, y, outs, c, dtype):
    tm = y.shape[0]
    tile[...] = y
    for out, d in zip(outs, DILS):
        for r in range(d):
            out[c, r] = tile[pl.ds(r, tm // d, stride=d), :].astype(dtype)


def _collect(tile, ins, c):
    tm = tile.shape[0]
    first = True
    for ref, d in zip(ins, DILS):
        for r in range(d):
            rows = pl.ds(r, tm // d, stride=d) if d > 1 else pl.ds(0, tm)
            part = ref[c, r].astype(F32)
            tile[rows, :] = part if first else tile[rows, :] + part
        first = False
    return tile[...]


def _attn_prep(qkv, gains2, tm):
    T = qkv.shape[0]
    scale = HEAD_DIM ** -0.5
    n = len(DILS)

    def body(qkv_ref, g_ref, qb_ref, kb_ref, vb_ref, *rest):
        outs, tile = rest[:-1], rest[-1]
        lo = _lo_mask((tm, PAIR))

        def spread(kind, c, y):
            _spread(tile, y, outs[kind * n:(kind + 1) * n], c, BF16)

        def normed(c, gi, mult):
            xv = qkv_ref[:, c * PAIR:(c + 1) * PAIR]
            r = lax.rsqrt(_half_sum(xv * xv, lo) * (1.0 / HEAD_DIM) + EPS)
            y = xv * r * g_ref[gi:gi + 1, :]
            return y * mult if mult != 1.0 else y

        def both_halves(v):
            sw = pltpu.roll(v, HEAD_DIM, 1)
            return jnp.where(lo, v, sw), jnp.where(lo, sw, v)

        for c in range(4):
            spread(0, c, normed(c, 0, scale))
            spread(1, c, normed(4 + c, 1, 1.0))
            spread(2, c, qkv_ref[:, (8 + c) * PAIR:(9 + c) * PAIR])
            qb_ref[c] = normed(12 + c, 2, scale).astype(BF16)
        k0, k1 = both_halves(normed(16, 3, 1.0))
        kb_ref[0] = k0.astype(BF16)
        kb_ref[1] = k1.astype(BF16)
        v0, v1 = both_halves(qkv_ref[:, 17 * PAIR:18 * PAIR])
        vb_ref[0] = v0.astype(BF16)
        vb_ref[1] = v1.astype(BF16)

    four = pl.BlockSpec((4, tm, PAIR), lambda i: (0, i, 0))
    two = pl.BlockSpec((2, tm, PAIR), lambda i: (0, i, 0))
    s4 = jax.ShapeDtypeStruct((4, T, PAIR), BF16)
    s2 = jax.ShapeDtypeStruct((2, T, PAIR), BF16)
    specs, shapes = _spread_specs(tm, T, BF16)
    res = pl.pallas_call(
        body, name="attn_prep", grid=(T // tm,),
        in_specs=[pl.BlockSpec((tm, qkv.shape[1]), lambda i: (i, 0)), pl.BlockSpec((4, PAIR), lambda i: (0, 0))],
        out_specs=[four, two, two] + specs * 3,
        out_shape=[s4, s2, s2] + shapes * 3,
        scratch_shapes=[pltpu.VMEM((tm, PAIR), F32)],
        compiler_params=_params("parallel"),
    )(qkv, gains2)
    qb, kb, vb = res[:3]
    per_d = [tuple(res[3 + kind * n + di].reshape(4 * d, T // d, PAIR) for kind in range(3))
             for di, d in enumerate(DILS)]
    return qb, kb, vb, per_d


def _loop_blocks(nb, body, init, per_iter):
    u = math.gcd(nb, per_iter)

    def outer(i, carry):
        for k in range(u):
            carry = body(i * u + k, carry)
        return carry

    return lax.fori_loop(0, nb // u, outer, init)


def _key_window(b, nb, L, R, W):
    start = pl.multiple_of(jnp.clip(b * BQ - R, 0, L - W), HEAD_DIM)
    return start, jnp.where(b == 0, 1, jnp.where(b == nb - 1, 2, 0))


def _stack_heads(v, lo):
    z = jnp.zeros_like(v)
    return jnp.concatenate([jnp.where(lo, v, z), jnp.where(lo, z, v)], axis=0)


def _unstack_heads(v2, lo):
    return jnp.where(lo, v2[:BQ], v2[BQ:])


def _row_vector(v, lo):
    r = lax.broadcasted_iota(jnp.int32, (BQ, PAIR), 0)
    ln = lax.broadcasted_iota(jnp.int32, (BQ, PAIR), 1)
    diag = (ln % HEAD_DIM) == (r % HEAD_DIM)
    top = jnp.sum(jnp.where(diag & (r < HEAD_DIM), v, 0.0), axis=0, keepdims=True)
    bot = jnp.sum(jnp.where(diag & (r >= HEAD_DIM), v, 0.0), axis=0, keepdims=True)
    top8, bot8 = jnp.broadcast_to(top, (8, PAIR)), jnp.broadcast_to(bot, (8, PAIR))
    lo8 = _lo_mask((8, PAIR))
    head0 = jnp.where(lo8, top8, pltpu.roll(bot8, HEAD_DIM, 1))
    head1 = jnp.where(lo8, pltpu.roll(top8, HEAD_DIM, 1), bot8)
    return jnp.concatenate([head0, head1], axis=1)[:1]


def _units_per_step(nb, pairs_per_kv):
    return max(1, 16 // nb) if pairs_per_kv == 1 else 1


def _attn_fwd(q, kp, vp, bias4, sink, R, pairs_per_kv, pairs_per_bias):
    N, L, _ = q.shape
    W = BQ + 2 * R
    nb = L // BQ
    assert L >= W and nb >= 2
    G = _units_per_step(nb, pairs_per_kv)

    def body(sink_ref, q_ref, k_ref, v_ref, bias_ref, o_ref, lse_ref):
        n = pl.program_id(0)
        lo_q = _lo_mask((BQ, PAIR))
        first = lax.broadcasted_iota(jnp.int32, (2 * BQ, 1), 0) < BQ

        def blk(f, carry):
            g, b = f // nb, f % nb
            u = n * G + g
            sk = jnp.where(first, sink_ref[2 * u], sink_ref[2 * u + 1])
            q0 = pl.multiple_of(b * BQ, BQ)
            q2 = _stack_heads(q_ref[g, pl.ds(q0, BQ), :], lo_q)
            k0, variant = _key_window(b, nb, L, R, W)
            kw = k_ref[g, pl.ds(k0, W), :]
            vw = v_ref[g, pl.ds(k0, W), :]
            s = _dot_nt(q2, kw) + bias_ref[variant]
            m = jnp.maximum(jnp.max(s, axis=1, keepdims=True), sk)
            p = jnp.exp(s - m)
            l = jnp.sum(p, axis=1, keepdims=True) + jnp.exp(sk - m)
            o2 = _dot(p.astype(BF16), vw) / l
            o_ref[g, pl.ds(q0, BQ), :] = _unstack_heads(o2, lo_q)
            lse_ref[g, pl.ds(q0, BQ), :] = _unstack_heads(jnp.broadcast_to(m + jnp.log(l), (2 * BQ, PAIR)), lo_q)
            return carry

        _loop_blocks(G * nb, blk, 0, 4)

    qspec = pl.BlockSpec((G, L, PAIR), lambda n: (n, 0, 0))
    kspec = pl.BlockSpec((G, L, PAIR), lambda n: (n // pairs_per_kv, 0, 0))
    return pl.pallas_call(
        body, name="attn_fwd", grid=(N // G,),
        in_specs=[pl.BlockSpec(memory_space=pltpu.SMEM), qspec, kspec, kspec,
                  pl.BlockSpec((None, 3, 2 * BQ, W), lambda n: (n * G // pairs_per_bias, 0, 0, 0))],
        out_specs=[qspec, qspec],
        out_shape=[jax.ShapeDtypeStruct((N, L, PAIR), F32), jax.ShapeDtypeStruct((N, L, PAIR), F32)],
        compiler_params=_params("parallel"),
    )(sink, q, kp, vp, bias4)


def _attn_bwd(q, kp, vp, bias4t, sink, o, lse, do, R, pairs_per_kv, pairs_per_bias):
    N, L, _ = q.shape
    Nk = kp.shape[0]
    Pb = bias4t.shape[0]
    W = BQ + 2 * R
    nb = L // BQ
    assert L >= W and nb >= 2
    G = _units_per_step(nb, pairs_per_kv)

    def body(sink_ref, q_ref, k_ref, v_ref, bias_ref, o_ref, lse_ref, do_ref,
             dq_ref, dk_ref, dv_ref, dbias_ref, dsink_ref, dk_acc, dv_acc):
        n = pl.program_id(0)
        lo_q = _lo_mask((BQ, PAIR))
        first = lax.broadcasted_iota(jnp.int32, (1, 2 * BQ), 1) < BQ
        dsink_ref[...] = jnp.zeros_like(dsink_ref)

        @pl.when(n % pairs_per_kv == 0)
        def _():
            dk_acc[...] = jnp.zeros_like(dk_acc)
            dv_acc[...] = jnp.zeros_like(dv_acc)

        @pl.when((n * G) % pairs_per_bias == 0)
        def _():
            dbias_ref[...] = jnp.zeros_like(dbias_ref)

        def blk(f, carry):
            g, b = f // nb, f % nb
            u = n * G + g
            sk = jnp.where(first, sink_ref[2 * u], sink_ref[2 * u + 1])
            q0 = pl.multiple_of(b * BQ, BQ)
            q2 = _stack_heads(q_ref[g, pl.ds(q0, BQ), :], lo_q)
            k0, variant = _key_window(b, nb, L, R, W)
            kw = k_ref[g, pl.ds(k0, W), :]
            vw = v_ref[g, pl.ds(k0, W), :]
            dov = do_ref[g, pl.ds(q0, BQ), :]
            lse = _row_vector(lse_ref[g, pl.ds(q0, BQ), :], lo_q)
            delta = _row_vector(_half_sum(dov.astype(F32) * o_ref[g, pl.ds(q0, BQ), :], lo_q), lo_q)
            do2 = _stack_heads(dov.astype(BF16), lo_q)
            st = _dot_nt(kw, q2) + bias_ref[variant]
            pt = jnp.exp(st - lse)
            dst = pt * (_dot_nt(vw, do2) - delta)
            dstb = dst.astype(BF16)
            dbias_ref[variant] += dst
            dk_acc[g, pl.ds(k0, W), :] += _dot(dstb, q2)
            dv_acc[g, pl.ds(k0, W), :] += _dot(pt.astype(BF16), do2)
            dq_ref[g, pl.ds(q0, BQ), :] = _unstack_heads(_dot_tn(dstb, kw), lo_q).astype(BF16)
            dsink_ref[g, pl.ds(0, 1), :] -= jnp.exp(sk - lse) * delta
            return carry

        _loop_blocks(G * nb, blk, 0, 4)
        dk_ref[...] = dk_acc[...].astype(BF16)
        dv_ref[...] = dv_acc[...].astype(BF16)

    qspec = pl.BlockSpec((G, L, PAIR), lambda n: (n, 0, 0))
    kspec = pl.BlockSpec((G, L, PAIR), lambda n: (n // pairs_per_kv, 0, 0))
    return pl.pallas_call(
        body, name="attn_bwd", grid=(N // G,),
        in_specs=[pl.BlockSpec(memory_space=pltpu.SMEM), qspec, kspec, kspec,
                  pl.BlockSpec((None, 3, W, 2 * BQ), lambda n: (n * G // pairs_per_bias, 0, 0, 0)),
                  qspec, qspec, qspec],
        out_specs=[qspec, kspec, kspec,
                   pl.BlockSpec((None, 3, W, 2 * BQ), lambda n: (n * G // pairs_per_bias, 0, 0, 0)),
                   pl.BlockSpec((G, 8, 2 * BQ), lambda n: (n, 0, 0))],
        out_shape=[jax.ShapeDtypeStruct((N, L, PAIR), BF16),
                   jax.ShapeDtypeStruct((Nk, L, PAIR), BF16),
                   jax.ShapeDtypeStruct((Nk, L, PAIR), BF16),
                   jax.ShapeDtypeStruct((Pb, 3, W, 2 * BQ), F32),
                   jax.ShapeDtypeStruct((N, 8, 2 * BQ), F32)],
        scratch_shapes=[pltpu.VMEM((G, L, PAIR), F32), pltpu.VMEM((G, L, PAIR), F32)],
        compiler_params=_params("arbitrary"),
    )(sink, q, kp, vp, bias4t, o, lse, do)


def _attn_merge(branch_outs, ob, tm):
    T = ob.shape[1]
    n = len(DILS)

    def body(*refs):
        o_in, l_in, ob_ref = refs[:n], refs[n:2 * n], refs[2 * n]
        o_out, l_out, cat_ref = refs[2 * n + 1:3 * n + 1], refs[3 * n + 1:4 * n + 1], refs[4 * n + 1]
        tiles = refs[4 * n + 2:]
        for c in range(4):
            o_nat, l_nat = [], []
            for di, d in enumerate(DILS):
                for kind, (src, dst) in enumerate(((o_in[di], o_nat), (l_in[di], l_nat))):
                    tile = tiles[2 * di + kind]
                    if d == 1:
                        dst.append(src[c, 0])
                    else:
                        for r in range(d):
                            tile[pl.ds(r, tm // d, stride=d), :] = src[c, r]
                        dst.append(tile[...])
            m = functools.reduce(jnp.maximum, l_nat)
            ws = [jnp.exp(l - m) for l in l_nat]
            z = sum(ws)
            o = sum(w * t for w, t in zip(ws, o_nat)) / z
            cat_ref[:, c * PAIR:(c + 1) * PAIR] = o.astype(BF16)
            cat_ref[:, (4 + c) * PAIR:(5 + c) * PAIR] = ob_ref[c].astype(BF16)
            _spread(tiles[0], o, o_out, c, F32)
            _spread(tiles[1], m + jnp.log(z), l_out, c, F32)

    specs, shapes = _spread_specs(tm, T, F32)
    four = pl.BlockSpec((4, tm, PAIR), lambda i: (0, i, 0))
    o_views = [o.reshape(4, d, T // d, PAIR) for (o, _), d in zip(branch_outs, DILS)]
    l_views = [l.reshape(4, d, T // d, PAIR) for (_, l), d in zip(branch_outs, DILS)]
    res = pl.pallas_call(
        body, name="attn_merge", grid=(T // tm,),
        in_specs=specs + specs + [four],
        out_specs=specs + specs + [pl.BlockSpec((tm, 8 * PAIR), lambda i: (i, 0))],
        out_shape=shapes + shapes + [jax.ShapeDtypeStruct((T, 8 * PAIR), BF16)],
        scratch_shapes=[pltpu.VMEM((tm, PAIR), F32)] * (2 * n),
        compiler_params=_params("parallel"),
    )(*o_views, *l_views, ob)
    merged = [(res[di].reshape(4 * d, T // d, PAIR), res[n + di].reshape(4 * d, T // d, PAIR))
              for di, d in enumerate(DILS)]
    return merged, res[2 * n]


def _weight_arg(w, blk):
    if blk is None:
        return pl.BlockSpec(w.shape, lambda i: (0, 0)), (lambda ref: ref[...])
    D = w.shape[2]
    return (pl.BlockSpec((N_DEV, 128, D), lambda i: (0, blk, 0)),
            lambda ref: ref[...].reshape(N_DEV * 128, D))


def _oproj_fwd(x, o_cat, w, blk, tm):
    T, D = x.shape
    wspec, wload = _weight_arg(w, blk)

    def body(x_ref, o_ref, w_ref, out_ref):
        out_ref[...] = x_ref[...] + _dot(o_ref[...], wload(w_ref))

    tok = pl.BlockSpec((tm, D), lambda i: (i, 0))
    return pl.pallas_call(
        body, name="oproj_fwd", grid=(T // tm,),
        in_specs=[tok, pl.BlockSpec((tm, o_cat.shape[1]), lambda i: (i, 0)), wspec],
        out_specs=tok, out_shape=jax.ShapeDtypeStruct((T, D), F32),
        compiler_params=_params("parallel"),
    )(x, o_cat, w)


def _oproj_bwd(dx, w, blk, tm, dep=None):
    T, D = dx.shape
    wspec, wload = _weight_arg(w, blk)

    def body(dx_ref, w_ref, dxb_ref, dob_ref, *rest):
        doa_refs, tile = rest[:-1], rest[-1]
        db = dx_ref[...].astype(BF16)
        dxb_ref[...] = db
        do = _dot_nt(db, wload(w_ref))
        for c in range(4):
            _spread(tile, do[:, c * PAIR:(c + 1) * PAIR], doa_refs, c, BF16)
            dob_ref[c] = do[:, (4 + c) * PAIR:(5 + c) * PAIR].astype(BF16)

    tok = pl.BlockSpec((tm, D), lambda i: (i, 0))
    specs, shapes = _spread_specs(tm, T, BF16)
    body, in_specs, args = _with_dep(body, dep, [tok, wspec], [dx, w])
    res = pl.pallas_call(
        body, name="oproj_bwd", grid=(T // tm,),
        in_specs=in_specs,
        out_specs=[tok, pl.BlockSpec((4, tm, PAIR), lambda i: (0, i, 0))] + specs,
        out_shape=[jax.ShapeDtypeStruct((T, D), BF16), jax.ShapeDtypeStruct((4, T, PAIR), BF16)] + shapes,
        scratch_shapes=[pltpu.VMEM((tm, PAIR), F32)],
        compiler_params=_params("parallel"),
    )(*args)
    return res[0], res[1], [t.reshape(4 * d, T // d, PAIR) for t, d in zip(res[2:], DILS)]


def _attn_post(qkv, gains2, dqa, dka, dva, dqb, dkb, dvb, tm):
    T, NQ = qkv.shape
    scale = HEAD_DIM ** -0.5

    n = len(DILS)

    def body(qkv_ref, g_ref, *rest):
        dq_refs, dk_refs, dv_refs = rest[:n], rest[n:2 * n], rest[2 * n:3 * n]
        qb_ref, kb_ref, vb_ref, out_ref, dg_ref, tile = rest[3 * n:]
        lo = _lo_mask((tm, PAIR))

        @pl.when(pl.program_id(0) == 0)
        def _():
            dg_ref[...] = jnp.zeros_like(dg_ref)

        def norm_bwd(c, gi, dy):
            xv = qkv_ref[:, c * PAIR:(c + 1) * PAIR]
            r = lax.rsqrt(_half_sum(xv * xv, lo) * (1.0 / HEAD_DIM) + EPS)
            xn = xv * r
            dg_ref[gi:gi + 1, :] += jnp.sum(dy * xn, axis=0, keepdims=True)
            dxn = dy * g_ref[gi:gi + 1, :]
            dx = r * (dxn - xn * (_half_sum(dxn * xn, lo) * (1.0 / HEAD_DIM)))
            out_ref[:, c * PAIR:(c + 1) * PAIR] = dx.astype(BF16)

        def fold(v):
            return v + pltpu.roll(v, HEAD_DIM, 1)

        for c in range(4):
            norm_bwd(c, 0, _collect(tile, dq_refs, c) * scale)
            norm_bwd(4 + c, 1, _collect(tile, dk_refs, c))
            out_ref[:, (8 + c) * PAIR:(9 + c) * PAIR] = _collect(tile, dv_refs, c).astype(BF16)
            norm_bwd(12 + c, 2, qb_ref[c].astype(F32) * scale)
        kb, vb = kb_ref[...].astype(F32), vb_ref[...].astype(F32)
        norm_bwd(16, 3, jnp.where(lo, fold(kb[0]), fold(kb[1])))
        out_ref[:, 17 * PAIR:18 * PAIR] = jnp.where(lo, fold(vb[0]), fold(vb[1])).astype(BF16)

    four = pl.BlockSpec((4, tm, PAIR), lambda i: (0, i, 0))
    two = pl.BlockSpec((2, tm, PAIR), lambda i: (0, i, 0))
    specs, _ = _spread_specs(tm, T, BF16)
    views = [t.reshape(4, d, T // d, PAIR) for group in (dqa, dka, dva) for t, d in zip(group, DILS)]
    return pl.pallas_call(
        body, name="attn_post", grid=(T // tm,),
        in_specs=[pl.BlockSpec((tm, NQ), lambda i: (i, 0)), pl.BlockSpec((4, PAIR), lambda i: (0, 0))]
        + specs * 3 + [four, two, two],
        out_specs=[pl.BlockSpec((tm, NQ), lambda i: (i, 0)), pl.BlockSpec((4, PAIR), lambda i: (0, 0))],
        out_shape=[jax.ShapeDtypeStruct((T, NQ), BF16), jax.ShapeDtypeStruct((4, PAIR), F32)],
        scratch_shapes=[pltpu.VMEM((tm, PAIR), F32)],
        compiler_params=_params("arbitrary"),
    )(qkv, gains2, *views, dqb, dkb, dvb)


def _dense_norm_bwd(dres, dz, w, blk, x, g, tm):
    T, D = x.shape
    N = dz.shape[1]
    wspec, wload = _weight_arg(w, blk)

    def body(dres_ref, dz_ref, w_ref, x_ref, g_ref, dx_ref, dgn_ref):
        i = pl.program_id(0)
        dx, dg = _norm_bwd(_dot_nt(dz_ref[...], wload(w_ref)), x_ref[...], g_ref[...])
        dx_ref[...] = dres_ref[...] + dx

        @pl.when(i == 0)
        def _():
            dgn_ref[...] = dg

        @pl.when(i > 0)
        def _():
            dgn_ref[...] += dg

    tok = pl.BlockSpec((tm, D), lambda i: (i, 0))
    row = pl.BlockSpec((1, D), lambda i: (0, 0))
    return pl.pallas_call(
        body, name="dense_norm_bwd", grid=(T // tm,),
        in_specs=[tok, pl.BlockSpec((tm, N), lambda i: (i, 0)), wspec, tok, row],
        out_specs=[tok, row],
        out_shape=[jax.ShapeDtypeStruct((T, D), F32), jax.ShapeDtypeStruct((1, D), F32)],
        compiler_params=_params("arbitrary"),
    )(dres, dz, w, x, g)


def _bias_reduce(onehot, dbm):
    Hb, K = dbm.shape

    def body(oh_ref, d_ref, out_ref):
        oh = oh_ref[...]
        d = d_ref[...]
        hi = d.astype(BF16)
        r1 = d - hi.astype(F32)
        mid = r1.astype(BF16)
        low = (r1 - mid.astype(F32)).astype(BF16)
        out_ref[...] = _dot_nt(hi, oh) + _dot_nt(mid, oh) + _dot_nt(low, oh)

    vm = pl.BlockSpec(memory_space=pltpu.VMEM)
    return pl.pallas_call(
        body, name="bias_reduce", in_specs=[vm, vm], out_specs=vm,
        out_shape=jax.ShapeDtypeStruct((Hb, N_BUCKETS), F32),
        compiler_params=pltpu.CompilerParams(vmem_limit_bytes=VMEM_LIMIT),
    )(onehot, dbm)


def _ple_fwd(x, g, wg, blk, p, wp, target, tm):
    T, D = x.shape
    P = p.shape[1]
    with_loss = target is not None
    wspec, wload = _weight_arg(wg, blk)

    def body(*refs):
        if with_loss:
            x_ref, g_ref, wg_ref, p_ref, wp_ref, t_ref, y_ref, hn_ref, gate_ref, pp_ref, pb_ref, loss_ref = refs
        else:
            x_ref, g_ref, wg_ref, p_ref, wp_ref, y_ref, hn_ref, gate_ref, pp_ref, pb_ref = refs
        i = pl.program_id(0)
        xv = x_ref[...]
        hb = (xv * _rstd(xv) * g_ref[...]).astype(BF16)
        hn_ref[...] = hb
        gate = _sigmoid(_dot(hb, wload(wg_ref)))
        pb = p_ref[...].astype(BF16)
        pb_ref[...] = pb
        pp = _dot(pb, wp_ref[...])
        gate_ref[...] = gate
        pp_ref[...] = pp
        y = xv + gate * pp
        if with_loss:
            err = y - t_ref[...]
            y_ref[...] = err * (1.0 / D)
            part = jnp.broadcast_to(0.5 * jnp.sum(jnp.sum(err * err, axis=1, keepdims=True) * (1.0 / D),
                                                  axis=0, keepdims=True), (1, 128))

            @pl.when(i == 0)
            def _():
                loss_ref[...] = part

            @pl.when(i > 0)
            def _():
                loss_ref[...] += part
        else:
            y_ref[...] = y

    tok = pl.BlockSpec((tm, D), lambda i: (i, 0))
    ptok = pl.BlockSpec((tm, P), lambda i: (i, 0))
    in_specs = [tok, pl.BlockSpec((1, D), lambda i: (0, 0)), wspec, ptok,
                pl.BlockSpec((P, D), lambda i: (0, 0))]
    out_specs = [tok, tok, tok, tok, ptok]
    out_shape = [jax.ShapeDtypeStruct((T, D), F32), jax.ShapeDtypeStruct((T, D), BF16),
                 jax.ShapeDtypeStruct((T, D), F32), jax.ShapeDtypeStruct((T, D), F32),
                 jax.ShapeDtypeStruct((T, P), BF16)]
    args = [x, g, wg, p, wp]
    if with_loss:
        in_specs.append(tok)
        out_specs.append(pl.BlockSpec((1, 128), lambda i: (0, 0)))
        out_shape.append(jax.ShapeDtypeStruct((1, 128), F32))
        args.append(target)
    return pl.pallas_call(
        body, name="ple_fwd_loss" if with_loss else "ple_fwd", grid=(T // tm,),
        in_specs=in_specs, out_specs=out_specs, out_shape=out_shape,
        compiler_params=_params("arbitrary" if with_loss else "parallel"),
    )(*args)


def _ple_bwd(dy, gate, pp, tm, dep=None):
    T, D = dy.shape

    def body(dy_ref, gate_ref, pp_ref, dgl_ref, dpp_ref):
        d = dy_ref[...]
        gt = gate_ref[...]
        dgl_ref[...] = (d * pp_ref[...] * gt * (1.0 - gt)).astype(BF16)
        dpp_ref[...] = (d * gt).astype(BF16)

    tok = pl.BlockSpec((tm, D), lambda i: (i, 0))
    body, in_specs, args = _with_dep(body, dep, [tok, tok, tok], [dy, gate, pp])
    return pl.pallas_call(
        body, name="ple_bwd", grid=(T // tm,), in_specs=in_specs, out_specs=[tok, tok],
        out_shape=[jax.ShapeDtypeStruct((T, D), BF16), jax.ShapeDtypeStruct((T, D), BF16)],
        compiler_params=_params("parallel"),
    )(*args)


def _adamw(w, g, m, v):
    shape = w.shape
    C = shape[-1]
    w2, g2, m2, v2 = (a.reshape(-1, C) for a in (w, g, m, v))
    Rn = w2.shape[0]
    tr = Rn
    for cand in (512, 352, 256):
        if Rn % cand == 0:
            tr = cand
            break
    c1 = 1.0 - ADAM_B1 ** ADAM_STEP
    c2 = 1.0 - ADAM_B2 ** ADAM_STEP

    def body(w_ref, g_ref, m_ref, v_ref, d_ref, nm_ref, nv_ref):
        gv = g_ref[...]
        mn = ADAM_B1 * m_ref[...] + (1.0 - ADAM_B1) * gv
        vn = ADAM_B2 * v_ref[...] + (1.0 - ADAM_B2) * (gv * gv)
        d_ref[...] = -ADAM_LR * ((mn / c1) / (jnp.sqrt(vn / c2) + ADAM_EPS) + ADAM_WD * w_ref[...])
        nm_ref[...] = mn
        nv_ref[...] = vn

    spec = pl.BlockSpec((tr, C), lambda i: (i, 0))
    sh = jax.ShapeDtypeStruct((Rn, C), F32)
    d, nm, nv = pl.pallas_call(
        body, name="adamw", grid=(Rn // tr,), in_specs=[spec] * 4, out_specs=[spec] * 3, out_shape=[sh] * 3,
        compiler_params=_params("parallel"),
    )(w2, g2, m2, v2)
    return d.reshape(shape), nm.reshape(shape), nv.reshape(shape)


def _my_place():
    x, y, c = lax.axis_index("x"), lax.axis_index("y"), lax.axis_index("c")
    chips = [(1 - x, y), (x, 1 - y), (1 - x, 1 - y)]
    return x, y, c, chips


def _all_gather(arrs):
    n = len(arrs)

    def body(*refs):
        x_refs, out_refs = refs[:n], refs[n:2 * n]
        send_sems, recv_sems, local_sems = refs[2 * n:]
        x, y, c, chips = _my_place()
        me, sibling = (x, y, c), (x, y, 1 - c)

        def copy(m, k, block, to, src=None):
            rows = out_refs[m].at[4 * block[0] + 2 * block[1] + block[2]]
            return pltpu.make_async_remote_copy(
                src_ref=rows if src is None else src, dst_ref=rows,
                send_sem=send_sems.at[7 * m + k], recv_sem=recv_sems.at[7 * m + k], device_id=to, device_id_type=MESH)

        mine = [pltpu.make_async_copy(x_refs[m], out_refs[m].at[4 * x + 2 * y + c], local_sems.at[m])
                for m in range(n)]
        for cp in mine:
            cp.start()
        first = []
        for m in range(n):
            first.append(copy(m, 0, me, sibling, src=x_refs[m]))
            first += [copy(m, 1 + j, me, (*chip, c), src=x_refs[m]) for j, chip in enumerate(chips)]
        for cp in first:
            cp.start()
        passed = []
        for m in range(n):
            for j, chip in enumerate(chips):
                copy(m, 1 + j, (*chip, c), me).wait_recv()
                cp = copy(m, 4 + j, (*chip, c), sibling)
                cp.start()
                passed.append(cp)
        for m in range(n):
            copy(m, 0, sibling, me).wait_recv()
            for j, chip in enumerate(chips):
                copy(m, 4 + j, (*chip, 1 - c), me).wait_recv()
        for cp in first + passed:
            cp.wait_send()
        for cp in mine:
            cp.wait()

    hbm = pl.BlockSpec(memory_space=pl.ANY)
    return pl.pallas_call(
        body, name="all_gather", in_specs=[hbm] * n, out_specs=[hbm] * n,
        out_shape=[jax.ShapeDtypeStruct((N_DEV,) + a.shape, a.dtype) for a in arrs],
        scratch_shapes=[pltpu.SemaphoreType.DMA((7 * n,)), pltpu.SemaphoreType.DMA((7 * n,)),
                        pltpu.SemaphoreType.DMA((n,))],
    )(*arrs)


def _peer(x, y, c, k):
    return (x ^ ((k >> 2) & 1), y ^ ((k >> 1) & 1), c ^ (k & 1))


HBM_SPEC = pl.BlockSpec(memory_space=pltpu.HBM)
SEM_SPEC = pl.BlockSpec(memory_space=pltpu.SEMAPHORE)


def _exchange_refs(srcs, lands, m, k, x, y, c, scatter):
    peer = _peer(x, y, c, k)
    if scatter:
        return srcs[m].at[4 * peer[0] + 2 * peer[1] + peer[2]], lands[m].at[k - 1], peer
    return srcs[m], lands[m].at[4 * x + 2 * y + c], peer


def _exchange_start(arrs, land_shapes, scatter, name):
    n = len(arrs)

    def body(*refs):
        srcs, lands = refs[:n], refs[n:2 * n]
        send_sems, recv_sems = refs[2 * n], refs[2 * n + 1]
        token = refs[-1]
        x, y, c, _ = _my_place()
        for m in range(n):
            for k in range(1, N_DEV):
                src, dst, peer = _exchange_refs(srcs, lands, m, k, x, y, c, scatter)
                pltpu.make_async_remote_copy(
                    src_ref=src, dst_ref=dst, send_sem=send_sems.at[7 * m + k - 1],
                    recv_sem=recv_sems.at[7 * m + k - 1], device_id=peer, device_id_type=MESH).start()
        token[...] = jnp.zeros_like(token)

    zones = [lax.empty(s_, a.dtype) for s_, a in zip(land_shapes, arrs)]
    outs = pl.pallas_call(
        body, name=name,
        out_shape=(pltpu.SemaphoreType.DMA((7 * n,)), pltpu.SemaphoreType.DMA((7 * n,)),
                   *[pltpu.HBM(a.shape, a.dtype) for a in arrs], *[pltpu.HBM(z.shape, z.dtype) for z in zones],
                   jax.ShapeDtypeStruct((8, 128), F32)),
        in_specs=[HBM_SPEC] * (2 * n),
        out_specs=(SEM_SPEC, SEM_SPEC, *[HBM_SPEC] * (2 * n), pl.BlockSpec(memory_space=pltpu.VMEM)),
        input_output_aliases={m: 2 + m for m in range(2 * n)},
        compiler_params=pltpu.CompilerParams(has_side_effects=pltpu.SideEffectType.DATAFLOW_SIDE_EFFECTING),
    )(*[pltpu.with_memory_space_constraint(a, pltpu.HBM) for a in arrs],
      *[pltpu.with_memory_space_constraint(z, pltpu.HBM) for z in zones])
    return outs[0], outs[1], list(outs[2:2 + n]), list(outs[2 + n:2 + 2 * n]), outs[-1]


def _exchange_wait(send_sems, recv_sems, arrs, zones, after, scatter, name):
    n = len(arrs)
    afters = list(after) if isinstance(after, (list, tuple)) else [after]

    def body(*refs):
        srcs, lands = refs[:n], refs[n:2 * n]
        send_sems, recv_sems = refs[2 * n], refs[2 * n + 1]
        x, y, c, _ = _my_place()
        for m in range(n):
            for k in range(1, N_DEV):
                src, dst, peer = _exchange_refs(srcs, lands, m, k, x, y, c, scatter)
                cp = pltpu.make_async_remote_copy(
                    src_ref=src, dst_ref=dst, send_sem=send_sems.at[7 * m + k - 1],
                    recv_sem=recv_sems.at[7 * m + k - 1], device_id=peer, device_id_type=MESH)
                cp.wait_send()
                cp.wait_recv()

    outs = pl.pallas_call(
        body, name=name,
        out_shape=tuple(pltpu.HBM(a.shape, a.dtype) for a in list(arrs) + list(zones)),
        in_specs=[HBM_SPEC] * (2 * n) + [SEM_SPEC, SEM_SPEC] + [pl.BlockSpec(memory_space=pl.ANY)] * len(afters),
        out_specs=tuple([HBM_SPEC] * (2 * n)),
        input_output_aliases={m: m for m in range(2 * n)},
        compiler_params=pltpu.CompilerParams(has_side_effects=pltpu.SideEffectType.DATAFLOW_SIDE_EFFECTING),
    )(*arrs, *zones, send_sems, recv_sems, *afters)
    return list(outs[n:])


def _sum_parts(own, parts, tr, dep=None):
    R, W = own.shape

    def body(own_ref, parts_ref, out_ref):
        acc = own_ref[...].astype(F32)
        for k in range(N_DEV - 1):
            acc = acc + parts_ref[k].astype(F32)
        out_ref[...] = acc

    in_specs = [pl.BlockSpec((tr, W), lambda i: (i, 0)), pl.BlockSpec((N_DEV - 1, tr, W), lambda i: (0, i, 0))]
    body, in_specs, args = _with_dep(body, dep, in_specs, [own, parts])
    return pl.pallas_call(
        body, name="sum_parts", grid=(R // tr,),
        in_specs=in_specs,
        out_specs=pl.BlockSpec((tr, W), lambda i: (i, 0)),
        out_shape=jax.ShapeDtypeStruct((R, W), F32),
        compiler_params=_params("parallel"),
    )(*args)


def _all_reduce_small(v, dep=None):
    Rn, Wd = v.shape

    def body(v_ref, out_ref, gat_ref, send_sems, recv_sems):
        x, y, c, _ = _my_place()
        me = 4 * x + 2 * y + c
        gat_ref[me] = v_ref[...]
        copies = []
        for k in range(1, N_DEV):
            fx, fy, fc = (k >> 2) & 1, (k >> 1) & 1, k & 1
            peer = (x ^ fx, y ^ fy, c ^ fc)
            cp = pltpu.make_async_remote_copy(
                src_ref=v_ref, dst_ref=gat_ref.at[me], send_sem=send_sems.at[k - 1], recv_sem=recv_sems.at[k - 1],
                device_id=peer, device_id_type=MESH)
            cp.start()
            copies.append(cp)
        for cp in copies:
            cp.wait_recv()
        for cp in copies:
            cp.wait_send()
        acc = gat_ref[0]
        for k in range(1, N_DEV):
            acc = acc + gat_ref[k]
        out_ref[...] = acc

    vm = pl.BlockSpec(memory_space=pltpu.VMEM)
    body, in_specs, args = _with_dep(body, dep, [vm], [v])
    return pl.pallas_call(
        body, name="all_reduce_small", in_specs=in_specs, out_specs=vm,
        out_shape=jax.ShapeDtypeStruct((Rn, Wd), F32),
        scratch_shapes=[pltpu.VMEM((N_DEV, Rn, Wd), F32), pltpu.SemaphoreType.DMA((7,)),
                        pltpu.SemaphoreType.DMA((7,))],
    )(*args)


def _t5_bucket(rel):
    half = N_BUCKETS // 2
    max_exact = half // 2
    ret = jnp.where(rel > 0, half, 0)
    n = jnp.abs(rel)
    nf = jnp.maximum(n, 1).astype(F32)
    large = max_exact + (jnp.log(nf / max_exact) / math.log(MAX_DISTANCE / max_exact)
                         * (half - max_exact)).astype(jnp.int32)
    large = jnp.minimum(large, half - 1)
    return ret + jnp.where(n < max_exact, n, large)


def _band(R, d):
    W = BQ + 2 * R
    rel = jnp.arange(W)[None, :] - R - jnp.arange(BQ)[:, None]
    return _t5_bucket(rel * d), jnp.abs(rel) <= R


def _onehot(R, d):
    bkt, in_band = _band(R, d)
    return ((bkt.reshape(1, -1) == jnp.arange(N_BUCKETS)[:, None]) & in_band.reshape(1, -1)).astype(BF16)


def _bias_expand(table_t, onehot):
    H = table_t.shape[0]
    K = onehot.shape[1]

    def body(t_ref, oh_ref, out_ref):
        oh = oh_ref[...]
        t = t_ref[...]
        hi = t.astype(BF16)
        r1 = t - hi.astype(F32)
        mid = r1.astype(BF16)
        low = (r1 - mid.astype(F32)).astype(BF16)
        marked = _dot(jnp.ones(t.shape, BF16), oh) > 0.5
        out_ref[...] = jnp.where(marked, _dot(hi, oh) + _dot(mid, oh) + _dot(low, oh), NEG)

    vm = pl.BlockSpec(memory_space=pltpu.VMEM)
    return pl.pallas_call(
        body, name="bias_expand", in_specs=[vm, vm], out_specs=vm,
        out_shape=jax.ShapeDtypeStruct((H, K), F32),
        compiler_params=pltpu.CompilerParams(vmem_limit_bytes=VMEM_LIMIT),
    )(table_t, onehot)


def _bias_matrix(table, R, d):
    return _bias_expand(table.T, _onehot(R, d)).reshape(table.shape[1], BQ, BQ + 2 * R)


def _bias_variants(base, R):
    H, _, W = base.shape
    fill = jnp.full((H, BQ, R), NEG, F32)
    first = jnp.concatenate([base[:, :, R:], fill], axis=2)
    last = jnp.concatenate([fill, base[:, :, :W - R]], axis=2)
    v = jnp.stack([base, first, last], axis=1)
    v = v.reshape(H // 2, 2, 3, BQ, W).transpose(0, 2, 1, 3, 4).reshape(H // 2, 3, 2 * BQ, W)
    return v, v.transpose(0, 1, 3, 2)


def _bias_grad(dbt, R, d):
    P, _, W, _ = dbt.shape
    dbt = dbt[:, 0].at[:, R:].add(dbt[:, 1, :W - R]).at[:, :W - R].add(dbt[:, 2, R:])
    dbm = dbt.reshape(P, W, 2, BQ).transpose(0, 2, 3, 1).reshape(2 * P, BQ * W)
    return _bias_reduce(_onehot(R, d), dbm).T


def _tile2(gain):
    return jnp.concatenate([gain, gain])


ROW_W_O, ROW_GATE, B_ROWS = 768, 896, 1024
BLK_W_O, BLK_GATE = ROW_W_O // 128, ROW_GATE // 128


def _pack_layer(wts, i):
    a = jnp.stack([wts["ffn1_w_in"][i], wts["ffn2_w_in"][i]])
    D = a.shape[1]
    b = jnp.concatenate([
        wts["ffn1_w_out"][i], wts["ffn2_w_out"][i],
        jnp.zeros((ROW_W_O - 2 * wts["ffn1_w_out"].shape[1], D), a.dtype), wts["w_o"][i], wts["w_ple_gate"][i]])
    return a, b, wts["w_qkv"][i], wts["w_ple_proj"][i]


def _unpack_layer(sums, like):
    w_in2, b1, proj, w_o, qkv, w_in1, w_out1 = sums
    n_out = like["ffn1_w_out"].shape[1]
    out = {}
    if w_in2 is not None:
        out.update(ffn2_w_in=w_in2, ffn2_w_out=b1[:n_out], w_ple_gate=b1[n_out:], w_ple_proj=proj)
    if w_o is not None:
        out.update(w_o=w_o, w_qkv=qkv)
    if w_in1 is not None:
        out.update(ffn1_w_in=w_in1, ffn1_w_out=w_out1)
    return out


def _col_sharded(g):
    return g.transpose(1, 0, 2).reshape(g.shape[1], -1)


def _to_col_shards(g):
    rows = g.shape[0]
    return g.reshape(rows, N_DEV, -1).transpose(1, 0, 2)


def _layer_weights(ga, gb, gq, gp):
    return dict(ga=ga, gb=gb, w_qkv=_col_sharded(gq), w_proj=_col_sharded(gp))


def _layer_fwd(x, p, w, sm, i, target, tm, biases, dep=None):
    ga, gb = w["ga"], w["gb"]
    saved = {}
    saved["x0"] = x
    x1, saved["h1"], saved["zg1"], saved["zu1"], saved["s1"] = _ffn_fwd(
        x, sm["norm_ffn1"][i][None], ga, gb, 0, 2 * tm, dep)
    saved["x1"] = x1
    qkv, saved["hm"] = _qkv_fwd(x1, sm["norm_mix"][i][None], w["w_qkv"], 2 * tm)
    saved["qkv"] = qkv
    gains2 = jnp.stack([_tile2(sm[k][i]) for k in ("q_norm_a", "k_norm_a", "q_norm_b", "k_norm_b")])
    saved["gains2"] = gains2
    qb, kb, vb, qkv_d = _attn_prep(qkv, gains2, tm)
    no_sink = jnp.full((8,), NEG, F32)
    branches = []
    outs = []
    for (R, d), bias, (qd, kd, vd) in zip(DILATED, biases[:3], qkv_d):
        sink = jnp.tile(no_sink, d)
        outs.append(_attn_fwd(qd, kd, vd, bias[0], sink, R, 1, d))
        branches.append((qd, kd, vd, bias, sink, R, d))
    bias_b = biases[3]
    sink_b = sm["sink_b"][i]
    ob, lb = _attn_fwd(qb, kb, vb, bias_b[0], sink_b, SWA_RADIUS, 2, 1)
    merged, o_cat = _attn_merge(outs, ob, tm)
    saved.update(branches=branches, b=(qb, kb, vb, bias_b, sink_b), merged=merged, ob=ob, lb=lb, o_cat=o_cat)
    x2 = _oproj_fwd(x1, o_cat, gb, BLK_W_O, 2 * tm)
    saved["x2"] = x2
    x3, saved["h2"], saved["zg2"], saved["zu2"], saved["s2"] = _ffn_fwd(
        x2, sm["norm_ffn2"][i][None], ga, gb, 1, 2 * tm)
    saved["x3"] = x3
    res = _ple_fwd(x3, sm["norm_ple"][i][None], gb, BLK_GATE, p, w["w_proj"], target, tm)
    y, saved["hp"], saved["gate"], saved["pp"], saved["pb"] = res[:5]
    loss = res[5] if target is not None else None
    return y, loss, saved


def _layer_bwd(dy, w, sm, i, sv, tm, dep=None, on_ready=None, on_small=None, on_last=None):
    ga, gb = w["ga"], w["gb"]
    gs = {}
    D = dy.shape[1]
    dgl, dpp = _ple_bwd(dy, sv["gate"], sv["pp"], tm, dep)
    d_gate = _matmul_tn(sv["hp"], dgl, D, 2 * tm)
    d_proj = _matmul_tn(sv["pb"], dpp, D, 2 * tm)
    dx3, gs["norm_ple"] = _dense_norm_bwd(dy, dgl, gb, BLK_GATE, sv["x3"], sm["norm_ple"][i][None], 2 * tm)
    dx2, dyb, dzg, dzu, gs["norm_ffn2"] = _ffn_bwd(dx3, sv["x2"], sm["norm_ffn2"][i][None], sv["zg2"], sv["zu2"],
                                                   ga, gb, 1, tm)
    dwin2, dwo2 = _ffn_dw(sv["h2"], dzg, dzu, sv["s2"], dyb, 2 * tm)
    half = dwo2.shape[1] // 2
    after_ffn2 = [dwin2, jnp.concatenate([dwo2.reshape(N_DEV, half, D), d_gate.reshape(N_DEV, -1, D)], axis=1),
                  _to_col_shards(d_proj)]
    token = None if on_ready is None else on_ready(0, after_ffn2)
    dx2b, do_b, do_a = _oproj_bwd(dx2, gb, BLK_W_O, tm, token)
    d_wo = _matmul_tn(sv["o_cat"], dx2b, D, 2 * tm)
    dqa, dka, dva, dbias = [], [], [], []
    for (qd, kd, vd, bias, sink, R, d), (oa, la), do_d in zip(sv["branches"], sv["merged"], do_a):
        dq, dk, dv, dbm, _ = _attn_bwd(qd, kd, vd, bias[1], sink, oa, la, do_d, R, 1, d)
        dqa.append(dq)
        dka.append(dk)
        dva.append(dv)
        dbias.append(dbm)
    qb, kb, vb, bias_b, sink_b = sv["b"]
    dqb, dkb, dvb, dbm_b, dsink = _attn_bwd(qb, kb, vb, bias_b[1], sink_b, sv["ob"], sv["lb"], do_b,
                                            SWA_RADIUS, 2, 1)
    gs["rel_bias"] = dbias + [dbm_b]
    gs["sink_b"] = jnp.sum(dsink[:, 0].reshape(-1, 2, BQ), axis=2).reshape(-1)
    dqkv, dgains2 = _attn_post(sv["qkv"], sv["gains2"], dqa, dka, dva, dqb,
                               dkb, dvb, tm)
    dgains = dgains2[:, :HEAD_DIM] + dgains2[:, HEAD_DIM:]
    for k, name in enumerate(("q_norm_a", "k_norm_a", "q_norm_b", "k_norm_b")):
        gs[name] = dgains[k]
    d_qkv = _matmul_tn(sv["hm"], dqkv, dqkv.shape[1] // 2, 2 * tm)
    after_mixer = [d_wo.reshape(N_DEV, -1, D), _to_col_shards(d_qkv)]
    token = None if on_ready is None else on_ready(1, after_mixer)
    dx1, gs["norm_mix"] = _dense_norm_bwd(dx2, dqkv, w["w_qkv"], None, sv["x1"], sm["norm_mix"][i][None], 2 * tm)
    g1 = sm["norm_ffn1"][i][None]
    if on_last is None:
        dx0, dyb, dzg, dzu, gs["norm_ffn1"] = _ffn_bwd(dx1, sv["x0"], g1, sv["zg1"], sv["zu1"], ga, gb, 0, tm, token)
        dwin1, dwo1 = _ffn_dw(sv["h1"], dzg, dzu, sv["s1"], dyb, 2 * tm)
        return dx0, (after_ffn2, after_mixer, [dwin1, dwo1.reshape(N_DEV, half, D)]), gs
    dyb, dzg, dzu = _ffn_bwd_dz(dx1, sv["zg1"], sv["zu1"], gb, 0, 2 * tm, token)
    dwin1, dwo1 = _ffn_dw(sv["h1"], dzg, dzu, sv["s1"], dyb, 2 * tm, on_small(gs))
    last = [dwin1, dwo1.reshape(N_DEV, half, D)]
    dx0, gs["norm_ffn1"] = _ffn_bwd_dx(dx1, sv["x0"], g1, dzg, dzu, ga, 0, 2 * tm, on_last(last))
    return dx0, (after_ffn2, after_mixer, last), gs


def _bias_matrices(rel_bias):
    biases = [_bias_variants(_bias_matrix(rel_bias[:, :8], R, d), R) for R, d in DILATED]
    biases.append(_bias_variants(_bias_matrix(rel_bias[:, 8:], SWA_RADIUS, 1), SWA_RADIUS))
    return biases


def _stack_small(per_layer):
    small = {}
    for k, v in per_layer.items():
        if k == "rel_bias":
            per_branch = [sum(parts) for parts in zip(*v.values())]
            drel_a = sum(_bias_grad(t, R, d) for t, (R, d) in zip(per_branch[:3], DILATED))
            small[k] = jnp.concatenate([drel_a, _bias_grad(per_branch[3], SWA_RADIUS, 1)], axis=1)
        else:
            small[k] = jnp.stack([v[i].reshape(-1) for i in sorted(v)])
    return small


TM = 512
SUM_TILES = (512, 480, 256, 128, 512, 512, 352)
LAST_GROUP = ("ffn1_w_in", "ffn1_w_out")


def _pack_small(d, extra=None):
    parts = [d[k].reshape(-1) for k in SMALL]
    if extra is not None:
        parts.append(extra.reshape(-1))
    flat = jnp.concatenate(parts)
    return jnp.pad(flat, (0, SMALL_ROWS * 128 - flat.shape[0])).reshape(SMALL_ROWS, 128)


def _unpack_small(buf, like):
    flat = buf.reshape(-1)
    out, off = {}, 0
    for k in SMALL:
        n = like[k].size
        out[k] = flat[off:off + n].reshape(like[k].shape)
        off += n
    return out, flat[off]


def kernel(x, p, rel_bias, norm_ffn1, ffn1_w_in, ffn1_w_out, norm_mix, w_qkv, q_norm_a, k_norm_a, q_norm_b, k_norm_b, sink_b, w_o, norm_ffn2, ffn2_w_in, ffn2_w_out, norm_ple, w_ple_gate, w_ple_proj, loss_target, m_rel_bias, m_norm_ffn1, m_ffn1_w_in, m_ffn1_w_out, m_norm_mix, m_w_qkv, m_q_norm_a, m_k_norm_a, m_q_norm_b, m_k_norm_b, m_sink_b, m_w_o, m_norm_ffn2, m_ffn2_w_in, m_ffn2_w_out, m_norm_ple, m_w_ple_gate, m_w_ple_proj, v_rel_bias, v_norm_ffn1, v_ffn1_w_in, v_ffn1_w_out, v_norm_mix, v_w_qkv, v_q_norm_a, v_k_norm_a, v_q_norm_b, v_k_norm_b, v_sink_b, v_w_o, v_norm_ffn2, v_ffn2_w_in, v_ffn2_w_out, v_norm_ple, v_w_ple_gate, v_w_ple_proj):
    wts = dict(rel_bias=rel_bias, norm_ffn1=norm_ffn1, ffn1_w_in=ffn1_w_in, ffn1_w_out=ffn1_w_out,
               norm_mix=norm_mix, w_qkv=w_qkv, q_norm_a=q_norm_a, k_norm_a=k_norm_a, q_norm_b=q_norm_b,
               k_norm_b=k_norm_b, sink_b=sink_b, w_o=w_o, norm_ffn2=norm_ffn2, ffn2_w_in=ffn2_w_in,
               ffn2_w_out=ffn2_w_out, norm_ple=norm_ple, w_ple_gate=w_ple_gate, w_ple_proj=w_ple_proj)
    mom = dict(rel_bias=m_rel_bias, norm_ffn1=m_norm_ffn1, ffn1_w_in=m_ffn1_w_in, ffn1_w_out=m_ffn1_w_out,
               norm_mix=m_norm_mix, w_qkv=m_w_qkv, q_norm_a=m_q_norm_a, k_norm_a=m_k_norm_a, q_norm_b=m_q_norm_b,
               k_norm_b=m_k_norm_b, sink_b=m_sink_b, w_o=m_w_o, norm_ffn2=m_norm_ffn2, ffn2_w_in=m_ffn2_w_in,
               ffn2_w_out=m_ffn2_w_out, norm_ple=m_norm_ple, w_ple_gate=m_w_ple_gate, w_ple_proj=m_w_ple_proj)
    var = dict(rel_bias=v_rel_bias, norm_ffn1=v_norm_ffn1, ffn1_w_in=v_ffn1_w_in, ffn1_w_out=v_ffn1_w_out,
               norm_mix=v_norm_mix, w_qkv=v_w_qkv, q_norm_a=v_q_norm_a, k_norm_a=v_k_norm_a, q_norm_b=v_q_norm_b,
               k_norm_b=v_k_norm_b, sink_b=v_sink_b, w_o=v_w_o, norm_ffn2=v_norm_ffn2, ffn2_w_in=v_ffn2_w_in,
               ffn2_w_out=v_ffn2_w_out, norm_ple=v_norm_ple, w_ple_gate=v_w_ple_gate, w_ple_proj=v_w_ple_proj)
    sm = {k: wts[k] for k in SMALL}
    me = 4 * lax.axis_index("x") + 2 * lax.axis_index("y") + lax.axis_index("c")
    packed = []
    for i in range(2):
        a, *rest = _pack_layer(wts, i)
        packed.append([t.astype(BF16) for t in [a.reshape(-1, a.shape[-1])] + rest])
    a_shape = (2, ffn1_w_in.shape[1], ffn1_w_in.shape[2])

    def weights_of(zones):
        return _layer_weights(zones[0].reshape((N_DEV,) + a_shape), *zones[1:])

    w0 = weights_of(_all_gather(packed[0]))
    zone_shapes = [(N_DEV,) + t.shape for t in packed[1]]
    ssem, rsem, thru, zones, token = _exchange_start(packed[1], zone_shapes, False, "gather_start")
    biases = _bias_matrices(rel_bias)
    x1, _, sv0 = _layer_fwd(x[0], p[0, 0], w0, sm, 0, None, TM, biases, dep=token)
    zones = _exchange_wait(ssem, rsem, thru, zones, x1, False, "gather_wait")
    w1 = weights_of([lax.dynamic_update_index_in_dim(z, t, me, 0) for z, t in zip(zones, packed[1])])
    dy, loss, sv1 = _layer_fwd(x1, p[1, 0], w1, sm, 1, loss_target[0], TM, biases)

    def slots_for(arrs):
        return [(N_DEV - 1,) + t.shape[1:] for t in arrs]

    held1, held = {}, {}

    def on_ready1(stage, group):
        held1[stage] = _exchange_start(group, slots_for(group), True, f"scatter1_start_{stage}")
        return held1[stage][4]

    dx1, groups1, gs1 = _layer_bwd(dy, w1, sm, 1, sv1, TM, on_ready=on_ready1)
    on_ready1(2, groups1[2])
    g1 = groups1[0] + groups1[1] + groups1[2]

    def on_ready(stage, group):
        if stage == 1:
            held["slots1"] = [t for st in (0, 1, 2)
                              for t in _exchange_wait(*held1[st][:4], group[0], True, f"scatter1_wait_{st}")]
        held[stage] = _exchange_start(group, slots_for(group), True, f"scatter_start_{stage}")
        return held[stage][4]

    def on_small(gs0):
        part = dict(gs0, norm_ffn1=jnp.zeros_like(gs1["norm_ffn1"]))
        gsmall = _stack_small({k: {0: part[k], 1: gs1[k]} for k in part})
        held["small"] = _all_reduce_small(_pack_small(gsmall, loss[0, :1]))
        return held["small"]

    def on_last(group):
        held["last"] = _exchange_start(group, slots_for(group), True, "scatter_start_2")
        return held["last"][4]

    dx, groups0, gs0 = _layer_bwd(dx1, w0, sm, 0, sv0, TM, dep=held1[2][4], on_ready=on_ready, on_small=on_small,
                                  on_last=on_last)
    last = groups0[2]
    slots0 = [_exchange_wait(*held[stage][:4], last[0], True, f"scatter_wait_{stage}") for stage in (0, 1)]

    def summed(arrs, slots, tiles, dep=None):
        return [_sum_parts(lax.dynamic_index_in_dim(t, me, 0, keepdims=False), s_, tr, dep)
                for t, s_, tr in zip(arrs, slots, tiles)]

    cover = held["last"][4]
    r1 = summed(g1, held["slots1"], SUM_TILES, cover)
    r0 = summed(groups0[0], slots0[0], SUM_TILES[:3], cover) + summed(groups0[1], slots0[1], SUM_TILES[3:5], cover)

    def update(names, layers):
        for k in names:
            grads[k] = jnp.stack([layers[0][k], layers[1][k]])
            delta[k], new_m[k], new_v[k] = _adamw(wts[k], grads[k], mom[k], var[k])

    grads, delta, new_m, new_v = {}, {}, {}, {}
    layer1 = _unpack_layer(r1, wts)
    update([k for k in BIG if k not in LAST_GROUP], [_unpack_layer(r0 + [None, None], wts), layer1])

    cover_done = [dx] + [delta[k] for k in BIG if k not in LAST_GROUP]
    slots_last = _exchange_wait(*held["last"][:4], cover_done, True, "scatter_wait_2")
    update(LAST_GROUP, [_unpack_layer([None] * 5 + summed(last, slots_last, SUM_TILES[5:]), wts), layer1])
    late = _all_reduce_small(gs0["norm_ffn1"].reshape(-1, 128), dep=slots_last[0])
    small_sum, loss_sum = _unpack_small(held["small"], sm)
    small_sum["norm_ffn1"] = small_sum["norm_ffn1"].at[0].add(late.reshape(-1))
    grads.update(small_sum)
    zeros = {k: jnp.zeros_like(wts[k]) for k in SMALL}
    ds, ms, vs = _adamw(_pack_small(wts), _pack_small(small_sum), _pack_small(mom), _pack_small(var))
    for packed, dst in ((ds, delta), (ms, new_m), (vs, new_v)):
        dst.update(_unpack_small(packed, zeros)[0])

    return (loss_sum, dx[None], *[grads[k] for k in WEIGHTS], *[delta[k] for k in WEIGHTS],
            *[new_m[k] for k in WEIGHTS], *[new_v[k] for k in WEIGHTS])
```

```python
import functools
import math

import jax
import jax.numpy as jnp
from jax import lax
from jax.experimental import pallas as pl
from jax.experimental.pallas import tpu as pltpu

F32 = jnp.float32
BF16 = jnp.bfloat16

N_DEV = 8
HEAD_DIM = 64
PAIR = 2 * HEAD_DIM
BQ = 128
N_BUCKETS = 32
MAX_DISTANCE = 1024
DILATED = ((64, 1), (64, 4), (64, 16))
SWA_RADIUS = 128
EPS = 1e-6
NEG = -1e30
ADAM_LR, ADAM_B1, ADAM_B2, ADAM_EPS, ADAM_WD, ADAM_STEP = 0.001, 0.9, 0.999, 1e-08, 0.01, 10
VMEM_LIMIT = 56 * 1024 * 1024
MESH = pl.DeviceIdType.MESH

BIG = ("ffn1_w_in", "ffn1_w_out", "w_qkv", "w_o", "ffn2_w_in", "ffn2_w_out", "w_ple_gate", "w_ple_proj")
SMALL = ("rel_bias", "norm_ffn1", "norm_mix", "q_norm_a", "k_norm_a", "q_norm_b", "k_norm_b", "sink_b",
         "norm_ffn2", "norm_ple")
WEIGHTS = ("rel_bias", "norm_ffn1", "ffn1_w_in", "ffn1_w_out", "norm_mix", "w_qkv", "q_norm_a", "k_norm_a",
           "q_norm_b", "k_norm_b", "sink_b", "w_o", "norm_ffn2", "ffn2_w_in", "ffn2_w_out", "norm_ple",
           "w_ple_gate", "w_ple_proj")
SMALL_ROWS = 96


def _params(*sem):
    return pltpu.CompilerParams(dimension_semantics=sem, vmem_limit_bytes=VMEM_LIMIT)


def _dot(a, b):
    return jnp.dot(a, b, preferred_element_type=F32)


def _dot_nt(a, b):
    return lax.dot_general(a, b, (((1,), (1,)), ((), ())), preferred_element_type=F32)


def _dot_tn(a, b):
    return lax.dot_general(a, b, (((0,), (0,)), ((), ())), preferred_element_type=F32)


def _sigmoid(x):
    return 1.0 / (1.0 + jnp.exp(-x))


def _rstd(xv):
    return lax.rsqrt(jnp.mean(xv * xv, axis=-1, keepdims=True) + EPS)


def _norm_bwd(dh, xv, gv):
    r = _rstd(xv)
    xn = xv * r
    dg = jnp.sum(dh * xn, axis=0, keepdims=True)
    dxn = dh * gv
    dx = r * (dxn - xn * jnp.mean(dxn * xn, axis=-1, keepdims=True))
    return dx, dg


def _lo_mask(shape):
    return lax.broadcasted_iota(jnp.int32, shape, len(shape) - 1) < HEAD_DIM


def _half_sum(t, lo):
    s0 = jnp.sum(jnp.where(lo, t, 0.0), axis=1, keepdims=True)
    s1 = jnp.sum(jnp.where(lo, 0.0, t), axis=1, keepdims=True)
    return jnp.where(lo, s0, s1)


FFN_PARTS = 2


def _ffn_weight_specs(f, nj, D, C):
    return [pl.BlockSpec((None, None, D, C), lambda i, j: (j, f, 0, 0)),
            pl.BlockSpec((None, None, D, C), lambda i, j: (j + nj, f, 0, 0)),
            pl.BlockSpec((2, C // 2, D), lambda i, j: (j, f, 0))]


def _with_dep(body, dep, in_specs, args):
    if dep is None:
        return body, in_specs, args

    def body_after(dep_ref, *refs):
        body(*refs)

    return body_after, [pl.BlockSpec(memory_space=pl.ANY)] + in_specs, [dep] + args


def _ffn_fwd(x, g, ga, gb, f, tm, dep=None):
    T, D = x.shape
    nj, C = ga.shape[0] // 2, ga.shape[3]

    def body(x_ref, g_ref, wg_ref, wu_ref, wo_ref, xo_ref, h_ref, zg_ref, zu_ref, s_ref, h_scr, acc):
        j = pl.program_id(1)

        @pl.when(j == 0)
        def _():
            xv = x_ref[...]
            hb = (xv * _rstd(xv) * g_ref[...]).astype(BF16)
            h_scr[...] = hb
            h_ref[...] = hb
            acc[...] = jnp.zeros_like(acc)

        wo = wo_ref[...].reshape(C, D)
        for part in range(FFN_PARTS):
            sl = pl.ds(part * (tm // FFN_PARTS), tm // FFN_PARTS)
            hb = h_scr[sl, :]
            gt = _dot(hb, wg_ref[...])
            up = _dot(hb, wu_ref[...])
            s = (gt * _sigmoid(gt) * up).astype(BF16)
            zg_ref[sl, :] = gt.astype(BF16)
            zu_ref[sl, :] = up.astype(BF16)
            s_ref[sl, :] = s
            acc[sl, :] += _dot(s, wo)

        @pl.when(j == nj - 1)
        def _():
            xo_ref[...] = x_ref[...] + 0.5 * acc[...]

    tok = pl.BlockSpec((tm, D), lambda i, j: (i, 0))
    chunk = pl.BlockSpec((None, tm, C), lambda i, j: (j, i, 0))
    in_specs = [tok, pl.BlockSpec((1, D), lambda i, j: (0, 0))] + _ffn_weight_specs(f, nj, D, C)
    body, in_specs, args = _with_dep(body, dep, in_specs, [x, g, ga, ga, gb])
    return pl.pallas_call(
        body, name="ffn_fwd", grid=(T // tm, nj),
        in_specs=in_specs,
        out_specs=[tok, tok, chunk, chunk, chunk],
        out_shape=[jax.ShapeDtypeStruct((T, D), F32), jax.ShapeDtypeStruct((T, D), BF16),
                   jax.ShapeDtypeStruct((nj, T, C), BF16), jax.ShapeDtypeStruct((nj, T, C), BF16),
                   jax.ShapeDtypeStruct((nj, T, C), BF16)],
        scratch_shapes=[pltpu.VMEM((tm, D), BF16), pltpu.VMEM((tm, D), F32)],
        compiler_params=_params("parallel", "arbitrary"),
    )(*args)


def _ffn_bwd(dxo, x, g, zg, zu, ga, gb, f, tm, dep=None):
    T, D = x.shape
    nj, C = ga.shape[0] // 2, ga.shape[3]

    def body(dxo_ref, x_ref, g_ref, zg_ref, zu_ref, wg_ref, wu_ref, wo_ref,
             dx_ref, dy_ref, dzg_ref, dzu_ref, dgn_ref, dy_scr, acc):
        i, j = pl.program_id(0), pl.program_id(1)

        @pl.when(j == 0)
        def _():
            dyb = (0.5 * dxo_ref[...]).astype(BF16)
            dy_scr[...] = dyb
            dy_ref[...] = dyb
            acc[...] = jnp.zeros_like(acc)

        wo = wo_ref[...].reshape(C, D)
        for part in range(FFN_PARTS):
            sl = pl.ds(part * (tm // FFN_PARTS), tm // FFN_PARTS)
            ds = _dot_nt(dy_scr[sl, :], wo)
            gt = zg_ref[sl, :].astype(F32)
            up = zu_ref[sl, :].astype(F32)
            sg = _sigmoid(gt)
            dgt = (ds * up * (sg * (1.0 + gt * (1.0 - sg)))).astype(BF16)
            dup = (ds * (gt * sg)).astype(BF16)
            dzg_ref[sl, :] = dgt
            dzu_ref[sl, :] = dup
            acc[sl, :] += _dot_nt(dgt, wg_ref[...]) + _dot_nt(dup, wu_ref[...])

        @pl.when(j == nj - 1)
        def _():
            dx, dg = _norm_bwd(acc[...], x_ref[...], g_ref[...])
            dx_ref[...] = dxo_ref[...] + dx

            @pl.when(i == 0)
            def _():
                dgn_ref[...] = dg

            @pl.when(i > 0)
            def _():
                dgn_ref[...] += dg

    tok = pl.BlockSpec((tm, D), lambda i, j: (i, 0), pipeline_mode=pl.Buffered(1))
    chunk = pl.BlockSpec((None, tm, C), lambda i, j: (j, i, 0))
    row = pl.BlockSpec((1, D), lambda i, j: (0, 0))
    in_specs = [tok, tok, row, chunk, chunk] + _ffn_weight_specs(f, nj, D, C)
    body, in_specs, args = _with_dep(body, dep, in_specs, [dxo, x, g, zg, zu, ga, ga, gb])
    return pl.pallas_call(
        body, name="ffn_bwd", grid=(T // tm, nj),
        in_specs=in_specs,
        out_specs=[tok, tok, chunk, chunk, row],
        out_shape=[jax.ShapeDtypeStruct((T, D), F32), jax.ShapeDtypeStruct((T, D), BF16),
                   jax.ShapeDtypeStruct((nj, T, C), BF16), jax.ShapeDtypeStruct((nj, T, C), BF16),
                   jax.ShapeDtypeStruct((1, D), F32)],
        scratch_shapes=[pltpu.VMEM((tm, D), BF16), pltpu.VMEM((tm, D), F32)],
        compiler_params=_params("arbitrary", "arbitrary"),
    )(*args)


def _ffn_bwd_dz(dxo, zg, zu, gb, f, tm, dep=None):
    T, D = dxo.shape
    nj, C = zg.shape[0], zg.shape[2]

    def body(dxo_ref, zg_ref, zu_ref, wo_ref, dy_ref, dzg_ref, dzu_ref, dy_scr):
        @pl.when(pl.program_id(1) == 0)
        def _():
            dyb = (0.5 * dxo_ref[...]).astype(BF16)
            dy_scr[...] = dyb
            dy_ref[...] = dyb

        wo = wo_ref[...].reshape(C, D)
        for part in range(FFN_PARTS):
            sl = pl.ds(part * (tm // FFN_PARTS), tm // FFN_PARTS)
            ds = _dot_nt(dy_scr[sl, :], wo)
            gt = zg_ref[sl, :].astype(F32)
            up = zu_ref[sl, :].astype(F32)
            sg = _sigmoid(gt)
            dzg_ref[sl, :] = (ds * up * (sg * (1.0 + gt * (1.0 - sg)))).astype(BF16)
            dzu_ref[sl, :] = (ds * (gt * sg)).astype(BF16)

    tok = pl.BlockSpec((tm, D), lambda i, j: (i, 0))
    chunk = pl.BlockSpec((None, tm, C), lambda i, j: (j, i, 0))
    in_specs = [tok, chunk, chunk, _ffn_weight_specs(f, nj, D, C)[2]]
    body, in_specs, args = _with_dep(body, dep, in_specs, [dxo, zg, zu, gb])
    return pl.pallas_call(
        body, name="ffn_bwd_dz", grid=(T // tm, nj),
        in_specs=in_specs, out_specs=[tok, chunk, chunk],
        out_shape=[jax.ShapeDtypeStruct((T, D), BF16), jax.ShapeDtypeStruct((nj, T, C), BF16),
                   jax.ShapeDtypeStruct((nj, T, C), BF16)],
        scratch_shapes=[pltpu.VMEM((tm, D), BF16)],
        compiler_params=_params("parallel", "arbitrary"),
    )(*args)


def _ffn_bwd_dx(dxo, x, g, dzg, dzu, ga, f, tm, dep=None):
    T, D = x.shape
    nj, C = ga.shape[0] // 2, ga.shape[3]

    def body(dxo_ref, x_ref, g_ref, dzg_ref, dzu_ref, wg_ref, wu_ref, dx_ref, dgn_ref, acc):
        i, j = pl.program_id(0), pl.program_id(1)

        @pl.when(j == 0)
        def _():
            acc[...] = jnp.zeros_like(acc)

        acc[...] += _dot_nt(dzg_ref[...], wg_ref[...]) + _dot_nt(dzu_ref[...], wu_ref[...])

        @pl.when(j == nj - 1)
        def _():
            dx, dg = _norm_bwd(acc[...], x_ref[...], g_ref[...])
            dx_ref[...] = dxo_ref[...] + dx

            @pl.when(i == 0)
            def _():
                dgn_ref[...] = dg

            @pl.when(i > 0)
            def _():
                dgn_ref[...] += dg

    tok = pl.BlockSpec((tm, D), lambda i, j: (i, 0))
    chunk = pl.BlockSpec((None, tm, C), lambda i, j: (j, i, 0))
    row = pl.BlockSpec((1, D), lambda i, j: (0, 0))
    in_specs = [tok, tok, row, chunk, chunk] + _ffn_weight_specs(f, nj, D, C)[:2]
    body, in_specs, args = _with_dep(body, dep, in_specs, [dxo, x, g, dzg, dzu, ga, ga])
    return pl.pallas_call(
        body, name="ffn_bwd_dx", grid=(T // tm, nj),
        in_specs=in_specs, out_specs=[tok, row],
        out_shape=[jax.ShapeDtypeStruct((T, D), F32), jax.ShapeDtypeStruct((1, D), F32)],
        scratch_shapes=[pltpu.VMEM((tm, D), F32)],
        compiler_params=_params("arbitrary", "arbitrary"),
    )(*args)


def _ffn_dw(h, dzg, dzu, s, dy, tk, dep=None):
    T, D = h.shape
    nj, C = s.shape[0], s.shape[2]
    nk = T // tk

    def body(h_ref, dzg_ref, dzu_ref, s_ref, dy_ref, dwin_ref, dwo_ref, ag, au, ao):
        k = pl.program_id(1)

        @pl.when(k == 0)
        def _():
            ag[...] = jnp.zeros_like(ag)
            au[...] = jnp.zeros_like(au)
            ao[...] = jnp.zeros_like(ao)

        hb = h_ref[...]
        ag[...] += _dot_tn(hb, dzg_ref[...])
        au[...] += _dot_tn(hb, dzu_ref[...])
        ao[...] += _dot_tn(s_ref[...], dy_ref[...])

        @pl.when(k == nk - 1)
        def _():
            dwin_ref[0] = ag[...].astype(BF16)
            dwin_ref[1] = au[...].astype(BF16)
            dwo_ref[...] = ao[...].astype(BF16)

    tok = pl.BlockSpec((tk, D), lambda j, k: (k, 0))
    chunk = pl.BlockSpec((None, tk, C), lambda j, k: (j, k, 0))
    body, in_specs, args = _with_dep(body, dep, [tok, chunk, chunk, chunk, tok], [h, dzg, dzu, s, dy])
    dwin, dwo = pl.pallas_call(
        body, name="ffn_dw", grid=(nj, nk),
        in_specs=in_specs,
        out_specs=[pl.BlockSpec((2, None, D, C), lambda j, k: (0, j, 0, 0)),
                   pl.BlockSpec((None, C, D), lambda j, k: (j, 0, 0))],
        out_shape=[jax.ShapeDtypeStruct((2, nj, D, C), BF16), jax.ShapeDtypeStruct((nj, C, D), BF16)],
        scratch_shapes=[pltpu.VMEM((D, C), F32), pltpu.VMEM((D, C), F32), pltpu.VMEM((C, D), F32)],
        compiler_params=_params("parallel", "arbitrary"),
    )(*args)
    return dwin.reshape(2 * nj, D, C), dwo


def _matmul_tn(a, b, tn, tk):
    T, Ka = a.shape
    N = b.shape[1]
    nk = T // tk

    def body(a_ref, b_ref, o_ref, acc):
        k = pl.program_id(1)

        @pl.when(k == 0)
        def _():
            acc[...] = jnp.zeros_like(acc)

        acc[...] += _dot_tn(a_ref[...], b_ref[...])

        @pl.when(k == nk - 1)
        def _():
            o_ref[...] = acc[...].astype(BF16)

    return pl.pallas_call(
        body, name="matmul_tn", grid=(N // tn, nk),
        in_specs=[pl.BlockSpec((tk, Ka), lambda n, k: (k, 0)), pl.BlockSpec((tk, tn), lambda n, k: (k, n))],
        out_specs=pl.BlockSpec((Ka, tn), lambda n, k: (0, n)),
        out_shape=jax.ShapeDtypeStruct((Ka, N), BF16),
        scratch_shapes=[pltpu.VMEM((Ka, tn), F32)],
        compiler_params=_params("parallel", "arbitrary"),
    )(a, b)


def _qkv_fwd(x, g, w, tm):
    T, D = x.shape
    N = w.shape[1]

    def body(x_ref, g_ref, w_ref, o_ref, h_ref):
        xv = x_ref[...]
        hb = (xv * _rstd(xv) * g_ref[...]).astype(BF16)
        h_ref[...] = hb
        o_ref[...] = _dot(hb, w_ref[...])

    return pl.pallas_call(
        body, name="qkv_fwd", grid=(T // tm,),
        in_specs=[pl.BlockSpec((tm, D), lambda i: (i, 0)), pl.BlockSpec((1, D), lambda i: (0, 0)),
                  pl.BlockSpec((D, N), lambda i: (0, 0))],
        out_specs=[pl.BlockSpec((tm, N), lambda i: (i, 0)), pl.BlockSpec((tm, D), lambda i: (i, 0))],
        out_shape=[jax.ShapeDtypeStruct((T, N), F32), jax.ShapeDtypeStruct((T, D), BF16)],
        compiler_params=_params("parallel"),
    )(x, g, w)


DILS = tuple(d for _, d in DILATED)


def _spread_specs(tm, T, dtype):
    specs = [pl.BlockSpec((4, d, tm // d, PAIR), lambda i: (0, 0, i, 0)) for d in DILS]
    shapes = [jax.ShapeDtypeStruct((4, d, T // d, PAIR), dtype) for d in DILS]
    return specs, shapes


def _spread(tile, y, outs, c, dtype):
    tm = y.shape[0]
    tile[...] = y
    for out, d in zip(outs, DILS):
        for r in range(d):
            out[c, r] = tile[pl.ds(r, tm // d, stride=d), :].astype(dtype)


def _collect(tile, ins, c):
    tm = tile.shape[0]
    first = True
    for ref, d in zip(ins, DILS):
        for r in range(d):
            rows = pl.ds(r, tm // d, stride=d) if d > 1 else pl.ds(0, tm)
            part = ref[c, r].astype(F32)
            tile[rows, :] = part if first else tile[rows, :] + part
        first = False
    return tile[...]


def _attn_prep(qkv, gains2, tm):
    T = qkv.shape[0]
    scale = HEAD_DIM ** -0.5
    n = len(DILS)

    def body(qkv_ref, g_ref, qb_ref, kb_ref, vb_ref, *rest):
        outs, tile = rest[:-1], rest[-1]
        lo = _lo_mask((tm, PAIR))

        def spread(kind, c, y):
            _spread(tile, y, outs[kind * n:(kind + 1) * n], c, BF16)

        def normed(c, gi, mult):
            xv = qkv_ref[:, c * PAIR:(c + 1) * PAIR]
            r = lax.rsqrt(_half_sum(xv * xv, lo) * (1.0 / HEAD_DIM) + EPS)
            y = xv * r * g_ref[gi:gi + 1, :]
            return y * mult if mult != 1.0 else y

        def both_halves(v):
            sw = pltpu.roll(v, HEAD_DIM, 1)
            return jnp.where(lo, v, sw), jnp.where(lo, sw, v)

        for c in range(4):
            spread(0, c, normed(c, 0, scale))
            spread(1, c, normed(4 + c, 1, 1.0))
            spread(2, c, qkv_ref[:, (8 + c) * PAIR:(9 + c) * PAIR])
            qb_ref[c] = normed(12 + c, 2, scale).astype(BF16)
        k0, k1 = both_halves(normed(16, 3, 1.0))
        kb_ref[0] = k0.astype(BF16)
        kb_ref[1] = k1.astype(BF16)
        v0, v1 = both_halves(qkv_ref[:, 17 * PAIR:18 * PAIR])
        vb_ref[0] = v0.astype(BF16)
        vb_ref[1] = v1.astype(BF16)

    four = pl.BlockSpec((4, tm, PAIR), lambda i: (0, i, 0))
    two = pl.BlockSpec((2, tm, PAIR), lambda i: (0, i, 0))
    s4 = jax.ShapeDtypeStruct((4, T, PAIR), BF16)
    s2 = jax.ShapeDtypeStruct((2, T, PAIR), BF16)
    specs, shapes = _spread_specs(tm, T, BF16)
    res = pl.pallas_call(
        body, name="attn_prep", grid=(T // tm,),
        in_specs=[pl.BlockSpec((tm, qkv.shape[1]), lambda i: (i, 0)), pl.BlockSpec((4, PAIR), lambda i: (0, 0))],
        out_specs=[four, two, two] + specs * 3,
        out_shape=[s4, s2, s2] + shapes * 3,
        scratch_shapes=[pltpu.VMEM((tm, PAIR), F32)],
        compiler_params=_params("parallel"),
    )(qkv, gains2)
    qb, kb, vb = res[:3]
    per_d = [tuple(res[3 + kind * n + di].reshape(4 * d, T // d, PAIR) for kind in range(3))
             for di, d in enumerate(DILS)]
    return qb, kb, vb, per_d


def _loop_blocks(nb, body, init, per_iter):
    u = math.gcd(nb, per_iter)

    def outer(i, carry):
        for k in range(u):
            carry = body(i * u + k, carry)
        return carry

    return lax.fori_loop(0, nb // u, outer, init)


def _key_window(b, nb, L, R, W):
    start = pl.multiple_of(jnp.clip(b * BQ - R, 0, L - W), HEAD_DIM)
    return start, jnp.where(b == 0, 1, jnp.where(b == nb - 1, 2, 0))


def _stack_heads(v, lo):
    z = jnp.zeros_like(v)
    return jnp.concatenate([jnp.where(lo, v, z), jnp.where(lo, z, v)], axis=0)


def _unstack_heads(v2, lo):
    return jnp.where(lo, v2[:BQ], v2[BQ:])


def _row_vector(v, lo):
    r = lax.broadcasted_iota(jnp.int32, (BQ, PAIR), 0)
    ln = lax.broadcasted_iota(jnp.int32, (BQ, PAIR), 1)
    diag = (ln % HEAD_DIM) == (r % HEAD_DIM)
    top = jnp.sum(jnp.where(diag & (r < HEAD_DIM), v, 0.0), axis=0, keepdims=True)
    bot = jnp.sum(jnp.where(diag & (r >= HEAD_DIM), v, 0.0), axis=0, keepdims=True)
    top8, bot8 = jnp.broadcast_to(top, (8, PAIR)), jnp.broadcast_to(bot, (8, PAIR))
    lo8 = _lo_mask((8, PAIR))
    head0 = jnp.where(lo8, top8, pltpu.roll(bot8, HEAD_DIM, 1))
    head1 = jnp.where(lo8, pltpu.roll(top8, HEAD_DIM, 1), bot8)
    return jnp.concatenate([head0, head1], axis=1)[:1]


def _units_per_step(nb, pairs_per_kv):
    return max(1, 16 // nb) if pairs_per_kv == 1 else 1


def _attn_fwd(q, kp, vp, bias4, sink, R, pairs_per_kv, pairs_per_bias):
    N, L, _ = q.shape
    W = BQ + 2 * R
    nb = L // BQ
    assert L >= W and nb >= 2
    G = _units_per_step(nb, pairs_per_kv)

    def body(sink_ref, q_ref, k_ref, v_ref, bias_ref, o_ref, lse_ref):
        n = pl.program_id(0)
        lo_q = _lo_mask((BQ, PAIR))
        first = lax.broadcasted_iota(jnp.int32, (2 * BQ, 1), 0) < BQ

        def blk(f, carry):
            g, b = f // nb, f % nb
            u = n * G + g
            sk = jnp.where(first, sink_ref[2 * u], sink_ref[2 * u + 1])
            q0 = pl.multiple_of(b * BQ, BQ)
            q2 = _stack_heads(q_ref[g, pl.ds(q0, BQ), :], lo_q)
            k0, variant = _key_window(b, nb, L, R, W)
            kw = k_ref[g, pl.ds(k0, W), :]
            vw = v_ref[g, pl.ds(k0, W), :]
            s = _dot_nt(q2, kw) + bias_ref[variant]
            m = jnp.maximum(jnp.max(s, axis=1, keepdims=True), sk)
            p = jnp.exp(s - m)
            l = jnp.sum(p, axis=1, keepdims=True) + jnp.exp(sk - m)
            o2 = _dot(p.astype(BF16), vw) / l
            o_ref[g, pl.ds(q0, BQ), :] = _unstack_heads(o2, lo_q)
            lse_ref[g, pl.ds(q0, BQ), :] = _unstack_heads(jnp.broadcast_to(m + jnp.log(l), (2 * BQ, PAIR)), lo_q)
            return carry

        _loop_blocks(G * nb, blk, 0, 4)

    qspec = pl.BlockSpec((G, L, PAIR), lambda n: (n, 0, 0))
    kspec = pl.BlockSpec((G, L, PAIR), lambda n: (n // pairs_per_kv, 0, 0))
    return pl.pallas_call(
        body, name="attn_fwd", grid=(N // G,),
        in_specs=[pl.BlockSpec(memory_space=pltpu.SMEM), qspec, kspec, kspec,
                  pl.BlockSpec((None, 3, 2 * BQ, W), lambda n: (n * G // pairs_per_bias, 0, 0, 0))],
        out_specs=[qspec, qspec],
        out_shape=[jax.ShapeDtypeStruct((N, L, PAIR), F32), jax.ShapeDtypeStruct((N, L, PAIR), F32)],
        compiler_params=_params("parallel"),
    )(sink, q, kp, vp, bias4)


def _attn_bwd(q, kp, vp, bias4t, sink, o, lse, do, R, pairs_per_kv, pairs_per_bias):
    N, L, _ = q.shape
    Nk = kp.shape[0]
    Pb = bias4t.shape[0]
    W = BQ + 2 * R
    nb = L // BQ
    assert L >= W and nb >= 2
    G = _units_per_step(nb, pairs_per_kv)

    def body(sink_ref, q_ref, k_ref, v_ref, bias_ref, o_ref, lse_ref, do_ref,
             dq_ref, dk_ref, dv_ref, dbias_ref, dsink_ref, dk_acc, dv_acc):
        n = pl.program_id(0)
        lo_q = _lo_mask((BQ, PAIR))
        first = lax.broadcasted_iota(jnp.int32, (1, 2 * BQ), 1) < BQ
        dsink_ref[...] = jnp.zeros_like(dsink_ref)

        @pl.when(n % pairs_per_kv == 0)
        def _():
            dk_acc[...] = jnp.zeros_like(dk_acc)
            dv_acc[...] = jnp.zeros_like(dv_acc)

        @pl.when((n * G) % pairs_per_bias == 0)
        def _():
            dbias_ref[...] = jnp.zeros_like(dbias_ref)

        def blk(f, carry):
            g, b = f // nb, f % nb
            u = n * G + g
            sk = jnp.where(first, sink_ref[2 * u], sink_ref[2 * u + 1])
            q0 = pl.multiple_of(b * BQ, BQ)
            q2 = _stack_heads(q_ref[g, pl.ds(q0, BQ), :], lo_q)
            k0, variant = _key_window(b, nb, L, R, W)
            kw = k_ref[g, pl.ds(k0, W), :]
            vw = v_ref[g, pl.ds(k0, W), :]
            dov = do_ref[g, pl.ds(q0, BQ), :]
            lse = _row_vector(lse_ref[g, pl.ds(q0, BQ), :], lo_q)
            delta = _row_vector(_half_sum(dov.astype(F32) * o_ref[g, pl.ds(q0, BQ), :], lo_q), lo_q)
            do2 = _stack_heads(dov.astype(BF16), lo_q)
            st = _dot_nt(kw, q2) + bias_ref[variant]
            pt = jnp.exp(st - lse)
            dst = pt * (_dot_nt(vw, do2) - delta)
            dstb = dst.astype(BF16)
            dbias_ref[variant] += dst
            dk_acc[g, pl.ds(k0, W), :] += _dot(dstb, q2)
            dv_acc[g, pl.ds(k0, W), :] += _dot(pt.astype(BF16), do2)
            dq_ref[g, pl.ds(q0, BQ), :] = _unstack_heads(_dot_tn(dstb, kw), lo_q).astype(BF16)
            dsink_ref[g, pl.ds(0, 1), :] -= jnp.exp(sk - lse) * delta
            return carry

        _loop_blocks(G * nb, blk, 0, 4)
        dk_ref[...] = dk_acc[...].astype(BF16)
        dv_ref[...] = dv_acc[...].astype(BF16)

    qspec = pl.BlockSpec((G, L, PAIR), lambda n: (n, 0, 0))
    kspec = pl.BlockSpec((G, L, PAIR), lambda n: (n // pairs_per_kv, 0, 0))
    return pl.pallas_call(
        body, name="attn_bwd", grid=(N // G,),
        in_specs=[pl.BlockSpec(memory_space=pltpu.SMEM), qspec, kspec, kspec,
                  pl.BlockSpec((None, 3, W, 2 * BQ), lambda n: (n * G // pairs_per_bias, 0, 0, 0)),
                  qspec, qspec, qspec],
        out_specs=[qspec, kspec, kspec,
                   pl.BlockSpec((None, 3, W, 2 * BQ), lambda n: (n * G // pairs_per_bias, 0, 0, 0)),
                   pl.BlockSpec((G, 8, 2 * BQ), lambda n: (n, 0, 0))],
        out_shape=[jax.ShapeDtypeStruct((N, L, PAIR), BF16),
                   jax.ShapeDtypeStruct((Nk, L, PAIR), BF16),
                   jax.ShapeDtypeStruct((Nk, L, PAIR), BF16),
                   jax.ShapeDtypeStruct((Pb, 3, W, 2 * BQ), F32),
                   jax.ShapeDtypeStruct((N, 8, 2 * BQ), F32)],
        scratch_shapes=[pltpu.VMEM((G, L, PAIR), F32), pltpu.VMEM((G, L, PAIR), F32)],
        compiler_params=_params("arbitrary"),
    )(sink, q, kp, vp, bias4t, o, lse, do)


def _attn_merge(branch_outs, ob, tm):
    T = ob.shape[1]
    n = len(DILS)

    def body(*refs):
        o_in, l_in, ob_ref = refs[:n], refs[n:2 * n], refs[2 * n]
        o_out, l_out, cat_ref = refs[2 * n + 1:3 * n + 1], refs[3 * n + 1:4 * n + 1], refs[4 * n + 1]
        tiles = refs[4 * n + 2:]
        for c in range(4):
            o_nat, l_nat = [], []
            for di, d in enumerate(DILS):
                for kind, (src, dst) in enumerate(((o_in[di], o_nat), (l_in[di], l_nat))):
                    tile = tiles[2 * di + kind]
                    if d == 1:
                        dst.append(src[c, 0])
                    else:
                        for r in range(d):
                            tile[pl.ds(r, tm // d, stride=d), :] = src[c, r]
                        dst.append(tile[...])
            m = functools.reduce(jnp.maximum, l_nat)
            ws = [jnp.exp(l - m) for l in l_nat]
            z = sum(ws)
            o = sum(w * t for w, t in zip(ws, o_nat)) / z
            cat_ref[:, c * PAIR:(c + 1) * PAIR] = o.astype(BF16)
            cat_ref[:, (4 + c) * PAIR:(5 + c) * PAIR] = ob_ref[c].astype(BF16)
            _spread(tiles[0], o, o_out, c, F32)
            _spread(tiles[1], m + jnp.log(z), l_out, c, F32)

    specs, shapes = _spread_specs(tm, T, F32)
    four = pl.BlockSpec((4, tm, PAIR), lambda i: (0, i, 0))
    o_views = [o.reshape(4, d, T // d, PAIR) for (o, _), d in zip(branch_outs, DILS)]
    l_views = [l.reshape(4, d, T // d, PAIR) for (_, l), d in zip(branch_outs, DILS)]
    res = pl.pallas_call(
        body, name="attn_merge", grid=(T // tm,),
        in_specs=specs + specs + [four],
        out_specs=specs + specs + [pl.BlockSpec((tm, 8 * PAIR), lambda i: (i, 0))],
        out_shape=shapes + shapes + [jax.ShapeDtypeStruct((T, 8 * PAIR), BF16)],
        scratch_shapes=[pltpu.VMEM((tm, PAIR), F32)] * (2 * n),
        compiler_params=_params("parallel"),
    )(*o_views, *l_views, ob)
    merged = [(res[di].reshape(4 * d, T // d, PAIR), res[n + di].reshape(4 * d, T // d, PAIR))
              for di, d in enumerate(DILS)]
    return merged, res[2 * n]


def _weight_arg(w, blk):
    if blk is None:
        return pl.BlockSpec(w.shape, lambda i: (0, 0)), (lambda ref: ref[...])
    D = w.shape[2]
    return (pl.BlockSpec((N_DEV, 128, D), lambda i: (0, blk, 0)),
            lambda ref: ref[...].reshape(N_DEV * 128, D))


def _oproj_fwd(x, o_cat, w, blk, tm):
    T, D = x.shape
    wspec, wload = _weight_arg(w, blk)

    def body(x_ref, o_ref, w_ref, out_ref):
        out_ref[...] = x_ref[...] + _dot(o_ref[...], wload(w_ref))

    tok = pl.BlockSpec((tm, D), lambda i: (i, 0))
    return pl.pallas_call(
        body, name="oproj_fwd", grid=(T // tm,),
        in_specs=[tok, pl.BlockSpec((tm, o_cat.shape[1]), lambda i: (i, 0)), wspec],
        out_specs=tok, out_shape=jax.ShapeDtypeStruct((T, D), F32),
        compiler_params=_params("parallel"),
    )(x, o_cat, w)


def _oproj_bwd(dx, w, blk, tm, dep=None):
    T, D = dx.shape
    wspec, wload = _weight_arg(w, blk)

    def body(dx_ref, w_ref, dxb_ref, dob_ref, *rest):
        doa_refs, tile = rest[:-1], rest[-1]
        db = dx_ref[...].astype(BF16)
        dxb_ref[...] = db
        do = _dot_nt(db, wload(w_ref))
        for c in range(4):
            _spread(tile, do[:, c * PAIR:(c + 1) * PAIR], doa_refs, c, BF16)
            dob_ref[c] = do[:, (4 + c) * PAIR:(5 + c) * PAIR].astype(BF16)

    tok = pl.BlockSpec((tm, D), lambda i: (i, 0))
    specs, shapes = _spread_specs(tm, T, BF16)
    body, in_specs, args = _with_dep(body, dep, [tok, wspec], [dx, w])
    res = pl.pallas_call(
        body, name="oproj_bwd", grid=(T // tm,),
        in_specs=in_specs,
        out_specs=[tok, pl.BlockSpec((4, tm, PAIR), lambda i: (0, i, 0))] + specs,
        out_shape=[jax.ShapeDtypeStruct((T, D), BF16), jax.ShapeDtypeStruct((4, T, PAIR), BF16)] + shapes,
        scratch_shapes=[pltpu.VMEM((tm, PAIR), F32)],
        compiler_params=_params("parallel"),
    )(*args)
    return res[0], res[1], [t.reshape(4 * d, T // d, PAIR) for t, d in zip(res[2:], DILS)]


def _attn_post(qkv, gains2, dqa, dka, dva, dqb, dkb, dvb, tm):
    T, NQ = qkv.shape
    scale = HEAD_DIM ** -0.5

    n = len(DILS)

    def body(qkv_ref, g_ref, *rest):
        dq_refs, dk_refs, dv_refs = rest[:n], rest[n:2 * n], rest[2 * n:3 * n]
        qb_ref, kb_ref, vb_ref, out_ref, dg_ref, tile = rest[3 * n:]
        lo = _lo_mask((tm, PAIR))

        @pl.when(pl.program_id(0) == 0)
        def _():
            dg_ref[...] = jnp.zeros_like(dg_ref)

        def norm_bwd(c, gi, dy):
            xv = qkv_ref[:, c * PAIR:(c + 1) * PAIR]
            r = lax.rsqrt(_half_sum(xv * xv, lo) * (1.0 / HEAD_DIM) + EPS)
            xn = xv * r
            dg_ref[gi:gi + 1, :] += jnp.sum(dy * xn, axis=0, keepdims=True)
            dxn = dy * g_ref[gi:gi + 1, :]
            dx = r * (dxn - xn * (_half_sum(dxn * xn, lo) * (1.0 / HEAD_DIM)))
            out_ref[:, c * PAIR:(c + 1) * PAIR] = dx.astype(BF16)

        def fold(v):
            return v + pltpu.roll(v, HEAD_DIM, 1)

        for c in range(4):
            norm_bwd(c, 0, _collect(tile, dq_refs, c) * scale)
            norm_bwd(4 + c, 1, _collect(tile, dk_refs, c))
            out_ref[:, (8 + c) * PAIR:(9 + c) * PAIR] = _collect(tile, dv_refs, c).astype(BF16)
            norm_bwd(12 + c, 2, qb_ref[c].astype(F32) * scale)
        kb, vb = kb_ref[...].astype(F32), vb_ref[...].astype(F32)
        norm_bwd(16, 3, jnp.where(lo, fold(kb[0]), fold(kb[1])))
        out_ref[:, 17 * PAIR:18 * PAIR] = jnp.where(lo, fold(vb[0]), fold(vb[1])).astype(BF16)

    four = pl.BlockSpec((4, tm, PAIR), lambda i: (0, i, 0))
    two = pl.BlockSpec((2, tm, PAIR), lambda i: (0, i, 0))
    specs, _ = _spread_specs(tm, T, BF16)
    views = [t.reshape(4, d, T // d, PAIR) for group in (dqa, dka, dva) for t, d in zip(group, DILS)]
    return pl.pallas_call(
        body, name="attn_post", grid=(T // tm,),
        in_specs=[pl.BlockSpec((tm, NQ), lambda i: (i, 0)), pl.BlockSpec((4, PAIR), lambda i: (0, 0))]
        + specs * 3 + [four, two, two],
        out_specs=[pl.BlockSpec((tm, NQ), lambda i: (i, 0)), pl.BlockSpec((4, PAIR), lambda i: (0, 0))],
        out_shape=[jax.ShapeDtypeStruct((T, NQ), BF16), jax.ShapeDtypeStruct((4, PAIR), F32)],
        scratch_shapes=[pltpu.VMEM((tm, PAIR), F32)],
        compiler_params=_params("arbitrary"),
    )(qkv, gains2, *views, dqb, dkb, dvb)


def _dense_norm_bwd(dres, dz, w, blk, x, g, tm):
    T, D = x.shape
    N = dz.shape[1]
    wspec, wload = _weight_arg(w, blk)

    def body(dres_ref, dz_ref, w_ref, x_ref, g_ref, dx_ref, dgn_ref):
        i = pl.program_id(0)
        dx, dg = _norm_bwd(_dot_nt(dz_ref[...], wload(w_ref)), x_ref[...], g_ref[...])
        dx_ref[...] = dres_ref[...] + dx

        @pl.when(i == 0)
        def _():
            dgn_ref[...] = dg

        @pl.when(i > 0)
        def _():
            dgn_ref[...] += dg

    tok = pl.BlockSpec((tm, D), lambda i: (i, 0))
    row = pl.BlockSpec((1, D), lambda i: (0, 0))
    return pl.pallas_call(
        body, name="dense_norm_bwd", grid=(T // tm,),
        in_specs=[tok, pl.BlockSpec((tm, N), lambda i: (i, 0)), wspec, tok, row],
        out_specs=[tok, row],
        out_shape=[jax.ShapeDtypeStruct((T, D), F32), jax.ShapeDtypeStruct((1, D), F32)],
        compiler_params=_params("arbitrary"),
    )(dres, dz, w, x, g)


def _bias_reduce(onehot, dbm):
    Hb, K = dbm.shape

    def body(oh_ref, d_ref, out_ref):
        oh = oh_ref[...]
        d = d_ref[...]
        hi = d.astype(BF16)
        r1 = d - hi.astype(F32)
        mid = r1.astype(BF16)
        low = (r1 - mid.astype(F32)).astype(BF16)
        out_ref[...] = _dot_nt(hi, oh) + _dot_nt(mid, oh) + _dot_nt(low, oh)

    vm = pl.BlockSpec(memory_space=pltpu.VMEM)
    return pl.pallas_call(
        body, name="bias_reduce", in_specs=[vm, vm], out_specs=vm,
        out_shape=jax.ShapeDtypeStruct((Hb, N_BUCKETS), F32),
        compiler_params=pltpu.CompilerParams(vmem_limit_bytes=VMEM_LIMIT),
    )(onehot, dbm)


def _ple_fwd(x, g, wg, blk, p, wp, target, tm):
    T, D = x.shape
    P = p.shape[1]
    with_loss = target is not None
    wspec, wload = _weight_arg(wg, blk)

    def body(*refs):
        if with_loss:
            x_ref, g_ref, wg_ref, p_ref, wp_ref, t_ref, y_ref, hn_ref, gate_ref, pp_ref, pb_ref, loss_ref = refs
        else:
            x_ref, g_ref, wg_ref, p_ref, wp_ref, y_ref, hn_ref, gate_ref, pp_ref, pb_ref = refs
        i = pl.program_id(0)
        xv = x_ref[...]
        hb = (xv * _rstd(xv) * g_ref[...]).astype(BF16)
        hn_ref[...] = hb
        gate = _sigmoid(_dot(hb, wload(wg_ref)))
        pb = p_ref[...].astype(BF16)
        pb_ref[...] = pb
        pp = _dot(pb, wp_ref[...])
        gate_ref[...] = gate
        pp_ref[...] = pp
        y = xv + gate * pp
        if with_loss:
            err = y - t_ref[...]
            y_ref[...] = err * (1.0 / D)
            part = jnp.broadcast_to(0.5 * jnp.sum(jnp.sum(err * err, axis=1, keepdims=True) * (1.0 / D),
                                                  axis=0, keepdims=True), (1, 128))

            @pl.when(i == 0)
            def _():
                loss_ref[...] = part

            @pl.when(i > 0)
            def _():
                loss_ref[...] += part
        else:
            y_ref[...] = y

    tok = pl.BlockSpec((tm, D), lambda i: (i, 0))
    ptok = pl.BlockSpec((tm, P), lambda i: (i, 0))
    in_specs = [tok, pl.BlockSpec((1, D), lambda i: (0, 0)), wspec, ptok,
                pl.BlockSpec((P, D), lambda i: (0, 0))]
    out_specs = [tok, tok, tok, tok, ptok]
    out_shape = [jax.ShapeDtypeStruct((T, D), F32), jax.ShapeDtypeStruct((T, D), BF16),
                 jax.ShapeDtypeStruct((T, D), F32), jax.ShapeDtypeStruct((T, D), F32),
                 jax.ShapeDtypeStruct((T, P), BF16)]
    args = [x, g, wg, p, wp]
    if with_loss:
        in_specs.append(tok)
        out_specs.append(pl.BlockSpec((1, 128), lambda i: (0, 0)))
        out_shape.append(jax.ShapeDtypeStruct((1, 128), F32))
        args.append(target)
    return pl.pallas_call(
        body, name="ple_fwd_loss" if with_loss else "ple_fwd", grid=(T // tm,),
        in_specs=in_specs, out_specs=out_specs, out_shape=out_shape,
        compiler_params=_params("arbitrary" if with_loss else "parallel"),
    )(*args)


def _ple_bwd(dy, gate, pp, tm, dep=None):
    T, D = dy.shape

    def body(dy_ref, gate_ref, pp_ref, dgl_ref, dpp_ref):
        d = dy_ref[...]
        gt = gate_ref[...]
        dgl_ref[...] = (d * pp_ref[...] * gt * (1.0 - gt)).astype(BF16)
        dpp_ref[...] = (d * gt).astype(BF16)

    tok = pl.BlockSpec((tm, D), lambda i: (i, 0))
    body, in_specs, args = _with_dep(body, dep, [tok, tok, tok], [dy, gate, pp])
    return pl.pallas_call(
        body, name="ple_bwd", grid=(T // tm,), in_specs=in_specs, out_specs=[tok, tok],
        out_shape=[jax.ShapeDtypeStruct((T, D), BF16), jax.ShapeDtypeStruct((T, D), BF16)],
        compiler_params=_params("parallel"),
    )(*args)


def _adamw(w, g, m, v):
    shape = w.shape
    C = shape[-1]
    w2, g2, m2, v2 = (a.reshape(-1, C) for a in (w, g, m, v))
    Rn = w2.shape[0]
    tr = Rn
    for cand in (512, 352, 256):
        if Rn % cand == 0:
            tr = cand
            break
    c1 = 1.0 - ADAM_B1 ** ADAM_STEP
    c2 = 1.0 - ADAM_B2 ** ADAM_STEP

    def body(w_ref, g_ref, m_ref, v_ref, d_ref, nm_ref, nv_ref):
        gv = g_ref[...]
        mn = ADAM_B1 * m_ref[...] + (1.0 - ADAM_B1) * gv
        vn = ADAM_B2 * v_ref[...] + (1.0 - ADAM_B2) * (gv * gv)
        d_ref[...] = -ADAM_LR * ((mn / c1) / (jnp.sqrt(vn / c2) + ADAM_EPS) + ADAM_WD * w_ref[...])
        nm_ref[...] = mn
        nv_ref[...] = vn

    spec = pl.BlockSpec((tr, C), lambda i: (i, 0))
    sh = jax.ShapeDtypeStruct((Rn, C), F32)
    d, nm, nv = pl.pallas_call(
        body, name="adamw", grid=(Rn // tr,), in_specs=[spec] * 4, out_specs=[spec] * 3, out_shape=[sh] * 3,
        compiler_params=_params("parallel"),
    )(w2, g2, m2, v2)
    return d.reshape(shape), nm.reshape(shape), nv.reshape(shape)


def _my_place():
    x, y, c = lax.axis_index("x"), lax.axis_index("y"), lax.axis_index("c")
    chips = [(1 - x, y), (x, 1 - y), (1 - x, 1 - y)]
    return x, y, c, chips


def _all_gather(arrs):
    n = len(arrs)

    def body(*refs):
        x_refs, out_refs = refs[:n], refs[n:2 * n]
        send_sems, recv_sems, local_sems = refs[2 * n:]
        x, y, c, chips = _my_place()
        me, sibling = (x, y, c), (x, y, 1 - c)

        def copy(m, k, block, to, src=None):
            rows = out_refs[m].at[4 * block[0] + 2 * block[1] + block[2]]
            return pltpu.make_async_remote_copy(
                src_ref=rows if src is None else src, dst_ref=rows,
                send_sem=send_sems.at[7 * m + k], recv_sem=recv_sems.at[7 * m + k], device_id=to, device_id_type=MESH)

        mine = [pltpu.make_async_copy(x_refs[m], out_refs[m].at[4 * x + 2 * y + c], local_sems.at[m])
                for m in range(n)]
        for cp in mine:
            cp.start()
        first = []
        for m in range(n):
            first.append(copy(m, 0, me, sibling, src=x_refs[m]))
            first += [copy(m, 1 + j, me, (*chip, c), src=x_refs[m]) for j, chip in enumerate(chips)]
        for cp in first:
            cp.start()
        passed = []
        for m in range(n):
            for j, chip in enumerate(chips):
                copy(m, 1 + j, (*chip, c), me).wait_recv()
                cp = copy(m, 4 + j, (*chip, c), sibling)
                cp.start()
                passed.append(cp)
        for m in range(n):
            copy(m, 0, sibling, me).wait_recv()
            for j, chip in enumerate(chips):
                copy(m, 4 + j, (*chip, 1 - c), me).wait_recv()
        for cp in first + passed:
            cp.wait_send()
        for cp in mine:
            cp.wait()

    hbm = pl.BlockSpec(memory_space=pl.ANY)
    return pl.pallas_call(
        body, name="all_gather", in_specs=[hbm] * n, out_specs=[hbm] * n,
        out_shape=[jax.ShapeDtypeStruct((N_DEV,) + a.shape, a.dtype) for a in arrs],
        scratch_shapes=[pltpu.SemaphoreType.DMA((7 * n,)), pltpu.SemaphoreType.DMA((7 * n,)),
                        pltpu.SemaphoreType.DMA((n,))],
    )(*arrs)


def _peer(x, y, c, k):
    return (x ^ ((k >> 2) & 1), y ^ ((k >> 1) & 1), c ^ (k & 1))


HBM_SPEC = pl.BlockSpec(memory_space=pltpu.HBM)
SEM_SPEC = pl.BlockSpec(memory_space=pltpu.SEMAPHORE)


def _exchange_refs(srcs, lands, m, k, x, y, c, scatter):
    peer = _peer(x, y, c, k)
    if scatter:
        return srcs[m].at[4 * peer[0] + 2 * peer[1] + peer[2]], lands[m].at[k - 1], peer
    return srcs[m], lands[m].at[4 * x + 2 * y + c], peer


def _exchange_start(arrs, land_shapes, scatter, name):
    n = len(arrs)

    def body(*refs):
        srcs, lands = refs[:n], refs[n:2 * n]
        send_sems, recv_sems = refs[2 * n], refs[2 * n + 1]
        token = refs[-1]
        x, y, c, _ = _my_place()
        for m in range(n):
            for k in range(1, N_DEV):
                src, dst, peer = _exchange_refs(srcs, lands, m, k, x, y, c, scatter)
                pltpu.make_async_remote_copy(
                    src_ref=src, dst_ref=dst, send_sem=send_sems.at[7 * m + k - 1],
                    recv_sem=recv_sems.at[7 * m + k - 1], device_id=peer, device_id_type=MESH).start()
        token[...] = jnp.zeros_like(token)

    zones = [lax.empty(s_, a.dtype) for s_, a in zip(land_shapes, arrs)]
    outs = pl.pallas_call(
        body, name=name,
        out_shape=(pltpu.SemaphoreType.DMA((7 * n,)), pltpu.SemaphoreType.DMA((7 * n,)),
                   *[pltpu.HBM(a.shape, a.dtype) for a in arrs], *[pltpu.HBM(z.shape, z.dtype) for z in zones],
                   jax.ShapeDtypeStruct((8, 128), F32)),
        in_specs=[HBM_SPEC] * (2 * n),
        out_specs=(SEM_SPEC, SEM_SPEC, *[HBM_SPEC] * (2 * n), pl.BlockSpec(memory_space=pltpu.VMEM)),
        input_output_aliases={m: 2 + m for m in range(2 * n)},
        compiler_params=pltpu.CompilerParams(has_side_effects=pltpu.SideEffectType.DATAFLOW_SIDE_EFFECTING),
    )(*[pltpu.with_memory_space_constraint(a, pltpu.HBM) for a in arrs],
      *[pltpu.with_memory_space_constraint(z, pltpu.HBM) for z in zones])
    return outs[0], outs[1], list(outs[2:2 + n]), list(outs[2 + n:2 + 2 * n]), outs[-1]


def _exchange_wait(send_sems, recv_sems, arrs, zones, after, scatter, name):
    n = len(arrs)
    afters = list(after) if isinstance(after, (list, tuple)) else [after]

    def body(*refs):
        srcs, lands = refs[:n], refs[n:2 * n]
        send_sems, recv_sems = refs[2 * n], refs[2 * n + 1]
        x, y, c, _ = _my_place()
        for m in range(n):
            for k in range(1, N_DEV):
                src, dst, peer = _exchange_refs(srcs, lands, m, k, x, y, c, scatter)
                cp = pltpu.make_async_remote_copy(
                    src_ref=src, dst_ref=dst, send_sem=send_sems.at[7 * m + k - 1],
                    recv_sem=recv_sems.at[7 * m + k - 1], device_id=peer, device_id_type=MESH)
                cp.wait_send()
                cp.wait_recv()

    outs = pl.pallas_call(
        body, name=name,
        out_shape=tuple(pltpu.HBM(a.shape, a.dtype) for a in list(arrs) + list(zones)),
        in_specs=[HBM_SPEC] * (2 * n) + [SEM_SPEC, SEM_SPEC] + [pl.BlockSpec(memory_space=pl.ANY)] * len(afters),
        out_specs=tuple([HBM_SPEC] * (2 * n)),
        input_output_aliases={m: m for m in range(2 * n)},
        compiler_params=pltpu.CompilerParams(has_side_effects=pltpu.SideEffectType.DATAFLOW_SIDE_EFFECTING),
    )(*arrs, *zones, send_sems, recv_sems, *afters)
    return list(outs[n:])


def _sum_parts(own, parts, tr, dep=None):
    R, W = own.shape

    def body(own_ref, parts_ref, out_ref):
        acc = own_ref[...].astype(F32)
        for k in range(N_DEV - 1):
            acc = acc + parts_ref[k].astype(F32)
        out_ref[...] = acc

    in_specs = [pl.BlockSpec((tr, W), lambda i: (i, 0)), pl.BlockSpec((N_DEV - 1, tr, W), lambda i: (0, i, 0))]
    body, in_specs, args = _with_dep(body, dep, in_specs, [own, parts])
    return pl.pallas_call(
        body, name="sum_parts", grid=(R // tr,),
        in_specs=in_specs,
        out_specs=pl.BlockSpec((tr, W), lambda i: (i, 0)),
        out_shape=jax.ShapeDtypeStruct((R, W), F32),
        compiler_params=_params("parallel"),
    )(*args)


def _all_reduce_small(v, dep=None):
    Rn, Wd = v.shape

    def body(v_ref, out_ref, gat_ref, send_sems, recv_sems):
        x, y, c, _ = _my_place()
        me = 4 * x + 2 * y + c
        gat_ref[me] = v_ref[...]
        copies = []
        for k in range(1, N_DEV):
            fx, fy, fc = (k >> 2) & 1, (k >> 1) & 1, k & 1
            peer = (x ^ fx, y ^ fy, c ^ fc)
            cp = pltpu.make_async_remote_copy(
                src_ref=v_ref, dst_ref=gat_ref.at[me], send_sem=send_sems.at[k - 1], recv_sem=recv_sems.at[k - 1],
                device_id=peer, device_id_type=MESH)
            cp.start()
            copies.append(cp)
        for cp in copies:
            cp.wait_recv()
        for cp in copies:
            cp.wait_send()
        acc = gat_ref[0]
        for k in range(1, N_DEV):
            acc = acc + gat_ref[k]
        out_ref[...] = acc

    vm = pl.BlockSpec(memory_space=pltpu.VMEM)
    body, in_specs, args = _with_dep(body, dep, [vm], [v])
    return pl.pallas_call(
        body, name="all_reduce_small", in_specs=in_specs, out_specs=vm,
        out_shape=jax.ShapeDtypeStruct((Rn, Wd), F32),
        scratch_shapes=[pltpu.VMEM((N_DEV, Rn, Wd), F32), pltpu.SemaphoreType.DMA((7,)),
                        pltpu.SemaphoreType.DMA((7,))],
    )(*args)


def _t5_bucket(rel):
    half = N_BUCKETS // 2
    max_exact = half // 2
    ret = jnp.where(rel > 0, half, 0)
    n = jnp.abs(rel)
    nf = jnp.maximum(n, 1).astype(F32)
    large = max_exact + (jnp.log(nf / max_exact) / math.log(MAX_DISTANCE / max_exact)
                         * (half - max_exact)).astype(jnp.int32)
    large = jnp.minimum(large, half - 1)
    return ret + jnp.where(n < max_exact, n, large)


def _band(R, d):
    W = BQ + 2 * R
    rel = jnp.arange(W)[None, :] - R - jnp.arange(BQ)[:, None]
    return _t5_bucket(rel * d), jnp.abs(rel) <= R


def _onehot(R, d):
    bkt, in_band = _band(R, d)
    return ((bkt.reshape(1, -1) == jnp.arange(N_BUCKETS)[:, None]) & in_band.reshape(1, -1)).astype(BF16)


def _bias_expand(table_t, onehot):
    H = table_t.shape[0]
    K = onehot.shape[1]

    def body(t_ref, oh_ref, out_ref):
        oh = oh_ref[...]
        t = t_ref[...]
        hi = t.astype(BF16)
        r1 = t - hi.astype(F32)
        mid = r1.astype(BF16)
        low = (r1 - mid.astype(F32)).astype(BF16)
        marked = _dot(jnp.ones(t.shape, BF16), oh) > 0.5
        out_ref[...] = jnp.where(marked, _dot(hi, oh) + _dot(mid, oh) + _dot(low, oh), NEG)

    vm = pl.BlockSpec(memory_space=pltpu.VMEM)
    return pl.pallas_call(
        body, name="bias_expand", in_specs=[vm, vm], out_specs=vm,
        out_shape=jax.ShapeDtypeStruct((H, K), F32),
        compiler_params=pltpu.CompilerParams(vmem_limit_bytes=VMEM_LIMIT),
    )(table_t, onehot)


def _bias_matrix(table, R, d):
    return _bias_expand(table.T, _onehot(R, d)).reshape(table.shape[1], BQ, BQ + 2 * R)


def _bias_variants(base, R):
    H, _, W = base.shape
    fill = jnp.full((H, BQ, R), NEG, F32)
    first = jnp.concatenate([base[:, :, R:], fill], axis=2)
    last = jnp.concatenate([fill, base[:, :, :W - R]], axis=2)
    v = jnp.stack([base, first, last], axis=1)
    v = v.reshape(H // 2, 2, 3, BQ, W).transpose(0, 2, 1, 3, 4).reshape(H // 2, 3, 2 * BQ, W)
    return v, v.transpose(0, 1, 3, 2)


def _bias_grad(dbt, R, d):
    P, _, W, _ = dbt.shape
    dbt = dbt[:, 0].at[:, R:].add(dbt[:, 1, :W - R]).at[:, :W - R].add(dbt[:, 2, R:])
    dbm = dbt.reshape(P, W, 2, BQ).transpose(0, 2, 3, 1).reshape(2 * P, BQ * W)
    return _bias_reduce(_onehot(R, d), dbm).T


def _tile2(gain):
    return jnp.concatenate([gain, gain])


ROW_W_O, ROW_GATE, B_ROWS = 768, 896, 1024
BLK_W_O, BLK_GATE = ROW_W_O // 128, ROW_GATE // 128


def _pack_layer(wts, i):
    a = jnp.stack([wts["ffn1_w_in"][i], wts["ffn2_w_in"][i]])
    D = a.shape[1]
    b = jnp.concatenate([
        wts["ffn1_w_out"][i], wts["ffn2_w_out"][i],
        jnp.zeros((ROW_W_O - 2 * wts["ffn1_w_out"].shape[1], D), a.dtype), wts["w_o"][i], wts["w_ple_gate"][i]])
    return a, b, wts["w_qkv"][i], wts["w_ple_proj"][i]


def _unpack_layer(sums, like):
    w_in2, b1, proj, w_o, qkv, w_in1, w_out1 = sums
    n_out = like["ffn1_w_out"].shape[1]
    out = {}
    if w_in2 is not None:
        out.update(ffn2_w_in=w_in2, ffn2_w_out=b1[:n_out], w_ple_gate=b1[n_out:], w_ple_proj=proj)
    if w_o is not None:
        out.update(w_o=w_o, w_qkv=qkv)
    if w_in1 is not None:
        out.update(ffn1_w_in=w_in1, ffn1_w_out=w_out1)
    return out


def _col_sharded(g):
    return g.transpose(1, 0, 2).reshape(g.shape[1], -1)


def _to_col_shards(g):
    rows = g.shape[0]
    return g.reshape(rows, N_DEV, -1).transpose(1, 0, 2)


def _layer_weights(ga, gb, gq, gp):
    return dict(ga=ga, gb=gb, w_qkv=_col_sharded(gq), w_proj=_col_sharded(gp))


def _layer_fwd(x, p, w, sm, i, target, tm, biases, dep=None):
    ga, gb = w["ga"], w["gb"]
    saved = {}
    saved["x0"] = x
    x1, saved["h1"], saved["zg1"], saved["zu1"], saved["s1"] = _ffn_fwd(
        x, sm["norm_ffn1"][i][None], ga, gb, 0, 2 * tm, dep)
    saved["x1"] = x1
    qkv, saved["hm"] = _qkv_fwd(x1, sm["norm_mix"][i][None], w["w_qkv"], 2 * tm)
    saved["qkv"] = qkv
    gains2 = jnp.stack([_tile2(sm[k][i]) for k in ("q_norm_a", "k_norm_a", "q_norm_b", "k_norm_b")])
    saved["gains2"] = gains2
    qb, kb, vb, qkv_d = _attn_prep(qkv, gains2, tm)
    no_sink = jnp.full((8,), NEG, F32)
    branches = []
    outs = []
    for (R, d), bias, (qd, kd, vd) in zip(DILATED, biases[:3], qkv_d):
        sink = jnp.tile(no_sink, d)
        outs.append(_attn_fwd(qd, kd, vd, bias[0], sink, R, 1, d))
        branches.append((qd, kd, vd, bias, sink, R, d))
    bias_b = biases[3]
    sink_b = sm["sink_b"][i]
    ob, lb = _attn_fwd(qb, kb, vb, bias_b[0], sink_b, SWA_RADIUS, 2, 1)
    merged, o_cat = _attn_merge(outs, ob, tm)
    saved.update(branches=branches, b=(qb, kb, vb, bias_b, sink_b), merged=merged, ob=ob, lb=lb, o_cat=o_cat)
    x2 = _oproj_fwd(x1, o_cat, gb, BLK_W_O, 2 * tm)
    saved["x2"] = x2
    x3, saved["h2"], saved["zg2"], saved["zu2"], saved["s2"] = _ffn_fwd(
        x2, sm["norm_ffn2"][i][None], ga, gb, 1, 2 * tm)
    saved["x3"] = x3
    res = _ple_fwd(x3, sm["norm_ple"][i][None], gb, BLK_GATE, p, w["w_proj"], target, tm)
    y, saved["hp"], saved["gate"], saved["pp"], saved["pb"] = res[:5]
    loss = res[5] if target is not None else None
    return y, loss, saved


def _layer_bwd(dy, w, sm, i, sv, tm, dep=None, on_ready=None, on_small=None, on_last=None):
    ga, gb = w["ga"], w["gb"]
    gs = {}
    D = dy.shape[1]
    dgl, dpp = _ple_bwd(dy, sv["gate"], sv["pp"], tm, dep)
    d_gate = _matmul_tn(sv["hp"], dgl, D, 2 * tm)
    d_proj = _matmul_tn(sv["pb"], dpp, D, 2 * tm)
    dx3, gs["norm_ple"] = _dense_norm_bwd(dy, dgl, gb, BLK_GATE, sv["x3"], sm["norm_ple"][i][None], 2 * tm)
    dx2, dyb, dzg, dzu, gs["norm_ffn2"] = _ffn_bwd(dx3, sv["x2"], sm["norm_ffn2"][i][None], sv["zg2"], sv["zu2"],
                                                   ga, gb, 1, 2 * tm)
    dwin2, dwo2 = _ffn_dw(sv["h2"], dzg, dzu, sv["s2"], dyb, 2 * tm)
    half = dwo2.shape[1] // 2
    after_ffn2 = [dwin2, jnp.concatenate([dwo2.reshape(N_DEV, half, D), d_gate.reshape(N_DEV, -1, D)], axis=1),
                  _to_col_shards(d_proj)]
    token = None if on_ready is None else on_ready(0, after_ffn2)
    dx2b, do_b, do_a = _oproj_bwd(dx2, gb, BLK_W_O, tm, token)
    d_wo = _matmul_tn(sv["o_cat"], dx2b, D, 2 * tm)
    dqa, dka, dva, dbias = [], [], [], []
    for (qd, kd, vd, bias, sink, R, d), (oa, la), do_d in zip(sv["branches"], sv["merged"], do_a):
        dq, dk, dv, dbm, _ = _attn_bwd(qd, kd, vd, bias[1], sink, oa, la, do_d, R, 1, d)
        dqa.append(dq)
        dka.append(dk)
        dva.append(dv)
        dbias.append(dbm)
    qb, kb, vb, bias_b, sink_b = sv["b"]
    dqb, dkb, dvb, dbm_b, dsink = _attn_bwd(qb, kb, vb, bias_b[1], sink_b, sv["ob"], sv["lb"], do_b,
                                            SWA_RADIUS, 2, 1)
    gs["rel_bias"] = dbias + [dbm_b]
    gs["sink_b"] = jnp.sum(dsink[:, 0].reshape(-1, 2, BQ), axis=2).reshape(-1)
    dqkv, dgains2 = _attn_post(sv["qkv"], sv["gains2"], dqa, dka, dva, dqb,
                               dkb, dvb, tm)
    dgains = dgains2[:, :HEAD_DIM] + dgains2[:, HEAD_DIM:]
    for k, name in enumerate(("q_norm_a", "k_norm_a", "q_norm_b", "k_norm_b")):
        gs[name] = dgains[k]
    d_qkv = _matmul_tn(sv["hm"], dqkv, dqkv.shape[1] // 2, 2 * tm)
    after_mixer = [d_wo.reshape(N_DEV, -1, D), _to_col_shards(d_qkv)]
    token = None if on_ready is None else on_ready(1, after_mixer)
    dx1, gs["norm_mix"] = _dense_norm_bwd(dx2, dqkv, w["w_qkv"], None, sv["x1"], sm["norm_mix"][i][None], 2 * tm)
    g1 = sm["norm_ffn1"][i][None]
    if on_last is None:
        dx0, dyb, dzg, dzu, gs["norm_ffn1"] = _ffn_bwd(dx1, sv["x0"], g1, sv["zg1"], sv["zu1"], ga, gb, 0, 2 * tm,
                                                       token)
        dwin1, dwo1 = _ffn_dw(sv["h1"], dzg, dzu, sv["s1"], dyb, 2 * tm)
        return dx0, (after_ffn2, after_mixer, [dwin1, dwo1.reshape(N_DEV, half, D)]), gs
    dyb, dzg, dzu = _ffn_bwd_dz(dx1, sv["zg1"], sv["zu1"], gb, 0, 2 * tm, token)
    dwin1, dwo1 = _ffn_dw(sv["h1"], dzg, dzu, sv["s1"], dyb, 2 * tm, on_small(gs))
    last = [dwin1, dwo1.reshape(N_DEV, half, D)]
    dx0, gs["norm_ffn1"] = _ffn_bwd_dx(dx1, sv["x0"], g1, dzg, dzu, ga, 0, 2 * tm, on_last(last))
    return dx0, (after_ffn2, after_mixer, last), gs


def _bias_matrices(rel_bias):
    biases = [_bias_variants(_bias_matrix(rel_bias[:, :8], R, d), R) for R, d in DILATED]
    biases.append(_bias_variants(_bias_matrix(rel_bias[:, 8:], SWA_RADIUS, 1), SWA_RADIUS))
    return biases


def _stack_small(per_layer):
    small = {}
    for k, v in per_layer.items():
        if k == "rel_bias":
            per_branch = [sum(parts) for parts in zip(*v.values())]
            drel_a = sum(_bias_grad(t, R, d) for t, (R, d) in zip(per_branch[:3], DILATED))
            small[k] = jnp.concatenate([drel_a, _bias_grad(per_branch[3], SWA_RADIUS, 1)], axis=1)
        else:
            small[k] = jnp.stack([v[i].reshape(-1) for i in sorted(v)])
    return small


TM = 512
SUM_TILES = (512, 480, 256, 128, 512, 512, 352)
LAST_GROUP = ("ffn1_w_in", "ffn1_w_out")


def _pack_small(d, extra=None):
    parts = [d[k].reshape(-1) for k in SMALL]
    if extra is not None:
        parts.append(extra.reshape(-1))
    flat = jnp.concatenate(parts)
    return jnp.pad(flat, (0, SMALL_ROWS * 128 - flat.shape[0])).reshape(SMALL_ROWS, 128)


def _unpack_small(buf, like):
    flat = buf.reshape(-1)
    out, off = {}, 0
    for k in SMALL:
        n = like[k].size
        out[k] = flat[off:off + n].reshape(like[k].shape)
        off += n
    return out, flat[off]


def kernel(x, p, rel_bias, norm_ffn1, ffn1_w_in, ffn1_w_out, norm_mix, w_qkv, q_norm_a, k_norm_a, q_norm_b, k_norm_b, sink_b, w_o, norm_ffn2, ffn2_w_in, ffn2_w_out, norm_ple, w_ple_gate, w_ple_proj, loss_target, m_rel_bias, m_norm_ffn1, m_ffn1_w_in, m_ffn1_w_out, m_norm_mix, m_w_qkv, m_q_norm_a, m_k_norm_a, m_q_norm_b, m_k_norm_b, m_sink_b, m_w_o, m_norm_ffn2, m_ffn2_w_in, m_ffn2_w_out, m_norm_ple, m_w_ple_gate, m_w_ple_proj, v_rel_bias, v_norm_ffn1, v_ffn1_w_in, v_ffn1_w_out, v_norm_mix, v_w_qkv, v_q_norm_a, v_k_norm_a, v_q_norm_b, v_k_norm_b, v_sink_b, v_w_o, v_norm_ffn2, v_ffn2_w_in, v_ffn2_w_out, v_norm_ple, v_w_ple_gate, v_w_ple_proj):
    wts = dict(rel_bias=rel_bias, norm_ffn1=norm_ffn1, ffn1_w_in=ffn1_w_in, ffn1_w_out=ffn1_w_out,
               norm_mix=norm_mix, w_qkv=w_qkv, q_norm_a=q_norm_a, k_norm_a=k_norm_a, q_norm_b=q_norm_b,
               k_norm_b=k_norm_b, sink_b=sink_b, w_o=w_o, norm_ffn2=norm_ffn2, ffn2_w_in=ffn2_w_in,
               ffn2_w_out=ffn2_w_out, norm_ple=norm_ple, w_ple_gate=w_ple_gate, w_ple_proj=w_ple_proj)
    mom = dict(rel_bias=m_rel_bias, norm_ffn1=m_norm_ffn1, ffn1_w_in=m_ffn1_w_in, ffn1_w_out=m_ffn1_w_out,
               norm_mix=m_norm_mix, w_qkv=m_w_qkv, q_norm_a=m_q_norm_a, k_norm_a=m_k_norm_a, q_norm_b=m_q_norm_b,
               k_norm_b=m_k_norm_b, sink_b=m_sink_b, w_o=m_w_o, norm_ffn2=m_norm_ffn2, ffn2_w_in=m_ffn2_w_in,
               ffn2_w_out=m_ffn2_w_out, norm_ple=m_norm_ple, w_ple_gate=m_w_ple_gate, w_ple_proj=m_w_ple_proj)
    var = dict(rel_bias=v_rel_bias, norm_ffn1=v_norm_ffn1, ffn1_w_in=v_ffn1_w_in, ffn1_w_out=v_ffn1_w_out,
               norm_mix=v_norm_mix, w_qkv=v_w_qkv, q_norm_a=v_q_norm_a, k_norm_a=v_k_norm_a, q_norm_b=v_q_norm_b,
               k_norm_b=v_k_norm_b, sink_b=v_sink_b, w_o=v_w_o, norm_ffn2=v_norm_ffn2, ffn2_w_in=v_ffn2_w_in,
               ffn2_w_out=v_ffn2_w_out, norm_ple=v_norm_ple, w_ple_gate=v_w_ple_gate, w_ple_proj=v_w_ple_proj)
    sm = {k: wts[k] for k in SMALL}
    me = 4 * lax.axis_index("x") + 2 * lax.axis_index("y") + lax.axis_index("c")
    packed = []
    for i in range(2):
        a, *rest = _pack_layer(wts, i)
        packed.append([t.astype(BF16) for t in [a.reshape(-1, a.shape[-1])] + rest])
    a_shape = (2, ffn1_w_in.shape[1], ffn1_w_in.shape[2])

    def weights_of(zones):
        return _layer_weights(zones[0].reshape((N_DEV,) + a_shape), *zones[1:])

    w0 = weights_of(_all_gather(packed[0]))
    zone_shapes = [(N_DEV,) + t.shape for t in packed[1]]
    ssem, rsem, thru, zones, token = _exchange_start(packed[1], zone_shapes, False, "gather_start")
    biases = _bias_matrices(rel_bias)
    x1, _, sv0 = _layer_fwd(x[0], p[0, 0], w0, sm, 0, None, TM, biases, dep=token)
    zones = _exchange_wait(ssem, rsem, thru, zones, x1, False, "gather_wait")
    w1 = weights_of([lax.dynamic_update_index_in_dim(z, t, me, 0) for z, t in zip(zones, packed[1])])
    dy, loss, sv1 = _layer_fwd(x1, p[1, 0], w1, sm, 1, loss_target[0], TM, biases)

    def slots_for(arrs):
        return [(N_DEV - 1,) + t.shape[1:] for t in arrs]

    held1, held = {}, {}

    def on_ready1(stage, group):
        held1[stage] = _exchange_start(group, slots_for(group), True, f"scatter1_start_{stage}")
        return held1[stage][4]

    dx1, groups1, gs1 = _layer_bwd(dy, w1, sm, 1, sv1, TM, on_ready=on_ready1)
    on_ready1(2, groups1[2])
    g1 = groups1[0] + groups1[1] + groups1[2]

    def on_ready(stage, group):
        if stage == 1:
            held["slots1"] = [t for st in (0, 1, 2)
                              for t in _exchange_wait(*held1[st][:4], group[0], True, f"scatter1_wait_{st}")]
        held[stage] = _exchange_start(group, slots_for(group), True, f"scatter_start_{stage}")
        return held[stage][4]

    def on_small(gs0):
        part = dict(gs0, norm_ffn1=jnp.zeros_like(gs1["norm_ffn1"]))
        gsmall = _stack_small({k: {0: part[k], 1: gs1[k]} for k in part})
        held["small"] = _all_reduce_small(_pack_small(gsmall, loss[0, :1]))
        return held["small"]

    def on_last(group):
        held["last"] = _exchange_start(group, slots_for(group), True, "scatter_start_2")
        return held["last"][4]

    dx, groups0, gs0 = _layer_bwd(dx1, w0, sm, 0, sv0, TM, dep=held1[2][4], on_ready=on_ready, on_small=on_small,
                                  on_last=on_last)
    last = groups0[2]
    slots0 = [_exchange_wait(*held[stage][:4], last[0], True, f"scatter_wait_{stage}") for stage in (0, 1)]

    def summed(arrs, slots, tiles, dep=None):
        return [_sum_parts(lax.dynamic_index_in_dim(t, me, 0, keepdims=False), s_, tr, dep)
                for t, s_, tr in zip(arrs, slots, tiles)]

    cover = held["last"][4]
    r1 = summed(g1, held["slots1"], SUM_TILES, cover)
    r0 = summed(groups0[0], slots0[0], SUM_TILES[:3], cover) + summed(groups0[1], slots0[1], SUM_TILES[3:5], cover)

    def update(names, layers):
        for k in names:
            grads[k] = jnp.stack([layers[0][k], layers[1][k]])
            delta[k], new_m[k], new_v[k] = _adamw(wts[k], grads[k], mom[k], var[k])

    grads, delta, new_m, new_v = {}, {}, {}, {}
    layer1 = _unpack_layer(r1, wts)
    update([k for k in BIG if k not in LAST_GROUP], [_unpack_layer(r0 + [None, None], wts), layer1])

    cover_done = [dx] + [delta[k] for k in BIG if k not in LAST_GROUP]
    slots_last = _exchange_wait(*held["last"][:4], cover_done, True, "scatter_wait_2")
    update(LAST_GROUP, [_unpack_layer([None] * 5 + summed(last, slots_last, SUM_TILES[5:]), wts), layer1])
    late = _all_reduce_small(gs0["norm_ffn1"].reshape(-1, 128), dep=slots_last[0])
    small_sum, loss_sum = _unpack_small(held["small"], sm)
    small_sum["norm_ffn1"] = small_sum["norm_ffn1"].at[0].add(late.reshape(-1))
    grads.update(small_sum)
    zeros = {k: jnp.zeros_like(wts[k]) for k in SMALL}
    ds, ms, vs = _adamw(_pack_small(wts), _pack_small(small_sum), _pack_small(mom), _pack_small(var))
    for packed, dst in ((ds, delta), (ms, new_m), (vs, new_v)):
        dst.update(_unpack_small(packed, zeros)[0])

    return (loss_sum, dx[None], *[grads[k] for k in WEIGHTS], *[delta[k] for k in WEIGHTS],
            *[new_m[k] for k in WEIGHTS], *[new_v[k] for k in WEIGHTS])
```

```python
import functools
import math

import jax
import jax.numpy as jnp
from jax import lax
from jax.experimental import pallas as pl
from jax.experimental.pallas import tpu as pltpu

F32 = jnp.float32
BF16 = jnp.bfloat16

N_DEV = 8
HEAD_DIM = 64
PAIR = 2 * HEAD_DIM
BQ = 128
N_BUCKETS = 32
MAX_DISTANCE = 1024
DILATED = ((64, 1), (64, 4), (64, 16))
SWA_RADIUS = 128
EPS = 1e-6
NEG = -1e30
ADAM_LR, ADAM_B1, ADAM_B2, ADAM_EPS, ADAM_WD, ADAM_STEP = 0.001, 0.9, 0.999, 1e-08, 0.01, 10
VMEM_LIMIT = 56 * 1024 * 1024
MESH = pl.DeviceIdType.MESH

BIG = ("ffn1_w_in", "ffn1_w_out", "w_qkv", "w_o", "ffn2_w_in", "ffn2_w_out", "w_ple_gate", "w_ple_proj")
SMALL = ("rel_bias", "norm_ffn1", "norm_mix", "q_norm_a", "k_norm_a", "q_norm_b", "k_norm_b", "sink_b",
         "norm_ffn2", "norm_ple")
WEIGHTS = ("rel_bias", "norm_ffn1", "ffn1_w_in", "ffn1_w_out", "norm_mix", "w_qkv", "q_norm_a", "k_norm_a",
           "q_norm_b", "k_norm_b", "sink_b", "w_o", "norm_ffn2", "ffn2_w_in", "ffn2_w_out", "norm_ple",
           "w_ple_gate", "w_ple_proj")
SMALL_ROWS = 96


def _params(*sem):
    return pltpu.CompilerParams(dimension_semantics=sem, vmem_limit_bytes=VMEM_LIMIT)


def _dot(a, b):
    return jnp.dot(a, b, preferred_element_type=F32)


def _dot_nt(a, b):
    return lax.dot_general(a, b, (((1,), (1,)), ((), ())), preferred_element_type=F32)


def _dot_tn(a, b):
    return lax.dot_general(a, b, (((0,), (0,)), ((), ())), preferred_element_type=F32)


def _sigmoid(x):
    return 1.0 / (1.0 + jnp.exp(-x))


def _rstd(xv):
    return lax.rsqrt(jnp.mean(xv * xv, axis=-1, keepdims=True) + EPS)


def _norm_bwd(dh, xv, gv):
    r = _rstd(xv)
    xn = xv * r
    dg = jnp.sum(dh * xn, axis=0, keepdims=True)
    dxn = dh * gv
    dx = r * (dxn - xn * jnp.mean(dxn * xn, axis=-1, keepdims=True))
    return dx, dg


def _lo_mask(shape):
    return lax.broadcasted_iota(jnp.int32, shape, len(shape) - 1) < HEAD_DIM


def _half_sum(t, lo):
    s0 = jnp.sum(jnp.where(lo, t, 0.0), axis=1, keepdims=True)
    s1 = jnp.sum(jnp.where(lo, 0.0, t), axis=1, keepdims=True)
    return jnp.where(lo, s0, s1)


FFN_PARTS = 2


def _ffn_weight_specs(f, nj, D, C):
    return [pl.BlockSpec((None, None, D, C), lambda i, j: (j, f, 0, 0)),
            pl.BlockSpec((None, None, D, C), lambda i, j: (j + nj, f, 0, 0)),
            pl.BlockSpec((2, C // 2, D), lambda i, j: (j, f, 0))]


def _with_dep(body, dep, in_specs, args):
    if dep is None:
        return body, in_specs, args

    def body_after(dep_ref, *refs):
        body(*refs)

    return body_after, [pl.BlockSpec(memory_space=pl.ANY)] + in_specs, [dep] + args


def _ffn_fwd(x, g, ga, gb, f, tm, dep=None):
    T, D = x.shape
    nj, C = ga.shape[0] // 2, ga.shape[3]

    def body(x_ref, g_ref, wg_ref, wu_ref, wo_ref, xo_ref, h_ref, zg_ref, zu_ref, s_ref, h_scr, acc):
        j = pl.program_id(1)

        @pl.when(j == 0)
        def _():
            xv = x_ref[...]
            hb = (xv * _rstd(xv) * g_ref[...]).astype(BF16)
            h_scr[...] = hb
            h_ref[...] = hb
            acc[...] = jnp.zeros_like(acc)

        wo = wo_ref[...].reshape(C, D)
        for part in range(FFN_PARTS):
            sl = pl.ds(part * (tm // FFN_PARTS), tm // FFN_PARTS)
            hb = h_scr[sl, :]
            gt = _dot(hb, wg_ref[...])
            up = _dot(hb, wu_ref[...])
            s = (gt * _sigmoid(gt) * up).astype(BF16)
            zg_ref[sl, :] = gt.astype(BF16)
            zu_ref[sl, :] = up.astype(BF16)
            s_ref[sl, :] = s
            acc[sl, :] += _dot(s, wo)

        @pl.when(j == nj - 1)
        def _():
            xo_ref[...] = x_ref[...] + 0.5 * acc[...]

    tok = pl.BlockSpec((tm, D), lambda i, j: (i, 0))
    chunk = pl.BlockSpec((None, tm, C), lambda i, j: (j, i, 0))
    in_specs = [tok, pl.BlockSpec((1, D), lambda i, j: (0, 0))] + _ffn_weight_specs(f, nj, D, C)
    body, in_specs, args = _with_dep(body, dep, in_specs, [x, g, ga, ga, gb])
    return pl.pallas_call(
        body, name="ffn_fwd", grid=(T // tm, nj),
        in_specs=in_specs,
        out_specs=[tok, tok, chunk, chunk, chunk],
        out_shape=[jax.ShapeDtypeStruct((T, D), F32), jax.ShapeDtypeStruct((T, D), BF16),
                   jax.ShapeDtypeStruct((nj, T, C), BF16), jax.ShapeDtypeStruct((nj, T, C), BF16),
                   jax.ShapeDtypeStruct((nj, T, C), BF16)],
        scratch_shapes=[pltpu.VMEM((tm, D), BF16), pltpu.VMEM((tm, D), F32)],
        compiler_params=_params("parallel", "arbitrary"),
    )(*args)


def _ffn_bwd(dxo, x, g, zg, zu, ga, gb, f, tm, dep=None):
    T, D = x.shape
    nj, C = ga.shape[0] // 2, ga.shape[3]

    def body(dxo_ref, x_ref, g_ref, zg_ref, zu_ref, wg_ref, wu_ref, wo_ref,
             dx_ref, dy_ref, dzg_ref, dzu_ref, dgn_ref, dy_scr, acc):
        i, j = pl.program_id(0), pl.program_id(1)

        @pl.when(j == 0)
        def _():
            dyb = (0.5 * dxo_ref[...]).astype(BF16)
            dy_scr[...] = dyb
            dy_ref[...] = dyb
            acc[...] = jnp.zeros_like(acc)

        wo = wo_ref[...].reshape(C, D)
        for part in range(FFN_PARTS):
            sl = pl.ds(part * (tm // FFN_PARTS), tm // FFN_PARTS)
            ds = _dot_nt(dy_scr[sl, :], wo)
            gt = zg_ref[sl, :].astype(F32)
            up = zu_ref[sl, :].astype(F32)
            sg = _sigmoid(gt)
            dgt = (ds * up * (sg * (1.0 + gt * (1.0 - sg)))).astype(BF16)
            dup = (ds * (gt * sg)).astype(BF16)
            dzg_ref[sl, :] = dgt
            dzu_ref[sl, :] = dup
            acc[sl, :] += _dot_nt(dgt, wg_ref[...]) + _dot_nt(dup, wu_ref[...])

        @pl.when(j == nj - 1)
        def _():
            dx, dg = _norm_bwd(acc[...], x_ref[...], g_ref[...])
            dx_ref[...] = dxo_ref[...] + dx

            @pl.when(i == 0)
            def _():
                dgn_ref[...] = dg

            @pl.when(i > 0)
            def _():
                dgn_ref[...] += dg

    tok = pl.BlockSpec((tm, D), lambda i, j: (i, 0))
    chunk = pl.BlockSpec((None, tm, C), lambda i, j: (j, i, 0))
    row = pl.BlockSpec((1, D), lambda i, j: (0, 0))
    in_specs = [tok, tok, row, chunk, chunk] + _ffn_weight_specs(f, nj, D, C)
    body, in_specs, args = _with_dep(body, dep, in_specs, [dxo, x, g, zg, zu, ga, ga, gb])
    return pl.pallas_call(
        body, name="ffn_bwd", grid=(T // tm, nj),
        in_specs=in_specs,
        out_specs=[tok, tok, chunk, chunk, row],
        out_shape=[jax.ShapeDtypeStruct((T, D), F32), jax.ShapeDtypeStruct((T, D), BF16),
                   jax.ShapeDtypeStruct((nj, T, C), BF16), jax.ShapeDtypeStruct((nj, T, C), BF16),
                   jax.ShapeDtypeStruct((1, D), F32)],
        scratch_shapes=[pltpu.VMEM((tm, D), BF16), pltpu.VMEM((tm, D), F32)],
        compiler_params=_params("arbitrary", "arbitrary"),
    )(*args)


def _ffn_bwd_dz(dxo, zg, zu, gb, f, tm, dep=None):
    T, D = dxo.shape
    nj, C = zg.shape[0], zg.shape[2]

    def body(dxo_ref, zg_ref, zu_ref, wo_ref, dy_ref, dzg_ref, dzu_ref, dy_scr):
        @pl.when(pl.program_id(1) == 0)
        def _():
            dyb = (0.5 * dxo_ref[...]).astype(BF16)
            dy_scr[...] = dyb
            dy_ref[...] = dyb

        wo = wo_ref[...].reshape(C, D)
        for part in range(FFN_PARTS):
            sl = pl.ds(part * (tm // FFN_PARTS), tm // FFN_PARTS)
            ds = _dot_nt(dy_scr[sl, :], wo)
            gt = zg_ref[sl, :].astype(F32)
            up = zu_ref[sl, :].astype(F32)
            sg = _sigmoid(gt)
            dzg_ref[sl, :] = (ds * up * (sg * (1.0 + gt * (1.0 - sg)))).astype(BF16)
            dzu_ref[sl, :] = (ds * (gt * sg)).astype(BF16)

    tok = pl.BlockSpec((tm, D), lambda i, j: (i, 0))
    chunk = pl.BlockSpec((None, tm, C), lambda i, j: (j, i, 0))
    in_specs = [tok, chunk, chunk, _ffn_weight_specs(f, nj, D, C)[2]]
    body, in_specs, args = _with_dep(body, dep, in_specs, [dxo, zg, zu, gb])
    return pl.pallas_call(
        body, name="ffn_bwd_dz", grid=(T // tm, nj),
        in_specs=in_specs, out_specs=[tok, chunk, chunk],
        out_shape=[jax.ShapeDtypeStruct((T, D), BF16), jax.ShapeDtypeStruct((nj, T, C), BF16),
                   jax.ShapeDtypeStruct((nj, T, C), BF16)],
        scratch_shapes=[pltpu.VMEM((tm, D), BF16)],
        compiler_params=_params("parallel", "arbitrary"),
    )(*args)


def _ffn_bwd_dx(dxo, x, g, dzg, dzu, ga, f, tm, dep=None):
    T, D = x.shape
    nj, C = ga.shape[0] // 2, ga.shape[3]

    def body(dxo_ref, x_ref, g_ref, dzg_ref, dzu_ref, wg_ref, wu_ref, dx_ref, dgn_ref, acc):
        i, j = pl.program_id(0), pl.program_id(1)

        @pl.when(j == 0)
        def _():
            acc[...] = jnp.zeros_like(acc)

        acc[...] += _dot_nt(dzg_ref[...], wg_ref[...]) + _dot_nt(dzu_ref[...], wu_ref[...])

        @pl.when(j == nj - 1)
        def _():
            dx, dg = _norm_bwd(acc[...], x_ref[...], g_ref[...])
            dx_ref[...] = dxo_ref[...] + dx

            @pl.when(i == 0)
            def _():
                dgn_ref[...] = dg

            @pl.when(i > 0)
            def _():
                dgn_ref[...] += dg

    tok = pl.BlockSpec((tm, D), lambda i, j: (i, 0))
    chunk = pl.BlockSpec((None, tm, C), lambda i, j: (j, i, 0))
    row = pl.BlockSpec((1, D), lambda i, j: (0, 0))
    in_specs = [tok, tok, row, chunk, chunk] + _ffn_weight_specs(f, nj, D, C)[:2]
    body, in_specs, args = _with_dep(body, dep, in_specs, [dxo, x, g, dzg, dzu, ga, ga])
    return pl.pallas_call(
        body, name="ffn_bwd_dx", grid=(T // tm, nj),
        in_specs=in_specs, out_specs=[tok, row],
        out_shape=[jax.ShapeDtypeStruct((T, D), F32), jax.ShapeDtypeStruct((1, D), F32)],
        scratch_shapes=[pltpu.VMEM((tm, D), F32)],
        compiler_params=_params("arbitrary", "arbitrary"),
    )(*args)


def _ffn_dw(h, dzg, dzu, s, dy, tk, dep=None):
    T, D = h.shape
    nj, C = s.shape[0], s.shape[2]
    nk = T // tk

    def body(h_ref, dzg_ref, dzu_ref, s_ref, dy_ref, dwin_ref, dwo_ref, ag, au, ao):
        k = pl.program_id(1)

        @pl.when(k == 0)
        def _():
            ag[...] = jnp.zeros_like(ag)
            au[...] = jnp.zeros_like(au)
            ao[...] = jnp.zeros_like(ao)

        hb = h_ref[...]
        ag[...] += _dot_tn(hb, dzg_ref[...])
        au[...] += _dot_tn(hb, dzu_ref[...])
        ao[...] += _dot_tn(s_ref[...], dy_ref[...])

        @pl.when(k == nk - 1)
        def _():
            dwin_ref[0] = ag[...].astype(BF16)
            dwin_ref[1] = au[...].astype(BF16)
            dwo_ref[...] = ao[...].astype(BF16)

    tok = pl.BlockSpec((tk, D), lambda j, k: (k, 0))
    chunk = pl.BlockSpec((None, tk, C), lambda j, k: (j, k, 0))
    body, in_specs, args = _with_dep(body, dep, [tok, chunk, chunk, chunk, tok], [h, dzg, dzu, s, dy])
    dwin, dwo = pl.pallas_call(
        body, name="ffn_dw", grid=(nj, nk),
        in_specs=in_specs,
        out_specs=[pl.BlockSpec((2, None, D, C), lambda j, k: (0, j, 0, 0)),
                   pl.BlockSpec((None, C, D), lambda j, k: (j, 0, 0))],
        out_shape=[jax.ShapeDtypeStruct((2, nj, D, C), BF16), jax.ShapeDtypeStruct((nj, C, D), BF16)],
        scratch_shapes=[pltpu.VMEM((D, C), F32), pltpu.VMEM((D, C), F32), pltpu.VMEM((C, D), F32)],
        compiler_params=_params("parallel", "arbitrary"),
    )(*args)
    return dwin.reshape(2 * nj, D, C), dwo


def _matmul_tn(a, b, tn, tk):
    T, Ka = a.shape
    N = b.shape[1]
    nk = T // tk

    def body(a_ref, b_ref, o_ref, acc):
        k = pl.program_id(1)

        @pl.when(k == 0)
        def _():
            acc[...] = jnp.zeros_like(acc)

        acc[...] += _dot_tn(a_ref[...], b_ref[...])

        @pl.when(k == nk - 1)
        def _():
            o_ref[...] = acc[...].astype(BF16)

    return pl.pallas_call(
        body, name="matmul_tn", grid=(N // tn, nk),
        in_specs=[pl.BlockSpec((tk, Ka), lambda n, k: (k, 0)), pl.BlockSpec((tk, tn), lambda n, k: (k, n))],
        out_specs=pl.BlockSpec((Ka, tn), lambda n, k: (0, n)),
        out_shape=jax.ShapeDtypeStruct((Ka, N), BF16),
        scratch_shapes=[pltpu.VMEM((Ka, tn), F32)],
        compiler_params=_params("parallel", "arbitrary"),
    )(a, b)


def _qkv_fwd(x, g, w, tm):
    T, D = x.shape
    N = w.shape[1]

    def body(x_ref, g_ref, w_ref, o_ref, h_ref):
        xv = x_ref[...]
        hb = (xv * _rstd(xv) * g_ref[...]).astype(BF16)
        h_ref[...] = hb
        o_ref[...] = _dot(hb, w_ref[...])

    return pl.pallas_call(
        body, name="qkv_fwd", grid=(T // tm,),
        in_specs=[pl.BlockSpec((tm, D), lambda i: (i, 0)), pl.BlockSpec((1, D), lambda i: (0, 0)),
                  pl.BlockSpec((D, N), lambda i: (0, 0))],
        out_specs=[pl.BlockSpec((tm, N), lambda i: (i, 0)), pl.BlockSpec((tm, D), lambda i: (i, 0))],
        out_shape=[jax.ShapeDtypeStruct((T, N), F32), jax.ShapeDtypeStruct((T, D), BF16)],
        compiler_params=_params("parallel"),
    )(x, g, w)


DILS = tuple(d for _, d in DILATED)


def _spread_specs(tm, T, dtype):
    specs = [pl.BlockSpec((4, d, tm // d, PAIR), lambda i: (0, 0, i, 0)) for d in DILS]
    shapes = [jax.ShapeDtypeStruct((4, d, T // d, PAIR), dtype) for d in DILS]
    return specs, shapes


def _spread(tile, y, outs, c, dtype):
    tm = y.shape[0]
    tile[...] = y
    for out, d in zip(outs, DILS):
        for r in range(d):
            out[c, r] = tile[pl.ds(r, tm // d, stride=d), :].astype(dtype)


def _collect(tile, ins, c):
    tm = tile.shape[0]
    first = True
    for ref, d in zip(ins, DILS):
        for r in range(d):
            rows = pl.ds(r, tm // d, stride=d) if d > 1 else pl.ds(0, tm)
            part = ref[c, r].astype(F32)
            tile[rows, :] = part if first else tile[rows, :] + part
        first = False
    return tile[...]


def _attn_prep(qkv, gains2, tm):
    T = qkv.shape[0]
    scale = HEAD_DIM ** -0.5
    n = len(DILS)

    def body(qkv_ref, g_ref, qb_ref, kb_ref, vb_ref, *rest):
        outs, tile = rest[:-1], rest[-1]
        lo = _lo_mask((tm, PAIR))

        def spread(kind, c, y):
            _spread(tile, y, outs[kind * n:(kind + 1) * n], c, BF16)

        def normed(c, gi, mult):
            xv = qkv_ref[:, c * PAIR:(c + 1) * PAIR]
            r = lax.rsqrt(_half_sum(xv * xv, lo) * (1.0 / HEAD_DIM) + EPS)
            y = xv * r * g_ref[gi:gi + 1, :]
            return y * mult if mult != 1.0 else y

        def both_halves(v):
            sw = pltpu.roll(v, HEAD_DIM, 1)
            return jnp.where(lo, v, sw), jnp.where(lo, sw, v)

        for c in range(4):
            spread(0, c, normed(c, 0, scale))
            spread(1, c, normed(4 + c, 1, 1.0))
            spread(2, c, qkv_ref[:, (8 + c) * PAIR:(9 + c) * PAIR])
            qb_ref[c] = normed(12 + c, 2, scale).astype(BF16)
        k0, k1 = both_halves(normed(16, 3, 1.0))
        kb_ref[0] = k0.astype(BF16)
        kb_ref[1] = k1.astype(BF16)
        v0, v1 = both_halves(qkv_ref[:, 17 * PAIR:18 * PAIR])
        vb_ref[0] = v0.astype(BF16)
        vb_ref[1] = v1.astype(BF16)

    four = pl.BlockSpec((4, tm, PAIR), lambda i: (0, i, 0))
    two = pl.BlockSpec((2, tm, PAIR), lambda i: (0, i, 0))
    s4 = jax.ShapeDtypeStruct((4, T, PAIR), BF16)
    s2 = jax.ShapeDtypeStruct((2, T, PAIR), BF16)
    specs, shapes = _spread_specs(tm, T, BF16)
    res = pl.pallas_call(
        body, name="attn_prep", grid=(T // tm,),
        in_specs=[pl.BlockSpec((tm, qkv.shape[1]), lambda i: (i, 0)), pl.BlockSpec((4, PAIR), lambda i: (0, 0))],
        out_specs=[four, two, two] + specs * 3,
        out_shape=[s4, s2, s2] + shapes * 3,
        scratch_shapes=[pltpu.VMEM((tm, PAIR), F32)],
        compiler_params=_params("parallel"),
    )(qkv, gains2)
    qb, kb, vb = res[:3]
    per_d = [tuple(res[3 + kind * n + di].reshape(4 * d, T // d, PAIR) for kind in range(3))
             for di, d in enumerate(DILS)]
    return qb, kb, vb, per_d


def _loop_blocks(nb, body, init, per_iter):
    u = math.gcd(nb, per_iter)

    def outer(i, carry):
        for k in range(u):
            carry = body(i * u + k, carry)
        return carry

    return lax.fori_loop(0, nb // u, outer, init)


def _key_window(b, nb, L, R, W):
    start = pl.multiple_of(jnp.clip(b * BQ - R, 0, L - W), HEAD_DIM)
    return start, jnp.where(b == 0, 1, jnp.where(b == nb - 1, 2, 0))


def _stack_heads(v, lo):
    z = jnp.zeros_like(v)
    return jnp.concatenate([jnp.where(lo, v, z), jnp.where(lo, z, v)], axis=0)


def _unstack_heads(v2, lo):
    return jnp.where(lo, v2[:BQ], v2[BQ:])


def _row_vector(v, lo):
    r = lax.broadcasted_iota(jnp.int32, (BQ, PAIR), 0)
    ln = lax.broadcasted_iota(jnp.int32, (BQ, PAIR), 1)
    diag = (ln % HEAD_DIM) == (r % HEAD_DIM)
    top = jnp.sum(jnp.where(diag & (r < HEAD_DIM), v, 0.0), axis=0, keepdims=True)
    bot = jnp.sum(jnp.where(diag & (r >= HEAD_DIM), v, 0.0), axis=0, keepdims=True)
    top8, bot8 = jnp.broadcast_to(top, (8, PAIR)), jnp.broadcast_to(bot, (8, PAIR))
    lo8 = _lo_mask((8, PAIR))
    head0 = jnp.where(lo8, top8, pltpu.roll(bot8, HEAD_DIM, 1))
    head1 = jnp.where(lo8, pltpu.roll(top8, HEAD_DIM, 1), bot8)
    return jnp.concatenate([head0, head1], axis=1)[:1]


def _units_per_step(nb, pairs_per_kv):
    return max(1, 16 // nb) if pairs_per_kv == 1 else 1


def _attn_fwd(q, kp, vp, bias4, sink, R, pairs_per_kv, pairs_per_bias):
    N, L, _ = q.shape
    W = BQ + 2 * R
    nb = L // BQ
    assert L >= W and nb >= 2
    G = _units_per_step(nb, pairs_per_kv)

    def body(sink_ref, q_ref, k_ref, v_ref, bias_ref, o_ref, lse_ref):
        n = pl.program_id(0)
        lo_q = _lo_mask((BQ, PAIR))
        first = lax.broadcasted_iota(jnp.int32, (2 * BQ, 1), 0) < BQ

        def blk(f, carry):
            g, b = f // nb, f % nb
            u = n * G + g
            sk = jnp.where(first, sink_ref[2 * u], sink_ref[2 * u + 1])
            q0 = pl.multiple_of(b * BQ, BQ)
            q2 = _stack_heads(q_ref[g, pl.ds(q0, BQ), :], lo_q)
            k0, variant = _key_window(b, nb, L, R, W)
            kw = k_ref[g, pl.ds(k0, W), :]
            vw = v_ref[g, pl.ds(k0, W), :]
            s = _dot_nt(q2, kw) + bias_ref[variant]
            m = jnp.maximum(jnp.max(s, axis=1, keepdims=True), sk)
            p = jnp.exp(s - m)
            l = jnp.sum(p, axis=1, keepdims=True) + jnp.exp(sk - m)
            o2 = _dot(p.astype(BF16), vw) / l
            o_ref[g, pl.ds(q0, BQ), :] = _unstack_heads(o2, lo_q)
            lse_ref[g, pl.ds(q0, BQ), :] = _unstack_heads(jnp.broadcast_to(m + jnp.log(l), (2 * BQ, PAIR)), lo_q)
            return carry

        _loop_blocks(G * nb, blk, 0, 8)

    qspec = pl.BlockSpec((G, L, PAIR), lambda n: (n, 0, 0))
    kspec = pl.BlockSpec((G, L, PAIR), lambda n: (n // pairs_per_kv, 0, 0))
    return pl.pallas_call(
        body, name="attn_fwd", grid=(N // G,),
        in_specs=[pl.BlockSpec(memory_space=pltpu.SMEM), qspec, kspec, kspec,
                  pl.BlockSpec((None, 3, 2 * BQ, W), lambda n: (n * G // pairs_per_bias, 0, 0, 0))],
        out_specs=[qspec, qspec],
        out_shape=[jax.ShapeDtypeStruct((N, L, PAIR), F32), jax.ShapeDtypeStruct((N, L, PAIR), F32)],
        compiler_params=_params("parallel"),
    )(sink, q, kp, vp, bias4)


def _attn_bwd(q, kp, vp, bias4t, sink, o, lse, do, R, pairs_per_kv, pairs_per_bias):
    N, L, _ = q.shape
    Nk = kp.shape[0]
    Pb = bias4t.shape[0]
    W = BQ + 2 * R
    nb = L // BQ
    assert L >= W and nb >= 2
    G = _units_per_step(nb, pairs_per_kv)

    def body(sink_ref, q_ref, k_ref, v_ref, bias_ref, o_ref, lse_ref, do_ref,
             dq_ref, dk_ref, dv_ref, dbias_ref, dsink_ref, dk_acc, dv_acc):
        n = pl.program_id(0)
        lo_q = _lo_mask((BQ, PAIR))
        first = lax.broadcasted_iota(jnp.int32, (1, 2 * BQ), 1) < BQ
        dsink_ref[...] = jnp.zeros_like(dsink_ref)

        @pl.when(n % pairs_per_kv == 0)
        def _():
            dk_acc[...] = jnp.zeros_like(dk_acc)
            dv_acc[...] = jnp.zeros_like(dv_acc)

        @pl.when((n * G) % pairs_per_bias == 0)
        def _():
            dbias_ref[...] = jnp.zeros_like(dbias_ref)

        def blk(f, carry):
            g, b = f // nb, f % nb
            u = n * G + g
            sk = jnp.where(first, sink_ref[2 * u], sink_ref[2 * u + 1])
            q0 = pl.multiple_of(b * BQ, BQ)
            q2 = _stack_heads(q_ref[g, pl.ds(q0, BQ), :], lo_q)
            k0, variant = _key_window(b, nb, L, R, W)
            kw = k_ref[g, pl.ds(k0, W), :]
            vw = v_ref[g, pl.ds(k0, W), :]
            dov = do_ref[g, pl.ds(q0, BQ), :]
            lse = _row_vector(lse_ref[g, pl.ds(q0, BQ), :], lo_q)
            delta = _row_vector(_half_sum(dov.astype(F32) * o_ref[g, pl.ds(q0, BQ), :], lo_q), lo_q)
            do2 = _stack_heads(dov.astype(BF16), lo_q)
            st = _dot_nt(kw, q2) + bias_ref[variant]
            pt = jnp.exp(st - lse)
            dst = pt * (_dot_nt(vw, do2) - delta)
            dstb = dst.astype(BF16)
            dbias_ref[variant] += dst
            dk_acc[g, pl.ds(k0, W), :] += _dot(dstb, q2)
            dv_acc[g, pl.ds(k0, W), :] += _dot(pt.astype(BF16), do2)
            dq_ref[g, pl.ds(q0, BQ), :] = _unstack_heads(_dot_tn(dstb, kw), lo_q).astype(BF16)
            dsink_ref[g, pl.ds(0, 1), :] -= jnp.exp(sk - lse) * delta
            return carry

        _loop_blocks(G * nb, blk, 0, 8)
        dk_ref[...] = dk_acc[...].astype(BF16)
        dv_ref[...] = dv_acc[...].astype(BF16)

    qspec = pl.BlockSpec((G, L, PAIR), lambda n: (n, 0, 0))
    kspec = pl.BlockSpec((G, L, PAIR), lambda n: (n // pairs_per_kv, 0, 0))
    return pl.pallas_call(
        body, name="attn_bwd", grid=(N // G,),
        in_specs=[pl.BlockSpec(memory_space=pltpu.SMEM), qspec, kspec, kspec,
                  pl.BlockSpec((None, 3, W, 2 * BQ), lambda n: (n * G // pairs_per_bias, 0, 0, 0)),
                  qspec, qspec, qspec],
        out_specs=[qspec, kspec, kspec,
                   pl.BlockSpec((None, 3, W, 2 * BQ), lambda n: (n * G // pairs_per_bias, 0, 0, 0)),
                   pl.BlockSpec((G, 8, 2 * BQ), lambda n: (n, 0, 0))],
        out_shape=[jax.ShapeDtypeStruct((N, L, PAIR), BF16),
                   jax.ShapeDtypeStruct((Nk, L, PAIR), BF16),
                   jax.ShapeDtypeStruct((Nk, L, PAIR), BF16),
                   jax.ShapeDtypeStruct((Pb, 3, W, 2 * BQ), F32),
                   jax.ShapeDtypeStruct((N, 8, 2 * BQ), F32)],
        scratch_shapes=[pltpu.VMEM((G, L, PAIR), F32), pltpu.VMEM((G, L, PAIR), F32)],
        compiler_params=_params("arbitrary"),
    )(sink, q, kp, vp, bias4t, o, lse, do)


def _attn_merge(branch_outs, ob, tm):
    T = ob.shape[1]
    n = len(DILS)

    def body(*refs):
        o_in, l_in, ob_ref = refs[:n], refs[n:2 * n], refs[2 * n]
        o_out, l_out, cat_ref = refs[2 * n + 1:3 * n + 1], refs[3 * n + 1:4 * n + 1], refs[4 * n + 1]
        tiles = refs[4 * n + 2:]
        for c in range(4):
            o_nat, l_nat = [], []
            for di, d in enumerate(DILS):
                for kind, (src, dst) in enumerate(((o_in[di], o_nat), (l_in[di], l_nat))):
                    tile = tiles[2 * di + kind]
                    if d == 1:
                        dst.append(src[c, 0])
                    else:
                        for r in range(d):
                            tile[pl.ds(r, tm // d, stride=d), :] = src[c, r]
                        dst.append(tile[...])
            m = functools.reduce(jnp.maximum, l_nat)
            ws = [jnp.exp(l - m) for l in l_nat]
            z = sum(ws)
            o = sum(w * t for w, t in zip(ws, o_nat)) / z
            cat_ref[:, c * PAIR:(c + 1) * PAIR] = o.astype(BF16)
            cat_ref[:, (4 + c) * PAIR:(5 + c) * PAIR] = ob_ref[c].astype(BF16)
            _spread(tiles[0], o, o_out, c, F32)
            _spread(tiles[1], m + jnp.log(z), l_out, c, F32)

    specs, shapes = _spread_specs(tm, T, F32)
    four = pl.BlockSpec((4, tm, PAIR), lambda i: (0, i, 0))
    o_views = [o.reshape(4, d, T // d, PAIR) for (o, _), d in zip(branch_outs, DILS)]
    l_views = [l.reshape(4, d, T // d, PAIR) for (_, l), d in zip(branch_outs, DILS)]
    res = pl.pallas_call(
        body, name="attn_merge", grid=(T // tm,),
        in_specs=specs + specs + [four],
        out_specs=specs + specs + [pl.BlockSpec((tm, 8 * PAIR), lambda i: (i, 0))],
        out_shape=shapes + shapes + [jax.ShapeDtypeStruct((T, 8 * PAIR), BF16)],
        scratch_shapes=[pltpu.VMEM((tm, PAIR), F32)] * (2 * n),
        compiler_params=_params("parallel"),
    )(*o_views, *l_views, ob)
    merged = [(res[di].reshape(4 * d, T // d, PAIR), res[n + di].reshape(4 * d, T // d, PAIR))
              for di, d in enumerate(DILS)]
    return merged, res[2 * n]


def _weight_arg(w, blk):
    if blk is None:
        return pl.BlockSpec(w.shape, lambda i: (0, 0)), (lambda ref: ref[...])
    D = w.shape[2]
    return (pl.BlockSpec((N_DEV, 128, D), lambda i: (0, blk, 0)),
            lambda ref: ref[...].reshape(N_DEV * 128, D))


def _oproj_fwd(x, o_cat, w, blk, tm):
    T, D = x.shape
    wspec, wload = _weight_arg(w, blk)

    def body(x_ref, o_ref, w_ref, out_ref):
        out_ref[...] = x_ref[...] + _dot(o_ref[...], wload(w_ref))

    tok = pl.BlockSpec((tm, D), lambda i: (i, 0))
    return pl.pallas_call(
        body, name="oproj_fwd", grid=(T // tm,),
        in_specs=[tok, pl.BlockSpec((tm, o_cat.shape[1]), lambda i: (i, 0)), wspec],
        out_specs=tok, out_shape=jax.ShapeDtypeStruct((T, D), F32),
        compiler_params=_params("parallel"),
    )(x, o_cat, w)


def _oproj_bwd(dx, w, blk, tm, dep=None):
    T, D = dx.shape
    wspec, wload = _weight_arg(w, blk)

    def body(dx_ref, w_ref, dxb_ref, dob_ref, *rest):
        doa_refs, tile = rest[:-1], rest[-1]
        db = dx_ref[...].astype(BF16)
        dxb_ref[...] = db
        do = _dot_nt(db, wload(w_ref))
        for c in range(4):
            _spread(tile, do[:, c * PAIR:(c + 1) * PAIR], doa_refs, c, BF16)
            dob_ref[c] = do[:, (4 + c) * PAIR:(5 + c) * PAIR].astype(BF16)

    tok = pl.BlockSpec((tm, D), lambda i: (i, 0))
    specs, shapes = _spread_specs(tm, T, BF16)
    body, in_specs, args = _with_dep(body, dep, [tok, wspec], [dx, w])
    res = pl.pallas_call(
        body, name="oproj_bwd", grid=(T // tm,),
        in_specs=in_specs,
        out_specs=[tok, pl.BlockSpec((4, tm, PAIR), lambda i: (0, i, 0))] + specs,
        out_shape=[jax.ShapeDtypeStruct((T, D), BF16), jax.ShapeDtypeStruct((4, T, PAIR), BF16)] + shapes,
        scratch_shapes=[pltpu.VMEM((tm, PAIR), F32)],
        compiler_params=_params("parallel"),
    )(*args)
    return res[0], res[1], [t.reshape(4 * d, T // d, PAIR) for t, d in zip(res[2:], DILS)]


def _attn_post(qkv, gains2, dqa, dka, dva, dqb, dkb, dvb, tm):
    T, NQ = qkv.shape
    scale = HEAD_DIM ** -0.5

    n = len(DILS)

    def body(qkv_ref, g_ref, *rest):
        dq_refs, dk_refs, dv_refs = rest[:n], rest[n:2 * n], rest[2 * n:3 * n]
        qb_ref, kb_ref, vb_ref, out_ref, dg_ref, tile = rest[3 * n:]
        lo = _lo_mask((tm, PAIR))

        @pl.when(pl.program_id(0) == 0)
        def _():
            dg_ref[...] = jnp.zeros_like(dg_ref)

        def norm_bwd(c, gi, dy):
            xv = qkv_ref[:, c * PAIR:(c + 1) * PAIR]
            r = lax.rsqrt(_half_sum(xv * xv, lo) * (1.0 / HEAD_DIM) + EPS)
            xn = xv * r
            dg_ref[gi:gi + 1, :] += jnp.sum(dy * xn, axis=0, keepdims=True)
            dxn = dy * g_ref[gi:gi + 1, :]
            dx = r * (dxn - xn * (_half_sum(dxn * xn, lo) * (1.0 / HEAD_DIM)))
            out_ref[:, c * PAIR:(c + 1) * PAIR] = dx.astype(BF16)

        def fold(v):
            return v + pltpu.roll(v, HEAD_DIM, 1)

        for c in range(4):
            norm_bwd(c, 0, _collect(tile, dq_refs, c) * scale)
            norm_bwd(4 + c, 1, _collect(tile, dk_refs, c))
            out_ref[:, (8 + c) * PAIR:(9 + c) * PAIR] = _collect(tile, dv_refs, c).astype(BF16)
            norm_bwd(12 + c, 2, qb_ref[c].astype(F32) * scale)
        kb, vb = kb_ref[...].astype(F32), vb_ref[...].astype(F32)
        norm_bwd(16, 3, jnp.where(lo, fold(kb[0]), fold(kb[1])))
        out_ref[:, 17 * PAIR:18 * PAIR] = jnp.where(lo, fold(vb[0]), fold(vb[1])).astype(BF16)

    four = pl.BlockSpec((4, tm, PAIR), lambda i: (0, i, 0))
    two = pl.BlockSpec((2, tm, PAIR), lambda i: (0, i, 0))
    specs, _ = _spread_specs(tm, T, BF16)
    views = [t.reshape(4, d, T // d, PAIR) for group in (dqa, dka, dva) for t, d in zip(group, DILS)]
    return pl.pallas_call(
        body, name="attn_post", grid=(T // tm,),
        in_specs=[pl.BlockSpec((tm, NQ), lambda i: (i, 0)), pl.BlockSpec((4, PAIR), lambda i: (0, 0))]
        + specs * 3 + [four, two, two],
        out_specs=[pl.BlockSpec((tm, NQ), lambda i: (i, 0)), pl.BlockSpec((4, PAIR), lambda i: (0, 0))],
        out_shape=[jax.ShapeDtypeStruct((T, NQ), BF16), jax.ShapeDtypeStruct((4, PAIR), F32)],
        scratch_shapes=[pltpu.VMEM((tm, PAIR), F32)],
        compiler_params=_params("arbitrary"),
    )(qkv, gains2, *views, dqb, dkb, dvb)


def _dense_norm_bwd(dres, dz, w, blk, x, g, tm):
    T, D = x.shape
    N = dz.shape[1]
    wspec, wload = _weight_arg(w, blk)

    def body(dres_ref, dz_ref, w_ref, x_ref, g_ref, dx_ref, dgn_ref):
        i = pl.program_id(0)
        dx, dg = _norm_bwd(_dot_nt(dz_ref[...], wload(w_ref)), x_ref[...], g_ref[...])
        dx_ref[...] = dres_ref[...] + dx

        @pl.when(i == 0)
        def _():
            dgn_ref[...] = dg

        @pl.when(i > 0)
        def _():
            dgn_ref[...] += dg

    tok = pl.BlockSpec((tm, D), lambda i: (i, 0))
    row = pl.BlockSpec((1, D), lambda i: (0, 0))
    return pl.pallas_call(
        body, name="dense_norm_bwd", grid=(T // tm,),
        in_specs=[tok, pl.BlockSpec((tm, N), lambda i: (i, 0)), wspec, tok, row],
        out_specs=[tok, row],
        out_shape=[jax.ShapeDtypeStruct((T, D), F32), jax.ShapeDtypeStruct((1, D), F32)],
        compiler_params=_params("arbitrary"),
    )(dres, dz, w, x, g)


def _bias_reduce(onehot, dbm):
    Hb, K = dbm.shape

    def body(oh_ref, d_ref, out_ref):
        oh = oh_ref[...]
        d = d_ref[...]
        hi = d.astype(BF16)
        r1 = d - hi.astype(F32)
        mid = r1.astype(BF16)
        low = (r1 - mid.astype(F32)).astype(BF16)
        out_ref[...] = _dot_nt(hi, oh) + _dot_nt(mid, oh) + _dot_nt(low, oh)

    vm = pl.BlockSpec(memory_space=pltpu.VMEM)
    return pl.pallas_call(
        body, name="bias_reduce", in_specs=[vm, vm], out_specs=vm,
        out_shape=jax.ShapeDtypeStruct((Hb, N_BUCKETS), F32),
        compiler_params=pltpu.CompilerParams(vmem_limit_bytes=VMEM_LIMIT),
    )(onehot, dbm)


def _ple_fwd(x, g, wg, blk, p, wp, target, tm):
    T, D = x.shape
    P = p.shape[1]
    with_loss = target is not None
    wspec, wload = _weight_arg(wg, blk)

    def body(*refs):
        if with_loss:
            x_ref, g_ref, wg_ref, p_ref, wp_ref, t_ref, y_ref, hn_ref, gate_ref, pp_ref, pb_ref, loss_ref = refs
        else:
            x_ref, g_ref, wg_ref, p_ref, wp_ref, y_ref, hn_ref, gate_ref, pp_ref, pb_ref = refs
        i = pl.program_id(0)
        xv = x_ref[...]
        hb = (xv * _rstd(xv) * g_ref[...]).astype(BF16)
        hn_ref[...] = hb
        gate = _sigmoid(_dot(hb, wload(wg_ref)))
        pb = p_ref[...].astype(BF16)
        pb_ref[...] = pb
        pp = _dot(pb, wp_ref[...])
        gate_ref[...] = gate
        pp_ref[...] = pp
        y = xv + gate * pp
        if with_loss:
            err = y - t_ref[...]
            y_ref[...] = err * (1.0 / D)
            part = jnp.broadcast_to(0.5 * jnp.sum(jnp.sum(err * err, axis=1, keepdims=True) * (1.0 / D),
                                                  axis=0, keepdims=True), (1, 128))

            @pl.when(i == 0)
            def _():
                loss_ref[...] = part

            @pl.when(i > 0)
            def _():
                loss_ref[...] += part
        else:
            y_ref[...] = y

    tok = pl.BlockSpec((tm, D), lambda i: (i, 0))
    ptok = pl.BlockSpec((tm, P), lambda i: (i, 0))
    in_specs = [tok, pl.BlockSpec((1, D), lambda i: (0, 0)), wspec, ptok,
                pl.BlockSpec((P, D), lambda i: (0, 0))]
    out_specs = [tok, tok, tok, tok, ptok]
    out_shape = [jax.ShapeDtypeStruct((T, D), F32), jax.ShapeDtypeStruct((T, D), BF16),
                 jax.ShapeDtypeStruct((T, D), F32), jax.ShapeDtypeStruct((T, D), F32),
                 jax.ShapeDtypeStruct((T, P), BF16)]
    args = [x, g, wg, p, wp]
    if with_loss:
        in_specs.append(tok)
        out_specs.append(pl.BlockSpec((1, 128), lambda i: (0, 0)))
        out_shape.append(jax.ShapeDtypeStruct((1, 128), F32))
        args.append(target)
    return pl.pallas_call(
        body, name="ple_fwd_loss" if with_loss else "ple_fwd", grid=(T // tm,),
        in_specs=in_specs, out_specs=out_specs, out_shape=out_shape,
        compiler_params=_params("arbitrary" if with_loss else "parallel"),
    )(*args)


def _ple_bwd(dy, gate, pp, tm, dep=None):
    T, D = dy.shape

    def body(dy_ref, gate_ref, pp_ref, dgl_ref, dpp_ref):
        d = dy_ref[...]
        gt = gate_ref[...]
        dgl_ref[...] = (d * pp_ref[...] * gt * (1.0 - gt)).astype(BF16)
        dpp_ref[...] = (d * gt).astype(BF16)

    tok = pl.BlockSpec((tm, D), lambda i: (i, 0))
    body, in_specs, args = _with_dep(body, dep, [tok, tok, tok], [dy, gate, pp])
    return pl.pallas_call(
        body, name="ple_bwd", grid=(T // tm,), in_specs=in_specs, out_specs=[tok, tok],
        out_shape=[jax.ShapeDtypeStruct((T, D), BF16), jax.ShapeDtypeStruct((T, D), BF16)],
        compiler_params=_params("parallel"),
    )(*args)


def _adamw(w, g, m, v):
    shape = w.shape
    C = shape[-1]
    w2, g2, m2, v2 = (a.reshape(-1, C) for a in (w, g, m, v))
    Rn = w2.shape[0]
    tr = Rn
    for cand in (512, 352, 256):
        if Rn % cand == 0:
            tr = cand
            break
    c1 = 1.0 - ADAM_B1 ** ADAM_STEP
    c2 = 1.0 - ADAM_B2 ** ADAM_STEP

    def body(w_ref, g_ref, m_ref, v_ref, d_ref, nm_ref, nv_ref):
        gv = g_ref[...]
        mn = ADAM_B1 * m_ref[...] + (1.0 - ADAM_B1) * gv
        vn = ADAM_B2 * v_ref[...] + (1.0 - ADAM_B2) * (gv * gv)
        d_ref[...] = -ADAM_LR * ((mn / c1) / (jnp.sqrt(vn / c2) + ADAM_EPS) + ADAM_WD * w_ref[...])
        nm_ref[...] = mn
        nv_ref[...] = vn

    spec = pl.BlockSpec((tr, C), lambda i: (i, 0))
    sh = jax.ShapeDtypeStruct((Rn, C), F32)
    d, nm, nv = pl.pallas_call(
        body, name="adamw", grid=(Rn // tr,), in_specs=[spec] * 4, out_specs=[spec] * 3, out_shape=[sh] * 3,
        compiler_params=_params("parallel"),
    )(w2, g2, m2, v2)
    return d.reshape(shape), nm.reshape(shape), nv.reshape(shape)


def _my_place():
    x, y, c = lax.axis_index("x"), lax.axis_index("y"), lax.axis_index("c")
    chips = [(1 - x, y), (x, 1 - y), (1 - x, 1 - y)]
    return x, y, c, chips


def _all_gather(arrs):
    n = len(arrs)

    def body(*refs):
        x_refs, out_refs = refs[:n], refs[n:2 * n]
        send_sems, recv_sems, local_sems = refs[2 * n:]
        x, y, c, chips = _my_place()
        me, sibling = (x, y, c), (x, y, 1 - c)

        def copy(m, k, block, to, src=None):
            rows = out_refs[m].at[4 * block[0] + 2 * block[1] + block[2]]
            return pltpu.make_async_remote_copy(
                src_ref=rows if src is None else src, dst_ref=rows,
                send_sem=send_sems.at[7 * m + k], recv_sem=recv_sems.at[7 * m + k], device_id=to, device_id_type=MESH)

        mine = [pltpu.make_async_copy(x_refs[m], out_refs[m].at[4 * x + 2 * y + c], local_sems.at[m])
                for m in range(n)]
        for cp in mine:
            cp.start()
        first = []
        for m in range(n):
            first.append(copy(m, 0, me, sibling, src=x_refs[m]))
            first += [copy(m, 1 + j, me, (*chip, c), src=x_refs[m]) for j, chip in enumerate(chips)]
        for cp in first:
            cp.start()
        passed = []
        for m in range(n):
            for j, chip in enumerate(chips):
                copy(m, 1 + j, (*chip, c), me).wait_recv()
                cp = copy(m, 4 + j, (*chip, c), sibling)
                cp.start()
                passed.append(cp)
        for m in range(n):
            copy(m, 0, sibling, me).wait_recv()
            for j, chip in enumerate(chips):
                copy(m, 4 + j, (*chip, 1 - c), me).wait_recv()
        for cp in first + passed:
            cp.wait_send()
        for cp in mine:
            cp.wait()

    hbm = pl.BlockSpec(memory_space=pl.ANY)
    return pl.pallas_call(
        body, name="all_gather", in_specs=[hbm] * n, out_specs=[hbm] * n,
        out_shape=[jax.ShapeDtypeStruct((N_DEV,) + a.shape, a.dtype) for a in arrs],
        scratch_shapes=[pltpu.SemaphoreType.DMA((7 * n,)), pltpu.SemaphoreType.DMA((7 * n,)),
                        pltpu.SemaphoreType.DMA((n,))],
    )(*arrs)


def _peer(x, y, c, k):
    return (x ^ ((k >> 2) & 1), y ^ ((k >> 1) & 1), c ^ (k & 1))


HBM_SPEC = pl.BlockSpec(memory_space=pltpu.HBM)
SEM_SPEC = pl.BlockSpec(memory_space=pltpu.SEMAPHORE)


def _exchange_refs(srcs, lands, m, k, x, y, c, scatter):
    peer = _peer(x, y, c, k)
    if scatter:
        return srcs[m].at[4 * peer[0] + 2 * peer[1] + peer[2]], lands[m].at[k - 1], peer
    return srcs[m], lands[m].at[4 * x + 2 * y + c], peer


def _exchange_start(arrs, land_shapes, scatter, name):
    n = len(arrs)

    def body(*refs):
        srcs, lands = refs[:n], refs[n:2 * n]
        send_sems, recv_sems = refs[2 * n], refs[2 * n + 1]
        token = refs[-1]
        x, y, c, _ = _my_place()
        for m in range(n):
            for k in range(1, N_DEV):
                src, dst, peer = _exchange_refs(srcs, lands, m, k, x, y, c, scatter)
                pltpu.make_async_remote_copy(
                    src_ref=src, dst_ref=dst, send_sem=send_sems.at[7 * m + k - 1],
                    recv_sem=recv_sems.at[7 * m + k - 1], device_id=peer, device_id_type=MESH).start()
        token[...] = jnp.zeros_like(token)

    zones = [lax.empty(s_, a.dtype) for s_, a in zip(land_shapes, arrs)]
    outs = pl.pallas_call(
        body, name=name,
        out_shape=(pltpu.SemaphoreType.DMA((7 * n,)), pltpu.SemaphoreType.DMA((7 * n,)),
                   *[pltpu.HBM(a.shape, a.dtype) for a in arrs], *[pltpu.HBM(z.shape, z.dtype) for z in zones],
                   jax.ShapeDtypeStruct((8, 128), F32)),
        in_specs=[HBM_SPEC] * (2 * n),
        out_specs=(SEM_SPEC, SEM_SPEC, *[HBM_SPEC] * (2 * n), pl.BlockSpec(memory_space=pltpu.VMEM)),
        input_output_aliases={m: 2 + m for m in range(2 * n)},
        compiler_params=pltpu.CompilerParams(has_side_effects=pltpu.SideEffectType.DATAFLOW_SIDE_EFFECTING),
    )(*[pltpu.with_memory_space_constraint(a, pltpu.HBM) for a in arrs],
      *[pltpu.with_memory_space_constraint(z, pltpu.HBM) for z in zones])
    return outs[0], outs[1], list(outs[2:2 + n]), list(outs[2 + n:2 + 2 * n]), outs[-1]


def _exchange_wait(send_sems, recv_sems, arrs, zones, after, scatter, name):
    n = len(arrs)
    afters = list(after) if isinstance(after, (list, tuple)) else [after]

    def body(*refs):
        srcs, lands = refs[:n], refs[n:2 * n]
        send_sems, recv_sems = refs[2 * n], refs[2 * n + 1]
        x, y, c, _ = _my_place()
        for m in range(n):
            for k in range(1, N_DEV):
                src, dst, peer = _exchange_refs(srcs, lands, m, k, x, y, c, scatter)
                cp = pltpu.make_async_remote_copy(
                    src_ref=src, dst_ref=dst, send_sem=send_sems.at[7 * m + k - 1],
                    recv_sem=recv_sems.at[7 * m + k - 1], device_id=peer, device_id_type=MESH)
                cp.wait_send()
                cp.wait_recv()

    outs = pl.pallas_call(
        body, name=name,
        out_shape=tuple(pltpu.HBM(a.shape, a.dtype) for a in list(arrs) + list(zones)),
        in_specs=[HBM_SPEC] * (2 * n) + [SEM_SPEC, SEM_SPEC] + [pl.BlockSpec(memory_space=pl.ANY)] * len(afters),
        out_specs=tuple([HBM_SPEC] * (2 * n)),
        input_output_aliases={m: m for m in range(2 * n)},
        compiler_params=pltpu.CompilerParams(has_side_effects=pltpu.SideEffectType.DATAFLOW_SIDE_EFFECTING),
    )(*arrs, *zones, send_sems, recv_sems, *afters)
    return list(outs[n:])


def _sum_parts(own, parts, tr, dep=None):
    R, W = own.shape

    def body(own_ref, parts_ref, out_ref):
        acc = own_ref[...].astype(F32)
        for k in range(N_DEV - 1):
            acc = acc + parts_ref[k].astype(F32)
        out_ref[...] = acc

    in_specs = [pl.BlockSpec((tr, W), lambda i: (i, 0)), pl.BlockSpec((N_DEV - 1, tr, W), lambda i: (0, i, 0))]
    body, in_specs, args = _with_dep(body, dep, in_specs, [own, parts])
    return pl.pallas_call(
        body, name="sum_parts", grid=(R // tr,),
        in_specs=in_specs,
        out_specs=pl.BlockSpec((tr, W), lambda i: (i, 0)),
        out_shape=jax.ShapeDtypeStruct((R, W), F32),
        compiler_params=_params("parallel"),
    )(*args)


def _all_reduce_small(v, dep=None):
    Rn, Wd = v.shape

    def body(v_ref, out_ref, gat_ref, send_sems, recv_sems):
        x, y, c, _ = _my_place()
        me = 4 * x + 2 * y + c
        gat_ref[me] = v_ref[...]
        copies = []
        for k in range(1, N_DEV):
            fx, fy, fc = (k >> 2) & 1, (k >> 1) & 1, k & 1
            peer = (x ^ fx, y ^ fy, c ^ fc)
            cp = pltpu.make_async_remote_copy(
                src_ref=v_ref, dst_ref=gat_ref.at[me], send_sem=send_sems.at[k - 1], recv_sem=recv_sems.at[k - 1],
                device_id=peer, device_id_type=MESH)
            cp.start()
            copies.append(cp)
        for cp in copies:
            cp.wait_recv()
        for cp in copies:
            cp.wait_send()
        acc = gat_ref[0]
        for k in range(1, N_DEV):
            acc = acc + gat_ref[k]
        out_ref[...] = acc

    vm = pl.BlockSpec(memory_space=pltpu.VMEM)
    body, in_specs, args = _with_dep(body, dep, [vm], [v])
    return pl.pallas_call(
        body, name="all_reduce_small", in_specs=in_specs, out_specs=vm,
        out_shape=jax.ShapeDtypeStruct((Rn, Wd), F32),
        scratch_shapes=[pltpu.VMEM((N_DEV, Rn, Wd), F32), pltpu.SemaphoreType.DMA((7,)),
                        pltpu.SemaphoreType.DMA((7,))],
    )(*args)


def _t5_bucket(rel):
    half = N_BUCKETS // 2
    max_exact = half // 2
    ret = jnp.where(rel > 0, half, 0)
    n = jnp.abs(rel)
    nf = jnp.maximum(n, 1).astype(F32)
    large = max_exact + (jnp.log(nf / max_exact) / math.log(MAX_DISTANCE / max_exact)
                         * (half - max_exact)).astype(jnp.int32)
    large = jnp.minimum(large, half - 1)
    return ret + jnp.where(n < max_exact, n, large)


def _band(R, d):
    W = BQ + 2 * R
    rel = jnp.arange(W)[None, :] - R - jnp.arange(BQ)[:, None]
    return _t5_bucket(rel * d), jnp.abs(rel) <= R


def _onehot(R, d):
    bkt, in_band = _band(R, d)
    return ((bkt.reshape(1, -1) == jnp.arange(N_BUCKETS)[:, None]) & in_band.reshape(1, -1)).astype(BF16)


def _bias_expand(table_t, onehot):
    H = table_t.shape[0]
    K = onehot.shape[1]

    def body(t_ref, oh_ref, out_ref):
        oh = oh_ref[...]
        t = t_ref[...]
        hi = t.astype(BF16)
        r1 = t - hi.astype(F32)
        mid = r1.astype(BF16)
        low = (r1 - mid.astype(F32)).astype(BF16)
        marked = _dot(jnp.ones(t.shape, BF16), oh) > 0.5
        out_ref[...] = jnp.where(marked, _dot(hi, oh) + _dot(mid, oh) + _dot(low, oh), NEG)

    vm = pl.BlockSpec(memory_space=pltpu.VMEM)
    return pl.pallas_call(
        body, name="bias_expand", in_specs=[vm, vm], out_specs=vm,
        out_shape=jax.ShapeDtypeStruct((H, K), F32),
        compiler_params=pltpu.CompilerParams(vmem_limit_bytes=VMEM_LIMIT),
    )(table_t, onehot)


def _bias_matrix(table, R, d):
    return _bias_expand(table.T, _onehot(R, d)).reshape(table.shape[1], BQ, BQ + 2 * R)


def _bias_variants(base, R):
    H, _, W = base.shape
    fill = jnp.full((H, BQ, R), NEG, F32)
    first = jnp.concatenate([base[:, :, R:], fill], axis=2)
    last = jnp.concatenate([fill, base[:, :, :W - R]], axis=2)
    v = jnp.stack([base, first, last], axis=1)
    v = v.reshape(H // 2, 2, 3, BQ, W).transpose(0, 2, 1, 3, 4).reshape(H // 2, 3, 2 * BQ, W)
    return v, v.transpose(0, 1, 3, 2)


def _bias_grad(dbt, R, d):
    P, _, W, _ = dbt.shape
    dbt = dbt[:, 0].at[:, R:].add(dbt[:, 1, :W - R]).at[:, :W - R].add(dbt[:, 2, R:])
    dbm = dbt.reshape(P, W, 2, BQ).transpose(0, 2, 3, 1).reshape(2 * P, BQ * W)
    return _bias_reduce(_onehot(R, d), dbm).T


def _tile2(gain):
    return jnp.concatenate([gain, gain])


ROW_W_O, ROW_GATE, B_ROWS = 768, 896, 1024
BLK_W_O, BLK_GATE = ROW_W_O // 128, ROW_GATE // 128


def _pack_layer(wts, i):
    a = jnp.stack([wts["ffn1_w_in"][i], wts["ffn2_w_in"][i]])
    D = a.shape[1]
    b = jnp.concatenate([
        wts["ffn1_w_out"][i], wts["ffn2_w_out"][i],
        jnp.zeros((ROW_W_O - 2 * wts["ffn1_w_out"].shape[1], D), a.dtype), wts["w_o"][i], wts["w_ple_gate"][i]])
    return a, b, wts["w_qkv"][i], wts["w_ple_proj"][i]


def _unpack_layer(sums, like):
    w_in2, b1, proj, w_o, qkv, w_in1, w_out1 = sums
    n_out = like["ffn1_w_out"].shape[1]
    out = {}
    if w_in2 is not None:
        out.update(ffn2_w_in=w_in2, ffn2_w_out=b1[:n_out], w_ple_gate=b1[n_out:], w_ple_proj=proj)
    if w_o is not None:
        out.update(w_o=w_o, w_qkv=qkv)
    if w_in1 is not None:
        out.update(ffn1_w_in=w_in1, ffn1_w_out=w_out1)
    return out


def _col_sharded(g):
    return g.transpose(1, 0, 2).reshape(g.shape[1], -1)


def _to_col_shards(g):
    rows = g.shape[0]
    return g.reshape(rows, N_DEV, -1).transpose(1, 0, 2)


def _layer_weights(ga, gb, gq, gp):
    return dict(ga=ga, gb=gb, w_qkv=_col_sharded(gq), w_proj=_col_sharded(gp))


def _layer_fwd(x, p, w, sm, i, target, tm, biases, dep=None):
    ga, gb = w["ga"], w["gb"]
    saved = {}
    saved["x0"] = x
    x1, saved["h1"], saved["zg1"], saved["zu1"], saved["s1"] = _ffn_fwd(
        x, sm["norm_ffn1"][i][None], ga, gb, 0, 2 * tm, dep)
    saved["x1"] = x1
    qkv, saved["hm"] = _qkv_fwd(x1, sm["norm_mix"][i][None], w["w_qkv"], 2 * tm)
    saved["qkv"] = qkv
    gains2 = jnp.stack([_tile2(sm[k][i]) for k in ("q_norm_a", "k_norm_a", "q_norm_b", "k_norm_b")])
    saved["gains2"] = gains2
    qb, kb, vb, qkv_d = _attn_prep(qkv, gains2, tm)
    no_sink = jnp.full((8,), NEG, F32)
    branches = []
    outs = []
    for (R, d), bias, (qd, kd, vd) in zip(DILATED, biases[:3], qkv_d):
        sink = jnp.tile(no_sink, d)
        outs.append(_attn_fwd(qd, kd, vd, bias[0], sink, R, 1, d))
        branches.append((qd, kd, vd, bias, sink, R, d))
    bias_b = biases[3]
    sink_b = sm["sink_b"][i]
    ob, lb = _attn_fwd(qb, kb, vb, bias_b[0], sink_b, SWA_RADIUS, 2, 1)
    merged, o_cat = _attn_merge(outs, ob, tm)
    saved.update(branches=branches, b=(qb, kb, vb, bias_b, sink_b), merged=merged, ob=ob, lb=lb, o_cat=o_cat)
    x2 = _oproj_fwd(x1, o_cat, gb, BLK_W_O, 2 * tm)
    saved["x2"] = x2
    x3, saved["h2"], saved["zg2"], saved["zu2"], saved["s2"] = _ffn_fwd(
        x2, sm["norm_ffn2"][i][None], ga, gb, 1, 2 * tm)
    saved["x3"] = x3
    res = _ple_fwd(x3, sm["norm_ple"][i][None], gb, BLK_GATE, p, w["w_proj"], target, tm)
    y, saved["hp"], saved["gate"], saved["pp"], saved["pb"] = res[:5]
    loss = res[5] if target is not None else None
    return y, loss, saved


def _layer_bwd(dy, w, sm, i, sv, tm, dep=None, on_ready=None, on_small=None, on_last=None):
    ga, gb = w["ga"], w["gb"]
    gs = {}
    D = dy.shape[1]
    dgl, dpp = _ple_bwd(dy, sv["gate"], sv["pp"], tm, dep)
    d_gate = _matmul_tn(sv["hp"], dgl, D, 2 * tm)
    d_proj = _matmul_tn(sv["pb"], dpp, D, 2 * tm)
    dx3, gs["norm_ple"] = _dense_norm_bwd(dy, dgl, gb, BLK_GATE, sv["x3"], sm["norm_ple"][i][None], 2 * tm)
    dx2, dyb, dzg, dzu, gs["norm_ffn2"] = _ffn_bwd(dx3, sv["x2"], sm["norm_ffn2"][i][None], sv["zg2"], sv["zu2"],
                                                   ga, gb, 1, tm)
    dwin2, dwo2 = _ffn_dw(sv["h2"], dzg, dzu, sv["s2"], dyb, 2 * tm)
    half = dwo2.shape[1] // 2
    after_ffn2 = [dwin2, jnp.concatenate([dwo2.reshape(N_DEV, half, D), d_gate.reshape(N_DEV, -1, D)], axis=1),
                  _to_col_shards(d_proj)]
    token = None if on_ready is None else on_ready(0, after_ffn2)
    dx2b, do_b, do_a = _oproj_bwd(dx2, gb, BLK_W_O, tm, token)
    d_wo = _matmul_tn(sv["o_cat"], dx2b, D, 2 * tm)
    dqa, dka, dva, dbias = [], [], [], []
    for (qd, kd, vd, bias, sink, R, d), (oa, la), do_d in zip(sv["branches"], sv["merged"], do_a):
        dq, dk, dv, dbm, _ = _attn_bwd(qd, kd, vd, bias[1], sink, oa, la, do_d, R, 1, d)
        dqa.append(dq)
        dka.append(dk)
        dva.append(dv)
        dbias.append(dbm)
    qb, kb, vb, bias_b, sink_b = sv["b"]
    dqb, dkb, dvb, dbm_b, dsink = _attn_bwd(qb, kb, vb, bias_b[1], sink_b, sv["ob"], sv["lb"], do_b,
                                            SWA_RADIUS, 2, 1)
    gs["rel_bias"] = dbias + [dbm_b]
    gs["sink_b"] = jnp.sum(dsink[:, 0].reshape(-1, 2, BQ), axis=2).reshape(-1)
    dqkv, dgains2 = _attn_post(sv["qkv"], sv["gains2"], dqa, dka, dva, dqb,
                               dkb, dvb, tm)
    dgains = dgains2[:, :HEAD_DIM] + dgains2[:, HEAD_DIM:]
    for k, name in enumerate(("q_norm_a", "k_norm_a", "q_norm_b", "k_norm_b")):
        gs[name] = dgains[k]
    d_qkv = _matmul_tn(sv["hm"], dqkv, dqkv.shape[1] // 2, 2 * tm)
    after_mixer = [d_wo.reshape(N_DEV, -1, D), _to_col_shards(d_qkv)]
    token = None if on_ready is None else on_ready(1, after_mixer)
    dx1, gs["norm_mix"] = _dense_norm_bwd(dx2, dqkv, w["w_qkv"], None, sv["x1"], sm["norm_mix"][i][None], 2 * tm)
    g1 = sm["norm_ffn1"][i][None]
    if on_last is None:
        dx0, dyb, dzg, dzu, gs["norm_ffn1"] = _ffn_bwd(dx1, sv["x0"], g1, sv["zg1"], sv["zu1"], ga, gb, 0, tm, token)
        dwin1, dwo1 = _ffn_dw(sv["h1"], dzg, dzu, sv["s1"], dyb, 2 * tm)
        return dx0, (after_ffn2, after_mixer, [dwin1, dwo1.reshape(N_DEV, half, D)]), gs
    dyb, dzg, dzu = _ffn_bwd_dz(dx1, sv["zg1"], sv["zu1"], gb, 0, 2 * tm, token)
    dwin1, dwo1 = _ffn_dw(sv["h1"], dzg, dzu, sv["s1"], dyb, 2 * tm, on_small(gs))
    last = [dwin1, dwo1.reshape(N_DEV, half, D)]
    dx0, gs["norm_ffn1"] = _ffn_bwd_dx(dx1, sv["x0"], g1, dzg, dzu, ga, 0, 2 * tm, on_last(last))
    return dx0, (after_ffn2, after_mixer, last), gs


def _bias_matrices(rel_bias):
    biases = [_bias_variants(_bias_matrix(rel_bias[:, :8], R, d), R) for R, d in DILATED]
    biases.append(_bias_variants(_bias_matrix(rel_bias[:, 8:], SWA_RADIUS, 1), SWA_RADIUS))
    return biases


def _stack_small(per_layer):
    small = {}
    for k, v in per_layer.items():
        if k == "rel_bias":
            per_branch = [sum(parts) for parts in zip(*v.values())]
            drel_a = sum(_bias_grad(t, R, d) for t, (R, d) in zip(per_branch[:3], DILATED))
            small[k] = jnp.concatenate([drel_a, _bias_grad(per_branch[3], SWA_RADIUS, 1)], axis=1)
        else:
            small[k] = jnp.stack([v[i].reshape(-1) for i in sorted(v)])
    return small


TM = 512
SUM_TILES = (512, 480, 256, 128, 512, 512, 352)
LAST_GROUP = ("ffn1_w_in", "ffn1_w_out")


def _pack_small(d, extra=None):
    parts = [d[k].reshape(-1) for k in SMALL]
    if extra is not None:
        parts.append(extra.reshape(-1))
    flat = jnp.concatenate(parts)
    return jnp.pad(flat, (0, SMALL_ROWS * 128 - flat.shape[0])).reshape(SMALL_ROWS, 128)


def _unpack_small(buf, like):
    flat = buf.reshape(-1)
    out, off = {}, 0
    for k in SMALL:
        n = like[k].size
        out[k] = flat[off:off + n].reshape(like[k].shape)
        off += n
    return out, flat[off]


def kernel(x, p, rel_bias, norm_ffn1, ffn1_w_in, ffn1_w_out, norm_mix, w_qkv, q_norm_a, k_norm_a, q_norm_b, k_norm_b, sink_b, w_o, norm_ffn2, ffn2_w_in, ffn2_w_out, norm_ple, w_ple_gate, w_ple_proj, loss_target, m_rel_bias, m_norm_ffn1, m_ffn1_w_in, m_ffn1_w_out, m_norm_mix, m_w_qkv, m_q_norm_a, m_k_norm_a, m_q_norm_b, m_k_norm_b, m_sink_b, m_w_o, m_norm_ffn2, m_ffn2_w_in, m_ffn2_w_out, m_norm_ple, m_w_ple_gate, m_w_ple_proj, v_rel_bias, v_norm_ffn1, v_ffn1_w_in, v_ffn1_w_out, v_norm_mix, v_w_qkv, v_q_norm_a, v_k_norm_a, v_q_norm_b, v_k_norm_b, v_sink_b, v_w_o, v_norm_ffn2, v_ffn2_w_in, v_ffn2_w_out, v_norm_ple, v_w_ple_gate, v_w_ple_proj):
    wts = dict(rel_bias=rel_bias, norm_ffn1=norm_ffn1, ffn1_w_in=ffn1_w_in, ffn1_w_out=ffn1_w_out,
               norm_mix=norm_mix, w_qkv=w_qkv, q_norm_a=q_norm_a, k_norm_a=k_norm_a, q_norm_b=q_norm_b,
               k_norm_b=k_norm_b, sink_b=sink_b, w_o=w_o, norm_ffn2=norm_ffn2, ffn2_w_in=ffn2_w_in,
               ffn2_w_out=ffn2_w_out, norm_ple=norm_ple, w_ple_gate=w_ple_gate, w_ple_proj=w_ple_proj)
    mom = dict(rel_bias=m_rel_bias, norm_ffn1=m_norm_ffn1, ffn1_w_in=m_ffn1_w_in, ffn1_w_out=m_ffn1_w_out,
               norm_mix=m_norm_mix, w_qkv=m_w_qkv, q_norm_a=m_q_norm_a, k_norm_a=m_k_norm_a, q_norm_b=m_q_norm_b,
               k_norm_b=m_k_norm_b, sink_b=m_sink_b, w_o=m_w_o, norm_ffn2=m_norm_ffn2, ffn2_w_in=m_ffn2_w_in,
               ffn2_w_out=m_ffn2_w_out, norm_ple=m_norm_ple, w_ple_gate=m_w_ple_gate, w_ple_proj=m_w_ple_proj)
    var = dict(rel_bias=v_rel_bias, norm_ffn1=v_norm_ffn1, ffn1_w_in=v_ffn1_w_in, ffn1_w_out=v_ffn1_w_out,
               norm_mix=v_norm_mix, w_qkv=v_w_qkv, q_norm_a=v_q_norm_a, k_norm_a=v_k_norm_a, q_norm_b=v_q_norm_b,
               k_norm_b=v_k_norm_b, sink_b=v_sink_b, w_o=v_w_o, norm_ffn2=v_norm_ffn2, ffn2_w_in=v_ffn2_w_in,
               ffn2_w_out=v_ffn2_w_out, norm_ple=v_norm_ple, w_ple_gate=v_w_ple_gate, w_ple_proj=v_w_ple_proj)
    sm = {k: wts[k] for k in SMALL}
    me = 4 * lax.axis_index("x") + 2 * lax.axis_index("y") + lax.axis_index("c")
    packed = []
    for i in range(2):
        a, *rest = _pack_layer(wts, i)
        packed.append([t.astype(BF16) for t in [a.reshape(-1, a.shape[-1])] + rest])
    a_shape = (2, ffn1_w_in.shape[1], ffn1_w_in.shape[2])

    def weights_of(zones):
        return _layer_weights(zones[0].reshape((N_DEV,) + a_shape), *zones[1:])

    w0 = weights_of(_all_gather(packed[0]))
    zone_shapes = [(N_DEV,) + t.shape for t in packed[1]]
    ssem, rsem, thru, zones, token = _exchange_start(packed[1], zone_shapes, False, "gather_start")
    biases = _bias_matrices(rel_bias)
    x1, _, sv0 = _layer_fwd(x[0], p[0, 0], w0, sm, 0, None, TM, biases, dep=token)
    zones = _exchange_wait(ssem, rsem, thru, zones, x1, False, "gather_wait")
    w1 = weights_of([lax.dynamic_update_index_in_dim(z, t, me, 0) for z, t in zip(zones, packed[1])])
    dy, loss, sv1 = _layer_fwd(x1, p[1, 0], w1, sm, 1, loss_target[0], TM, biases)

    def slots_for(arrs):
        return [(N_DEV - 1,) + t.shape[1:] for t in arrs]

    held1, held = {}, {}

    def on_ready1(stage, group):
        held1[stage] = _exchange_start(group, slots_for(group), True, f"scatter1_start_{stage}")
        return held1[stage][4]

    dx1, groups1, gs1 = _layer_bwd(dy, w1, sm, 1, sv1, TM, on_ready=on_ready1)
    on_ready1(2, groups1[2])
    g1 = groups1[0] + groups1[1] + groups1[2]

    def on_ready(stage, group):
        if stage == 1:
            held["slots1"] = [t for st in (0, 1, 2)
                              for t in _exchange_wait(*held1[st][:4], group[0], True, f"scatter1_wait_{st}")]
        held[stage] = _exchange_start(group, slots_for(group), True, f"scatter_start_{stage}")
        return held[stage][4]

    def on_small(gs0):
        part = dict(gs0, norm_ffn1=jnp.zeros_like(gs1["norm_ffn1"]))
        gsmall = _stack_small({k: {0: part[k], 1: gs1[k]} for k in part})
        held["small"] = _all_reduce_small(_pack_small(gsmall, loss[0, :1]))
        return held["small"]

    def on_last(group):
        held["last"] = _exchange_start(group, slots_for(group), True, "scatter_start_2")
        return held["last"][4]

    dx, groups0, gs0 = _layer_bwd(dx1, w0, sm, 0, sv0, TM, dep=held1[2][4], on_ready=on_ready, on_small=on_small,
                                  on_last=on_last)
    last = groups0[2]
    slots0 = [_exchange_wait(*held[stage][:4], last[0], True, f"scatter_wait_{stage}") for stage in (0, 1)]

    def summed(arrs, slots, tiles, dep=None):
        return [_sum_parts(lax.dynamic_index_in_dim(t, me, 0, keepdims=False), s_, tr, dep)
                for t, s_, tr in zip(arrs, slots, tiles)]

    cover = held["last"][4]
    r1 = summed(g1, held["slots1"], SUM_TILES, cover)
    r0 = summed(groups0[0], slots0[0], SUM_TILES[:3], cover) + summed(groups0[1], slots0[1], SUM_TILES[3:5], cover)

    def update(names, layers):
        for k in names:
            grads[k] = jnp.stack([layers[0][k], layers[1][k]])
            delta[k], new_m[k], new_v[k] = _adamw(wts[k], grads[k], mom[k], var[k])

    grads, delta, new_m, new_v = {}, {}, {}, {}
    layer1 = _unpack_layer(r1, wts)
    update([k for k in BIG if k not in LAST_GROUP], [_unpack_layer(r0 + [None, None], wts), layer1])

    cover_done = [dx] + [delta[k] for k in BIG if k not in LAST_GROUP]
    slots_last = _exchange_wait(*held["last"][:4], cover_done, True, "scatter_wait_2")
    update(LAST_GROUP, [_unpack_layer([None] * 5 + summed(last, slots_last, SUM_TILES[5:]), wts), layer1])
    late = _all_reduce_small(gs0["norm_ffn1"].reshape(-1, 128), dep=slots_last[0])
    small_sum, loss_sum = _unpack_small(held["small"], sm)
    small_sum["norm_ffn1"] = small_sum["norm_ffn1"].at[0].add(late.reshape(-1))
    grads.update(small_sum)
    zeros = {k: jnp.zeros_like(wts[k]) for k in SMALL}
    ds, ms, vs = _adamw(_pack_small(wts), _pack_small(small_sum), _pack_small(mom), _pack_small(var))
    for packed, dst in ((ds, delta), (ms, new_m), (vs, new_v)):
        dst.update(_unpack_small(packed, zeros)[0])

    return (loss_sum, dx[None], *[grads[k] for k in WEIGHTS], *[delta[k] for k in WEIGHTS],
            *[new_m[k] for k in WEIGHTS], *[new_v[k] for k in WEIGHTS])
```

```python
import functools
import math

import jax
import jax.numpy as jnp
from jax import lax
from jax.experimental import pallas as pl
from jax.experimental.pallas import tpu as pltpu

F32 = jnp.float32
BF16 = jnp.bfloat16

N_DEV = 8
HEAD_DIM = 64
PAIR = 2 * HEAD_DIM
BQ = 128
N_BUCKETS = 32
MAX_DISTANCE = 1024
DILATED = ((64, 1), (64, 4), (64, 16))
SWA_RADIUS = 128
EPS = 1e-6
NEG = -1e30
ADAM_LR, ADAM_B1, ADAM_B2, ADAM_EPS, ADAM_WD, ADAM_STEP = 0.001, 0.9, 0.999, 1e-08, 0.01, 10
VMEM_LIMIT = 56 * 1024 * 1024
MESH = pl.DeviceIdType.MESH

BIG = ("ffn1_w_in", "ffn1_w_out", "w_qkv", "w_o", "ffn2_w_in", "ffn2_w_out", "w_ple_gate", "w_ple_proj")
SMALL = ("rel_bias", "norm_ffn1", "norm_mix", "q_norm_a", "k_norm_a", "q_norm_b", "k_norm_b", "sink_b",
         "norm_ffn2", "norm_ple")
WEIGHTS = ("rel_bias", "norm_ffn1", "ffn1_w_in", "ffn1_w_out", "norm_mix", "w_qkv", "q_norm_a", "k_norm_a",
           "q_norm_b", "k_norm_b", "sink_b", "w_o", "norm_ffn2", "ffn2_w_in", "ffn2_w_out", "norm_ple",
           "w_ple_gate", "w_ple_proj")
SMALL_ROWS = 96


def _params(*sem):
    return pltpu.CompilerParams(dimension_semantics=sem, vmem_limit_bytes=VMEM_LIMIT)


def _dot(a, b):
    return jnp.dot(a, b, preferred_element_type=F32)


def _dot_nt(a, b):
    return lax.dot_general(a, b, (((1,), (1,)), ((), ())), preferred_element_type=F32)


def _dot_tn(a, b):
    return lax.dot_general(a, b, (((0,), (0,)), ((), ())), preferred_element_type=F32)


def _sigmoid(x):
    return 1.0 / (1.0 + jnp.exp(-x))


def _rstd(xv):
    return lax.rsqrt(jnp.mean(xv * xv, axis=-1, keepdims=True) + EPS)


def _norm_bwd(dh, xv, gv):
    r = _rstd(xv)
    xn = xv * r
    dg = jnp.sum(dh * xn, axis=0, keepdims=True)
    dxn = dh * gv
    dx = r * (dxn - xn * jnp.mean(dxn * xn, axis=-1, keepdims=True))
    return dx, dg


def _lo_mask(shape):
    return lax.broadcasted_iota(jnp.int32, shape, len(shape) - 1) < HEAD_DIM


def _half_sum(t, lo):
    s0 = jnp.sum(jnp.where(lo, t, 0.0), axis=1, keepdims=True)
    s1 = jnp.sum(jnp.where(lo, 0.0, t), axis=1, keepdims=True)
    return jnp.where(lo, s0, s1)


FFN_PARTS = 2


def _ffn_weight_specs(f, nj, D, C):
    return [pl.BlockSpec((None, None, D, C), lambda i, j: (j, f, 0, 0)),
            pl.BlockSpec((None, None, D, C), lambda i, j: (j + nj, f, 0, 0)),
            pl.BlockSpec((2, C // 2, D), lambda i, j: (j, f, 0))]


def _with_dep(body, dep, in_specs, args):
    if dep is None:
        return body, in_specs, args

    def body_after(dep_ref, *refs):
        body(*refs)

    return body_after, [pl.BlockSpec(memory_space=pl.ANY)] + in_specs, [dep] + args


def _ffn_fwd(x, g, ga, gb, f, tm, dep=None):
    T, D = x.shape
    nj, C = ga.shape[0] // 2, ga.shape[3]

    def body(x_ref, g_ref, wg_ref, wu_ref, wo_ref, xo_ref, h_ref, zg_ref, zu_ref, s_ref, h_scr, acc):
        j = pl.program_id(1)

        @pl.when(j == 0)
        def _():
            xv = x_ref[...]
            hb = (xv * _rstd(xv) * g_ref[...]).astype(BF16)
            h_scr[...] = hb
            h_ref[...] = hb
            acc[...] = jnp.zeros_like(acc)

        wo = wo_ref[...].reshape(C, D)
        for part in range(FFN_PARTS):
            sl = pl.ds(part * (tm // FFN_PARTS), tm // FFN_PARTS)
            hb = h_scr[sl, :]
            gt = _dot(hb, wg_ref[...])
            up = _dot(hb, wu_ref[...])
            s = (gt * _sigmoid(gt) * up).astype(BF16)
            zg_ref[sl, :] = gt.astype(BF16)
            zu_ref[sl, :] = up.astype(BF16)
            s_ref[sl, :] = s
            acc[sl, :] += _dot(s, wo)

        @pl.when(j == nj - 1)
        def _():
            xo_ref[...] = x_ref[...] + 0.5 * acc[...]

    tok = pl.BlockSpec((tm, D), lambda i, j: (i, 0))
    chunk = pl.BlockSpec((None, tm, C), lambda i, j: (j, i, 0))
    in_specs = [tok, pl.BlockSpec((1, D), lambda i, j: (0, 0))] + _ffn_weight_specs(f, nj, D, C)
    body, in_specs, args = _with_dep(body, dep, in_specs, [x, g, ga, ga, gb])
    return pl.pallas_call(
        body, name="ffn_fwd", grid=(T // tm, nj),
        in_specs=in_specs,
        out_specs=[tok, tok, chunk, chunk, chunk],
        out_shape=[jax.ShapeDtypeStruct((T, D), F32), jax.ShapeDtypeStruct((T, D), BF16),
                   jax.ShapeDtypeStruct((nj, T, C), BF16), jax.ShapeDtypeStruct((nj, T, C), BF16),
                   jax.ShapeDtypeStruct((nj, T, C), BF16)],
        scratch_shapes=[pltpu.VMEM((tm, D), BF16), pltpu.VMEM((tm, D), F32)],
        compiler_params=_params("parallel", "arbitrary"),
    )(*args)


def _ffn_bwd(dxo, x, g, zg, zu, ga, gb, f, tm, dep=None):
    T, D = x.shape
    nj, C = ga.shape[0] // 2, ga.shape[3]

    def body(dxo_ref, x_ref, g_ref, zg_ref, zu_ref, wg_ref, wu_ref, wo_ref,
             dx_ref, dy_ref, dzg_ref, dzu_ref, dgn_ref, dy_scr, acc):
        i, j = pl.program_id(0), pl.program_id(1)

        @pl.when(j == 0)
        def _():
            dyb = (0.5 * dxo_ref[...]).astype(BF16)
            dy_scr[...] = dyb
            dy_ref[...] = dyb
            acc[...] = jnp.zeros_like(acc)

        wo = wo_ref[...].reshape(C, D)
        for part in range(FFN_PARTS):
            sl = pl.ds(part * (tm // FFN_PARTS), tm // FFN_PARTS)
            ds = _dot_nt(dy_scr[sl, :], wo)
            gt = zg_ref[sl, :].astype(F32)
            up = zu_ref[sl, :].astype(F32)
            sg = _sigmoid(gt)
            dgt = (ds * up * (sg * (1.0 + gt * (1.0 - sg)))).astype(BF16)
            dup = (ds * (gt * sg)).astype(BF16)
            dzg_ref[sl, :] = dgt
            dzu_ref[sl, :] = dup
            acc[sl, :] += _dot_nt(dgt, wg_ref[...]) + _dot_nt(dup, wu_ref[...])

        @pl.when(j == nj - 1)
        def _():
            dx, dg = _norm_bwd(acc[...], x_ref[...], g_ref[...])
            dx_ref[...] = dxo_ref[...] + dx

            @pl.when(i == 0)
            def _():
                dgn_ref[...] = dg

            @pl.when(i > 0)
            def _():
                dgn_ref[...] += dg

    tok = pl.BlockSpec((tm, D), lambda i, j: (i, 0))
    chunk = pl.BlockSpec((None, tm, C), lambda i, j: (j, i, 0))
    row = pl.BlockSpec((1, D), lambda i, j: (0, 0))
    in_specs = [tok, tok, row, chunk, chunk] + _ffn_weight_specs(f, nj, D, C)
    body, in_specs, args = _with_dep(body, dep, in_specs, [dxo, x, g, zg, zu, ga, ga, gb])
    return pl.pallas_call(
        body, name="ffn_bwd", grid=(T // tm, nj),
        in_specs=in_specs,
        out_specs=[tok, tok, chunk, chunk, row],
        out_shape=[jax.ShapeDtypeStruct((T, D), F32), jax.ShapeDtypeStruct((T, D), BF16),
                   jax.ShapeDtypeStruct((nj, T, C), BF16), jax.ShapeDtypeStruct((nj, T, C), BF16),
                   jax.ShapeDtypeStruct((1, D), F32)],
        scratch_shapes=[pltpu.VMEM((tm, D), BF16), pltpu.VMEM((tm, D), F32)],
        compiler_params=_params("arbitrary", "arbitrary"),
    )(*args)


def _ffn_bwd_dz(dxo, zg, zu, gb, f, tm, dep=None):
    T, D = dxo.shape
    nj, C = zg.shape[0], zg.shape[2]

    def body(dxo_ref, zg_ref, zu_ref, wo_ref, dy_ref, dzg_ref, dzu_ref, dy_scr):
        @pl.when(pl.program_id(1) == 0)
        def _():
            dyb = (0.5 * dxo_ref[...]).astype(BF16)
            dy_scr[...] = dyb
            dy_ref[...] = dyb

        wo = wo_ref[...].reshape(C, D)
        for part in range(FFN_PARTS):
            sl = pl.ds(part * (tm // FFN_PARTS), tm // FFN_PARTS)
            ds = _dot_nt(dy_scr[sl, :], wo)
            gt = zg_ref[sl, :].astype(F32)
            up = zu_ref[sl, :].astype(F32)
            sg = _sigmoid(gt)
            dzg_ref[sl, :] = (ds * up * (sg * (1.0 + gt * (1.0 - sg)))).astype(BF16)
            dzu_ref[sl, :] = (ds * (gt * sg)).astype(BF16)

    tok = pl.BlockSpec((tm, D), lambda i, j: (i, 0))
    chunk = pl.BlockSpec((None, tm, C), lambda i, j: (j, i, 0))
    in_specs = [tok, chunk, chunk, _ffn_weight_specs(f, nj, D, C)[2]]
    body, in_specs, args = _with_dep(body, dep, in_specs, [dxo, zg, zu, gb])
    return pl.pallas_call(
        body, name="ffn_bwd_dz", grid=(T // tm, nj),
        in_specs=in_specs, out_specs=[tok, chunk, chunk],
        out_shape=[jax.ShapeDtypeStruct((T, D), BF16), jax.ShapeDtypeStruct((nj, T, C), BF16),
                   jax.ShapeDtypeStruct((nj, T, C), BF16)],
        scratch_shapes=[pltpu.VMEM((tm, D), BF16)],
        compiler_params=_params("parallel", "arbitrary"),
    )(*args)


def _ffn_bwd_dx(dxo, x, g, dzg, dzu, ga, f, tm, dep=None):
    T, D = x.shape
    nj, C = ga.shape[0] // 2, ga.shape[3]

    def body(dxo_ref, x_ref, g_ref, dzg_ref, dzu_ref, wg_ref, wu_ref, dx_ref, dgn_ref, acc):
        i, j = pl.program_id(0), pl.program_id(1)

        @pl.when(j == 0)
        def _():
            acc[...] = jnp.zeros_like(acc)

        acc[...] += _dot_nt(dzg_ref[...], wg_ref[...]) + _dot_nt(dzu_ref[...], wu_ref[...])

        @pl.when(j == nj - 1)
        def _():
            dx, dg = _norm_bwd(acc[...], x_ref[...], g_ref[...])
            dx_ref[...] = dxo_ref[...] + dx

            @pl.when(i == 0)
            def _():
                dgn_ref[...] = dg

            @pl.when(i > 0)
            def _():
                dgn_ref[...] += dg

    tok = pl.BlockSpec((tm, D), lambda i, j: (i, 0))
    chunk = pl.BlockSpec((None, tm, C), lambda i, j: (j, i, 0))
    row = pl.BlockSpec((1, D), lambda i, j: (0, 0))
    in_specs = [tok, tok, row, chunk, chunk] + _ffn_weight_specs(f, nj, D, C)[:2]
    body, in_specs, args = _with_dep(body, dep, in_specs, [dxo, x, g, dzg, dzu, ga, ga])
    return pl.pallas_call(
        body, name="ffn_bwd_dx", grid=(T // tm, nj),
        in_specs=in_specs, out_specs=[tok, row],
        out_shape=[jax.ShapeDtypeStruct((T, D), F32), jax.ShapeDtypeStruct((1, D), F32)],
        scratch_shapes=[pltpu.VMEM((tm, D), F32)],
        compiler_params=_params("arbitrary", "arbitrary"),
    )(*args)


def _ffn_dw(h, dzg, dzu, s, dy, tk, dep=None):
    T, D = h.shape
    nj, C = s.shape[0], s.shape[2]
    nk = T // tk

    def body(h_ref, dzg_ref, dzu_ref, s_ref, dy_ref, dwin_ref, dwo_ref, ag, au, ao):
        k = pl.program_id(1)

        @pl.when(k == 0)
        def _():
            ag[...] = jnp.zeros_like(ag)
            au[...] = jnp.zeros_like(au)
            ao[...] = jnp.zeros_like(ao)

        hb = h_ref[...]
        ag[...] += _dot_tn(hb, dzg_ref[...])
        au[...] += _dot_tn(hb, dzu_ref[...])
        ao[...] += _dot_tn(s_ref[...], dy_ref[...])

        @pl.when(k == nk - 1)
        def _():
            dwin_ref[0] = ag[...].astype(BF16)
            dwin_ref[1] = au[...].astype(BF16)
            dwo_ref[...] = ao[...].astype(BF16)

    tok = pl.BlockSpec((tk, D), lambda j, k: (k, 0))
    chunk = pl.BlockSpec((None, tk, C), lambda j, k: (j, k, 0))
    body, in_specs, args = _with_dep(body, dep, [tok, chunk, chunk, chunk, tok], [h, dzg, dzu, s, dy])
    dwin, dwo = pl.pallas_call(
        body, name="ffn_dw", grid=(nj, nk),
        in_specs=in_specs,
        out_specs=[pl.BlockSpec((2, None, D, C), lambda j, k: (0, j, 0, 0)),
                   pl.BlockSpec((None, C, D), lambda j, k: (j, 0, 0))],
        out_shape=[jax.ShapeDtypeStruct((2, nj, D, C), BF16), jax.ShapeDtypeStruct((nj, C, D), BF16)],
        scratch_shapes=[pltpu.VMEM((D, C), F32), pltpu.VMEM((D, C), F32), pltpu.VMEM((C, D), F32)],
        compiler_params=_params("parallel", "arbitrary"),
    )(*args)
    return dwin.reshape(2 * nj, D, C), dwo


def _matmul_tn(a, b, tn, tk):
    T, Ka = a.shape
    N = b.shape[1]
    nk = T // tk

    def body(a_ref, b_ref, o_ref, acc):
        k = pl.program_id(1)

        @pl.when(k == 0)
        def _():
            acc[...] = jnp.zeros_like(acc)

        acc[...] += _dot_tn(a_ref[...], b_ref[...])

        @pl.when(k == nk - 1)
        def _():
            o_ref[...] = acc[...].astype(BF16)

    return pl.pallas_call(
        body, name="matmul_tn", grid=(N // tn, nk),
        in_specs=[pl.BlockSpec((tk, Ka), lambda n, k: (k, 0)), pl.BlockSpec((tk, tn), lambda n, k: (k, n))],
        out_specs=pl.BlockSpec((Ka, tn), lambda n, k: (0, n)),
        out_shape=jax.ShapeDtypeStruct((Ka, N), BF16),
        scratch_shapes=[pltpu.VMEM((Ka, tn), F32)],
        compiler_params=_params("parallel", "arbitrary"),
    )(a, b)


def _qkv_fwd(x, g, w, tm):
    T, D = x.shape
    N = w.shape[1]

    def body(x_ref, g_ref, w_ref, o_ref, h_ref):
        xv = x_ref[...]
        hb = (xv * _rstd(xv) * g_ref[...]).astype(BF16)
        h_ref[...] = hb
        o_ref[...] = _dot(hb, w_ref[...])

    return pl.pallas_call(
        body, name="qkv_fwd", grid=(T // tm,),
        in_specs=[pl.BlockSpec((tm, D), lambda i: (i, 0)), pl.BlockSpec((1, D), lambda i: (0, 0)),
                  pl.BlockSpec((D, N), lambda i: (0, 0))],
        out_specs=[pl.BlockSpec((tm, N), lambda i: (i, 0)), pl.BlockSpec((tm, D), lambda i: (i, 0))],
        out_shape=[jax.ShapeDtypeStruct((T, N), F32), jax.ShapeDtypeStruct((T, D), BF16)],
        compiler_params=_params("parallel"),
    )(x, g, w)


DILS = tuple(d for _, d in DILATED)


def _spread_specs(tm, T, dtype):
    specs = [pl.BlockSpec((4, d, tm // d, PAIR), lambda i: (0, 0, i, 0)) for d in DILS]
    shapes = [jax.ShapeDtypeStruct((4, d, T // d, PAIR), dtype) for d in DILS]
    return specs, shapes


def _spread(tile, y, outs, c, dtype):
    tm = y.shape[0]
    tile[...] = y
    for out, d in zip(outs, DILS):
        for r in range(d):
            out[c, r] = tile[pl.ds(r, tm // d, stride=d), :].astype(dtype)


def _collect(tile, ins, c):
    tm = tile.shape[0]
    first = True
    for ref, d in zip(ins, DILS):
        for r in range(d):
            rows = pl.ds(r, tm // d, stride=d) if d > 1 else pl.ds(0, tm)
            part = ref[c, r].astype(F32)
            tile[rows, :] = part if first else tile[rows, :] + part
        first = False
    return tile[...]


def _attn_prep(qkv, gains2, tm):
    T = qkv.shape[0]
    scale = HEAD_DIM ** -0.5
    n = len(DILS)

    def body(qkv_ref, g_ref, qb_ref, kb_ref, vb_ref, *rest):
        outs, tile = rest[:-1], rest[-1]
        lo = _lo_mask((tm, PAIR))

        def spread(kind, c, y):
            _spread(tile, y, outs[kind * n:(kind + 1) * n], c, BF16)

        def normed(c, gi, mult):
            xv = qkv_ref[:, c * PAIR:(c + 1) * PAIR]
            r = lax.rsqrt(_half_sum(xv * xv, lo) * (1.0 / HEAD_DIM) + EPS)
            y = xv * r * g_ref[gi:gi + 1, :]
            return y * mult if mult != 1.0 else y

        def both_halves(v):
            sw = pltpu.roll(v, HEAD_DIM, 1)
            return jnp.where(lo, v, sw), jnp.where(lo, sw, v)

        for c in range(4):
            spread(0, c, normed(c, 0, scale))
            spread(1, c, normed(4 + c, 1, 1.0))
            spread(2, c, qkv_ref[:, (8 + c) * PAIR:(9 + c) * PAIR])
            qb_ref[c] = normed(12 + c, 2, scale).astype(BF16)
        k0, k1 = both_halves(normed(16, 3, 1.0))
        kb_ref[0] = k0.astype(BF16)
        kb_ref[1] = k1.astype(BF16)
        v0, v1 = both_halves(qkv_ref[:, 17 * PAIR:18 * PAIR])
        vb_ref[0] = v0.astype(BF16)
        vb_ref[1] = v1.astype(BF16)

    four = pl.BlockSpec((4, tm, PAIR), lambda i: (0, i, 0))
    two = pl.BlockSpec((2, tm, PAIR), lambda i: (0, i, 0))
    s4 = jax.ShapeDtypeStruct((4, T, PAIR), BF16)
    s2 = jax.ShapeDtypeStruct((2, T, PAIR), BF16)
    specs, shapes = _spread_specs(tm, T, BF16)
    res = pl.pallas_call(
        body, name="attn_prep", grid=(T // tm,),
        in_specs=[pl.BlockSpec((tm, qkv.shape[1]), lambda i: (i, 0)), pl.BlockSpec((4, PAIR), lambda i: (0, 0))],
        out_specs=[four, two, two] + specs * 3,
        out_shape=[s4, s2, s2] + shapes * 3,
        scratch_shapes=[pltpu.VMEM((tm, PAIR), F32)],
        compiler_params=_params("parallel"),
    )(qkv, gains2)
    qb, kb, vb = res[:3]
    per_d = [tuple(res[3 + kind * n + di].reshape(4 * d, T // d, PAIR) for kind in range(3))
             for di, d in enumerate(DILS)]
    return qb, kb, vb, per_d


def _loop_blocks(nb, body, init, per_iter):
    u = math.gcd(nb, per_iter)

    def outer(i, carry):
        for k in range(u):
            carry = body(i * u + k, carry)
        return carry

    return lax.fori_loop(0, nb // u, outer, init)


def _key_window(b, nb, L, R, W):
    start = pl.multiple_of(jnp.clip(b * BQ - R, 0, L - W), HEAD_DIM)
    return start, jnp.where(b == 0, 1, jnp.where(b == nb - 1, 2, 0))


def _stack_heads(v, lo):
    z = jnp.zeros_like(v)
    return jnp.concatenate([jnp.where(lo, v, z), jnp.where(lo, z, v)], axis=0)


def _unstack_heads(v2, lo):
    return jnp.where(lo, v2[:BQ], v2[BQ:])


def _row_vector(v, lo):
    r = lax.broadcasted_iota(jnp.int32, (BQ, PAIR), 0)
    ln = lax.broadcasted_iota(jnp.int32, (BQ, PAIR), 1)
    diag = (ln % HEAD_DIM) == (r % HEAD_DIM)
    top = jnp.sum(jnp.where(diag & (r < HEAD_DIM), v, 0.0), axis=0, keepdims=True)
    bot = jnp.sum(jnp.where(diag & (r >= HEAD_DIM), v, 0.0), axis=0, keepdims=True)
    top8, bot8 = jnp.broadcast_to(top, (8, PAIR)), jnp.broadcast_to(bot, (8, PAIR))
    lo8 = _lo_mask((8, PAIR))
    head0 = jnp.where(lo8, top8, pltpu.roll(bot8, HEAD_DIM, 1))
    head1 = jnp.where(lo8, pltpu.roll(top8, HEAD_DIM, 1), bot8)
    return jnp.concatenate([head0, head1], axis=1)[:1]


def _units_per_step(nb, pairs_per_kv):
    return max(1, 16 // nb) if pairs_per_kv == 1 else 1


def _attn_fwd(q, kp, vp, bias4, sink, R, pairs_per_kv, pairs_per_bias):
    N, L, _ = q.shape
    W = BQ + 2 * R
    nb = L // BQ
    assert L >= W and nb >= 2
    G = _units_per_step(nb, pairs_per_kv)

    def body(sink_ref, q_ref, k_ref, v_ref, bias_ref, o_ref, lse_ref):
        n = pl.program_id(0)
        lo_q = _lo_mask((BQ, PAIR))
        first = lax.broadcasted_iota(jnp.int32, (2 * BQ, 1), 0) < BQ

        def blk(f, carry):
            g, b = f // nb, f % nb
            u = n * G + g
            sk = jnp.where(first, sink_ref[2 * u], sink_ref[2 * u + 1])
            q0 = pl.multiple_of(b * BQ, BQ)
            q2 = _stack_heads(q_ref[g, pl.ds(q0, BQ), :], lo_q)
            k0, variant = _key_window(b, nb, L, R, W)
            kw = k_ref[g, pl.ds(k0, W), :]
            vw = v_ref[g, pl.ds(k0, W), :]
            s = _dot_nt(q2, kw) + bias_ref[variant]
            m = jnp.maximum(jnp.max(s, axis=1, keepdims=True), sk)
            p = jnp.exp(s - m)
            l = jnp.sum(p, axis=1, keepdims=True) + jnp.exp(sk - m)
            o2 = _dot(p.astype(BF16), vw) / l
            o_ref[g, pl.ds(q0, BQ), :] = _unstack_heads(o2, lo_q)
            lse_ref[g, pl.ds(q0, BQ), :] = _unstack_heads(jnp.broadcast_to(m + jnp.log(l), (2 * BQ, PAIR)), lo_q)
            return carry

        _loop_blocks(G * nb, blk, 0, 16)

    qspec = pl.BlockSpec((G, L, PAIR), lambda n: (n, 0, 0))
    kspec = pl.BlockSpec((G, L, PAIR), lambda n: (n // pairs_per_kv, 0, 0))
    return pl.pallas_call(
        body, name="attn_fwd", grid=(N // G,),
        in_specs=[pl.BlockSpec(memory_space=pltpu.SMEM), qspec, kspec, kspec,
                  pl.BlockSpec((None, 3, 2 * BQ, W), lambda n: (n * G // pairs_per_bias, 0, 0, 0))],
        out_specs=[qspec, qspec],
        out_shape=[jax.ShapeDtypeStruct((N, L, PAIR), F32), jax.ShapeDtypeStruct((N, L, PAIR), F32)],
        compiler_params=_params("parallel"),
    )(sink, q, kp, vp, bias4)


def _attn_bwd(q, kp, vp, bias4t, sink, o, lse, do, R, pairs_per_kv, pairs_per_bias):
    N, L, _ = q.shape
    Nk = kp.shape[0]
    Pb = bias4t.shape[0]
    W = BQ + 2 * R
    nb = L // BQ
    assert L >= W and nb >= 2
    G = _units_per_step(nb, pairs_per_kv)

    def body(sink_ref, q_ref, k_ref, v_ref, bias_ref, o_ref, lse_ref, do_ref,
             dq_ref, dk_ref, dv_ref, dbias_ref, dsink_ref, dk_acc, dv_acc):
        n = pl.program_id(0)
        lo_q = _lo_mask((BQ, PAIR))
        first = lax.broadcasted_iota(jnp.int32, (1, 2 * BQ), 1) < BQ
        dsink_ref[...] = jnp.zeros_like(dsink_ref)

        @pl.when(n % pairs_per_kv == 0)
        def _():
            dk_acc[...] = jnp.zeros_like(dk_acc)
            dv_acc[...] = jnp.zeros_like(dv_acc)

        @pl.when((n * G) % pairs_per_bias == 0)
        def _():
            dbias_ref[...] = jnp.zeros_like(dbias_ref)

        def blk(f, carry):
            g, b = f // nb, f % nb
            u = n * G + g
            sk = jnp.where(first, sink_ref[2 * u], sink_ref[2 * u + 1])
            q0 = pl.multiple_of(b * BQ, BQ)
            q2 = _stack_heads(q_ref[g, pl.ds(q0, BQ), :], lo_q)
            k0, variant = _key_window(b, nb, L, R, W)
            kw = k_ref[g, pl.ds(k0, W), :]
            vw = v_ref[g, pl.ds(k0, W), :]
            dov = do_ref[g, pl.ds(q0, BQ), :]
            lse = _row_vector(lse_ref[g, pl.ds(q0, BQ), :], lo_q)
            delta = _row_vector(_half_sum(dov.astype(F32) * o_ref[g, pl.ds(q0, BQ), :], lo_q), lo_q)
            do2 = _stack_heads(dov.astype(BF16), lo_q)
            st = _dot_nt(kw, q2) + bias_ref[variant]
            pt = jnp.exp(st - lse)
            dst = pt * (_dot_nt(vw, do2) - delta)
            dstb = dst.astype(BF16)
            dbias_ref[variant] += dst
            dk_acc[g, pl.ds(k0, W), :] += _dot(dstb, q2)
            dv_acc[g, pl.ds(k0, W), :] += _dot(pt.astype(BF16), do2)
            dq_ref[g, pl.ds(q0, BQ), :] = _unstack_heads(_dot_tn(dstb, kw), lo_q).astype(BF16)
            dsink_ref[g, pl.ds(0, 1), :] -= jnp.exp(sk - lse) * delta
            return carry

        _loop_blocks(G * nb, blk, 0, 8)
        dk_ref[...] = dk_acc[...].astype(BF16)
        dv_ref[...] = dv_acc[...].astype(BF16)

    qspec = pl.BlockSpec((G, L, PAIR), lambda n: (n, 0, 0))
    kspec = pl.BlockSpec((G, L, PAIR), lambda n: (n // pairs_per_kv, 0, 0))
    return pl.pallas_call(
        body, name="attn_bwd", grid=(N // G,),
        in_specs=[pl.BlockSpec(memory_space=pltpu.SMEM), qspec, kspec, kspec,
                  pl.BlockSpec((None, 3, W, 2 * BQ), lambda n: (n * G // pairs_per_bias, 0, 0, 0)),
                  qspec, qspec, qspec],
        out_specs=[qspec, kspec, kspec,
                   pl.BlockSpec((None, 3, W, 2 * BQ), lambda n: (n * G // pairs_per_bias, 0, 0, 0)),
                   pl.BlockSpec((G, 8, 2 * BQ), lambda n: (n, 0, 0))],
        out_shape=[jax.ShapeDtypeStruct((N, L, PAIR), BF16),
                   jax.ShapeDtypeStruct((Nk, L, PAIR), BF16),
                   jax.ShapeDtypeStruct((Nk, L, PAIR), BF16),
                   jax.ShapeDtypeStruct((Pb, 3, W, 2 * BQ), F32),
                   jax.ShapeDtypeStruct((N, 8, 2 * BQ), F32)],
        scratch_shapes=[pltpu.VMEM((G, L, PAIR), F32), pltpu.VMEM((G, L, PAIR), F32)],
        compiler_params=_params("arbitrary"),
    )(sink, q, kp, vp, bias4t, o, lse, do)


def _attn_merge(branch_outs, ob, tm):
    T = ob.shape[1]
    n = len(DILS)

    def body(*refs):
        o_in, l_in, ob_ref = refs[:n], refs[n:2 * n], refs[2 * n]
        o_out, l_out, cat_ref = refs[2 * n + 1:3 * n + 1], refs[3 * n + 1:4 * n + 1], refs[4 * n + 1]
        tiles = refs[4 * n + 2:]
        for c in range(4):
            o_nat, l_nat = [], []
            for di, d in enumerate(DILS):
                for kind, (src, dst) in enumerate(((o_in[di], o_nat), (l_in[di], l_nat))):
                    tile = tiles[2 * di + kind]
                    if d == 1:
                        dst.append(src[c, 0])
                    else:
                        for r in range(d):
                            tile[pl.ds(r, tm // d, stride=d), :] = src[c, r]
                        dst.append(tile[...])
            m = functools.reduce(jnp.maximum, l_nat)
            ws = [jnp.exp(l - m) for l in l_nat]
            z = sum(ws)
            o = sum(w * t for w, t in zip(ws, o_nat)) / z
            cat_ref[:, c * PAIR:(c + 1) * PAIR] = o.astype(BF16)
            cat_ref[:, (4 + c) * PAIR:(5 + c) * PAIR] = ob_ref[c].astype(BF16)
            _spread(tiles[0], o, o_out, c, F32)
            _spread(tiles[1], m + jnp.log(z), l_out, c, F32)

    specs, shapes = _spread_specs(tm, T, F32)
    four = pl.BlockSpec((4, tm, PAIR), lambda i: (0, i, 0))
    o_views = [o.reshape(4, d, T // d, PAIR) for (o, _), d in zip(branch_outs, DILS)]
    l_views = [l.reshape(4, d, T // d, PAIR) for (_, l), d in zip(branch_outs, DILS)]
    res = pl.pallas_call(
        body, name="attn_merge", grid=(T // tm,),
        in_specs=specs + specs + [four],
        out_specs=specs + specs + [pl.BlockSpec((tm, 8 * PAIR), lambda i: (i, 0))],
        out_shape=shapes + shapes + [jax.ShapeDtypeStruct((T, 8 * PAIR), BF16)],
        scratch_shapes=[pltpu.VMEM((tm, PAIR), F32)] * (2 * n),
        compiler_params=_params("parallel"),
    )(*o_views, *l_views, ob)
    merged = [(res[di].reshape(4 * d, T // d, PAIR), res[n + di].reshape(4 * d, T // d, PAIR))
              for di, d in enumerate(DILS)]
    return merged, res[2 * n]


def _weight_arg(w, blk):
    if blk is None:
        return pl.BlockSpec(w.shape, lambda i: (0, 0)), (lambda ref: ref[...])
    D = w.shape[2]
    return (pl.BlockSpec((N_DEV, 128, D), lambda i: (0, blk, 0)),
            lambda ref: ref[...].reshape(N_DEV * 128, D))


def _oproj_fwd(x, o_cat, w, blk, tm):
    T, D = x.shape
    wspec, wload = _weight_arg(w, blk)

    def body(x_ref, o_ref, w_ref, out_ref):
        out_ref[...] = x_ref[...] + _dot(o_ref[...], wload(w_ref))

    tok = pl.BlockSpec((tm, D), lambda i: (i, 0))
    return pl.pallas_call(
        body, name="oproj_fwd", grid=(T // tm,),
        in_specs=[tok, pl.BlockSpec((tm, o_cat.shape[1]), lambda i: (i, 0)), wspec],
        out_specs=tok, out_shape=jax.ShapeDtypeStruct((T, D), F32),
        compiler_params=_params("parallel"),
    )(x, o_cat, w)


def _oproj_bwd(dx, w, blk, tm, dep=None):
    T, D = dx.shape
    wspec, wload = _weight_arg(w, blk)

    def body(dx_ref, w_ref, dxb_ref, dob_ref, *rest):
        doa_refs, tile = rest[:-1], rest[-1]
        db = dx_ref[...].astype(BF16)
        dxb_ref[...] = db
        do = _dot_nt(db, wload(w_ref))
        for c in range(4):
            _spread(tile, do[:, c * PAIR:(c + 1) * PAIR], doa_refs, c, BF16)
            dob_ref[c] = do[:, (4 + c) * PAIR:(5 + c) * PAIR].astype(BF16)

    tok = pl.BlockSpec((tm, D), lambda i: (i, 0))
    specs, shapes = _spread_specs(tm, T, BF16)
    body, in_specs, args = _with_dep(body, dep, [tok, wspec], [dx, w])
    res = pl.pallas_call(
        body, name="oproj_bwd", grid=(T // tm,),
        in_specs=in_specs,
        out_specs=[tok, pl.BlockSpec((4, tm, PAIR), lambda i: (0, i, 0))] + specs,
        out_shape=[jax.ShapeDtypeStruct((T, D), BF16), jax.ShapeDtypeStruct((4, T, PAIR), BF16)] + shapes,
        scratch_shapes=[pltpu.VMEM((tm, PAIR), F32)],
        compiler_params=_params("parallel"),
    )(*args)
    return res[0], res[1], [t.reshape(4 * d, T // d, PAIR) for t, d in zip(res[2:], DILS)]


def _attn_post(qkv, gains2, dqa, dka, dva, dqb, dkb, dvb, tm):
    T, NQ = qkv.shape
    scale = HEAD_DIM ** -0.5

    n = len(DILS)

    def body(qkv_ref, g_ref, *rest):
        dq_refs, dk_refs, dv_refs = rest[:n], rest[n:2 * n], rest[2 * n:3 * n]
        qb_ref, kb_ref, vb_ref, out_ref, dg_ref, tile = rest[3 * n:]
        lo = _lo_mask((tm, PAIR))

        @pl.when(pl.program_id(0) == 0)
        def _():
            dg_ref[...] = jnp.zeros_like(dg_ref)

        def norm_bwd(c, gi, dy):
            xv = qkv_ref[:, c * PAIR:(c + 1) * PAIR]
            r = lax.rsqrt(_half_sum(xv * xv, lo) * (1.0 / HEAD_DIM) + EPS)
            xn = xv * r
            dg_ref[gi:gi + 1, :] += jnp.sum(dy * xn, axis=0, keepdims=True)
            dxn = dy * g_ref[gi:gi + 1, :]
            dx = r * (dxn - xn * (_half_sum(dxn * xn, lo) * (1.0 / HEAD_DIM)))
            out_ref[:, c * PAIR:(c + 1) * PAIR] = dx.astype(BF16)

        def fold(v):
            return v + pltpu.roll(v, HEAD_DIM, 1)

        for c in range(4):
            norm_bwd(c, 0, _collect(tile, dq_refs, c) * scale)
            norm_bwd(4 + c, 1, _collect(tile, dk_refs, c))
            out_ref[:, (8 + c) * PAIR:(9 + c) * PAIR] = _collect(tile, dv_refs, c).astype(BF16)
            norm_bwd(12 + c, 2, qb_ref[c].astype(F32) * scale)
        kb, vb = kb_ref[...].astype(F32), vb_ref[...].astype(F32)
        norm_bwd(16, 3, jnp.where(lo, fold(kb[0]), fold(kb[1])))
        out_ref[:, 17 * PAIR:18 * PAIR] = jnp.where(lo, fold(vb[0]), fold(vb[1])).astype(BF16)

    four = pl.BlockSpec((4, tm, PAIR), lambda i: (0, i, 0))
    two = pl.BlockSpec((2, tm, PAIR), lambda i: (0, i, 0))
    specs, _ = _spread_specs(tm, T, BF16)
    views = [t.reshape(4, d, T // d, PAIR) for group in (dqa, dka, dva) for t, d in zip(group, DILS)]
    return pl.pallas_call(
        body, name="attn_post", grid=(T // tm,),
        in_specs=[pl.BlockSpec((tm, NQ), lambda i: (i, 0)), pl.BlockSpec((4, PAIR), lambda i: (0, 0))]
        + specs * 3 + [four, two, two],
        out_specs=[pl.BlockSpec((tm, NQ), lambda i: (i, 0)), pl.BlockSpec((4, PAIR), lambda i: (0, 0))],
        out_shape=[jax.ShapeDtypeStruct((T, NQ), BF16), jax.ShapeDtypeStruct((4, PAIR), F32)],
        scratch_shapes=[pltpu.VMEM((tm, PAIR), F32)],
        compiler_params=_params("arbitrary"),
    )(qkv, gains2, *views, dqb, dkb, dvb)


def _dense_norm_bwd(dres, dz, w, blk, x, g, tm):
    T, D = x.shape
    N = dz.shape[1]
    wspec, wload = _weight_arg(w, blk)

    def body(dres_ref, dz_ref, w_ref, x_ref, g_ref, dx_ref, dgn_ref):
        i = pl.program_id(0)
        dx, dg = _norm_bwd(_dot_nt(dz_ref[...], wload(w_ref)), x_ref[...], g_ref[...])
        dx_ref[...] = dres_ref[...] + dx

        @pl.when(i == 0)
        def _():
            dgn_ref[...] = dg

        @pl.when(i > 0)
        def _():
            dgn_ref[...] += dg

    tok = pl.BlockSpec((tm, D), lambda i: (i, 0))
    row = pl.BlockSpec((1, D), lambda i: (0, 0))
    return pl.pallas_call(
        body, name="dense_norm_bwd", grid=(T // tm,),
        in_specs=[tok, pl.BlockSpec((tm, N), lambda i: (i, 0)), wspec, tok, row],
        out_specs=[tok, row],
        out_shape=[jax.ShapeDtypeStruct((T, D), F32), jax.ShapeDtypeStruct((1, D), F32)],
        compiler_params=_params("arbitrary"),
    )(dres, dz, w, x, g)


def _bias_reduce(onehot, dbm):
    Hb, K = dbm.shape

    def body(oh_ref, d_ref, out_ref):
        oh = oh_ref[...]
        d = d_ref[...]
        hi = d.astype(BF16)
        r1 = d - hi.astype(F32)
        mid = r1.astype(BF16)
        low = (r1 - mid.astype(F32)).astype(BF16)
        out_ref[...] = _dot_nt(hi, oh) + _dot_nt(mid, oh) + _dot_nt(low, oh)

    vm = pl.BlockSpec(memory_space=pltpu.VMEM)
    return pl.pallas_call(
        body, name="bias_reduce", in_specs=[vm, vm], out_specs=vm,
        out_shape=jax.ShapeDtypeStruct((Hb, N_BUCKETS), F32),
        compiler_params=pltpu.CompilerParams(vmem_limit_bytes=VMEM_LIMIT),
    )(onehot, dbm)


def _ple_fwd(x, g, wg, blk, p, wp, target, tm):
    T, D = x.shape
    P = p.shape[1]
    with_loss = target is not None
    wspec, wload = _weight_arg(wg, blk)

    def body(*refs):
        if with_loss:
            x_ref, g_ref, wg_ref, p_ref, wp_ref, t_ref, y_ref, hn_ref, gate_ref, pp_ref, pb_ref, loss_ref = refs
        else:
            x_ref, g_ref, wg_ref, p_ref, wp_ref, y_ref, hn_ref, gate_ref, pp_ref, pb_ref = refs
        i = pl.program_id(0)
        xv = x_ref[...]
        hb = (xv * _rstd(xv) * g_ref[...]).astype(BF16)
        hn_ref[...] = hb
        gate = _sigmoid(_dot(hb, wload(wg_ref)))
        pb = p_ref[...].astype(BF16)
        pb_ref[...] = pb
        pp = _dot(pb, wp_ref[...])
        gate_ref[...] = gate
        pp_ref[...] = pp
        y = xv + gate * pp
        if with_loss:
            err = y - t_ref[...]
            y_ref[...] = err * (1.0 / D)
            part = jnp.broadcast_to(0.5 * jnp.sum(jnp.sum(err * err, axis=1, keepdims=True) * (1.0 / D),
                                                  axis=0, keepdims=True), (1, 128))

            @pl.when(i == 0)
            def _():
                loss_ref[...] = part

            @pl.when(i > 0)
            def _():
                loss_ref[...] += part
        else:
            y_ref[...] = y

    tok = pl.BlockSpec((tm, D), lambda i: (i, 0))
    ptok = pl.BlockSpec((tm, P), lambda i: (i, 0))
    in_specs = [tok, pl.BlockSpec((1, D), lambda i: (0, 0)), wspec, ptok,
                pl.BlockSpec((P, D), lambda i: (0, 0))]
    out_specs = [tok, tok, tok, tok, ptok]
    out_shape = [jax.ShapeDtypeStruct((T, D), F32), jax.ShapeDtypeStruct((T, D), BF16),
                 jax.ShapeDtypeStruct((T, D), F32), jax.ShapeDtypeStruct((T, D), F32),
                 jax.ShapeDtypeStruct((T, P), BF16)]
    args = [x, g, wg, p, wp]
    if with_loss:
        in_specs.append(tok)
        out_specs.append(pl.BlockSpec((1, 128), lambda i: (0, 0)))
        out_shape.append(jax.ShapeDtypeStruct((1, 128), F32))
        args.append(target)
    return pl.pallas_call(
        body, name="ple_fwd_loss" if with_loss else "ple_fwd", grid=(T // tm,),
        in_specs=in_specs, out_specs=out_specs, out_shape=out_shape,
        compiler_params=_params("arbitrary" if with_loss else "parallel"),
    )(*args)


def _ple_bwd(dy, gate, pp, tm, dep=None):
    T, D = dy.shape

    def body(dy_ref, gate_ref, pp_ref, dgl_ref, dpp_ref):
        d = dy_ref[...]
        gt = gate_ref[...]
        dgl_ref[...] = (d * pp_ref[...] * gt * (1.0 - gt)).astype(BF16)
        dpp_ref[...] = (d * gt).astype(BF16)

    tok = pl.BlockSpec((tm, D), lambda i: (i, 0))
    body, in_specs, args = _with_dep(body, dep, [tok, tok, tok], [dy, gate, pp])
    return pl.pallas_call(
        body, name="ple_bwd", grid=(T // tm,), in_specs=in_specs, out_specs=[tok, tok],
        out_shape=[jax.ShapeDtypeStruct((T, D), BF16), jax.ShapeDtypeStruct((T, D), BF16)],
        compiler_params=_params("parallel"),
    )(*args)


def _adamw(w, g, m, v):
    shape = w.shape
    C = shape[-1]
    w2, g2, m2, v2 = (a.reshape(-1, C) for a in (w, g, m, v))
    Rn = w2.shape[0]
    tr = Rn
    for cand in (512, 352, 256):
        if Rn % cand == 0:
            tr = cand
            break
    c1 = 1.0 - ADAM_B1 ** ADAM_STEP
    c2 = 1.0 - ADAM_B2 ** ADAM_STEP

    def body(w_ref, g_ref, m_ref, v_ref, d_ref, nm_ref, nv_ref):
        gv = g_ref[...]
        mn = ADAM_B1 * m_ref[...] + (1.0 - ADAM_B1) * gv
        vn = ADAM_B2 * v_ref[...] + (1.0 - ADAM_B2) * (gv * gv)
        d_ref[...] = -ADAM_LR * ((mn / c1) / (jnp.sqrt(vn / c2) + ADAM_EPS) + ADAM_WD * w_ref[...])
        nm_ref[...] = mn
        nv_ref[...] = vn

    spec = pl.BlockSpec((tr, C), lambda i: (i, 0))
    sh = jax.ShapeDtypeStruct((Rn, C), F32)
    d, nm, nv = pl.pallas_call(
        body, name="adamw", grid=(Rn // tr,), in_specs=[spec] * 4, out_specs=[spec] * 3, out_shape=[sh] * 3,
        compiler_params=_params("parallel"),
    )(w2, g2, m2, v2)
    return d.reshape(shape), nm.reshape(shape), nv.reshape(shape)


def _my_place():
    x, y, c = lax.axis_index("x"), lax.axis_index("y"), lax.axis_index("c")
    chips = [(1 - x, y), (x, 1 - y), (1 - x, 1 - y)]
    return x, y, c, chips


def _all_gather(arrs):
    n = len(arrs)

    def body(*refs):
        x_refs, out_refs = refs[:n], refs[n:2 * n]
        send_sems, recv_sems, local_sems = refs[2 * n:]
        x, y, c, chips = _my_place()
        me, sibling = (x, y, c), (x, y, 1 - c)

        def copy(m, k, block, to, src=None):
            rows = out_refs[m].at[4 * block[0] + 2 * block[1] + block[2]]
            return pltpu.make_async_remote_copy(
                src_ref=rows if src is None else src, dst_ref=rows,
                send_sem=send_sems.at[7 * m + k], recv_sem=recv_sems.at[7 * m + k], device_id=to, device_id_type=MESH)

        mine = [pltpu.make_async_copy(x_refs[m], out_refs[m].at[4 * x + 2 * y + c], local_sems.at[m])
                for m in range(n)]
        for cp in mine:
            cp.start()
        first = []
        for m in range(n):
            first.append(copy(m, 0, me, sibling, src=x_refs[m]))
            first += [copy(m, 1 + j, me, (*chip, c), src=x_refs[m]) for j, chip in enumerate(chips)]
        for cp in first:
            cp.start()
        passed = []
        for m in range(n):
            for j, chip in enumerate(chips):
                copy(m, 1 + j, (*chip, c), me).wait_recv()
                cp = copy(m, 4 + j, (*chip, c), sibling)
                cp.start()
                passed.append(cp)
        for m in range(n):
            copy(m, 0, sibling, me).wait_recv()
            for j, chip in enumerate(chips):
                copy(m, 4 + j, (*chip, 1 - c), me).wait_recv()
        for cp in first + passed:
            cp.wait_send()
        for cp in mine:
            cp.wait()

    hbm = pl.BlockSpec(memory_space=pl.ANY)
    return pl.pallas_call(
        body, name="all_gather", in_specs=[hbm] * n, out_specs=[hbm] * n,
        out_shape=[jax.ShapeDtypeStruct((N_DEV,) + a.shape, a.dtype) for a in arrs],
        scratch_shapes=[pltpu.SemaphoreType.DMA((7 * n,)), pltpu.SemaphoreType.DMA((7 * n,)),
                        pltpu.SemaphoreType.DMA((n,))],
    )(*arrs)


def _peer(x, y, c, k):
    return (x ^ ((k >> 2) & 1), y ^ ((k >> 1) & 1), c ^ (k & 1))


HBM_SPEC = pl.BlockSpec(memory_space=pltpu.HBM)
SEM_SPEC = pl.BlockSpec(memory_space=pltpu.SEMAPHORE)


def _exchange_refs(srcs, lands, m, k, x, y, c, scatter):
    peer = _peer(x, y, c, k)
    if scatter:
        return srcs[m].at[4 * peer[0] + 2 * peer[1] + peer[2]], lands[m].at[k - 1], peer
    return srcs[m], lands[m].at[4 * x + 2 * y + c], peer


def _exchange_start(arrs, land_shapes, scatter, name):
    n = len(arrs)

    def body(*refs):
        srcs, lands = refs[:n], refs[n:2 * n]
        send_sems, recv_sems = refs[2 * n], refs[2 * n + 1]
        token = refs[-1]
        x, y, c, _ = _my_place()
        for m in range(n):
            for k in range(1, N_DEV):
                src, dst, peer = _exchange_refs(srcs, lands, m, k, x, y, c, scatter)
                pltpu.make_async_remote_copy(
                    src_ref=src, dst_ref=dst, send_sem=send_sems.at[7 * m + k - 1],
                    recv_sem=recv_sems.at[7 * m + k - 1], device_id=peer, device_id_type=MESH).start()
        token[...] = jnp.zeros_like(token)

    zones = [lax.empty(s_, a.dtype) for s_, a in zip(land_shapes, arrs)]
    outs = pl.pallas_call(
        body, name=name,
        out_shape=(pltpu.SemaphoreType.DMA((7 * n,)), pltpu.SemaphoreType.DMA((7 * n,)),
                   *[pltpu.HBM(a.shape, a.dtype) for a in arrs], *[pltpu.HBM(z.shape, z.dtype) for z in zones],
                   jax.ShapeDtypeStruct((8, 128), F32)),
        in_specs=[HBM_SPEC] * (2 * n),
        out_specs=(SEM_SPEC, SEM_SPEC, *[HBM_SPEC] * (2 * n), pl.BlockSpec(memory_space=pltpu.VMEM)),
        input_output_aliases={m: 2 + m for m in range(2 * n)},
        compiler_params=pltpu.CompilerParams(has_side_effects=pltpu.SideEffectType.DATAFLOW_SIDE_EFFECTING),
    )(*[pltpu.with_memory_space_constraint(a, pltpu.HBM) for a in arrs],
      *[pltpu.with_memory_space_constraint(z, pltpu.HBM) for z in zones])
    return outs[0], outs[1], list(outs[2:2 + n]), list(outs[2 + n:2 + 2 * n]), outs[-1]


def _exchange_wait(send_sems, recv_sems, arrs, zones, after, scatter, name):
    n = len(arrs)
    afters = list(after) if isinstance(after, (list, tuple)) else [after]

    def body(*refs):
        srcs, lands = refs[:n], refs[n:2 * n]
        send_sems, recv_sems = refs[2 * n], refs[2 * n + 1]
        x, y, c, _ = _my_place()
        for m in range(n):
            for k in range(1, N_DEV):
                src, dst, peer = _exchange_refs(srcs, lands, m, k, x, y, c, scatter)
                cp = pltpu.make_async_remote_copy(
                    src_ref=src, dst_ref=dst, send_sem=send_sems.at[7 * m + k - 1],
                    recv_sem=recv_sems.at[7 * m + k - 1], device_id=peer, device_id_type=MESH)
                cp.wait_send()
                cp.wait_recv()

    outs = pl.pallas_call(
        body, name=name,
        out_shape=tuple(pltpu.HBM(a.shape, a.dtype) for a in list(arrs) + list(zones)),
        in_specs=[HBM_SPEC] * (2 * n) + [SEM_SPEC, SEM_SPEC] + [pl.BlockSpec(memory_space=pl.ANY)] * len(afters),
        out_specs=tuple([HBM_SPEC] * (2 * n)),
        input_output_aliases={m: m for m in range(2 * n)},
        compiler_params=pltpu.CompilerParams(has_side_effects=pltpu.SideEffectType.DATAFLOW_SIDE_EFFECTING),
    )(*arrs, *zones, send_sems, recv_sems, *afters)
    return list(outs[n:])


def _sum_parts(own, parts, tr, dep=None):
    R, W = own.shape

    def body(own_ref, parts_ref, out_ref):
        acc = own_ref[...].astype(F32)
        for k in range(N_DEV - 1):
            acc = acc + parts_ref[k].astype(F32)
        out_ref[...] = acc

    in_specs = [pl.BlockSpec((tr, W), lambda i: (i, 0)), pl.BlockSpec((N_DEV - 1, tr, W), lambda i: (0, i, 0))]
    body, in_specs, args = _with_dep(body, dep, in_specs, [own, parts])
    return pl.pallas_call(
        body, name="sum_parts", grid=(R // tr,),
        in_specs=in_specs,
        out_specs=pl.BlockSpec((tr, W), lambda i: (i, 0)),
        out_shape=jax.ShapeDtypeStruct((R, W), F32),
        compiler_params=_params("parallel"),
    )(*args)


def _all_reduce_small(v, dep=None):
    Rn, Wd = v.shape

    def body(v_ref, out_ref, gat_ref, send_sems, recv_sems):
        x, y, c, _ = _my_place()
        me = 4 * x + 2 * y + c
        gat_ref[me] = v_ref[...]
        copies = []
        for k in range(1, N_DEV):
            fx, fy, fc = (k >> 2) & 1, (k >> 1) & 1, k & 1
            peer = (x ^ fx, y ^ fy, c ^ fc)
            cp = pltpu.make_async_remote_copy(
                src_ref=v_ref, dst_ref=gat_ref.at[me], send_sem=send_sems.at[k - 1], recv_sem=recv_sems.at[k - 1],
                device_id=peer, device_id_type=MESH)
            cp.start()
            copies.append(cp)
        for cp in copies:
            cp.wait_recv()
        for cp in copies:
            cp.wait_send()
        acc = gat_ref[0]
        for k in range(1, N_DEV):
            acc = acc + gat_ref[k]
        out_ref[...] = acc

    vm = pl.BlockSpec(memory_space=pltpu.VMEM)
    body, in_specs, args = _with_dep(body, dep, [vm], [v])
    return pl.pallas_call(
        body, name="all_reduce_small", in_specs=in_specs, out_specs=vm,
        out_shape=jax.ShapeDtypeStruct((Rn, Wd), F32),
        scratch_shapes=[pltpu.VMEM((N_DEV, Rn, Wd), F32), pltpu.SemaphoreType.DMA((7,)),
                        pltpu.SemaphoreType.DMA((7,))],
    )(*args)


def _t5_bucket(rel):
    half = N_BUCKETS // 2
    max_exact = half // 2
    ret = jnp.where(rel > 0, half, 0)
    n = jnp.abs(rel)
    nf = jnp.maximum(n, 1).astype(F32)
    large = max_exact + (jnp.log(nf / max_exact) / math.log(MAX_DISTANCE / max_exact)
                         * (half - max_exact)).astype(jnp.int32)
    large = jnp.minimum(large, half - 1)
    return ret + jnp.where(n < max_exact, n, large)


def _band(R, d):
    W = BQ + 2 * R
    rel = jnp.arange(W)[None, :] - R - jnp.arange(BQ)[:, None]
    return _t5_bucket(rel * d), jnp.abs(rel) <= R


def _onehot(R, d):
    bkt, in_band = _band(R, d)
    return ((bkt.reshape(1, -1) == jnp.arange(N_BUCKETS)[:, None]) & in_band.reshape(1, -1)).astype(BF16)


def _bias_expand(table_t, onehot):
    H = table_t.shape[0]
    K = onehot.shape[1]

    def body(t_ref, oh_ref, out_ref):
        oh = oh_ref[...]
        t = t_ref[...]
        hi = t.astype(BF16)
        r1 = t - hi.astype(F32)
        mid = r1.astype(BF16)
        low = (r1 - mid.astype(F32)).astype(BF16)
        marked = _dot(jnp.ones(t.shape, BF16), oh) > 0.5
        out_ref[...] = jnp.where(marked, _dot(hi, oh) + _dot(mid, oh) + _dot(low, oh), NEG)

    vm = pl.BlockSpec(memory_space=pltpu.VMEM)
    return pl.pallas_call(
        body, name="bias_expand", in_specs=[vm, vm], out_specs=vm,
        out_shape=jax.ShapeDtypeStruct((H, K), F32),
        compiler_params=pltpu.CompilerParams(vmem_limit_bytes=VMEM_LIMIT),
    )(table_t, onehot)


def _bias_matrix(table, R, d):
    return _bias_expand(table.T, _onehot(R, d)).reshape(table.shape[1], BQ, BQ + 2 * R)


def _bias_variants(base, R):
    H, _, W = base.shape
    fill = jnp.full((H, BQ, R), NEG, F32)
    first = jnp.concatenate([base[:, :, R:], fill], axis=2)
    last = jnp.concatenate([fill, base[:, :, :W - R]], axis=2)
    v = jnp.stack([base, first, last], axis=1)
    v = v.reshape(H // 2, 2, 3, BQ, W).transpose(0, 2, 1, 3, 4).reshape(H // 2, 3, 2 * BQ, W)
    return v, v.transpose(0, 1, 3, 2)


def _bias_grad(dbt, R, d):
    P, _, W, _ = dbt.shape
    dbt = dbt[:, 0].at[:, R:].add(dbt[:, 1, :W - R]).at[:, :W - R].add(dbt[:, 2, R:])
    dbm = dbt.reshape(P, W, 2, BQ).transpose(0, 2, 3, 1).reshape(2 * P, BQ * W)
    return _bias_reduce(_onehot(R, d), dbm).T


def _tile2(gain):
    return jnp.concatenate([gain, gain])


ROW_W_O, ROW_GATE, B_ROWS = 768, 896, 1024
BLK_W_O, BLK_GATE = ROW_W_O // 128, ROW_GATE // 128


def _pack_layer(wts, i):
    a = jnp.stack([wts["ffn1_w_in"][i], wts["ffn2_w_in"][i]])
    D = a.shape[1]
    b = jnp.concatenate([
        wts["ffn1_w_out"][i], wts["ffn2_w_out"][i],
        jnp.zeros((ROW_W_O - 2 * wts["ffn1_w_out"].shape[1], D), a.dtype), wts["w_o"][i], wts["w_ple_gate"][i]])
    return a, b, wts["w_qkv"][i], wts["w_ple_proj"][i]


def _unpack_layer(sums, like):
    w_in2, b1, proj, w_o, qkv, w_in1, w_out1 = sums
    n_out = like["ffn1_w_out"].shape[1]
    out = {}
    if w_in2 is not None:
        out.update(ffn2_w_in=w_in2, ffn2_w_out=b1[:n_out], w_ple_gate=b1[n_out:], w_ple_proj=proj)
    if w_o is not None:
        out.update(w_o=w_o, w_qkv=qkv)
    if w_in1 is not None:
        out.update(ffn1_w_in=w_in1, ffn1_w_out=w_out1)
    return out


def _col_sharded(g):
    return g.transpose(1, 0, 2).reshape(g.shape[1], -1)


def _to_col_shards(g):
    rows = g.shape[0]
    return g.reshape(rows, N_DEV, -1).transpose(1, 0, 2)


def _layer_weights(ga, gb, gq, gp):
    return dict(ga=ga, gb=gb, w_qkv=_col_sharded(gq), w_proj=_col_sharded(gp))


def _layer_fwd(x, p, w, sm, i, target, tm, biases, dep=None):
    ga, gb = w["ga"], w["gb"]
    saved = {}
    saved["x0"] = x
    x1, saved["h1"], saved["zg1"], saved["zu1"], saved["s1"] = _ffn_fwd(
        x, sm["norm_ffn1"][i][None], ga, gb, 0, 2 * tm, dep)
    saved["x1"] = x1
    qkv, saved["hm"] = _qkv_fwd(x1, sm["norm_mix"][i][None], w["w_qkv"], 2 * tm)
    saved["qkv"] = qkv
    gains2 = jnp.stack([_tile2(sm[k][i]) for k in ("q_norm_a", "k_norm_a", "q_norm_b", "k_norm_b")])
    saved["gains2"] = gains2
    qb, kb, vb, qkv_d = _attn_prep(qkv, gains2, tm)
    no_sink = jnp.full((8,), NEG, F32)
    branches = []
    outs = []
    for (R, d), bias, (qd, kd, vd) in zip(DILATED, biases[:3], qkv_d):
        sink = jnp.tile(no_sink, d)
        outs.append(_attn_fwd(qd, kd, vd, bias[0], sink, R, 1, d))
        branches.append((qd, kd, vd, bias, sink, R, d))
    bias_b = biases[3]
    sink_b = sm["sink_b"][i]
    ob, lb = _attn_fwd(qb, kb, vb, bias_b[0], sink_b, SWA_RADIUS, 2, 1)
    merged, o_cat = _attn_merge(outs, ob, tm)
    saved.update(branches=branches, b=(qb, kb, vb, bias_b, sink_b), merged=merged, ob=ob, lb=lb, o_cat=o_cat)
    x2 = _oproj_fwd(x1, o_cat, gb, BLK_W_O, 2 * tm)
    saved["x2"] = x2
    x3, saved["h2"], saved["zg2"], saved["zu2"], saved["s2"] = _ffn_fwd(
        x2, sm["norm_ffn2"][i][None], ga, gb, 1, 2 * tm)
    saved["x3"] = x3
    res = _ple_fwd(x3, sm["norm_ple"][i][None], gb, BLK_GATE, p, w["w_proj"], target, tm)
    y, saved["hp"], saved["gate"], saved["pp"], saved["pb"] = res[:5]
    loss = res[5] if target is not None else None
    return y, loss, saved


def _layer_bwd(dy, w, sm, i, sv, tm, dep=None, on_ready=None, on_small=None, on_last=None):
    ga, gb = w["ga"], w["gb"]
    gs = {}
    D = dy.shape[1]
    dgl, dpp = _ple_bwd(dy, sv["gate"], sv["pp"], tm, dep)
    d_gate = _matmul_tn(sv["hp"], dgl, D, 2 * tm)
    d_proj = _matmul_tn(sv["pb"], dpp, D, 2 * tm)
    dx3, gs["norm_ple"] = _dense_norm_bwd(dy, dgl, gb, BLK_GATE, sv["x3"], sm["norm_ple"][i][None], 2 * tm)
    dx2, dyb, dzg, dzu, gs["norm_ffn2"] = _ffn_bwd(dx3, sv["x2"], sm["norm_ffn2"][i][None], sv["zg2"], sv["zu2"],
                                                   ga, gb, 1, tm)
    dwin2, dwo2 = _ffn_dw(sv["h2"], dzg, dzu, sv["s2"], dyb, 2 * tm)
    half = dwo2.shape[1] // 2
    after_ffn2 = [dwin2, jnp.concatenate([dwo2.reshape(N_DEV, half, D), d_gate.reshape(N_DEV, -1, D)], axis=1),
                  _to_col_shards(d_proj)]
    token = None if on_ready is None else on_ready(0, after_ffn2)
    dx2b, do_b, do_a = _oproj_bwd(dx2, gb, BLK_W_O, tm, token)
    d_wo = _matmul_tn(sv["o_cat"], dx2b, D, 2 * tm)
    dqa, dka, dva, dbias = [], [], [], []
    for (qd, kd, vd, bias, sink, R, d), (oa, la), do_d in zip(sv["branches"], sv["merged"], do_a):
        dq, dk, dv, dbm, _ = _attn_bwd(qd, kd, vd, bias[1], sink, oa, la, do_d, R, 1, d)
        dqa.append(dq)
        dka.append(dk)
        dva.append(dv)
        dbias.append(dbm)
    qb, kb, vb, bias_b, sink_b = sv["b"]
    dqb, dkb, dvb, dbm_b, dsink = _attn_bwd(qb, kb, vb, bias_b[1], sink_b, sv["ob"], sv["lb"], do_b,
                                            SWA_RADIUS, 2, 1)
    gs["rel_bias"] = dbias + [dbm_b]
    gs["sink_b"] = jnp.sum(dsink[:, 0].reshape(-1, 2, BQ), axis=2).reshape(-1)
    dqkv, dgains2 = _attn_post(sv["qkv"], sv["gains2"], dqa, dka, dva, dqb,
                               dkb, dvb, tm)
    dgains = dgains2[:, :HEAD_DIM] + dgains2[:, HEAD_DIM:]
    for k, name in enumerate(("q_norm_a", "k_norm_a", "q_norm_b", "k_norm_b")):
        gs[name] = dgains[k]
    d_qkv = _matmul_tn(sv["hm"], dqkv, dqkv.shape[1] // 2, 2 * tm)
    after_mixer = [d_wo.reshape(N_DEV, -1, D), _to_col_shards(d_qkv)]
    token = None if on_ready is None else on_ready(1, after_mixer)
    dx1, gs["norm_mix"] = _dense_norm_bwd(dx2, dqkv, w["w_qkv"], None, sv["x1"], sm["norm_mix"][i][None], 2 * tm)
    g1 = sm["norm_ffn1"][i][None]
    if on_last is None:
        dx0, dyb, dzg, dzu, gs["norm_ffn1"] = _ffn_bwd(dx1, sv["x0"], g1, sv["zg1"], sv["zu1"], ga, gb, 0, tm, token)
        dwin1, dwo1 = _ffn_dw(sv["h1"], dzg, dzu, sv["s1"], dyb, 2 * tm)
        return dx0, (after_ffn2, after_mixer, [dwin1, dwo1.reshape(N_DEV, half, D)]), gs
    dyb, dzg, dzu = _ffn_bwd_dz(dx1, sv["zg1"], sv["zu1"], gb, 0, 2 * tm, token)
    dwin1, dwo1 = _ffn_dw(sv["h1"], dzg, dzu, sv["s1"], dyb, 2 * tm, on_small(gs))
    last = [dwin1, dwo1.reshape(N_DEV, half, D)]
    dx0, gs["norm_ffn1"] = _ffn_bwd_dx(dx1, sv["x0"], g1, dzg, dzu, ga, 0, 2 * tm, on_last(last))
    return dx0, (after_ffn2, after_mixer, last), gs


def _bias_matrices(rel_bias):
    biases = [_bias_variants(_bias_matrix(rel_bias[:, :8], R, d), R) for R, d in DILATED]
    biases.append(_bias_variants(_bias_matrix(rel_bias[:, 8:], SWA_RADIUS, 1), SWA_RADIUS))
    return biases


def _stack_small(per_layer):
    small = {}
    for k, v in per_layer.items():
        if k == "rel_bias":
            per_branch = [sum(parts) for parts in zip(*v.values())]
            drel_a = sum(_bias_grad(t, R, d) for t, (R, d) in zip(per_branch[:3], DILATED))
            small[k] = jnp.concatenate([drel_a, _bias_grad(per_branch[3], SWA_RADIUS, 1)], axis=1)
        else:
            small[k] = jnp.stack([v[i].reshape(-1) for i in sorted(v)])
    return small


TM = 512
SUM_TILES = (512, 480, 256, 128, 512, 512, 352)
LAST_GROUP = ("ffn1_w_in", "ffn1_w_out")


def _pack_small(d, extra=None):
    parts = [d[k].reshape(-1) for k in SMALL]
    if extra is not None:
        parts.append(extra.reshape(-1))
    flat = jnp.concatenate(parts)
    return jnp.pad(flat, (0, SMALL_ROWS * 128 - flat.shape[0])).reshape(SMALL_ROWS, 128)


def _unpack_small(buf, like):
    flat = buf.reshape(-1)
    out, off = {}, 0
    for k in SMALL:
        n = like[k].size
        out[k] = flat[off:off + n].reshape(like[k].shape)
        off += n
    return out, flat[off]


def kernel(x, p, rel_bias, norm_ffn1, ffn1_w_in, ffn1_w_out, norm_mix, w_qkv, q_norm_a, k_norm_a, q_norm_b, k_norm_b, sink_b, w_o, norm_ffn2, ffn2_w_in, ffn2_w_out, norm_ple, w_ple_gate, w_ple_proj, loss_target, m_rel_bias, m_norm_ffn1, m_ffn1_w_in, m_ffn1_w_out, m_norm_mix, m_w_qkv, m_q_norm_a, m_k_norm_a, m_q_norm_b, m_k_norm_b, m_sink_b, m_w_o, m_norm_ffn2, m_ffn2_w_in, m_ffn2_w_out, m_norm_ple, m_w_ple_gate, m_w_ple_proj, v_rel_bias, v_norm_ffn1, v_ffn1_w_in, v_ffn1_w_out, v_norm_mix, v_w_qkv, v_q_norm_a, v_k_norm_a, v_q_norm_b, v_k_norm_b, v_sink_b, v_w_o, v_norm_ffn2, v_ffn2_w_in, v_ffn2_w_out, v_norm_ple, v_w_ple_gate, v_w_ple_proj):
    wts = dict(rel_bias=rel_bias, norm_ffn1=norm_ffn1, ffn1_w_in=ffn1_w_in, ffn1_w_out=ffn1_w_out,
               norm_mix=norm_mix, w_qkv=w_qkv, q_norm_a=q_norm_a, k_norm_a=k_norm_a, q_norm_b=q_norm_b,
               k_norm_b=k_norm_b, sink_b=sink_b, w_o=w_o, norm_ffn2=norm_ffn2, ffn2_w_in=ffn2_w_in,
               ffn2_w_out=ffn2_w_out, norm_ple=norm_ple, w_ple_gate=w_ple_gate, w_ple_proj=w_ple_proj)
    mom = dict(rel_bias=m_rel_bias, norm_ffn1=m_norm_ffn1, ffn1_w_in=m_ffn1_w_in, ffn1_w_out=m_ffn1_w_out,
               norm_mix=m_norm_mix, w_qkv=m_w_qkv, q_norm_a=m_q_norm_a, k_norm_a=m_k_norm_a, q_norm_b=m_q_norm_b,
               k_norm_b=m_k_norm_b, sink_b=m_sink_b, w_o=m_w_o, norm_ffn2=m_norm_ffn2, ffn2_w_in=m_ffn2_w_in,
               ffn2_w_out=m_ffn2_w_out, norm_ple=m_norm_ple, w_ple_gate=m_w_ple_gate, w_ple_proj=m_w_ple_proj)
    var = dict(rel_bias=v_rel_bias, norm_ffn1=v_norm_ffn1, ffn1_w_in=v_ffn1_w_in, ffn1_w_out=v_ffn1_w_out,
               norm_mix=v_norm_mix, w_qkv=v_w_qkv, q_norm_a=v_q_norm_a, k_norm_a=v_k_norm_a, q_norm_b=v_q_norm_b,
               k_norm_b=v_k_norm_b, sink_b=v_sink_b, w_o=v_w_o, norm_ffn2=v_norm_ffn2, ffn2_w_in=v_ffn2_w_in,
               ffn2_w_out=v_ffn2_w_out, norm_ple=v_norm_ple, w_ple_gate=v_w_ple_gate, w_ple_proj=v_w_ple_proj)
    sm = {k: wts[k] for k in SMALL}
    me = 4 * lax.axis_index("x") + 2 * lax.axis_index("y") + lax.axis_index("c")
    packed = []
    for i in range(2):
        a, *rest = _pack_layer(wts, i)
        packed.append([t.astype(BF16) for t in [a.reshape(-1, a.shape[-1])] + rest])
    a_shape = (2, ffn1_w_in.shape[1], ffn1_w_in.shape[2])

    def weights_of(zones):
        return _layer_weights(zones[0].reshape((N_DEV,) + a_shape), *zones[1:])

    w0 = weights_of(_all_gather(packed[0]))
    zone_shapes = [(N_DEV,) + t.shape for t in packed[1]]
    ssem, rsem, thru, zones, token = _exchange_start(packed[1], zone_shapes, False, "gather_start")
    biases = _bias_matrices(rel_bias)
    x1, _, sv0 = _layer_fwd(x[0], p[0, 0], w0, sm, 0, None, TM, biases, dep=token)
    zones = _exchange_wait(ssem, rsem, thru, zones, x1, False, "gather_wait")
    w1 = weights_of([lax.dynamic_update_index_in_dim(z, t, me, 0) for z, t in zip(zones, packed[1])])
    dy, loss, sv1 = _layer_fwd(x1, p[1, 0], w1, sm, 1, loss_target[0], TM, biases)

    def slots_for(arrs):
        return [(N_DEV - 1,) + t.shape[1:] for t in arrs]

    held1, held = {}, {}

    def on_ready1(stage, group):
        held1[stage] = _exchange_start(group, slots_for(group), True, f"scatter1_start_{stage}")
        return held1[stage][4]

    dx1, groups1, gs1 = _layer_bwd(dy, w1, sm, 1, sv1, TM, on_ready=on_ready1)
    on_ready1(2, groups1[2])
    g1 = groups1[0] + groups1[1] + groups1[2]

    def on_ready(stage, group):
        if stage == 1:
            held["slots1"] = [t for st in (0, 1, 2)
                              for t in _exchange_wait(*held1[st][:4], group[0], True, f"scatter1_wait_{st}")]
        held[stage] = _exchange_start(group, slots_for(group), True, f"scatter_start_{stage}")
        return held[stage][4]

    def on_small(gs0):
        part = dict(gs0, norm_ffn1=jnp.zeros_like(gs1["norm_ffn1"]))
        gsmall = _stack_small({k: {0: part[k], 1: gs1[k]} for k in part})
        held["small"] = _all_reduce_small(_pack_small(gsmall, loss[0, :1]))
        return held["small"]

    def on_last(group):
        held["last"] = _exchange_start(group, slots_for(group), True, "scatter_start_2")
        return held["last"][4]

    dx, groups0, gs0 = _layer_bwd(dx1, w0, sm, 0, sv0, TM, dep=held1[2][4], on_ready=on_ready, on_small=on_small,
                                  on_last=on_last)
    last = groups0[2]
    slots0 = [_exchange_wait(*held[stage][:4], last[0], True, f"scatter_wait_{stage}") for stage in (0, 1)]

    def summed(arrs, slots, tiles, dep=None):
        return [_sum_parts(lax.dynamic_index_in_dim(t, me, 0, keepdims=False), s_, tr, dep)
                for t, s_, tr in zip(arrs, slots, tiles)]

    cover = held["last"][4]
    r1 = summed(g1, held["slots1"], SUM_TILES, cover)
    r0 = summed(groups0[0], slots0[0], SUM_TILES[:3], cover) + summed(groups0[1], slots0[1], SUM_TILES[3:5], cover)

    def update(names, layers):
        for k in names:
            grads[k] = jnp.stack([layers[0][k], layers[1][k]])
            delta[k], new_m[k], new_v[k] = _adamw(wts[k], grads[k], mom[k], var[k])

    grads, delta, new_m, new_v = {}, {}, {}, {}
    layer1 = _unpack_layer(r1, wts)
    update([k for k in BIG if k not in LAST_GROUP], [_unpack_layer(r0 + [None, None], wts), layer1])

    cover_done = [dx] + [delta[k] for k in BIG if k not in LAST_GROUP]
    slots_last = _exchange_wait(*held["last"][:4], cover_done, True, "scatter_wait_2")
    update(LAST_GROUP, [_unpack_layer([None] * 5 + summed(last, slots_last, SUM_TILES[5:]), wts), layer1])
    late = _all_reduce_small(gs0["norm_ffn1"].reshape(-1, 128), dep=slots_last[0])
    small_sum, loss_sum = _unpack_small(held["small"], sm)
    small_sum["norm_ffn1"] = small_sum["norm_ffn1"].at[0].add(late.reshape(-1))
    grads.update(small_sum)
    zeros = {k: jnp.zeros_like(wts[k]) for k in SMALL}
    ds, ms, vs = _adamw(_pack_small(wts), _pack_small(small_sum), _pack_small(mom), _pack_small(var))
    for packed, dst in ((ds, delta), (ms, new_m), (vs, new_v)):
        dst.update(_unpack_small(packed, zeros)[0])

    return (loss_sum, dx[None], *[grads[k] for k in WEIGHTS], *[delta[k] for k in WEIGHTS],
            *[new_m[k] for k in WEIGHTS], *[new_v[k] for k in WEIGHTS])
```

```python
import functools
import math

import jax
import jax.numpy as jnp
from jax import lax
from jax.experimental import pallas as pl
from jax.experimental.pallas import tpu as pltpu

F32 = jnp.float32
BF16 = jnp.bfloat16

N_DEV = 8
HEAD_DIM = 64
PAIR = 2 * HEAD_DIM
BQ = 128
N_BUCKETS = 32
MAX_DISTANCE = 1024
DILATED = ((64, 1), (64, 4), (64, 16))
SWA_RADIUS = 128
EPS = 1e-6
NEG = -1e30
ADAM_LR, ADAM_B1, ADAM_B2, ADAM_EPS, ADAM_WD, ADAM_STEP = 0.001, 0.9, 0.999, 1e-08, 0.01, 10
VMEM_LIMIT = 56 * 1024 * 1024
MESH = pl.DeviceIdType.MESH

BIG = ("ffn1_w_in", "ffn1_w_out", "w_qkv", "w_o", "ffn2_w_in", "ffn2_w_out", "w_ple_gate", "w_ple_proj")
SMALL = ("rel_bias", "norm_ffn1", "norm_mix", "q_norm_a", "k_norm_a", "q_norm_b", "k_norm_b", "sink_b",
         "norm_ffn2", "norm_ple")
WEIGHTS = ("rel_bias", "norm_ffn1", "ffn1_w_in", "ffn1_w_out", "norm_mix", "w_qkv", "q_norm_a", "k_norm_a",
           "q_norm_b", "k_norm_b", "sink_b", "w_o", "norm_ffn2", "ffn2_w_in", "ffn2_w_out", "norm_ple",
           "w_ple_gate", "w_ple_proj")
SMALL_ROWS = 96


def _params(*sem):
    return pltpu.CompilerParams(dimension_semantics=sem, vmem_limit_bytes=VMEM_LIMIT)


def _dot(a, b):
    return jnp.dot(a, b, preferred_element_type=F32)


def _dot_nt(a, b):
    return lax.dot_general(a, b, (((1,), (1,)), ((), ())), preferred_element_type=F32)


def _dot_tn(a, b):
    return lax.dot_general(a, b, (((0,), (0,)), ((), ())), preferred_element_type=F32)


def _sigmoid(x):
    return 1.0 / (1.0 + jnp.exp(-x))


def _rstd(xv):
    return lax.rsqrt(jnp.mean(xv * xv, axis=-1, keepdims=True) + EPS)


def _norm_bwd(dh, xv, gv):
    r = _rstd(xv)
    xn = xv * r
    dg = jnp.sum(dh * xn, axis=0, keepdims=True)
    dxn = dh * gv
    dx = r * (dxn - xn * jnp.mean(dxn * xn, axis=-1, keepdims=True))
    return dx, dg


def _lo_mask(shape):
    return lax.broadcasted_iota(jnp.int32, shape, len(shape) - 1) < HEAD_DIM


def _half_sum(t, lo):
    s0 = jnp.sum(jnp.where(lo, t, 0.0), axis=1, keepdims=True)
    s1 = jnp.sum(jnp.where(lo, 0.0, t), axis=1, keepdims=True)
    return jnp.where(lo, s0, s1)


FFN_PARTS = 2


def _ffn_weight_specs(f, nj, D, C):
    return [pl.BlockSpec((None, None, D, C), lambda i, j: (j, f, 0, 0)),
            pl.BlockSpec((None, None, D, C), lambda i, j: (j + nj, f, 0, 0)),
            pl.BlockSpec((2, C // 2, D), lambda i, j: (j, f, 0))]


def _with_dep(body, dep, in_specs, args):
    if dep is None:
        return body, in_specs, args

    def body_after(dep_ref, *refs):
        body(*refs)

    return body_after, [pl.BlockSpec(memory_space=pl.ANY)] + in_specs, [dep] + args


def _ffn_fwd(x, g, ga, gb, f, tm, dep=None):
    T, D = x.shape
    nj, C = ga.shape[0] // 2, ga.shape[3]

    def body(x_ref, g_ref, wg_ref, wu_ref, wo_ref, xo_ref, h_ref, zg_ref, zu_ref, s_ref, h_scr, acc):
        j = pl.program_id(1)

        @pl.when(j == 0)
        def _():
            xv = x_ref[...]
            hb = (xv * _rstd(xv) * g_ref[...]).astype(BF16)
            h_scr[...] = hb
            h_ref[...] = hb
            acc[...] = jnp.zeros_like(acc)

        wo = wo_ref[...].reshape(C, D)
        for part in range(FFN_PARTS):
            sl = pl.ds(part * (tm // FFN_PARTS), tm // FFN_PARTS)
            hb = h_scr[sl, :]
            gt = _dot(hb, wg_ref[...])
            up = _dot(hb, wu_ref[...])
            s = (gt * _sigmoid(gt) * up).astype(BF16)
            zg_ref[sl, :] = gt.astype(BF16)
            zu_ref[sl, :] = up.astype(BF16)
            s_ref[sl, :] = s
            acc[sl, :] += _dot(s, wo)

        @pl.when(j == nj - 1)
        def _():
            xo_ref[...] = x_ref[...] + 0.5 * acc[...]

    tok = pl.BlockSpec((tm, D), lambda i, j: (i, 0))
    chunk = pl.BlockSpec((None, tm, C), lambda i, j: (j, i, 0))
    in_specs = [tok, pl.BlockSpec((1, D), lambda i, j: (0, 0))] + _ffn_weight_specs(f, nj, D, C)
    body, in_specs, args = _with_dep(body, dep, in_specs, [x, g, ga, ga, gb])
    return pl.pallas_call(
        body, name="ffn_fwd", grid=(T // tm, nj),
        in_specs=in_specs,
        out_specs=[tok, tok, chunk, chunk, chunk],
        out_shape=[jax.ShapeDtypeStruct((T, D), F32), jax.ShapeDtypeStruct((T, D), BF16),
                   jax.ShapeDtypeStruct((nj, T, C), BF16), jax.ShapeDtypeStruct((nj, T, C), BF16),
                   jax.ShapeDtypeStruct((nj, T, C), BF16)],
        scratch_shapes=[pltpu.VMEM((tm, D), BF16), pltpu.VMEM((tm, D), F32)],
        compiler_params=_params("parallel", "arbitrary"),
    )(*args)


def _ffn_bwd(dxo, x, g, zg, zu, ga, gb, f, tm, dep=None):
    T, D = x.shape
    nj, C = ga.shape[0] // 2, ga.shape[3]

    def body(dxo_ref, x_ref, g_ref, zg_ref, zu_ref, wg_ref, wu_ref, wo_ref,
             dx_ref, dy_ref, dzg_ref, dzu_ref, dgn_ref, dy_scr, acc):
        i, j = pl.program_id(0), pl.program_id(1)

        @pl.when(j == 0)
        def _():
            dyb = (0.5 * dxo_ref[...]).astype(BF16)
            dy_scr[...] = dyb
            dy_ref[...] = dyb
            acc[...] = jnp.zeros_like(acc)

        wo = wo_ref[...].reshape(C, D)
        for part in range(FFN_PARTS):
            sl = pl.ds(part * (tm // FFN_PARTS), tm // FFN_PARTS)
            ds = _dot_nt(dy_scr[sl, :], wo)
            gt = zg_ref[sl, :].astype(F32)
            up = zu_ref[sl, :].astype(F32)
            sg = _sigmoid(gt)
            dgt = (ds * up * (sg * (1.0 + gt * (1.0 - sg)))).astype(BF16)
            dup = (ds * (gt * sg)).astype(BF16)
            dzg_ref[sl, :] = dgt
            dzu_ref[sl, :] = dup
            acc[sl, :] += _dot_nt(dgt, wg_ref[...]) + _dot_nt(dup, wu_ref[...])

        @pl.when(j == nj - 1)
        def _():
            dx, dg = _norm_bwd(acc[...], x_ref[...], g_ref[...])
            dx_ref[...] = dxo_ref[...] + dx

            @pl.when(i == 0)
            def _():
                dgn_ref[...] = dg

            @pl.when(i > 0)
            def _():
                dgn_ref[...] += dg

    tok = pl.BlockSpec((tm, D), lambda i, j: (i, 0))
    chunk = pl.BlockSpec((None, tm, C), lambda i, j: (j, i, 0))
    row = pl.BlockSpec((1, D), lambda i, j: (0, 0))
    in_specs = [tok, tok, row, chunk, chunk] + _ffn_weight_specs(f, nj, D, C)
    body, in_specs, args = _with_dep(body, dep, in_specs, [dxo, x, g, zg, zu, ga, ga, gb])
    return pl.pallas_call(
        body, name="ffn_bwd", grid=(T // tm, nj),
        in_specs=in_specs,
        out_specs=[tok, tok, chunk, chunk, row],
        out_shape=[jax.ShapeDtypeStruct((T, D), F32), jax.ShapeDtypeStruct((T, D), BF16),
                   jax.ShapeDtypeStruct((nj, T, C), BF16), jax.ShapeDtypeStruct((nj, T, C), BF16),
                   jax.ShapeDtypeStruct((1, D), F32)],
        scratch_shapes=[pltpu.VMEM((tm, D), BF16), pltpu.VMEM((tm, D), F32)],
        compiler_params=_params("arbitrary", "arbitrary"),
    )(*args)


def _ffn_bwd_dz(dxo, zg, zu, gb, f, tm, dep=None):
    T, D = dxo.shape
    nj, C = zg.shape[0], zg.shape[2]

    def body(dxo_ref, zg_ref, zu_ref, wo_ref, dy_ref, dzg_ref, dzu_ref, dy_scr):
        @pl.when(pl.program_id(1) == 0)
        def _():
            dyb = (0.5 * dxo_ref[...]).astype(BF16)
            dy_scr[...] = dyb
            dy_ref[...] = dyb

        wo = wo_ref[...].reshape(C, D)
        for part in range(FFN_PARTS):
            sl = pl.ds(part * (tm // FFN_PARTS), tm // FFN_PARTS)
            ds = _dot_nt(dy_scr[sl, :], wo)
            gt = zg_ref[sl, :].astype(F32)
            up = zu_ref[sl, :].astype(F32)
            sg = _sigmoid(gt)
            dzg_ref[sl, :] = (ds * up * (sg * (1.0 + gt * (1.0 - sg)))).astype(BF16)
            dzu_ref[sl, :] = (ds * (gt * sg)).astype(BF16)

    tok = pl.BlockSpec((tm, D), lambda i, j: (i, 0))
    chunk = pl.BlockSpec((None, tm, C), lambda i, j: (j, i, 0))
    in_specs = [tok, chunk, chunk, _ffn_weight_specs(f, nj, D, C)[2]]
    body, in_specs, args = _with_dep(body, dep, in_specs, [dxo, zg, zu, gb])
    return pl.pallas_call(
        body, name="ffn_bwd_dz", grid=(T // tm, nj),
        in_specs=in_specs, out_specs=[tok, chunk, chunk],
        out_shape=[jax.ShapeDtypeStruct((T, D), BF16), jax.ShapeDtypeStruct((nj, T, C), BF16),
                   jax.ShapeDtypeStruct((nj, T, C), BF16)],
        scratch_shapes=[pltpu.VMEM((tm, D), BF16)],
        compiler_params=_params("parallel", "arbitrary"),
    )(*args)


def _ffn_bwd_dx(dxo, x, g, dzg, dzu, ga, f, tm, dep=None):
    T, D = x.shape
    nj, C = ga.shape[0] // 2, ga.shape[3]

    def body(dxo_ref, x_ref, g_ref, dzg_ref, dzu_ref, wg_ref, wu_ref, dx_ref, dgn_ref, acc):
        i, j = pl.program_id(0), pl.program_id(1)

        @pl.when(j == 0)
        def _():
            acc[...] = jnp.zeros_like(acc)

        acc[...] += _dot_nt(dzg_ref[...], wg_ref[...]) + _dot_nt(dzu_ref[...], wu_ref[...])

        @pl.when(j == nj - 1)
        def _():
            dx, dg = _norm_bwd(acc[...], x_ref[...], g_ref[...])
            dx_ref[...] = dxo_ref[...] + dx

            @pl.when(i == 0)
            def _():
                dgn_ref[...] = dg

            @pl.when(i > 0)
            def _():
                dgn_ref[...] += dg

    tok = pl.BlockSpec((tm, D), lambda i, j: (i, 0))
    chunk = pl.BlockSpec((None, tm, C), lambda i, j: (j, i, 0))
    row = pl.BlockSpec((1, D), lambda i, j: (0, 0))
    in_specs = [tok, tok, row, chunk, chunk] + _ffn_weight_specs(f, nj, D, C)[:2]
    body, in_specs, args = _with_dep(body, dep, in_specs, [dxo, x, g, dzg, dzu, ga, ga])
    return pl.pallas_call(
        body, name="ffn_bwd_dx", grid=(T // tm, nj),
        in_specs=in_specs, out_specs=[tok, row],
        out_shape=[jax.ShapeDtypeStruct((T, D), F32), jax.ShapeDtypeStruct((1, D), F32)],
        scratch_shapes=[pltpu.VMEM((tm, D), F32)],
        compiler_params=_params("arbitrary", "arbitrary"),
    )(*args)


def _ffn_dw(h, dzg, dzu, s, dy, tk, dep=None):
    T, D = h.shape
    nj, C = s.shape[0], s.shape[2]
    nk = T // tk

    def body(h_ref, dzg_ref, dzu_ref, s_ref, dy_ref, dwin_ref, dwo_ref, ag, au, ao):
        k = pl.program_id(1)

        @pl.when(k == 0)
        def _():
            ag[...] = jnp.zeros_like(ag)
            au[...] = jnp.zeros_like(au)
            ao[...] = jnp.zeros_like(ao)

        hb = h_ref[...]
        ag[...] += _dot_tn(hb, dzg_ref[...])
        au[...] += _dot_tn(hb, dzu_ref[...])
        ao[...] += _dot_tn(s_ref[...], dy_ref[...])

        @pl.when(k == nk - 1)
        def _():
            dwin_ref[0] = ag[...].astype(BF16)
            dwin_ref[1] = au[...].astype(BF16)
            dwo_ref[...] = ao[...].astype(BF16)

    tok = pl.BlockSpec((tk, D), lambda j, k: (k, 0))
    chunk = pl.BlockSpec((None, tk, C), lambda j, k: (j, k, 0))
    body, in_specs, args = _with_dep(body, dep, [tok, chunk, chunk, chunk, tok], [h, dzg, dzu, s, dy])
    dwin, dwo = pl.pallas_call(
        body, name="ffn_dw", grid=(nj, nk),
        in_specs=in_specs,
        out_specs=[pl.BlockSpec((2, None, D, C), lambda j, k: (0, j, 0, 0)),
                   pl.BlockSpec((None, C, D), lambda j, k: (j, 0, 0))],
        out_shape=[jax.ShapeDtypeStruct((2, nj, D, C), BF16), jax.ShapeDtypeStruct((nj, C, D), BF16)],
        scratch_shapes=[pltpu.VMEM((D, C), F32), pltpu.VMEM((D, C), F32), pltpu.VMEM((C, D), F32)],
        compiler_params=_params("parallel", "arbitrary"),
    )(*args)
    return dwin.reshape(2 * nj, D, C), dwo


def _matmul_tn(a, b, tn, tk):
    T, Ka = a.shape
    N = b.shape[1]
    nk = T // tk

    def body(a_ref, b_ref, o_ref, acc):
        k = pl.program_id(1)

        @pl.when(k == 0)
        def _():
            acc[...] = jnp.zeros_like(acc)

        acc[...] += _dot_tn(a_ref[...], b_ref[...])

        @pl.when(k == nk - 1)
        def _():
            o_ref[...] = acc[...].astype(BF16)

    return pl.pallas_call(
        body, name="matmul_tn", grid=(N // tn, nk),
        in_specs=[pl.BlockSpec((tk, Ka), lambda n, k: (k, 0)), pl.BlockSpec((tk, tn), lambda n, k: (k, n))],
        out_specs=pl.BlockSpec((Ka, tn), lambda n, k: (0, n)),
        out_shape=jax.ShapeDtypeStruct((Ka, N), BF16),
        scratch_shapes=[pltpu.VMEM((Ka, tn), F32)],
        compiler_params=_params("parallel", "arbitrary"),
    )(a, b)


def _qkv_fwd(x, g, w, tm):
    T, D = x.shape
    N = w.shape[1]

    def body(x_ref, g_ref, w_ref, o_ref, h_ref):
        xv = x_ref[...]
        hb = (xv * _rstd(xv) * g_ref[...]).astype(BF16)
        h_ref[...] = hb
        o_ref[...] = _dot(hb, w_ref[...])

    return pl.pallas_call(
        body, name="qkv_fwd", grid=(T // tm,),
        in_specs=[pl.BlockSpec((tm, D), lambda i: (i, 0)), pl.BlockSpec((1, D), lambda i: (0, 0)),
                  pl.BlockSpec((D, N), lambda i: (0, 0))],
        out_specs=[pl.BlockSpec((tm, N), lambda i: (i, 0)), pl.BlockSpec((tm, D), lambda i: (i, 0))],
        out_shape=[jax.ShapeDtypeStruct((T, N), F32), jax.ShapeDtypeStruct((T, D), BF16)],
        compiler_params=_params("parallel"),
    )(x, g, w)


DILS = tuple(d for _, d in DILATED)


def _spread_specs(tm, T, dtype):
    specs = [pl.BlockSpec((4, d, tm // d, PAIR), lambda i: (0, 0, i, 0)) for d in DILS]
    shapes = [jax.ShapeDtypeStruct((4, d, T // d, PAIR), dtype) for d in DILS]
    return specs, shapes


def _spread(tile, y, outs, c, dtype):
    tm = y.shape[0]
    tile[...] = y
    for out, d in zip(outs, DILS):
        for r in range(d):
            out[c, r] = tile[pl.ds(r, tm // d, stride=d), :].astype(dtype)


def _collect(tile, ins, c):
    tm = tile.shape[0]
    first = True
    for ref, d in zip(ins, DILS):
        for r in range(d):
            rows = pl.ds(r, tm // d, stride=d) if d > 1 else pl.ds(0, tm)
            part = ref[c, r].astype(F32)
            tile[rows, :] = part if first else tile[rows, :] + part
        first = False
    return tile[...]


def _attn_prep(qkv, gains2, tm):
    T = qkv.shape[0]
    scale = HEAD_DIM ** -0.5
    n = len(DILS)

    def body(qkv_ref, g_ref, qb_ref, kb_ref, vb_ref, *rest):
        outs, tile = rest[:-1], rest[-1]
        lo = _lo_mask((tm, PAIR))

        def spread(kind, c, y):
            _spread(tile, y, outs[kind * n:(kind + 1) * n], c, BF16)

        def normed(c, gi, mult):
            xv = qkv_ref[:, c * PAIR:(c + 1) * PAIR]
            r = lax.rsqrt(_half_sum(xv * xv, lo) * (1.0 / HEAD_DIM) + EPS)
            y = xv * r * g_ref[gi:gi + 1, :]
            return y * mult if mult != 1.0 else y

        def both_halves(v):
            sw = pltpu.roll(v, HEAD_DIM, 1)
            return jnp.where(lo, v, sw), jnp.where(lo, sw, v)

        for c in range(4):
            spread(0, c, normed(c, 0, scale))
            spread(1, c, normed(4 + c, 1, 1.0))
            spread(2, c, qkv_ref[:, (8 + c) * PAIR:(9 + c) * PAIR])
            qb_ref[c] = normed(12 + c, 2, scale).astype(BF16)
        k0, k1 = both_halves(normed(16, 3, 1.0))
        kb_ref[0] = k0.astype(BF16)
        kb_ref[1] = k1.astype(BF16)
        v0, v1 = both_halves(qkv_ref[:, 17 * PAIR:18 * PAIR])
        vb_ref[0] = v0.astype(BF16)
        vb_ref[1] = v1.astype(BF16)

    four = pl.BlockSpec((4, tm, PAIR), lambda i: (0, i, 0))
    two = pl.BlockSpec((2, tm, PAIR), lambda i: (0, i, 0))
    s4 = jax.ShapeDtypeStruct((4, T, PAIR), BF16)
    s2 = jax.ShapeDtypeStruct((2, T, PAIR), BF16)
    specs, shapes = _spread_specs(tm, T, BF16)
    res = pl.pallas_call(
        body, name="attn_prep", grid=(T // tm,),
        in_specs=[pl.BlockSpec((tm, qkv.shape[1]), lambda i: (i, 0)), pl.BlockSpec((4, PAIR), lambda i: (0, 0))],
        out_specs=[four, two, two] + specs * 3,
        out_shape=[s4, s2, s2] + shapes * 3,
        scratch_shapes=[pltpu.VMEM((tm, PAIR), F32)],
        compiler_params=_params("parallel"),
    )(qkv, gains2)
    qb, kb, vb = res[:3]
    per_d = [tuple(res[3 + kind * n + di].reshape(4 * d, T // d, PAIR) for kind in range(3))
             for di, d in enumerate(DILS)]
    return qb, kb, vb, per_d


def _loop_blocks(nb, body, init, per_iter):
    u = math.gcd(nb, per_iter)

    def outer(i, carry):
        for k in range(u):
            carry = body(i * u + k, carry)
        return carry

    return lax.fori_loop(0, nb // u, outer, init)


def _key_window(b, nb, L, R, W):
    start = pl.multiple_of(jnp.clip(b * BQ - R, 0, L - W), HEAD_DIM)
    return start, jnp.where(b == 0, 1, jnp.where(b == nb - 1, 2, 0))


def _stack_heads(v, lo):
    z = jnp.zeros_like(v)
    return jnp.concatenate([jnp.where(lo, v, z), jnp.where(lo, z, v)], axis=0)


def _unstack_heads(v2, lo):
    return jnp.where(lo, v2[:BQ], v2[BQ:])


def _row_vector(v, lo):
    r = lax.broadcasted_iota(jnp.int32, (BQ, PAIR), 0)
    ln = lax.broadcasted_iota(jnp.int32, (BQ, PAIR), 1)
    diag = (ln % HEAD_DIM) == (r % HEAD_DIM)
    top = jnp.sum(jnp.where(diag & (r < HEAD_DIM), v, 0.0), axis=0, keepdims=True)
    bot = jnp.sum(jnp.where(diag & (r >= HEAD_DIM), v, 0.0), axis=0, keepdims=True)
    top8, bot8 = jnp.broadcast_to(top, (8, PAIR)), jnp.broadcast_to(bot, (8, PAIR))
    lo8 = _lo_mask((8, PAIR))
    head0 = jnp.where(lo8, top8, pltpu.roll(bot8, HEAD_DIM, 1))
    head1 = jnp.where(lo8, pltpu.roll(top8, HEAD_DIM, 1), bot8)
    return jnp.concatenate([head0, head1], axis=1)[:1]


def _units_per_step(nb, pairs_per_kv):
    return max(1, 16 // nb) if pairs_per_kv == 1 else 1


def _attn_fwd(q, kp, vp, bias4, sink, R, pairs_per_kv, pairs_per_bias):
    N, L, _ = q.shape
    W = BQ + 2 * R
    nb = L // BQ
    assert L >= W and nb >= 2
    G = _units_per_step(nb, pairs_per_kv)

    def body(sink_ref, q_ref, k_ref, v_ref, bias_ref, o_ref, lse_ref):
        n = pl.program_id(0)
        lo_q = _lo_mask((BQ, PAIR))
        first = lax.broadcasted_iota(jnp.int32, (2 * BQ, 1), 0) < BQ

        def blk(f, carry):
            g, b = f // nb, f % nb
            u = n * G + g
            sk = jnp.where(first, sink_ref[2 * u], sink_ref[2 * u + 1])
            q0 = pl.multiple_of(b * BQ, BQ)
            q2 = _stack_heads(q_ref[g, pl.ds(q0, BQ), :], lo_q)
            k0, variant = _key_window(b, nb, L, R, W)
            kw = k_ref[g, pl.ds(k0, W), :]
            vw = v_ref[g, pl.ds(k0, W), :]
            s = _dot_nt(q2, kw) + bias_ref[variant]
            m = jnp.maximum(jnp.max(s, axis=1, keepdims=True), sk)
            p = jnp.exp(s - m)
            l = jnp.sum(p, axis=1, keepdims=True) + jnp.exp(sk - m)
            o2 = _dot(p.astype(BF16), vw) / l
            o_ref[g, pl.ds(q0, BQ), :] = _unstack_heads(o2, lo_q)
            lse_ref[g, pl.ds(q0, BQ), :] = _unstack_heads(jnp.broadcast_to(m + jnp.log(l), (2 * BQ, PAIR)), lo_q)
            return carry

        _loop_blocks(G * nb, blk, 0, 16)

    qspec = pl.BlockSpec((G, L, PAIR), lambda n: (n, 0, 0))
    kspec = pl.BlockSpec((G, L, PAIR), lambda n: (n // pairs_per_kv, 0, 0))
    return pl.pallas_call(
        body, name="attn_fwd", grid=(N // G,),
        in_specs=[pl.BlockSpec(memory_space=pltpu.SMEM), qspec, kspec, kspec,
                  pl.BlockSpec((None, 3, 2 * BQ, W), lambda n: (n * G // pairs_per_bias, 0, 0, 0))],
        out_specs=[qspec, qspec],
        out_shape=[jax.ShapeDtypeStruct((N, L, PAIR), F32), jax.ShapeDtypeStruct((N, L, PAIR), F32)],
        compiler_params=_params("parallel"),
    )(sink, q, kp, vp, bias4)


def _attn_bwd(q, kp, vp, bias4t, sink, o, lse, do, R, pairs_per_kv, pairs_per_bias):
    N, L, _ = q.shape
    Nk = kp.shape[0]
    Pb = bias4t.shape[0]
    W = BQ + 2 * R
    nb = L // BQ
    assert L >= W and nb >= 2
    G = _units_per_step(nb, pairs_per_kv)

    def body(sink_ref, q_ref, k_ref, v_ref, bias_ref, o_ref, lse_ref, do_ref,
             dq_ref, dk_ref, dv_ref, dbias_ref, dsink_ref, dk_acc, dv_acc):
        n = pl.program_id(0)
        lo_q = _lo_mask((BQ, PAIR))
        first = lax.broadcasted_iota(jnp.int32, (1, 2 * BQ), 1) < BQ
        dsink_ref[...] = jnp.zeros_like(dsink_ref)

        @pl.when(n % pairs_per_kv == 0)
        def _():
            dk_acc[...] = jnp.zeros_like(dk_acc)
            dv_acc[...] = jnp.zeros_like(dv_acc)

        @pl.when((n * G) % pairs_per_bias == 0)
        def _():
            dbias_ref[...] = jnp.zeros_like(dbias_ref)

        def blk(f, carry):
            g, b = f // nb, f % nb
            u = n * G + g
            sk = jnp.where(first, sink_ref[2 * u], sink_ref[2 * u + 1])
            q0 = pl.multiple_of(b * BQ, BQ)
            q2 = _stack_heads(q_ref[g, pl.ds(q0, BQ), :], lo_q)
            k0, variant = _key_window(b, nb, L, R, W)
            kw = k_ref[g, pl.ds(k0, W), :]
            vw = v_ref[g, pl.ds(k0, W), :]
            dov = do_ref[g, pl.ds(q0, BQ), :]
            lse = _row_vector(lse_ref[g, pl.ds(q0, BQ), :], lo_q)
            delta = _row_vector(_half_sum(dov.astype(F32) * o_ref[g, pl.ds(q0, BQ), :], lo_q), lo_q)
            do2 = _stack_heads(dov.astype(BF16), lo_q)
            st = _dot_nt(kw, q2) + bias_ref[variant]
            pt = jnp.exp(st - lse)
            dst = pt * (_dot_nt(vw, do2) - delta)
            dstb = dst.astype(BF16)
            dbias_ref[variant] += dst
            dk_acc[g, pl.ds(k0, W), :] += _dot(dstb, q2)
            dv_acc[g, pl.ds(k0, W), :] += _dot(pt.astype(BF16), do2)
            dq_ref[g, pl.ds(q0, BQ), :] = _unstack_heads(_dot_tn(dstb, kw), lo_q).astype(BF16)
            dsink_ref[g, pl.ds(0, 1), :] -= jnp.exp(sk - lse) * delta
            return carry

        _loop_blocks(G * nb, blk, 0, 8)
        dk_ref[...] = dk_acc[...].astype(BF16)
        dv_ref[...] = dv_acc[...].astype(BF16)

    qspec = pl.BlockSpec((G, L, PAIR), lambda n: (n, 0, 0))
    kspec = pl.BlockSpec((G, L, PAIR), lambda n: (n // pairs_per_kv, 0, 0))
    return pl.pallas_call(
        body, name="attn_bwd", grid=(N // G,),
        in_specs=[pl.BlockSpec(memory_space=pltpu.SMEM), qspec, kspec, kspec,
                  pl.BlockSpec((None, 3, W, 2 * BQ), lambda n: (n * G // pairs_per_bias, 0, 0, 0)),
                  qspec, qspec, qspec],
        out_specs=[qspec, kspec, kspec,
                   pl.BlockSpec((None, 3, W, 2 * BQ), lambda n: (n * G // pairs_per_bias, 0, 0, 0)),
                   pl.BlockSpec((G, 8, 2 * BQ), lambda n: (n, 0, 0))],
        out_shape=[jax.ShapeDtypeStruct((N, L, PAIR), BF16),
                   jax.ShapeDtypeStruct((Nk, L, PAIR), BF16),
                   jax.ShapeDtypeStruct((Nk, L, PAIR), BF16),
                   jax.ShapeDtypeStruct((Pb, 3, W, 2 * BQ), F32),
                   jax.ShapeDtypeStruct((N, 8, 2 * BQ), F32)],
        scratch_shapes=[pltpu.VMEM((G, L, PAIR), F32), pltpu.VMEM((G, L, PAIR), F32)],
        compiler_params=_params("arbitrary"),
    )(sink, q, kp, vp, bias4t, o, lse, do)


def _attn_merge(branch_outs, ob, tm):
    T = ob.shape[1]
    n = len(DILS)

    def body(*refs):
        o_in, l_in, ob_ref = refs[:n], refs[n:2 * n], refs[2 * n]
        o_out, l_out, cat_ref = refs[2 * n + 1:3 * n + 1], refs[3 * n + 1:4 * n + 1], refs[4 * n + 1]
        tiles = refs[4 * n + 2:]
        for c in range(4):
            o_nat, l_nat = [], []
            for di, d in enumerate(DILS):
                for kind, (src, dst) in enumerate(((o_in[di], o_nat), (l_in[di], l_nat))):
                    tile = tiles[2 * di + kind]
                    if d == 1:
                        dst.append(src[c, 0])
                    else:
                        for r in range(d):
                            tile[pl.ds(r, tm // d, stride=d), :] = src[c, r]
                        dst.append(tile[...])
            m = functools.reduce(jnp.maximum, l_nat)
            ws = [jnp.exp(l - m) for l in l_nat]
            z = sum(ws)
            o = sum(w * t for w, t in zip(ws, o_nat)) / z
            cat_ref[:, c * PAIR:(c + 1) * PAIR] = o.astype(BF16)
            cat_ref[:, (4 + c) * PAIR:(5 + c) * PAIR] = ob_ref[c].astype(BF16)
            _spread(tiles[0], o, o_out, c, F32)
            _spread(tiles[1], m + jnp.log(z), l_out, c, F32)

    specs, shapes = _spread_specs(tm, T, F32)
    four = pl.BlockSpec((4, tm, PAIR), lambda i: (0, i, 0))
    o_views = [o.reshape(4, d, T // d, PAIR) for (o, _), d in zip(branch_outs, DILS)]
    l_views = [l.reshape(4, d, T // d, PAIR) for (_, l), d in zip(branch_outs, DILS)]
    res = pl.pallas_call(
        body, name="attn_merge", grid=(T // tm,),
        in_specs=specs + specs + [four],
        out_specs=specs + specs + [pl.BlockSpec((tm, 8 * PAIR), lambda i: (i, 0))],
        out_shape=shapes + shapes + [jax.ShapeDtypeStruct((T, 8 * PAIR), BF16)],
        scratch_shapes=[pltpu.VMEM((tm, PAIR), F32)] * (2 * n),
        compiler_params=_params("parallel"),
    )(*o_views, *l_views, ob)
    merged = [(res[di].reshape(4 * d, T // d, PAIR), res[n + di].reshape(4 * d, T // d, PAIR))
              for di, d in enumerate(DILS)]
    return merged, res[2 * n]


def _weight_arg(w, blk):
    if blk is None:
        return pl.BlockSpec(w.shape, lambda i: (0, 0)), (lambda ref: ref[...])
    D = w.shape[2]
    return (pl.BlockSpec((N_DEV, 128, D), lambda i: (0, blk, 0)),
            lambda ref: ref[...].reshape(N_DEV * 128, D))


def _oproj_fwd(x, o_cat, w, blk, tm):
    T, D = x.shape
    wspec, wload = _weight_arg(w, blk)

    def body(x_ref, o_ref, w_ref, out_ref):
        out_ref[...] = x_ref[...] + _dot(o_ref[...], wload(w_ref))

    tok = pl.BlockSpec((tm, D), lambda i: (i, 0))
    return pl.pallas_call(
        body, name="oproj_fwd", grid=(T // tm,),
        in_specs=[tok, pl.BlockSpec((tm, o_cat.shape[1]), lambda i: (i, 0)), wspec],
        out_specs=tok, out_shape=jax.ShapeDtypeStruct((T, D), F32),
        compiler_params=_params("parallel"),
    )(x, o_cat, w)


def _oproj_bwd(dx, w, blk, tm, dep=None):
    T, D = dx.shape
    wspec, wload = _weight_arg(w, blk)

    def body(dx_ref, w_ref, dxb_ref, dob_ref, *rest):
        doa_refs, tile = rest[:-1], rest[-1]
        db = dx_ref[...].astype(BF16)
        dxb_ref[...] = db
        do = _dot_nt(db, wload(w_ref))
        for c in range(4):
            _spread(tile, do[:, c * PAIR:(c + 1) * PAIR], doa_refs, c, BF16)
            dob_ref[c] = do[:, (4 + c) * PAIR:(5 + c) * PAIR].astype(BF16)

    tok = pl.BlockSpec((tm, D), lambda i: (i, 0))
    specs, shapes = _spread_specs(tm, T, BF16)
    body, in_specs, args = _with_dep(body, dep, [tok, wspec], [dx, w])
    res = pl.pallas_call(
        body, name="oproj_bwd", grid=(T // tm,),
        in_specs=in_specs,
        out_specs=[tok, pl.BlockSpec((4, tm, PAIR), lambda i: (0, i, 0))] + specs,
        out_shape=[jax.ShapeDtypeStruct((T, D), BF16), jax.ShapeDtypeStruct((4, T, PAIR), BF16)] + shapes,
        scratch_shapes=[pltpu.VMEM((tm, PAIR), F32)],
        compiler_params=_params("parallel"),
    )(*args)
    return res[0], res[1], [t.reshape(4 * d, T // d, PAIR) for t, d in zip(res[2:], DILS)]


def _attn_post(qkv, gains2, dqa, dka, dva, dqb, dkb, dvb, tm):
    T, NQ = qkv.shape
    scale = HEAD_DIM ** -0.5

    n = len(DILS)

    def body(qkv_ref, g_ref, *rest):
        dq_refs, dk_refs, dv_refs = rest[:n], rest[n:2 * n], rest[2 * n:3 * n]
        qb_ref, kb_ref, vb_ref, out_ref, dg_ref, tile = rest[3 * n:]
        lo = _lo_mask((tm, PAIR))

        @pl.when(pl.program_id(0) == 0)
        def _():
            dg_ref[...] = jnp.zeros_like(dg_ref)

        def norm_bwd(c, gi, dy):
            xv = qkv_ref[:, c * PAIR:(c + 1) * PAIR]
            r = lax.rsqrt(_half_sum(xv * xv, lo) * (1.0 / HEAD_DIM) + EPS)
            xn = xv * r
            dg_ref[gi:gi + 1, :] += jnp.sum(dy * xn, axis=0, keepdims=True)
            dxn = dy * g_ref[gi:gi + 1, :]
            dx = r * (dxn - xn * (_half_sum(dxn * xn, lo) * (1.0 / HEAD_DIM)))
            out_ref[:, c * PAIR:(c + 1) * PAIR] = dx.astype(BF16)

        def fold(v):
            return v + pltpu.roll(v, HEAD_DIM, 1)

        for c in range(4):
            norm_bwd(c, 0, _collect(tile, dq_refs, c) * scale)
            norm_bwd(4 + c, 1, _collect(tile, dk_refs, c))
            out_ref[:, (8 + c) * PAIR:(9 + c) * PAIR] = _collect(tile, dv_refs, c).astype(BF16)
            norm_bwd(12 + c, 2, qb_ref[c].astype(F32) * scale)
        kb, vb = kb_ref[...].astype(F32), vb_ref[...].astype(F32)
        norm_bwd(16, 3, jnp.where(lo, fold(kb[0]), fold(kb[1])))
        out_ref[:, 17 * PAIR:18 * PAIR] = jnp.where(lo, fold(vb[0]), fold(vb[1])).astype(BF16)

    four = pl.BlockSpec((4, tm, PAIR), lambda i: (0, i, 0))
    two = pl.BlockSpec((2, tm, PAIR), lambda i: (0, i, 0))
    specs, _ = _spread_specs(tm, T, BF16)
    views = [t.reshape(4, d, T // d, PAIR) for group in (dqa, dka, dva) for t, d in zip(group, DILS)]
    return pl.pallas_call(
        body, name="attn_post", grid=(T // tm,),
        in_specs=[pl.BlockSpec((tm, NQ), lambda i: (i, 0)), pl.BlockSpec((4, PAIR), lambda i: (0, 0))]
        + specs * 3 + [four, two, two],
        out_specs=[pl.BlockSpec((tm, NQ), lambda i: (i, 0)), pl.BlockSpec((4, PAIR), lambda i: (0, 0))],
        out_shape=[jax.ShapeDtypeStruct((T, NQ), BF16), jax.ShapeDtypeStruct((4, PAIR), F32)],
        scratch_shapes=[pltpu.VMEM((tm, PAIR), F32)],
        compiler_params=_params("arbitrary"),
    )(qkv, gains2, *views, dqb, dkb, dvb)


def _dense_norm_bwd(dres, dz, w, blk, x, g, tm):
    T, D = x.shape
    N = dz.shape[1]
    wspec, wload = _weight_arg(w, blk)

    def body(dres_ref, dz_ref, w_ref, x_ref, g_ref, dx_ref, dgn_ref):
        i = pl.program_id(0)
        dx, dg = _norm_bwd(_dot_nt(dz_ref[...], wload(w_ref)), x_ref[...], g_ref[...])
        dx_ref[...] = dres_ref[...] + dx

        @pl.when(i == 0)
        def _():
            dgn_ref[...] = dg

        @pl.when(i > 0)
        def _():
            dgn_ref[...] += dg

    tok = pl.BlockSpec((tm, D), lambda i: (i, 0))
    row = pl.BlockSpec((1, D), lambda i: (0, 0))
    return pl.pallas_call(
        body, name="dense_norm_bwd", grid=(T // tm,),
        in_specs=[tok, pl.BlockSpec((tm, N), lambda i: (i, 0)), wspec, tok, row],
        out_specs=[tok, row],
        out_shape=[jax.ShapeDtypeStruct((T, D), F32), jax.ShapeDtypeStruct((1, D), F32)],
        compiler_params=_params("arbitrary"),
    )(dres, dz, w, x, g)


def _bias_reduce(onehot, dbm):
    Hb, K = dbm.shape

    def body(oh_ref, d_ref, out_ref):
        oh = oh_ref[...]
        d = d_ref[...]
        hi = d.astype(BF16)
        r1 = d - hi.astype(F32)
        mid = r1.astype(BF16)
        low = (r1 - mid.astype(F32)).astype(BF16)
        out_ref[...] = _dot_nt(hi, oh) + _dot_nt(mid, oh) + _dot_nt(low, oh)

    vm = pl.BlockSpec(memory_space=pltpu.VMEM)
    return pl.pallas_call(
        body, name="bias_reduce", in_specs=[vm, vm], out_specs=vm,
        out_shape=jax.ShapeDtypeStruct((Hb, N_BUCKETS), F32),
        compiler_params=pltpu.CompilerParams(vmem_limit_bytes=VMEM_LIMIT),
    )(onehot, dbm)


def _ple_fwd(x, g, wg, blk, p, wp, target, tm):
    T, D = x.shape
    P = p.shape[1]
    with_loss = target is not None
    wspec, wload = _weight_arg(wg, blk)

    def body(*refs):
        if with_loss:
            x_ref, g_ref, wg_ref, p_ref, wp_ref, t_ref, y_ref, hn_ref, gate_ref, pp_ref, pb_ref, loss_ref = refs
        else:
            x_ref, g_ref, wg_ref, p_ref, wp_ref, y_ref, hn_ref, gate_ref, pp_ref, pb_ref = refs
        i = pl.program_id(0)
        xv = x_ref[...]
        hb = (xv * _rstd(xv) * g_ref[...]).astype(BF16)
        hn_ref[...] = hb
        gate = _sigmoid(_dot(hb, wload(wg_ref)))
        pb = p_ref[...].astype(BF16)
        pb_ref[...] = pb
        pp = _dot(pb, wp_ref[...])
        gate_ref[...] = gate
        pp_ref[...] = pp
        y = xv + gate * pp
        if with_loss:
            err = y - t_ref[...]
            y_ref[...] = err * (1.0 / D)
            part = jnp.broadcast_to(0.5 * jnp.sum(jnp.sum(err * err, axis=1, keepdims=True) * (1.0 / D),
                                                  axis=0, keepdims=True), (1, 128))

            @pl.when(i == 0)
            def _():
                loss_ref[...] = part

            @pl.when(i > 0)
            def _():
                loss_ref[...] += part
        else:
            y_ref[...] = y

    tok = pl.BlockSpec((tm, D), lambda i: (i, 0))
    ptok = pl.BlockSpec((tm, P), lambda i: (i, 0))
    in_specs = [tok, pl.BlockSpec((1, D), lambda i: (0, 0)), wspec, ptok,
                pl.BlockSpec((P, D), lambda i: (0, 0))]
    out_specs = [tok, tok, tok, tok, ptok]
    out_shape = [jax.ShapeDtypeStruct((T, D), F32), jax.ShapeDtypeStruct((T, D), BF16),
                 jax.ShapeDtypeStruct((T, D), F32), jax.ShapeDtypeStruct((T, D), F32),
                 jax.ShapeDtypeStruct((T, P), BF16)]
    args = [x, g, wg, p, wp]
    if with_loss:
        in_specs.append(tok)
        out_specs.append(pl.BlockSpec((1, 128), lambda i: (0, 0)))
        out_shape.append(jax.ShapeDtypeStruct((1, 128), F32))
        args.append(target)
    return pl.pallas_call(
        body, name="ple_fwd_loss" if with_loss else "ple_fwd", grid=(T // tm,),
        in_specs=in_specs, out_specs=out_specs, out_shape=out_shape,
        compiler_params=_params("arbitrary" if with_loss else "parallel"),
    )(*args)


def _ple_bwd(dy, gate, pp, tm, dep=None):
    T, D = dy.shape

    def body(dy_ref, gate_ref, pp_ref, dgl_ref, dpp_ref):
        d = dy_ref[...]
        gt = gate_ref[...]
        dgl_ref[...] = (d * pp_ref[...] * gt * (1.0 - gt)).astype(BF16)
        dpp_ref[...] = (d * gt).astype(BF16)

    tok = pl.BlockSpec((tm, D), lambda i: (i, 0))
    body, in_specs, args = _with_dep(body, dep, [tok, tok, tok], [dy, gate, pp])
    return pl.pallas_call(
        body, name="ple_bwd", grid=(T // tm,), in_specs=in_specs, out_specs=[tok, tok],
        out_shape=[jax.ShapeDtypeStruct((T, D), BF16), jax.ShapeDtypeStruct((T, D), BF16)],
        compiler_params=_params("parallel"),
    )(*args)


def _adamw(w, g, m, v):
    shape = w.shape
    C = shape[-1]
    w2, g2, m2, v2 = (a.reshape(-1, C) for a in (w, g, m, v))
    Rn = w2.shape[0]
    tr = Rn
    for cand in (512, 352, 256):
        if Rn % cand == 0:
            tr = cand
            break
    c1 = 1.0 - ADAM_B1 ** ADAM_STEP
    c2 = 1.0 - ADAM_B2 ** ADAM_STEP

    def body(w_ref, g_ref, m_ref, v_ref, d_ref, nm_ref, nv_ref):
        gv = g_ref[...]
        mn = ADAM_B1 * m_ref[...] + (1.0 - ADAM_B1) * gv
        vn = ADAM_B2 * v_ref[...] + (1.0 - ADAM_B2) * (gv * gv)
        d_ref[...] = -ADAM_LR * ((mn / c1) / (jnp.sqrt(vn / c2) + ADAM_EPS) + ADAM_WD * w_ref[...])
        nm_ref[...] = mn
        nv_ref[...] = vn

    spec = pl.BlockSpec((tr, C), lambda i: (i, 0))
    sh = jax.ShapeDtypeStruct((Rn, C), F32)
    d, nm, nv = pl.pallas_call(
        body, name="adamw", grid=(Rn // tr,), in_specs=[spec] * 4, out_specs=[spec] * 3, out_shape=[sh] * 3,
        compiler_params=_params("parallel"),
    )(w2, g2, m2, v2)
    return d.reshape(shape), nm.reshape(shape), nv.reshape(shape)


def _my_place():
    x, y, c = lax.axis_index("x"), lax.axis_index("y"), lax.axis_index("c")
    chips = [(1 - x, y), (x, 1 - y), (1 - x, 1 - y)]
    return x, y, c, chips


def _all_gather(arrs):
    n = len(arrs)

    def body(*refs):
        x_refs, out_refs = refs[:n], refs[n:2 * n]
        send_sems, recv_sems, local_sems = refs[2 * n:]
        x, y, c, chips = _my_place()
        me, sibling = (x, y, c), (x, y, 1 - c)

        def copy(m, k, block, to, src=None):
            rows = out_refs[m].at[4 * block[0] + 2 * block[1] + block[2]]
            return pltpu.make_async_remote_copy(
                src_ref=rows if src is None else src, dst_ref=rows,
                send_sem=send_sems.at[7 * m + k], recv_sem=recv_sems.at[7 * m + k], device_id=to, device_id_type=MESH)

        mine = [pltpu.make_async_copy(x_refs[m], out_refs[m].at[4 * x + 2 * y + c], local_sems.at[m])
                for m in range(n)]
        for cp in mine:
            cp.start()
        first = []
        for m in range(n):
            first.append(copy(m, 0, me, sibling, src=x_refs[m]))
            first += [copy(m, 1 + j, me, (*chip, c), src=x_refs[m]) for j, chip in enumerate(chips)]
        for cp in first:
            cp.start()
        passed = []
        for m in range(n):
            for j, chip in enumerate(chips):
                copy(m, 1 + j, (*chip, c), me).wait_recv()
                cp = copy(m, 4 + j, (*chip, c), sibling)
                cp.start()
                passed.append(cp)
        for m in range(n):
            copy(m, 0, sibling, me).wait_recv()
            for j, chip in enumerate(chips):
                copy(m, 4 + j, (*chip, 1 - c), me).wait_recv()
        for cp in first + passed:
            cp.wait_send()
        for cp in mine:
            cp.wait()

    hbm = pl.BlockSpec(memory_space=pl.ANY)
    return pl.pallas_call(
        body, name="all_gather", in_specs=[hbm] * n, out_specs=[hbm] * n,
        out_shape=[jax.ShapeDtypeStruct((N_DEV,) + a.shape, a.dtype) for a in arrs],
        scratch_shapes=[pltpu.SemaphoreType.DMA((7 * n,)), pltpu.SemaphoreType.DMA((7 * n,)),
                        pltpu.SemaphoreType.DMA((n,))],
    )(*arrs)


def _peer(x, y, c, k):
    return (x ^ ((k >> 2) & 1), y ^ ((k >> 1) & 1), c ^ (k & 1))


HBM_SPEC = pl.BlockSpec(memory_space=pltpu.HBM)
SEM_SPEC = pl.BlockSpec(memory_space=pltpu.SEMAPHORE)


def _exchange_refs(srcs, lands, m, k, x, y, c, scatter):
    peer = _peer(x, y, c, k)
    if scatter:
        return srcs[m].at[4 * peer[0] + 2 * peer[1] + peer[2]], lands[m].at[k - 1], peer
    return srcs[m], lands[m].at[4 * x + 2 * y + c], peer


def _exchange_start(arrs, land_shapes, scatter, name):
    n = len(arrs)

    def body(*refs):
        srcs, lands = refs[:n], refs[n:2 * n]
        send_sems, recv_sems = refs[2 * n], refs[2 * n + 1]
        token = refs[-1]
        x, y, c, _ = _my_place()
        for m in range(n):
            for k in range(1, N_DEV):
                src, dst, peer = _exchange_refs(srcs, lands, m, k, x, y, c, scatter)
                pltpu.make_async_remote_copy(
                    src_ref=src, dst_ref=dst, send_sem=send_sems.at[7 * m + k - 1],
                    recv_sem=recv_sems.at[7 * m + k - 1], device_id=peer, device_id_type=MESH).start()
        token[...] = jnp.zeros_like(token)

    zones = [lax.empty(s_, a.dtype) for s_, a in zip(land_shapes, arrs)]
    outs = pl.pallas_call(
        body, name=name,
        out_shape=(pltpu.SemaphoreType.DMA((7 * n,)), pltpu.SemaphoreType.DMA((7 * n,)),
                   *[pltpu.HBM(a.shape, a.dtype) for a in arrs], *[pltpu.HBM(z.shape, z.dtype) for z in zones],
                   jax.ShapeDtypeStruct((8, 128), F32)),
        in_specs=[HBM_SPEC] * (2 * n),
        out_specs=(SEM_SPEC, SEM_SPEC, *[HBM_SPEC] * (2 * n), pl.BlockSpec(memory_space=pltpu.VMEM)),
        input_output_aliases={m: 2 + m for m in range(2 * n)},
        compiler_params=pltpu.CompilerParams(has_side_effects=pltpu.SideEffectType.DATAFLOW_SIDE_EFFECTING),
    )(*[pltpu.with_memory_space_constraint(a, pltpu.HBM) for a in arrs],
      *[pltpu.with_memory_space_constraint(z, pltpu.HBM) for z in zones])
    return outs[0], outs[1], list(outs[2:2 + n]), list(outs[2 + n:2 + 2 * n]), outs[-1]


def _exchange_wait(send_sems, recv_sems, arrs, zones, after, scatter, name):
    n = len(arrs)
    afters = list(after) if isinstance(after, (list, tuple)) else [after]

    def body(*refs):
        srcs, lands = refs[:n], refs[n:2 * n]
        send_sems, recv_sems = refs[2 * n], refs[2 * n + 1]
        x, y, c, _ = _my_place()
        for m in range(n):
            for k in range(1, N_DEV):
                src, dst, peer = _exchange_refs(srcs, lands, m, k, x, y, c, scatter)
                cp = pltpu.make_async_remote_copy(
                    src_ref=src, dst_ref=dst, send_sem=send_sems.at[7 * m + k - 1],
                    recv_sem=recv_sems.at[7 * m + k - 1], device_id=peer, device_id_type=MESH)
                cp.wait_send()
                cp.wait_recv()

    outs = pl.pallas_call(
        body, name=name,
        out_shape=tuple(pltpu.HBM(a.shape, a.dtype) for a in list(arrs) + list(zones)),
        in_specs=[HBM_SPEC] * (2 * n) + [SEM_SPEC, SEM_SPEC] + [pl.BlockSpec(memory_space=pl.ANY)] * len(afters),
        out_specs=tuple([HBM_SPEC] * (2 * n)),
        input_output_aliases={m: m for m in range(2 * n)},
        compiler_params=pltpu.CompilerParams(has_side_effects=pltpu.SideEffectType.DATAFLOW_SIDE_EFFECTING),
    )(*arrs, *zones, send_sems, recv_sems, *afters)
    return list(outs[n:])


def _sum_parts(own, parts, tr, dep=None):
    R, W = own.shape

    def body(own_ref, parts_ref, out_ref):
        acc = own_ref[...].astype(F32)
        for k in range(N_DEV - 1):
            acc = acc + parts_ref[k].astype(F32)
        out_ref[...] = acc

    in_specs = [pl.BlockSpec((tr, W), lambda i: (i, 0)), pl.BlockSpec((N_DEV - 1, tr, W), lambda i: (0, i, 0))]
    body, in_specs, args = _with_dep(body, dep, in_specs, [own, parts])
    return pl.pallas_call(
        body, name="sum_parts", grid=(R // tr,),
        in_specs=in_specs,
        out_specs=pl.BlockSpec((tr, W), lambda i: (i, 0)),
        out_shape=jax.ShapeDtypeStruct((R, W), F32),
        compiler_params=_params("parallel"),
    )(*args)


def _all_reduce_small(v, dep=None):
    Rn, Wd = v.shape

    def body(v_ref, out_ref, gat_ref, send_sems, recv_sems):
        x, y, c, _ = _my_place()
        me = 4 * x + 2 * y + c
        gat_ref[me] = v_ref[...]
        copies = []
        for k in range(1, N_DEV):
            fx, fy, fc = (k >> 2) & 1, (k >> 1) & 1, k & 1
            peer = (x ^ fx, y ^ fy, c ^ fc)
            cp = pltpu.make_async_remote_copy(
                src_ref=v_ref, dst_ref=gat_ref.at[me], send_sem=send_sems.at[k - 1], recv_sem=recv_sems.at[k - 1],
                device_id=peer, device_id_type=MESH)
            cp.start()
            copies.append(cp)
        for cp in copies:
            cp.wait_recv()
        for cp in copies:
            cp.wait_send()
        acc = gat_ref[0]
        for k in range(1, N_DEV):
            acc = acc + gat_ref[k]
        out_ref[...] = acc

    vm = pl.BlockSpec(memory_space=pltpu.VMEM)
    body, in_specs, args = _with_dep(body, dep, [vm], [v])
    return pl.pallas_call(
        body, name="all_reduce_small", in_specs=in_specs, out_specs=vm,
        out_shape=jax.ShapeDtypeStruct((Rn, Wd), F32),
        scratch_shapes=[pltpu.VMEM((N_DEV, Rn, Wd), F32), pltpu.SemaphoreType.DMA((7,)),
                        pltpu.SemaphoreType.DMA((7,))],
    )(*args)


def _t5_bucket(rel):
    half = N_BUCKETS // 2
    max_exact = half // 2
    ret = jnp.where(rel > 0, half, 0)
    n = jnp.abs(rel)
    nf = jnp.maximum(n, 1).astype(F32)
    large = max_exact + (jnp.log(nf / max_exact) / math.log(MAX_DISTANCE / max_exact)
                         * (half - max_exact)).astype(jnp.int32)
    large = jnp.minimum(large, half - 1)
    return ret + jnp.where(n < max_exact, n, large)


def _band(R, d):
    W = BQ + 2 * R
    rel = jnp.arange(W)[None, :] - R - jnp.arange(BQ)[:, None]
    return _t5_bucket(rel * d), jnp.abs(rel) <= R


def _onehot(R, d):
    bkt, in_band = _band(R, d)
    return ((bkt.reshape(1, -1) == jnp.arange(N_BUCKETS)[:, None]) & in_band.reshape(1, -1)).astype(BF16)


def _bias_expand(table_t, onehot):
    H = table_t.shape[0]
    K = onehot.shape[1]

    def body(t_ref, oh_ref, out_ref):
        oh = oh_ref[...]
        t = t_ref[...]
        hi = t.astype(BF16)
        r1 = t - hi.astype(F32)
        mid = r1.astype(BF16)
        low = (r1 - mid.astype(F32)).astype(BF16)
        marked = _dot(jnp.ones(t.shape, BF16), oh) > 0.5
        out_ref[...] = jnp.where(marked, _dot(hi, oh) + _dot(mid, oh) + _dot(low, oh), NEG)

    vm = pl.BlockSpec(memory_space=pltpu.VMEM)
    return pl.pallas_call(
        body, name="bias_expand", in_specs=[vm, vm], out_specs=vm,
        out_shape=jax.ShapeDtypeStruct((H, K), F32),
        compiler_params=pltpu.CompilerParams(vmem_limit_bytes=VMEM_LIMIT),
    )(table_t, onehot)


def _bias_matrix(table, R, d):
    return _bias_expand(table.T, _onehot(R, d)).reshape(table.shape[1], BQ, BQ + 2 * R)


def _bias_variants(base, R):
    H, _, W = base.shape
    fill = jnp.full((H, BQ, R), NEG, F32)
    first = jnp.concatenate([base[:, :, R:], fill], axis=2)
    last = jnp.concatenate([fill, base[:, :, :W - R]], axis=2)
    v = jnp.stack([base, first, last], axis=1)
    v = v.reshape(H // 2, 2, 3, BQ, W).transpose(0, 2, 1, 3, 4).reshape(H // 2, 3, 2 * BQ, W)
    return v, v.transpose(0, 1, 3, 2)


def _bias_grad(dbt, R, d):
    P, _, W, _ = dbt.shape
    dbt = dbt[:, 0].at[:, R:].add(dbt[:, 1, :W - R]).at[:, :W - R].add(dbt[:, 2, R:])
    dbm = dbt.reshape(P, W, 2, BQ).transpose(0, 2, 3, 1).reshape(2 * P, BQ * W)
    return _bias_reduce(_onehot(R, d), dbm).T


def _tile2(gain):
    return jnp.concatenate([gain, gain])


ROW_W_O, ROW_GATE, B_ROWS = 768, 896, 1024
BLK_W_O, BLK_GATE = ROW_W_O // 128, ROW_GATE // 128


def _pack_layer(wts, i):
    a = jnp.stack([wts["ffn1_w_in"][i], wts["ffn2_w_in"][i]])
    D = a.shape[1]
    b = jnp.concatenate([
        wts["ffn1_w_out"][i], wts["ffn2_w_out"][i],
        jnp.zeros((ROW_W_O - 2 * wts["ffn1_w_out"].shape[1], D), a.dtype), wts["w_o"][i], wts["w_ple_gate"][i]])
    return a, b, wts["w_qkv"][i], wts["w_ple_proj"][i]


def _unpack_layer(sums, like):
    w_in2, b1, proj, w_o, qkv, w_in1, w_out1 = sums
    n_out = like["ffn1_w_out"].shape[1]
    out = {}
    if w_in2 is not None:
        out.update(ffn2_w_in=w_in2, ffn2_w_out=b1[:n_out], w_ple_gate=b1[n_out:], w_ple_proj=proj)
    if w_o is not None:
        out.update(w_o=w_o, w_qkv=qkv)
    if w_in1 is not None:
        out.update(ffn1_w_in=w_in1, ffn1_w_out=w_out1)
    return out


def _col_sharded(g):
    return g.transpose(1, 0, 2).reshape(g.shape[1], -1)


def _to_col_shards(g):
    rows = g.shape[0]
    return g.reshape(rows, N_DEV, -1).transpose(1, 0, 2)


def _layer_weights(ga, gb, gq, gp):
    return dict(ga=ga, gb=gb, w_qkv=_col_sharded(gq), w_proj=_col_sharded(gp))


def _layer_fwd(x, p, w, sm, i, target, tm, biases, dep=None):
    ga, gb = w["ga"], w["gb"]
    saved = {}
    saved["x0"] = x
    x1, saved["h1"], saved["zg1"], saved["zu1"], saved["s1"] = _ffn_fwd(
        x, sm["norm_ffn1"][i][None], ga, gb, 0, 2 * tm, dep)
    saved["x1"] = x1
    qkv, saved["hm"] = _qkv_fwd(x1, sm["norm_mix"][i][None], w["w_qkv"], 2 * tm)
    saved["qkv"] = qkv
    gains2 = jnp.stack([_tile2(sm[k][i]) for k in ("q_norm_a", "k_norm_a", "q_norm_b", "k_norm_b")])
    saved["gains2"] = gains2
    qb, kb, vb, qkv_d = _attn_prep(qkv, gains2, tm)
    no_sink = jnp.full((8,), NEG, F32)
    branches = []
    outs = []
    for (R, d), bias, (qd, kd, vd) in zip(DILATED, biases[:3], qkv_d):
        sink = jnp.tile(no_sink, d)
        outs.append(_attn_fwd(qd, kd, vd, bias[0], sink, R, 1, d))
        branches.append((qd, kd, vd, bias, sink, R, d))
    bias_b = biases[3]
    sink_b = sm["sink_b"][i]
    ob, lb = _attn_fwd(qb, kb, vb, bias_b[0], sink_b, SWA_RADIUS, 2, 1)
    merged, o_cat = _attn_merge(outs, ob, tm)
    saved.update(branches=branches, b=(qb, kb, vb, bias_b, sink_b), merged=merged, ob=ob, lb=lb, o_cat=o_cat)
    x2 = _oproj_fwd(x1, o_cat, gb, BLK_W_O, 2 * tm)
    saved["x2"] = x2
    x3, saved["h2"], saved["zg2"], saved["zu2"], saved["s2"] = _ffn_fwd(
        x2, sm["norm_ffn2"][i][None], ga, gb, 1, 2 * tm)
    saved["x3"] = x3
    res = _ple_fwd(x3, sm["norm_ple"][i][None], gb, BLK_GATE, p, w["w_proj"], target, tm)
    y, saved["hp"], saved["gate"], saved["pp"], saved["pb"] = res[:5]
    loss = res[5] if target is not None else None
    return y, loss, saved


def _layer_bwd(dy, w, sm, i, sv, tm, dep=None, on_ready=None, on_small=None, on_last=None):
    ga, gb = w["ga"], w["gb"]
    gs = {}
    D = dy.shape[1]
    dgl, dpp = _ple_bwd(dy, sv["gate"], sv["pp"], tm, dep)
    d_gate = _matmul_tn(sv["hp"], dgl, D, 4 * tm)
    d_proj = _matmul_tn(sv["pb"], dpp, D, 4 * tm)
    dx3, gs["norm_ple"] = _dense_norm_bwd(dy, dgl, gb, BLK_GATE, sv["x3"], sm["norm_ple"][i][None], 2 * tm)
    dx2, dyb, dzg, dzu, gs["norm_ffn2"] = _ffn_bwd(dx3, sv["x2"], sm["norm_ffn2"][i][None], sv["zg2"], sv["zu2"],
                                                   ga, gb, 1, tm)
    dwin2, dwo2 = _ffn_dw(sv["h2"], dzg, dzu, sv["s2"], dyb, 4 * tm)
    half = dwo2.shape[1] // 2
    after_ffn2 = [dwin2, jnp.concatenate([dwo2.reshape(N_DEV, half, D), d_gate.reshape(N_DEV, -1, D)], axis=1),
                  _to_col_shards(d_proj)]
    token = None if on_ready is None else on_ready(0, after_ffn2)
    dx2b, do_b, do_a = _oproj_bwd(dx2, gb, BLK_W_O, tm, token)
    d_wo = _matmul_tn(sv["o_cat"], dx2b, D, 4 * tm)
    dqa, dka, dva, dbias = [], [], [], []
    for (qd, kd, vd, bias, sink, R, d), (oa, la), do_d in zip(sv["branches"], sv["merged"], do_a):
        dq, dk, dv, dbm, _ = _attn_bwd(qd, kd, vd, bias[1], sink, oa, la, do_d, R, 1, d)
        dqa.append(dq)
        dka.append(dk)
        dva.append(dv)
        dbias.append(dbm)
    qb, kb, vb, bias_b, sink_b = sv["b"]
    dqb, dkb, dvb, dbm_b, dsink = _attn_bwd(qb, kb, vb, bias_b[1], sink_b, sv["ob"], sv["lb"], do_b,
                                            SWA_RADIUS, 2, 1)
    gs["rel_bias"] = dbias + [dbm_b]
    gs["sink_b"] = jnp.sum(dsink[:, 0].reshape(-1, 2, BQ), axis=2).reshape(-1)
    dqkv, dgains2 = _attn_post(sv["qkv"], sv["gains2"], dqa, dka, dva, dqb,
                               dkb, dvb, tm)
    dgains = dgains2[:, :HEAD_DIM] + dgains2[:, HEAD_DIM:]
    for k, name in enumerate(("q_norm_a", "k_norm_a", "q_norm_b", "k_norm_b")):
        gs[name] = dgains[k]
    d_qkv = _matmul_tn(sv["hm"], dqkv, dqkv.shape[1] // 2, 4 * tm)
    after_mixer = [d_wo.reshape(N_DEV, -1, D), _to_col_shards(d_qkv)]
    token = None if on_ready is None else on_ready(1, after_mixer)
    dx1, gs["norm_mix"] = _dense_norm_bwd(dx2, dqkv, w["w_qkv"], None, sv["x1"], sm["norm_mix"][i][None], 2 * tm)
    g1 = sm["norm_ffn1"][i][None]
    if on_last is None:
        dx0, dyb, dzg, dzu, gs["norm_ffn1"] = _ffn_bwd(dx1, sv["x0"], g1, sv["zg1"], sv["zu1"], ga, gb, 0, tm, token)
        dwin1, dwo1 = _ffn_dw(sv["h1"], dzg, dzu, sv["s1"], dyb, 4 * tm)
        return dx0, (after_ffn2, after_mixer, [dwin1, dwo1.reshape(N_DEV, half, D)]), gs
    dyb, dzg, dzu = _ffn_bwd_dz(dx1, sv["zg1"], sv["zu1"], gb, 0, 2 * tm, token)
    dwin1, dwo1 = _ffn_dw(sv["h1"], dzg, dzu, sv["s1"], dyb, 4 * tm, on_small(gs))
    last = [dwin1, dwo1.reshape(N_DEV, half, D)]
    dx0, gs["norm_ffn1"] = _ffn_bwd_dx(dx1, sv["x0"], g1, dzg, dzu, ga, 0, 2 * tm, on_last(last))
    return dx0, (after_ffn2, after_mixer, last), gs


def _bias_matrices(rel_bias):
    biases = [_bias_variants(_bias_matrix(rel_bias[:, :8], R, d), R) for R, d in DILATED]
    biases.append(_bias_variants(_bias_matrix(rel_bias[:, 8:], SWA_RADIUS, 1), SWA_RADIUS))
    return biases


def _stack_small(per_layer):
    small = {}
    for k, v in per_layer.items():
        if k == "rel_bias":
            per_branch = [sum(parts) for parts in zip(*v.values())]
            drel_a = sum(_bias_grad(t, R, d) for t, (R, d) in zip(per_branch[:3], DILATED))
            small[k] = jnp.concatenate([drel_a, _bias_grad(per_branch[3], SWA_RADIUS, 1)], axis=1)
        else:
            small[k] = jnp.stack([v[i].reshape(-1) for i in sorted(v)])
    return small


TM = 512
SUM_TILES = (512, 480, 256, 128, 512, 512, 352)
LAST_GROUP = ("ffn1_w_in", "ffn1_w_out")


def _pack_small(d, extra=None):
    parts = [d[k].reshape(-1) for k in SMALL]
    if extra is not None:
        parts.append(extra.reshape(-1))
    flat = jnp.concatenate(parts)
    return jnp.pad(flat, (0, SMALL_ROWS * 128 - flat.shape[0])).reshape(SMALL_ROWS, 128)


def _unpack_small(buf, like):
    flat = buf.reshape(-1)
    out, off = {}, 0
    for k in SMALL:
        n = like[k].size
        out[k] = flat[off:off + n].reshape(like[k].shape)
        off += n
    return out, flat[off]


def kernel(x, p, rel_bias, norm_ffn1, ffn1_w_in, ffn1_w_out, norm_mix, w_qkv, q_norm_a, k_norm_a, q_norm_b, k_norm_b, sink_b, w_o, norm_ffn2, ffn2_w_in, ffn2_w_out, norm_ple, w_ple_gate, w_ple_proj, loss_target, m_rel_bias, m_norm_ffn1, m_ffn1_w_in, m_ffn1_w_out, m_norm_mix, m_w_qkv, m_q_norm_a, m_k_norm_a, m_q_norm_b, m_k_norm_b, m_sink_b, m_w_o, m_norm_ffn2, m_ffn2_w_in, m_ffn2_w_out, m_norm_ple, m_w_ple_gate, m_w_ple_proj, v_rel_bias, v_norm_ffn1, v_ffn1_w_in, v_ffn1_w_out, v_norm_mix, v_w_qkv, v_q_norm_a, v_k_norm_a, v_q_norm_b, v_k_norm_b, v_sink_b, v_w_o, v_norm_ffn2, v_ffn2_w_in, v_ffn2_w_out, v_norm_ple, v_w_ple_gate, v_w_ple_proj):
    wts = dict(rel_bias=rel_bias, norm_ffn1=norm_ffn1, ffn1_w_in=ffn1_w_in, ffn1_w_out=ffn1_w_out,
               norm_mix=norm_mix, w_qkv=w_qkv, q_norm_a=q_norm_a, k_norm_a=k_norm_a, q_norm_b=q_norm_b,
               k_norm_b=k_norm_b, sink_b=sink_b, w_o=w_o, norm_ffn2=norm_ffn2, ffn2_w_in=ffn2_w_in,
               ffn2_w_out=ffn2_w_out, norm_ple=norm_ple, w_ple_gate=w_ple_gate, w_ple_proj=w_ple_proj)
    mom = dict(rel_bias=m_rel_bias, norm_ffn1=m_norm_ffn1, ffn1_w_in=m_ffn1_w_in, ffn1_w_out=m_ffn1_w_out,
               norm_mix=m_norm_mix, w_qkv=m_w_qkv, q_norm_a=m_q_norm_a, k_norm_a=m_k_norm_a, q_norm_b=m_q_norm_b,
               k_norm_b=m_k_norm_b, sink_b=m_sink_b, w_o=m_w_o, norm_ffn2=m_norm_ffn2, ffn2_w_in=m_ffn2_w_in,
               ffn2_w_out=m_ffn2_w_out, norm_ple=m_norm_ple, w_ple_gate=m_w_ple_gate, w_ple_proj=m_w_ple_proj)
    var = dict(rel_bias=v_rel_bias, norm_ffn1=v_norm_ffn1, ffn1_w_in=v_ffn1_w_in, ffn1_w_out=v_ffn1_w_out,
               norm_mix=v_norm_mix, w_qkv=v_w_qkv, q_norm_a=v_q_norm_a, k_norm_a=v_k_norm_a, q_norm_b=v_q_norm_b,
               k_norm_b=v_k_norm_b, sink_b=v_sink_b, w_o=v_w_o, norm_ffn2=v_norm_ffn2, ffn2_w_in=v_ffn2_w_in,
               ffn2_w_out=v_ffn2_w_out, norm_ple=v_norm_ple, w_ple_gate=v_w_ple_gate, w_ple_proj=v_w_ple_proj)
    sm = {k: wts[k] for k in SMALL}
    me = 4 * lax.axis_index("x") + 2 * lax.axis_index("y") + lax.axis_index("c")
    packed = []
    for i in range(2):
        a, *rest = _pack_layer(wts, i)
        packed.append([t.astype(BF16) for t in [a.reshape(-1, a.shape[-1])] + rest])
    a_shape = (2, ffn1_w_in.shape[1], ffn1_w_in.shape[2])

    def weights_of(zones):
        return _layer_weights(zones[0].reshape((N_DEV,) + a_shape), *zones[1:])

    w0 = weights_of(_all_gather(packed[0]))
    zone_shapes = [(N_DEV,) + t.shape for t in packed[1]]
    ssem, rsem, thru, zones, token = _exchange_start(packed[1], zone_shapes, False, "gather_start")
    biases = _bias_matrices(rel_bias)
    x1, _, sv0 = _layer_fwd(x[0], p[0, 0], w0, sm, 0, None, TM, biases, dep=token)
    zones = _exchange_wait(ssem, rsem, thru, zones, x1, False, "gather_wait")
    w1 = weights_of([lax.dynamic_update_index_in_dim(z, t, me, 0) for z, t in zip(zones, packed[1])])
    dy, loss, sv1 = _layer_fwd(x1, p[1, 0], w1, sm, 1, loss_target[0], TM, biases)

    def slots_for(arrs):
        return [(N_DEV - 1,) + t.shape[1:] for t in arrs]

    held1, held = {}, {}

    def on_ready1(stage, group):
        held1[stage] = _exchange_start(group, slots_for(group), True, f"scatter1_start_{stage}")
        return held1[stage][4]

    dx1, groups1, gs1 = _layer_bwd(dy, w1, sm, 1, sv1, TM, on_ready=on_ready1)
    on_ready1(2, groups1[2])
    g1 = groups1[0] + groups1[1] + groups1[2]

    def on_ready(stage, group):
        if stage == 1:
            held["slots1"] = [t for st in (0, 1, 2)
                              for t in _exchange_wait(*held1[st][:4], group[0], True, f"scatter1_wait_{st}")]
        held[stage] = _exchange_start(group, slots_for(group), True, f"scatter_start_{stage}")
        return held[stage][4]

    def on_small(gs0):
        part = dict(gs0, norm_ffn1=jnp.zeros_like(gs1["norm_ffn1"]))
        gsmall = _stack_small({k: {0: part[k], 1: gs1[k]} for k in part})
        held["small"] = _all_reduce_small(_pack_small(gsmall, loss[0, :1]))
        return held["small"]

    def on_last(group):
        held["last"] = _exchange_start(group, slots_for(group), True, "scatter_start_2")
        return held["last"][4]

    dx, groups0, gs0 = _layer_bwd(dx1, w0, sm, 0, sv0, TM, dep=held1[2][4], on_ready=on_ready, on_small=on_small,
                                  on_last=on_last)
    last = groups0[2]
    slots0 = [_exchange_wait(*held[stage][:4], last[0], True, f"scatter_wait_{stage}") for stage in (0, 1)]

    def summed(arrs, slots, tiles, dep=None):
        return [_sum_parts(lax.dynamic_index_in_dim(t, me, 0, keepdims=False), s_, tr, dep)
                for t, s_, tr in zip(arrs, slots, tiles)]

    cover = held["last"][4]
    r1 = summed(g1, held["slots1"], SUM_TILES, cover)
    r0 = summed(groups0[0], slots0[0], SUM_TILES[:3], cover) + summed(groups0[1], slots0[1], SUM_TILES[3:5], cover)

    def update(names, layers):
        for k in names:
            grads[k] = jnp.stack([layers[0][k], layers[1][k]])
            delta[k], new_m[k], new_v[k] = _adamw(wts[k], grads[k], mom[k], var[k])

    grads, delta, new_m, new_v = {}, {}, {}, {}
    layer1 = _unpack_layer(r1, wts)
    update([k for k in BIG if k not in LAST_GROUP], [_unpack_layer(r0 + [None, None], wts), layer1])

    cover_done = [dx] + [delta[k] for k in BIG if k not in LAST_GROUP]
    slots_last = _exchange_wait(*held["last"][:4], cover_done, True, "scatter_wait_2")
    update(LAST_GROUP, [_unpack_layer([None] * 5 + summed(last, slots_last, SUM_TILES[5:]), wts), layer1])
    late = _all_reduce_small(gs0["norm_ffn1"].reshape(-1, 128), dep=slots_last[0])
    small_sum, loss_sum = _unpack_small(held["small"], sm)
    small_sum["norm_ffn1"] = small_sum["norm_ffn1"].at[0].add(late.reshape(-1))
    grads.update(small_sum)
    zeros = {k: jnp.zeros_like(wts[k]) for k in SMALL}
    ds, ms, vs = _adamw(_pack_small(wts), _pack_small(small_sum), _pack_small(mom), _pack_small(var))
    for packed, dst in ((ds, delta), (ms, new_m), (vs, new_v)):
        dst.update(_unpack_small(packed, zeros)[0])

    return (loss_sum, dx[None], *[grads[k] for k in WEIGHTS], *[delta[k] for k in WEIGHTS],
            *[new_m[k] for k in WEIGHTS], *[new_v[k] for k in WEIGHTS])
```

```python
import functools
import math

import jax
import jax.numpy as jnp
from jax import lax
from jax.experimental import pallas as pl
from jax.experimental.pallas import tpu as pltpu

F32 = jnp.float32
BF16 = jnp.bfloat16

N_DEV = 8
HEAD_DIM = 64
PAIR = 2 * HEAD_DIM
BQ = 128
N_BUCKETS = 32
MAX_DISTANCE = 1024
DILATED = ((64, 1), (64, 4), (64, 16))
SWA_RADIUS = 128
EPS = 1e-6
NEG = -1e30
ADAM_LR, ADAM_B1, ADAM_B2, ADAM_EPS, ADAM_WD, ADAM_STEP = 0.001, 0.9, 0.999, 1e-08, 0.01, 10
VMEM_LIMIT = 56 * 1024 * 1024
MESH = pl.DeviceIdType.MESH

BIG = ("ffn1_w_in", "ffn1_w_out", "w_qkv", "w_o", "ffn2_w_in", "ffn2_w_out", "w_ple_gate", "w_ple_proj")
SMALL = ("rel_bias", "norm_ffn1", "norm_mix", "q_norm_a", "k_norm_a", "q_norm_b", "k_norm_b", "sink_b",
         "norm_ffn2", "norm_ple")
WEIGHTS = ("rel_bias", "norm_ffn1", "ffn1_w_in", "ffn1_w_out", "norm_mix", "w_qkv", "q_norm_a", "k_norm_a",
           "q_norm_b", "k_norm_b", "sink_b", "w_o", "norm_ffn2", "ffn2_w_in", "ffn2_w_out", "norm_ple",
           "w_ple_gate", "w_ple_proj")
SMALL_ROWS = 96


def _params(*sem):
    return pltpu.CompilerParams(dimension_semantics=sem, vmem_limit_bytes=VMEM_LIMIT)


def _dot(a, b):
    return jnp.dot(a, b, preferred_element_type=F32)


def _dot_nt(a, b):
    return lax.dot_general(a, b, (((1,), (1,)), ((), ())), preferred_element_type=F32)


def _dot_tn(a, b):
    return lax.dot_general(a, b, (((0,), (0,)), ((), ())), preferred_element_type=F32)


def _sigmoid(x):
    return 1.0 / (1.0 + jnp.exp(-x))


def _rstd(xv):
    return lax.rsqrt(jnp.mean(xv * xv, axis=-1, keepdims=True) + EPS)


def _norm_bwd(dh, xv, gv):
    r = _rstd(xv)
    xn = xv * r
    dg = jnp.sum(dh * xn, axis=0, keepdims=True)
    dxn = dh * gv
    dx = r * (dxn - xn * jnp.mean(dxn * xn, axis=-1, keepdims=True))
    return dx, dg


def _lo_mask(shape):
    return lax.broadcasted_iota(jnp.int32, shape, len(shape) - 1) < HEAD_DIM


def _half_sum(t, lo):
    s0 = jnp.sum(jnp.where(lo, t, 0.0), axis=1, keepdims=True)
    s1 = jnp.sum(jnp.where(lo, 0.0, t), axis=1, keepdims=True)
    return jnp.where(lo, s0, s1)


FFN_PARTS = 2


def _ffn_weight_specs(f, nj, D, C):
    return [pl.BlockSpec((None, None, D, C), lambda i, j: (j, f, 0, 0)),
            pl.BlockSpec((None, None, D, C), lambda i, j: (j + nj, f, 0, 0)),
            pl.BlockSpec((2, C // 2, D), lambda i, j: (j, f, 0))]


def _with_dep(body, dep, in_specs, args):
    if dep is None:
        return body, in_specs, args

    def body_after(dep_ref, *refs):
        body(*refs)

    return body_after, [pl.BlockSpec(memory_space=pl.ANY)] + in_specs, [dep] + args


def _ffn_fwd(x, g, ga, gb, f, tm, dep=None):
    T, D = x.shape
    nj, C = ga.shape[0] // 2, ga.shape[3]

    def body(x_ref, g_ref, wg_ref, wu_ref, wo_ref, xo_ref, h_ref, zg_ref, zu_ref, s_ref, h_scr, acc):
        j = pl.program_id(1)

        @pl.when(j == 0)
        def _():
            xv = x_ref[...]
            hb = (xv * _rstd(xv) * g_ref[...]).astype(BF16)
            h_scr[...] = hb
            h_ref[...] = hb
            acc[...] = jnp.zeros_like(acc)

        wo = wo_ref[...].reshape(C, D)
        for part in range(FFN_PARTS):
            sl = pl.ds(part * (tm // FFN_PARTS), tm // FFN_PARTS)
            hb = h_scr[sl, :]
            gt = _dot(hb, wg_ref[...])
            up = _dot(hb, wu_ref[...])
            s = (gt * _sigmoid(gt) * up).astype(BF16)
            zg_ref[sl, :] = gt.astype(BF16)
            zu_ref[sl, :] = up.astype(BF16)
            s_ref[sl, :] = s
            acc[sl, :] += _dot(s, wo)

        @pl.when(j == nj - 1)
        def _():
            xo_ref[...] = x_ref[...] + 0.5 * acc[...]

    tok = pl.BlockSpec((tm, D), lambda i, j: (i, 0))
    chunk = pl.BlockSpec((None, tm, C), lambda i, j: (j, i, 0))
    in_specs = [tok, pl.BlockSpec((1, D), lambda i, j: (0, 0))] + _ffn_weight_specs(f, nj, D, C)
    body, in_specs, args = _with_dep(body, dep, in_specs, [x, g, ga, ga, gb])
    return pl.pallas_call(
        body, name="ffn_fwd", grid=(T // tm, nj),
        in_specs=in_specs,
        out_specs=[tok, tok, chunk, chunk, chunk],
        out_shape=[jax.ShapeDtypeStruct((T, D), F32), jax.ShapeDtypeStruct((T, D), BF16),
                   jax.ShapeDtypeStruct((nj, T, C), BF16), jax.ShapeDtypeStruct((nj, T, C), BF16),
                   jax.ShapeDtypeStruct((nj, T, C), BF16)],
        scratch_shapes=[pltpu.VMEM((tm, D), BF16), pltpu.VMEM((tm, D), F32)],
        compiler_params=_params("parallel", "arbitrary"),
    )(*args)


def _ffn_bwd(dxo, x, g, zg, zu, ga, gb, f, tm, dep=None):
    T, D = x.shape
    nj, C = ga.shape[0] // 2, ga.shape[3]

    def body(dxo_ref, x_ref, g_ref, zg_ref, zu_ref, wg_ref, wu_ref, wo_ref,
             dx_ref, dy_ref, dzg_ref, dzu_ref, dgn_ref, dy_scr, acc):
        i, j = pl.program_id(0), pl.program_id(1)

        @pl.when(j == 0)
        def _():
            dyb = (0.5 * dxo_ref[...]).astype(BF16)
            dy_scr[...] = dyb
            dy_ref[...] = dyb
            acc[...] = jnp.zeros_like(acc)

        wo = wo_ref[...].reshape(C, D)
        for part in range(FFN_PARTS):
            sl = pl.ds(part * (tm // FFN_PARTS), tm // FFN_PARTS)
            ds = _dot_nt(dy_scr[sl, :], wo)
            gt = zg_ref[sl, :].astype(F32)
            up = zu_ref[sl, :].astype(F32)
            sg = _sigmoid(gt)
            dgt = (ds * up * (sg * (1.0 + gt * (1.0 - sg)))).astype(BF16)
            dup = (ds * (gt * sg)).astype(BF16)
            dzg_ref[sl, :] = dgt
            dzu_ref[sl, :] = dup
            acc[sl, :] += _dot_nt(dgt, wg_ref[...]) + _dot_nt(dup, wu_ref[...])

        @pl.when(j == nj - 1)
        def _():
            dx, dg = _norm_bwd(acc[...], x_ref[...], g_ref[...])
            dx_ref[...] = dxo_ref[...] + dx

            @pl.when(i == 0)
            def _():
                dgn_ref[...] = dg

            @pl.when(i > 0)
            def _():
                dgn_ref[...] += dg

    tok = pl.BlockSpec((tm, D), lambda i, j: (i, 0))
    chunk = pl.BlockSpec((None, tm, C), lambda i, j: (j, i, 0))
    row = pl.BlockSpec((1, D), lambda i, j: (0, 0))
    in_specs = [tok, tok, row, chunk, chunk] + _ffn_weight_specs(f, nj, D, C)
    body, in_specs, args = _with_dep(body, dep, in_specs, [dxo, x, g, zg, zu, ga, ga, gb])
    return pl.pallas_call(
        body, name="ffn_bwd", grid=(T // tm, nj),
        in_specs=in_specs,
        out_specs=[tok, tok, chunk, chunk, row],
        out_shape=[jax.ShapeDtypeStruct((T, D), F32), jax.ShapeDtypeStruct((T, D), BF16),
                   jax.ShapeDtypeStruct((nj, T, C), BF16), jax.ShapeDtypeStruct((nj, T, C), BF16),
                   jax.ShapeDtypeStruct((1, D), F32)],
        scratch_shapes=[pltpu.VMEM((tm, D), BF16), pltpu.VMEM((tm, D), F32)],
        compiler_params=_params("arbitrary", "arbitrary"),
    )(*args)


def _ffn_bwd_dz(dxo, zg, zu, gb, f, tm, dep=None):
    T, D = dxo.shape
    nj, C = zg.shape[0], zg.shape[2]

    def body(dxo_ref, zg_ref, zu_ref, wo_ref, dy_ref, dzg_ref, dzu_ref, dy_scr):
        @pl.when(pl.program_id(1) == 0)
        def _():
            dyb = (0.5 * dxo_ref[...]).astype(BF16)
            dy_scr[...] = dyb
            dy_ref[...] = dyb

        wo = wo_ref[...].reshape(C, D)
        for part in range(FFN_PARTS):
            sl = pl.ds(part * (tm // FFN_PARTS), tm // FFN_PARTS)
            ds = _dot_nt(dy_scr[sl, :], wo)
            gt = zg_ref[sl, :].astype(F32)
            up = zu_ref[sl, :].astype(F32)
            sg = _sigmoid(gt)
            dzg_ref[sl, :] = (ds * up * (sg * (1.0 + gt * (1.0 - sg)))).astype(BF16)
            dzu_ref[sl, :] = (ds * (gt * sg)).astype(BF16)

    tok = pl.BlockSpec((tm, D), lambda i, j: (i, 0))
    chunk = pl.BlockSpec((None, tm, C), lambda i, j: (j, i, 0))
    in_specs = [tok, chunk, chunk, _ffn_weight_specs(f, nj, D, C)[2]]
    body, in_specs, args = _with_dep(body, dep, in_specs, [dxo, zg, zu, gb])
    return pl.pallas_call(
        body, name="ffn_bwd_dz", grid=(T // tm, nj),
        in_specs=in_specs, out_specs=[tok, chunk, chunk],
        out_shape=[jax.ShapeDtypeStruct((T, D), BF16), jax.ShapeDtypeStruct((nj, T, C), BF16),
                   jax.ShapeDtypeStruct((nj, T, C), BF16)],
        scratch_shapes=[pltpu.VMEM((tm, D), BF16)],
        compiler_params=_params("parallel", "arbitrary"),
    )(*args)


def _ffn_bwd_dx(dxo, x, g, dzg, dzu, ga, f, tm, dep=None):
    T, D = x.shape
    nj, C = ga.shape[0] // 2, ga.shape[3]

    def body(dxo_ref, x_ref, g_ref, dzg_ref, dzu_ref, wg_ref, wu_ref, dx_ref, dgn_ref, acc):
        i, j = pl.program_id(0), pl.program_id(1)

        @pl.when(j == 0)
        def _():
            acc[...] = jnp.zeros_like(acc)

        acc[...] += _dot_nt(dzg_ref[...], wg_ref[...]) + _dot_nt(dzu_ref[...], wu_ref[...])

        @pl.when(j == nj - 1)
        def _():
            dx, dg = _norm_bwd(acc[...], x_ref[...], g_ref[...])
            dx_ref[...] = dxo_ref[...] + dx

            @pl.when(i == 0)
            def _():
                dgn_ref[...] = dg

            @pl.when(i > 0)
            def _():
                dgn_ref[...] += dg

    tok = pl.BlockSpec((tm, D), lambda i, j: (i, 0))
    chunk = pl.BlockSpec((None, tm, C), lambda i, j: (j, i, 0))
    row = pl.BlockSpec((1, D), lambda i, j: (0, 0))
    in_specs = [tok, tok, row, chunk, chunk] + _ffn_weight_specs(f, nj, D, C)[:2]
    body, in_specs, args = _with_dep(body, dep, in_specs, [dxo, x, g, dzg, dzu, ga, ga])
    return pl.pallas_call(
        body, name="ffn_bwd_dx", grid=(T // tm, nj),
        in_specs=in_specs, out_specs=[tok, row],
        out_shape=[jax.ShapeDtypeStruct((T, D), F32), jax.ShapeDtypeStruct((1, D), F32)],
        scratch_shapes=[pltpu.VMEM((tm, D), F32)],
        compiler_params=_params("arbitrary", "arbitrary"),
    )(*args)


def _ffn_dw(h, dzg, dzu, s, dy, tk, dep=None):
    T, D = h.shape
    nj, C = s.shape[0], s.shape[2]
    nk = T // tk

    def body(h_ref, dzg_ref, dzu_ref, s_ref, dy_ref, dwin_ref, dwo_ref, ag, au, ao):
        k = pl.program_id(1)

        @pl.when(k == 0)
        def _():
            ag[...] = jnp.zeros_like(ag)
            au[...] = jnp.zeros_like(au)
            ao[...] = jnp.zeros_like(ao)

        hb = h_ref[...]
        ag[...] += _dot_tn(hb, dzg_ref[...])
        au[...] += _dot_tn(hb, dzu_ref[...])
        ao[...] += _dot_tn(s_ref[...], dy_ref[...])

        @pl.when(k == nk - 1)
        def _():
            dwin_ref[0] = ag[...].astype(BF16)
            dwin_ref[1] = au[...].astype(BF16)
            dwo_ref[...] = ao[...].astype(BF16)

    tok = pl.BlockSpec((tk, D), lambda j, k: (k, 0))
    chunk = pl.BlockSpec((None, tk, C), lambda j, k: (j, k, 0))
    body, in_specs, args = _with_dep(body, dep, [tok, chunk, chunk, chunk, tok], [h, dzg, dzu, s, dy])
    dwin, dwo = pl.pallas_call(
        body, name="ffn_dw", grid=(nj, nk),
        in_specs=in_specs,
        out_specs=[pl.BlockSpec((2, None, D, C), lambda j, k: (0, j, 0, 0)),
                   pl.BlockSpec((None, C, D), lambda j, k: (j, 0, 0))],
        out_shape=[jax.ShapeDtypeStruct((2, nj, D, C), BF16), jax.ShapeDtypeStruct((nj, C, D), BF16)],
        scratch_shapes=[pltpu.VMEM((D, C), F32), pltpu.VMEM((D, C), F32), pltpu.VMEM((C, D), F32)],
        compiler_params=_params("parallel", "arbitrary"),
    )(*args)
    return dwin.reshape(2 * nj, D, C), dwo


def _matmul_tn(a, b, tn, tk):
    T, Ka = a.shape
    N = b.shape[1]
    nk = T // tk

    def body(a_ref, b_ref, o_ref, acc):
        k = pl.program_id(1)

        @pl.when(k == 0)
        def _():
            acc[...] = jnp.zeros_like(acc)

        acc[...] += _dot_tn(a_ref[...], b_ref[...])

        @pl.when(k == nk - 1)
        def _():
            o_ref[...] = acc[...].astype(BF16)

    return pl.pallas_call(
        body, name="matmul_tn", grid=(N // tn, nk),
        in_specs=[pl.BlockSpec((tk, Ka), lambda n, k: (k, 0)), pl.BlockSpec((tk, tn), lambda n, k: (k, n))],
        out_specs=pl.BlockSpec((Ka, tn), lambda n, k: (0, n)),
        out_shape=jax.ShapeDtypeStruct((Ka, N), BF16),
        scratch_shapes=[pltpu.VMEM((Ka, tn), F32)],
        compiler_params=_params("parallel", "arbitrary"),
    )(a, b)


def _qkv_fwd(x, g, w, tm):
    T, D = x.shape
    N = w.shape[1]

    def body(x_ref, g_ref, w_ref, o_ref, h_ref):
        xv = x_ref[...]
        hb = (xv * _rstd(xv) * g_ref[...]).astype(BF16)
        h_ref[...] = hb
        o_ref[...] = _dot(hb, w_ref[...])

    return pl.pallas_call(
        body, name="qkv_fwd", grid=(T // tm,),
        in_specs=[pl.BlockSpec((tm, D), lambda i: (i, 0)), pl.BlockSpec((1, D), lambda i: (0, 0)),
                  pl.BlockSpec((D, N), lambda i: (0, 0))],
        out_specs=[pl.BlockSpec((tm, N), lambda i: (i, 0)), pl.BlockSpec((tm, D), lambda i: (i, 0))],
        out_shape=[jax.ShapeDtypeStruct((T, N), F32), jax.ShapeDtypeStruct((T, D), BF16)],
        compiler_params=_params("parallel"),
    )(x, g, w)


DILS = tuple(d for _, d in DILATED)


def _spread_specs(tm, T, dtype):
    specs = [pl.BlockSpec((4, d, tm // d, PAIR), lambda i: (0, 0, i, 0)) for d in DILS]
    shapes = [jax.ShapeDtypeStruct((4, d, T // d, PAIR), dtype) for d in DILS]
    return specs, shapes


def _spread(tile, y, outs, c, dtype):
    tm = y.shape[0]
    tile[...] = y
    for out, d in zip(outs, DILS):
        for r in range(d):
            out[c, r] = tile[pl.ds(r, tm // d, stride=d), :].astype(dtype)


def _collect(tile, ins, c):
    tm = tile.shape[0]
    first = True
    for ref, d in zip(ins, DILS):
        for r in range(d):
            rows = pl.ds(r, tm // d, stride=d) if d > 1 else pl.ds(0, tm)
            part = ref[c, r].astype(F32)
            tile[rows, :] = part if first else tile[rows, :] + part
        first = False
    return tile[...]


def _attn_prep(qkv, gains2, tm):
    T = qkv.shape[0]
    scale = HEAD_DIM ** -0.5
    n = len(DILS)

    def body(qkv_ref, g_ref, qb_ref, kb_ref, vb_ref, *rest):
        outs, tile = rest[:-1], rest[-1]
        lo = _lo_mask((tm, PAIR))

        def spread(kind, c, y):
            _spread(tile, y, outs[kind * n:(kind + 1) * n], c, BF16)

        def normed(c, gi, mult):
            xv = qkv_ref[:, c * PAIR:(c + 1) * PAIR]
            r = lax.rsqrt(_half_sum(xv * xv, lo) * (1.0 / HEAD_DIM) + EPS)
            y = xv * r * g_ref[gi:gi + 1, :]
            return y * mult if mult != 1.0 else y

        def both_halves(v):
            sw = pltpu.roll(v, HEAD_DIM, 1)
            return jnp.where(lo, v, sw), jnp.where(lo, sw, v)

        for c in range(4):
            spread(0, c, normed(c, 0, scale))
            spread(1, c, normed(4 + c, 1, 1.0))
            spread(2, c, qkv_ref[:, (8 + c) * PAIR:(9 + c) * PAIR])
            qb_ref[c] = normed(12 + c, 2, scale).astype(BF16)
        k0, k1 = both_halves(normed(16, 3, 1.0))
        kb_ref[0] = k0.astype(BF16)
        kb_ref[1] = k1.astype(BF16)
        v0, v1 = both_halves(qkv_ref[:, 17 * PAIR:18 * PAIR])
        vb_ref[0] = v0.astype(BF16)
        vb_ref[1] = v1.astype(BF16)

    four = pl.BlockSpec((4, tm, PAIR), lambda i: (0, i, 0))
    two = pl.BlockSpec((2, tm, PAIR), lambda i: (0, i, 0))
    s4 = jax.ShapeDtypeStruct((4, T, PAIR), BF16)
    s2 = jax.ShapeDtypeStruct((2, T, PAIR), BF16)
    specs, shapes = _spread_specs(tm, T, BF16)
    res = pl.pallas_call(
        body, name="attn_prep", grid=(T // tm,),
        in_specs=[pl.BlockSpec((tm, qkv.shape[1]), lambda i: (i, 0)), pl.BlockSpec((4, PAIR), lambda i: (0, 0))],
        out_specs=[four, two, two] + specs * 3,
        out_shape=[s4, s2, s2] + shapes * 3,
        scratch_shapes=[pltpu.VMEM((tm, PAIR), F32)],
        compiler_params=_params("parallel"),
    )(qkv, gains2)
    qb, kb, vb = res[:3]
    per_d = [tuple(res[3 + kind * n + di].reshape(4 * d, T // d, PAIR) for kind in range(3))
             for di, d in enumerate(DILS)]
    return qb, kb, vb, per_d


def _loop_blocks(nb, body, init, per_iter):
    u = math.gcd(nb, per_iter)

    def outer(i, carry):
        for k in range(u):
            carry = body(i * u + k, carry)
        return carry

    return lax.fori_loop(0, nb // u, outer, init)


def _key_window(b, nb, L, R, W):
    start = pl.multiple_of(jnp.clip(b * BQ - R, 0, L - W), HEAD_DIM)
    return start, jnp.where(b == 0, 1, jnp.where(b == nb - 1, 2, 0))


def _stack_heads(v, lo):
    z = jnp.zeros_like(v)
    return jnp.concatenate([jnp.where(lo, v, z), jnp.where(lo, z, v)], axis=0)


def _unstack_heads(v2, lo):
    return jnp.where(lo, v2[:BQ], v2[BQ:])


def _row_vector(v, lo):
    r = lax.broadcasted_iota(jnp.int32, (BQ, PAIR), 0)
    ln = lax.broadcasted_iota(jnp.int32, (BQ, PAIR), 1)
    diag = (ln % HEAD_DIM) == (r % HEAD_DIM)
    top = jnp.sum(jnp.where(diag & (r < HEAD_DIM), v, 0.0), axis=0, keepdims=True)
    bot = jnp.sum(jnp.where(diag & (r >= HEAD_DIM), v, 0.0), axis=0, keepdims=True)
    top8, bot8 = jnp.broadcast_to(top, (8, PAIR)), jnp.broadcast_to(bot, (8, PAIR))
    lo8 = _lo_mask((8, PAIR))
    head0 = jnp.where(lo8, top8, pltpu.roll(bot8, HEAD_DIM, 1))
    head1 = jnp.where(lo8, pltpu.roll(top8, HEAD_DIM, 1), bot8)
    return jnp.concatenate([head0, head1], axis=1)[:1]


def _units_per_step(nb, pairs_per_kv):
    return max(1, 16 // nb) if pairs_per_kv == 1 else 1


def _attn_fwd(q, kp, vp, bias4, sink, R, pairs_per_kv, pairs_per_bias):
    N, L, _ = q.shape
    W = BQ + 2 * R
    nb = L // BQ
    assert L >= W and nb >= 2
    G = _units_per_step(nb, pairs_per_kv)

    def body(sink_ref, q_ref, k_ref, v_ref, bias_ref, o_ref, lse_ref):
        n = pl.program_id(0)
        lo_q = _lo_mask((BQ, PAIR))
        first = lax.broadcasted_iota(jnp.int32, (2 * BQ, 1), 0) < BQ

        def blk(f, carry):
            g, b = f // nb, f % nb
            u = n * G + g
            sk = jnp.where(first, sink_ref[2 * u], sink_ref[2 * u + 1])
            q0 = pl.multiple_of(b * BQ, BQ)
            q2 = _stack_heads(q_ref[g, pl.ds(q0, BQ), :], lo_q)
            k0, variant = _key_window(b, nb, L, R, W)
            kw = k_ref[g, pl.ds(k0, W), :]
            vw = v_ref[g, pl.ds(k0, W), :]
            s = _dot_nt(q2, kw) + bias_ref[variant]
            m = jnp.maximum(jnp.max(s, axis=1, keepdims=True), sk)
            p = jnp.exp(s - m)
            l = jnp.sum(p, axis=1, keepdims=True) + jnp.exp(sk - m)
            o2 = _dot(p.astype(BF16), vw) / l
            o_ref[g, pl.ds(q0, BQ), :] = _unstack_heads(o2, lo_q)
            lse_ref[g, pl.ds(q0, BQ), :] = _unstack_heads(jnp.broadcast_to(m + jnp.log(l), (2 * BQ, PAIR)), lo_q)
            return carry

        _loop_blocks(G * nb, blk, 0, 16)

    qspec = pl.BlockSpec((G, L, PAIR), lambda n: (n, 0, 0))
    kspec = pl.BlockSpec((G, L, PAIR), lambda n: (n // pairs_per_kv, 0, 0))
    return pl.pallas_call(
        body, name="attn_fwd", grid=(N // G,),
        in_specs=[pl.BlockSpec(memory_space=pltpu.SMEM), qspec, kspec, kspec,
                  pl.BlockSpec((None, 3, 2 * BQ, W), lambda n: (n * G // pairs_per_bias, 0, 0, 0))],
        out_specs=[qspec, qspec],
        out_shape=[jax.ShapeDtypeStruct((N, L, PAIR), F32), jax.ShapeDtypeStruct((N, L, PAIR), F32)],
        compiler_params=_params("parallel"),
    )(sink, q, kp, vp, bias4)


def _attn_bwd(q, kp, vp, bias4t, sink, o, lse, do, R, pairs_per_kv, pairs_per_bias):
    N, L, _ = q.shape
    Nk = kp.shape[0]
    Pb = bias4t.shape[0]
    W = BQ + 2 * R
    nb = L // BQ
    assert L >= W and nb >= 2
    G = _units_per_step(nb, pairs_per_kv)

    def body(sink_ref, q_ref, k_ref, v_ref, bias_ref, o_ref, lse_ref, do_ref,
             dq_ref, dk_ref, dv_ref, dbias_ref, dsink_ref, dk_acc, dv_acc):
        n = pl.program_id(0)
        lo_q = _lo_mask((BQ, PAIR))
        first = lax.broadcasted_iota(jnp.int32, (1, 2 * BQ), 1) < BQ
        dsink_ref[...] = jnp.zeros_like(dsink_ref)

        @pl.when(n % pairs_per_kv == 0)
        def _():
            dk_acc[...] = jnp.zeros_like(dk_acc)
            dv_acc[...] = jnp.zeros_like(dv_acc)

        @pl.when((n * G) % pairs_per_bias == 0)
        def _():
            dbias_ref[...] = jnp.zeros_like(dbias_ref)

        def blk(f, carry):
            g, b = f // nb, f % nb
            u = n * G + g
            sk = jnp.where(first, sink_ref[2 * u], sink_ref[2 * u + 1])
            q0 = pl.multiple_of(b * BQ, BQ)
            q2 = _stack_heads(q_ref[g, pl.ds(q0, BQ), :], lo_q)
            k0, variant = _key_window(b, nb, L, R, W)
            kw = k_ref[g, pl.ds(k0, W), :]
            vw = v_ref[g, pl.ds(k0, W), :]
            dov = do_ref[g, pl.ds(q0, BQ), :]
            lse = _row_vector(lse_ref[g, pl.ds(q0, BQ), :], lo_q)
            delta = _row_vector(_half_sum(dov.astype(F32) * o_ref[g, pl.ds(q0, BQ), :], lo_q), lo_q)
            do2 = _stack_heads(dov.astype(BF16), lo_q)
            st = _dot_nt(kw, q2) + bias_ref[variant]
            pt = jnp.exp(st - lse)
            dst = pt * (_dot_nt(vw, do2) - delta)
            dstb = dst.astype(BF16)
            dbias_ref[variant] += dst
            dk_acc[g, pl.ds(k0, W), :] += _dot(dstb, q2)
            dv_acc[g, pl.ds(k0, W), :] += _dot(pt.astype(BF16), do2)
            dq_ref[g, pl.ds(q0, BQ), :] = _unstack_heads(_dot_tn(dstb, kw), lo_q).astype(BF16)
            dsink_ref[g, pl.ds(0, 1), :] -= jnp.exp(sk - lse) * delta
            return carry

        _loop_blocks(G * nb, blk, 0, 8)
        dk_ref[...] = dk_acc[...].astype(BF16)
        dv_ref[...] = dv_acc[...].astype(BF16)

    qspec = pl.BlockSpec((G, L, PAIR), lambda n: (n, 0, 0))
    kspec = pl.BlockSpec((G, L, PAIR), lambda n: (n // pairs_per_kv, 0, 0))
    return pl.pallas_call(
        body, name="attn_bwd", grid=(N // G,),
        in_specs=[pl.BlockSpec(memory_space=pltpu.SMEM), qspec, kspec, kspec,
                  pl.BlockSpec((None, 3, W, 2 * BQ), lambda n: (n * G // pairs_per_bias, 0, 0, 0)),
                  qspec, qspec, qspec],
        out_specs=[qspec, kspec, kspec,
                   pl.BlockSpec((None, 3, W, 2 * BQ), lambda n: (n * G // pairs_per_bias, 0, 0, 0)),
                   pl.BlockSpec((G, 8, 2 * BQ), lambda n: (n, 0, 0))],
        out_shape=[jax.ShapeDtypeStruct((N, L, PAIR), BF16),
                   jax.ShapeDtypeStruct((Nk, L, PAIR), BF16),
                   jax.ShapeDtypeStruct((Nk, L, PAIR), BF16),
                   jax.ShapeDtypeStruct((Pb, 3, W, 2 * BQ), F32),
                   jax.ShapeDtypeStruct((N, 8, 2 * BQ), F32)],
        scratch_shapes=[pltpu.VMEM((G, L, PAIR), F32), pltpu.VMEM((G, L, PAIR), F32)],
        compiler_params=_params("arbitrary"),
    )(sink, q, kp, vp, bias4t, o, lse, do)


def _attn_merge(branch_outs, ob, tm):
    T = ob.shape[1]
    n = len(DILS)

    def body(*refs):
        o_in, l_in, ob_ref = refs[:n], refs[n:2 * n], refs[2 * n]
        o_out, l_out, cat_ref = refs[2 * n + 1:3 * n + 1], refs[3 * n + 1:4 * n + 1], refs[4 * n + 1]
        tiles = refs[4 * n + 2:]
        for c in range(4):
            o_nat, l_nat = [], []
            for di, d in enumerate(DILS):
                for kind, (src, dst) in enumerate(((o_in[di], o_nat), (l_in[di], l_nat))):
                    tile = tiles[2 * di + kind]
                    if d == 1:
                        dst.append(src[c, 0])
                    else:
                        for r in range(d):
                            tile[pl.ds(r, tm // d, stride=d), :] = src[c, r]
                        dst.append(tile[...])
            m = functools.reduce(jnp.maximum, l_nat)
            ws = [jnp.exp(l - m) for l in l_nat]
            z = sum(ws)
            o = sum(w * t for w, t in zip(ws, o_nat)) / z
            cat_ref[:, c * PAIR:(c + 1) * PAIR] = o.astype(BF16)
            cat_ref[:, (4 + c) * PAIR:(5 + c) * PAIR] = ob_ref[c].astype(BF16)
            _spread(tiles[0], o, o_out, c, F32)
            _spread(tiles[1], m + jnp.log(z), l_out, c, F32)

    specs, shapes = _spread_specs(tm, T, F32)
    four = pl.BlockSpec((4, tm, PAIR), lambda i: (0, i, 0))
    o_views = [o.reshape(4, d, T // d, PAIR) for (o, _), d in zip(branch_outs, DILS)]
    l_views = [l.reshape(4, d, T // d, PAIR) for (_, l), d in zip(branch_outs, DILS)]
    res = pl.pallas_call(
        body, name="attn_merge", grid=(T // tm,),
        in_specs=specs + specs + [four],
        out_specs=specs + specs + [pl.BlockSpec((tm, 8 * PAIR), lambda i: (i, 0))],
        out_shape=shapes + shapes + [jax.ShapeDtypeStruct((T, 8 * PAIR), BF16)],
        scratch_shapes=[pltpu.VMEM((tm, PAIR), F32)] * (2 * n),
        compiler_params=_params("parallel"),
    )(*o_views, *l_views, ob)
    merged = [(res[di].reshape(4 * d, T // d, PAIR), res[n + di].reshape(4 * d, T // d, PAIR))
              for di, d in enumerate(DILS)]
    return merged, res[2 * n]


def _weight_arg(w, blk):
    if blk is None:
        return pl.BlockSpec(w.shape, lambda i: (0, 0)), (lambda ref: ref[...])
    D = w.shape[2]
    return (pl.BlockSpec((N_DEV, 128, D), lambda i: (0, blk, 0)),
            lambda ref: ref[...].reshape(N_DEV * 128, D))


def _oproj_fwd(x, o_cat, w, blk, tm):
    T, D = x.shape
    wspec, wload = _weight_arg(w, blk)

    def body(x_ref, o_ref, w_ref, out_ref):
        out_ref[...] = x_ref[...] + _dot(o_ref[...], wload(w_ref))

    tok = pl.BlockSpec((tm, D), lambda i: (i, 0))
    return pl.pallas_call(
        body, name="oproj_fwd", grid=(T // tm,),
        in_specs=[tok, pl.BlockSpec((tm, o_cat.shape[1]), lambda i: (i, 0)), wspec],
        out_specs=tok, out_shape=jax.ShapeDtypeStruct((T, D), F32),
        compiler_params=_params("parallel"),
    )(x, o_cat, w)


def _oproj_bwd(dx, w, blk, tm, dep=None):
    T, D = dx.shape
    wspec, wload = _weight_arg(w, blk)

    def body(dx_ref, w_ref, dxb_ref, dob_ref, *rest):
        doa_refs, tile = rest[:-1], rest[-1]
        db = dx_ref[...].astype(BF16)
        dxb_ref[...] = db
        do = _dot_nt(db, wload(w_ref))
        for c in range(4):
            _spread(tile, do[:, c * PAIR:(c + 1) * PAIR], doa_refs, c, BF16)
            dob_ref[c] = do[:, (4 + c) * PAIR:(5 + c) * PAIR].astype(BF16)

    tok = pl.BlockSpec((tm, D), lambda i: (i, 0))
    specs, shapes = _spread_specs(tm, T, BF16)
    body, in_specs, args = _with_dep(body, dep, [tok, wspec], [dx, w])
    res = pl.pallas_call(
        body, name="oproj_bwd", grid=(T // tm,),
        in_specs=in_specs,
        out_specs=[tok, pl.BlockSpec((4, tm, PAIR), lambda i: (0, i, 0))] + specs,
        out_shape=[jax.ShapeDtypeStruct((T, D), BF16), jax.ShapeDtypeStruct((4, T, PAIR), BF16)] + shapes,
        scratch_shapes=[pltpu.VMEM((tm, PAIR), F32)],
        compiler_params=_params("parallel"),
    )(*args)
    return res[0], res[1], [t.reshape(4 * d, T // d, PAIR) for t, d in zip(res[2:], DILS)]


def _attn_post(qkv, gains2, dqa, dka, dva, dqb, dkb, dvb, tm):
    T, NQ = qkv.shape
    scale = HEAD_DIM ** -0.5

    n = len(DILS)

    def body(qkv_ref, g_ref, *rest):
        dq_refs, dk_refs, dv_refs = rest[:n], rest[n:2 * n], rest[2 * n:3 * n]
        qb_ref, kb_ref, vb_ref, out_ref, dg_ref, tile = rest[3 * n:]
        lo = _lo_mask((tm, PAIR))

        @pl.when(pl.program_id(0) == 0)
        def _():
            dg_ref[...] = jnp.zeros_like(dg_ref)

        def norm_bwd(c, gi, dy):
            xv = qkv_ref[:, c * PAIR:(c + 1) * PAIR]
            r = lax.rsqrt(_half_sum(xv * xv, lo) * (1.0 / HEAD_DIM) + EPS)
            xn = xv * r
            dg_ref[gi:gi + 1, :] += jnp.sum(dy * xn, axis=0, keepdims=True)
            dxn = dy * g_ref[gi:gi + 1, :]
            dx = r * (dxn - xn * (_half_sum(dxn * xn, lo) * (1.0 / HEAD_DIM)))
            out_ref[:, c * PAIR:(c + 1) * PAIR] = dx.astype(BF16)

        def fold(v):
            return v + pltpu.roll(v, HEAD_DIM, 1)

        for c in range(4):
            norm_bwd(c, 0, _collect(tile, dq_refs, c) * scale)
            norm_bwd(4 + c, 1, _collect(tile, dk_refs, c))
            out_ref[:, (8 + c) * PAIR:(9 + c) * PAIR] = _collect(tile, dv_refs, c).astype(BF16)
            norm_bwd(12 + c, 2, qb_ref[c].astype(F32) * scale)
        kb, vb = kb_ref[...].astype(F32), vb_ref[...].astype(F32)
        norm_bwd(16, 3, jnp.where(lo, fold(kb[0]), fold(kb[1])))
        out_ref[:, 17 * PAIR:18 * PAIR] = jnp.where(lo, fold(vb[0]), fold(vb[1])).astype(BF16)

    four = pl.BlockSpec((4, tm, PAIR), lambda i: (0, i, 0))
    two = pl.BlockSpec((2, tm, PAIR), lambda i: (0, i, 0))
    specs, _ = _spread_specs(tm, T, BF16)
    views = [t.reshape(4, d, T // d, PAIR) for group in (dqa, dka, dva) for t, d in zip(group, DILS)]
    return pl.pallas_call(
        body, name="attn_post", grid=(T // tm,),
        in_specs=[pl.BlockSpec((tm, NQ), lambda i: (i, 0)), pl.BlockSpec((4, PAIR), lambda i: (0, 0))]
        + specs * 3 + [four, two, two],
        out_specs=[pl.BlockSpec((tm, NQ), lambda i: (i, 0)), pl.BlockSpec((4, PAIR), lambda i: (0, 0))],
        out_shape=[jax.ShapeDtypeStruct((T, NQ), BF16), jax.ShapeDtypeStruct((4, PAIR), F32)],
        scratch_shapes=[pltpu.VMEM((tm, PAIR), F32)],
        compiler_params=_params("arbitrary"),
    )(qkv, gains2, *views, dqb, dkb, dvb)


def _dense_norm_bwd(dres, dz, w, blk, x, g, tm):
    T, D = x.shape
    N = dz.shape[1]
    wspec, wload = _weight_arg(w, blk)

    def body(dres_ref, dz_ref, w_ref, x_ref, g_ref, dx_ref, dgn_ref):
        i = pl.program_id(0)
        dx, dg = _norm_bwd(_dot_nt(dz_ref[...], wload(w_ref)), x_ref[...], g_ref[...])
        dx_ref[...] = dres_ref[...] + dx

        @pl.when(i == 0)
        def _():
            dgn_ref[...] = dg

        @pl.when(i > 0)
        def _():
            dgn_ref[...] += dg

    tok = pl.BlockSpec((tm, D), lambda i: (i, 0))
    row = pl.BlockSpec((1, D), lambda i: (0, 0))
    return pl.pallas_call(
        body, name="dense_norm_bwd", grid=(T // tm,),
        in_specs=[tok, pl.BlockSpec((tm, N), lambda i: (i, 0)), wspec, tok, row],
        out_specs=[tok, row],
        out_shape=[jax.ShapeDtypeStruct((T, D), F32), jax.ShapeDtypeStruct((1, D), F32)],
        compiler_params=_params("arbitrary"),
    )(dres, dz, w, x, g)


def _bias_reduce(onehot, dbm):
    Hb, K = dbm.shape

    def body(oh_ref, d_ref, out_ref):
        oh = oh_ref[...]
        d = d_ref[...]
        hi = d.astype(BF16)
        r1 = d - hi.astype(F32)
        mid = r1.astype(BF16)
        low = (r1 - mid.astype(F32)).astype(BF16)
        out_ref[...] = _dot_nt(hi, oh) + _dot_nt(mid, oh) + _dot_nt(low, oh)

    vm = pl.BlockSpec(memory_space=pltpu.VMEM)
    return pl.pallas_call(
        body, name="bias_reduce", in_specs=[vm, vm], out_specs=vm,
        out_shape=jax.ShapeDtypeStruct((Hb, N_BUCKETS), F32),
        compiler_params=pltpu.CompilerParams(vmem_limit_bytes=VMEM_LIMIT),
    )(onehot, dbm)


def _ple_fwd(x, g, wg, blk, p, wp, target, tm):
    T, D = x.shape
    P = p.shape[1]
    with_loss = target is not None
    wspec, wload = _weight_arg(wg, blk)

    def body(*refs):
        if with_loss:
            x_ref, g_ref, wg_ref, p_ref, wp_ref, t_ref, y_ref, hn_ref, gate_ref, pp_ref, pb_ref, loss_ref = refs
        else:
            x_ref, g_ref, wg_ref, p_ref, wp_ref, y_ref, hn_ref, gate_ref, pp_ref, pb_ref = refs
        i = pl.program_id(0)
        xv = x_ref[...]
        hb = (xv * _rstd(xv) * g_ref[...]).astype(BF16)
        hn_ref[...] = hb
        gate = _sigmoid(_dot(hb, wload(wg_ref)))
        pb = p_ref[...].astype(BF16)
        pb_ref[...] = pb
        pp = _dot(pb, wp_ref[...])
        gate_ref[...] = gate
        pp_ref[...] = pp
        y = xv + gate * pp
        if with_loss:
            err = y - t_ref[...]
            y_ref[...] = err * (1.0 / D)
            part = jnp.broadcast_to(0.5 * jnp.sum(jnp.sum(err * err, axis=1, keepdims=True) * (1.0 / D),
                                                  axis=0, keepdims=True), (1, 128))

            @pl.when(i == 0)
            def _():
                loss_ref[...] = part

            @pl.when(i > 0)
            def _():
                loss_ref[...] += part
        else:
            y_ref[...] = y

    tok = pl.BlockSpec((tm, D), lambda i: (i, 0))
    ptok = pl.BlockSpec((tm, P), lambda i: (i, 0))
    in_specs = [tok, pl.BlockSpec((1, D), lambda i: (0, 0)), wspec, ptok,
                pl.BlockSpec((P, D), lambda i: (0, 0))]
    out_specs = [tok, tok, tok, tok, ptok]
    out_shape = [jax.ShapeDtypeStruct((T, D), F32), jax.ShapeDtypeStruct((T, D), BF16),
                 jax.ShapeDtypeStruct((T, D), F32), jax.ShapeDtypeStruct((T, D), F32),
                 jax.ShapeDtypeStruct((T, P), BF16)]
    args = [x, g, wg, p, wp]
    if with_loss:
        in_specs.append(tok)
        out_specs.append(pl.BlockSpec((1, 128), lambda i: (0, 0)))
        out_shape.append(jax.ShapeDtypeStruct((1, 128), F32))
        args.append(target)
    return pl.pallas_call(
        body, name="ple_fwd_loss" if with_loss else "ple_fwd", grid=(T // tm,),
        in_specs=in_specs, out_specs=out_specs, out_shape=out_shape,
        compiler_params=_params("arbitrary" if with_loss else "parallel"),
    )(*args)


def _ple_bwd(dy, gate, pp, tm, dep=None):
    T, D = dy.shape

    def body(dy_ref, gate_ref, pp_ref, dgl_ref, dpp_ref):
        d = dy_ref[...]
        gt = gate_ref[...]
        dgl_ref[...] = (d * pp_ref[...] * gt * (1.0 - gt)).astype(BF16)
        dpp_ref[...] = (d * gt).astype(BF16)

    tok = pl.BlockSpec((tm, D), lambda i: (i, 0))
    body, in_specs, args = _with_dep(body, dep, [tok, tok, tok], [dy, gate, pp])
    return pl.pallas_call(
        body, name="ple_bwd", grid=(T // tm,), in_specs=in_specs, out_specs=[tok, tok],
        out_shape=[jax.ShapeDtypeStruct((T, D), BF16), jax.ShapeDtypeStruct((T, D), BF16)],
        compiler_params=_params("parallel"),
    )(*args)


def _adamw(w, g, m, v):
    shape = w.shape
    C = shape[-1]
    w2, g2, m2, v2 = (a.reshape(-1, C) for a in (w, g, m, v))
    Rn = w2.shape[0]
    tr = Rn
    for cand in (512, 352, 256):
        if Rn % cand == 0:
            tr = cand
            break
    c1 = 1.0 - ADAM_B1 ** ADAM_STEP
    c2 = 1.0 - ADAM_B2 ** ADAM_STEP

    def body(w_ref, g_ref, m_ref, v_ref, d_ref, nm_ref, nv_ref):
        gv = g_ref[...]
        mn = ADAM_B1 * m_ref[...] + (1.0 - ADAM_B1) * gv
        vn = ADAM_B2 * v_ref[...] + (1.0 - ADAM_B2) * (gv * gv)
        d_ref[...] = -ADAM_LR * ((mn / c1) / (jnp.sqrt(vn / c2) + ADAM_EPS) + ADAM_WD * w_ref[...])
        nm_ref[...] = mn
        nv_ref[...] = vn

    spec = pl.BlockSpec((tr, C), lambda i: (i, 0))
    sh = jax.ShapeDtypeStruct((Rn, C), F32)
    d, nm, nv = pl.pallas_call(
        body, name="adamw", grid=(Rn // tr,), in_specs=[spec] * 4, out_specs=[spec] * 3, out_shape=[sh] * 3,
        compiler_params=_params("parallel"),
    )(w2, g2, m2, v2)
    return d.reshape(shape), nm.reshape(shape), nv.reshape(shape)


def _my_place():
    x, y, c = lax.axis_index("x"), lax.axis_index("y"), lax.axis_index("c")
    chips = [(1 - x, y), (x, 1 - y), (1 - x, 1 - y)]
    return x, y, c, chips


def _all_gather(arrs):
    n = len(arrs)

    def body(*refs):
        x_refs, out_refs = refs[:n], refs[n:2 * n]
        send_sems, recv_sems, local_sems = refs[2 * n:]
        x, y, c, chips = _my_place()
        me, sibling = (x, y, c), (x, y, 1 - c)

        def copy(m, k, block, to, src=None):
            rows = out_refs[m].at[4 * block[0] + 2 * block[1] + block[2]]
            return pltpu.make_async_remote_copy(
                src_ref=rows if src is None else src, dst_ref=rows,
                send_sem=send_sems.at[7 * m + k], recv_sem=recv_sems.at[7 * m + k], device_id=to, device_id_type=MESH)

        mine = [pltpu.make_async_copy(x_refs[m], out_refs[m].at[4 * x + 2 * y + c], local_sems.at[m])
                for m in range(n)]
        for cp in mine:
            cp.start()
        first = []
        for m in range(n):
            first.append(copy(m, 0, me, sibling, src=x_refs[m]))
            first += [copy(m, 1 + j, me, (*chip, c), src=x_refs[m]) for j, chip in enumerate(chips)]
        for cp in first:
            cp.start()
        passed = []
        for m in range(n):
            for j, chip in enumerate(chips):
                copy(m, 1 + j, (*chip, c), me).wait_recv()
                cp = copy(m, 4 + j, (*chip, c), sibling)
                cp.start()
                passed.append(cp)
        for m in range(n):
            copy(m, 0, sibling, me).wait_recv()
            for j, chip in enumerate(chips):
                copy(m, 4 + j, (*chip, 1 - c), me).wait_recv()
        for cp in first + passed:
            cp.wait_send()
        for cp in mine:
            cp.wait()

    hbm = pl.BlockSpec(memory_space=pl.ANY)
    return pl.pallas_call(
        body, name="all_gather", in_specs=[hbm] * n, out_specs=[hbm] * n,
        out_shape=[jax.ShapeDtypeStruct((N_DEV,) + a.shape, a.dtype) for a in arrs],
        scratch_shapes=[pltpu.SemaphoreType.DMA((7 * n,)), pltpu.SemaphoreType.DMA((7 * n,)),
                        pltpu.SemaphoreType.DMA((n,))],
    )(*arrs)


def _peer(x, y, c, k):
    return (x ^ ((k >> 2) & 1), y ^ ((k >> 1) & 1), c ^ (k & 1))


HBM_SPEC = pl.BlockSpec(memory_space=pltpu.HBM)
SEM_SPEC = pl.BlockSpec(memory_space=pltpu.SEMAPHORE)


def _exchange_refs(srcs, lands, m, k, x, y, c, scatter):
    peer = _peer(x, y, c, k)
    if scatter:
        return srcs[m].at[4 * peer[0] + 2 * peer[1] + peer[2]], lands[m].at[k - 1], peer
    return srcs[m], lands[m].at[4 * x + 2 * y + c], peer


def _exchange_start(arrs, land_shapes, scatter, name):
    n = len(arrs)

    def body(*refs):
        srcs, lands = refs[:n], refs[n:2 * n]
        send_sems, recv_sems = refs[2 * n], refs[2 * n + 1]
        token = refs[-1]
        x, y, c, _ = _my_place()
        for m in range(n):
            for k in range(1, N_DEV):
                src, dst, peer = _exchange_refs(srcs, lands, m, k, x, y, c, scatter)
                pltpu.make_async_remote_copy(
                    src_ref=src, dst_ref=dst, send_sem=send_sems.at[7 * m + k - 1],
                    recv_sem=recv_sems.at[7 * m + k - 1], device_id=peer, device_id_type=MESH).start()
        token[...] = jnp.zeros_like(token)

    zones = [lax.empty(s_, a.dtype) for s_, a in zip(land_shapes, arrs)]
    outs = pl.pallas_call(
        body, name=name,
        out_shape=(pltpu.SemaphoreType.DMA((7 * n,)), pltpu.SemaphoreType.DMA((7 * n,)),
                   *[pltpu.HBM(a.shape, a.dtype) for a in arrs], *[pltpu.HBM(z.shape, z.dtype) for z in zones],
                   jax.ShapeDtypeStruct((8, 128), F32)),
        in_specs=[HBM_SPEC] * (2 * n),
        out_specs=(SEM_SPEC, SEM_SPEC, *[HBM_SPEC] * (2 * n), pl.BlockSpec(memory_space=pltpu.VMEM)),
        input_output_aliases={m: 2 + m for m in range(2 * n)},
        compiler_params=pltpu.CompilerParams(has_side_effects=pltpu.SideEffectType.DATAFLOW_SIDE_EFFECTING),
    )(*[pltpu.with_memory_space_constraint(a, pltpu.HBM) for a in arrs],
      *[pltpu.with_memory_space_constraint(z, pltpu.HBM) for z in zones])
    return outs[0], outs[1], list(outs[2:2 + n]), list(outs[2 + n:2 + 2 * n]), outs[-1]


def _exchange_wait(send_sems, recv_sems, arrs, zones, after, scatter, name):
    n = len(arrs)
    afters = list(after) if isinstance(after, (list, tuple)) else [after]

    def body(*refs):
        srcs, lands = refs[:n], refs[n:2 * n]
        send_sems, recv_sems = refs[2 * n], refs[2 * n + 1]
        x, y, c, _ = _my_place()
        for m in range(n):
            for k in range(1, N_DEV):
                src, dst, peer = _exchange_refs(srcs, lands, m, k, x, y, c, scatter)
                cp = pltpu.make_async_remote_copy(
                    src_ref=src, dst_ref=dst, send_sem=send_sems.at[7 * m + k - 1],
                    recv_sem=recv_sems.at[7 * m + k - 1], device_id=peer, device_id_type=MESH)
                cp.wait_send()
                cp.wait_recv()

    outs = pl.pallas_call(
        body, name=name,
        out_shape=tuple(pltpu.HBM(a.shape, a.dtype) for a in list(arrs) + list(zones)),
        in_specs=[HBM_SPEC] * (2 * n) + [SEM_SPEC, SEM_SPEC] + [pl.BlockSpec(memory_space=pl.ANY)] * len(afters),
        out_specs=tuple([HBM_SPEC] * (2 * n)),
        input_output_aliases={m: m for m in range(2 * n)},
        compiler_params=pltpu.CompilerParams(has_side_effects=pltpu.SideEffectType.DATAFLOW_SIDE_EFFECTING),
    )(*arrs, *zones, send_sems, recv_sems, *afters)
    return list(outs[n:])


def _sum_parts(own, parts, tr, dep=None):
    R, W = own.shape

    def body(own_ref, parts_ref, out_ref):
        acc = own_ref[...].astype(F32)
        for k in range(N_DEV - 1):
            acc = acc + parts_ref[k].astype(F32)
        out_ref[...] = acc

    in_specs = [pl.BlockSpec((tr, W), lambda i: (i, 0)), pl.BlockSpec((N_DEV - 1, tr, W), lambda i: (0, i, 0))]
    body, in_specs, args = _with_dep(body, dep, in_specs, [own, parts])
    return pl.pallas_call(
        body, name="sum_parts", grid=(R // tr,),
        in_specs=in_specs,
        out_specs=pl.BlockSpec((tr, W), lambda i: (i, 0)),
        out_shape=jax.ShapeDtypeStruct((R, W), F32),
        compiler_params=_params("parallel"),
    )(*args)


def _all_reduce_small(v, dep=None):
    Rn, Wd = v.shape

    def body(v_ref, out_ref, gat_ref, send_sems, recv_sems):
        x, y, c, _ = _my_place()
        me = 4 * x + 2 * y + c
        gat_ref[me] = v_ref[...]
        copies = []
        for k in range(1, N_DEV):
            fx, fy, fc = (k >> 2) & 1, (k >> 1) & 1, k & 1
            peer = (x ^ fx, y ^ fy, c ^ fc)
            cp = pltpu.make_async_remote_copy(
                src_ref=v_ref, dst_ref=gat_ref.at[me], send_sem=send_sems.at[k - 1], recv_sem=recv_sems.at[k - 1],
                device_id=peer, device_id_type=MESH)
            cp.start()
            copies.append(cp)
        for cp in copies:
            cp.wait_recv()
        for cp in copies:
            cp.wait_send()
        acc = gat_ref[0]
        for k in range(1, N_DEV):
            acc = acc + gat_ref[k]
        out_ref[...] = acc

    vm = pl.BlockSpec(memory_space=pltpu.VMEM)
    body, in_specs, args = _with_dep(body, dep, [vm], [v])
    return pl.pallas_call(
        body, name="all_reduce_small", in_specs=in_specs, out_specs=vm,
        out_shape=jax.ShapeDtypeStruct((Rn, Wd), F32),
        scratch_shapes=[pltpu.VMEM((N_DEV, Rn, Wd), F32), pltpu.SemaphoreType.DMA((7,)),
                        pltpu.SemaphoreType.DMA((7,))],
    )(*args)


def _t5_bucket(rel):
    half = N_BUCKETS // 2
    max_exact = half // 2
    ret = jnp.where(rel > 0, half, 0)
    n = jnp.abs(rel)
    nf = jnp.maximum(n, 1).astype(F32)
    large = max_exact + (jnp.log(nf / max_exact) / math.log(MAX_DISTANCE / max_exact)
                         * (half - max_exact)).astype(jnp.int32)
    large = jnp.minimum(large, half - 1)
    return ret + jnp.where(n < max_exact, n, large)


def _band(R, d):
    W = BQ + 2 * R
    rel = jnp.arange(W)[None, :] - R - jnp.arange(BQ)[:, None]
    return _t5_bucket(rel * d), jnp.abs(rel) <= R


def _onehot(R, d):
    bkt, in_band = _band(R, d)
    return ((bkt.reshape(1, -1) == jnp.arange(N_BUCKETS)[:, None]) & in_band.reshape(1, -1)).astype(BF16)


def _bias_expand(table_t, onehot):
    H = table_t.shape[0]
    K = onehot.shape[1]

    def body(t_ref, oh_ref, out_ref):
        oh = oh_ref[...]
        t = t_ref[...]
        hi = t.astype(BF16)
        r1 = t - hi.astype(F32)
        mid = r1.astype(BF16)
        low = (r1 - mid.astype(F32)).astype(BF16)
        marked = _dot(jnp.ones(t.shape, BF16), oh) > 0.5
        out_ref[...] = jnp.where(marked, _dot(hi, oh) + _dot(mid, oh) + _dot(low, oh), NEG)

    vm = pl.BlockSpec(memory_space=pltpu.VMEM)
    return pl.pallas_call(
        body, name="bias_expand", in_specs=[vm, vm], out_specs=vm,
        out_shape=jax.ShapeDtypeStruct((H, K), F32),
        compiler_params=pltpu.CompilerParams(vmem_limit_bytes=VMEM_LIMIT),
    )(table_t, onehot)


def _bias_matrix(table, R, d):
    return _bias_expand(table.T, _onehot(R, d)).reshape(table.shape[1], BQ, BQ + 2 * R)


def _bias_variants(base, R):
    H, _, W = base.shape
    fill = jnp.full((H, BQ, R), NEG, F32)
    first = jnp.concatenate([base[:, :, R:], fill], axis=2)
    last = jnp.concatenate([fill, base[:, :, :W - R]], axis=2)
    v = jnp.stack([base, first, last], axis=1)
    v = v.reshape(H // 2, 2, 3, BQ, W).transpose(0, 2, 1, 3, 4).reshape(H // 2, 3, 2 * BQ, W)
    return v, v.transpose(0, 1, 3, 2)


def _bias_grad(dbt, R, d):
    P, _, W, _ = dbt.shape
    dbt = dbt[:, 0].at[:, R:].add(dbt[:, 1, :W - R]).at[:, :W - R].add(dbt[:, 2, R:])
    dbm = dbt.reshape(P, W, 2, BQ).transpose(0, 2, 3, 1).reshape(2 * P, BQ * W)
    return _bias_reduce(_onehot(R, d), dbm).T


def _tile2(gain):
    return jnp.concatenate([gain, gain])


ROW_W_O, ROW_GATE, B_ROWS = 768, 896, 1024
BLK_W_O, BLK_GATE = ROW_W_O // 128, ROW_GATE // 128


def _pack_layer(wts, i):
    a = jnp.stack([wts["ffn1_w_in"][i], wts["ffn2_w_in"][i]])
    D = a.shape[1]
    b = jnp.concatenate([
        wts["ffn1_w_out"][i], wts["ffn2_w_out"][i],
        jnp.zeros((ROW_W_O - 2 * wts["ffn1_w_out"].shape[1], D), a.dtype), wts["w_o"][i], wts["w_ple_gate"][i]])
    return a, b, wts["w_qkv"][i], wts["w_ple_proj"][i]


def _unpack_layer(sums, like):
    w_in2, b1, proj, w_o, qkv, w_in1, w_out1 = sums
    n_out = like["ffn1_w_out"].shape[1]
    out = {}
    if w_in2 is not None:
        out.update(ffn2_w_in=w_in2, ffn2_w_out=b1[:n_out], w_ple_gate=b1[n_out:], w_ple_proj=proj)
    if w_o is not None:
        out.update(w_o=w_o, w_qkv=qkv)
    if w_in1 is not None:
        out.update(ffn1_w_in=w_in1, ffn1_w_out=w_out1)
    return out


def _col_sharded(g):
    return g.transpose(1, 0, 2).reshape(g.shape[1], -1)


def _to_col_shards(g):
    rows = g.shape[0]
    return g.reshape(rows, N_DEV, -1).transpose(1, 0, 2)


def _layer_weights(ga, gb, gq, gp):
    return dict(ga=ga, gb=gb, w_qkv=_col_sharded(gq), w_proj=_col_sharded(gp))


def _layer_fwd(x, p, w, sm, i, target, tm, biases, dep=None):
    ga, gb = w["ga"], w["gb"]
    saved = {}
    saved["x0"] = x
    x1, saved["h1"], saved["zg1"], saved["zu1"], saved["s1"] = _ffn_fwd(
        x, sm["norm_ffn1"][i][None], ga, gb, 0, 2 * tm, dep)
    saved["x1"] = x1
    qkv, saved["hm"] = _qkv_fwd(x1, sm["norm_mix"][i][None], w["w_qkv"], 2 * tm)
    saved["qkv"] = qkv
    gains2 = jnp.stack([_tile2(sm[k][i]) for k in ("q_norm_a", "k_norm_a", "q_norm_b", "k_norm_b")])
    saved["gains2"] = gains2
    qb, kb, vb, qkv_d = _attn_prep(qkv, gains2, 2 * tm)
    no_sink = jnp.full((8,), NEG, F32)
    branches = []
    outs = []
    for (R, d), bias, (qd, kd, vd) in zip(DILATED, biases[:3], qkv_d):
        sink = jnp.tile(no_sink, d)
        outs.append(_attn_fwd(qd, kd, vd, bias[0], sink, R, 1, d))
        branches.append((qd, kd, vd, bias, sink, R, d))
    bias_b = biases[3]
    sink_b = sm["sink_b"][i]
    ob, lb = _attn_fwd(qb, kb, vb, bias_b[0], sink_b, SWA_RADIUS, 2, 1)
    merged, o_cat = _attn_merge(outs, ob, tm)
    saved.update(branches=branches, b=(qb, kb, vb, bias_b, sink_b), merged=merged, ob=ob, lb=lb, o_cat=o_cat)
    x2 = _oproj_fwd(x1, o_cat, gb, BLK_W_O, 2 * tm)
    saved["x2"] = x2
    x3, saved["h2"], saved["zg2"], saved["zu2"], saved["s2"] = _ffn_fwd(
        x2, sm["norm_ffn2"][i][None], ga, gb, 1, 2 * tm)
    saved["x3"] = x3
    res = _ple_fwd(x3, sm["norm_ple"][i][None], gb, BLK_GATE, p, w["w_proj"], target, tm)
    y, saved["hp"], saved["gate"], saved["pp"], saved["pb"] = res[:5]
    loss = res[5] if target is not None else None
    return y, loss, saved


def _layer_bwd(dy, w, sm, i, sv, tm, dep=None, on_ready=None, on_small=None, on_last=None):
    ga, gb = w["ga"], w["gb"]
    gs = {}
    D = dy.shape[1]
    dgl, dpp = _ple_bwd(dy, sv["gate"], sv["pp"], tm, dep)
    d_gate = _matmul_tn(sv["hp"], dgl, D, 4 * tm)
    d_proj = _matmul_tn(sv["pb"], dpp, D, 4 * tm)
    dx3, gs["norm_ple"] = _dense_norm_bwd(dy, dgl, gb, BLK_GATE, sv["x3"], sm["norm_ple"][i][None], 2 * tm)
    dx2, dyb, dzg, dzu, gs["norm_ffn2"] = _ffn_bwd(dx3, sv["x2"], sm["norm_ffn2"][i][None], sv["zg2"], sv["zu2"],
                                                   ga, gb, 1, tm)
    dwin2, dwo2 = _ffn_dw(sv["h2"], dzg, dzu, sv["s2"], dyb, 4 * tm)
    half = dwo2.shape[1] // 2
    after_ffn2 = [dwin2, jnp.concatenate([dwo2.reshape(N_DEV, half, D), d_gate.reshape(N_DEV, -1, D)], axis=1),
                  _to_col_shards(d_proj)]
    token = None if on_ready is None else on_ready(0, after_ffn2)
    dx2b, do_b, do_a = _oproj_bwd(dx2, gb, BLK_W_O, tm, token)
    d_wo = _matmul_tn(sv["o_cat"], dx2b, D, 4 * tm)
    dqa, dka, dva, dbias = [], [], [], []
    for (qd, kd, vd, bias, sink, R, d), (oa, la), do_d in zip(sv["branches"], sv["merged"], do_a):
        dq, dk, dv, dbm, _ = _attn_bwd(qd, kd, vd, bias[1], sink, oa, la, do_d, R, 1, d)
        dqa.append(dq)
        dka.append(dk)
        dva.append(dv)
        dbias.append(dbm)
    qb, kb, vb, bias_b, sink_b = sv["b"]
    dqb, dkb, dvb, dbm_b, dsink = _attn_bwd(qb, kb, vb, bias_b[1], sink_b, sv["ob"], sv["lb"], do_b,
                                            SWA_RADIUS, 2, 1)
    gs["rel_bias"] = dbias + [dbm_b]
    gs["sink_b"] = jnp.sum(dsink[:, 0].reshape(-1, 2, BQ), axis=2).reshape(-1)
    dqkv, dgains2 = _attn_post(sv["qkv"], sv["gains2"], dqa, dka, dva, dqb,
                               dkb, dvb, tm)
    dgains = dgains2[:, :HEAD_DIM] + dgains2[:, HEAD_DIM:]
    for k, name in enumerate(("q_norm_a", "k_norm_a", "q_norm_b", "k_norm_b")):
        gs[name] = dgains[k]
    d_qkv = _matmul_tn(sv["hm"], dqkv, dqkv.shape[1] // 2, 4 * tm)
    after_mixer = [d_wo.reshape(N_DEV, -1, D), _to_col_shards(d_qkv)]
    token = None if on_ready is None else on_ready(1, after_mixer)
    dx1, gs["norm_mix"] = _dense_norm_bwd(dx2, dqkv, w["w_qkv"], None, sv["x1"], sm["norm_mix"][i][None], 2 * tm)
    g1 = sm["norm_ffn1"][i][None]
    if on_last is None:
        dx0, dyb, dzg, dzu, gs["norm_ffn1"] = _ffn_bwd(dx1, sv["x0"], g1, sv["zg1"], sv["zu1"], ga, gb, 0, tm, token)
        dwin1, dwo1 = _ffn_dw(sv["h1"], dzg, dzu, sv["s1"], dyb, 4 * tm)
        return dx0, (after_ffn2, after_mixer, [dwin1, dwo1.reshape(N_DEV, half, D)]), gs
    dyb, dzg, dzu = _ffn_bwd_dz(dx1, sv["zg1"], sv["zu1"], gb, 0, 2 * tm, token)
    dwin1, dwo1 = _ffn_dw(sv["h1"], dzg, dzu, sv["s1"], dyb, 4 * tm, on_small(gs))
    last = [dwin1, dwo1.reshape(N_DEV, half, D)]
    dx0, gs["norm_ffn1"] = _ffn_bwd_dx(dx1, sv["x0"], g1, dzg, dzu, ga, 0, 2 * tm, on_last(last))
    return dx0, (after_ffn2, after_mixer, last), gs


def _bias_matrices(rel_bias):
    biases = [_bias_variants(_bias_matrix(rel_bias[:, :8], R, d), R) for R, d in DILATED]
    biases.append(_bias_variants(_bias_matrix(rel_bias[:, 8:], SWA_RADIUS, 1), SWA_RADIUS))
    return biases


def _stack_small(per_layer):
    small = {}
    for k, v in per_layer.items():
        if k == "rel_bias":
            per_branch = [sum(parts) for parts in zip(*v.values())]
            drel_a = sum(_bias_grad(t, R, d) for t, (R, d) in zip(per_branch[:3], DILATED))
            small[k] = jnp.concatenate([drel_a, _bias_grad(per_branch[3], SWA_RADIUS, 1)], axis=1)
        else:
            small[k] = jnp.stack([v[i].reshape(-1) for i in sorted(v)])
    return small


TM = 512
SUM_TILES = (512, 480, 256, 128, 512, 512, 352)
LAST_GROUP = ("ffn1_w_in", "ffn1_w_out")


def _pack_small(d, extra=None):
    parts = [d[k].reshape(-1) for k in SMALL]
    if extra is not None:
        parts.append(extra.reshape(-1))
    flat = jnp.concatenate(parts)
    return jnp.pad(flat, (0, SMALL_ROWS * 128 - flat.shape[0])).reshape(SMALL_ROWS, 128)


def _unpack_small(buf, like):
    flat = buf.reshape(-1)
    out, off = {}, 0
    for k in SMALL:
        n = like[k].size
        out[k] = flat[off:off + n].reshape(like[k].shape)
        off += n
    return out, flat[off]


def kernel(x, p, rel_bias, norm_ffn1, ffn1_w_in, ffn1_w_out, norm_mix, w_qkv, q_norm_a, k_norm_a, q_norm_b, k_norm_b, sink_b, w_o, norm_ffn2, ffn2_w_in, ffn2_w_out, norm_ple, w_ple_gate, w_ple_proj, loss_target, m_rel_bias, m_norm_ffn1, m_ffn1_w_in, m_ffn1_w_out, m_norm_mix, m_w_qkv, m_q_norm_a, m_k_norm_a, m_q_norm_b, m_k_norm_b, m_sink_b, m_w_o, m_norm_ffn2, m_ffn2_w_in, m_ffn2_w_out, m_norm_ple, m_w_ple_gate, m_w_ple_proj, v_rel_bias, v_norm_ffn1, v_ffn1_w_in, v_ffn1_w_out, v_norm_mix, v_w_qkv, v_q_norm_a, v_k_norm_a, v_q_norm_b, v_k_norm_b, v_sink_b, v_w_o, v_norm_ffn2, v_ffn2_w_in, v_ffn2_w_out, v_norm_ple, v_w_ple_gate, v_w_ple_proj):
    wts = dict(rel_bias=rel_bias, norm_ffn1=norm_ffn1, ffn1_w_in=ffn1_w_in, ffn1_w_out=ffn1_w_out,
               norm_mix=norm_mix, w_qkv=w_qkv, q_norm_a=q_norm_a, k_norm_a=k_norm_a, q_norm_b=q_norm_b,
               k_norm_b=k_norm_b, sink_b=sink_b, w_o=w_o, norm_ffn2=norm_ffn2, ffn2_w_in=ffn2_w_in,
               ffn2_w_out=ffn2_w_out, norm_ple=norm_ple, w_ple_gate=w_ple_gate, w_ple_proj=w_ple_proj)
    mom = dict(rel_bias=m_rel_bias, norm_ffn1=m_norm_ffn1, ffn1_w_in=m_ffn1_w_in, ffn1_w_out=m_ffn1_w_out,
               norm_mix=m_norm_mix, w_qkv=m_w_qkv, q_norm_a=m_q_norm_a, k_norm_a=m_k_norm_a, q_norm_b=m_q_norm_b,
               k_norm_b=m_k_norm_b, sink_b=m_sink_b, w_o=m_w_o, norm_ffn2=m_norm_ffn2, ffn2_w_in=m_ffn2_w_in,
               ffn2_w_out=m_ffn2_w_out, norm_ple=m_norm_ple, w_ple_gate=m_w_ple_gate, w_ple_proj=m_w_ple_proj)
    var = dict(rel_bias=v_rel_bias, norm_ffn1=v_norm_ffn1, ffn1_w_in=v_ffn1_w_in, ffn1_w_out=v_ffn1_w_out,
               norm_mix=v_norm_mix, w_qkv=v_w_qkv, q_norm_a=v_q_norm_a, k_norm_a=v_k_norm_a, q_norm_b=v_q_norm_b,
               k_norm_b=v_k_norm_b, sink_b=v_sink_b, w_o=v_w_o, norm_ffn2=v_norm_ffn2, ffn2_w_in=v_ffn2_w_in,
               ffn2_w_out=v_ffn2_w_out, norm_ple=v_norm_ple, w_ple_gate=v_w_ple_gate, w_ple_proj=v_w_ple_proj)
    sm = {k: wts[k] for k in SMALL}
    me = 4 * lax.axis_index("x") + 2 * lax.axis_index("y") + lax.axis_index("c")
    packed = []
    for i in range(2):
        a, *rest = _pack_layer(wts, i)
        packed.append([t.astype(BF16) for t in [a.reshape(-1, a.shape[-1])] + rest])
    a_shape = (2, ffn1_w_in.shape[1], ffn1_w_in.shape[2])

    def weights_of(zones):
        return _layer_weights(zones[0].reshape((N_DEV,) + a_shape), *zones[1:])

    w0 = weights_of(_all_gather(packed[0]))
    zone_shapes = [(N_DEV,) + t.shape for t in packed[1]]
    ssem, rsem, thru, zones, token = _exchange_start(packed[1], zone_shapes, False, "gather_start")
    biases = _bias_matrices(rel_bias)
    x1, _, sv0 = _layer_fwd(x[0], p[0, 0], w0, sm, 0, None, TM, biases, dep=token)
    zones = _exchange_wait(ssem, rsem, thru, zones, x1, False, "gather_wait")
    w1 = weights_of([lax.dynamic_update_index_in_dim(z, t, me, 0) for z, t in zip(zones, packed[1])])
    dy, loss, sv1 = _layer_fwd(x1, p[1, 0], w1, sm, 1, loss_target[0], TM, biases)

    def slots_for(arrs):
        return [(N_DEV - 1,) + t.shape[1:] for t in arrs]

    held1, held = {}, {}

    def on_ready1(stage, group):
        held1[stage] = _exchange_start(group, slots_for(group), True, f"scatter1_start_{stage}")
        return held1[stage][4]

    dx1, groups1, gs1 = _layer_bwd(dy, w1, sm, 1, sv1, TM, on_ready=on_ready1)
    on_ready1(2, groups1[2])
    g1 = groups1[0] + groups1[1] + groups1[2]

    def on_ready(stage, group):
        if stage == 1:
            held["slots1"] = [t for st in (0, 1, 2)
                              for t in _exchange_wait(*held1[st][:4], group[0], True, f"scatter1_wait_{st}")]
        held[stage] = _exchange_start(group, slots_for(group), True, f"scatter_start_{stage}")
        return held[stage][4]

    def on_small(gs0):
        part = dict(gs0, norm_ffn1=jnp.zeros_like(gs1["norm_ffn1"]))
        gsmall = _stack_small({k: {0: part[k], 1: gs1[k]} for k in part})
        held["small"] = _all_reduce_small(_pack_small(gsmall, loss[0, :1]))
        return held["small"]

    def on_last(group):
        held["last"] = _exchange_start(group, slots_for(group), True, "scatter_start_2")
        return held["last"][4]

    dx, groups0, gs0 = _layer_bwd(dx1, w0, sm, 0, sv0, TM, dep=held1[2][4], on_ready=on_ready, on_small=on_small,
                                  on_last=on_last)
    last = groups0[2]
    slots0 = [_exchange_wait(*held[stage][:4], last[0], True, f"scatter_wait_{stage}") for stage in (0, 1)]

    def summed(arrs, slots, tiles, dep=None):
        return [_sum_parts(lax.dynamic_index_in_dim(t, me, 0, keepdims=False), s_, tr, dep)
                for t, s_, tr in zip(arrs, slots, tiles)]

    cover = held["last"][4]
    r1 = summed(g1, held["slots1"], SUM_TILES, cover)
    r0 = summed(groups0[0], slots0[0], SUM_TILES[:3], cover) + summed(groups0[1], slots0[1], SUM_TILES[3:5], cover)

    def update(names, layers):
        for k in names:
            grads[k] = jnp.stack([layers[0][k], layers[1][k]])
            delta[k], new_m[k], new_v[k] = _adamw(wts[k], grads[k], mom[k], var[k])

    grads, delta, new_m, new_v = {}, {}, {}, {}
    layer1 = _unpack_layer(r1, wts)
    update([k for k in BIG if k not in LAST_GROUP], [_unpack_layer(r0 + [None, None], wts), layer1])

    cover_done = [dx] + [delta[k] for k in BIG if k not in LAST_GROUP]
    slots_last = _exchange_wait(*held["last"][:4], cover_done, True, "scatter_wait_2")
    update(LAST_GROUP, [_unpack_layer([None] * 5 + summed(last, slots_last, SUM_TILES[5:]), wts), layer1])
    late = _all_reduce_small(gs0["norm_ffn1"].reshape(-1, 128), dep=slots_last[0])
    small_sum, loss_sum = _unpack_small(held["small"], sm)
    small_sum["norm_ffn1"] = small_sum["norm_ffn1"].at[0].add(late.reshape(-1))
    grads.update(small_sum)
    zeros = {k: jnp.zeros_like(wts[k]) for k in SMALL}
    ds, ms, vs = _adamw(_pack_small(wts), _pack_small(small_sum), _pack_small(mom), _pack_small(var))
    for packed, dst in ((ds, delta), (ms, new_m), (vs, new_v)):
        dst.update(_unpack_small(packed, zeros)[0])

    return (loss_sum, dx[None], *[grads[k] for k in WEIGHTS], *[delta[k] for k in WEIGHTS],
            *[new_m[k] for k in WEIGHTS], *[new_v[k] for k in WEIGHTS])
```

```python
import functools
import math

import jax
import jax.numpy as jnp
from jax import lax
from jax.experimental import pallas as pl
from jax.experimental.pallas import tpu as pltpu

F32 = jnp.float32
BF16 = jnp.bfloat16

N_DEV = 8
HEAD_DIM = 64
PAIR = 2 * HEAD_DIM
BQ = 128
N_BUCKETS = 32
MAX_DISTANCE = 1024
DILATED = ((64, 1), (64, 4), (64, 16))
SWA_RADIUS = 128
EPS = 1e-6
NEG = -1e30
ADAM_LR, ADAM_B1, ADAM_B2, ADAM_EPS, ADAM_WD, ADAM_STEP = 0.001, 0.9, 0.999, 1e-08, 0.01, 10
VMEM_LIMIT = 56 * 1024 * 1024
MESH = pl.DeviceIdType.MESH

BIG = ("ffn1_w_in", "ffn1_w_out", "w_qkv", "w_o", "ffn2_w_in", "ffn2_w_out", "w_ple_gate", "w_ple_proj")
SMALL = ("rel_bias", "norm_ffn1", "norm_mix", "q_norm_a", "k_norm_a", "q_norm_b", "k_norm_b", "sink_b",
         "norm_ffn2", "norm_ple")
WEIGHTS = ("rel_bias", "norm_ffn1", "ffn1_w_in", "ffn1_w_out", "norm_mix", "w_qkv", "q_norm_a", "k_norm_a",
           "q_norm_b", "k_norm_b", "sink_b", "w_o", "norm_ffn2", "ffn2_w_in", "ffn2_w_out", "norm_ple",
           "w_ple_gate", "w_ple_proj")
SMALL_ROWS = 96


def _params(*sem):
    return pltpu.CompilerParams(dimension_semantics=sem, vmem_limit_bytes=VMEM_LIMIT)


def _dot(a, b):
    return jnp.dot(a, b, preferred_element_type=F32)


def _dot_nt(a, b):
    return lax.dot_general(a, b, (((1,), (1,)), ((), ())), preferred_element_type=F32)


def _dot_tn(a, b):
    return lax.dot_general(a, b, (((0,), (0,)), ((), ())), preferred_element_type=F32)


def _sigmoid(x):
    return 1.0 / (1.0 + jnp.exp(-x))


def _rstd(xv):
    return lax.rsqrt(jnp.mean(xv * xv, axis=-1, keepdims=True) + EPS)


def _norm_bwd(dh, xv, gv):
    r = _rstd(xv)
    xn = xv * r
    dg = jnp.sum(dh * xn, axis=0, keepdims=True)
    dxn = dh * gv
    dx = r * (dxn - xn * jnp.mean(dxn * xn, axis=-1, keepdims=True))
    return dx, dg


def _lo_mask(shape):
    return lax.broadcasted_iota(jnp.int32, shape, len(shape) - 1) < HEAD_DIM


def _half_sum(t, lo):
    s0 = jnp.sum(jnp.where(lo, t, 0.0), axis=1, keepdims=True)
    s1 = jnp.sum(jnp.where(lo, 0.0, t), axis=1, keepdims=True)
    return jnp.where(lo, s0, s1)


FFN_PARTS = 2


def _ffn_weight_specs(f, nj, D, C):
    return [pl.BlockSpec((None, None, D, C), lambda i, j: (j, f, 0, 0)),
            pl.BlockSpec((None, None, D, C), lambda i, j: (j + nj, f, 0, 0)),
            pl.BlockSpec((2, C // 2, D), lambda i, j: (j, f, 0))]


def _with_dep(body, dep, in_specs, args):
    if dep is None:
        return body, in_specs, args

    def body_after(dep_ref, *refs):
        body(*refs)

    return body_after, [pl.BlockSpec(memory_space=pl.ANY)] + in_specs, [dep] + args


def _ffn_fwd(x, g, ga, gb, f, tm, dep=None):
    T, D = x.shape
    nj, C = ga.shape[0] // 2, ga.shape[3]

    def body(x_ref, g_ref, wg_ref, wu_ref, wo_ref, xo_ref, h_ref, zg_ref, zu_ref, s_ref, h_scr, acc):
        j = pl.program_id(1)

        @pl.when(j == 0)
        def _():
            xv = x_ref[...]
            hb = (xv * _rstd(xv) * g_ref[...]).astype(BF16)
            h_scr[...] = hb
            h_ref[...] = hb
            acc[...] = jnp.zeros_like(acc)

        wo = wo_ref[...].reshape(C, D)
        for part in range(FFN_PARTS):
            sl = pl.ds(part * (tm // FFN_PARTS), tm // FFN_PARTS)
            hb = h_scr[sl, :]
            gt = _dot(hb, wg_ref[...])
            up = _dot(hb, wu_ref[...])
            s = (gt * _sigmoid(gt) * up).astype(BF16)
            zg_ref[sl, :] = gt.astype(BF16)
            zu_ref[sl, :] = up.astype(BF16)
            s_ref[sl, :] = s
            acc[sl, :] += _dot(s, wo)

        @pl.when(j == nj - 1)
        def _():
            xo_ref[...] = x_ref[...] + 0.5 * acc[...]

    tok = pl.BlockSpec((tm, D), lambda i, j: (i, 0))
    chunk = pl.BlockSpec((None, tm, C), lambda i, j: (j, i, 0))
    in_specs = [tok, pl.BlockSpec((1, D), lambda i, j: (0, 0))] + _ffn_weight_specs(f, nj, D, C)
    body, in_specs, args = _with_dep(body, dep, in_specs, [x, g, ga, ga, gb])
    return pl.pallas_call(
        body, name="ffn_fwd", grid=(T // tm, nj),
        in_specs=in_specs,
        out_specs=[tok, tok, chunk, chunk, chunk],
        out_shape=[jax.ShapeDtypeStruct((T, D), F32), jax.ShapeDtypeStruct((T, D), BF16),
                   jax.ShapeDtypeStruct((nj, T, C), BF16), jax.ShapeDtypeStruct((nj, T, C), BF16),
                   jax.ShapeDtypeStruct((nj, T, C), BF16)],
        scratch_shapes=[pltpu.VMEM((tm, D), BF16), pltpu.VMEM((tm, D), F32)],
        compiler_params=_params("parallel", "arbitrary"),
    )(*args)


def _ffn_bwd(dxo, x, g, zg, zu, ga, gb, f, tm, dep=None):
    T, D = x.shape
    nj, C = ga.shape[0] // 2, ga.shape[3]

    def body(dxo_ref, x_ref, g_ref, zg_ref, zu_ref, wg_ref, wu_ref, wo_ref,
             dx_ref, dy_ref, dzg_ref, dzu_ref, dgn_ref, dy_scr, acc):
        i, j = pl.program_id(0), pl.program_id(1)

        @pl.when(j == 0)
        def _():
            dyb = (0.5 * dxo_ref[...]).astype(BF16)
            dy_scr[...] = dyb
            dy_ref[...] = dyb
            acc[...] = jnp.zeros_like(acc)

        wo = wo_ref[...].reshape(C, D)
        for part in range(FFN_PARTS):
            sl = pl.ds(part * (tm // FFN_PARTS), tm // FFN_PARTS)
            ds = _dot_nt(dy_scr[sl, :], wo)
            gt = zg_ref[sl, :].astype(F32)
            up = zu_ref[sl, :].astype(F32)
            sg = _sigmoid(gt)
            dgt = (ds * up * (sg * (1.0 + gt * (1.0 - sg)))).astype(BF16)
            dup = (ds * (gt * sg)).astype(BF16)
            dzg_ref[sl, :] = dgt
            dzu_ref[sl, :] = dup
            acc[sl, :] += _dot_nt(dgt, wg_ref[...]) + _dot_nt(dup, wu_ref[...])

        @pl.when(j == nj - 1)
        def _():
            dx, dg = _norm_bwd(acc[...], x_ref[...], g_ref[...])
            dx_ref[...] = dxo_ref[...] + dx

            @pl.when(i == 0)
            def _():
                dgn_ref[...] = dg

            @pl.when(i > 0)
            def _():
                dgn_ref[...] += dg

    tok = pl.BlockSpec((tm, D), lambda i, j: (i, 0))
    chunk = pl.BlockSpec((None, tm, C), lambda i, j: (j, i, 0))
    row = pl.BlockSpec((1, D), lambda i, j: (0, 0))
    in_specs = [tok, tok, row, chunk, chunk] + _ffn_weight_specs(f, nj, D, C)
    body, in_specs, args = _with_dep(body, dep, in_specs, [dxo, x, g, zg, zu, ga, ga, gb])
    return pl.pallas_call(
        body, name="ffn_bwd", grid=(T // tm, nj),
        in_specs=in_specs,
        out_specs=[tok, tok, chunk, chunk, row],
        out_shape=[jax.ShapeDtypeStruct((T, D), F32), jax.ShapeDtypeStruct((T, D), BF16),
                   jax.ShapeDtypeStruct((nj, T, C), BF16), jax.ShapeDtypeStruct((nj, T, C), BF16),
                   jax.ShapeDtypeStruct((1, D), F32)],
        scratch_shapes=[pltpu.VMEM((tm, D), BF16), pltpu.VMEM((tm, D), F32)],
        compiler_params=_params("arbitrary", "arbitrary"),
    )(*args)


def _ffn_bwd_dz(dxo, zg, zu, gb, f, tm, dep=None):
    T, D = dxo.shape
    nj, C = zg.shape[0], zg.shape[2]

    def body(dxo_ref, zg_ref, zu_ref, wo_ref, dy_ref, dzg_ref, dzu_ref, dy_scr):
        @pl.when(pl.program_id(1) == 0)
        def _():
            dyb = (0.5 * dxo_ref[...]).astype(BF16)
            dy_scr[...] = dyb
            dy_ref[...] = dyb

        wo = wo_ref[...].reshape(C, D)
        for part in range(FFN_PARTS):
            sl = pl.ds(part * (tm // FFN_PARTS), tm // FFN_PARTS)
            ds = _dot_nt(dy_scr[sl, :], wo)
            gt = zg_ref[sl, :].astype(F32)
            up = zu_ref[sl, :].astype(F32)
            sg = _sigmoid(gt)
            dzg_ref[sl, :] = (ds * up * (sg * (1.0 + gt * (1.0 - sg)))).astype(BF16)
            dzu_ref[sl, :] = (ds * (gt * sg)).astype(BF16)

    tok = pl.BlockSpec((tm, D), lambda i, j: (i, 0))
    chunk = pl.BlockSpec((None, tm, C), lambda i, j: (j, i, 0))
    in_specs = [tok, chunk, chunk, _ffn_weight_specs(f, nj, D, C)[2]]
    body, in_specs, args = _with_dep(body, dep, in_specs, [dxo, zg, zu, gb])
    return pl.pallas_call(
        body, name="ffn_bwd_dz", grid=(T // tm, nj),
        in_specs=in_specs, out_specs=[tok, chunk, chunk],
        out_shape=[jax.ShapeDtypeStruct((T, D), BF16), jax.ShapeDtypeStruct((nj, T, C), BF16),
                   jax.ShapeDtypeStruct((nj, T, C), BF16)],
        scratch_shapes=[pltpu.VMEM((tm, D), BF16)],
        compiler_params=_params("parallel", "arbitrary"),
    )(*args)


def _ffn_bwd_dx(dxo, x, g, dzg, dzu, ga, f, tm, dep=None):
    T, D = x.shape
    nj, C = ga.shape[0] // 2, ga.shape[3]

    def body(dxo_ref, x_ref, g_ref, dzg_ref, dzu_ref, wg_ref, wu_ref, dx_ref, dgn_ref, acc):
        i, j = pl.program_id(0), pl.program_id(1)

        @pl.when(j == 0)
        def _():
            acc[...] = jnp.zeros_like(acc)

        acc[...] += _dot_nt(dzg_ref[...], wg_ref[...]) + _dot_nt(dzu_ref[...], wu_ref[...])

        @pl.when(j == nj - 1)
        def _():
            dx, dg = _norm_bwd(acc[...], x_ref[...], g_ref[...])
            dx_ref[...] = dxo_ref[...] + dx

            @pl.when(i == 0)
            def _():
                dgn_ref[...] = dg

            @pl.when(i > 0)
            def _():
                dgn_ref[...] += dg

    tok = pl.BlockSpec((tm, D), lambda i, j: (i, 0))
    chunk = pl.BlockSpec((None, tm, C), lambda i, j: (j, i, 0))
    row = pl.BlockSpec((1, D), lambda i, j: (0, 0))
    in_specs = [tok, tok, row, chunk, chunk] + _ffn_weight_specs(f, nj, D, C)[:2]
    body, in_specs, args = _with_dep(body, dep, in_specs, [dxo, x, g, dzg, dzu, ga, ga])
    return pl.pallas_call(
        body, name="ffn_bwd_dx", grid=(T // tm, nj),
        in_specs=in_specs, out_specs=[tok, row],
        out_shape=[jax.ShapeDtypeStruct((T, D), F32), jax.ShapeDtypeStruct((1, D), F32)],
        scratch_shapes=[pltpu.VMEM((tm, D), F32)],
        compiler_params=_params("arbitrary", "arbitrary"),
    )(*args)


def _ffn_dw(h, dzg, dzu, s, dy, tk, dep=None):
    T, D = h.shape
    nj, C = s.shape[0], s.shape[2]
    nk = T // tk

    def body(h_ref, dzg_ref, dzu_ref, s_ref, dy_ref, dwin_ref, dwo_ref, ag, au, ao):
        k = pl.program_id(1)

        @pl.when(k == 0)
        def _():
            ag[...] = jnp.zeros_like(ag)
            au[...] = jnp.zeros_like(au)
            ao[...] = jnp.zeros_like(ao)

        hb = h_ref[...]
        ag[...] += _dot_tn(hb, dzg_ref[...])
        au[...] += _dot_tn(hb, dzu_ref[...])
        ao[...] += _dot_tn(s_ref[...], dy_ref[...])

        @pl.when(k == nk - 1)
        def _():
            dwin_ref[0] = ag[...].astype(BF16)
            dwin_ref[1] = au[...].astype(BF16)
            dwo_ref[...] = ao[...].astype(BF16)

    tok = pl.BlockSpec((tk, D), lambda j, k: (k, 0))
    chunk = pl.BlockSpec((None, tk, C), lambda j, k: (j, k, 0))
    body, in_specs, args = _with_dep(body, dep, [tok, chunk, chunk, chunk, tok], [h, dzg, dzu, s, dy])
    dwin, dwo = pl.pallas_call(
        body, name="ffn_dw", grid=(nj, nk),
        in_specs=in_specs,
        out_specs=[pl.BlockSpec((2, None, D, C), lambda j, k: (0, j, 0, 0)),
                   pl.BlockSpec((None, C, D), lambda j, k: (j, 0, 0))],
        out_shape=[jax.ShapeDtypeStruct((2, nj, D, C), BF16), jax.ShapeDtypeStruct((nj, C, D), BF16)],
        scratch_shapes=[pltpu.VMEM((D, C), F32), pltpu.VMEM((D, C), F32), pltpu.VMEM((C, D), F32)],
        compiler_params=_params("parallel", "arbitrary"),
    )(*args)
    return dwin.reshape(2 * nj, D, C), dwo


def _matmul_tn(a, b, tn, tk):
    T, Ka = a.shape
    N = b.shape[1]
    nk = T // tk

    def body(a_ref, b_ref, o_ref, acc):
        k = pl.program_id(1)

        @pl.when(k == 0)
        def _():
            acc[...] = jnp.zeros_like(acc)

        acc[...] += _dot_tn(a_ref[...], b_ref[...])

        @pl.when(k == nk - 1)
        def _():
            o_ref[...] = acc[...].astype(BF16)

    return pl.pallas_call(
        body, name="matmul_tn", grid=(N // tn, nk),
        in_specs=[pl.BlockSpec((tk, Ka), lambda n, k: (k, 0)), pl.BlockSpec((tk, tn), lambda n, k: (k, n))],
        out_specs=pl.BlockSpec((Ka, tn), lambda n, k: (0, n)),
        out_shape=jax.ShapeDtypeStruct((Ka, N), BF16),
        scratch_shapes=[pltpu.VMEM((Ka, tn), F32)],
        compiler_params=_params("parallel", "arbitrary"),
    )(a, b)


def _qkv_fwd(x, g, w, tm):
    T, D = x.shape
    N = w.shape[1]

    def body(x_ref, g_ref, w_ref, o_ref, h_ref):
        xv = x_ref[...]
        hb = (xv * _rstd(xv) * g_ref[...]).astype(BF16)
        h_ref[...] = hb
        o_ref[...] = _dot(hb, w_ref[...])

    return pl.pallas_call(
        body, name="qkv_fwd", grid=(T // tm,),
        in_specs=[pl.BlockSpec((tm, D), lambda i: (i, 0)), pl.BlockSpec((1, D), lambda i: (0, 0)),
                  pl.BlockSpec((D, N), lambda i: (0, 0))],
        out_specs=[pl.BlockSpec((tm, N), lambda i: (i, 0)), pl.BlockSpec((tm, D), lambda i: (i, 0))],
        out_shape=[jax.ShapeDtypeStruct((T, N), F32), jax.ShapeDtypeStruct((T, D), BF16)],
        compiler_params=_params("parallel"),
    )(x, g, w)


DILS = tuple(d for _, d in DILATED)


def _spread_specs(tm, T, dtype):
    specs = [pl.BlockSpec((4, d, tm // d, PAIR), lambda i: (0, 0, i, 0)) for d in DILS]
    shapes = [jax.ShapeDtypeStruct((4, d, T // d, PAIR), dtype) for d in DILS]
    return specs, shapes


def _spread(tile, y, outs, c, dtype):
    tm = y.shape[0]
    tile[...] = y
    for out, d in zip(outs, DILS):
        for r in range(d):
            out[c, r] = tile[pl.ds(r, tm // d, stride=d), :].astype(dtype)


def _collect(tile, ins, c):
    tm = tile.shape[0]
    first = True
    for ref, d in zip(ins, DILS):
        for r in range(d):
            rows = pl.ds(r, tm // d, stride=d) if d > 1 else pl.ds(0, tm)
            part = ref[c, r].astype(F32)
            tile[rows, :] = part if first else tile[rows, :] + part
        first = False
    return tile[...]


def _attn_prep(qkv, gains2, tm):
    T = qkv.shape[0]
    scale = HEAD_DIM ** -0.5
    n = len(DILS)

    def body(qkv_ref, g_ref, qb_ref, kb_ref, vb_ref, *rest):
        outs, tile = rest[:-1], rest[-1]
        lo = _lo_mask((tm, PAIR))

        def spread(kind, c, y):
            _spread(tile, y, outs[kind * n:(kind + 1) * n], c, BF16)

        def normed(c, gi, mult):
            xv = qkv_ref[:, c * PAIR:(c + 1) * PAIR]
            r = lax.rsqrt(_half_sum(xv * xv, lo) * (1.0 / HEAD_DIM) + EPS)
            y = xv * r * g_ref[gi:gi + 1, :]
            return y * mult if mult != 1.0 else y

        def both_halves(v):
            sw = pltpu.roll(v, HEAD_DIM, 1)
            return jnp.where(lo, v, sw), jnp.where(lo, sw, v)

        for c in range(4):
            spread(0, c, normed(c, 0, scale))
            spread(1, c, normed(4 + c, 1, 1.0))
            spread(2, c, qkv_ref[:, (8 + c) * PAIR:(9 + c) * PAIR])
            qb_ref[c] = normed(12 + c, 2, scale).astype(BF16)
        k0, k1 = both_halves(normed(16, 3, 1.0))
        kb_ref[0] = k0.astype(BF16)
        kb_ref[1] = k1.astype(BF16)
        v0, v1 = both_halves(qkv_ref[:, 17 * PAIR:18 * PAIR])
        vb_ref[0] = v0.astype(BF16)
        vb_ref[1] = v1.astype(BF16)

    four = pl.BlockSpec((4, tm, PAIR), lambda i: (0, i, 0))
    two = pl.BlockSpec((2, tm, PAIR), lambda i: (0, i, 0))
    s4 = jax.ShapeDtypeStruct((4, T, PAIR), BF16)
    s2 = jax.ShapeDtypeStruct((2, T, PAIR), BF16)
    specs, shapes = _spread_specs(tm, T, BF16)
    res = pl.pallas_call(
        body, name="attn_prep", grid=(T // tm,),
        in_specs=[pl.BlockSpec((tm, qkv.shape[1]), lambda i: (i, 0)), pl.BlockSpec((4, PAIR), lambda i: (0, 0))],
        out_specs=[four, two, two] + specs * 3,
        out_shape=[s4, s2, s2] + shapes * 3,
        scratch_shapes=[pltpu.VMEM((tm, PAIR), F32)],
        compiler_params=_params("parallel"),
    )(qkv, gains2)
    qb, kb, vb = res[:3]
    per_d = [tuple(res[3 + kind * n + di].reshape(4 * d, T // d, PAIR) for kind in range(3))
             for di, d in enumerate(DILS)]
    return qb, kb, vb, per_d


def _loop_blocks(nb, body, init, per_iter):
    u = math.gcd(nb, per_iter)

    def outer(i, carry):
        for k in range(u):
            carry = body(i * u + k, carry)
        return carry

    return lax.fori_loop(0, nb // u, outer, init)


def _key_window(b, nb, L, R, W):
    start = pl.multiple_of(jnp.clip(b * BQ - R, 0, L - W), HEAD_DIM)
    return start, jnp.where(b == 0, 1, jnp.where(b == nb - 1, 2, 0))


def _stack_heads(v, lo):
    z = jnp.zeros_like(v)
    return jnp.concatenate([jnp.where(lo, v, z), jnp.where(lo, z, v)], axis=0)


def _unstack_heads(v2, lo):
    return jnp.where(lo, v2[:BQ], v2[BQ:])


def _row_vector(v, lo):
    r = lax.broadcasted_iota(jnp.int32, (BQ, PAIR), 0)
    ln = lax.broadcasted_iota(jnp.int32, (BQ, PAIR), 1)
    diag = (ln % HEAD_DIM) == (r % HEAD_DIM)
    top = jnp.sum(jnp.where(diag & (r < HEAD_DIM), v, 0.0), axis=0, keepdims=True)
    bot = jnp.sum(jnp.where(diag & (r >= HEAD_DIM), v, 0.0), axis=0, keepdims=True)
    top8, bot8 = jnp.broadcast_to(top, (8, PAIR)), jnp.broadcast_to(bot, (8, PAIR))
    lo8 = _lo_mask((8, PAIR))
    head0 = jnp.where(lo8, top8, pltpu.roll(bot8, HEAD_DIM, 1))
    head1 = jnp.where(lo8, pltpu.roll(top8, HEAD_DIM, 1), bot8)
    return jnp.concatenate([head0, head1], axis=1)[:1]


def _units_per_step(nb, pairs_per_kv):
    return max(1, 16 // nb) if pairs_per_kv == 1 else 1


def _attn_fwd(q, kp, vp, bias4, sink, R, pairs_per_kv, pairs_per_bias):
    N, L, _ = q.shape
    W = BQ + 2 * R
    nb = L // BQ
    assert L >= W and nb >= 2
    G = _units_per_step(nb, pairs_per_kv)

    def body(sink_ref, q_ref, k_ref, v_ref, bias_ref, o_ref, lse_ref):
        n = pl.program_id(0)
        lo_q = _lo_mask((BQ, PAIR))
        first = lax.broadcasted_iota(jnp.int32, (2 * BQ, 1), 0) < BQ

        def blk(f, carry):
            g, b = f // nb, f % nb
            u = n * G + g
            sk = jnp.where(first, sink_ref[2 * u], sink_ref[2 * u + 1])
            q0 = pl.multiple_of(b * BQ, BQ)
            q2 = _stack_heads(q_ref[g, pl.ds(q0, BQ), :], lo_q)
            k0, variant = _key_window(b, nb, L, R, W)
            kw = k_ref[g, pl.ds(k0, W), :]
            vw = v_ref[g, pl.ds(k0, W), :]
            s = _dot_nt(q2, kw) + bias_ref[variant]
            m = jnp.maximum(jnp.max(s, axis=1, keepdims=True), sk)
            p = jnp.exp(s - m)
            l = jnp.sum(p, axis=1, keepdims=True) + jnp.exp(sk - m)
            o2 = _dot(p.astype(BF16), vw) / l
            o_ref[g, pl.ds(q0, BQ), :] = _unstack_heads(o2, lo_q)
            lse_ref[g, pl.ds(q0, BQ), :] = _unstack_heads(jnp.broadcast_to(m + jnp.log(l), (2 * BQ, PAIR)), lo_q)
            return carry

        _loop_blocks(G * nb, blk, 0, 16)

    qspec = pl.BlockSpec((G, L, PAIR), lambda n: (n, 0, 0))
    kspec = pl.BlockSpec((G, L, PAIR), lambda n: (n // pairs_per_kv, 0, 0))
    return pl.pallas_call(
        body, name="attn_fwd", grid=(N // G,),
        in_specs=[pl.BlockSpec(memory_space=pltpu.SMEM), qspec, kspec, kspec,
                  pl.BlockSpec((None, 3, 2 * BQ, W), lambda n: (n * G // pairs_per_bias, 0, 0, 0))],
        out_specs=[qspec, qspec],
        out_shape=[jax.ShapeDtypeStruct((N, L, PAIR), F32), jax.ShapeDtypeStruct((N, L, PAIR), F32)],
        compiler_params=_params("parallel"),
    )(sink, q, kp, vp, bias4)


def _attn_bwd(q, kp, vp, bias4t, sink, o, lse, do, R, pairs_per_kv, pairs_per_bias):
    N, L, _ = q.shape
    Nk = kp.shape[0]
    Pb = bias4t.shape[0]
    W = BQ + 2 * R
    nb = L // BQ
    assert L >= W and nb >= 2
    G = _units_per_step(nb, pairs_per_kv)

    def body(sink_ref, q_ref, k_ref, v_ref, bias_ref, o_ref, lse_ref, do_ref,
             dq_ref, dk_ref, dv_ref, dbias_ref, dsink_ref, dk_acc, dv_acc):
        n = pl.program_id(0)
        lo_q = _lo_mask((BQ, PAIR))
        first = lax.broadcasted_iota(jnp.int32, (1, 2 * BQ), 1) < BQ
        dsink_ref[...] = jnp.zeros_like(dsink_ref)

        @pl.when(n % pairs_per_kv == 0)
        def _():
            dk_acc[...] = jnp.zeros_like(dk_acc)
            dv_acc[...] = jnp.zeros_like(dv_acc)

        @pl.when((n * G) % pairs_per_bias == 0)
        def _():
            dbias_ref[...] = jnp.zeros_like(dbias_ref)

        def blk(f, carry):
            g, b = f // nb, f % nb
            u = n * G + g
            sk = jnp.where(first, sink_ref[2 * u], sink_ref[2 * u + 1])
            q0 = pl.multiple_of(b * BQ, BQ)
            q2 = _stack_heads(q_ref[g, pl.ds(q0, BQ), :], lo_q)
            k0, variant = _key_window(b, nb, L, R, W)
            kw = k_ref[g, pl.ds(k0, W), :]
            vw = v_ref[g, pl.ds(k0, W), :]
            dov = do_ref[g, pl.ds(q0, BQ), :]
            lse = _row_vector(lse_ref[g, pl.ds(q0, BQ), :], lo_q)
            delta = _row_vector(_half_sum(dov.astype(F32) * o_ref[g, pl.ds(q0, BQ), :], lo_q), lo_q)
            do2 = _stack_heads(dov.astype(BF16), lo_q)
            st = _dot_nt(kw, q2) + bias_ref[variant]
            pt = jnp.exp(st - lse)
            dst = pt * (_dot_nt(vw, do2) - delta)
            dstb = dst.astype(BF16)
            dbias_ref[variant] += dst
            dk_acc[g, pl.ds(k0, W), :] += _dot(dstb, q2)
            dv_acc[g, pl.ds(k0, W), :] += _dot(pt.astype(BF16), do2)
            dq_ref[g, pl.ds(q0, BQ), :] = _unstack_heads(_dot_tn(dstb, kw), lo_q).astype(BF16)
            dsink_ref[g, pl.ds(0, 1), :] -= jnp.exp(sk - lse) * delta
            return carry

        _loop_blocks(G * nb, blk, 0, 8)
        dk_ref[...] = dk_acc[...].astype(BF16)
        dv_ref[...] = dv_acc[...].astype(BF16)

    qspec = pl.BlockSpec((G, L, PAIR), lambda n: (n, 0, 0))
    kspec = pl.BlockSpec((G, L, PAIR), lambda n: (n // pairs_per_kv, 0, 0))
    return pl.pallas_call(
        body, name="attn_bwd", grid=(N // G,),
        in_specs=[pl.BlockSpec(memory_space=pltpu.SMEM), qspec, kspec, kspec,
                  pl.BlockSpec((None, 3, W, 2 * BQ), lambda n: (n * G // pairs_per_bias, 0, 0, 0)),
                  qspec, qspec, qspec],
        out_specs=[qspec, kspec, kspec,
                   pl.BlockSpec((None, 3, W, 2 * BQ), lambda n: (n * G // pairs_per_bias, 0, 0, 0)),
                   pl.BlockSpec((G, 8, 2 * BQ), lambda n: (n, 0, 0))],
        out_shape=[jax.ShapeDtypeStruct((N, L, PAIR), BF16),
                   jax.ShapeDtypeStruct((Nk, L, PAIR), BF16),
                   jax.ShapeDtypeStruct((Nk, L, PAIR), BF16),
                   jax.ShapeDtypeStruct((Pb, 3, W, 2 * BQ), F32),
                   jax.ShapeDtypeStruct((N, 8, 2 * BQ), F32)],
        scratch_shapes=[pltpu.VMEM((G, L, PAIR), F32), pltpu.VMEM((G, L, PAIR), F32)],
        compiler_params=_params("arbitrary"),
    )(sink, q, kp, vp, bias4t, o, lse, do)


def _attn_merge(branch_outs, ob, tm):
    T = ob.shape[1]
    n = len(DILS)

    def body(*refs):
        o_in, l_in, ob_ref = refs[:n], refs[n:2 * n], refs[2 * n]
        o_out, l_out, cat_ref = refs[2 * n + 1:3 * n + 1], refs[3 * n + 1:4 * n + 1], refs[4 * n + 1]
        tiles = refs[4 * n + 2:]
        for c in range(4):
            o_nat, l_nat = [], []
            for di, d in enumerate(DILS):
                for kind, (src, dst) in enumerate(((o_in[di], o_nat), (l_in[di], l_nat))):
                    tile = tiles[2 * di + kind]
                    if d == 1:
                        dst.append(src[c, 0])
                    else:
                        for r in range(d):
                            tile[pl.ds(r, tm // d, stride=d), :] = src[c, r]
                        dst.append(tile[...])
            m = functools.reduce(jnp.maximum, l_nat)
            ws = [jnp.exp(l - m) for l in l_nat]
            z = sum(ws)
            o = sum(w * t for w, t in zip(ws, o_nat)) / z
            cat_ref[:, c * PAIR:(c + 1) * PAIR] = o.astype(BF16)
            cat_ref[:, (4 + c) * PAIR:(5 + c) * PAIR] = ob_ref[c].astype(BF16)
            _spread(tiles[0], o, o_out, c, F32)
            _spread(tiles[1], m + jnp.log(z), l_out, c, F32)

    specs, shapes = _spread_specs(tm, T, F32)
    four = pl.BlockSpec((4, tm, PAIR), lambda i: (0, i, 0))
    o_views = [o.reshape(4, d, T // d, PAIR) for (o, _), d in zip(branch_outs, DILS)]
    l_views = [l.reshape(4, d, T // d, PAIR) for (_, l), d in zip(branch_outs, DILS)]
    res = pl.pallas_call(
        body, name="attn_merge", grid=(T // tm,),
        in_specs=specs + specs + [four],
        out_specs=specs + specs + [pl.BlockSpec((tm, 8 * PAIR), lambda i: (i, 0))],
        out_shape=shapes + shapes + [jax.ShapeDtypeStruct((T, 8 * PAIR), BF16)],
        scratch_shapes=[pltpu.VMEM((tm, PAIR), F32)] * (2 * n),
        compiler_params=_params("parallel"),
    )(*o_views, *l_views, ob)
    merged = [(res[di].reshape(4 * d, T // d, PAIR), res[n + di].reshape(4 * d, T // d, PAIR))
              for di, d in enumerate(DILS)]
    return merged, res[2 * n]


def _weight_arg(w, blk):
    if blk is None:
        return pl.BlockSpec(w.shape, lambda i: (0, 0)), (lambda ref: ref[...])
    D = w.shape[2]
    return (pl.BlockSpec((N_DEV, 128, D), lambda i: (0, blk, 0)),
            lambda ref: ref[...].reshape(N_DEV * 128, D))


def _oproj_fwd(x, o_cat, w, blk, tm):
    T, D = x.shape
    wspec, wload = _weight_arg(w, blk)

    def body(x_ref, o_ref, w_ref, out_ref):
        out_ref[...] = x_ref[...] + _dot(o_ref[...], wload(w_ref))

    tok = pl.BlockSpec((tm, D), lambda i: (i, 0))
    return pl.pallas_call(
        body, name="oproj_fwd", grid=(T // tm,),
        in_specs=[tok, pl.BlockSpec((tm, o_cat.shape[1]), lambda i: (i, 0)), wspec],
        out_specs=tok, out_shape=jax.ShapeDtypeStruct((T, D), F32),
        compiler_params=_params("parallel"),
    )(x, o_cat, w)


def _oproj_bwd(dx, w, blk, tm, dep=None):
    T, D = dx.shape
    wspec, wload = _weight_arg(w, blk)

    def body(dx_ref, w_ref, dxb_ref, dob_ref, *rest):
        doa_refs, tile = rest[:-1], rest[-1]
        db = dx_ref[...].astype(BF16)
        dxb_ref[...] = db
        do = _dot_nt(db, wload(w_ref))
        for c in range(4):
            _spread(tile, do[:, c * PAIR:(c + 1) * PAIR], doa_refs, c, BF16)
            dob_ref[c] = do[:, (4 + c) * PAIR:(5 + c) * PAIR].astype(BF16)

    tok = pl.BlockSpec((tm, D), lambda i: (i, 0))
    specs, shapes = _spread_specs(tm, T, BF16)
    body, in_specs, args = _with_dep(body, dep, [tok, wspec], [dx, w])
    res = pl.pallas_call(
        body, name="oproj_bwd", grid=(T // tm,),
        in_specs=in_specs,
        out_specs=[tok, pl.BlockSpec((4, tm, PAIR), lambda i: (0, i, 0))] + specs,
        out_shape=[jax.ShapeDtypeStruct((T, D), BF16), jax.ShapeDtypeStruct((4, T, PAIR), BF16)] + shapes,
        scratch_shapes=[pltpu.VMEM((tm, PAIR), F32)],
        compiler_params=_params("parallel"),
    )(*args)
    return res[0], res[1], [t.reshape(4 * d, T // d, PAIR) for t, d in zip(res[2:], DILS)]


def _attn_post(qkv, gains2, dqa, dka, dva, dqb, dkb, dvb, tm):
    T, NQ = qkv.shape
    scale = HEAD_DIM ** -0.5

    n = len(DILS)

    def body(qkv_ref, g_ref, *rest):
        dq_refs, dk_refs, dv_refs = rest[:n], rest[n:2 * n], rest[2 * n:3 * n]
        qb_ref, kb_ref, vb_ref, out_ref, dg_ref, tile = rest[3 * n:]
        lo = _lo_mask((tm, PAIR))

        @pl.when(pl.program_id(0) == 0)
        def _():
            dg_ref[...] = jnp.zeros_like(dg_ref)

        def norm_bwd(c, gi, dy):
            xv = qkv_ref[:, c * PAIR:(c + 1) * PAIR]
            r = lax.rsqrt(_half_sum(xv * xv, lo) * (1.0 / HEAD_DIM) + EPS)
            xn = xv * r
            dg_ref[gi:gi + 1, :] += jnp.sum(dy * xn, axis=0, keepdims=True)
            dxn = dy * g_ref[gi:gi + 1, :]
            dx = r * (dxn - xn * (_half_sum(dxn * xn, lo) * (1.0 / HEAD_DIM)))
            out_ref[:, c * PAIR:(c + 1) * PAIR] = dx.astype(BF16)

        def fold(v):
            return v + pltpu.roll(v, HEAD_DIM, 1)

        for c in range(4):
            norm_bwd(c, 0, _collect(tile, dq_refs, c) * scale)
            norm_bwd(4 + c, 1, _collect(tile, dk_refs, c))
            out_ref[:, (8 + c) * PAIR:(9 + c) * PAIR] = _collect(tile, dv_refs, c).astype(BF16)
            norm_bwd(12 + c, 2, qb_ref[c].astype(F32) * scale)
        kb, vb = kb_ref[...].astype(F32), vb_ref[...].astype(F32)
        norm_bwd(16, 3, jnp.where(lo, fold(kb[0]), fold(kb[1])))
        out_ref[:, 17 * PAIR:18 * PAIR] = jnp.where(lo, fold(vb[0]), fold(vb[1])).astype(BF16)

    four = pl.BlockSpec((4, tm, PAIR), lambda i: (0, i, 0))
    two = pl.BlockSpec((2, tm, PAIR), lambda i: (0, i, 0))
    specs, _ = _spread_specs(tm, T, BF16)
    views = [t.reshape(4, d, T // d, PAIR) for group in (dqa, dka, dva) for t, d in zip(group, DILS)]
    return pl.pallas_call(
        body, name="attn_post", grid=(T // tm,),
        in_specs=[pl.BlockSpec((tm, NQ), lambda i: (i, 0)), pl.BlockSpec((4, PAIR), lambda i: (0, 0))]
        + specs * 3 + [four, two, two],
        out_specs=[pl.BlockSpec((tm, NQ), lambda i: (i, 0)), pl.BlockSpec((4, PAIR), lambda i: (0, 0))],
        out_shape=[jax.ShapeDtypeStruct((T, NQ), BF16), jax.ShapeDtypeStruct((4, PAIR), F32)],
        scratch_shapes=[pltpu.VMEM((tm, PAIR), F32)],
        compiler_params=_params("arbitrary"),
    )(qkv, gains2, *views, dqb, dkb, dvb)


def _dense_norm_bwd(dres, dz, w, blk, x, g, tm):
    T, D = x.shape
    N = dz.shape[1]
    wspec, wload = _weight_arg(w, blk)

    def body(dres_ref, dz_ref, w_ref, x_ref, g_ref, dx_ref, dgn_ref):
        i = pl.program_id(0)
        dx, dg = _norm_bwd(_dot_nt(dz_ref[...], wload(w_ref)), x_ref[...], g_ref[...])
        dx_ref[...] = dres_ref[...] + dx

        @pl.when(i == 0)
        def _():
            dgn_ref[...] = dg

        @pl.when(i > 0)
        def _():
            dgn_ref[...] += dg

    tok = pl.BlockSpec((tm, D), lambda i: (i, 0))
    row = pl.BlockSpec((1, D), lambda i: (0, 0))
    return pl.pallas_call(
        body, name="dense_norm_bwd", grid=(T // tm,),
        in_specs=[tok, pl.BlockSpec((tm, N), lambda i: (i, 0)), wspec, tok, row],
        out_specs=[tok, row],
        out_shape=[jax.ShapeDtypeStruct((T, D), F32), jax.ShapeDtypeStruct((1, D), F32)],
        compiler_params=_params("arbitrary"),
    )(dres, dz, w, x, g)


def _bias_reduce(onehot, dbm):
    Hb, K = dbm.shape

    def body(oh_ref, d_ref, out_ref):
        oh = oh_ref[...]
        d = d_ref[...]
        hi = d.astype(BF16)
        r1 = d - hi.astype(F32)
        mid = r1.astype(BF16)
        low = (r1 - mid.astype(F32)).astype(BF16)
        out_ref[...] = _dot_nt(hi, oh) + _dot_nt(mid, oh) + _dot_nt(low, oh)

    vm = pl.BlockSpec(memory_space=pltpu.VMEM)
    return pl.pallas_call(
        body, name="bias_reduce", in_specs=[vm, vm], out_specs=vm,
        out_shape=jax.ShapeDtypeStruct((Hb, N_BUCKETS), F32),
        compiler_params=pltpu.CompilerParams(vmem_limit_bytes=VMEM_LIMIT),
    )(onehot, dbm)


def _ple_fwd(x, g, wg, blk, p, wp, target, tm):
    T, D = x.shape
    P = p.shape[1]
    with_loss = target is not None
    wspec, wload = _weight_arg(wg, blk)

    def body(*refs):
        if with_loss:
            x_ref, g_ref, wg_ref, p_ref, wp_ref, t_ref, y_ref, hn_ref, gate_ref, pp_ref, pb_ref, loss_ref = refs
        else:
            x_ref, g_ref, wg_ref, p_ref, wp_ref, y_ref, hn_ref, gate_ref, pp_ref, pb_ref = refs
        i = pl.program_id(0)
        xv = x_ref[...]
        hb = (xv * _rstd(xv) * g_ref[...]).astype(BF16)
        hn_ref[...] = hb
        gate = _sigmoid(_dot(hb, wload(wg_ref)))
        pb = p_ref[...].astype(BF16)
        pb_ref[...] = pb
        pp = _dot(pb, wp_ref[...])
        gate_ref[...] = gate
        pp_ref[...] = pp
        y = xv + gate * pp
        if with_loss:
            err = y - t_ref[...]
            y_ref[...] = err * (1.0 / D)
            part = jnp.broadcast_to(0.5 * jnp.sum(jnp.sum(err * err, axis=1, keepdims=True) * (1.0 / D),
                                                  axis=0, keepdims=True), (1, 128))

            @pl.when(i == 0)
            def _():
                loss_ref[...] = part

            @pl.when(i > 0)
            def _():
                loss_ref[...] += part
        else:
            y_ref[...] = y

    tok = pl.BlockSpec((tm, D), lambda i: (i, 0))
    ptok = pl.BlockSpec((tm, P), lambda i: (i, 0))
    in_specs = [tok, pl.BlockSpec((1, D), lambda i: (0, 0)), wspec, ptok,
                pl.BlockSpec((P, D), lambda i: (0, 0))]
    out_specs = [tok, tok, tok, tok, ptok]
    out_shape = [jax.ShapeDtypeStruct((T, D), F32), jax.ShapeDtypeStruct((T, D), BF16),
                 jax.ShapeDtypeStruct((T, D), F32), jax.ShapeDtypeStruct((T, D), F32),
                 jax.ShapeDtypeStruct((T, P), BF16)]
    args = [x, g, wg, p, wp]
    if with_loss:
        in_specs.append(tok)
        out_specs.append(pl.BlockSpec((1, 128), lambda i: (0, 0)))
        out_shape.append(jax.ShapeDtypeStruct((1, 128), F32))
        args.append(target)
    return pl.pallas_call(
        body, name="ple_fwd_loss" if with_loss else "ple_fwd", grid=(T // tm,),
        in_specs=in_specs, out_specs=out_specs, out_shape=out_shape,
        compiler_params=_params("arbitrary" if with_loss else "parallel"),
    )(*args)


def _ple_bwd(dy, gate, pp, tm, dep=None):
    T, D = dy.shape

    def body(dy_ref, gate_ref, pp_ref, dgl_ref, dpp_ref):
        d = dy_ref[...]
        gt = gate_ref[...]
        dgl_ref[...] = (d * pp_ref[...] * gt * (1.0 - gt)).astype(BF16)
        dpp_ref[...] = (d * gt).astype(BF16)

    tok = pl.BlockSpec((tm, D), lambda i: (i, 0))
    body, in_specs, args = _with_dep(body, dep, [tok, tok, tok], [dy, gate, pp])
    return pl.pallas_call(
        body, name="ple_bwd", grid=(T // tm,), in_specs=in_specs, out_specs=[tok, tok],
        out_shape=[jax.ShapeDtypeStruct((T, D), BF16), jax.ShapeDtypeStruct((T, D), BF16)],
        compiler_params=_params("parallel"),
    )(*args)


def _adamw(w, g, m, v):
    shape = w.shape
    C = shape[-1]
    w2, g2, m2, v2 = (a.reshape(-1, C) for a in (w, g, m, v))
    Rn = w2.shape[0]
    tr = Rn
    for cand in (512, 352, 256):
        if Rn % cand == 0:
            tr = cand
            break
    c1 = 1.0 - ADAM_B1 ** ADAM_STEP
    c2 = 1.0 - ADAM_B2 ** ADAM_STEP

    def body(w_ref, g_ref, m_ref, v_ref, d_ref, nm_ref, nv_ref):
        gv = g_ref[...]
        mn = ADAM_B1 * m_ref[...] + (1.0 - ADAM_B1) * gv
        vn = ADAM_B2 * v_ref[...] + (1.0 - ADAM_B2) * (gv * gv)
        d_ref[...] = -ADAM_LR * ((mn / c1) / (jnp.sqrt(vn / c2) + ADAM_EPS) + ADAM_WD * w_ref[...])
        nm_ref[...] = mn
        nv_ref[...] = vn

    spec = pl.BlockSpec((tr, C), lambda i: (i, 0))
    sh = jax.ShapeDtypeStruct((Rn, C), F32)
    d, nm, nv = pl.pallas_call(
        body, name="adamw", grid=(Rn // tr,), in_specs=[spec] * 4, out_specs=[spec] * 3, out_shape=[sh] * 3,
        compiler_params=_params("parallel"),
    )(w2, g2, m2, v2)
    return d.reshape(shape), nm.reshape(shape), nv.reshape(shape)


def _my_place():
    x, y, c = lax.axis_index("x"), lax.axis_index("y"), lax.axis_index("c")
    chips = [(1 - x, y), (x, 1 - y), (1 - x, 1 - y)]
    return x, y, c, chips


def _all_gather(arrs):
    n = len(arrs)

    def body(*refs):
        x_refs, out_refs = refs[:n], refs[n:2 * n]
        send_sems, recv_sems, local_sems = refs[2 * n:]
        x, y, c, chips = _my_place()
        me, sibling = (x, y, c), (x, y, 1 - c)

        def copy(m, k, block, to, src=None):
            rows = out_refs[m].at[4 * block[0] + 2 * block[1] + block[2]]
            return pltpu.make_async_remote_copy(
                src_ref=rows if src is None else src, dst_ref=rows,
                send_sem=send_sems.at[7 * m + k], recv_sem=recv_sems.at[7 * m + k], device_id=to, device_id_type=MESH)

        mine = [pltpu.make_async_copy(x_refs[m], out_refs[m].at[4 * x + 2 * y + c], local_sems.at[m])
                for m in range(n)]
        for cp in mine:
            cp.start()
        first = []
        for m in range(n):
            first.append(copy(m, 0, me, sibling, src=x_refs[m]))
            first += [copy(m, 1 + j, me, (*chip, c), src=x_refs[m]) for j, chip in enumerate(chips)]
        for cp in first:
            cp.start()
        passed = []
        for m in range(n):
            for j, chip in enumerate(chips):
                copy(m, 1 + j, (*chip, c), me).wait_recv()
                cp = copy(m, 4 + j, (*chip, c), sibling)
                cp.start()
                passed.append(cp)
        for m in range(n):
            copy(m, 0, sibling, me).wait_recv()
            for j, chip in enumerate(chips):
                copy(m, 4 + j, (*chip, 1 - c), me).wait_recv()
        for cp in first + passed:
            cp.wait_send()
        for cp in mine:
            cp.wait()

    hbm = pl.BlockSpec(memory_space=pl.ANY)
    return pl.pallas_call(
        body, name="all_gather", in_specs=[hbm] * n, out_specs=[hbm] * n,
        out_shape=[jax.ShapeDtypeStruct((N_DEV,) + a.shape, a.dtype) for a in arrs],
        scratch_shapes=[pltpu.SemaphoreType.DMA((7 * n,)), pltpu.SemaphoreType.DMA((7 * n,)),
                        pltpu.SemaphoreType.DMA((n,))],
    )(*arrs)


def _peer(x, y, c, k):
    return (x ^ ((k >> 2) & 1), y ^ ((k >> 1) & 1), c ^ (k & 1))


HBM_SPEC = pl.BlockSpec(memory_space=pltpu.HBM)
SEM_SPEC = pl.BlockSpec(memory_space=pltpu.SEMAPHORE)


def _exchange_refs(srcs, lands, m, k, x, y, c, scatter):
    peer = _peer(x, y, c, k)
    if scatter:
        return srcs[m].at[4 * peer[0] + 2 * peer[1] + peer[2]], lands[m].at[k - 1], peer
    return srcs[m], lands[m].at[4 * x + 2 * y + c], peer


ALL_PEERS = tuple(range(1, N_DEV))
SAME_CORE_PEERS = (2, 4, 6)


def _exchange_start(arrs, land_shapes, scatter, name, relations=ALL_PEERS):
    n, nr = len(arrs), len(relations)

    def body(*refs):
        srcs, lands = refs[:n], refs[n:2 * n]
        send_sems, recv_sems = refs[2 * n], refs[2 * n + 1]
        token = refs[-1]
        x, y, c, _ = _my_place()
        for m in range(n):
            for j, k in enumerate(relations):
                src, dst, peer = _exchange_refs(srcs, lands, m, k, x, y, c, scatter)
                pltpu.make_async_remote_copy(
                    src_ref=src, dst_ref=dst, send_sem=send_sems.at[nr * m + j],
                    recv_sem=recv_sems.at[nr * m + j], device_id=peer, device_id_type=MESH).start()
        token[...] = jnp.zeros_like(token)

    zones = [lax.empty(s_, a.dtype) for s_, a in zip(land_shapes, arrs)]
    outs = pl.pallas_call(
        body, name=name,
        out_shape=(pltpu.SemaphoreType.DMA((nr * n,)), pltpu.SemaphoreType.DMA((nr * n,)),
                   *[pltpu.HBM(a.shape, a.dtype) for a in arrs], *[pltpu.HBM(z.shape, z.dtype) for z in zones],
                   jax.ShapeDtypeStruct((8, 128), F32)),
        in_specs=[HBM_SPEC] * (2 * n),
        out_specs=(SEM_SPEC, SEM_SPEC, *[HBM_SPEC] * (2 * n), pl.BlockSpec(memory_space=pltpu.VMEM)),
        input_output_aliases={m: 2 + m for m in range(2 * n)},
        compiler_params=pltpu.CompilerParams(has_side_effects=pltpu.SideEffectType.DATAFLOW_SIDE_EFFECTING),
    )(*[pltpu.with_memory_space_constraint(a, pltpu.HBM) for a in arrs],
      *[pltpu.with_memory_space_constraint(z, pltpu.HBM) for z in zones])
    return outs[0], outs[1], list(outs[2:2 + n]), list(outs[2 + n:2 + 2 * n]), outs[-1]


def _exchange_wait(send_sems, recv_sems, arrs, zones, after, scatter, name, relations=ALL_PEERS):
    n, nr = len(arrs), len(relations)
    afters = list(after) if isinstance(after, (list, tuple)) else [after]

    def body(*refs):
        srcs, lands = refs[:n], refs[n:2 * n]
        send_sems, recv_sems = refs[2 * n], refs[2 * n + 1]
        x, y, c, _ = _my_place()
        for m in range(n):
            for j, k in enumerate(relations):
                src, dst, peer = _exchange_refs(srcs, lands, m, k, x, y, c, scatter)
                cp = pltpu.make_async_remote_copy(
                    src_ref=src, dst_ref=dst, send_sem=send_sems.at[nr * m + j],
                    recv_sem=recv_sems.at[nr * m + j], device_id=peer, device_id_type=MESH)
                cp.wait_send()
                cp.wait_recv()

    outs = pl.pallas_call(
        body, name=name,
        out_shape=tuple(pltpu.HBM(a.shape, a.dtype) for a in list(arrs) + list(zones)),
        in_specs=[HBM_SPEC] * (2 * n) + [SEM_SPEC, SEM_SPEC] + [pl.BlockSpec(memory_space=pl.ANY)] * len(afters),
        out_specs=tuple([HBM_SPEC] * (2 * n)),
        input_output_aliases={m: m for m in range(2 * n)},
        compiler_params=pltpu.CompilerParams(has_side_effects=pltpu.SideEffectType.DATAFLOW_SIDE_EFFECTING),
    )(*arrs, *zones, send_sems, recv_sems, *afters)
    return list(outs[n:])


def _hand_to_sibling(zones):
    n = len(zones)

    def body(*refs):
        outs = refs[n:2 * n]
        send_sems, recv_sems = refs[2 * n:]
        x, y, c, chips = _my_place()
        copies = []
        for m in range(n):
            for a, q in enumerate([(x, y)] + chips):
                block = outs[m].at[4 * q[0] + 2 * q[1] + c]
                cp = pltpu.make_async_remote_copy(
                    src_ref=block, dst_ref=block, send_sem=send_sems.at[4 * m + a], recv_sem=recv_sems.at[4 * m + a],
                    device_id=(x, y, 1 - c), device_id_type=MESH)
                cp.start()
                copies.append(cp)
        for cp in copies:
            cp.wait_recv()
        for cp in copies:
            cp.wait_send()

    hbm = pl.BlockSpec(memory_space=pl.ANY)
    return pl.pallas_call(
        body, name="hand_to_sibling", in_specs=[hbm] * n, out_specs=[hbm] * n,
        out_shape=[jax.ShapeDtypeStruct(z.shape, z.dtype) for z in zones],
        input_output_aliases={m: m for m in range(n)},
        scratch_shapes=[pltpu.SemaphoreType.DMA((4 * n,)), pltpu.SemaphoreType.DMA((4 * n,))],
    )(*zones)


def _sum_parts(own, parts, tr, dep=None):
    R, W = own.shape

    def body(own_ref, parts_ref, out_ref):
        acc = own_ref[...].astype(F32)
        for k in range(N_DEV - 1):
            acc = acc + parts_ref[k].astype(F32)
        out_ref[...] = acc

    in_specs = [pl.BlockSpec((tr, W), lambda i: (i, 0)), pl.BlockSpec((N_DEV - 1, tr, W), lambda i: (0, i, 0))]
    body, in_specs, args = _with_dep(body, dep, in_specs, [own, parts])
    return pl.pallas_call(
        body, name="sum_parts", grid=(R // tr,),
        in_specs=in_specs,
        out_specs=pl.BlockSpec((tr, W), lambda i: (i, 0)),
        out_shape=jax.ShapeDtypeStruct((R, W), F32),
        compiler_params=_params("parallel"),
    )(*args)


def _all_reduce_small(v, dep=None):
    Rn, Wd = v.shape

    def body(v_ref, out_ref, gat_ref, send_sems, recv_sems):
        x, y, c, _ = _my_place()
        me = 4 * x + 2 * y + c
        gat_ref[me] = v_ref[...]
        copies = []
        for k in range(1, N_DEV):
            fx, fy, fc = (k >> 2) & 1, (k >> 1) & 1, k & 1
            peer = (x ^ fx, y ^ fy, c ^ fc)
            cp = pltpu.make_async_remote_copy(
                src_ref=v_ref, dst_ref=gat_ref.at[me], send_sem=send_sems.at[k - 1], recv_sem=recv_sems.at[k - 1],
                device_id=peer, device_id_type=MESH)
            cp.start()
            copies.append(cp)
        for cp in copies:
            cp.wait_recv()
        for cp in copies:
            cp.wait_send()
        acc = gat_ref[0]
        for k in range(1, N_DEV):
            acc = acc + gat_ref[k]
        out_ref[...] = acc

    vm = pl.BlockSpec(memory_space=pltpu.VMEM)
    body, in_specs, args = _with_dep(body, dep, [vm], [v])
    return pl.pallas_call(
        body, name="all_reduce_small", in_specs=in_specs, out_specs=vm,
        out_shape=jax.ShapeDtypeStruct((Rn, Wd), F32),
        scratch_shapes=[pltpu.VMEM((N_DEV, Rn, Wd), F32), pltpu.SemaphoreType.DMA((7,)),
                        pltpu.SemaphoreType.DMA((7,))],
    )(*args)


def _t5_bucket(rel):
    half = N_BUCKETS // 2
    max_exact = half // 2
    ret = jnp.where(rel > 0, half, 0)
    n = jnp.abs(rel)
    nf = jnp.maximum(n, 1).astype(F32)
    large = max_exact + (jnp.log(nf / max_exact) / math.log(MAX_DISTANCE / max_exact)
                         * (half - max_exact)).astype(jnp.int32)
    large = jnp.minimum(large, half - 1)
    return ret + jnp.where(n < max_exact, n, large)


def _band(R, d):
    W = BQ + 2 * R
    rel = jnp.arange(W)[None, :] - R - jnp.arange(BQ)[:, None]
    return _t5_bucket(rel * d), jnp.abs(rel) <= R


def _onehot(R, d):
    bkt, in_band = _band(R, d)
    return ((bkt.reshape(1, -1) == jnp.arange(N_BUCKETS)[:, None]) & in_band.reshape(1, -1)).astype(BF16)


def _bias_expand(table_t, onehot):
    H = table_t.shape[0]
    K = onehot.shape[1]

    def body(t_ref, oh_ref, out_ref):
        oh = oh_ref[...]
        t = t_ref[...]
        hi = t.astype(BF16)
        r1 = t - hi.astype(F32)
        mid = r1.astype(BF16)
        low = (r1 - mid.astype(F32)).astype(BF16)
        marked = _dot(jnp.ones(t.shape, BF16), oh) > 0.5
        out_ref[...] = jnp.where(marked, _dot(hi, oh) + _dot(mid, oh) + _dot(low, oh), NEG)

    vm = pl.BlockSpec(memory_space=pltpu.VMEM)
    return pl.pallas_call(
        body, name="bias_expand", in_specs=[vm, vm], out_specs=vm,
        out_shape=jax.ShapeDtypeStruct((H, K), F32),
        compiler_params=pltpu.CompilerParams(vmem_limit_bytes=VMEM_LIMIT),
    )(table_t, onehot)


def _bias_matrix(table, R, d):
    return _bias_expand(table.T, _onehot(R, d)).reshape(table.shape[1], BQ, BQ + 2 * R)


def _bias_variants(base, R):
    H, _, W = base.shape
    fill = jnp.full((H, BQ, R), NEG, F32)
    first = jnp.concatenate([base[:, :, R:], fill], axis=2)
    last = jnp.concatenate([fill, base[:, :, :W - R]], axis=2)
    v = jnp.stack([base, first, last], axis=1)
    v = v.reshape(H // 2, 2, 3, BQ, W).transpose(0, 2, 1, 3, 4).reshape(H // 2, 3, 2 * BQ, W)
    return v, v.transpose(0, 1, 3, 2)


def _bias_grad(dbt, R, d):
    P, _, W, _ = dbt.shape
    dbt = dbt[:, 0].at[:, R:].add(dbt[:, 1, :W - R]).at[:, :W - R].add(dbt[:, 2, R:])
    dbm = dbt.reshape(P, W, 2, BQ).transpose(0, 2, 3, 1).reshape(2 * P, BQ * W)
    return _bias_reduce(_onehot(R, d), dbm).T


def _tile2(gain):
    return jnp.concatenate([gain, gain])


ROW_W_O, ROW_GATE, B_ROWS = 768, 896, 1024
BLK_W_O, BLK_GATE = ROW_W_O // 128, ROW_GATE // 128


def _pack_layer(wts, i):
    a = jnp.stack([wts["ffn1_w_in"][i], wts["ffn2_w_in"][i]])
    D = a.shape[1]
    b = jnp.concatenate([
        wts["ffn1_w_out"][i], wts["ffn2_w_out"][i],
        jnp.zeros((ROW_W_O - 2 * wts["ffn1_w_out"].shape[1], D), a.dtype), wts["w_o"][i], wts["w_ple_gate"][i]])
    return a, b, wts["w_qkv"][i], wts["w_ple_proj"][i]


def _unpack_layer(sums, like):
    w_in2, b1, proj, w_o, qkv, w_in1, w_out1 = sums
    n_out = like["ffn1_w_out"].shape[1]
    out = {}
    if w_in2 is not None:
        out.update(ffn2_w_in=w_in2, ffn2_w_out=b1[:n_out], w_ple_gate=b1[n_out:], w_ple_proj=proj)
    if w_o is not None:
        out.update(w_o=w_o, w_qkv=qkv)
    if w_in1 is not None:
        out.update(ffn1_w_in=w_in1, ffn1_w_out=w_out1)
    return out


def _col_sharded(g):
    return g.transpose(1, 0, 2).reshape(g.shape[1], -1)


def _to_col_shards(g):
    rows = g.shape[0]
    return g.reshape(rows, N_DEV, -1).transpose(1, 0, 2)


def _layer_weights(ga, gb, gq, gp):
    return dict(ga=ga, gb=gb, w_qkv=_col_sharded(gq), w_proj=_col_sharded(gp))


def _layer_fwd(x, p, w, sm, i, target, tm, biases, dep=None):
    ga, gb = w["ga"], w["gb"]
    saved = {}
    saved["x0"] = x
    x1, saved["h1"], saved["zg1"], saved["zu1"], saved["s1"] = _ffn_fwd(
        x, sm["norm_ffn1"][i][None], ga, gb, 0, 2 * tm, dep)
    saved["x1"] = x1
    qkv, saved["hm"] = _qkv_fwd(x1, sm["norm_mix"][i][None], w["w_qkv"], 2 * tm)
    saved["qkv"] = qkv
    gains2 = jnp.stack([_tile2(sm[k][i]) for k in ("q_norm_a", "k_norm_a", "q_norm_b", "k_norm_b")])
    saved["gains2"] = gains2
    qb, kb, vb, qkv_d = _attn_prep(qkv, gains2, tm)
    no_sink = jnp.full((8,), NEG, F32)
    branches = []
    outs = []
    for (R, d), bias, (qd, kd, vd) in zip(DILATED, biases[:3], qkv_d):
        sink = jnp.tile(no_sink, d)
        outs.append(_attn_fwd(qd, kd, vd, bias[0], sink, R, 1, d))
        branches.append((qd, kd, vd, bias, sink, R, d))
    bias_b = biases[3]
    sink_b = sm["sink_b"][i]
    ob, lb = _attn_fwd(qb, kb, vb, bias_b[0], sink_b, SWA_RADIUS, 2, 1)
    merged, o_cat = _attn_merge(outs, ob, tm)
    saved.update(branches=branches, b=(qb, kb, vb, bias_b, sink_b), merged=merged, ob=ob, lb=lb, o_cat=o_cat)
    x2 = _oproj_fwd(x1, o_cat, gb, BLK_W_O, 2 * tm)
    saved["x2"] = x2
    x3, saved["h2"], saved["zg2"], saved["zu2"], saved["s2"] = _ffn_fwd(
        x2, sm["norm_ffn2"][i][None], ga, gb, 1, 2 * tm)
    saved["x3"] = x3
    res = _ple_fwd(x3, sm["norm_ple"][i][None], gb, BLK_GATE, p, w["w_proj"], target, tm)
    y, saved["hp"], saved["gate"], saved["pp"], saved["pb"] = res[:5]
    loss = res[5] if target is not None else None
    return y, loss, saved


def _layer_bwd(dy, w, sm, i, sv, tm, dep=None, on_ready=None, on_small=None, on_last=None):
    ga, gb = w["ga"], w["gb"]
    gs = {}
    D = dy.shape[1]
    dgl, dpp = _ple_bwd(dy, sv["gate"], sv["pp"], tm, dep)
    d_gate = _matmul_tn(sv["hp"], dgl, D, 4 * tm)
    d_proj = _matmul_tn(sv["pb"], dpp, D, 4 * tm)
    dx3, gs["norm_ple"] = _dense_norm_bwd(dy, dgl, gb, BLK_GATE, sv["x3"], sm["norm_ple"][i][None], 2 * tm)
    dx2, dyb, dzg, dzu, gs["norm_ffn2"] = _ffn_bwd(dx3, sv["x2"], sm["norm_ffn2"][i][None], sv["zg2"], sv["zu2"],
                                                   ga, gb, 1, tm)
    dwin2, dwo2 = _ffn_dw(sv["h2"], dzg, dzu, sv["s2"], dyb, 4 * tm)
    half = dwo2.shape[1] // 2
    after_ffn2 = [dwin2, jnp.concatenate([dwo2.reshape(N_DEV, half, D), d_gate.reshape(N_DEV, -1, D)], axis=1),
                  _to_col_shards(d_proj)]
    token = None if on_ready is None else on_ready(0, after_ffn2)
    dx2b, do_b, do_a = _oproj_bwd(dx2, gb, BLK_W_O, tm, token)
    d_wo = _matmul_tn(sv["o_cat"], dx2b, D, 4 * tm)
    dqa, dka, dva, dbias = [], [], [], []
    for (qd, kd, vd, bias, sink, R, d), (oa, la), do_d in zip(sv["branches"], sv["merged"], do_a):
        dq, dk, dv, dbm, _ = _attn_bwd(qd, kd, vd, bias[1], sink, oa, la, do_d, R, 1, d)
        dqa.append(dq)
        dka.append(dk)
        dva.append(dv)
        dbias.append(dbm)
    qb, kb, vb, bias_b, sink_b = sv["b"]
    dqb, dkb, dvb, dbm_b, dsink = _attn_bwd(qb, kb, vb, bias_b[1], sink_b, sv["ob"], sv["lb"], do_b,
                                            SWA_RADIUS, 2, 1)
    gs["rel_bias"] = dbias + [dbm_b]
    gs["sink_b"] = jnp.sum(dsink[:, 0].reshape(-1, 2, BQ), axis=2).reshape(-1)
    dqkv, dgains2 = _attn_post(sv["qkv"], sv["gains2"], dqa, dka, dva, dqb,
                               dkb, dvb, tm)
    dgains = dgains2[:, :HEAD_DIM] + dgains2[:, HEAD_DIM:]
    for k, name in enumerate(("q_norm_a", "k_norm_a", "q_norm_b", "k_norm_b")):
        gs[name] = dgains[k]
    d_qkv = _matmul_tn(sv["hm"], dqkv, dqkv.shape[1] // 2, 4 * tm)
    after_mixer = [d_wo.reshape(N_DEV, -1, D), _to_col_shards(d_qkv)]
    token = None if on_ready is None else on_ready(1, after_mixer)
    dx1, gs["norm_mix"] = _dense_norm_bwd(dx2, dqkv, w["w_qkv"], None, sv["x1"], sm["norm_mix"][i][None], 2 * tm)
    g1 = sm["norm_ffn1"][i][None]
    if on_last is None:
        dx0, dyb, dzg, dzu, gs["norm_ffn1"] = _ffn_bwd(dx1, sv["x0"], g1, sv["zg1"], sv["zu1"], ga, gb, 0, tm, token)
        dwin1, dwo1 = _ffn_dw(sv["h1"], dzg, dzu, sv["s1"], dyb, 4 * tm)
        return dx0, (after_ffn2, after_mixer, [dwin1, dwo1.reshape(N_DEV, half, D)]), gs
    dyb, dzg, dzu = _ffn_bwd_dz(dx1, sv["zg1"], sv["zu1"], gb, 0, 2 * tm, token)
    dwin1, dwo1 = _ffn_dw(sv["h1"], dzg, dzu, sv["s1"], dyb, 4 * tm, on_small(gs))
    last = [dwin1, dwo1.reshape(N_DEV, half, D)]
    dx0, gs["norm_ffn1"] = _ffn_bwd_dx(dx1, sv["x0"], g1, dzg, dzu, ga, 0, 2 * tm, on_last(last))
    return dx0, (after_ffn2, after_mixer, last), gs


def _bias_matrices(rel_bias):
    biases = [_bias_variants(_bias_matrix(rel_bias[:, :8], R, d), R) for R, d in DILATED]
    biases.append(_bias_variants(_bias_matrix(rel_bias[:, 8:], SWA_RADIUS, 1), SWA_RADIUS))
    return biases


def _stack_small(per_layer):
    small = {}
    for k, v in per_layer.items():
        if k == "rel_bias":
            per_branch = [sum(parts) for parts in zip(*v.values())]
            drel_a = sum(_bias_grad(t, R, d) for t, (R, d) in zip(per_branch[:3], DILATED))
            small[k] = jnp.concatenate([drel_a, _bias_grad(per_branch[3], SWA_RADIUS, 1)], axis=1)
        else:
            small[k] = jnp.stack([v[i].reshape(-1) for i in sorted(v)])
    return small


TM = 512
SUM_TILES = (512, 480, 256, 128, 512, 512, 352)
LAST_GROUP = ("ffn1_w_in", "ffn1_w_out")


def _pack_small(d, extra=None):
    parts = [d[k].reshape(-1) for k in SMALL]
    if extra is not None:
        parts.append(extra.reshape(-1))
    flat = jnp.concatenate(parts)
    return jnp.pad(flat, (0, SMALL_ROWS * 128 - flat.shape[0])).reshape(SMALL_ROWS, 128)


def _unpack_small(buf, like):
    flat = buf.reshape(-1)
    out, off = {}, 0
    for k in SMALL:
        n = like[k].size
        out[k] = flat[off:off + n].reshape(like[k].shape)
        off += n
    return out, flat[off]


def kernel(x, p, rel_bias, norm_ffn1, ffn1_w_in, ffn1_w_out, norm_mix, w_qkv, q_norm_a, k_norm_a, q_norm_b, k_norm_b, sink_b, w_o, norm_ffn2, ffn2_w_in, ffn2_w_out, norm_ple, w_ple_gate, w_ple_proj, loss_target, m_rel_bias, m_norm_ffn1, m_ffn1_w_in, m_ffn1_w_out, m_norm_mix, m_w_qkv, m_q_norm_a, m_k_norm_a, m_q_norm_b, m_k_norm_b, m_sink_b, m_w_o, m_norm_ffn2, m_ffn2_w_in, m_ffn2_w_out, m_norm_ple, m_w_ple_gate, m_w_ple_proj, v_rel_bias, v_norm_ffn1, v_ffn1_w_in, v_ffn1_w_out, v_norm_mix, v_w_qkv, v_q_norm_a, v_k_norm_a, v_q_norm_b, v_k_norm_b, v_sink_b, v_w_o, v_norm_ffn2, v_ffn2_w_in, v_ffn2_w_out, v_norm_ple, v_w_ple_gate, v_w_ple_proj):
    wts = dict(rel_bias=rel_bias, norm_ffn1=norm_ffn1, ffn1_w_in=ffn1_w_in, ffn1_w_out=ffn1_w_out,
               norm_mix=norm_mix, w_qkv=w_qkv, q_norm_a=q_norm_a, k_norm_a=k_norm_a, q_norm_b=q_norm_b,
               k_norm_b=k_norm_b, sink_b=sink_b, w_o=w_o, norm_ffn2=norm_ffn2, ffn2_w_in=ffn2_w_in,
               ffn2_w_out=ffn2_w_out, norm_ple=norm_ple, w_ple_gate=w_ple_gate, w_ple_proj=w_ple_proj)
    mom = dict(rel_bias=m_rel_bias, norm_ffn1=m_norm_ffn1, ffn1_w_in=m_ffn1_w_in, ffn1_w_out=m_ffn1_w_out,
               norm_mix=m_norm_mix, w_qkv=m_w_qkv, q_norm_a=m_q_norm_a, k_norm_a=m_k_norm_a, q_norm_b=m_q_norm_b,
               k_norm_b=m_k_norm_b, sink_b=m_sink_b, w_o=m_w_o, norm_ffn2=m_norm_ffn2, ffn2_w_in=m_ffn2_w_in,
               ffn2_w_out=m_ffn2_w_out, norm_ple=m_norm_ple, w_ple_gate=m_w_ple_gate, w_ple_proj=m_w_ple_proj)
    var = dict(rel_bias=v_rel_bias, norm_ffn1=v_norm_ffn1, ffn1_w_in=v_ffn1_w_in, ffn1_w_out=v_ffn1_w_out,
               norm_mix=v_norm_mix, w_qkv=v_w_qkv, q_norm_a=v_q_norm_a, k_norm_a=v_k_norm_a, q_norm_b=v_q_norm_b,
               k_norm_b=v_k_norm_b, sink_b=v_sink_b, w_o=v_w_o, norm_ffn2=v_norm_ffn2, ffn2_w_in=v_ffn2_w_in,
               ffn2_w_out=v_ffn2_w_out, norm_ple=v_norm_ple, w_ple_gate=v_w_ple_gate, w_ple_proj=v_w_ple_proj)
    sm = {k: wts[k] for k in SMALL}
    me = 4 * lax.axis_index("x") + 2 * lax.axis_index("y") + lax.axis_index("c")
    packed = []
    for i in range(2):
        a, *rest = _pack_layer(wts, i)
        packed.append([t.astype(BF16) for t in [a.reshape(-1, a.shape[-1])] + rest])
    a_shape = (2, ffn1_w_in.shape[1], ffn1_w_in.shape[2])

    def weights_of(zones):
        return _layer_weights(zones[0].reshape((N_DEV,) + a_shape), *zones[1:])

    zone_shapes = [(N_DEV,) + t.shape for t in packed[1]]
    ex0 = _exchange_start(packed[0], zone_shapes, False, "gather0_start", SAME_CORE_PEERS)
    biases = _bias_matrices(rel_bias)
    zones0 = _exchange_wait(*ex0[:4], biases[3][0], False, "gather0_wait", SAME_CORE_PEERS)
    zones0 = [lax.dynamic_update_index_in_dim(z, t, me, 0) for z, t in zip(zones0, packed[0])]
    w0 = weights_of(_hand_to_sibling(zones0))
    ssem, rsem, thru, zones, token = _exchange_start(packed[1], zone_shapes, False, "gather_start")
    x1, _, sv0 = _layer_fwd(x[0], p[0, 0], w0, sm, 0, None, TM, biases, dep=token)
    zones = _exchange_wait(ssem, rsem, thru, zones, x1, False, "gather_wait")
    w1 = weights_of([lax.dynamic_update_index_in_dim(z, t, me, 0) for z, t in zip(zones, packed[1])])
    dy, loss, sv1 = _layer_fwd(x1, p[1, 0], w1, sm, 1, loss_target[0], TM, biases)

    def slots_for(arrs):
        return [(N_DEV - 1,) + t.shape[1:] for t in arrs]

    held1, held = {}, {}

    def on_ready1(stage, group):
        held1[stage] = _exchange_start(group, slots_for(group), True, f"scatter1_start_{stage}")
        return held1[stage][4]

    dx1, groups1, gs1 = _layer_bwd(dy, w1, sm, 1, sv1, TM, on_ready=on_ready1)
    on_ready1(2, groups1[2])
    g1 = groups1[0] + groups1[1] + groups1[2]

    def on_ready(stage, group):
        if stage == 1:
            held["slots1"] = [t for st in (0, 1, 2)
                              for t in _exchange_wait(*held1[st][:4], group[0], True, f"scatter1_wait_{st}")]
        held[stage] = _exchange_start(group, slots_for(group), True, f"scatter_start_{stage}")
        return held[stage][4]

    def on_small(gs0):
        part = dict(gs0, norm_ffn1=jnp.zeros_like(gs1["norm_ffn1"]))
        gsmall = _stack_small({k: {0: part[k], 1: gs1[k]} for k in part})
        held["small"] = _all_reduce_small(_pack_small(gsmall, loss[0, :1]))
        return held["small"]

    def on_last(group):
        held["last"] = _exchange_start(group, slots_for(group), True, "scatter_start_2")
        return held["last"][4]

    dx, groups0, gs0 = _layer_bwd(dx1, w0, sm, 0, sv0, TM, dep=held1[2][4], on_ready=on_ready, on_small=on_small,
                                  on_last=on_last)
    last = groups0[2]
    slots0 = [_exchange_wait(*held[stage][:4], last[0], True, f"scatter_wait_{stage}") for stage in (0, 1)]

    def summed(arrs, slots, tiles, dep=None):
        return [_sum_parts(lax.dynamic_index_in_dim(t, me, 0, keepdims=False), s_, tr, dep)
                for t, s_, tr in zip(arrs, slots, tiles)]

    cover = held["last"][4]
    r1 = summed(g1, held["slots1"], SUM_TILES, cover)
    r0 = summed(groups0[0], slots0[0], SUM_TILES[:3], cover) + summed(groups0[1], slots0[1], SUM_TILES[3:5], cover)

    def update(names, layers):
        for k in names:
            grads[k] = jnp.stack([layers[0][k], layers[1][k]])
            delta[k], new_m[k], new_v[k] = _adamw(wts[k], grads[k], mom[k], var[k])

    grads, delta, new_m, new_v = {}, {}, {}, {}
    layer1 = _unpack_layer(r1, wts)
    update([k for k in BIG if k not in LAST_GROUP], [_unpack_layer(r0 + [None, None], wts), layer1])

    cover_done = [dx] + [delta[k] for k in BIG if k not in LAST_GROUP]
    slots_last = _exchange_wait(*held["last"][:4], cover_done, True, "scatter_wait_2")
    update(LAST_GROUP, [_unpack_layer([None] * 5 + summed(last, slots_last, SUM_TILES[5:]), wts), layer1])
    late = _all_reduce_small(gs0["norm_ffn1"].reshape(-1, 128), dep=slots_last[0])
    small_sum, loss_sum = _unpack_small(held["small"], sm)
    small_sum["norm_ffn1"] = small_sum["norm_ffn1"].at[0].add(late.reshape(-1))
    grads.update(small_sum)
    zeros = {k: jnp.zeros_like(wts[k]) for k in SMALL}
    ds, ms, vs = _adamw(_pack_small(wts), _pack_small(small_sum), _pack_small(mom), _pack_small(var))
    for packed, dst in ((ds, delta), (ms, new_m), (vs, new_v)):
        dst.update(_unpack_small(packed, zeros)[0])

    return (loss_sum, dx[None], *[grads[k] for k in WEIGHTS], *[delta[k] for k in WEIGHTS],
            *[new_m[k] for k in WEIGHTS], *[new_v[k] for k in WEIGHTS])
```

```python
import functools
import math

import jax
import jax.numpy as jnp
from jax import lax
from jax.experimental import pallas as pl
from jax.experimental.pallas import tpu as pltpu

F32 = jnp.float32
BF16 = jnp.bfloat16

N_DEV = 8
HEAD_DIM = 64
PAIR = 2 * HEAD_DIM
BQ = 128
N_BUCKETS = 32
MAX_DISTANCE = 1024
DILATED = ((64, 1), (64, 4), (64, 16))
SWA_RADIUS = 128
EPS = 1e-6
NEG = -1e30
ADAM_LR, ADAM_B1, ADAM_B2, ADAM_EPS, ADAM_WD, ADAM_STEP = 0.001, 0.9, 0.999, 1e-08, 0.01, 10
VMEM_LIMIT = 56 * 1024 * 1024
MESH = pl.DeviceIdType.MESH

BIG = ("ffn1_w_in", "ffn1_w_out", "w_qkv", "w_o", "ffn2_w_in", "ffn2_w_out", "w_ple_gate", "w_ple_proj")
SMALL = ("rel_bias", "norm_ffn1", "norm_mix", "q_norm_a", "k_norm_a", "q_norm_b", "k_norm_b", "sink_b",
         "norm_ffn2", "norm_ple")
WEIGHTS = ("rel_bias", "norm_ffn1", "ffn1_w_in", "ffn1_w_out", "norm_mix", "w_qkv", "q_norm_a", "k_norm_a",
           "q_norm_b", "k_norm_b", "sink_b", "w_o", "norm_ffn2", "ffn2_w_in", "ffn2_w_out", "norm_ple",
           "w_ple_gate", "w_ple_proj")
SMALL_ROWS = 96


def _params(*sem):
    return pltpu.CompilerParams(dimension_semantics=sem, vmem_limit_bytes=VMEM_LIMIT)


def _dot(a, b):
    return jnp.dot(a, b, preferred_element_type=F32)


def _dot_nt(a, b):
    return lax.dot_general(a, b, (((1,), (1,)), ((), ())), preferred_element_type=F32)


def _dot_tn(a, b):
    return lax.dot_general(a, b, (((0,), (0,)), ((), ())), preferred_element_type=F32)


def _sigmoid(x):
    return 1.0 / (1.0 + jnp.exp(-x))


def _rstd(xv):
    return lax.rsqrt(jnp.mean(xv * xv, axis=-1, keepdims=True) + EPS)


def _norm_bwd(dh, xv, gv):
    r = _rstd(xv)
    xn = xv * r
    dg = jnp.sum(dh * xn, axis=0, keepdims=True)
    dxn = dh * gv
    dx = r * (dxn - xn * jnp.mean(dxn * xn, axis=-1, keepdims=True))
    return dx, dg


def _lo_mask(shape):
    return lax.broadcasted_iota(jnp.int32, shape, len(shape) - 1) < HEAD_DIM


def _half_sum(t, lo):
    s0 = jnp.sum(jnp.where(lo, t, 0.0), axis=1, keepdims=True)
    s1 = jnp.sum(jnp.where(lo, 0.0, t), axis=1, keepdims=True)
    return jnp.where(lo, s0, s1)


FFN_PARTS = 2


def _ffn_weight_specs(f, nj, D, C):
    return [pl.BlockSpec((None, None, D, C), lambda i, j: (j, f, 0, 0)),
            pl.BlockSpec((None, None, D, C), lambda i, j: (j + nj, f, 0, 0)),
            pl.BlockSpec((2, C // 2, D), lambda i, j: (j, f, 0))]


def _with_dep(body, dep, in_specs, args):
    if dep is None:
        return body, in_specs, args

    def body_after(dep_ref, *refs):
        body(*refs)

    return body_after, [pl.BlockSpec(memory_space=pl.ANY)] + in_specs, [dep] + args


def _ffn_fwd(x, g, ga, gb, f, tm, dep=None):
    T, D = x.shape
    nj, C = ga.shape[0] // 2, ga.shape[3]

    def body(x_ref, g_ref, wg_ref, wu_ref, wo_ref, xo_ref, h_ref, zg_ref, zu_ref, s_ref, h_scr, acc):
        j = pl.program_id(1)

        @pl.when(j == 0)
        def _():
            xv = x_ref[...]
            hb = (xv * _rstd(xv) * g_ref[...]).astype(BF16)
            h_scr[...] = hb
            h_ref[...] = hb
            acc[...] = jnp.zeros_like(acc)

        wo = wo_ref[...].reshape(C, D)
        for part in range(FFN_PARTS):
            sl = pl.ds(part * (tm // FFN_PARTS), tm // FFN_PARTS)
            hb = h_scr[sl, :]
            gt = _dot(hb, wg_ref[...])
            up = _dot(hb, wu_ref[...])
            s = (gt * _sigmoid(gt) * up).astype(BF16)
            zg_ref[sl, :] = gt.astype(BF16)
            zu_ref[sl, :] = up.astype(BF16)
            s_ref[sl, :] = s
            acc[sl, :] += _dot(s, wo)

        @pl.when(j == nj - 1)
        def _():
            xo_ref[...] = x_ref[...] + 0.5 * acc[...]

    tok = pl.BlockSpec((tm, D), lambda i, j: (i, 0))
    chunk = pl.BlockSpec((None, tm, C), lambda i, j: (j, i, 0))
    in_specs = [tok, pl.BlockSpec((1, D), lambda i, j: (0, 0))] + _ffn_weight_specs(f, nj, D, C)
    body, in_specs, args = _with_dep(body, dep, in_specs, [x, g, ga, ga, gb])
    return pl.pallas_call(
        body, name="ffn_fwd", grid=(T // tm, nj),
        in_specs=in_specs,
        out_specs=[tok, tok, chunk, chunk, chunk],
        out_shape=[jax.ShapeDtypeStruct((T, D), F32), jax.ShapeDtypeStruct((T, D), BF16),
                   jax.ShapeDtypeStruct((nj, T, C), BF16), jax.ShapeDtypeStruct((nj, T, C), BF16),
                   jax.ShapeDtypeStruct((nj, T, C), BF16)],
        scratch_shapes=[pltpu.VMEM((tm, D), BF16), pltpu.VMEM((tm, D), F32)],
        compiler_params=_params("parallel", "arbitrary"),
    )(*args)


def _ffn_bwd(dxo, x, g, zg, zu, ga, gb, f, tm, dep=None):
    T, D = x.shape
    nj, C = ga.shape[0] // 2, ga.shape[3]

    def body(dxo_ref, x_ref, g_ref, zg_ref, zu_ref, wg_ref, wu_ref, wo_ref,
             dx_ref, dy_ref, dzg_ref, dzu_ref, dgn_ref, dy_scr, acc):
        i, j = pl.program_id(0), pl.program_id(1)

        @pl.when(j == 0)
        def _():
            dyb = (0.5 * dxo_ref[...]).astype(BF16)
            dy_scr[...] = dyb
            dy_ref[...] = dyb
            acc[...] = jnp.zeros_like(acc)

        wo = wo_ref[...].reshape(C, D)
        for part in range(FFN_PARTS):
            sl = pl.ds(part * (tm // FFN_PARTS), tm // FFN_PARTS)
            ds = _dot_nt(dy_scr[sl, :], wo)
            gt = zg_ref[sl, :].astype(F32)
            up = zu_ref[sl, :].astype(F32)
            sg = _sigmoid(gt)
            dgt = (ds * up * (sg * (1.0 + gt * (1.0 - sg)))).astype(BF16)
            dup = (ds * (gt * sg)).astype(BF16)
            dzg_ref[sl, :] = dgt
            dzu_ref[sl, :] = dup
            acc[sl, :] += _dot_nt(dgt, wg_ref[...]) + _dot_nt(dup, wu_ref[...])

        @pl.when(j == nj - 1)
        def _():
            dx, dg = _norm_bwd(acc[...], x_ref[...], g_ref[...])
            dx_ref[...] = dxo_ref[...] + dx

            @pl.when(i == 0)
            def _():
                dgn_ref[...] = dg

            @pl.when(i > 0)
            def _():
                dgn_ref[...] += dg

    tok = pl.BlockSpec((tm, D), lambda i, j: (i, 0))
    chunk = pl.BlockSpec((None, tm, C), lambda i, j: (j, i, 0))
    row = pl.BlockSpec((1, D), lambda i, j: (0, 0))
    in_specs = [tok, tok, row, chunk, chunk] + _ffn_weight_specs(f, nj, D, C)
    body, in_specs, args = _with_dep(body, dep, in_specs, [dxo, x, g, zg, zu, ga, ga, gb])
    return pl.pallas_call(
        body, name="ffn_bwd", grid=(T // tm, nj),
        in_specs=in_specs,
        out_specs=[tok, tok, chunk, chunk, row],
        out_shape=[jax.ShapeDtypeStruct((T, D), F32), jax.ShapeDtypeStruct((T, D), BF16),
                   jax.ShapeDtypeStruct((nj, T, C), BF16), jax.ShapeDtypeStruct((nj, T, C), BF16),
                   jax.ShapeDtypeStruct((1, D), F32)],
        scratch_shapes=[pltpu.VMEM((tm, D), BF16), pltpu.VMEM((tm, D), F32)],
        compiler_params=_params("arbitrary", "arbitrary"),
    )(*args)


def _ffn_bwd_dz(dxo, zg, zu, gb, f, tm, dep=None):
    T, D = dxo.shape
    nj, C = zg.shape[0], zg.shape[2]

    def body(dxo_ref, zg_ref, zu_ref, wo_ref, dy_ref, dzg_ref, dzu_ref, dy_scr):
        @pl.when(pl.program_id(1) == 0)
        def _():
            dyb = (0.5 * dxo_ref[...]).astype(BF16)
            dy_scr[...] = dyb
            dy_ref[...] = dyb

        wo = wo_ref[...].reshape(C, D)
        for part in range(FFN_PARTS):
            sl = pl.ds(part * (tm // FFN_PARTS), tm // FFN_PARTS)
            ds = _dot_nt(dy_scr[sl, :], wo)
            gt = zg_ref[sl, :].astype(F32)
            up = zu_ref[sl, :].astype(F32)
            sg = _sigmoid(gt)
            dzg_ref[sl, :] = (ds * up * (sg * (1.0 + gt * (1.0 - sg)))).astype(BF16)
            dzu_ref[sl, :] = (ds * (gt * sg)).astype(BF16)

    tok = pl.BlockSpec((tm, D), lambda i, j: (i, 0))
    chunk = pl.BlockSpec((None, tm, C), lambda i, j: (j, i, 0))
    in_specs = [tok, chunk, chunk, _ffn_weight_specs(f, nj, D, C)[2]]
    body, in_specs, args = _with_dep(body, dep, in_specs, [dxo, zg, zu, gb])
    return pl.pallas_call(
        body, name="ffn_bwd_dz", grid=(T // tm, nj),
        in_specs=in_specs, out_specs=[tok, chunk, chunk],
        out_shape=[jax.ShapeDtypeStruct((T, D), BF16), jax.ShapeDtypeStruct((nj, T, C), BF16),
                   jax.ShapeDtypeStruct((nj, T, C), BF16)],
        scratch_shapes=[pltpu.VMEM((tm, D), BF16)],
        compiler_params=_params("parallel", "arbitrary"),
    )(*args)


def _ffn_bwd_dx(dxo, x, g, dzg, dzu, ga, f, tm, dep=None):
    T, D = x.shape
    nj, C = ga.shape[0] // 2, ga.shape[3]

    def body(dxo_ref, x_ref, g_ref, dzg_ref, dzu_ref, wg_ref, wu_ref, dx_ref, dgn_ref, acc):
        i, j = pl.program_id(0), pl.program_id(1)

        @pl.when(j == 0)
        def _():
            acc[...] = jnp.zeros_like(acc)

        acc[...] += _dot_nt(dzg_ref[...], wg_ref[...]) + _dot_nt(dzu_ref[...], wu_ref[...])

        @pl.when(j == nj - 1)
        def _():
            dx, dg = _norm_bwd(acc[...], x_ref[...], g_ref[...])
            dx_ref[...] = dxo_ref[...] + dx

            @pl.when(i == 0)
            def _():
                dgn_ref[...] = dg

            @pl.when(i > 0)
            def _():
                dgn_ref[...] += dg

    tok = pl.BlockSpec((tm, D), lambda i, j: (i, 0))
    chunk = pl.BlockSpec((None, tm, C), lambda i, j: (j, i, 0))
    row = pl.BlockSpec((1, D), lambda i, j: (0, 0))
    in_specs = [tok, tok, row, chunk, chunk] + _ffn_weight_specs(f, nj, D, C)[:2]
    body, in_specs, args = _with_dep(body, dep, in_specs, [dxo, x, g, dzg, dzu, ga, ga])
    return pl.pallas_call(
        body, name="ffn_bwd_dx", grid=(T // tm, nj),
        in_specs=in_specs, out_specs=[tok, row],
        out_shape=[jax.ShapeDtypeStruct((T, D), F32), jax.ShapeDtypeStruct((1, D), F32)],
        scratch_shapes=[pltpu.VMEM((tm, D), F32)],
        compiler_params=_params("arbitrary", "arbitrary"),
    )(*args)


def _ffn_dw(h, dzg, dzu, s, dy, tk, dep=None):
    T, D = h.shape
    nj, C = s.shape[0], s.shape[2]
    nk = T // tk

    def body(h_ref, dzg_ref, dzu_ref, s_ref, dy_ref, dwin_ref, dwo_ref, ag, au, ao):
        k = pl.program_id(1)

        @pl.when(k == 0)
        def _():
            ag[...] = jnp.zeros_like(ag)
            au[...] = jnp.zeros_like(au)
            ao[...] = jnp.zeros_like(ao)

        hb = h_ref[...]
        ag[...] += _dot_tn(hb, dzg_ref[...])
        au[...] += _dot_tn(hb, dzu_ref[...])
        ao[...] += _dot_tn(s_ref[...], dy_ref[...])

        @pl.when(k == nk - 1)
        def _():
            dwin_ref[0] = ag[...].astype(BF16)
            dwin_ref[1] = au[...].astype(BF16)
            dwo_ref[...] = ao[...].astype(BF16)

    tok = pl.BlockSpec((tk, D), lambda j, k: (k, 0))
    chunk = pl.BlockSpec((None, tk, C), lambda j, k: (j, k, 0))
    body, in_specs, args = _with_dep(body, dep, [tok, chunk, chunk, chunk, tok], [h, dzg, dzu, s, dy])
    dwin, dwo = pl.pallas_call(
        body, name="ffn_dw", grid=(nj, nk),
        in_specs=in_specs,
        out_specs=[pl.BlockSpec((2, None, D, C), lambda j, k: (0, j, 0, 0)),
                   pl.BlockSpec((None, C, D), lambda j, k: (j, 0, 0))],
        out_shape=[jax.ShapeDtypeStruct((2, nj, D, C), BF16), jax.ShapeDtypeStruct((nj, C, D), BF16)],
        scratch_shapes=[pltpu.VMEM((D, C), F32), pltpu.VMEM((D, C), F32), pltpu.VMEM((C, D), F32)],
        compiler_params=_params("parallel", "arbitrary"),
    )(*args)
    return dwin.reshape(2 * nj, D, C), dwo


def _matmul_tn(a, b, tn, tk):
    T, Ka = a.shape
    N = b.shape[1]
    nk = T // tk

    def body(a_ref, b_ref, o_ref, acc):
        k = pl.program_id(1)

        @pl.when(k == 0)
        def _():
            acc[...] = jnp.zeros_like(acc)

        acc[...] += _dot_tn(a_ref[...], b_ref[...])

        @pl.when(k == nk - 1)
        def _():
            o_ref[...] = acc[...].astype(BF16)

    return pl.pallas_call(
        body, name="matmul_tn", grid=(N // tn, nk),
        in_specs=[pl.BlockSpec((tk, Ka), lambda n, k: (k, 0)), pl.BlockSpec((tk, tn), lambda n, k: (k, n))],
        out_specs=pl.BlockSpec((Ka, tn), lambda n, k: (0, n)),
        out_shape=jax.ShapeDtypeStruct((Ka, N), BF16),
        scratch_shapes=[pltpu.VMEM((Ka, tn), F32)],
        compiler_params=_params("parallel", "arbitrary"),
    )(a, b)


def _qkv_fwd(x, g, w, tm):
    T, D = x.shape
    N = w.shape[1]

    def body(x_ref, g_ref, w_ref, o_ref, h_ref):
        xv = x_ref[...]
        hb = (xv * _rstd(xv) * g_ref[...]).astype(BF16)
        h_ref[...] = hb
        o_ref[...] = _dot(hb, w_ref[...])

    return pl.pallas_call(
        body, name="qkv_fwd", grid=(T // tm,),
        in_specs=[pl.BlockSpec((tm, D), lambda i: (i, 0)), pl.BlockSpec((1, D), lambda i: (0, 0)),
                  pl.BlockSpec((D, N), lambda i: (0, 0))],
        out_specs=[pl.BlockSpec((tm, N), lambda i: (i, 0)), pl.BlockSpec((tm, D), lambda i: (i, 0))],
        out_shape=[jax.ShapeDtypeStruct((T, N), F32), jax.ShapeDtypeStruct((T, D), BF16)],
        compiler_params=_params("parallel"),
    )(x, g, w)


DILS = tuple(d for _, d in DILATED)


def _spread_specs(tm, T, dtype):
    specs = [pl.BlockSpec((4, d, tm // d, PAIR), lambda i: (0, 0, i, 0)) for d in DILS]
    shapes = [jax.ShapeDtypeStruct((4, d, T // d, PAIR), dtype) for d in DILS]
    return specs, shapes


def _spread(tile, y, outs, c, dtype):
    tm = y.shape[0]
    tile[...] = y
    for out, d in zip(outs, DILS):
        for r in range(d):
            out[c, r] = tile[pl.ds(r, tm // d, stride=d), :].astype(dtype)


def _collect(tile, ins, c):
    tm = tile.shape[0]
    first = True
    for ref, d in zip(ins, DILS):
        for r in range(d):
            rows = pl.ds(r, tm // d, stride=d) if d > 1 else pl.ds(0, tm)
            part = ref[c, r].astype(F32)
            tile[rows, :] = part if first else tile[rows, :] + part
        first = False
    return tile[...]


def _attn_prep(qkv, gains2, tm):
    T = qkv.shape[0]
    scale = HEAD_DIM ** -0.5
    n = len(DILS)

    def body(qkv_ref, g_ref, qb_ref, kb_ref, vb_ref, *rest):
        outs, tile = rest[:-1], rest[-1]
        lo = _lo_mask((tm, PAIR))

        def spread(kind, c, y):
            _spread(tile, y, outs[kind * n:(kind + 1) * n], c, BF16)

        def normed(c, gi, mult):
            xv = qkv_ref[:, c * PAIR:(c + 1) * PAIR]
            r = lax.rsqrt(_half_sum(xv * xv, lo) * (1.0 / HEAD_DIM) + EPS)
            y = xv * r * g_ref[gi:gi + 1, :]
            return y * mult if mult != 1.0 else y

        def both_halves(v):
            sw = pltpu.roll(v, HEAD_DIM, 1)
            return jnp.where(lo, v, sw), jnp.where(lo, sw, v)

        for c in range(4):
            spread(0, c, normed(c, 0, scale))
            spread(1, c, normed(4 + c, 1, 1.0))
            spread(2, c, qkv_ref[:, (8 + c) * PAIR:(9 + c) * PAIR])
            qb_ref[c] = normed(12 + c, 2, scale).astype(BF16)
        k0, k1 = both_halves(normed(16, 3, 1.0))
        kb_ref[0] = k0.astype(BF16)
        kb_ref[1] = k1.astype(BF16)
        v0, v1 = both_halves(qkv_ref[:, 17 * PAIR:18 * PAIR])
        vb_ref[0] = v0.astype(BF16)
        vb_ref[1] = v1.astype(BF16)

    four = pl.BlockSpec((4, tm, PAIR), lambda i: (0, i, 0))
    two = pl.BlockSpec((2, tm, PAIR), lambda i: (0, i, 0))
    s4 = jax.ShapeDtypeStruct((4, T, PAIR), BF16)
    s2 = jax.ShapeDtypeStruct((2, T, PAIR), BF16)
    specs, shapes = _spread_specs(tm, T, BF16)
    res = pl.pallas_call(
        body, name="attn_prep", grid=(T // tm,),
        in_specs=[pl.BlockSpec((tm, qkv.shape[1]), lambda i: (i, 0)), pl.BlockSpec((4, PAIR), lambda i: (0, 0))],
        out_specs=[four, two, two] + specs * 3,
        out_shape=[s4, s2, s2] + shapes * 3,
        scratch_shapes=[pltpu.VMEM((tm, PAIR), F32)],
        compiler_params=_params("parallel"),
    )(qkv, gains2)
    qb, kb, vb = res[:3]
    per_d = [tuple(res[3 + kind * n + di].reshape(4 * d, T // d, PAIR) for kind in range(3))
             for di, d in enumerate(DILS)]
    return qb, kb, vb, per_d


def _loop_blocks(nb, body, init, per_iter):
    u = math.gcd(nb, per_iter)

    def outer(i, carry):
        for k in range(u):
            carry = body(i * u + k, carry)
        return carry

    return lax.fori_loop(0, nb // u, outer, init)


def _key_window(b, nb, L, R, W):
    start = pl.multiple_of(jnp.clip(b * BQ - R, 0, L - W), HEAD_DIM)
    return start, jnp.where(b == 0, 1, jnp.where(b == nb - 1, 2, 0))


def _stack_heads(v, lo):
    z = jnp.zeros_like(v)
    return jnp.concatenate([jnp.where(lo, v, z), jnp.where(lo, z, v)], axis=0)


def _unstack_heads(v2, lo):
    return jnp.where(lo, v2[:BQ], v2[BQ:])


def _row_vector(v, lo):
    r = lax.broadcasted_iota(jnp.int32, (BQ, PAIR), 0)
    ln = lax.broadcasted_iota(jnp.int32, (BQ, PAIR), 1)
    diag = (ln % HEAD_DIM) == (r % HEAD_DIM)
    top = jnp.sum(jnp.where(diag & (r < HEAD_DIM), v, 0.0), axis=0, keepdims=True)
    bot = jnp.sum(jnp.where(diag & (r >= HEAD_DIM), v, 0.0), axis=0, keepdims=True)
    top8, bot8 = jnp.broadcast_to(top, (8, PAIR)), jnp.broadcast_to(bot, (8, PAIR))
    lo8 = _lo_mask((8, PAIR))
    head0 = jnp.where(lo8, top8, pltpu.roll(bot8, HEAD_DIM, 1))
    head1 = jnp.where(lo8, pltpu.roll(top8, HEAD_DIM, 1), bot8)
    return jnp.concatenate([head0, head1], axis=1)[:1]


def _units_per_step(nb, pairs_per_kv):
    return max(1, 16 // nb) if pairs_per_kv == 1 else 1


def _attn_fwd(q, kp, vp, bias4, sink, R, pairs_per_kv, pairs_per_bias):
    N, L, _ = q.shape
    W = BQ + 2 * R
    nb = L // BQ
    assert L >= W and nb >= 2
    G = _units_per_step(nb, pairs_per_kv)

    def body(sink_ref, q_ref, k_ref, v_ref, bias_ref, o_ref, lse_ref):
        n = pl.program_id(0)
        lo_q = _lo_mask((BQ, PAIR))
        first = lax.broadcasted_iota(jnp.int32, (2 * BQ, 1), 0) < BQ

        def blk(f, carry):
            g, b = f // nb, f % nb
            u = n * G + g
            sk = jnp.where(first, sink_ref[2 * u], sink_ref[2 * u + 1])
            q0 = pl.multiple_of(b * BQ, BQ)
            q2 = _stack_heads(q_ref[g, pl.ds(q0, BQ), :], lo_q)
            k0, variant = _key_window(b, nb, L, R, W)
            kw = k_ref[g, pl.ds(k0, W), :]
            vw = v_ref[g, pl.ds(k0, W), :]
            s = _dot_nt(q2, kw) + bias_ref[variant]
            m = jnp.maximum(jnp.max(s, axis=1, keepdims=True), sk)
            p = jnp.exp(s - m)
            l = jnp.sum(p, axis=1, keepdims=True) + jnp.exp(sk - m)
            o2 = _dot(p.astype(BF16), vw) / l
            o_ref[g, pl.ds(q0, BQ), :] = _unstack_heads(o2, lo_q)
            lse_ref[g, pl.ds(q0, BQ), :] = _unstack_heads(jnp.broadcast_to(m + jnp.log(l), (2 * BQ, PAIR)), lo_q)
            return carry

        _loop_blocks(G * nb, blk, 0, 16)

    qspec = pl.BlockSpec((G, L, PAIR), lambda n: (n, 0, 0))
    kspec = pl.BlockSpec((G, L, PAIR), lambda n: (n // pairs_per_kv, 0, 0))
    return pl.pallas_call(
        body, name="attn_fwd", grid=(N // G,),
        in_specs=[pl.BlockSpec(memory_space=pltpu.SMEM), qspec, kspec, kspec,
                  pl.BlockSpec((None, 3, 2 * BQ, W), lambda n: (n * G // pairs_per_bias, 0, 0, 0))],
        out_specs=[qspec, qspec],
        out_shape=[jax.ShapeDtypeStruct((N, L, PAIR), F32), jax.ShapeDtypeStruct((N, L, PAIR), F32)],
        compiler_params=_params("parallel"),
    )(sink, q, kp, vp, bias4)


def _attn_bwd(q, kp, vp, bias4t, sink, o, lse, do, R, pairs_per_kv, pairs_per_bias):
    N, L, _ = q.shape
    Nk = kp.shape[0]
    Pb = bias4t.shape[0]
    W = BQ + 2 * R
    nb = L // BQ
    assert L >= W and nb >= 2
    G = _units_per_step(nb, pairs_per_kv)

    def body(sink_ref, q_ref, k_ref, v_ref, bias_ref, o_ref, lse_ref, do_ref,
             dq_ref, dk_ref, dv_ref, dbias_ref, dsink_ref, dk_acc, dv_acc):
        n = pl.program_id(0)
        lo_q = _lo_mask((BQ, PAIR))
        first = lax.broadcasted_iota(jnp.int32, (1, 2 * BQ), 1) < BQ
        dsink_ref[...] = jnp.zeros_like(dsink_ref)

        @pl.when(n % pairs_per_kv == 0)
        def _():
            dk_acc[...] = jnp.zeros_like(dk_acc)
            dv_acc[...] = jnp.zeros_like(dv_acc)

        @pl.when((n * G) % pairs_per_bias == 0)
        def _():
            dbias_ref[...] = jnp.zeros_like(dbias_ref)

        def blk(f, carry):
            g, b = f // nb, f % nb
            u = n * G + g
            sk = jnp.where(first, sink_ref[2 * u], sink_ref[2 * u + 1])
            q0 = pl.multiple_of(b * BQ, BQ)
            q2 = _stack_heads(q_ref[g, pl.ds(q0, BQ), :], lo_q)
            k0, variant = _key_window(b, nb, L, R, W)
            kw = k_ref[g, pl.ds(k0, W), :]
            vw = v_ref[g, pl.ds(k0, W), :]
            dov = do_ref[g, pl.ds(q0, BQ), :]
            lse = _row_vector(lse_ref[g, pl.ds(q0, BQ), :], lo_q)
            delta = _row_vector(_half_sum(dov.astype(F32) * o_ref[g, pl.ds(q0, BQ), :], lo_q), lo_q)
            do2 = _stack_heads(dov.astype(BF16), lo_q)
            st = _dot_nt(kw, q2) + bias_ref[variant]
            pt = jnp.exp(st - lse)
            dst = pt * (_dot_nt(vw, do2) - delta)
            dstb = dst.astype(BF16)
            dbias_ref[variant] += dst
            dk_acc[g, pl.ds(k0, W), :] += _dot(dstb, q2)
            dv_acc[g, pl.ds(k0, W), :] += _dot(pt.astype(BF16), do2)
            dq_ref[g, pl.ds(q0, BQ), :] = _unstack_heads(_dot_tn(dstb, kw), lo_q).astype(BF16)
            dsink_ref[g, pl.ds(0, 1), :] -= jnp.exp(sk - lse) * delta
            return carry

        _loop_blocks(G * nb, blk, 0, 16)
        dk_ref[...] = dk_acc[...].astype(BF16)
        dv_ref[...] = dv_acc[...].astype(BF16)

    qspec = pl.BlockSpec((G, L, PAIR), lambda n: (n, 0, 0))
    kspec = pl.BlockSpec((G, L, PAIR), lambda n: (n // pairs_per_kv, 0, 0))
    return pl.pallas_call(
        body, name="attn_bwd", grid=(N // G,),
        in_specs=[pl.BlockSpec(memory_space=pltpu.SMEM), qspec, kspec, kspec,
                  pl.BlockSpec((None, 3, W, 2 * BQ), lambda n: (n * G // pairs_per_bias, 0, 0, 0)),
                  qspec, qspec, qspec],
        out_specs=[qspec, kspec, kspec,
                   pl.BlockSpec((None, 3, W, 2 * BQ), lambda n: (n * G // pairs_per_bias, 0, 0, 0)),
                   pl.BlockSpec((G, 8, 2 * BQ), lambda n: (n, 0, 0))],
        out_shape=[jax.ShapeDtypeStruct((N, L, PAIR), BF16),
                   jax.ShapeDtypeStruct((Nk, L, PAIR), BF16),
                   jax.ShapeDtypeStruct((Nk, L, PAIR), BF16),
                   jax.ShapeDtypeStruct((Pb, 3, W, 2 * BQ), F32),
                   jax.ShapeDtypeStruct((N, 8, 2 * BQ), F32)],
        scratch_shapes=[pltpu.VMEM((G, L, PAIR), F32), pltpu.VMEM((G, L, PAIR), F32)],
        compiler_params=_params("arbitrary"),
    )(sink, q, kp, vp, bias4t, o, lse, do)


def _attn_merge(branch_outs, ob, tm):
    T = ob.shape[1]
    n = len(DILS)

    def body(*refs):
        o_in, l_in, ob_ref = refs[:n], refs[n:2 * n], refs[2 * n]
        o_out, l_out, cat_ref = refs[2 * n + 1:3 * n + 1], refs[3 * n + 1:4 * n + 1], refs[4 * n + 1]
        tiles = refs[4 * n + 2:]
        for c in range(4):
            o_nat, l_nat = [], []
            for di, d in enumerate(DILS):
                for kind, (src, dst) in enumerate(((o_in[di], o_nat), (l_in[di], l_nat))):
                    tile = tiles[2 * di + kind]
                    if d == 1:
                        dst.append(src[c, 0])
                    else:
                        for r in range(d):
                            tile[pl.ds(r, tm // d, stride=d), :] = src[c, r]
                        dst.append(tile[...])
            m = functools.reduce(jnp.maximum, l_nat)
            ws = [jnp.exp(l - m) for l in l_nat]
            z = sum(ws)
            o = sum(w * t for w, t in zip(ws, o_nat)) / z
            cat_ref[:, c * PAIR:(c + 1) * PAIR] = o.astype(BF16)
            cat_ref[:, (4 + c) * PAIR:(5 + c) * PAIR] = ob_ref[c].astype(BF16)
            _spread(tiles[0], o, o_out, c, F32)
            _spread(tiles[1], m + jnp.log(z), l_out, c, F32)

    specs, shapes = _spread_specs(tm, T, F32)
    four = pl.BlockSpec((4, tm, PAIR), lambda i: (0, i, 0))
    o_views = [o.reshape(4, d, T // d, PAIR) for (o, _), d in zip(branch_outs, DILS)]
    l_views = [l.reshape(4, d, T // d, PAIR) for (_, l), d in zip(branch_outs, DILS)]
    res = pl.pallas_call(
        body, name="attn_merge", grid=(T // tm,),
        in_specs=specs + specs + [four],
        out_specs=specs + specs + [pl.BlockSpec((tm, 8 * PAIR), lambda i: (i, 0))],
        out_shape=shapes + shapes + [jax.ShapeDtypeStruct((T, 8 * PAIR), BF16)],
        scratch_shapes=[pltpu.VMEM((tm, PAIR), F32)] * (2 * n),
        compiler_params=_params("parallel"),
    )(*o_views, *l_views, ob)
    merged = [(res[di].reshape(4 * d, T // d, PAIR), res[n + di].reshape(4 * d, T // d, PAIR))
              for di, d in enumerate(DILS)]
    return merged, res[2 * n]


def _weight_arg(w, blk):
    if blk is None:
        return pl.BlockSpec(w.shape, lambda i: (0, 0)), (lambda ref: ref[...])
    D = w.shape[2]
    return (pl.BlockSpec((N_DEV, 128, D), lambda i: (0, blk, 0)),
            lambda ref: ref[...].reshape(N_DEV * 128, D))


def _oproj_fwd(x, o_cat, w, blk, tm):
    T, D = x.shape
    wspec, wload = _weight_arg(w, blk)

    def body(x_ref, o_ref, w_ref, out_ref):
        out_ref[...] = x_ref[...] + _dot(o_ref[...], wload(w_ref))

    tok = pl.BlockSpec((tm, D), lambda i: (i, 0))
    return pl.pallas_call(
        body, name="oproj_fwd", grid=(T // tm,),
        in_specs=[tok, pl.BlockSpec((tm, o_cat.shape[1]), lambda i: (i, 0)), wspec],
        out_specs=tok, out_shape=jax.ShapeDtypeStruct((T, D), F32),
        compiler_params=_params("parallel"),
    )(x, o_cat, w)


def _oproj_bwd(dx, w, blk, tm, dep=None):
    T, D = dx.shape
    wspec, wload = _weight_arg(w, blk)

    def body(dx_ref, w_ref, dxb_ref, dob_ref, *rest):
        doa_refs, tile = rest[:-1], rest[-1]
        db = dx_ref[...].astype(BF16)
        dxb_ref[...] = db
        do = _dot_nt(db, wload(w_ref))
        for c in range(4):
            _spread(tile, do[:, c * PAIR:(c + 1) * PAIR], doa_refs, c, BF16)
            dob_ref[c] = do[:, (4 + c) * PAIR:(5 + c) * PAIR].astype(BF16)

    tok = pl.BlockSpec((tm, D), lambda i: (i, 0))
    specs, shapes = _spread_specs(tm, T, BF16)
    body, in_specs, args = _with_dep(body, dep, [tok, wspec], [dx, w])
    res = pl.pallas_call(
        body, name="oproj_bwd", grid=(T // tm,),
        in_specs=in_specs,
        out_specs=[tok, pl.BlockSpec((4, tm, PAIR), lambda i: (0, i, 0))] + specs,
        out_shape=[jax.ShapeDtypeStruct((T, D), BF16), jax.ShapeDtypeStruct((4, T, PAIR), BF16)] + shapes,
        scratch_shapes=[pltpu.VMEM((tm, PAIR), F32)],
        compiler_params=_params("parallel"),
    )(*args)
    return res[0], res[1], [t.reshape(4 * d, T // d, PAIR) for t, d in zip(res[2:], DILS)]


def _attn_post(qkv, gains2, dqa, dka, dva, dqb, dkb, dvb, tm):
    T, NQ = qkv.shape
    scale = HEAD_DIM ** -0.5

    n = len(DILS)

    def body(qkv_ref, g_ref, *rest):
        dq_refs, dk_refs, dv_refs = rest[:n], rest[n:2 * n], rest[2 * n:3 * n]
        qb_ref, kb_ref, vb_ref, out_ref, dg_ref, tile = rest[3 * n:]
        lo = _lo_mask((tm, PAIR))

        @pl.when(pl.program_id(0) == 0)
        def _():
            dg_ref[...] = jnp.zeros_like(dg_ref)

        def norm_bwd(c, gi, dy):
            xv = qkv_ref[:, c * PAIR:(c + 1) * PAIR]
            r = lax.rsqrt(_half_sum(xv * xv, lo) * (1.0 / HEAD_DIM) + EPS)
            xn = xv * r
            dg_ref[gi:gi + 1, :] += jnp.sum(dy * xn, axis=0, keepdims=True)
            dxn = dy * g_ref[gi:gi + 1, :]
            dx = r * (dxn - xn * (_half_sum(dxn * xn, lo) * (1.0 / HEAD_DIM)))
            out_ref[:, c * PAIR:(c + 1) * PAIR] = dx.astype(BF16)

        def fold(v):
            return v + pltpu.roll(v, HEAD_DIM, 1)

        for c in range(4):
            norm_bwd(c, 0, _collect(tile, dq_refs, c) * scale)
            norm_bwd(4 + c, 1, _collect(tile, dk_refs, c))
            out_ref[:, (8 + c) * PAIR:(9 + c) * PAIR] = _collect(tile, dv_refs, c).astype(BF16)
            norm_bwd(12 + c, 2, qb_ref[c].astype(F32) * scale)
        kb, vb = kb_ref[...].astype(F32), vb_ref[...].astype(F32)
        norm_bwd(16, 3, jnp.where(lo, fold(kb[0]), fold(kb[1])))
        out_ref[:, 17 * PAIR:18 * PAIR] = jnp.where(lo, fold(vb[0]), fold(vb[1])).astype(BF16)

    four = pl.BlockSpec((4, tm, PAIR), lambda i: (0, i, 0))
    two = pl.BlockSpec((2, tm, PAIR), lambda i: (0, i, 0))
    specs, _ = _spread_specs(tm, T, BF16)
    views = [t.reshape(4, d, T // d, PAIR) for group in (dqa, dka, dva) for t, d in zip(group, DILS)]
    return pl.pallas_call(
        body, name="attn_post", grid=(T // tm,),
        in_specs=[pl.BlockSpec((tm, NQ), lambda i: (i, 0)), pl.BlockSpec((4, PAIR), lambda i: (0, 0))]
        + specs * 3 + [four, two, two],
        out_specs=[pl.BlockSpec((tm, NQ), lambda i: (i, 0)), pl.BlockSpec((4, PAIR), lambda i: (0, 0))],
        out_shape=[jax.ShapeDtypeStruct((T, NQ), BF16), jax.ShapeDtypeStruct((4, PAIR), F32)],
        scratch_shapes=[pltpu.VMEM((tm, PAIR), F32)],
        compiler_params=_params("arbitrary"),
    )(qkv, gains2, *views, dqb, dkb, dvb)


def _dense_norm_bwd(dres, dz, w, blk, x, g, tm):
    T, D = x.shape
    N = dz.shape[1]
    wspec, wload = _weight_arg(w, blk)

    def body(dres_ref, dz_ref, w_ref, x_ref, g_ref, dx_ref, dgn_ref):
        i = pl.program_id(0)
        dx, dg = _norm_bwd(_dot_nt(dz_ref[...], wload(w_ref)), x_ref[...], g_ref[...])
        dx_ref[...] = dres_ref[...] + dx

        @pl.when(i == 0)
        def _():
            dgn_ref[...] = dg

        @pl.when(i > 0)
        def _():
            dgn_ref[...] += dg

    tok = pl.BlockSpec((tm, D), lambda i: (i, 0))
    row = pl.BlockSpec((1, D), lambda i: (0, 0))
    return pl.pallas_call(
        body, name="dense_norm_bwd", grid=(T // tm,),
        in_specs=[tok, pl.BlockSpec((tm, N), lambda i: (i, 0)), wspec, tok, row],
        out_specs=[tok, row],
        out_shape=[jax.ShapeDtypeStruct((T, D), F32), jax.ShapeDtypeStruct((1, D), F32)],
        compiler_params=_params("arbitrary"),
    )(dres, dz, w, x, g)


def _bias_reduce(onehot, dbm):
    Hb, K = dbm.shape

    def body(oh_ref, d_ref, out_ref):
        oh = oh_ref[...]
        d = d_ref[...]
        hi = d.astype(BF16)
        r1 = d - hi.astype(F32)
        mid = r1.astype(BF16)
        low = (r1 - mid.astype(F32)).astype(BF16)
        out_ref[...] = _dot_nt(hi, oh) + _dot_nt(mid, oh) + _dot_nt(low, oh)

    vm = pl.BlockSpec(memory_space=pltpu.VMEM)
    return pl.pallas_call(
        body, name="bias_reduce", in_specs=[vm, vm], out_specs=vm,
        out_shape=jax.ShapeDtypeStruct((Hb, N_BUCKETS), F32),
        compiler_params=pltpu.CompilerParams(vmem_limit_bytes=VMEM_LIMIT),
    )(onehot, dbm)


def _ple_fwd(x, g, wg, blk, p, wp, target, tm):
    T, D = x.shape
    P = p.shape[1]
    with_loss = target is not None
    wspec, wload = _weight_arg(wg, blk)

    def body(*refs):
        if with_loss:
            x_ref, g_ref, wg_ref, p_ref, wp_ref, t_ref, y_ref, hn_ref, gate_ref, pp_ref, pb_ref, loss_ref = refs
        else:
            x_ref, g_ref, wg_ref, p_ref, wp_ref, y_ref, hn_ref, gate_ref, pp_ref, pb_ref = refs
        i = pl.program_id(0)
        xv = x_ref[...]
        hb = (xv * _rstd(xv) * g_ref[...]).astype(BF16)
        hn_ref[...] = hb
        gate = _sigmoid(_dot(hb, wload(wg_ref)))
        pb = p_ref[...].astype(BF16)
        pb_ref[...] = pb
        pp = _dot(pb, wp_ref[...])
        gate_ref[...] = gate
        pp_ref[...] = pp
        y = xv + gate * pp
        if with_loss:
            err = y - t_ref[...]
            y_ref[...] = err * (1.0 / D)
            part = jnp.broadcast_to(0.5 * jnp.sum(jnp.sum(err * err, axis=1, keepdims=True) * (1.0 / D),
                                                  axis=0, keepdims=True), (1, 128))

            @pl.when(i == 0)
            def _():
                loss_ref[...] = part

            @pl.when(i > 0)
            def _():
                loss_ref[...] += part
        else:
            y_ref[...] = y

    tok = pl.BlockSpec((tm, D), lambda i: (i, 0))
    ptok = pl.BlockSpec((tm, P), lambda i: (i, 0))
    in_specs = [tok, pl.BlockSpec((1, D), lambda i: (0, 0)), wspec, ptok,
                pl.BlockSpec((P, D), lambda i: (0, 0))]
    out_specs = [tok, tok, tok, tok, ptok]
    out_shape = [jax.ShapeDtypeStruct((T, D), F32), jax.ShapeDtypeStruct((T, D), BF16),
                 jax.ShapeDtypeStruct((T, D), F32), jax.ShapeDtypeStruct((T, D), F32),
                 jax.ShapeDtypeStruct((T, P), BF16)]
    args = [x, g, wg, p, wp]
    if with_loss:
        in_specs.append(tok)
        out_specs.append(pl.BlockSpec((1, 128), lambda i: (0, 0)))
        out_shape.append(jax.ShapeDtypeStruct((1, 128), F32))
        args.append(target)
    return pl.pallas_call(
        body, name="ple_fwd_loss" if with_loss else "ple_fwd", grid=(T // tm,),
        in_specs=in_specs, out_specs=out_specs, out_shape=out_shape,
        compiler_params=_params("arbitrary" if with_loss else "parallel"),
    )(*args)


def _ple_bwd(dy, gate, pp, tm, dep=None):
    T, D = dy.shape

    def body(dy_ref, gate_ref, pp_ref, dgl_ref, dpp_ref):
        d = dy_ref[...]
        gt = gate_ref[...]
        dgl_ref[...] = (d * pp_ref[...] * gt * (1.0 - gt)).astype(BF16)
        dpp_ref[...] = (d * gt).astype(BF16)

    tok = pl.BlockSpec((tm, D), lambda i: (i, 0))
    body, in_specs, args = _with_dep(body, dep, [tok, tok, tok], [dy, gate, pp])
    return pl.pallas_call(
        body, name="ple_bwd", grid=(T // tm,), in_specs=in_specs, out_specs=[tok, tok],
        out_shape=[jax.ShapeDtypeStruct((T, D), BF16), jax.ShapeDtypeStruct((T, D), BF16)],
        compiler_params=_params("parallel"),
    )(*args)


def _adamw(w, g, m, v):
    shape = w.shape
    C = shape[-1]
    w2, g2, m2, v2 = (a.reshape(-1, C) for a in (w, g, m, v))
    Rn = w2.shape[0]
    tr = Rn
    for cand in (512, 352, 256):
        if Rn % cand == 0:
            tr = cand
            break
    c1 = 1.0 - ADAM_B1 ** ADAM_STEP
    c2 = 1.0 - ADAM_B2 ** ADAM_STEP

    def body(w_ref, g_ref, m_ref, v_ref, d_ref, nm_ref, nv_ref):
        gv = g_ref[...]
        mn = ADAM_B1 * m_ref[...] + (1.0 - ADAM_B1) * gv
        vn = ADAM_B2 * v_ref[...] + (1.0 - ADAM_B2) * (gv * gv)
        d_ref[...] = -ADAM_LR * ((mn / c1) / (jnp.sqrt(vn / c2) + ADAM_EPS) + ADAM_WD * w_ref[...])
        nm_ref[...] = mn
        nv_ref[...] = vn

    spec = pl.BlockSpec((tr, C), lambda i: (i, 0))
    sh = jax.ShapeDtypeStruct((Rn, C), F32)
    d, nm, nv = pl.pallas_call(
        body, name="adamw", grid=(Rn // tr,), in_specs=[spec] * 4, out_specs=[spec] * 3, out_shape=[sh] * 3,
        compiler_params=_params("parallel"),
    )(w2, g2, m2, v2)
    return d.reshape(shape), nm.reshape(shape), nv.reshape(shape)


def _my_place():
    x, y, c = lax.axis_index("x"), lax.axis_index("y"), lax.axis_index("c")
    chips = [(1 - x, y), (x, 1 - y), (1 - x, 1 - y)]
    return x, y, c, chips


def _all_gather(arrs):
    n = len(arrs)

    def body(*refs):
        x_refs, out_refs = refs[:n], refs[n:2 * n]
        send_sems, recv_sems, local_sems = refs[2 * n:]
        x, y, c, chips = _my_place()
        me, sibling = (x, y, c), (x, y, 1 - c)

        def copy(m, k, block, to, src=None):
            rows = out_refs[m].at[4 * block[0] + 2 * block[1] + block[2]]
            return pltpu.make_async_remote_copy(
                src_ref=rows if src is None else src, dst_ref=rows,
                send_sem=send_sems.at[7 * m + k], recv_sem=recv_sems.at[7 * m + k], device_id=to, device_id_type=MESH)

        mine = [pltpu.make_async_copy(x_refs[m], out_refs[m].at[4 * x + 2 * y + c], local_sems.at[m])
                for m in range(n)]
        for cp in mine:
            cp.start()
        first = []
        for m in range(n):
            first.append(copy(m, 0, me, sibling, src=x_refs[m]))
            first += [copy(m, 1 + j, me, (*chip, c), src=x_refs[m]) for j, chip in enumerate(chips)]
        for cp in first:
            cp.start()
        passed = []
        for m in range(n):
            for j, chip in enumerate(chips):
                copy(m, 1 + j, (*chip, c), me).wait_recv()
                cp = copy(m, 4 + j, (*chip, c), sibling)
                cp.start()
                passed.append(cp)
        for m in range(n):
            copy(m, 0, sibling, me).wait_recv()
            for j, chip in enumerate(chips):
                copy(m, 4 + j, (*chip, 1 - c), me).wait_recv()
        for cp in first + passed:
            cp.wait_send()
        for cp in mine:
            cp.wait()

    hbm = pl.BlockSpec(memory_space=pl.ANY)
    return pl.pallas_call(
        body, name="all_gather", in_specs=[hbm] * n, out_specs=[hbm] * n,
        out_shape=[jax.ShapeDtypeStruct((N_DEV,) + a.shape, a.dtype) for a in arrs],
        scratch_shapes=[pltpu.SemaphoreType.DMA((7 * n,)), pltpu.SemaphoreType.DMA((7 * n,)),
                        pltpu.SemaphoreType.DMA((n,))],
    )(*arrs)


def _peer(x, y, c, k):
    return (x ^ ((k >> 2) & 1), y ^ ((k >> 1) & 1), c ^ (k & 1))


HBM_SPEC = pl.BlockSpec(memory_space=pltpu.HBM)
SEM_SPEC = pl.BlockSpec(memory_space=pltpu.SEMAPHORE)


def _exchange_refs(srcs, lands, m, k, x, y, c, scatter):
    peer = _peer(x, y, c, k)
    if scatter:
        return srcs[m].at[4 * peer[0] + 2 * peer[1] + peer[2]], lands[m].at[k - 1], peer
    return srcs[m], lands[m].at[4 * x + 2 * y + c], peer


def _exchange_start(arrs, land_shapes, scatter, name):
    n = len(arrs)

    def body(*refs):
        srcs, lands = refs[:n], refs[n:2 * n]
        send_sems, recv_sems = refs[2 * n], refs[2 * n + 1]
        token = refs[-1]
        x, y, c, _ = _my_place()
        for m in range(n):
            for k in range(1, N_DEV):
                src, dst, peer = _exchange_refs(srcs, lands, m, k, x, y, c, scatter)
                pltpu.make_async_remote_copy(
                    src_ref=src, dst_ref=dst, send_sem=send_sems.at[7 * m + k - 1],
                    recv_sem=recv_sems.at[7 * m + k - 1], device_id=peer, device_id_type=MESH).start()
        token[...] = jnp.zeros_like(token)

    zones = [lax.empty(s_, a.dtype) for s_, a in zip(land_shapes, arrs)]
    outs = pl.pallas_call(
        body, name=name,
        out_shape=(pltpu.SemaphoreType.DMA((7 * n,)), pltpu.SemaphoreType.DMA((7 * n,)),
                   *[pltpu.HBM(a.shape, a.dtype) for a in arrs], *[pltpu.HBM(z.shape, z.dtype) for z in zones],
                   jax.ShapeDtypeStruct((8, 128), F32)),
        in_specs=[HBM_SPEC] * (2 * n),
        out_specs=(SEM_SPEC, SEM_SPEC, *[HBM_SPEC] * (2 * n), pl.BlockSpec(memory_space=pltpu.VMEM)),
        input_output_aliases={m: 2 + m for m in range(2 * n)},
        compiler_params=pltpu.CompilerParams(has_side_effects=pltpu.SideEffectType.DATAFLOW_SIDE_EFFECTING),
    )(*[pltpu.with_memory_space_constraint(a, pltpu.HBM) for a in arrs],
      *[pltpu.with_memory_space_constraint(z, pltpu.HBM) for z in zones])
    return outs[0], outs[1], list(outs[2:2 + n]), list(outs[2 + n:2 + 2 * n]), outs[-1]


def _exchange_wait(send_sems, recv_sems, arrs, zones, after, scatter, name):
    n = len(arrs)
    afters = list(after) if isinstance(after, (list, tuple)) else [after]

    def body(*refs):
        srcs, lands = refs[:n], refs[n:2 * n]
        send_sems, recv_sems = refs[2 * n], refs[2 * n + 1]
        x, y, c, _ = _my_place()
        for m in range(n):
            for k in range(1, N_DEV):
                src, dst, peer = _exchange_refs(srcs, lands, m, k, x, y, c, scatter)
                cp = pltpu.make_async_remote_copy(
                    src_ref=src, dst_ref=dst, send_sem=send_sems.at[7 * m + k - 1],
                    recv_sem=recv_sems.at[7 * m + k - 1], device_id=peer, device_id_type=MESH)
                cp.wait_send()
                cp.wait_recv()

    outs = pl.pallas_call(
        body, name=name,
        out_shape=tuple(pltpu.HBM(a.shape, a.dtype) for a in list(arrs) + list(zones)),
        in_specs=[HBM_SPEC] * (2 * n) + [SEM_SPEC, SEM_SPEC] + [pl.BlockSpec(memory_space=pl.ANY)] * len(afters),
        out_specs=tuple([HBM_SPEC] * (2 * n)),
        input_output_aliases={m: m for m in range(2 * n)},
        compiler_params=pltpu.CompilerParams(has_side_effects=pltpu.SideEffectType.DATAFLOW_SIDE_EFFECTING),
    )(*arrs, *zones, send_sems, recv_sems, *afters)
    return list(outs[n:])


def _sum_parts(own, parts, tr, dep=None):
    R, W = own.shape

    def body(own_ref, parts_ref, out_ref):
        acc = own_ref[...].astype(F32)
        for k in range(N_DEV - 1):
            acc = acc + parts_ref[k].astype(F32)
        out_ref[...] = acc

    in_specs = [pl.BlockSpec((tr, W), lambda i: (i, 0)), pl.BlockSpec((N_DEV - 1, tr, W), lambda i: (0, i, 0))]
    body, in_specs, args = _with_dep(body, dep, in_specs, [own, parts])
    return pl.pallas_call(
        body, name="sum_parts", grid=(R // tr,),
        in_specs=in_specs,
        out_specs=pl.BlockSpec((tr, W), lambda i: (i, 0)),
        out_shape=jax.ShapeDtypeStruct((R, W), F32),
        compiler_params=_params("parallel"),
    )(*args)


def _all_reduce_small(v, dep=None):
    Rn, Wd = v.shape

    def body(v_ref, out_ref, gat_ref, send_sems, recv_sems):
        x, y, c, _ = _my_place()
        me = 4 * x + 2 * y + c
        gat_ref[me] = v_ref[...]
        copies = []
        for k in range(1, N_DEV):
            fx, fy, fc = (k >> 2) & 1, (k >> 1) & 1, k & 1
            peer = (x ^ fx, y ^ fy, c ^ fc)
            cp = pltpu.make_async_remote_copy(
                src_ref=v_ref, dst_ref=gat_ref.at[me], send_sem=send_sems.at[k - 1], recv_sem=recv_sems.at[k - 1],
                device_id=peer, device_id_type=MESH)
            cp.start()
            copies.append(cp)
        for cp in copies:
            cp.wait_recv()
        for cp in copies:
            cp.wait_send()
        acc = gat_ref[0]
        for k in range(1, N_DEV):
            acc = acc + gat_ref[k]
        out_ref[...] = acc

    vm = pl.BlockSpec(memory_space=pltpu.VMEM)
    body, in_specs, args = _with_dep(body, dep, [vm], [v])
    return pl.pallas_call(
        body, name="all_reduce_small", in_specs=in_specs, out_specs=vm,
        out_shape=jax.ShapeDtypeStruct((Rn, Wd), F32),
        scratch_shapes=[pltpu.VMEM((N_DEV, Rn, Wd), F32), pltpu.SemaphoreType.DMA((7,)),
                        pltpu.SemaphoreType.DMA((7,))],
    )(*args)


def _t5_bucket(rel):
    half = N_BUCKETS // 2
    max_exact = half // 2
    ret = jnp.where(rel > 0, half, 0)
    n = jnp.abs(rel)
    nf = jnp.maximum(n, 1).astype(F32)
    large = max_exact + (jnp.log(nf / max_exact) / math.log(MAX_DISTANCE / max_exact)
                         * (half - max_exact)).astype(jnp.int32)
    large = jnp.minimum(large, half - 1)
    return ret + jnp.where(n < max_exact, n, large)


def _band(R, d):
    W = BQ + 2 * R
    rel = jnp.arange(W)[None, :] - R - jnp.arange(BQ)[:, None]
    return _t5_bucket(rel * d), jnp.abs(rel) <= R


def _onehot(R, d):
    bkt, in_band = _band(R, d)
    return ((bkt.reshape(1, -1) == jnp.arange(N_BUCKETS)[:, None]) & in_band.reshape(1, -1)).astype(BF16)


def _bias_expand(table_t, onehot):
    H = table_t.shape[0]
    K = onehot.shape[1]

    def body(t_ref, oh_ref, out_ref):
        oh = oh_ref[...]
        t = t_ref[...]
        hi = t.astype(BF16)
        r1 = t - hi.astype(F32)
        mid = r1.astype(BF16)
        low = (r1 - mid.astype(F32)).astype(BF16)
        marked = _dot(jnp.ones(t.shape, BF16), oh) > 0.5
        out_ref[...] = jnp.where(marked, _dot(hi, oh) + _dot(mid, oh) + _dot(low, oh), NEG)

    vm = pl.BlockSpec(memory_space=pltpu.VMEM)
    return pl.pallas_call(
        body, name="bias_expand", in_specs=[vm, vm], out_specs=vm,
        out_shape=jax.ShapeDtypeStruct((H, K), F32),
        compiler_params=pltpu.CompilerParams(vmem_limit_bytes=VMEM_LIMIT),
    )(table_t, onehot)


def _bias_matrix(table, R, d):
    return _bias_expand(table.T, _onehot(R, d)).reshape(table.shape[1], BQ, BQ + 2 * R)


def _bias_variants(base, R):
    H, _, W = base.shape
    fill = jnp.full((H, BQ, R), NEG, F32)
    first = jnp.concatenate([base[:, :, R:], fill], axis=2)
    last = jnp.concatenate([fill, base[:, :, :W - R]], axis=2)
    v = jnp.stack([base, first, last], axis=1)
    v = v.reshape(H // 2, 2, 3, BQ, W).transpose(0, 2, 1, 3, 4).reshape(H // 2, 3, 2 * BQ, W)
    return v, v.transpose(0, 1, 3, 2)


def _bias_grad(dbt, R, d):
    P, _, W, _ = dbt.shape
    dbt = dbt[:, 0].at[:, R:].add(dbt[:, 1, :W - R]).at[:, :W - R].add(dbt[:, 2, R:])
    dbm = dbt.reshape(P, W, 2, BQ).transpose(0, 2, 3, 1).reshape(2 * P, BQ * W)
    return _bias_reduce(_onehot(R, d), dbm).T


def _tile2(gain):
    return jnp.concatenate([gain, gain])


ROW_W_O, ROW_GATE, B_ROWS = 768, 896, 1024
BLK_W_O, BLK_GATE = ROW_W_O // 128, ROW_GATE // 128


def _pack_layer(wts, i):
    a = jnp.stack([wts["ffn1_w_in"][i], wts["ffn2_w_in"][i]])
    D = a.shape[1]
    b = jnp.concatenate([
        wts["ffn1_w_out"][i], wts["ffn2_w_out"][i],
        jnp.zeros((ROW_W_O - 2 * wts["ffn1_w_out"].shape[1], D), a.dtype), wts["w_o"][i], wts["w_ple_gate"][i]])
    return a, b, wts["w_qkv"][i], wts["w_ple_proj"][i]


def _unpack_layer(sums, like):
    w_in2, b1, proj, w_o, qkv, w_in1, w_out1 = sums
    n_out = like["ffn1_w_out"].shape[1]
    out = {}
    if w_in2 is not None:
        out.update(ffn2_w_in=w_in2, ffn2_w_out=b1[:n_out], w_ple_gate=b1[n_out:], w_ple_proj=proj)
    if w_o is not None:
        out.update(w_o=w_o, w_qkv=qkv)
    if w_in1 is not None:
        out.update(ffn1_w_in=w_in1, ffn1_w_out=w_out1)
    return out


def _col_sharded(g):
    return g.transpose(1, 0, 2).reshape(g.shape[1], -1)


def _to_col_shards(g):
    rows = g.shape[0]
    return g.reshape(rows, N_DEV, -1).transpose(1, 0, 2)


def _layer_weights(ga, gb, gq, gp):
    return dict(ga=ga, gb=gb, w_qkv=_col_sharded(gq), w_proj=_col_sharded(gp))


def _layer_fwd(x, p, w, sm, i, target, tm, biases, dep=None):
    ga, gb = w["ga"], w["gb"]
    saved = {}
    saved["x0"] = x
    x1, saved["h1"], saved["zg1"], saved["zu1"], saved["s1"] = _ffn_fwd(
        x, sm["norm_ffn1"][i][None], ga, gb, 0, 2 * tm, dep)
    saved["x1"] = x1
    qkv, saved["hm"] = _qkv_fwd(x1, sm["norm_mix"][i][None], w["w_qkv"], 2 * tm)
    saved["qkv"] = qkv
    gains2 = jnp.stack([_tile2(sm[k][i]) for k in ("q_norm_a", "k_norm_a", "q_norm_b", "k_norm_b")])
    saved["gains2"] = gains2
    qb, kb, vb, qkv_d = _attn_prep(qkv, gains2, tm)
    no_sink = jnp.full((8,), NEG, F32)
    branches = []
    outs = []
    for (R, d), bias, (qd, kd, vd) in zip(DILATED, biases[:3], qkv_d):
        sink = jnp.tile(no_sink, d)
        outs.append(_attn_fwd(qd, kd, vd, bias[0], sink, R, 1, d))
        branches.append((qd, kd, vd, bias, sink, R, d))
    bias_b = biases[3]
    sink_b = sm["sink_b"][i]
    ob, lb = _attn_fwd(qb, kb, vb, bias_b[0], sink_b, SWA_RADIUS, 2, 1)
    merged, o_cat = _attn_merge(outs, ob, tm)
    saved.update(branches=branches, b=(qb, kb, vb, bias_b, sink_b), merged=merged, ob=ob, lb=lb, o_cat=o_cat)
    x2 = _oproj_fwd(x1, o_cat, gb, BLK_W_O, 2 * tm)
    saved["x2"] = x2
    x3, saved["h2"], saved["zg2"], saved["zu2"], saved["s2"] = _ffn_fwd(
        x2, sm["norm_ffn2"][i][None], ga, gb, 1, 2 * tm)
    saved["x3"] = x3
    res = _ple_fwd(x3, sm["norm_ple"][i][None], gb, BLK_GATE, p, w["w_proj"], target, tm)
    y, saved["hp"], saved["gate"], saved["pp"], saved["pb"] = res[:5]
    loss = res[5] if target is not None else None
    return y, loss, saved


def _layer_bwd(dy, w, sm, i, sv, tm, dep=None, on_ready=None, on_small=None, on_last=None):
    ga, gb = w["ga"], w["gb"]
    gs = {}
    D = dy.shape[1]
    dgl, dpp = _ple_bwd(dy, sv["gate"], sv["pp"], tm, dep)
    d_gate = _matmul_tn(sv["hp"], dgl, D, 4 * tm)
    d_proj = _matmul_tn(sv["pb"], dpp, D, 4 * tm)
    dx3, gs["norm_ple"] = _dense_norm_bwd(dy, dgl, gb, BLK_GATE, sv["x3"], sm["norm_ple"][i][None], 2 * tm)
    dx2, dyb, dzg, dzu, gs["norm_ffn2"] = _ffn_bwd(dx3, sv["x2"], sm["norm_ffn2"][i][None], sv["zg2"], sv["zu2"],
                                                   ga, gb, 1, tm)
    dwin2, dwo2 = _ffn_dw(sv["h2"], dzg, dzu, sv["s2"], dyb, 4 * tm)
    half = dwo2.shape[1] // 2
    after_ffn2 = [dwin2, jnp.concatenate([dwo2.reshape(N_DEV, half, D), d_gate.reshape(N_DEV, -1, D)], axis=1),
                  _to_col_shards(d_proj)]
    token = None if on_ready is None else on_ready(0, after_ffn2)
    dx2b, do_b, do_a = _oproj_bwd(dx2, gb, BLK_W_O, tm, token)
    d_wo = _matmul_tn(sv["o_cat"], dx2b, D, 4 * tm)
    dqa, dka, dva, dbias = [], [], [], []
    for (qd, kd, vd, bias, sink, R, d), (oa, la), do_d in zip(sv["branches"], sv["merged"], do_a):
        dq, dk, dv, dbm, _ = _attn_bwd(qd, kd, vd, bias[1], sink, oa, la, do_d, R, 1, d)
        dqa.append(dq)
        dka.append(dk)
        dva.append(dv)
        dbias.append(dbm)
    qb, kb, vb, bias_b, sink_b = sv["b"]
    dqb, dkb, dvb, dbm_b, dsink = _attn_bwd(qb, kb, vb, bias_b[1], sink_b, sv["ob"], sv["lb"], do_b,
                                            SWA_RADIUS, 2, 1)
    gs["rel_bias"] = dbias + [dbm_b]
    gs["sink_b"] = jnp.sum(dsink[:, 0].reshape(-1, 2, BQ), axis=2).reshape(-1)
    dqkv, dgains2 = _attn_post(sv["qkv"], sv["gains2"], dqa, dka, dva, dqb,
                               dkb, dvb, tm)
    dgains = dgains2[:, :HEAD_DIM] + dgains2[:, HEAD_DIM:]
    for k, name in enumerate(("q_norm_a", "k_norm_a", "q_norm_b", "k_norm_b")):
        gs[name] = dgains[k]
    d_qkv = _matmul_tn(sv["hm"], dqkv, dqkv.shape[1] // 2, 4 * tm)
    after_mixer = [d_wo.reshape(N_DEV, -1, D), _to_col_shards(d_qkv)]
    token = None if on_ready is None else on_ready(1, after_mixer)
    dx1, gs["norm_mix"] = _dense_norm_bwd(dx2, dqkv, w["w_qkv"], None, sv["x1"], sm["norm_mix"][i][None], 2 * tm)
    g1 = sm["norm_ffn1"][i][None]
    if on_last is None:
        dx0, dyb, dzg, dzu, gs["norm_ffn1"] = _ffn_bwd(dx1, sv["x0"], g1, sv["zg1"], sv["zu1"], ga, gb, 0, tm, token)
        dwin1, dwo1 = _ffn_dw(sv["h1"], dzg, dzu, sv["s1"], dyb, 4 * tm)
        return dx0, (after_ffn2, after_mixer, [dwin1, dwo1.reshape(N_DEV, half, D)]), gs
    dyb, dzg, dzu = _ffn_bwd_dz(dx1, sv["zg1"], sv["zu1"], gb, 0, 2 * tm, token)
    dwin1, dwo1 = _ffn_dw(sv["h1"], dzg, dzu, sv["s1"], dyb, 4 * tm, on_small(gs))
    last = [dwin1, dwo1.reshape(N_DEV, half, D)]
    dx0, gs["norm_ffn1"] = _ffn_bwd_dx(dx1, sv["x0"], g1, dzg, dzu, ga, 0, 2 * tm, on_last(last))
    return dx0, (after_ffn2, after_mixer, last), gs


def _bias_matrices(rel_bias):
    biases = [_bias_variants(_bias_matrix(rel_bias[:, :8], R, d), R) for R, d in DILATED]
    biases.append(_bias_variants(_bias_matrix(rel_bias[:, 8:], SWA_RADIUS, 1), SWA_RADIUS))
    return biases


def _stack_small(per_layer):
    small = {}
    for k, v in per_layer.items():
        if k == "rel_bias":
            per_branch = [sum(parts) for parts in zip(*v.values())]
            drel_a = sum(_bias_grad(t, R, d) for t, (R, d) in zip(per_branch[:3], DILATED))
            small[k] = jnp.concatenate([drel_a, _bias_grad(per_branch[3], SWA_RADIUS, 1)], axis=1)
        else:
            small[k] = jnp.stack([v[i].reshape(-1) for i in sorted(v)])
    return small


TM = 512
SUM_TILES = (512, 480, 256, 128, 512, 512, 352)
LAST_GROUP = ("ffn1_w_in", "ffn1_w_out")


def _pack_small(d, extra=None):
    parts = [d[k].reshape(-1) for k in SMALL]
    if extra is not None:
        parts.append(extra.reshape(-1))
    flat = jnp.concatenate(parts)
    return jnp.pad(flat, (0, SMALL_ROWS * 128 - flat.shape[0])).reshape(SMALL_ROWS, 128)


def _unpack_small(buf, like):
    flat = buf.reshape(-1)
    out, off = {}, 0
    for k in SMALL:
        n = like[k].size
        out[k] = flat[off:off + n].reshape(like[k].shape)
        off += n
    return out, flat[off]


def kernel(x, p, rel_bias, norm_ffn1, ffn1_w_in, ffn1_w_out, norm_mix, w_qkv, q_norm_a, k_norm_a, q_norm_b, k_norm_b, sink_b, w_o, norm_ffn2, ffn2_w_in, ffn2_w_out, norm_ple, w_ple_gate, w_ple_proj, loss_target, m_rel_bias, m_norm_ffn1, m_ffn1_w_in, m_ffn1_w_out, m_norm_mix, m_w_qkv, m_q_norm_a, m_k_norm_a, m_q_norm_b, m_k_norm_b, m_sink_b, m_w_o, m_norm_ffn2, m_ffn2_w_in, m_ffn2_w_out, m_norm_ple, m_w_ple_gate, m_w_ple_proj, v_rel_bias, v_norm_ffn1, v_ffn1_w_in, v_ffn1_w_out, v_norm_mix, v_w_qkv, v_q_norm_a, v_k_norm_a, v_q_norm_b, v_k_norm_b, v_sink_b, v_w_o, v_norm_ffn2, v_ffn2_w_in, v_ffn2_w_out, v_norm_ple, v_w_ple_gate, v_w_ple_proj):
    wts = dict(rel_bias=rel_bias, norm_ffn1=norm_ffn1, ffn1_w_in=ffn1_w_in, ffn1_w_out=ffn1_w_out,
               norm_mix=norm_mix, w_qkv=w_qkv, q_norm_a=q_norm_a, k_norm_a=k_norm_a, q_norm_b=q_norm_b,
               k_norm_b=k_norm_b, sink_b=sink_b, w_o=w_o, norm_ffn2=norm_ffn2, ffn2_w_in=ffn2_w_in,
               ffn2_w_out=ffn2_w_out, norm_ple=norm_ple, w_ple_gate=w_ple_gate, w_ple_proj=w_ple_proj)
    mom = dict(rel_bias=m_rel_bias, norm_ffn1=m_norm_ffn1, ffn1_w_in=m_ffn1_w_in, ffn1_w_out=m_ffn1_w_out,
               norm_mix=m_norm_mix, w_qkv=m_w_qkv, q_norm_a=m_q_norm_a, k_norm_a=m_k_norm_a, q_norm_b=m_q_norm_b,
               k_norm_b=m_k_norm_b, sink_b=m_sink_b, w_o=m_w_o, norm_ffn2=m_norm_ffn2, ffn2_w_in=m_ffn2_w_in,
               ffn2_w_out=m_ffn2_w_out, norm_ple=m_norm_ple, w_ple_gate=m_w_ple_gate, w_ple_proj=m_w_ple_proj)
    var = dict(rel_bias=v_rel_bias, norm_ffn1=v_norm_ffn1, ffn1_w_in=v_ffn1_w_in, ffn1_w_out=v_ffn1_w_out,
               norm_mix=v_norm_mix, w_qkv=v_w_qkv, q_norm_a=v_q_norm_a, k_norm_a=v_k_norm_a, q_norm_b=v_q_norm_b,
               k_norm_b=v_k_norm_b, sink_b=v_sink_b, w_o=v_w_o, norm_ffn2=v_norm_ffn2, ffn2_w_in=v_ffn2_w_in,
               ffn2_w_out=v_ffn2_w_out, norm_ple=v_norm_ple, w_ple_gate=v_w_ple_gate, w_ple_proj=v_w_ple_proj)
    sm = {k: wts[k] for k in SMALL}
    me = 4 * lax.axis_index("x") + 2 * lax.axis_index("y") + lax.axis_index("c")
    packed = []
    for i in range(2):
        a, *rest = _pack_layer(wts, i)
        packed.append([t.astype(BF16) for t in [a.reshape(-1, a.shape[-1])] + rest])
    a_shape = (2, ffn1_w_in.shape[1], ffn1_w_in.shape[2])

    def weights_of(zones):
        return _layer_weights(zones[0].reshape((N_DEV,) + a_shape), *zones[1:])

    w0 = weights_of(_all_gather(packed[0]))
    zone_shapes = [(N_DEV,) + t.shape for t in packed[1]]
    ssem, rsem, thru, zones, token = _exchange_start(packed[1], zone_shapes, False, "gather_start")
    biases = _bias_matrices(rel_bias)
    x1, _, sv0 = _layer_fwd(x[0], p[0, 0], w0, sm, 0, None, TM, biases, dep=token)
    zones = _exchange_wait(ssem, rsem, thru, zones, x1, False, "gather_wait")
    w1 = weights_of([lax.dynamic_update_index_in_dim(z, t, me, 0) for z, t in zip(zones, packed[1])])
    dy, loss, sv1 = _layer_fwd(x1, p[1, 0], w1, sm, 1, loss_target[0], TM, biases)

    def slots_for(arrs):
        return [(N_DEV - 1,) + t.shape[1:] for t in arrs]

    held1, held = {}, {}

    def on_ready1(stage, group):
        held1[stage] = _exchange_start(group, slots_for(group), True, f"scatter1_start_{stage}")
        return held1[stage][4]

    dx1, groups1, gs1 = _layer_bwd(dy, w1, sm, 1, sv1, TM, on_ready=on_ready1)
    on_ready1(2, groups1[2])
    g1 = groups1[0] + groups1[1] + groups1[2]

    def on_ready(stage, group):
        if stage == 1:
            held["slots1"] = [t for st in (0, 1, 2)
                              for t in _exchange_wait(*held1[st][:4], group[0], True, f"scatter1_wait_{st}")]
        held[stage] = _exchange_start(group, slots_for(group), True, f"scatter_start_{stage}")
        return held[stage][4]

    def on_small(gs0):
        part = dict(gs0, norm_ffn1=jnp.zeros_like(gs1["norm_ffn1"]))
        gsmall = _stack_small({k: {0: part[k], 1: gs1[k]} for k in part})
        held["small"] = _all_reduce_small(_pack_small(gsmall, loss[0, :1]))
        return held["small"]

    def on_last(group):
        held["last"] = _exchange_start(group, slots_for(group), True, "scatter_start_2")
        return held["last"][4]

    dx, groups0, gs0 = _layer_bwd(dx1, w0, sm, 0, sv0, TM, dep=held1[2][4], on_ready=on_ready, on_small=on_small,
                                  on_last=on_last)
    last = groups0[2]
    slots0 = [_exchange_wait(*held[stage][:4], last[0], True, f"scatter_wait_{stage}") for stage in (0, 1)]

    def summed(arrs, slots, tiles, dep=None):
        return [_sum_parts(lax.dynamic_index_in_dim(t, me, 0, keepdims=False), s_, tr, dep)
                for t, s_, tr in zip(arrs, slots, tiles)]

    cover = held["last"][4]
    r1 = summed(g1, held["slots1"], SUM_TILES, cover)
    r0 = summed(groups0[0], slots0[0], SUM_TILES[:3], cover) + summed(groups0[1], slots0[1], SUM_TILES[3:5], cover)

    def update(names, layers):
        for k in names:
            grads[k] = jnp.stack([layers[0][k], layers[1][k]])
            delta[k], new_m[k], new_v[k] = _adamw(wts[k], grads[k], mom[k], var[k])

    grads, delta, new_m, new_v = {}, {}, {}, {}
    layer1 = _unpack_layer(r1, wts)
    update([k for k in BIG if k not in LAST_GROUP], [_unpack_layer(r0 + [None, None], wts), layer1])

    cover_done = [dx] + [delta[k] for k in BIG if k not in LAST_GROUP]
    slots_last = _exchange_wait(*held["last"][:4], cover_done, True, "scatter_wait_2")
    update(LAST_GROUP, [_unpack_layer([None] * 5 + summed(last, slots_last, SUM_TILES[5:]), wts), layer1])
    late = _all_reduce_small(gs0["norm_ffn1"].reshape(-1, 128), dep=slots_last[0])
    small_sum, loss_sum = _unpack_small(held["small"], sm)
    small_sum["norm_ffn1"] = small_sum["norm_ffn1"].at[0].add(late.reshape(-1))
    grads.update(small_sum)
    zeros = {k: jnp.zeros_like(wts[k]) for k in SMALL}
    ds, ms, vs = _adamw(_pack_small(wts), _pack_small(small_sum), _pack_small(mom), _pack_small(var))
    for packed, dst in ((ds, delta), (ms, new_m), (vs, new_v)):
        dst.update(_unpack_small(packed, zeros)[0])

    return (loss_sum, dx[None], *[grads[k] for k in WEIGHTS], *[delta[k] for k in WEIGHTS],
            *[new_m[k] for k in WEIGHTS], *[new_v[k] for k in WEIGHTS])
```

```python
import functools
import math

import jax
import jax.numpy as jnp
from jax import lax
from jax.experimental import pallas as pl
from jax.experimental.pallas import tpu as pltpu

F32 = jnp.float32
BF16 = jnp.bfloat16

N_DEV = 8
HEAD_DIM = 64
PAIR = 2 * HEAD_DIM
BQ = 128
N_BUCKETS = 32
MAX_DISTANCE = 1024
DILATED = ((64, 1), (64, 4), (64, 16))
SWA_RADIUS = 128
EPS = 1e-6
NEG = -1e30
ADAM_LR, ADAM_B1, ADAM_B2, ADAM_EPS, ADAM_WD, ADAM_STEP = 0.001, 0.9, 0.999, 1e-08, 0.01, 10
VMEM_LIMIT = 56 * 1024 * 1024
MESH = pl.DeviceIdType.MESH

BIG = ("ffn1_w_in", "ffn1_w_out", "w_qkv", "w_o", "ffn2_w_in", "ffn2_w_out", "w_ple_gate", "w_ple_proj")
SMALL = ("rel_bias", "norm_ffn1", "norm_mix", "q_norm_a", "k_norm_a", "q_norm_b", "k_norm_b", "sink_b",
         "norm_ffn2", "norm_ple")
WEIGHTS = ("rel_bias", "norm_ffn1", "ffn1_w_in", "ffn1_w_out", "norm_mix", "w_qkv", "q_norm_a", "k_norm_a",
           "q_norm_b", "k_norm_b", "sink_b", "w_o", "norm_ffn2", "ffn2_w_in", "ffn2_w_out", "norm_ple",
           "w_ple_gate", "w_ple_proj")
SMALL_ROWS = 96


def _params(*sem):
    return pltpu.CompilerParams(dimension_semantics=sem, vmem_limit_bytes=VMEM_LIMIT)


def _dot(a, b):
    return jnp.dot(a, b, preferred_element_type=F32)


def _dot_nt(a, b):
    return lax.dot_general(a, b, (((1,), (1,)), ((), ())), preferred_element_type=F32)


def _dot_tn(a, b):
    return lax.dot_general(a, b, (((0,), (0,)), ((), ())), preferred_element_type=F32)


def _sigmoid(x):
    return 1.0 / (1.0 + jnp.exp(-x))


def _rstd(xv):
    return lax.rsqrt(jnp.mean(xv * xv, axis=-1, keepdims=True) + EPS)


def _norm_bwd(dh, xv, gv):
    r = _rstd(xv)
    xn = xv * r
    dg = jnp.sum(dh * xn, axis=0, keepdims=True)
    dxn = dh * gv
    dx = r * (dxn - xn * jnp.mean(dxn * xn, axis=-1, keepdims=True))
    return dx, dg


def _lo_mask(shape):
    return lax.broadcasted_iota(jnp.int32, shape, len(shape) - 1) < HEAD_DIM


def _half_sum(t, lo):
    s0 = jnp.sum(jnp.where(lo, t, 0.0), axis=1, keepdims=True)
    s1 = jnp.sum(jnp.where(lo, 0.0, t), axis=1, keepdims=True)
    return jnp.where(lo, s0, s1)


FFN_PARTS = 2


def _ffn_weight_specs(f, nj, D, C):
    return [pl.BlockSpec((None, None, D, C), lambda i, j: (j, f, 0, 0)),
            pl.BlockSpec((None, None, D, C), lambda i, j: (j + nj, f, 0, 0)),
            pl.BlockSpec((2, C // 2, D), lambda i, j: (j, f, 0))]


def _with_dep(body, dep, in_specs, args):
    if dep is None:
        return body, in_specs, args

    def body_after(dep_ref, *refs):
        body(*refs)

    return body_after, [pl.BlockSpec(memory_space=pl.ANY)] + in_specs, [dep] + args


def _ffn_fwd(x, g, ga, gb, f, tm, dep=None):
    T, D = x.shape
    nj, C = ga.shape[0] // 2, ga.shape[3]

    def body(x_ref, g_ref, wg_ref, wu_ref, wo_ref, xo_ref, h_ref, zg_ref, zu_ref, s_ref, h_scr, acc):
        j = pl.program_id(1)

        @pl.when(j == 0)
        def _():
            xv = x_ref[...]
            hb = (xv * _rstd(xv) * g_ref[...]).astype(BF16)
            h_scr[...] = hb
            h_ref[...] = hb
            acc[...] = jnp.zeros_like(acc)

        wo = wo_ref[...].reshape(C, D)
        for part in range(FFN_PARTS):
            sl = pl.ds(part * (tm // FFN_PARTS), tm // FFN_PARTS)
            hb = h_scr[sl, :]
            gt = _dot(hb, wg_ref[...])
            up = _dot(hb, wu_ref[...])
            s = (gt * _sigmoid(gt) * up).astype(BF16)
            zg_ref[sl, :] = gt.astype(BF16)
            zu_ref[sl, :] = up.astype(BF16)
            s_ref[sl, :] = s
            acc[sl, :] += _dot(s, wo)

        @pl.when(j == nj - 1)
        def _():
            xo_ref[...] = x_ref[...] + 0.5 * acc[...]

    tok = pl.BlockSpec((tm, D), lambda i, j: (i, 0))
    chunk = pl.BlockSpec((None, tm, C), lambda i, j: (j, i, 0))
    in_specs = [tok, pl.BlockSpec((1, D), lambda i, j: (0, 0))] + _ffn_weight_specs(f, nj, D, C)
    body, in_specs, args = _with_dep(body, dep, in_specs, [x, g, ga, ga, gb])
    return pl.pallas_call(
        body, name="ffn_fwd", grid=(T // tm, nj),
        in_specs=in_specs,
        out_specs=[tok, tok, chunk, chunk, chunk],
        out_shape=[jax.ShapeDtypeStruct((T, D), F32), jax.ShapeDtypeStruct((T, D), BF16),
                   jax.ShapeDtypeStruct((nj, T, C), BF16), jax.ShapeDtypeStruct((nj, T, C), BF16),
                   jax.ShapeDtypeStruct((nj, T, C), BF16)],
        scratch_shapes=[pltpu.VMEM((tm, D), BF16), pltpu.VMEM((tm, D), F32)],
        compiler_params=_params("parallel", "arbitrary"),
    )(*args)


def _ffn_bwd(dxo, x, g, zg, zu, ga, gb, f, tm, dep=None):
    T, D = x.shape
    nj, C = ga.shape[0] // 2, ga.shape[3]

    def body(dxo_ref, x_ref, g_ref, zg_ref, zu_ref, wg_ref, wu_ref, wo_ref,
             dx_ref, dy_ref, dzg_ref, dzu_ref, dgn_ref, dy_scr, acc):
        i, j = pl.program_id(0), pl.program_id(1)

        @pl.when(j == 0)
        def _():
            dyb = (0.5 * dxo_ref[...]).astype(BF16)
            dy_scr[...] = dyb
            dy_ref[...] = dyb
            acc[...] = jnp.zeros_like(acc)

        wo = wo_ref[...].reshape(C, D)
        for part in range(FFN_PARTS):
            sl = pl.ds(part * (tm // FFN_PARTS), tm // FFN_PARTS)
            ds = _dot_nt(dy_scr[sl, :], wo)
            gt = zg_ref[sl, :].astype(F32)
            up = zu_ref[sl, :].astype(F32)
            sg = _sigmoid(gt)
            dgt = (ds * up * (sg * (1.0 + gt * (1.0 - sg)))).astype(BF16)
            dup = (ds * (gt * sg)).astype(BF16)
            dzg_ref[sl, :] = dgt
            dzu_ref[sl, :] = dup
            acc[sl, :] += _dot_nt(dgt, wg_ref[...]) + _dot_nt(dup, wu_ref[...])

        @pl.when(j == nj - 1)
        def _():
            dx, dg = _norm_bwd(acc[...], x_ref[...], g_ref[...])
            dx_ref[...] = dxo_ref[...] + dx

            @pl.when(i == 0)
            def _():
                dgn_ref[...] = dg

            @pl.when(i > 0)
            def _():
                dgn_ref[...] += dg

    tok = pl.BlockSpec((tm, D), lambda i, j: (i, 0))
    chunk = pl.BlockSpec((None, tm, C), lambda i, j: (j, i, 0))
    row = pl.BlockSpec((1, D), lambda i, j: (0, 0))
    in_specs = [tok, tok, row, chunk, chunk] + _ffn_weight_specs(f, nj, D, C)
    body, in_specs, args = _with_dep(body, dep, in_specs, [dxo, x, g, zg, zu, ga, ga, gb])
    return pl.pallas_call(
        body, name="ffn_bwd", grid=(T // tm, nj),
        in_specs=in_specs,
        out_specs=[tok, tok, chunk, chunk, row],
        out_shape=[jax.ShapeDtypeStruct((T, D), F32), jax.ShapeDtypeStruct((T, D), BF16),
                   jax.ShapeDtypeStruct((nj, T, C), BF16), jax.ShapeDtypeStruct((nj, T, C), BF16),
                   jax.ShapeDtypeStruct((1, D), F32)],
        scratch_shapes=[pltpu.VMEM((tm, D), BF16), pltpu.VMEM((tm, D), F32)],
        compiler_params=_params("arbitrary", "arbitrary"),
    )(*args)


def _ffn_bwd_dz(dxo, zg, zu, gb, f, tm, dep=None):
    T, D = dxo.shape
    nj, C = zg.shape[0], zg.shape[2]

    def body(dxo_ref, zg_ref, zu_ref, wo_ref, dy_ref, dzg_ref, dzu_ref, dy_scr):
        @pl.when(pl.program_id(1) == 0)
        def _():
            dyb = (0.5 * dxo_ref[...]).astype(BF16)
            dy_scr[...] = dyb
            dy_ref[...] = dyb

        wo = wo_ref[...].reshape(C, D)
        for part in range(FFN_PARTS):
            sl = pl.ds(part * (tm // FFN_PARTS), tm // FFN_PARTS)
            ds = _dot_nt(dy_scr[sl, :], wo)
            gt = zg_ref[sl, :].astype(F32)
            up = zu_ref[sl, :].astype(F32)
            sg = _sigmoid(gt)
            dzg_ref[sl, :] = (ds * up * (sg * (1.0 + gt * (1.0 - sg)))).astype(BF16)
            dzu_ref[sl, :] = (ds * (gt * sg)).astype(BF16)

    tok = pl.BlockSpec((tm, D), lambda i, j: (i, 0))
    chunk = pl.BlockSpec((None, tm, C), lambda i, j: (j, i, 0))
    in_specs = [tok, chunk, chunk, _ffn_weight_specs(f, nj, D, C)[2]]
    body, in_specs, args = _with_dep(body, dep, in_specs, [dxo, zg, zu, gb])
    return pl.pallas_call(
        body, name="ffn_bwd_dz", grid=(T // tm, nj),
        in_specs=in_specs, out_specs=[tok, chunk, chunk],
        out_shape=[jax.ShapeDtypeStruct((T, D), BF16), jax.ShapeDtypeStruct((nj, T, C), BF16),
                   jax.ShapeDtypeStruct((nj, T, C), BF16)],
        scratch_shapes=[pltpu.VMEM((tm, D), BF16)],
        compiler_params=_params("parallel", "arbitrary"),
    )(*args)


def _ffn_bwd_dx(dxo, x, g, dzg, dzu, ga, f, tm, dep=None):
    T, D = x.shape
    nj, C = ga.shape[0] // 2, ga.shape[3]

    def body(dxo_ref, x_ref, g_ref, dzg_ref, dzu_ref, wg_ref, wu_ref, dx_ref, dgn_ref, acc):
        i, j = pl.program_id(0), pl.program_id(1)

        @pl.when(j == 0)
        def _():
            acc[...] = jnp.zeros_like(acc)

        acc[...] += _dot_nt(dzg_ref[...], wg_ref[...]) + _dot_nt(dzu_ref[...], wu_ref[...])

        @pl.when(j == nj - 1)
        def _():
            dx, dg = _norm_bwd(acc[...], x_ref[...], g_ref[...])
            dx_ref[...] = dxo_ref[...] + dx

            @pl.when(i == 0)
            def _():
                dgn_ref[...] = dg

            @pl.when(i > 0)
            def _():
                dgn_ref[...] += dg

    tok = pl.BlockSpec((tm, D), lambda i, j: (i, 0))
    chunk = pl.BlockSpec((None, tm, C), lambda i, j: (j, i, 0))
    row = pl.BlockSpec((1, D), lambda i, j: (0, 0))
    in_specs = [tok, tok, row, chunk, chunk] + _ffn_weight_specs(f, nj, D, C)[:2]
    body, in_specs, args = _with_dep(body, dep, in_specs, [dxo, x, g, dzg, dzu, ga, ga])
    return pl.pallas_call(
        body, name="ffn_bwd_dx", grid=(T // tm, nj),
        in_specs=in_specs, out_specs=[tok, row],
        out_shape=[jax.ShapeDtypeStruct((T, D), F32), jax.ShapeDtypeStruct((1, D), F32)],
        scratch_shapes=[pltpu.VMEM((tm, D), F32)],
        compiler_params=_params("arbitrary", "arbitrary"),
    )(*args)


def _ffn_dw(h, dzg, dzu, s, dy, tk, dep=None):
    T, D = h.shape
    nj, C = s.shape[0], s.shape[2]
    nk = T // tk

    def body(h_ref, dzg_ref, dzu_ref, s_ref, dy_ref, dwin_ref, dwo_ref, ag, au, ao):
        k = pl.program_id(1)

        @pl.when(k == 0)
        def _():
            ag[...] = jnp.zeros_like(ag)
            au[...] = jnp.zeros_like(au)
            ao[...] = jnp.zeros_like(ao)

        hb = h_ref[...]
        ag[...] += _dot_tn(hb, dzg_ref[...])
        au[...] += _dot_tn(hb, dzu_ref[...])
        ao[...] += _dot_tn(s_ref[...], dy_ref[...])

        @pl.when(k == nk - 1)
        def _():
            dwin_ref[0] = ag[...].astype(BF16)
            dwin_ref[1] = au[...].astype(BF16)
            dwo_ref[...] = ao[...].astype(BF16)

    tok = pl.BlockSpec((tk, D), lambda j, k: (k, 0))
    chunk = pl.BlockSpec((None, tk, C), lambda j, k: (j, k, 0))
    body, in_specs, args = _with_dep(body, dep, [tok, chunk, chunk, chunk, tok], [h, dzg, dzu, s, dy])
    dwin, dwo = pl.pallas_call(
        body, name="ffn_dw", grid=(nj, nk),
        in_specs=in_specs,
        out_specs=[pl.BlockSpec((2, None, D, C), lambda j, k: (0, j, 0, 0)),
                   pl.BlockSpec((None, C, D), lambda j, k: (j, 0, 0))],
        out_shape=[jax.ShapeDtypeStruct((2, nj, D, C), BF16), jax.ShapeDtypeStruct((nj, C, D), BF16)],
        scratch_shapes=[pltpu.VMEM((D, C), F32), pltpu.VMEM((D, C), F32), pltpu.VMEM((C, D), F32)],
        compiler_params=_params("parallel", "arbitrary"),
    )(*args)
    return dwin.reshape(2 * nj, D, C), dwo


def _matmul_tn(a, b, tn, tk):
    T, Ka = a.shape
    N = b.shape[1]
    nk = T // tk

    def body(a_ref, b_ref, o_ref, acc):
        k = pl.program_id(1)

        @pl.when(k == 0)
        def _():
            acc[...] = jnp.zeros_like(acc)

        acc[...] += _dot_tn(a_ref[...], b_ref[...])

        @pl.when(k == nk - 1)
        def _():
            o_ref[...] = acc[...].astype(BF16)

    return pl.pallas_call(
        body, name="matmul_tn", grid=(N // tn, nk),
        in_specs=[pl.BlockSpec((tk, Ka), lambda n, k: (k, 0)), pl.BlockSpec((tk, tn), lambda n, k: (k, n))],
        out_specs=pl.BlockSpec((Ka, tn), lambda n, k: (0, n)),
        out_shape=jax.ShapeDtypeStruct((Ka, N), BF16),
        scratch_shapes=[pltpu.VMEM((Ka, tn), F32)],
        compiler_params=_params("parallel", "arbitrary"),
    )(a, b)


def _qkv_fwd(x, g, w, tm):
    T, D = x.shape
    N = w.shape[1]

    def body(x_ref, g_ref, w_ref, o_ref, h_ref):
        xv = x_ref[...]
        hb = (xv * _rstd(xv) * g_ref[...]).astype(BF16)
        h_ref[...] = hb
        o_ref[...] = _dot(hb, w_ref[...])

    return pl.pallas_call(
        body, name="qkv_fwd", grid=(T // tm,),
        in_specs=[pl.BlockSpec((tm, D), lambda i: (i, 0)), pl.BlockSpec((1, D), lambda i: (0, 0)),
                  pl.BlockSpec((D, N), lambda i: (0, 0))],
        out_specs=[pl.BlockSpec((tm, N), lambda i: (i, 0)), pl.BlockSpec((tm, D), lambda i: (i, 0))],
        out_shape=[jax.ShapeDtypeStruct((T, N), F32), jax.ShapeDtypeStruct((T, D), BF16)],
        compiler_params=_params("parallel"),
    )(x, g, w)


DILS = tuple(d for _, d in DILATED)


def _spread_specs(tm, T, dtype):
    specs = [pl.BlockSpec((4, d, tm // d, PAIR), lambda i: (0, 0, i, 0)) for d in DILS]
    shapes = [jax.ShapeDtypeStruct((4, d, T // d, PAIR), dtype) for d in DILS]
    return specs, shapes


def _spread(tile, y, outs, c, dtype):
    tm = y.shape[0]
    tile[...] = y
    for out, d in zip(outs, DILS):
        for r in range(d):
            out[c, r] = tile[pl.ds(r, tm // d, stride=d), :].astype(dtype)


def _collect(tile, ins, c):
    tm = tile.shape[0]
    first = True
    for ref, d in zip(ins, DILS):
        for r in range(d):
            rows = pl.ds(r, tm // d, stride=d) if d > 1 else pl.ds(0, tm)
            part = ref[c, r].astype(F32)
            tile[rows, :] = part if first else tile[rows, :] + part
        first = False
    return tile[...]


def _attn_prep(qkv, gains2, tm):
    T = qkv.shape[0]
    scale = HEAD_DIM ** -0.5
    n = len(DILS)

    def body(qkv_ref, g_ref, qb_ref, kb_ref, vb_ref, *rest):
        outs, tile = rest[:-1], rest[-1]
        lo = _lo_mask((tm, PAIR))

        def spread(kind, c, y):
            _spread(tile, y, outs[kind * n:(kind + 1) * n], c, BF16)

        def normed(c, gi, mult):
            xv = qkv_ref[:, c * PAIR:(c + 1) * PAIR]
            r = lax.rsqrt(_half_sum(xv * xv, lo) * (1.0 / HEAD_DIM) + EPS)
            y = xv * r * g_ref[gi:gi + 1, :]
            return y * mult if mult != 1.0 else y

        def both_halves(v):
            sw = pltpu.roll(v, HEAD_DIM, 1)
            return jnp.where(lo, v, sw), jnp.where(lo, sw, v)

        for c in range(4):
            spread(0, c, normed(c, 0, scale))
            spread(1, c, normed(4 + c, 1, 1.0))
            spread(2, c, qkv_ref[:, (8 + c) * PAIR:(9 + c) * PAIR])
            qb_ref[c] = normed(12 + c, 2, scale).astype(BF16)
        k0, k1 = both_halves(normed(16, 3, 1.0))
        kb_ref[0] = k0.astype(BF16)
        kb_ref[1] = k1.astype(BF16)
        v0, v1 = both_halves(qkv_ref[:, 17 * PAIR:18 * PAIR])
        vb_ref[0] = v0.astype(BF16)
        vb_ref[1] = v1.astype(BF16)

    four = pl.BlockSpec((4, tm, PAIR), lambda i: (0, i, 0))
    two = pl.BlockSpec((2, tm, PAIR), lambda i: (0, i, 0))
    s4 = jax.ShapeDtypeStruct((4, T, PAIR), BF16)
    s2 = jax.ShapeDtypeStruct((2, T, PAIR), BF16)
    specs, shapes = _spread_specs(tm, T, BF16)
    res = pl.pallas_call(
        body, name="attn_prep", grid=(T // tm,),
        in_specs=[pl.BlockSpec((tm, qkv.shape[1]), lambda i: (i, 0)), pl.BlockSpec((4, PAIR), lambda i: (0, 0))],
        out_specs=[four, two, two] + specs * 3,
        out_shape=[s4, s2, s2] + shapes * 3,
        scratch_shapes=[pltpu.VMEM((tm, PAIR), F32)],
        compiler_params=_params("parallel"),
    )(qkv, gains2)
    qb, kb, vb = res[:3]
    per_d = [tuple(res[3 + kind * n + di].reshape(4 * d, T // d, PAIR) for kind in range(3))
             for di, d in enumerate(DILS)]
    return qb, kb, vb, per_d


def _loop_blocks(nb, body, init, per_iter):
    u = math.gcd(nb, per_iter)

    def outer(i, carry):
        for k in range(u):
            carry = body(i * u + k, carry)
        return carry

    return lax.fori_loop(0, nb // u, outer, init)


def _key_window(b, nb, L, R, W):
    start = pl.multiple_of(jnp.clip(b * BQ - R, 0, L - W), HEAD_DIM)
    return start, jnp.where(b == 0, 1, jnp.where(b == nb - 1, 2, 0))


def _stack_heads(v, lo):
    z = jnp.zeros_like(v)
    return jnp.concatenate([jnp.where(lo, v, z), jnp.where(lo, z, v)], axis=0)


def _unstack_heads(v2, lo):
    return jnp.where(lo, v2[:BQ], v2[BQ:])


def _row_vector(v, lo):
    r = lax.broadcasted_iota(jnp.int32, (BQ, PAIR), 0)
    ln = lax.broadcasted_iota(jnp.int32, (BQ, PAIR), 1)
    diag = (ln % HEAD_DIM) == (r % HEAD_DIM)
    top = jnp.sum(jnp.where(diag & (r < HEAD_DIM), v, 0.0), axis=0, keepdims=True)
    bot = jnp.sum(jnp.where(diag & (r >= HEAD_DIM), v, 0.0), axis=0, keepdims=True)
    top8, bot8 = jnp.broadcast_to(top, (8, PAIR)), jnp.broadcast_to(bot, (8, PAIR))
    lo8 = _lo_mask((8, PAIR))
    head0 = jnp.where(lo8, top8, pltpu.roll(bot8, HEAD_DIM, 1))
    head1 = jnp.where(lo8, pltpu.roll(top8, HEAD_DIM, 1), bot8)
    return jnp.concatenate([head0, head1], axis=1)[:1]


def _units_per_step(nb, pairs_per_kv):
    return max(1, 16 // nb) if pairs_per_kv == 1 else 1


def _attn_fwd(q, kp, vp, bias4, sink, R, pairs_per_kv, pairs_per_bias):
    N, L, _ = q.shape
    W = BQ + 2 * R
    nb = L // BQ
    assert L >= W and nb >= 2
    G = _units_per_step(nb, pairs_per_kv)

    def body(sink_ref, q_ref, k_ref, v_ref, bias_ref, o_ref, lse_ref):
        n = pl.program_id(0)
        lo_q = _lo_mask((BQ, PAIR))
        first = lax.broadcasted_iota(jnp.int32, (2 * BQ, 1), 0) < BQ

        def blk(f, carry):
            g, b = f // nb, f % nb
            u = n * G + g
            sk = jnp.where(first, sink_ref[2 * u], sink_ref[2 * u + 1])
            q0 = pl.multiple_of(b * BQ, BQ)
            q2 = _stack_heads(q_ref[g, pl.ds(q0, BQ), :], lo_q)
            k0, variant = _key_window(b, nb, L, R, W)
            kw = k_ref[g, pl.ds(k0, W), :]
            vw = v_ref[g, pl.ds(k0, W), :]
            s = _dot_nt(q2, kw) + bias_ref[variant]
            m = jnp.maximum(jnp.max(s, axis=1, keepdims=True), sk)
            p = jnp.exp(s - m)
            l = jnp.sum(p, axis=1, keepdims=True) + jnp.exp(sk - m)
            o2 = _dot(p.astype(BF16), vw) / l
            o_ref[g, pl.ds(q0, BQ), :] = _unstack_heads(o2, lo_q)
            lse_ref[g, pl.ds(q0, BQ), :] = _unstack_heads(jnp.broadcast_to(m + jnp.log(l), (2 * BQ, PAIR)), lo_q)
            return carry

        _loop_blocks(G * nb, blk, 0, 16)

    qspec = pl.BlockSpec((G, L, PAIR), lambda n: (n, 0, 0))
    kspec = pl.BlockSpec((G, L, PAIR), lambda n: (n // pairs_per_kv, 0, 0))
    return pl.pallas_call(
        body, name="attn_fwd", grid=(N // G,),
        in_specs=[pl.BlockSpec(memory_space=pltpu.SMEM), qspec, kspec, kspec,
                  pl.BlockSpec((None, 3, 2 * BQ, W), lambda n: (n * G // pairs_per_bias, 0, 0, 0))],
        out_specs=[qspec, qspec],
        out_shape=[jax.ShapeDtypeStruct((N, L, PAIR), F32), jax.ShapeDtypeStruct((N, L, PAIR), F32)],
        compiler_params=_params("parallel"),
    )(sink, q, kp, vp, bias4)


def _attn_bwd(q, kp, vp, bias4t, sink, o, lse, do, R, pairs_per_kv, pairs_per_bias):
    N, L, _ = q.shape
    Nk = kp.shape[0]
    Pb = bias4t.shape[0]
    W = BQ + 2 * R
    nb = L // BQ
    assert L >= W and nb >= 2
    G = _units_per_step(nb, pairs_per_kv)

    def body(sink_ref, q_ref, k_ref, v_ref, bias_ref, o_ref, lse_ref, do_ref,
             dq_ref, dk_ref, dv_ref, dbias_ref, dsink_ref, dk_acc, dv_acc):
        n = pl.program_id(0)
        lo_q = _lo_mask((BQ, PAIR))
        first = lax.broadcasted_iota(jnp.int32, (1, 2 * BQ), 1) < BQ
        dsink_ref[...] = jnp.zeros_like(dsink_ref)

        @pl.when(n % pairs_per_kv == 0)
        def _():
            dk_acc[...] = jnp.zeros_like(dk_acc)
            dv_acc[...] = jnp.zeros_like(dv_acc)

        @pl.when((n * G) % pairs_per_bias == 0)
        def _():
            dbias_ref[...] = jnp.zeros_like(dbias_ref)

        def blk(f, carry):
            g, b = f // nb, f % nb
            u = n * G + g
            sk = jnp.where(first, sink_ref[2 * u], sink_ref[2 * u + 1])
            q0 = pl.multiple_of(b * BQ, BQ)
            q2 = _stack_heads(q_ref[g, pl.ds(q0, BQ), :], lo_q)
            k0, variant = _key_window(b, nb, L, R, W)
            kw = k_ref[g, pl.ds(k0, W), :]
            vw = v_ref[g, pl.ds(k0, W), :]
            dov = do_ref[g, pl.ds(q0, BQ), :]
            lse = _row_vector(lse_ref[g, pl.ds(q0, BQ), :], lo_q)
            delta = _row_vector(_half_sum(dov.astype(F32) * o_ref[g, pl.ds(q0, BQ), :], lo_q), lo_q)
            do2 = _stack_heads(dov.astype(BF16), lo_q)
            st = _dot_nt(kw, q2) + bias_ref[variant]
            pt = jnp.exp(st - lse)
            dst = pt * (_dot_nt(vw, do2) - delta)
            dstb = dst.astype(BF16)
            dbias_ref[variant] += dst
            dk_acc[g, pl.ds(k0, W), :] += _dot(dstb, q2)
            dv_acc[g, pl.ds(k0, W), :] += _dot(pt.astype(BF16), do2)
            dq_ref[g, pl.ds(q0, BQ), :] = _unstack_heads(_dot_tn(dstb, kw), lo_q).astype(BF16)
            dsink_ref[g, pl.ds(0, 1), :] -= jnp.exp(sk - lse) * delta
            return carry

        _loop_blocks(G * nb, blk, 0, 16)
        dk_ref[...] = dk_acc[...].astype(BF16)
        dv_ref[...] = dv_acc[...].astype(BF16)

    qspec = pl.BlockSpec((G, L, PAIR), lambda n: (n, 0, 0))
    kspec = pl.BlockSpec((G, L, PAIR), lambda n: (n // pairs_per_kv, 0, 0))
    return pl.pallas_call(
        body, name="attn_bwd", grid=(N // G,),
        in_specs=[pl.BlockSpec(memory_space=pltpu.SMEM), qspec, kspec, kspec,
                  pl.BlockSpec((None, 3, W, 2 * BQ), lambda n: (n * G // pairs_per_bias, 0, 0, 0)),
                  qspec, qspec, qspec],
        out_specs=[qspec, kspec, kspec,
                   pl.BlockSpec((None, 3, W, 2 * BQ), lambda n: (n * G // pairs_per_bias, 0, 0, 0)),
                   pl.BlockSpec((G, 8, 2 * BQ), lambda n: (n, 0, 0))],
        out_shape=[jax.ShapeDtypeStruct((N, L, PAIR), BF16),
                   jax.ShapeDtypeStruct((Nk, L, PAIR), BF16),
                   jax.ShapeDtypeStruct((Nk, L, PAIR), BF16),
                   jax.ShapeDtypeStruct((Pb, 3, W, 2 * BQ), F32),
                   jax.ShapeDtypeStruct((N, 8, 2 * BQ), F32)],
        scratch_shapes=[pltpu.VMEM((G, L, PAIR), F32), pltpu.VMEM((G, L, PAIR), F32)],
        compiler_params=_params("arbitrary"),
    )(sink, q, kp, vp, bias4t, o, lse, do)


def _attn_merge(branch_outs, ob, tm):
    T = ob.shape[1]
    n = len(DILS)

    def body(*refs):
        o_in, l_in, ob_ref = refs[:n], refs[n:2 * n], refs[2 * n]
        o_out, l_out, cat_ref = refs[2 * n + 1:3 * n + 1], refs[3 * n + 1:4 * n + 1], refs[4 * n + 1]
        tiles = refs[4 * n + 2:]
        for c in range(4):
            o_nat, l_nat = [], []
            for di, d in enumerate(DILS):
                for kind, (src, dst) in enumerate(((o_in[di], o_nat), (l_in[di], l_nat))):
                    tile = tiles[2 * di + kind]
                    if d == 1:
                        dst.append(src[c, 0])
                    else:
                        for r in range(d):
                            tile[pl.ds(r, tm // d, stride=d), :] = src[c, r]
                        dst.append(tile[...])
            m = functools.reduce(jnp.maximum, l_nat)
            ws = [jnp.exp(l - m) for l in l_nat]
            z = sum(ws)
            o = sum(w * t for w, t in zip(ws, o_nat)) / z
            cat_ref[:, c * PAIR:(c + 1) * PAIR] = o.astype(BF16)
            cat_ref[:, (4 + c) * PAIR:(5 + c) * PAIR] = ob_ref[c].astype(BF16)
            _spread(tiles[0], o, o_out, c, F32)
            _spread(tiles[1], m + jnp.log(z), l_out, c, F32)

    specs, shapes = _spread_specs(tm, T, F32)
    four = pl.BlockSpec((4, tm, PAIR), lambda i: (0, i, 0))
    o_views = [o.reshape(4, d, T // d, PAIR) for (o, _), d in zip(branch_outs, DILS)]
    l_views = [l.reshape(4, d, T // d, PAIR) for (_, l), d in zip(branch_outs, DILS)]
    res = pl.pallas_call(
        body, name="attn_merge", grid=(T // tm,),
        in_specs=specs + specs + [four],
        out_specs=specs + specs + [pl.BlockSpec((tm, 8 * PAIR), lambda i: (i, 0))],
        out_shape=shapes + shapes + [jax.ShapeDtypeStruct((T, 8 * PAIR), BF16)],
        scratch_shapes=[pltpu.VMEM((tm, PAIR), F32)] * (2 * n),
        compiler_params=_params("parallel"),
    )(*o_views, *l_views, ob)
    merged = [(res[di].reshape(4 * d, T // d, PAIR), res[n + di].reshape(4 * d, T // d, PAIR))
              for di, d in enumerate(DILS)]
    return merged, res[2 * n]


def _weight_arg(w, blk):
    if blk is None:
        return pl.BlockSpec(w.shape, lambda i: (0, 0)), (lambda ref: ref[...])
    D = w.shape[2]
    return (pl.BlockSpec((N_DEV, 128, D), lambda i: (0, blk, 0)),
            lambda ref: ref[...].reshape(N_DEV * 128, D))


def _oproj_fwd(x, o_cat, w, blk, tm):
    T, D = x.shape
    wspec, wload = _weight_arg(w, blk)

    def body(x_ref, o_ref, w_ref, out_ref):
        out_ref[...] = x_ref[...] + _dot(o_ref[...], wload(w_ref))

    tok = pl.BlockSpec((tm, D), lambda i: (i, 0))
    return pl.pallas_call(
        body, name="oproj_fwd", grid=(T // tm,),
        in_specs=[tok, pl.BlockSpec((tm, o_cat.shape[1]), lambda i: (i, 0)), wspec],
        out_specs=tok, out_shape=jax.ShapeDtypeStruct((T, D), F32),
        compiler_params=_params("parallel"),
    )(x, o_cat, w)


def _oproj_bwd(dx, w, blk, tm, dep=None):
    T, D = dx.shape
    wspec, wload = _weight_arg(w, blk)

    def body(dx_ref, w_ref, dxb_ref, dob_ref, *rest):
        doa_refs, tile = rest[:-1], rest[-1]
        db = dx_ref[...].astype(BF16)
        dxb_ref[...] = db
        do = _dot_nt(db, wload(w_ref))
        for c in range(4):
            _spread(tile, do[:, c * PAIR:(c + 1) * PAIR], doa_refs, c, BF16)
            dob_ref[c] = do[:, (4 + c) * PAIR:(5 + c) * PAIR].astype(BF16)

    tok = pl.BlockSpec((tm, D), lambda i: (i, 0))
    specs, shapes = _spread_specs(tm, T, BF16)
    body, in_specs, args = _with_dep(body, dep, [tok, wspec], [dx, w])
    res = pl.pallas_call(
        body, name="oproj_bwd", grid=(T // tm,),
        in_specs=in_specs,
        out_specs=[tok, pl.BlockSpec((4, tm, PAIR), lambda i: (0, i, 0))] + specs,
        out_shape=[jax.ShapeDtypeStruct((T, D), BF16), jax.ShapeDtypeStruct((4, T, PAIR), BF16)] + shapes,
        scratch_shapes=[pltpu.VMEM((tm, PAIR), F32)],
        compiler_params=_params("parallel"),
    )(*args)
    return res[0], res[1], [t.reshape(4 * d, T // d, PAIR) for t, d in zip(res[2:], DILS)]


def _attn_post(qkv, gains2, dqa, dka, dva, dqb, dkb, dvb, tm):
    T, NQ = qkv.shape
    scale = HEAD_DIM ** -0.5

    n = len(DILS)

    def body(qkv_ref, g_ref, *rest):
        dq_refs, dk_refs, dv_refs = rest[:n], rest[n:2 * n], rest[2 * n:3 * n]
        qb_ref, kb_ref, vb_ref, out_ref, dg_ref, tile = rest[3 * n:]
        lo = _lo_mask((tm, PAIR))

        @pl.when(pl.program_id(0) == 0)
        def _():
            dg_ref[...] = jnp.zeros_like(dg_ref)

        def norm_bwd(c, gi, dy):
            xv = qkv_ref[:, c * PAIR:(c + 1) * PAIR]
            r = lax.rsqrt(_half_sum(xv * xv, lo) * (1.0 / HEAD_DIM) + EPS)
            xn = xv * r
            dg_ref[gi:gi + 1, :] += jnp.sum(dy * xn, axis=0, keepdims=True)
            dxn = dy * g_ref[gi:gi + 1, :]
            dx = r * (dxn - xn * (_half_sum(dxn * xn, lo) * (1.0 / HEAD_DIM)))
            out_ref[:, c * PAIR:(c + 1) * PAIR] = dx.astype(BF16)

        def fold(v):
            return v + pltpu.roll(v, HEAD_DIM, 1)

        for c in range(4):
            norm_bwd(c, 0, _collect(tile, dq_refs, c) * scale)
            norm_bwd(4 + c, 1, _collect(tile, dk_refs, c))
            out_ref[:, (8 + c) * PAIR:(9 + c) * PAIR] = _collect(tile, dv_refs, c).astype(BF16)
            norm_bwd(12 + c, 2, qb_ref[c].astype(F32) * scale)
        kb, vb = kb_ref[...].astype(F32), vb_ref[...].astype(F32)
        norm_bwd(16, 3, jnp.where(lo, fold(kb[0]), fold(kb[1])))
        out_ref[:, 17 * PAIR:18 * PAIR] = jnp.where(lo, fold(vb[0]), fold(vb[1])).astype(BF16)

    four = pl.BlockSpec((4, tm, PAIR), lambda i: (0, i, 0))
    two = pl.BlockSpec((2, tm, PAIR), lambda i: (0, i, 0))
    specs, _ = _spread_specs(tm, T, BF16)
    views = [t.reshape(4, d, T // d, PAIR) for group in (dqa, dka, dva) for t, d in zip(group, DILS)]
    return pl.pallas_call(
        body, name="attn_post", grid=(T // tm,),
        in_specs=[pl.BlockSpec((tm, NQ), lambda i: (i, 0)), pl.BlockSpec((4, PAIR), lambda i: (0, 0))]
        + specs * 3 + [four, two, two],
        out_specs=[pl.BlockSpec((tm, NQ), lambda i: (i, 0)), pl.BlockSpec((4, PAIR), lambda i: (0, 0))],
        out_shape=[jax.ShapeDtypeStruct((T, NQ), BF16), jax.ShapeDtypeStruct((4, PAIR), F32)],
        scratch_shapes=[pltpu.VMEM((tm, PAIR), F32)],
        compiler_params=_params("arbitrary"),
    )(qkv, gains2, *views, dqb, dkb, dvb)


def _dense_norm_bwd(dres, dz, w, blk, x, g, tm):
    T, D = x.shape
    N = dz.shape[1]
    wspec, wload = _weight_arg(w, blk)

    def body(dres_ref, dz_ref, w_ref, x_ref, g_ref, dx_ref, dgn_ref):
        i = pl.program_id(0)
        dx, dg = _norm_bwd(_dot_nt(dz_ref[...], wload(w_ref)), x_ref[...], g_ref[...])
        dx_ref[...] = dres_ref[...] + dx

        @pl.when(i == 0)
        def _():
            dgn_ref[...] = dg

        @pl.when(i > 0)
        def _():
            dgn_ref[...] += dg

    tok = pl.BlockSpec((tm, D), lambda i: (i, 0))
    row = pl.BlockSpec((1, D), lambda i: (0, 0))
    return pl.pallas_call(
        body, name="dense_norm_bwd", grid=(T // tm,),
        in_specs=[tok, pl.BlockSpec((tm, N), lambda i: (i, 0)), wspec, tok, row],
        out_specs=[tok, row],
        out_shape=[jax.ShapeDtypeStruct((T, D), F32), jax.ShapeDtypeStruct((1, D), F32)],
        compiler_params=_params("arbitrary"),
    )(dres, dz, w, x, g)


def _bias_reduce(onehot, dbm):
    Hb, K = dbm.shape

    def body(oh_ref, d_ref, out_ref):
        oh = oh_ref[...]
        d = d_ref[...]
        hi = d.astype(BF16)
        r1 = d - hi.astype(F32)
        mid = r1.astype(BF16)
        low = (r1 - mid.astype(F32)).astype(BF16)
        out_ref[...] = _dot_nt(hi, oh) + _dot_nt(mid, oh) + _dot_nt(low, oh)

    vm = pl.BlockSpec(memory_space=pltpu.VMEM)
    return pl.pallas_call(
        body, name="bias_reduce", in_specs=[vm, vm], out_specs=vm,
        out_shape=jax.ShapeDtypeStruct((Hb, N_BUCKETS), F32),
        compiler_params=pltpu.CompilerParams(vmem_limit_bytes=VMEM_LIMIT),
    )(onehot, dbm)


def _ple_fwd(x, g, wg, blk, p, wp, target, tm):
    T, D = x.shape
    P = p.shape[1]
    with_loss = target is not None
    wspec, wload = _weight_arg(wg, blk)

    def body(*refs):
        if with_loss:
            x_ref, g_ref, wg_ref, p_ref, wp_ref, t_ref, y_ref, hn_ref, gate_ref, pp_ref, pb_ref, loss_ref = refs
        else:
            x_ref, g_ref, wg_ref, p_ref, wp_ref, y_ref, hn_ref, gate_ref, pp_ref, pb_ref = refs
        i = pl.program_id(0)
        xv = x_ref[...]
        hb = (xv * _rstd(xv) * g_ref[...]).astype(BF16)
        hn_ref[...] = hb
        gate = _sigmoid(_dot(hb, wload(wg_ref)))
        pb = p_ref[...].astype(BF16)
        pb_ref[...] = pb
        pp = _dot(pb, wp_ref[...])
        gate_ref[...] = gate
        pp_ref[...] = pp
        y = xv + gate * pp
        if with_loss:
            err = y - t_ref[...]
            y_ref[...] = err * (1.0 / D)
            part = jnp.broadcast_to(0.5 * jnp.sum(jnp.sum(err * err, axis=1, keepdims=True) * (1.0 / D),
                                                  axis=0, keepdims=True), (1, 128))

            @pl.when(i == 0)
            def _():
                loss_ref[...] = part

            @pl.when(i > 0)
            def _():
                loss_ref[...] += part
        else:
            y_ref[...] = y

    tok = pl.BlockSpec((tm, D), lambda i: (i, 0))
    ptok = pl.BlockSpec((tm, P), lambda i: (i, 0))
    in_specs = [tok, pl.BlockSpec((1, D), lambda i: (0, 0)), wspec, ptok,
                pl.BlockSpec((P, D), lambda i: (0, 0))]
    out_specs = [tok, tok, tok, tok, ptok]
    out_shape = [jax.ShapeDtypeStruct((T, D), F32), jax.ShapeDtypeStruct((T, D), BF16),
                 jax.ShapeDtypeStruct((T, D), F32), jax.ShapeDtypeStruct((T, D), F32),
                 jax.ShapeDtypeStruct((T, P), BF16)]
    args = [x, g, wg, p, wp]
    if with_loss:
        in_specs.append(tok)
        out_specs.append(pl.BlockSpec((1, 128), lambda i: (0, 0)))
        out_shape.append(jax.ShapeDtypeStruct((1, 128), F32))
        args.append(target)
    return pl.pallas_call(
        body, name="ple_fwd_loss" if with_loss else "ple_fwd", grid=(T // tm,),
        in_specs=in_specs, out_specs=out_specs, out_shape=out_shape,
        compiler_params=_params("arbitrary" if with_loss else "parallel"),
    )(*args)


def _ple_bwd(dy, gate, pp, tm, dep=None):
    T, D = dy.shape

    def body(dy_ref, gate_ref, pp_ref, dgl_ref, dpp_ref):
        d = dy_ref[...]
        gt = gate_ref[...]
        dgl_ref[...] = (d * pp_ref[...] * gt * (1.0 - gt)).astype(BF16)
        dpp_ref[...] = (d * gt).astype(BF16)

    tok = pl.BlockSpec((tm, D), lambda i: (i, 0))
    body, in_specs, args = _with_dep(body, dep, [tok, tok, tok], [dy, gate, pp])
    return pl.pallas_call(
        body, name="ple_bwd", grid=(T // tm,), in_specs=in_specs, out_specs=[tok, tok],
        out_shape=[jax.ShapeDtypeStruct((T, D), BF16), jax.ShapeDtypeStruct((T, D), BF16)],
        compiler_params=_params("parallel"),
    )(*args)


def _adamw(w, g, m, v):
    shape = w.shape
    C = shape[-1]
    w2, g2, m2, v2 = (a.reshape(-1, C) for a in (w, g, m, v))
    Rn = w2.shape[0]
    tr = Rn
    for cand in (1024, 704, 512, 352, 256):
        if Rn % cand == 0:
            tr = cand
            break
    c1 = 1.0 - ADAM_B1 ** ADAM_STEP
    c2 = 1.0 - ADAM_B2 ** ADAM_STEP

    def body(w_ref, g_ref, m_ref, v_ref, d_ref, nm_ref, nv_ref):
        gv = g_ref[...]
        mn = ADAM_B1 * m_ref[...] + (1.0 - ADAM_B1) * gv
        vn = ADAM_B2 * v_ref[...] + (1.0 - ADAM_B2) * (gv * gv)
        d_ref[...] = -ADAM_LR * ((mn / c1) / (jnp.sqrt(vn / c2) + ADAM_EPS) + ADAM_WD * w_ref[...])
        nm_ref[...] = mn
        nv_ref[...] = vn

    spec = pl.BlockSpec((tr, C), lambda i: (i, 0))
    sh = jax.ShapeDtypeStruct((Rn, C), F32)
    d, nm, nv = pl.pallas_call(
        body, name="adamw", grid=(Rn // tr,), in_specs=[spec] * 4, out_specs=[spec] * 3, out_shape=[sh] * 3,
        compiler_params=_params("parallel"),
    )(w2, g2, m2, v2)
    return d.reshape(shape), nm.reshape(shape), nv.reshape(shape)


def _my_place():
    x, y, c = lax.axis_index("x"), lax.axis_index("y"), lax.axis_index("c")
    chips = [(1 - x, y), (x, 1 - y), (1 - x, 1 - y)]
    return x, y, c, chips


def _all_gather(arrs):
    n = len(arrs)

    def body(*refs):
        x_refs, out_refs = refs[:n], refs[n:2 * n]
        send_sems, recv_sems, local_sems = refs[2 * n:]
        x, y, c, chips = _my_place()
        me, sibling = (x, y, c), (x, y, 1 - c)

        def copy(m, k, block, to, src=None):
            rows = out_refs[m].at[4 * block[0] + 2 * block[1] + block[2]]
            return pltpu.make_async_remote_copy(
                src_ref=rows if src is None else src, dst_ref=rows,
                send_sem=send_sems.at[7 * m + k], recv_sem=recv_sems.at[7 * m + k], device_id=to, device_id_type=MESH)

        mine = [pltpu.make_async_copy(x_refs[m], out_refs[m].at[4 * x + 2 * y + c], local_sems.at[m])
                for m in range(n)]
        for cp in mine:
            cp.start()
        first = []
        for m in range(n):
            first.append(copy(m, 0, me, sibling, src=x_refs[m]))
            first += [copy(m, 1 + j, me, (*chip, c), src=x_refs[m]) for j, chip in enumerate(chips)]
        for cp in first:
            cp.start()
        passed = []
        for m in range(n):
            for j, chip in enumerate(chips):
                copy(m, 1 + j, (*chip, c), me).wait_recv()
                cp = copy(m, 4 + j, (*chip, c), sibling)
                cp.start()
                passed.append(cp)
        for m in range(n):
            copy(m, 0, sibling, me).wait_recv()
            for j, chip in enumerate(chips):
                copy(m, 4 + j, (*chip, 1 - c), me).wait_recv()
        for cp in first + passed:
            cp.wait_send()
        for cp in mine:
            cp.wait()

    hbm = pl.BlockSpec(memory_space=pl.ANY)
    return pl.pallas_call(
        body, name="all_gather", in_specs=[hbm] * n, out_specs=[hbm] * n,
        out_shape=[jax.ShapeDtypeStruct((N_DEV,) + a.shape, a.dtype) for a in arrs],
        scratch_shapes=[pltpu.SemaphoreType.DMA((7 * n,)), pltpu.SemaphoreType.DMA((7 * n,)),
                        pltpu.SemaphoreType.DMA((n,))],
    )(*arrs)


def _peer(x, y, c, k):
    return (x ^ ((k >> 2) & 1), y ^ ((k >> 1) & 1), c ^ (k & 1))


HBM_SPEC = pl.BlockSpec(memory_space=pltpu.HBM)
SEM_SPEC = pl.BlockSpec(memory_space=pltpu.SEMAPHORE)


def _exchange_refs(srcs, lands, m, k, x, y, c, scatter):
    peer = _peer(x, y, c, k)
    if scatter:
        return srcs[m].at[4 * peer[0] + 2 * peer[1] + peer[2]], lands[m].at[k - 1], peer
    return srcs[m], lands[m].at[4 * x + 2 * y + c], peer


def _exchange_start(arrs, land_shapes, scatter, name):
    n = len(arrs)

    def body(*refs):
        srcs, lands = refs[:n], refs[n:2 * n]
        send_sems, recv_sems = refs[2 * n], refs[2 * n + 1]
        token = refs[-1]
        x, y, c, _ = _my_place()
        for m in range(n):
            for k in range(1, N_DEV):
                src, dst, peer = _exchange_refs(srcs, lands, m, k, x, y, c, scatter)
                pltpu.make_async_remote_copy(
                    src_ref=src, dst_ref=dst, send_sem=send_sems.at[7 * m + k - 1],
                    recv_sem=recv_sems.at[7 * m + k - 1], device_id=peer, device_id_type=MESH).start()
        token[...] = jnp.zeros_like(token)

    zones = [lax.empty(s_, a.dtype) for s_, a in zip(land_shapes, arrs)]
    outs = pl.pallas_call(
        body, name=name,
        out_shape=(pltpu.SemaphoreType.DMA((7 * n,)), pltpu.SemaphoreType.DMA((7 * n,)),
                   *[pltpu.HBM(a.shape, a.dtype) for a in arrs], *[pltpu.HBM(z.shape, z.dtype) for z in zones],
                   jax.ShapeDtypeStruct((8, 128), F32)),
        in_specs=[HBM_SPEC] * (2 * n),
        out_specs=(SEM_SPEC, SEM_SPEC, *[HBM_SPEC] * (2 * n), pl.BlockSpec(memory_space=pltpu.VMEM)),
        input_output_aliases={m: 2 + m for m in range(2 * n)},
        compiler_params=pltpu.CompilerParams(has_side_effects=pltpu.SideEffectType.DATAFLOW_SIDE_EFFECTING),
    )(*[pltpu.with_memory_space_constraint(a, pltpu.HBM) for a in arrs],
      *[pltpu.with_memory_space_constraint(z, pltpu.HBM) for z in zones])
    return outs[0], outs[1], list(outs[2:2 + n]), list(outs[2 + n:2 + 2 * n]), outs[-1]


def _exchange_wait(send_sems, recv_sems, arrs, zones, after, scatter, name):
    n = len(arrs)
    afters = list(after) if isinstance(after, (list, tuple)) else [after]

    def body(*refs):
        srcs, lands = refs[:n], refs[n:2 * n]
        send_sems, recv_sems = refs[2 * n], refs[2 * n + 1]
        x, y, c, _ = _my_place()
        for m in range(n):
            for k in range(1, N_DEV):
                src, dst, peer = _exchange_refs(srcs, lands, m, k, x, y, c, scatter)
                cp = pltpu.make_async_remote_copy(
                    src_ref=src, dst_ref=dst, send_sem=send_sems.at[7 * m + k - 1],
                    recv_sem=recv_sems.at[7 * m + k - 1], device_id=peer, device_id_type=MESH)
                cp.wait_send()
                cp.wait_recv()

    outs = pl.pallas_call(
        body, name=name,
        out_shape=tuple(pltpu.HBM(a.shape, a.dtype) for a in list(arrs) + list(zones)),
        in_specs=[HBM_SPEC] * (2 * n) + [SEM_SPEC, SEM_SPEC] + [pl.BlockSpec(memory_space=pl.ANY)] * len(afters),
        out_specs=tuple([HBM_SPEC] * (2 * n)),
        input_output_aliases={m: m for m in range(2 * n)},
        compiler_params=pltpu.CompilerParams(has_side_effects=pltpu.SideEffectType.DATAFLOW_SIDE_EFFECTING),
    )(*arrs, *zones, send_sems, recv_sems, *afters)
    return list(outs[n:])


def _sum_parts(own, parts, tr, dep=None):
    R, W = own.shape

    def body(own_ref, parts_ref, out_ref):
        acc = own_ref[...].astype(F32)
        for k in range(N_DEV - 1):
            acc = acc + parts_ref[k].astype(F32)
        out_ref[...] = acc

    in_specs = [pl.BlockSpec((tr, W), lambda i: (i, 0)), pl.BlockSpec((N_DEV - 1, tr, W), lambda i: (0, i, 0))]
    body, in_specs, args = _with_dep(body, dep, in_specs, [own, parts])
    return pl.pallas_call(
        body, name="sum_parts", grid=(R // tr,),
        in_specs=in_specs,
        out_specs=pl.BlockSpec((tr, W), lambda i: (i, 0)),
        out_shape=jax.ShapeDtypeStruct((R, W), F32),
        compiler_params=_params("parallel"),
    )(*args)


def _all_reduce_small(v, dep=None):
    Rn, Wd = v.shape

    def body(v_ref, out_ref, gat_ref, send_sems, recv_sems):
        x, y, c, _ = _my_place()
        me = 4 * x + 2 * y + c
        gat_ref[me] = v_ref[...]
        copies = []
        for k in range(1, N_DEV):
            fx, fy, fc = (k >> 2) & 1, (k >> 1) & 1, k & 1
            peer = (x ^ fx, y ^ fy, c ^ fc)
            cp = pltpu.make_async_remote_copy(
                src_ref=v_ref, dst_ref=gat_ref.at[me], send_sem=send_sems.at[k - 1], recv_sem=recv_sems.at[k - 1],
                device_id=peer, device_id_type=MESH)
            cp.start()
            copies.append(cp)
        for cp in copies:
            cp.wait_recv()
        for cp in copies:
            cp.wait_send()
        acc = gat_ref[0]
        for k in range(1, N_DEV):
            acc = acc + gat_ref[k]
        out_ref[...] = acc

    vm = pl.BlockSpec(memory_space=pltpu.VMEM)
    body, in_specs, args = _with_dep(body, dep, [vm], [v])
    return pl.pallas_call(
        body, name="all_reduce_small", in_specs=in_specs, out_specs=vm,
        out_shape=jax.ShapeDtypeStruct((Rn, Wd), F32),
        scratch_shapes=[pltpu.VMEM((N_DEV, Rn, Wd), F32), pltpu.SemaphoreType.DMA((7,)),
                        pltpu.SemaphoreType.DMA((7,))],
    )(*args)


def _t5_bucket(rel):
    half = N_BUCKETS // 2
    max_exact = half // 2
    ret = jnp.where(rel > 0, half, 0)
    n = jnp.abs(rel)
    nf = jnp.maximum(n, 1).astype(F32)
    large = max_exact + (jnp.log(nf / max_exact) / math.log(MAX_DISTANCE / max_exact)
                         * (half - max_exact)).astype(jnp.int32)
    large = jnp.minimum(large, half - 1)
    return ret + jnp.where(n < max_exact, n, large)


def _band(R, d):
    W = BQ + 2 * R
    rel = jnp.arange(W)[None, :] - R - jnp.arange(BQ)[:, None]
    return _t5_bucket(rel * d), jnp.abs(rel) <= R


def _onehot(R, d):
    bkt, in_band = _band(R, d)
    return ((bkt.reshape(1, -1) == jnp.arange(N_BUCKETS)[:, None]) & in_band.reshape(1, -1)).astype(BF16)


def _bias_expand(table_t, onehot):
    H = table_t.shape[0]
    K = onehot.shape[1]

    def body(t_ref, oh_ref, out_ref):
        oh = oh_ref[...]
        t = t_ref[...]
        hi = t.astype(BF16)
        r1 = t - hi.astype(F32)
        mid = r1.astype(BF16)
        low = (r1 - mid.astype(F32)).astype(BF16)
        marked = _dot(jnp.ones(t.shape, BF16), oh) > 0.5
        out_ref[...] = jnp.where(marked, _dot(hi, oh) + _dot(mid, oh) + _dot(low, oh), NEG)

    vm = pl.BlockSpec(memory_space=pltpu.VMEM)
    return pl.pallas_call(
        body, name="bias_expand", in_specs=[vm, vm], out_specs=vm,
        out_shape=jax.ShapeDtypeStruct((H, K), F32),
        compiler_params=pltpu.CompilerParams(vmem_limit_bytes=VMEM_LIMIT),
    )(table_t, onehot)


def _bias_matrix(table, R, d):
    return _bias_expand(table.T, _onehot(R, d)).reshape(table.shape[1], BQ, BQ + 2 * R)


def _bias_variants(base, R):
    H, _, W = base.shape
    fill = jnp.full((H, BQ, R), NEG, F32)
    first = jnp.concatenate([base[:, :, R:], fill], axis=2)
    last = jnp.concatenate([fill, base[:, :, :W - R]], axis=2)
    v = jnp.stack([base, first, last], axis=1)
    v = v.reshape(H // 2, 2, 3, BQ, W).transpose(0, 2, 1, 3, 4).reshape(H // 2, 3, 2 * BQ, W)
    return v, v.transpose(0, 1, 3, 2)


def _bias_grad(dbt, R, d):
    P, _, W, _ = dbt.shape
    dbt = dbt[:, 0].at[:, R:].add(dbt[:, 1, :W - R]).at[:, :W - R].add(dbt[:, 2, R:])
    dbm = dbt.reshape(P, W, 2, BQ).transpose(0, 2, 3, 1).reshape(2 * P, BQ * W)
    return _bias_reduce(_onehot(R, d), dbm).T


def _tile2(gain):
    return jnp.concatenate([gain, gain])


ROW_W_O, ROW_GATE, B_ROWS = 768, 896, 1024
BLK_W_O, BLK_GATE = ROW_W_O // 128, ROW_GATE // 128


def _pack_layer(wts, i):
    a = jnp.stack([wts["ffn1_w_in"][i], wts["ffn2_w_in"][i]])
    D = a.shape[1]
    b = jnp.concatenate([
        wts["ffn1_w_out"][i], wts["ffn2_w_out"][i],
        jnp.zeros((ROW_W_O - 2 * wts["ffn1_w_out"].shape[1], D), a.dtype), wts["w_o"][i], wts["w_ple_gate"][i]])
    return a, b, wts["w_qkv"][i], wts["w_ple_proj"][i]


def _unpack_layer(sums, like):
    w_in2, b1, proj, w_o, qkv, w_in1, w_out1 = sums
    n_out = like["ffn1_w_out"].shape[1]
    out = {}
    if w_in2 is not None:
        out.update(ffn2_w_in=w_in2, ffn2_w_out=b1[:n_out], w_ple_gate=b1[n_out:], w_ple_proj=proj)
    if w_o is not None:
        out.update(w_o=w_o, w_qkv=qkv)
    if w_in1 is not None:
        out.update(ffn1_w_in=w_in1, ffn1_w_out=w_out1)
    return out


def _col_sharded(g):
    return g.transpose(1, 0, 2).reshape(g.shape[1], -1)


def _to_col_shards(g):
    rows = g.shape[0]
    return g.reshape(rows, N_DEV, -1).transpose(1, 0, 2)


def _layer_weights(ga, gb, gq, gp):
    return dict(ga=ga, gb=gb, w_qkv=_col_sharded(gq), w_proj=_col_sharded(gp))


def _layer_fwd(x, p, w, sm, i, target, tm, biases, dep=None):
    ga, gb = w["ga"], w["gb"]
    saved = {}
    saved["x0"] = x
    x1, saved["h1"], saved["zg1"], saved["zu1"], saved["s1"] = _ffn_fwd(
        x, sm["norm_ffn1"][i][None], ga, gb, 0, 2 * tm, dep)
    saved["x1"] = x1
    qkv, saved["hm"] = _qkv_fwd(x1, sm["norm_mix"][i][None], w["w_qkv"], 2 * tm)
    saved["qkv"] = qkv
    gains2 = jnp.stack([_tile2(sm[k][i]) for k in ("q_norm_a", "k_norm_a", "q_norm_b", "k_norm_b")])
    saved["gains2"] = gains2
    qb, kb, vb, qkv_d = _attn_prep(qkv, gains2, tm)
    no_sink = jnp.full((8,), NEG, F32)
    branches = []
    outs = []
    for (R, d), bias, (qd, kd, vd) in zip(DILATED, biases[:3], qkv_d):
        sink = jnp.tile(no_sink, d)
        outs.append(_attn_fwd(qd, kd, vd, bias[0], sink, R, 1, d))
        branches.append((qd, kd, vd, bias, sink, R, d))
    bias_b = biases[3]
    sink_b = sm["sink_b"][i]
    ob, lb = _attn_fwd(qb, kb, vb, bias_b[0], sink_b, SWA_RADIUS, 2, 1)
    merged, o_cat = _attn_merge(outs, ob, tm)
    saved.update(branches=branches, b=(qb, kb, vb, bias_b, sink_b), merged=merged, ob=ob, lb=lb, o_cat=o_cat)
    x2 = _oproj_fwd(x1, o_cat, gb, BLK_W_O, 2 * tm)
    saved["x2"] = x2
    x3, saved["h2"], saved["zg2"], saved["zu2"], saved["s2"] = _ffn_fwd(
        x2, sm["norm_ffn2"][i][None], ga, gb, 1, 2 * tm)
    saved["x3"] = x3
    res = _ple_fwd(x3, sm["norm_ple"][i][None], gb, BLK_GATE, p, w["w_proj"], target, tm)
    y, saved["hp"], saved["gate"], saved["pp"], saved["pb"] = res[:5]
    loss = res[5] if target is not None else None
    return y, loss, saved


def _layer_bwd(dy, w, sm, i, sv, tm, dep=None, on_ready=None, on_small=None, on_last=None):
    ga, gb = w["ga"], w["gb"]
    gs = {}
    D = dy.shape[1]
    dgl, dpp = _ple_bwd(dy, sv["gate"], sv["pp"], tm, dep)
    d_gate = _matmul_tn(sv["hp"], dgl, D, 4 * tm)
    d_proj = _matmul_tn(sv["pb"], dpp, D, 4 * tm)
    dx3, gs["norm_ple"] = _dense_norm_bwd(dy, dgl, gb, BLK_GATE, sv["x3"], sm["norm_ple"][i][None], 2 * tm)
    dx2, dyb, dzg, dzu, gs["norm_ffn2"] = _ffn_bwd(dx3, sv["x2"], sm["norm_ffn2"][i][None], sv["zg2"], sv["zu2"],
                                                   ga, gb, 1, tm)
    dwin2, dwo2 = _ffn_dw(sv["h2"], dzg, dzu, sv["s2"], dyb, 4 * tm)
    half = dwo2.shape[1] // 2
    after_ffn2 = [dwin2, jnp.concatenate([dwo2.reshape(N_DEV, half, D), d_gate.reshape(N_DEV, -1, D)], axis=1),
                  _to_col_shards(d_proj)]
    token = None if on_ready is None else on_ready(0, after_ffn2)
    dx2b, do_b, do_a = _oproj_bwd(dx2, gb, BLK_W_O, tm, token)
    d_wo = _matmul_tn(sv["o_cat"], dx2b, D, 4 * tm)
    dqa, dka, dva, dbias = [], [], [], []
    for (qd, kd, vd, bias, sink, R, d), (oa, la), do_d in zip(sv["branches"], sv["merged"], do_a):
        dq, dk, dv, dbm, _ = _attn_bwd(qd, kd, vd, bias[1], sink, oa, la, do_d, R, 1, d)
        dqa.append(dq)
        dka.append(dk)
        dva.append(dv)
        dbias.append(dbm)
    qb, kb, vb, bias_b, sink_b = sv["b"]
    dqb, dkb, dvb, dbm_b, dsink = _attn_bwd(qb, kb, vb, bias_b[1], sink_b, sv["ob"], sv["lb"], do_b,
                                            SWA_RADIUS, 2, 1)
    gs["rel_bias"] = dbias + [dbm_b]
    gs["sink_b"] = jnp.sum(dsink[:, 0].reshape(-1, 2, BQ), axis=2).reshape(-1)
    dqkv, dgains2 = _attn_post(sv["qkv"], sv["gains2"], dqa, dka, dva, dqb,
                               dkb, dvb, tm)
    dgains = dgains2[:, :HEAD_DIM] + dgains2[:, HEAD_DIM:]
    for k, name in enumerate(("q_norm_a", "k_norm_a", "q_norm_b", "k_norm_b")):
        gs[name] = dgains[k]
    d_qkv = _matmul_tn(sv["hm"], dqkv, dqkv.shape[1] // 2, 4 * tm)
    after_mixer = [d_wo.reshape(N_DEV, -1, D), _to_col_shards(d_qkv)]
    token = None if on_ready is None else on_ready(1, after_mixer)
    dx1, gs["norm_mix"] = _dense_norm_bwd(dx2, dqkv, w["w_qkv"], None, sv["x1"], sm["norm_mix"][i][None], 2 * tm)
    g1 = sm["norm_ffn1"][i][None]
    if on_last is None:
        dx0, dyb, dzg, dzu, gs["norm_ffn1"] = _ffn_bwd(dx1, sv["x0"], g1, sv["zg1"], sv["zu1"], ga, gb, 0, tm, token)
        dwin1, dwo1 = _ffn_dw(sv["h1"], dzg, dzu, sv["s1"], dyb, 4 * tm)
        return dx0, (after_ffn2, after_mixer, [dwin1, dwo1.reshape(N_DEV, half, D)]), gs
    dyb, dzg, dzu = _ffn_bwd_dz(dx1, sv["zg1"], sv["zu1"], gb, 0, 2 * tm, token)
    dwin1, dwo1 = _ffn_dw(sv["h1"], dzg, dzu, sv["s1"], dyb, 4 * tm, on_small(gs))
    last = [dwin1, dwo1.reshape(N_DEV, half, D)]
    dx0, gs["norm_ffn1"] = _ffn_bwd_dx(dx1, sv["x0"], g1, dzg, dzu, ga, 0, 2 * tm, on_last(last))
    return dx0, (after_ffn2, after_mixer, last), gs


def _bias_matrices(rel_bias):
    biases = [_bias_variants(_bias_matrix(rel_bias[:, :8], R, d), R) for R, d in DILATED]
    biases.append(_bias_variants(_bias_matrix(rel_bias[:, 8:], SWA_RADIUS, 1), SWA_RADIUS))
    return biases


def _stack_small(per_layer):
    small = {}
    for k, v in per_layer.items():
        if k == "rel_bias":
            per_branch = [sum(parts) for parts in zip(*v.values())]
            drel_a = sum(_bias_grad(t, R, d) for t, (R, d) in zip(per_branch[:3], DILATED))
            small[k] = jnp.concatenate([drel_a, _bias_grad(per_branch[3], SWA_RADIUS, 1)], axis=1)
        else:
            small[k] = jnp.stack([v[i].reshape(-1) for i in sorted(v)])
    return small


TM = 512
SUM_TILES = (512, 480, 256, 128, 512, 512, 352)
LAST_GROUP = ("ffn1_w_in", "ffn1_w_out")


def _pack_small(d, extra=None):
    parts = [d[k].reshape(-1) for k in SMALL]
    if extra is not None:
        parts.append(extra.reshape(-1))
    flat = jnp.concatenate(parts)
    return jnp.pad(flat, (0, SMALL_ROWS * 128 - flat.shape[0])).reshape(SMALL_ROWS, 128)


def _unpack_small(buf, like):
    flat = buf.reshape(-1)
    out, off = {}, 0
    for k in SMALL:
        n = like[k].size
        out[k] = flat[off:off + n].reshape(like[k].shape)
        off += n
    return out, flat[off]


def kernel(x, p, rel_bias, norm_ffn1, ffn1_w_in, ffn1_w_out, norm_mix, w_qkv, q_norm_a, k_norm_a, q_norm_b, k_norm_b, sink_b, w_o, norm_ffn2, ffn2_w_in, ffn2_w_out, norm_ple, w_ple_gate, w_ple_proj, loss_target, m_rel_bias, m_norm_ffn1, m_ffn1_w_in, m_ffn1_w_out, m_norm_mix, m_w_qkv, m_q_norm_a, m_k_norm_a, m_q_norm_b, m_k_norm_b, m_sink_b, m_w_o, m_norm_ffn2, m_ffn2_w_in, m_ffn2_w_out, m_norm_ple, m_w_ple_gate, m_w_ple_proj, v_rel_bias, v_norm_ffn1, v_ffn1_w_in, v_ffn1_w_out, v_norm_mix, v_w_qkv, v_q_norm_a, v_k_norm_a, v_q_norm_b, v_k_norm_b, v_sink_b, v_w_o, v_norm_ffn2, v_ffn2_w_in, v_ffn2_w_out, v_norm_ple, v_w_ple_gate, v_w_ple_proj):
    wts = dict(rel_bias=rel_bias, norm_ffn1=norm_ffn1, ffn1_w_in=ffn1_w_in, ffn1_w_out=ffn1_w_out,
               norm_mix=norm_mix, w_qkv=w_qkv, q_norm_a=q_norm_a, k_norm_a=k_norm_a, q_norm_b=q_norm_b,
               k_norm_b=k_norm_b, sink_b=sink_b, w_o=w_o, norm_ffn2=norm_ffn2, ffn2_w_in=ffn2_w_in,
               ffn2_w_out=ffn2_w_out, norm_ple=norm_ple, w_ple_gate=w_ple_gate, w_ple_proj=w_ple_proj)
    mom = dict(rel_bias=m_rel_bias, norm_ffn1=m_norm_ffn1, ffn1_w_in=m_ffn1_w_in, ffn1_w_out=m_ffn1_w_out,
               norm_mix=m_norm_mix, w_qkv=m_w_qkv, q_norm_a=m_q_norm_a, k_norm_a=m_k_norm_a, q_norm_b=m_q_norm_b,
               k_norm_b=m_k_norm_b, sink_b=m_sink_b, w_o=m_w_o, norm_ffn2=m_norm_ffn2, ffn2_w_in=m_ffn2_w_in,
               ffn2_w_out=m_ffn2_w_out, norm_ple=m_norm_ple, w_ple_gate=m_w_ple_gate, w_ple_proj=m_w_ple_proj)
    var = dict(rel_bias=v_rel_bias, norm_ffn1=v_norm_ffn1, ffn1_w_in=v_ffn1_w_in, ffn1_w_out=v_ffn1_w_out,
               norm_mix=v_norm_mix, w_qkv=v_w_qkv, q_norm_a=v_q_norm_a, k_norm_a=v_k_norm_a, q_norm_b=v_q_norm_b,
               k_norm_b=v_k_norm_b, sink_b=v_sink_b, w_o=v_w_o, norm_ffn2=v_norm_ffn2, ffn2_w_in=v_ffn2_w_in,
               ffn2_w_out=v_ffn2_w_out, norm_ple=v_norm_ple, w_ple_gate=v_w_ple_gate, w_ple_proj=v_w_ple_proj)
    sm = {k: wts[k] for k in SMALL}
    me = 4 * lax.axis_index("x") + 2 * lax.axis_index("y") + lax.axis_index("c")
    packed = []
    for i in range(2):
        a, *rest = _pack_layer(wts, i)
        packed.append([t.astype(BF16) for t in [a.reshape(-1, a.shape[-1])] + rest])
    a_shape = (2, ffn1_w_in.shape[1], ffn1_w_in.shape[2])

    def weights_of(zones):
        return _layer_weights(zones[0].reshape((N_DEV,) + a_shape), *zones[1:])

    w0 = weights_of(_all_gather(packed[0]))
    zone_shapes = [(N_DEV,) + t.shape for t in packed[1]]
    ssem, rsem, thru, zones, token = _exchange_start(packed[1], zone_shapes, False, "gather_start")
    biases = _bias_matrices(rel_bias)
    x1, _, sv0 = _layer_fwd(x[0], p[0, 0], w0, sm, 0, None, TM, biases, dep=token)
    zones = _exchange_wait(ssem, rsem, thru, zones, x1, False, "gather_wait")
    w1 = weights_of([lax.dynamic_update_index_in_dim(z, t, me, 0) for z, t in zip(zones, packed[1])])
    dy, loss, sv1 = _layer_fwd(x1, p[1, 0], w1, sm, 1, loss_target[0], TM, biases)

    def slots_for(arrs):
        return [(N_DEV - 1,) + t.shape[1:] for t in arrs]

    held1, held = {}, {}

    def on_ready1(stage, group):
        held1[stage] = _exchange_start(group, slots_for(group), True, f"scatter1_start_{stage}")
        return held1[stage][4]

    dx1, groups1, gs1 = _layer_bwd(dy, w1, sm, 1, sv1, TM, on_ready=on_ready1)
    on_ready1(2, groups1[2])
    g1 = groups1[0] + groups1[1] + groups1[2]

    def on_ready(stage, group):
        if stage == 1:
            held["slots1"] = [t for st in (0, 1, 2)
                              for t in _exchange_wait(*held1[st][:4], group[0], True, f"scatter1_wait_{st}")]
        held[stage] = _exchange_start(group, slots_for(group), True, f"scatter_start_{stage}")
        return held[stage][4]

    def on_small(gs0):
        part = dict(gs0, norm_ffn1=jnp.zeros_like(gs1["norm_ffn1"]))
        gsmall = _stack_small({k: {0: part[k], 1: gs1[k]} for k in part})
        held["small"] = _all_reduce_small(_pack_small(gsmall, loss[0, :1]))
        return held["small"]

    def on_last(group):
        held["last"] = _exchange_start(group, slots_for(group), True, "scatter_start_2")
        return held["last"][4]

    dx, groups0, gs0 = _layer_bwd(dx1, w0, sm, 0, sv0, TM, dep=held1[2][4], on_ready=on_ready, on_small=on_small,
                                  on_last=on_last)
    last = groups0[2]
    slots0 = [_exchange_wait(*held[stage][:4], last[0], True, f"scatter_wait_{stage}") for stage in (0, 1)]

    def summed(arrs, slots, tiles, dep=None):
        return [_sum_parts(lax.dynamic_index_in_dim(t, me, 0, keepdims=False), s_, tr, dep)
                for t, s_, tr in zip(arrs, slots, tiles)]

    cover = held["last"][4]
    r1 = summed(g1, held["slots1"], SUM_TILES, cover)
    r0 = summed(groups0[0], slots0[0], SUM_TILES[:3], cover) + summed(groups0[1], slots0[1], SUM_TILES[3:5], cover)

    def update(names, layers):
        for k in names:
            grads[k] = jnp.stack([layers[0][k], layers[1][k]])
            delta[k], new_m[k], new_v[k] = _adamw(wts[k], grads[k], mom[k], var[k])

    grads, delta, new_m, new_v = {}, {}, {}, {}
    layer1 = _unpack_layer(r1, wts)
    update([k for k in BIG if k not in LAST_GROUP], [_unpack_layer(r0 + [None, None], wts), layer1])

    cover_done = [dx] + [delta[k] for k in BIG if k not in LAST_GROUP]
    slots_last = _exchange_wait(*held["last"][:4], cover_done, True, "scatter_wait_2")
    update(LAST_GROUP, [_unpack_layer([None] * 5 + summed(last, slots_last, SUM_TILES[5:]), wts), layer1])
    late = _all_reduce_small(gs0["norm_ffn1"].reshape(-1, 128), dep=slots_last[0])
    small_sum, loss_sum = _unpack_small(held["small"], sm)
    small_sum["norm_ffn1"] = small_sum["norm_ffn1"].at[0].add(late.reshape(-1))
    grads.update(small_sum)
    zeros = {k: jnp.zeros_like(wts[k]) for k in SMALL}
    ds, ms, vs = _adamw(_pack_small(wts), _pack_small(small_sum), _pack_small(mom), _pack_small(var))
    for packed, dst in ((ds, delta), (ms, new_m), (vs, new_v)):
        dst.update(_unpack_small(packed, zeros)[0])

    return (loss_sum, dx[None], *[grads[k] for k in WEIGHTS], *[delta[k] for k in WEIGHTS],
            *[new_m[k] for k in WEIGHTS], *[new_v[k] for k in WEIGHTS])
```
